```python
import jax, jax.numpy as jnp
from jax import lax
import numpy as np


D_MODEL = 1024
BATCH = 8
SEQ = 2048
DEPTH = 2

BRANCH_WIDTH = 512
N_BRANCH = 4
EPS = 1e-6
S5_GROUP = 16
S5_GROUPS = BRANCH_WIDTH // S5_GROUP
S5_STATE = 64
S5_STEP_MIN = 1e-3
S5_STEP_MAX = 1e-1
SGU_CHUNK = 128
SGU_HEADS = 8
SGU_HEAD_DIM = BRANCH_WIDTH // SGU_HEADS
M2_HEAD_DIM = 64
M2_HEADS = BRANCH_WIDTH // M2_HEAD_DIM
M2_GROUPS = 2
M2_STATE = 128
M2_CONV = 4
M2_CHUNK = 128
M2_CONV_CH = BRANCH_WIDTH + 2 * M2_GROUPS * M2_STATE
M2_DT_MIN = 1e-3
M2_DT_MAX = 1e-1
SC_CONV = 3

IN_SIZES = (
    BRANCH_WIDTH, BRANCH_WIDTH,
    BRANCH_WIDTH, BRANCH_WIDTH, BRANCH_WIDTH,
    BRANCH_WIDTH, M2_CONV_CH, M2_HEADS,
    BRANCH_WIDTH, BRANCH_WIDTH, BRANCH_WIDTH, BRANCH_WIDTH,
    N_BRANCH * D_MODEL,
)
IN_DIM = int(sum(IN_SIZES))
IN_SPLITS = [int(v) for v in np.cumsum(IN_SIZES)[:-1]]

kernel_name = 'hybrid_s5_sgu_ssd_shortconv_gated_merge'


def rmsnorm(x, w):
    x32 = x.astype(jnp.float32)
    y = x32 * lax.rsqrt(jnp.mean(x32 * x32, axis=-1, keepdims=True) + EPS)
    return (y * w.astype(jnp.float32)).astype(x.dtype)


def causal_depthwise_conv(x, w):
    k, c = w.shape
    return lax.conv_general_dilated(
        x, w[:, None, :].astype(x.dtype), window_strides=(1,), padding=[(k - 1, 0)],
        dimension_numbers=('NWC', 'WIO', 'NWC'), feature_group_count=c)


def _complex_affine_combine(e1, e2):
    a1r, a1i, b1r, b1i = e1
    a2r, a2i, b2r, b2i = e2
    ar = a1r * a2r - a1i * a2i
    ai = a1r * a2i + a1i * a2r
    br = a2r * b1r - a2i * b1i + b2r
    bi = a2r * b1i + a2i * b1r + b2i
    return (ar, ai, br, bi)


def s5_branch(u, gate, lam_re, lam_im, b_re, b_im, c_re, c_im, d, log_step, w_glu):
    bsz, seq_len, _ = u.shape
    f32 = jnp.float32
    u32 = u.astype(f32).reshape(bsz, seq_len, S5_GROUPS, S5_GROUP)
    step = jnp.exp(log_step.astype(f32))[:, None]
    lr, li = lam_re.astype(f32), lam_im.astype(f32)
    mag = jnp.exp(lr * step)
    ab_re, ab_im = mag * jnp.cos(li * step), mag * jnp.sin(li * step)
    den = lr * lr + li * li
    nr = ab_re - 1.0
    coef_re = (nr * lr + ab_im * li) / den
    coef_im = (ab_im * lr - nr * li) / den
    br, bi = b_re.astype(f32), b_im.astype(f32)
    bb_re = coef_re[..., None] * br - coef_im[..., None] * bi
    bb_im = coef_re[..., None] * bi + coef_im[..., None] * br
    bu_re = jnp.einsum('blgp,gnp->blgn', u32, bb_re)
    bu_im = jnp.einsum('blgp,gnp->blgn', u32, bb_im)
    a_re = jnp.broadcast_to(ab_re, bu_re.shape)
    a_im = jnp.broadcast_to(ab_im, bu_re.shape)
    _, _, s_re, s_im = lax.associative_scan(
        _complex_affine_combine, (a_re, a_im, bu_re, bu_im), axis=1)
    y = (jnp.einsum('blgn,gpn->blgp', s_re, c_re.astype(f32))
         - jnp.einsum('blgn,gpn->blgp', s_im, c_im.astype(f32))
         + d.astype(f32) * u32)
    y = jax.nn.gelu(y.reshape(bsz, seq_len, BRANCH_WIDTH))
    y = y * jax.nn.sigmoid(y @ w_glu.astype(f32))
    return (y * jax.nn.silu(gate.astype(f32))).astype(u.dtype)


def sgu_branch(u, v, gate, ln_w, ln_b, w_s, b_s):
    bsz, seq_len, _ = u.shape
    f32 = jnp.float32
    u32 = jax.nn.gelu(u.astype(f32))
    v32 = jax.nn.gelu(v.astype(f32))
    mu = jnp.mean(v32, axis=-1, keepdims=True)
    var = jnp.mean(jnp.square(v32 - mu), axis=-1, keepdims=True)
    vn = (v32 - mu) * lax.rsqrt(var + EPS) * ln_w.astype(f32) + ln_b.astype(f32)
    vn = vn.reshape(bsz, seq_len // SGU_CHUNK, SGU_CHUNK, SGU_HEADS, SGU_HEAD_DIM)
    mask = jnp.tril(jnp.ones((SGU_CHUNK, SGU_CHUNK), dtype=bool))
    w_m = jnp.where(mask, w_s.astype(f32), 0.0)
    s = jnp.einsum('hts,bcshe->bcthe', w_m, vn) + b_s.astype(f32).T[:, :, None]
    out = u32 * s.reshape(bsz, seq_len, BRANCH_WIDTH)
    return (out * jax.nn.silu(gate.astype(f32))).astype(u.dtype)


def segsum(a):
    t = a.shape[-1]
    cs = jnp.cumsum(a, axis=-1)
    diff = cs[..., :, None] - cs[..., None, :]
    mask = jnp.tril(jnp.ones((t, t), dtype=bool))
    return jnp.where(mask, diff, -jnp.inf)


def mamba2_branch(z, xbc, dt_raw, conv_w, conv_b, dt_bias, a_log, d, norm_w):
    bsz, seq_len, _ = z.shape
    f32 = jnp.float32
    nc, q = seq_len // M2_CHUNK, M2_CHUNK
    xbc = jax.nn.silu((causal_depthwise_conv(xbc, conv_w) + conv_b).astype(f32))
    x, bm, cm = jnp.split(xbc, [BRANCH_WIDTH, BRANCH_WIDTH + M2_GROUPS * M2_STATE], axis=-1)
    rep = M2_HEADS // M2_GROUPS
    x = x.reshape(bsz, nc, q, M2_HEADS, M2_HEAD_DIM)
    bm = jnp.repeat(bm.reshape(bsz, seq_len, M2_GROUPS, M2_STATE), rep, axis=2).reshape(bsz, nc, q, M2_HEADS, M2_STATE)
    cm = jnp.repeat(cm.reshape(bsz, seq_len, M2_GROUPS, M2_STATE), rep, axis=2).reshape(bsz, nc, q, M2_HEADS, M2_STATE)
    dt = jax.nn.softplus(dt_raw.astype(f32) + dt_bias.astype(f32))
    a = -jnp.exp(a_log.astype(f32))
    da = (dt * a).reshape(bsz, nc, q, M2_HEADS).transpose(0, 3, 1, 2)
    a_cs = jnp.cumsum(da, axis=-1)
    xdt = x * dt.reshape(bsz, nc, q, M2_HEADS)[..., None]
    scores = jnp.einsum('bclhn,bcshn->bhcls', cm, bm) * jnp.exp(segsum(da))
    y_diag = jnp.einsum('bhcls,bcshp->bclhp', scores, xdt)
    decay_states = jnp.exp(a_cs[..., -1:] - a_cs)
    states = jnp.einsum('bclhn,bhcl,bclhp->bchpn', bm, decay_states, xdt)
    states = jnp.concatenate([jnp.zeros_like(states[:, :1]), states], axis=1)
    decay_chunk = jnp.exp(segsum(jnp.pad(a_cs[..., -1], ((0, 0), (0, 0), (1, 0)))))
    states = jnp.einsum('bhzc,bchpn->bzhpn', decay_chunk, states)[:, :-1]
    y_off = jnp.einsum('bclhn,bchpn,bhcl->bclhp', cm, states, jnp.exp(a_cs))
    y = y_diag + y_off + d.astype(f32)[:, None] * x
    y = y.reshape(bsz, seq_len, BRANCH_WIDTH) * jax.nn.silu(z.astype(f32))
    y = y * lax.rsqrt(jnp.mean(y * y, axis=-1, keepdims=True) + EPS) * norm_w.astype(f32)
    return y.astype(z.dtype)


def shortconv_branch(bg, cg, h, gate, conv_w):
    y = bg * causal_depthwise_conv(cg * h, conv_w)
    return y * jax.nn.silu(gate)


def _fwd_setup_inputs(seed: int = 0) -> dict:
    key = jax.random.key(seed)
    ks = jax.random.split(key, 32)
    f32 = jnp.float32
    W, D, G, N, P = BRANCH_WIDTH, D_MODEL, S5_GROUPS, S5_STATE, S5_GROUP
    nrm = lambda k, shape, s: jax.random.normal(k, shape, f32) * s
    x = jax.random.normal(ks[0], (BATCH, SEQ, D), f32)
    norm_w = 1.0 + nrm(ks[1], (DEPTH, D), 0.01)
    w_in = nrm(ks[2], (DEPTH, D, IN_DIM), D ** -0.5)
    s5_lambda_re = -0.5 + nrm(ks[3], (DEPTH, G, N), 0.01)
    s5_lambda_im = jnp.pi * jnp.arange(N, dtype=f32)[None, None, :] + nrm(ks[4], (DEPTH, G, N), 0.01)
    s5_b_re = nrm(ks[5], (DEPTH, G, N, P), (2.0 * P) ** -0.5)
    s5_b_im = nrm(ks[6], (DEPTH, G, N, P), (2.0 * P) ** -0.5)
    s5_c_re = nrm(ks[7], (DEPTH, G, P, N), (2.0 * N) ** -0.5)
    s5_c_im = nrm(ks[8], (DEPTH, G, P, N), (2.0 * N) ** -0.5)
    s5_d = nrm(ks[9], (DEPTH, G, P), 1.0)
    s5_log_step = jax.random.uniform(ks[10], (DEPTH, G), f32, np.log(S5_STEP_MIN), np.log(S5_STEP_MAX))
    s5_w_glu = nrm(ks[11], (DEPTH, W, W), W ** -0.5)
    sgu_ln_w = 1.0 + nrm(ks[12], (DEPTH, W), 0.01)
    sgu_ln_b = nrm(ks[13], (DEPTH, W), 0.01)
    sgu_w = nrm(ks[14], (DEPTH, SGU_HEADS, SGU_CHUNK, SGU_CHUNK), SGU_CHUNK ** -0.5)
    sgu_b = 1.0 + nrm(ks[15], (DEPTH, SGU_HEADS, SGU_CHUNK), 0.1)
    m2_conv_w = nrm(ks[16], (DEPTH, M2_CONV, M2_CONV_CH), M2_CONV ** -0.5)
    m2_conv_b = nrm(ks[17], (DEPTH, M2_CONV_CH), 0.01)
    dt0 = jnp.exp(jax.random.uniform(ks[18], (DEPTH, M2_HEADS), f32, np.log(M2_DT_MIN), np.log(M2_DT_MAX)))
    m2_dt_bias = dt0 + jnp.log(-jnp.expm1(-dt0))
    m2_a_log = jnp.log(jax.random.uniform(ks[19], (DEPTH, M2_HEADS), f32, 1.0, 16.0))
    m2_d = 1.0 + nrm(ks[20], (DEPTH, M2_HEADS), 0.01)
    m2_norm_w = 1.0 + nrm(ks[21], (DEPTH, W), 0.01)
    sc_conv_w = nrm(ks[22], (DEPTH, SC_CONV, W), SC_CONV ** -0.5)
    merge_b = nrm(ks[23], (DEPTH, N_BRANCH, D), 0.01)
    w_branch = nrm(ks[24], (DEPTH, N_BRANCH, W, D), W ** -0.5)
    w_out = nrm(ks[25], (DEPTH, D, D), D ** -0.5)
    final_norm_w = 1.0 + nrm(ks[26], (D,), 0.01)
    return {'x': x, 'norm_w': norm_w, 'w_in': w_in,
            's5_lambda_re': s5_lambda_re, 's5_lambda_im': s5_lambda_im,
            's5_b_re': s5_b_re, 's5_b_im': s5_b_im, 's5_c_re': s5_c_re, 's5_c_im': s5_c_im,
            's5_d': s5_d, 's5_log_step': s5_log_step, 's5_w_glu': s5_w_glu,
            'sgu_ln_w': sgu_ln_w, 'sgu_ln_b': sgu_ln_b, 'sgu_w': sgu_w, 'sgu_b': sgu_b,
            'm2_conv_w': m2_conv_w, 'm2_conv_b': m2_conv_b, 'm2_dt_bias': m2_dt_bias,
            'm2_a_log': m2_a_log, 'm2_d': m2_d, 'm2_norm_w': m2_norm_w,
            'sc_conv_w': sc_conv_w, 'merge_b': merge_b, 'w_branch': w_branch,
            'w_out': w_out, 'final_norm_w': final_norm_w}


def _fwd_reference(x, norm_w, w_in, s5_lambda_re, s5_lambda_im, s5_b_re, s5_b_im, s5_c_re, s5_c_im,
              s5_d, s5_log_step, s5_w_glu, sgu_ln_w, sgu_ln_b, sgu_w, sgu_b,
              m2_conv_w, m2_conv_b, m2_dt_bias, m2_a_log, m2_d, m2_norm_w,
              sc_conv_w, merge_b, w_branch, w_out, final_norm_w):
    bsz, seq_len, _ = x.shape
    for i in range(DEPTH):
        h = rmsnorm(x, norm_w[i])
        (s5_u, s5_g, sgu_u, sgu_v, sgu_g, m2_z, m2_xbc, m2_dt,
         sc_b, sc_c, sc_h, sc_g, merge_logits) = jnp.split(h @ w_in[i], IN_SPLITS, axis=-1)
        y_a = s5_branch(s5_u, s5_g, s5_lambda_re[i], s5_lambda_im[i], s5_b_re[i], s5_b_im[i],
                        s5_c_re[i], s5_c_im[i], s5_d[i], s5_log_step[i], s5_w_glu[i])
        y_b = sgu_branch(sgu_u, sgu_v, sgu_g, sgu_ln_w[i], sgu_ln_b[i], sgu_w[i], sgu_b[i])
        y_c = mamba2_branch(m2_z, m2_xbc, m2_dt, m2_conv_w[i], m2_conv_b[i], m2_dt_bias[i],
                            m2_a_log[i], m2_d[i], m2_norm_w[i])
        y_d = shortconv_branch(sc_b, sc_c, sc_h, sc_g, sc_conv_w[i])
        branches = jnp.stack([y_a, y_b, y_c, y_d], axis=2)
        branch_out = jnp.einsum('blkw,kwd->blkd', branches, w_branch[i])
        gates = jax.nn.sigmoid(
            merge_logits.reshape(bsz, seq_len, N_BRANCH, D_MODEL).astype(jnp.float32)
            + merge_b[i].astype(jnp.float32))
        merged = jnp.einsum('blkd,blkd->bld', gates, branch_out.astype(jnp.float32)).astype(x.dtype)
        x = x + merged @ w_out[i]
    return rmsnorm(x, final_norm_w)


import jax as _jax
import jax.numpy as _jnp

TWIN_FORMAT = 'train_step'
FWD_PARAMS = ['x', 'norm_w', 'w_in', 's5_lambda_re', 's5_lambda_im', 's5_b_re', 's5_b_im', 's5_c_re', 's5_c_im', 's5_d', 's5_log_step', 's5_w_glu', 'sgu_ln_w', 'sgu_ln_b', 'sgu_w', 'sgu_b', 'm2_conv_w', 'm2_conv_b', 'm2_dt_bias', 'm2_a_log', 'm2_d', 'm2_norm_w', 'sc_conv_w', 'merge_b', 'w_branch', 'w_out', 'final_norm_w']
TWIN_WEIGHTS = ['norm_w', 'w_in', 's5_lambda_re', 's5_lambda_im', 's5_b_re', 's5_b_im', 's5_c_re', 's5_c_im', 's5_d', 's5_log_step', 's5_w_glu', 'sgu_ln_w', 'sgu_ln_b', 'sgu_w', 'sgu_b', 'm2_conv_w', 'm2_conv_b', 'm2_dt_bias', 'm2_a_log', 'm2_d', 'm2_norm_w', 'sc_conv_w', 'merge_b', 'w_branch', 'w_out', 'final_norm_w']
TWIN_DIFF_INPUT = 'x'
TWIN_INPUTS = ['x', 'norm_w', 'w_in', 's5_lambda_re', 's5_lambda_im', 's5_b_re', 's5_b_im', 's5_c_re', 's5_c_im', 's5_d', 's5_log_step', 's5_w_glu', 'sgu_ln_w', 'sgu_ln_b', 'sgu_w', 'sgu_b', 'm2_conv_w', 'm2_conv_b', 'm2_dt_bias', 'm2_a_log', 'm2_d', 'm2_norm_w', 'sc_conv_w', 'merge_b', 'w_branch', 'w_out', 'final_norm_w', 'loss_target', 'm_norm_w', 'm_w_in', 'm_s5_lambda_re', 'm_s5_lambda_im', 'm_s5_b_re', 'm_s5_b_im', 'm_s5_c_re', 'm_s5_c_im', 'm_s5_d', 'm_s5_log_step', 'm_s5_w_glu', 'm_sgu_ln_w', 'm_sgu_ln_b', 'm_sgu_w', 'm_sgu_b', 'm_m2_conv_w', 'm_m2_conv_b', 'm_m2_dt_bias', 'm_m2_a_log', 'm_m2_d', 'm_m2_norm_w', 'm_sc_conv_w', 'm_merge_b', 'm_w_branch', 'm_w_out', 'm_final_norm_w', 'v_norm_w', 'v_w_in', 'v_s5_lambda_re', 'v_s5_lambda_im', 'v_s5_b_re', 'v_s5_b_im', 'v_s5_c_re', 'v_s5_c_im', 'v_s5_d', 'v_s5_log_step', 'v_s5_w_glu', 'v_sgu_ln_w', 'v_sgu_ln_b', 'v_sgu_w', 'v_sgu_b', 'v_m2_conv_w', 'v_m2_conv_b', 'v_m2_dt_bias', 'v_m2_a_log', 'v_m2_d', 'v_m2_norm_w', 'v_sc_conv_w', 'v_merge_b', 'v_w_branch', 'v_w_out', 'v_final_norm_w']
TWIN_OUTPUTS = ['loss', 'grad_x', 'grad_norm_w', 'grad_w_in', 'grad_s5_lambda_re', 'grad_s5_lambda_im', 'grad_s5_b_re', 'grad_s5_b_im', 'grad_s5_c_re', 'grad_s5_c_im', 'grad_s5_d', 'grad_s5_log_step', 'grad_s5_w_glu', 'grad_sgu_ln_w', 'grad_sgu_ln_b', 'grad_sgu_w', 'grad_sgu_b', 'grad_m2_conv_w', 'grad_m2_conv_b', 'grad_m2_dt_bias', 'grad_m2_a_log', 'grad_m2_d', 'grad_m2_norm_w', 'grad_sc_conv_w', 'grad_merge_b', 'grad_w_branch', 'grad_w_out', 'grad_final_norm_w', 'delta_norm_w', 'delta_w_in', 'delta_s5_lambda_re', 'delta_s5_lambda_im', 'delta_s5_b_re', 'delta_s5_b_im', 'delta_s5_c_re', 'delta_s5_c_im', 'delta_s5_d', 'delta_s5_log_step', 'delta_s5_w_glu', 'delta_sgu_ln_w', 'delta_sgu_ln_b', 'delta_sgu_w', 'delta_sgu_b', 'delta_m2_conv_w', 'delta_m2_conv_b', 'delta_m2_dt_bias', 'delta_m2_a_log', 'delta_m2_d', 'delta_m2_norm_w', 'delta_sc_conv_w', 'delta_merge_b', 'delta_w_branch', 'delta_w_out', 'delta_final_norm_w', 'new_m_norm_w', 'new_m_w_in', 'new_m_s5_lambda_re', 'new_m_s5_lambda_im', 'new_m_s5_b_re', 'new_m_s5_b_im', 'new_m_s5_c_re', 'new_m_s5_c_im', 'new_m_s5_d', 'new_m_s5_log_step', 'new_m_s5_w_glu', 'new_m_sgu_ln_w', 'new_m_sgu_ln_b', 'new_m_sgu_w', 'new_m_sgu_b', 'new_m_m2_conv_w', 'new_m_m2_conv_b', 'new_m_m2_dt_bias', 'new_m_m2_a_log', 'new_m_m2_d', 'new_m_m2_norm_w', 'new_m_sc_conv_w', 'new_m_merge_b', 'new_m_w_branch', 'new_m_w_out', 'new_m_final_norm_w', 'new_v_norm_w', 'new_v_w_in', 'new_v_s5_lambda_re', 'new_v_s5_lambda_im', 'new_v_s5_b_re', 'new_v_s5_b_im', 'new_v_s5_c_re', 'new_v_s5_c_im', 'new_v_s5_d', 'new_v_s5_log_step', 'new_v_s5_w_glu', 'new_v_sgu_ln_w', 'new_v_sgu_ln_b', 'new_v_sgu_w', 'new_v_sgu_b', 'new_v_m2_conv_w', 'new_v_m2_conv_b', 'new_v_m2_dt_bias', 'new_v_m2_a_log', 'new_v_m2_d', 'new_v_m2_norm_w', 'new_v_sc_conv_w', 'new_v_merge_b', 'new_v_w_branch', 'new_v_w_out', 'new_v_final_norm_w']
TWIN_LEAF_KINDS = {'loss': 'loss', 'grad_x': 'grad_x', 'grad_norm_w': 'grad_w', 'grad_w_in': 'grad_w', 'grad_s5_lambda_re': 'grad_w', 'grad_s5_lambda_im': 'grad_w', 'grad_s5_b_re': 'grad_w', 'grad_s5_b_im': 'grad_w', 'grad_s5_c_re': 'grad_w', 'grad_s5_c_im': 'grad_w', 'grad_s5_d': 'grad_w', 'grad_s5_log_step': 'grad_w', 'grad_s5_w_glu': 'grad_w', 'grad_sgu_ln_w': 'grad_w', 'grad_sgu_ln_b': 'grad_w', 'grad_sgu_w': 'grad_w', 'grad_sgu_b': 'grad_w', 'grad_m2_conv_w': 'grad_w', 'grad_m2_conv_b': 'grad_w', 'grad_m2_dt_bias': 'grad_w', 'grad_m2_a_log': 'grad_w', 'grad_m2_d': 'grad_w', 'grad_m2_norm_w': 'grad_w', 'grad_sc_conv_w': 'grad_w', 'grad_merge_b': 'grad_w', 'grad_w_branch': 'grad_w', 'grad_w_out': 'grad_w', 'grad_final_norm_w': 'grad_w', 'delta_norm_w': 'delta_w', 'delta_w_in': 'delta_w', 'delta_s5_lambda_re': 'delta_w', 'delta_s5_lambda_im': 'delta_w', 'delta_s5_b_re': 'delta_w', 'delta_s5_b_im': 'delta_w', 'delta_s5_c_re': 'delta_w', 'delta_s5_c_im': 'delta_w', 'delta_s5_d': 'delta_w', 'delta_s5_log_step': 'delta_w', 'delta_s5_w_glu': 'delta_w', 'delta_sgu_ln_w': 'delta_w', 'delta_sgu_ln_b': 'delta_w', 'delta_sgu_w': 'delta_w', 'delta_sgu_b': 'delta_w', 'delta_m2_conv_w': 'delta_w', 'delta_m2_conv_b': 'delta_w', 'delta_m2_dt_bias': 'delta_w', 'delta_m2_a_log': 'delta_w', 'delta_m2_d': 'delta_w', 'delta_m2_norm_w': 'delta_w', 'delta_sc_conv_w': 'delta_w', 'delta_merge_b': 'delta_w', 'delta_w_branch': 'delta_w', 'delta_w_out': 'delta_w', 'delta_final_norm_w': 'delta_w', 'new_m_norm_w': 'new_m', 'new_m_w_in': 'new_m', 'new_m_s5_lambda_re': 'new_m', 'new_m_s5_lambda_im': 'new_m', 'new_m_s5_b_re': 'new_m', 'new_m_s5_b_im': 'new_m', 'new_m_s5_c_re': 'new_m', 'new_m_s5_c_im': 'new_m', 'new_m_s5_d': 'new_m', 'new_m_s5_log_step': 'new_m', 'new_m_s5_w_glu': 'new_m', 'new_m_sgu_ln_w': 'new_m', 'new_m_sgu_ln_b': 'new_m', 'new_m_sgu_w': 'new_m', 'new_m_sgu_b': 'new_m', 'new_m_m2_conv_w': 'new_m', 'new_m_m2_conv_b': 'new_m', 'new_m_m2_dt_bias': 'new_m', 'new_m_m2_a_log': 'new_m', 'new_m_m2_d': 'new_m', 'new_m_m2_norm_w': 'new_m', 'new_m_sc_conv_w': 'new_m', 'new_m_merge_b': 'new_m', 'new_m_w_branch': 'new_m', 'new_m_w_out': 'new_m', 'new_m_final_norm_w': 'new_m', 'new_v_norm_w': 'new_v', 'new_v_w_in': 'new_v', 'new_v_s5_lambda_re': 'new_v', 'new_v_s5_lambda_im': 'new_v', 'new_v_s5_b_re': 'new_v', 'new_v_s5_b_im': 'new_v', 'new_v_s5_c_re': 'new_v', 'new_v_s5_c_im': 'new_v', 'new_v_s5_d': 'new_v', 'new_v_s5_log_step': 'new_v', 'new_v_s5_w_glu': 'new_v', 'new_v_sgu_ln_w': 'new_v', 'new_v_sgu_ln_b': 'new_v', 'new_v_sgu_w': 'new_v', 'new_v_sgu_b': 'new_v', 'new_v_m2_conv_w': 'new_v', 'new_v_m2_conv_b': 'new_v', 'new_v_m2_dt_bias': 'new_v', 'new_v_m2_a_log': 'new_v', 'new_v_m2_d': 'new_v', 'new_v_m2_norm_w': 'new_v', 'new_v_sc_conv_w': 'new_v', 'new_v_merge_b': 'new_v', 'new_v_w_branch': 'new_v', 'new_v_w_out': 'new_v', 'new_v_final_norm_w': 'new_v'}


def _forward(args):
    return _fwd_reference(*[args[k] for k in FWD_PARAMS])


def _output_shape():
    out = _jax.eval_shape(lambda: _forward(_fwd_setup_inputs(0)))
    return out.shape, out.dtype

N_MICROBATCH = 1
ADAM_LR = 0.001
ADAM_B1 = 0.9
ADAM_B2 = 0.999
ADAM_EPS = 1e-08
ADAM_WD = 0.01
ADAM_STEP = 10
PER_EXAMPLE_BATCH_AXIS = {'x': 0, 'loss_target': 0}
SHARED_INPUTS = []
_WEIGHT_DTYPES = {'norm_w': _jnp.float32, 'w_in': _jnp.float32, 's5_lambda_re': _jnp.float32, 's5_lambda_im': _jnp.float32, 's5_b_re': _jnp.float32, 's5_b_im': _jnp.float32, 's5_c_re': _jnp.float32, 's5_c_im': _jnp.float32, 's5_d': _jnp.float32, 's5_log_step': _jnp.float32, 's5_w_glu': _jnp.float32, 'sgu_ln_w': _jnp.float32, 'sgu_ln_b': _jnp.float32, 'sgu_w': _jnp.float32, 'sgu_b': _jnp.float32, 'm2_conv_w': _jnp.float32, 'm2_conv_b': _jnp.float32, 'm2_dt_bias': _jnp.float32, 'm2_a_log': _jnp.float32, 'm2_d': _jnp.float32, 'm2_norm_w': _jnp.float32, 'sc_conv_w': _jnp.float32, 'merge_b': _jnp.float32, 'w_branch': _jnp.float32, 'w_out': _jnp.float32, 'final_norm_w': _jnp.float32}
MOMENT_SCALE = {'norm_w': 1.308815e-01, 'w_in': 4.019845e-02, 's5_lambda_re': 8.344069e-04, 's5_lambda_im': 7.880025e-04, 's5_b_re': 5.525262e-04, 's5_b_im': 5.583841e-04, 's5_c_re': 1.106960e-03, 's5_c_im': 1.125221e-03, 's5_d': 1.869913e-02, 's5_log_step': 6.553393e-01, 's5_w_glu': 5.237449e-03, 'sgu_ln_w': 2.220113e-02, 'sgu_ln_b': 2.278997e-02, 'sgu_w': 1.626101e-02, 'sgu_b': 2.340687e-02, 'm2_conv_w': 6.343152e-02, 'm2_conv_b': 8.003949e-02, 'm2_dt_bias': 1.934907e-01, 'm2_a_log': 2.628183e-01, 'm2_d': 4.095495e-01, 'm2_norm_w': 8.421434e-02, 'sc_conv_w': 4.893299e-02, 'merge_b': 1.461068e-02, 'w_branch': 3.747250e-02, 'w_out': 7.468005e-02, 'final_norm_w': 1.601545e+01}


def _to_microbatches(a, axis):
    t = _jnp.moveaxis(a, axis, 0)
    t = t.reshape((N_MICROBATCH, t.shape[0] // N_MICROBATCH) + t.shape[1:])
    return _jnp.moveaxis(t, 1, axis + 1)


def setup_inputs(seed: int = 0) -> dict:
    inp = _fwd_setup_inputs(seed)
    key = _jax.random.fold_in(_jax.random.key(seed), 7919)
    shape, _ = _output_shape()
    out = dict(inp)
    out["loss_target"] = _jax.random.normal(_jax.random.fold_in(key, 0), shape, _jnp.float32)
    for i, name in enumerate(TWIN_WEIGHTS):
        w = inp[name].astype(_jnp.float32)
        if MOMENT_SCALE is None:
            s = _jnp.sqrt(_jnp.mean(_jnp.square(w)) + 1e-30)
        else:
            s = MOMENT_SCALE[name]
        km, kv = _jax.random.split(_jax.random.fold_in(key, i + 1))
        out[name] = w
        out["m_" + name] = s * _jax.random.normal(km, w.shape, _jnp.float32)
        out["v_" + name] = (s * s) * _jax.random.uniform(kv, w.shape, _jnp.float32, 0.5, 1.5)
    if N_MICROBATCH > 1:
        for name, axis in PER_EXAMPLE_BATCH_AXIS.items():
            out[name] = _to_microbatches(out[name], axis)
    return {'x': out['x'], 'norm_w': out['norm_w'], 'w_in': out['w_in'], 's5_lambda_re': out['s5_lambda_re'], 's5_lambda_im': out['s5_lambda_im'], 's5_b_re': out['s5_b_re'], 's5_b_im': out['s5_b_im'], 's5_c_re': out['s5_c_re'], 's5_c_im': out['s5_c_im'], 's5_d': out['s5_d'], 's5_log_step': out['s5_log_step'], 's5_w_glu': out['s5_w_glu'], 'sgu_ln_w': out['sgu_ln_w'], 'sgu_ln_b': out['sgu_ln_b'], 'sgu_w': out['sgu_w'], 'sgu_b': out['sgu_b'], 'm2_conv_w': out['m2_conv_w'], 'm2_conv_b': out['m2_conv_b'], 'm2_dt_bias': out['m2_dt_bias'], 'm2_a_log': out['m2_a_log'], 'm2_d': out['m2_d'], 'm2_norm_w': out['m2_norm_w'], 'sc_conv_w': out['sc_conv_w'], 'merge_b': out['merge_b'], 'w_branch': out['w_branch'], 'w_out': out['w_out'], 'final_norm_w': out['final_norm_w'], 'loss_target': out['loss_target'], 'm_norm_w': out['m_norm_w'], 'm_w_in': out['m_w_in'], 'm_s5_lambda_re': out['m_s5_lambda_re'], 'm_s5_lambda_im': out['m_s5_lambda_im'], 'm_s5_b_re': out['m_s5_b_re'], 'm_s5_b_im': out['m_s5_b_im'], 'm_s5_c_re': out['m_s5_c_re'], 'm_s5_c_im': out['m_s5_c_im'], 'm_s5_d': out['m_s5_d'], 'm_s5_log_step': out['m_s5_log_step'], 'm_s5_w_glu': out['m_s5_w_glu'], 'm_sgu_ln_w': out['m_sgu_ln_w'], 'm_sgu_ln_b': out['m_sgu_ln_b'], 'm_sgu_w': out['m_sgu_w'], 'm_sgu_b': out['m_sgu_b'], 'm_m2_conv_w': out['m_m2_conv_w'], 'm_m2_conv_b': out['m_m2_conv_b'], 'm_m2_dt_bias': out['m_m2_dt_bias'], 'm_m2_a_log': out['m_m2_a_log'], 'm_m2_d': out['m_m2_d'], 'm_m2_norm_w': out['m_m2_norm_w'], 'm_sc_conv_w': out['m_sc_conv_w'], 'm_merge_b': out['m_merge_b'], 'm_w_branch': out['m_w_branch'], 'm_w_out': out['m_w_out'], 'm_final_norm_w': out['m_final_norm_w'], 'v_norm_w': out['v_norm_w'], 'v_w_in': out['v_w_in'], 'v_s5_lambda_re': out['v_s5_lambda_re'], 'v_s5_lambda_im': out['v_s5_lambda_im'], 'v_s5_b_re': out['v_s5_b_re'], 'v_s5_b_im': out['v_s5_b_im'], 'v_s5_c_re': out['v_s5_c_re'], 'v_s5_c_im': out['v_s5_c_im'], 'v_s5_d': out['v_s5_d'], 'v_s5_log_step': out['v_s5_log_step'], 'v_s5_w_glu': out['v_s5_w_glu'], 'v_sgu_ln_w': out['v_sgu_ln_w'], 'v_sgu_ln_b': out['v_sgu_ln_b'], 'v_sgu_w': out['v_sgu_w'], 'v_sgu_b': out['v_sgu_b'], 'v_m2_conv_w': out['v_m2_conv_w'], 'v_m2_conv_b': out['v_m2_conv_b'], 'v_m2_dt_bias': out['v_m2_dt_bias'], 'v_m2_a_log': out['v_m2_a_log'], 'v_m2_d': out['v_m2_d'], 'v_m2_norm_w': out['v_m2_norm_w'], 'v_sc_conv_w': out['v_sc_conv_w'], 'v_merge_b': out['v_merge_b'], 'v_w_branch': out['v_w_branch'], 'v_w_out': out['v_w_out'], 'v_final_norm_w': out['v_final_norm_w']}


def _loss(weights, diff, rest, loss_target):
    with _jax.named_scope("forward"):
        args = {**rest, TWIN_DIFF_INPUT: diff, **{k: w.astype(_WEIGHT_DTYPES[k]) for k, w in weights.items()}}
        y = _forward(args)
    with _jax.named_scope("loss_head"):
        err = _jnp.square(y.astype(_jnp.float32) - loss_target)
        return 0.5 * _jnp.sum(_jnp.mean(err, axis=-1)) if err.ndim else 0.5 * err


def _adamw(w, g, m, v):
    m = ADAM_B1 * m + (1.0 - ADAM_B1) * g
    v = ADAM_B2 * v + (1.0 - ADAM_B2) * _jnp.square(g)
    m_hat = m / (1.0 - ADAM_B1 ** ADAM_STEP)
    v_hat = v / (1.0 - ADAM_B2 ** ADAM_STEP)
    delta = -ADAM_LR * (m_hat / (_jnp.sqrt(v_hat) + ADAM_EPS) + ADAM_WD * w)
    return delta, m, v


def reference(x, norm_w, w_in, s5_lambda_re, s5_lambda_im, s5_b_re, s5_b_im, s5_c_re, s5_c_im, s5_d, s5_log_step, s5_w_glu, sgu_ln_w, sgu_ln_b, sgu_w, sgu_b, m2_conv_w, m2_conv_b, m2_dt_bias, m2_a_log, m2_d, m2_norm_w, sc_conv_w, merge_b, w_branch, w_out, final_norm_w, loss_target, m_norm_w, m_w_in, m_s5_lambda_re, m_s5_lambda_im, m_s5_b_re, m_s5_b_im, m_s5_c_re, m_s5_c_im, m_s5_d, m_s5_log_step, m_s5_w_glu, m_sgu_ln_w, m_sgu_ln_b, m_sgu_w, m_sgu_b, m_m2_conv_w, m_m2_conv_b, m_m2_dt_bias, m_m2_a_log, m_m2_d, m_m2_norm_w, m_sc_conv_w, m_merge_b, m_w_branch, m_w_out, m_final_norm_w, v_norm_w, v_w_in, v_s5_lambda_re, v_s5_lambda_im, v_s5_b_re, v_s5_b_im, v_s5_c_re, v_s5_c_im, v_s5_d, v_s5_log_step, v_s5_w_glu, v_sgu_ln_w, v_sgu_ln_b, v_sgu_w, v_sgu_b, v_m2_conv_w, v_m2_conv_b, v_m2_dt_bias, v_m2_a_log, v_m2_d, v_m2_norm_w, v_sc_conv_w, v_merge_b, v_w_branch, v_w_out, v_final_norm_w):
    given = dict(x=x, norm_w=norm_w, w_in=w_in, s5_lambda_re=s5_lambda_re, s5_lambda_im=s5_lambda_im, s5_b_re=s5_b_re, s5_b_im=s5_b_im, s5_c_re=s5_c_re, s5_c_im=s5_c_im, s5_d=s5_d, s5_log_step=s5_log_step, s5_w_glu=s5_w_glu, sgu_ln_w=sgu_ln_w, sgu_ln_b=sgu_ln_b, sgu_w=sgu_w, sgu_b=sgu_b, m2_conv_w=m2_conv_w, m2_conv_b=m2_conv_b, m2_dt_bias=m2_dt_bias, m2_a_log=m2_a_log, m2_d=m2_d, m2_norm_w=m2_norm_w, sc_conv_w=sc_conv_w, merge_b=merge_b, w_branch=w_branch, w_out=w_out, final_norm_w=final_norm_w, loss_target=loss_target, m_norm_w=m_norm_w, m_w_in=m_w_in, m_s5_lambda_re=m_s5_lambda_re, m_s5_lambda_im=m_s5_lambda_im, m_s5_b_re=m_s5_b_re, m_s5_b_im=m_s5_b_im, m_s5_c_re=m_s5_c_re, m_s5_c_im=m_s5_c_im, m_s5_d=m_s5_d, m_s5_log_step=m_s5_log_step, m_s5_w_glu=m_s5_w_glu, m_sgu_ln_w=m_sgu_ln_w, m_sgu_ln_b=m_sgu_ln_b, m_sgu_w=m_sgu_w, m_sgu_b=m_sgu_b, m_m2_conv_w=m_m2_conv_w, m_m2_conv_b=m_m2_conv_b, m_m2_dt_bias=m_m2_dt_bias, m_m2_a_log=m_m2_a_log, m_m2_d=m_m2_d, m_m2_norm_w=m_m2_norm_w, m_sc_conv_w=m_sc_conv_w, m_merge_b=m_merge_b, m_w_branch=m_w_branch, m_w_out=m_w_out, m_final_norm_w=m_final_norm_w, v_norm_w=v_norm_w, v_w_in=v_w_in, v_s5_lambda_re=v_s5_lambda_re, v_s5_lambda_im=v_s5_lambda_im, v_s5_b_re=v_s5_b_re, v_s5_b_im=v_s5_b_im, v_s5_c_re=v_s5_c_re, v_s5_c_im=v_s5_c_im, v_s5_d=v_s5_d, v_s5_log_step=v_s5_log_step, v_s5_w_glu=v_s5_w_glu, v_sgu_ln_w=v_sgu_ln_w, v_sgu_ln_b=v_sgu_ln_b, v_sgu_w=v_sgu_w, v_sgu_b=v_sgu_b, v_m2_conv_w=v_m2_conv_w, v_m2_conv_b=v_m2_conv_b, v_m2_dt_bias=v_m2_dt_bias, v_m2_a_log=v_m2_a_log, v_m2_d=v_m2_d, v_m2_norm_w=v_m2_norm_w, v_sc_conv_w=v_sc_conv_w, v_merge_b=v_merge_b, v_w_branch=v_w_branch, v_w_out=v_w_out, v_final_norm_w=v_final_norm_w)
    weights = {n: given[n] for n in TWIN_WEIGHTS}
    shared = {n: given[n] for n in SHARED_INPUTS}
    per_example = {n: given[n] for n in ['x']}
    grad_fn = _jax.value_and_grad(_loss, argnums=(0, 1))

    def one_microbatch(ex, loss_target):
        ex = dict(ex)
        diff = ex.pop(TWIN_DIFF_INPUT)
        return grad_fn(weights, diff, {**shared, **ex}, loss_target)

    if N_MICROBATCH == 1:
        loss, (grad_w, grad_x) = one_microbatch(per_example, given["loss_target"])
    else:
        def body(carry, xs):
            loss_sum, grad_sum = carry
            l_k, (gw_k, gx_k) = one_microbatch(xs[0], xs[1])
            with _jax.named_scope("update"):
                return (loss_sum + l_k, _jax.tree.map(_jnp.add, grad_sum, gw_k)), gx_k

        init = (_jnp.zeros((), _jnp.float32), _jax.tree.map(_jnp.zeros_like, weights))
        (loss, grad_w), grad_x = _jax.lax.scan(body, init, (per_example, given["loss_target"]))
    with _jax.named_scope("update"):
        delta_w, new_m, new_v = {}, {}, {}
        for n in TWIN_WEIGHTS:
            delta_w[n], new_m[n], new_v[n] = _adamw(weights[n], grad_w[n], given["m_" + n], given["v_" + n])
    return (loss, grad_x, *[grad_w[n] for n in TWIN_WEIGHTS], *[delta_w[n] for n in TWIN_WEIGHTS],
            *[new_m[n] for n in TWIN_WEIGHTS], *[new_v[n] for n in TWIN_WEIGHTS])
```

```python
import functools

import jax
import jax.numpy as jnp
from jax import lax
from jax.experimental import pallas as pl
from jax.experimental.pallas import tpu as pltpu

F32 = jnp.float32
BF16 = jnp.bfloat16

N_DEV = 8
SEQ = 2048
D_MODEL = 1024
DEPTH = 2
BW = 512
N_BRANCH = 4
EPS = 1e-6
S5_GROUPS, S5_STATE, S5_P = 32, 64, 16
S5_CH = S5_GROUPS * S5_STATE
SGU_CHUNK, SGU_HEADS = 128, 8
M2_HEADS, M2_HEAD_DIM, M2_STATE, M2_CHUNK, M2_CONV = 8, 64, 128, 128, 4
M2_CONV_CH = 1024
SC_CONV = 3
IN_DIM = 10248
DT_COL = 4096
DT_PAD = 120
IN_PAD = IN_DIM + DT_PAD
C_MERGE = 0
C_S5U, C_S5G, C_SGU_U, C_SGU_V, C_SGU_G, C_M2Z, C_M2X = 4096, 4608, 5120, 5632, 6144, 6656, 7168
C_DT, C_SCB, C_SCC, C_SCH, C_SCG = 8192, 8320, 8832, 9344, 9856
MERGE_COL = 6152
SHARD_IN = IN_DIM // N_DEV

ADAM_LR, ADAM_B1, ADAM_B2, ADAM_EPS, ADAM_WD, ADAM_STEP = 0.001, 0.9, 0.999, 1e-08, 0.01, 10

VMEM_LIMIT = 56 * 1024 * 1024
LANES = 128

MESH = pl.DeviceIdType.MESH


def _cparams(sem=None, **kw):
    return pltpu.CompilerParams(dimension_semantics=sem, vmem_limit_bytes=VMEM_LIMIT, **kw)


def _dg(a, b, ca, cb, precision=None):
    return lax.dot_general(a, b, (((ca,), (cb,)), ((), ())), precision=precision,
                           preferred_element_type=F32)


@functools.partial(jax.custom_vjp, nondiff_argnums=(2, 3))
def _bdot(a, b, ca, cb):
    return _dg(a.astype(BF16), b.astype(BF16), ca, cb)


def _bdot_fwd(a, b, ca, cb):
    return _bdot(a, b, ca, cb), (a, b)


def _bdot_bwd(ca, cb, res, g):
    a, b = res
    gb, ab, bb = g.astype(BF16), a.astype(BF16), b.astype(BF16)
    da = _dg(gb, bb, 1, 1 - cb) if ca == 1 else _dg(bb, gb, 1 - cb, 1)
    db = _dg(ab, gb, 1 - ca, 0) if cb == 0 else _dg(gb, ab, 0, 1 - ca)
    return da.astype(a.dtype), db.astype(b.dtype)


_bdot.defvjp(_bdot_fwd, _bdot_bwd)


def _rms(x, w):
    return x * lax.rsqrt(jnp.mean(x * x, axis=-1, keepdims=True) + EPS) * w


def _silu(x):
    return x * jax.nn.sigmoid(x)


def _gelu(x):
    return 0.5 * x * (1.0 + jnp.tanh(0.7978845608028654 * (x + 0.044715 * (x * x * x))))


def _softplus(x):
    return jnp.maximum(x, 0.0) + jnp.log1p(jnp.exp(-jnp.abs(x)))


def _shift_down(x, s):
    if s == 0:
        return x
    row = lax.broadcasted_iota(jnp.int32, x.shape, 0)
    return jnp.where(row >= s, pltpu.roll(x, s, 0), 0.0)


def _shift_up(x, s):
    if s == 0:
        return x
    n = x.shape[0]
    row = lax.broadcasted_iota(jnp.int32, x.shape, 0)
    return jnp.where(row < n - s, pltpu.roll(x, n - s, 0), 0.0)


def _matmul(a, b, ca, cb, out_dtype, tm, tn, tk, name, residual=None):
    m = a.shape[1 - ca]
    k = a.shape[ca]
    n = b.shape[1 - cb]
    assert b.shape[cb] == k and m % tm == 0 and n % tn == 0 and k % tk == 0
    nk = k // tk
    a_spec = pl.BlockSpec((tm, tk), lambda i, j, kk: (i, kk)) if ca == 1 else pl.BlockSpec((tk, tm), lambda i, j, kk: (kk, i))
    b_spec = pl.BlockSpec((tk, tn), lambda i, j, kk: (kk, j)) if cb == 0 else pl.BlockSpec((tn, tk), lambda i, j, kk: (j, kk))
    o_spec = pl.BlockSpec((tm, tn), lambda i, j, kk: (i, j))
    has_res = residual is not None

    def body(*refs):
        if has_res:
            a_ref, b_ref, r_ref, o_ref, acc = refs
        else:
            a_ref, b_ref, o_ref, acc = refs
        kk = pl.program_id(2)
        part = _dg(a_ref[...].astype(BF16), b_ref[...].astype(BF16), ca, cb)

        @pl.when(kk == 0)
        def _():
            acc[...] = part

        @pl.when(kk > 0)
        def _():
            acc[...] += part

        @pl.when(kk == nk - 1)
        def _():
            r = acc[...]
            if has_res:
                r = r + r_ref[...]
            o_ref[...] = r.astype(out_dtype)

    ins = [a, b] + ([residual] if has_res else [])
    specs = [a_spec, b_spec] + ([o_spec] if has_res else [])
    return pl.pallas_call(
        body, name=name, grid=(m // tm, n // tn, nk), in_specs=specs, out_specs=o_spec,
        out_shape=jax.ShapeDtypeStruct((m, n), out_dtype),
        scratch_shapes=[pltpu.VMEM((tm, tn), F32)],
        compiler_params=_cparams(("parallel", "parallel", "arbitrary")),
    )(*ins)


ROW_TILE = 512


def _rmsnorm_fwd(x, w, name):
    def body(x_ref, w_ref, o_ref):
        o_ref[...] = _rms(x_ref[...], w_ref[...]).astype(BF16)

    return pl.pallas_call(
        body, name=name, grid=(SEQ // ROW_TILE,),
        in_specs=[pl.BlockSpec((ROW_TILE, D_MODEL), lambda i: (i, 0)), pl.BlockSpec((1, D_MODEL), lambda i: (0, 0))],
        out_specs=pl.BlockSpec((ROW_TILE, D_MODEL), lambda i: (i, 0)),
        out_shape=jax.ShapeDtypeStruct((SEQ, D_MODEL), BF16),
        compiler_params=_cparams(("parallel",)),
    )(x, w)


def _rmsnorm_bwd(x, w, dh, dres, name):
    def body(x_ref, w_ref, dh_ref, dres_ref, dx_ref, dw_ref):
        _, vjp = jax.vjp(_rms, x_ref[...], w_ref[...])
        dx, dw = vjp(dh_ref[...])
        dx_ref[...] = dx + dres_ref[...]

        @pl.when(pl.program_id(0) == 0)
        def _():
            dw_ref[...] = dw

        @pl.when(pl.program_id(0) > 0)
        def _():
            dw_ref[...] += dw

    tile = pl.BlockSpec((ROW_TILE, D_MODEL), lambda i: (i, 0))
    vec = pl.BlockSpec((1, D_MODEL), lambda i: (0, 0))
    return pl.pallas_call(
        body, name=name, grid=(SEQ // ROW_TILE,),
        in_specs=[tile, vec, tile, tile], out_specs=[tile, vec],
        out_shape=[jax.ShapeDtypeStruct((SEQ, D_MODEL), F32), jax.ShapeDtypeStruct((1, D_MODEL), F32)],
        compiler_params=_cparams(("arbitrary",)),
    )(x, w, dh, dres)


def _loss_head(x, w, target):
    def body(x_ref, w_ref, t_ref, loss_ref, dx_ref, dw_ref):
        tgt = t_ref[...]

        def f(xv, wv):
            err = _rms(xv, wv) - tgt
            return 0.5 * jnp.sum(jnp.mean(err * err, axis=-1))

        loss, vjp = jax.vjp(f, x_ref[...], w_ref[...])
        dx, dw = vjp(jnp.ones((), F32))
        dx_ref[...] = dx
        lrow = jnp.full((1, LANES), loss, F32)

        @pl.when(pl.program_id(0) == 0)
        def _():
            dw_ref[...] = dw
            loss_ref[...] = lrow

        @pl.when(pl.program_id(0) > 0)
        def _():
            dw_ref[...] += dw
            loss_ref[...] += lrow

    tile = pl.BlockSpec((ROW_TILE, D_MODEL), lambda i: (i, 0))
    vec = pl.BlockSpec((1, D_MODEL), lambda i: (0, 0))
    return pl.pallas_call(
        body, name="loss_head", grid=(SEQ // ROW_TILE,),
        in_specs=[tile, vec, tile], out_specs=[pl.BlockSpec((1, LANES), lambda i: (0, 0)), tile, vec],
        out_shape=[jax.ShapeDtypeStruct((1, LANES), F32), jax.ShapeDtypeStruct((SEQ, D_MODEL), F32),
                   jax.ShapeDtypeStruct((1, D_MODEL), F32)],
        compiler_params=_cparams(("arbitrary",)),
    )(x, w, target)


S5_T = 256


def _s5_post(ypre, gate, wglu):
    y = _gelu(ypre)
    y = y * jax.nn.sigmoid(_bdot(y, wglu, 1, 0))
    return y * _silu(gate)


def _s5_fwd(proj, bbre, bbim, cre, cim, a2, dvec, wglu, name):
    def body(u_ref, g_ref, bbre_ref, bbim_ref, cre_ref, cim_ref, a_ref, d_ref, wg_ref, o_ref, sre_ref, sim_ref, st):
        @pl.when(pl.program_id(0) == 0)
        def _():
            st[...] = jnp.zeros_like(st)

        u = u_ref[...]
        ub = u.astype(BF16)
        sre_ref[...] = _dg(ub, bbre_ref[...], 1, 0)
        sim_ref[...] = _dg(ub, bbim_ref[...], 1, 0)
        ar, ai = a_ref[0:1, :], a_ref[1:2, :]

        def step(t, carry):
            sr, si = carry
            nr = ar * sr - ai * si + sre_ref[pl.ds(t, 1), :]
            ni = ar * si + ai * sr + sim_ref[pl.ds(t, 1), :]
            sre_ref[pl.ds(t, 1), :] = nr
            sim_ref[pl.ds(t, 1), :] = ni
            return nr, ni

        sr, si = lax.fori_loop(0, S5_T, step, (st[0:1, :], st[1:2, :]), unroll=8)
        st[0:1, :] = sr
        st[1:2, :] = si
        ypre = (_dg(sre_ref[...].astype(BF16), cre_ref[...], 1, 0) - _dg(sim_ref[...].astype(BF16), cim_ref[...], 1, 0)
                + d_ref[...] * u)
        o_ref[...] = _s5_post(ypre, g_ref[...], wg_ref[...]).astype(BF16)

    full = lambda shape: pl.BlockSpec(shape, lambda c: (0, 0))
    return pl.pallas_call(
        body, name=name, grid=(SEQ // S5_T,),
        in_specs=[pl.BlockSpec((S5_T, BW), lambda c: (c, C_S5U // BW)), pl.BlockSpec((S5_T, BW), lambda c: (c, C_S5G // BW)),
                  full((BW, S5_CH)), full((BW, S5_CH)), full((S5_CH, BW)), full((S5_CH, BW)),
                  full((2, S5_CH)), full((1, BW)), full((BW, BW))],
        out_specs=[pl.BlockSpec((S5_T, BW), lambda c: (c, 0)), pl.BlockSpec((S5_T, S5_CH), lambda c: (c, 0)),
                   pl.BlockSpec((S5_T, S5_CH), lambda c: (c, 0))],
        out_shape=[jax.ShapeDtypeStruct((SEQ, BW), BF16), jax.ShapeDtypeStruct((SEQ, S5_CH), F32),
                   jax.ShapeDtypeStruct((SEQ, S5_CH), F32)],
        scratch_shapes=[pltpu.VMEM((2, S5_CH), F32)],
        compiler_params=_cparams(("arbitrary",)),
    )(proj, proj, bbre, bbim, cre, cim, a2, dvec, wglu)


def _s5_bwd(proj, dout, sre, sim, bbre, bbim, cre, cim, a2, dvec, wglu, name):
    nc = SEQ // S5_T

    def body(u_ref, g_ref, do_ref, sre_ref, sim_ref, pre_ref, pim_ref, bbre_ref, bbim_ref, cre_ref, cim_ref, a_ref,
             d_ref, wg_ref, dp_ref, dbbre_ref, dbbim_ref, dcre_ref, dcim_ref, da_ref, dd_ref, dwg_ref,
             gre, gim, st):
        c = nc - 1 - pl.program_id(0)

        @pl.when(pl.program_id(0) == 0)
        def _():
            st[...] = jnp.zeros_like(st)
            for r in (dbbre_ref, dbbim_ref, dcre_ref, dcim_ref, da_ref, dd_ref, dwg_ref):
                r[...] = jnp.zeros_like(r)

        u = u_ref[...]
        s_re, s_im = sre_ref[...], sim_ref[...]

        def head(s_re, s_im, cre_v, cim_v, dv, uv, gv, wg):
            ypre = _bdot(s_re, cre_v, 1, 0) - _bdot(s_im, cim_v, 1, 0) + dv * uv
            return _s5_post(ypre, gv, wg)

        _, vjp = jax.vjp(head, s_re, s_im, cre_ref[...].astype(F32), cim_ref[...].astype(F32), d_ref[...], u,
                         g_ref[...], wg_ref[...].astype(F32))
        ds_re, ds_im, dcre, dcim, dd, du_d, dgate, dwg = vjp(do_ref[0])
        dcre_ref[...] += dcre
        dcim_ref[...] += dcim
        dd_ref[...] += dd
        dwg_ref[...] += dwg
        dp_ref[:, BW:] = dgate
        gre[...] = ds_re
        gim[...] = ds_im
        ar, ai = a_ref[0:1, :], a_ref[1:2, :]

        def step(i, carry):
            t = S5_T - 1 - i
            gr, gi = carry
            nr = gre[pl.ds(t, 1), :] + gr
            ni = gim[pl.ds(t, 1), :] + gi
            gre[pl.ds(t, 1), :] = nr
            gim[pl.ds(t, 1), :] = ni
            return ar * nr + ai * ni, ar * ni - ai * nr

        gr, gi = lax.fori_loop(0, S5_T, step, (st[0:1, :], st[1:2, :]), unroll=8)
        st[0:1, :] = gr
        st[1:2, :] = gi
        g_re, g_im = gre[...], gim[...]
        first = jnp.where(c > 0, 1.0, 0.0)
        row = lax.broadcasted_iota(jnp.int32, (S5_T, S5_CH), 0)
        p_re = jnp.where(row == 0, pre_ref[7:8, :] * first, pltpu.roll(s_re, 1, 0))
        p_im = jnp.where(row == 0, pim_ref[7:8, :] * first, pltpu.roll(s_im, 1, 0))
        da_ref[0:1, :] += jnp.sum(g_re * p_re + g_im * p_im, axis=0, keepdims=True)
        da_ref[1:2, :] += jnp.sum(g_im * p_re - g_re * p_im, axis=0, keepdims=True)
        ub, grb, gib = u.astype(BF16), g_re.astype(BF16), g_im.astype(BF16)
        dbbre_ref[...] += _dg(ub, grb, 0, 0)
        dbbim_ref[...] += _dg(ub, gib, 0, 0)
        dp_ref[:, :BW] = du_d + _dg(grb, bbre_ref[...], 1, 1) + _dg(gib, bbim_ref[...], 1, 1)

    full = lambda shape: pl.BlockSpec(shape, lambda i: (0, 0))
    rev = lambda w, col=0: pl.BlockSpec((S5_T, w), lambda i: (nc - 1 - i, col))
    prev = pl.BlockSpec((8, S5_CH), lambda i: (jnp.maximum((nc - 1 - i) * (S5_T // 8) - 1, 0), 0))
    return pl.pallas_call(
        body, name=name, grid=(nc,),
        in_specs=[rev(BW, C_S5U // BW), rev(BW, C_S5G // BW), pl.BlockSpec((1, S5_T, BW), lambda i: (0, nc - 1 - i, 0)),
                  rev(S5_CH), rev(S5_CH), prev, prev,
                  full((BW, S5_CH)), full((BW, S5_CH)), full((S5_CH, BW)), full((S5_CH, BW)),
                  full((2, S5_CH)), full((1, BW)), full((BW, BW))],
        out_specs=[rev(2 * BW), full((BW, S5_CH)), full((BW, S5_CH)), full((S5_CH, BW)), full((S5_CH, BW)),
                   full((2, S5_CH)), full((1, BW)), full((BW, BW))],
        out_shape=[jax.ShapeDtypeStruct((SEQ, 2 * BW), F32),
                   jax.ShapeDtypeStruct((BW, S5_CH), F32), jax.ShapeDtypeStruct((BW, S5_CH), F32),
                   jax.ShapeDtypeStruct((S5_CH, BW), F32), jax.ShapeDtypeStruct((S5_CH, BW), F32),
                   jax.ShapeDtypeStruct((2, S5_CH), F32), jax.ShapeDtypeStruct((1, BW), F32),
                   jax.ShapeDtypeStruct((BW, BW), F32)],
        scratch_shapes=[pltpu.VMEM((S5_T, S5_CH), F32), pltpu.VMEM((S5_T, S5_CH), F32), pltpu.VMEM((2, S5_CH), F32)],
        compiler_params=_cparams(("arbitrary",)),
    )(proj, proj, dout, sre, sim, sre, sim, bbre, bbim, cre, cim, a2, dvec, wglu)


def _s5_disc(lam_re, lam_im, b_re, b_im, c_re, c_im, d, log_step):
    step = jnp.exp(log_step)[:, None]
    mag = jnp.exp(lam_re * step)
    ab_re, ab_im = mag * jnp.cos(lam_im * step), mag * jnp.sin(lam_im * step)
    den = lam_re * lam_re + lam_im * lam_im
    nr = ab_re - 1.0
    coef_re = (nr * lam_re + ab_im * lam_im) / den
    coef_im = (ab_im * lam_re - nr * lam_im) / den
    bb_re = coef_re[..., None] * b_re - coef_im[..., None] * b_im
    bb_im = coef_re[..., None] * b_im + coef_im[..., None] * b_re
    eye = jnp.eye(S5_GROUPS, dtype=F32)
    bbre = jnp.einsum('gnp,gh->gphn', bb_re, eye).reshape(BW, S5_CH)
    bbim = jnp.einsum('gnp,gh->gphn', bb_im, eye).reshape(BW, S5_CH)
    cre = jnp.einsum('gpn,gh->gnhp', c_re, eye).reshape(S5_CH, BW)
    cim = jnp.einsum('gpn,gh->gnhp', c_im, eye).reshape(S5_CH, BW)
    a2 = jnp.stack([ab_re.reshape(-1), ab_im.reshape(-1)])
    return bbre, bbim, cre, cim, a2, d.reshape(1, BW)


def _left_lanes(shape):
    return lax.broadcasted_iota(jnp.int32, shape, 1) < 64


def _sgu_chunk(u, v, gate, ln_w, ln_b, w, bias):
    u32, v32 = _gelu(u), _gelu(v)
    mu = jnp.mean(v32, axis=-1, keepdims=True)
    var = jnp.mean(jnp.square(v32 - mu), axis=-1, keepdims=True)
    vn = (v32 - mu) * lax.rsqrt(var + EPS) * ln_w + ln_b
    t_i = lax.broadcasted_iota(jnp.int32, (SGU_CHUNK, SGU_CHUNK), 0)
    s_i = lax.broadcasted_iota(jnp.int32, (SGU_CHUNK, SGU_CHUNK), 1)
    causal = t_i >= s_i
    left = _left_lanes((SGU_CHUNK, LANES))
    sgate = _silu(gate)
    outs = []
    for j in range(BW // LANES):
        vb = vn[:, j * LANES:(j + 1) * LANES]
        s_blk = (_bdot(jnp.where(causal, w[2 * j], 0.0), jnp.where(left, vb, 0.0), 1, 0)
                 + _bdot(jnp.where(causal, w[2 * j + 1], 0.0), jnp.where(left, 0.0, vb), 1, 0))
        sl = slice(j * LANES, (j + 1) * LANES)
        outs.append(u32[:, sl] * (s_blk + bias[:, sl]) * sgate[:, sl])
    return outs


def _sgu_fwd(proj, ln_w, ln_b, w, bias, name):
    def body(u_ref, v_ref, g_ref, lw_ref, lb_ref, w_ref, b_ref, o_ref):
        outs = _sgu_chunk(u_ref[...], v_ref[...], g_ref[...], lw_ref[...], lb_ref[...], w_ref[...], b_ref[...])
        for j, o in enumerate(outs):
            o_ref[:, j * LANES:(j + 1) * LANES] = o.astype(BF16)

    blk = lambda col: pl.BlockSpec((SGU_CHUNK, BW), lambda c: (c, col // BW))
    vec = pl.BlockSpec((1, BW), lambda c: (0, 0))
    return pl.pallas_call(
        body, name=name, grid=(SEQ // SGU_CHUNK,),
        in_specs=[blk(C_SGU_U), blk(C_SGU_V), blk(C_SGU_G), vec, vec,
                  pl.BlockSpec((SGU_HEADS, SGU_CHUNK, SGU_CHUNK), lambda c: (0, 0, 0)),
                  pl.BlockSpec((SGU_CHUNK, BW), lambda c: (0, 0))],
        out_specs=pl.BlockSpec((SGU_CHUNK, BW), lambda c: (c, 0)),
        out_shape=jax.ShapeDtypeStruct((SEQ, BW), BF16),
        compiler_params=_cparams(("parallel",)),
    )(proj, proj, proj, ln_w, ln_b, w, bias)


def _sgu_bwd(proj, dout, ln_w, ln_b, w, bias, name):
    def body(u_ref, v_ref, g_ref, do_ref, lw_ref, lb_ref, w_ref, b_ref, dp_ref, dlw_ref, dlb_ref, dw_ref, db_ref):
        _, vjp = jax.vjp(_sgu_chunk, u_ref[...], v_ref[...], g_ref[...], lw_ref[...], lb_ref[...], w_ref[...], b_ref[...])
        do = do_ref[0]
        du, dv, dgate, dlw, dlb, dw, db = vjp([do[:, j * LANES:(j + 1) * LANES] for j in range(BW // LANES)])
        dp_ref[:, 0:BW] = du
        dp_ref[:, BW:2 * BW] = dv
        dp_ref[:, 2 * BW:3 * BW] = dgate

        @pl.when(pl.program_id(0) == 0)
        def _():
            dlw_ref[...] = dlw
            dlb_ref[...] = dlb
            dw_ref[...] = dw
            db_ref[...] = db

        @pl.when(pl.program_id(0) > 0)
        def _():
            dlw_ref[...] += dlw
            dlb_ref[...] += dlb
            dw_ref[...] += dw
            db_ref[...] += db

    blk = lambda col: pl.BlockSpec((SGU_CHUNK, BW), lambda c: (c, col // BW))
    vec = pl.BlockSpec((1, BW), lambda c: (0, 0))
    wsp = pl.BlockSpec((SGU_HEADS, SGU_CHUNK, SGU_CHUNK), lambda c: (0, 0, 0))
    bsp = pl.BlockSpec((SGU_CHUNK, BW), lambda c: (0, 0))
    return pl.pallas_call(
        body, name=name, grid=(SEQ // SGU_CHUNK,),
        in_specs=[blk(C_SGU_U), blk(C_SGU_V), blk(C_SGU_G), pl.BlockSpec((1, SGU_CHUNK, BW), lambda c: (1, c, 0)),
                  vec, vec, wsp, bsp],
        out_specs=[pl.BlockSpec((SGU_CHUNK, 3 * BW), lambda c: (c, 0)), vec, vec, wsp, bsp],
        out_shape=[jax.ShapeDtypeStruct((SEQ, 3 * BW), F32), jax.ShapeDtypeStruct((1, BW), F32),
                   jax.ShapeDtypeStruct((1, BW), F32), jax.ShapeDtypeStruct((SGU_HEADS, SGU_CHUNK, SGU_CHUNK), F32),
                   jax.ShapeDtypeStruct((SGU_CHUNK, BW), F32)],
        compiler_params=_cparams(("arbitrary",)),
    )(proj, proj, proj, dout, ln_w, ln_b, w, bias)


CONV_BLK = 256


def _m2_conv_fwd(proj, w, b, name):
    def body(x_ref, w_ref, b_ref, o_ref):
        x = x_ref[...]
        acc = jnp.zeros_like(x) + b_ref[...]
        for k in range(M2_CONV):
            acc = acc + w_ref[k:k + 1, :] * _shift_down(x, M2_CONV - 1 - k)
        o_ref[...] = _silu(acc)

    return pl.pallas_call(
        body, name=name, grid=(M2_CONV_CH // CONV_BLK,),
        in_specs=[pl.BlockSpec((SEQ, CONV_BLK), lambda j: (0, C_M2X // CONV_BLK + j)),
                  pl.BlockSpec((M2_CONV, CONV_BLK), lambda j: (0, j)), pl.BlockSpec((1, CONV_BLK), lambda j: (0, j))],
        out_specs=pl.BlockSpec((SEQ, CONV_BLK), lambda j: (0, j)),
        out_shape=jax.ShapeDtypeStruct((SEQ, M2_CONV_CH), F32),
        compiler_params=_cparams(("parallel",)),
    )(proj, w, b)


def _m2_conv_bwd(proj, dxa, w, b, name):
    def body(x_ref, d_ref, w_ref, b_ref, dx_ref, dw_ref, db_ref):
        x = x_ref[...]
        xs = [_shift_down(x, M2_CONV - 1 - k) for k in range(M2_CONV)]
        acc = jnp.zeros_like(x) + b_ref[...]
        for k in range(M2_CONV):
            acc = acc + w_ref[k:k + 1, :] * xs[k]
        sg = jax.nn.sigmoid(acc)
        dacc = d_ref[...] * (sg * (1.0 + acc * (1.0 - sg)))
        dx = jnp.zeros_like(x)
        for k in range(M2_CONV):
            dx = dx + w_ref[k:k + 1, :] * _shift_up(dacc, M2_CONV - 1 - k)
            dw_ref[k:k + 1, :] = jnp.sum(dacc * xs[k], axis=0, keepdims=True)
        dx_ref[...] = dx
        db_ref[...] = jnp.sum(dacc, axis=0, keepdims=True)

    return pl.pallas_call(
        body, name=name, grid=(M2_CONV_CH // CONV_BLK,),
        in_specs=[pl.BlockSpec((SEQ, CONV_BLK), lambda j: (0, C_M2X // CONV_BLK + j)),
                  pl.BlockSpec((SEQ, CONV_BLK), lambda j: (0, j)),
                  pl.BlockSpec((M2_CONV, CONV_BLK), lambda j: (0, j)), pl.BlockSpec((1, CONV_BLK), lambda j: (0, j))],
        out_specs=[pl.BlockSpec((SEQ, CONV_BLK), lambda j: (0, j)), pl.BlockSpec((M2_CONV, CONV_BLK), lambda j: (0, j)),
                   pl.BlockSpec((1, CONV_BLK), lambda j: (0, j))],
        out_shape=[jax.ShapeDtypeStruct((SEQ, M2_CONV_CH), F32), jax.ShapeDtypeStruct((M2_CONV, M2_CONV_CH), F32),
                   jax.ShapeDtypeStruct((1, M2_CONV_CH), F32)],
        compiler_params=_cparams(("parallel",)),
    )(proj, dxa, w, b)


N_PAIR = M2_HEADS // 2
HI = lax.Precision.HIGHEST


def _col(a, h):
    lane = lax.broadcasted_iota(jnp.int32, a.shape, 1)
    return jnp.sum(jnp.where(lane == h, a, 0.0), axis=1, keepdims=True)


def _row(a, h):
    sub = lax.broadcasted_iota(jnp.int32, a.shape, 0)
    return jnp.sum(jnp.where(sub == h, a, 0.0), axis=0, keepdims=True)


def _ssd_chunk(xs, bms, cms, dtr, zs, states, dt_bias, a_log, dfs, nws):
    q = M2_CHUNK
    dt = _softplus(dtr + dt_bias)
    da = dt * (-jnp.exp(a_log))
    l_i = lax.broadcasted_iota(jnp.int32, (q, q), 0)
    s_i = lax.broadcasted_iota(jnp.int32, (q, q), 1)
    causal = l_i >= s_i
    tril = jnp.where(causal, 1.0, 0.0)
    a_cs = _dg(tril, da, 1, 0, HI)
    a_cs_t = _dg(da, tril, 0, 1, HI)
    a_end = _row(a_cs, q - 1)
    left = _left_lanes((q, LANES))
    left1 = _left_lanes((1, LANES))
    ys, nexts = [], []
    for j in range(N_PAIR):
        grp = j // 2
        bm, cm = bms[grp], cms[grp]
        h0, h1 = 2 * j, 2 * j + 1
        cb = _bdot(cm, bm, 1, 1)
        xdt = xs[j] * jnp.where(left, _col(dt, h0), _col(dt, h1))
        acs0, acs1 = _col(a_cs, h0), _col(a_cs, h1)
        y = _bdot(cm, states[j], 1, 0) * jnp.where(left, jnp.exp(acs0), jnp.exp(acs1))
        s_new = states[j] * jnp.where(left1, jnp.exp(_col(a_end, h0)), jnp.exp(_col(a_end, h1)))
        for h, acs, xh in ((h0, acs0, jnp.where(left, xdt, 0.0)), (h1, acs1, jnp.where(left, 0.0, xdt))):
            decay = jnp.exp(jnp.where(causal, acs - _row(a_cs_t, h), -jnp.inf))
            y = y + _bdot(cb * decay, xh, 1, 0)
            s_new = s_new + _bdot(bm * jnp.exp(_col(a_end, h) - acs), xh, 0, 0)
        ys.append((y + dfs[j] * xs[j]) * _silu(zs[j]))
        nexts.append(s_new)
    ssq = sum(jnp.sum(y * y, axis=-1, keepdims=True) for y in ys)
    scale = lax.rsqrt(ssq / BW + EPS)
    return [y * scale * nw for y, nw in zip(ys, nws)], nexts


def _blocks(ref, n, width=LANES):
    return [ref[:, j * width:(j + 1) * width] for j in range(n)]


def _ssd_fwd(proj, xa, dtr, dt_bias, a_log, dfull, nw, name):
    nc = SEQ // M2_CHUNK

    def body(x_ref, b_ref, c_ref, dt_ref, z_ref, dtb_ref, al_ref, df_ref, nw_ref, o_ref, sin_ref, st):
        @pl.when(pl.program_id(0) == 0)
        def _():
            st[...] = jnp.zeros_like(st)

        states = [st[j] for j in range(N_PAIR)]
        for j in range(N_PAIR):
            sin_ref[0, j] = states[j]
        ys, nexts = _ssd_chunk(_blocks(x_ref, 4), _blocks(b_ref, 2), _blocks(c_ref, 2), dt_ref[...], _blocks(z_ref, 4),
                               states, dtb_ref[...], al_ref[...], _blocks(df_ref, 4), _blocks(nw_ref, 4))
        for j in range(N_PAIR):
            o_ref[:, j * LANES:(j + 1) * LANES] = ys[j].astype(BF16)
            st[j] = nexts[j]

    vec8 = pl.BlockSpec((1, M2_HEADS), lambda c: (0, 0))
    vec = pl.BlockSpec((1, BW), lambda c: (0, 0))
    return pl.pallas_call(
        body, name=name, grid=(nc,),
        in_specs=[pl.BlockSpec((M2_CHUNK, BW), lambda c: (c, 0)), pl.BlockSpec((M2_CHUNK, 256), lambda c: (c, 2)),
                  pl.BlockSpec((M2_CHUNK, 256), lambda c: (c, 3)), pl.BlockSpec((M2_CHUNK, M2_HEADS), lambda c: (c, 0)),
                  pl.BlockSpec((M2_CHUNK, BW), lambda c: (c, C_M2Z // BW)), vec8, vec8, vec, vec],
        out_specs=[pl.BlockSpec((M2_CHUNK, BW), lambda c: (c, 0)),
                   pl.BlockSpec((1, N_PAIR, M2_STATE, LANES), lambda c: (c, 0, 0, 0))],
        out_shape=[jax.ShapeDtypeStruct((SEQ, BW), BF16), jax.ShapeDtypeStruct((nc, N_PAIR, M2_STATE, LANES), F32)],
        scratch_shapes=[pltpu.VMEM((N_PAIR, M2_STATE, LANES), F32)],
        compiler_params=_cparams(("arbitrary",)),
    )(xa, xa, xa, dtr, proj, dt_bias, a_log, dfull, nw)


def _ssd_bwd(proj, xa, dtr, dout, s_in, dt_bias, a_log, dfull, nw, name):
    nc = SEQ // M2_CHUNK

    def body(x_ref, b_ref, c_ref, dt_ref, z_ref, do_ref, sin_ref, dtb_ref, al_ref, df_ref, nw_ref,
             dxa_ref, ddt_ref, dz_ref, ddtb_ref, dal_ref, ddf_ref, dnw_ref, dst):
        @pl.when(pl.program_id(0) == 0)
        def _():
            dst[...] = jnp.zeros_like(dst)
            for r in (ddtb_ref, dal_ref, ddf_ref, dnw_ref):
                r[...] = jnp.zeros_like(r)

        states = [sin_ref[0, j] for j in range(N_PAIR)]
        _, vjp = jax.vjp(_ssd_chunk, _blocks(x_ref, 4), _blocks(b_ref, 2), _blocks(c_ref, 2), dt_ref[...],
                         _blocks(z_ref, 4), states, dtb_ref[...], al_ref[...], _blocks(df_ref, 4), _blocks(nw_ref, 4))
        dxs, dbs, dcs, ddt, dzs, dstates, ddtb, dal, ddfs, dnws = vjp(
            ([do_ref[0, :, j * LANES:(j + 1) * LANES] for j in range(N_PAIR)], [dst[j] for j in range(N_PAIR)]))
        for j in range(N_PAIR):
            sl = slice(j * LANES, (j + 1) * LANES)
            dxa_ref[:, sl] = dxs[j]
            dz_ref[:, sl] = dzs[j]
            dst[j] = dstates[j]
            ddf_ref[:, sl] += ddfs[j]
            dnw_ref[:, sl] += dnws[j]
        for g in range(2):
            dxa_ref[:, BW + g * LANES:BW + (g + 1) * LANES] = dbs[g]
            dxa_ref[:, BW + 256 + g * LANES:BW + 256 + (g + 1) * LANES] = dcs[g]
        ddt_ref[...] = ddt
        ddtb_ref[...] += ddtb
        dal_ref[...] += dal

    rev = lambda w, col=0: pl.BlockSpec((M2_CHUNK, w), lambda i: (nc - 1 - i, col))
    vec8 = pl.BlockSpec((1, M2_HEADS), lambda i: (0, 0))
    vec = pl.BlockSpec((1, BW), lambda i: (0, 0))
    return pl.pallas_call(
        body, name=name, grid=(nc,),
        in_specs=[rev(BW), rev(256, 2), rev(256, 3), rev(M2_HEADS), rev(BW, C_M2Z // BW),
                  pl.BlockSpec((1, M2_CHUNK, BW), lambda i: (2, nc - 1 - i, 0)),
                  pl.BlockSpec((1, N_PAIR, M2_STATE, LANES), lambda i: (nc - 1 - i, 0, 0, 0)), vec8, vec8, vec, vec],
        out_specs=[rev(M2_CONV_CH), rev(M2_HEADS), rev(BW), vec8, vec8, vec, vec],
        out_shape=[jax.ShapeDtypeStruct((SEQ, M2_CONV_CH), F32), jax.ShapeDtypeStruct((SEQ, M2_HEADS), F32),
                   jax.ShapeDtypeStruct((SEQ, BW), F32), jax.ShapeDtypeStruct((1, M2_HEADS), F32),
                   jax.ShapeDtypeStruct((1, M2_HEADS), F32), jax.ShapeDtypeStruct((1, BW), F32),
                   jax.ShapeDtypeStruct((1, BW), F32)],
        scratch_shapes=[pltpu.VMEM((N_PAIR, M2_STATE, LANES), F32)],
        compiler_params=_cparams(("arbitrary",)),
    )(xa, xa, xa, dtr, proj, dout, s_in, dt_bias, a_log, dfull, nw)


def _sc_specs():
    col = lambda c0: pl.BlockSpec((SEQ, LANES), lambda j: (0, c0 // LANES + j))
    return [col(C_SCB), col(C_SCC), col(C_SCH), col(C_SCG)]


def _sc_fwd(proj, w, name):
    def body(b_ref, c_ref, h_ref, g_ref, w_ref, o_ref):
        ch = c_ref[...] * h_ref[...]
        acc = jnp.zeros_like(ch)
        for k in range(SC_CONV):
            acc = acc + w_ref[k:k + 1, :] * _shift_down(ch, SC_CONV - 1 - k)
        o_ref[...] = (b_ref[...] * acc * _silu(g_ref[...])).astype(BF16)

    return pl.pallas_call(
        body, name=name, grid=(BW // LANES,),
        in_specs=_sc_specs() + [pl.BlockSpec((SC_CONV, LANES), lambda j: (0, j))],
        out_specs=pl.BlockSpec((SEQ, LANES), lambda j: (0, j)),
        out_shape=jax.ShapeDtypeStruct((SEQ, BW), BF16),
        compiler_params=_cparams(("parallel",)),
    )(proj, proj, proj, proj, w)


def _sc_bwd(proj, dout, w, name):
    def body(b_ref, c_ref, h_ref, g_ref, do_ref, w_ref, db_ref, dc_ref, dh_ref, dg_ref, dw_ref):
        cv, hv, gv = c_ref[...], h_ref[...], g_ref[...]
        ch = cv * hv
        chs = [_shift_down(ch, SC_CONV - 1 - k) for k in range(SC_CONV)]
        acc = jnp.zeros_like(ch)
        for k in range(SC_CONV):
            acc = acc + w_ref[k:k + 1, :] * chs[k]
        sg = jax.nn.sigmoid(gv)
        do = do_ref[0]
        bv = b_ref[...]
        db_ref[...] = do * acc * (gv * sg)
        dg_ref[...] = do * bv * acc * (sg * (1.0 + gv * (1.0 - sg)))
        dacc = do * bv * (gv * sg)
        dch = jnp.zeros_like(ch)
        for k in range(SC_CONV):
            dch = dch + w_ref[k:k + 1, :] * _shift_up(dacc, SC_CONV - 1 - k)
            dw_ref[k:k + 1, :] = jnp.sum(dacc * chs[k], axis=0, keepdims=True)
        dc_ref[...] = dch * hv
        dh_ref[...] = dch * cv

    col = pl.BlockSpec((SEQ, LANES), lambda j: (0, j))
    wsp = pl.BlockSpec((SC_CONV, LANES), lambda j: (0, j))
    act = jax.ShapeDtypeStruct((SEQ, BW), F32)
    return pl.pallas_call(
        body, name=name, grid=(BW // LANES,),
        in_specs=_sc_specs() + [pl.BlockSpec((1, SEQ, LANES), lambda j: (3, 0, j)), wsp], out_specs=[col, col, col, col, wsp],
        out_shape=[act, act, act, act, jax.ShapeDtypeStruct((SC_CONV, BW), F32)],
        compiler_params=_cparams(("parallel",)),
    )(proj, proj, proj, proj, dout, w)


MERGE_T = 256
MERGE_BWD_T = 512


def _merge_fwd(proj, ys, merge_b, w_branch, name):
    def body(y_ref, lg_ref, b_ref, w_ref, o_ref):
        acc = jnp.zeros((MERGE_T, D_MODEL), F32)
        for k in range(N_BRANCH):
            gate = jax.nn.sigmoid(lg_ref[:, k * D_MODEL:(k + 1) * D_MODEL] + b_ref[k])
            acc = acc + gate * _dg(y_ref[k], w_ref[k], 1, 0)
        o_ref[...] = acc.astype(BF16)

    return pl.pallas_call(
        body, name=name, grid=(SEQ // MERGE_T,),
        in_specs=[pl.BlockSpec((N_BRANCH, MERGE_T, BW), lambda i: (0, i, 0)),
                  pl.BlockSpec((MERGE_T, N_BRANCH * D_MODEL), lambda i: (i, C_MERGE // (N_BRANCH * D_MODEL))),
                  pl.BlockSpec((N_BRANCH, 1, D_MODEL), lambda i: (0, 0, 0)),
                  pl.BlockSpec((N_BRANCH, BW, D_MODEL), lambda i: (0, 0, 0))],
        out_specs=pl.BlockSpec((MERGE_T, D_MODEL), lambda i: (i, 0)),
        out_shape=jax.ShapeDtypeStruct((SEQ, D_MODEL), BF16),
        compiler_params=_cparams(("parallel",)),
    )(ys, proj, merge_b, w_branch)


def _merge_bwd(proj, ys, dm, merge_b, w_branch, name):
    nt = SEQ // MERGE_BWD_T

    def body(y_ref, lg_ref, dm_ref, b_ref, w_ref, dy_ref, dlg_ref, dw_ref, db_ref, dw_acc):
        i = pl.program_id(1)
        gate = jax.nn.sigmoid(lg_ref[...] + b_ref[0])
        y = y_ref[0]
        dmv = dm_ref[...]
        dbo = (gate * dmv).astype(BF16)
        dlg = _dg(y, w_ref[0], 1, 0) * dmv * gate * (1.0 - gate)
        dlg_ref[...] = dlg
        dy_ref[0] = _dg(dbo, w_ref[0], 1, 1)
        dwp = _dg(y, dbo, 0, 0)
        dbp = jnp.sum(dlg, axis=0, keepdims=True)

        @pl.when(i == 0)
        def _():
            dw_acc[...] = dwp
            db_ref[0] = dbp

        @pl.when(i > 0)
        def _():
            dw_acc[...] += dwp
            db_ref[0] += dbp

        @pl.when(i == nt - 1)
        def _():
            dw_ref[0] = dw_acc[...].astype(BF16)

    return pl.pallas_call(
        body, name=name, grid=(N_BRANCH, nt),
        in_specs=[pl.BlockSpec((1, MERGE_BWD_T, BW), lambda k, i: (k, i, 0)),
                  pl.BlockSpec((MERGE_BWD_T, D_MODEL), lambda k, i: (i, C_MERGE // D_MODEL + k)),
                  pl.BlockSpec((MERGE_BWD_T, D_MODEL), lambda k, i: (i, 0)),
                  pl.BlockSpec((1, 1, D_MODEL), lambda k, i: (k, 0, 0)),
                  pl.BlockSpec((1, BW, D_MODEL), lambda k, i: (k, 0, 0))],
        out_specs=[pl.BlockSpec((1, MERGE_BWD_T, BW), lambda k, i: (k, i, 0)),
                   pl.BlockSpec((MERGE_BWD_T, D_MODEL), lambda k, i: (i, k)),
                   pl.BlockSpec((1, BW, D_MODEL), lambda k, i: (k, 0, 0)),
                   pl.BlockSpec((1, 1, D_MODEL), lambda k, i: (k, 0, 0))],
        out_shape=[jax.ShapeDtypeStruct((N_BRANCH, SEQ, BW), F32), jax.ShapeDtypeStruct((SEQ, N_BRANCH * D_MODEL), F32),
                   jax.ShapeDtypeStruct((N_BRANCH, BW, D_MODEL), BF16), jax.ShapeDtypeStruct((N_BRANCH, 1, D_MODEL), F32)],
        scratch_shapes=[pltpu.VMEM((BW, D_MODEL), F32)],
        compiler_params=_cparams(("parallel", "arbitrary")),
    )(ys, proj, dm, merge_b, w_branch)


def _adamw(gslots, w, m, v, rows, name):
    n, r, c = gslots.shape
    assert w.shape == (r, c) and r % rows == 0

    def body(g_ref, w_ref, m_ref, v_ref, go_ref, d_ref, mo_ref, vo_ref):
        g = g_ref[0].astype(F32)
        for s in range(1, n):
            g = g + g_ref[s].astype(F32)
        mn = ADAM_B1 * m_ref[...] + (1.0 - ADAM_B1) * g
        vn = ADAM_B2 * v_ref[...] + (1.0 - ADAM_B2) * jnp.square(g)
        m_hat = mn / (1.0 - ADAM_B1 ** ADAM_STEP)
        v_hat = vn / (1.0 - ADAM_B2 ** ADAM_STEP)
        go_ref[...] = g
        d_ref[...] = -ADAM_LR * (m_hat / (jnp.sqrt(v_hat) + ADAM_EPS) + ADAM_WD * w_ref[...])
        mo_ref[...] = mn
        vo_ref[...] = vn

    blk = pl.BlockSpec((rows, c), lambda i: (i, 0))
    out = jax.ShapeDtypeStruct((r, c), F32)
    return pl.pallas_call(
        body, name=name, grid=(r // rows,),
        in_specs=[pl.BlockSpec((n, rows, c), lambda i: (0, i, 0)), blk, blk, blk],
        out_specs=[blk, blk, blk, blk], out_shape=[out, out, out, out],
        compiler_params=_cparams(("parallel",)),
    )(gslots, w, m, v)


def _slot_sum(gslots, name):
    n, r, c = gslots.shape

    def body(g_ref, o_ref):
        g = g_ref[0]
        for s in range(1, n):
            g = g + g_ref[s]
        o_ref[...] = g

    return pl.pallas_call(
        body, name=name, in_specs=[pl.BlockSpec((n, r, c), lambda: (0, 0, 0))],
        out_specs=pl.BlockSpec((r, c), lambda: (0, 0)), out_shape=jax.ShapeDtypeStruct((r, c), F32),
        compiler_params=_cparams(None),
    )(gslots)


def _me_and_peers():
    x, y, c = lax.axis_index("x"), lax.axis_index("y"), lax.axis_index("c")
    me = 4 * x + 2 * y + c
    peers = []
    for k in range(1, N_DEV):
        px = 1 - x if (k >> 2) & 1 else x
        py = 1 - y if (k >> 1) & 1 else y
        pc = 1 - c if k & 1 else c
        peers.append((4 * px + 2 * py + pc, (px, py, pc)))
    return me, peers


def _exchange(tensors, gather, name):
    n = len(tensors)

    def body(*refs):
        ins, outs = refs[:n], refs[n:2 * n]
        send_sems, recv_sems, local_sems = refs[2 * n:]
        me, peers = _me_and_peers()
        started = []
        for t in range(n):
            own = pltpu.make_async_copy(ins[t] if gather else ins[t].at[me], outs[t].at[me], local_sems.at[t])
            own.start()
            started.append(own)
            for k, (pidx, pos) in enumerate(peers):
                cp = pltpu.make_async_remote_copy(
                    src_ref=ins[t] if gather else ins[t].at[pidx], dst_ref=outs[t].at[me],
                    send_sem=send_sems.at[t, k], recv_sem=recv_sems.at[t, k], device_id=pos, device_id_type=MESH)
                cp.start()
                started.append(cp)
        for cp in started:
            cp.wait()

    any_spec = pl.BlockSpec(memory_space=pl.ANY)
    outs = pl.pallas_call(
        body, name=name, in_specs=[any_spec] * n, out_specs=[any_spec] * n,
        out_shape=[jax.ShapeDtypeStruct(((N_DEV,) + t.shape) if gather else t.shape, t.dtype) for t in tensors],
        scratch_shapes=[pltpu.SemaphoreType.DMA((n, N_DEV - 1)), pltpu.SemaphoreType.DMA((n, N_DEV - 1)),
                        pltpu.SemaphoreType.DMA((n,))],
        compiler_params=pltpu.CompilerParams(has_side_effects=True),
    )(*tensors)
    return list(outs)


WEIGHTS = ['norm_w', 'w_in', 's5_lambda_re', 's5_lambda_im', 's5_b_re', 's5_b_im', 's5_c_re', 's5_c_im', 's5_d',
           's5_log_step', 's5_w_glu', 'sgu_ln_w', 'sgu_ln_b', 'sgu_w', 'sgu_b', 'm2_conv_w', 'm2_conv_b', 'm2_dt_bias',
           'm2_a_log', 'm2_d', 'm2_norm_w', 'sc_conv_w', 'merge_b', 'w_branch', 'w_out', 'final_norm_w']
BIG_SHARDED = ['w_in', 'w_branch', 'w_out', 's5_w_glu']
SMALL_SHARDED = ['m2_conv_w', 'sc_conv_w', 'merge_b']
REPLICATED = [n for n in WEIGHTS if n not in BIG_SHARDED + SMALL_SHARDED]
S5_NAMES = ['s5_lambda_re', 's5_lambda_im', 's5_b_re', 's5_b_im', 's5_c_re', 's5_c_im', 's5_d', 's5_log_step']


def _pad_in(w):
    z = jnp.zeros(w.shape[:-1] + (DT_PAD,), w.dtype)
    return jnp.concatenate([w[..., MERGE_COL:], w[..., :DT_COL + 8], z, w[..., DT_COL + 8:MERGE_COL]], axis=-1)


def _unpad_in(g):
    nm = N_BRANCH * D_MODEL
    return jnp.concatenate([g[..., nm:nm + DT_COL + 8], g[..., nm + DT_COL + 8 + DT_PAD:], g[..., :nm]], axis=-1)


def _rows128(flat, row_mult=8):
    n = flat.shape[0]
    per = LANES * row_mult
    total = -(-n // per) * per
    return jnp.pad(flat, (0, total - n)).reshape(total // LANES, LANES)


def _layer_fwd(x, i, p, full):
    nw = p['norm_w'][i].reshape(1, D_MODEL)
    h = _rmsnorm_fwd(x, nw, f"rms_fwd{i}")
    proj = _matmul(h, full['w_in'][i], 1, 0, F32, 1024, 1152, 1024, f"proj{i}")
    disc, disc_vjp = jax.vjp(_s5_disc, *[p[n][i] for n in S5_NAMES])
    s5w = [t.astype(BF16) for t in disc[:4]] + [disc[4], disc[5], full['s5_w_glu'][i]]
    ya, sre, sim = _s5_fwd(proj, *s5w, f"s5_fwd{i}")
    sgw = [p['sgu_ln_w'][i].reshape(1, BW), p['sgu_ln_b'][i].reshape(1, BW), p['sgu_w'][i],
           jnp.repeat(p['sgu_b'][i].T, BW // SGU_HEADS, axis=1)]
    yb = _sgu_fwd(proj, *sgw, f"sgu_fwd{i}")
    cw, cb = full['m2_conv_w'][i], p['m2_conv_b'][i].reshape(1, M2_CONV_CH)
    xa = _m2_conv_fwd(proj, cw, cb, f"m2conv_fwd{i}")
    dtr = proj[:, C_DT:C_DT + M2_HEADS]
    m2w = [p['m2_dt_bias'][i].reshape(1, M2_HEADS), p['m2_a_log'][i].reshape(1, M2_HEADS),
           jnp.repeat(p['m2_d'][i], M2_HEAD_DIM).reshape(1, BW), p['m2_norm_w'][i].reshape(1, BW)]
    yc, s_in = _ssd_fwd(proj, xa, dtr, *m2w, f"ssd_fwd{i}")
    scw = full['sc_conv_w'][i]
    yd = _sc_fwd(proj, scw, f"sc_fwd{i}")
    ys = jnp.stack([ya, yb, yc, yd])
    mb = full['merge_b'][i].reshape(N_BRANCH, 1, D_MODEL)
    merged = _merge_fwd(proj, ys, mb, full['w_branch'][i], f"merge_fwd{i}")
    x_new = _matmul(merged, full['w_out'][i], 1, 0, F32, 1024, 1024, 1024, f"out{i}", residual=x)
    saved = dict(x=x, nw=nw, h=h, proj=proj, disc_vjp=disc_vjp, s5w=s5w, sre=sre, sim=sim, sgw=sgw, cw=cw, cb=cb, xa=xa,
                 dtr=dtr, m2w=m2w, s_in=s_in, scw=scw, ys=ys, mb=mb, merged=merged)
    return x_new, saved


def _layer_bwd(dx_out, i, sv, full):
    g = {}
    proj = sv['proj']
    dm = _matmul(dx_out, full['w_out'][i], 1, 1, F32, 1024, 1024, 1024, f"dmerged{i}")
    g['w_out'] = _matmul(sv['merged'], dx_out, 0, 0, BF16, 1024, 1024, 1024, f"gw_out{i}")
    dys, dlogits, g['w_branch'], dmb = _merge_bwd(proj, sv['ys'], dm, sv['mb'], full['w_branch'][i], f"merge_bwd{i}")
    g['merge_b'] = dmb.reshape(N_BRANCH, D_MODEL)
    dp_s5, dbbre, dbbim, dcre, dcim, da, dd, dwg = _s5_bwd(proj, dys, sv['sre'], sv['sim'], *sv['s5w'], f"s5_bwd{i}")
    for n, t in zip(S5_NAMES, sv['disc_vjp']((dbbre, dbbim, dcre, dcim, da, dd))):
        g[n] = t
    g['s5_w_glu'] = dwg.astype(BF16)
    dp_sgu, dlw, dlb, g['sgu_w'], dbias = _sgu_bwd(proj, dys, *sv['sgw'], f"sgu_bwd{i}")
    g['sgu_ln_w'], g['sgu_ln_b'] = dlw[0], dlb[0]
    g['sgu_b'] = dbias.reshape(SGU_CHUNK, SGU_HEADS, BW // SGU_HEADS).sum(-1).T
    dxa, ddt, dz, ddtb, dal, ddf, dnw = _ssd_bwd(proj, sv['xa'], sv['dtr'], dys, sv['s_in'], *sv['m2w'], f"ssd_bwd{i}")
    dxbc, g['m2_conv_w'], dcb = _m2_conv_bwd(proj, dxa, sv['cw'], sv['cb'], f"m2conv_bwd{i}")
    g['m2_conv_b'], g['m2_dt_bias'], g['m2_a_log'], g['m2_norm_w'] = dcb[0], ddtb[0], dal[0], dnw[0]
    g['m2_d'] = ddf.reshape(M2_HEADS, M2_HEAD_DIM).sum(-1)
    dsb, dsc, dsh, dsg, g['sc_conv_w'] = _sc_bwd(proj, dys, sv['scw'], f"sc_bwd{i}")
    dproj = jnp.concatenate([dlogits, dp_s5, dp_sgu, dz, dxbc, ddt, jnp.zeros((SEQ, DT_PAD), F32), dsb, dsc, dsh, dsg], axis=1)
    dh = _matmul(dproj, full['w_in'][i], 1, 1, F32, 1024, 1024, 1152, f"dh{i}")
    g['w_in'] = _matmul(sv['h'], dproj, 0, 0, BF16, 1024, 1152, 1024, f"gw_in{i}")
    dx_in, dnw_l = _rmsnorm_bwd(sv['x'], sv['nw'], dh, dx_out, f"rms_bwd{i}")
    g['norm_w'] = dnw_l[0]
    return dx_in, g


def _split8(t, axis):
    shp = t.shape
    t = t.reshape(shp[:axis] + (N_DEV, shp[axis] // N_DEV) + shp[axis + 1:])
    return jnp.moveaxis(t, axis, 0)


def _join8(t, axis):
    t = jnp.moveaxis(t, 0, axis)
    shp = t.shape
    return t.reshape(shp[:axis] + (shp[axis] * shp[axis + 1],) + shp[axis + 2:])


def _local_step(xs, target, p, full):
    saved = []
    for i in range(DEPTH):
        xs, sv = _layer_fwd(xs, i, p, full)
        saved.append(sv)
    loss_row, dx, dfw = _loss_head(xs, p['final_norm_w'].reshape(1, D_MODEL), target)
    layer_g = [None] * DEPTH
    for i in reversed(range(DEPTH)):
        dx, layer_g[i] = _layer_bwd(dx, i, saved[i], full)
    grads = {n: jnp.stack([layer_g[i][n] for i in range(DEPTH)]) for n in WEIGHTS if n != 'final_norm_w'}
    grads['final_norm_w'] = dfw[0]
    grads['w_in'] = _unpad_in(grads['w_in'])
    return loss_row, dx, grads


SHARD_AXIS = {'w_in': 2, 'w_branch': 3, 'w_out': 1, 's5_w_glu': 1, 'm2_conv_w': 2, 'sc_conv_w': 2, 'merge_b': 2}


def kernel(x, norm_w, w_in, s5_lambda_re, s5_lambda_im, s5_b_re, s5_b_im, s5_c_re, s5_c_im, s5_d, s5_log_step, s5_w_glu, sgu_ln_w, sgu_ln_b, sgu_w, sgu_b, m2_conv_w, m2_conv_b, m2_dt_bias, m2_a_log, m2_d, m2_norm_w, sc_conv_w, merge_b, w_branch, w_out, final_norm_w, loss_target, m_norm_w, m_w_in, m_s5_lambda_re, m_s5_lambda_im, m_s5_b_re, m_s5_b_im, m_s5_c_re, m_s5_c_im, m_s5_d, m_s5_log_step, m_s5_w_glu, m_sgu_ln_w, m_sgu_ln_b, m_sgu_w, m_sgu_b, m_m2_conv_w, m_m2_conv_b, m_m2_dt_bias, m_m2_a_log, m_m2_d, m_m2_norm_w, m_sc_conv_w, m_merge_b, m_w_branch, m_w_out, m_final_norm_w, v_norm_w, v_w_in, v_s5_lambda_re, v_s5_lambda_im, v_s5_b_re, v_s5_b_im, v_s5_c_re, v_s5_c_im, v_s5_d, v_s5_log_step, v_s5_w_glu, v_sgu_ln_w, v_sgu_ln_b, v_sgu_w, v_sgu_b, v_m2_conv_w, v_m2_conv_b, v_m2_dt_bias, v_m2_a_log, v_m2_d, v_m2_norm_w, v_sc_conv_w, v_merge_b, v_w_branch, v_w_out, v_final_norm_w):
    loc = locals()
    p = {n: loc[n] for n in WEIGHTS}
    mom = {n: loc['m_' + n] for n in WEIGHTS}
    vel = {n: loc['v_' + n] for n in WEIGHTS}

    small_sizes = [p[n].size for n in SMALL_SHARDED]
    small_pack = _rows128(jnp.concatenate([p[n].reshape(-1) for n in SMALL_SHARDED]))
    gathered = _exchange([p[n].astype(BF16) for n in BIG_SHARDED] + [small_pack], True, "gather_weights")
    full = {}
    for n, t in zip(BIG_SHARDED, gathered):
        full[n] = _join8(t, SHARD_AXIS[n])
    full['w_in'] = _pad_in(full['w_in'])
    small_all = gathered[-1].reshape(N_DEV, -1)
    off = 0
    for n, sz in zip(SMALL_SHARDED, small_sizes):
        full[n] = _join8(small_all[:, off:off + sz].reshape((N_DEV,) + p[n].shape), SHARD_AXIS[n])
        off += sz

    loss_row, dx, grads = _local_step(x[0], loss_target[0], p, full)
    loss = lax.psum(loss_row[0, 0], ("x", "y", "c"))

    repl_flat = jnp.concatenate([grads[n].reshape(-1) for n in REPLICATED])
    n_repl = repl_flat.shape[0]
    repl_rows = _rows128(repl_flat, 8 * N_DEV)
    rr = repl_rows.shape[0] // N_DEV
    small_g = jnp.concatenate(
        [_split8(grads[n], SHARD_AXIS[n]).reshape(N_DEV, -1) for n in SMALL_SHARDED] + [repl_rows.reshape(N_DEV, -1)], axis=1)
    n_small = small_g.shape[1] - rr * LANES
    pad = (-small_g.shape[1]) % (8 * LANES)
    small_g = jnp.pad(small_g, ((0, 0), (0, pad))).reshape(N_DEV, -1, LANES)
    landed = _exchange([_split8(grads[n], SHARD_AXIS[n]) for n in BIG_SHARDED] + [small_g], False, "scatter_grads")

    out_g, out_d, out_m, out_v = {}, {}, {}, {}
    for n, t in zip(BIG_SHARDED, landed):
        shp = p[n].shape
        c = shp[-1]
        r = p[n].size // c
        res = _adamw(t.reshape(N_DEV, r, c), p[n].reshape(r, c), mom[n].reshape(r, c), vel[n].reshape(r, c),
                     {'w_in': 256, 'w_branch': 512, 'w_out': 128, 's5_w_glu': 128}[n], "adamw_" + n)
        out_g[n], out_d[n], out_m[n], out_v[n] = [o.reshape(shp) for o in res]
    small_sum = _slot_sum(landed[-1], "sum_small").reshape(-1)
    repl_part = small_sum[n_small:n_small + rr * LANES].reshape(rr, LANES)
    repl_all = _exchange([repl_part], True, "gather_small")[0].reshape(-1)[:n_repl]
    names = SMALL_SHARDED + REPLICATED
    g_all = _rows128(jnp.concatenate([small_sum[:n_small], repl_all]))
    res = _adamw(g_all[None], *[_rows128(jnp.concatenate([d[n].reshape(-1) for n in names])) for d in (p, mom, vel)],
                 g_all.shape[0], "adamw_small")
    flat = [o.reshape(-1) for o in res]
    off = 0
    for n in names:
        sz = p[n].size
        out_g[n], out_d[n], out_m[n], out_v[n] = [f[off:off + sz].reshape(p[n].shape) for f in flat]
        off += sz
    return (loss, dx[None], *[out_g[n] for n in WEIGHTS], *[out_d[n] for n in WEIGHTS],
            *[out_m[n] for n in WEIGHTS], *[out_v[n] for n in WEIGHTS])
```

```python
import functools

import jax
import jax.numpy as jnp
from jax import lax
from jax.experimental import pallas as pl
from jax.experimental.pallas import tpu as pltpu

F32 = jnp.float32
BF16 = jnp.bfloat16

N_DEV = 8
SEQ = 2048
D_MODEL = 1024
DEPTH = 2
BW = 512
N_BRANCH = 4
EPS = 1e-6
S5_GROUPS, S5_STATE, S5_P = 32, 64, 16
S5_CH = S5_GROUPS * S5_STATE
SGU_CHUNK, SGU_HEADS = 128, 8
M2_HEADS, M2_HEAD_DIM, M2_STATE, M2_CHUNK, M2_CONV = 8, 64, 128, 128, 4
M2_CONV_CH = 1024
SC_CONV = 3
IN_DIM = 10248
IN_PAD = 11264
C_MERGE = 0
C_S5U, C_S5G = 4096, 4608
C_M2X = 5120
C_SGU_U, C_SGU_V, C_SGU_G = 6144, 6656, 7168
C_M2Z, C_DT = 8192, 8704
C_SC = 9216
SHARD_IN = IN_DIM // N_DEV

ADAM_LR, ADAM_B1, ADAM_B2, ADAM_EPS, ADAM_WD, ADAM_STEP = 0.001, 0.9, 0.999, 1e-08, 0.01, 10

VMEM_LIMIT = 56 * 1024 * 1024
LANES = 128

MESH = pl.DeviceIdType.MESH


def _cparams(sem=None, **kw):
    return pltpu.CompilerParams(dimension_semantics=sem, vmem_limit_bytes=VMEM_LIMIT, **kw)


def _dg(a, b, ca, cb, precision=None):
    return lax.dot_general(a, b, (((ca,), (cb,)), ((), ())), precision=precision,
                           preferred_element_type=F32)


@functools.partial(jax.custom_vjp, nondiff_argnums=(2, 3))
def _bdot(a, b, ca, cb):
    return _dg(a.astype(BF16), b.astype(BF16), ca, cb)


def _bdot_fwd(a, b, ca, cb):
    return _bdot(a, b, ca, cb), (a, b)


def _bdot_bwd(ca, cb, res, g):
    a, b = res
    gb, ab, bb = g.astype(BF16), a.astype(BF16), b.astype(BF16)
    da = _dg(gb, bb, 1, 1 - cb) if ca == 1 else _dg(bb, gb, 1 - cb, 1)
    db = _dg(ab, gb, 1 - ca, 0) if cb == 0 else _dg(gb, ab, 0, 1 - ca)
    return da.astype(a.dtype), db.astype(b.dtype)


_bdot.defvjp(_bdot_fwd, _bdot_bwd)


def _rms(x, w):
    return x * lax.rsqrt(jnp.mean(x * x, axis=-1, keepdims=True) + EPS) * w


def _silu(x):
    return x * jax.nn.sigmoid(x)


def _gelu(x):
    return 0.5 * x * (1.0 + jnp.tanh(0.7978845608028654 * (x + 0.044715 * (x * x * x))))


def _softplus(x):
    return jnp.maximum(x, 0.0) + jnp.log1p(jnp.exp(-jnp.abs(x)))


def _shift_down(x, s):
    if s == 0:
        return x
    row = lax.broadcasted_iota(jnp.int32, x.shape, 0)
    return jnp.where(row >= s, pltpu.roll(x, s, 0), 0.0)


def _shift_up(x, s):
    if s == 0:
        return x
    n = x.shape[0]
    row = lax.broadcasted_iota(jnp.int32, x.shape, 0)
    return jnp.where(row < n - s, pltpu.roll(x, n - s, 0), 0.0)


def _matmul(a, b, ca, cb, out_dtype, tm, tn, tk, name, residual=None):
    m = a.shape[1 - ca]
    k = a.shape[ca]
    n = b.shape[1 - cb]
    assert b.shape[cb] == k and m % tm == 0 and n % tn == 0 and k % tk == 0
    nk = k // tk
    a_spec = pl.BlockSpec((tm, tk), lambda i, j, kk: (i, kk)) if ca == 1 else pl.BlockSpec((tk, tm), lambda i, j, kk: (kk, i))
    b_spec = pl.BlockSpec((tk, tn), lambda i, j, kk: (kk, j)) if cb == 0 else pl.BlockSpec((tn, tk), lambda i, j, kk: (j, kk))
    o_spec = pl.BlockSpec((tm, tn), lambda i, j, kk: (i, j))
    has_res = residual is not None

    def body(*refs):
        if has_res:
            a_ref, b_ref, r_ref, o_ref, acc = refs
        else:
            a_ref, b_ref, o_ref, acc = refs
        kk = pl.program_id(2)
        part = _dg(a_ref[...].astype(BF16), b_ref[...].astype(BF16), ca, cb)

        @pl.when(kk == 0)
        def _():
            acc[...] = part

        @pl.when(kk > 0)
        def _():
            acc[...] += part

        @pl.when(kk == nk - 1)
        def _():
            r = acc[...]
            if has_res:
                r = r + r_ref[...]
            o_ref[...] = r.astype(out_dtype)

    ins = [a, b] + ([residual] if has_res else [])
    specs = [a_spec, b_spec] + ([o_spec] if has_res else [])
    return pl.pallas_call(
        body, name=name, grid=(m // tm, n // tn, nk), in_specs=specs, out_specs=o_spec,
        out_shape=jax.ShapeDtypeStruct((m, n), out_dtype),
        scratch_shapes=[pltpu.VMEM((tm, tn), F32)],
        compiler_params=_cparams(("parallel", "parallel", "arbitrary")),
    )(*ins)


ROW_TILE = 512


def _rmsnorm_fwd(x, w, name):
    def body(x_ref, w_ref, o_ref):
        o_ref[...] = _rms(x_ref[...], w_ref[...]).astype(BF16)

    return pl.pallas_call(
        body, name=name, grid=(SEQ // ROW_TILE,),
        in_specs=[pl.BlockSpec((ROW_TILE, D_MODEL), lambda i: (i, 0)), pl.BlockSpec((1, D_MODEL), lambda i: (0, 0))],
        out_specs=pl.BlockSpec((ROW_TILE, D_MODEL), lambda i: (i, 0)),
        out_shape=jax.ShapeDtypeStruct((SEQ, D_MODEL), BF16),
        compiler_params=_cparams(("parallel",)),
    )(x, w)


def _rmsnorm_bwd(x, w, dh, dres, name):
    def body(x_ref, w_ref, dh_ref, dres_ref, dx_ref, dw_ref):
        _, vjp = jax.vjp(_rms, x_ref[...], w_ref[...])
        dx, dw = vjp(dh_ref[...])
        dx_ref[...] = dx + dres_ref[...]

        @pl.when(pl.program_id(0) == 0)
        def _():
            dw_ref[...] = dw

        @pl.when(pl.program_id(0) > 0)
        def _():
            dw_ref[...] += dw

    tile = pl.BlockSpec((ROW_TILE, D_MODEL), lambda i: (i, 0))
    vec = pl.BlockSpec((1, D_MODEL), lambda i: (0, 0))
    return pl.pallas_call(
        body, name=name, grid=(SEQ // ROW_TILE,),
        in_specs=[tile, vec, tile, tile], out_specs=[tile, vec],
        out_shape=[jax.ShapeDtypeStruct((SEQ, D_MODEL), F32), jax.ShapeDtypeStruct((1, D_MODEL), F32)],
        compiler_params=_cparams(("arbitrary",)),
    )(x, w, dh, dres)


def _loss_head(x, w, target):
    def body(x_ref, w_ref, t_ref, loss_ref, dx_ref, dw_ref):
        tgt = t_ref[...]

        def f(xv, wv):
            err = _rms(xv, wv) - tgt
            return 0.5 * jnp.sum(jnp.mean(err * err, axis=-1))

        loss, vjp = jax.vjp(f, x_ref[...], w_ref[...])
        dx, dw = vjp(jnp.ones((), F32))
        dx_ref[...] = dx
        lrow = jnp.full((1, LANES), loss, F32)

        @pl.when(pl.program_id(0) == 0)
        def _():
            dw_ref[...] = dw
            loss_ref[...] = lrow

        @pl.when(pl.program_id(0) > 0)
        def _():
            dw_ref[...] += dw
            loss_ref[...] += lrow

    tile = pl.BlockSpec((ROW_TILE, D_MODEL), lambda i: (i, 0))
    vec = pl.BlockSpec((1, D_MODEL), lambda i: (0, 0))
    return pl.pallas_call(
        body, name="loss_head", grid=(SEQ // ROW_TILE,),
        in_specs=[tile, vec, tile], out_specs=[pl.BlockSpec((1, LANES), lambda i: (0, 0)), tile, vec],
        out_shape=[jax.ShapeDtypeStruct((1, LANES), F32), jax.ShapeDtypeStruct((SEQ, D_MODEL), F32),
                   jax.ShapeDtypeStruct((1, D_MODEL), F32)],
        compiler_params=_cparams(("arbitrary",)),
    )(x, w, target)


S5_T = 256


def _s5_post(ypre, gate, wglu):
    y = _gelu(ypre)
    y = y * jax.nn.sigmoid(_bdot(y, wglu, 1, 0))
    return y * _silu(gate)


def _s5_fwd(proj, bbre, bbim, cre, cim, a2, dvec, wglu, name):
    def body(u_ref, g_ref, bbre_ref, bbim_ref, cre_ref, cim_ref, a_ref, d_ref, wg_ref, o_ref, sre_ref, sim_ref, st):
        @pl.when(pl.program_id(0) == 0)
        def _():
            st[...] = jnp.zeros_like(st)

        u = u_ref[...]
        ub = u.astype(BF16)
        sre_ref[...] = _dg(ub, bbre_ref[...], 1, 0)
        sim_ref[...] = _dg(ub, bbim_ref[...], 1, 0)
        ar, ai = a_ref[0:1, :], a_ref[1:2, :]

        def step(t, carry):
            sr, si = carry
            nr = ar * sr - ai * si + sre_ref[pl.ds(t, 1), :]
            ni = ar * si + ai * sr + sim_ref[pl.ds(t, 1), :]
            sre_ref[pl.ds(t, 1), :] = nr
            sim_ref[pl.ds(t, 1), :] = ni
            return nr, ni

        sr, si = lax.fori_loop(0, S5_T, step, (st[0:1, :], st[1:2, :]), unroll=8)
        st[0:1, :] = sr
        st[1:2, :] = si
        ypre = (_dg(sre_ref[...].astype(BF16), cre_ref[...], 1, 0) - _dg(sim_ref[...].astype(BF16), cim_ref[...], 1, 0)
                + d_ref[...] * u)
        o_ref[...] = _s5_post(ypre, g_ref[...], wg_ref[...]).astype(BF16)

    full = lambda shape: pl.BlockSpec(shape, lambda c: (0, 0))
    return pl.pallas_call(
        body, name=name, grid=(SEQ // S5_T,),
        in_specs=[pl.BlockSpec((S5_T, BW), lambda c: (c, C_S5U // BW)), pl.BlockSpec((S5_T, BW), lambda c: (c, C_S5G // BW)),
                  full((BW, S5_CH)), full((BW, S5_CH)), full((S5_CH, BW)), full((S5_CH, BW)),
                  full((2, S5_CH)), full((1, BW)), full((BW, BW))],
        out_specs=[pl.BlockSpec((S5_T, BW), lambda c: (c, 0)), pl.BlockSpec((S5_T, S5_CH), lambda c: (c, 0)),
                   pl.BlockSpec((S5_T, S5_CH), lambda c: (c, 0))],
        out_shape=[jax.ShapeDtypeStruct((SEQ, BW), BF16), jax.ShapeDtypeStruct((SEQ, S5_CH), F32),
                   jax.ShapeDtypeStruct((SEQ, S5_CH), F32)],
        scratch_shapes=[pltpu.VMEM((2, S5_CH), F32)],
        compiler_params=_cparams(("arbitrary",)),
    )(proj, proj, bbre, bbim, cre, cim, a2, dvec, wglu)


def _s5_bwd(proj, dproj, dout, sre, sim, bbre, bbim, cre, cim, a2, dvec, wglu, name):
    nc = SEQ // S5_T

    def body(u_ref, g_ref, do_ref, sre_ref, sim_ref, pre_ref, pim_ref, bbre_ref, bbim_ref, cre_ref, cim_ref, a_ref,
             d_ref, wg_ref, dproj_in, dp_ref, dbbre_ref, dbbim_ref, dcre_ref, dcim_ref, da_ref, dd_ref, dwg_ref,
             gre, gim, st):
        c = nc - 1 - pl.program_id(0)

        @pl.when(pl.program_id(0) == 0)
        def _():
            st[...] = jnp.zeros_like(st)
            for r in (dbbre_ref, dbbim_ref, dcre_ref, dcim_ref, da_ref, dd_ref, dwg_ref):
                r[...] = jnp.zeros_like(r)

        u = u_ref[...]
        s_re, s_im = sre_ref[...], sim_ref[...]

        def head(s_re, s_im, cre_v, cim_v, dv, uv, gv, wg):
            ypre = _bdot(s_re, cre_v, 1, 0) - _bdot(s_im, cim_v, 1, 0) + dv * uv
            return _s5_post(ypre, gv, wg)

        _, vjp = jax.vjp(head, s_re, s_im, cre_ref[...].astype(F32), cim_ref[...].astype(F32), d_ref[...], u,
                         g_ref[...], wg_ref[...].astype(F32))
        ds_re, ds_im, dcre, dcim, dd, du_d, dgate, dwg = vjp(do_ref[0])
        dcre_ref[...] += dcre
        dcim_ref[...] += dcim
        dd_ref[...] += dd
        dwg_ref[...] += dwg
        dp_ref[:, BW:] = dgate
        gre[...] = ds_re
        gim[...] = ds_im
        ar, ai = a_ref[0:1, :], a_ref[1:2, :]

        def step(i, carry):
            t = S5_T - 1 - i
            gr, gi = carry
            nr = gre[pl.ds(t, 1), :] + gr
            ni = gim[pl.ds(t, 1), :] + gi
            gre[pl.ds(t, 1), :] = nr
            gim[pl.ds(t, 1), :] = ni
            return ar * nr + ai * ni, ar * ni - ai * nr

        gr, gi = lax.fori_loop(0, S5_T, step, (st[0:1, :], st[1:2, :]), unroll=8)
        st[0:1, :] = gr
        st[1:2, :] = gi
        g_re, g_im = gre[...], gim[...]
        first = jnp.where(c > 0, 1.0, 0.0)
        row = lax.broadcasted_iota(jnp.int32, (S5_T, S5_CH), 0)
        p_re = jnp.where(row == 0, pre_ref[7:8, :] * first, pltpu.roll(s_re, 1, 0))
        p_im = jnp.where(row == 0, pim_ref[7:8, :] * first, pltpu.roll(s_im, 1, 0))
        da_ref[0:1, :] += jnp.sum(g_re * p_re + g_im * p_im, axis=0, keepdims=True)
        da_ref[1:2, :] += jnp.sum(g_im * p_re - g_re * p_im, axis=0, keepdims=True)
        ub, grb, gib = u.astype(BF16), g_re.astype(BF16), g_im.astype(BF16)
        dbbre_ref[...] += _dg(ub, grb, 0, 0)
        dbbim_ref[...] += _dg(ub, gib, 0, 0)
        dp_ref[:, :BW] = du_d + _dg(grb, bbre_ref[...], 1, 1) + _dg(gib, bbim_ref[...], 1, 1)

    full = lambda shape: pl.BlockSpec(shape, lambda i: (0, 0))
    rev = lambda w, col=0: pl.BlockSpec((S5_T, w), lambda i: (nc - 1 - i, col))
    prev = pl.BlockSpec((8, S5_CH), lambda i: (jnp.maximum((nc - 1 - i) * (S5_T // 8) - 1, 0), 0))
    return pl.pallas_call(
        body, name=name, grid=(nc,),
        in_specs=[rev(BW, C_S5U // BW), rev(BW, C_S5G // BW), pl.BlockSpec((1, S5_T, BW), lambda i: (0, nc - 1 - i, 0)),
                  rev(S5_CH), rev(S5_CH), prev, prev,
                  full((BW, S5_CH)), full((BW, S5_CH)), full((S5_CH, BW)), full((S5_CH, BW)),
                  full((2, S5_CH)), full((1, BW)), full((BW, BW)), pl.BlockSpec(memory_space=pl.ANY)],
        out_specs=[rev(2 * BW, C_S5U // (2 * BW)), full((BW, S5_CH)), full((BW, S5_CH)), full((S5_CH, BW)), full((S5_CH, BW)),
                   full((2, S5_CH)), full((1, BW)), full((BW, BW))],
        input_output_aliases={14: 0},
        out_shape=[jax.ShapeDtypeStruct((SEQ, IN_PAD), F32),
                   jax.ShapeDtypeStruct((BW, S5_CH), F32), jax.ShapeDtypeStruct((BW, S5_CH), F32),
                   jax.ShapeDtypeStruct((S5_CH, BW), F32), jax.ShapeDtypeStruct((S5_CH, BW), F32),
                   jax.ShapeDtypeStruct((2, S5_CH), F32), jax.ShapeDtypeStruct((1, BW), F32),
                   jax.ShapeDtypeStruct((BW, BW), F32)],
        scratch_shapes=[pltpu.VMEM((S5_T, S5_CH), F32), pltpu.VMEM((S5_T, S5_CH), F32), pltpu.VMEM((2, S5_CH), F32)],
        compiler_params=_cparams(("arbitrary",)),
    )(proj, proj, dout, sre, sim, sre, sim, bbre, bbim, cre, cim, a2, dvec, wglu, dproj)


def _s5_disc(lam_re, lam_im, b_re, b_im, c_re, c_im, d, log_step):
    step = jnp.exp(log_step)[:, None]
    mag = jnp.exp(lam_re * step)
    ab_re, ab_im = mag * jnp.cos(lam_im * step), mag * jnp.sin(lam_im * step)
    den = lam_re * lam_re + lam_im * lam_im
    nr = ab_re - 1.0
    coef_re = (nr * lam_re + ab_im * lam_im) / den
    coef_im = (ab_im * lam_re - nr * lam_im) / den
    bb_re = coef_re[..., None] * b_re - coef_im[..., None] * b_im
    bb_im = coef_re[..., None] * b_im + coef_im[..., None] * b_re
    eye = jnp.eye(S5_GROUPS, dtype=F32)
    bbre = jnp.einsum('gnp,gh->gphn', bb_re, eye).reshape(BW, S5_CH)
    bbim = jnp.einsum('gnp,gh->gphn', bb_im, eye).reshape(BW, S5_CH)
    cre = jnp.einsum('gpn,gh->gnhp', c_re, eye).reshape(S5_CH, BW)
    cim = jnp.einsum('gpn,gh->gnhp', c_im, eye).reshape(S5_CH, BW)
    a2 = jnp.stack([ab_re.reshape(-1), ab_im.reshape(-1)])
    return bbre, bbim, cre, cim, a2, d.reshape(1, BW)


def _left_lanes(shape):
    return lax.broadcasted_iota(jnp.int32, shape, 1) < 64


def _sgu_chunk(u, v, gate, ln_w, ln_b, w, bias):
    u32, v32 = _gelu(u), _gelu(v)
    mu = jnp.mean(v32, axis=-1, keepdims=True)
    var = jnp.mean(jnp.square(v32 - mu), axis=-1, keepdims=True)
    vn = (v32 - mu) * lax.rsqrt(var + EPS) * ln_w + ln_b
    t_i = lax.broadcasted_iota(jnp.int32, (SGU_CHUNK, SGU_CHUNK), 0)
    s_i = lax.broadcasted_iota(jnp.int32, (SGU_CHUNK, SGU_CHUNK), 1)
    causal = t_i >= s_i
    left = _left_lanes((SGU_CHUNK, LANES))
    sgate = _silu(gate)
    outs = []
    for j in range(BW // LANES):
        vb = vn[:, j * LANES:(j + 1) * LANES]
        s_blk = (_bdot(jnp.where(causal, w[2 * j], 0.0), jnp.where(left, vb, 0.0), 1, 0)
                 + _bdot(jnp.where(causal, w[2 * j + 1], 0.0), jnp.where(left, 0.0, vb), 1, 0))
        sl = slice(j * LANES, (j + 1) * LANES)
        outs.append(u32[:, sl] * (s_blk + bias[:, sl]) * sgate[:, sl])
    return outs


def _sgu_fwd(proj, ln_w, ln_b, w, bias, name):
    def body(u_ref, v_ref, g_ref, lw_ref, lb_ref, w_ref, b_ref, o_ref):
        outs = _sgu_chunk(u_ref[...], v_ref[...], g_ref[...], lw_ref[...], lb_ref[...], w_ref[...], b_ref[...])
        for j, o in enumerate(outs):
            o_ref[:, j * LANES:(j + 1) * LANES] = o.astype(BF16)

    blk = lambda col: pl.BlockSpec((SGU_CHUNK, BW), lambda c: (c, col // BW))
    vec = pl.BlockSpec((1, BW), lambda c: (0, 0))
    return pl.pallas_call(
        body, name=name, grid=(SEQ // SGU_CHUNK,),
        in_specs=[blk(C_SGU_U), blk(C_SGU_V), blk(C_SGU_G), vec, vec,
                  pl.BlockSpec((SGU_HEADS, SGU_CHUNK, SGU_CHUNK), lambda c: (0, 0, 0)),
                  pl.BlockSpec((SGU_CHUNK, BW), lambda c: (0, 0))],
        out_specs=pl.BlockSpec((SGU_CHUNK, BW), lambda c: (c, 0)),
        out_shape=jax.ShapeDtypeStruct((SEQ, BW), BF16),
        compiler_params=_cparams(("parallel",)),
    )(proj, proj, proj, ln_w, ln_b, w, bias)


def _sgu_bwd(proj, dproj, dout, ln_w, ln_b, w, bias, name):
    def body(u_ref, v_ref, g_ref, do_ref, lw_ref, lb_ref, w_ref, b_ref, dproj_in, dp_ref, dlw_ref, dlb_ref, dw_ref, db_ref):
        _, vjp = jax.vjp(_sgu_chunk, u_ref[...], v_ref[...], g_ref[...], lw_ref[...], lb_ref[...], w_ref[...], b_ref[...])
        do = do_ref[0]
        du, dv, dgate, dlw, dlb, dw, db = vjp([do[:, j * LANES:(j + 1) * LANES] for j in range(BW // LANES)])
        dp_ref[:, 0:BW] = du
        dp_ref[:, BW:2 * BW] = dv
        dp_ref[:, 2 * BW:3 * BW] = dgate
        dp_ref[:, 3 * BW:] = jnp.zeros((SGU_CHUNK, BW), F32)

        @pl.when(pl.program_id(0) == 0)
        def _():
            dlw_ref[...] = dlw
            dlb_ref[...] = dlb
            dw_ref[...] = dw
            db_ref[...] = db

        @pl.when(pl.program_id(0) > 0)
        def _():
            dlw_ref[...] += dlw
            dlb_ref[...] += dlb
            dw_ref[...] += dw
            db_ref[...] += db

    blk = lambda col: pl.BlockSpec((SGU_CHUNK, BW), lambda c: (c, col // BW))
    vec = pl.BlockSpec((1, BW), lambda c: (0, 0))
    wsp = pl.BlockSpec((SGU_HEADS, SGU_CHUNK, SGU_CHUNK), lambda c: (0, 0, 0))
    bsp = pl.BlockSpec((SGU_CHUNK, BW), lambda c: (0, 0))
    return pl.pallas_call(
        body, name=name, grid=(SEQ // SGU_CHUNK,),
        in_specs=[blk(C_SGU_U), blk(C_SGU_V), blk(C_SGU_G), pl.BlockSpec((1, SGU_CHUNK, BW), lambda c: (1, c, 0)),
                  vec, vec, wsp, bsp, pl.BlockSpec(memory_space=pl.ANY)],
        out_specs=[pl.BlockSpec((SGU_CHUNK, 4 * BW), lambda c: (c, C_SGU_U // (4 * BW))), vec, vec, wsp, bsp],
        input_output_aliases={8: 0},
        out_shape=[jax.ShapeDtypeStruct((SEQ, IN_PAD), F32), jax.ShapeDtypeStruct((1, BW), F32),
                   jax.ShapeDtypeStruct((1, BW), F32), jax.ShapeDtypeStruct((SGU_HEADS, SGU_CHUNK, SGU_CHUNK), F32),
                   jax.ShapeDtypeStruct((SGU_CHUNK, BW), F32)],
        compiler_params=_cparams(("arbitrary",)),
    )(proj, proj, proj, dout, ln_w, ln_b, w, bias, dproj)


CONV_BLK = 256


def _m2_conv_fwd(proj, w, b, name):
    def body(x_ref, w_ref, b_ref, o_ref):
        x = x_ref[...]
        acc = jnp.zeros_like(x) + b_ref[...]
        for k in range(M2_CONV):
            acc = acc + w_ref[k:k + 1, :] * _shift_down(x, M2_CONV - 1 - k)
        o_ref[...] = _silu(acc)

    return pl.pallas_call(
        body, name=name, grid=(M2_CONV_CH // CONV_BLK,),
        in_specs=[pl.BlockSpec((SEQ, CONV_BLK), lambda j: (0, C_M2X // CONV_BLK + j)),
                  pl.BlockSpec((M2_CONV, CONV_BLK), lambda j: (0, j)), pl.BlockSpec((1, CONV_BLK), lambda j: (0, j))],
        out_specs=pl.BlockSpec((SEQ, CONV_BLK), lambda j: (0, j)),
        out_shape=jax.ShapeDtypeStruct((SEQ, M2_CONV_CH), F32),
        compiler_params=_cparams(("parallel",)),
    )(proj, w, b)


def _m2_conv_bwd(proj, dproj, dxa, w, b, name):
    def body(x_ref, d_ref, w_ref, b_ref, dproj_in, dx_ref, dw_ref, db_ref):
        x = x_ref[...]
        xs = [_shift_down(x, M2_CONV - 1 - k) for k in range(M2_CONV)]
        acc = jnp.zeros_like(x) + b_ref[...]
        for k in range(M2_CONV):
            acc = acc + w_ref[k:k + 1, :] * xs[k]
        sg = jax.nn.sigmoid(acc)
        dacc = d_ref[...] * (sg * (1.0 + acc * (1.0 - sg)))
        dx = jnp.zeros_like(x)
        for k in range(M2_CONV):
            dx = dx + w_ref[k:k + 1, :] * _shift_up(dacc, M2_CONV - 1 - k)
            dw_ref[k:k + 1, :] = jnp.sum(dacc * xs[k], axis=0, keepdims=True)
        dx_ref[...] = dx
        db_ref[...] = jnp.sum(dacc, axis=0, keepdims=True)

    return pl.pallas_call(
        body, name=name, grid=(M2_CONV_CH // CONV_BLK,),
        in_specs=[pl.BlockSpec((SEQ, CONV_BLK), lambda j: (0, C_M2X // CONV_BLK + j)),
                  pl.BlockSpec((SEQ, CONV_BLK), lambda j: (0, j)),
                  pl.BlockSpec((M2_CONV, CONV_BLK), lambda j: (0, j)), pl.BlockSpec((1, CONV_BLK), lambda j: (0, j)),
                  pl.BlockSpec(memory_space=pl.ANY)],
        out_specs=[pl.BlockSpec((SEQ, CONV_BLK), lambda j: (0, C_M2X // CONV_BLK + j)),
                   pl.BlockSpec((M2_CONV, CONV_BLK), lambda j: (0, j)), pl.BlockSpec((1, CONV_BLK), lambda j: (0, j))],
        input_output_aliases={4: 0},
        out_shape=[jax.ShapeDtypeStruct((SEQ, IN_PAD), F32), jax.ShapeDtypeStruct((M2_CONV, M2_CONV_CH), F32),
                   jax.ShapeDtypeStruct((1, M2_CONV_CH), F32)],
        compiler_params=_cparams(("parallel",)),
    )(proj, dxa, w, b, dproj)


N_PAIR = M2_HEADS // 2
HI = lax.Precision.HIGHEST


def _col(a, h):
    lane = lax.broadcasted_iota(jnp.int32, a.shape, 1)
    return jnp.sum(jnp.where(lane == h, a, 0.0), axis=1, keepdims=True)


def _row(a, h):
    sub = lax.broadcasted_iota(jnp.int32, a.shape, 0)
    return jnp.sum(jnp.where(sub == h, a, 0.0), axis=0, keepdims=True)


def _ssd_chunk(xs, bms, cms, dtr, zs, states, dt_bias, a_log, dfs, nws):
    q = M2_CHUNK
    dt = _softplus(dtr + dt_bias)
    da = dt * (-jnp.exp(a_log))
    l_i = lax.broadcasted_iota(jnp.int32, (q, q), 0)
    s_i = lax.broadcasted_iota(jnp.int32, (q, q), 1)
    causal = l_i >= s_i
    tril = jnp.where(causal, 1.0, 0.0)
    a_cs = _dg(tril, da, 1, 0, HI)
    a_cs_t = _dg(da, tril, 0, 1, HI)
    a_end = _row(a_cs, q - 1)
    left = _left_lanes((q, LANES))
    left1 = _left_lanes((1, LANES))
    ys, nexts = [], []
    for j in range(N_PAIR):
        grp = j // 2
        bm, cm = bms[grp], cms[grp]
        h0, h1 = 2 * j, 2 * j + 1
        cb = _bdot(cm, bm, 1, 1)
        xdt = xs[j] * jnp.where(left, _col(dt, h0), _col(dt, h1))
        acs0, acs1 = _col(a_cs, h0), _col(a_cs, h1)
        y = _bdot(cm, states[j], 1, 0) * jnp.where(left, jnp.exp(acs0), jnp.exp(acs1))
        s_new = states[j] * jnp.where(left1, jnp.exp(_col(a_end, h0)), jnp.exp(_col(a_end, h1)))
        for h, acs, xh in ((h0, acs0, jnp.where(left, xdt, 0.0)), (h1, acs1, jnp.where(left, 0.0, xdt))):
            decay = jnp.exp(jnp.where(causal, acs - _row(a_cs_t, h), -jnp.inf))
            y = y + _bdot(cb * decay, xh, 1, 0)
            s_new = s_new + _bdot(bm * jnp.exp(_col(a_end, h) - acs), xh, 0, 0)
        ys.append((y + dfs[j] * xs[j]) * _silu(zs[j]))
        nexts.append(s_new)
    ssq = sum(jnp.sum(y * y, axis=-1, keepdims=True) for y in ys)
    scale = lax.rsqrt(ssq / BW + EPS)
    return [y * scale * nw for y, nw in zip(ys, nws)], nexts


def _blocks(ref, n, width=LANES):
    return [ref[:, j * width:(j + 1) * width] for j in range(n)]


def _ssd_fwd(proj, xa, dt_bias, a_log, dfull, nw, name):
    nc = SEQ // M2_CHUNK

    def body(x_ref, b_ref, c_ref, dt_ref, z_ref, dtb_ref, al_ref, df_ref, nw_ref, o_ref, sin_ref, st):
        @pl.when(pl.program_id(0) == 0)
        def _():
            st[...] = jnp.zeros_like(st)

        states = [st[j] for j in range(N_PAIR)]
        for j in range(N_PAIR):
            sin_ref[0, j] = states[j]
        ys, nexts = _ssd_chunk(_blocks(x_ref, 4), _blocks(b_ref, 2), _blocks(c_ref, 2), dt_ref[...], _blocks(z_ref, 4),
                               states, dtb_ref[...], al_ref[...], _blocks(df_ref, 4), _blocks(nw_ref, 4))
        for j in range(N_PAIR):
            o_ref[:, j * LANES:(j + 1) * LANES] = ys[j].astype(BF16)
            st[j] = nexts[j]

    vec8 = pl.BlockSpec((1, LANES), lambda c: (0, 0))
    vec = pl.BlockSpec((1, BW), lambda c: (0, 0))
    return pl.pallas_call(
        body, name=name, grid=(nc,),
        in_specs=[pl.BlockSpec((M2_CHUNK, BW), lambda c: (c, 0)), pl.BlockSpec((M2_CHUNK, 256), lambda c: (c, 2)),
                  pl.BlockSpec((M2_CHUNK, 256), lambda c: (c, 3)), pl.BlockSpec((M2_CHUNK, LANES), lambda c: (c, C_DT // LANES)),
                  pl.BlockSpec((M2_CHUNK, BW), lambda c: (c, C_M2Z // BW)), vec8, vec8, vec, vec],
        out_specs=[pl.BlockSpec((M2_CHUNK, BW), lambda c: (c, 0)),
                   pl.BlockSpec((1, N_PAIR, M2_STATE, LANES), lambda c: (c, 0, 0, 0))],
        out_shape=[jax.ShapeDtypeStruct((SEQ, BW), BF16), jax.ShapeDtypeStruct((nc, N_PAIR, M2_STATE, LANES), F32)],
        scratch_shapes=[pltpu.VMEM((N_PAIR, M2_STATE, LANES), F32)],
        compiler_params=_cparams(("arbitrary",)),
    )(xa, xa, xa, proj, proj, dt_bias, a_log, dfull, nw)


def _ssd_bwd(proj, dproj, xa, dout, s_in, dt_bias, a_log, dfull, nw, name):
    nc = SEQ // M2_CHUNK

    def body(x_ref, b_ref, c_ref, dt_ref, z_ref, do_ref, sin_ref, dtb_ref, al_ref, df_ref, nw_ref, dproj_in,
             dp_ref, dxa_ref, ddtb_ref, dal_ref, ddf_ref, dnw_ref, dst):
        @pl.when(pl.program_id(0) == 0)
        def _():
            dst[...] = jnp.zeros_like(dst)
            for r in (ddtb_ref, dal_ref, ddf_ref, dnw_ref):
                r[...] = jnp.zeros_like(r)

        states = [sin_ref[0, j] for j in range(N_PAIR)]
        _, vjp = jax.vjp(_ssd_chunk, _blocks(x_ref, 4), _blocks(b_ref, 2), _blocks(c_ref, 2), dt_ref[...],
                         _blocks(z_ref, 4), states, dtb_ref[...], al_ref[...], _blocks(df_ref, 4), _blocks(nw_ref, 4))
        dxs, dbs, dcs, ddt, dzs, dstates, ddtb, dal, ddfs, dnws = vjp(
            ([do_ref[0, :, j * LANES:(j + 1) * LANES] for j in range(N_PAIR)], [dst[j] for j in range(N_PAIR)]))
        for j in range(N_PAIR):
            sl = slice(j * LANES, (j + 1) * LANES)
            dxa_ref[:, sl] = dxs[j]
            dp_ref[:, sl] = dzs[j]
            dst[j] = dstates[j]
            ddf_ref[:, sl] += ddfs[j]
            dnw_ref[:, sl] += dnws[j]
        for g in range(2):
            dxa_ref[:, BW + g * LANES:BW + (g + 1) * LANES] = dbs[g]
            dxa_ref[:, BW + 256 + g * LANES:BW + 256 + (g + 1) * LANES] = dcs[g]
        dp_ref[:, BW:BW + LANES] = ddt
        dp_ref[:, BW + LANES:] = jnp.zeros((M2_CHUNK, 2 * BW - BW - LANES), F32)
        ddtb_ref[...] += ddtb
        dal_ref[...] += dal

    rev = lambda w, col=0: pl.BlockSpec((M2_CHUNK, w), lambda i: (nc - 1 - i, col))
    vec8 = pl.BlockSpec((1, LANES), lambda i: (0, 0))
    vec = pl.BlockSpec((1, BW), lambda i: (0, 0))
    return pl.pallas_call(
        body, name=name, grid=(nc,),
        in_specs=[rev(BW), rev(256, 2), rev(256, 3), rev(LANES, C_DT // LANES), rev(BW, C_M2Z // BW),
                  pl.BlockSpec((1, M2_CHUNK, BW), lambda i: (2, nc - 1 - i, 0)),
                  pl.BlockSpec((1, N_PAIR, M2_STATE, LANES), lambda i: (nc - 1 - i, 0, 0, 0)), vec8, vec8, vec, vec,
                  pl.BlockSpec(memory_space=pl.ANY)],
        out_specs=[rev(2 * BW, C_M2Z // (2 * BW)), rev(M2_CONV_CH), vec8, vec8, vec, vec],
        input_output_aliases={11: 0},
        out_shape=[jax.ShapeDtypeStruct((SEQ, IN_PAD), F32), jax.ShapeDtypeStruct((SEQ, M2_CONV_CH), F32),
                   jax.ShapeDtypeStruct((1, LANES), F32), jax.ShapeDtypeStruct((1, LANES), F32),
                   jax.ShapeDtypeStruct((1, BW), F32), jax.ShapeDtypeStruct((1, BW), F32)],
        scratch_shapes=[pltpu.VMEM((N_PAIR, M2_STATE, LANES), F32)],
        compiler_params=_cparams(("arbitrary",)),
    )(xa, xa, xa, proj, proj, dout, s_in, dt_bias, a_log, dfull, nw, dproj)


def _sc_specs():
    col = lambda kind: pl.BlockSpec((SEQ, LANES), lambda j: (0, C_SC // LANES + 4 * j + kind))
    return [col(0), col(1), col(2), col(3)]


def _sc_fwd(proj, w, name):
    def body(b_ref, c_ref, h_ref, g_ref, w_ref, o_ref):
        ch = c_ref[...] * h_ref[...]
        acc = jnp.zeros_like(ch)
        for k in range(SC_CONV):
            acc = acc + w_ref[k:k + 1, :] * _shift_down(ch, SC_CONV - 1 - k)
        o_ref[...] = (b_ref[...] * acc * _silu(g_ref[...])).astype(BF16)

    return pl.pallas_call(
        body, name=name, grid=(BW // LANES,),
        in_specs=_sc_specs() + [pl.BlockSpec((SC_CONV, LANES), lambda j: (0, j))],
        out_specs=pl.BlockSpec((SEQ, LANES), lambda j: (0, j)),
        out_shape=jax.ShapeDtypeStruct((SEQ, BW), BF16),
        compiler_params=_cparams(("parallel",)),
    )(proj, proj, proj, proj, w)


def _sc_bwd(proj, dproj, dout, w, name):
    def body(b_ref, c_ref, h_ref, g_ref, do_ref, w_ref, dproj_in, dp_ref, dw_ref):
        cv, hv, gv = c_ref[...], h_ref[...], g_ref[...]
        ch = cv * hv
        chs = [_shift_down(ch, SC_CONV - 1 - k) for k in range(SC_CONV)]
        acc = jnp.zeros_like(ch)
        for k in range(SC_CONV):
            acc = acc + w_ref[k:k + 1, :] * chs[k]
        sg = jax.nn.sigmoid(gv)
        do = do_ref[0]
        bv = b_ref[...]
        dp_ref[:, 0:LANES] = do * acc * (gv * sg)
        dp_ref[:, 3 * LANES:] = do * bv * acc * (sg * (1.0 + gv * (1.0 - sg)))
        dacc = do * bv * (gv * sg)
        dch = jnp.zeros_like(ch)
        for k in range(SC_CONV):
            dch = dch + w_ref[k:k + 1, :] * _shift_up(dacc, SC_CONV - 1 - k)
            dw_ref[k:k + 1, :] = jnp.sum(dacc * chs[k], axis=0, keepdims=True)
        dp_ref[:, LANES:2 * LANES] = dch * hv
        dp_ref[:, 2 * LANES:3 * LANES] = dch * cv

    wsp = pl.BlockSpec((SC_CONV, LANES), lambda j: (0, j))
    return pl.pallas_call(
        body, name=name, grid=(BW // LANES,),
        in_specs=_sc_specs() + [pl.BlockSpec((1, SEQ, LANES), lambda j: (3, 0, j)), wsp, pl.BlockSpec(memory_space=pl.ANY)],
        out_specs=[pl.BlockSpec((SEQ, 4 * LANES), lambda j: (0, C_SC // (4 * LANES) + j)), wsp],
        input_output_aliases={6: 0},
        out_shape=[jax.ShapeDtypeStruct((SEQ, IN_PAD), F32), jax.ShapeDtypeStruct((SC_CONV, BW), F32)],
        compiler_params=_cparams(("parallel",)),
    )(proj, proj, proj, proj, dout, w, dproj)


MERGE_T = 256
MERGE_BWD_T = 512


def _merge_fwd(proj, ys, merge_b, w_branch, name):
    def body(y_ref, lg_ref, b_ref, w_ref, o_ref):
        acc = jnp.zeros((MERGE_T, D_MODEL), F32)
        for k in range(N_BRANCH):
            gate = jax.nn.sigmoid(lg_ref[:, k * D_MODEL:(k + 1) * D_MODEL] + b_ref[k])
            acc = acc + gate * _dg(y_ref[k], w_ref[k], 1, 0)
        o_ref[...] = acc.astype(BF16)

    return pl.pallas_call(
        body, name=name, grid=(SEQ // MERGE_T,),
        in_specs=[pl.BlockSpec((N_BRANCH, MERGE_T, BW), lambda i: (0, i, 0)),
                  pl.BlockSpec((MERGE_T, N_BRANCH * D_MODEL), lambda i: (i, C_MERGE // (N_BRANCH * D_MODEL))),
                  pl.BlockSpec((N_BRANCH, 1, D_MODEL), lambda i: (0, 0, 0)),
                  pl.BlockSpec((N_BRANCH, BW, D_MODEL), lambda i: (0, 0, 0))],
        out_specs=pl.BlockSpec((MERGE_T, D_MODEL), lambda i: (i, 0)),
        out_shape=jax.ShapeDtypeStruct((SEQ, D_MODEL), BF16),
        compiler_params=_cparams(("parallel",)),
    )(ys, proj, merge_b, w_branch)


def _merge_bwd(proj, ys, dm, merge_b, w_branch, name):
    nt = SEQ // MERGE_BWD_T

    def body(y_ref, lg_ref, dm_ref, b_ref, w_ref, dy_ref, dlg_ref, dw_ref, db_ref, dw_acc):
        i = pl.program_id(1)
        gate = jax.nn.sigmoid(lg_ref[...] + b_ref[0])
        y = y_ref[0]
        dmv = dm_ref[...]
        dbo = (gate * dmv).astype(BF16)
        dlg = _dg(y, w_ref[0], 1, 0) * dmv * gate * (1.0 - gate)
        dlg_ref[...] = dlg
        dy_ref[0] = _dg(dbo, w_ref[0], 1, 1)
        dwp = _dg(y, dbo, 0, 0)
        dbp = jnp.sum(dlg, axis=0, keepdims=True)

        @pl.when(i == 0)
        def _():
            dw_acc[...] = dwp
            db_ref[0] = dbp

        @pl.when(i > 0)
        def _():
            dw_acc[...] += dwp
            db_ref[0] += dbp

        @pl.when(i == nt - 1)
        def _():
            dw_ref[0] = dw_acc[...].astype(BF16)

    return pl.pallas_call(
        body, name=name, grid=(N_BRANCH, nt),
        in_specs=[pl.BlockSpec((1, MERGE_BWD_T, BW), lambda k, i: (k, i, 0)),
                  pl.BlockSpec((MERGE_BWD_T, D_MODEL), lambda k, i: (i, C_MERGE // D_MODEL + k)),
                  pl.BlockSpec((MERGE_BWD_T, D_MODEL), lambda k, i: (i, 0)),
                  pl.BlockSpec((1, 1, D_MODEL), lambda k, i: (k, 0, 0)),
                  pl.BlockSpec((1, BW, D_MODEL), lambda k, i: (k, 0, 0))],
        out_specs=[pl.BlockSpec((1, MERGE_BWD_T, BW), lambda k, i: (k, i, 0)),
                   pl.BlockSpec((MERGE_BWD_T, D_MODEL), lambda k, i: (i, k)),
                   pl.BlockSpec((1, BW, D_MODEL), lambda k, i: (k, 0, 0)),
                   pl.BlockSpec((1, 1, D_MODEL), lambda k, i: (k, 0, 0))],
        out_shape=[jax.ShapeDtypeStruct((N_BRANCH, SEQ, BW), F32), jax.ShapeDtypeStruct((SEQ, IN_PAD), F32),
                   jax.ShapeDtypeStruct((N_BRANCH, BW, D_MODEL), BF16), jax.ShapeDtypeStruct((N_BRANCH, 1, D_MODEL), F32)],
        scratch_shapes=[pltpu.VMEM((BW, D_MODEL), F32)],
        compiler_params=_cparams(("parallel", "arbitrary")),
    )(ys, proj, dm, merge_b, w_branch)


def _adamw(glist, w, m, v, rows, name):
    nl = len(glist)
    n, r, c = glist[0].shape
    assert w.shape == (nl, r, c) and r % rows == 0
    nb = r // rows

    def body(*refs):
        g_refs = refs[:nl]
        w_ref, m_ref, v_ref, go_ref, d_ref, mo_ref, vo_ref = refs[nl:]
        for layer in range(nl):
            @pl.when(pl.program_id(0) == layer)
            def _(g_ref=g_refs[layer]):
                g = g_ref[0].astype(F32)
                for s in range(1, n):
                    g = g + g_ref[s].astype(F32)
                mn = ADAM_B1 * m_ref[0] + (1.0 - ADAM_B1) * g
                vn = ADAM_B2 * v_ref[0] + (1.0 - ADAM_B2) * jnp.square(g)
                m_hat = mn / (1.0 - ADAM_B1 ** ADAM_STEP)
                v_hat = vn / (1.0 - ADAM_B2 ** ADAM_STEP)
                go_ref[0] = g
                d_ref[0] = -ADAM_LR * (m_hat / (jnp.sqrt(v_hat) + ADAM_EPS) + ADAM_WD * w_ref[0])
                mo_ref[0] = mn
                vo_ref[0] = vn

    def g_spec(layer):
        return pl.BlockSpec((n, rows, c), lambda a, i: (0, jnp.where(a < layer, 0, jnp.where(a == layer, i, nb - 1)), 0))

    blk = pl.BlockSpec((1, rows, c), lambda a, i: (a, i, 0))
    out = jax.ShapeDtypeStruct((nl, r, c), F32)
    return pl.pallas_call(
        body, name=name, grid=(nl, nb),
        in_specs=[g_spec(layer) for layer in range(nl)] + [blk, blk, blk],
        out_specs=[blk, blk, blk, blk], out_shape=[out, out, out, out],
        compiler_params=_cparams(("arbitrary", "arbitrary")),
    )(*glist, w, m, v)


def _slot_sum(gslots, name):
    n, r, c = gslots.shape

    def body(g_ref, o_ref):
        g = g_ref[0]
        for s in range(1, n):
            g = g + g_ref[s]
        o_ref[...] = g

    return pl.pallas_call(
        body, name=name, in_specs=[pl.BlockSpec((n, r, c), lambda: (0, 0, 0))],
        out_specs=pl.BlockSpec((r, c), lambda: (0, 0)), out_shape=jax.ShapeDtypeStruct((r, c), F32),
        compiler_params=_cparams(None),
    )(gslots)


def _me_and_peers():
    x, y, c = lax.axis_index("x"), lax.axis_index("y"), lax.axis_index("c")
    me = 4 * x + 2 * y + c
    peers = []
    for k in range(1, N_DEV):
        px = 1 - x if (k >> 2) & 1 else x
        py = 1 - y if (k >> 1) & 1 else y
        pc = 1 - c if k & 1 else c
        peers.append((4 * px + 2 * py + pc, (px, py, pc)))
    return me, peers


def _exchange(tensors, gather, name):
    n = len(tensors)

    def body(*refs):
        ins, outs = refs[:n], refs[n:2 * n]
        send_sems, recv_sems, local_sems = refs[2 * n:]
        me, peers = _me_and_peers()
        started = []
        for t in range(n):
            own = pltpu.make_async_copy(ins[t] if gather else ins[t].at[me], outs[t].at[me], local_sems.at[t])
            own.start()
            started.append(own)
            for k, (pidx, pos) in enumerate(peers):
                cp = pltpu.make_async_remote_copy(
                    src_ref=ins[t] if gather else ins[t].at[pidx], dst_ref=outs[t].at[me],
                    send_sem=send_sems.at[t, k], recv_sem=recv_sems.at[t, k], device_id=pos, device_id_type=MESH)
                cp.start()
                started.append(cp)
        for cp in started:
            cp.wait()

    any_spec = pl.BlockSpec(memory_space=pl.ANY)
    outs = pl.pallas_call(
        body, name=name, in_specs=[any_spec] * n, out_specs=[any_spec] * n,
        out_shape=[jax.ShapeDtypeStruct(((N_DEV,) + t.shape) if gather else t.shape, t.dtype) for t in tensors],
        scratch_shapes=[pltpu.SemaphoreType.DMA((n, N_DEV - 1)), pltpu.SemaphoreType.DMA((n, N_DEV - 1)),
                        pltpu.SemaphoreType.DMA((n,))],
        compiler_params=pltpu.CompilerParams(has_side_effects=True),
    )(*tensors)
    return list(outs)


_HBM = pl.BlockSpec(memory_space=pltpu.HBM)
_SEM = pl.BlockSpec(memory_space=pltpu.SEMAPHORE)
_EFFECT = pltpu.SideEffectType.DATAFLOW_SIDE_EFFECTING


def _xchg_copies(ins, lands, send_sems, recv_sems, local_sems, gather):
    me, peers = _me_and_peers()
    local, remote = [], []
    for t in range(len(ins)):
        local.append(pltpu.make_async_copy(ins[t] if gather else ins[t].at[me], lands[t].at[me], local_sems.at[t]))
        for k, (pidx, pos) in enumerate(peers):
            remote.append(pltpu.make_async_remote_copy(
                src_ref=ins[t] if gather else ins[t].at[pidx], dst_ref=lands[t].at[me],
                send_sem=send_sems.at[t * (N_DEV - 1) + k], recv_sem=recv_sems.at[t * (N_DEV - 1) + k],
                device_id=pos, device_id_type=MESH))
    return local, remote


def _xchg_start(tensors, gather, name):
    n = len(tensors)
    land_shapes = [((N_DEV,) + t.shape) if gather else t.shape for t in tensors]

    def body(*refs):
        ins, lands = refs[:n], refs[n:2 * n]
        send_sems, recv_sems, local_sems = refs[2 * n:2 * n + 3]
        token = refs[-1]
        local, remote = _xchg_copies(ins, lands, send_sems, recv_sems, local_sems, gather)
        for cp in local + remote:
            cp.start()
        token[...] = jnp.zeros_like(token)

    outs = pl.pallas_call(
        body, name=name,
        out_shape=(pltpu.SemaphoreType.DMA((n * (N_DEV - 1),)), pltpu.SemaphoreType.DMA((n * (N_DEV - 1),)),
                   pltpu.SemaphoreType.DMA((n,)),
                   *[pltpu.HBM(t.shape, t.dtype) for t in tensors],
                   *[pltpu.HBM(s, t.dtype) for s, t in zip(land_shapes, tensors)],
                   jax.ShapeDtypeStruct((8, LANES), F32)),
        in_specs=[_HBM] * (2 * n),
        out_specs=(_SEM, _SEM, _SEM, *[_HBM] * (2 * n), pl.BlockSpec(memory_space=pltpu.VMEM)),
        input_output_aliases={t: 3 + t for t in range(2 * n)},
        compiler_params=pltpu.CompilerParams(has_side_effects=_EFFECT),
    )(*[pltpu.with_memory_space_constraint(t, pltpu.HBM) for t in tensors],
      *[pltpu.with_memory_space_constraint(lax.empty(s, t.dtype), pltpu.HBM) for s, t in zip(land_shapes, tensors)])
    return outs[:-1], outs[-1]


def _xchg_wait(state, gather, after, name):
    send_sems, recv_sems, local_sems = state[:3]
    n = (len(state) - 3) // 2
    thru = state[3:]

    def body(*refs):
        ins, lands = refs[:n], refs[n:2 * n]
        s_sems, r_sems, l_sems = refs[2 * n:2 * n + 3]
        local, remote = _xchg_copies(ins, lands, s_sems, r_sems, l_sems, gather)
        for cp in local:
            cp.wait()
        for cp in remote:
            cp.wait_send()
            cp.wait_recv()

    outs = pl.pallas_call(
        body, name=name,
        out_shape=tuple(pltpu.HBM(t.shape, t.dtype) for t in thru),
        in_specs=[_HBM] * (2 * n) + [_SEM, _SEM, _SEM, pl.BlockSpec(memory_space=pl.ANY)],
        out_specs=tuple([_HBM] * (2 * n)),
        input_output_aliases={t: t for t in range(2 * n)},
        compiler_params=pltpu.CompilerParams(has_side_effects=_EFFECT),
    )(*thru, send_sems, recv_sems, local_sems, after)
    return list(outs[n:])


WEIGHTS = ['norm_w', 'w_in', 's5_lambda_re', 's5_lambda_im', 's5_b_re', 's5_b_im', 's5_c_re', 's5_c_im', 's5_d',
           's5_log_step', 's5_w_glu', 'sgu_ln_w', 'sgu_ln_b', 'sgu_w', 'sgu_b', 'm2_conv_w', 'm2_conv_b', 'm2_dt_bias',
           'm2_a_log', 'm2_d', 'm2_norm_w', 'sc_conv_w', 'merge_b', 'w_branch', 'w_out', 'final_norm_w']
BIG_SHARDED = ['w_in', 'w_branch', 'w_out', 's5_w_glu']
SMALL_SHARDED = ['m2_conv_w', 'sc_conv_w', 'merge_b']
REPLICATED = [n for n in WEIGHTS if n not in BIG_SHARDED + SMALL_SHARDED]
S5_NAMES = ['s5_lambda_re', 's5_lambda_im', 's5_b_re', 's5_b_im', 's5_c_re', 's5_c_im', 's5_d', 's5_log_step']


def _sc_interleave(t):
    lead = t.shape[:-1]
    return jnp.swapaxes(t.reshape(lead + (4, 4, LANES)), -3, -2).reshape(lead + (4 * BW,))


def _pad_in(w):
    z = lambda n: jnp.zeros(w.shape[:-1] + (n,), w.dtype)
    return jnp.concatenate([w[..., 6152:], w[..., 0:1024], w[..., 3072:4096], w[..., 1024:2560], z(512),
                            w[..., 2560:3072], w[..., 4096:4104], z(504), _sc_interleave(w[..., 4104:6152])], axis=-1)


def _unpad_in(g):
    return jnp.concatenate([g[..., C_S5U:C_S5U + 1024], g[..., C_SGU_U:C_SGU_U + 1536], g[..., C_M2Z:C_M2Z + 512],
                            g[..., C_M2X:C_M2X + 1024], g[..., C_DT:C_DT + 8], _sc_interleave(g[..., C_SC:]),
                            g[..., :N_BRANCH * D_MODEL]], axis=-1)


def _rows128(flat, row_mult=8):
    n = flat.shape[0]
    per = LANES * row_mult
    total = -(-n // per) * per
    return jnp.pad(flat, (0, total - n)).reshape(total // LANES, LANES)


def _pad_lanes(v):
    return jnp.pad(v, (0, LANES - v.shape[0])).reshape(1, LANES)


def _layer_fwd(x, i, p, full):
    nw = p['norm_w'][i].reshape(1, D_MODEL)
    h = _rmsnorm_fwd(x, nw, f"rms_fwd{i}")
    proj = _matmul(h, full['w_in'], 1, 0, F32, 1024, 1024, 1024, f"proj{i}")
    disc, disc_vjp = jax.vjp(_s5_disc, *[p[n][i] for n in S5_NAMES])
    s5w = [t.astype(BF16) for t in disc[:4]] + [disc[4], disc[5], full['s5_w_glu']]
    ya, sre, sim = _s5_fwd(proj, *s5w, f"s5_fwd{i}")
    sgw = [p['sgu_ln_w'][i].reshape(1, BW), p['sgu_ln_b'][i].reshape(1, BW), p['sgu_w'][i],
           jnp.repeat(p['sgu_b'][i].T, BW // SGU_HEADS, axis=1)]
    yb = _sgu_fwd(proj, *sgw, f"sgu_fwd{i}")
    cw, cb = full['m2_conv_w'], p['m2_conv_b'][i].reshape(1, M2_CONV_CH)
    xa = _m2_conv_fwd(proj, cw, cb, f"m2conv_fwd{i}")
    m2w = [_pad_lanes(p['m2_dt_bias'][i]), _pad_lanes(p['m2_a_log'][i]),
           jnp.repeat(p['m2_d'][i], M2_HEAD_DIM).reshape(1, BW), p['m2_norm_w'][i].reshape(1, BW)]
    yc, s_in = _ssd_fwd(proj, xa, *m2w, f"ssd_fwd{i}")
    scw = full['sc_conv_w']
    yd = _sc_fwd(proj, scw, f"sc_fwd{i}")
    ys = jnp.stack([ya, yb, yc, yd])
    mb = full['merge_b'].reshape(N_BRANCH, 1, D_MODEL)
    merged = _merge_fwd(proj, ys, mb, full['w_branch'], f"merge_fwd{i}")
    x_new = _matmul(merged, full['w_out'], 1, 0, F32, 1024, 1024, 1024, f"out{i}", residual=x)
    saved = dict(x=x, nw=nw, h=h, proj=proj, disc_vjp=disc_vjp, s5w=s5w, sre=sre, sim=sim, sgw=sgw, cw=cw, cb=cb, xa=xa,
                 m2w=m2w, s_in=s_in, scw=scw, ys=ys, mb=mb, merged=merged)
    return x_new, saved


def _layer_bwd(dx_out, i, sv, full):
    g = {}
    proj = sv['proj']
    dm = _matmul(dx_out, full['w_out'], 1, 1, F32, 1024, 1024, 1024, f"dmerged{i}")
    g['w_out'] = _matmul(sv['merged'], dx_out, 0, 0, BF16, 1024, 1024, 1024, f"gw_out{i}")
    dys, dproj, g['w_branch'], dmb = _merge_bwd(proj, sv['ys'], dm, sv['mb'], full['w_branch'], f"merge_bwd{i}")
    g['merge_b'] = dmb.reshape(N_BRANCH, D_MODEL)
    dproj, dbbre, dbbim, dcre, dcim, da, dd, dwg = _s5_bwd(proj, dproj, dys, sv['sre'], sv['sim'], *sv['s5w'], f"s5_bwd{i}")
    for n, t in zip(S5_NAMES, sv['disc_vjp']((dbbre, dbbim, dcre, dcim, da, dd))):
        g[n] = t
    g['s5_w_glu'] = dwg.astype(BF16)
    dproj, dlw, dlb, g['sgu_w'], dbias = _sgu_bwd(proj, dproj, dys, *sv['sgw'], f"sgu_bwd{i}")
    g['sgu_ln_w'], g['sgu_ln_b'] = dlw[0], dlb[0]
    g['sgu_b'] = dbias.reshape(SGU_CHUNK, SGU_HEADS, BW // SGU_HEADS).sum(-1).T
    dproj, dxa, ddtb, dal, ddf, dnw = _ssd_bwd(proj, dproj, sv['xa'], dys, sv['s_in'], *sv['m2w'], f"ssd_bwd{i}")
    dproj, g['m2_conv_w'], dcb = _m2_conv_bwd(proj, dproj, dxa, sv['cw'], sv['cb'], f"m2conv_bwd{i}")
    g['m2_conv_b'], g['m2_norm_w'] = dcb[0], dnw[0]
    g['m2_dt_bias'], g['m2_a_log'] = ddtb[0, :M2_HEADS], dal[0, :M2_HEADS]
    g['m2_d'] = ddf.reshape(M2_HEADS, M2_HEAD_DIM).sum(-1)
    dproj, g['sc_conv_w'] = _sc_bwd(proj, dproj, dys, sv['scw'], f"sc_bwd{i}")
    dh = _matmul(dproj, full['w_in'], 1, 1, F32, 1024, 1024, 1024, f"dh{i}")
    g['w_in'] = _matmul(sv['h'], dproj, 0, 0, BF16, 1024, 1024, 1024, f"gw_in{i}")
    dx_in, dnw_l = _rmsnorm_bwd(sv['x'], sv['nw'], dh, dx_out, f"rms_bwd{i}")
    g['norm_w'] = dnw_l[0]
    return dx_in, g


def _split8(t, axis):
    shp = t.shape
    t = t.reshape(shp[:axis] + (N_DEV, shp[axis] // N_DEV) + shp[axis + 1:])
    return jnp.moveaxis(t, axis, 0)


def _join8(t, axis):
    t = jnp.moveaxis(t, 0, axis)
    shp = t.shape
    return t.reshape(shp[:axis] + (shp[axis] * shp[axis + 1],) + shp[axis + 2:])


SHARD_AXIS = {'w_in': 2, 'w_branch': 3, 'w_out': 1, 's5_w_glu': 1, 'm2_conv_w': 2, 'sc_conv_w': 2, 'merge_b': 2}


def _layer_weights(gathered):
    full = {n: _join8(t, SHARD_AXIS[n] - 1) for n, t in zip(BIG_SHARDED, gathered)}
    full['w_in'] = _pad_in(full['w_in'])
    return full


def _layer_grad_blocks(g):
    return [_split8(_unpad_in(g[n]) if n == 'w_in' else g[n], SHARD_AXIS[n] - 1) for n in BIG_SHARDED]


def kernel(x, norm_w, w_in, s5_lambda_re, s5_lambda_im, s5_b_re, s5_b_im, s5_c_re, s5_c_im, s5_d, s5_log_step, s5_w_glu, sgu_ln_w, sgu_ln_b, sgu_w, sgu_b, m2_conv_w, m2_conv_b, m2_dt_bias, m2_a_log, m2_d, m2_norm_w, sc_conv_w, merge_b, w_branch, w_out, final_norm_w, loss_target, m_norm_w, m_w_in, m_s5_lambda_re, m_s5_lambda_im, m_s5_b_re, m_s5_b_im, m_s5_c_re, m_s5_c_im, m_s5_d, m_s5_log_step, m_s5_w_glu, m_sgu_ln_w, m_sgu_ln_b, m_sgu_w, m_sgu_b, m_m2_conv_w, m_m2_conv_b, m_m2_dt_bias, m_m2_a_log, m_m2_d, m_m2_norm_w, m_sc_conv_w, m_merge_b, m_w_branch, m_w_out, m_final_norm_w, v_norm_w, v_w_in, v_s5_lambda_re, v_s5_lambda_im, v_s5_b_re, v_s5_b_im, v_s5_c_re, v_s5_c_im, v_s5_d, v_s5_log_step, v_s5_w_glu, v_sgu_ln_w, v_sgu_ln_b, v_sgu_w, v_sgu_b, v_m2_conv_w, v_m2_conv_b, v_m2_dt_bias, v_m2_a_log, v_m2_d, v_m2_norm_w, v_sc_conv_w, v_merge_b, v_w_branch, v_w_out, v_final_norm_w):
    loc = locals()
    p = {n: loc[n] for n in WEIGHTS}
    mom = {n: loc['m_' + n] for n in WEIGHTS}
    vel = {n: loc['v_' + n] for n in WEIGHTS}

    small_sizes = [p[n].size for n in SMALL_SHARDED]
    small_pack = _rows128(jnp.concatenate([p[n].reshape(-1) for n in SMALL_SHARDED]))
    gath0, tok = _xchg_start([p[n][0].astype(BF16) for n in BIG_SHARDED] + [small_pack], True, "gather0_start")
    gath1, tok = _xchg_start([(p[n][1] + tok[0, 0] if n == 's5_w_glu' else p[n][1]).astype(BF16) for n in BIG_SHARDED],
                             True, "gather1_start")
    got0 = _xchg_wait(gath0, True, tok, "gather0_wait")
    small_all = got0[-1].reshape(N_DEV, -1)
    small_full, off = {}, 0
    for n, sz in zip(SMALL_SHARDED, small_sizes):
        small_full[n] = _join8(small_all[:, off:off + sz].reshape((N_DEV,) + p[n].shape), SHARD_AXIS[n])
        off += sz
    full = [dict(_layer_weights(got0[:-1]), **{n: small_full[n][0] for n in SMALL_SHARDED})]

    saved, layer_g = [None] * DEPTH, [None] * DEPTH
    xs, saved[0] = _layer_fwd(x[0], 0, p, full[0])
    got1 = _xchg_wait(gath1, True, xs, "gather1_wait")
    full.append(dict(_layer_weights(got1), **{n: small_full[n][1] for n in SMALL_SHARDED}))
    xs, saved[1] = _layer_fwd(xs, 1, p, full[1])
    loss_row, dx, dfw = _loss_head(xs, final_norm_w.reshape(1, D_MODEL), loss_target[0])
    loss = lax.psum(loss_row[0, 0], ("x", "y", "c"))
    dx, layer_g[1] = _layer_bwd(dx, 1, saved[1], full[1])
    scat1, tok = _xchg_start(_layer_grad_blocks(layer_g[1]), False, "scatter1_start")
    saved[0]['mb'] = saved[0]['mb'] + tok[0, 0]
    dx, layer_g[0] = _layer_bwd(dx, 0, saved[0], full[0])
    grads = {n: jnp.stack([layer_g[i][n] for i in range(DEPTH)]) for n in SMALL_SHARDED + REPLICATED if n != 'final_norm_w'}
    grads['final_norm_w'] = dfw[0]

    repl_flat = jnp.concatenate([grads[n].reshape(-1) for n in REPLICATED])
    n_repl = repl_flat.shape[0]
    repl_rows = _rows128(repl_flat, 8 * N_DEV)
    rr = repl_rows.shape[0] // N_DEV
    small_g = jnp.concatenate(
        [_split8(grads[n], SHARD_AXIS[n]).reshape(N_DEV, -1) for n in SMALL_SHARDED] + [repl_rows.reshape(N_DEV, -1)], axis=1)
    n_small = small_g.shape[1] - rr * LANES
    pad = (-small_g.shape[1]) % (8 * LANES)
    small_g = jnp.pad(small_g, ((0, 0), (0, pad))).reshape(N_DEV, -1, LANES)
    scat0, tok = _xchg_start(_layer_grad_blocks(layer_g[0]) + [small_g], False, "scatter0_start")
    landed1 = _xchg_wait(scat1, False, tok, "scatter1_wait")
    landed0 = _xchg_wait(scat0, False, landed1[0], "scatter0_wait")

    out_g, out_d, out_m, out_v = {}, {}, {}, {}
    for k, n in enumerate(BIG_SHARDED):
        shp = p[n].shape
        c = shp[-1]
        r = p[n].size // (DEPTH * c)
        res = _adamw([t.reshape(N_DEV, r, c) for t in (landed0[k], landed1[k])],
                     *[d[n].reshape(DEPTH, r, c) for d in (p, mom, vel)],
                     {'w_in': 256, 'w_branch': 512, 'w_out': 128, 's5_w_glu': 64}[n], "adamw_" + n)
        out_g[n], out_d[n], out_m[n], out_v[n] = [o.reshape(shp) for o in res]
    small_sum = _slot_sum(landed0[-1], "sum_small").reshape(-1)
    repl_part = small_sum[n_small:n_small + rr * LANES].reshape(rr, LANES)
    repl_all = _exchange([repl_part], True, "gather_small")[0].reshape(-1)[:n_repl]
    names = SMALL_SHARDED + REPLICATED
    g_all = _rows128(jnp.concatenate([small_sum[:n_small], repl_all]))
    res = _adamw([g_all[None]], *[_rows128(jnp.concatenate([d[n].reshape(-1) for n in names]))[None] for d in (p, mom, vel)],
                 g_all.shape[0], "adamw_small")
    flat = [o.reshape(-1) for o in res]
    off = 0
    for n in names:
        sz = p[n].size
        out_g[n], out_d[n], out_m[n], out_v[n] = [f[off:off + sz].reshape(p[n].shape) for f in flat]
        off += sz
    return (loss, dx[None], *[out_g[n] for n in WEIGHTS], *[out_d[n] for n in WEIGHTS],
            *[out_m[n] for n in WEIGHTS], *[out_v[n] for n in WEIGHTS])
```

```python
import functools

import jax
import jax.numpy as jnp
from jax import lax
from jax.experimental import pallas as pl
from jax.experimental.pallas import tpu as pltpu

F32 = jnp.float32
BF16 = jnp.bfloat16

N_DEV = 8
SEQ = 2048
D_MODEL = 1024
DEPTH = 2
BW = 512
N_BRANCH = 4
EPS = 1e-6
S5_GROUPS, S5_STATE, S5_P = 32, 64, 16
S5_CH = S5_GROUPS * S5_STATE
SGU_CHUNK, SGU_HEADS = 128, 8
M2_HEADS, M2_HEAD_DIM, M2_STATE, M2_CHUNK, M2_CONV = 8, 64, 128, 128, 4
M2_CONV_CH = 1024
SC_CONV = 3
IN_DIM = 10248
IN_PAD = 11264
C_MERGE = 0
C_S5U, C_S5G = 4096, 4608
C_M2X = 5120
C_SGU_U, C_SGU_V, C_SGU_G = 6144, 6656, 7168
C_M2Z, C_DT = 8192, 8704
C_SC = 9216
SHARD_IN = IN_DIM // N_DEV

ADAM_LR, ADAM_B1, ADAM_B2, ADAM_EPS, ADAM_WD, ADAM_STEP = 0.001, 0.9, 0.999, 1e-08, 0.01, 10

VMEM_LIMIT = 56 * 1024 * 1024
LANES = 128

MESH = pl.DeviceIdType.MESH


def _cparams(sem=None, **kw):
    return pltpu.CompilerParams(dimension_semantics=sem, vmem_limit_bytes=VMEM_LIMIT, **kw)


def _dg(a, b, ca, cb, precision=None):
    return lax.dot_general(a, b, (((ca,), (cb,)), ((), ())), precision=precision,
                           preferred_element_type=F32)


@functools.partial(jax.custom_vjp, nondiff_argnums=(2, 3))
def _bdot(a, b, ca, cb):
    return _dg(a.astype(BF16), b.astype(BF16), ca, cb)


def _bdot_fwd(a, b, ca, cb):
    return _bdot(a, b, ca, cb), (a, b)


def _bdot_bwd(ca, cb, res, g):
    a, b = res
    gb, ab, bb = g.astype(BF16), a.astype(BF16), b.astype(BF16)
    da = _dg(gb, bb, 1, 1 - cb) if ca == 1 else _dg(bb, gb, 1 - cb, 1)
    db = _dg(ab, gb, 1 - ca, 0) if cb == 0 else _dg(gb, ab, 0, 1 - ca)
    return da.astype(a.dtype), db.astype(b.dtype)


_bdot.defvjp(_bdot_fwd, _bdot_bwd)


def _rms(x, w):
    return x * lax.rsqrt(jnp.mean(x * x, axis=-1, keepdims=True) + EPS) * w


def _silu(x):
    return x * jax.nn.sigmoid(x)


def _gelu(x):
    return 0.5 * x * (1.0 + jnp.tanh(0.7978845608028654 * (x + 0.044715 * (x * x * x))))


def _softplus(x):
    return jnp.maximum(x, 0.0) + jnp.log1p(jnp.exp(-jnp.abs(x)))


def _shift_down(x, s):
    if s == 0:
        return x
    row = lax.broadcasted_iota(jnp.int32, x.shape, 0)
    return jnp.where(row >= s, pltpu.roll(x, s, 0), 0.0)


def _shift_up(x, s):
    if s == 0:
        return x
    n = x.shape[0]
    row = lax.broadcasted_iota(jnp.int32, x.shape, 0)
    return jnp.where(row < n - s, pltpu.roll(x, n - s, 0), 0.0)


def _matmul(a, b, ca, cb, out_dtype, tm, tn, tk, name, residual=None, after=None):
    m = a.shape[1 - ca]
    k = a.shape[ca]
    n = b.shape[1 - cb]
    assert b.shape[cb] == k and m % tm == 0 and n % tn == 0 and k % tk == 0
    nk = k // tk
    a_spec = pl.BlockSpec((tm, tk), lambda i, j, kk: (i, kk)) if ca == 1 else pl.BlockSpec((tk, tm), lambda i, j, kk: (kk, i))
    b_spec = pl.BlockSpec((tk, tn), lambda i, j, kk: (kk, j)) if cb == 0 else pl.BlockSpec((tn, tk), lambda i, j, kk: (j, kk))
    o_spec = pl.BlockSpec((tm, tn), lambda i, j, kk: (i, j))
    has_res = residual is not None

    def body(*refs):
        refs = refs[:2 + has_res] + refs[2 + has_res + (after is not None):]
        if has_res:
            a_ref, b_ref, r_ref, o_ref, acc = refs
        else:
            a_ref, b_ref, o_ref, acc = refs
        kk = pl.program_id(2)
        part = _dg(a_ref[...].astype(BF16), b_ref[...].astype(BF16), ca, cb)

        @pl.when(kk == 0)
        def _():
            acc[...] = part

        @pl.when(kk > 0)
        def _():
            acc[...] += part

        @pl.when(kk == nk - 1)
        def _():
            r = acc[...]
            if has_res:
                r = r + r_ref[...]
            o_ref[...] = r.astype(out_dtype)

    ins = [a, b] + ([residual] if has_res else []) + ([after] if after is not None else [])
    specs = [a_spec, b_spec] + ([o_spec] if has_res else []) + ([pl.BlockSpec(memory_space=pl.ANY)] if after is not None else [])
    return pl.pallas_call(
        body, name=name, grid=(m // tm, n // tn, nk), in_specs=specs, out_specs=o_spec,
        out_shape=jax.ShapeDtypeStruct((m, n), out_dtype),
        scratch_shapes=[pltpu.VMEM((tm, tn), F32)],
        compiler_params=_cparams(("parallel", "parallel", "arbitrary")),
    )(*ins)


ROW_TILE = 512


def _rmsnorm_fwd(x, w, name):
    def body(x_ref, w_ref, o_ref):
        o_ref[...] = _rms(x_ref[...], w_ref[...]).astype(BF16)

    return pl.pallas_call(
        body, name=name, grid=(SEQ // ROW_TILE,),
        in_specs=[pl.BlockSpec((ROW_TILE, D_MODEL), lambda i: (i, 0)), pl.BlockSpec((1, D_MODEL), lambda i: (0, 0))],
        out_specs=pl.BlockSpec((ROW_TILE, D_MODEL), lambda i: (i, 0)),
        out_shape=jax.ShapeDtypeStruct((SEQ, D_MODEL), BF16),
        compiler_params=_cparams(("parallel",)),
    )(x, w)


def _rmsnorm_bwd(x, w, dh, dres, name):
    def body(x_ref, w_ref, dh_ref, dres_ref, dx_ref, dw_ref):
        _, vjp = jax.vjp(_rms, x_ref[...], w_ref[...])
        dx, dw = vjp(dh_ref[...])
        dx_ref[...] = dx + dres_ref[...]

        @pl.when(pl.program_id(0) == 0)
        def _():
            dw_ref[...] = dw

        @pl.when(pl.program_id(0) > 0)
        def _():
            dw_ref[...] += dw

    tile = pl.BlockSpec((ROW_TILE, D_MODEL), lambda i: (i, 0))
    vec = pl.BlockSpec((1, D_MODEL), lambda i: (0, 0))
    return pl.pallas_call(
        body, name=name, grid=(SEQ // ROW_TILE,),
        in_specs=[tile, vec, tile, tile], out_specs=[tile, vec],
        out_shape=[jax.ShapeDtypeStruct((SEQ, D_MODEL), F32), jax.ShapeDtypeStruct((1, D_MODEL), F32)],
        compiler_params=_cparams(("arbitrary",)),
    )(x, w, dh, dres)


def _loss_head(x, w, target):
    def body(x_ref, w_ref, t_ref, loss_ref, dx_ref, dw_ref):
        tgt = t_ref[...]

        def f(xv, wv):
            err = _rms(xv, wv) - tgt
            return 0.5 * jnp.sum(jnp.mean(err * err, axis=-1))

        loss, vjp = jax.vjp(f, x_ref[...], w_ref[...])
        dx, dw = vjp(jnp.ones((), F32))
        dx_ref[...] = dx
        lrow = jnp.full((1, LANES), loss, F32)

        @pl.when(pl.program_id(0) == 0)
        def _():
            dw_ref[...] = dw
            loss_ref[...] = lrow

        @pl.when(pl.program_id(0) > 0)
        def _():
            dw_ref[...] += dw
            loss_ref[...] += lrow

    tile = pl.BlockSpec((ROW_TILE, D_MODEL), lambda i: (i, 0))
    vec = pl.BlockSpec((1, D_MODEL), lambda i: (0, 0))
    return pl.pallas_call(
        body, name="loss_head", grid=(SEQ // ROW_TILE,),
        in_specs=[tile, vec, tile], out_specs=[pl.BlockSpec((1, LANES), lambda i: (0, 0)), tile, vec],
        out_shape=[jax.ShapeDtypeStruct((1, LANES), F32), jax.ShapeDtypeStruct((SEQ, D_MODEL), F32),
                   jax.ShapeDtypeStruct((1, D_MODEL), F32)],
        compiler_params=_cparams(("arbitrary",)),
    )(x, w, target)


S5_T = 256


def _s5_post(ypre, gate, wglu):
    y = _gelu(ypre)
    y = y * jax.nn.sigmoid(_bdot(y, wglu, 1, 0))
    return y * _silu(gate)


def _s5_fwd(proj, bbre, bbim, cre, cim, a2, dvec, wglu, name):
    def body(u_ref, g_ref, bbre_ref, bbim_ref, cre_ref, cim_ref, a_ref, d_ref, wg_ref, o_ref, sre_ref, sim_ref, st):
        @pl.when(pl.program_id(0) == 0)
        def _():
            st[...] = jnp.zeros_like(st)

        u = u_ref[...]
        ub = u.astype(BF16)
        sre_ref[...] = _dg(ub, bbre_ref[...], 1, 0)
        sim_ref[...] = _dg(ub, bbim_ref[...], 1, 0)
        ar, ai = a_ref[0:1, :], a_ref[1:2, :]

        def step(t, carry):
            sr, si = carry
            nr = ar * sr - ai * si + sre_ref[pl.ds(t, 1), :]
            ni = ar * si + ai * sr + sim_ref[pl.ds(t, 1), :]
            sre_ref[pl.ds(t, 1), :] = nr
            sim_ref[pl.ds(t, 1), :] = ni
            return nr, ni

        sr, si = lax.fori_loop(0, S5_T, step, (st[0:1, :], st[1:2, :]), unroll=8)
        st[0:1, :] = sr
        st[1:2, :] = si
        ypre = (_dg(sre_ref[...].astype(BF16), cre_ref[...], 1, 0) - _dg(sim_ref[...].astype(BF16), cim_ref[...], 1, 0)
                + d_ref[...] * u)
        o_ref[...] = _s5_post(ypre, g_ref[...], wg_ref[...]).astype(BF16)

    full = lambda shape: pl.BlockSpec(shape, lambda c: (0, 0))
    return pl.pallas_call(
        body, name=name, grid=(SEQ // S5_T,),
        in_specs=[pl.BlockSpec((S5_T, BW), lambda c: (c, C_S5U // BW)), pl.BlockSpec((S5_T, BW), lambda c: (c, C_S5G // BW)),
                  full((BW, S5_CH)), full((BW, S5_CH)), full((S5_CH, BW)), full((S5_CH, BW)),
                  full((2, S5_CH)), full((1, BW)), full((BW, BW))],
        out_specs=[pl.BlockSpec((S5_T, BW), lambda c: (c, 0)), pl.BlockSpec((S5_T, S5_CH), lambda c: (c, 0)),
                   pl.BlockSpec((S5_T, S5_CH), lambda c: (c, 0))],
        out_shape=[jax.ShapeDtypeStruct((SEQ, BW), BF16), jax.ShapeDtypeStruct((SEQ, S5_CH), F32),
                   jax.ShapeDtypeStruct((SEQ, S5_CH), F32)],
        scratch_shapes=[pltpu.VMEM((2, S5_CH), F32)],
        compiler_params=_cparams(("arbitrary",)),
    )(proj, proj, bbre, bbim, cre, cim, a2, dvec, wglu)


def _s5_bwd(proj, dproj, dout, sre, sim, bbre, bbim, cre, cim, a2, dvec, wglu, name):
    nc = SEQ // S5_T

    def body(u_ref, g_ref, do_ref, sre_ref, sim_ref, pre_ref, pim_ref, bbre_ref, bbim_ref, cre_ref, cim_ref, a_ref,
             d_ref, wg_ref, dproj_in, dp_ref, dbbre_ref, dbbim_ref, dcre_ref, dcim_ref, da_ref, dd_ref, dwg_ref,
             gre, gim, st):
        c = nc - 1 - pl.program_id(0)

        @pl.when(pl.program_id(0) == 0)
        def _():
            st[...] = jnp.zeros_like(st)
            for r in (dbbre_ref, dbbim_ref, dcre_ref, dcim_ref, da_ref, dd_ref, dwg_ref):
                r[...] = jnp.zeros_like(r)

        u = u_ref[...]
        s_re, s_im = sre_ref[...], sim_ref[...]

        def head(s_re, s_im, cre_v, cim_v, dv, uv, gv, wg):
            ypre = _bdot(s_re, cre_v, 1, 0) - _bdot(s_im, cim_v, 1, 0) + dv * uv
            return _s5_post(ypre, gv, wg)

        _, vjp = jax.vjp(head, s_re, s_im, cre_ref[...].astype(F32), cim_ref[...].astype(F32), d_ref[...], u,
                         g_ref[...], wg_ref[...].astype(F32))
        ds_re, ds_im, dcre, dcim, dd, du_d, dgate, dwg = vjp(do_ref[0])
        dcre_ref[...] += dcre
        dcim_ref[...] += dcim
        dd_ref[...] += dd
        dwg_ref[...] += dwg
        dp_ref[:, BW:] = dgate
        gre[...] = ds_re
        gim[...] = ds_im
        ar, ai = a_ref[0:1, :], a_ref[1:2, :]

        def step(i, carry):
            t = S5_T - 1 - i
            gr, gi = carry
            nr = gre[pl.ds(t, 1), :] + gr
            ni = gim[pl.ds(t, 1), :] + gi
            gre[pl.ds(t, 1), :] = nr
            gim[pl.ds(t, 1), :] = ni
            return ar * nr + ai * ni, ar * ni - ai * nr

        gr, gi = lax.fori_loop(0, S5_T, step, (st[0:1, :], st[1:2, :]), unroll=8)
        st[0:1, :] = gr
        st[1:2, :] = gi
        g_re, g_im = gre[...], gim[...]
        first = jnp.where(c > 0, 1.0, 0.0)
        row = lax.broadcasted_iota(jnp.int32, (S5_T, S5_CH), 0)
        p_re = jnp.where(row == 0, pre_ref[7:8, :] * first, pltpu.roll(s_re, 1, 0))
        p_im = jnp.where(row == 0, pim_ref[7:8, :] * first, pltpu.roll(s_im, 1, 0))
        da_ref[0:1, :] += jnp.sum(g_re * p_re + g_im * p_im, axis=0, keepdims=True)
        da_ref[1:2, :] += jnp.sum(g_im * p_re - g_re * p_im, axis=0, keepdims=True)
        ub, grb, gib = u.astype(BF16), g_re.astype(BF16), g_im.astype(BF16)
        dbbre_ref[...] += _dg(ub, grb, 0, 0)
        dbbim_ref[...] += _dg(ub, gib, 0, 0)
        dp_ref[:, :BW] = du_d + _dg(grb, bbre_ref[...], 1, 1) + _dg(gib, bbim_ref[...], 1, 1)

    full = lambda shape: pl.BlockSpec(shape, lambda i: (0, 0))
    rev = lambda w, col=0: pl.BlockSpec((S5_T, w), lambda i: (nc - 1 - i, col))
    prev = pl.BlockSpec((8, S5_CH), lambda i: (jnp.maximum((nc - 1 - i) * (S5_T // 8) - 1, 0), 0))
    return pl.pallas_call(
        body, name=name, grid=(nc,),
        in_specs=[rev(BW, C_S5U // BW), rev(BW, C_S5G // BW), pl.BlockSpec((1, S5_T, BW), lambda i: (0, nc - 1 - i, 0)),
                  rev(S5_CH), rev(S5_CH), prev, prev,
                  full((BW, S5_CH)), full((BW, S5_CH)), full((S5_CH, BW)), full((S5_CH, BW)),
                  full((2, S5_CH)), full((1, BW)), full((BW, BW)), pl.BlockSpec(memory_space=pl.ANY)],
        out_specs=[rev(2 * BW, C_S5U // (2 * BW)), full((BW, S5_CH)), full((BW, S5_CH)), full((S5_CH, BW)), full((S5_CH, BW)),
                   full((2, S5_CH)), full((1, BW)), full((BW, BW))],
        input_output_aliases={14: 0},
        out_shape=[jax.ShapeDtypeStruct((SEQ, IN_PAD), F32),
                   jax.ShapeDtypeStruct((BW, S5_CH), F32), jax.ShapeDtypeStruct((BW, S5_CH), F32),
                   jax.ShapeDtypeStruct((S5_CH, BW), F32), jax.ShapeDtypeStruct((S5_CH, BW), F32),
                   jax.ShapeDtypeStruct((2, S5_CH), F32), jax.ShapeDtypeStruct((1, BW), F32),
                   jax.ShapeDtypeStruct((BW, BW), F32)],
        scratch_shapes=[pltpu.VMEM((S5_T, S5_CH), F32), pltpu.VMEM((S5_T, S5_CH), F32), pltpu.VMEM((2, S5_CH), F32)],
        compiler_params=_cparams(("arbitrary",)),
    )(proj, proj, dout, sre, sim, sre, sim, bbre, bbim, cre, cim, a2, dvec, wglu, dproj)


def _s5_disc(lam_re, lam_im, b_re, b_im, c_re, c_im, d, log_step):
    step = jnp.exp(log_step)[:, None]
    mag = jnp.exp(lam_re * step)
    ab_re, ab_im = mag * jnp.cos(lam_im * step), mag * jnp.sin(lam_im * step)
    den = lam_re * lam_re + lam_im * lam_im
    nr = ab_re - 1.0
    coef_re = (nr * lam_re + ab_im * lam_im) / den
    coef_im = (ab_im * lam_re - nr * lam_im) / den
    bb_re = coef_re[..., None] * b_re - coef_im[..., None] * b_im
    bb_im = coef_re[..., None] * b_im + coef_im[..., None] * b_re
    def block_diag(t, rows_per, cols_per):
        wide = jnp.tile(t.reshape(S5_GROUPS * rows_per, cols_per), (1, S5_GROUPS))
        r = lax.broadcasted_iota(jnp.int32, wide.shape, 0) // rows_per
        c = lax.broadcasted_iota(jnp.int32, wide.shape, 1) // cols_per
        return jnp.where(r == c, wide, 0.0)

    bbre = block_diag(jnp.swapaxes(bb_re, 1, 2), S5_P, S5_STATE)
    bbim = block_diag(jnp.swapaxes(bb_im, 1, 2), S5_P, S5_STATE)
    cre = block_diag(jnp.swapaxes(c_re, 1, 2), S5_STATE, S5_P)
    cim = block_diag(jnp.swapaxes(c_im, 1, 2), S5_STATE, S5_P)
    a2 = jnp.stack([ab_re.reshape(-1), ab_im.reshape(-1)])
    return bbre, bbim, cre, cim, a2, d.reshape(1, BW)


def _left_lanes(shape):
    return lax.broadcasted_iota(jnp.int32, shape, 1) < 64


def _sgu_chunk(u, v, gate, ln_w, ln_b, w, bias):
    u32, v32 = _gelu(u), _gelu(v)
    mu = jnp.mean(v32, axis=-1, keepdims=True)
    var = jnp.mean(jnp.square(v32 - mu), axis=-1, keepdims=True)
    vn = (v32 - mu) * lax.rsqrt(var + EPS) * ln_w + ln_b
    t_i = lax.broadcasted_iota(jnp.int32, (SGU_CHUNK, SGU_CHUNK), 0)
    s_i = lax.broadcasted_iota(jnp.int32, (SGU_CHUNK, SGU_CHUNK), 1)
    causal = t_i >= s_i
    left = _left_lanes((SGU_CHUNK, LANES))
    sgate = _silu(gate)
    outs = []
    for j in range(BW // LANES):
        vb = vn[:, j * LANES:(j + 1) * LANES]
        s_blk = (_bdot(jnp.where(causal, w[2 * j], 0.0), jnp.where(left, vb, 0.0), 1, 0)
                 + _bdot(jnp.where(causal, w[2 * j + 1], 0.0), jnp.where(left, 0.0, vb), 1, 0))
        sl = slice(j * LANES, (j + 1) * LANES)
        outs.append(u32[:, sl] * (s_blk + bias[:, sl]) * sgate[:, sl])
    return outs


def _sgu_fwd(proj, ln_w, ln_b, w, bias, name):
    def body(u_ref, v_ref, g_ref, lw_ref, lb_ref, w_ref, b_ref, o_ref):
        outs = _sgu_chunk(u_ref[...], v_ref[...], g_ref[...], lw_ref[...], lb_ref[...], w_ref[...], b_ref[...])
        for j, o in enumerate(outs):
            o_ref[:, j * LANES:(j + 1) * LANES] = o.astype(BF16)

    blk = lambda col: pl.BlockSpec((SGU_CHUNK, BW), lambda c: (c, col // BW))
    vec = pl.BlockSpec((1, BW), lambda c: (0, 0))
    return pl.pallas_call(
        body, name=name, grid=(SEQ // SGU_CHUNK,),
        in_specs=[blk(C_SGU_U), blk(C_SGU_V), blk(C_SGU_G), vec, vec,
                  pl.BlockSpec((SGU_HEADS, SGU_CHUNK, SGU_CHUNK), lambda c: (0, 0, 0)),
                  pl.BlockSpec((SGU_CHUNK, BW), lambda c: (0, 0))],
        out_specs=pl.BlockSpec((SGU_CHUNK, BW), lambda c: (c, 0)),
        out_shape=jax.ShapeDtypeStruct((SEQ, BW), BF16),
        compiler_params=_cparams(("parallel",)),
    )(proj, proj, proj, ln_w, ln_b, w, bias)


def _sgu_bwd(proj, dproj, dout, ln_w, ln_b, w, bias, name):
    def body(u_ref, v_ref, g_ref, do_ref, lw_ref, lb_ref, w_ref, b_ref, dproj_in, dp_ref, dlw_ref, dlb_ref, dw_ref, db_ref):
        _, vjp = jax.vjp(_sgu_chunk, u_ref[...], v_ref[...], g_ref[...], lw_ref[...], lb_ref[...], w_ref[...], b_ref[...])
        do = do_ref[0]
        du, dv, dgate, dlw, dlb, dw, db = vjp([do[:, j * LANES:(j + 1) * LANES] for j in range(BW // LANES)])
        dp_ref[:, 0:BW] = du
        dp_ref[:, BW:2 * BW] = dv
        dp_ref[:, 2 * BW:3 * BW] = dgate
        dp_ref[:, 3 * BW:] = jnp.zeros((SGU_CHUNK, BW), F32)

        @pl.when(pl.program_id(0) == 0)
        def _():
            dlw_ref[...] = dlw
            dlb_ref[...] = dlb
            dw_ref[...] = dw
            db_ref[...] = db

        @pl.when(pl.program_id(0) > 0)
        def _():
            dlw_ref[...] += dlw
            dlb_ref[...] += dlb
            dw_ref[...] += dw
            db_ref[...] += db

    blk = lambda col: pl.BlockSpec((SGU_CHUNK, BW), lambda c: (c, col // BW))
    vec = pl.BlockSpec((1, BW), lambda c: (0, 0))
    wsp = pl.BlockSpec((SGU_HEADS, SGU_CHUNK, SGU_CHUNK), lambda c: (0, 0, 0))
    bsp = pl.BlockSpec((SGU_CHUNK, BW), lambda c: (0, 0))
    return pl.pallas_call(
        body, name=name, grid=(SEQ // SGU_CHUNK,),
        in_specs=[blk(C_SGU_U), blk(C_SGU_V), blk(C_SGU_G), pl.BlockSpec((1, SGU_CHUNK, BW), lambda c: (1, c, 0)),
                  vec, vec, wsp, bsp, pl.BlockSpec(memory_space=pl.ANY)],
        out_specs=[pl.BlockSpec((SGU_CHUNK, 4 * BW), lambda c: (c, C_SGU_U // (4 * BW))), vec, vec, wsp, bsp],
        input_output_aliases={8: 0},
        out_shape=[jax.ShapeDtypeStruct((SEQ, IN_PAD), F32), jax.ShapeDtypeStruct((1, BW), F32),
                   jax.ShapeDtypeStruct((1, BW), F32), jax.ShapeDtypeStruct((SGU_HEADS, SGU_CHUNK, SGU_CHUNK), F32),
                   jax.ShapeDtypeStruct((SGU_CHUNK, BW), F32)],
        compiler_params=_cparams(("arbitrary",)),
    )(proj, proj, proj, dout, ln_w, ln_b, w, bias, dproj)


CONV_BLK = 256


def _m2_conv_fwd(proj, w, b, name):
    def body(x_ref, w_ref, b_ref, o_ref):
        x = x_ref[...]
        acc = jnp.zeros_like(x) + b_ref[...]
        for k in range(M2_CONV):
            acc = acc + w_ref[k:k + 1, :] * _shift_down(x, M2_CONV - 1 - k)
        o_ref[...] = _silu(acc)

    return pl.pallas_call(
        body, name=name, grid=(M2_CONV_CH // CONV_BLK,),
        in_specs=[pl.BlockSpec((SEQ, CONV_BLK), lambda j: (0, C_M2X // CONV_BLK + j)),
                  pl.BlockSpec((M2_CONV, CONV_BLK), lambda j: (0, j)), pl.BlockSpec((1, CONV_BLK), lambda j: (0, j))],
        out_specs=pl.BlockSpec((SEQ, CONV_BLK), lambda j: (0, j)),
        out_shape=jax.ShapeDtypeStruct((SEQ, M2_CONV_CH), F32),
        compiler_params=_cparams(("parallel",)),
    )(proj, w, b)


def _m2_conv_bwd(proj, dproj, dxa, w, b, name):
    def body(x_ref, d_ref, w_ref, b_ref, dproj_in, dx_ref, dw_ref, db_ref):
        x = x_ref[...]
        xs = [_shift_down(x, M2_CONV - 1 - k) for k in range(M2_CONV)]
        acc = jnp.zeros_like(x) + b_ref[...]
        for k in range(M2_CONV):
            acc = acc + w_ref[k:k + 1, :] * xs[k]
        sg = jax.nn.sigmoid(acc)
        dacc = d_ref[...] * (sg * (1.0 + acc * (1.0 - sg)))
        dx = jnp.zeros_like(x)
        for k in range(M2_CONV):
            dx = dx + w_ref[k:k + 1, :] * _shift_up(dacc, M2_CONV - 1 - k)
            dw_ref[k:k + 1, :] = jnp.sum(dacc * xs[k], axis=0, keepdims=True)
        dx_ref[...] = dx
        db_ref[...] = jnp.sum(dacc, axis=0, keepdims=True)

    return pl.pallas_call(
        body, name=name, grid=(M2_CONV_CH // CONV_BLK,),
        in_specs=[pl.BlockSpec((SEQ, CONV_BLK), lambda j: (0, C_M2X // CONV_BLK + j)),
                  pl.BlockSpec((SEQ, CONV_BLK), lambda j: (0, j)),
                  pl.BlockSpec((M2_CONV, CONV_BLK), lambda j: (0, j)), pl.BlockSpec((1, CONV_BLK), lambda j: (0, j)),
                  pl.BlockSpec(memory_space=pl.ANY)],
        out_specs=[pl.BlockSpec((SEQ, CONV_BLK), lambda j: (0, C_M2X // CONV_BLK + j)),
                   pl.BlockSpec((M2_CONV, CONV_BLK), lambda j: (0, j)), pl.BlockSpec((1, CONV_BLK), lambda j: (0, j))],
        input_output_aliases={4: 0},
        out_shape=[jax.ShapeDtypeStruct((SEQ, IN_PAD), F32), jax.ShapeDtypeStruct((M2_CONV, M2_CONV_CH), F32),
                   jax.ShapeDtypeStruct((1, M2_CONV_CH), F32)],
        compiler_params=_cparams(("parallel",)),
    )(proj, dxa, w, b, dproj)


N_PAIR = M2_HEADS // 2
HI = lax.Precision.HIGHEST


def _col(a, h):
    lane = lax.broadcasted_iota(jnp.int32, a.shape, 1)
    return jnp.sum(jnp.where(lane == h, a, 0.0), axis=1, keepdims=True)


def _row(a, h):
    sub = lax.broadcasted_iota(jnp.int32, a.shape, 0)
    return jnp.sum(jnp.where(sub == h, a, 0.0), axis=0, keepdims=True)


def _ssd_chunk(xs, bms, cms, dtr, zs, states, dt_bias, a_log, dfs, nws):
    q = M2_CHUNK
    dt = _softplus(dtr + dt_bias)
    da = dt * (-jnp.exp(a_log))
    l_i = lax.broadcasted_iota(jnp.int32, (q, q), 0)
    s_i = lax.broadcasted_iota(jnp.int32, (q, q), 1)
    causal = l_i >= s_i
    tril = jnp.where(causal, 1.0, 0.0)
    a_cs = _dg(tril, da, 1, 0, HI)
    a_cs_t = _dg(da, tril, 0, 1, HI)
    a_end = _row(a_cs, q - 1)
    left = _left_lanes((q, LANES))
    left1 = _left_lanes((1, LANES))
    ys, nexts = [], []
    for j in range(N_PAIR):
        grp = j // 2
        bm, cm = bms[grp], cms[grp]
        h0, h1 = 2 * j, 2 * j + 1
        cb = _bdot(cm, bm, 1, 1)
        xdt = xs[j] * jnp.where(left, _col(dt, h0), _col(dt, h1))
        acs0, acs1 = _col(a_cs, h0), _col(a_cs, h1)
        y = _bdot(cm, states[j], 1, 0) * jnp.where(left, jnp.exp(acs0), jnp.exp(acs1))
        s_new = states[j] * jnp.where(left1, jnp.exp(_col(a_end, h0)), jnp.exp(_col(a_end, h1)))
        for h, acs, xh in ((h0, acs0, jnp.where(left, xdt, 0.0)), (h1, acs1, jnp.where(left, 0.0, xdt))):
            decay = jnp.exp(jnp.where(causal, acs - _row(a_cs_t, h), -jnp.inf))
            y = y + _bdot(cb * decay, xh, 1, 0)
            s_new = s_new + _bdot(bm * jnp.exp(_col(a_end, h) - acs), xh, 0, 0)
        ys.append((y + dfs[j] * xs[j]) * _silu(zs[j]))
        nexts.append(s_new)
    ssq = sum(jnp.sum(y * y, axis=-1, keepdims=True) for y in ys)
    scale = lax.rsqrt(ssq / BW + EPS)
    return [y * scale * nw for y, nw in zip(ys, nws)], nexts


def _blocks(ref, n, width=LANES):
    return [ref[:, j * width:(j + 1) * width] for j in range(n)]


def _ssd_fwd(proj, xa, dt_bias, a_log, dfull, nw, name):
    nc = SEQ // M2_CHUNK

    def body(x_ref, b_ref, c_ref, dt_ref, z_ref, dtb_ref, al_ref, df_ref, nw_ref, o_ref, sin_ref, st):
        @pl.when(pl.program_id(0) == 0)
        def _():
            st[...] = jnp.zeros_like(st)

        states = [st[j] for j in range(N_PAIR)]
        for j in range(N_PAIR):
            sin_ref[0, j] = states[j]
        ys, nexts = _ssd_chunk(_blocks(x_ref, 4), _blocks(b_ref, 2), _blocks(c_ref, 2), dt_ref[...], _blocks(z_ref, 4),
                               states, dtb_ref[...], al_ref[...], _blocks(df_ref, 4), _blocks(nw_ref, 4))
        for j in range(N_PAIR):
            o_ref[:, j * LANES:(j + 1) * LANES] = ys[j].astype(BF16)
            st[j] = nexts[j]

    vec8 = pl.BlockSpec((1, LANES), lambda c: (0, 0))
    vec = pl.BlockSpec((1, BW), lambda c: (0, 0))
    return pl.pallas_call(
        body, name=name, grid=(nc,),
        in_specs=[pl.BlockSpec((M2_CHUNK, BW), lambda c: (c, 0)), pl.BlockSpec((M2_CHUNK, 256), lambda c: (c, 2)),
                  pl.BlockSpec((M2_CHUNK, 256), lambda c: (c, 3)), pl.BlockSpec((M2_CHUNK, LANES), lambda c: (c, C_DT // LANES)),
                  pl.BlockSpec((M2_CHUNK, BW), lambda c: (c, C_M2Z // BW)), vec8, vec8, vec, vec],
        out_specs=[pl.BlockSpec((M2_CHUNK, BW), lambda c: (c, 0)),
                   pl.BlockSpec((1, N_PAIR, M2_STATE, LANES), lambda c: (c, 0, 0, 0))],
        out_shape=[jax.ShapeDtypeStruct((SEQ, BW), BF16), jax.ShapeDtypeStruct((nc, N_PAIR, M2_STATE, LANES), F32)],
        scratch_shapes=[pltpu.VMEM((N_PAIR, M2_STATE, LANES), F32)],
        compiler_params=_cparams(("arbitrary",)),
    )(xa, xa, xa, proj, proj, dt_bias, a_log, dfull, nw)


def _ssd_bwd(proj, dproj, xa, dout, s_in, dt_bias, a_log, dfull, nw, name):
    nc = SEQ // M2_CHUNK

    def body(x_ref, b_ref, c_ref, dt_ref, z_ref, do_ref, sin_ref, dtb_ref, al_ref, df_ref, nw_ref, dproj_in,
             dp_ref, dxa_ref, ddtb_ref, dal_ref, ddf_ref, dnw_ref, dst):
        @pl.when(pl.program_id(0) == 0)
        def _():
            dst[...] = jnp.zeros_like(dst)
            for r in (ddtb_ref, dal_ref, ddf_ref, dnw_ref):
                r[...] = jnp.zeros_like(r)

        states = [sin_ref[0, j] for j in range(N_PAIR)]
        _, vjp = jax.vjp(_ssd_chunk, _blocks(x_ref, 4), _blocks(b_ref, 2), _blocks(c_ref, 2), dt_ref[...],
                         _blocks(z_ref, 4), states, dtb_ref[...], al_ref[...], _blocks(df_ref, 4), _blocks(nw_ref, 4))
        dxs, dbs, dcs, ddt, dzs, dstates, ddtb, dal, ddfs, dnws = vjp(
            ([do_ref[0, :, j * LANES:(j + 1) * LANES] for j in range(N_PAIR)], [dst[j] for j in range(N_PAIR)]))
        for j in range(N_PAIR):
            sl = slice(j * LANES, (j + 1) * LANES)
            dxa_ref[:, sl] = dxs[j]
            dp_ref[:, sl] = dzs[j]
            dst[j] = dstates[j]
            ddf_ref[:, sl] += ddfs[j]
            dnw_ref[:, sl] += dnws[j]
        for g in range(2):
            dxa_ref[:, BW + g * LANES:BW + (g + 1) * LANES] = dbs[g]
            dxa_ref[:, BW + 256 + g * LANES:BW + 256 + (g + 1) * LANES] = dcs[g]
        dp_ref[:, BW:BW + LANES] = ddt
        dp_ref[:, BW + LANES:] = jnp.zeros((M2_CHUNK, 2 * BW - BW - LANES), F32)
        ddtb_ref[...] += ddtb
        dal_ref[...] += dal

    rev = lambda w, col=0: pl.BlockSpec((M2_CHUNK, w), lambda i: (nc - 1 - i, col))
    vec8 = pl.BlockSpec((1, LANES), lambda i: (0, 0))
    vec = pl.BlockSpec((1, BW), lambda i: (0, 0))
    return pl.pallas_call(
        body, name=name, grid=(nc,),
        in_specs=[rev(BW), rev(256, 2), rev(256, 3), rev(LANES, C_DT // LANES), rev(BW, C_M2Z // BW),
                  pl.BlockSpec((1, M2_CHUNK, BW), lambda i: (2, nc - 1 - i, 0)),
                  pl.BlockSpec((1, N_PAIR, M2_STATE, LANES), lambda i: (nc - 1 - i, 0, 0, 0)), vec8, vec8, vec, vec,
                  pl.BlockSpec(memory_space=pl.ANY)],
        out_specs=[rev(2 * BW, C_M2Z // (2 * BW)), rev(M2_CONV_CH), vec8, vec8, vec, vec],
        input_output_aliases={11: 0},
        out_shape=[jax.ShapeDtypeStruct((SEQ, IN_PAD), F32), jax.ShapeDtypeStruct((SEQ, M2_CONV_CH), F32),
                   jax.ShapeDtypeStruct((1, LANES), F32), jax.ShapeDtypeStruct((1, LANES), F32),
                   jax.ShapeDtypeStruct((1, BW), F32), jax.ShapeDtypeStruct((1, BW), F32)],
        scratch_shapes=[pltpu.VMEM((N_PAIR, M2_STATE, LANES), F32)],
        compiler_params=_cparams(("arbitrary",)),
    )(xa, xa, xa, proj, proj, dout, s_in, dt_bias, a_log, dfull, nw, dproj)


def _sc_specs():
    col = lambda kind: pl.BlockSpec((SEQ, LANES), lambda j: (0, C_SC // LANES + 4 * j + kind))
    return [col(0), col(1), col(2), col(3)]


def _sc_fwd(proj, w, name):
    def body(b_ref, c_ref, h_ref, g_ref, w_ref, o_ref):
        ch = c_ref[...] * h_ref[...]
        acc = jnp.zeros_like(ch)
        for k in range(SC_CONV):
            acc = acc + w_ref[k:k + 1, :] * _shift_down(ch, SC_CONV - 1 - k)
        o_ref[...] = (b_ref[...] * acc * _silu(g_ref[...])).astype(BF16)

    return pl.pallas_call(
        body, name=name, grid=(BW // LANES,),
        in_specs=_sc_specs() + [pl.BlockSpec((SC_CONV, LANES), lambda j: (0, j))],
        out_specs=pl.BlockSpec((SEQ, LANES), lambda j: (0, j)),
        out_shape=jax.ShapeDtypeStruct((SEQ, BW), BF16),
        compiler_params=_cparams(("parallel",)),
    )(proj, proj, proj, proj, w)


def _sc_bwd(proj, dproj, dout, w, name):
    def body(b_ref, c_ref, h_ref, g_ref, do_ref, w_ref, dproj_in, dp_ref, dw_ref):
        cv, hv, gv = c_ref[...], h_ref[...], g_ref[...]
        ch = cv * hv
        chs = [_shift_down(ch, SC_CONV - 1 - k) for k in range(SC_CONV)]
        acc = jnp.zeros_like(ch)
        for k in range(SC_CONV):
            acc = acc + w_ref[k:k + 1, :] * chs[k]
        sg = jax.nn.sigmoid(gv)
        do = do_ref[0]
        bv = b_ref[...]
        dp_ref[:, 0:LANES] = do * acc * (gv * sg)
        dp_ref[:, 3 * LANES:] = do * bv * acc * (sg * (1.0 + gv * (1.0 - sg)))
        dacc = do * bv * (gv * sg)
        dch = jnp.zeros_like(ch)
        for k in range(SC_CONV):
            dch = dch + w_ref[k:k + 1, :] * _shift_up(dacc, SC_CONV - 1 - k)
            dw_ref[k:k + 1, :] = jnp.sum(dacc * chs[k], axis=0, keepdims=True)
        dp_ref[:, LANES:2 * LANES] = dch * hv
        dp_ref[:, 2 * LANES:3 * LANES] = dch * cv

    wsp = pl.BlockSpec((SC_CONV, LANES), lambda j: (0, j))
    return pl.pallas_call(
        body, name=name, grid=(BW // LANES,),
        in_specs=_sc_specs() + [pl.BlockSpec((1, SEQ, LANES), lambda j: (3, 0, j)), wsp, pl.BlockSpec(memory_space=pl.ANY)],
        out_specs=[pl.BlockSpec((SEQ, 4 * LANES), lambda j: (0, C_SC // (4 * LANES) + j)), wsp],
        input_output_aliases={6: 0},
        out_shape=[jax.ShapeDtypeStruct((SEQ, IN_PAD), F32), jax.ShapeDtypeStruct((SC_CONV, BW), F32)],
        compiler_params=_cparams(("parallel",)),
    )(proj, proj, proj, proj, dout, w, dproj)


MERGE_T = 256
MERGE_BWD_T = 512


def _merge_fwd(proj, ys, merge_b, w_branch, name):
    def body(y_ref, lg_ref, b_ref, w_ref, o_ref):
        acc = jnp.zeros((MERGE_T, D_MODEL), F32)
        for k in range(N_BRANCH):
            gate = jax.nn.sigmoid(lg_ref[:, k * D_MODEL:(k + 1) * D_MODEL] + b_ref[k])
            acc = acc + gate * _dg(y_ref[k], w_ref[k], 1, 0)
        o_ref[...] = acc.astype(BF16)

    return pl.pallas_call(
        body, name=name, grid=(SEQ // MERGE_T,),
        in_specs=[pl.BlockSpec((N_BRANCH, MERGE_T, BW), lambda i: (0, i, 0)),
                  pl.BlockSpec((MERGE_T, N_BRANCH * D_MODEL), lambda i: (i, C_MERGE // (N_BRANCH * D_MODEL))),
                  pl.BlockSpec((N_BRANCH, 1, D_MODEL), lambda i: (0, 0, 0)),
                  pl.BlockSpec((N_BRANCH, BW, D_MODEL), lambda i: (0, 0, 0))],
        out_specs=pl.BlockSpec((MERGE_T, D_MODEL), lambda i: (i, 0)),
        out_shape=jax.ShapeDtypeStruct((SEQ, D_MODEL), BF16),
        compiler_params=_cparams(("parallel",)),
    )(ys, proj, merge_b, w_branch)


def _merge_bwd(proj, ys, dm, merge_b, w_branch, name):
    nt = SEQ // MERGE_BWD_T

    def body(y_ref, lg_ref, dm_ref, b_ref, w_ref, dy_ref, dlg_ref, dw_ref, db_ref, dw_acc):
        i = pl.program_id(1)
        gate = jax.nn.sigmoid(lg_ref[...] + b_ref[0])
        y = y_ref[0]
        dmv = dm_ref[...]
        dbo = (gate * dmv).astype(BF16)
        dlg = _dg(y, w_ref[0], 1, 0) * dmv * gate * (1.0 - gate)
        dlg_ref[...] = dlg
        dy_ref[0] = _dg(dbo, w_ref[0], 1, 1)
        dwp = _dg(y, dbo, 0, 0)
        dbp = jnp.sum(dlg, axis=0, keepdims=True)

        @pl.when(i == 0)
        def _():
            dw_acc[...] = dwp
            db_ref[0] = dbp

        @pl.when(i > 0)
        def _():
            dw_acc[...] += dwp
            db_ref[0] += dbp

        @pl.when(i == nt - 1)
        def _():
            dw_ref[0] = dw_acc[...].astype(BF16)

    return pl.pallas_call(
        body, name=name, grid=(N_BRANCH, nt),
        in_specs=[pl.BlockSpec((1, MERGE_BWD_T, BW), lambda k, i: (k, i, 0)),
                  pl.BlockSpec((MERGE_BWD_T, D_MODEL), lambda k, i: (i, C_MERGE // D_MODEL + k)),
                  pl.BlockSpec((MERGE_BWD_T, D_MODEL), lambda k, i: (i, 0)),
                  pl.BlockSpec((1, 1, D_MODEL), lambda k, i: (k, 0, 0)),
                  pl.BlockSpec((1, BW, D_MODEL), lambda k, i: (k, 0, 0))],
        out_specs=[pl.BlockSpec((1, MERGE_BWD_T, BW), lambda k, i: (k, i, 0)),
                   pl.BlockSpec((MERGE_BWD_T, D_MODEL), lambda k, i: (i, k)),
                   pl.BlockSpec((1, BW, D_MODEL), lambda k, i: (k, 0, 0)),
                   pl.BlockSpec((1, 1, D_MODEL), lambda k, i: (k, 0, 0))],
        out_shape=[jax.ShapeDtypeStruct((N_BRANCH, SEQ, BW), F32), jax.ShapeDtypeStruct((SEQ, IN_PAD), F32),
                   jax.ShapeDtypeStruct((N_BRANCH, BW, D_MODEL), BF16), jax.ShapeDtypeStruct((N_BRANCH, 1, D_MODEL), F32)],
        scratch_shapes=[pltpu.VMEM((BW, D_MODEL), F32)],
        compiler_params=_cparams(("parallel", "arbitrary")),
    )(ys, proj, dm, merge_b, w_branch)


def _adamw(glist, w, m, v, rows, name):
    nl = len(glist)
    n, r, c = glist[0].shape
    assert w.shape == (nl, r, c) and r % rows == 0
    nb = r // rows

    def body(*refs):
        g_refs = refs[:nl]
        w_ref, m_ref, v_ref, go_ref, d_ref, mo_ref, vo_ref = refs[nl:]
        for layer in range(nl):
            @pl.when(pl.program_id(0) == layer)
            def _(g_ref=g_refs[layer]):
                g = g_ref[0].astype(F32)
                for s in range(1, n):
                    g = g + g_ref[s].astype(F32)
                mn = ADAM_B1 * m_ref[0] + (1.0 - ADAM_B1) * g
                vn = ADAM_B2 * v_ref[0] + (1.0 - ADAM_B2) * jnp.square(g)
                m_hat = mn / (1.0 - ADAM_B1 ** ADAM_STEP)
                v_hat = vn / (1.0 - ADAM_B2 ** ADAM_STEP)
                go_ref[0] = g
                d_ref[0] = -ADAM_LR * (m_hat / (jnp.sqrt(v_hat) + ADAM_EPS) + ADAM_WD * w_ref[0])
                mo_ref[0] = mn
                vo_ref[0] = vn

    def g_spec(layer):
        return pl.BlockSpec((n, rows, c), lambda a, i: (0, jnp.where(a < layer, 0, jnp.where(a == layer, i, nb - 1)), 0))

    blk = pl.BlockSpec((1, rows, c), lambda a, i: (a, i, 0))
    out = jax.ShapeDtypeStruct((nl, r, c), F32)
    return pl.pallas_call(
        body, name=name, grid=(nl, nb),
        in_specs=[g_spec(layer) for layer in range(nl)] + [blk, blk, blk],
        out_specs=[blk, blk, blk, blk], out_shape=[out, out, out, out],
        compiler_params=_cparams(("arbitrary", "arbitrary")),
    )(*glist, w, m, v)


def _slot_sum(gslots, name):
    n, r, c = gslots.shape

    def body(g_ref, o_ref):
        g = g_ref[0]
        for s in range(1, n):
            g = g + g_ref[s]
        o_ref[...] = g

    return pl.pallas_call(
        body, name=name, in_specs=[pl.BlockSpec((n, r, c), lambda: (0, 0, 0))],
        out_specs=pl.BlockSpec((r, c), lambda: (0, 0)), out_shape=jax.ShapeDtypeStruct((r, c), F32),
        compiler_params=_cparams(None),
    )(gslots)


def _me_and_peers():
    x, y, c = lax.axis_index("x"), lax.axis_index("y"), lax.axis_index("c")
    me = 4 * x + 2 * y + c
    peers = []
    for k in range(1, N_DEV):
        px = 1 - x if (k >> 2) & 1 else x
        py = 1 - y if (k >> 1) & 1 else y
        pc = 1 - c if k & 1 else c
        peers.append((4 * px + 2 * py + pc, (px, py, pc)))
    return me, peers


def _exchange(tensors, gather, name):
    n = len(tensors)

    def body(*refs):
        ins, outs = refs[:n], refs[n:2 * n]
        send_sems, recv_sems, local_sems = refs[2 * n:]
        me, peers = _me_and_peers()
        started = []
        for t in range(n):
            own = pltpu.make_async_copy(ins[t] if gather else ins[t].at[me], outs[t].at[me], local_sems.at[t])
            own.start()
            started.append(own)
            for k, (pidx, pos) in enumerate(peers):
                cp = pltpu.make_async_remote_copy(
                    src_ref=ins[t] if gather else ins[t].at[pidx], dst_ref=outs[t].at[me],
                    send_sem=send_sems.at[t, k], recv_sem=recv_sems.at[t, k], device_id=pos, device_id_type=MESH)
                cp.start()
                started.append(cp)
        for cp in started:
            cp.wait()

    any_spec = pl.BlockSpec(memory_space=pl.ANY)
    outs = pl.pallas_call(
        body, name=name, in_specs=[any_spec] * n, out_specs=[any_spec] * n,
        out_shape=[jax.ShapeDtypeStruct(((N_DEV,) + t.shape) if gather else t.shape, t.dtype) for t in tensors],
        scratch_shapes=[pltpu.SemaphoreType.DMA((n, N_DEV - 1)), pltpu.SemaphoreType.DMA((n, N_DEV - 1)),
                        pltpu.SemaphoreType.DMA((n,))],
        compiler_params=pltpu.CompilerParams(has_side_effects=True),
    )(*tensors)
    return list(outs)


_HBM = pl.BlockSpec(memory_space=pltpu.HBM)
_SEM = pl.BlockSpec(memory_space=pltpu.SEMAPHORE)
_EFFECT = pltpu.SideEffectType.DATAFLOW_SIDE_EFFECTING


def _xchg_copies(ins, lands, send_sems, recv_sems, local_sems, gather):
    me, peers = _me_and_peers()
    local, remote = [], []
    for t in range(len(ins)):
        local.append(pltpu.make_async_copy(ins[t] if gather else ins[t].at[me], lands[t].at[me], local_sems.at[t]))
        for k, (pidx, pos) in enumerate(peers):
            remote.append(pltpu.make_async_remote_copy(
                src_ref=ins[t] if gather else ins[t].at[pidx], dst_ref=lands[t].at[me],
                send_sem=send_sems.at[t * (N_DEV - 1) + k], recv_sem=recv_sems.at[t * (N_DEV - 1) + k],
                device_id=pos, device_id_type=MESH))
    return local, remote


def _xchg_start(tensors, gather, name):
    n = len(tensors)
    land_shapes = [((N_DEV,) + t.shape) if gather else t.shape for t in tensors]

    def body(*refs):
        ins, lands = refs[:n], refs[n:2 * n]
        send_sems, recv_sems, local_sems = refs[2 * n:2 * n + 3]
        token = refs[-1]
        local, remote = _xchg_copies(ins, lands, send_sems, recv_sems, local_sems, gather)
        for cp in local + remote:
            cp.start()
        token[...] = jnp.zeros_like(token)

    outs = pl.pallas_call(
        body, name=name,
        out_shape=(pltpu.SemaphoreType.DMA((n * (N_DEV - 1),)), pltpu.SemaphoreType.DMA((n * (N_DEV - 1),)),
                   pltpu.SemaphoreType.DMA((n,)),
                   *[pltpu.HBM(t.shape, t.dtype) for t in tensors],
                   *[pltpu.HBM(s, t.dtype) for s, t in zip(land_shapes, tensors)],
                   jax.ShapeDtypeStruct((8, LANES), F32)),
        in_specs=[_HBM] * (2 * n),
        out_specs=(_SEM, _SEM, _SEM, *[_HBM] * (2 * n), pl.BlockSpec(memory_space=pltpu.VMEM)),
        input_output_aliases={t: 3 + t for t in range(2 * n)},
        compiler_params=pltpu.CompilerParams(has_side_effects=_EFFECT),
    )(*[pltpu.with_memory_space_constraint(t, pltpu.HBM) for t in tensors],
      *[pltpu.with_memory_space_constraint(lax.empty(s, t.dtype), pltpu.HBM) for s, t in zip(land_shapes, tensors)])
    return outs[:-1], outs[-1]


def _xchg_wait(state, gather, after, name):
    send_sems, recv_sems, local_sems = state[:3]
    n = (len(state) - 3) // 2
    thru = state[3:]

    def body(*refs):
        ins, lands = refs[:n], refs[n:2 * n]
        s_sems, r_sems, l_sems = refs[2 * n:2 * n + 3]
        local, remote = _xchg_copies(ins, lands, s_sems, r_sems, l_sems, gather)
        for cp in local:
            cp.wait()
        for cp in remote:
            cp.wait_send()
            cp.wait_recv()

    outs = pl.pallas_call(
        body, name=name,
        out_shape=tuple(pltpu.HBM(t.shape, t.dtype) for t in thru),
        in_specs=[_HBM] * (2 * n) + [_SEM, _SEM, _SEM, pl.BlockSpec(memory_space=pl.ANY)],
        out_specs=tuple([_HBM] * (2 * n)),
        input_output_aliases={t: t for t in range(2 * n)},
        compiler_params=pltpu.CompilerParams(has_side_effects=_EFFECT),
    )(*thru, send_sems, recv_sems, local_sems, after)
    return list(outs[n:])


WEIGHTS = ['norm_w', 'w_in', 's5_lambda_re', 's5_lambda_im', 's5_b_re', 's5_b_im', 's5_c_re', 's5_c_im', 's5_d',
           's5_log_step', 's5_w_glu', 'sgu_ln_w', 'sgu_ln_b', 'sgu_w', 'sgu_b', 'm2_conv_w', 'm2_conv_b', 'm2_dt_bias',
           'm2_a_log', 'm2_d', 'm2_norm_w', 'sc_conv_w', 'merge_b', 'w_branch', 'w_out', 'final_norm_w']
BIG_SHARDED = ['w_in', 'w_branch', 'w_out', 's5_w_glu']
SMALL_SHARDED = ['m2_conv_w', 'sc_conv_w', 'merge_b']
REPLICATED = [n for n in WEIGHTS if n not in BIG_SHARDED + SMALL_SHARDED]
S5_NAMES = ['s5_lambda_re', 's5_lambda_im', 's5_b_re', 's5_b_im', 's5_c_re', 's5_c_im', 's5_d', 's5_log_step']


def _sc_interleave(t):
    lead = t.shape[:-1]
    return jnp.swapaxes(t.reshape(lead + (4, 4, LANES)), -3, -2).reshape(lead + (4 * BW,))


def _pad_in(w):
    z = lambda n: jnp.zeros(w.shape[:-1] + (n,), w.dtype)
    return jnp.concatenate([w[..., 6152:], w[..., 0:1024], w[..., 3072:4096], w[..., 1024:2560], z(512),
                            w[..., 2560:3072], w[..., 4096:4104], z(504), _sc_interleave(w[..., 4104:6152])], axis=-1)


def _unpad_in(g):
    return jnp.concatenate([g[..., C_S5U:C_S5U + 1024], g[..., C_SGU_U:C_SGU_U + 1536], g[..., C_M2Z:C_M2Z + 512],
                            g[..., C_M2X:C_M2X + 1024], g[..., C_DT:C_DT + 8], _sc_interleave(g[..., C_SC:]),
                            g[..., :N_BRANCH * D_MODEL]], axis=-1)


ROW_BLOCK = 8 * LANES


def _pack_rows(tensors, row_mult, batched=False):
    parts = []
    for t in tensors:
        f = t.reshape((t.shape[0], -1) if batched else (1, -1))
        f = jnp.pad(f, ((0, 0), (0, (-f.shape[1]) % ROW_BLOCK)))
        parts.append(f.reshape(f.shape[0], -1, LANES))
    out = jnp.concatenate(parts, axis=1)
    out = jnp.pad(out, ((0, 0), (0, (-out.shape[1]) % row_mult), (0, 0)))
    return out if batched else out[0]


def _unpack_rows(rows, shapes):
    out, r0 = [], 0
    for shp in shapes:
        size = 1
        for s in shp:
            size *= s
        nr = -(-size // ROW_BLOCK) * 8
        out.append(rows[r0:r0 + nr].reshape(-1)[:size].reshape(shp))
        r0 += nr
    return out


def _rows128(flat, row_mult=8):
    n = flat.shape[0]
    per = LANES * row_mult
    total = -(-n // per) * per
    return jnp.pad(flat, (0, total - n)).reshape(total // LANES, LANES)


def _pad_lanes(v):
    return jnp.pad(v, (0, LANES - v.shape[0])).reshape(1, LANES)


def _layer_prep(i, p):
    disc, disc_vjp = jax.vjp(_s5_disc, *[p[n][i] for n in S5_NAMES])
    prep = dict(
        nw=p['norm_w'][i].reshape(1, D_MODEL), disc_vjp=disc_vjp,
        s5small=[t.astype(BF16) for t in disc[:4]] + [disc[4], disc[5]],
        sgw=[p['sgu_ln_w'][i].reshape(1, BW), p['sgu_ln_b'][i].reshape(1, BW), p['sgu_w'][i],
             jnp.repeat(p['sgu_b'][i].T, BW // SGU_HEADS, axis=1)],
        cb=p['m2_conv_b'][i].reshape(1, M2_CONV_CH),
        m2w=[_pad_lanes(p['m2_dt_bias'][i]), _pad_lanes(p['m2_a_log'][i]),
             jnp.repeat(p['m2_d'][i], M2_HEAD_DIM).reshape(1, BW), p['m2_norm_w'][i].reshape(1, BW)])
    touch = [t[0, 0].astype(F32) for t in prep['s5small']] + [prep['sgw'][3][0, 0], prep['m2w'][2][0, 0]]
    return prep, sum(touch[1:], touch[0])


def _layer_fwd(x, h, i, prep, full):
    proj = _matmul(h, full['w_in'], 1, 0, F32, 1024, 1024, 1024, f"proj{i}")
    s5w = prep['s5small'] + [full['s5_w_glu']]
    ya, sre, sim = _s5_fwd(proj, *s5w, f"s5_fwd{i}")
    yb = _sgu_fwd(proj, *prep['sgw'], f"sgu_fwd{i}")
    cw = full['m2_conv_w']
    xa = _m2_conv_fwd(proj, cw, prep['cb'], f"m2conv_fwd{i}")
    yc, s_in = _ssd_fwd(proj, xa, *prep['m2w'], f"ssd_fwd{i}")
    scw = full['sc_conv_w']
    yd = _sc_fwd(proj, scw, f"sc_fwd{i}")
    ys = jnp.stack([ya, yb, yc, yd])
    mb = full['merge_b'].reshape(N_BRANCH, 1, D_MODEL)
    merged = _merge_fwd(proj, ys, mb, full['w_branch'], f"merge_fwd{i}")
    x_new = _matmul(merged, full['w_out'], 1, 0, F32, 1024, 1024, 1024, f"out{i}", residual=x)
    saved = dict(x=x, nw=prep['nw'], h=h, proj=proj, disc_vjp=prep['disc_vjp'], s5w=s5w, sre=sre, sim=sim, sgw=prep['sgw'],
                 cw=cw, cb=prep['cb'], xa=xa, m2w=prep['m2w'], s_in=s_in, scw=scw, ys=ys, mb=mb, merged=merged)
    return x_new, saved


def _layer_bwd(dx_out, i, sv, full, on_large_grads=None):
    g = {}
    proj = sv['proj']
    dm = _matmul(dx_out, full['w_out'], 1, 1, F32, 1024, 1024, 1024, f"dmerged{i}")
    g['w_out'] = _matmul(sv['merged'], dx_out, 0, 0, BF16, 1024, 1024, 1024, f"gw_out{i}")
    dys, dproj, g['w_branch'], dmb = _merge_bwd(proj, sv['ys'], dm, sv['mb'], full['w_branch'], f"merge_bwd{i}")
    g['merge_b'] = dmb.reshape(N_BRANCH, D_MODEL)
    dproj, dbbre, dbbim, dcre, dcim, da, dd, dwg = _s5_bwd(proj, dproj, dys, sv['sre'], sv['sim'], *sv['s5w'], f"s5_bwd{i}")
    for n, t in zip(S5_NAMES, sv['disc_vjp']((dbbre, dbbim, dcre, dcim, da, dd))):
        g[n] = t
    g['s5_w_glu'] = dwg.astype(BF16)
    dproj, dlw, dlb, g['sgu_w'], dbias = _sgu_bwd(proj, dproj, dys, *sv['sgw'], f"sgu_bwd{i}")
    g['sgu_ln_w'], g['sgu_ln_b'] = dlw[0], dlb[0]
    g['sgu_b'] = dbias.reshape(SGU_CHUNK, SGU_HEADS, BW // SGU_HEADS).sum(-1).T
    dproj, dxa, ddtb, dal, ddf, dnw = _ssd_bwd(proj, dproj, sv['xa'], dys, sv['s_in'], *sv['m2w'], f"ssd_bwd{i}")
    dproj, g['m2_conv_w'], dcb = _m2_conv_bwd(proj, dproj, dxa, sv['cw'], sv['cb'], f"m2conv_bwd{i}")
    g['m2_conv_b'], g['m2_norm_w'] = dcb[0], dnw[0]
    g['m2_dt_bias'], g['m2_a_log'] = ddtb[0, :M2_HEADS], dal[0, :M2_HEADS]
    g['m2_d'] = ddf.reshape(M2_HEADS, M2_HEAD_DIM).sum(-1)
    dproj, g['sc_conv_w'] = _sc_bwd(proj, dproj, dys, sv['scw'], f"sc_bwd{i}")
    g['w_in'] = _matmul(sv['h'], dproj, 0, 0, BF16, 1024, 1024, 1024, f"gw_in{i}")
    tok = on_large_grads(g) if on_large_grads else None
    dh = _matmul(dproj, full['w_in'], 1, 1, F32, 1024, 1024, 1024, f"dh{i}", after=tok)
    dx_in, dnw_l = _rmsnorm_bwd(sv['x'], sv['nw'], dh, dx_out, f"rms_bwd{i}")
    g['norm_w'] = dnw_l[0]
    return dx_in, g


def _split8(t, axis):
    shp = t.shape
    t = t.reshape(shp[:axis] + (N_DEV, shp[axis] // N_DEV) + shp[axis + 1:])
    return jnp.moveaxis(t, axis, 0)


def _join8(t, axis):
    t = jnp.moveaxis(t, 0, axis)
    shp = t.shape
    return t.reshape(shp[:axis] + (shp[axis] * shp[axis + 1],) + shp[axis + 2:])


SHARD_AXIS = {'w_in': 2, 'w_branch': 3, 'w_out': 1, 's5_w_glu': 1, 'm2_conv_w': 2, 'sc_conv_w': 2, 'merge_b': 2}


def _layer_weights(gathered):
    full = {n: _join8(t, SHARD_AXIS[n] - 1) for n, t in zip(BIG_SHARDED, gathered)}
    full['w_in'] = _pad_in(full['w_in'])
    return full


def _layer_grad_blocks(g):
    return [_split8(_unpad_in(g[n]) if n == 'w_in' else g[n], SHARD_AXIS[n] - 1) for n in BIG_SHARDED]


def kernel(x, norm_w, w_in, s5_lambda_re, s5_lambda_im, s5_b_re, s5_b_im, s5_c_re, s5_c_im, s5_d, s5_log_step, s5_w_glu, sgu_ln_w, sgu_ln_b, sgu_w, sgu_b, m2_conv_w, m2_conv_b, m2_dt_bias, m2_a_log, m2_d, m2_norm_w, sc_conv_w, merge_b, w_branch, w_out, final_norm_w, loss_target, m_norm_w, m_w_in, m_s5_lambda_re, m_s5_lambda_im, m_s5_b_re, m_s5_b_im, m_s5_c_re, m_s5_c_im, m_s5_d, m_s5_log_step, m_s5_w_glu, m_sgu_ln_w, m_sgu_ln_b, m_sgu_w, m_sgu_b, m_m2_conv_w, m_m2_conv_b, m_m2_dt_bias, m_m2_a_log, m_m2_d, m_m2_norm_w, m_sc_conv_w, m_merge_b, m_w_branch, m_w_out, m_final_norm_w, v_norm_w, v_w_in, v_s5_lambda_re, v_s5_lambda_im, v_s5_b_re, v_s5_b_im, v_s5_c_re, v_s5_c_im, v_s5_d, v_s5_log_step, v_s5_w_glu, v_sgu_ln_w, v_sgu_ln_b, v_sgu_w, v_sgu_b, v_m2_conv_w, v_m2_conv_b, v_m2_dt_bias, v_m2_a_log, v_m2_d, v_m2_norm_w, v_sc_conv_w, v_merge_b, v_w_branch, v_w_out, v_final_norm_w):
    loc = locals()
    p = {n: loc[n] for n in WEIGHTS}
    mom = {n: loc['m_' + n] for n in WEIGHTS}
    vel = {n: loc['v_' + n] for n in WEIGHTS}

    small_sizes = [p[n].size for n in SMALL_SHARDED]
    small_pack = _rows128(jnp.concatenate([p[n].reshape(-1) for n in SMALL_SHARDED]))
    gath0, tok = _xchg_start([p[n][0].astype(BF16) for n in BIG_SHARDED] + [small_pack], True, "gather0_start")
    gath1, tok = _xchg_start([(p[n][1] + tok[0, 0] if n == 's5_w_glu' else p[n][1]).astype(BF16) for n in BIG_SHARDED],
                             True, "gather1_start")
    later = dict(p, **{n: p[n] + tok[0, 0] for n in ('norm_w', 's5_log_step', 'sgu_b', 'm2_d')})
    preps = [_layer_prep(i, later) for i in range(DEPTH)]
    h0 = _rmsnorm_fwd(x[0], preps[0][0]['nw'], "rms_fwd0")
    got0 = _xchg_wait(gath0, True, tok + (preps[0][1] + preps[1][1] + h0[0, 0].astype(F32)), "gather0_wait")
    small_all = got0[-1].reshape(N_DEV, -1)
    small_full, off = {}, 0
    for n, sz in zip(SMALL_SHARDED, small_sizes):
        small_full[n] = _join8(small_all[:, off:off + sz].reshape((N_DEV,) + p[n].shape), SHARD_AXIS[n])
        off += sz
    full = [dict(_layer_weights(got0[:-1]), **{n: small_full[n][0] for n in SMALL_SHARDED})]

    saved, layer_g = [None] * DEPTH, [None] * DEPTH
    xs, saved[0] = _layer_fwd(x[0], h0, 0, preps[0][0], full[0])
    h1 = _rmsnorm_fwd(xs, preps[1][0]['nw'], "rms_fwd1")
    got1 = _xchg_wait(gath1, True, h1, "gather1_wait")
    full.append(dict(_layer_weights(got1), **{n: small_full[n][1] for n in SMALL_SHARDED}))
    xs, saved[1] = _layer_fwd(xs, h1, 1, preps[1][0], full[1])
    loss_row, dx, dfw = _loss_head(xs, final_norm_w.reshape(1, D_MODEL), loss_target[0])
    loss = lax.psum(loss_row[0, 0], ("x", "y", "c"))
    scat = [None] * DEPTH

    def start_scatter(i):
        def start(g):
            scat[i], tok = _xchg_start(_layer_grad_blocks(g), False, f"scatter{i}_start")
            return tok
        return start

    dx, layer_g[1] = _layer_bwd(dx, 1, saved[1], full[1], start_scatter(1))
    dx, layer_g[0] = _layer_bwd(dx, 0, saved[0], full[0], start_scatter(0))
    grads = {n: jnp.stack([layer_g[i][n] for i in range(DEPTH)]) for n in SMALL_SHARDED + REPLICATED if n != 'final_norm_w'}
    grads['final_norm_w'] = dfw[0]

    out_g, out_d, out_m, out_v = {}, {}, {}, {}
    repl_rows = _pack_rows([grads[n] for n in REPLICATED], 8 * N_DEV)
    rr = repl_rows.shape[0] // N_DEV
    shard_rows = _pack_rows([_split8(grads[n], SHARD_AXIS[n]) for n in SMALL_SHARDED], 8, batched=True)
    rs = shard_rows.shape[1]
    small_g = jnp.concatenate([shard_rows, repl_rows.reshape(N_DEV, rr, LANES)], axis=1)
    small_sum = _slot_sum(_exchange([small_g], False, "scatter_small")[0], "sum_small")
    repl_all = _exchange([small_sum[rs:]], True, "gather_small")[0].reshape(N_DEV * rr, LANES)
    g_all = jnp.concatenate([small_sum[:rs], repl_all], axis=0)
    names = SMALL_SHARDED + REPLICATED
    packed = [jnp.concatenate([_pack_rows([d[n] for n in SMALL_SHARDED], 8), _pack_rows([d[n] for n in REPLICATED], 8 * N_DEV)],
                              axis=0) for d in (p, mom, vel)]
    res = _adamw([g_all[None]], *[t[None] for t in packed], g_all.shape[0], "adamw_small")
    for o, dst in zip(res, (out_g, out_d, out_m, out_v)):
        pieces = (_unpack_rows(o[0, :rs], [p[n].shape for n in SMALL_SHARDED])
                  + _unpack_rows(o[0, rs:], [p[n].shape for n in REPLICATED]))
        dst.update(zip(names, pieces))

    landed1 = _xchg_wait(scat[1], False, res[0], "scatter1_wait")
    landed0 = _xchg_wait(scat[0], False, landed1[0], "scatter0_wait")
    for k, n in enumerate(BIG_SHARDED):
        shp = p[n].shape
        c = shp[-1]
        r = p[n].size // (DEPTH * c)
        big = _adamw([t.reshape(N_DEV, r, c) for t in (landed0[k], landed1[k])],
                     *[d[n].reshape(DEPTH, r, c) for d in (p, mom, vel)],
                     {'w_in': 256, 'w_branch': 512, 'w_out': 128, 's5_w_glu': 64}[n], "adamw_" + n)
        out_g[n], out_d[n], out_m[n], out_v[n] = [o.reshape(shp) for o in big]
    return (loss, dx[None], *[out_g[n] for n in WEIGHTS], *[out_d[n] for n in WEIGHTS],
            *[out_m[n] for n in WEIGHTS], *[out_v[n] for n in WEIGHTS])
```

```python
import functools

import jax
import jax.numpy as jnp
import numpy as np
from jax import lax
from jax.experimental import pallas as pl
from jax.experimental.pallas import tpu as pltpu

F32 = jnp.float32
BF16 = jnp.bfloat16

N_DEV = 8
SEQ = 2048
D_MODEL = 1024
DEPTH = 2
BW = 512
N_BRANCH = 4
EPS = 1e-6
S5_GROUPS, S5_STATE, S5_P = 32, 64, 16
S5_CH = S5_GROUPS * S5_STATE
SGU_CHUNK, SGU_HEADS = 128, 8
M2_HEADS, M2_HEAD_DIM, M2_STATE, M2_CHUNK, M2_CONV = 8, 64, 128, 128, 4
M2_CONV_CH = 1024
SC_CONV = 3
IN_DIM = 10248
IN_PAD = 11264
C_MERGE = 0
C_S5U, C_S5G = 4096, 4608
C_M2X = 5120
C_SGU_U, C_SGU_V, C_SGU_G = 6144, 6656, 7168
C_M2Z, C_DT = 8192, 8704
C_SC = 9216
SHARD_IN = IN_DIM // N_DEV

ADAM_LR, ADAM_B1, ADAM_B2, ADAM_EPS, ADAM_WD, ADAM_STEP = 0.001, 0.9, 0.999, 1e-08, 0.01, 10

VMEM_LIMIT = 56 * 1024 * 1024
LANES = 128

MESH = pl.DeviceIdType.MESH


def _cparams(sem=None, **kw):
    return pltpu.CompilerParams(dimension_semantics=sem, vmem_limit_bytes=VMEM_LIMIT, **kw)


def _dg(a, b, ca, cb, precision=None):
    return lax.dot_general(a, b, (((ca,), (cb,)), ((), ())), precision=precision,
                           preferred_element_type=F32)


@functools.partial(jax.custom_vjp, nondiff_argnums=(2, 3))
def _bdot(a, b, ca, cb):
    return _dg(a.astype(BF16), b.astype(BF16), ca, cb)


def _bdot_fwd(a, b, ca, cb):
    return _bdot(a, b, ca, cb), (a, b)


def _bdot_bwd(ca, cb, res, g):
    a, b = res
    gb, ab, bb = g.astype(BF16), a.astype(BF16), b.astype(BF16)
    da = _dg(gb, bb, 1, 1 - cb) if ca == 1 else _dg(bb, gb, 1 - cb, 1)
    db = _dg(ab, gb, 1 - ca, 0) if cb == 0 else _dg(gb, ab, 0, 1 - ca)
    return da.astype(a.dtype), db.astype(b.dtype)


_bdot.defvjp(_bdot_fwd, _bdot_bwd)


def _rms(x, w):
    return x * lax.rsqrt(jnp.mean(x * x, axis=-1, keepdims=True) + EPS) * w


def _silu(x):
    return x * jax.nn.sigmoid(x)


def _gelu(x):
    return 0.5 * x * (1.0 + jnp.tanh(0.7978845608028654 * (x + 0.044715 * (x * x * x))))


def _softplus(x):
    return jnp.maximum(x, 0.0) + jnp.log1p(jnp.exp(-jnp.abs(x)))


def _shift_down(x, s):
    if s == 0:
        return x
    row = lax.broadcasted_iota(jnp.int32, x.shape, 0)
    return jnp.where(row >= s, pltpu.roll(x, s, 0), 0.0)


def _shift_up(x, s):
    if s == 0:
        return x
    n = x.shape[0]
    row = lax.broadcasted_iota(jnp.int32, x.shape, 0)
    return jnp.where(row < n - s, pltpu.roll(x, n - s, 0), 0.0)


def _matmul(a, b, ca, cb, out_dtype, tm, tn, tk, name, residual=None, after=None):
    m = a.shape[1 - ca]
    k = a.shape[ca]
    n = b.shape[1 - cb]
    assert b.shape[cb] == k and m % tm == 0 and n % tn == 0 and k % tk == 0
    nk = k // tk
    a_spec = pl.BlockSpec((tm, tk), lambda i, j, kk: (i, kk)) if ca == 1 else pl.BlockSpec((tk, tm), lambda i, j, kk: (kk, i))
    b_spec = pl.BlockSpec((tk, tn), lambda i, j, kk: (kk, j)) if cb == 0 else pl.BlockSpec((tn, tk), lambda i, j, kk: (j, kk))
    o_spec = pl.BlockSpec((tm, tn), lambda i, j, kk: (i, j))
    has_res = residual is not None

    def body(*refs):
        refs = refs[:2 + has_res] + refs[2 + has_res + (after is not None):]
        if has_res:
            a_ref, b_ref, r_ref, o_ref, acc = refs
        else:
            a_ref, b_ref, o_ref, acc = refs
        kk = pl.program_id(2)
        part = _dg(a_ref[...].astype(BF16), b_ref[...].astype(BF16), ca, cb)

        @pl.when(kk == 0)
        def _():
            acc[...] = part

        @pl.when(kk > 0)
        def _():
            acc[...] += part

        @pl.when(kk == nk - 1)
        def _():
            r = acc[...]
            if has_res:
                r = r + r_ref[...]
            o_ref[...] = r.astype(out_dtype)

    ins = [a, b] + ([residual] if has_res else []) + ([after] if after is not None else [])
    specs = [a_spec, b_spec] + ([o_spec] if has_res else []) + ([pl.BlockSpec(memory_space=pl.ANY)] if after is not None else [])
    return pl.pallas_call(
        body, name=name, grid=(m // tm, n // tn, nk), in_specs=specs, out_specs=o_spec,
        out_shape=jax.ShapeDtypeStruct((m, n), out_dtype),
        scratch_shapes=[pltpu.VMEM((tm, tn), F32)],
        compiler_params=_cparams(("parallel", "parallel", "arbitrary")),
    )(*ins)


ROW_TILE = 512


def _rmsnorm_fwd(x, w, name):
    def body(x_ref, w_ref, o_ref):
        o_ref[...] = _rms(x_ref[...], w_ref[...]).astype(BF16)

    return pl.pallas_call(
        body, name=name, grid=(SEQ // ROW_TILE,),
        in_specs=[pl.BlockSpec((ROW_TILE, D_MODEL), lambda i: (i, 0)), pl.BlockSpec((1, D_MODEL), lambda i: (0, 0))],
        out_specs=pl.BlockSpec((ROW_TILE, D_MODEL), lambda i: (i, 0)),
        out_shape=jax.ShapeDtypeStruct((SEQ, D_MODEL), BF16),
        compiler_params=_cparams(("parallel",)),
    )(x, w)


def _rmsnorm_bwd(x, w, dh, dres, name):
    def body(x_ref, w_ref, dh_ref, dres_ref, dx_ref, dw_ref):
        _, vjp = jax.vjp(_rms, x_ref[...], w_ref[...])
        dx, dw = vjp(dh_ref[...])
        dx_ref[...] = dx + dres_ref[...]

        @pl.when(pl.program_id(0) == 0)
        def _():
            dw_ref[...] = dw

        @pl.when(pl.program_id(0) > 0)
        def _():
            dw_ref[...] += dw

    tile = pl.BlockSpec((ROW_TILE, D_MODEL), lambda i: (i, 0))
    vec = pl.BlockSpec((1, D_MODEL), lambda i: (0, 0))
    return pl.pallas_call(
        body, name=name, grid=(SEQ // ROW_TILE,),
        in_specs=[tile, vec, tile, tile], out_specs=[tile, vec],
        out_shape=[jax.ShapeDtypeStruct((SEQ, D_MODEL), F32), jax.ShapeDtypeStruct((1, D_MODEL), F32)],
        compiler_params=_cparams(("arbitrary",)),
    )(x, w, dh, dres)


def _loss_head(x, w, target):
    def body(x_ref, w_ref, t_ref, loss_ref, dx_ref, dw_ref):
        tgt = t_ref[...]

        def f(xv, wv):
            err = _rms(xv, wv) - tgt
            return 0.5 * jnp.sum(jnp.mean(err * err, axis=-1))

        loss, vjp = jax.vjp(f, x_ref[...], w_ref[...])
        dx, dw = vjp(jnp.ones((), F32))
        dx_ref[...] = dx
        lrow = jnp.full((1, LANES), loss, F32)

        @pl.when(pl.program_id(0) == 0)
        def _():
            dw_ref[...] = dw
            loss_ref[...] = lrow

        @pl.when(pl.program_id(0) > 0)
        def _():
            dw_ref[...] += dw
            loss_ref[...] += lrow

    tile = pl.BlockSpec((ROW_TILE, D_MODEL), lambda i: (i, 0))
    vec = pl.BlockSpec((1, D_MODEL), lambda i: (0, 0))
    return pl.pallas_call(
        body, name="loss_head", grid=(SEQ // ROW_TILE,),
        in_specs=[tile, vec, tile], out_specs=[pl.BlockSpec((1, LANES), lambda i: (0, 0)), tile, vec],
        out_shape=[jax.ShapeDtypeStruct((1, LANES), F32), jax.ShapeDtypeStruct((SEQ, D_MODEL), F32),
                   jax.ShapeDtypeStruct((1, D_MODEL), F32)],
        compiler_params=_cparams(("arbitrary",)),
    )(x, w, target)


S5_T = 256


def _s5_post(ypre, gate, wglu):
    y = _gelu(ypre)
    y = y * jax.nn.sigmoid(_bdot(y, wglu, 1, 0))
    return y * _silu(gate)


def _s5_fwd(proj, bbre, bbim, cre, cim, a2, dvec, wglu, name):
    def body(u_ref, g_ref, bbre_ref, bbim_ref, cre_ref, cim_ref, a_ref, d_ref, wg_ref, o_ref, sre_ref, sim_ref, st):
        @pl.when(pl.program_id(0) == 0)
        def _():
            st[...] = jnp.zeros_like(st)

        u = u_ref[...]
        ub = u.astype(BF16)
        sre_ref[...] = _dg(ub, bbre_ref[...], 1, 0)
        sim_ref[...] = _dg(ub, bbim_ref[...], 1, 0)
        ar, ai = a_ref[0:1, :], a_ref[1:2, :]

        def step(t, carry):
            sr, si = carry
            nr = ar * sr - ai * si + sre_ref[pl.ds(t, 1), :]
            ni = ar * si + ai * sr + sim_ref[pl.ds(t, 1), :]
            sre_ref[pl.ds(t, 1), :] = nr
            sim_ref[pl.ds(t, 1), :] = ni
            return nr, ni

        sr, si = lax.fori_loop(0, S5_T, step, (st[0:1, :], st[1:2, :]), unroll=8)
        st[0:1, :] = sr
        st[1:2, :] = si
        ypre = (_dg(sre_ref[...].astype(BF16), cre_ref[...], 1, 0) - _dg(sim_ref[...].astype(BF16), cim_ref[...], 1, 0)
                + d_ref[...] * u)
        o_ref[...] = _s5_post(ypre, g_ref[...], wg_ref[...]).astype(BF16)

    full = lambda shape: pl.BlockSpec(shape, lambda c: (0, 0))
    return pl.pallas_call(
        body, name=name, grid=(SEQ // S5_T,),
        in_specs=[pl.BlockSpec((S5_T, BW), lambda c: (c, C_S5U // BW)), pl.BlockSpec((S5_T, BW), lambda c: (c, C_S5G // BW)),
                  full((BW, S5_CH)), full((BW, S5_CH)), full((S5_CH, BW)), full((S5_CH, BW)),
                  full((2, S5_CH)), full((1, BW)), full((BW, BW))],
        out_specs=[pl.BlockSpec((S5_T, BW), lambda c: (c, 0)), pl.BlockSpec((S5_T, S5_CH), lambda c: (c, 0)),
                   pl.BlockSpec((S5_T, S5_CH), lambda c: (c, 0))],
        out_shape=[jax.ShapeDtypeStruct((SEQ, BW), BF16), jax.ShapeDtypeStruct((SEQ, S5_CH), F32),
                   jax.ShapeDtypeStruct((SEQ, S5_CH), F32)],
        scratch_shapes=[pltpu.VMEM((2, S5_CH), F32)],
        compiler_params=_cparams(("arbitrary",)),
    )(proj, proj, bbre, bbim, cre, cim, a2, dvec, wglu)


def _s5_bwd(proj, dproj, dout, sre, sim, bbre, bbim, cre, cim, a2, dvec, wglu, name):
    nc = SEQ // S5_T

    def body(u_ref, g_ref, do_ref, sre_ref, sim_ref, pre_ref, pim_ref, bbre_ref, bbim_ref, cre_ref, cim_ref, a_ref,
             d_ref, wg_ref, dproj_in, dp_ref, dbbre_ref, dbbim_ref, dcre_ref, dcim_ref, da_ref, dd_ref, dwg_ref,
             gre, gim, st):
        c = nc - 1 - pl.program_id(0)

        @pl.when(pl.program_id(0) == 0)
        def _():
            st[...] = jnp.zeros_like(st)
            for r in (dbbre_ref, dbbim_ref, dcre_ref, dcim_ref, da_ref, dd_ref, dwg_ref):
                r[...] = jnp.zeros_like(r)

        u = u_ref[...]
        s_re, s_im = sre_ref[...], sim_ref[...]

        def head(s_re, s_im, cre_v, cim_v, dv, uv, gv, wg):
            ypre = _bdot(s_re, cre_v, 1, 0) - _bdot(s_im, cim_v, 1, 0) + dv * uv
            return _s5_post(ypre, gv, wg)

        _, vjp = jax.vjp(head, s_re, s_im, cre_ref[...].astype(F32), cim_ref[...].astype(F32), d_ref[...], u,
                         g_ref[...], wg_ref[...].astype(F32))
        ds_re, ds_im, dcre, dcim, dd, du_d, dgate, dwg = vjp(do_ref[0])
        dcre_ref[...] += dcre
        dcim_ref[...] += dcim
        dd_ref[...] += dd
        dwg_ref[...] += dwg
        dp_ref[:, BW:] = dgate
        gre[...] = ds_re
        gim[...] = ds_im
        ar, ai = a_ref[0:1, :], a_ref[1:2, :]

        def step(i, carry):
            t = S5_T - 1 - i
            gr, gi = carry
            nr = gre[pl.ds(t, 1), :] + gr
            ni = gim[pl.ds(t, 1), :] + gi
            gre[pl.ds(t, 1), :] = nr
            gim[pl.ds(t, 1), :] = ni
            return ar * nr + ai * ni, ar * ni - ai * nr

        gr, gi = lax.fori_loop(0, S5_T, step, (st[0:1, :], st[1:2, :]), unroll=8)
        st[0:1, :] = gr
        st[1:2, :] = gi
        g_re, g_im = gre[...], gim[...]
        first = jnp.where(c > 0, 1.0, 0.0)
        row = lax.broadcasted_iota(jnp.int32, (S5_T, S5_CH), 0)
        p_re = jnp.where(row == 0, pre_ref[7:8, :] * first, pltpu.roll(s_re, 1, 0))
        p_im = jnp.where(row == 0, pim_ref[7:8, :] * first, pltpu.roll(s_im, 1, 0))
        da_ref[0:1, :] += jnp.sum(g_re * p_re + g_im * p_im, axis=0, keepdims=True)
        da_ref[1:2, :] += jnp.sum(g_im * p_re - g_re * p_im, axis=0, keepdims=True)
        ub, grb, gib = u.astype(BF16), g_re.astype(BF16), g_im.astype(BF16)
        dbbre_ref[...] += _dg(ub, grb, 0, 0)
        dbbim_ref[...] += _dg(ub, gib, 0, 0)
        dp_ref[:, :BW] = du_d + _dg(grb, bbre_ref[...], 1, 1) + _dg(gib, bbim_ref[...], 1, 1)

    full = lambda shape: pl.BlockSpec(shape, lambda i: (0, 0))
    rev = lambda w, col=0: pl.BlockSpec((S5_T, w), lambda i: (nc - 1 - i, col))
    prev = pl.BlockSpec((8, S5_CH), lambda i: (jnp.maximum((nc - 1 - i) * (S5_T // 8) - 1, 0), 0))
    return pl.pallas_call(
        body, name=name, grid=(nc,),
        in_specs=[rev(BW, C_S5U // BW), rev(BW, C_S5G // BW), pl.BlockSpec((1, S5_T, BW), lambda i: (0, nc - 1 - i, 0)),
                  rev(S5_CH), rev(S5_CH), prev, prev,
                  full((BW, S5_CH)), full((BW, S5_CH)), full((S5_CH, BW)), full((S5_CH, BW)),
                  full((2, S5_CH)), full((1, BW)), full((BW, BW)), pl.BlockSpec(memory_space=pl.ANY)],
        out_specs=[rev(2 * BW, C_S5U // (2 * BW)), full((BW, S5_CH)), full((BW, S5_CH)), full((S5_CH, BW)), full((S5_CH, BW)),
                   full((2, S5_CH)), full((1, BW)), full((BW, BW))],
        input_output_aliases={14: 0},
        out_shape=[jax.ShapeDtypeStruct((SEQ, IN_PAD), F32),
                   jax.ShapeDtypeStruct((BW, S5_CH), F32), jax.ShapeDtypeStruct((BW, S5_CH), F32),
                   jax.ShapeDtypeStruct((S5_CH, BW), F32), jax.ShapeDtypeStruct((S5_CH, BW), F32),
                   jax.ShapeDtypeStruct((2, S5_CH), F32), jax.ShapeDtypeStruct((1, BW), F32),
                   jax.ShapeDtypeStruct((BW, BW), F32)],
        scratch_shapes=[pltpu.VMEM((S5_T, S5_CH), F32), pltpu.VMEM((S5_T, S5_CH), F32), pltpu.VMEM((2, S5_CH), F32)],
        compiler_params=_cparams(("arbitrary",)),
    )(proj, proj, dout, sre, sim, sre, sim, bbre, bbim, cre, cim, a2, dvec, wglu, dproj)


def _s5_disc(lam_re, lam_im, b_re, b_im, c_re, c_im, d, log_step):
    step = jnp.exp(log_step)[:, None]
    mag = jnp.exp(lam_re * step)
    ab_re, ab_im = mag * jnp.cos(lam_im * step), mag * jnp.sin(lam_im * step)
    den = lam_re * lam_re + lam_im * lam_im
    nr = ab_re - 1.0
    coef_re = (nr * lam_re + ab_im * lam_im) / den
    coef_im = (ab_im * lam_re - nr * lam_im) / den
    bb_re = coef_re[..., None] * b_re - coef_im[..., None] * b_im
    bb_im = coef_re[..., None] * b_im + coef_im[..., None] * b_re
    def block_diag(t, rows_per, cols_per):
        wide = jnp.tile(t.reshape(S5_GROUPS * rows_per, cols_per), (1, S5_GROUPS))
        r = lax.broadcasted_iota(jnp.int32, wide.shape, 0) // rows_per
        c = lax.broadcasted_iota(jnp.int32, wide.shape, 1) // cols_per
        return jnp.where(r == c, wide, 0.0)

    bbre = block_diag(jnp.swapaxes(bb_re, 1, 2), S5_P, S5_STATE)
    bbim = block_diag(jnp.swapaxes(bb_im, 1, 2), S5_P, S5_STATE)
    cre = block_diag(jnp.swapaxes(c_re, 1, 2), S5_STATE, S5_P)
    cim = block_diag(jnp.swapaxes(c_im, 1, 2), S5_STATE, S5_P)
    a2 = jnp.stack([ab_re.reshape(-1), ab_im.reshape(-1)])
    return bbre, bbim, cre, cim, a2, d.reshape(1, BW)


def _left_lanes(shape):
    return lax.broadcasted_iota(jnp.int32, shape, 1) < 64


def _sgu_chunk(u, v, gate, ln_w, ln_b, w, bias):
    u32, v32 = _gelu(u), _gelu(v)
    mu = jnp.mean(v32, axis=-1, keepdims=True)
    var = jnp.mean(jnp.square(v32 - mu), axis=-1, keepdims=True)
    vn = (v32 - mu) * lax.rsqrt(var + EPS) * ln_w + ln_b
    t_i = lax.broadcasted_iota(jnp.int32, (SGU_CHUNK, SGU_CHUNK), 0)
    s_i = lax.broadcasted_iota(jnp.int32, (SGU_CHUNK, SGU_CHUNK), 1)
    causal = t_i >= s_i
    left = _left_lanes((SGU_CHUNK, LANES))
    sgate = _silu(gate)
    outs = []
    for j in range(BW // LANES):
        vb = vn[:, j * LANES:(j + 1) * LANES]
        s_blk = (_bdot(jnp.where(causal, w[2 * j], 0.0), jnp.where(left, vb, 0.0), 1, 0)
                 + _bdot(jnp.where(causal, w[2 * j + 1], 0.0), jnp.where(left, 0.0, vb), 1, 0))
        sl = slice(j * LANES, (j + 1) * LANES)
        outs.append(u32[:, sl] * (s_blk + bias[:, sl]) * sgate[:, sl])
    return outs


def _sgu_fwd(proj, ln_w, ln_b, w, bias, name):
    def body(u_ref, v_ref, g_ref, lw_ref, lb_ref, w_ref, b_ref, o_ref):
        outs = _sgu_chunk(u_ref[...], v_ref[...], g_ref[...], lw_ref[...], lb_ref[...], w_ref[...], b_ref[...])
        for j, o in enumerate(outs):
            o_ref[:, j * LANES:(j + 1) * LANES] = o.astype(BF16)

    blk = lambda col: pl.BlockSpec((SGU_CHUNK, BW), lambda c: (c, col // BW))
    vec = pl.BlockSpec((1, BW), lambda c: (0, 0))
    return pl.pallas_call(
        body, name=name, grid=(SEQ // SGU_CHUNK,),
        in_specs=[blk(C_SGU_U), blk(C_SGU_V), blk(C_SGU_G), vec, vec,
                  pl.BlockSpec((SGU_HEADS, SGU_CHUNK, SGU_CHUNK), lambda c: (0, 0, 0)),
                  pl.BlockSpec((SGU_CHUNK, BW), lambda c: (0, 0))],
        out_specs=pl.BlockSpec((SGU_CHUNK, BW), lambda c: (c, 0)),
        out_shape=jax.ShapeDtypeStruct((SEQ, BW), BF16),
        compiler_params=_cparams(("parallel",)),
    )(proj, proj, proj, ln_w, ln_b, w, bias)


def _sgu_bwd(proj, dproj, dout, ln_w, ln_b, w, bias, name):
    def body(u_ref, v_ref, g_ref, do_ref, lw_ref, lb_ref, w_ref, b_ref, dproj_in, dp_ref, dlw_ref, dlb_ref, dw_ref, db_ref):
        _, vjp = jax.vjp(_sgu_chunk, u_ref[...], v_ref[...], g_ref[...], lw_ref[...], lb_ref[...], w_ref[...], b_ref[...])
        do = do_ref[0]
        du, dv, dgate, dlw, dlb, dw, db = vjp([do[:, j * LANES:(j + 1) * LANES] for j in range(BW // LANES)])
        dp_ref[:, 0:BW] = du
        dp_ref[:, BW:2 * BW] = dv
        dp_ref[:, 2 * BW:3 * BW] = dgate
        dp_ref[:, 3 * BW:] = jnp.zeros((SGU_CHUNK, BW), F32)

        @pl.when(pl.program_id(0) == 0)
        def _():
            dlw_ref[...] = dlw
            dlb_ref[...] = dlb
            dw_ref[...] = dw
            db_ref[...] = db

        @pl.when(pl.program_id(0) > 0)
        def _():
            dlw_ref[...] += dlw
            dlb_ref[...] += dlb
            dw_ref[...] += dw
            db_ref[...] += db

    blk = lambda col: pl.BlockSpec((SGU_CHUNK, BW), lambda c: (c, col // BW))
    vec = pl.BlockSpec((1, BW), lambda c: (0, 0))
    wsp = pl.BlockSpec((SGU_HEADS, SGU_CHUNK, SGU_CHUNK), lambda c: (0, 0, 0))
    bsp = pl.BlockSpec((SGU_CHUNK, BW), lambda c: (0, 0))
    return pl.pallas_call(
        body, name=name, grid=(SEQ // SGU_CHUNK,),
        in_specs=[blk(C_SGU_U), blk(C_SGU_V), blk(C_SGU_G), pl.BlockSpec((1, SGU_CHUNK, BW), lambda c: (1, c, 0)),
                  vec, vec, wsp, bsp, pl.BlockSpec(memory_space=pl.ANY)],
        out_specs=[pl.BlockSpec((SGU_CHUNK, 4 * BW), lambda c: (c, C_SGU_U // (4 * BW))), vec, vec, wsp, bsp],
        input_output_aliases={8: 0},
        out_shape=[jax.ShapeDtypeStruct((SEQ, IN_PAD), F32), jax.ShapeDtypeStruct((1, BW), F32),
                   jax.ShapeDtypeStruct((1, BW), F32), jax.ShapeDtypeStruct((SGU_HEADS, SGU_CHUNK, SGU_CHUNK), F32),
                   jax.ShapeDtypeStruct((SGU_CHUNK, BW), F32)],
        compiler_params=_cparams(("arbitrary",)),
    )(proj, proj, proj, dout, ln_w, ln_b, w, bias, dproj)


CONV_BLK = 256


def _m2_conv_fwd(proj, w, b, name):
    def body(x_ref, w_ref, b_ref, o_ref):
        x = x_ref[...]
        acc = jnp.zeros_like(x) + b_ref[...]
        for k in range(M2_CONV):
            acc = acc + w_ref[k:k + 1, :] * _shift_down(x, M2_CONV - 1 - k)
        o_ref[...] = _silu(acc)

    return pl.pallas_call(
        body, name=name, grid=(M2_CONV_CH // CONV_BLK,),
        in_specs=[pl.BlockSpec((SEQ, CONV_BLK), lambda j: (0, C_M2X // CONV_BLK + j)),
                  pl.BlockSpec((M2_CONV, CONV_BLK), lambda j: (0, j)), pl.BlockSpec((1, CONV_BLK), lambda j: (0, j))],
        out_specs=pl.BlockSpec((SEQ, CONV_BLK), lambda j: (0, j)),
        out_shape=jax.ShapeDtypeStruct((SEQ, M2_CONV_CH), F32),
        compiler_params=_cparams(("parallel",)),
    )(proj, w, b)


def _m2_conv_bwd(proj, dproj, dxa, w, b, name):
    def body(x_ref, d_ref, w_ref, b_ref, dproj_in, dx_ref, dw_ref, db_ref):
        x = x_ref[...]
        xs = [_shift_down(x, M2_CONV - 1 - k) for k in range(M2_CONV)]
        acc = jnp.zeros_like(x) + b_ref[...]
        for k in range(M2_CONV):
            acc = acc + w_ref[k:k + 1, :] * xs[k]
        sg = jax.nn.sigmoid(acc)
        dacc = d_ref[...] * (sg * (1.0 + acc * (1.0 - sg)))
        dx = jnp.zeros_like(x)
        for k in range(M2_CONV):
            dx = dx + w_ref[k:k + 1, :] * _shift_up(dacc, M2_CONV - 1 - k)
            dw_ref[k:k + 1, :] = jnp.sum(dacc * xs[k], axis=0, keepdims=True)
        dx_ref[...] = dx
        db_ref[...] = jnp.sum(dacc, axis=0, keepdims=True)

    return pl.pallas_call(
        body, name=name, grid=(M2_CONV_CH // CONV_BLK,),
        in_specs=[pl.BlockSpec((SEQ, CONV_BLK), lambda j: (0, C_M2X // CONV_BLK + j)),
                  pl.BlockSpec((SEQ, CONV_BLK), lambda j: (0, j)),
                  pl.BlockSpec((M2_CONV, CONV_BLK), lambda j: (0, j)), pl.BlockSpec((1, CONV_BLK), lambda j: (0, j)),
                  pl.BlockSpec(memory_space=pl.ANY)],
        out_specs=[pl.BlockSpec((SEQ, CONV_BLK), lambda j: (0, C_M2X // CONV_BLK + j)),
                   pl.BlockSpec((M2_CONV, CONV_BLK), lambda j: (0, j)), pl.BlockSpec((1, CONV_BLK), lambda j: (0, j))],
        input_output_aliases={4: 0},
        out_shape=[jax.ShapeDtypeStruct((SEQ, IN_PAD), F32), jax.ShapeDtypeStruct((M2_CONV, M2_CONV_CH), F32),
                   jax.ShapeDtypeStruct((1, M2_CONV_CH), F32)],
        compiler_params=_cparams(("parallel",)),
    )(proj, dxa, w, b, dproj)


N_PAIR = M2_HEADS // 2
HI = lax.Precision.HIGHEST


def _col(a, h):
    lane = lax.broadcasted_iota(jnp.int32, a.shape, 1)
    return jnp.sum(jnp.where(lane == h, a, 0.0), axis=1, keepdims=True)


def _row(a, h):
    sub = lax.broadcasted_iota(jnp.int32, a.shape, 0)
    return jnp.sum(jnp.where(sub == h, a, 0.0), axis=0, keepdims=True)


def _ssd_chunk(xs, bms, cms, dtr, zs, states, dt_bias, a_log, dfs, nws):
    q = M2_CHUNK
    dt = _softplus(dtr + dt_bias)
    da = dt * (-jnp.exp(a_log))
    l_i = lax.broadcasted_iota(jnp.int32, (q, q), 0)
    s_i = lax.broadcasted_iota(jnp.int32, (q, q), 1)
    causal = l_i >= s_i
    tril = jnp.where(causal, 1.0, 0.0)
    a_cs = _dg(tril, da, 1, 0, HI)
    a_cs_t = _dg(da, tril, 0, 1, HI)
    a_end = _row(a_cs, q - 1)
    left = _left_lanes((q, LANES))
    left1 = _left_lanes((1, LANES))
    ys, nexts = [], []
    for j in range(N_PAIR):
        grp = j // 2
        bm, cm = bms[grp], cms[grp]
        h0, h1 = 2 * j, 2 * j + 1
        cb = _bdot(cm, bm, 1, 1)
        xdt = xs[j] * jnp.where(left, _col(dt, h0), _col(dt, h1))
        acs0, acs1 = _col(a_cs, h0), _col(a_cs, h1)
        y = _bdot(cm, states[j], 1, 0) * jnp.where(left, jnp.exp(acs0), jnp.exp(acs1))
        s_new = states[j] * jnp.where(left1, jnp.exp(_col(a_end, h0)), jnp.exp(_col(a_end, h1)))
        for h, acs, xh in ((h0, acs0, jnp.where(left, xdt, 0.0)), (h1, acs1, jnp.where(left, 0.0, xdt))):
            decay = jnp.exp(jnp.where(causal, acs - _row(a_cs_t, h), -jnp.inf))
            y = y + _bdot(cb * decay, xh, 1, 0)
            s_new = s_new + _bdot(bm * jnp.exp(_col(a_end, h) - acs), xh, 0, 0)
        ys.append((y + dfs[j] * xs[j]) * _silu(zs[j]))
        nexts.append(s_new)
    ssq = sum(jnp.sum(y * y, axis=-1, keepdims=True) for y in ys)
    scale = lax.rsqrt(ssq / BW + EPS)
    return [y * scale * nw for y, nw in zip(ys, nws)], nexts


def _blocks(ref, n, width=LANES):
    return [ref[:, j * width:(j + 1) * width] for j in range(n)]


def _ssd_fwd(proj, xa, dt_bias, a_log, dfull, nw, name):
    nc = SEQ // M2_CHUNK

    def body(x_ref, b_ref, c_ref, dt_ref, z_ref, dtb_ref, al_ref, df_ref, nw_ref, o_ref, sin_ref, st):
        @pl.when(pl.program_id(0) == 0)
        def _():
            st[...] = jnp.zeros_like(st)

        states = [st[j] for j in range(N_PAIR)]
        for j in range(N_PAIR):
            sin_ref[0, j] = states[j]
        ys, nexts = _ssd_chunk(_blocks(x_ref, 4), _blocks(b_ref, 2), _blocks(c_ref, 2), dt_ref[...], _blocks(z_ref, 4),
                               states, dtb_ref[...], al_ref[...], _blocks(df_ref, 4), _blocks(nw_ref, 4))
        for j in range(N_PAIR):
            o_ref[:, j * LANES:(j + 1) * LANES] = ys[j].astype(BF16)
            st[j] = nexts[j]

    vec8 = pl.BlockSpec((1, LANES), lambda c: (0, 0))
    vec = pl.BlockSpec((1, BW), lambda c: (0, 0))
    return pl.pallas_call(
        body, name=name, grid=(nc,),
        in_specs=[pl.BlockSpec((M2_CHUNK, BW), lambda c: (c, 0)), pl.BlockSpec((M2_CHUNK, 256), lambda c: (c, 2)),
                  pl.BlockSpec((M2_CHUNK, 256), lambda c: (c, 3)), pl.BlockSpec((M2_CHUNK, LANES), lambda c: (c, C_DT // LANES)),
                  pl.BlockSpec((M2_CHUNK, BW), lambda c: (c, C_M2Z // BW)), vec8, vec8, vec, vec],
        out_specs=[pl.BlockSpec((M2_CHUNK, BW), lambda c: (c, 0)),
                   pl.BlockSpec((1, N_PAIR, M2_STATE, LANES), lambda c: (c, 0, 0, 0))],
        out_shape=[jax.ShapeDtypeStruct((SEQ, BW), BF16), jax.ShapeDtypeStruct((nc, N_PAIR, M2_STATE, LANES), F32)],
        scratch_shapes=[pltpu.VMEM((N_PAIR, M2_STATE, LANES), F32)],
        compiler_params=_cparams(("arbitrary",)),
    )(xa, xa, xa, proj, proj, dt_bias, a_log, dfull, nw)


def _ssd_bwd(proj, dproj, xa, dout, s_in, dt_bias, a_log, dfull, nw, name):
    nc = SEQ // M2_CHUNK

    def body(x_ref, b_ref, c_ref, dt_ref, z_ref, do_ref, sin_ref, dtb_ref, al_ref, df_ref, nw_ref, dproj_in,
             dp_ref, dxa_ref, ddtb_ref, dal_ref, ddf_ref, dnw_ref, dst):
        @pl.when(pl.program_id(0) == 0)
        def _():
            dst[...] = jnp.zeros_like(dst)
            for r in (ddtb_ref, dal_ref, ddf_ref, dnw_ref):
                r[...] = jnp.zeros_like(r)

        states = [sin_ref[0, j] for j in range(N_PAIR)]
        _, vjp = jax.vjp(_ssd_chunk, _blocks(x_ref, 4), _blocks(b_ref, 2), _blocks(c_ref, 2), dt_ref[...],
                         _blocks(z_ref, 4), states, dtb_ref[...], al_ref[...], _blocks(df_ref, 4), _blocks(nw_ref, 4))
        dxs, dbs, dcs, ddt, dzs, dstates, ddtb, dal, ddfs, dnws = vjp(
            ([do_ref[0, :, j * LANES:(j + 1) * LANES] for j in range(N_PAIR)], [dst[j] for j in range(N_PAIR)]))
        for j in range(N_PAIR):
            sl = slice(j * LANES, (j + 1) * LANES)
            dxa_ref[:, sl] = dxs[j]
            dp_ref[:, sl] = dzs[j]
            dst[j] = dstates[j]
            ddf_ref[:, sl] += ddfs[j]
            dnw_ref[:, sl] += dnws[j]
        for g in range(2):
            dxa_ref[:, BW + g * LANES:BW + (g + 1) * LANES] = dbs[g]
            dxa_ref[:, BW + 256 + g * LANES:BW + 256 + (g + 1) * LANES] = dcs[g]
        dp_ref[:, BW:BW + LANES] = ddt
        dp_ref[:, BW + LANES:] = jnp.zeros((M2_CHUNK, 2 * BW - BW - LANES), F32)
        ddtb_ref[...] += ddtb
        dal_ref[...] += dal

    rev = lambda w, col=0: pl.BlockSpec((M2_CHUNK, w), lambda i: (nc - 1 - i, col))
    vec8 = pl.BlockSpec((1, LANES), lambda i: (0, 0))
    vec = pl.BlockSpec((1, BW), lambda i: (0, 0))
    return pl.pallas_call(
        body, name=name, grid=(nc,),
        in_specs=[rev(BW), rev(256, 2), rev(256, 3), rev(LANES, C_DT // LANES), rev(BW, C_M2Z // BW),
                  pl.BlockSpec((1, M2_CHUNK, BW), lambda i: (2, nc - 1 - i, 0)),
                  pl.BlockSpec((1, N_PAIR, M2_STATE, LANES), lambda i: (nc - 1 - i, 0, 0, 0)), vec8, vec8, vec, vec,
                  pl.BlockSpec(memory_space=pl.ANY)],
        out_specs=[rev(2 * BW, C_M2Z // (2 * BW)), rev(M2_CONV_CH), vec8, vec8, vec, vec],
        input_output_aliases={11: 0},
        out_shape=[jax.ShapeDtypeStruct((SEQ, IN_PAD), F32), jax.ShapeDtypeStruct((SEQ, M2_CONV_CH), F32),
                   jax.ShapeDtypeStruct((1, LANES), F32), jax.ShapeDtypeStruct((1, LANES), F32),
                   jax.ShapeDtypeStruct((1, BW), F32), jax.ShapeDtypeStruct((1, BW), F32)],
        scratch_shapes=[pltpu.VMEM((N_PAIR, M2_STATE, LANES), F32)],
        compiler_params=_cparams(("arbitrary",)),
    )(xa, xa, xa, proj, proj, dout, s_in, dt_bias, a_log, dfull, nw, dproj)


def _sc_specs():
    col = lambda kind: pl.BlockSpec((SEQ, LANES), lambda j: (0, C_SC // LANES + 4 * j + kind))
    return [col(0), col(1), col(2), col(3)]


def _sc_fwd(proj, w, name):
    def body(b_ref, c_ref, h_ref, g_ref, w_ref, o_ref):
        ch = c_ref[...] * h_ref[...]
        acc = jnp.zeros_like(ch)
        for k in range(SC_CONV):
            acc = acc + w_ref[k:k + 1, :] * _shift_down(ch, SC_CONV - 1 - k)
        o_ref[...] = (b_ref[...] * acc * _silu(g_ref[...])).astype(BF16)

    return pl.pallas_call(
        body, name=name, grid=(BW // LANES,),
        in_specs=_sc_specs() + [pl.BlockSpec((SC_CONV, LANES), lambda j: (0, j))],
        out_specs=pl.BlockSpec((SEQ, LANES), lambda j: (0, j)),
        out_shape=jax.ShapeDtypeStruct((SEQ, BW), BF16),
        compiler_params=_cparams(("parallel",)),
    )(proj, proj, proj, proj, w)


def _sc_bwd(proj, dproj, dout, w, name):
    def body(b_ref, c_ref, h_ref, g_ref, do_ref, w_ref, dproj_in, dp_ref, dw_ref):
        cv, hv, gv = c_ref[...], h_ref[...], g_ref[...]
        ch = cv * hv
        chs = [_shift_down(ch, SC_CONV - 1 - k) for k in range(SC_CONV)]
        acc = jnp.zeros_like(ch)
        for k in range(SC_CONV):
            acc = acc + w_ref[k:k + 1, :] * chs[k]
        sg = jax.nn.sigmoid(gv)
        do = do_ref[0]
        bv = b_ref[...]
        dp_ref[:, 0:LANES] = do * acc * (gv * sg)
        dp_ref[:, 3 * LANES:] = do * bv * acc * (sg * (1.0 + gv * (1.0 - sg)))
        dacc = do * bv * (gv * sg)
        dch = jnp.zeros_like(ch)
        for k in range(SC_CONV):
            dch = dch + w_ref[k:k + 1, :] * _shift_up(dacc, SC_CONV - 1 - k)
            dw_ref[k:k + 1, :] = jnp.sum(dacc * chs[k], axis=0, keepdims=True)
        dp_ref[:, LANES:2 * LANES] = dch * hv
        dp_ref[:, 2 * LANES:3 * LANES] = dch * cv

    wsp = pl.BlockSpec((SC_CONV, LANES), lambda j: (0, j))
    return pl.pallas_call(
        body, name=name, grid=(BW // LANES,),
        in_specs=_sc_specs() + [pl.BlockSpec((1, SEQ, LANES), lambda j: (3, 0, j)), wsp, pl.BlockSpec(memory_space=pl.ANY)],
        out_specs=[pl.BlockSpec((SEQ, 4 * LANES), lambda j: (0, C_SC // (4 * LANES) + j)), wsp],
        input_output_aliases={6: 0},
        out_shape=[jax.ShapeDtypeStruct((SEQ, IN_PAD), F32), jax.ShapeDtypeStruct((SC_CONV, BW), F32)],
        compiler_params=_cparams(("parallel",)),
    )(proj, proj, proj, proj, dout, w, dproj)


MERGE_T = 256
MERGE_BWD_T = 512


def _merge_fwd(proj, ys, merge_b, w_branch, name):
    def body(y_ref, lg_ref, b_ref, w_ref, o_ref):
        acc = jnp.zeros((MERGE_T, D_MODEL), F32)
        for k in range(N_BRANCH):
            gate = jax.nn.sigmoid(lg_ref[:, k * D_MODEL:(k + 1) * D_MODEL] + b_ref[k])
            acc = acc + gate * _dg(y_ref[k], w_ref[k], 1, 0)
        o_ref[...] = acc.astype(BF16)

    return pl.pallas_call(
        body, name=name, grid=(SEQ // MERGE_T,),
        in_specs=[pl.BlockSpec((N_BRANCH, MERGE_T, BW), lambda i: (0, i, 0)),
                  pl.BlockSpec((MERGE_T, N_BRANCH * D_MODEL), lambda i: (i, C_MERGE // (N_BRANCH * D_MODEL))),
                  pl.BlockSpec((N_BRANCH, 1, D_MODEL), lambda i: (0, 0, 0)),
                  pl.BlockSpec((N_BRANCH, BW, D_MODEL), lambda i: (0, 0, 0))],
        out_specs=pl.BlockSpec((MERGE_T, D_MODEL), lambda i: (i, 0)),
        out_shape=jax.ShapeDtypeStruct((SEQ, D_MODEL), BF16),
        compiler_params=_cparams(("parallel",)),
    )(ys, proj, merge_b, w_branch)


def _merge_bwd(proj, ys, dm, merge_b, w_branch, name):
    nt = SEQ // MERGE_BWD_T

    def body(y_ref, lg_ref, dm_ref, b_ref, w_ref, dy_ref, dlg_ref, dw_ref, db_ref, dw_acc):
        i = pl.program_id(1)
        gate = jax.nn.sigmoid(lg_ref[...] + b_ref[0])
        y = y_ref[0]
        dmv = dm_ref[...]
        dbo = (gate * dmv).astype(BF16)
        dlg = _dg(y, w_ref[0], 1, 0) * dmv * gate * (1.0 - gate)
        dlg_ref[...] = dlg
        dy_ref[0] = _dg(dbo, w_ref[0], 1, 1)
        dwp = _dg(y, dbo, 0, 0)
        dbp = jnp.sum(dlg, axis=0, keepdims=True)

        @pl.when(i == 0)
        def _():
            dw_acc[...] = dwp
            db_ref[0] = dbp

        @pl.when(i > 0)
        def _():
            dw_acc[...] += dwp
            db_ref[0] += dbp

        @pl.when(i == nt - 1)
        def _():
            dw_ref[0] = dw_acc[...].astype(BF16)

    return pl.pallas_call(
        body, name=name, grid=(N_BRANCH, nt),
        in_specs=[pl.BlockSpec((1, MERGE_BWD_T, BW), lambda k, i: (k, i, 0)),
                  pl.BlockSpec((MERGE_BWD_T, D_MODEL), lambda k, i: (i, C_MERGE // D_MODEL + k)),
                  pl.BlockSpec((MERGE_BWD_T, D_MODEL), lambda k, i: (i, 0)),
                  pl.BlockSpec((1, 1, D_MODEL), lambda k, i: (k, 0, 0)),
                  pl.BlockSpec((1, BW, D_MODEL), lambda k, i: (k, 0, 0))],
        out_specs=[pl.BlockSpec((1, MERGE_BWD_T, BW), lambda k, i: (k, i, 0)),
                   pl.BlockSpec((MERGE_BWD_T, D_MODEL), lambda k, i: (i, k)),
                   pl.BlockSpec((1, BW, D_MODEL), lambda k, i: (k, 0, 0)),
                   pl.BlockSpec((1, 1, D_MODEL), lambda k, i: (k, 0, 0))],
        out_shape=[jax.ShapeDtypeStruct((N_BRANCH, SEQ, BW), F32), jax.ShapeDtypeStruct((SEQ, IN_PAD), F32),
                   jax.ShapeDtypeStruct((N_BRANCH, BW, D_MODEL), BF16), jax.ShapeDtypeStruct((N_BRANCH, 1, D_MODEL), F32)],
        scratch_shapes=[pltpu.VMEM((BW, D_MODEL), F32)],
        compiler_params=_cparams(("parallel", "arbitrary")),
    )(ys, proj, dm, merge_b, w_branch)


def _adamw(glist, w, m, v, rows, name):
    nl = len(glist)
    n, r, c = glist[0].shape
    assert w.shape == (nl, r, c) and r % rows == 0
    nb = r // rows

    def body(*refs):
        g_refs = refs[:nl]
        w_ref, m_ref, v_ref, go_ref, d_ref, mo_ref, vo_ref = refs[nl:]
        for layer in range(nl):
            @pl.when(pl.program_id(0) == layer)
            def _(g_ref=g_refs[layer]):
                g = g_ref[0].astype(F32)
                for s in range(1, n):
                    g = g + g_ref[s].astype(F32)
                mn = ADAM_B1 * m_ref[0] + (1.0 - ADAM_B1) * g
                vn = ADAM_B2 * v_ref[0] + (1.0 - ADAM_B2) * jnp.square(g)
                m_hat = mn / (1.0 - ADAM_B1 ** ADAM_STEP)
                v_hat = vn / (1.0 - ADAM_B2 ** ADAM_STEP)
                go_ref[0] = g
                d_ref[0] = -ADAM_LR * (m_hat / (jnp.sqrt(v_hat) + ADAM_EPS) + ADAM_WD * w_ref[0])
                mo_ref[0] = mn
                vo_ref[0] = vn

    def g_spec(layer):
        return pl.BlockSpec((n, rows, c), lambda a, i: (0, jnp.where(a < layer, 0, jnp.where(a == layer, i, nb - 1)), 0))

    blk = pl.BlockSpec((1, rows, c), lambda a, i: (a, i, 0))
    out = jax.ShapeDtypeStruct((nl, r, c), F32)
    return pl.pallas_call(
        body, name=name, grid=(nl, nb),
        in_specs=[g_spec(layer) for layer in range(nl)] + [blk, blk, blk],
        out_specs=[blk, blk, blk, blk], out_shape=[out, out, out, out],
        compiler_params=_cparams(("arbitrary", "arbitrary")),
    )(*glist, w, m, v)


def _slot_sum(gslots, name):
    n, r, c = gslots.shape

    def body(g_ref, o_ref):
        g = g_ref[0]
        for s in range(1, n):
            g = g + g_ref[s]
        o_ref[...] = g

    return pl.pallas_call(
        body, name=name, in_specs=[pl.BlockSpec((n, r, c), lambda: (0, 0, 0))],
        out_specs=pl.BlockSpec((r, c), lambda: (0, 0)), out_shape=jax.ShapeDtypeStruct((r, c), F32),
        compiler_params=_cparams(None),
    )(gslots)


def _me_and_peers():
    x, y, c = lax.axis_index("x"), lax.axis_index("y"), lax.axis_index("c")
    me = 4 * x + 2 * y + c
    peers = []
    for k in range(1, N_DEV):
        px = 1 - x if (k >> 2) & 1 else x
        py = 1 - y if (k >> 1) & 1 else y
        pc = 1 - c if k & 1 else c
        peers.append((4 * px + 2 * py + pc, (px, py, pc)))
    return me, peers


def _exchange(tensors, gather, name):
    n = len(tensors)

    def body(*refs):
        ins, outs = refs[:n], refs[n:2 * n]
        send_sems, recv_sems, local_sems = refs[2 * n:]
        me, peers = _me_and_peers()
        started = []
        for t in range(n):
            own = pltpu.make_async_copy(ins[t] if gather else ins[t].at[me], outs[t].at[me], local_sems.at[t])
            own.start()
            started.append(own)
            for k, (pidx, pos) in enumerate(peers):
                cp = pltpu.make_async_remote_copy(
                    src_ref=ins[t] if gather else ins[t].at[pidx], dst_ref=outs[t].at[me],
                    send_sem=send_sems.at[t, k], recv_sem=recv_sems.at[t, k], device_id=pos, device_id_type=MESH)
                cp.start()
                started.append(cp)
        for cp in started:
            cp.wait()

    any_spec = pl.BlockSpec(memory_space=pl.ANY)
    outs = pl.pallas_call(
        body, name=name, in_specs=[any_spec] * n, out_specs=[any_spec] * n,
        out_shape=[jax.ShapeDtypeStruct(((N_DEV,) + t.shape) if gather else t.shape, t.dtype) for t in tensors],
        scratch_shapes=[pltpu.SemaphoreType.DMA((n, N_DEV - 1)), pltpu.SemaphoreType.DMA((n, N_DEV - 1)),
                        pltpu.SemaphoreType.DMA((n,))],
        compiler_params=pltpu.CompilerParams(has_side_effects=True),
    )(*tensors)
    return list(outs)


_HBM = pl.BlockSpec(memory_space=pltpu.HBM)
_SEM = pl.BlockSpec(memory_space=pltpu.SEMAPHORE)
_EFFECT = pltpu.SideEffectType.DATAFLOW_SIDE_EFFECTING


def _xchg_copies(ins, lands, send_sems, recv_sems, local_sems, gather):
    me, peers = _me_and_peers()
    local, remote = [], []
    for t in range(len(ins)):
        local.append(pltpu.make_async_copy(ins[t] if gather else ins[t].at[me], lands[t].at[me], local_sems.at[t]))
        for k, (pidx, pos) in enumerate(peers):
            remote.append(pltpu.make_async_remote_copy(
                src_ref=ins[t] if gather else ins[t].at[pidx], dst_ref=lands[t].at[me],
                send_sem=send_sems.at[t * (N_DEV - 1) + k], recv_sem=recv_sems.at[t * (N_DEV - 1) + k],
                device_id=pos, device_id_type=MESH))
    return local, remote


def _xchg_start(tensors, gather, name):
    n = len(tensors)
    land_shapes = [((N_DEV,) + t.shape) if gather else t.shape for t in tensors]

    def body(*refs):
        ins, lands = refs[:n], refs[n:2 * n]
        send_sems, recv_sems, local_sems = refs[2 * n:2 * n + 3]
        token = refs[-1]
        local, remote = _xchg_copies(ins, lands, send_sems, recv_sems, local_sems, gather)
        for cp in local + remote:
            cp.start()
        token[...] = jnp.zeros_like(token)

    outs = pl.pallas_call(
        body, name=name,
        out_shape=(pltpu.SemaphoreType.DMA((n * (N_DEV - 1),)), pltpu.SemaphoreType.DMA((n * (N_DEV - 1),)),
                   pltpu.SemaphoreType.DMA((n,)),
                   *[pltpu.HBM(t.shape, t.dtype) for t in tensors],
                   *[pltpu.HBM(s, t.dtype) for s, t in zip(land_shapes, tensors)],
                   jax.ShapeDtypeStruct((8, LANES), F32)),
        in_specs=[_HBM] * (2 * n),
        out_specs=(_SEM, _SEM, _SEM, *[_HBM] * (2 * n), pl.BlockSpec(memory_space=pltpu.VMEM)),
        input_output_aliases={t: 3 + t for t in range(2 * n)},
        compiler_params=pltpu.CompilerParams(has_side_effects=_EFFECT),
    )(*[pltpu.with_memory_space_constraint(t, pltpu.HBM) for t in tensors],
      *[pltpu.with_memory_space_constraint(lax.empty(s, t.dtype), pltpu.HBM) for s, t in zip(land_shapes, tensors)])
    return outs[:-1], outs[-1]


def _xchg_wait(state, gather, after, name):
    send_sems, recv_sems, local_sems = state[:3]
    n = (len(state) - 3) // 2
    thru = state[3:]

    def body(*refs):
        ins, lands = refs[:n], refs[n:2 * n]
        s_sems, r_sems, l_sems = refs[2 * n:2 * n + 3]
        local, remote = _xchg_copies(ins, lands, s_sems, r_sems, l_sems, gather)
        for cp in local:
            cp.wait()
        for cp in remote:
            cp.wait_send()
            cp.wait_recv()

    outs = pl.pallas_call(
        body, name=name,
        out_shape=tuple(pltpu.HBM(t.shape, t.dtype) for t in thru),
        in_specs=[_HBM] * (2 * n) + [_SEM, _SEM, _SEM, pl.BlockSpec(memory_space=pl.ANY)],
        out_specs=tuple([_HBM] * (2 * n)),
        input_output_aliases={t: t for t in range(2 * n)},
        compiler_params=pltpu.CompilerParams(has_side_effects=_EFFECT),
    )(*thru, send_sems, recv_sems, local_sems, after)
    return list(outs[n:])


WEIGHTS = ['norm_w', 'w_in', 's5_lambda_re', 's5_lambda_im', 's5_b_re', 's5_b_im', 's5_c_re', 's5_c_im', 's5_d',
           's5_log_step', 's5_w_glu', 'sgu_ln_w', 'sgu_ln_b', 'sgu_w', 'sgu_b', 'm2_conv_w', 'm2_conv_b', 'm2_dt_bias',
           'm2_a_log', 'm2_d', 'm2_norm_w', 'sc_conv_w', 'merge_b', 'w_branch', 'w_out', 'final_norm_w']
BIG_SHARDED = ['w_in', 'w_branch', 'w_out', 's5_w_glu']
SMALL_SHARDED = ['m2_conv_w', 'sc_conv_w', 'merge_b']
REPLICATED = [n for n in WEIGHTS if n not in BIG_SHARDED + SMALL_SHARDED]
S5_NAMES = ['s5_lambda_re', 's5_lambda_im', 's5_b_re', 's5_b_im', 's5_c_re', 's5_c_im', 's5_d', 's5_log_step']


def _sc_interleave(t):
    lead = t.shape[:-1]
    return jnp.swapaxes(t.reshape(lead + (4, 4, LANES)), -3, -2).reshape(lead + (4 * BW,))


def _pad_in(w):
    z = lambda n: jnp.zeros(w.shape[:-1] + (n,), w.dtype)
    return jnp.concatenate([w[..., 6152:], w[..., 0:1024], w[..., 3072:4096], w[..., 1024:2560], z(512),
                            w[..., 2560:3072], w[..., 4096:4104], z(504), _sc_interleave(w[..., 4104:6152])], axis=-1)


def _unpad_in(g):
    return jnp.concatenate([g[..., C_S5U:C_S5U + 1024], g[..., C_SGU_U:C_SGU_U + 1536], g[..., C_M2Z:C_M2Z + 512],
                            g[..., C_M2X:C_M2X + 1024], g[..., C_DT:C_DT + 8], _sc_interleave(g[..., C_SC:]),
                            g[..., :N_BRANCH * D_MODEL]], axis=-1)


ROW_BLOCK = 8 * LANES


def _pack_rows(tensors, row_mult, batched=False):
    parts = []
    for t in tensors:
        f = t.reshape((t.shape[0], -1) if batched else (1, -1))
        f = jnp.pad(f, ((0, 0), (0, (-f.shape[1]) % ROW_BLOCK)))
        parts.append(f.reshape(f.shape[0], -1, LANES))
    out = jnp.concatenate(parts, axis=1)
    out = jnp.pad(out, ((0, 0), (0, (-out.shape[1]) % row_mult), (0, 0)))
    return out if batched else out[0]


def _unpack_rows(rows, shapes):
    out, r0 = [], 0
    for shp in shapes:
        size = 1
        for s in shp:
            size *= s
        nr = -(-size // ROW_BLOCK) * 8
        out.append(rows[r0:r0 + nr].reshape(-1)[:size].reshape(shp))
        r0 += nr
    return out


def _kernel_col_map():
    m = np.full(IN_PAD, -1, np.int64)
    m[C_MERGE:C_MERGE + 4096] = np.arange(6152, 10248)
    m[C_S5U:C_S5U + 1024] = np.arange(0, 1024)
    m[C_M2X:C_M2X + 1024] = np.arange(3072, 4096)
    m[C_SGU_U:C_SGU_U + 1536] = np.arange(1024, 2560)
    m[C_M2Z:C_M2Z + 512] = np.arange(2560, 3072)
    m[C_DT:C_DT + 8] = np.arange(4096, 4104)
    for j in range(4):
        for kind in range(4):
            k0 = C_SC + 4 * LANES * j + LANES * kind
            m[k0:k0 + LANES] = 4104 + BW * kind + LANES * j + np.arange(LANES)
    return m


def _lane_pieces(sources):
    pieces, cur = [], None
    for lane, src in enumerate(sources):
        key = None if src is None else (src[0], src[1] // LANES, (lane - src[1]) % LANES)
        if cur is not None and key == cur[0]:
            cur[2] = lane + 1
        else:
            if cur is not None and cur[0] is not None:
                pieces.append((*cur[0], cur[1], cur[2]))
            cur = [key, lane, lane + 1]
    if cur is not None and cur[0] is not None:
        pieces.append((*cur[0], cur[1], cur[2]))
    return pieces


def _assemble_block(pieces, load, rows, dtype):
    lane = lax.broadcasted_iota(jnp.int32, (rows, LANES), 1)
    out = None
    for arr, sb, shift, lo, hi in pieces:
        v = load(arr, sb)
        if shift:
            v = pltpu.roll(v, shift, 1)
        if out is None and lo == 0 and hi == LANES:
            out = v
        else:
            out = jnp.where((lane >= lo) & (lane < hi), v, jnp.zeros((rows, LANES), dtype) if out is None else out)
    return jnp.zeros((rows, LANES), dtype) if out is None else out


RELAYOUT_ROWS = 256
SHARD_BLOCKS = -(-SHARD_IN // LANES)


def _load_shard_block(ref, rows):
    def load(j, sb):
        if sb == SHARD_BLOCKS - 1:
            return jnp.broadcast_to(ref[j, :, SHARD_IN - 1:SHARD_IN], (rows, LANES))
        return ref[j, :, sb * LANES:(sb + 1) * LANES]
    return load


def _relayout_w_in(gathered, name):
    kmap = _kernel_col_map()
    dtype = gathered.dtype

    def body(src_ref, o_ref):
        load = _load_shard_block(src_ref, RELAYOUT_ROWS)
        for ob in range(IN_PAD // LANES):
            srcs = [None if kmap[ob * LANES + l] < 0 else (int(kmap[ob * LANES + l]) // SHARD_IN, int(kmap[ob * LANES + l]) % SHARD_IN)
                    for l in range(LANES)]
            o_ref[:, ob * LANES:(ob + 1) * LANES] = _assemble_block(_lane_pieces(srcs), load, RELAYOUT_ROWS, dtype)

    return pl.pallas_call(
        body, name=name, grid=(D_MODEL // RELAYOUT_ROWS,),
        in_specs=[pl.BlockSpec((N_DEV, RELAYOUT_ROWS, SHARD_IN), lambda i: (0, i, 0))],
        out_specs=pl.BlockSpec((RELAYOUT_ROWS, IN_PAD), lambda i: (i, 0)),
        out_shape=jax.ShapeDtypeStruct((D_MODEL, IN_PAD), dtype),
        compiler_params=_cparams(("parallel",)),
    )(gathered)


def _relayout_g_in(gw, name):
    kmap = _kernel_col_map()
    kinv = np.zeros(IN_DIM, np.int64)
    kinv[kmap[kmap >= 0]] = np.nonzero(kmap >= 0)[0]
    dtype = gw.dtype

    def body(src_ref, o_ref):
        load = lambda _, sb: src_ref[:, sb * LANES:(sb + 1) * LANES]
        for j in range(N_DEV):
            for ob in range(SHARD_BLOCKS):
                srcs = [(0, int(kinv[SHARD_IN * j + ob * LANES + l])) if ob * LANES + l < SHARD_IN else None for l in range(LANES)]
                blk = _assemble_block(_lane_pieces(srcs), load, RELAYOUT_ROWS, dtype)
                if ob == SHARD_BLOCKS - 1:
                    o_ref[j, :, SHARD_IN - 1:SHARD_IN] = blk[:, 0:1]
                else:
                    o_ref[j, :, ob * LANES:(ob + 1) * LANES] = blk

    return pl.pallas_call(
        body, name=name, grid=(D_MODEL // RELAYOUT_ROWS,),
        in_specs=[pl.BlockSpec((RELAYOUT_ROWS, IN_PAD), lambda i: (i, 0))],
        out_specs=pl.BlockSpec((N_DEV, RELAYOUT_ROWS, SHARD_IN), lambda i: (0, i, 0)),
        out_shape=jax.ShapeDtypeStruct((N_DEV, D_MODEL, SHARD_IN), dtype),
        compiler_params=_cparams(("parallel",)),
    )(gw)


def _rows128(flat, row_mult=8):
    n = flat.shape[0]
    per = LANES * row_mult
    total = -(-n // per) * per
    return jnp.pad(flat, (0, total - n)).reshape(total // LANES, LANES)


def _pad_lanes(v):
    return jnp.pad(v, (0, LANES - v.shape[0])).reshape(1, LANES)


def _layer_prep(i, p):
    disc, disc_vjp = jax.vjp(_s5_disc, *[p[n][i] for n in S5_NAMES])
    prep = dict(
        nw=p['norm_w'][i].reshape(1, D_MODEL), disc_vjp=disc_vjp,
        s5small=[t.astype(BF16) for t in disc[:4]] + [disc[4], disc[5]],
        sgw=[p['sgu_ln_w'][i].reshape(1, BW), p['sgu_ln_b'][i].reshape(1, BW), p['sgu_w'][i],
             jnp.repeat(p['sgu_b'][i].T, BW // SGU_HEADS, axis=1)],
        cb=p['m2_conv_b'][i].reshape(1, M2_CONV_CH),
        m2w=[_pad_lanes(p['m2_dt_bias'][i]), _pad_lanes(p['m2_a_log'][i]),
             jnp.repeat(p['m2_d'][i], M2_HEAD_DIM).reshape(1, BW), p['m2_norm_w'][i].reshape(1, BW)])
    touch = [t[0, 0].astype(F32) for t in prep['s5small']] + [prep['sgw'][3][0, 0], prep['m2w'][2][0, 0]]
    return prep, sum(touch[1:], touch[0])


def _layer_fwd(x, h, i, prep, full):
    proj = _matmul(h, full['w_in'], 1, 0, F32, 1024, 1024, 1024, f"proj{i}")
    s5w = prep['s5small'] + [full['s5_w_glu']]
    ya, sre, sim = _s5_fwd(proj, *s5w, f"s5_fwd{i}")
    yb = _sgu_fwd(proj, *prep['sgw'], f"sgu_fwd{i}")
    cw = full['m2_conv_w']
    xa = _m2_conv_fwd(proj, cw, prep['cb'], f"m2conv_fwd{i}")
    yc, s_in = _ssd_fwd(proj, xa, *prep['m2w'], f"ssd_fwd{i}")
    scw = full['sc_conv_w']
    yd = _sc_fwd(proj, scw, f"sc_fwd{i}")
    ys = jnp.stack([ya, yb, yc, yd])
    mb = full['merge_b'].reshape(N_BRANCH, 1, D_MODEL)
    merged = _merge_fwd(proj, ys, mb, full['w_branch'], f"merge_fwd{i}")
    x_new = _matmul(merged, full['w_out'], 1, 0, F32, 1024, 1024, 1024, f"out{i}", residual=x)
    saved = dict(x=x, nw=prep['nw'], h=h, proj=proj, disc_vjp=prep['disc_vjp'], s5w=s5w, sre=sre, sim=sim, sgw=prep['sgw'],
                 cw=cw, cb=prep['cb'], xa=xa, m2w=prep['m2w'], s_in=s_in, scw=scw, ys=ys, mb=mb, merged=merged)
    return x_new, saved


def _layer_bwd(dx_out, i, sv, full, on_large_grads=None):
    g = {}
    proj = sv['proj']
    dm = _matmul(dx_out, full['w_out'], 1, 1, F32, 1024, 1024, 1024, f"dmerged{i}")
    g['w_out'] = _matmul(sv['merged'], dx_out, 0, 0, BF16, 1024, 1024, 1024, f"gw_out{i}")
    dys, dproj, g['w_branch'], dmb = _merge_bwd(proj, sv['ys'], dm, sv['mb'], full['w_branch'], f"merge_bwd{i}")
    g['merge_b'] = dmb.reshape(N_BRANCH, D_MODEL)
    dproj, dbbre, dbbim, dcre, dcim, da, dd, dwg = _s5_bwd(proj, dproj, dys, sv['sre'], sv['sim'], *sv['s5w'], f"s5_bwd{i}")
    for n, t in zip(S5_NAMES, sv['disc_vjp']((dbbre, dbbim, dcre, dcim, da, dd))):
        g[n] = t
    g['s5_w_glu'] = dwg.astype(BF16)
    dproj, dlw, dlb, g['sgu_w'], dbias = _sgu_bwd(proj, dproj, dys, *sv['sgw'], f"sgu_bwd{i}")
    g['sgu_ln_w'], g['sgu_ln_b'] = dlw[0], dlb[0]
    g['sgu_b'] = dbias.reshape(SGU_CHUNK, SGU_HEADS, BW // SGU_HEADS).sum(-1).T
    dproj, dxa, ddtb, dal, ddf, dnw = _ssd_bwd(proj, dproj, sv['xa'], dys, sv['s_in'], *sv['m2w'], f"ssd_bwd{i}")
    dproj, g['m2_conv_w'], dcb = _m2_conv_bwd(proj, dproj, dxa, sv['cw'], sv['cb'], f"m2conv_bwd{i}")
    g['m2_conv_b'], g['m2_norm_w'] = dcb[0], dnw[0]
    g['m2_dt_bias'], g['m2_a_log'] = ddtb[0, :M2_HEADS], dal[0, :M2_HEADS]
    g['m2_d'] = ddf.reshape(M2_HEADS, M2_HEAD_DIM).sum(-1)
    dproj, g['sc_conv_w'] = _sc_bwd(proj, dproj, dys, sv['scw'], f"sc_bwd{i}")
    g['w_in'] = _matmul(sv['h'], dproj, 0, 0, BF16, 1024, 1024, 1024, f"gw_in{i}")
    tok = on_large_grads(g) if on_large_grads else None
    dh = _matmul(dproj, full['w_in'], 1, 1, F32, 1024, 1024, 1024, f"dh{i}", after=tok)
    dx_in, dnw_l = _rmsnorm_bwd(sv['x'], sv['nw'], dh, dx_out, f"rms_bwd{i}")
    g['norm_w'] = dnw_l[0]
    return dx_in, g


def _split8(t, axis):
    shp = t.shape
    t = t.reshape(shp[:axis] + (N_DEV, shp[axis] // N_DEV) + shp[axis + 1:])
    return jnp.moveaxis(t, axis, 0)


def _join8(t, axis):
    t = jnp.moveaxis(t, 0, axis)
    shp = t.shape
    return t.reshape(shp[:axis] + (shp[axis] * shp[axis + 1],) + shp[axis + 2:])


SHARD_AXIS = {'w_in': 2, 'w_branch': 3, 'w_out': 1, 's5_w_glu': 1, 'm2_conv_w': 2, 'sc_conv_w': 2, 'merge_b': 2}


def _layer_weights(gathered, i):
    full = {n: _join8(t, SHARD_AXIS[n] - 1) for n, t in zip(BIG_SHARDED, gathered) if n != 'w_in'}
    full['w_in'] = _relayout_w_in(gathered[BIG_SHARDED.index('w_in')], f"relayout_w_in{i}")
    return full


def _layer_grad_blocks(g, i):
    return [_relayout_g_in(g[n], f"relayout_g_in{i}") if n == 'w_in' else _split8(g[n], SHARD_AXIS[n] - 1) for n in BIG_SHARDED]


def kernel(x, norm_w, w_in, s5_lambda_re, s5_lambda_im, s5_b_re, s5_b_im, s5_c_re, s5_c_im, s5_d, s5_log_step, s5_w_glu, sgu_ln_w, sgu_ln_b, sgu_w, sgu_b, m2_conv_w, m2_conv_b, m2_dt_bias, m2_a_log, m2_d, m2_norm_w, sc_conv_w, merge_b, w_branch, w_out, final_norm_w, loss_target, m_norm_w, m_w_in, m_s5_lambda_re, m_s5_lambda_im, m_s5_b_re, m_s5_b_im, m_s5_c_re, m_s5_c_im, m_s5_d, m_s5_log_step, m_s5_w_glu, m_sgu_ln_w, m_sgu_ln_b, m_sgu_w, m_sgu_b, m_m2_conv_w, m_m2_conv_b, m_m2_dt_bias, m_m2_a_log, m_m2_d, m_m2_norm_w, m_sc_conv_w, m_merge_b, m_w_branch, m_w_out, m_final_norm_w, v_norm_w, v_w_in, v_s5_lambda_re, v_s5_lambda_im, v_s5_b_re, v_s5_b_im, v_s5_c_re, v_s5_c_im, v_s5_d, v_s5_log_step, v_s5_w_glu, v_sgu_ln_w, v_sgu_ln_b, v_sgu_w, v_sgu_b, v_m2_conv_w, v_m2_conv_b, v_m2_dt_bias, v_m2_a_log, v_m2_d, v_m2_norm_w, v_sc_conv_w, v_merge_b, v_w_branch, v_w_out, v_final_norm_w):
    loc = locals()
    p = {n: loc[n] for n in WEIGHTS}
    mom = {n: loc['m_' + n] for n in WEIGHTS}
    vel = {n: loc['v_' + n] for n in WEIGHTS}

    small_sizes = [p[n].size for n in SMALL_SHARDED]
    small_pack = _rows128(jnp.concatenate([p[n].reshape(-1) for n in SMALL_SHARDED]))
    gath0, tok = _xchg_start([p[n][0].astype(BF16) for n in BIG_SHARDED] + [small_pack], True, "gather0_start")
    gath1, tok = _xchg_start([(p[n][1] + tok[0, 0] if n == 's5_w_glu' else p[n][1]).astype(BF16) for n in BIG_SHARDED],
                             True, "gather1_start")
    later = dict(p, **{n: p[n] + tok[0, 0] for n in ('norm_w', 's5_log_step', 'sgu_b', 'm2_d')})
    preps = [_layer_prep(i, later) for i in range(DEPTH)]
    h0 = _rmsnorm_fwd(x[0], preps[0][0]['nw'], "rms_fwd0")
    got0 = _xchg_wait(gath0, True, tok + (preps[0][1] + preps[1][1] + h0[0, 0].astype(F32)), "gather0_wait")
    small_all = got0[-1].reshape(N_DEV, -1)
    small_full, off = {}, 0
    for n, sz in zip(SMALL_SHARDED, small_sizes):
        small_full[n] = _join8(small_all[:, off:off + sz].reshape((N_DEV,) + p[n].shape), SHARD_AXIS[n])
        off += sz
    full = [dict(_layer_weights(got0[:-1], 0), **{n: small_full[n][0] for n in SMALL_SHARDED})]

    saved, layer_g = [None] * DEPTH, [None] * DEPTH
    xs, saved[0] = _layer_fwd(x[0], h0, 0, preps[0][0], full[0])
    h1 = _rmsnorm_fwd(xs, preps[1][0]['nw'], "rms_fwd1")
    got1 = _xchg_wait(gath1, True, h1, "gather1_wait")
    full.append(dict(_layer_weights(got1, 1), **{n: small_full[n][1] for n in SMALL_SHARDED}))
    xs, saved[1] = _layer_fwd(xs, h1, 1, preps[1][0], full[1])
    loss_row, dx, dfw = _loss_head(xs, final_norm_w.reshape(1, D_MODEL), loss_target[0])
    loss = lax.psum(loss_row[0, 0], ("x", "y", "c"))
    scat = [None] * DEPTH

    def start_scatter(i):
        def start(g):
            scat[i], tok = _xchg_start(_layer_grad_blocks(g, i), False, f"scatter{i}_start")
            return tok
        return start

    dx, layer_g[1] = _layer_bwd(dx, 1, saved[1], full[1], start_scatter(1))
    dx, layer_g[0] = _layer_bwd(dx, 0, saved[0], full[0], start_scatter(0))
    grads = {n: jnp.stack([layer_g[i][n] for i in range(DEPTH)]) for n in SMALL_SHARDED + REPLICATED if n != 'final_norm_w'}
    grads['final_norm_w'] = dfw[0]

    out_g, out_d, out_m, out_v = {}, {}, {}, {}
    repl_rows = _pack_rows([grads[n] for n in REPLICATED], 8 * N_DEV)
    rr = repl_rows.shape[0] // N_DEV
    shard_rows = _pack_rows([_split8(grads[n], SHARD_AXIS[n]) for n in SMALL_SHARDED], 8, batched=True)
    rs = shard_rows.shape[1]
    small_g = jnp.concatenate([shard_rows, repl_rows.reshape(N_DEV, rr, LANES)], axis=1)
    small_sum = _slot_sum(_exchange([small_g], False, "scatter_small")[0], "sum_small")
    repl_all = _exchange([small_sum[rs:]], True, "gather_small")[0].reshape(N_DEV * rr, LANES)
    g_all = jnp.concatenate([small_sum[:rs], repl_all], axis=0)
    names = SMALL_SHARDED + REPLICATED
    packed = [jnp.concatenate([_pack_rows([d[n] for n in SMALL_SHARDED], 8), _pack_rows([d[n] for n in REPLICATED], 8 * N_DEV)],
                              axis=0) for d in (p, mom, vel)]
    res = _adamw([g_all[None]], *[t[None] for t in packed], g_all.shape[0], "adamw_small")
    for o, dst in zip(res, (out_g, out_d, out_m, out_v)):
        pieces = (_unpack_rows(o[0, :rs], [p[n].shape for n in SMALL_SHARDED])
                  + _unpack_rows(o[0, rs:], [p[n].shape for n in REPLICATED]))
        dst.update(zip(names, pieces))

    landed1 = _xchg_wait(scat[1], False, res[0], "scatter1_wait")
    landed0 = _xchg_wait(scat[0], False, landed1[0], "scatter0_wait")
    for k, n in enumerate(BIG_SHARDED):
        shp = p[n].shape
        c = shp[-1]
        r = p[n].size // (DEPTH * c)
        big = _adamw([t.reshape(N_DEV, r, c) for t in (landed0[k], landed1[k])],
                     *[d[n].reshape(DEPTH, r, c) for d in (p, mom, vel)],
                     {'w_in': 256, 'w_branch': 512, 'w_out': 128, 's5_w_glu': 64}[n], "adamw_" + n)
        out_g[n], out_d[n], out_m[n], out_v[n] = [o.reshape(shp) for o in big]
    return (loss, dx[None], *[out_g[n] for n in WEIGHTS], *[out_d[n] for n in WEIGHTS],
            *[out_m[n] for n in WEIGHTS], *[out_v[n] for n in WEIGHTS])
```

```python
import functools

import jax
import jax.numpy as jnp
import numpy as np
from jax import lax
from jax.experimental import pallas as pl
from jax.experimental.pallas import tpu as pltpu

F32 = jnp.float32
BF16 = jnp.bfloat16

N_DEV = 8
SEQ = 2048
D_MODEL = 1024
DEPTH = 2
BW = 512
N_BRANCH = 4
EPS = 1e-6
S5_GROUPS, S5_STATE, S5_P = 32, 64, 16
S5_CH = S5_GROUPS * S5_STATE
SGU_CHUNK, SGU_HEADS = 128, 8
M2_HEADS, M2_HEAD_DIM, M2_STATE, M2_CHUNK, M2_CONV = 8, 64, 128, 128, 4
M2_CONV_CH = 1024
SC_CONV = 3
IN_DIM = 10248
IN_PAD = 11264
C_MERGE = 0
C_S5U, C_S5G = 4096, 4608
C_M2X = 5120
C_SGU_U, C_SGU_V, C_SGU_G = 6144, 6656, 7168
C_M2Z, C_DT = 8192, 8704
C_SC = 9216
SHARD_IN = IN_DIM // N_DEV

ADAM_LR, ADAM_B1, ADAM_B2, ADAM_EPS, ADAM_WD, ADAM_STEP = 0.001, 0.9, 0.999, 1e-08, 0.01, 10

VMEM_LIMIT = 56 * 1024 * 1024
LANES = 128

MESH = pl.DeviceIdType.MESH


def _cparams(sem=None, **kw):
    return pltpu.CompilerParams(dimension_semantics=sem, vmem_limit_bytes=VMEM_LIMIT, **kw)


def _dg(a, b, ca, cb, precision=None):
    return lax.dot_general(a, b, (((ca,), (cb,)), ((), ())), precision=precision,
                           preferred_element_type=F32)


@functools.partial(jax.custom_vjp, nondiff_argnums=(2, 3))
def _bdot(a, b, ca, cb):
    return _dg(a.astype(BF16), b.astype(BF16), ca, cb)


def _bdot_fwd(a, b, ca, cb):
    return _bdot(a, b, ca, cb), (a, b)


def _bdot_bwd(ca, cb, res, g):
    a, b = res
    gb, ab, bb = g.astype(BF16), a.astype(BF16), b.astype(BF16)
    da = _dg(gb, bb, 1, 1 - cb) if ca == 1 else _dg(bb, gb, 1 - cb, 1)
    db = _dg(ab, gb, 1 - ca, 0) if cb == 0 else _dg(gb, ab, 0, 1 - ca)
    return da.astype(a.dtype), db.astype(b.dtype)


_bdot.defvjp(_bdot_fwd, _bdot_bwd)


def _rms(x, w):
    return x * lax.rsqrt(jnp.mean(x * x, axis=-1, keepdims=True) + EPS) * w


def _silu(x):
    return x * jax.nn.sigmoid(x)


def _gelu(x):
    return 0.5 * x * (1.0 + jnp.tanh(0.7978845608028654 * (x + 0.044715 * (x * x * x))))


def _softplus(x):
    return jnp.maximum(x, 0.0) + jnp.log1p(jnp.exp(-jnp.abs(x)))


def _shift_down(x, s):
    if s == 0:
        return x
    row = lax.broadcasted_iota(jnp.int32, x.shape, 0)
    return jnp.where(row >= s, pltpu.roll(x, s, 0), 0.0)


def _shift_up(x, s):
    if s == 0:
        return x
    n = x.shape[0]
    row = lax.broadcasted_iota(jnp.int32, x.shape, 0)
    return jnp.where(row < n - s, pltpu.roll(x, n - s, 0), 0.0)


def _matmul(a, b, ca, cb, out_dtype, tm, tn, tk, name, residual=None, after=None):
    m = a.shape[1 - ca]
    k = a.shape[ca]
    n = b.shape[1 - cb]
    assert b.shape[cb] == k and m % tm == 0 and n % tn == 0 and k % tk == 0
    nk = k // tk
    a_spec = pl.BlockSpec((tm, tk), lambda i, j, kk: (i, kk)) if ca == 1 else pl.BlockSpec((tk, tm), lambda i, j, kk: (kk, i))
    b_spec = pl.BlockSpec((tk, tn), lambda i, j, kk: (kk, j)) if cb == 0 else pl.BlockSpec((tn, tk), lambda i, j, kk: (j, kk))
    o_spec = pl.BlockSpec((tm, tn), lambda i, j, kk: (i, j))
    has_res = residual is not None

    def body(*refs):
        refs = refs[:2 + has_res] + refs[2 + has_res + (after is not None):]
        if has_res:
            a_ref, b_ref, r_ref, o_ref, acc = refs
        else:
            a_ref, b_ref, o_ref, acc = refs
        kk = pl.program_id(2)
        part = _dg(a_ref[...].astype(BF16), b_ref[...].astype(BF16), ca, cb)

        @pl.when(kk == 0)
        def _():
            acc[...] = part

        @pl.when(kk > 0)
        def _():
            acc[...] += part

        @pl.when(kk == nk - 1)
        def _():
            r = acc[...]
            if has_res:
                r = r + r_ref[...]
            o_ref[...] = r.astype(out_dtype)

    ins = [a, b] + ([residual] if has_res else []) + ([after] if after is not None else [])
    specs = [a_spec, b_spec] + ([o_spec] if has_res else []) + ([pl.BlockSpec(memory_space=pl.ANY)] if after is not None else [])
    return pl.pallas_call(
        body, name=name, grid=(m // tm, n // tn, nk), in_specs=specs, out_specs=o_spec,
        out_shape=jax.ShapeDtypeStruct((m, n), out_dtype),
        scratch_shapes=[pltpu.VMEM((tm, tn), F32)],
        compiler_params=_cparams(("parallel", "parallel", "arbitrary")),
    )(*ins)


ROW_TILE = 512


def _rmsnorm_fwd(x, w, name):
    def body(x_ref, w_ref, o_ref):
        o_ref[...] = _rms(x_ref[...], w_ref[...]).astype(BF16)

    return pl.pallas_call(
        body, name=name, grid=(SEQ // ROW_TILE,),
        in_specs=[pl.BlockSpec((ROW_TILE, D_MODEL), lambda i: (i, 0)), pl.BlockSpec((1, D_MODEL), lambda i: (0, 0))],
        out_specs=pl.BlockSpec((ROW_TILE, D_MODEL), lambda i: (i, 0)),
        out_shape=jax.ShapeDtypeStruct((SEQ, D_MODEL), BF16),
        compiler_params=_cparams(("parallel",)),
    )(x, w)


def _rmsnorm_bwd(x, w, dh, dres, name):
    def body(x_ref, w_ref, dh_ref, dres_ref, dx_ref, dw_ref):
        _, vjp = jax.vjp(_rms, x_ref[...], w_ref[...])
        dx, dw = vjp(dh_ref[...])
        dx_ref[...] = dx + dres_ref[...]

        @pl.when(pl.program_id(0) == 0)
        def _():
            dw_ref[...] = dw

        @pl.when(pl.program_id(0) > 0)
        def _():
            dw_ref[...] += dw

    tile = pl.BlockSpec((ROW_TILE, D_MODEL), lambda i: (i, 0))
    vec = pl.BlockSpec((1, D_MODEL), lambda i: (0, 0))
    return pl.pallas_call(
        body, name=name, grid=(SEQ // ROW_TILE,),
        in_specs=[tile, vec, tile, tile], out_specs=[tile, vec],
        out_shape=[jax.ShapeDtypeStruct((SEQ, D_MODEL), F32), jax.ShapeDtypeStruct((1, D_MODEL), F32)],
        compiler_params=_cparams(("arbitrary",)),
    )(x, w, dh, dres)


def _loss_head(x, w, target):
    def body(x_ref, w_ref, t_ref, loss_ref, dx_ref, dw_ref):
        tgt = t_ref[...]

        def f(xv, wv):
            err = _rms(xv, wv) - tgt
            return 0.5 * jnp.sum(jnp.mean(err * err, axis=-1))

        loss, vjp = jax.vjp(f, x_ref[...], w_ref[...])
        dx, dw = vjp(jnp.ones((), F32))
        dx_ref[...] = dx
        lrow = jnp.full((1, LANES), loss, F32)

        @pl.when(pl.program_id(0) == 0)
        def _():
            dw_ref[...] = dw
            loss_ref[...] = lrow

        @pl.when(pl.program_id(0) > 0)
        def _():
            dw_ref[...] += dw
            loss_ref[...] += lrow

    tile = pl.BlockSpec((ROW_TILE, D_MODEL), lambda i: (i, 0))
    vec = pl.BlockSpec((1, D_MODEL), lambda i: (0, 0))
    return pl.pallas_call(
        body, name="loss_head", grid=(SEQ // ROW_TILE,),
        in_specs=[tile, vec, tile], out_specs=[pl.BlockSpec((1, LANES), lambda i: (0, 0)), tile, vec],
        out_shape=[jax.ShapeDtypeStruct((1, LANES), F32), jax.ShapeDtypeStruct((SEQ, D_MODEL), F32),
                   jax.ShapeDtypeStruct((1, D_MODEL), F32)],
        compiler_params=_cparams(("arbitrary",)),
    )(x, w, target)


S5_T = 256


def _s5_post(ypre, gate, wglu):
    y = _gelu(ypre)
    y = y * jax.nn.sigmoid(_bdot(y, wglu, 1, 0))
    return y * _silu(gate)


def _s5_fwd(proj, bbre, bbim, cre, cim, a2, dvec, wglu, name):
    def body(u_ref, g_ref, bbre_ref, bbim_ref, cre_ref, cim_ref, a_ref, d_ref, wg_ref, o_ref, sre_ref, sim_ref, st):
        @pl.when(pl.program_id(0) == 0)
        def _():
            st[...] = jnp.zeros_like(st)

        u = u_ref[...]
        ub = u.astype(BF16)
        sre_ref[...] = _dg(ub, bbre_ref[...], 1, 0)
        sim_ref[...] = _dg(ub, bbim_ref[...], 1, 0)
        ar, ai = a_ref[0:1, :], a_ref[1:2, :]

        def step(t, carry):
            sr, si = carry
            nr = ar * sr - ai * si + sre_ref[pl.ds(t, 1), :]
            ni = ar * si + ai * sr + sim_ref[pl.ds(t, 1), :]
            sre_ref[pl.ds(t, 1), :] = nr
            sim_ref[pl.ds(t, 1), :] = ni
            return nr, ni

        sr, si = lax.fori_loop(0, S5_T, step, (st[0:1, :], st[1:2, :]), unroll=8)
        st[0:1, :] = sr
        st[1:2, :] = si
        ypre = (_dg(sre_ref[...].astype(BF16), cre_ref[...], 1, 0) - _dg(sim_ref[...].astype(BF16), cim_ref[...], 1, 0)
                + d_ref[...] * u)
        o_ref[...] = _s5_post(ypre, g_ref[...], wg_ref[...]).astype(BF16)

    full = lambda shape: pl.BlockSpec(shape, lambda c: (0, 0))
    return pl.pallas_call(
        body, name=name, grid=(SEQ // S5_T,),
        in_specs=[pl.BlockSpec((S5_T, BW), lambda c: (c, C_S5U // BW)), pl.BlockSpec((S5_T, BW), lambda c: (c, C_S5G // BW)),
                  full((BW, S5_CH)), full((BW, S5_CH)), full((S5_CH, BW)), full((S5_CH, BW)),
                  full((2, S5_CH)), full((1, BW)), full((BW, BW))],
        out_specs=[pl.BlockSpec((S5_T, BW), lambda c: (c, 0)), pl.BlockSpec((S5_T, S5_CH), lambda c: (c, 0)),
                   pl.BlockSpec((S5_T, S5_CH), lambda c: (c, 0))],
        out_shape=[jax.ShapeDtypeStruct((SEQ, BW), BF16), jax.ShapeDtypeStruct((SEQ, S5_CH), F32),
                   jax.ShapeDtypeStruct((SEQ, S5_CH), F32)],
        scratch_shapes=[pltpu.VMEM((2, S5_CH), F32)],
        compiler_params=_cparams(("arbitrary",)),
    )(proj, proj, bbre, bbim, cre, cim, a2, dvec, wglu)


def _s5_bwd(proj, dproj, dout, sre, sim, bbre, bbim, cre, cim, a2, dvec, wglu, name):
    nc = SEQ // S5_T

    def body(u_ref, g_ref, do_ref, sre_ref, sim_ref, pre_ref, pim_ref, bbre_ref, bbim_ref, cre_ref, cim_ref, a_ref,
             d_ref, wg_ref, dproj_in, dp_ref, dbbre_ref, dbbim_ref, dcre_ref, dcim_ref, da_ref, dd_ref, dwg_ref,
             gre, gim, st):
        c = nc - 1 - pl.program_id(0)

        @pl.when(pl.program_id(0) == 0)
        def _():
            st[...] = jnp.zeros_like(st)
            for r in (dbbre_ref, dbbim_ref, dcre_ref, dcim_ref, da_ref, dd_ref, dwg_ref):
                r[...] = jnp.zeros_like(r)

        u = u_ref[...]
        s_re, s_im = sre_ref[...], sim_ref[...]

        def head(s_re, s_im, cre_v, cim_v, dv, uv, gv, wg):
            ypre = _bdot(s_re, cre_v, 1, 0) - _bdot(s_im, cim_v, 1, 0) + dv * uv
            return _s5_post(ypre, gv, wg)

        _, vjp = jax.vjp(head, s_re, s_im, cre_ref[...].astype(F32), cim_ref[...].astype(F32), d_ref[...], u,
                         g_ref[...], wg_ref[...].astype(F32))
        ds_re, ds_im, dcre, dcim, dd, du_d, dgate, dwg = vjp(do_ref[0])
        dcre_ref[...] += dcre
        dcim_ref[...] += dcim
        dd_ref[...] += dd
        dwg_ref[...] += dwg
        dp_ref[:, BW:] = dgate
        gre[...] = ds_re
        gim[...] = ds_im
        ar, ai = a_ref[0:1, :], a_ref[1:2, :]

        def step(i, carry):
            t = S5_T - 1 - i
            gr, gi = carry
            nr = gre[pl.ds(t, 1), :] + gr
            ni = gim[pl.ds(t, 1), :] + gi
            gre[pl.ds(t, 1), :] = nr
            gim[pl.ds(t, 1), :] = ni
            return ar * nr + ai * ni, ar * ni - ai * nr

        gr, gi = lax.fori_loop(0, S5_T, step, (st[0:1, :], st[1:2, :]), unroll=8)
        st[0:1, :] = gr
        st[1:2, :] = gi
        g_re, g_im = gre[...], gim[...]
        first = jnp.where(c > 0, 1.0, 0.0)
        row = lax.broadcasted_iota(jnp.int32, (S5_T, S5_CH), 0)
        p_re = jnp.where(row == 0, pre_ref[7:8, :] * first, pltpu.roll(s_re, 1, 0))
        p_im = jnp.where(row == 0, pim_ref[7:8, :] * first, pltpu.roll(s_im, 1, 0))
        da_ref[0:1, :] += jnp.sum(g_re * p_re + g_im * p_im, axis=0, keepdims=True)
        da_ref[1:2, :] += jnp.sum(g_im * p_re - g_re * p_im, axis=0, keepdims=True)
        ub, grb, gib = u.astype(BF16), g_re.astype(BF16), g_im.astype(BF16)
        dbbre_ref[...] += _dg(ub, grb, 0, 0)
        dbbim_ref[...] += _dg(ub, gib, 0, 0)
        dp_ref[:, :BW] = du_d + _dg(grb, bbre_ref[...], 1, 1) + _dg(gib, bbim_ref[...], 1, 1)

    full = lambda shape: pl.BlockSpec(shape, lambda i: (0, 0))
    rev = lambda w, col=0: pl.BlockSpec((S5_T, w), lambda i: (nc - 1 - i, col))
    prev = pl.BlockSpec((8, S5_CH), lambda i: (jnp.maximum((nc - 1 - i) * (S5_T // 8) - 1, 0), 0))
    return pl.pallas_call(
        body, name=name, grid=(nc,),
        in_specs=[rev(BW, C_S5U // BW), rev(BW, C_S5G // BW), pl.BlockSpec((1, S5_T, BW), lambda i: (0, nc - 1 - i, 0)),
                  rev(S5_CH), rev(S5_CH), prev, prev,
                  full((BW, S5_CH)), full((BW, S5_CH)), full((S5_CH, BW)), full((S5_CH, BW)),
                  full((2, S5_CH)), full((1, BW)), full((BW, BW)), pl.BlockSpec(memory_space=pl.ANY)],
        out_specs=[rev(2 * BW, C_S5U // (2 * BW)), full((BW, S5_CH)), full((BW, S5_CH)), full((S5_CH, BW)), full((S5_CH, BW)),
                   full((2, S5_CH)), full((1, BW)), full((BW, BW))],
        input_output_aliases={14: 0},
        out_shape=[jax.ShapeDtypeStruct((SEQ, IN_PAD), F32),
                   jax.ShapeDtypeStruct((BW, S5_CH), F32), jax.ShapeDtypeStruct((BW, S5_CH), F32),
                   jax.ShapeDtypeStruct((S5_CH, BW), F32), jax.ShapeDtypeStruct((S5_CH, BW), F32),
                   jax.ShapeDtypeStruct((2, S5_CH), F32), jax.ShapeDtypeStruct((1, BW), F32),
                   jax.ShapeDtypeStruct((BW, BW), F32)],
        scratch_shapes=[pltpu.VMEM((S5_T, S5_CH), F32), pltpu.VMEM((S5_T, S5_CH), F32), pltpu.VMEM((2, S5_CH), F32)],
        compiler_params=_cparams(("arbitrary",)),
    )(proj, proj, dout, sre, sim, sre, sim, bbre, bbim, cre, cim, a2, dvec, wglu, dproj)


def _s5_disc(lam_re, lam_im, b_re, b_im, c_re, c_im, d, log_step):
    step = jnp.exp(log_step)[:, None]
    mag = jnp.exp(lam_re * step)
    ab_re, ab_im = mag * jnp.cos(lam_im * step), mag * jnp.sin(lam_im * step)
    den = lam_re * lam_re + lam_im * lam_im
    nr = ab_re - 1.0
    coef_re = (nr * lam_re + ab_im * lam_im) / den
    coef_im = (ab_im * lam_re - nr * lam_im) / den
    bb_re = coef_re[..., None] * b_re - coef_im[..., None] * b_im
    bb_im = coef_re[..., None] * b_im + coef_im[..., None] * b_re
    def block_diag(t, rows_per, cols_per):
        wide = jnp.tile(t.reshape(S5_GROUPS * rows_per, cols_per), (1, S5_GROUPS))
        r = lax.broadcasted_iota(jnp.int32, wide.shape, 0) // rows_per
        c = lax.broadcasted_iota(jnp.int32, wide.shape, 1) // cols_per
        return jnp.where(r == c, wide, 0.0)

    bbre = block_diag(jnp.swapaxes(bb_re, 1, 2), S5_P, S5_STATE)
    bbim = block_diag(jnp.swapaxes(bb_im, 1, 2), S5_P, S5_STATE)
    cre = block_diag(jnp.swapaxes(c_re, 1, 2), S5_STATE, S5_P)
    cim = block_diag(jnp.swapaxes(c_im, 1, 2), S5_STATE, S5_P)
    a2 = jnp.stack([ab_re.reshape(-1), ab_im.reshape(-1)])
    return bbre, bbim, cre, cim, a2, d.reshape(1, BW)


def _left_lanes(shape):
    return lax.broadcasted_iota(jnp.int32, shape, 1) < 64


def _sgu_chunk(u, v, gate, ln_w, ln_b, w, bias):
    u32, v32 = _gelu(u), _gelu(v)
    mu = jnp.mean(v32, axis=-1, keepdims=True)
    var = jnp.mean(jnp.square(v32 - mu), axis=-1, keepdims=True)
    vn = (v32 - mu) * lax.rsqrt(var + EPS) * ln_w + ln_b
    t_i = lax.broadcasted_iota(jnp.int32, (SGU_CHUNK, SGU_CHUNK), 0)
    s_i = lax.broadcasted_iota(jnp.int32, (SGU_CHUNK, SGU_CHUNK), 1)
    causal = t_i >= s_i
    left = _left_lanes((SGU_CHUNK, LANES))
    sgate = _silu(gate)
    outs = []
    for j in range(BW // LANES):
        vb = vn[:, j * LANES:(j + 1) * LANES]
        s_blk = (_bdot(jnp.where(causal, w[2 * j], 0.0), jnp.where(left, vb, 0.0), 1, 0)
                 + _bdot(jnp.where(causal, w[2 * j + 1], 0.0), jnp.where(left, 0.0, vb), 1, 0))
        sl = slice(j * LANES, (j + 1) * LANES)
        outs.append(u32[:, sl] * (s_blk + bias[:, sl]) * sgate[:, sl])
    return outs


def _sgu_fwd(proj, ln_w, ln_b, w, bias, name):
    def body(u_ref, v_ref, g_ref, lw_ref, lb_ref, w_ref, b_ref, o_ref):
        outs = _sgu_chunk(u_ref[...], v_ref[...], g_ref[...], lw_ref[...], lb_ref[...], w_ref[...], b_ref[...])
        for j, o in enumerate(outs):
            o_ref[:, j * LANES:(j + 1) * LANES] = o.astype(BF16)

    blk = lambda col: pl.BlockSpec((SGU_CHUNK, BW), lambda c: (c, col // BW))
    vec = pl.BlockSpec((1, BW), lambda c: (0, 0))
    return pl.pallas_call(
        body, name=name, grid=(SEQ // SGU_CHUNK,),
        in_specs=[blk(C_SGU_U), blk(C_SGU_V), blk(C_SGU_G), vec, vec,
                  pl.BlockSpec((SGU_HEADS, SGU_CHUNK, SGU_CHUNK), lambda c: (0, 0, 0)),
                  pl.BlockSpec((SGU_CHUNK, BW), lambda c: (0, 0))],
        out_specs=pl.BlockSpec((SGU_CHUNK, BW), lambda c: (c, 0)),
        out_shape=jax.ShapeDtypeStruct((SEQ, BW), BF16),
        compiler_params=_cparams(("parallel",)),
    )(proj, proj, proj, ln_w, ln_b, w, bias)


def _sgu_bwd(proj, dproj, dout, ln_w, ln_b, w, bias, name):
    def body(u_ref, v_ref, g_ref, do_ref, lw_ref, lb_ref, w_ref, b_ref, dproj_in, dp_ref, dlw_ref, dlb_ref, dw_ref, db_ref):
        _, vjp = jax.vjp(_sgu_chunk, u_ref[...], v_ref[...], g_ref[...], lw_ref[...], lb_ref[...], w_ref[...], b_ref[...])
        do = do_ref[0]
        du, dv, dgate, dlw, dlb, dw, db = vjp([do[:, j * LANES:(j + 1) * LANES] for j in range(BW // LANES)])
        dp_ref[:, 0:BW] = du
        dp_ref[:, BW:2 * BW] = dv
        dp_ref[:, 2 * BW:3 * BW] = dgate
        dp_ref[:, 3 * BW:] = jnp.zeros((SGU_CHUNK, BW), F32)

        @pl.when(pl.program_id(0) == 0)
        def _():
            dlw_ref[...] = dlw
            dlb_ref[...] = dlb
            dw_ref[...] = dw
            db_ref[...] = db

        @pl.when(pl.program_id(0) > 0)
        def _():
            dlw_ref[...] += dlw
            dlb_ref[...] += dlb
            dw_ref[...] += dw
            db_ref[...] += db

    blk = lambda col: pl.BlockSpec((SGU_CHUNK, BW), lambda c: (c, col // BW))
    vec = pl.BlockSpec((1, BW), lambda c: (0, 0))
    wsp = pl.BlockSpec((SGU_HEADS, SGU_CHUNK, SGU_CHUNK), lambda c: (0, 0, 0))
    bsp = pl.BlockSpec((SGU_CHUNK, BW), lambda c: (0, 0))
    return pl.pallas_call(
        body, name=name, grid=(SEQ // SGU_CHUNK,),
        in_specs=[blk(C_SGU_U), blk(C_SGU_V), blk(C_SGU_G), pl.BlockSpec((1, SGU_CHUNK, BW), lambda c: (1, c, 0)),
                  vec, vec, wsp, bsp, pl.BlockSpec(memory_space=pl.ANY)],
        out_specs=[pl.BlockSpec((SGU_CHUNK, 4 * BW), lambda c: (c, C_SGU_U // (4 * BW))), vec, vec, wsp, bsp],
        input_output_aliases={8: 0},
        out_shape=[jax.ShapeDtypeStruct((SEQ, IN_PAD), F32), jax.ShapeDtypeStruct((1, BW), F32),
                   jax.ShapeDtypeStruct((1, BW), F32), jax.ShapeDtypeStruct((SGU_HEADS, SGU_CHUNK, SGU_CHUNK), F32),
                   jax.ShapeDtypeStruct((SGU_CHUNK, BW), F32)],
        compiler_params=_cparams(("arbitrary",)),
    )(proj, proj, proj, dout, ln_w, ln_b, w, bias, dproj)


CONV_BLK = 256


def _m2_conv_fwd(proj, w, b, name):
    def body(x_ref, w_ref, b_ref, o_ref):
        x = x_ref[...]
        acc = jnp.zeros_like(x) + b_ref[...]
        for k in range(M2_CONV):
            acc = acc + w_ref[k:k + 1, :] * _shift_down(x, M2_CONV - 1 - k)
        o_ref[...] = _silu(acc)

    return pl.pallas_call(
        body, name=name, grid=(M2_CONV_CH // CONV_BLK,),
        in_specs=[pl.BlockSpec((SEQ, CONV_BLK), lambda j: (0, C_M2X // CONV_BLK + j)),
                  pl.BlockSpec((M2_CONV, CONV_BLK), lambda j: (0, j)), pl.BlockSpec((1, CONV_BLK), lambda j: (0, j))],
        out_specs=pl.BlockSpec((SEQ, CONV_BLK), lambda j: (0, j)),
        out_shape=jax.ShapeDtypeStruct((SEQ, M2_CONV_CH), F32),
        compiler_params=_cparams(("parallel",)),
    )(proj, w, b)


def _m2_conv_bwd(proj, dproj, dxa, w, b, name):
    def body(x_ref, d_ref, w_ref, b_ref, dproj_in, dx_ref, dw_ref, db_ref):
        x = x_ref[...]
        xs = [_shift_down(x, M2_CONV - 1 - k) for k in range(M2_CONV)]
        acc = jnp.zeros_like(x) + b_ref[...]
        for k in range(M2_CONV):
            acc = acc + w_ref[k:k + 1, :] * xs[k]
        sg = jax.nn.sigmoid(acc)
        dacc = d_ref[...] * (sg * (1.0 + acc * (1.0 - sg)))
        dx = jnp.zeros_like(x)
        for k in range(M2_CONV):
            dx = dx + w_ref[k:k + 1, :] * _shift_up(dacc, M2_CONV - 1 - k)
            dw_ref[k:k + 1, :] = jnp.sum(dacc * xs[k], axis=0, keepdims=True)
        dx_ref[...] = dx
        db_ref[...] = jnp.sum(dacc, axis=0, keepdims=True)

    return pl.pallas_call(
        body, name=name, grid=(M2_CONV_CH // CONV_BLK,),
        in_specs=[pl.BlockSpec((SEQ, CONV_BLK), lambda j: (0, C_M2X // CONV_BLK + j)),
                  pl.BlockSpec((SEQ, CONV_BLK), lambda j: (0, j)),
                  pl.BlockSpec((M2_CONV, CONV_BLK), lambda j: (0, j)), pl.BlockSpec((1, CONV_BLK), lambda j: (0, j)),
                  pl.BlockSpec(memory_space=pl.ANY)],
        out_specs=[pl.BlockSpec((SEQ, CONV_BLK), lambda j: (0, C_M2X // CONV_BLK + j)),
                   pl.BlockSpec((M2_CONV, CONV_BLK), lambda j: (0, j)), pl.BlockSpec((1, CONV_BLK), lambda j: (0, j))],
        input_output_aliases={4: 0},
        out_shape=[jax.ShapeDtypeStruct((SEQ, IN_PAD), F32), jax.ShapeDtypeStruct((M2_CONV, M2_CONV_CH), F32),
                   jax.ShapeDtypeStruct((1, M2_CONV_CH), F32)],
        compiler_params=_cparams(("parallel",)),
    )(proj, dxa, w, b, dproj)


N_PAIR = M2_HEADS // 2
HI = lax.Precision.HIGHEST


def _col(a, h):
    lane = lax.broadcasted_iota(jnp.int32, a.shape, 1)
    return jnp.sum(jnp.where(lane == h, a, 0.0), axis=1, keepdims=True)


def _row(a, h):
    sub = lax.broadcasted_iota(jnp.int32, a.shape, 0)
    return jnp.sum(jnp.where(sub == h, a, 0.0), axis=0, keepdims=True)


def _ssd_chunk(xs, bms, cms, dtr, zs, states, dt_bias, a_log, dfs, nws):
    q = M2_CHUNK
    dt = _softplus(dtr + dt_bias)
    da = dt * (-jnp.exp(a_log))
    l_i = lax.broadcasted_iota(jnp.int32, (q, q), 0)
    s_i = lax.broadcasted_iota(jnp.int32, (q, q), 1)
    causal = l_i >= s_i
    tril = jnp.where(causal, 1.0, 0.0)
    a_cs = _dg(tril, da, 1, 0, HI)
    a_cs_t = _dg(da, tril, 0, 1, HI)
    a_end = _row(a_cs, q - 1)
    left = _left_lanes((q, LANES))
    left1 = _left_lanes((1, LANES))
    ys, nexts = [], []
    for j in range(N_PAIR):
        grp = j // 2
        bm, cm = bms[grp], cms[grp]
        h0, h1 = 2 * j, 2 * j + 1
        cb = _bdot(cm, bm, 1, 1)
        xdt = xs[j] * jnp.where(left, _col(dt, h0), _col(dt, h1))
        acs0, acs1 = _col(a_cs, h0), _col(a_cs, h1)
        y = _bdot(cm, states[j], 1, 0) * jnp.where(left, jnp.exp(acs0), jnp.exp(acs1))
        s_new = states[j] * jnp.where(left1, jnp.exp(_col(a_end, h0)), jnp.exp(_col(a_end, h1)))
        for h, acs, xh in ((h0, acs0, jnp.where(left, xdt, 0.0)), (h1, acs1, jnp.where(left, 0.0, xdt))):
            decay = jnp.exp(jnp.where(causal, acs - _row(a_cs_t, h), -jnp.inf))
            y = y + _bdot(cb * decay, xh, 1, 0)
            s_new = s_new + _bdot(bm * jnp.exp(_col(a_end, h) - acs), xh, 0, 0)
        ys.append((y + dfs[j] * xs[j]) * _silu(zs[j]))
        nexts.append(s_new)
    ssq = sum(jnp.sum(y * y, axis=-1, keepdims=True) for y in ys)
    scale = lax.rsqrt(ssq / BW + EPS)
    return [y * scale * nw for y, nw in zip(ys, nws)], nexts


def _blocks(ref, n, width=LANES):
    return [ref[:, j * width:(j + 1) * width] for j in range(n)]


def _ssd_fwd(proj, xa, dt_bias, a_log, dfull, nw, name):
    nc = SEQ // M2_CHUNK

    def body(x_ref, b_ref, c_ref, dt_ref, z_ref, dtb_ref, al_ref, df_ref, nw_ref, o_ref, sin_ref, st):
        @pl.when(pl.program_id(0) == 0)
        def _():
            st[...] = jnp.zeros_like(st)

        states = [st[j] for j in range(N_PAIR)]
        for j in range(N_PAIR):
            sin_ref[0, j] = states[j]
        ys, nexts = _ssd_chunk(_blocks(x_ref, 4), _blocks(b_ref, 2), _blocks(c_ref, 2), dt_ref[...], _blocks(z_ref, 4),
                               states, dtb_ref[...], al_ref[...], _blocks(df_ref, 4), _blocks(nw_ref, 4))
        for j in range(N_PAIR):
            o_ref[:, j * LANES:(j + 1) * LANES] = ys[j].astype(BF16)
            st[j] = nexts[j]

    vec8 = pl.BlockSpec((1, LANES), lambda c: (0, 0))
    vec = pl.BlockSpec((1, BW), lambda c: (0, 0))
    return pl.pallas_call(
        body, name=name, grid=(nc,),
        in_specs=[pl.BlockSpec((M2_CHUNK, BW), lambda c: (c, 0)), pl.BlockSpec((M2_CHUNK, 256), lambda c: (c, 2)),
                  pl.BlockSpec((M2_CHUNK, 256), lambda c: (c, 3)), pl.BlockSpec((M2_CHUNK, LANES), lambda c: (c, C_DT // LANES)),
                  pl.BlockSpec((M2_CHUNK, BW), lambda c: (c, C_M2Z // BW)), vec8, vec8, vec, vec],
        out_specs=[pl.BlockSpec((M2_CHUNK, BW), lambda c: (c, 0)),
                   pl.BlockSpec((1, N_PAIR, M2_STATE, LANES), lambda c: (c, 0, 0, 0))],
        out_shape=[jax.ShapeDtypeStruct((SEQ, BW), BF16), jax.ShapeDtypeStruct((nc, N_PAIR, M2_STATE, LANES), F32)],
        scratch_shapes=[pltpu.VMEM((N_PAIR, M2_STATE, LANES), F32)],
        compiler_params=_cparams(("arbitrary",)),
    )(xa, xa, xa, proj, proj, dt_bias, a_log, dfull, nw)


def _ssd_bwd(proj, dproj, xa, dout, s_in, dt_bias, a_log, dfull, nw, name):
    nc = SEQ // M2_CHUNK

    def body(x_ref, b_ref, c_ref, dt_ref, z_ref, do_ref, sin_ref, dtb_ref, al_ref, df_ref, nw_ref, dproj_in,
             dp_ref, dxa_ref, ddtb_ref, dal_ref, ddf_ref, dnw_ref, dst):
        @pl.when(pl.program_id(0) == 0)
        def _():
            dst[...] = jnp.zeros_like(dst)
            for r in (ddtb_ref, dal_ref, ddf_ref, dnw_ref):
                r[...] = jnp.zeros_like(r)

        states = [sin_ref[0, j] for j in range(N_PAIR)]
        _, vjp = jax.vjp(_ssd_chunk, _blocks(x_ref, 4), _blocks(b_ref, 2), _blocks(c_ref, 2), dt_ref[...],
                         _blocks(z_ref, 4), states, dtb_ref[...], al_ref[...], _blocks(df_ref, 4), _blocks(nw_ref, 4))
        dxs, dbs, dcs, ddt, dzs, dstates, ddtb, dal, ddfs, dnws = vjp(
            ([do_ref[0, :, j * LANES:(j + 1) * LANES] for j in range(N_PAIR)], [dst[j] for j in range(N_PAIR)]))
        for j in range(N_PAIR):
            sl = slice(j * LANES, (j + 1) * LANES)
            dxa_ref[:, sl] = dxs[j]
            dp_ref[:, sl] = dzs[j]
            dst[j] = dstates[j]
            ddf_ref[:, sl] += ddfs[j]
            dnw_ref[:, sl] += dnws[j]
        for g in range(2):
            dxa_ref[:, BW + g * LANES:BW + (g + 1) * LANES] = dbs[g]
            dxa_ref[:, BW + 256 + g * LANES:BW + 256 + (g + 1) * LANES] = dcs[g]
        dp_ref[:, BW:BW + LANES] = ddt
        dp_ref[:, BW + LANES:] = jnp.zeros((M2_CHUNK, 2 * BW - BW - LANES), F32)
        ddtb_ref[...] += ddtb
        dal_ref[...] += dal

    rev = lambda w, col=0: pl.BlockSpec((M2_CHUNK, w), lambda i: (nc - 1 - i, col))
    vec8 = pl.BlockSpec((1, LANES), lambda i: (0, 0))
    vec = pl.BlockSpec((1, BW), lambda i: (0, 0))
    return pl.pallas_call(
        body, name=name, grid=(nc,),
        in_specs=[rev(BW), rev(256, 2), rev(256, 3), rev(LANES, C_DT // LANES), rev(BW, C_M2Z // BW),
                  pl.BlockSpec((1, M2_CHUNK, BW), lambda i: (2, nc - 1 - i, 0)),
                  pl.BlockSpec((1, N_PAIR, M2_STATE, LANES), lambda i: (nc - 1 - i, 0, 0, 0)), vec8, vec8, vec, vec,
                  pl.BlockSpec(memory_space=pl.ANY)],
        out_specs=[rev(2 * BW, C_M2Z // (2 * BW)), rev(M2_CONV_CH), vec8, vec8, vec, vec],
        input_output_aliases={11: 0},
        out_shape=[jax.ShapeDtypeStruct((SEQ, IN_PAD), F32), jax.ShapeDtypeStruct((SEQ, M2_CONV_CH), F32),
                   jax.ShapeDtypeStruct((1, LANES), F32), jax.ShapeDtypeStruct((1, LANES), F32),
                   jax.ShapeDtypeStruct((1, BW), F32), jax.ShapeDtypeStruct((1, BW), F32)],
        scratch_shapes=[pltpu.VMEM((N_PAIR, M2_STATE, LANES), F32)],
        compiler_params=_cparams(("arbitrary",)),
    )(xa, xa, xa, proj, proj, dout, s_in, dt_bias, a_log, dfull, nw, dproj)


def _sc_specs():
    col = lambda kind: pl.BlockSpec((SEQ, LANES), lambda j: (0, C_SC // LANES + 4 * j + kind))
    return [col(0), col(1), col(2), col(3)]


def _sc_fwd(proj, w, name):
    def body(b_ref, c_ref, h_ref, g_ref, w_ref, o_ref):
        ch = c_ref[...] * h_ref[...]
        acc = jnp.zeros_like(ch)
        for k in range(SC_CONV):
            acc = acc + w_ref[k:k + 1, :] * _shift_down(ch, SC_CONV - 1 - k)
        o_ref[...] = (b_ref[...] * acc * _silu(g_ref[...])).astype(BF16)

    return pl.pallas_call(
        body, name=name, grid=(BW // LANES,),
        in_specs=_sc_specs() + [pl.BlockSpec((SC_CONV, LANES), lambda j: (0, j))],
        out_specs=pl.BlockSpec((SEQ, LANES), lambda j: (0, j)),
        out_shape=jax.ShapeDtypeStruct((SEQ, BW), BF16),
        compiler_params=_cparams(("parallel",)),
    )(proj, proj, proj, proj, w)


def _sc_bwd(proj, dproj, dout, w, name):
    def body(b_ref, c_ref, h_ref, g_ref, do_ref, w_ref, dproj_in, dp_ref, dw_ref):
        cv, hv, gv = c_ref[...], h_ref[...], g_ref[...]
        ch = cv * hv
        chs = [_shift_down(ch, SC_CONV - 1 - k) for k in range(SC_CONV)]
        acc = jnp.zeros_like(ch)
        for k in range(SC_CONV):
            acc = acc + w_ref[k:k + 1, :] * chs[k]
        sg = jax.nn.sigmoid(gv)
        do = do_ref[0]
        bv = b_ref[...]
        dp_ref[:, 0:LANES] = do * acc * (gv * sg)
        dp_ref[:, 3 * LANES:] = do * bv * acc * (sg * (1.0 + gv * (1.0 - sg)))
        dacc = do * bv * (gv * sg)
        dch = jnp.zeros_like(ch)
        for k in range(SC_CONV):
            dch = dch + w_ref[k:k + 1, :] * _shift_up(dacc, SC_CONV - 1 - k)
            dw_ref[k:k + 1, :] = jnp.sum(dacc * chs[k], axis=0, keepdims=True)
        dp_ref[:, LANES:2 * LANES] = dch * hv
        dp_ref[:, 2 * LANES:3 * LANES] = dch * cv

    wsp = pl.BlockSpec((SC_CONV, LANES), lambda j: (0, j))
    return pl.pallas_call(
        body, name=name, grid=(BW // LANES,),
        in_specs=_sc_specs() + [pl.BlockSpec((1, SEQ, LANES), lambda j: (3, 0, j)), wsp, pl.BlockSpec(memory_space=pl.ANY)],
        out_specs=[pl.BlockSpec((SEQ, 4 * LANES), lambda j: (0, C_SC // (4 * LANES) + j)), wsp],
        input_output_aliases={6: 0},
        out_shape=[jax.ShapeDtypeStruct((SEQ, IN_PAD), F32), jax.ShapeDtypeStruct((SC_CONV, BW), F32)],
        compiler_params=_cparams(("parallel",)),
    )(proj, proj, proj, proj, dout, w, dproj)


MERGE_T = 256
MERGE_BWD_T = 512


def _merge_fwd(proj, ys, merge_b, w_branch, name):
    def body(y_ref, lg_ref, b_ref, w_ref, o_ref):
        acc = jnp.zeros((MERGE_T, D_MODEL), F32)
        for k in range(N_BRANCH):
            gate = jax.nn.sigmoid(lg_ref[:, k * D_MODEL:(k + 1) * D_MODEL] + b_ref[k])
            acc = acc + gate * _dg(y_ref[k], w_ref[k], 1, 0)
        o_ref[...] = acc.astype(BF16)

    return pl.pallas_call(
        body, name=name, grid=(SEQ // MERGE_T,),
        in_specs=[pl.BlockSpec((N_BRANCH, MERGE_T, BW), lambda i: (0, i, 0)),
                  pl.BlockSpec((MERGE_T, N_BRANCH * D_MODEL), lambda i: (i, C_MERGE // (N_BRANCH * D_MODEL))),
                  pl.BlockSpec((N_BRANCH, 1, D_MODEL), lambda i: (0, 0, 0)),
                  pl.BlockSpec((N_BRANCH, BW, D_MODEL), lambda i: (0, 0, 0))],
        out_specs=pl.BlockSpec((MERGE_T, D_MODEL), lambda i: (i, 0)),
        out_shape=jax.ShapeDtypeStruct((SEQ, D_MODEL), BF16),
        compiler_params=_cparams(("parallel",)),
    )(ys, proj, merge_b, w_branch)


def _merge_bwd(proj, ys, dm, merge_b, w_branch, name):
    nt = SEQ // MERGE_BWD_T

    def body(y_ref, lg_ref, dm_ref, b_ref, w_ref, dy_ref, dlg_ref, dw_ref, db_ref, dw_acc):
        i = pl.program_id(1)
        gate = jax.nn.sigmoid(lg_ref[...] + b_ref[0])
        y = y_ref[0]
        dmv = dm_ref[...]
        dbo = (gate * dmv).astype(BF16)
        dlg = _dg(y, w_ref[0], 1, 0) * dmv * gate * (1.0 - gate)
        dlg_ref[...] = dlg
        dy_ref[0] = _dg(dbo, w_ref[0], 1, 1)
        dwp = _dg(y, dbo, 0, 0)
        dbp = jnp.sum(dlg, axis=0, keepdims=True)

        @pl.when(i == 0)
        def _():
            dw_acc[...] = dwp
            db_ref[0] = dbp

        @pl.when(i > 0)
        def _():
            dw_acc[...] += dwp
            db_ref[0] += dbp

        @pl.when(i == nt - 1)
        def _():
            dw_ref[0] = dw_acc[...].astype(BF16)

    return pl.pallas_call(
        body, name=name, grid=(N_BRANCH, nt),
        in_specs=[pl.BlockSpec((1, MERGE_BWD_T, BW), lambda k, i: (k, i, 0)),
                  pl.BlockSpec((MERGE_BWD_T, D_MODEL), lambda k, i: (i, C_MERGE // D_MODEL + k)),
                  pl.BlockSpec((MERGE_BWD_T, D_MODEL), lambda k, i: (i, 0)),
                  pl.BlockSpec((1, 1, D_MODEL), lambda k, i: (k, 0, 0)),
                  pl.BlockSpec((1, BW, D_MODEL), lambda k, i: (k, 0, 0))],
        out_specs=[pl.BlockSpec((1, MERGE_BWD_T, BW), lambda k, i: (k, i, 0)),
                   pl.BlockSpec((MERGE_BWD_T, D_MODEL), lambda k, i: (i, k)),
                   pl.BlockSpec((1, BW, D_MODEL), lambda k, i: (k, 0, 0)),
                   pl.BlockSpec((1, 1, D_MODEL), lambda k, i: (k, 0, 0))],
        out_shape=[jax.ShapeDtypeStruct((N_BRANCH, SEQ, BW), F32), jax.ShapeDtypeStruct((SEQ, IN_PAD), F32),
                   jax.ShapeDtypeStruct((N_BRANCH, BW, D_MODEL), BF16), jax.ShapeDtypeStruct((N_BRANCH, 1, D_MODEL), F32)],
        scratch_shapes=[pltpu.VMEM((BW, D_MODEL), F32)],
        compiler_params=_cparams(("parallel", "arbitrary")),
    )(ys, proj, dm, merge_b, w_branch)


def _adamw(glist, w, m, v, rows, name):
    nl = len(glist)
    n, r, c = glist[0].shape
    assert w.shape == (nl, r, c) and r % rows == 0
    nb = r // rows

    def body(*refs):
        g_refs = refs[:nl]
        w_ref, m_ref, v_ref, go_ref, d_ref, mo_ref, vo_ref = refs[nl:]
        for layer in range(nl):
            @pl.when(pl.program_id(0) == layer)
            def _(g_ref=g_refs[layer]):
                g = g_ref[0].astype(F32)
                for s in range(1, n):
                    g = g + g_ref[s].astype(F32)
                mn = ADAM_B1 * m_ref[0] + (1.0 - ADAM_B1) * g
                vn = ADAM_B2 * v_ref[0] + (1.0 - ADAM_B2) * jnp.square(g)
                m_hat = mn / (1.0 - ADAM_B1 ** ADAM_STEP)
                v_hat = vn / (1.0 - ADAM_B2 ** ADAM_STEP)
                go_ref[0] = g
                d_ref[0] = -ADAM_LR * (m_hat / (jnp.sqrt(v_hat) + ADAM_EPS) + ADAM_WD * w_ref[0])
                mo_ref[0] = mn
                vo_ref[0] = vn

    def g_spec(layer):
        return pl.BlockSpec((n, rows, c), lambda a, i: (0, jnp.where(a < layer, 0, jnp.where(a == layer, i, nb - 1)), 0))

    blk = pl.BlockSpec((1, rows, c), lambda a, i: (a, i, 0))
    out = jax.ShapeDtypeStruct((nl, r, c), F32)
    return pl.pallas_call(
        body, name=name, grid=(nl, nb),
        in_specs=[g_spec(layer) for layer in range(nl)] + [blk, blk, blk],
        out_specs=[blk, blk, blk, blk], out_shape=[out, out, out, out],
        compiler_params=_cparams(("arbitrary", "arbitrary")),
    )(*glist, w, m, v)


def _slot_sum(gslots, name):
    n, r, c = gslots.shape

    def body(g_ref, o_ref):
        g = g_ref[0]
        for s in range(1, n):
            g = g + g_ref[s]
        o_ref[...] = g

    return pl.pallas_call(
        body, name=name, in_specs=[pl.BlockSpec((n, r, c), lambda: (0, 0, 0))],
        out_specs=pl.BlockSpec((r, c), lambda: (0, 0)), out_shape=jax.ShapeDtypeStruct((r, c), F32),
        compiler_params=_cparams(None),
    )(gslots)


def _me_and_peers():
    x, y, c = lax.axis_index("x"), lax.axis_index("y"), lax.axis_index("c")
    me = 4 * x + 2 * y + c
    peers = []
    for k in range(1, N_DEV):
        px = 1 - x if (k >> 2) & 1 else x
        py = 1 - y if (k >> 1) & 1 else y
        pc = 1 - c if k & 1 else c
        peers.append((4 * px + 2 * py + pc, (px, py, pc)))
    return me, peers


def _exchange(tensors, gather, name):
    n = len(tensors)

    def body(*refs):
        ins, outs = refs[:n], refs[n:2 * n]
        send_sems, recv_sems, local_sems = refs[2 * n:]
        me, peers = _me_and_peers()
        started = []
        for t in range(n):
            own = pltpu.make_async_copy(ins[t] if gather else ins[t].at[me], outs[t].at[me], local_sems.at[t])
            own.start()
            started.append(own)
            for k, (pidx, pos) in enumerate(peers):
                cp = pltpu.make_async_remote_copy(
                    src_ref=ins[t] if gather else ins[t].at[pidx], dst_ref=outs[t].at[me],
                    send_sem=send_sems.at[t, k], recv_sem=recv_sems.at[t, k], device_id=pos, device_id_type=MESH)
                cp.start()
                started.append(cp)
        for cp in started:
            cp.wait()

    any_spec = pl.BlockSpec(memory_space=pl.ANY)
    outs = pl.pallas_call(
        body, name=name, in_specs=[any_spec] * n, out_specs=[any_spec] * n,
        out_shape=[jax.ShapeDtypeStruct(((N_DEV,) + t.shape) if gather else t.shape, t.dtype) for t in tensors],
        scratch_shapes=[pltpu.SemaphoreType.DMA((n, N_DEV - 1)), pltpu.SemaphoreType.DMA((n, N_DEV - 1)),
                        pltpu.SemaphoreType.DMA((n,))],
        compiler_params=pltpu.CompilerParams(has_side_effects=True),
    )(*tensors)
    return list(outs)


_HBM = pl.BlockSpec(memory_space=pltpu.HBM)
_SEM = pl.BlockSpec(memory_space=pltpu.SEMAPHORE)
_EFFECT = pltpu.SideEffectType.DATAFLOW_SIDE_EFFECTING


N_CHIP = N_DEV // 2


def _chip_peers():
    x, y, c = lax.axis_index("x"), lax.axis_index("y"), lax.axis_index("c")
    chips = []
    for d in range(1, N_CHIP):
        px = 1 - x if (d >> 1) & 1 else x
        py = 1 - y if d & 1 else y
        chips.append((2 * px + py, (px, py)))
    return (x, y, c), 2 * x + y, chips


def _plan_gather(ins, lands, send_sems, recv_sems, local_sems, first=0):
    (x, y, c), q, chips = _chip_peers()
    me = 2 * q + c
    plan = dict(start=[], relay_wait=[], relay_start=[], local=[], sends=[], recvs=[])
    for t in range(len(ins)):
        base = (first + t) * 7
        sem = lambda k: dict(send_sem=send_sems.at[base + k], recv_sem=recv_sems.at[base + k], device_id_type=MESH)
        own = pltpu.make_async_copy(ins[t], lands[t].at[me], local_sems.at[first + t])
        to_sib = pltpu.make_async_remote_copy(src_ref=ins[t], dst_ref=lands[t].at[me], device_id=(x, y, 1 - c), **sem(0))
        plan['start'] += [own, to_sib]
        plan['local'].append(own)
        plan['sends'].append(to_sib)
        plan['recvs'].append(to_sib)
        for d, (pq, (px, py)) in enumerate(chips):
            to_chip = pltpu.make_async_remote_copy(src_ref=ins[t], dst_ref=lands[t].at[me], device_id=(px, py, c), **sem(1 + d))
            blk = lands[t].at[2 * pq + c]
            fwd = pltpu.make_async_remote_copy(src_ref=blk, dst_ref=blk, device_id=(x, y, 1 - c), **sem(4 + d))
            plan['start'].append(to_chip)
            plan['relay_wait'].append(to_chip)
            plan['relay_start'].append(fwd)
            plan['sends'] += [to_chip, fwd]
            plan['recvs'].append(fwd)
    return plan


def _plan_pair(ins, lands, send_sems, recv_sems, local_sems):
    (x, y, c), q, chips = _chip_peers()
    plan = dict(start=[], local=[], sends=[], recvs=[])
    for t in range(len(ins)):
        for k in range(N_CHIP):
            cp = pltpu.make_async_remote_copy(
                src_ref=ins[t].at[2 * k + 1 - c], dst_ref=lands[t].at[k], send_sem=send_sems.at[t * N_CHIP + k],
                recv_sem=recv_sems.at[t * N_CHIP + k], device_id=(x, y, 1 - c), device_id_type=MESH)
            plan['start'].append(cp)
            plan['sends'].append(cp)
            plan['recvs'].append(cp)
    return plan


def _plan_chips(ins, lands, send_sems, recv_sems, local_sems):
    (x, y, c), q, chips = _chip_peers()
    plan = dict(start=[], local=[], sends=[], recvs=[])
    for t in range(len(ins)):
        own = pltpu.make_async_copy(ins[t].at[q], lands[t].at[q], local_sems.at[t])
        plan['start'].append(own)
        plan['local'].append(own)
        for d, (pq, (px, py)) in enumerate(chips):
            cp = pltpu.make_async_remote_copy(
                src_ref=ins[t].at[pq], dst_ref=lands[t].at[q], send_sem=send_sems.at[t * 3 + d],
                recv_sem=recv_sems.at[t * 3 + d], device_id=(px, py, c), device_id_type=MESH)
            plan['start'].append(cp)
            plan['sends'].append(cp)
            plan['recvs'].append(cp)
    return plan


def _split_start(plan_fn, tensors, land_shapes, n_sems, name):
    n = len(tensors)

    def body(*refs):
        ins, lands = refs[:n], refs[n:2 * n]
        plan = plan_fn(ins, lands, *refs[2 * n:2 * n + 3])
        for cp in plan['start']:
            cp.start()
        refs[-1][...] = jnp.zeros_like(refs[-1])

    outs = pl.pallas_call(
        body, name=name,
        out_shape=(pltpu.SemaphoreType.DMA((n_sems,)), pltpu.SemaphoreType.DMA((n_sems,)), pltpu.SemaphoreType.DMA((n,)),
                   *[pltpu.HBM(t.shape, t.dtype) for t in tensors],
                   *[pltpu.HBM(s, t.dtype) for s, t in zip(land_shapes, tensors)],
                   jax.ShapeDtypeStruct((8, LANES), F32)),
        in_specs=[_HBM] * (2 * n),
        out_specs=(_SEM, _SEM, _SEM, *[_HBM] * (2 * n), pl.BlockSpec(memory_space=pltpu.VMEM)),
        input_output_aliases={t: 3 + t for t in range(2 * n)},
        compiler_params=pltpu.CompilerParams(has_side_effects=_EFFECT),
    )(*[pltpu.with_memory_space_constraint(t, pltpu.HBM) for t in tensors],
      *[pltpu.with_memory_space_constraint(lax.empty(s, t.dtype), pltpu.HBM) for s, t in zip(land_shapes, tensors)])
    return outs[:-1], outs[-1]


def _split_relay(plan_fn, state, after, name):
    sems, thru = state[:3], state[3:]
    n = len(thru) // 2

    def arrived(*refs):
        plan = plan_fn(refs[:n], refs[n:2 * n], *refs[2 * n:2 * n + 3])
        for cp in plan['relay_wait']:
            cp.wait_recv()

    thru = pl.pallas_call(
        arrived, name=name + "_arrived",
        out_shape=tuple(pltpu.HBM(t.shape, t.dtype) for t in thru),
        in_specs=[_HBM] * (2 * n) + [_SEM, _SEM, _SEM, pl.BlockSpec(memory_space=pl.ANY)],
        out_specs=tuple([_HBM] * (2 * n)),
        input_output_aliases={t: t for t in range(2 * n)},
        compiler_params=pltpu.CompilerParams(has_side_effects=_EFFECT),
    )(*thru, *sems, after)

    def forward(*refs):
        plan = plan_fn(refs[:n], refs[n:2 * n], *refs[2 * n:2 * n + 3])
        for cp in plan['relay_start']:
            cp.start()
        refs[-1][...] = jnp.zeros_like(refs[-1])

    outs = pl.pallas_call(
        forward, name=name + "_forward",
        out_shape=(*[pltpu.HBM(t.shape, t.dtype) for t in thru], jax.ShapeDtypeStruct((8, LANES), F32)),
        in_specs=[_HBM] * (2 * n) + [_SEM, _SEM, _SEM],
        out_specs=(*[_HBM] * (2 * n), pl.BlockSpec(memory_space=pltpu.VMEM)),
        input_output_aliases={t: t for t in range(2 * n)},
        compiler_params=pltpu.CompilerParams(has_side_effects=_EFFECT),
    )(*thru, *sems)
    return (*sems, *outs[:-1]), outs[-1]


def _split_wait(plan_fn, state, after, name):
    sems, thru = state[:3], state[3:]
    n = len(thru) // 2

    def body(*refs):
        plan = plan_fn(refs[:n], refs[n:2 * n], *refs[2 * n:2 * n + 3])
        for cp in plan['local']:
            cp.wait()
        for cp in plan['sends']:
            cp.wait_send()
        for cp in plan['recvs']:
            cp.wait_recv()

    outs = pl.pallas_call(
        body, name=name,
        out_shape=tuple(pltpu.HBM(t.shape, t.dtype) for t in thru),
        in_specs=[_HBM] * (2 * n) + [_SEM, _SEM, _SEM, pl.BlockSpec(memory_space=pl.ANY)],
        out_specs=tuple([_HBM] * (2 * n)),
        input_output_aliases={t: t for t in range(2 * n)},
        compiler_params=pltpu.CompilerParams(has_side_effects=_EFFECT),
    )(*thru, *sems, after)
    return list(outs[n:])


def _pair_sum(mine, theirs, name):
    _, r, c = mine.shape
    rows = r if r <= 512 else 256

    def body(core_ref, a_ref, b_ref, o_ref):
        o_ref[0] = (a_ref[0].astype(F32) + b_ref[0].astype(F32)).astype(o_ref.dtype)

    return pl.pallas_call(
        body, name=name,
        grid_spec=pltpu.PrefetchScalarGridSpec(
            num_scalar_prefetch=1, grid=(N_CHIP, r // rows),
            in_specs=[pl.BlockSpec((1, rows, c), lambda k, i, core: (2 * k + core[0], i, 0)),
                      pl.BlockSpec((1, rows, c), lambda k, i, core: (k, i, 0))],
            out_specs=pl.BlockSpec((1, rows, c), lambda k, i, core: (k, i, 0))),
        out_shape=jax.ShapeDtypeStruct((N_CHIP, r, c), mine.dtype),
        compiler_params=_cparams(("parallel", "parallel")),
    )(lax.axis_index("c").astype(jnp.int32).reshape(1), mine, theirs)


WEIGHTS = ['norm_w', 'w_in', 's5_lambda_re', 's5_lambda_im', 's5_b_re', 's5_b_im', 's5_c_re', 's5_c_im', 's5_d',
           's5_log_step', 's5_w_glu', 'sgu_ln_w', 'sgu_ln_b', 'sgu_w', 'sgu_b', 'm2_conv_w', 'm2_conv_b', 'm2_dt_bias',
           'm2_a_log', 'm2_d', 'm2_norm_w', 'sc_conv_w', 'merge_b', 'w_branch', 'w_out', 'final_norm_w']
BIG_SHARDED = ['w_in', 'w_branch', 'w_out', 's5_w_glu']
SMALL_SHARDED = ['m2_conv_w', 'sc_conv_w', 'merge_b']
REPLICATED = [n for n in WEIGHTS if n not in BIG_SHARDED + SMALL_SHARDED]
S5_NAMES = ['s5_lambda_re', 's5_lambda_im', 's5_b_re', 's5_b_im', 's5_c_re', 's5_c_im', 's5_d', 's5_log_step']


def _sc_interleave(t):
    lead = t.shape[:-1]
    return jnp.swapaxes(t.reshape(lead + (4, 4, LANES)), -3, -2).reshape(lead + (4 * BW,))


def _pad_in(w):
    z = lambda n: jnp.zeros(w.shape[:-1] + (n,), w.dtype)
    return jnp.concatenate([w[..., 6152:], w[..., 0:1024], w[..., 3072:4096], w[..., 1024:2560], z(512),
                            w[..., 2560:3072], w[..., 4096:4104], z(504), _sc_interleave(w[..., 4104:6152])], axis=-1)


def _unpad_in(g):
    return jnp.concatenate([g[..., C_S5U:C_S5U + 1024], g[..., C_SGU_U:C_SGU_U + 1536], g[..., C_M2Z:C_M2Z + 512],
                            g[..., C_M2X:C_M2X + 1024], g[..., C_DT:C_DT + 8], _sc_interleave(g[..., C_SC:]),
                            g[..., :N_BRANCH * D_MODEL]], axis=-1)


ROW_BLOCK = 8 * LANES


def _pack_rows(tensors, row_mult, batched=False):
    parts = []
    for t in tensors:
        f = t.reshape((t.shape[0], -1) if batched else (1, -1))
        f = jnp.pad(f, ((0, 0), (0, (-f.shape[1]) % ROW_BLOCK)))
        parts.append(f.reshape(f.shape[0], -1, LANES))
    out = jnp.concatenate(parts, axis=1)
    out = jnp.pad(out, ((0, 0), (0, (-out.shape[1]) % row_mult), (0, 0)))
    return out if batched else out[0]


def _unpack_rows(rows, shapes):
    out, r0 = [], 0
    for shp in shapes:
        size = 1
        for s in shp:
            size *= s
        nr = -(-size // ROW_BLOCK) * 8
        out.append(rows[r0:r0 + nr].reshape(-1)[:size].reshape(shp))
        r0 += nr
    return out


def _kernel_col_map():
    m = np.full(IN_PAD, -1, np.int64)
    m[C_MERGE:C_MERGE + 4096] = np.arange(6152, 10248)
    m[C_S5U:C_S5U + 1024] = np.arange(0, 1024)
    m[C_M2X:C_M2X + 1024] = np.arange(3072, 4096)
    m[C_SGU_U:C_SGU_U + 1536] = np.arange(1024, 2560)
    m[C_M2Z:C_M2Z + 512] = np.arange(2560, 3072)
    m[C_DT:C_DT + 8] = np.arange(4096, 4104)
    for j in range(4):
        for kind in range(4):
            k0 = C_SC + 4 * LANES * j + LANES * kind
            m[k0:k0 + LANES] = 4104 + BW * kind + LANES * j + np.arange(LANES)
    return m


def _lane_pieces(sources):
    pieces, cur = [], None
    for lane, src in enumerate(sources):
        key = None if src is None else (src[0], src[1] // LANES, (lane - src[1]) % LANES)
        if cur is not None and key == cur[0]:
            cur[2] = lane + 1
        else:
            if cur is not None and cur[0] is not None:
                pieces.append((*cur[0], cur[1], cur[2]))
            cur = [key, lane, lane + 1]
    if cur is not None and cur[0] is not None:
        pieces.append((*cur[0], cur[1], cur[2]))
    return pieces


def _assemble_block(pieces, load, rows, dtype):
    lane = lax.broadcasted_iota(jnp.int32, (rows, LANES), 1)
    out = None
    for arr, sb, shift, lo, hi in pieces:
        v = load(arr, sb)
        if shift:
            v = pltpu.roll(v, shift, 1)
        if out is None and lo == 0 and hi == LANES:
            out = v
        else:
            out = jnp.where((lane >= lo) & (lane < hi), v, jnp.zeros((rows, LANES), dtype) if out is None else out)
    return jnp.zeros((rows, LANES), dtype) if out is None else out


RELAYOUT_ROWS = 256
SHARD_BLOCKS = -(-SHARD_IN // LANES)


def _load_shard_block(ref, rows):
    def load(j, sb):
        if sb == SHARD_BLOCKS - 1:
            return jnp.broadcast_to(ref[j, :, SHARD_IN - 1:SHARD_IN], (rows, LANES))
        return ref[j, :, sb * LANES:(sb + 1) * LANES]
    return load


def _relayout_w_in(gathered, name):
    kmap = _kernel_col_map()
    dtype = gathered.dtype

    def body(src_ref, o_ref):
        load = _load_shard_block(src_ref, RELAYOUT_ROWS)
        for ob in range(IN_PAD // LANES):
            srcs = [None if kmap[ob * LANES + l] < 0 else (int(kmap[ob * LANES + l]) // SHARD_IN, int(kmap[ob * LANES + l]) % SHARD_IN)
                    for l in range(LANES)]
            o_ref[:, ob * LANES:(ob + 1) * LANES] = _assemble_block(_lane_pieces(srcs), load, RELAYOUT_ROWS, dtype)

    return pl.pallas_call(
        body, name=name, grid=(D_MODEL // RELAYOUT_ROWS,),
        in_specs=[pl.BlockSpec((N_DEV, RELAYOUT_ROWS, SHARD_IN), lambda i: (0, i, 0))],
        out_specs=pl.BlockSpec((RELAYOUT_ROWS, IN_PAD), lambda i: (i, 0)),
        out_shape=jax.ShapeDtypeStruct((D_MODEL, IN_PAD), dtype),
        compiler_params=_cparams(("parallel",)),
    )(gathered)


def _relayout_g_in(gw, name):
    kmap = _kernel_col_map()
    kinv = np.zeros(IN_DIM, np.int64)
    kinv[kmap[kmap >= 0]] = np.nonzero(kmap >= 0)[0]
    dtype = gw.dtype

    def body(src_ref, o_ref):
        load = lambda _, sb: src_ref[:, sb * LANES:(sb + 1) * LANES]
        for j in range(N_DEV):
            for ob in range(SHARD_BLOCKS):
                srcs = [(0, int(kinv[SHARD_IN * j + ob * LANES + l])) if ob * LANES + l < SHARD_IN else None for l in range(LANES)]
                blk = _assemble_block(_lane_pieces(srcs), load, RELAYOUT_ROWS, dtype)
                if ob == SHARD_BLOCKS - 1:
                    o_ref[j, :, SHARD_IN - 1:SHARD_IN] = blk[:, 0:1]
                else:
                    o_ref[j, :, ob * LANES:(ob + 1) * LANES] = blk

    return pl.pallas_call(
        body, name=name, grid=(D_MODEL // RELAYOUT_ROWS,),
        in_specs=[pl.BlockSpec((RELAYOUT_ROWS, IN_PAD), lambda i: (i, 0))],
        out_specs=pl.BlockSpec((N_DEV, RELAYOUT_ROWS, SHARD_IN), lambda i: (0, i, 0)),
        out_shape=jax.ShapeDtypeStruct((N_DEV, D_MODEL, SHARD_IN), dtype),
        compiler_params=_cparams(("parallel",)),
    )(gw)


def _rows128(flat, row_mult=8):
    n = flat.shape[0]
    per = LANES * row_mult
    total = -(-n // per) * per
    return jnp.pad(flat, (0, total - n)).reshape(total // LANES, LANES)


def _pad_lanes(v):
    return jnp.pad(v, (0, LANES - v.shape[0])).reshape(1, LANES)


def _layer_prep(i, p):
    disc, disc_vjp = jax.vjp(_s5_disc, *[p[n][i] for n in S5_NAMES])
    prep = dict(
        nw=p['norm_w'][i].reshape(1, D_MODEL), disc_vjp=disc_vjp,
        s5small=[t.astype(BF16) for t in disc[:4]] + [disc[4], disc[5]],
        sgw=[p['sgu_ln_w'][i].reshape(1, BW), p['sgu_ln_b'][i].reshape(1, BW), p['sgu_w'][i],
             jnp.repeat(p['sgu_b'][i].T, BW // SGU_HEADS, axis=1)],
        cb=p['m2_conv_b'][i].reshape(1, M2_CONV_CH),
        m2w=[_pad_lanes(p['m2_dt_bias'][i]), _pad_lanes(p['m2_a_log'][i]),
             jnp.repeat(p['m2_d'][i], M2_HEAD_DIM).reshape(1, BW), p['m2_norm_w'][i].reshape(1, BW)])
    touch = [t[0, 0].astype(F32) for t in prep['s5small']] + [prep['sgw'][3][0, 0], prep['m2w'][2][0, 0]]
    return prep, sum(touch[1:], touch[0])


def _layer_fwd(x, h, i, prep, w_in, other_weights):
    proj = _matmul(h, w_in, 1, 0, F32, 1024, 1024, 1024, f"proj{i}")
    full = dict(other_weights(proj), w_in=w_in)
    s5w = prep['s5small'] + [full['s5_w_glu']]
    ya, sre, sim = _s5_fwd(proj, *s5w, f"s5_fwd{i}")
    yb = _sgu_fwd(proj, *prep['sgw'], f"sgu_fwd{i}")
    cw = full['m2_conv_w']
    xa = _m2_conv_fwd(proj, cw, prep['cb'], f"m2conv_fwd{i}")
    yc, s_in = _ssd_fwd(proj, xa, *prep['m2w'], f"ssd_fwd{i}")
    scw = full['sc_conv_w']
    yd = _sc_fwd(proj, scw, f"sc_fwd{i}")
    ys = jnp.stack([ya, yb, yc, yd])
    mb = full['merge_b'].reshape(N_BRANCH, 1, D_MODEL)
    merged = _merge_fwd(proj, ys, mb, full['w_branch'], f"merge_fwd{i}")
    x_new = _matmul(merged, full['w_out'], 1, 0, F32, 1024, 1024, 1024, f"out{i}", residual=x)
    saved = dict(x=x, nw=prep['nw'], h=h, proj=proj, disc_vjp=prep['disc_vjp'], s5w=s5w, sre=sre, sim=sim, sgw=prep['sgw'],
                 cw=cw, cb=prep['cb'], xa=xa, m2w=prep['m2w'], s_in=s_in, scw=scw, ys=ys, mb=mb, merged=merged)
    return x_new, saved, full


def _layer_bwd(dx_out, i, sv, full, on_large_grads=None):
    g = {}
    proj = sv['proj']
    dm = _matmul(dx_out, full['w_out'], 1, 1, F32, 1024, 1024, 1024, f"dmerged{i}")
    g['w_out'] = _matmul(sv['merged'], dx_out, 0, 0, BF16, 1024, 1024, 1024, f"gw_out{i}")
    dys, dproj, g['w_branch'], dmb = _merge_bwd(proj, sv['ys'], dm, sv['mb'], full['w_branch'], f"merge_bwd{i}")
    g['merge_b'] = dmb.reshape(N_BRANCH, D_MODEL)
    dproj, dbbre, dbbim, dcre, dcim, da, dd, dwg = _s5_bwd(proj, dproj, dys, sv['sre'], sv['sim'], *sv['s5w'], f"s5_bwd{i}")
    for n, t in zip(S5_NAMES, sv['disc_vjp']((dbbre, dbbim, dcre, dcim, da, dd))):
        g[n] = t
    g['s5_w_glu'] = dwg.astype(BF16)
    dproj, dlw, dlb, g['sgu_w'], dbias = _sgu_bwd(proj, dproj, dys, *sv['sgw'], f"sgu_bwd{i}")
    g['sgu_ln_w'], g['sgu_ln_b'] = dlw[0], dlb[0]
    g['sgu_b'] = dbias.reshape(SGU_CHUNK, SGU_HEADS, BW // SGU_HEADS).sum(-1).T
    dproj, dxa, ddtb, dal, ddf, dnw = _ssd_bwd(proj, dproj, sv['xa'], dys, sv['s_in'], *sv['m2w'], f"ssd_bwd{i}")
    dproj, g['m2_conv_w'], dcb = _m2_conv_bwd(proj, dproj, dxa, sv['cw'], sv['cb'], f"m2conv_bwd{i}")
    g['m2_conv_b'], g['m2_norm_w'] = dcb[0], dnw[0]
    g['m2_dt_bias'], g['m2_a_log'] = ddtb[0, :M2_HEADS], dal[0, :M2_HEADS]
    g['m2_d'] = ddf.reshape(M2_HEADS, M2_HEAD_DIM).sum(-1)
    dproj, g['sc_conv_w'] = _sc_bwd(proj, dproj, dys, sv['scw'], f"sc_bwd{i}")
    g['w_in'] = _matmul(sv['h'], dproj, 0, 0, BF16, 1024, 1024, 1024, f"gw_in{i}")
    tok = on_large_grads(g) if on_large_grads else None
    dh = _matmul(dproj, full['w_in'], 1, 1, F32, 1024, 1024, 1024, f"dh{i}", after=tok)
    dx_in, dnw_l = _rmsnorm_bwd(sv['x'], sv['nw'], dh, dx_out, f"rms_bwd{i}")
    g['norm_w'] = dnw_l[0]
    return dx_in, g


def _split8(t, axis):
    shp = t.shape
    t = t.reshape(shp[:axis] + (N_DEV, shp[axis] // N_DEV) + shp[axis + 1:])
    return jnp.moveaxis(t, axis, 0)


def _join8(t, axis):
    t = jnp.moveaxis(t, 0, axis)
    shp = t.shape
    return t.reshape(shp[:axis] + (shp[axis] * shp[axis + 1],) + shp[axis + 2:])


SHARD_AXIS = {'w_in': 2, 'w_branch': 3, 'w_out': 1, 's5_w_glu': 1, 'm2_conv_w': 2, 'sc_conv_w': 2, 'merge_b': 2}


OTHER_BIG = [n for n in BIG_SHARDED if n != 'w_in']


def _other_weights(gathered):
    return {n: _join8(t, SHARD_AXIS[n] - 1) for n, t in zip(OTHER_BIG, gathered)}


def _layer_grad_blocks(g, i):
    blocks = [_relayout_g_in(g[n], f"relayout_g_in{i}") if n == 'w_in' else _split8(g[n], SHARD_AXIS[n] - 1) for n in BIG_SHARDED]
    return [b.reshape(N_DEV, -1, b.shape[-1]) for b in blocks]


def _start_reduce_scatter(blocks, i):
    pair_shapes = [(N_CHIP,) + b.shape[1:] for b in blocks]
    state, tok = _split_start(_plan_pair, blocks, pair_shapes, N_CHIP * len(blocks), f"pair{i}_start")
    theirs = _split_wait(_plan_pair, state, tok, f"pair{i}_wait")
    sums = [_pair_sum(b, t, f"pair_sum{i}_{k}") for k, (b, t) in enumerate(zip(blocks, theirs))]
    return _split_start(_plan_chips, sums, pair_shapes, 3 * len(blocks), f"chips{i}_start")


def kernel(x, norm_w, w_in, s5_lambda_re, s5_lambda_im, s5_b_re, s5_b_im, s5_c_re, s5_c_im, s5_d, s5_log_step, s5_w_glu, sgu_ln_w, sgu_ln_b, sgu_w, sgu_b, m2_conv_w, m2_conv_b, m2_dt_bias, m2_a_log, m2_d, m2_norm_w, sc_conv_w, merge_b, w_branch, w_out, final_norm_w, loss_target, m_norm_w, m_w_in, m_s5_lambda_re, m_s5_lambda_im, m_s5_b_re, m_s5_b_im, m_s5_c_re, m_s5_c_im, m_s5_d, m_s5_log_step, m_s5_w_glu, m_sgu_ln_w, m_sgu_ln_b, m_sgu_w, m_sgu_b, m_m2_conv_w, m_m2_conv_b, m_m2_dt_bias, m_m2_a_log, m_m2_d, m_m2_norm_w, m_sc_conv_w, m_merge_b, m_w_branch, m_w_out, m_final_norm_w, v_norm_w, v_w_in, v_s5_lambda_re, v_s5_lambda_im, v_s5_b_re, v_s5_b_im, v_s5_c_re, v_s5_c_im, v_s5_d, v_s5_log_step, v_s5_w_glu, v_sgu_ln_w, v_sgu_ln_b, v_sgu_w, v_sgu_b, v_m2_conv_w, v_m2_conv_b, v_m2_dt_bias, v_m2_a_log, v_m2_d, v_m2_norm_w, v_sc_conv_w, v_merge_b, v_w_branch, v_w_out, v_final_norm_w):
    loc = locals()
    p = {n: loc[n] for n in WEIGHTS}
    mom = {n: loc['m_' + n] for n in WEIGHTS}
    vel = {n: loc['v_' + n] for n in WEIGHTS}

    small_sizes = [p[n].size for n in SMALL_SHARDED]
    small_pack = _rows128(jnp.concatenate([p[n].reshape(-1) for n in SMALL_SHARDED]))
    shards = ([p['w_in'][0].astype(BF16)] + [p[n][0].astype(BF16) for n in OTHER_BIG] + [small_pack]
              + [p[n][1].astype(BF16) for n in BIG_SHARDED])
    gath, tok = _split_start(_plan_gather, shards, [(N_DEV,) + t.shape for t in shards], 7 * len(shards), "gather_start")
    sems, srcs, lands = gath[:3], gath[3:3 + len(shards)], gath[3 + len(shards):]

    def gathered(lo, hi, after, name):
        plan = functools.partial(_plan_gather, first=lo)
        state, tok = _split_relay(plan, (*sems, *srcs[lo:hi], *lands[lo:hi]), after, name + "_relay")
        return _split_wait(plan, state, tok, name + "_wait")

    later = dict(p, **{n: p[n] + tok[0, 0] for n in ('norm_w', 's5_log_step', 'sgu_b', 'm2_d')})
    preps = [_layer_prep(i, later) for i in range(DEPTH)]
    h0 = _rmsnorm_fwd(x[0], preps[0][0]['nw'], "rms_fwd0")
    got = gathered(0, 1, tok + (preps[0][1] + preps[1][1] + h0[0, 0].astype(F32)), "gather_w_in0")
    small_full = {}

    def other_weights0(proj):
        got = gathered(1, 5, proj, "gather_rest0")
        small_all, off = got[-1].reshape(N_DEV, -1), 0
        for n, sz in zip(SMALL_SHARDED, small_sizes):
            small_full[n] = _join8(small_all[:, off:off + sz].reshape((N_DEV,) + p[n].shape), SHARD_AXIS[n])
            off += sz
        return dict(_other_weights(got[:-1]), **{n: small_full[n][0] for n in SMALL_SHARDED})

    saved, layer_g, full = [None] * DEPTH, [None] * DEPTH, [None] * DEPTH
    xs, saved[0], full[0] = _layer_fwd(x[0], h0, 0, preps[0][0], _relayout_w_in(got[0], "relayout_w_in0"), other_weights0)
    h1 = _rmsnorm_fwd(xs, preps[1][0]['nw'], "rms_fwd1")
    got = gathered(5, 9, h1, "gather1")
    xs, saved[1], full[1] = _layer_fwd(
        xs, h1, 1, preps[1][0], _relayout_w_in(got[0], "relayout_w_in1"),
        lambda proj: dict(_other_weights(got[1:]), **{n: small_full[n][1] for n in SMALL_SHARDED}))
    loss_row, dx, dfw = _loss_head(xs, final_norm_w.reshape(1, D_MODEL), loss_target[0])
    loss = lax.psum(loss_row[0, 0], ("x", "y", "c"))
    scat = [None] * DEPTH

    def start_scatter(i):
        def start(g):
            scat[i], tok = _start_reduce_scatter(_layer_grad_blocks(g, i), i)
            return tok
        return start

    dx, layer_g[1] = _layer_bwd(dx, 1, saved[1], full[1], start_scatter(1))
    dx, layer_g[0] = _layer_bwd(dx, 0, saved[0], full[0], start_scatter(0))
    grads = {n: jnp.stack([layer_g[i][n] for i in range(DEPTH)]) for n in SMALL_SHARDED + REPLICATED if n != 'final_norm_w'}
    grads['final_norm_w'] = dfw[0]

    out_g, out_d, out_m, out_v = {}, {}, {}, {}
    repl_rows = _pack_rows([grads[n] for n in REPLICATED], 8 * N_DEV)
    rr = repl_rows.shape[0] // N_DEV
    shard_rows = _pack_rows([_split8(grads[n], SHARD_AXIS[n]) for n in SMALL_SHARDED], 8, batched=True)
    rs = shard_rows.shape[1]
    small_g = jnp.concatenate([shard_rows, repl_rows.reshape(N_DEV, rr, LANES)], axis=1)
    small_sum = _slot_sum(_exchange([small_g], False, "scatter_small")[0], "sum_small")
    repl_all = _exchange([small_sum[rs:]], True, "gather_small")[0].reshape(N_DEV * rr, LANES)
    g_all = jnp.concatenate([small_sum[:rs], repl_all], axis=0)
    names = SMALL_SHARDED + REPLICATED
    packed = [jnp.concatenate([_pack_rows([d[n] for n in SMALL_SHARDED], 8), _pack_rows([d[n] for n in REPLICATED], 8 * N_DEV)],
                              axis=0) for d in (p, mom, vel)]
    res = _adamw([g_all[None]], *[t[None] for t in packed], g_all.shape[0], "adamw_small")
    for o, dst in zip(res, (out_g, out_d, out_m, out_v)):
        pieces = (_unpack_rows(o[0, :rs], [p[n].shape for n in SMALL_SHARDED])
                  + _unpack_rows(o[0, rs:], [p[n].shape for n in REPLICATED]))
        dst.update(zip(names, pieces))

    landed1 = _split_wait(_plan_chips, scat[1], res[0], "chips1_wait")
    landed0 = _split_wait(_plan_chips, scat[0], landed1[0], "chips0_wait")
    for k, n in enumerate(BIG_SHARDED):
        shp = p[n].shape
        c = shp[-1]
        r = p[n].size // (DEPTH * c)
        big = _adamw([landed0[k], landed1[k]],
                     *[d[n].reshape(DEPTH, r, c) for d in (p, mom, vel)],
                     {'w_in': 256, 'w_branch': 512, 'w_out': 128, 's5_w_glu': 64}[n], "adamw_" + n)
        out_g[n], out_d[n], out_m[n], out_v[n] = [o.reshape(shp) for o in big]
    return (loss, dx[None], *[out_g[n] for n in WEIGHTS], *[out_d[n] for n in WEIGHTS],
            *[out_m[n] for n in WEIGHTS], *[out_v[n] for n in WEIGHTS])
```

```python
import functools

import jax
import jax.numpy as jnp
import numpy as np
from jax import lax
from jax.experimental import pallas as pl
from jax.experimental.pallas import tpu as pltpu

F32 = jnp.float32
BF16 = jnp.bfloat16

N_DEV = 8
SEQ = 2048
D_MODEL = 1024
DEPTH = 2
BW = 512
N_BRANCH = 4
EPS = 1e-6
S5_GROUPS, S5_STATE, S5_P = 32, 64, 16
S5_CH = S5_GROUPS * S5_STATE
SGU_CHUNK, SGU_HEADS = 128, 8
M2_HEADS, M2_HEAD_DIM, M2_STATE, M2_CHUNK, M2_CONV = 8, 64, 128, 128, 4
M2_CONV_CH = 1024
SC_CONV = 3
IN_DIM = 10248
IN_PAD = 11264
C_MERGE = 0
C_S5U, C_S5G = 4096, 4608
C_M2X = 5120
C_SGU_U, C_SGU_V, C_SGU_G = 6144, 6656, 7168
C_M2Z, C_DT = 8192, 8704
C_SC = 9216
SHARD_IN = IN_DIM // N_DEV

ADAM_LR, ADAM_B1, ADAM_B2, ADAM_EPS, ADAM_WD, ADAM_STEP = 0.001, 0.9, 0.999, 1e-08, 0.01, 10

VMEM_LIMIT = 56 * 1024 * 1024
LANES = 128

MESH = pl.DeviceIdType.MESH


def _cparams(sem=None, **kw):
    return pltpu.CompilerParams(dimension_semantics=sem, vmem_limit_bytes=VMEM_LIMIT, **kw)


def _dg(a, b, ca, cb, precision=None):
    return lax.dot_general(a, b, (((ca,), (cb,)), ((), ())), precision=precision,
                           preferred_element_type=F32)


@functools.partial(jax.custom_vjp, nondiff_argnums=(2, 3))
def _bdot(a, b, ca, cb):
    return _dg(a.astype(BF16), b.astype(BF16), ca, cb)


def _bdot_fwd(a, b, ca, cb):
    return _bdot(a, b, ca, cb), (a, b)


def _bdot_bwd(ca, cb, res, g):
    a, b = res
    gb, ab, bb = g.astype(BF16), a.astype(BF16), b.astype(BF16)
    da = _dg(gb, bb, 1, 1 - cb) if ca == 1 else _dg(bb, gb, 1 - cb, 1)
    db = _dg(ab, gb, 1 - ca, 0) if cb == 0 else _dg(gb, ab, 0, 1 - ca)
    return da.astype(a.dtype), db.astype(b.dtype)


_bdot.defvjp(_bdot_fwd, _bdot_bwd)


def _rms(x, w):
    return x * lax.rsqrt(jnp.mean(x * x, axis=-1, keepdims=True) + EPS) * w


def _silu(x):
    return x * jax.nn.sigmoid(x)


def _gelu(x):
    return 0.5 * x * (1.0 + jnp.tanh(0.7978845608028654 * (x + 0.044715 * (x * x * x))))


def _softplus(x):
    return jnp.maximum(x, 0.0) + jnp.log1p(jnp.exp(-jnp.abs(x)))


def _shift_down(x, s):
    if s == 0:
        return x
    row = lax.broadcasted_iota(jnp.int32, x.shape, 0)
    return jnp.where(row >= s, pltpu.roll(x, s, 0), 0.0)


def _shift_up(x, s):
    if s == 0:
        return x
    n = x.shape[0]
    row = lax.broadcasted_iota(jnp.int32, x.shape, 0)
    return jnp.where(row < n - s, pltpu.roll(x, n - s, 0), 0.0)


def _matmul(a, b, ca, cb, out_dtype, tm, tn, tk, name, residual=None, after=None):
    m = a.shape[1 - ca]
    k = a.shape[ca]
    n = b.shape[1 - cb]
    assert b.shape[cb] == k and m % tm == 0 and n % tn == 0 and k % tk == 0
    nk = k // tk
    a_spec = pl.BlockSpec((tm, tk), lambda i, j, kk: (i, kk)) if ca == 1 else pl.BlockSpec((tk, tm), lambda i, j, kk: (kk, i))
    b_spec = pl.BlockSpec((tk, tn), lambda i, j, kk: (kk, j)) if cb == 0 else pl.BlockSpec((tn, tk), lambda i, j, kk: (j, kk))
    o_spec = pl.BlockSpec((tm, tn), lambda i, j, kk: (i, j))
    has_res = residual is not None

    def body(*refs):
        refs = refs[:2 + has_res] + refs[2 + has_res + (after is not None):]
        if has_res:
            a_ref, b_ref, r_ref, o_ref, acc = refs
        else:
            a_ref, b_ref, o_ref, acc = refs
        kk = pl.program_id(2)
        part = _dg(a_ref[...].astype(BF16), b_ref[...].astype(BF16), ca, cb)

        @pl.when(kk == 0)
        def _():
            acc[...] = part

        @pl.when(kk > 0)
        def _():
            acc[...] += part

        @pl.when(kk == nk - 1)
        def _():
            r = acc[...]
            if has_res:
                r = r + r_ref[...]
            o_ref[...] = r.astype(out_dtype)

    ins = [a, b] + ([residual] if has_res else []) + ([after] if after is not None else [])
    specs = [a_spec, b_spec] + ([o_spec] if has_res else []) + ([pl.BlockSpec(memory_space=pl.ANY)] if after is not None else [])
    return pl.pallas_call(
        body, name=name, grid=(m // tm, n // tn, nk), in_specs=specs, out_specs=o_spec,
        out_shape=jax.ShapeDtypeStruct((m, n), out_dtype),
        scratch_shapes=[pltpu.VMEM((tm, tn), F32)],
        compiler_params=_cparams(("parallel", "parallel", "arbitrary")),
    )(*ins)


ROW_TILE = 512


def _rmsnorm_fwd(x, w, name):
    def body(x_ref, w_ref, o_ref):
        o_ref[...] = _rms(x_ref[...], w_ref[...]).astype(BF16)

    return pl.pallas_call(
        body, name=name, grid=(SEQ // ROW_TILE,),
        in_specs=[pl.BlockSpec((ROW_TILE, D_MODEL), lambda i: (i, 0)), pl.BlockSpec((1, D_MODEL), lambda i: (0, 0))],
        out_specs=pl.BlockSpec((ROW_TILE, D_MODEL), lambda i: (i, 0)),
        out_shape=jax.ShapeDtypeStruct((SEQ, D_MODEL), BF16),
        compiler_params=_cparams(("parallel",)),
    )(x, w)


def _rmsnorm_bwd(x, w, dh, dres, name):
    def body(x_ref, w_ref, dh_ref, dres_ref, dx_ref, dw_ref):
        _, vjp = jax.vjp(_rms, x_ref[...], w_ref[...])
        dx, dw = vjp(dh_ref[...])
        dx_ref[...] = dx + dres_ref[...]

        @pl.when(pl.program_id(0) == 0)
        def _():
            dw_ref[...] = dw

        @pl.when(pl.program_id(0) > 0)
        def _():
            dw_ref[...] += dw

    tile = pl.BlockSpec((ROW_TILE, D_MODEL), lambda i: (i, 0))
    vec = pl.BlockSpec((1, D_MODEL), lambda i: (0, 0))
    return pl.pallas_call(
        body, name=name, grid=(SEQ // ROW_TILE,),
        in_specs=[tile, vec, tile, tile], out_specs=[tile, vec],
        out_shape=[jax.ShapeDtypeStruct((SEQ, D_MODEL), F32), jax.ShapeDtypeStruct((1, D_MODEL), F32)],
        compiler_params=_cparams(("arbitrary",)),
    )(x, w, dh, dres)


def _loss_head(x, w, target):
    def body(x_ref, w_ref, t_ref, loss_ref, dx_ref, dw_ref):
        tgt = t_ref[...]

        def f(xv, wv):
            err = _rms(xv, wv) - tgt
            return 0.5 * jnp.sum(jnp.mean(err * err, axis=-1))

        loss, vjp = jax.vjp(f, x_ref[...], w_ref[...])
        dx, dw = vjp(jnp.ones((), F32))
        dx_ref[...] = dx
        lrow = jnp.full((1, LANES), loss, F32)

        @pl.when(pl.program_id(0) == 0)
        def _():
            dw_ref[...] = dw
            loss_ref[...] = lrow

        @pl.when(pl.program_id(0) > 0)
        def _():
            dw_ref[...] += dw
            loss_ref[...] += lrow

    tile = pl.BlockSpec((ROW_TILE, D_MODEL), lambda i: (i, 0))
    vec = pl.BlockSpec((1, D_MODEL), lambda i: (0, 0))
    return pl.pallas_call(
        body, name="loss_head", grid=(SEQ // ROW_TILE,),
        in_specs=[tile, vec, tile], out_specs=[pl.BlockSpec((1, LANES), lambda i: (0, 0)), tile, vec],
        out_shape=[jax.ShapeDtypeStruct((1, LANES), F32), jax.ShapeDtypeStruct((SEQ, D_MODEL), F32),
                   jax.ShapeDtypeStruct((1, D_MODEL), F32)],
        compiler_params=_cparams(("arbitrary",)),
    )(x, w, target)


S5_T = 256
S5_BLOCKS = [(slice(j * 256, (j + 1) * 256), slice(j * 1024, (j + 1) * 1024)) for j in range(2)]


def _s5_post(ypre, gate, wglu):
    y = _gelu(ypre)
    y = y * jax.nn.sigmoid(_bdot(y, wglu, 1, 0))
    return y * _silu(gate)


def _s5_fwd(proj, bbre, bbim, cre, cim, a2, dvec, wglu, name):
    def body(u_ref, g_ref, bbre_ref, bbim_ref, cre_ref, cim_ref, a_ref, d_ref, wg_ref, o_ref, sre_ref, sim_ref, st):
        @pl.when(pl.program_id(0) == 0)
        def _():
            st[...] = jnp.zeros_like(st)

        u = u_ref[...]
        ub = u.astype(BF16)
        for us, ss in S5_BLOCKS:
            sre_ref[:, ss] = _dg(ub[:, us], bbre_ref[us, ss], 1, 0)
            sim_ref[:, ss] = _dg(ub[:, us], bbim_ref[us, ss], 1, 0)
        ar, ai = a_ref[0:1, :], a_ref[1:2, :]

        def step(t, carry):
            sr, si = carry
            nr = ar * sr - ai * si + sre_ref[pl.ds(t, 1), :]
            ni = ar * si + ai * sr + sim_ref[pl.ds(t, 1), :]
            sre_ref[pl.ds(t, 1), :] = nr
            sim_ref[pl.ds(t, 1), :] = ni
            return nr, ni

        sr, si = lax.fori_loop(0, S5_T, step, (st[0:1, :], st[1:2, :]), unroll=8)
        st[0:1, :] = sr
        st[1:2, :] = si
        ypre = jnp.concatenate(
            [_dg(sre_ref[:, ss].astype(BF16), cre_ref[ss, us], 1, 0) - _dg(sim_ref[:, ss].astype(BF16), cim_ref[ss, us], 1, 0)
             for us, ss in S5_BLOCKS], axis=1) + d_ref[...] * u
        o_ref[...] = _s5_post(ypre, g_ref[...], wg_ref[...]).astype(BF16)

    full = lambda shape: pl.BlockSpec(shape, lambda c: (0, 0))
    return pl.pallas_call(
        body, name=name, grid=(SEQ // S5_T,),
        in_specs=[pl.BlockSpec((S5_T, BW), lambda c: (c, C_S5U // BW)), pl.BlockSpec((S5_T, BW), lambda c: (c, C_S5G // BW)),
                  full((BW, S5_CH)), full((BW, S5_CH)), full((S5_CH, BW)), full((S5_CH, BW)),
                  full((2, S5_CH)), full((1, BW)), full((BW, BW))],
        out_specs=[pl.BlockSpec((S5_T, BW), lambda c: (c, 0)), pl.BlockSpec((S5_T, S5_CH), lambda c: (c, 0)),
                   pl.BlockSpec((S5_T, S5_CH), lambda c: (c, 0))],
        out_shape=[jax.ShapeDtypeStruct((SEQ, BW), BF16), jax.ShapeDtypeStruct((SEQ, S5_CH), F32),
                   jax.ShapeDtypeStruct((SEQ, S5_CH), F32)],
        scratch_shapes=[pltpu.VMEM((2, S5_CH), F32)],
        compiler_params=_cparams(("arbitrary",)),
    )(proj, proj, bbre, bbim, cre, cim, a2, dvec, wglu)


def _s5_bwd(proj, dproj, dout, sre, sim, bbre, bbim, cre, cim, a2, dvec, wglu, name):
    nc = SEQ // S5_T

    def body(u_ref, g_ref, do_ref, sre_ref, sim_ref, pre_ref, pim_ref, bbre_ref, bbim_ref, cre_ref, cim_ref, a_ref,
             d_ref, wg_ref, dproj_in, dp_ref, dbbre_ref, dbbim_ref, dcre_ref, dcim_ref, da_ref, dd_ref, dwg_ref,
             gre, gim, st):
        c = nc - 1 - pl.program_id(0)

        @pl.when(pl.program_id(0) == 0)
        def _():
            st[...] = jnp.zeros_like(st)
            for r in (dbbre_ref, dbbim_ref, dcre_ref, dcim_ref, da_ref, dd_ref, dwg_ref):
                r[...] = jnp.zeros_like(r)

        u = u_ref[...]
        s_re, s_im = sre_ref[...], sim_ref[...]

        def head(s_res, s_ims, cres, cims, dv, uv, gv, wg):
            ypre = jnp.concatenate([_bdot(sr, cr, 1, 0) - _bdot(si, ci, 1, 0)
                                    for sr, si, cr, ci in zip(s_res, s_ims, cres, cims)], axis=1) + dv * uv
            return _s5_post(ypre, gv, wg)

        _, vjp = jax.vjp(head, [sre_ref[:, ss] for _, ss in S5_BLOCKS], [sim_ref[:, ss] for _, ss in S5_BLOCKS],
                         [cre_ref[ss, us].astype(F32) for us, ss in S5_BLOCKS],
                         [cim_ref[ss, us].astype(F32) for us, ss in S5_BLOCKS],
                         d_ref[...], u, g_ref[...], wg_ref[...].astype(F32))
        ds_res, ds_ims, dcres, dcims, dd, du_d, dgate, dwg = vjp(do_ref[0])
        for k, (us, ss) in enumerate(S5_BLOCKS):
            dcre_ref[ss, us] += dcres[k]
            dcim_ref[ss, us] += dcims[k]
            gre[:, ss] = ds_res[k]
            gim[:, ss] = ds_ims[k]
        dd_ref[...] += dd
        dwg_ref[...] += dwg
        dp_ref[:, BW:] = dgate.astype(BF16)
        ar, ai = a_ref[0:1, :], a_ref[1:2, :]

        def step(i, carry):
            t = S5_T - 1 - i
            gr, gi = carry
            nr = gre[pl.ds(t, 1), :] + gr
            ni = gim[pl.ds(t, 1), :] + gi
            gre[pl.ds(t, 1), :] = nr
            gim[pl.ds(t, 1), :] = ni
            return ar * nr + ai * ni, ar * ni - ai * nr

        gr, gi = lax.fori_loop(0, S5_T, step, (st[0:1, :], st[1:2, :]), unroll=8)
        st[0:1, :] = gr
        st[1:2, :] = gi
        g_re, g_im = gre[...], gim[...]
        first = jnp.where(c > 0, 1.0, 0.0)
        row = lax.broadcasted_iota(jnp.int32, (S5_T, S5_CH), 0)
        p_re = jnp.where(row == 0, pre_ref[7:8, :] * first, pltpu.roll(s_re, 1, 0))
        p_im = jnp.where(row == 0, pim_ref[7:8, :] * first, pltpu.roll(s_im, 1, 0))
        da_ref[0:1, :] += jnp.sum(g_re * p_re + g_im * p_im, axis=0, keepdims=True)
        da_ref[1:2, :] += jnp.sum(g_im * p_re - g_re * p_im, axis=0, keepdims=True)
        ub, grb, gib = u.astype(BF16), g_re.astype(BF16), g_im.astype(BF16)
        du_s = []
        for us, ss in S5_BLOCKS:
            dbbre_ref[us, ss] += _dg(ub[:, us], grb[:, ss], 0, 0)
            dbbim_ref[us, ss] += _dg(ub[:, us], gib[:, ss], 0, 0)
            du_s.append(_dg(grb[:, ss], bbre_ref[us, ss], 1, 1) + _dg(gib[:, ss], bbim_ref[us, ss], 1, 1))
        dp_ref[:, :BW] = (du_d + jnp.concatenate(du_s, axis=1)).astype(BF16)

    full = lambda shape: pl.BlockSpec(shape, lambda i: (0, 0))
    rev = lambda w, col=0: pl.BlockSpec((S5_T, w), lambda i: (nc - 1 - i, col))
    prev = pl.BlockSpec((8, S5_CH), lambda i: (jnp.maximum((nc - 1 - i) * (S5_T // 8) - 1, 0), 0))
    return pl.pallas_call(
        body, name=name, grid=(nc,),
        in_specs=[rev(BW, C_S5U // BW), rev(BW, C_S5G // BW), pl.BlockSpec((1, S5_T, BW), lambda i: (0, nc - 1 - i, 0)),
                  rev(S5_CH), rev(S5_CH), prev, prev,
                  full((BW, S5_CH)), full((BW, S5_CH)), full((S5_CH, BW)), full((S5_CH, BW)),
                  full((2, S5_CH)), full((1, BW)), full((BW, BW)), pl.BlockSpec(memory_space=pl.ANY)],
        out_specs=[rev(2 * BW, C_S5U // (2 * BW)), full((BW, S5_CH)), full((BW, S5_CH)), full((S5_CH, BW)), full((S5_CH, BW)),
                   full((2, S5_CH)), full((1, BW)), full((BW, BW))],
        input_output_aliases={14: 0},
        out_shape=[jax.ShapeDtypeStruct((SEQ, IN_PAD), BF16),
                   jax.ShapeDtypeStruct((BW, S5_CH), F32), jax.ShapeDtypeStruct((BW, S5_CH), F32),
                   jax.ShapeDtypeStruct((S5_CH, BW), F32), jax.ShapeDtypeStruct((S5_CH, BW), F32),
                   jax.ShapeDtypeStruct((2, S5_CH), F32), jax.ShapeDtypeStruct((1, BW), F32),
                   jax.ShapeDtypeStruct((BW, BW), F32)],
        scratch_shapes=[pltpu.VMEM((S5_T, S5_CH), F32), pltpu.VMEM((S5_T, S5_CH), F32), pltpu.VMEM((2, S5_CH), F32)],
        compiler_params=_cparams(("arbitrary",)),
    )(proj, proj, dout, sre, sim, sre, sim, bbre, bbim, cre, cim, a2, dvec, wglu, dproj)


def _s5_disc(lam_re, lam_im, b_re, b_im, c_re, c_im, d, log_step):
    step = jnp.exp(log_step)[:, None]
    mag = jnp.exp(lam_re * step)
    ab_re, ab_im = mag * jnp.cos(lam_im * step), mag * jnp.sin(lam_im * step)
    den = lam_re * lam_re + lam_im * lam_im
    nr = ab_re - 1.0
    coef_re = (nr * lam_re + ab_im * lam_im) / den
    coef_im = (ab_im * lam_re - nr * lam_im) / den
    bb_re = coef_re[..., None] * b_re - coef_im[..., None] * b_im
    bb_im = coef_re[..., None] * b_im + coef_im[..., None] * b_re
    def block_diag(t, rows_per, cols_per):
        wide = jnp.tile(t.reshape(S5_GROUPS * rows_per, cols_per), (1, S5_GROUPS))
        r = lax.broadcasted_iota(jnp.int32, wide.shape, 0) // rows_per
        c = lax.broadcasted_iota(jnp.int32, wide.shape, 1) // cols_per
        return jnp.where(r == c, wide, 0.0)

    bbre = block_diag(jnp.swapaxes(bb_re, 1, 2), S5_P, S5_STATE)
    bbim = block_diag(jnp.swapaxes(bb_im, 1, 2), S5_P, S5_STATE)
    cre = block_diag(jnp.swapaxes(c_re, 1, 2), S5_STATE, S5_P)
    cim = block_diag(jnp.swapaxes(c_im, 1, 2), S5_STATE, S5_P)
    a2 = jnp.stack([ab_re.reshape(-1), ab_im.reshape(-1)])
    return bbre, bbim, cre, cim, a2, d.reshape(1, BW)


def _left_lanes(shape):
    return lax.broadcasted_iota(jnp.int32, shape, 1) < 64


def _sgu_chunk(u, v, gate, ln_w, ln_b, w, bias):
    u32, v32 = _gelu(u), _gelu(v)
    mu = jnp.mean(v32, axis=-1, keepdims=True)
    var = jnp.mean(jnp.square(v32 - mu), axis=-1, keepdims=True)
    vn = (v32 - mu) * lax.rsqrt(var + EPS) * ln_w + ln_b
    t_i = lax.broadcasted_iota(jnp.int32, (SGU_CHUNK, SGU_CHUNK), 0)
    s_i = lax.broadcasted_iota(jnp.int32, (SGU_CHUNK, SGU_CHUNK), 1)
    causal = t_i >= s_i
    left = _left_lanes((SGU_CHUNK, LANES))
    sgate = _silu(gate)
    outs = []
    for j in range(BW // LANES):
        vb = vn[:, j * LANES:(j + 1) * LANES]
        s_blk = (_bdot(jnp.where(causal, w[2 * j], 0.0), jnp.where(left, vb, 0.0), 1, 0)
                 + _bdot(jnp.where(causal, w[2 * j + 1], 0.0), jnp.where(left, 0.0, vb), 1, 0))
        sl = slice(j * LANES, (j + 1) * LANES)
        outs.append(u32[:, sl] * (s_blk + bias[:, sl]) * sgate[:, sl])
    return outs


def _sgu_fwd(proj, ln_w, ln_b, w, bias, name):
    def body(u_ref, v_ref, g_ref, lw_ref, lb_ref, w_ref, b_ref, o_ref):
        outs = _sgu_chunk(u_ref[...], v_ref[...], g_ref[...], lw_ref[...], lb_ref[...], w_ref[...], b_ref[...])
        for j, o in enumerate(outs):
            o_ref[:, j * LANES:(j + 1) * LANES] = o.astype(BF16)

    blk = lambda col: pl.BlockSpec((SGU_CHUNK, BW), lambda c: (c, col // BW))
    vec = pl.BlockSpec((1, BW), lambda c: (0, 0))
    return pl.pallas_call(
        body, name=name, grid=(SEQ // SGU_CHUNK,),
        in_specs=[blk(C_SGU_U), blk(C_SGU_V), blk(C_SGU_G), vec, vec,
                  pl.BlockSpec((SGU_HEADS, SGU_CHUNK, SGU_CHUNK), lambda c: (0, 0, 0)),
                  pl.BlockSpec((SGU_CHUNK, BW), lambda c: (0, 0))],
        out_specs=pl.BlockSpec((SGU_CHUNK, BW), lambda c: (c, 0)),
        out_shape=jax.ShapeDtypeStruct((SEQ, BW), BF16),
        compiler_params=_cparams(("parallel",)),
    )(proj, proj, proj, ln_w, ln_b, w, bias)


def _sgu_bwd(proj, dproj, dout, ln_w, ln_b, w, bias, name):
    def body(u_ref, v_ref, g_ref, do_ref, lw_ref, lb_ref, w_ref, b_ref, dproj_in, dp_ref, dlw_ref, dlb_ref, dw_ref, db_ref):
        _, vjp = jax.vjp(_sgu_chunk, u_ref[...], v_ref[...], g_ref[...], lw_ref[...], lb_ref[...], w_ref[...], b_ref[...])
        do = do_ref[0]
        du, dv, dgate, dlw, dlb, dw, db = vjp([do[:, j * LANES:(j + 1) * LANES] for j in range(BW // LANES)])
        dp_ref[:, 0:BW] = du.astype(BF16)
        dp_ref[:, BW:2 * BW] = dv.astype(BF16)
        dp_ref[:, 2 * BW:3 * BW] = dgate.astype(BF16)
        dp_ref[:, 3 * BW:] = jnp.zeros((SGU_CHUNK, BW), BF16)

        @pl.when(pl.program_id(0) == 0)
        def _():
            dlw_ref[...] = dlw
            dlb_ref[...] = dlb
            dw_ref[...] = dw
            db_ref[...] = db

        @pl.when(pl.program_id(0) > 0)
        def _():
            dlw_ref[...] += dlw
            dlb_ref[...] += dlb
            dw_ref[...] += dw
            db_ref[...] += db

    blk = lambda col: pl.BlockSpec((SGU_CHUNK, BW), lambda c: (c, col // BW))
    vec = pl.BlockSpec((1, BW), lambda c: (0, 0))
    wsp = pl.BlockSpec((SGU_HEADS, SGU_CHUNK, SGU_CHUNK), lambda c: (0, 0, 0))
    bsp = pl.BlockSpec((SGU_CHUNK, BW), lambda c: (0, 0))
    return pl.pallas_call(
        body, name=name, grid=(SEQ // SGU_CHUNK,),
        in_specs=[blk(C_SGU_U), blk(C_SGU_V), blk(C_SGU_G), pl.BlockSpec((1, SGU_CHUNK, BW), lambda c: (1, c, 0)),
                  vec, vec, wsp, bsp, pl.BlockSpec(memory_space=pl.ANY)],
        out_specs=[pl.BlockSpec((SGU_CHUNK, 4 * BW), lambda c: (c, C_SGU_U // (4 * BW))), vec, vec, wsp, bsp],
        input_output_aliases={8: 0},
        out_shape=[jax.ShapeDtypeStruct((SEQ, IN_PAD), BF16), jax.ShapeDtypeStruct((1, BW), F32),
                   jax.ShapeDtypeStruct((1, BW), F32), jax.ShapeDtypeStruct((SGU_HEADS, SGU_CHUNK, SGU_CHUNK), F32),
                   jax.ShapeDtypeStruct((SGU_CHUNK, BW), F32)],
        compiler_params=_cparams(("arbitrary",)),
    )(proj, proj, proj, dout, ln_w, ln_b, w, bias, dproj)


CONV_BLK = 256


def _m2_conv_fwd(proj, w, b, name):
    def body(x_ref, w_ref, b_ref, o_ref):
        x = x_ref[...]
        acc = jnp.zeros_like(x) + b_ref[...]
        for k in range(M2_CONV):
            acc = acc + w_ref[k:k + 1, :] * _shift_down(x, M2_CONV - 1 - k)
        o_ref[...] = _silu(acc)

    return pl.pallas_call(
        body, name=name, grid=(M2_CONV_CH // CONV_BLK,),
        in_specs=[pl.BlockSpec((SEQ, CONV_BLK), lambda j: (0, C_M2X // CONV_BLK + j)),
                  pl.BlockSpec((M2_CONV, CONV_BLK), lambda j: (0, j)), pl.BlockSpec((1, CONV_BLK), lambda j: (0, j))],
        out_specs=pl.BlockSpec((SEQ, CONV_BLK), lambda j: (0, j)),
        out_shape=jax.ShapeDtypeStruct((SEQ, M2_CONV_CH), F32),
        compiler_params=_cparams(("parallel",)),
    )(proj, w, b)


def _m2_conv_bwd(proj, dproj, dxa, w, b, name):
    def body(x_ref, d_ref, w_ref, b_ref, dproj_in, dx_ref, dw_ref, db_ref):
        x = x_ref[...]
        xs = [_shift_down(x, M2_CONV - 1 - k) for k in range(M2_CONV)]
        acc = jnp.zeros_like(x) + b_ref[...]
        for k in range(M2_CONV):
            acc = acc + w_ref[k:k + 1, :] * xs[k]
        sg = jax.nn.sigmoid(acc)
        dacc = d_ref[...] * (sg * (1.0 + acc * (1.0 - sg)))
        dx = jnp.zeros_like(x)
        for k in range(M2_CONV):
            dx = dx + w_ref[k:k + 1, :] * _shift_up(dacc, M2_CONV - 1 - k)
            dw_ref[k:k + 1, :] = jnp.sum(dacc * xs[k], axis=0, keepdims=True)
        dx_ref[...] = dx.astype(BF16)
        db_ref[...] = jnp.sum(dacc, axis=0, keepdims=True)

    return pl.pallas_call(
        body, name=name, grid=(M2_CONV_CH // CONV_BLK,),
        in_specs=[pl.BlockSpec((SEQ, CONV_BLK), lambda j: (0, C_M2X // CONV_BLK + j)),
                  pl.BlockSpec((SEQ, CONV_BLK), lambda j: (0, j)),
                  pl.BlockSpec((M2_CONV, CONV_BLK), lambda j: (0, j)), pl.BlockSpec((1, CONV_BLK), lambda j: (0, j)),
                  pl.BlockSpec(memory_space=pl.ANY)],
        out_specs=[pl.BlockSpec((SEQ, CONV_BLK), lambda j: (0, C_M2X // CONV_BLK + j)),
                   pl.BlockSpec((M2_CONV, CONV_BLK), lambda j: (0, j)), pl.BlockSpec((1, CONV_BLK), lambda j: (0, j))],
        input_output_aliases={4: 0},
        out_shape=[jax.ShapeDtypeStruct((SEQ, IN_PAD), BF16), jax.ShapeDtypeStruct((M2_CONV, M2_CONV_CH), F32),
                   jax.ShapeDtypeStruct((1, M2_CONV_CH), F32)],
        compiler_params=_cparams(("parallel",)),
    )(proj, dxa, w, b, dproj)


N_PAIR = M2_HEADS // 2
HI = lax.Precision.HIGHEST


def _col(a, h):
    lane = lax.broadcasted_iota(jnp.int32, a.shape, 1)
    return jnp.sum(jnp.where(lane == h, a, 0.0), axis=1, keepdims=True)


def _row(a, h):
    sub = lax.broadcasted_iota(jnp.int32, a.shape, 0)
    return jnp.sum(jnp.where(sub == h, a, 0.0), axis=0, keepdims=True)


def _ssd_chunk(xs, bms, cms, dtr, zs, states, dt_bias, a_log, dfs, nws):
    q = M2_CHUNK
    dt = _softplus(dtr + dt_bias)
    da = dt * (-jnp.exp(a_log))
    l_i = lax.broadcasted_iota(jnp.int32, (q, q), 0)
    s_i = lax.broadcasted_iota(jnp.int32, (q, q), 1)
    causal = l_i >= s_i
    tril = jnp.where(causal, 1.0, 0.0)
    a_cs = _dg(tril, da, 1, 0, HI)
    a_cs_t = _dg(da, tril, 0, 1, HI)
    a_end = _row(a_cs, q - 1)
    left = _left_lanes((q, LANES))
    left1 = _left_lanes((1, LANES))
    ys, nexts = [], []
    for j in range(N_PAIR):
        grp = j // 2
        bm, cm = bms[grp], cms[grp]
        h0, h1 = 2 * j, 2 * j + 1
        cb = _bdot(cm, bm, 1, 1)
        xdt = xs[j] * jnp.where(left, _col(dt, h0), _col(dt, h1))
        acs0, acs1 = _col(a_cs, h0), _col(a_cs, h1)
        y = _bdot(cm, states[j], 1, 0) * jnp.where(left, jnp.exp(acs0), jnp.exp(acs1))
        s_new = states[j] * jnp.where(left1, jnp.exp(_col(a_end, h0)), jnp.exp(_col(a_end, h1)))
        for h, acs, xh in ((h0, acs0, jnp.where(left, xdt, 0.0)), (h1, acs1, jnp.where(left, 0.0, xdt))):
            decay = jnp.exp(jnp.where(causal, acs - _row(a_cs_t, h), -jnp.inf))
            y = y + _bdot(cb * decay, xh, 1, 0)
            s_new = s_new + _bdot(bm * jnp.exp(_col(a_end, h) - acs), xh, 0, 0)
        ys.append((y + dfs[j] * xs[j]) * _silu(zs[j]))
        nexts.append(s_new)
    ssq = sum(jnp.sum(y * y, axis=-1, keepdims=True) for y in ys)
    scale = lax.rsqrt(ssq / BW + EPS)
    return [y * scale * nw for y, nw in zip(ys, nws)], nexts


def _blocks(ref, n, width=LANES):
    return [ref[:, j * width:(j + 1) * width] for j in range(n)]


def _ssd_fwd(proj, xa, dt_bias, a_log, dfull, nw, name):
    nc = SEQ // M2_CHUNK

    def body(x_ref, b_ref, c_ref, dt_ref, z_ref, dtb_ref, al_ref, df_ref, nw_ref, o_ref, sin_ref, st):
        @pl.when(pl.program_id(0) == 0)
        def _():
            st[...] = jnp.zeros_like(st)

        states = [st[j] for j in range(N_PAIR)]
        for j in range(N_PAIR):
            sin_ref[0, j] = states[j]
        ys, nexts = _ssd_chunk(_blocks(x_ref, 4), _blocks(b_ref, 2), _blocks(c_ref, 2), dt_ref[...], _blocks(z_ref, 4),
                               states, dtb_ref[...], al_ref[...], _blocks(df_ref, 4), _blocks(nw_ref, 4))
        for j in range(N_PAIR):
            o_ref[:, j * LANES:(j + 1) * LANES] = ys[j].astype(BF16)
            st[j] = nexts[j]

    vec8 = pl.BlockSpec((1, LANES), lambda c: (0, 0))
    vec = pl.BlockSpec((1, BW), lambda c: (0, 0))
    return pl.pallas_call(
        body, name=name, grid=(nc,),
        in_specs=[pl.BlockSpec((M2_CHUNK, BW), lambda c: (c, 0)), pl.BlockSpec((M2_CHUNK, 256), lambda c: (c, 2)),
                  pl.BlockSpec((M2_CHUNK, 256), lambda c: (c, 3)), pl.BlockSpec((M2_CHUNK, LANES), lambda c: (c, C_DT // LANES)),
                  pl.BlockSpec((M2_CHUNK, BW), lambda c: (c, C_M2Z // BW)), vec8, vec8, vec, vec],
        out_specs=[pl.BlockSpec((M2_CHUNK, BW), lambda c: (c, 0)),
                   pl.BlockSpec((1, N_PAIR, M2_STATE, LANES), lambda c: (c, 0, 0, 0))],
        out_shape=[jax.ShapeDtypeStruct((SEQ, BW), BF16), jax.ShapeDtypeStruct((nc, N_PAIR, M2_STATE, LANES), F32)],
        scratch_shapes=[pltpu.VMEM((N_PAIR, M2_STATE, LANES), F32)],
        compiler_params=_cparams(("arbitrary",)),
    )(xa, xa, xa, proj, proj, dt_bias, a_log, dfull, nw)


def _ssd_bwd(proj, dproj, xa, dout, s_in, dt_bias, a_log, dfull, nw, name):
    nc = SEQ // M2_CHUNK

    def body(x_ref, b_ref, c_ref, dt_ref, z_ref, do_ref, sin_ref, dtb_ref, al_ref, df_ref, nw_ref, dproj_in,
             dp_ref, dxa_ref, ddtb_ref, dal_ref, ddf_ref, dnw_ref, dst):
        @pl.when(pl.program_id(0) == 0)
        def _():
            dst[...] = jnp.zeros_like(dst)
            for r in (ddtb_ref, dal_ref, ddf_ref, dnw_ref):
                r[...] = jnp.zeros_like(r)

        states = [sin_ref[0, j] for j in range(N_PAIR)]
        _, vjp = jax.vjp(_ssd_chunk, _blocks(x_ref, 4), _blocks(b_ref, 2), _blocks(c_ref, 2), dt_ref[...],
                         _blocks(z_ref, 4), states, dtb_ref[...], al_ref[...], _blocks(df_ref, 4), _blocks(nw_ref, 4))
        dxs, dbs, dcs, ddt, dzs, dstates, ddtb, dal, ddfs, dnws = vjp(
            ([do_ref[0, :, j * LANES:(j + 1) * LANES] for j in range(N_PAIR)], [dst[j] for j in range(N_PAIR)]))
        for j in range(N_PAIR):
            sl = slice(j * LANES, (j + 1) * LANES)
            dxa_ref[:, sl] = dxs[j]
            dp_ref[:, sl] = dzs[j].astype(BF16)
            dst[j] = dstates[j]
            ddf_ref[:, sl] += ddfs[j]
            dnw_ref[:, sl] += dnws[j]
        for g in range(2):
            dxa_ref[:, BW + g * LANES:BW + (g + 1) * LANES] = dbs[g]
            dxa_ref[:, BW + 256 + g * LANES:BW + 256 + (g + 1) * LANES] = dcs[g]
        dp_ref[:, BW:BW + LANES] = ddt.astype(BF16)
        dp_ref[:, BW + LANES:] = jnp.zeros((M2_CHUNK, 2 * BW - BW - LANES), BF16)
        ddtb_ref[...] += ddtb
        dal_ref[...] += dal

    rev = lambda w, col=0: pl.BlockSpec((M2_CHUNK, w), lambda i: (nc - 1 - i, col))
    vec8 = pl.BlockSpec((1, LANES), lambda i: (0, 0))
    vec = pl.BlockSpec((1, BW), lambda i: (0, 0))
    return pl.pallas_call(
        body, name=name, grid=(nc,),
        in_specs=[rev(BW), rev(256, 2), rev(256, 3), rev(LANES, C_DT // LANES), rev(BW, C_M2Z // BW),
                  pl.BlockSpec((1, M2_CHUNK, BW), lambda i: (2, nc - 1 - i, 0)),
                  pl.BlockSpec((1, N_PAIR, M2_STATE, LANES), lambda i: (nc - 1 - i, 0, 0, 0)), vec8, vec8, vec, vec,
                  pl.BlockSpec(memory_space=pl.ANY)],
        out_specs=[rev(2 * BW, C_M2Z // (2 * BW)), rev(M2_CONV_CH), vec8, vec8, vec, vec],
        input_output_aliases={11: 0},
        out_shape=[jax.ShapeDtypeStruct((SEQ, IN_PAD), BF16), jax.ShapeDtypeStruct((SEQ, M2_CONV_CH), F32),
                   jax.ShapeDtypeStruct((1, LANES), F32), jax.ShapeDtypeStruct((1, LANES), F32),
                   jax.ShapeDtypeStruct((1, BW), F32), jax.ShapeDtypeStruct((1, BW), F32)],
        scratch_shapes=[pltpu.VMEM((N_PAIR, M2_STATE, LANES), F32)],
        compiler_params=_cparams(("arbitrary",)),
    )(xa, xa, xa, proj, proj, dout, s_in, dt_bias, a_log, dfull, nw, dproj)


def _sc_specs():
    col = lambda kind: pl.BlockSpec((SEQ, LANES), lambda j: (0, C_SC // LANES + 4 * j + kind))
    return [col(0), col(1), col(2), col(3)]


def _sc_fwd(proj, w, name):
    def body(b_ref, c_ref, h_ref, g_ref, w_ref, o_ref):
        ch = c_ref[...] * h_ref[...]
        acc = jnp.zeros_like(ch)
        for k in range(SC_CONV):
            acc = acc + w_ref[k:k + 1, :] * _shift_down(ch, SC_CONV - 1 - k)
        o_ref[...] = (b_ref[...] * acc * _silu(g_ref[...])).astype(BF16)

    return pl.pallas_call(
        body, name=name, grid=(BW // LANES,),
        in_specs=_sc_specs() + [pl.BlockSpec((SC_CONV, LANES), lambda j: (0, j))],
        out_specs=pl.BlockSpec((SEQ, LANES), lambda j: (0, j)),
        out_shape=jax.ShapeDtypeStruct((SEQ, BW), BF16),
        compiler_params=_cparams(("parallel",)),
    )(proj, proj, proj, proj, w)


def _sc_bwd(proj, dproj, dout, w, name):
    def body(b_ref, c_ref, h_ref, g_ref, do_ref, w_ref, dproj_in, dp_ref, dw_ref):
        cv, hv, gv = c_ref[...], h_ref[...], g_ref[...]
        ch = cv * hv
        chs = [_shift_down(ch, SC_CONV - 1 - k) for k in range(SC_CONV)]
        acc = jnp.zeros_like(ch)
        for k in range(SC_CONV):
            acc = acc + w_ref[k:k + 1, :] * chs[k]
        sg = jax.nn.sigmoid(gv)
        do = do_ref[0]
        bv = b_ref[...]
        dp_ref[:, 0:LANES] = (do * acc * (gv * sg)).astype(BF16)
        dp_ref[:, 3 * LANES:] = (do * bv * acc * (sg * (1.0 + gv * (1.0 - sg)))).astype(BF16)
        dacc = do * bv * (gv * sg)
        dch = jnp.zeros_like(ch)
        for k in range(SC_CONV):
            dch = dch + w_ref[k:k + 1, :] * _shift_up(dacc, SC_CONV - 1 - k)
            dw_ref[k:k + 1, :] = jnp.sum(dacc * chs[k], axis=0, keepdims=True)
        dp_ref[:, LANES:2 * LANES] = (dch * hv).astype(BF16)
        dp_ref[:, 2 * LANES:3 * LANES] = (dch * cv).astype(BF16)

    wsp = pl.BlockSpec((SC_CONV, LANES), lambda j: (0, j))
    return pl.pallas_call(
        body, name=name, grid=(BW // LANES,),
        in_specs=_sc_specs() + [pl.BlockSpec((1, SEQ, LANES), lambda j: (3, 0, j)), wsp, pl.BlockSpec(memory_space=pl.ANY)],
        out_specs=[pl.BlockSpec((SEQ, 4 * LANES), lambda j: (0, C_SC // (4 * LANES) + j)), wsp],
        input_output_aliases={6: 0},
        out_shape=[jax.ShapeDtypeStruct((SEQ, IN_PAD), BF16), jax.ShapeDtypeStruct((SC_CONV, BW), F32)],
        compiler_params=_cparams(("parallel",)),
    )(proj, proj, proj, proj, dout, w, dproj)


MERGE_T = 256
MERGE_BWD_T = 512


def _merge_fwd(proj, ys, merge_b, w_branch, name):
    def body(y_ref, lg_ref, b_ref, w_ref, o_ref):
        acc = jnp.zeros((MERGE_T, D_MODEL), F32)
        for k in range(N_BRANCH):
            gate = jax.nn.sigmoid(lg_ref[:, k * D_MODEL:(k + 1) * D_MODEL] + b_ref[k])
            acc = acc + gate * _dg(y_ref[k], w_ref[k], 1, 0)
        o_ref[...] = acc.astype(BF16)

    return pl.pallas_call(
        body, name=name, grid=(SEQ // MERGE_T,),
        in_specs=[pl.BlockSpec((N_BRANCH, MERGE_T, BW), lambda i: (0, i, 0)),
                  pl.BlockSpec((MERGE_T, N_BRANCH * D_MODEL), lambda i: (i, C_MERGE // (N_BRANCH * D_MODEL))),
                  pl.BlockSpec((N_BRANCH, 1, D_MODEL), lambda i: (0, 0, 0)),
                  pl.BlockSpec((N_BRANCH, BW, D_MODEL), lambda i: (0, 0, 0))],
        out_specs=pl.BlockSpec((MERGE_T, D_MODEL), lambda i: (i, 0)),
        out_shape=jax.ShapeDtypeStruct((SEQ, D_MODEL), BF16),
        compiler_params=_cparams(("parallel",)),
    )(ys, proj, merge_b, w_branch)


def _merge_bwd(proj, ys, dm, merge_b, w_branch, name):
    nt = SEQ // MERGE_BWD_T

    def body(y_ref, lg_ref, dm_ref, b_ref, w_ref, dy_ref, dlg_ref, dw_ref, db_ref, dw_acc):
        i = pl.program_id(1)
        gate = jax.nn.sigmoid(lg_ref[...] + b_ref[0])
        y = y_ref[0]
        dmv = dm_ref[...]
        dbo = (gate * dmv).astype(BF16)
        dlg = _dg(y, w_ref[0], 1, 0) * dmv * gate * (1.0 - gate)
        dlg_ref[...] = dlg.astype(BF16)
        dy_ref[0] = _dg(dbo, w_ref[0], 1, 1)
        dwp = _dg(y, dbo, 0, 0)
        dbp = jnp.sum(dlg, axis=0, keepdims=True)

        @pl.when(i == 0)
        def _():
            dw_acc[...] = dwp
            db_ref[0] = dbp

        @pl.when(i > 0)
        def _():
            dw_acc[...] += dwp
            db_ref[0] += dbp

        @pl.when(i == nt - 1)
        def _():
            dw_ref[0] = dw_acc[...].astype(BF16)

    return pl.pallas_call(
        body, name=name, grid=(N_BRANCH, nt),
        in_specs=[pl.BlockSpec((1, MERGE_BWD_T, BW), lambda k, i: (k, i, 0)),
                  pl.BlockSpec((MERGE_BWD_T, D_MODEL), lambda k, i: (i, C_MERGE // D_MODEL + k)),
                  pl.BlockSpec((MERGE_BWD_T, D_MODEL), lambda k, i: (i, 0)),
                  pl.BlockSpec((1, 1, D_MODEL), lambda k, i: (k, 0, 0)),
                  pl.BlockSpec((1, BW, D_MODEL), lambda k, i: (k, 0, 0))],
        out_specs=[pl.BlockSpec((1, MERGE_BWD_T, BW), lambda k, i: (k, i, 0)),
                   pl.BlockSpec((MERGE_BWD_T, D_MODEL), lambda k, i: (i, k)),
                   pl.BlockSpec((1, BW, D_MODEL), lambda k, i: (k, 0, 0)),
                   pl.BlockSpec((1, 1, D_MODEL), lambda k, i: (k, 0, 0))],
        out_shape=[jax.ShapeDtypeStruct((N_BRANCH, SEQ, BW), F32), jax.ShapeDtypeStruct((SEQ, IN_PAD), BF16),
                   jax.ShapeDtypeStruct((N_BRANCH, BW, D_MODEL), BF16), jax.ShapeDtypeStruct((N_BRANCH, 1, D_MODEL), F32)],
        scratch_shapes=[pltpu.VMEM((BW, D_MODEL), F32)],
        compiler_params=_cparams(("parallel", "arbitrary")),
    )(ys, proj, dm, merge_b, w_branch)


def _adamw(glist, w, m, v, rows, name):
    nl = len(glist)
    n, r, c = glist[0].shape
    assert w.shape == (nl, r, c) and r % rows == 0
    nb = r // rows

    def body(*refs):
        g_refs = refs[:nl]
        w_ref, m_ref, v_ref, go_ref, d_ref, mo_ref, vo_ref = refs[nl:]
        for layer in range(nl):
            @pl.when(pl.program_id(0) == layer)
            def _(g_ref=g_refs[layer]):
                g = g_ref[0].astype(F32)
                for s in range(1, n):
                    g = g + g_ref[s].astype(F32)
                mn = ADAM_B1 * m_ref[0] + (1.0 - ADAM_B1) * g
                vn = ADAM_B2 * v_ref[0] + (1.0 - ADAM_B2) * jnp.square(g)
                m_hat = mn / (1.0 - ADAM_B1 ** ADAM_STEP)
                v_hat = vn / (1.0 - ADAM_B2 ** ADAM_STEP)
                go_ref[0] = g
                d_ref[0] = -ADAM_LR * (m_hat / (jnp.sqrt(v_hat) + ADAM_EPS) + ADAM_WD * w_ref[0])
                mo_ref[0] = mn
                vo_ref[0] = vn

    def g_spec(layer):
        return pl.BlockSpec((n, rows, c), lambda a, i: (0, jnp.where(a < layer, 0, jnp.where(a == layer, i, nb - 1)), 0))

    blk = pl.BlockSpec((1, rows, c), lambda a, i: (a, i, 0))
    out = jax.ShapeDtypeStruct((nl, r, c), F32)
    return pl.pallas_call(
        body, name=name, grid=(nl, nb),
        in_specs=[g_spec(layer) for layer in range(nl)] + [blk, blk, blk],
        out_specs=[blk, blk, blk, blk], out_shape=[out, out, out, out],
        compiler_params=_cparams(("arbitrary", "arbitrary")),
    )(*glist, w, m, v)


def _slot_sum(gslots, name):
    n, r, c = gslots.shape

    def body(g_ref, o_ref):
        g = g_ref[0]
        for s in range(1, n):
            g = g + g_ref[s]
        o_ref[...] = g

    return pl.pallas_call(
        body, name=name, in_specs=[pl.BlockSpec((n, r, c), lambda: (0, 0, 0))],
        out_specs=pl.BlockSpec((r, c), lambda: (0, 0)), out_shape=jax.ShapeDtypeStruct((r, c), F32),
        compiler_params=_cparams(None),
    )(gslots)


def _me_and_peers():
    x, y, c = lax.axis_index("x"), lax.axis_index("y"), lax.axis_index("c")
    me = 4 * x + 2 * y + c
    peers = []
    for k in range(1, N_DEV):
        px = 1 - x if (k >> 2) & 1 else x
        py = 1 - y if (k >> 1) & 1 else y
        pc = 1 - c if k & 1 else c
        peers.append((4 * px + 2 * py + pc, (px, py, pc)))
    return me, peers


def _exchange(tensors, gather, name):
    n = len(tensors)

    def body(*refs):
        ins, outs = refs[:n], refs[n:2 * n]
        send_sems, recv_sems, local_sems = refs[2 * n:]
        me, peers = _me_and_peers()
        started = []
        for t in range(n):
            own = pltpu.make_async_copy(ins[t] if gather else ins[t].at[me], outs[t].at[me], local_sems.at[t])
            own.start()
            started.append(own)
            for k, (pidx, pos) in enumerate(peers):
                cp = pltpu.make_async_remote_copy(
                    src_ref=ins[t] if gather else ins[t].at[pidx], dst_ref=outs[t].at[me],
                    send_sem=send_sems.at[t, k], recv_sem=recv_sems.at[t, k], device_id=pos, device_id_type=MESH)
                cp.start()
                started.append(cp)
        for cp in started:
            cp.wait()

    any_spec = pl.BlockSpec(memory_space=pl.ANY)
    outs = pl.pallas_call(
        body, name=name, in_specs=[any_spec] * n, out_specs=[any_spec] * n,
        out_shape=[jax.ShapeDtypeStruct(((N_DEV,) + t.shape) if gather else t.shape, t.dtype) for t in tensors],
        scratch_shapes=[pltpu.SemaphoreType.DMA((n, N_DEV - 1)), pltpu.SemaphoreType.DMA((n, N_DEV - 1)),
                        pltpu.SemaphoreType.DMA((n,))],
        compiler_params=pltpu.CompilerParams(has_side_effects=True),
    )(*tensors)
    return list(outs)


_HBM = pl.BlockSpec(memory_space=pltpu.HBM)
_SEM = pl.BlockSpec(memory_space=pltpu.SEMAPHORE)
_EFFECT = pltpu.SideEffectType.DATAFLOW_SIDE_EFFECTING


N_CHIP = N_DEV // 2


def _chip_peers():
    x, y, c = lax.axis_index("x"), lax.axis_index("y"), lax.axis_index("c")
    chips = []
    for d in range(1, N_CHIP):
        px = 1 - x if (d >> 1) & 1 else x
        py = 1 - y if d & 1 else y
        chips.append((2 * px + py, (px, py)))
    return (x, y, c), 2 * x + y, chips


def _plan_gather(ins, lands, send_sems, recv_sems, local_sems, first=0):
    (x, y, c), q, chips = _chip_peers()
    me = 2 * q + c
    plan = dict(start=[], relay_wait=[], relay_start=[], local=[], sends=[], recvs=[])
    for t in range(len(ins)):
        base = (first + t) * 7
        sem = lambda k: dict(send_sem=send_sems.at[base + k], recv_sem=recv_sems.at[base + k], device_id_type=MESH)
        own = pltpu.make_async_copy(ins[t], lands[t].at[me], local_sems.at[first + t])
        to_sib = pltpu.make_async_remote_copy(src_ref=ins[t], dst_ref=lands[t].at[me], device_id=(x, y, 1 - c), **sem(0))
        plan['start'] += [own, to_sib]
        plan['local'].append(own)
        plan['sends'].append(to_sib)
        plan['recvs'].append(to_sib)
        for d, (pq, (px, py)) in enumerate(chips):
            to_chip = pltpu.make_async_remote_copy(src_ref=ins[t], dst_ref=lands[t].at[me], device_id=(px, py, c), **sem(1 + d))
            blk = lands[t].at[2 * pq + c]
            fwd = pltpu.make_async_remote_copy(src_ref=blk, dst_ref=blk, device_id=(x, y, 1 - c), **sem(4 + d))
            plan['start'].append(to_chip)
            plan['relay_wait'].append(to_chip)
            plan['relay_start'].append(fwd)
            plan['sends'] += [to_chip, fwd]
            plan['recvs'].append(fwd)
    return plan


def _plan_pair(ins, lands, send_sems, recv_sems, local_sems):
    (x, y, c), q, chips = _chip_peers()
    plan = dict(start=[], local=[], sends=[], recvs=[])
    for t in range(len(ins)):
        for k in range(N_CHIP):
            cp = pltpu.make_async_remote_copy(
                src_ref=ins[t].at[2 * k + 1 - c], dst_ref=lands[t].at[k], send_sem=send_sems.at[t * N_CHIP + k],
                recv_sem=recv_sems.at[t * N_CHIP + k], device_id=(x, y, 1 - c), device_id_type=MESH)
            plan['start'].append(cp)
            plan['sends'].append(cp)
            plan['recvs'].append(cp)
    return plan


def _plan_chips(ins, lands, send_sems, recv_sems, local_sems):
    (x, y, c), q, chips = _chip_peers()
    plan = dict(start=[], local=[], sends=[], recvs=[])
    for t in range(len(ins)):
        own = pltpu.make_async_copy(ins[t].at[q], lands[t].at[q], local_sems.at[t])
        plan['start'].append(own)
        plan['local'].append(own)
        for d, (pq, (px, py)) in enumerate(chips):
            cp = pltpu.make_async_remote_copy(
                src_ref=ins[t].at[pq], dst_ref=lands[t].at[q], send_sem=send_sems.at[t * 3 + d],
                recv_sem=recv_sems.at[t * 3 + d], device_id=(px, py, c), device_id_type=MESH)
            plan['start'].append(cp)
            plan['sends'].append(cp)
            plan['recvs'].append(cp)
    return plan


def _split_start(plan_fn, tensors, land_shapes, n_sems, name):
    n = len(tensors)

    def body(*refs):
        ins, lands = refs[:n], refs[n:2 * n]
        plan = plan_fn(ins, lands, *refs[2 * n:2 * n + 3])
        for cp in plan['start']:
            cp.start()
        refs[-1][...] = jnp.zeros_like(refs[-1])

    outs = pl.pallas_call(
        body, name=name,
        out_shape=(pltpu.SemaphoreType.DMA((n_sems,)), pltpu.SemaphoreType.DMA((n_sems,)), pltpu.SemaphoreType.DMA((n,)),
                   *[pltpu.HBM(t.shape, t.dtype) for t in tensors],
                   *[pltpu.HBM(s, t.dtype) for s, t in zip(land_shapes, tensors)],
                   jax.ShapeDtypeStruct((8, LANES), F32)),
        in_specs=[_HBM] * (2 * n),
        out_specs=(_SEM, _SEM, _SEM, *[_HBM] * (2 * n), pl.BlockSpec(memory_space=pltpu.VMEM)),
        input_output_aliases={t: 3 + t for t in range(2 * n)},
        compiler_params=pltpu.CompilerParams(has_side_effects=_EFFECT),
    )(*[pltpu.with_memory_space_constraint(t, pltpu.HBM) for t in tensors],
      *[pltpu.with_memory_space_constraint(lax.empty(s, t.dtype), pltpu.HBM) for s, t in zip(land_shapes, tensors)])
    return outs[:-1], outs[-1]


def _split_relay(plan_fn, state, after, name):
    sems, thru = state[:3], state[3:]
    n = len(thru) // 2

    def arrived(*refs):
        plan = plan_fn(refs[:n], refs[n:2 * n], *refs[2 * n:2 * n + 3])
        for cp in plan['relay_wait']:
            cp.wait_recv()

    thru = pl.pallas_call(
        arrived, name=name + "_arrived",
        out_shape=tuple(pltpu.HBM(t.shape, t.dtype) for t in thru),
        in_specs=[_HBM] * (2 * n) + [_SEM, _SEM, _SEM, pl.BlockSpec(memory_space=pl.ANY)],
        out_specs=tuple([_HBM] * (2 * n)),
        input_output_aliases={t: t for t in range(2 * n)},
        compiler_params=pltpu.CompilerParams(has_side_effects=_EFFECT),
    )(*thru, *sems, after)

    def forward(*refs):
        plan = plan_fn(refs[:n], refs[n:2 * n], *refs[2 * n:2 * n + 3])
        for cp in plan['relay_start']:
            cp.start()
        refs[-1][...] = jnp.zeros_like(refs[-1])

    outs = pl.pallas_call(
        forward, name=name + "_forward",
        out_shape=(*[pltpu.HBM(t.shape, t.dtype) for t in thru], jax.ShapeDtypeStruct((8, LANES), F32)),
        in_specs=[_HBM] * (2 * n) + [_SEM, _SEM, _SEM],
        out_specs=(*[_HBM] * (2 * n), pl.BlockSpec(memory_space=pltpu.VMEM)),
        input_output_aliases={t: t for t in range(2 * n)},
        compiler_params=pltpu.CompilerParams(has_side_effects=_EFFECT),
    )(*thru, *sems)
    return (*sems, *outs[:-1]), outs[-1]


def _split_wait(plan_fn, state, after, name, with_sources=False):
    sems, thru = state[:3], state[3:]
    n = len(thru) // 2

    def body(*refs):
        plan = plan_fn(refs[:n], refs[n:2 * n], *refs[2 * n:2 * n + 3])
        for cp in plan['local']:
            cp.wait()
        for cp in plan['sends']:
            cp.wait_send()
        for cp in plan['recvs']:
            cp.wait_recv()

    outs = pl.pallas_call(
        body, name=name,
        out_shape=tuple(pltpu.HBM(t.shape, t.dtype) for t in thru),
        in_specs=[_HBM] * (2 * n) + [_SEM, _SEM, _SEM, pl.BlockSpec(memory_space=pl.ANY)],
        out_specs=tuple([_HBM] * (2 * n)),
        input_output_aliases={t: t for t in range(2 * n)},
        compiler_params=pltpu.CompilerParams(has_side_effects=_EFFECT),
    )(*thru, *sems, after)
    return (list(outs[:n]), list(outs[n:])) if with_sources else list(outs[n:])


PAIR_SUM_BLOCK = 512 * 1024


def _pair_sum(mine, theirs, name):
    _, r, c = mine.shape
    rows = r
    while rows * c > PAIR_SUM_BLOCK and rows % 32 == 0:
        rows //= 2

    def body(core_ref, a_ref, b_ref, o_ref):
        o_ref[0] = (a_ref[0].astype(F32) + b_ref[0].astype(F32)).astype(o_ref.dtype)

    return pl.pallas_call(
        body, name=name,
        grid_spec=pltpu.PrefetchScalarGridSpec(
            num_scalar_prefetch=1, grid=(N_CHIP, r // rows),
            in_specs=[pl.BlockSpec((1, rows, c), lambda k, i, core: (2 * k + core[0], i, 0)),
                      pl.BlockSpec((1, rows, c), lambda k, i, core: (k, i, 0))],
            out_specs=pl.BlockSpec((1, rows, c), lambda k, i, core: (k, i, 0))),
        out_shape=jax.ShapeDtypeStruct((N_CHIP, r, c), mine.dtype),
        compiler_params=_cparams(("parallel", "parallel")),
    )(lax.axis_index("c").astype(jnp.int32).reshape(1), mine, theirs)


WEIGHTS = ['norm_w', 'w_in', 's5_lambda_re', 's5_lambda_im', 's5_b_re', 's5_b_im', 's5_c_re', 's5_c_im', 's5_d',
           's5_log_step', 's5_w_glu', 'sgu_ln_w', 'sgu_ln_b', 'sgu_w', 'sgu_b', 'm2_conv_w', 'm2_conv_b', 'm2_dt_bias',
           'm2_a_log', 'm2_d', 'm2_norm_w', 'sc_conv_w', 'merge_b', 'w_branch', 'w_out', 'final_norm_w']
BIG_SHARDED = ['w_in', 'w_branch', 'w_out', 's5_w_glu']
SMALL_SHARDED = ['m2_conv_w', 'sc_conv_w', 'merge_b']
REPLICATED = [n for n in WEIGHTS if n not in BIG_SHARDED + SMALL_SHARDED]
S5_NAMES = ['s5_lambda_re', 's5_lambda_im', 's5_b_re', 's5_b_im', 's5_c_re', 's5_c_im', 's5_d', 's5_log_step']


def _sc_interleave(t):
    lead = t.shape[:-1]
    return jnp.swapaxes(t.reshape(lead + (4, 4, LANES)), -3, -2).reshape(lead + (4 * BW,))


def _pad_in(w):
    z = lambda n: jnp.zeros(w.shape[:-1] + (n,), w.dtype)
    return jnp.concatenate([w[..., 6152:], w[..., 0:1024], w[..., 3072:4096], w[..., 1024:2560], z(512),
                            w[..., 2560:3072], w[..., 4096:4104], z(504), _sc_interleave(w[..., 4104:6152])], axis=-1)


def _unpad_in(g):
    return jnp.concatenate([g[..., C_S5U:C_S5U + 1024], g[..., C_SGU_U:C_SGU_U + 1536], g[..., C_M2Z:C_M2Z + 512],
                            g[..., C_M2X:C_M2X + 1024], g[..., C_DT:C_DT + 8], _sc_interleave(g[..., C_SC:]),
                            g[..., :N_BRANCH * D_MODEL]], axis=-1)


ROW_BLOCK = 8 * LANES


def _pack_rows(tensors, row_mult, batched=False):
    parts = []
    for t in tensors:
        f = t.reshape((t.shape[0], -1) if batched else (1, -1))
        f = jnp.pad(f, ((0, 0), (0, (-f.shape[1]) % ROW_BLOCK)))
        parts.append(f.reshape(f.shape[0], -1, LANES))
    out = jnp.concatenate(parts, axis=1)
    out = jnp.pad(out, ((0, 0), (0, (-out.shape[1]) % row_mult), (0, 0)))
    return out if batched else out[0]


def _unpack_rows(rows, shapes):
    out, r0 = [], 0
    for shp in shapes:
        size = 1
        for s in shp:
            size *= s
        nr = -(-size // ROW_BLOCK) * 8
        out.append(rows[r0:r0 + nr].reshape(-1)[:size].reshape(shp))
        r0 += nr
    return out


def _kernel_col_map():
    m = np.full(IN_PAD, -1, np.int64)
    m[C_MERGE:C_MERGE + 4096] = np.arange(6152, 10248)
    m[C_S5U:C_S5U + 1024] = np.arange(0, 1024)
    m[C_M2X:C_M2X + 1024] = np.arange(3072, 4096)
    m[C_SGU_U:C_SGU_U + 1536] = np.arange(1024, 2560)
    m[C_M2Z:C_M2Z + 512] = np.arange(2560, 3072)
    m[C_DT:C_DT + 8] = np.arange(4096, 4104)
    for j in range(4):
        for kind in range(4):
            k0 = C_SC + 4 * LANES * j + LANES * kind
            m[k0:k0 + LANES] = 4104 + BW * kind + LANES * j + np.arange(LANES)
    return m


def _lane_pieces(sources):
    pieces, cur = [], None
    for lane, src in enumerate(sources):
        key = None if src is None else (src[0], src[1] // LANES, (lane - src[1]) % LANES)
        if cur is not None and key == cur[0]:
            cur[2] = lane + 1
        else:
            if cur is not None and cur[0] is not None:
                pieces.append((*cur[0], cur[1], cur[2]))
            cur = [key, lane, lane + 1]
    if cur is not None and cur[0] is not None:
        pieces.append((*cur[0], cur[1], cur[2]))
    return pieces


def _assemble_block(pieces, load, rows, dtype):
    lane = lax.broadcasted_iota(jnp.int32, (rows, LANES), 1)
    out = None
    for arr, sb, shift, lo, hi in pieces:
        v = load(arr, sb)
        if shift:
            v = pltpu.roll(v, shift, 1)
        if out is None and lo == 0 and hi == LANES:
            out = v
        else:
            out = jnp.where((lane >= lo) & (lane < hi), v, jnp.zeros((rows, LANES), dtype) if out is None else out)
    return jnp.zeros((rows, LANES), dtype) if out is None else out


RELAYOUT_ROWS = 256
SHARD_BLOCKS = -(-SHARD_IN // LANES)


def _load_shard_block(ref, rows):
    def load(j, sb):
        if sb == SHARD_BLOCKS - 1:
            return jnp.broadcast_to(ref[j, :, SHARD_IN - 1:SHARD_IN], (rows, LANES))
        return ref[j, :, sb * LANES:(sb + 1) * LANES]
    return load


def _relayout_w_in(gathered, name):
    kmap = _kernel_col_map()
    dtype = gathered.dtype

    def body(src_ref, o_ref):
        load = _load_shard_block(src_ref, RELAYOUT_ROWS)
        for ob in range(IN_PAD // LANES):
            srcs = [None if kmap[ob * LANES + l] < 0 else (int(kmap[ob * LANES + l]) // SHARD_IN, int(kmap[ob * LANES + l]) % SHARD_IN)
                    for l in range(LANES)]
            o_ref[:, ob * LANES:(ob + 1) * LANES] = _assemble_block(_lane_pieces(srcs), load, RELAYOUT_ROWS, dtype)

    return pl.pallas_call(
        body, name=name, grid=(D_MODEL // RELAYOUT_ROWS,),
        in_specs=[pl.BlockSpec((N_DEV, RELAYOUT_ROWS, SHARD_IN), lambda i: (0, i, 0))],
        out_specs=pl.BlockSpec((RELAYOUT_ROWS, IN_PAD), lambda i: (i, 0)),
        out_shape=jax.ShapeDtypeStruct((D_MODEL, IN_PAD), dtype),
        compiler_params=_cparams(("parallel",)),
    )(gathered)


def _relayout_g_in(gw, name):
    kmap = _kernel_col_map()
    kinv = np.zeros(IN_DIM, np.int64)
    kinv[kmap[kmap >= 0]] = np.nonzero(kmap >= 0)[0]
    dtype = gw.dtype

    def body(src_ref, o_ref):
        load = lambda _, sb: src_ref[:, sb * LANES:(sb + 1) * LANES]
        for j in range(N_DEV):
            for ob in range(SHARD_BLOCKS):
                srcs = [(0, int(kinv[SHARD_IN * j + ob * LANES + l])) if ob * LANES + l < SHARD_IN else None for l in range(LANES)]
                blk = _assemble_block(_lane_pieces(srcs), load, RELAYOUT_ROWS, dtype)
                if ob == SHARD_BLOCKS - 1:
                    o_ref[j, :, SHARD_IN - 1:SHARD_IN] = blk[:, 0:1]
                else:
                    o_ref[j, :, ob * LANES:(ob + 1) * LANES] = blk

    return pl.pallas_call(
        body, name=name, grid=(D_MODEL // RELAYOUT_ROWS,),
        in_specs=[pl.BlockSpec((RELAYOUT_ROWS, IN_PAD), lambda i: (i, 0))],
        out_specs=pl.BlockSpec((N_DEV, RELAYOUT_ROWS, SHARD_IN), lambda i: (0, i, 0)),
        out_shape=jax.ShapeDtypeStruct((N_DEV, D_MODEL, SHARD_IN), dtype),
        compiler_params=_cparams(("parallel",)),
    )(gw)


def _rows128(flat, row_mult=8):
    n = flat.shape[0]
    per = LANES * row_mult
    total = -(-n // per) * per
    return jnp.pad(flat, (0, total - n)).reshape(total // LANES, LANES)


def _pad_lanes(v):
    return jnp.pad(v, (0, LANES - v.shape[0])).reshape(1, LANES)


def _layer_prep(i, p):
    disc, disc_vjp = jax.vjp(_s5_disc, *[p[n][i] for n in S5_NAMES])
    prep = dict(
        nw=p['norm_w'][i].reshape(1, D_MODEL), disc_vjp=disc_vjp,
        s5small=[t.astype(BF16) for t in disc[:4]] + [disc[4], disc[5]],
        sgw=[p['sgu_ln_w'][i].reshape(1, BW), p['sgu_ln_b'][i].reshape(1, BW), p['sgu_w'][i],
             jnp.repeat(p['sgu_b'][i].T, BW // SGU_HEADS, axis=1)],
        cb=p['m2_conv_b'][i].reshape(1, M2_CONV_CH),
        m2w=[_pad_lanes(p['m2_dt_bias'][i]), _pad_lanes(p['m2_a_log'][i]),
             jnp.repeat(p['m2_d'][i], M2_HEAD_DIM).reshape(1, BW), p['m2_norm_w'][i].reshape(1, BW)])
    touch = [t[0, 0].astype(F32) for t in prep['s5small']] + [prep['sgw'][3][0, 0], prep['m2w'][2][0, 0]]
    return prep, sum(touch[1:], touch[0])


def _layer_fwd(x, h, i, prep, w_in, other_weights):
    proj = _matmul(h, w_in, 1, 0, F32, 1024, 1024, 1024, f"proj{i}")
    full = dict(other_weights(proj), w_in=w_in)
    s5w = prep['s5small'] + [full['s5_w_glu']]
    ya, sre, sim = _s5_fwd(proj, *s5w, f"s5_fwd{i}")
    yb = _sgu_fwd(proj, *prep['sgw'], f"sgu_fwd{i}")
    cw = full['m2_conv_w']
    xa = _m2_conv_fwd(proj, cw, prep['cb'], f"m2conv_fwd{i}")
    yc, s_in = _ssd_fwd(proj, xa, *prep['m2w'], f"ssd_fwd{i}")
    scw = full['sc_conv_w']
    yd = _sc_fwd(proj, scw, f"sc_fwd{i}")
    ys = jnp.stack([ya, yb, yc, yd])
    mb = full['merge_b'].reshape(N_BRANCH, 1, D_MODEL)
    merged = _merge_fwd(proj, ys, mb, full['w_branch'], f"merge_fwd{i}")
    x_new = _matmul(merged, full['w_out'], 1, 0, F32, 1024, 1024, 1024, f"out{i}", residual=x)
    saved = dict(x=x, nw=prep['nw'], h=h, proj=proj, disc_vjp=prep['disc_vjp'], s5w=s5w, sre=sre, sim=sim, sgw=prep['sgw'],
                 cw=cw, cb=prep['cb'], xa=xa, m2w=prep['m2w'], s_in=s_in, scw=scw, ys=ys, mb=mb, merged=merged)
    return x_new, saved, full


def _layer_bwd(dx_out, i, sv, full, on_large_grads=None):
    g = {}
    proj = sv['proj']
    dm = _matmul(dx_out, full['w_out'], 1, 1, F32, 1024, 1024, 1024, f"dmerged{i}")
    g['w_out'] = _matmul(sv['merged'], dx_out, 0, 0, BF16, 1024, 1024, 1024, f"gw_out{i}")
    dys, dproj, g['w_branch'], dmb = _merge_bwd(proj, sv['ys'], dm, sv['mb'], full['w_branch'], f"merge_bwd{i}")
    g['merge_b'] = dmb.reshape(N_BRANCH, D_MODEL)
    dproj, dbbre, dbbim, dcre, dcim, da, dd, dwg = _s5_bwd(proj, dproj, dys, sv['sre'], sv['sim'], *sv['s5w'], f"s5_bwd{i}")
    for n, t in zip(S5_NAMES, sv['disc_vjp']((dbbre, dbbim, dcre, dcim, da, dd))):
        g[n] = t
    g['s5_w_glu'] = dwg.astype(BF16)
    dproj, dlw, dlb, g['sgu_w'], dbias = _sgu_bwd(proj, dproj, dys, *sv['sgw'], f"sgu_bwd{i}")
    g['sgu_ln_w'], g['sgu_ln_b'] = dlw[0], dlb[0]
    g['sgu_b'] = dbias.reshape(SGU_CHUNK, SGU_HEADS, BW // SGU_HEADS).sum(-1).T
    dproj, dxa, ddtb, dal, ddf, dnw = _ssd_bwd(proj, dproj, sv['xa'], dys, sv['s_in'], *sv['m2w'], f"ssd_bwd{i}")
    dproj, g['m2_conv_w'], dcb = _m2_conv_bwd(proj, dproj, dxa, sv['cw'], sv['cb'], f"m2conv_bwd{i}")
    g['m2_conv_b'], g['m2_norm_w'] = dcb[0], dnw[0]
    g['m2_dt_bias'], g['m2_a_log'] = ddtb[0, :M2_HEADS], dal[0, :M2_HEADS]
    g['m2_d'] = ddf.reshape(M2_HEADS, M2_HEAD_DIM).sum(-1)
    dproj, g['sc_conv_w'] = _sc_bwd(proj, dproj, dys, sv['scw'], f"sc_bwd{i}")
    g['w_in'] = _matmul(sv['h'], dproj, 0, 0, BF16, 1024, 1024, 1024, f"gw_in{i}")
    tok = on_large_grads(g) if on_large_grads else None
    dh = _matmul(dproj, full['w_in'], 1, 1, F32, 1024, 1024, 1024, f"dh{i}", after=tok)
    dx_in, dnw_l = _rmsnorm_bwd(sv['x'], sv['nw'], dh, dx_out, f"rms_bwd{i}")
    g['norm_w'] = dnw_l[0]
    return dx_in, g


def _split8(t, axis):
    shp = t.shape
    t = t.reshape(shp[:axis] + (N_DEV, shp[axis] // N_DEV) + shp[axis + 1:])
    return jnp.moveaxis(t, axis, 0)


def _join8(t, axis):
    t = jnp.moveaxis(t, 0, axis)
    shp = t.shape
    return t.reshape(shp[:axis] + (shp[axis] * shp[axis + 1],) + shp[axis + 2:])


SHARD_AXIS = {'w_in': 2, 'w_branch': 3, 'w_out': 1, 's5_w_glu': 1, 'm2_conv_w': 2, 'sc_conv_w': 2, 'merge_b': 2}


OTHER_BIG = [n for n in BIG_SHARDED if n != 'w_in']


def _other_weights(gathered):
    return {n: _join8(t, SHARD_AXIS[n] - 1) for n, t in zip(OTHER_BIG, gathered)}


def _layer_grad_blocks(g, i):
    blocks = [_relayout_g_in(g[n], f"relayout_g_in{i}") if n == 'w_in' else _split8(g[n], SHARD_AXIS[n] - 1) for n in BIG_SHARDED]
    return [b.reshape(N_DEV, -1, b.shape[-1]) for b in blocks]


def _start_reduce_scatter(blocks, i):
    pair_shapes = [(N_CHIP,) + b.shape[1:] for b in blocks]
    state, tok = _split_start(_plan_pair, blocks, pair_shapes, N_CHIP * len(blocks), f"pair{i}_start")
    mine, theirs = _split_wait(_plan_pair, state, tok, f"pair{i}_wait", with_sources=True)
    sums = [_pair_sum(b, t, f"pair_sum{i}_{k}") for k, (b, t) in enumerate(zip(mine, theirs))]
    return _split_start(_plan_chips, sums, pair_shapes, 3 * len(blocks), f"chips{i}_start")


def kernel(x, norm_w, w_in, s5_lambda_re, s5_lambda_im, s5_b_re, s5_b_im, s5_c_re, s5_c_im, s5_d, s5_log_step, s5_w_glu, sgu_ln_w, sgu_ln_b, sgu_w, sgu_b, m2_conv_w, m2_conv_b, m2_dt_bias, m2_a_log, m2_d, m2_norm_w, sc_conv_w, merge_b, w_branch, w_out, final_norm_w, loss_target, m_norm_w, m_w_in, m_s5_lambda_re, m_s5_lambda_im, m_s5_b_re, m_s5_b_im, m_s5_c_re, m_s5_c_im, m_s5_d, m_s5_log_step, m_s5_w_glu, m_sgu_ln_w, m_sgu_ln_b, m_sgu_w, m_sgu_b, m_m2_conv_w, m_m2_conv_b, m_m2_dt_bias, m_m2_a_log, m_m2_d, m_m2_norm_w, m_sc_conv_w, m_merge_b, m_w_branch, m_w_out, m_final_norm_w, v_norm_w, v_w_in, v_s5_lambda_re, v_s5_lambda_im, v_s5_b_re, v_s5_b_im, v_s5_c_re, v_s5_c_im, v_s5_d, v_s5_log_step, v_s5_w_glu, v_sgu_ln_w, v_sgu_ln_b, v_sgu_w, v_sgu_b, v_m2_conv_w, v_m2_conv_b, v_m2_dt_bias, v_m2_a_log, v_m2_d, v_m2_norm_w, v_sc_conv_w, v_merge_b, v_w_branch, v_w_out, v_final_norm_w):
    loc = locals()
    p = {n: loc[n] for n in WEIGHTS}
    mom = {n: loc['m_' + n] for n in WEIGHTS}
    vel = {n: loc['v_' + n] for n in WEIGHTS}

    small_sizes = [p[n].size for n in SMALL_SHARDED]
    small_pack = _rows128(jnp.concatenate([p[n].reshape(-1) for n in SMALL_SHARDED]))
    shards = ([p['w_in'][0].astype(BF16)] + [p[n][0].astype(BF16) for n in OTHER_BIG] + [small_pack]
              + [p[n][1].astype(BF16) for n in BIG_SHARDED])
    gath, tok = _split_start(_plan_gather, shards, [(N_DEV,) + t.shape for t in shards], 7 * len(shards), "gather_start")
    sems, srcs, lands = gath[:3], gath[3:3 + len(shards)], gath[3 + len(shards):]

    def gathered(lo, hi, after, name):
        plan = functools.partial(_plan_gather, first=lo)
        state, tok = _split_relay(plan, (*sems, *srcs[lo:hi], *lands[lo:hi]), after, name + "_relay")
        return _split_wait(plan, state, tok, name + "_wait")

    later = dict(p, **{n: p[n] + tok[0, 0] for n in ('norm_w', 's5_log_step', 'sgu_b', 'm2_d')})
    preps = [_layer_prep(i, later) for i in range(DEPTH)]
    h0 = _rmsnorm_fwd(x[0], preps[0][0]['nw'], "rms_fwd0")
    got = gathered(0, 1, tok + (preps[0][1] + preps[1][1] + h0[0, 0].astype(F32)), "gather_w_in0")
    small_full = {}

    def other_weights0(proj):
        got = gathered(1, 5, proj, "gather_rest0")
        small_all, off = got[-1].reshape(N_DEV, -1), 0
        for n, sz in zip(SMALL_SHARDED, small_sizes):
            small_full[n] = _join8(small_all[:, off:off + sz].reshape((N_DEV,) + p[n].shape), SHARD_AXIS[n])
            off += sz
        return dict(_other_weights(got[:-1]), **{n: small_full[n][0] for n in SMALL_SHARDED})

    saved, layer_g, full = [None] * DEPTH, [None] * DEPTH, [None] * DEPTH
    xs, saved[0], full[0] = _layer_fwd(x[0], h0, 0, preps[0][0], _relayout_w_in(got[0], "relayout_w_in0"), other_weights0)
    h1 = _rmsnorm_fwd(xs, preps[1][0]['nw'], "rms_fwd1")
    got = gathered(5, 9, h1, "gather1")
    xs, saved[1], full[1] = _layer_fwd(
        xs, h1, 1, preps[1][0], _relayout_w_in(got[0], "relayout_w_in1"),
        lambda proj: dict(_other_weights(got[1:]), **{n: small_full[n][1] for n in SMALL_SHARDED}))
    loss_row, dx, dfw = _loss_head(xs, final_norm_w.reshape(1, D_MODEL), loss_target[0])
    loss = lax.psum(loss_row[0, 0], ("x", "y", "c"))
    scat = [None] * DEPTH

    def start_scatter(i):
        def start(g):
            scat[i], tok = _start_reduce_scatter(_layer_grad_blocks(g, i), i)
            return tok
        return start

    dx, layer_g[1] = _layer_bwd(dx, 1, saved[1], full[1], start_scatter(1))
    dx, layer_g[0] = _layer_bwd(dx, 0, saved[0], full[0], start_scatter(0))
    grads = {n: jnp.stack([layer_g[i][n] for i in range(DEPTH)]) for n in SMALL_SHARDED + REPLICATED if n != 'final_norm_w'}
    grads['final_norm_w'] = dfw[0]

    out_g, out_d, out_m, out_v = {}, {}, {}, {}
    repl_rows = _pack_rows([grads[n] for n in REPLICATED], 8 * N_DEV)
    rr = repl_rows.shape[0] // N_DEV
    shard_rows = _pack_rows([_split8(grads[n], SHARD_AXIS[n]) for n in SMALL_SHARDED], 8, batched=True)
    rs = shard_rows.shape[1]
    small_g = jnp.concatenate([shard_rows, repl_rows.reshape(N_DEV, rr, LANES)], axis=1)
    small_sum = _slot_sum(_exchange([small_g], False, "scatter_small")[0], "sum_small")
    repl_all = _exchange([small_sum[rs:]], True, "gather_small")[0].reshape(N_DEV * rr, LANES)
    g_all = jnp.concatenate([small_sum[:rs], repl_all], axis=0)
    names = SMALL_SHARDED + REPLICATED
    packed = [jnp.concatenate([_pack_rows([d[n] for n in SMALL_SHARDED], 8), _pack_rows([d[n] for n in REPLICATED], 8 * N_DEV)],
                              axis=0) for d in (p, mom, vel)]
    res = _adamw([g_all[None]], *[t[None] for t in packed], g_all.shape[0], "adamw_small")
    for o, dst in zip(res, (out_g, out_d, out_m, out_v)):
        pieces = (_unpack_rows(o[0, :rs], [p[n].shape for n in SMALL_SHARDED])
                  + _unpack_rows(o[0, rs:], [p[n].shape for n in REPLICATED]))
        dst.update(zip(names, pieces))

    landed1 = _split_wait(_plan_chips, scat[1], res[0], "chips1_wait")
    landed0 = _split_wait(_plan_chips, scat[0], landed1[0], "chips0_wait")
    for k, n in enumerate(BIG_SHARDED):
        shp = p[n].shape
        c = shp[-1]
        r = p[n].size // (DEPTH * c)
        big = _adamw([landed0[k], landed1[k]],
                     *[d[n].reshape(DEPTH, r, c) for d in (p, mom, vel)],
                     {'w_in': 256, 'w_branch': 512, 'w_out': 128, 's5_w_glu': 64}[n], "adamw_" + n)
        out_g[n], out_d[n], out_m[n], out_v[n] = [o.reshape(shp) for o in big]
    return (loss, dx[None], *[out_g[n] for n in WEIGHTS], *[out_d[n] for n in WEIGHTS],
            *[out_m[n] for n in WEIGHTS], *[out_v[n] for n in WEIGHTS])
```

```python
import functools

import jax
import jax.numpy as jnp
import numpy as np
from jax import lax
from jax.experimental import pallas as pl
from jax.experimental.pallas import tpu as pltpu

F32 = jnp.float32
BF16 = jnp.bfloat16

N_DEV = 8
SEQ = 2048
D_MODEL = 1024
DEPTH = 2
BW = 512
N_BRANCH = 4
EPS = 1e-6
S5_GROUPS, S5_STATE, S5_P = 32, 64, 16
S5_CH = S5_GROUPS * S5_STATE
SGU_CHUNK, SGU_HEADS = 128, 8
M2_HEADS, M2_HEAD_DIM, M2_STATE, M2_CHUNK, M2_CONV = 8, 64, 128, 128, 4
M2_CONV_CH = 1024
SC_CONV = 3
IN_DIM = 10248
IN_PAD = 11264
C_MERGE = 0
C_S5U, C_S5G = 4096, 4608
C_M2X = 5120
C_SGU_U, C_SGU_V, C_SGU_G = 6144, 6656, 7168
C_M2Z, C_DT = 8192, 8704
C_SC = 9216
SHARD_IN = IN_DIM // N_DEV

ADAM_LR, ADAM_B1, ADAM_B2, ADAM_EPS, ADAM_WD, ADAM_STEP = 0.001, 0.9, 0.999, 1e-08, 0.01, 10

VMEM_LIMIT = 56 * 1024 * 1024
LANES = 128

MESH = pl.DeviceIdType.MESH


def _cparams(sem=None, **kw):
    return pltpu.CompilerParams(dimension_semantics=sem, vmem_limit_bytes=VMEM_LIMIT, **kw)


def _dg(a, b, ca, cb, precision=None):
    return lax.dot_general(a, b, (((ca,), (cb,)), ((), ())), precision=precision,
                           preferred_element_type=F32)


@functools.partial(jax.custom_vjp, nondiff_argnums=(2, 3))
def _bdot(a, b, ca, cb):
    return _dg(a.astype(BF16), b.astype(BF16), ca, cb)


def _bdot_fwd(a, b, ca, cb):
    return _bdot(a, b, ca, cb), (a, b)


def _bdot_bwd(ca, cb, res, g):
    a, b = res
    gb, ab, bb = g.astype(BF16), a.astype(BF16), b.astype(BF16)
    da = _dg(gb, bb, 1, 1 - cb) if ca == 1 else _dg(bb, gb, 1 - cb, 1)
    db = _dg(ab, gb, 1 - ca, 0) if cb == 0 else _dg(gb, ab, 0, 1 - ca)
    return da.astype(a.dtype), db.astype(b.dtype)


_bdot.defvjp(_bdot_fwd, _bdot_bwd)


def _rms(x, w):
    return x * lax.rsqrt(jnp.mean(x * x, axis=-1, keepdims=True) + EPS) * w


def _silu(x):
    return x * jax.nn.sigmoid(x)


def _gelu(x):
    return 0.5 * x * (1.0 + jnp.tanh(0.7978845608028654 * (x + 0.044715 * (x * x * x))))


def _softplus(x):
    return jnp.maximum(x, 0.0) + jnp.log1p(jnp.exp(-jnp.abs(x)))


def _shift_down(x, s):
    if s == 0:
        return x
    row = lax.broadcasted_iota(jnp.int32, x.shape, 0)
    return jnp.where(row >= s, pltpu.roll(x, s, 0), 0.0)


def _shift_up(x, s):
    if s == 0:
        return x
    n = x.shape[0]
    row = lax.broadcasted_iota(jnp.int32, x.shape, 0)
    return jnp.where(row < n - s, pltpu.roll(x, n - s, 0), 0.0)


def _matmul(a, b, ca, cb, out_dtype, tm, tn, tk, name, residual=None, after=None):
    m = a.shape[1 - ca]
    k = a.shape[ca]
    n = b.shape[1 - cb]
    assert b.shape[cb] == k and m % tm == 0 and n % tn == 0 and k % tk == 0
    nk = k // tk
    a_spec = pl.BlockSpec((tm, tk), lambda i, j, kk: (i, kk)) if ca == 1 else pl.BlockSpec((tk, tm), lambda i, j, kk: (kk, i))
    b_spec = pl.BlockSpec((tk, tn), lambda i, j, kk: (kk, j)) if cb == 0 else pl.BlockSpec((tn, tk), lambda i, j, kk: (j, kk))
    o_spec = pl.BlockSpec((tm, tn), lambda i, j, kk: (i, j))
    has_res = residual is not None

    def body(*refs):
        refs = refs[:2 + has_res] + refs[2 + has_res + (after is not None):]
        if has_res:
            a_ref, b_ref, r_ref, o_ref, acc = refs
        else:
            a_ref, b_ref, o_ref, acc = refs
        kk = pl.program_id(2)
        part = _dg(a_ref[...].astype(BF16), b_ref[...].astype(BF16), ca, cb)

        @pl.when(kk == 0)
        def _():
            acc[...] = part

        @pl.when(kk > 0)
        def _():
            acc[...] += part

        @pl.when(kk == nk - 1)
        def _():
            r = acc[...]
            if has_res:
                r = r + r_ref[...]
            o_ref[...] = r.astype(out_dtype)

    ins = [a, b] + ([residual] if has_res else []) + ([after] if after is not None else [])
    specs = [a_spec, b_spec] + ([o_spec] if has_res else []) + ([pl.BlockSpec(memory_space=pl.ANY)] if after is not None else [])
    return pl.pallas_call(
        body, name=name, grid=(m // tm, n // tn, nk), in_specs=specs, out_specs=o_spec,
        out_shape=jax.ShapeDtypeStruct((m, n), out_dtype),
        scratch_shapes=[pltpu.VMEM((tm, tn), F32)],
        compiler_params=_cparams(("parallel", "parallel", "arbitrary")),
    )(*ins)


ROW_TILE = 512


def _rmsnorm_fwd(x, w, name):
    def body(x_ref, w_ref, o_ref):
        o_ref[...] = _rms(x_ref[...], w_ref[...]).astype(BF16)

    return pl.pallas_call(
        body, name=name, grid=(SEQ // ROW_TILE,),
        in_specs=[pl.BlockSpec((ROW_TILE, D_MODEL), lambda i: (i, 0)), pl.BlockSpec((1, D_MODEL), lambda i: (0, 0))],
        out_specs=pl.BlockSpec((ROW_TILE, D_MODEL), lambda i: (i, 0)),
        out_shape=jax.ShapeDtypeStruct((SEQ, D_MODEL), BF16),
        compiler_params=_cparams(("parallel",)),
    )(x, w)


def _rmsnorm_bwd(x, w, dh, dres, name):
    def body(x_ref, w_ref, dh_ref, dres_ref, dx_ref, dw_ref):
        _, vjp = jax.vjp(_rms, x_ref[...], w_ref[...])
        dx, dw = vjp(dh_ref[...])
        dx_ref[...] = dx + dres_ref[...]

        @pl.when(pl.program_id(0) == 0)
        def _():
            dw_ref[...] = dw

        @pl.when(pl.program_id(0) > 0)
        def _():
            dw_ref[...] += dw

    tile = pl.BlockSpec((ROW_TILE, D_MODEL), lambda i: (i, 0))
    vec = pl.BlockSpec((1, D_MODEL), lambda i: (0, 0))
    return pl.pallas_call(
        body, name=name, grid=(SEQ // ROW_TILE,),
        in_specs=[tile, vec, tile, tile], out_specs=[tile, vec],
        out_shape=[jax.ShapeDtypeStruct((SEQ, D_MODEL), F32), jax.ShapeDtypeStruct((1, D_MODEL), F32)],
        compiler_params=_cparams(("arbitrary",)),
    )(x, w, dh, dres)


def _loss_head(x, w, target):
    def body(x_ref, w_ref, t_ref, loss_ref, dx_ref, dw_ref):
        tgt = t_ref[...]

        def f(xv, wv):
            err = _rms(xv, wv) - tgt
            return 0.5 * jnp.sum(jnp.mean(err * err, axis=-1))

        loss, vjp = jax.vjp(f, x_ref[...], w_ref[...])
        dx, dw = vjp(jnp.ones((), F32))
        dx_ref[...] = dx
        lrow = jnp.full((1, LANES), loss, F32)

        @pl.when(pl.program_id(0) == 0)
        def _():
            dw_ref[...] = dw
            loss_ref[...] = lrow

        @pl.when(pl.program_id(0) > 0)
        def _():
            dw_ref[...] += dw
            loss_ref[...] += lrow

    tile = pl.BlockSpec((ROW_TILE, D_MODEL), lambda i: (i, 0))
    vec = pl.BlockSpec((1, D_MODEL), lambda i: (0, 0))
    return pl.pallas_call(
        body, name="loss_head", grid=(SEQ // ROW_TILE,),
        in_specs=[tile, vec, tile], out_specs=[pl.BlockSpec((1, LANES), lambda i: (0, 0)), tile, vec],
        out_shape=[jax.ShapeDtypeStruct((1, LANES), F32), jax.ShapeDtypeStruct((SEQ, D_MODEL), F32),
                   jax.ShapeDtypeStruct((1, D_MODEL), F32)],
        compiler_params=_cparams(("arbitrary",)),
    )(x, w, target)


S5_T = 256
S5_BLOCKS = [(slice(j * 256, (j + 1) * 256), slice(j * 1024, (j + 1) * 1024)) for j in range(2)]


def _s5_post(ypre, gate, wglu):
    y = _gelu(ypre)
    y = y * jax.nn.sigmoid(_bdot(y, wglu, 1, 0))
    return y * _silu(gate)


def _s5_fwd(proj, bbre, bbim, cre, cim, a2, dvec, wglu, name):
    def body(u_ref, g_ref, bbre_ref, bbim_ref, cre_ref, cim_ref, a_ref, d_ref, wg_ref, o_ref, sre_ref, sim_ref, st):
        @pl.when(pl.program_id(0) == 0)
        def _():
            st[...] = jnp.zeros_like(st)

        u = u_ref[...]
        ub = u.astype(BF16)
        for us, ss in S5_BLOCKS:
            sre_ref[:, ss] = _dg(ub[:, us], bbre_ref[us, ss], 1, 0)
            sim_ref[:, ss] = _dg(ub[:, us], bbim_ref[us, ss], 1, 0)
        ar, ai = a_ref[0:1, :], a_ref[1:2, :]

        def step(t, carry):
            sr, si = carry
            nr = ar * sr - ai * si + sre_ref[pl.ds(t, 1), :]
            ni = ar * si + ai * sr + sim_ref[pl.ds(t, 1), :]
            sre_ref[pl.ds(t, 1), :] = nr
            sim_ref[pl.ds(t, 1), :] = ni
            return nr, ni

        sr, si = lax.fori_loop(0, S5_T, step, (st[0:1, :], st[1:2, :]), unroll=8)
        st[0:1, :] = sr
        st[1:2, :] = si
        ypre = jnp.concatenate(
            [_dg(sre_ref[:, ss].astype(BF16), cre_ref[ss, us], 1, 0) - _dg(sim_ref[:, ss].astype(BF16), cim_ref[ss, us], 1, 0)
             for us, ss in S5_BLOCKS], axis=1) + d_ref[...] * u
        o_ref[...] = _s5_post(ypre, g_ref[...], wg_ref[...]).astype(BF16)

    full = lambda shape: pl.BlockSpec(shape, lambda c: (0, 0))
    return pl.pallas_call(
        body, name=name, grid=(SEQ // S5_T,),
        in_specs=[pl.BlockSpec((S5_T, BW), lambda c: (c, C_S5U // BW)), pl.BlockSpec((S5_T, BW), lambda c: (c, C_S5G // BW)),
                  full((BW, S5_CH)), full((BW, S5_CH)), full((S5_CH, BW)), full((S5_CH, BW)),
                  full((2, S5_CH)), full((1, BW)), full((BW, BW))],
        out_specs=[pl.BlockSpec((S5_T, BW), lambda c: (c, 0)), pl.BlockSpec((S5_T, S5_CH), lambda c: (c, 0)),
                   pl.BlockSpec((S5_T, S5_CH), lambda c: (c, 0))],
        out_shape=[jax.ShapeDtypeStruct((SEQ, BW), BF16), jax.ShapeDtypeStruct((SEQ, S5_CH), F32),
                   jax.ShapeDtypeStruct((SEQ, S5_CH), F32)],
        scratch_shapes=[pltpu.VMEM((2, S5_CH), F32)],
        compiler_params=_cparams(("arbitrary",)),
    )(proj, proj, bbre, bbim, cre, cim, a2, dvec, wglu)


def _s5_bwd(proj, dproj, dout, sre, sim, bbre, bbim, cre, cim, a2, dvec, wglu, name):
    nc = SEQ // S5_T

    def body(u_ref, g_ref, do_ref, sre_ref, sim_ref, pre_ref, pim_ref, bbre_ref, bbim_ref, cre_ref, cim_ref, a_ref,
             d_ref, wg_ref, dproj_in, dp_ref, dbbre_ref, dbbim_ref, dcre_ref, dcim_ref, da_ref, dd_ref, dwg_ref,
             gre, gim, st):
        c = nc - 1 - pl.program_id(0)

        @pl.when(pl.program_id(0) == 0)
        def _():
            st[...] = jnp.zeros_like(st)
            for r in (dbbre_ref, dbbim_ref, dcre_ref, dcim_ref, da_ref, dd_ref, dwg_ref):
                r[...] = jnp.zeros_like(r)

        u = u_ref[...]
        s_re, s_im = sre_ref[...], sim_ref[...]

        def head(s_res, s_ims, cres, cims, dv, uv, gv, wg):
            ypre = jnp.concatenate([_bdot(sr, cr, 1, 0) - _bdot(si, ci, 1, 0)
                                    for sr, si, cr, ci in zip(s_res, s_ims, cres, cims)], axis=1) + dv * uv
            return _s5_post(ypre, gv, wg)

        _, vjp = jax.vjp(head, [sre_ref[:, ss] for _, ss in S5_BLOCKS], [sim_ref[:, ss] for _, ss in S5_BLOCKS],
                         [cre_ref[ss, us].astype(F32) for us, ss in S5_BLOCKS],
                         [cim_ref[ss, us].astype(F32) for us, ss in S5_BLOCKS],
                         d_ref[...], u, g_ref[...], wg_ref[...].astype(F32))
        ds_res, ds_ims, dcres, dcims, dd, du_d, dgate, dwg = vjp(do_ref[0])
        for k, (us, ss) in enumerate(S5_BLOCKS):
            dcre_ref[ss, us] += dcres[k]
            dcim_ref[ss, us] += dcims[k]
            gre[:, ss] = ds_res[k]
            gim[:, ss] = ds_ims[k]
        dd_ref[...] += dd
        dwg_ref[...] += dwg
        dp_ref[:, BW:] = dgate.astype(BF16)
        ar, ai = a_ref[0:1, :], a_ref[1:2, :]

        def step(i, carry):
            t = S5_T - 1 - i
            gr, gi = carry
            nr = gre[pl.ds(t, 1), :] + gr
            ni = gim[pl.ds(t, 1), :] + gi
            gre[pl.ds(t, 1), :] = nr
            gim[pl.ds(t, 1), :] = ni
            return ar * nr + ai * ni, ar * ni - ai * nr

        gr, gi = lax.fori_loop(0, S5_T, step, (st[0:1, :], st[1:2, :]), unroll=8)
        st[0:1, :] = gr
        st[1:2, :] = gi
        g_re, g_im = gre[...], gim[...]
        first = jnp.where(c > 0, 1.0, 0.0)
        row = lax.broadcasted_iota(jnp.int32, (S5_T, S5_CH), 0)
        p_re = jnp.where(row == 0, pre_ref[7:8, :] * first, pltpu.roll(s_re, 1, 0))
        p_im = jnp.where(row == 0, pim_ref[7:8, :] * first, pltpu.roll(s_im, 1, 0))
        da_ref[0:1, :] += jnp.sum(g_re * p_re + g_im * p_im, axis=0, keepdims=True)
        da_ref[1:2, :] += jnp.sum(g_im * p_re - g_re * p_im, axis=0, keepdims=True)
        ub, grb, gib = u.astype(BF16), g_re.astype(BF16), g_im.astype(BF16)
        du_s = []
        for us, ss in S5_BLOCKS:
            dbbre_ref[us, ss] += _dg(ub[:, us], grb[:, ss], 0, 0)
            dbbim_ref[us, ss] += _dg(ub[:, us], gib[:, ss], 0, 0)
            du_s.append(_dg(grb[:, ss], bbre_ref[us, ss], 1, 1) + _dg(gib[:, ss], bbim_ref[us, ss], 1, 1))
        dp_ref[:, :BW] = (du_d + jnp.concatenate(du_s, axis=1)).astype(BF16)

    full = lambda shape: pl.BlockSpec(shape, lambda i: (0, 0))
    rev = lambda w, col=0: pl.BlockSpec((S5_T, w), lambda i: (nc - 1 - i, col))
    prev = pl.BlockSpec((8, S5_CH), lambda i: (jnp.maximum((nc - 1 - i) * (S5_T // 8) - 1, 0), 0))
    return pl.pallas_call(
        body, name=name, grid=(nc,),
        in_specs=[rev(BW, C_S5U // BW), rev(BW, C_S5G // BW), pl.BlockSpec((1, S5_T, BW), lambda i: (0, nc - 1 - i, 0)),
                  rev(S5_CH), rev(S5_CH), prev, prev,
                  full((BW, S5_CH)), full((BW, S5_CH)), full((S5_CH, BW)), full((S5_CH, BW)),
                  full((2, S5_CH)), full((1, BW)), full((BW, BW)), pl.BlockSpec(memory_space=pl.ANY)],
        out_specs=[rev(2 * BW, C_S5U // (2 * BW)), full((BW, S5_CH)), full((BW, S5_CH)), full((S5_CH, BW)), full((S5_CH, BW)),
                   full((2, S5_CH)), full((1, BW)), full((BW, BW))],
        input_output_aliases={14: 0},
        out_shape=[jax.ShapeDtypeStruct((SEQ, IN_PAD), BF16),
                   jax.ShapeDtypeStruct((BW, S5_CH), F32), jax.ShapeDtypeStruct((BW, S5_CH), F32),
                   jax.ShapeDtypeStruct((S5_CH, BW), F32), jax.ShapeDtypeStruct((S5_CH, BW), F32),
                   jax.ShapeDtypeStruct((2, S5_CH), F32), jax.ShapeDtypeStruct((1, BW), F32),
                   jax.ShapeDtypeStruct((BW, BW), F32)],
        scratch_shapes=[pltpu.VMEM((S5_T, S5_CH), F32), pltpu.VMEM((S5_T, S5_CH), F32), pltpu.VMEM((2, S5_CH), F32)],
        compiler_params=_cparams(("arbitrary",)),
    )(proj, proj, dout, sre, sim, sre, sim, bbre, bbim, cre, cim, a2, dvec, wglu, dproj)


def _s5_disc(lam_re, lam_im, b_re, b_im, c_re, c_im, d, log_step):
    step = jnp.exp(log_step)[:, None]
    mag = jnp.exp(lam_re * step)
    ab_re, ab_im = mag * jnp.cos(lam_im * step), mag * jnp.sin(lam_im * step)
    den = lam_re * lam_re + lam_im * lam_im
    nr = ab_re - 1.0
    coef_re = (nr * lam_re + ab_im * lam_im) / den
    coef_im = (ab_im * lam_re - nr * lam_im) / den
    bb_re = coef_re[..., None] * b_re - coef_im[..., None] * b_im
    bb_im = coef_re[..., None] * b_im + coef_im[..., None] * b_re
    def block_diag(t, rows_per, cols_per):
        wide = jnp.tile(t.reshape(S5_GROUPS * rows_per, cols_per), (1, S5_GROUPS))
        r = lax.broadcasted_iota(jnp.int32, wide.shape, 0) // rows_per
        c = lax.broadcasted_iota(jnp.int32, wide.shape, 1) // cols_per
        return jnp.where(r == c, wide, 0.0)

    bbre = block_diag(jnp.swapaxes(bb_re, 1, 2), S5_P, S5_STATE)
    bbim = block_diag(jnp.swapaxes(bb_im, 1, 2), S5_P, S5_STATE)
    cre = block_diag(jnp.swapaxes(c_re, 1, 2), S5_STATE, S5_P)
    cim = block_diag(jnp.swapaxes(c_im, 1, 2), S5_STATE, S5_P)
    a2 = jnp.stack([ab_re.reshape(-1), ab_im.reshape(-1)])
    return bbre, bbim, cre, cim, a2, d.reshape(1, BW)


def _left_lanes(shape):
    return lax.broadcasted_iota(jnp.int32, shape, 1) < 64


def _sgu_chunk(u, v, gate, ln_w, ln_b, w, bias):
    u32, v32 = _gelu(u), _gelu(v)
    mu = jnp.mean(v32, axis=-1, keepdims=True)
    var = jnp.mean(jnp.square(v32 - mu), axis=-1, keepdims=True)
    vn = (v32 - mu) * lax.rsqrt(var + EPS) * ln_w + ln_b
    t_i = lax.broadcasted_iota(jnp.int32, (SGU_CHUNK, SGU_CHUNK), 0)
    s_i = lax.broadcasted_iota(jnp.int32, (SGU_CHUNK, SGU_CHUNK), 1)
    causal = t_i >= s_i
    left = _left_lanes((SGU_CHUNK, LANES))
    sgate = _silu(gate)
    outs = []
    for j in range(BW // LANES):
        vb = vn[:, j * LANES:(j + 1) * LANES]
        s_blk = (_bdot(jnp.where(causal, w[2 * j], 0.0), jnp.where(left, vb, 0.0), 1, 0)
                 + _bdot(jnp.where(causal, w[2 * j + 1], 0.0), jnp.where(left, 0.0, vb), 1, 0))
        sl = slice(j * LANES, (j + 1) * LANES)
        outs.append(u32[:, sl] * (s_blk + bias[:, sl]) * sgate[:, sl])
    return outs


def _sgu_fwd(proj, ln_w, ln_b, w, bias, name):
    def body(u_ref, v_ref, g_ref, lw_ref, lb_ref, w_ref, b_ref, o_ref):
        outs = _sgu_chunk(u_ref[...], v_ref[...], g_ref[...], lw_ref[...], lb_ref[...], w_ref[...], b_ref[...])
        for j, o in enumerate(outs):
            o_ref[:, j * LANES:(j + 1) * LANES] = o.astype(BF16)

    blk = lambda col: pl.BlockSpec((SGU_CHUNK, BW), lambda c: (c, col // BW))
    vec = pl.BlockSpec((1, BW), lambda c: (0, 0))
    return pl.pallas_call(
        body, name=name, grid=(SEQ // SGU_CHUNK,),
        in_specs=[blk(C_SGU_U), blk(C_SGU_V), blk(C_SGU_G), vec, vec,
                  pl.BlockSpec((SGU_HEADS, SGU_CHUNK, SGU_CHUNK), lambda c: (0, 0, 0)),
                  pl.BlockSpec((SGU_CHUNK, BW), lambda c: (0, 0))],
        out_specs=pl.BlockSpec((SGU_CHUNK, BW), lambda c: (c, 0)),
        out_shape=jax.ShapeDtypeStruct((SEQ, BW), BF16),
        compiler_params=_cparams(("parallel",)),
    )(proj, proj, proj, ln_w, ln_b, w, bias)


def _sgu_bwd(proj, dproj, dout, ln_w, ln_b, w, bias, name):
    def body(u_ref, v_ref, g_ref, do_ref, lw_ref, lb_ref, w_ref, b_ref, dproj_in, dp_ref, dlw_ref, dlb_ref, dw_ref, db_ref):
        _, vjp = jax.vjp(_sgu_chunk, u_ref[...], v_ref[...], g_ref[...], lw_ref[...], lb_ref[...], w_ref[...], b_ref[...])
        do = do_ref[0]
        du, dv, dgate, dlw, dlb, dw, db = vjp([do[:, j * LANES:(j + 1) * LANES] for j in range(BW // LANES)])
        dp_ref[:, 0:BW] = du.astype(BF16)
        dp_ref[:, BW:2 * BW] = dv.astype(BF16)
        dp_ref[:, 2 * BW:3 * BW] = dgate.astype(BF16)
        dp_ref[:, 3 * BW:] = jnp.zeros((SGU_CHUNK, BW), BF16)

        @pl.when(pl.program_id(0) == 0)
        def _():
            dlw_ref[...] = dlw
            dlb_ref[...] = dlb
            dw_ref[...] = dw
            db_ref[...] = db

        @pl.when(pl.program_id(0) > 0)
        def _():
            dlw_ref[...] += dlw
            dlb_ref[...] += dlb
            dw_ref[...] += dw
            db_ref[...] += db

    blk = lambda col: pl.BlockSpec((SGU_CHUNK, BW), lambda c: (c, col // BW))
    vec = pl.BlockSpec((1, BW), lambda c: (0, 0))
    wsp = pl.BlockSpec((SGU_HEADS, SGU_CHUNK, SGU_CHUNK), lambda c: (0, 0, 0))
    bsp = pl.BlockSpec((SGU_CHUNK, BW), lambda c: (0, 0))
    return pl.pallas_call(
        body, name=name, grid=(SEQ // SGU_CHUNK,),
        in_specs=[blk(C_SGU_U), blk(C_SGU_V), blk(C_SGU_G), pl.BlockSpec((1, SGU_CHUNK, BW), lambda c: (1, c, 0)),
                  vec, vec, wsp, bsp, pl.BlockSpec(memory_space=pl.ANY)],
        out_specs=[pl.BlockSpec((SGU_CHUNK, 4 * BW), lambda c: (c, C_SGU_U // (4 * BW))), vec, vec, wsp, bsp],
        input_output_aliases={8: 0},
        out_shape=[jax.ShapeDtypeStruct((SEQ, IN_PAD), BF16), jax.ShapeDtypeStruct((1, BW), F32),
                   jax.ShapeDtypeStruct((1, BW), F32), jax.ShapeDtypeStruct((SGU_HEADS, SGU_CHUNK, SGU_CHUNK), F32),
                   jax.ShapeDtypeStruct((SGU_CHUNK, BW), F32)],
        compiler_params=_cparams(("arbitrary",)),
    )(proj, proj, proj, dout, ln_w, ln_b, w, bias, dproj)


CONV_BLK = 256


def _m2_conv_fwd(proj, w, b, name):
    def body(x_ref, w_ref, b_ref, o_ref):
        x = x_ref[...]
        acc = jnp.zeros_like(x) + b_ref[...]
        for k in range(M2_CONV):
            acc = acc + w_ref[k:k + 1, :] * _shift_down(x, M2_CONV - 1 - k)
        o_ref[...] = _silu(acc)

    return pl.pallas_call(
        body, name=name, grid=(M2_CONV_CH // CONV_BLK,),
        in_specs=[pl.BlockSpec((SEQ, CONV_BLK), lambda j: (0, C_M2X // CONV_BLK + j)),
                  pl.BlockSpec((M2_CONV, CONV_BLK), lambda j: (0, j)), pl.BlockSpec((1, CONV_BLK), lambda j: (0, j))],
        out_specs=pl.BlockSpec((SEQ, CONV_BLK), lambda j: (0, j)),
        out_shape=jax.ShapeDtypeStruct((SEQ, M2_CONV_CH), F32),
        compiler_params=_cparams(("parallel",)),
    )(proj, w, b)


def _m2_conv_bwd(proj, dproj, dxa, w, b, name):
    def body(x_ref, d_ref, w_ref, b_ref, dproj_in, dx_ref, dw_ref, db_ref):
        x = x_ref[...]
        xs = [_shift_down(x, M2_CONV - 1 - k) for k in range(M2_CONV)]
        acc = jnp.zeros_like(x) + b_ref[...]
        for k in range(M2_CONV):
            acc = acc + w_ref[k:k + 1, :] * xs[k]
        sg = jax.nn.sigmoid(acc)
        dacc = d_ref[...] * (sg * (1.0 + acc * (1.0 - sg)))
        dx = jnp.zeros_like(x)
        for k in range(M2_CONV):
            dx = dx + w_ref[k:k + 1, :] * _shift_up(dacc, M2_CONV - 1 - k)
            dw_ref[k:k + 1, :] = jnp.sum(dacc * xs[k], axis=0, keepdims=True)
        dx_ref[...] = dx.astype(BF16)
        db_ref[...] = jnp.sum(dacc, axis=0, keepdims=True)

    return pl.pallas_call(
        body, name=name, grid=(M2_CONV_CH // CONV_BLK,),
        in_specs=[pl.BlockSpec((SEQ, CONV_BLK), lambda j: (0, C_M2X // CONV_BLK + j)),
                  pl.BlockSpec((SEQ, CONV_BLK), lambda j: (0, j)),
                  pl.BlockSpec((M2_CONV, CONV_BLK), lambda j: (0, j)), pl.BlockSpec((1, CONV_BLK), lambda j: (0, j)),
                  pl.BlockSpec(memory_space=pl.ANY)],
        out_specs=[pl.BlockSpec((SEQ, CONV_BLK), lambda j: (0, C_M2X // CONV_BLK + j)),
                   pl.BlockSpec((M2_CONV, CONV_BLK), lambda j: (0, j)), pl.BlockSpec((1, CONV_BLK), lambda j: (0, j))],
        input_output_aliases={4: 0},
        out_shape=[jax.ShapeDtypeStruct((SEQ, IN_PAD), BF16), jax.ShapeDtypeStruct((M2_CONV, M2_CONV_CH), F32),
                   jax.ShapeDtypeStruct((1, M2_CONV_CH), F32)],
        compiler_params=_cparams(("parallel",)),
    )(proj, dxa, w, b, dproj)


N_PAIR = M2_HEADS // 2
HI = lax.Precision.HIGHEST


def _col(a, h):
    lane = lax.broadcasted_iota(jnp.int32, a.shape, 1)
    return jnp.sum(jnp.where(lane == h, a, 0.0), axis=1, keepdims=True)


def _row(a, h):
    sub = lax.broadcasted_iota(jnp.int32, a.shape, 0)
    return jnp.sum(jnp.where(sub == h, a, 0.0), axis=0, keepdims=True)


def _ssd_chunk(xs, bms, cms, dtr, zs, states, dt_bias, a_log, dfs, nws):
    q = M2_CHUNK
    dt = _softplus(dtr + dt_bias)
    da = dt * (-jnp.exp(a_log))
    l_i = lax.broadcasted_iota(jnp.int32, (q, q), 0)
    s_i = lax.broadcasted_iota(jnp.int32, (q, q), 1)
    causal = l_i >= s_i
    tril = jnp.where(causal, 1.0, 0.0)
    a_cs = _dg(tril, da, 1, 0, HI)
    a_cs_t = _dg(da, tril, 0, 1, HI)
    a_end = _row(a_cs, q - 1)
    left = _left_lanes((q, LANES))
    left1 = _left_lanes((1, LANES))
    ys, nexts = [], []
    for j in range(N_PAIR):
        grp = j // 2
        bm, cm = bms[grp], cms[grp]
        h0, h1 = 2 * j, 2 * j + 1
        cb = _bdot(cm, bm, 1, 1)
        xdt = xs[j] * jnp.where(left, _col(dt, h0), _col(dt, h1))
        acs0, acs1 = _col(a_cs, h0), _col(a_cs, h1)
        y = _bdot(cm, states[j], 1, 0) * jnp.where(left, jnp.exp(acs0), jnp.exp(acs1))
        s_new = states[j] * jnp.where(left1, jnp.exp(_col(a_end, h0)), jnp.exp(_col(a_end, h1)))
        for h, acs, xh in ((h0, acs0, jnp.where(left, xdt, 0.0)), (h1, acs1, jnp.where(left, 0.0, xdt))):
            decay = jnp.exp(jnp.where(causal, acs - _row(a_cs_t, h), -jnp.inf))
            y = y + _bdot(cb * decay, xh, 1, 0)
            s_new = s_new + _bdot(bm * jnp.exp(_col(a_end, h) - acs), xh, 0, 0)
        ys.append((y + dfs[j] * xs[j]) * _silu(zs[j]))
        nexts.append(s_new)
    ssq = sum(jnp.sum(y * y, axis=-1, keepdims=True) for y in ys)
    scale = lax.rsqrt(ssq / BW + EPS)
    return [y * scale * nw for y, nw in zip(ys, nws)], nexts


def _blocks(ref, n, width=LANES):
    return [ref[:, j * width:(j + 1) * width] for j in range(n)]


def _ssd_fwd(proj, xa, dt_bias, a_log, dfull, nw, name):
    nc = SEQ // M2_CHUNK

    def body(x_ref, b_ref, c_ref, dt_ref, z_ref, dtb_ref, al_ref, df_ref, nw_ref, o_ref, sin_ref, st):
        @pl.when(pl.program_id(0) == 0)
        def _():
            st[...] = jnp.zeros_like(st)

        states = [st[j] for j in range(N_PAIR)]
        for j in range(N_PAIR):
            sin_ref[0, j] = states[j]
        ys, nexts = _ssd_chunk(_blocks(x_ref, 4), _blocks(b_ref, 2), _blocks(c_ref, 2), dt_ref[...], _blocks(z_ref, 4),
                               states, dtb_ref[...], al_ref[...], _blocks(df_ref, 4), _blocks(nw_ref, 4))
        for j in range(N_PAIR):
            o_ref[:, j * LANES:(j + 1) * LANES] = ys[j].astype(BF16)
            st[j] = nexts[j]

    vec8 = pl.BlockSpec((1, LANES), lambda c: (0, 0))
    vec = pl.BlockSpec((1, BW), lambda c: (0, 0))
    return pl.pallas_call(
        body, name=name, grid=(nc,),
        in_specs=[pl.BlockSpec((M2_CHUNK, BW), lambda c: (c, 0)), pl.BlockSpec((M2_CHUNK, 256), lambda c: (c, 2)),
                  pl.BlockSpec((M2_CHUNK, 256), lambda c: (c, 3)), pl.BlockSpec((M2_CHUNK, LANES), lambda c: (c, C_DT // LANES)),
                  pl.BlockSpec((M2_CHUNK, BW), lambda c: (c, C_M2Z // BW)), vec8, vec8, vec, vec],
        out_specs=[pl.BlockSpec((M2_CHUNK, BW), lambda c: (c, 0)),
                   pl.BlockSpec((1, N_PAIR, M2_STATE, LANES), lambda c: (c, 0, 0, 0))],
        out_shape=[jax.ShapeDtypeStruct((SEQ, BW), BF16), jax.ShapeDtypeStruct((nc, N_PAIR, M2_STATE, LANES), F32)],
        scratch_shapes=[pltpu.VMEM((N_PAIR, M2_STATE, LANES), F32)],
        compiler_params=_cparams(("arbitrary",)),
    )(xa, xa, xa, proj, proj, dt_bias, a_log, dfull, nw)


def _ssd_bwd(proj, dproj, xa, dout, s_in, dt_bias, a_log, dfull, nw, name):
    nc = SEQ // M2_CHUNK

    def body(x_ref, b_ref, c_ref, dt_ref, z_ref, do_ref, sin_ref, dtb_ref, al_ref, df_ref, nw_ref, dproj_in,
             dp_ref, dxa_ref, ddtb_ref, dal_ref, ddf_ref, dnw_ref, dst):
        @pl.when(pl.program_id(0) == 0)
        def _():
            dst[...] = jnp.zeros_like(dst)
            for r in (ddtb_ref, dal_ref, ddf_ref, dnw_ref):
                r[...] = jnp.zeros_like(r)

        states = [sin_ref[0, j] for j in range(N_PAIR)]
        _, vjp = jax.vjp(_ssd_chunk, _blocks(x_ref, 4), _blocks(b_ref, 2), _blocks(c_ref, 2), dt_ref[...],
                         _blocks(z_ref, 4), states, dtb_ref[...], al_ref[...], _blocks(df_ref, 4), _blocks(nw_ref, 4))
        dxs, dbs, dcs, ddt, dzs, dstates, ddtb, dal, ddfs, dnws = vjp(
            ([do_ref[0, :, j * LANES:(j + 1) * LANES] for j in range(N_PAIR)], [dst[j] for j in range(N_PAIR)]))
        for j in range(N_PAIR):
            sl = slice(j * LANES, (j + 1) * LANES)
            dxa_ref[:, sl] = dxs[j]
            dp_ref[:, sl] = dzs[j].astype(BF16)
            dst[j] = dstates[j]
            ddf_ref[:, sl] += ddfs[j]
            dnw_ref[:, sl] += dnws[j]
        for g in range(2):
            dxa_ref[:, BW + g * LANES:BW + (g + 1) * LANES] = dbs[g]
            dxa_ref[:, BW + 256 + g * LANES:BW + 256 + (g + 1) * LANES] = dcs[g]
        dp_ref[:, BW:BW + LANES] = ddt.astype(BF16)
        dp_ref[:, BW + LANES:] = jnp.zeros((M2_CHUNK, 2 * BW - BW - LANES), BF16)
        ddtb_ref[...] += ddtb
        dal_ref[...] += dal

    rev = lambda w, col=0: pl.BlockSpec((M2_CHUNK, w), lambda i: (nc - 1 - i, col))
    vec8 = pl.BlockSpec((1, LANES), lambda i: (0, 0))
    vec = pl.BlockSpec((1, BW), lambda i: (0, 0))
    return pl.pallas_call(
        body, name=name, grid=(nc,),
        in_specs=[rev(BW), rev(256, 2), rev(256, 3), rev(LANES, C_DT // LANES), rev(BW, C_M2Z // BW),
                  pl.BlockSpec((1, M2_CHUNK, BW), lambda i: (2, nc - 1 - i, 0)),
                  pl.BlockSpec((1, N_PAIR, M2_STATE, LANES), lambda i: (nc - 1 - i, 0, 0, 0)), vec8, vec8, vec, vec,
                  pl.BlockSpec(memory_space=pl.ANY)],
        out_specs=[rev(2 * BW, C_M2Z // (2 * BW)), rev(M2_CONV_CH), vec8, vec8, vec, vec],
        input_output_aliases={11: 0},
        out_shape=[jax.ShapeDtypeStruct((SEQ, IN_PAD), BF16), jax.ShapeDtypeStruct((SEQ, M2_CONV_CH), F32),
                   jax.ShapeDtypeStruct((1, LANES), F32), jax.ShapeDtypeStruct((1, LANES), F32),
                   jax.ShapeDtypeStruct((1, BW), F32), jax.ShapeDtypeStruct((1, BW), F32)],
        scratch_shapes=[pltpu.VMEM((N_PAIR, M2_STATE, LANES), F32)],
        compiler_params=_cparams(("arbitrary",)),
    )(xa, xa, xa, proj, proj, dout, s_in, dt_bias, a_log, dfull, nw, dproj)


def _sc_specs():
    col = lambda kind: pl.BlockSpec((SEQ, LANES), lambda j: (0, C_SC // LANES + 4 * j + kind))
    return [col(0), col(1), col(2), col(3)]


def _sc_fwd(proj, w, name):
    def body(b_ref, c_ref, h_ref, g_ref, w_ref, o_ref):
        ch = c_ref[...] * h_ref[...]
        acc = jnp.zeros_like(ch)
        for k in range(SC_CONV):
            acc = acc + w_ref[k:k + 1, :] * _shift_down(ch, SC_CONV - 1 - k)
        o_ref[...] = (b_ref[...] * acc * _silu(g_ref[...])).astype(BF16)

    return pl.pallas_call(
        body, name=name, grid=(BW // LANES,),
        in_specs=_sc_specs() + [pl.BlockSpec((SC_CONV, LANES), lambda j: (0, j))],
        out_specs=pl.BlockSpec((SEQ, LANES), lambda j: (0, j)),
        out_shape=jax.ShapeDtypeStruct((SEQ, BW), BF16),
        compiler_params=_cparams(("parallel",)),
    )(proj, proj, proj, proj, w)


def _sc_bwd(proj, dproj, dout, w, name):
    def body(b_ref, c_ref, h_ref, g_ref, do_ref, w_ref, dproj_in, dp_ref, dw_ref):
        cv, hv, gv = c_ref[...], h_ref[...], g_ref[...]
        ch = cv * hv
        chs = [_shift_down(ch, SC_CONV - 1 - k) for k in range(SC_CONV)]
        acc = jnp.zeros_like(ch)
        for k in range(SC_CONV):
            acc = acc + w_ref[k:k + 1, :] * chs[k]
        sg = jax.nn.sigmoid(gv)
        do = do_ref[0]
        bv = b_ref[...]
        dp_ref[:, 0:LANES] = (do * acc * (gv * sg)).astype(BF16)
        dp_ref[:, 3 * LANES:] = (do * bv * acc * (sg * (1.0 + gv * (1.0 - sg)))).astype(BF16)
        dacc = do * bv * (gv * sg)
        dch = jnp.zeros_like(ch)
        for k in range(SC_CONV):
            dch = dch + w_ref[k:k + 1, :] * _shift_up(dacc, SC_CONV - 1 - k)
            dw_ref[k:k + 1, :] = jnp.sum(dacc * chs[k], axis=0, keepdims=True)
        dp_ref[:, LANES:2 * LANES] = (dch * hv).astype(BF16)
        dp_ref[:, 2 * LANES:3 * LANES] = (dch * cv).astype(BF16)

    wsp = pl.BlockSpec((SC_CONV, LANES), lambda j: (0, j))
    return pl.pallas_call(
        body, name=name, grid=(BW // LANES,),
        in_specs=_sc_specs() + [pl.BlockSpec((1, SEQ, LANES), lambda j: (3, 0, j)), wsp, pl.BlockSpec(memory_space=pl.ANY)],
        out_specs=[pl.BlockSpec((SEQ, 4 * LANES), lambda j: (0, C_SC // (4 * LANES) + j)), wsp],
        input_output_aliases={6: 0},
        out_shape=[jax.ShapeDtypeStruct((SEQ, IN_PAD), BF16), jax.ShapeDtypeStruct((SC_CONV, BW), F32)],
        compiler_params=_cparams(("parallel",)),
    )(proj, proj, proj, proj, dout, w, dproj)


MERGE_T = 256
MERGE_BWD_T = 512


def _merge_fwd(proj, ys, merge_b, w_branch, name):
    def body(y_ref, lg_ref, b_ref, w_ref, o_ref):
        acc = jnp.zeros((MERGE_T, D_MODEL), F32)
        for k in range(N_BRANCH):
            gate = jax.nn.sigmoid(lg_ref[:, k * D_MODEL:(k + 1) * D_MODEL] + b_ref[k])
            acc = acc + gate * _dg(y_ref[k], w_ref[k], 1, 0)
        o_ref[...] = acc.astype(BF16)

    return pl.pallas_call(
        body, name=name, grid=(SEQ // MERGE_T,),
        in_specs=[pl.BlockSpec((N_BRANCH, MERGE_T, BW), lambda i: (0, i, 0)),
                  pl.BlockSpec((MERGE_T, N_BRANCH * D_MODEL), lambda i: (i, C_MERGE // (N_BRANCH * D_MODEL))),
                  pl.BlockSpec((N_BRANCH, 1, D_MODEL), lambda i: (0, 0, 0)),
                  pl.BlockSpec((N_BRANCH, BW, D_MODEL), lambda i: (0, 0, 0))],
        out_specs=pl.BlockSpec((MERGE_T, D_MODEL), lambda i: (i, 0)),
        out_shape=jax.ShapeDtypeStruct((SEQ, D_MODEL), BF16),
        compiler_params=_cparams(("parallel",)),
    )(ys, proj, merge_b, w_branch)


def _merge_bwd(proj, ys, dm, merge_b, w_branch, name):
    nt = SEQ // MERGE_BWD_T

    def body(y_ref, lg_ref, dm_ref, b_ref, w_ref, dy_ref, dlg_ref, dw_ref, db_ref, dw_acc):
        i = pl.program_id(1)
        gate = jax.nn.sigmoid(lg_ref[...] + b_ref[0])
        y = y_ref[0]
        dmv = dm_ref[...]
        dbo = (gate * dmv).astype(BF16)
        dlg = _dg(y, w_ref[0], 1, 0) * dmv * gate * (1.0 - gate)
        dlg_ref[...] = dlg.astype(BF16)
        dy_ref[0] = _dg(dbo, w_ref[0], 1, 1)
        dwp = _dg(y, dbo, 0, 0)
        dbp = jnp.sum(dlg, axis=0, keepdims=True)

        @pl.when(i == 0)
        def _():
            dw_acc[...] = dwp
            db_ref[0] = dbp

        @pl.when(i > 0)
        def _():
            dw_acc[...] += dwp
            db_ref[0] += dbp

        @pl.when(i == nt - 1)
        def _():
            dw_ref[0] = dw_acc[...].astype(BF16)

    return pl.pallas_call(
        body, name=name, grid=(N_BRANCH, nt),
        in_specs=[pl.BlockSpec((1, MERGE_BWD_T, BW), lambda k, i: (k, i, 0)),
                  pl.BlockSpec((MERGE_BWD_T, D_MODEL), lambda k, i: (i, C_MERGE // D_MODEL + k)),
                  pl.BlockSpec((MERGE_BWD_T, D_MODEL), lambda k, i: (i, 0)),
                  pl.BlockSpec((1, 1, D_MODEL), lambda k, i: (k, 0, 0)),
                  pl.BlockSpec((1, BW, D_MODEL), lambda k, i: (k, 0, 0))],
        out_specs=[pl.BlockSpec((1, MERGE_BWD_T, BW), lambda k, i: (k, i, 0)),
                   pl.BlockSpec((MERGE_BWD_T, D_MODEL), lambda k, i: (i, k)),
                   pl.BlockSpec((1, BW, D_MODEL), lambda k, i: (k, 0, 0)),
                   pl.BlockSpec((1, 1, D_MODEL), lambda k, i: (k, 0, 0))],
        out_shape=[jax.ShapeDtypeStruct((N_BRANCH, SEQ, BW), F32), jax.ShapeDtypeStruct((SEQ, IN_PAD), BF16),
                   jax.ShapeDtypeStruct((N_BRANCH, BW, D_MODEL), BF16), jax.ShapeDtypeStruct((N_BRANCH, 1, D_MODEL), F32)],
        scratch_shapes=[pltpu.VMEM((BW, D_MODEL), F32)],
        compiler_params=_cparams(("parallel", "arbitrary")),
    )(ys, proj, dm, merge_b, w_branch)


def _adamw(glist, w, m, v, rows, name):
    nl = len(glist)
    n, r, c = glist[0].shape
    assert w.shape == (nl, r, c) and r % rows == 0
    nb = r // rows

    def body(*refs):
        g_refs = refs[:nl]
        w_ref, m_ref, v_ref, go_ref, d_ref, mo_ref, vo_ref = refs[nl:]
        for layer in range(nl):
            @pl.when(pl.program_id(0) == layer)
            def _(g_ref=g_refs[layer]):
                g = g_ref[0].astype(F32)
                for s in range(1, n):
                    g = g + g_ref[s].astype(F32)
                mn = ADAM_B1 * m_ref[0] + (1.0 - ADAM_B1) * g
                vn = ADAM_B2 * v_ref[0] + (1.0 - ADAM_B2) * jnp.square(g)
                m_hat = mn / (1.0 - ADAM_B1 ** ADAM_STEP)
                v_hat = vn / (1.0 - ADAM_B2 ** ADAM_STEP)
                go_ref[0] = g
                d_ref[0] = -ADAM_LR * (m_hat / (jnp.sqrt(v_hat) + ADAM_EPS) + ADAM_WD * w_ref[0])
                mo_ref[0] = mn
                vo_ref[0] = vn

    def g_spec(layer):
        return pl.BlockSpec((n, rows, c), lambda a, i: (0, jnp.where(a < layer, 0, jnp.where(a == layer, i, nb - 1)), 0))

    blk = pl.BlockSpec((1, rows, c), lambda a, i: (a, i, 0))
    out = jax.ShapeDtypeStruct((nl, r, c), F32)
    return pl.pallas_call(
        body, name=name, grid=(nl, nb),
        in_specs=[g_spec(layer) for layer in range(nl)] + [blk, blk, blk],
        out_specs=[blk, blk, blk, blk], out_shape=[out, out, out, out],
        compiler_params=_cparams(("arbitrary", "arbitrary")),
    )(*glist, w, m, v)


X_ROWS_PER_COL = 2 * (D_MODEL // LANES)


def _w_in_to_x(w):
    t = jnp.transpose(w, (2, 0, 1)).reshape(SHARD_IN, DEPTH, D_MODEL // LANES, LANES)
    return jnp.transpose(t, (0, 2, 1, 3)).reshape(SHARD_IN * X_ROWS_PER_COL, LANES)


def _w_in_from_x(xv):
    t = jnp.transpose(xv.reshape(SHARD_IN, D_MODEL // LANES, DEPTH, LANES), (0, 2, 1, 3))
    return jnp.transpose(t.reshape(SHARD_IN, DEPTH, D_MODEL), (1, 2, 0))


def _adamw_w_in(glist, w, m, v, name):
    n = glist[0].shape[0]
    cols = LANES
    rows = cols * X_ROWS_PER_COL

    def body(g0_ref, g1_ref, w_ref, m_ref, v_ref, go_ref, d_ref, mo_ref, vo_ref):
        for layer, g_ref in enumerate((g0_ref, g1_ref)):
            g = g_ref[0].astype(F32)
            for s in range(1, n):
                g = g + g_ref[s].astype(F32)
            gt = g.T
            for t in range(D_MODEL // LANES):
                sel = (pl.ds(2 * t + layer, cols, stride=X_ROWS_PER_COL), slice(None))
                gs = gt[:, t * LANES:(t + 1) * LANES]
                mn = ADAM_B1 * m_ref[sel] + (1.0 - ADAM_B1) * gs
                vn = ADAM_B2 * v_ref[sel] + (1.0 - ADAM_B2) * jnp.square(gs)
                m_hat = mn / (1.0 - ADAM_B1 ** ADAM_STEP)
                v_hat = vn / (1.0 - ADAM_B2 ** ADAM_STEP)
                go_ref[sel] = gs
                d_ref[sel] = -ADAM_LR * (m_hat / (jnp.sqrt(v_hat) + ADAM_EPS) + ADAM_WD * w_ref[sel])
                mo_ref[sel] = mn
                vo_ref[sel] = vn

    g_spec = pl.BlockSpec((n, D_MODEL, cols), lambda i: (0, 0, i))
    blk = pl.BlockSpec((rows, LANES), lambda i: (i, 0))
    out = jax.ShapeDtypeStruct((SHARD_IN * X_ROWS_PER_COL, LANES), F32)
    res = pl.pallas_call(
        body, name=name, grid=(-(-SHARD_IN // cols),),
        in_specs=[g_spec, g_spec, blk, blk, blk], out_specs=[blk, blk, blk, blk], out_shape=[out, out, out, out],
        compiler_params=_cparams(("parallel",)),
    )(*glist, _w_in_to_x(w), _w_in_to_x(m), _w_in_to_x(v))
    return [_w_in_from_x(o) for o in res]


def _slot_sum(gslots, name):
    n, r, c = gslots.shape

    def body(g_ref, o_ref):
        g = g_ref[0]
        for s in range(1, n):
            g = g + g_ref[s]
        o_ref[...] = g

    return pl.pallas_call(
        body, name=name, in_specs=[pl.BlockSpec((n, r, c), lambda: (0, 0, 0))],
        out_specs=pl.BlockSpec((r, c), lambda: (0, 0)), out_shape=jax.ShapeDtypeStruct((r, c), F32),
        compiler_params=_cparams(None),
    )(gslots)


def _me_and_peers():
    x, y, c = lax.axis_index("x"), lax.axis_index("y"), lax.axis_index("c")
    me = 4 * x + 2 * y + c
    peers = []
    for k in range(1, N_DEV):
        px = 1 - x if (k >> 2) & 1 else x
        py = 1 - y if (k >> 1) & 1 else y
        pc = 1 - c if k & 1 else c
        peers.append((4 * px + 2 * py + pc, (px, py, pc)))
    return me, peers


def _exchange(tensors, gather, name):
    n = len(tensors)

    def body(*refs):
        ins, outs = refs[:n], refs[n:2 * n]
        send_sems, recv_sems, local_sems = refs[2 * n:]
        me, peers = _me_and_peers()
        started = []
        for t in range(n):
            own = pltpu.make_async_copy(ins[t] if gather else ins[t].at[me], outs[t].at[me], local_sems.at[t])
            own.start()
            started.append(own)
            for k, (pidx, pos) in enumerate(peers):
                cp = pltpu.make_async_remote_copy(
                    src_ref=ins[t] if gather else ins[t].at[pidx], dst_ref=outs[t].at[me],
                    send_sem=send_sems.at[t, k], recv_sem=recv_sems.at[t, k], device_id=pos, device_id_type=MESH)
                cp.start()
                started.append(cp)
        for cp in started:
            cp.wait()

    any_spec = pl.BlockSpec(memory_space=pl.ANY)
    outs = pl.pallas_call(
        body, name=name, in_specs=[any_spec] * n, out_specs=[any_spec] * n,
        out_shape=[jax.ShapeDtypeStruct(((N_DEV,) + t.shape) if gather else t.shape, t.dtype) for t in tensors],
        scratch_shapes=[pltpu.SemaphoreType.DMA((n, N_DEV - 1)), pltpu.SemaphoreType.DMA((n, N_DEV - 1)),
                        pltpu.SemaphoreType.DMA((n,))],
        compiler_params=pltpu.CompilerParams(has_side_effects=True),
    )(*tensors)
    return list(outs)


_HBM = pl.BlockSpec(memory_space=pltpu.HBM)
_SEM = pl.BlockSpec(memory_space=pltpu.SEMAPHORE)
_EFFECT = pltpu.SideEffectType.DATAFLOW_SIDE_EFFECTING


N_CHIP = N_DEV // 2


def _chip_peers():
    x, y, c = lax.axis_index("x"), lax.axis_index("y"), lax.axis_index("c")
    chips = []
    for d in range(1, N_CHIP):
        px = 1 - x if (d >> 1) & 1 else x
        py = 1 - y if d & 1 else y
        chips.append((2 * px + py, (px, py)))
    return (x, y, c), 2 * x + y, chips


def _plan_gather(ins, lands, send_sems, recv_sems, local_sems, first=0):
    (x, y, c), q, chips = _chip_peers()
    me = 2 * q + c
    plan = dict(start=[], relay_wait=[], relay_start=[], local=[], sends=[], recvs=[])
    for t in range(len(ins)):
        base = (first + t) * 7
        sem = lambda k: dict(send_sem=send_sems.at[base + k], recv_sem=recv_sems.at[base + k], device_id_type=MESH)
        own = pltpu.make_async_copy(ins[t], lands[t].at[me], local_sems.at[first + t])
        to_sib = pltpu.make_async_remote_copy(src_ref=ins[t], dst_ref=lands[t].at[me], device_id=(x, y, 1 - c), **sem(0))
        plan['start'] += [own, to_sib]
        plan['local'].append(own)
        plan['sends'].append(to_sib)
        plan['recvs'].append(to_sib)
        for d, (pq, (px, py)) in enumerate(chips):
            to_chip = pltpu.make_async_remote_copy(src_ref=ins[t], dst_ref=lands[t].at[me], device_id=(px, py, c), **sem(1 + d))
            blk = lands[t].at[2 * pq + c]
            fwd = pltpu.make_async_remote_copy(src_ref=blk, dst_ref=blk, device_id=(x, y, 1 - c), **sem(4 + d))
            plan['start'].append(to_chip)
            plan['relay_wait'].append(to_chip)
            plan['relay_start'].append(fwd)
            plan['sends'] += [to_chip, fwd]
            plan['recvs'].append(fwd)
    return plan


def _plan_pair(ins, lands, send_sems, recv_sems, local_sems):
    (x, y, c), q, chips = _chip_peers()
    plan = dict(start=[], local=[], sends=[], recvs=[])
    for t in range(len(ins)):
        for k in range(N_CHIP):
            cp = pltpu.make_async_remote_copy(
                src_ref=ins[t].at[2 * k + 1 - c], dst_ref=lands[t].at[k], send_sem=send_sems.at[t * N_CHIP + k],
                recv_sem=recv_sems.at[t * N_CHIP + k], device_id=(x, y, 1 - c), device_id_type=MESH)
            plan['start'].append(cp)
            plan['sends'].append(cp)
            plan['recvs'].append(cp)
    return plan


def _plan_chips(ins, lands, send_sems, recv_sems, local_sems):
    (x, y, c), q, chips = _chip_peers()
    plan = dict(start=[], local=[], sends=[], recvs=[])
    for t in range(len(ins)):
        own = pltpu.make_async_copy(ins[t].at[q], lands[t].at[q], local_sems.at[t])
        plan['start'].append(own)
        plan['local'].append(own)
        for d, (pq, (px, py)) in enumerate(chips):
            cp = pltpu.make_async_remote_copy(
                src_ref=ins[t].at[pq], dst_ref=lands[t].at[q], send_sem=send_sems.at[t * 3 + d],
                recv_sem=recv_sems.at[t * 3 + d], device_id=(px, py, c), device_id_type=MESH)
            plan['start'].append(cp)
            plan['sends'].append(cp)
            plan['recvs'].append(cp)
    return plan


def _split_start(plan_fn, tensors, land_shapes, n_sems, name):
    n = len(tensors)

    def body(*refs):
        ins, lands = refs[:n], refs[n:2 * n]
        plan = plan_fn(ins, lands, *refs[2 * n:2 * n + 3])
        for cp in plan['start']:
            cp.start()
        refs[-1][...] = jnp.zeros_like(refs[-1])

    outs = pl.pallas_call(
        body, name=name,
        out_shape=(pltpu.SemaphoreType.DMA((n_sems,)), pltpu.SemaphoreType.DMA((n_sems,)), pltpu.SemaphoreType.DMA((n,)),
                   *[pltpu.HBM(t.shape, t.dtype) for t in tensors],
                   *[pltpu.HBM(s, t.dtype) for s, t in zip(land_shapes, tensors)],
                   jax.ShapeDtypeStruct((8, LANES), F32)),
        in_specs=[_HBM] * (2 * n),
        out_specs=(_SEM, _SEM, _SEM, *[_HBM] * (2 * n), pl.BlockSpec(memory_space=pltpu.VMEM)),
        input_output_aliases={t: 3 + t for t in range(2 * n)},
        compiler_params=pltpu.CompilerParams(has_side_effects=_EFFECT),
    )(*[pltpu.with_memory_space_constraint(t, pltpu.HBM) for t in tensors],
      *[pltpu.with_memory_space_constraint(lax.empty(s, t.dtype), pltpu.HBM) for s, t in zip(land_shapes, tensors)])
    return outs[:-1], outs[-1]


def _split_relay(plan_fn, state, after, name):
    sems, thru = state[:3], state[3:]
    n = len(thru) // 2

    def arrived(*refs):
        plan = plan_fn(refs[:n], refs[n:2 * n], *refs[2 * n:2 * n + 3])
        for cp in plan['relay_wait']:
            cp.wait_recv()

    thru = pl.pallas_call(
        arrived, name=name + "_arrived",
        out_shape=tuple(pltpu.HBM(t.shape, t.dtype) for t in thru),
        in_specs=[_HBM] * (2 * n) + [_SEM, _SEM, _SEM, pl.BlockSpec(memory_space=pl.ANY)],
        out_specs=tuple([_HBM] * (2 * n)),
        input_output_aliases={t: t for t in range(2 * n)},
        compiler_params=pltpu.CompilerParams(has_side_effects=_EFFECT),
    )(*thru, *sems, after)

    def forward(*refs):
        plan = plan_fn(refs[:n], refs[n:2 * n], *refs[2 * n:2 * n + 3])
        for cp in plan['relay_start']:
            cp.start()
        refs[-1][...] = jnp.zeros_like(refs[-1])

    outs = pl.pallas_call(
        forward, name=name + "_forward",
        out_shape=(*[pltpu.HBM(t.shape, t.dtype) for t in thru], jax.ShapeDtypeStruct((8, LANES), F32)),
        in_specs=[_HBM] * (2 * n) + [_SEM, _SEM, _SEM],
        out_specs=(*[_HBM] * (2 * n), pl.BlockSpec(memory_space=pltpu.VMEM)),
        input_output_aliases={t: t for t in range(2 * n)},
        compiler_params=pltpu.CompilerParams(has_side_effects=_EFFECT),
    )(*thru, *sems)
    return (*sems, *outs[:-1]), outs[-1]


def _split_wait(plan_fn, state, after, name, with_sources=False):
    sems, thru = state[:3], state[3:]
    n = len(thru) // 2

    def body(*refs):
        plan = plan_fn(refs[:n], refs[n:2 * n], *refs[2 * n:2 * n + 3])
        for cp in plan['local']:
            cp.wait()
        for cp in plan['sends']:
            cp.wait_send()
        for cp in plan['recvs']:
            cp.wait_recv()

    outs = pl.pallas_call(
        body, name=name,
        out_shape=tuple(pltpu.HBM(t.shape, t.dtype) for t in thru),
        in_specs=[_HBM] * (2 * n) + [_SEM, _SEM, _SEM, pl.BlockSpec(memory_space=pl.ANY)],
        out_specs=tuple([_HBM] * (2 * n)),
        input_output_aliases={t: t for t in range(2 * n)},
        compiler_params=pltpu.CompilerParams(has_side_effects=_EFFECT),
    )(*thru, *sems, after)
    return (list(outs[:n]), list(outs[n:])) if with_sources else list(outs[n:])


PAIR_SUM_BLOCK = 512 * 1024


def _pair_sum(mine, theirs, name):
    _, r, c = mine.shape
    rows = r
    while rows * c > PAIR_SUM_BLOCK and rows % 32 == 0:
        rows //= 2

    def body(core_ref, a_ref, b_ref, o_ref):
        o_ref[0] = (a_ref[0].astype(F32) + b_ref[0].astype(F32)).astype(o_ref.dtype)

    return pl.pallas_call(
        body, name=name,
        grid_spec=pltpu.PrefetchScalarGridSpec(
            num_scalar_prefetch=1, grid=(N_CHIP, r // rows),
            in_specs=[pl.BlockSpec((1, rows, c), lambda k, i, core: (2 * k + core[0], i, 0)),
                      pl.BlockSpec((1, rows, c), lambda k, i, core: (k, i, 0))],
            out_specs=pl.BlockSpec((1, rows, c), lambda k, i, core: (k, i, 0))),
        out_shape=jax.ShapeDtypeStruct((N_CHIP, r, c), mine.dtype),
        compiler_params=_cparams(("parallel", "parallel")),
    )(lax.axis_index("c").astype(jnp.int32).reshape(1), mine, theirs)


WEIGHTS = ['norm_w', 'w_in', 's5_lambda_re', 's5_lambda_im', 's5_b_re', 's5_b_im', 's5_c_re', 's5_c_im', 's5_d',
           's5_log_step', 's5_w_glu', 'sgu_ln_w', 'sgu_ln_b', 'sgu_w', 'sgu_b', 'm2_conv_w', 'm2_conv_b', 'm2_dt_bias',
           'm2_a_log', 'm2_d', 'm2_norm_w', 'sc_conv_w', 'merge_b', 'w_branch', 'w_out', 'final_norm_w']
BIG_SHARDED = ['w_in', 'w_branch', 'w_out', 's5_w_glu']
SMALL_SHARDED = ['m2_conv_w', 'sc_conv_w', 'merge_b']
REPLICATED = [n for n in WEIGHTS if n not in BIG_SHARDED + SMALL_SHARDED]
S5_NAMES = ['s5_lambda_re', 's5_lambda_im', 's5_b_re', 's5_b_im', 's5_c_re', 's5_c_im', 's5_d', 's5_log_step']


def _sc_interleave(t):
    lead = t.shape[:-1]
    return jnp.swapaxes(t.reshape(lead + (4, 4, LANES)), -3, -2).reshape(lead + (4 * BW,))


def _pad_in(w):
    z = lambda n: jnp.zeros(w.shape[:-1] + (n,), w.dtype)
    return jnp.concatenate([w[..., 6152:], w[..., 0:1024], w[..., 3072:4096], w[..., 1024:2560], z(512),
                            w[..., 2560:3072], w[..., 4096:4104], z(504), _sc_interleave(w[..., 4104:6152])], axis=-1)


def _unpad_in(g):
    return jnp.concatenate([g[..., C_S5U:C_S5U + 1024], g[..., C_SGU_U:C_SGU_U + 1536], g[..., C_M2Z:C_M2Z + 512],
                            g[..., C_M2X:C_M2X + 1024], g[..., C_DT:C_DT + 8], _sc_interleave(g[..., C_SC:]),
                            g[..., :N_BRANCH * D_MODEL]], axis=-1)


ROW_BLOCK = 8 * LANES


def _pack_rows(tensors, row_mult, batched=False):
    parts = []
    for t in tensors:
        f = t.reshape((t.shape[0], -1) if batched else (1, -1))
        f = jnp.pad(f, ((0, 0), (0, (-f.shape[1]) % ROW_BLOCK)))
        parts.append(f.reshape(f.shape[0], -1, LANES))
    out = jnp.concatenate(parts, axis=1)
    out = jnp.pad(out, ((0, 0), (0, (-out.shape[1]) % row_mult), (0, 0)))
    return out if batched else out[0]


def _unpack_rows(rows, shapes):
    out, r0 = [], 0
    for shp in shapes:
        size = 1
        for s in shp:
            size *= s
        nr = -(-size // ROW_BLOCK) * 8
        out.append(rows[r0:r0 + nr].reshape(-1)[:size].reshape(shp))
        r0 += nr
    return out


def _kernel_col_map():
    m = np.full(IN_PAD, -1, np.int64)
    m[C_MERGE:C_MERGE + 4096] = np.arange(6152, 10248)
    m[C_S5U:C_S5U + 1024] = np.arange(0, 1024)
    m[C_M2X:C_M2X + 1024] = np.arange(3072, 4096)
    m[C_SGU_U:C_SGU_U + 1536] = np.arange(1024, 2560)
    m[C_M2Z:C_M2Z + 512] = np.arange(2560, 3072)
    m[C_DT:C_DT + 8] = np.arange(4096, 4104)
    for j in range(4):
        for kind in range(4):
            k0 = C_SC + 4 * LANES * j + LANES * kind
            m[k0:k0 + LANES] = 4104 + BW * kind + LANES * j + np.arange(LANES)
    return m


def _lane_pieces(sources):
    pieces, cur = [], None
    for lane, src in enumerate(sources):
        key = None if src is None else (src[0], src[1] // LANES, (lane - src[1]) % LANES)
        if cur is not None and key == cur[0]:
            cur[2] = lane + 1
        else:
            if cur is not None and cur[0] is not None:
                pieces.append((*cur[0], cur[1], cur[2]))
            cur = [key, lane, lane + 1]
    if cur is not None and cur[0] is not None:
        pieces.append((*cur[0], cur[1], cur[2]))
    return pieces


def _assemble_block(pieces, load, rows, dtype):
    lane = lax.broadcasted_iota(jnp.int32, (rows, LANES), 1)
    out = None
    for arr, sb, shift, lo, hi in pieces:
        v = load(arr, sb)
        if shift:
            v = pltpu.roll(v, shift, 1)
        if out is None and lo == 0 and hi == LANES:
            out = v
        else:
            out = jnp.where((lane >= lo) & (lane < hi), v, jnp.zeros((rows, LANES), dtype) if out is None else out)
    return jnp.zeros((rows, LANES), dtype) if out is None else out


RELAYOUT_ROWS = 256
SHARD_BLOCKS = -(-SHARD_IN // LANES)


def _load_shard_block(ref, rows):
    def load(j, sb):
        if sb == SHARD_BLOCKS - 1:
            return jnp.broadcast_to(ref[j, :, SHARD_IN - 1:SHARD_IN], (rows, LANES))
        return ref[j, :, sb * LANES:(sb + 1) * LANES]
    return load


def _relayout_w_in(gathered, name):
    kmap = _kernel_col_map()
    dtype = gathered.dtype

    def body(src_ref, o_ref):
        load = _load_shard_block(src_ref, RELAYOUT_ROWS)
        for ob in range(IN_PAD // LANES):
            srcs = [None if kmap[ob * LANES + l] < 0 else (int(kmap[ob * LANES + l]) // SHARD_IN, int(kmap[ob * LANES + l]) % SHARD_IN)
                    for l in range(LANES)]
            o_ref[:, ob * LANES:(ob + 1) * LANES] = _assemble_block(_lane_pieces(srcs), load, RELAYOUT_ROWS, dtype)

    return pl.pallas_call(
        body, name=name, grid=(D_MODEL // RELAYOUT_ROWS,),
        in_specs=[pl.BlockSpec((N_DEV, RELAYOUT_ROWS, SHARD_IN), lambda i: (0, i, 0))],
        out_specs=pl.BlockSpec((RELAYOUT_ROWS, IN_PAD), lambda i: (i, 0)),
        out_shape=jax.ShapeDtypeStruct((D_MODEL, IN_PAD), dtype),
        compiler_params=_cparams(("parallel",)),
    )(gathered)


def _relayout_g_in(gw, name):
    kmap = _kernel_col_map()
    kinv = np.zeros(IN_DIM, np.int64)
    kinv[kmap[kmap >= 0]] = np.nonzero(kmap >= 0)[0]
    dtype = gw.dtype

    def body(src_ref, o_ref):
        load = lambda _, sb: src_ref[:, sb * LANES:(sb + 1) * LANES]
        for j in range(N_DEV):
            for ob in range(SHARD_BLOCKS):
                srcs = [(0, int(kinv[SHARD_IN * j + ob * LANES + l])) if ob * LANES + l < SHARD_IN else None for l in range(LANES)]
                blk = _assemble_block(_lane_pieces(srcs), load, RELAYOUT_ROWS, dtype)
                if ob == SHARD_BLOCKS - 1:
                    o_ref[j, :, SHARD_IN - 1:SHARD_IN] = blk[:, 0:1]
                else:
                    o_ref[j, :, ob * LANES:(ob + 1) * LANES] = blk

    return pl.pallas_call(
        body, name=name, grid=(D_MODEL // RELAYOUT_ROWS,),
        in_specs=[pl.BlockSpec((RELAYOUT_ROWS, IN_PAD), lambda i: (i, 0))],
        out_specs=pl.BlockSpec((N_DEV, RELAYOUT_ROWS, SHARD_IN), lambda i: (0, i, 0)),
        out_shape=jax.ShapeDtypeStruct((N_DEV, D_MODEL, SHARD_IN), dtype),
        compiler_params=_cparams(("parallel",)),
    )(gw)


def _rows128(flat, row_mult=8):
    n = flat.shape[0]
    per = LANES * row_mult
    total = -(-n // per) * per
    return jnp.pad(flat, (0, total - n)).reshape(total // LANES, LANES)


def _pad_lanes(v):
    return jnp.pad(v, (0, LANES - v.shape[0])).reshape(1, LANES)


def _layer_prep(i, p):
    disc, disc_vjp = jax.vjp(_s5_disc, *[p[n][i] for n in S5_NAMES])
    prep = dict(
        nw=p['norm_w'][i].reshape(1, D_MODEL), disc_vjp=disc_vjp,
        s5small=[t.astype(BF16) for t in disc[:4]] + [disc[4], disc[5]],
        sgw=[p['sgu_ln_w'][i].reshape(1, BW), p['sgu_ln_b'][i].reshape(1, BW), p['sgu_w'][i],
             jnp.repeat(p['sgu_b'][i].T, BW // SGU_HEADS, axis=1)],
        cb=p['m2_conv_b'][i].reshape(1, M2_CONV_CH),
        m2w=[_pad_lanes(p['m2_dt_bias'][i]), _pad_lanes(p['m2_a_log'][i]),
             jnp.repeat(p['m2_d'][i], M2_HEAD_DIM).reshape(1, BW), p['m2_norm_w'][i].reshape(1, BW)])
    touch = [t[0, 0].astype(F32) for t in prep['s5small']] + [prep['sgw'][3][0, 0], prep['m2w'][2][0, 0]]
    return prep, sum(touch[1:], touch[0])


def _layer_fwd(x, h, i, prep, w_in, other_weights):
    proj = _matmul(h, w_in, 1, 0, F32, 1024, 1024, 1024, f"proj{i}")
    full = dict(other_weights(proj), w_in=w_in)
    s5w = prep['s5small'] + [full['s5_w_glu']]
    ya, sre, sim = _s5_fwd(proj, *s5w, f"s5_fwd{i}")
    yb = _sgu_fwd(proj, *prep['sgw'], f"sgu_fwd{i}")
    cw = full['m2_conv_w']
    xa = _m2_conv_fwd(proj, cw, prep['cb'], f"m2conv_fwd{i}")
    yc, s_in = _ssd_fwd(proj, xa, *prep['m2w'], f"ssd_fwd{i}")
    scw = full['sc_conv_w']
    yd = _sc_fwd(proj, scw, f"sc_fwd{i}")
    ys = jnp.stack([ya, yb, yc, yd])
    mb = full['merge_b'].reshape(N_BRANCH, 1, D_MODEL)
    merged = _merge_fwd(proj, ys, mb, full['w_branch'], f"merge_fwd{i}")
    x_new = _matmul(merged, full['w_out'], 1, 0, F32, 1024, 1024, 1024, f"out{i}", residual=x)
    saved = dict(x=x, nw=prep['nw'], h=h, proj=proj, disc_vjp=prep['disc_vjp'], s5w=s5w, sre=sre, sim=sim, sgw=prep['sgw'],
                 cw=cw, cb=prep['cb'], xa=xa, m2w=prep['m2w'], s_in=s_in, scw=scw, ys=ys, mb=mb, merged=merged)
    return x_new, saved, full


def _layer_bwd(dx_out, i, sv, full, on_large_grads=None):
    g = {}
    proj = sv['proj']
    dm = _matmul(dx_out, full['w_out'], 1, 1, F32, 1024, 1024, 1024, f"dmerged{i}")
    g['w_out'] = _matmul(sv['merged'], dx_out, 0, 0, BF16, 1024, 1024, 1024, f"gw_out{i}")
    dys, dproj, g['w_branch'], dmb = _merge_bwd(proj, sv['ys'], dm, sv['mb'], full['w_branch'], f"merge_bwd{i}")
    g['merge_b'] = dmb.reshape(N_BRANCH, D_MODEL)
    dproj, dbbre, dbbim, dcre, dcim, da, dd, dwg = _s5_bwd(proj, dproj, dys, sv['sre'], sv['sim'], *sv['s5w'], f"s5_bwd{i}")
    for n, t in zip(S5_NAMES, sv['disc_vjp']((dbbre, dbbim, dcre, dcim, da, dd))):
        g[n] = t
    g['s5_w_glu'] = dwg.astype(BF16)
    dproj, dlw, dlb, g['sgu_w'], dbias = _sgu_bwd(proj, dproj, dys, *sv['sgw'], f"sgu_bwd{i}")
    g['sgu_ln_w'], g['sgu_ln_b'] = dlw[0], dlb[0]
    g['sgu_b'] = dbias.reshape(SGU_CHUNK, SGU_HEADS, BW // SGU_HEADS).sum(-1).T
    dproj, dxa, ddtb, dal, ddf, dnw = _ssd_bwd(proj, dproj, sv['xa'], dys, sv['s_in'], *sv['m2w'], f"ssd_bwd{i}")
    dproj, g['m2_conv_w'], dcb = _m2_conv_bwd(proj, dproj, dxa, sv['cw'], sv['cb'], f"m2conv_bwd{i}")
    g['m2_conv_b'], g['m2_norm_w'] = dcb[0], dnw[0]
    g['m2_dt_bias'], g['m2_a_log'] = ddtb[0, :M2_HEADS], dal[0, :M2_HEADS]
    g['m2_d'] = ddf.reshape(M2_HEADS, M2_HEAD_DIM).sum(-1)
    dproj, g['sc_conv_w'] = _sc_bwd(proj, dproj, dys, sv['scw'], f"sc_bwd{i}")
    g['w_in'] = _matmul(sv['h'], dproj, 0, 0, BF16, 1024, 1024, 1024, f"gw_in{i}")
    tok = on_large_grads(g) if on_large_grads else None
    dh = _matmul(dproj, full['w_in'], 1, 1, F32, 1024, 1024, 1024, f"dh{i}", after=tok)
    dx_in, dnw_l = _rmsnorm_bwd(sv['x'], sv['nw'], dh, dx_out, f"rms_bwd{i}")
    g['norm_w'] = dnw_l[0]
    return dx_in, g


def _split8(t, axis):
    shp = t.shape
    t = t.reshape(shp[:axis] + (N_DEV, shp[axis] // N_DEV) + shp[axis + 1:])
    return jnp.moveaxis(t, axis, 0)


def _join8(t, axis):
    t = jnp.moveaxis(t, 0, axis)
    shp = t.shape
    return t.reshape(shp[:axis] + (shp[axis] * shp[axis + 1],) + shp[axis + 2:])


SHARD_AXIS = {'w_in': 2, 'w_branch': 3, 'w_out': 1, 's5_w_glu': 1, 'm2_conv_w': 2, 'sc_conv_w': 2, 'merge_b': 2}


OTHER_BIG = [n for n in BIG_SHARDED if n != 'w_in']


def _other_weights(gathered):
    return {n: _join8(t, SHARD_AXIS[n] - 1) for n, t in zip(OTHER_BIG, gathered)}


def _layer_grad_blocks(g, i):
    blocks = [_relayout_g_in(g[n], f"relayout_g_in{i}") if n == 'w_in' else _split8(g[n], SHARD_AXIS[n] - 1) for n in BIG_SHARDED]
    return [b.reshape(N_DEV, -1, b.shape[-1]) for b in blocks]


def _start_reduce_scatter(blocks, i):
    pair_shapes = [(N_CHIP,) + b.shape[1:] for b in blocks]
    state, tok = _split_start(_plan_pair, blocks, pair_shapes, N_CHIP * len(blocks), f"pair{i}_start")
    mine, theirs = _split_wait(_plan_pair, state, tok, f"pair{i}_wait", with_sources=True)
    sums = [_pair_sum(b, t, f"pair_sum{i}_{k}") for k, (b, t) in enumerate(zip(mine, theirs))]
    return _split_start(_plan_chips, sums, pair_shapes, 3 * len(blocks), f"chips{i}_start")


def kernel(x, norm_w, w_in, s5_lambda_re, s5_lambda_im, s5_b_re, s5_b_im, s5_c_re, s5_c_im, s5_d, s5_log_step, s5_w_glu, sgu_ln_w, sgu_ln_b, sgu_w, sgu_b, m2_conv_w, m2_conv_b, m2_dt_bias, m2_a_log, m2_d, m2_norm_w, sc_conv_w, merge_b, w_branch, w_out, final_norm_w, loss_target, m_norm_w, m_w_in, m_s5_lambda_re, m_s5_lambda_im, m_s5_b_re, m_s5_b_im, m_s5_c_re, m_s5_c_im, m_s5_d, m_s5_log_step, m_s5_w_glu, m_sgu_ln_w, m_sgu_ln_b, m_sgu_w, m_sgu_b, m_m2_conv_w, m_m2_conv_b, m_m2_dt_bias, m_m2_a_log, m_m2_d, m_m2_norm_w, m_sc_conv_w, m_merge_b, m_w_branch, m_w_out, m_final_norm_w, v_norm_w, v_w_in, v_s5_lambda_re, v_s5_lambda_im, v_s5_b_re, v_s5_b_im, v_s5_c_re, v_s5_c_im, v_s5_d, v_s5_log_step, v_s5_w_glu, v_sgu_ln_w, v_sgu_ln_b, v_sgu_w, v_sgu_b, v_m2_conv_w, v_m2_conv_b, v_m2_dt_bias, v_m2_a_log, v_m2_d, v_m2_norm_w, v_sc_conv_w, v_merge_b, v_w_branch, v_w_out, v_final_norm_w):
    loc = locals()
    p = {n: loc[n] for n in WEIGHTS}
    mom = {n: loc['m_' + n] for n in WEIGHTS}
    vel = {n: loc['v_' + n] for n in WEIGHTS}

    small_sizes = [p[n].size for n in SMALL_SHARDED]
    small_pack = _rows128(jnp.concatenate([p[n].reshape(-1) for n in SMALL_SHARDED]))
    shards = ([p['w_in'][0].astype(BF16)] + [p[n][0].astype(BF16) for n in OTHER_BIG] + [small_pack]
              + [p[n][1].astype(BF16) for n in BIG_SHARDED])
    gath, tok = _split_start(_plan_gather, shards, [(N_DEV,) + t.shape for t in shards], 7 * len(shards), "gather_start")
    sems, srcs, lands = gath[:3], gath[3:3 + len(shards)], gath[3 + len(shards):]

    def gathered(lo, hi, after, name):
        plan = functools.partial(_plan_gather, first=lo)
        state, tok = _split_relay(plan, (*sems, *srcs[lo:hi], *lands[lo:hi]), after, name + "_relay")
        return _split_wait(plan, state, tok, name + "_wait")

    later = dict(p, **{n: p[n] + tok[0, 0] for n in ('norm_w', 's5_log_step', 'sgu_b', 'm2_d')})
    preps = [_layer_prep(i, later) for i in range(DEPTH)]
    h0 = _rmsnorm_fwd(x[0], preps[0][0]['nw'], "rms_fwd0")
    got = gathered(0, 1, tok + (preps[0][1] + preps[1][1] + h0[0, 0].astype(F32)), "gather_w_in0")
    small_full = {}

    def other_weights0(proj):
        got = gathered(1, 5, proj, "gather_rest0")
        small_all, off = got[-1].reshape(N_DEV, -1), 0
        for n, sz in zip(SMALL_SHARDED, small_sizes):
            small_full[n] = _join8(small_all[:, off:off + sz].reshape((N_DEV,) + p[n].shape), SHARD_AXIS[n])
            off += sz
        return dict(_other_weights(got[:-1]), **{n: small_full[n][0] for n in SMALL_SHARDED})

    saved, layer_g, full = [None] * DEPTH, [None] * DEPTH, [None] * DEPTH
    xs, saved[0], full[0] = _layer_fwd(x[0], h0, 0, preps[0][0], _relayout_w_in(got[0], "relayout_w_in0"), other_weights0)
    h1 = _rmsnorm_fwd(xs, preps[1][0]['nw'], "rms_fwd1")
    got = gathered(5, 9, h1, "gather1")
    xs, saved[1], full[1] = _layer_fwd(
        xs, h1, 1, preps[1][0], _relayout_w_in(got[0], "relayout_w_in1"),
        lambda proj: dict(_other_weights(got[1:]), **{n: small_full[n][1] for n in SMALL_SHARDED}))
    loss_row, dx, dfw = _loss_head(xs, final_norm_w.reshape(1, D_MODEL), loss_target[0])
    loss = lax.psum(loss_row[0, 0], ("x", "y", "c"))
    scat = [None] * DEPTH

    def start_scatter(i):
        def start(g):
            scat[i], tok = _start_reduce_scatter(_layer_grad_blocks(g, i), i)
            return tok
        return start

    dx, layer_g[1] = _layer_bwd(dx, 1, saved[1], full[1], start_scatter(1))
    dx, layer_g[0] = _layer_bwd(dx, 0, saved[0], full[0], start_scatter(0))
    grads = {n: jnp.stack([layer_g[i][n] for i in range(DEPTH)]) for n in SMALL_SHARDED + REPLICATED if n != 'final_norm_w'}
    grads['final_norm_w'] = dfw[0]

    out_g, out_d, out_m, out_v = {}, {}, {}, {}
    repl_rows = _pack_rows([grads[n] for n in REPLICATED], 8 * N_DEV)
    rr = repl_rows.shape[0] // N_DEV
    shard_rows = _pack_rows([_split8(grads[n], SHARD_AXIS[n]) for n in SMALL_SHARDED], 8, batched=True)
    rs = shard_rows.shape[1]
    small_g = jnp.concatenate([shard_rows, repl_rows.reshape(N_DEV, rr, LANES)], axis=1)
    small_sum = _slot_sum(_exchange([small_g], False, "scatter_small")[0], "sum_small")
    repl_all = _exchange([small_sum[rs:]], True, "gather_small")[0].reshape(N_DEV * rr, LANES)
    g_all = jnp.concatenate([small_sum[:rs], repl_all], axis=0)
    names = SMALL_SHARDED + REPLICATED
    packed = [jnp.concatenate([_pack_rows([d[n] for n in SMALL_SHARDED], 8), _pack_rows([d[n] for n in REPLICATED], 8 * N_DEV)],
                              axis=0) for d in (p, mom, vel)]
    res = _adamw([g_all[None]], *[t[None] for t in packed], g_all.shape[0], "adamw_small")
    for o, dst in zip(res, (out_g, out_d, out_m, out_v)):
        pieces = (_unpack_rows(o[0, :rs], [p[n].shape for n in SMALL_SHARDED])
                  + _unpack_rows(o[0, rs:], [p[n].shape for n in REPLICATED]))
        dst.update(zip(names, pieces))

    landed1 = _split_wait(_plan_chips, scat[1], res[0], "chips1_wait")
    landed0 = _split_wait(_plan_chips, scat[0], landed1[0], "chips0_wait")
    for k, n in enumerate(BIG_SHARDED):
        shp = p[n].shape
        c = shp[-1]
        r = p[n].size // (DEPTH * c)
        if n == 'w_in':
            big = _adamw_w_in([landed0[k], landed1[k]], p[n], mom[n], vel[n], "adamw_w_in")
        else:
            big = _adamw([landed0[k], landed1[k]], *[d[n].reshape(DEPTH, r, c) for d in (p, mom, vel)],
                         {'w_branch': 512, 'w_out': 128, 's5_w_glu': 64}[n], "adamw_" + n)
        out_g[n], out_d[n], out_m[n], out_v[n] = [o.reshape(shp) for o in big]
    return (loss, dx[None], *[out_g[n] for n in WEIGHTS], *[out_d[n] for n in WEIGHTS],
            *[out_m[n] for n in WEIGHTS], *[out_v[n] for n in WEIGHTS])
```

```python
import functools

import jax
import jax.numpy as jnp
import numpy as np
from jax import lax
from jax.experimental import pallas as pl
from jax.experimental.pallas import tpu as pltpu

F32 = jnp.float32
BF16 = jnp.bfloat16

N_DEV = 8
SEQ = 2048
D_MODEL = 1024
DEPTH = 2
BW = 512
N_BRANCH = 4
EPS = 1e-6
S5_GROUPS, S5_STATE, S5_P = 32, 64, 16
S5_CH = S5_GROUPS * S5_STATE
SGU_CHUNK, SGU_HEADS = 128, 8
M2_HEADS, M2_HEAD_DIM, M2_STATE, M2_CHUNK, M2_CONV = 8, 64, 128, 128, 4
M2_CONV_CH = 1024
SC_CONV = 3
IN_DIM = 10248
IN_PAD = 11264
C_MERGE = 0
C_S5U, C_S5G = 4096, 4608
C_M2X = 5120
C_SGU_U, C_SGU_V, C_SGU_G = 6144, 6656, 7168
C_M2Z, C_DT = 8192, 8704
C_SC = 9216
SHARD_IN = IN_DIM // N_DEV

ADAM_LR, ADAM_B1, ADAM_B2, ADAM_EPS, ADAM_WD, ADAM_STEP = 0.001, 0.9, 0.999, 1e-08, 0.01, 10

VMEM_LIMIT = 56 * 1024 * 1024
LANES = 128

MESH = pl.DeviceIdType.MESH


def _cparams(sem=None, **kw):
    return pltpu.CompilerParams(dimension_semantics=sem, vmem_limit_bytes=VMEM_LIMIT, **kw)


def _dg(a, b, ca, cb, precision=None):
    return lax.dot_general(a, b, (((ca,), (cb,)), ((), ())), precision=precision,
                           preferred_element_type=F32)


@functools.partial(jax.custom_vjp, nondiff_argnums=(2, 3))
def _bdot(a, b, ca, cb):
    return _dg(a.astype(BF16), b.astype(BF16), ca, cb)


def _bdot_fwd(a, b, ca, cb):
    return _bdot(a, b, ca, cb), (a, b)


def _bdot_bwd(ca, cb, res, g):
    a, b = res
    gb, ab, bb = g.astype(BF16), a.astype(BF16), b.astype(BF16)
    da = _dg(gb, bb, 1, 1 - cb) if ca == 1 else _dg(bb, gb, 1 - cb, 1)
    db = _dg(ab, gb, 1 - ca, 0) if cb == 0 else _dg(gb, ab, 0, 1 - ca)
    return da.astype(a.dtype), db.astype(b.dtype)


_bdot.defvjp(_bdot_fwd, _bdot_bwd)


def _rms(x, w):
    return x * lax.rsqrt(jnp.mean(x * x, axis=-1, keepdims=True) + EPS) * w


def _silu(x):
    return x * jax.nn.sigmoid(x)


def _gelu(x):
    return 0.5 * x * (1.0 + jnp.tanh(0.7978845608028654 * (x + 0.044715 * (x * x * x))))


def _softplus(x):
    return jnp.maximum(x, 0.0) + jnp.log1p(jnp.exp(-jnp.abs(x)))


def _shift_down(x, s):
    if s == 0:
        return x
    row = lax.broadcasted_iota(jnp.int32, x.shape, 0)
    return jnp.where(row >= s, pltpu.roll(x, s, 0), 0.0)


def _shift_up(x, s):
    if s == 0:
        return x
    n = x.shape[0]
    row = lax.broadcasted_iota(jnp.int32, x.shape, 0)
    return jnp.where(row < n - s, pltpu.roll(x, n - s, 0), 0.0)


def _matmul(a, b, ca, cb, out_dtype, tm, tn, tk, name, residual=None, after=None):
    m = a.shape[1 - ca]
    k = a.shape[ca]
    n = b.shape[1 - cb]
    assert b.shape[cb] == k and m % tm == 0 and n % tn == 0 and k % tk == 0
    nk = k // tk
    a_spec = pl.BlockSpec((tm, tk), lambda i, j, kk: (i, kk)) if ca == 1 else pl.BlockSpec((tk, tm), lambda i, j, kk: (kk, i))
    b_spec = pl.BlockSpec((tk, tn), lambda i, j, kk: (kk, j)) if cb == 0 else pl.BlockSpec((tn, tk), lambda i, j, kk: (j, kk))
    o_spec = pl.BlockSpec((tm, tn), lambda i, j, kk: (i, j))
    has_res = residual is not None

    def body(*refs):
        refs = refs[:2 + has_res] + refs[2 + has_res + (after is not None):]
        if has_res:
            a_ref, b_ref, r_ref, o_ref, acc = refs
        else:
            a_ref, b_ref, o_ref, acc = refs
        kk = pl.program_id(2)
        part = _dg(a_ref[...].astype(BF16), b_ref[...].astype(BF16), ca, cb)

        @pl.when(kk == 0)
        def _():
            acc[...] = part

        @pl.when(kk > 0)
        def _():
            acc[...] += part

        @pl.when(kk == nk - 1)
        def _():
            r = acc[...]
            if has_res:
                r = r + r_ref[...]
            o_ref[...] = r.astype(out_dtype)

    ins = [a, b] + ([residual] if has_res else []) + ([after] if after is not None else [])
    specs = [a_spec, b_spec] + ([o_spec] if has_res else []) + ([pl.BlockSpec(memory_space=pl.ANY)] if after is not None else [])
    return pl.pallas_call(
        body, name=name, grid=(m // tm, n // tn, nk), in_specs=specs, out_specs=o_spec,
        out_shape=jax.ShapeDtypeStruct((m, n), out_dtype),
        scratch_shapes=[pltpu.VMEM((tm, tn), F32)],
        compiler_params=_cparams(("parallel", "parallel", "arbitrary")),
    )(*ins)


ROW_TILE = 512


def _rmsnorm_fwd(x, w, name):
    def body(x_ref, w_ref, o_ref):
        o_ref[...] = _rms(x_ref[...], w_ref[...]).astype(BF16)

    return pl.pallas_call(
        body, name=name, grid=(SEQ // ROW_TILE,),
        in_specs=[pl.BlockSpec((ROW_TILE, D_MODEL), lambda i: (i, 0)), pl.BlockSpec((1, D_MODEL), lambda i: (0, 0))],
        out_specs=pl.BlockSpec((ROW_TILE, D_MODEL), lambda i: (i, 0)),
        out_shape=jax.ShapeDtypeStruct((SEQ, D_MODEL), BF16),
        compiler_params=_cparams(("parallel",)),
    )(x, w)


def _rmsnorm_bwd(x, w, dh, dres, name):
    def body(x_ref, w_ref, dh_ref, dres_ref, dx_ref, dw_ref):
        _, vjp = jax.vjp(_rms, x_ref[...], w_ref[...])
        dx, dw = vjp(dh_ref[...])
        dx_ref[...] = dx + dres_ref[...]

        @pl.when(pl.program_id(0) == 0)
        def _():
            dw_ref[...] = dw

        @pl.when(pl.program_id(0) > 0)
        def _():
            dw_ref[...] += dw

    tile = pl.BlockSpec((ROW_TILE, D_MODEL), lambda i: (i, 0))
    vec = pl.BlockSpec((1, D_MODEL), lambda i: (0, 0))
    return pl.pallas_call(
        body, name=name, grid=(SEQ // ROW_TILE,),
        in_specs=[tile, vec, tile, tile], out_specs=[tile, vec],
        out_shape=[jax.ShapeDtypeStruct((SEQ, D_MODEL), F32), jax.ShapeDtypeStruct((1, D_MODEL), F32)],
        compiler_params=_cparams(("arbitrary",)),
    )(x, w, dh, dres)


def _loss_head(x, w, target):
    def body(x_ref, w_ref, t_ref, loss_ref, dx_ref, dw_ref):
        tgt = t_ref[...]

        def f(xv, wv):
            err = _rms(xv, wv) - tgt
            return 0.5 * jnp.sum(jnp.mean(err * err, axis=-1))

        loss, vjp = jax.vjp(f, x_ref[...], w_ref[...])
        dx, dw = vjp(jnp.ones((), F32))
        dx_ref[...] = dx
        lrow = jnp.full((1, LANES), loss, F32)

        @pl.when(pl.program_id(0) == 0)
        def _():
            dw_ref[...] = dw
            loss_ref[...] = lrow

        @pl.when(pl.program_id(0) > 0)
        def _():
            dw_ref[...] += dw
            loss_ref[...] += lrow

    tile = pl.BlockSpec((ROW_TILE, D_MODEL), lambda i: (i, 0))
    vec = pl.BlockSpec((1, D_MODEL), lambda i: (0, 0))
    return pl.pallas_call(
        body, name="loss_head", grid=(SEQ // ROW_TILE,),
        in_specs=[tile, vec, tile], out_specs=[pl.BlockSpec((1, LANES), lambda i: (0, 0)), tile, vec],
        out_shape=[jax.ShapeDtypeStruct((1, LANES), F32), jax.ShapeDtypeStruct((SEQ, D_MODEL), F32),
                   jax.ShapeDtypeStruct((1, D_MODEL), F32)],
        compiler_params=_cparams(("arbitrary",)),
    )(x, w, target)


S5_T = 256
S5_BLOCKS = [(slice(j * 256, (j + 1) * 256), slice(j * 1024, (j + 1) * 1024)) for j in range(2)]


def _s5_post(ypre, gate, wglu):
    y = _gelu(ypre)
    y = y * jax.nn.sigmoid(_bdot(y, wglu, 1, 0))
    return y * _silu(gate)


def _s5_fwd(proj, bbre, bbim, cre, cim, a2, dvec, wglu, name):
    def body(u_ref, g_ref, bbre_ref, bbim_ref, cre_ref, cim_ref, a_ref, d_ref, wg_ref, o_ref, sre_ref, sim_ref, st):
        @pl.when(pl.program_id(0) == 0)
        def _():
            st[...] = jnp.zeros_like(st)

        u = u_ref[...]
        ub = u.astype(BF16)
        for us, ss in S5_BLOCKS:
            sre_ref[:, ss] = _dg(ub[:, us], bbre_ref[us, ss], 1, 0)
            sim_ref[:, ss] = _dg(ub[:, us], bbim_ref[us, ss], 1, 0)
        ar, ai = a_ref[0:1, :], a_ref[1:2, :]

        def step(t, carry):
            sr, si = carry
            nr = ar * sr - ai * si + sre_ref[pl.ds(t, 1), :]
            ni = ar * si + ai * sr + sim_ref[pl.ds(t, 1), :]
            sre_ref[pl.ds(t, 1), :] = nr
            sim_ref[pl.ds(t, 1), :] = ni
            return nr, ni

        sr, si = lax.fori_loop(0, S5_T, step, (st[0:1, :], st[1:2, :]), unroll=8)
        st[0:1, :] = sr
        st[1:2, :] = si
        ypre = jnp.concatenate(
            [_dg(sre_ref[:, ss].astype(BF16), cre_ref[ss, us], 1, 0) - _dg(sim_ref[:, ss].astype(BF16), cim_ref[ss, us], 1, 0)
             for us, ss in S5_BLOCKS], axis=1) + d_ref[...] * u
        o_ref[...] = _s5_post(ypre, g_ref[...], wg_ref[...]).astype(BF16)

    full = lambda shape: pl.BlockSpec(shape, lambda c: (0, 0))
    return pl.pallas_call(
        body, name=name, grid=(SEQ // S5_T,),
        in_specs=[pl.BlockSpec((S5_T, BW), lambda c: (c, C_S5U // BW)), pl.BlockSpec((S5_T, BW), lambda c: (c, C_S5G // BW)),
                  full((BW, S5_CH)), full((BW, S5_CH)), full((S5_CH, BW)), full((S5_CH, BW)),
                  full((2, S5_CH)), full((1, BW)), full((BW, BW))],
        out_specs=[pl.BlockSpec((S5_T, BW), lambda c: (c, 0)), pl.BlockSpec((S5_T, S5_CH), lambda c: (c, 0)),
                   pl.BlockSpec((S5_T, S5_CH), lambda c: (c, 0))],
        out_shape=[jax.ShapeDtypeStruct((SEQ, BW), BF16), jax.ShapeDtypeStruct((SEQ, S5_CH), F32),
                   jax.ShapeDtypeStruct((SEQ, S5_CH), F32)],
        scratch_shapes=[pltpu.VMEM((2, S5_CH), F32)],
        compiler_params=_cparams(("arbitrary",)),
    )(proj, proj, bbre, bbim, cre, cim, a2, dvec, wglu)


def _s5_bwd(proj, dproj, dout, sre, sim, bbre, bbim, cre, cim, a2, dvec, wglu, name):
    nc = SEQ // S5_T

    def body(u_ref, g_ref, do_ref, sre_ref, sim_ref, pre_ref, pim_ref, bbre_ref, bbim_ref, cre_ref, cim_ref, a_ref,
             d_ref, wg_ref, dproj_in, dp_ref, dbbre_ref, dbbim_ref, dcre_ref, dcim_ref, da_ref, dd_ref, dwg_ref,
             gre, gim, st):
        c = nc - 1 - pl.program_id(0)

        @pl.when(pl.program_id(0) == 0)
        def _():
            st[...] = jnp.zeros_like(st)
            for r in (dbbre_ref, dbbim_ref, dcre_ref, dcim_ref, da_ref, dd_ref, dwg_ref):
                r[...] = jnp.zeros_like(r)

        u = u_ref[...]
        s_re, s_im = sre_ref[...], sim_ref[...]

        def head(s_res, s_ims, cres, cims, dv, uv, gv, wg):
            ypre = jnp.concatenate([_bdot(sr, cr, 1, 0) - _bdot(si, ci, 1, 0)
                                    for sr, si, cr, ci in zip(s_res, s_ims, cres, cims)], axis=1) + dv * uv
            return _s5_post(ypre, gv, wg)

        _, vjp = jax.vjp(head, [sre_ref[:, ss] for _, ss in S5_BLOCKS], [sim_ref[:, ss] for _, ss in S5_BLOCKS],
                         [cre_ref[ss, us].astype(F32) for us, ss in S5_BLOCKS],
                         [cim_ref[ss, us].astype(F32) for us, ss in S5_BLOCKS],
                         d_ref[...], u, g_ref[...], wg_ref[...].astype(F32))
        ds_res, ds_ims, dcres, dcims, dd, du_d, dgate, dwg = vjp(do_ref[0])
        for k, (us, ss) in enumerate(S5_BLOCKS):
            dcre_ref[ss, us] += dcres[k]
            dcim_ref[ss, us] += dcims[k]
            gre[:, ss] = ds_res[k]
            gim[:, ss] = ds_ims[k]
        dd_ref[...] += dd
        dwg_ref[...] += dwg
        dp_ref[:, BW:] = dgate.astype(BF16)
        ar, ai = a_ref[0:1, :], a_ref[1:2, :]

        def step(i, carry):
            t = S5_T - 1 - i
            gr, gi = carry
            nr = gre[pl.ds(t, 1), :] + gr
            ni = gim[pl.ds(t, 1), :] + gi
            gre[pl.ds(t, 1), :] = nr
            gim[pl.ds(t, 1), :] = ni
            return ar * nr + ai * ni, ar * ni - ai * nr

        gr, gi = lax.fori_loop(0, S5_T, step, (st[0:1, :], st[1:2, :]), unroll=8)
        st[0:1, :] = gr
        st[1:2, :] = gi
        g_re, g_im = gre[...], gim[...]
        first = jnp.where(c > 0, 1.0, 0.0)
        row = lax.broadcasted_iota(jnp.int32, (S5_T, S5_CH), 0)
        p_re = jnp.where(row == 0, pre_ref[7:8, :] * first, pltpu.roll(s_re, 1, 0))
        p_im = jnp.where(row == 0, pim_ref[7:8, :] * first, pltpu.roll(s_im, 1, 0))
        da_ref[0:1, :] += jnp.sum(g_re * p_re + g_im * p_im, axis=0, keepdims=True)
        da_ref[1:2, :] += jnp.sum(g_im * p_re - g_re * p_im, axis=0, keepdims=True)
        ub, grb, gib = u.astype(BF16), g_re.astype(BF16), g_im.astype(BF16)
        du_s = []
        for us, ss in S5_BLOCKS:
            dbbre_ref[us, ss] += _dg(ub[:, us], grb[:, ss], 0, 0)
            dbbim_ref[us, ss] += _dg(ub[:, us], gib[:, ss], 0, 0)
            du_s.append(_dg(grb[:, ss], bbre_ref[us, ss], 1, 1) + _dg(gib[:, ss], bbim_ref[us, ss], 1, 1))
        dp_ref[:, :BW] = (du_d + jnp.concatenate(du_s, axis=1)).astype(BF16)

    full = lambda shape: pl.BlockSpec(shape, lambda i: (0, 0))
    rev = lambda w, col=0: pl.BlockSpec((S5_T, w), lambda i: (nc - 1 - i, col))
    prev = pl.BlockSpec((8, S5_CH), lambda i: (jnp.maximum((nc - 1 - i) * (S5_T // 8) - 1, 0), 0))
    return pl.pallas_call(
        body, name=name, grid=(nc,),
        in_specs=[rev(BW, C_S5U // BW), rev(BW, C_S5G // BW), pl.BlockSpec((1, S5_T, BW), lambda i: (0, nc - 1 - i, 0)),
                  rev(S5_CH), rev(S5_CH), prev, prev,
                  full((BW, S5_CH)), full((BW, S5_CH)), full((S5_CH, BW)), full((S5_CH, BW)),
                  full((2, S5_CH)), full((1, BW)), full((BW, BW)), pl.BlockSpec(memory_space=pl.ANY)],
        out_specs=[rev(2 * BW, C_S5U // (2 * BW)), full((BW, S5_CH)), full((BW, S5_CH)), full((S5_CH, BW)), full((S5_CH, BW)),
                   full((2, S5_CH)), full((1, BW)), full((BW, BW))],
        input_output_aliases={14: 0},
        out_shape=[jax.ShapeDtypeStruct((SEQ, IN_PAD), BF16),
                   jax.ShapeDtypeStruct((BW, S5_CH), F32), jax.ShapeDtypeStruct((BW, S5_CH), F32),
                   jax.ShapeDtypeStruct((S5_CH, BW), F32), jax.ShapeDtypeStruct((S5_CH, BW), F32),
                   jax.ShapeDtypeStruct((2, S5_CH), F32), jax.ShapeDtypeStruct((1, BW), F32),
                   jax.ShapeDtypeStruct((BW, BW), F32)],
        scratch_shapes=[pltpu.VMEM((S5_T, S5_CH), F32), pltpu.VMEM((S5_T, S5_CH), F32), pltpu.VMEM((2, S5_CH), F32)],
        compiler_params=_cparams(("arbitrary",)),
    )(proj, proj, dout, sre, sim, sre, sim, bbre, bbim, cre, cim, a2, dvec, wglu, dproj)


def _s5_disc(lam_re, lam_im, b_re, b_im, c_re, c_im, d, log_step):
    step = jnp.exp(log_step)[:, None]
    mag = jnp.exp(lam_re * step)
    ab_re, ab_im = mag * jnp.cos(lam_im * step), mag * jnp.sin(lam_im * step)
    den = lam_re * lam_re + lam_im * lam_im
    nr = ab_re - 1.0
    coef_re = (nr * lam_re + ab_im * lam_im) / den
    coef_im = (ab_im * lam_re - nr * lam_im) / den
    bb_re = coef_re[..., None] * b_re - coef_im[..., None] * b_im
    bb_im = coef_re[..., None] * b_im + coef_im[..., None] * b_re
    def block_diag(t, rows_per, cols_per):
        wide = jnp.tile(t.reshape(S5_GROUPS * rows_per, cols_per), (1, S5_GROUPS))
        r = lax.broadcasted_iota(jnp.int32, wide.shape, 0) // rows_per
        c = lax.broadcasted_iota(jnp.int32, wide.shape, 1) // cols_per
        return jnp.where(r == c, wide, 0.0)

    bbre = block_diag(jnp.swapaxes(bb_re, 1, 2), S5_P, S5_STATE)
    bbim = block_diag(jnp.swapaxes(bb_im, 1, 2), S5_P, S5_STATE)
    cre = block_diag(jnp.swapaxes(c_re, 1, 2), S5_STATE, S5_P)
    cim = block_diag(jnp.swapaxes(c_im, 1, 2), S5_STATE, S5_P)
    a2 = jnp.stack([ab_re.reshape(-1), ab_im.reshape(-1)])
    return bbre, bbim, cre, cim, a2, d.reshape(1, BW)


def _left_lanes(shape):
    return lax.broadcasted_iota(jnp.int32, shape, 1) < 64


def _sgu_chunk(u, v, gate, ln_w, ln_b, w, bias):
    u32, v32 = _gelu(u), _gelu(v)
    mu = jnp.mean(v32, axis=-1, keepdims=True)
    var = jnp.mean(jnp.square(v32 - mu), axis=-1, keepdims=True)
    vn = (v32 - mu) * lax.rsqrt(var + EPS) * ln_w + ln_b
    t_i = lax.broadcasted_iota(jnp.int32, (SGU_CHUNK, SGU_CHUNK), 0)
    s_i = lax.broadcasted_iota(jnp.int32, (SGU_CHUNK, SGU_CHUNK), 1)
    causal = t_i >= s_i
    left = _left_lanes((SGU_CHUNK, LANES))
    sgate = _silu(gate)
    outs = []
    for j in range(BW // LANES):
        vb = vn[:, j * LANES:(j + 1) * LANES]
        s_blk = (_bdot(jnp.where(causal, w[2 * j], 0.0), jnp.where(left, vb, 0.0), 1, 0)
                 + _bdot(jnp.where(causal, w[2 * j + 1], 0.0), jnp.where(left, 0.0, vb), 1, 0))
        sl = slice(j * LANES, (j + 1) * LANES)
        outs.append(u32[:, sl] * (s_blk + bias[:, sl]) * sgate[:, sl])
    return outs


def _sgu_fwd(proj, ln_w, ln_b, w, bias, name):
    def body(u_ref, v_ref, g_ref, lw_ref, lb_ref, w_ref, b_ref, o_ref):
        outs = _sgu_chunk(u_ref[...], v_ref[...], g_ref[...], lw_ref[...], lb_ref[...], w_ref[...], b_ref[...])
        for j, o in enumerate(outs):
            o_ref[:, j * LANES:(j + 1) * LANES] = o.astype(BF16)

    blk = lambda col: pl.BlockSpec((SGU_CHUNK, BW), lambda c: (c, col // BW))
    vec = pl.BlockSpec((1, BW), lambda c: (0, 0))
    return pl.pallas_call(
        body, name=name, grid=(SEQ // SGU_CHUNK,),
        in_specs=[blk(C_SGU_U), blk(C_SGU_V), blk(C_SGU_G), vec, vec,
                  pl.BlockSpec((SGU_HEADS, SGU_CHUNK, SGU_CHUNK), lambda c: (0, 0, 0)),
                  pl.BlockSpec((SGU_CHUNK, BW), lambda c: (0, 0))],
        out_specs=pl.BlockSpec((SGU_CHUNK, BW), lambda c: (c, 0)),
        out_shape=jax.ShapeDtypeStruct((SEQ, BW), BF16),
        compiler_params=_cparams(("parallel",)),
    )(proj, proj, proj, ln_w, ln_b, w, bias)


def _sgu_bwd(proj, dproj, dout, ln_w, ln_b, w, bias, name):
    def body(u_ref, v_ref, g_ref, do_ref, lw_ref, lb_ref, w_ref, b_ref, dproj_in, dp_ref, dlw_ref, dlb_ref, dw_ref, db_ref):
        _, vjp = jax.vjp(_sgu_chunk, u_ref[...], v_ref[...], g_ref[...], lw_ref[...], lb_ref[...], w_ref[...], b_ref[...])
        do = do_ref[0]
        du, dv, dgate, dlw, dlb, dw, db = vjp([do[:, j * LANES:(j + 1) * LANES] for j in range(BW // LANES)])
        dp_ref[:, 0:BW] = du.astype(BF16)
        dp_ref[:, BW:2 * BW] = dv.astype(BF16)
        dp_ref[:, 2 * BW:3 * BW] = dgate.astype(BF16)
        dp_ref[:, 3 * BW:] = jnp.zeros((SGU_CHUNK, BW), BF16)

        @pl.when(pl.program_id(0) == 0)
        def _():
            dlw_ref[...] = dlw
            dlb_ref[...] = dlb
            dw_ref[...] = dw
            db_ref[...] = db

        @pl.when(pl.program_id(0) > 0)
        def _():
            dlw_ref[...] += dlw
            dlb_ref[...] += dlb
            dw_ref[...] += dw
            db_ref[...] += db

    blk = lambda col: pl.BlockSpec((SGU_CHUNK, BW), lambda c: (c, col // BW))
    vec = pl.BlockSpec((1, BW), lambda c: (0, 0))
    wsp = pl.BlockSpec((SGU_HEADS, SGU_CHUNK, SGU_CHUNK), lambda c: (0, 0, 0))
    bsp = pl.BlockSpec((SGU_CHUNK, BW), lambda c: (0, 0))
    return pl.pallas_call(
        body, name=name, grid=(SEQ // SGU_CHUNK,),
        in_specs=[blk(C_SGU_U), blk(C_SGU_V), blk(C_SGU_G), pl.BlockSpec((1, SGU_CHUNK, BW), lambda c: (1, c, 0)),
                  vec, vec, wsp, bsp, pl.BlockSpec(memory_space=pl.ANY)],
        out_specs=[pl.BlockSpec((SGU_CHUNK, 4 * BW), lambda c: (c, C_SGU_U // (4 * BW))), vec, vec, wsp, bsp],
        input_output_aliases={8: 0},
        out_shape=[jax.ShapeDtypeStruct((SEQ, IN_PAD), BF16), jax.ShapeDtypeStruct((1, BW), F32),
                   jax.ShapeDtypeStruct((1, BW), F32), jax.ShapeDtypeStruct((SGU_HEADS, SGU_CHUNK, SGU_CHUNK), F32),
                   jax.ShapeDtypeStruct((SGU_CHUNK, BW), F32)],
        compiler_params=_cparams(("arbitrary",)),
    )(proj, proj, proj, dout, ln_w, ln_b, w, bias, dproj)


CONV_BLK = 256


def _m2_conv_fwd(proj, w, b, name):
    def body(x_ref, w_ref, b_ref, o_ref):
        x = x_ref[...]
        acc = jnp.zeros_like(x) + b_ref[...]
        for k in range(M2_CONV):
            acc = acc + w_ref[k:k + 1, :] * _shift_down(x, M2_CONV - 1 - k)
        o_ref[...] = _silu(acc)

    return pl.pallas_call(
        body, name=name, grid=(M2_CONV_CH // CONV_BLK,),
        in_specs=[pl.BlockSpec((SEQ, CONV_BLK), lambda j: (0, C_M2X // CONV_BLK + j)),
                  pl.BlockSpec((M2_CONV, CONV_BLK), lambda j: (0, j)), pl.BlockSpec((1, CONV_BLK), lambda j: (0, j))],
        out_specs=pl.BlockSpec((SEQ, CONV_BLK), lambda j: (0, j)),
        out_shape=jax.ShapeDtypeStruct((SEQ, M2_CONV_CH), F32),
        compiler_params=_cparams(("parallel",)),
    )(proj, w, b)


def _m2_conv_bwd(proj, dproj, dxa, w, b, name):
    def body(x_ref, d_ref, w_ref, b_ref, dproj_in, dx_ref, dw_ref, db_ref):
        x = x_ref[...]
        xs = [_shift_down(x, M2_CONV - 1 - k) for k in range(M2_CONV)]
        acc = jnp.zeros_like(x) + b_ref[...]
        for k in range(M2_CONV):
            acc = acc + w_ref[k:k + 1, :] * xs[k]
        sg = jax.nn.sigmoid(acc)
        dacc = d_ref[...] * (sg * (1.0 + acc * (1.0 - sg)))
        dx = jnp.zeros_like(x)
        for k in range(M2_CONV):
            dx = dx + w_ref[k:k + 1, :] * _shift_up(dacc, M2_CONV - 1 - k)
            dw_ref[k:k + 1, :] = jnp.sum(dacc * xs[k], axis=0, keepdims=True)
        dx_ref[...] = dx.astype(BF16)
        db_ref[...] = jnp.sum(dacc, axis=0, keepdims=True)

    return pl.pallas_call(
        body, name=name, grid=(M2_CONV_CH // CONV_BLK,),
        in_specs=[pl.BlockSpec((SEQ, CONV_BLK), lambda j: (0, C_M2X // CONV_BLK + j)),
                  pl.BlockSpec((SEQ, CONV_BLK), lambda j: (0, j)),
                  pl.BlockSpec((M2_CONV, CONV_BLK), lambda j: (0, j)), pl.BlockSpec((1, CONV_BLK), lambda j: (0, j)),
                  pl.BlockSpec(memory_space=pl.ANY)],
        out_specs=[pl.BlockSpec((SEQ, CONV_BLK), lambda j: (0, C_M2X // CONV_BLK + j)),
                   pl.BlockSpec((M2_CONV, CONV_BLK), lambda j: (0, j)), pl.BlockSpec((1, CONV_BLK), lambda j: (0, j))],
        input_output_aliases={4: 0},
        out_shape=[jax.ShapeDtypeStruct((SEQ, IN_PAD), BF16), jax.ShapeDtypeStruct((M2_CONV, M2_CONV_CH), F32),
                   jax.ShapeDtypeStruct((1, M2_CONV_CH), F32)],
        compiler_params=_cparams(("parallel",)),
    )(proj, dxa, w, b, dproj)


N_PAIR = M2_HEADS // 2
HI = lax.Precision.HIGHEST


def _col(a, h):
    lane = lax.broadcasted_iota(jnp.int32, a.shape, 1)
    return jnp.sum(jnp.where(lane == h, a, 0.0), axis=1, keepdims=True)


def _row(a, h):
    sub = lax.broadcasted_iota(jnp.int32, a.shape, 0)
    return jnp.sum(jnp.where(sub == h, a, 0.0), axis=0, keepdims=True)


def _ssd_chunk(xs, bms, cms, dtr, zs, states, dt_bias, a_log, dfs, nws):
    q = M2_CHUNK
    dt = _softplus(dtr + dt_bias)
    da = dt * (-jnp.exp(a_log))
    l_i = lax.broadcasted_iota(jnp.int32, (q, q), 0)
    s_i = lax.broadcasted_iota(jnp.int32, (q, q), 1)
    causal = l_i >= s_i
    tril = jnp.where(causal, 1.0, 0.0)
    a_cs = _dg(tril, da, 1, 0, HI)
    a_cs_t = _dg(da, tril, 0, 1, HI)
    a_end = _row(a_cs, q - 1)
    left = _left_lanes((q, LANES))
    left1 = _left_lanes((1, LANES))
    ys, nexts = [], []
    for j in range(N_PAIR):
        grp = j // 2
        bm, cm = bms[grp], cms[grp]
        h0, h1 = 2 * j, 2 * j + 1
        cb = _bdot(cm, bm, 1, 1)
        xdt = xs[j] * jnp.where(left, _col(dt, h0), _col(dt, h1))
        acs0, acs1 = _col(a_cs, h0), _col(a_cs, h1)
        y = _bdot(cm, states[j], 1, 0) * jnp.where(left, jnp.exp(acs0), jnp.exp(acs1))
        s_new = states[j] * jnp.where(left1, jnp.exp(_col(a_end, h0)), jnp.exp(_col(a_end, h1)))
        for h, acs, xh in ((h0, acs0, jnp.where(left, xdt, 0.0)), (h1, acs1, jnp.where(left, 0.0, xdt))):
            decay = jnp.exp(jnp.where(causal, acs - _row(a_cs_t, h), -jnp.inf))
            y = y + _bdot(cb * decay, xh, 1, 0)
            s_new = s_new + _bdot(bm * jnp.exp(_col(a_end, h) - acs), xh, 0, 0)
        ys.append((y + dfs[j] * xs[j]) * _silu(zs[j]))
        nexts.append(s_new)
    ssq = sum(jnp.sum(y * y, axis=-1, keepdims=True) for y in ys)
    scale = lax.rsqrt(ssq / BW + EPS)
    return [y * scale * nw for y, nw in zip(ys, nws)], nexts


def _blocks(ref, n, width=LANES):
    return [ref[:, j * width:(j + 1) * width] for j in range(n)]


def _ssd_fwd(proj, xa, dt_bias, a_log, dfull, nw, name):
    nc = SEQ // M2_CHUNK

    def body(x_ref, b_ref, c_ref, dt_ref, z_ref, dtb_ref, al_ref, df_ref, nw_ref, o_ref, sin_ref, st):
        @pl.when(pl.program_id(0) == 0)
        def _():
            st[...] = jnp.zeros_like(st)

        states = [st[j] for j in range(N_PAIR)]
        for j in range(N_PAIR):
            sin_ref[0, j] = states[j]
        ys, nexts = _ssd_chunk(_blocks(x_ref, 4), _blocks(b_ref, 2), _blocks(c_ref, 2), dt_ref[...], _blocks(z_ref, 4),
                               states, dtb_ref[...], al_ref[...], _blocks(df_ref, 4), _blocks(nw_ref, 4))
        for j in range(N_PAIR):
            o_ref[:, j * LANES:(j + 1) * LANES] = ys[j].astype(BF16)
            st[j] = nexts[j]

    vec8 = pl.BlockSpec((1, LANES), lambda c: (0, 0))
    vec = pl.BlockSpec((1, BW), lambda c: (0, 0))
    return pl.pallas_call(
        body, name=name, grid=(nc,),
        in_specs=[pl.BlockSpec((M2_CHUNK, BW), lambda c: (c, 0)), pl.BlockSpec((M2_CHUNK, 256), lambda c: (c, 2)),
                  pl.BlockSpec((M2_CHUNK, 256), lambda c: (c, 3)), pl.BlockSpec((M2_CHUNK, LANES), lambda c: (c, C_DT // LANES)),
                  pl.BlockSpec((M2_CHUNK, BW), lambda c: (c, C_M2Z // BW)), vec8, vec8, vec, vec],
        out_specs=[pl.BlockSpec((M2_CHUNK, BW), lambda c: (c, 0)),
                   pl.BlockSpec((1, N_PAIR, M2_STATE, LANES), lambda c: (c, 0, 0, 0))],
        out_shape=[jax.ShapeDtypeStruct((SEQ, BW), BF16), jax.ShapeDtypeStruct((nc, N_PAIR, M2_STATE, LANES), F32)],
        scratch_shapes=[pltpu.VMEM((N_PAIR, M2_STATE, LANES), F32)],
        compiler_params=_cparams(("arbitrary",)),
    )(xa, xa, xa, proj, proj, dt_bias, a_log, dfull, nw)


def _ssd_bwd(proj, dproj, xa, dout, s_in, dt_bias, a_log, dfull, nw, name):
    nc = SEQ // M2_CHUNK

    def body(x_ref, b_ref, c_ref, dt_ref, z_ref, do_ref, sin_ref, dtb_ref, al_ref, df_ref, nw_ref, dproj_in,
             dp_ref, dxa_ref, ddtb_ref, dal_ref, ddf_ref, dnw_ref, dst):
        @pl.when(pl.program_id(0) == 0)
        def _():
            dst[...] = jnp.zeros_like(dst)
            for r in (ddtb_ref, dal_ref, ddf_ref, dnw_ref):
                r[...] = jnp.zeros_like(r)

        states = [sin_ref[0, j] for j in range(N_PAIR)]
        _, vjp = jax.vjp(_ssd_chunk, _blocks(x_ref, 4), _blocks(b_ref, 2), _blocks(c_ref, 2), dt_ref[...],
                         _blocks(z_ref, 4), states, dtb_ref[...], al_ref[...], _blocks(df_ref, 4), _blocks(nw_ref, 4))
        dxs, dbs, dcs, ddt, dzs, dstates, ddtb, dal, ddfs, dnws = vjp(
            ([do_ref[0, :, j * LANES:(j + 1) * LANES] for j in range(N_PAIR)], [dst[j] for j in range(N_PAIR)]))
        for j in range(N_PAIR):
            sl = slice(j * LANES, (j + 1) * LANES)
            dxa_ref[:, sl] = dxs[j]
            dp_ref[:, sl] = dzs[j].astype(BF16)
            dst[j] = dstates[j]
            ddf_ref[:, sl] += ddfs[j]
            dnw_ref[:, sl] += dnws[j]
        for g in range(2):
            dxa_ref[:, BW + g * LANES:BW + (g + 1) * LANES] = dbs[g]
            dxa_ref[:, BW + 256 + g * LANES:BW + 256 + (g + 1) * LANES] = dcs[g]
        dp_ref[:, BW:BW + LANES] = ddt.astype(BF16)
        dp_ref[:, BW + LANES:] = jnp.zeros((M2_CHUNK, 2 * BW - BW - LANES), BF16)
        ddtb_ref[...] += ddtb
        dal_ref[...] += dal

    rev = lambda w, col=0: pl.BlockSpec((M2_CHUNK, w), lambda i: (nc - 1 - i, col))
    vec8 = pl.BlockSpec((1, LANES), lambda i: (0, 0))
    vec = pl.BlockSpec((1, BW), lambda i: (0, 0))
    return pl.pallas_call(
        body, name=name, grid=(nc,),
        in_specs=[rev(BW), rev(256, 2), rev(256, 3), rev(LANES, C_DT // LANES), rev(BW, C_M2Z // BW),
                  pl.BlockSpec((1, M2_CHUNK, BW), lambda i: (2, nc - 1 - i, 0)),
                  pl.BlockSpec((1, N_PAIR, M2_STATE, LANES), lambda i: (nc - 1 - i, 0, 0, 0)), vec8, vec8, vec, vec,
                  pl.BlockSpec(memory_space=pl.ANY)],
        out_specs=[rev(2 * BW, C_M2Z // (2 * BW)), rev(M2_CONV_CH), vec8, vec8, vec, vec],
        input_output_aliases={11: 0},
        out_shape=[jax.ShapeDtypeStruct((SEQ, IN_PAD), BF16), jax.ShapeDtypeStruct((SEQ, M2_CONV_CH), F32),
                   jax.ShapeDtypeStruct((1, LANES), F32), jax.ShapeDtypeStruct((1, LANES), F32),
                   jax.ShapeDtypeStruct((1, BW), F32), jax.ShapeDtypeStruct((1, BW), F32)],
        scratch_shapes=[pltpu.VMEM((N_PAIR, M2_STATE, LANES), F32)],
        compiler_params=_cparams(("arbitrary",)),
    )(xa, xa, xa, proj, proj, dout, s_in, dt_bias, a_log, dfull, nw, dproj)


def _sc_specs():
    col = lambda kind: pl.BlockSpec((SEQ, LANES), lambda j: (0, C_SC // LANES + 4 * j + kind))
    return [col(0), col(1), col(2), col(3)]


def _sc_fwd(proj, w, name):
    def body(b_ref, c_ref, h_ref, g_ref, w_ref, o_ref):
        ch = c_ref[...] * h_ref[...]
        acc = jnp.zeros_like(ch)
        for k in range(SC_CONV):
            acc = acc + w_ref[k:k + 1, :] * _shift_down(ch, SC_CONV - 1 - k)
        o_ref[...] = (b_ref[...] * acc * _silu(g_ref[...])).astype(BF16)

    return pl.pallas_call(
        body, name=name, grid=(BW // LANES,),
        in_specs=_sc_specs() + [pl.BlockSpec((SC_CONV, LANES), lambda j: (0, j))],
        out_specs=pl.BlockSpec((SEQ, LANES), lambda j: (0, j)),
        out_shape=jax.ShapeDtypeStruct((SEQ, BW), BF16),
        compiler_params=_cparams(("parallel",)),
    )(proj, proj, proj, proj, w)


def _sc_bwd(proj, dproj, dout, w, name):
    def body(b_ref, c_ref, h_ref, g_ref, do_ref, w_ref, dproj_in, dp_ref, dw_ref):
        cv, hv, gv = c_ref[...], h_ref[...], g_ref[...]
        ch = cv * hv
        chs = [_shift_down(ch, SC_CONV - 1 - k) for k in range(SC_CONV)]
        acc = jnp.zeros_like(ch)
        for k in range(SC_CONV):
            acc = acc + w_ref[k:k + 1, :] * chs[k]
        sg = jax.nn.sigmoid(gv)
        do = do_ref[0]
        bv = b_ref[...]
        dp_ref[:, 0:LANES] = (do * acc * (gv * sg)).astype(BF16)
        dp_ref[:, 3 * LANES:] = (do * bv * acc * (sg * (1.0 + gv * (1.0 - sg)))).astype(BF16)
        dacc = do * bv * (gv * sg)
        dch = jnp.zeros_like(ch)
        for k in range(SC_CONV):
            dch = dch + w_ref[k:k + 1, :] * _shift_up(dacc, SC_CONV - 1 - k)
            dw_ref[k:k + 1, :] = jnp.sum(dacc * chs[k], axis=0, keepdims=True)
        dp_ref[:, LANES:2 * LANES] = (dch * hv).astype(BF16)
        dp_ref[:, 2 * LANES:3 * LANES] = (dch * cv).astype(BF16)

    wsp = pl.BlockSpec((SC_CONV, LANES), lambda j: (0, j))
    return pl.pallas_call(
        body, name=name, grid=(BW // LANES,),
        in_specs=_sc_specs() + [pl.BlockSpec((1, SEQ, LANES), lambda j: (3, 0, j)), wsp, pl.BlockSpec(memory_space=pl.ANY)],
        out_specs=[pl.BlockSpec((SEQ, 4 * LANES), lambda j: (0, C_SC // (4 * LANES) + j)), wsp],
        input_output_aliases={6: 0},
        out_shape=[jax.ShapeDtypeStruct((SEQ, IN_PAD), BF16), jax.ShapeDtypeStruct((SC_CONV, BW), F32)],
        compiler_params=_cparams(("parallel",)),
    )(proj, proj, proj, proj, dout, w, dproj)


MERGE_T = 256
MERGE_BWD_T = 512


def _merge_fwd(proj, ys, merge_b, w_branch, name):
    def body(y_ref, lg_ref, b_ref, w_ref, o_ref):
        acc = jnp.zeros((MERGE_T, D_MODEL), F32)
        for k in range(N_BRANCH):
            gate = jax.nn.sigmoid(lg_ref[:, k * D_MODEL:(k + 1) * D_MODEL] + b_ref[k])
            acc = acc + gate * _dg(y_ref[k], w_ref[k], 1, 0)
        o_ref[...] = acc.astype(BF16)

    return pl.pallas_call(
        body, name=name, grid=(SEQ // MERGE_T,),
        in_specs=[pl.BlockSpec((N_BRANCH, MERGE_T, BW), lambda i: (0, i, 0)),
                  pl.BlockSpec((MERGE_T, N_BRANCH * D_MODEL), lambda i: (i, C_MERGE // (N_BRANCH * D_MODEL))),
                  pl.BlockSpec((N_BRANCH, 1, D_MODEL), lambda i: (0, 0, 0)),
                  pl.BlockSpec((N_BRANCH, BW, D_MODEL), lambda i: (0, 0, 0))],
        out_specs=pl.BlockSpec((MERGE_T, D_MODEL), lambda i: (i, 0)),
        out_shape=jax.ShapeDtypeStruct((SEQ, D_MODEL), BF16),
        compiler_params=_cparams(("parallel",)),
    )(ys, proj, merge_b, w_branch)


def _merge_bwd(proj, ys, dm, merge_b, w_branch, name):
    nt = SEQ // MERGE_BWD_T

    def body(y_ref, lg_ref, dm_ref, b_ref, w_ref, dy_ref, dlg_ref, dw_ref, db_ref, dw_acc):
        i = pl.program_id(1)
        gate = jax.nn.sigmoid(lg_ref[...] + b_ref[0])
        y = y_ref[0]
        dmv = dm_ref[...]
        dbo = (gate * dmv).astype(BF16)
        dlg = _dg(y, w_ref[0], 1, 0) * dmv * gate * (1.0 - gate)
        dlg_ref[...] = dlg.astype(BF16)
        dy_ref[0] = _dg(dbo, w_ref[0], 1, 1)
        dwp = _dg(y, dbo, 0, 0)
        dbp = jnp.sum(dlg, axis=0, keepdims=True)

        @pl.when(i == 0)
        def _():
            dw_acc[...] = dwp
            db_ref[0] = dbp

        @pl.when(i > 0)
        def _():
            dw_acc[...] += dwp
            db_ref[0] += dbp

        @pl.when(i == nt - 1)
        def _():
            dw_ref[0] = dw_acc[...].astype(BF16)

    return pl.pallas_call(
        body, name=name, grid=(N_BRANCH, nt),
        in_specs=[pl.BlockSpec((1, MERGE_BWD_T, BW), lambda k, i: (k, i, 0)),
                  pl.BlockSpec((MERGE_BWD_T, D_MODEL), lambda k, i: (i, C_MERGE // D_MODEL + k)),
                  pl.BlockSpec((MERGE_BWD_T, D_MODEL), lambda k, i: (i, 0)),
                  pl.BlockSpec((1, 1, D_MODEL), lambda k, i: (k, 0, 0)),
                  pl.BlockSpec((1, BW, D_MODEL), lambda k, i: (k, 0, 0))],
        out_specs=[pl.BlockSpec((1, MERGE_BWD_T, BW), lambda k, i: (k, i, 0)),
                   pl.BlockSpec((MERGE_BWD_T, D_MODEL), lambda k, i: (i, k)),
                   pl.BlockSpec((1, BW, D_MODEL), lambda k, i: (k, 0, 0)),
                   pl.BlockSpec((1, 1, D_MODEL), lambda k, i: (k, 0, 0))],
        out_shape=[jax.ShapeDtypeStruct((N_BRANCH, SEQ, BW), F32), jax.ShapeDtypeStruct((SEQ, IN_PAD), BF16),
                   jax.ShapeDtypeStruct((N_BRANCH, BW, D_MODEL), BF16), jax.ShapeDtypeStruct((N_BRANCH, 1, D_MODEL), F32)],
        scratch_shapes=[pltpu.VMEM((BW, D_MODEL), F32)],
        compiler_params=_cparams(("parallel", "arbitrary")),
    )(ys, proj, dm, merge_b, w_branch)


def _adamw(glist, w, m, v, rows, name):
    nl = len(glist)
    n, r, c = glist[0].shape
    assert w.shape == (nl, r, c) and r % rows == 0
    nb = r // rows

    def body(*refs):
        g_refs = refs[:nl]
        w_ref, m_ref, v_ref, go_ref, d_ref, mo_ref, vo_ref = refs[nl:]
        for layer in range(nl):
            @pl.when(pl.program_id(0) == layer)
            def _(g_ref=g_refs[layer]):
                g = g_ref[0].astype(F32)
                for s in range(1, n):
                    g = g + g_ref[s].astype(F32)
                mn = ADAM_B1 * m_ref[0] + (1.0 - ADAM_B1) * g
                vn = ADAM_B2 * v_ref[0] + (1.0 - ADAM_B2) * jnp.square(g)
                m_hat = mn / (1.0 - ADAM_B1 ** ADAM_STEP)
                v_hat = vn / (1.0 - ADAM_B2 ** ADAM_STEP)
                go_ref[0] = g
                d_ref[0] = -ADAM_LR * (m_hat / (jnp.sqrt(v_hat) + ADAM_EPS) + ADAM_WD * w_ref[0])
                mo_ref[0] = mn
                vo_ref[0] = vn

    def g_spec(layer):
        return pl.BlockSpec((n, rows, c), lambda a, i: (0, jnp.where(a < layer, 0, jnp.where(a == layer, i, nb - 1)), 0))

    blk = pl.BlockSpec((1, rows, c), lambda a, i: (a, i, 0))
    out = jax.ShapeDtypeStruct((nl, r, c), F32)
    return pl.pallas_call(
        body, name=name, grid=(nl, nb),
        in_specs=[g_spec(layer) for layer in range(nl)] + [blk, blk, blk],
        out_specs=[blk, blk, blk, blk], out_shape=[out, out, out, out],
        compiler_params=_cparams(("arbitrary", "arbitrary")),
    )(*glist, w, m, v)


X_ROWS_PER_COL = 2 * (D_MODEL // LANES)


def _w_in_to_x(w):
    t = jnp.transpose(w, (2, 0, 1)).reshape(SHARD_IN, DEPTH, D_MODEL // LANES, LANES)
    return jnp.transpose(t, (0, 2, 1, 3)).reshape(SHARD_IN * X_ROWS_PER_COL, LANES)


def _w_in_from_x(xv):
    t = jnp.transpose(xv.reshape(SHARD_IN, D_MODEL // LANES, DEPTH, LANES), (0, 2, 1, 3))
    return jnp.transpose(t.reshape(SHARD_IN, DEPTH, D_MODEL), (1, 2, 0))


def _adamw_w_in(glist, w, m, v, name):
    n = glist[0].shape[0]
    cols = LANES
    rows = cols * X_ROWS_PER_COL

    def body(g0_ref, g1_ref, w_ref, m_ref, v_ref, go_ref, d_ref, mo_ref, vo_ref):
        for layer, g_ref in enumerate((g0_ref, g1_ref)):
            g = g_ref[0].astype(F32)
            for s in range(1, n):
                g = g + g_ref[s].astype(F32)
            gt = g.T
            for t in range(D_MODEL // LANES):
                sel = (pl.ds(2 * t + layer, cols, stride=X_ROWS_PER_COL), slice(None))
                gs = gt[:, t * LANES:(t + 1) * LANES]
                mn = ADAM_B1 * m_ref[sel] + (1.0 - ADAM_B1) * gs
                vn = ADAM_B2 * v_ref[sel] + (1.0 - ADAM_B2) * jnp.square(gs)
                m_hat = mn / (1.0 - ADAM_B1 ** ADAM_STEP)
                v_hat = vn / (1.0 - ADAM_B2 ** ADAM_STEP)
                go_ref[sel] = gs
                d_ref[sel] = -ADAM_LR * (m_hat / (jnp.sqrt(v_hat) + ADAM_EPS) + ADAM_WD * w_ref[sel])
                mo_ref[sel] = mn
                vo_ref[sel] = vn

    g_spec = pl.BlockSpec((n, D_MODEL, cols), lambda i: (0, 0, i))
    blk = pl.BlockSpec((rows, LANES), lambda i: (i, 0))
    out = jax.ShapeDtypeStruct((SHARD_IN * X_ROWS_PER_COL, LANES), F32)
    res = pl.pallas_call(
        body, name=name, grid=(-(-SHARD_IN // cols),),
        in_specs=[g_spec, g_spec, blk, blk, blk], out_specs=[blk, blk, blk, blk], out_shape=[out, out, out, out],
        compiler_params=_cparams(("parallel",)),
    )(*glist, _w_in_to_x(w), _w_in_to_x(m), _w_in_to_x(v))
    return [_w_in_from_x(o) for o in res]


def _slot_sum(gslots, name):
    n, r, c = gslots.shape

    def body(g_ref, o_ref):
        g = g_ref[0]
        for s in range(1, n):
            g = g + g_ref[s]
        o_ref[...] = g

    return pl.pallas_call(
        body, name=name, in_specs=[pl.BlockSpec((n, r, c), lambda: (0, 0, 0))],
        out_specs=pl.BlockSpec((r, c), lambda: (0, 0)), out_shape=jax.ShapeDtypeStruct((r, c), F32),
        compiler_params=_cparams(None),
    )(gslots)


def _me_and_peers():
    x, y, c = lax.axis_index("x"), lax.axis_index("y"), lax.axis_index("c")
    me = 4 * x + 2 * y + c
    peers = []
    for k in range(1, N_DEV):
        px = 1 - x if (k >> 2) & 1 else x
        py = 1 - y if (k >> 1) & 1 else y
        pc = 1 - c if k & 1 else c
        peers.append((4 * px + 2 * py + pc, (px, py, pc)))
    return me, peers


def _exchange(tensors, gather, name):
    n = len(tensors)

    def body(*refs):
        ins, outs = refs[:n], refs[n:2 * n]
        send_sems, recv_sems, local_sems = refs[2 * n:]
        me, peers = _me_and_peers()
        started = []
        for t in range(n):
            own = pltpu.make_async_copy(ins[t] if gather else ins[t].at[me], outs[t].at[me], local_sems.at[t])
            own.start()
            started.append(own)
            for k, (pidx, pos) in enumerate(peers):
                cp = pltpu.make_async_remote_copy(
                    src_ref=ins[t] if gather else ins[t].at[pidx], dst_ref=outs[t].at[me],
                    send_sem=send_sems.at[t, k], recv_sem=recv_sems.at[t, k], device_id=pos, device_id_type=MESH)
                cp.start()
                started.append(cp)
        for cp in started:
            cp.wait()

    any_spec = pl.BlockSpec(memory_space=pl.ANY)
    outs = pl.pallas_call(
        body, name=name, in_specs=[any_spec] * n, out_specs=[any_spec] * n,
        out_shape=[jax.ShapeDtypeStruct(((N_DEV,) + t.shape) if gather else t.shape, t.dtype) for t in tensors],
        scratch_shapes=[pltpu.SemaphoreType.DMA((n, N_DEV - 1)), pltpu.SemaphoreType.DMA((n, N_DEV - 1)),
                        pltpu.SemaphoreType.DMA((n,))],
        compiler_params=pltpu.CompilerParams(has_side_effects=True),
    )(*tensors)
    return list(outs)


_HBM = pl.BlockSpec(memory_space=pltpu.HBM)
_SEM = pl.BlockSpec(memory_space=pltpu.SEMAPHORE)
_EFFECT = pltpu.SideEffectType.DATAFLOW_SIDE_EFFECTING


N_CHIP = N_DEV // 2


def _chip_peers():
    x, y, c = lax.axis_index("x"), lax.axis_index("y"), lax.axis_index("c")
    chips = []
    for d in range(1, N_CHIP):
        px = 1 - x if (d >> 1) & 1 else x
        py = 1 - y if d & 1 else y
        chips.append((2 * px + py, (px, py)))
    return (x, y, c), 2 * x + y, chips


def _plan_gather(ins, lands, send_sems, recv_sems, local_sems, first=0):
    (x, y, c), q, chips = _chip_peers()
    me = 2 * q + c
    plan = dict(start=[], relay_wait=[], relay_start=[], local=[], sends=[], recvs=[])
    for t in range(len(ins)):
        base = (first + t) * 7
        sem = lambda k: dict(send_sem=send_sems.at[base + k], recv_sem=recv_sems.at[base + k], device_id_type=MESH)
        own = pltpu.make_async_copy(ins[t], lands[t].at[me], local_sems.at[first + t])
        to_sib = pltpu.make_async_remote_copy(src_ref=ins[t], dst_ref=lands[t].at[me], device_id=(x, y, 1 - c), **sem(0))
        plan['start'] += [own, to_sib]
        plan['local'].append(own)
        plan['sends'].append(to_sib)
        plan['recvs'].append(to_sib)
        for d, (pq, (px, py)) in enumerate(chips):
            to_chip = pltpu.make_async_remote_copy(src_ref=ins[t], dst_ref=lands[t].at[me], device_id=(px, py, c), **sem(1 + d))
            blk = lands[t].at[2 * pq + c]
            fwd = pltpu.make_async_remote_copy(src_ref=blk, dst_ref=blk, device_id=(x, y, 1 - c), **sem(4 + d))
            plan['start'].append(to_chip)
            plan['relay_wait'].append(to_chip)
            plan['relay_start'].append(fwd)
            plan['sends'] += [to_chip, fwd]
            plan['recvs'].append(fwd)
    return plan


def _plan_pair(ins, lands, send_sems, recv_sems, local_sems):
    (x, y, c), q, chips = _chip_peers()
    plan = dict(start=[], local=[], sends=[], recvs=[])
    for t in range(len(ins)):
        for k in range(N_CHIP):
            cp = pltpu.make_async_remote_copy(
                src_ref=ins[t].at[2 * k + 1 - c], dst_ref=lands[t].at[k], send_sem=send_sems.at[t * N_CHIP + k],
                recv_sem=recv_sems.at[t * N_CHIP + k], device_id=(x, y, 1 - c), device_id_type=MESH)
            plan['start'].append(cp)
            plan['sends'].append(cp)
            plan['recvs'].append(cp)
    return plan


def _plan_chips(ins, lands, send_sems, recv_sems, local_sems):
    (x, y, c), q, chips = _chip_peers()
    plan = dict(start=[], local=[], sends=[], recvs=[])
    for t in range(len(ins)):
        own = pltpu.make_async_copy(ins[t].at[q], lands[t].at[q], local_sems.at[t])
        plan['start'].append(own)
        plan['local'].append(own)
        for d, (pq, (px, py)) in enumerate(chips):
            cp = pltpu.make_async_remote_copy(
                src_ref=ins[t].at[pq], dst_ref=lands[t].at[q], send_sem=send_sems.at[t * 3 + d],
                recv_sem=recv_sems.at[t * 3 + d], device_id=(px, py, c), device_id_type=MESH)
            plan['start'].append(cp)
            plan['sends'].append(cp)
            plan['recvs'].append(cp)
    return plan


def _split_start(plan_fn, tensors, land_shapes, n_sems, name, after=None):
    n = len(tensors)
    extra = [] if after is None else [after]

    def body(*refs):
        ins, lands = refs[:n], refs[n:2 * n]
        plan = plan_fn(ins, lands, *refs[2 * n + len(extra):2 * n + len(extra) + 3])
        for cp in plan['start']:
            cp.start()
        refs[-1][...] = jnp.zeros_like(refs[-1])

    outs = pl.pallas_call(
        body, name=name,
        out_shape=(pltpu.SemaphoreType.DMA((n_sems,)), pltpu.SemaphoreType.DMA((n_sems,)), pltpu.SemaphoreType.DMA((n,)),
                   *[pltpu.HBM(t.shape, t.dtype) for t in tensors],
                   *[pltpu.HBM(s, t.dtype) for s, t in zip(land_shapes, tensors)],
                   jax.ShapeDtypeStruct((8, LANES), F32)),
        in_specs=[_HBM] * (2 * n) + [pl.BlockSpec(memory_space=pl.ANY)] * len(extra),
        out_specs=(_SEM, _SEM, _SEM, *[_HBM] * (2 * n), pl.BlockSpec(memory_space=pltpu.VMEM)),
        input_output_aliases={t: 3 + t for t in range(2 * n)},
        compiler_params=pltpu.CompilerParams(has_side_effects=_EFFECT),
    )(*[pltpu.with_memory_space_constraint(t, pltpu.HBM) for t in tensors],
      *[pltpu.with_memory_space_constraint(lax.empty(s, t.dtype), pltpu.HBM) for s, t in zip(land_shapes, tensors)], *extra)
    return outs[:-1], outs[-1]


def _split_relay(plan_fn, state, after, name):
    sems, thru = state[:3], state[3:]
    n = len(thru) // 2

    def arrived(*refs):
        plan = plan_fn(refs[:n], refs[n:2 * n], *refs[2 * n:2 * n + 3])
        for cp in plan['relay_wait']:
            cp.wait_recv()

    thru = pl.pallas_call(
        arrived, name=name + "_arrived",
        out_shape=tuple(pltpu.HBM(t.shape, t.dtype) for t in thru),
        in_specs=[_HBM] * (2 * n) + [_SEM, _SEM, _SEM, pl.BlockSpec(memory_space=pl.ANY)],
        out_specs=tuple([_HBM] * (2 * n)),
        input_output_aliases={t: t for t in range(2 * n)},
        compiler_params=pltpu.CompilerParams(has_side_effects=_EFFECT),
    )(*thru, *sems, after)

    def forward(*refs):
        plan = plan_fn(refs[:n], refs[n:2 * n], *refs[2 * n:2 * n + 3])
        for cp in plan['relay_start']:
            cp.start()
        refs[-1][...] = jnp.zeros_like(refs[-1])

    outs = pl.pallas_call(
        forward, name=name + "_forward",
        out_shape=(*[pltpu.HBM(t.shape, t.dtype) for t in thru], jax.ShapeDtypeStruct((8, LANES), F32)),
        in_specs=[_HBM] * (2 * n) + [_SEM, _SEM, _SEM],
        out_specs=(*[_HBM] * (2 * n), pl.BlockSpec(memory_space=pltpu.VMEM)),
        input_output_aliases={t: t for t in range(2 * n)},
        compiler_params=pltpu.CompilerParams(has_side_effects=_EFFECT),
    )(*thru, *sems)
    return (*sems, *outs[:-1]), outs[-1]


def _split_wait(plan_fn, state, after, name, with_sources=False):
    sems, thru = state[:3], state[3:]
    n = len(thru) // 2

    def body(*refs):
        plan = plan_fn(refs[:n], refs[n:2 * n], *refs[2 * n:2 * n + 3])
        for cp in plan['local']:
            cp.wait()
        for cp in plan['sends']:
            cp.wait_send()
        for cp in plan['recvs']:
            cp.wait_recv()

    outs = pl.pallas_call(
        body, name=name,
        out_shape=tuple(pltpu.HBM(t.shape, t.dtype) for t in thru),
        in_specs=[_HBM] * (2 * n) + [_SEM, _SEM, _SEM, pl.BlockSpec(memory_space=pl.ANY)],
        out_specs=tuple([_HBM] * (2 * n)),
        input_output_aliases={t: t for t in range(2 * n)},
        compiler_params=pltpu.CompilerParams(has_side_effects=_EFFECT),
    )(*thru, *sems, after)
    return (list(outs[:n]), list(outs[n:])) if with_sources else list(outs[n:])


PAIR_SUM_BLOCK = 512 * 1024


def _pair_sum(mine, theirs, name):
    _, r, c = mine.shape
    rows = r
    while rows * c > PAIR_SUM_BLOCK and rows % 32 == 0:
        rows //= 2

    def body(core_ref, a_ref, b_ref, o_ref):
        o_ref[0] = (a_ref[0].astype(F32) + b_ref[0].astype(F32)).astype(o_ref.dtype)

    return pl.pallas_call(
        body, name=name,
        grid_spec=pltpu.PrefetchScalarGridSpec(
            num_scalar_prefetch=1, grid=(N_CHIP, r // rows),
            in_specs=[pl.BlockSpec((1, rows, c), lambda k, i, core: (2 * k + core[0], i, 0)),
                      pl.BlockSpec((1, rows, c), lambda k, i, core: (k, i, 0))],
            out_specs=pl.BlockSpec((1, rows, c), lambda k, i, core: (k, i, 0))),
        out_shape=jax.ShapeDtypeStruct((N_CHIP, r, c), mine.dtype),
        compiler_params=_cparams(("parallel", "parallel")),
    )(lax.axis_index("c").astype(jnp.int32).reshape(1), mine, theirs)


WEIGHTS = ['norm_w', 'w_in', 's5_lambda_re', 's5_lambda_im', 's5_b_re', 's5_b_im', 's5_c_re', 's5_c_im', 's5_d',
           's5_log_step', 's5_w_glu', 'sgu_ln_w', 'sgu_ln_b', 'sgu_w', 'sgu_b', 'm2_conv_w', 'm2_conv_b', 'm2_dt_bias',
           'm2_a_log', 'm2_d', 'm2_norm_w', 'sc_conv_w', 'merge_b', 'w_branch', 'w_out', 'final_norm_w']
BIG_SHARDED = ['w_in', 'w_branch', 'w_out', 's5_w_glu']
SMALL_SHARDED = ['m2_conv_w', 'sc_conv_w', 'merge_b']
REPLICATED = [n for n in WEIGHTS if n not in BIG_SHARDED + SMALL_SHARDED]
S5_NAMES = ['s5_lambda_re', 's5_lambda_im', 's5_b_re', 's5_b_im', 's5_c_re', 's5_c_im', 's5_d', 's5_log_step']


def _sc_interleave(t):
    lead = t.shape[:-1]
    return jnp.swapaxes(t.reshape(lead + (4, 4, LANES)), -3, -2).reshape(lead + (4 * BW,))


def _pad_in(w):
    z = lambda n: jnp.zeros(w.shape[:-1] + (n,), w.dtype)
    return jnp.concatenate([w[..., 6152:], w[..., 0:1024], w[..., 3072:4096], w[..., 1024:2560], z(512),
                            w[..., 2560:3072], w[..., 4096:4104], z(504), _sc_interleave(w[..., 4104:6152])], axis=-1)


def _unpad_in(g):
    return jnp.concatenate([g[..., C_S5U:C_S5U + 1024], g[..., C_SGU_U:C_SGU_U + 1536], g[..., C_M2Z:C_M2Z + 512],
                            g[..., C_M2X:C_M2X + 1024], g[..., C_DT:C_DT + 8], _sc_interleave(g[..., C_SC:]),
                            g[..., :N_BRANCH * D_MODEL]], axis=-1)


ROW_BLOCK = 8 * LANES


def _pack_rows(tensors, row_mult, batched=False):
    parts = []
    for t in tensors:
        f = t.reshape((t.shape[0], -1) if batched else (1, -1))
        f = jnp.pad(f, ((0, 0), (0, (-f.shape[1]) % ROW_BLOCK)))
        parts.append(f.reshape(f.shape[0], -1, LANES))
    out = jnp.concatenate(parts, axis=1)
    out = jnp.pad(out, ((0, 0), (0, (-out.shape[1]) % row_mult), (0, 0)))
    return out if batched else out[0]


def _unpack_rows(rows, shapes):
    out, r0 = [], 0
    for shp in shapes:
        size = 1
        for s in shp:
            size *= s
        nr = -(-size // ROW_BLOCK) * 8
        out.append(rows[r0:r0 + nr].reshape(-1)[:size].reshape(shp))
        r0 += nr
    return out


def _kernel_col_map():
    m = np.full(IN_PAD, -1, np.int64)
    m[C_MERGE:C_MERGE + 4096] = np.arange(6152, 10248)
    m[C_S5U:C_S5U + 1024] = np.arange(0, 1024)
    m[C_M2X:C_M2X + 1024] = np.arange(3072, 4096)
    m[C_SGU_U:C_SGU_U + 1536] = np.arange(1024, 2560)
    m[C_M2Z:C_M2Z + 512] = np.arange(2560, 3072)
    m[C_DT:C_DT + 8] = np.arange(4096, 4104)
    for j in range(4):
        for kind in range(4):
            k0 = C_SC + 4 * LANES * j + LANES * kind
            m[k0:k0 + LANES] = 4104 + BW * kind + LANES * j + np.arange(LANES)
    return m


def _lane_pieces(sources):
    pieces, cur = [], None
    for lane, src in enumerate(sources):
        key = None if src is None else (src[0], src[1] // LANES, (lane - src[1]) % LANES)
        if cur is not None and key == cur[0]:
            cur[2] = lane + 1
        else:
            if cur is not None and cur[0] is not None:
                pieces.append((*cur[0], cur[1], cur[2]))
            cur = [key, lane, lane + 1]
    if cur is not None and cur[0] is not None:
        pieces.append((*cur[0], cur[1], cur[2]))
    return pieces


def _assemble_block(pieces, load, rows, dtype):
    lane = lax.broadcasted_iota(jnp.int32, (rows, LANES), 1)
    out = None
    for arr, sb, shift, lo, hi in pieces:
        v = load(arr, sb)
        if shift:
            v = pltpu.roll(v, shift, 1)
        if out is None and lo == 0 and hi == LANES:
            out = v
        else:
            out = jnp.where((lane >= lo) & (lane < hi), v, jnp.zeros((rows, LANES), dtype) if out is None else out)
    return jnp.zeros((rows, LANES), dtype) if out is None else out


RELAYOUT_ROWS = 256
SHARD_BLOCKS = -(-SHARD_IN // LANES)


def _load_shard_block(ref, rows):
    def load(j, sb):
        if sb == SHARD_BLOCKS - 1:
            return jnp.broadcast_to(ref[j, :, SHARD_IN - 1:SHARD_IN], (rows, LANES))
        return ref[j, :, sb * LANES:(sb + 1) * LANES]
    return load


def _relayout_w_in(gathered, name):
    kmap = _kernel_col_map()
    dtype = gathered.dtype

    def body(src_ref, o_ref):
        load = _load_shard_block(src_ref, RELAYOUT_ROWS)
        for ob in range(IN_PAD // LANES):
            srcs = [None if kmap[ob * LANES + l] < 0 else (int(kmap[ob * LANES + l]) // SHARD_IN, int(kmap[ob * LANES + l]) % SHARD_IN)
                    for l in range(LANES)]
            o_ref[:, ob * LANES:(ob + 1) * LANES] = _assemble_block(_lane_pieces(srcs), load, RELAYOUT_ROWS, dtype)

    return pl.pallas_call(
        body, name=name, grid=(D_MODEL // RELAYOUT_ROWS,),
        in_specs=[pl.BlockSpec((N_DEV, RELAYOUT_ROWS, SHARD_IN), lambda i: (0, i, 0))],
        out_specs=pl.BlockSpec((RELAYOUT_ROWS, IN_PAD), lambda i: (i, 0)),
        out_shape=jax.ShapeDtypeStruct((D_MODEL, IN_PAD), dtype),
        compiler_params=_cparams(("parallel",)),
    )(gathered)


def _relayout_g_in(gw, name):
    kmap = _kernel_col_map()
    kinv = np.zeros(IN_DIM, np.int64)
    kinv[kmap[kmap >= 0]] = np.nonzero(kmap >= 0)[0]
    dtype = gw.dtype

    def body(src_ref, o_ref):
        load = lambda _, sb: src_ref[:, sb * LANES:(sb + 1) * LANES]
        for j in range(N_DEV):
            for ob in range(SHARD_BLOCKS):
                srcs = [(0, int(kinv[SHARD_IN * j + ob * LANES + l])) if ob * LANES + l < SHARD_IN else None for l in range(LANES)]
                blk = _assemble_block(_lane_pieces(srcs), load, RELAYOUT_ROWS, dtype)
                if ob == SHARD_BLOCKS - 1:
                    o_ref[j, :, SHARD_IN - 1:SHARD_IN] = blk[:, 0:1]
                else:
                    o_ref[j, :, ob * LANES:(ob + 1) * LANES] = blk

    return pl.pallas_call(
        body, name=name, grid=(D_MODEL // RELAYOUT_ROWS,),
        in_specs=[pl.BlockSpec((RELAYOUT_ROWS, IN_PAD), lambda i: (i, 0))],
        out_specs=pl.BlockSpec((N_DEV, RELAYOUT_ROWS, SHARD_IN), lambda i: (0, i, 0)),
        out_shape=jax.ShapeDtypeStruct((N_DEV, D_MODEL, SHARD_IN), dtype),
        compiler_params=_cparams(("parallel",)),
    )(gw)


def _rows128(flat, row_mult=8):
    n = flat.shape[0]
    per = LANES * row_mult
    total = -(-n // per) * per
    return jnp.pad(flat, (0, total - n)).reshape(total // LANES, LANES)


def _pad_lanes(v):
    return jnp.pad(v, (0, LANES - v.shape[0])).reshape(1, LANES)


def _layer_prep(i, p):
    disc, disc_vjp = jax.vjp(_s5_disc, *[p[n][i] for n in S5_NAMES])
    prep = dict(
        nw=p['norm_w'][i].reshape(1, D_MODEL), disc_vjp=disc_vjp,
        s5small=[t.astype(BF16) for t in disc[:4]] + [disc[4], disc[5]],
        sgw=[p['sgu_ln_w'][i].reshape(1, BW), p['sgu_ln_b'][i].reshape(1, BW), p['sgu_w'][i],
             jnp.repeat(p['sgu_b'][i].T, BW // SGU_HEADS, axis=1)],
        cb=p['m2_conv_b'][i].reshape(1, M2_CONV_CH),
        m2w=[_pad_lanes(p['m2_dt_bias'][i]), _pad_lanes(p['m2_a_log'][i]),
             jnp.repeat(p['m2_d'][i], M2_HEAD_DIM).reshape(1, BW), p['m2_norm_w'][i].reshape(1, BW)])
    touch = [t[0, 0].astype(F32) for t in prep['s5small']] + [prep['sgw'][3][0, 0], prep['m2w'][2][0, 0]]
    return prep, sum(touch[1:], touch[0])


def _layer_fwd(x, h, i, prep, w_in, other_weights, before_merge=None):
    proj = _matmul(h, w_in, 1, 0, F32, 1024, 1024, 1024, f"proj{i}")
    full = dict(other_weights(proj), w_in=w_in)
    s5w = prep['s5small'] + [full['s5_w_glu']]
    ya, sre, sim = _s5_fwd(proj, *s5w, f"s5_fwd{i}")
    yb = _sgu_fwd(proj, *prep['sgw'], f"sgu_fwd{i}")
    cw = full['m2_conv_w']
    xa = _m2_conv_fwd(proj, cw, prep['cb'], f"m2conv_fwd{i}")
    yc, s_in = _ssd_fwd(proj, xa, *prep['m2w'], f"ssd_fwd{i}")
    scw = full['sc_conv_w']
    yd = _sc_fwd(proj, scw, f"sc_fwd{i}")
    ys = jnp.stack([ya, yb, yc, yd])
    mb = full['merge_b'].reshape(N_BRANCH, 1, D_MODEL)
    if before_merge is not None:
        mb = mb + before_merge(ys)[0, 0]
    merged = _merge_fwd(proj, ys, mb, full['w_branch'], f"merge_fwd{i}")
    x_new = _matmul(merged, full['w_out'], 1, 0, F32, 1024, 1024, 1024, f"out{i}", residual=x)
    saved = dict(x=x, nw=prep['nw'], h=h, proj=proj, disc_vjp=prep['disc_vjp'], s5w=s5w, sre=sre, sim=sim, sgw=prep['sgw'],
                 cw=cw, cb=prep['cb'], xa=xa, m2w=prep['m2w'], s_in=s_in, scw=scw, ys=ys, mb=mb, merged=merged)
    return x_new, saved, full


def _layer_bwd(dx_out, i, sv, full, on_large_grads=None, after_dh=None):
    g = {}
    proj = sv['proj']
    dm = _matmul(dx_out, full['w_out'], 1, 1, F32, 1024, 1024, 1024, f"dmerged{i}")
    g['w_out'] = _matmul(sv['merged'], dx_out, 0, 0, BF16, 1024, 1024, 1024, f"gw_out{i}")
    dys, dproj, g['w_branch'], dmb = _merge_bwd(proj, sv['ys'], dm, sv['mb'], full['w_branch'], f"merge_bwd{i}")
    g['merge_b'] = dmb.reshape(N_BRANCH, D_MODEL)
    dproj, dbbre, dbbim, dcre, dcim, da, dd, dwg = _s5_bwd(proj, dproj, dys, sv['sre'], sv['sim'], *sv['s5w'], f"s5_bwd{i}")
    for n, t in zip(S5_NAMES, sv['disc_vjp']((dbbre, dbbim, dcre, dcim, da, dd))):
        g[n] = t
    g['s5_w_glu'] = dwg.astype(BF16)
    dproj, dlw, dlb, g['sgu_w'], dbias = _sgu_bwd(proj, dproj, dys, *sv['sgw'], f"sgu_bwd{i}")
    g['sgu_ln_w'], g['sgu_ln_b'] = dlw[0], dlb[0]
    g['sgu_b'] = dbias.reshape(SGU_CHUNK, SGU_HEADS, BW // SGU_HEADS).sum(-1).T
    dproj, dxa, ddtb, dal, ddf, dnw = _ssd_bwd(proj, dproj, sv['xa'], dys, sv['s_in'], *sv['m2w'], f"ssd_bwd{i}")
    dproj, g['m2_conv_w'], dcb = _m2_conv_bwd(proj, dproj, dxa, sv['cw'], sv['cb'], f"m2conv_bwd{i}")
    g['m2_conv_b'], g['m2_norm_w'] = dcb[0], dnw[0]
    g['m2_dt_bias'], g['m2_a_log'] = ddtb[0, :M2_HEADS], dal[0, :M2_HEADS]
    g['m2_d'] = ddf.reshape(M2_HEADS, M2_HEAD_DIM).sum(-1)
    dproj, g['sc_conv_w'] = _sc_bwd(proj, dproj, dys, sv['scw'], f"sc_bwd{i}")
    g['w_in'] = _matmul(sv['h'], dproj, 0, 0, BF16, 1024, 1024, 1024, f"gw_in{i}")
    tok = on_large_grads(g) if on_large_grads else None
    dh = _matmul(dproj, full['w_in'], 1, 1, F32, 1024, 1024, 1024, f"dh{i}", after=tok)
    nw = sv['nw'] if after_dh is None else sv['nw'] + after_dh(dh)[0, 0]
    dx_in, dnw_l = _rmsnorm_bwd(sv['x'], nw, dh, dx_out, f"rms_bwd{i}")
    g['norm_w'] = dnw_l[0]
    return dx_in, g


def _split8(t, axis):
    shp = t.shape
    t = t.reshape(shp[:axis] + (N_DEV, shp[axis] // N_DEV) + shp[axis + 1:])
    return jnp.moveaxis(t, axis, 0)


def _join8(t, axis):
    t = jnp.moveaxis(t, 0, axis)
    shp = t.shape
    return t.reshape(shp[:axis] + (shp[axis] * shp[axis + 1],) + shp[axis + 2:])


SHARD_AXIS = {'w_in': 2, 'w_branch': 3, 'w_out': 1, 's5_w_glu': 1, 'm2_conv_w': 2, 'sc_conv_w': 2, 'merge_b': 2}


OTHER_BIG = [n for n in BIG_SHARDED if n != 'w_in']


def _other_weights(gathered):
    return {n: _join8(t, SHARD_AXIS[n] - 1) for n, t in zip(OTHER_BIG, gathered)}


def _layer_grad_blocks(g, i):
    blocks = [_relayout_g_in(g[n], f"relayout_g_in{i}") if n == 'w_in' else _split8(g[n], SHARD_AXIS[n] - 1) for n in BIG_SHARDED]
    return [b.reshape(N_DEV, -1, b.shape[-1]) for b in blocks]


def _pair_start(blocks, i):
    shapes = [(N_CHIP,) + b.shape[1:] for b in blocks]
    return _split_start(_plan_pair, blocks, shapes, N_CHIP * len(blocks), f"pair{i}_start")


def _pair_sums(state, after, i):
    mine, theirs = _split_wait(_plan_pair, state, after, f"pair{i}_wait", with_sources=True)
    return [_pair_sum(b, t, f"pair_sum{i}_{k}") for k, (b, t) in enumerate(zip(mine, theirs))]


def _chips_start(sums, i, after=None):
    return _split_start(_plan_chips, sums, [s.shape for s in sums], 3 * len(sums), f"chips{i}_start", after)


def kernel(x, norm_w, w_in, s5_lambda_re, s5_lambda_im, s5_b_re, s5_b_im, s5_c_re, s5_c_im, s5_d, s5_log_step, s5_w_glu, sgu_ln_w, sgu_ln_b, sgu_w, sgu_b, m2_conv_w, m2_conv_b, m2_dt_bias, m2_a_log, m2_d, m2_norm_w, sc_conv_w, merge_b, w_branch, w_out, final_norm_w, loss_target, m_norm_w, m_w_in, m_s5_lambda_re, m_s5_lambda_im, m_s5_b_re, m_s5_b_im, m_s5_c_re, m_s5_c_im, m_s5_d, m_s5_log_step, m_s5_w_glu, m_sgu_ln_w, m_sgu_ln_b, m_sgu_w, m_sgu_b, m_m2_conv_w, m_m2_conv_b, m_m2_dt_bias, m_m2_a_log, m_m2_d, m_m2_norm_w, m_sc_conv_w, m_merge_b, m_w_branch, m_w_out, m_final_norm_w, v_norm_w, v_w_in, v_s5_lambda_re, v_s5_lambda_im, v_s5_b_re, v_s5_b_im, v_s5_c_re, v_s5_c_im, v_s5_d, v_s5_log_step, v_s5_w_glu, v_sgu_ln_w, v_sgu_ln_b, v_sgu_w, v_sgu_b, v_m2_conv_w, v_m2_conv_b, v_m2_dt_bias, v_m2_a_log, v_m2_d, v_m2_norm_w, v_sc_conv_w, v_merge_b, v_w_branch, v_w_out, v_final_norm_w):
    loc = locals()
    p = {n: loc[n] for n in WEIGHTS}
    mom = {n: loc['m_' + n] for n in WEIGHTS}
    vel = {n: loc['v_' + n] for n in WEIGHTS}

    small_sizes = [p[n].size for n in SMALL_SHARDED]
    small_pack = _rows128(jnp.concatenate([p[n].reshape(-1) for n in SMALL_SHARDED]))
    shards = ([p['w_in'][0].astype(BF16)] + [p[n][0].astype(BF16) for n in OTHER_BIG] + [small_pack]
              + [p[n][1].astype(BF16) for n in BIG_SHARDED])
    gath, tok = _split_start(_plan_gather, shards, [(N_DEV,) + t.shape for t in shards], 7 * len(shards), "gather_start")
    sems, srcs, lands = gath[:3], gath[3:3 + len(shards)], gath[3 + len(shards):]

    def relayed(lo, hi, after, name):
        plan = functools.partial(_plan_gather, first=lo)
        state, tok = _split_relay(plan, (*sems, *srcs[lo:hi], *lands[lo:hi]), after, name + "_relay")
        return (plan, state, name), tok

    def arrived(relay, after):
        plan, state, name = relay
        return _split_wait(plan, state, after, name + "_wait")

    def gathered(lo, hi, after, name):
        relay, tok = relayed(lo, hi, after, name)
        return arrived(relay, tok)

    later = dict(p, **{n: p[n] + tok[0, 0] for n in ('norm_w', 's5_log_step', 'sgu_b', 'm2_d')})
    preps = [_layer_prep(i, later) for i in range(DEPTH)]
    h0 = _rmsnorm_fwd(x[0], preps[0][0]['nw'], "rms_fwd0")
    got = gathered(0, 1, tok + (preps[0][1] + preps[1][1] + h0[0, 0].astype(F32)), "gather_w_in0")
    small_full = {}

    def other_weights0(proj):
        got = gathered(1, 5, proj, "gather_rest0")
        small_all, off = got[-1].reshape(N_DEV, -1), 0
        for n, sz in zip(SMALL_SHARDED, small_sizes):
            small_full[n] = _join8(small_all[:, off:off + sz].reshape((N_DEV,) + p[n].shape), SHARD_AXIS[n])
            off += sz
        return dict(_other_weights(got[:-1]), **{n: small_full[n][0] for n in SMALL_SHARDED})

    saved, layer_g, full = [None] * DEPTH, [None] * DEPTH, [None] * DEPTH
    relay1 = []

    def relay_layer1(ys):
        relay, tok = relayed(5, 9, ys, "gather1")
        relay1.append(relay)
        return tok

    xs, saved[0], full[0] = _layer_fwd(x[0], h0, 0, preps[0][0], _relayout_w_in(got[0], "relayout_w_in0"), other_weights0,
                                       relay_layer1)
    h1 = _rmsnorm_fwd(xs, preps[1][0]['nw'], "rms_fwd1")
    got = arrived(relay1[0], h1)
    xs, saved[1], full[1] = _layer_fwd(
        xs, h1, 1, preps[1][0], _relayout_w_in(got[0], "relayout_w_in1"),
        lambda proj: dict(_other_weights(got[1:]), **{n: small_full[n][1] for n in SMALL_SHARDED}))
    loss_row, dx, dfw = _loss_head(xs, final_norm_w.reshape(1, D_MODEL), loss_target[0])
    loss = lax.psum(loss_row[0, 0], ("x", "y", "c"))
    pairs, scat = [None] * DEPTH, [None] * DEPTH

    def start_pairs(i):
        def start(g):
            pairs[i], tok = _pair_start(_layer_grad_blocks(g, i), i)
            return tok
        return start

    def send_chip_sums1(dh):
        scat[1], tok = _chips_start(_pair_sums(pairs[1], dh, 1), 1)
        return tok

    dx, layer_g[1] = _layer_bwd(dx, 1, saved[1], full[1], start_pairs(1), send_chip_sums1)
    dx, layer_g[0] = _layer_bwd(dx, 0, saved[0], full[0], start_pairs(0))
    sums0 = _pair_sums(pairs[0], dx, 0)
    grads = {n: jnp.stack([layer_g[i][n] for i in range(DEPTH)]) for n in SMALL_SHARDED + REPLICATED if n != 'final_norm_w'}
    grads['final_norm_w'] = dfw[0]

    out_g, out_d, out_m, out_v = {}, {}, {}, {}
    repl_rows = _pack_rows([grads[n] for n in REPLICATED], 8 * N_DEV)
    rr = repl_rows.shape[0] // N_DEV
    shard_rows = _pack_rows([_split8(grads[n], SHARD_AXIS[n]) for n in SMALL_SHARDED], 8, batched=True)
    rs = shard_rows.shape[1]
    small_g = jnp.concatenate([shard_rows, repl_rows.reshape(N_DEV, rr, LANES)], axis=1)
    small_sum = _slot_sum(_exchange([small_g], False, "scatter_small")[0], "sum_small")
    repl_all = _exchange([small_sum[rs:]], True, "gather_small")[0].reshape(N_DEV * rr, LANES)
    g_all = jnp.concatenate([small_sum[:rs], repl_all], axis=0)
    scat[0], tok = _chips_start(sums0, 0, after=g_all)
    names = SMALL_SHARDED + REPLICATED
    packed = [jnp.concatenate([_pack_rows([d[n] for n in SMALL_SHARDED], 8), _pack_rows([d[n] for n in REPLICATED], 8 * N_DEV)],
                              axis=0) for d in (p, mom, vel)]
    res = _adamw([g_all[None]], packed[0][None] + tok[0, 0], packed[1][None], packed[2][None], g_all.shape[0], "adamw_small")
    for o, dst in zip(res, (out_g, out_d, out_m, out_v)):
        pieces = (_unpack_rows(o[0, :rs], [p[n].shape for n in SMALL_SHARDED])
                  + _unpack_rows(o[0, rs:], [p[n].shape for n in REPLICATED]))
        dst.update(zip(names, pieces))

    landed1 = _split_wait(_plan_chips, scat[1], res[0], "chips1_wait")
    landed0 = _split_wait(_plan_chips, scat[0], landed1[0], "chips0_wait")
    for k, n in enumerate(BIG_SHARDED):
        shp = p[n].shape
        c = shp[-1]
        r = p[n].size // (DEPTH * c)
        if n == 'w_in':
            big = _adamw_w_in([landed0[k], landed1[k]], p[n], mom[n], vel[n], "adamw_w_in")
        else:
            big = _adamw([landed0[k], landed1[k]], *[d[n].reshape(DEPTH, r, c) for d in (p, mom, vel)],
                         {'w_branch': 512, 'w_out': 128, 's5_w_glu': 64}[n], "adamw_" + n)
        out_g[n], out_d[n], out_m[n], out_v[n] = [o.reshape(shp) for o in big]
    return (loss, dx[None], *[out_g[n] for n in WEIGHTS], *[out_d[n] for n in WEIGHTS],
            *[out_m[n] for n in WEIGHTS], *[out_v[n] for n in WEIGHTS])
```

```python
import functools

import jax
import jax.numpy as jnp
import numpy as np
from jax import lax
from jax.experimental import pallas as pl
from jax.experimental.pallas import tpu as pltpu

F32 = jnp.float32
BF16 = jnp.bfloat16

N_DEV = 8
SEQ = 2048
D_MODEL = 1024
DEPTH = 2
BW = 512
N_BRANCH = 4
EPS = 1e-6
S5_GROUPS, S5_STATE, S5_P = 32, 64, 16
S5_CH = S5_GROUPS * S5_STATE
SGU_CHUNK, SGU_HEADS = 128, 8
M2_HEADS, M2_HEAD_DIM, M2_STATE, M2_CHUNK, M2_CONV = 8, 64, 128, 128, 4
M2_CONV_CH = 1024
SC_CONV = 3
IN_DIM = 10248
IN_PAD = 11264
C_MERGE = 0
C_S5U, C_S5G = 4096, 4608
C_M2X = 5120
C_SGU_U, C_SGU_V, C_SGU_G = 6144, 6656, 7168
C_M2Z, C_DT = 8192, 8704
C_SC = 9216
SHARD_IN = IN_DIM // N_DEV

ADAM_LR, ADAM_B1, ADAM_B2, ADAM_EPS, ADAM_WD, ADAM_STEP = 0.001, 0.9, 0.999, 1e-08, 0.01, 10

VMEM_LIMIT = 56 * 1024 * 1024
LANES = 128

MESH = pl.DeviceIdType.MESH


def _cparams(sem=None, **kw):
    return pltpu.CompilerParams(dimension_semantics=sem, vmem_limit_bytes=VMEM_LIMIT, **kw)


def _dg(a, b, ca, cb, precision=None):
    return lax.dot_general(a, b, (((ca,), (cb,)), ((), ())), precision=precision,
                           preferred_element_type=F32)


@functools.partial(jax.custom_vjp, nondiff_argnums=(2, 3))
def _bdot(a, b, ca, cb):
    return _dg(a.astype(BF16), b.astype(BF16), ca, cb)


def _bdot_fwd(a, b, ca, cb):
    return _bdot(a, b, ca, cb), (a, b)


def _bdot_bwd(ca, cb, res, g):
    a, b = res
    gb, ab, bb = g.astype(BF16), a.astype(BF16), b.astype(BF16)
    da = _dg(gb, bb, 1, 1 - cb) if ca == 1 else _dg(bb, gb, 1 - cb, 1)
    db = _dg(ab, gb, 1 - ca, 0) if cb == 0 else _dg(gb, ab, 0, 1 - ca)
    return da.astype(a.dtype), db.astype(b.dtype)


_bdot.defvjp(_bdot_fwd, _bdot_bwd)


def _rms(x, w):
    return x * lax.rsqrt(jnp.mean(x * x, axis=-1, keepdims=True) + EPS) * w


def _silu(x):
    return x * jax.nn.sigmoid(x)


def _gelu(x):
    return 0.5 * x * (1.0 + jnp.tanh(0.7978845608028654 * (x + 0.044715 * (x * x * x))))


def _softplus(x):
    return jnp.maximum(x, 0.0) + jnp.log1p(jnp.exp(-jnp.abs(x)))


def _shift_down(x, s):
    if s == 0:
        return x
    row = lax.broadcasted_iota(jnp.int32, x.shape, 0)
    return jnp.where(row >= s, pltpu.roll(x, s, 0), 0.0)


def _shift_up(x, s):
    if s == 0:
        return x
    n = x.shape[0]
    row = lax.broadcasted_iota(jnp.int32, x.shape, 0)
    return jnp.where(row < n - s, pltpu.roll(x, n - s, 0), 0.0)


def _matmul(a, b, ca, cb, out_dtype, tm, tn, tk, name, residual=None, after=None):
    m = a.shape[1 - ca]
    k = a.shape[ca]
    n = b.shape[1 - cb]
    assert b.shape[cb] == k and m % tm == 0 and n % tn == 0 and k % tk == 0
    nk = k // tk
    a_spec = pl.BlockSpec((tm, tk), lambda i, j, kk: (i, kk)) if ca == 1 else pl.BlockSpec((tk, tm), lambda i, j, kk: (kk, i))
    b_spec = pl.BlockSpec((tk, tn), lambda i, j, kk: (kk, j)) if cb == 0 else pl.BlockSpec((tn, tk), lambda i, j, kk: (j, kk))
    o_spec = pl.BlockSpec((tm, tn), lambda i, j, kk: (i, j))
    has_res = residual is not None

    def body(*refs):
        refs = refs[:2 + has_res] + refs[2 + has_res + (after is not None):]
        if has_res:
            a_ref, b_ref, r_ref, o_ref, acc = refs
        else:
            a_ref, b_ref, o_ref, acc = refs
        kk = pl.program_id(2)
        part = _dg(a_ref[...].astype(BF16), b_ref[...].astype(BF16), ca, cb)

        @pl.when(kk == 0)
        def _():
            acc[...] = part

        @pl.when(kk > 0)
        def _():
            acc[...] += part

        @pl.when(kk == nk - 1)
        def _():
            r = acc[...]
            if has_res:
                r = r + r_ref[...]
            o_ref[...] = r.astype(out_dtype)

    ins = [a, b] + ([residual] if has_res else []) + ([after] if after is not None else [])
    specs = [a_spec, b_spec] + ([o_spec] if has_res else []) + ([pl.BlockSpec(memory_space=pl.ANY)] if after is not None else [])
    return pl.pallas_call(
        body, name=name, grid=(m // tm, n // tn, nk), in_specs=specs, out_specs=o_spec,
        out_shape=jax.ShapeDtypeStruct((m, n), out_dtype),
        scratch_shapes=[pltpu.VMEM((tm, tn), F32)],
        compiler_params=_cparams(("parallel", "parallel", "arbitrary")),
    )(*ins)


ROW_TILE = 512


def _rmsnorm_fwd(x, w, name):
    def body(x_ref, w_ref, o_ref):
        o_ref[...] = _rms(x_ref[...], w_ref[...]).astype(BF16)

    return pl.pallas_call(
        body, name=name, grid=(SEQ // ROW_TILE,),
        in_specs=[pl.BlockSpec((ROW_TILE, D_MODEL), lambda i: (i, 0)), pl.BlockSpec((1, D_MODEL), lambda i: (0, 0))],
        out_specs=pl.BlockSpec((ROW_TILE, D_MODEL), lambda i: (i, 0)),
        out_shape=jax.ShapeDtypeStruct((SEQ, D_MODEL), BF16),
        compiler_params=_cparams(("parallel",)),
    )(x, w)


def _rmsnorm_bwd(x, w, dh, dres, name):
    def body(x_ref, w_ref, dh_ref, dres_ref, dx_ref, dw_ref):
        _, vjp = jax.vjp(_rms, x_ref[...], w_ref[...])
        dx, dw = vjp(dh_ref[...])
        dx_ref[...] = dx + dres_ref[...]

        @pl.when(pl.program_id(0) == 0)
        def _():
            dw_ref[...] = dw

        @pl.when(pl.program_id(0) > 0)
        def _():
            dw_ref[...] += dw

    tile = pl.BlockSpec((ROW_TILE, D_MODEL), lambda i: (i, 0))
    vec = pl.BlockSpec((1, D_MODEL), lambda i: (0, 0))
    return pl.pallas_call(
        body, name=name, grid=(SEQ // ROW_TILE,),
        in_specs=[tile, vec, tile, tile], out_specs=[tile, vec],
        out_shape=[jax.ShapeDtypeStruct((SEQ, D_MODEL), F32), jax.ShapeDtypeStruct((1, D_MODEL), F32)],
        compiler_params=_cparams(("arbitrary",)),
    )(x, w, dh, dres)


def _loss_head(x, w, target):
    def body(x_ref, w_ref, t_ref, loss_ref, dx_ref, dw_ref):
        tgt = t_ref[...]

        def f(xv, wv):
            err = _rms(xv, wv) - tgt
            return 0.5 * jnp.sum(jnp.mean(err * err, axis=-1))

        loss, vjp = jax.vjp(f, x_ref[...], w_ref[...])
        dx, dw = vjp(jnp.ones((), F32))
        dx_ref[...] = dx
        lrow = jnp.full((1, LANES), loss, F32)

        @pl.when(pl.program_id(0) == 0)
        def _():
            dw_ref[...] = dw
            loss_ref[...] = lrow

        @pl.when(pl.program_id(0) > 0)
        def _():
            dw_ref[...] += dw
            loss_ref[...] += lrow

    tile = pl.BlockSpec((ROW_TILE, D_MODEL), lambda i: (i, 0))
    vec = pl.BlockSpec((1, D_MODEL), lambda i: (0, 0))
    return pl.pallas_call(
        body, name="loss_head", grid=(SEQ // ROW_TILE,),
        in_specs=[tile, vec, tile], out_specs=[pl.BlockSpec((1, LANES), lambda i: (0, 0)), tile, vec],
        out_shape=[jax.ShapeDtypeStruct((1, LANES), F32), jax.ShapeDtypeStruct((SEQ, D_MODEL), F32),
                   jax.ShapeDtypeStruct((1, D_MODEL), F32)],
        compiler_params=_cparams(("arbitrary",)),
    )(x, w, target)


S5_T = 256
S5_BLOCKS = [(slice(j * 256, (j + 1) * 256), slice(j * 1024, (j + 1) * 1024)) for j in range(2)]


def _s5_post(ypre, gate, wglu):
    y = _gelu(ypre)
    y = y * jax.nn.sigmoid(_bdot(y, wglu, 1, 0))
    return y * _silu(gate)


def _s5_fwd(proj, bbre, bbim, cre, cim, a2, dvec, wglu, name):
    def body(u_ref, g_ref, bbre_ref, bbim_ref, cre_ref, cim_ref, a_ref, d_ref, wg_ref, o_ref, sre_ref, sim_ref, st):
        @pl.when(pl.program_id(0) == 0)
        def _():
            st[...] = jnp.zeros_like(st)

        u = u_ref[...]
        ub = u.astype(BF16)
        for us, ss in S5_BLOCKS:
            sre_ref[:, ss] = _dg(ub[:, us], bbre_ref[us, ss], 1, 0)
            sim_ref[:, ss] = _dg(ub[:, us], bbim_ref[us, ss], 1, 0)
        ar, ai = a_ref[0:1, :], a_ref[1:2, :]

        def step(t, carry):
            sr, si = carry
            nr = ar * sr - ai * si + sre_ref[pl.ds(t, 1), :]
            ni = ar * si + ai * sr + sim_ref[pl.ds(t, 1), :]
            sre_ref[pl.ds(t, 1), :] = nr
            sim_ref[pl.ds(t, 1), :] = ni
            return nr, ni

        sr, si = lax.fori_loop(0, S5_T, step, (st[0:1, :], st[1:2, :]), unroll=8)
        st[0:1, :] = sr
        st[1:2, :] = si
        ypre = jnp.concatenate(
            [_dg(sre_ref[:, ss].astype(BF16), cre_ref[ss, us], 1, 0) - _dg(sim_ref[:, ss].astype(BF16), cim_ref[ss, us], 1, 0)
             for us, ss in S5_BLOCKS], axis=1) + d_ref[...] * u
        o_ref[...] = _s5_post(ypre, g_ref[...], wg_ref[...]).astype(BF16)

    full = lambda shape: pl.BlockSpec(shape, lambda c: (0, 0))
    return pl.pallas_call(
        body, name=name, grid=(SEQ // S5_T,),
        in_specs=[pl.BlockSpec((S5_T, BW), lambda c: (c, C_S5U // BW)), pl.BlockSpec((S5_T, BW), lambda c: (c, C_S5G // BW)),
                  full((BW, S5_CH)), full((BW, S5_CH)), full((S5_CH, BW)), full((S5_CH, BW)),
                  full((2, S5_CH)), full((1, BW)), full((BW, BW))],
        out_specs=[pl.BlockSpec((S5_T, BW), lambda c: (c, 0)), pl.BlockSpec((S5_T, S5_CH), lambda c: (c, 0)),
                   pl.BlockSpec((S5_T, S5_CH), lambda c: (c, 0))],
        out_shape=[jax.ShapeDtypeStruct((SEQ, BW), BF16), jax.ShapeDtypeStruct((SEQ, S5_CH), F32),
                   jax.ShapeDtypeStruct((SEQ, S5_CH), F32)],
        scratch_shapes=[pltpu.VMEM((2, S5_CH), F32)],
        compiler_params=_cparams(("arbitrary",)),
    )(proj, proj, bbre, bbim, cre, cim, a2, dvec, wglu)


def _s5_bwd(proj, dproj, dout, sre, sim, bbre, bbim, cre, cim, a2, dvec, wglu, name):
    nc = SEQ // S5_T

    def body(u_ref, g_ref, do_ref, sre_ref, sim_ref, pre_ref, pim_ref, bbre_ref, bbim_ref, cre_ref, cim_ref, a_ref,
             d_ref, wg_ref, dproj_in, dp_ref, dbbre_ref, dbbim_ref, dcre_ref, dcim_ref, da_ref, dd_ref, dwg_ref,
             gre, gim, st):
        c = nc - 1 - pl.program_id(0)

        @pl.when(pl.program_id(0) == 0)
        def _():
            st[...] = jnp.zeros_like(st)
            for r in (dbbre_ref, dbbim_ref, dcre_ref, dcim_ref, da_ref, dd_ref, dwg_ref):
                r[...] = jnp.zeros_like(r)

        u = u_ref[...]
        s_re, s_im = sre_ref[...], sim_ref[...]

        def head(s_res, s_ims, cres, cims, dv, uv, gv, wg):
            ypre = jnp.concatenate([_bdot(sr, cr, 1, 0) - _bdot(si, ci, 1, 0)
                                    for sr, si, cr, ci in zip(s_res, s_ims, cres, cims)], axis=1) + dv * uv
            return _s5_post(ypre, gv, wg)

        _, vjp = jax.vjp(head, [sre_ref[:, ss] for _, ss in S5_BLOCKS], [sim_ref[:, ss] for _, ss in S5_BLOCKS],
                         [cre_ref[ss, us].astype(F32) for us, ss in S5_BLOCKS],
                         [cim_ref[ss, us].astype(F32) for us, ss in S5_BLOCKS],
                         d_ref[...], u, g_ref[...], wg_ref[...].astype(F32))
        ds_res, ds_ims, dcres, dcims, dd, du_d, dgate, dwg = vjp(do_ref[0])
        for k, (us, ss) in enumerate(S5_BLOCKS):
            dcre_ref[ss, us] += dcres[k]
            dcim_ref[ss, us] += dcims[k]
            gre[:, ss] = ds_res[k]
            gim[:, ss] = ds_ims[k]
        dd_ref[...] += dd
        dwg_ref[...] += dwg
        dp_ref[:, BW:] = dgate.astype(BF16)
        ar, ai = a_ref[0:1, :], a_ref[1:2, :]

        def step(i, carry):
            t = S5_T - 1 - i
            gr, gi = carry
            nr = gre[pl.ds(t, 1), :] + gr
            ni = gim[pl.ds(t, 1), :] + gi
            gre[pl.ds(t, 1), :] = nr
            gim[pl.ds(t, 1), :] = ni
            return ar * nr + ai * ni, ar * ni - ai * nr

        gr, gi = lax.fori_loop(0, S5_T, step, (st[0:1, :], st[1:2, :]), unroll=8)
        st[0:1, :] = gr
        st[1:2, :] = gi
        g_re, g_im = gre[...], gim[...]
        first = jnp.where(c > 0, 1.0, 0.0)
        row = lax.broadcasted_iota(jnp.int32, (S5_T, S5_CH), 0)
        p_re = jnp.where(row == 0, pre_ref[7:8, :] * first, pltpu.roll(s_re, 1, 0))
        p_im = jnp.where(row == 0, pim_ref[7:8, :] * first, pltpu.roll(s_im, 1, 0))
        da_ref[0:1, :] += jnp.sum(g_re * p_re + g_im * p_im, axis=0, keepdims=True)
        da_ref[1:2, :] += jnp.sum(g_im * p_re - g_re * p_im, axis=0, keepdims=True)
        ub, grb, gib = u.astype(BF16), g_re.astype(BF16), g_im.astype(BF16)
        du_s = []
        for us, ss in S5_BLOCKS:
            dbbre_ref[us, ss] += _dg(ub[:, us], grb[:, ss], 0, 0)
            dbbim_ref[us, ss] += _dg(ub[:, us], gib[:, ss], 0, 0)
            du_s.append(_dg(grb[:, ss], bbre_ref[us, ss], 1, 1) + _dg(gib[:, ss], bbim_ref[us, ss], 1, 1))
        dp_ref[:, :BW] = (du_d + jnp.concatenate(du_s, axis=1)).astype(BF16)

    full = lambda shape: pl.BlockSpec(shape, lambda i: (0, 0))
    rev = lambda w, col=0: pl.BlockSpec((S5_T, w), lambda i: (nc - 1 - i, col))
    prev = pl.BlockSpec((8, S5_CH), lambda i: (jnp.maximum((nc - 1 - i) * (S5_T // 8) - 1, 0), 0))
    return pl.pallas_call(
        body, name=name, grid=(nc,),
        in_specs=[rev(BW, C_S5U // BW), rev(BW, C_S5G // BW), pl.BlockSpec((1, S5_T, BW), lambda i: (0, nc - 1 - i, 0)),
                  rev(S5_CH), rev(S5_CH), prev, prev,
                  full((BW, S5_CH)), full((BW, S5_CH)), full((S5_CH, BW)), full((S5_CH, BW)),
                  full((2, S5_CH)), full((1, BW)), full((BW, BW)), pl.BlockSpec(memory_space=pl.ANY)],
        out_specs=[rev(2 * BW, C_S5U // (2 * BW)), full((BW, S5_CH)), full((BW, S5_CH)), full((S5_CH, BW)), full((S5_CH, BW)),
                   full((2, S5_CH)), full((1, BW)), full((BW, BW))],
        input_output_aliases={14: 0},
        out_shape=[jax.ShapeDtypeStruct((SEQ, IN_PAD), BF16),
                   jax.ShapeDtypeStruct((BW, S5_CH), F32), jax.ShapeDtypeStruct((BW, S5_CH), F32),
                   jax.ShapeDtypeStruct((S5_CH, BW), F32), jax.ShapeDtypeStruct((S5_CH, BW), F32),
                   jax.ShapeDtypeStruct((2, S5_CH), F32), jax.ShapeDtypeStruct((1, BW), F32),
                   jax.ShapeDtypeStruct((BW, BW), F32)],
        scratch_shapes=[pltpu.VMEM((S5_T, S5_CH), F32), pltpu.VMEM((S5_T, S5_CH), F32), pltpu.VMEM((2, S5_CH), F32)],
        compiler_params=_cparams(("arbitrary",)),
    )(proj, proj, dout, sre, sim, sre, sim, bbre, bbim, cre, cim, a2, dvec, wglu, dproj)


def _s5_disc(lam_re, lam_im, b_re, b_im, c_re, c_im, d, log_step):
    step = jnp.exp(log_step)[:, None]
    mag = jnp.exp(lam_re * step)
    ab_re, ab_im = mag * jnp.cos(lam_im * step), mag * jnp.sin(lam_im * step)
    den = lam_re * lam_re + lam_im * lam_im
    nr = ab_re - 1.0
    coef_re = (nr * lam_re + ab_im * lam_im) / den
    coef_im = (ab_im * lam_re - nr * lam_im) / den
    bb_re = coef_re[..., None] * b_re - coef_im[..., None] * b_im
    bb_im = coef_re[..., None] * b_im + coef_im[..., None] * b_re
    def block_diag(t, rows_per, cols_per):
        wide = jnp.tile(t.reshape(S5_GROUPS * rows_per, cols_per), (1, S5_GROUPS))
        r = lax.broadcasted_iota(jnp.int32, wide.shape, 0) // rows_per
        c = lax.broadcasted_iota(jnp.int32, wide.shape, 1) // cols_per
        return jnp.where(r == c, wide, 0.0)

    bbre = block_diag(jnp.swapaxes(bb_re, 1, 2), S5_P, S5_STATE)
    bbim = block_diag(jnp.swapaxes(bb_im, 1, 2), S5_P, S5_STATE)
    cre = block_diag(jnp.swapaxes(c_re, 1, 2), S5_STATE, S5_P)
    cim = block_diag(jnp.swapaxes(c_im, 1, 2), S5_STATE, S5_P)
    a2 = jnp.stack([ab_re.reshape(-1), ab_im.reshape(-1)])
    return bbre, bbim, cre, cim, a2, d.reshape(1, BW)


def _left_lanes(shape):
    return lax.broadcasted_iota(jnp.int32, shape, 1) < 64


def _sgu_chunk(u, v, gate, ln_w, ln_b, w, bias):
    u32, v32 = _gelu(u), _gelu(v)
    mu = jnp.mean(v32, axis=-1, keepdims=True)
    var = jnp.mean(jnp.square(v32 - mu), axis=-1, keepdims=True)
    vn = (v32 - mu) * lax.rsqrt(var + EPS) * ln_w + ln_b
    t_i = lax.broadcasted_iota(jnp.int32, (SGU_CHUNK, SGU_CHUNK), 0)
    s_i = lax.broadcasted_iota(jnp.int32, (SGU_CHUNK, SGU_CHUNK), 1)
    causal = t_i >= s_i
    left = _left_lanes((SGU_CHUNK, LANES))
    sgate = _silu(gate)
    outs = []
    for j in range(BW // LANES):
        vb = vn[:, j * LANES:(j + 1) * LANES]
        s_blk = (_bdot(jnp.where(causal, w[2 * j], 0.0), jnp.where(left, vb, 0.0), 1, 0)
                 + _bdot(jnp.where(causal, w[2 * j + 1], 0.0), jnp.where(left, 0.0, vb), 1, 0))
        sl = slice(j * LANES, (j + 1) * LANES)
        outs.append(u32[:, sl] * (s_blk + bias[:, sl]) * sgate[:, sl])
    return outs


def _sgu_fwd(proj, ln_w, ln_b, w, bias, name):
    def body(u_ref, v_ref, g_ref, lw_ref, lb_ref, w_ref, b_ref, o_ref):
        outs = _sgu_chunk(u_ref[...], v_ref[...], g_ref[...], lw_ref[...], lb_ref[...], w_ref[...], b_ref[...])
        for j, o in enumerate(outs):
            o_ref[:, j * LANES:(j + 1) * LANES] = o.astype(BF16)

    blk = lambda col: pl.BlockSpec((SGU_CHUNK, BW), lambda c: (c, col // BW))
    vec = pl.BlockSpec((1, BW), lambda c: (0, 0))
    return pl.pallas_call(
        body, name=name, grid=(SEQ // SGU_CHUNK,),
        in_specs=[blk(C_SGU_U), blk(C_SGU_V), blk(C_SGU_G), vec, vec,
                  pl.BlockSpec((SGU_HEADS, SGU_CHUNK, SGU_CHUNK), lambda c: (0, 0, 0)),
                  pl.BlockSpec((SGU_CHUNK, BW), lambda c: (0, 0))],
        out_specs=pl.BlockSpec((SGU_CHUNK, BW), lambda c: (c, 0)),
        out_shape=jax.ShapeDtypeStruct((SEQ, BW), BF16),
        compiler_params=_cparams(("parallel",)),
    )(proj, proj, proj, ln_w, ln_b, w, bias)


def _sgu_bwd(proj, dproj, dout, ln_w, ln_b, w, bias, name):
    def body(u_ref, v_ref, g_ref, do_ref, lw_ref, lb_ref, w_ref, b_ref, dproj_in, dp_ref, dlw_ref, dlb_ref, dw_ref, db_ref):
        _, vjp = jax.vjp(_sgu_chunk, u_ref[...], v_ref[...], g_ref[...], lw_ref[...], lb_ref[...], w_ref[...], b_ref[...])
        do = do_ref[0]
        du, dv, dgate, dlw, dlb, dw, db = vjp([do[:, j * LANES:(j + 1) * LANES] for j in range(BW // LANES)])
        dp_ref[:, 0:BW] = du.astype(BF16)
        dp_ref[:, BW:2 * BW] = dv.astype(BF16)
        dp_ref[:, 2 * BW:3 * BW] = dgate.astype(BF16)
        dp_ref[:, 3 * BW:] = jnp.zeros((SGU_CHUNK, BW), BF16)

        @pl.when(pl.program_id(0) == 0)
        def _():
            dlw_ref[...] = dlw
            dlb_ref[...] = dlb
            dw_ref[...] = dw
            db_ref[...] = db

        @pl.when(pl.program_id(0) > 0)
        def _():
            dlw_ref[...] += dlw
            dlb_ref[...] += dlb
            dw_ref[...] += dw
            db_ref[...] += db

    blk = lambda col: pl.BlockSpec((SGU_CHUNK, BW), lambda c: (c, col // BW))
    vec = pl.BlockSpec((1, BW), lambda c: (0, 0))
    wsp = pl.BlockSpec((SGU_HEADS, SGU_CHUNK, SGU_CHUNK), lambda c: (0, 0, 0))
    bsp = pl.BlockSpec((SGU_CHUNK, BW), lambda c: (0, 0))
    return pl.pallas_call(
        body, name=name, grid=(SEQ // SGU_CHUNK,),
        in_specs=[blk(C_SGU_U), blk(C_SGU_V), blk(C_SGU_G), pl.BlockSpec((1, SGU_CHUNK, BW), lambda c: (1, c, 0)),
                  vec, vec, wsp, bsp, pl.BlockSpec(memory_space=pl.ANY)],
        out_specs=[pl.BlockSpec((SGU_CHUNK, 4 * BW), lambda c: (c, C_SGU_U // (4 * BW))), vec, vec, wsp, bsp],
        input_output_aliases={8: 0},
        out_shape=[jax.ShapeDtypeStruct((SEQ, IN_PAD), BF16), jax.ShapeDtypeStruct((1, BW), F32),
                   jax.ShapeDtypeStruct((1, BW), F32), jax.ShapeDtypeStruct((SGU_HEADS, SGU_CHUNK, SGU_CHUNK), F32),
                   jax.ShapeDtypeStruct((SGU_CHUNK, BW), F32)],
        compiler_params=_cparams(("arbitrary",)),
    )(proj, proj, proj, dout, ln_w, ln_b, w, bias, dproj)


CONV_BLK = 256


def _m2_conv_fwd(proj, w, b, name):
    def body(x_ref, w_ref, b_ref, o_ref):
        x = x_ref[...]
        acc = jnp.zeros_like(x) + b_ref[...]
        for k in range(M2_CONV):
            acc = acc + w_ref[k:k + 1, :] * _shift_down(x, M2_CONV - 1 - k)
        o_ref[...] = _silu(acc)

    return pl.pallas_call(
        body, name=name, grid=(M2_CONV_CH // CONV_BLK,),
        in_specs=[pl.BlockSpec((SEQ, CONV_BLK), lambda j: (0, C_M2X // CONV_BLK + j)),
                  pl.BlockSpec((M2_CONV, CONV_BLK), lambda j: (0, j)), pl.BlockSpec((1, CONV_BLK), lambda j: (0, j))],
        out_specs=pl.BlockSpec((SEQ, CONV_BLK), lambda j: (0, j)),
        out_shape=jax.ShapeDtypeStruct((SEQ, M2_CONV_CH), F32),
        compiler_params=_cparams(("parallel",)),
    )(proj, w, b)


def _m2_conv_bwd(proj, dproj, dxa, w, b, name):
    def body(x_ref, d_ref, w_ref, b_ref, dproj_in, dx_ref, dw_ref, db_ref):
        x = x_ref[...]
        xs = [_shift_down(x, M2_CONV - 1 - k) for k in range(M2_CONV)]
        acc = jnp.zeros_like(x) + b_ref[...]
        for k in range(M2_CONV):
            acc = acc + w_ref[k:k + 1, :] * xs[k]
        sg = jax.nn.sigmoid(acc)
        dacc = d_ref[...] * (sg * (1.0 + acc * (1.0 - sg)))
        dx = jnp.zeros_like(x)
        for k in range(M2_CONV):
            dx = dx + w_ref[k:k + 1, :] * _shift_up(dacc, M2_CONV - 1 - k)
            dw_ref[k:k + 1, :] = jnp.sum(dacc * xs[k], axis=0, keepdims=True)
        dx_ref[...] = dx.astype(BF16)
        db_ref[...] = jnp.sum(dacc, axis=0, keepdims=True)

    return pl.pallas_call(
        body, name=name, grid=(M2_CONV_CH // CONV_BLK,),
        in_specs=[pl.BlockSpec((SEQ, CONV_BLK), lambda j: (0, C_M2X // CONV_BLK + j)),
                  pl.BlockSpec((SEQ, CONV_BLK), lambda j: (0, j)),
                  pl.BlockSpec((M2_CONV, CONV_BLK), lambda j: (0, j)), pl.BlockSpec((1, CONV_BLK), lambda j: (0, j)),
                  pl.BlockSpec(memory_space=pl.ANY)],
        out_specs=[pl.BlockSpec((SEQ, CONV_BLK), lambda j: (0, C_M2X // CONV_BLK + j)),
                   pl.BlockSpec((M2_CONV, CONV_BLK), lambda j: (0, j)), pl.BlockSpec((1, CONV_BLK), lambda j: (0, j))],
        input_output_aliases={4: 0},
        out_shape=[jax.ShapeDtypeStruct((SEQ, IN_PAD), BF16), jax.ShapeDtypeStruct((M2_CONV, M2_CONV_CH), F32),
                   jax.ShapeDtypeStruct((1, M2_CONV_CH), F32)],
        compiler_params=_cparams(("parallel",)),
    )(proj, dxa, w, b, dproj)


N_PAIR = M2_HEADS // 2
HI = lax.Precision.HIGHEST


def _col(a, h):
    lane = lax.broadcasted_iota(jnp.int32, a.shape, 1)
    return jnp.sum(jnp.where(lane == h, a, 0.0), axis=1, keepdims=True)


def _row(a, h):
    sub = lax.broadcasted_iota(jnp.int32, a.shape, 0)
    return jnp.sum(jnp.where(sub == h, a, 0.0), axis=0, keepdims=True)


def _ssd_chunk(xs, bms, cms, dtr, zs, states, dt_bias, a_log, dfs, nws):
    q = M2_CHUNK
    dt = _softplus(dtr + dt_bias)
    da = dt * (-jnp.exp(a_log))
    l_i = lax.broadcasted_iota(jnp.int32, (q, q), 0)
    s_i = lax.broadcasted_iota(jnp.int32, (q, q), 1)
    causal = l_i >= s_i
    tril = jnp.where(causal, 1.0, 0.0)
    a_cs = _dg(tril, da, 1, 0, HI)
    a_cs_t = _dg(da, tril, 0, 1, HI)
    a_end = _row(a_cs, q - 1)
    left = _left_lanes((q, LANES))
    left1 = _left_lanes((1, LANES))
    ys, nexts = [], []
    for j in range(N_PAIR):
        grp = j // 2
        bm, cm = bms[grp], cms[grp]
        h0, h1 = 2 * j, 2 * j + 1
        cb = _bdot(cm, bm, 1, 1)
        xdt = xs[j] * jnp.where(left, _col(dt, h0), _col(dt, h1))
        acs0, acs1 = _col(a_cs, h0), _col(a_cs, h1)
        y = _bdot(cm, states[j], 1, 0) * jnp.where(left, jnp.exp(acs0), jnp.exp(acs1))
        s_new = states[j] * jnp.where(left1, jnp.exp(_col(a_end, h0)), jnp.exp(_col(a_end, h1)))
        for h, acs, xh in ((h0, acs0, jnp.where(left, xdt, 0.0)), (h1, acs1, jnp.where(left, 0.0, xdt))):
            decay = jnp.exp(jnp.where(causal, acs - _row(a_cs_t, h), -jnp.inf))
            y = y + _bdot(cb * decay, xh, 1, 0)
            s_new = s_new + _bdot(bm * jnp.exp(_col(a_end, h) - acs), xh, 0, 0)
        ys.append((y + dfs[j] * xs[j]) * _silu(zs[j]))
        nexts.append(s_new)
    ssq = sum(jnp.sum(y * y, axis=-1, keepdims=True) for y in ys)
    scale = lax.rsqrt(ssq / BW + EPS)
    return [y * scale * nw for y, nw in zip(ys, nws)], nexts


def _blocks(ref, n, width=LANES):
    return [ref[:, j * width:(j + 1) * width] for j in range(n)]


def _ssd_fwd(proj, xa, dt_bias, a_log, dfull, nw, name):
    nc = SEQ // M2_CHUNK

    def body(x_ref, b_ref, c_ref, dt_ref, z_ref, dtb_ref, al_ref, df_ref, nw_ref, o_ref, sin_ref, st):
        @pl.when(pl.program_id(0) == 0)
        def _():
            st[...] = jnp.zeros_like(st)

        states = [st[j] for j in range(N_PAIR)]
        for j in range(N_PAIR):
            sin_ref[0, j] = states[j]
        ys, nexts = _ssd_chunk(_blocks(x_ref, 4), _blocks(b_ref, 2), _blocks(c_ref, 2), dt_ref[...], _blocks(z_ref, 4),
                               states, dtb_ref[...], al_ref[...], _blocks(df_ref, 4), _blocks(nw_ref, 4))
        for j in range(N_PAIR):
            o_ref[:, j * LANES:(j + 1) * LANES] = ys[j].astype(BF16)
            st[j] = nexts[j]

    vec8 = pl.BlockSpec((1, LANES), lambda c: (0, 0))
    vec = pl.BlockSpec((1, BW), lambda c: (0, 0))
    return pl.pallas_call(
        body, name=name, grid=(nc,),
        in_specs=[pl.BlockSpec((M2_CHUNK, BW), lambda c: (c, 0)), pl.BlockSpec((M2_CHUNK, 256), lambda c: (c, 2)),
                  pl.BlockSpec((M2_CHUNK, 256), lambda c: (c, 3)), pl.BlockSpec((M2_CHUNK, LANES), lambda c: (c, C_DT // LANES)),
                  pl.BlockSpec((M2_CHUNK, BW), lambda c: (c, C_M2Z // BW)), vec8, vec8, vec, vec],
        out_specs=[pl.BlockSpec((M2_CHUNK, BW), lambda c: (c, 0)),
                   pl.BlockSpec((1, N_PAIR, M2_STATE, LANES), lambda c: (c, 0, 0, 0))],
        out_shape=[jax.ShapeDtypeStruct((SEQ, BW), BF16), jax.ShapeDtypeStruct((nc, N_PAIR, M2_STATE, LANES), F32)],
        scratch_shapes=[pltpu.VMEM((N_PAIR, M2_STATE, LANES), F32)],
        compiler_params=_cparams(("arbitrary",)),
    )(xa, xa, xa, proj, proj, dt_bias, a_log, dfull, nw)


def _ssd_bwd(proj, dproj, xa, dout, s_in, dt_bias, a_log, dfull, nw, name):
    nc = SEQ // M2_CHUNK

    def body(x_ref, b_ref, c_ref, dt_ref, z_ref, do_ref, sin_ref, dtb_ref, al_ref, df_ref, nw_ref, dproj_in,
             dp_ref, dxa_ref, ddtb_ref, dal_ref, ddf_ref, dnw_ref, dst):
        @pl.when(pl.program_id(0) == 0)
        def _():
            dst[...] = jnp.zeros_like(dst)
            for r in (ddtb_ref, dal_ref, ddf_ref, dnw_ref):
                r[...] = jnp.zeros_like(r)

        states = [sin_ref[0, j] for j in range(N_PAIR)]
        _, vjp = jax.vjp(_ssd_chunk, _blocks(x_ref, 4), _blocks(b_ref, 2), _blocks(c_ref, 2), dt_ref[...],
                         _blocks(z_ref, 4), states, dtb_ref[...], al_ref[...], _blocks(df_ref, 4), _blocks(nw_ref, 4))
        dxs, dbs, dcs, ddt, dzs, dstates, ddtb, dal, ddfs, dnws = vjp(
            ([do_ref[0, :, j * LANES:(j + 1) * LANES] for j in range(N_PAIR)], [dst[j] for j in range(N_PAIR)]))
        for j in range(N_PAIR):
            sl = slice(j * LANES, (j + 1) * LANES)
            dxa_ref[:, sl] = dxs[j]
            dp_ref[:, sl] = dzs[j].astype(BF16)
            dst[j] = dstates[j]
            ddf_ref[:, sl] += ddfs[j]
            dnw_ref[:, sl] += dnws[j]
        for g in range(2):
            dxa_ref[:, BW + g * LANES:BW + (g + 1) * LANES] = dbs[g]
            dxa_ref[:, BW + 256 + g * LANES:BW + 256 + (g + 1) * LANES] = dcs[g]
        dp_ref[:, BW:BW + LANES] = ddt.astype(BF16)
        dp_ref[:, BW + LANES:] = jnp.zeros((M2_CHUNK, 2 * BW - BW - LANES), BF16)
        ddtb_ref[...] += ddtb
        dal_ref[...] += dal

    rev = lambda w, col=0: pl.BlockSpec((M2_CHUNK, w), lambda i: (nc - 1 - i, col))
    vec8 = pl.BlockSpec((1, LANES), lambda i: (0, 0))
    vec = pl.BlockSpec((1, BW), lambda i: (0, 0))
    return pl.pallas_call(
        body, name=name, grid=(nc,),
        in_specs=[rev(BW), rev(256, 2), rev(256, 3), rev(LANES, C_DT // LANES), rev(BW, C_M2Z // BW),
                  pl.BlockSpec((1, M2_CHUNK, BW), lambda i: (2, nc - 1 - i, 0)),
                  pl.BlockSpec((1, N_PAIR, M2_STATE, LANES), lambda i: (nc - 1 - i, 0, 0, 0)), vec8, vec8, vec, vec,
                  pl.BlockSpec(memory_space=pl.ANY)],
        out_specs=[rev(2 * BW, C_M2Z // (2 * BW)), rev(M2_CONV_CH), vec8, vec8, vec, vec],
        input_output_aliases={11: 0},
        out_shape=[jax.ShapeDtypeStruct((SEQ, IN_PAD), BF16), jax.ShapeDtypeStruct((SEQ, M2_CONV_CH), F32),
                   jax.ShapeDtypeStruct((1, LANES), F32), jax.ShapeDtypeStruct((1, LANES), F32),
                   jax.ShapeDtypeStruct((1, BW), F32), jax.ShapeDtypeStruct((1, BW), F32)],
        scratch_shapes=[pltpu.VMEM((N_PAIR, M2_STATE, LANES), F32)],
        compiler_params=_cparams(("arbitrary",)),
    )(xa, xa, xa, proj, proj, dout, s_in, dt_bias, a_log, dfull, nw, dproj)


def _sc_specs():
    col = lambda kind: pl.BlockSpec((SEQ, LANES), lambda j: (0, C_SC // LANES + 4 * j + kind))
    return [col(0), col(1), col(2), col(3)]


def _sc_fwd(proj, w, name):
    def body(b_ref, c_ref, h_ref, g_ref, w_ref, o_ref):
        ch = c_ref[...] * h_ref[...]
        acc = jnp.zeros_like(ch)
        for k in range(SC_CONV):
            acc = acc + w_ref[k:k + 1, :] * _shift_down(ch, SC_CONV - 1 - k)
        o_ref[...] = (b_ref[...] * acc * _silu(g_ref[...])).astype(BF16)

    return pl.pallas_call(
        body, name=name, grid=(BW // LANES,),
        in_specs=_sc_specs() + [pl.BlockSpec((SC_CONV, LANES), lambda j: (0, j))],
        out_specs=pl.BlockSpec((SEQ, LANES), lambda j: (0, j)),
        out_shape=jax.ShapeDtypeStruct((SEQ, BW), BF16),
        compiler_params=_cparams(("parallel",)),
    )(proj, proj, proj, proj, w)


def _sc_bwd(proj, dproj, dout, w, name):
    def body(b_ref, c_ref, h_ref, g_ref, do_ref, w_ref, dproj_in, dp_ref, dw_ref):
        cv, hv, gv = c_ref[...], h_ref[...], g_ref[...]
        ch = cv * hv
        chs = [_shift_down(ch, SC_CONV - 1 - k) for k in range(SC_CONV)]
        acc = jnp.zeros_like(ch)
        for k in range(SC_CONV):
            acc = acc + w_ref[k:k + 1, :] * chs[k]
        sg = jax.nn.sigmoid(gv)
        do = do_ref[0]
        bv = b_ref[...]
        dp_ref[:, 0:LANES] = (do * acc * (gv * sg)).astype(BF16)
        dp_ref[:, 3 * LANES:] = (do * bv * acc * (sg * (1.0 + gv * (1.0 - sg)))).astype(BF16)
        dacc = do * bv * (gv * sg)
        dch = jnp.zeros_like(ch)
        for k in range(SC_CONV):
            dch = dch + w_ref[k:k + 1, :] * _shift_up(dacc, SC_CONV - 1 - k)
            dw_ref[k:k + 1, :] = jnp.sum(dacc * chs[k], axis=0, keepdims=True)
        dp_ref[:, LANES:2 * LANES] = (dch * hv).astype(BF16)
        dp_ref[:, 2 * LANES:3 * LANES] = (dch * cv).astype(BF16)

    wsp = pl.BlockSpec((SC_CONV, LANES), lambda j: (0, j))
    return pl.pallas_call(
        body, name=name, grid=(BW // LANES,),
        in_specs=_sc_specs() + [pl.BlockSpec((1, SEQ, LANES), lambda j: (3, 0, j)), wsp, pl.BlockSpec(memory_space=pl.ANY)],
        out_specs=[pl.BlockSpec((SEQ, 4 * LANES), lambda j: (0, C_SC // (4 * LANES) + j)), wsp],
        input_output_aliases={6: 0},
        out_shape=[jax.ShapeDtypeStruct((SEQ, IN_PAD), BF16), jax.ShapeDtypeStruct((SC_CONV, BW), F32)],
        compiler_params=_cparams(("parallel",)),
    )(proj, proj, proj, proj, dout, w, dproj)


MERGE_T = 256
MERGE_BWD_T = 512


def _merge_fwd(proj, ys, merge_b, w_branch, name):
    def body(y_ref, lg_ref, b_ref, w_ref, o_ref):
        acc = jnp.zeros((MERGE_T, D_MODEL), F32)
        for k in range(N_BRANCH):
            gate = jax.nn.sigmoid(lg_ref[:, k * D_MODEL:(k + 1) * D_MODEL] + b_ref[k])
            acc = acc + gate * _dg(y_ref[k], w_ref[k], 1, 0)
        o_ref[...] = acc.astype(BF16)

    return pl.pallas_call(
        body, name=name, grid=(SEQ // MERGE_T,),
        in_specs=[pl.BlockSpec((N_BRANCH, MERGE_T, BW), lambda i: (0, i, 0)),
                  pl.BlockSpec((MERGE_T, N_BRANCH * D_MODEL), lambda i: (i, C_MERGE // (N_BRANCH * D_MODEL))),
                  pl.BlockSpec((N_BRANCH, 1, D_MODEL), lambda i: (0, 0, 0)),
                  pl.BlockSpec((N_BRANCH, BW, D_MODEL), lambda i: (0, 0, 0))],
        out_specs=pl.BlockSpec((MERGE_T, D_MODEL), lambda i: (i, 0)),
        out_shape=jax.ShapeDtypeStruct((SEQ, D_MODEL), BF16),
        compiler_params=_cparams(("parallel",)),
    )(ys, proj, merge_b, w_branch)


def _merge_bwd(proj, ys, dm, merge_b, w_branch, name):
    nt = SEQ // MERGE_BWD_T

    def body(y_ref, lg_ref, dm_ref, b_ref, w_ref, dy_ref, dlg_ref, dw_ref, db_ref, dw_acc):
        i = pl.program_id(1)
        gate = jax.nn.sigmoid(lg_ref[...] + b_ref[0])
        y = y_ref[0]
        dmv = dm_ref[...]
        dbo = (gate * dmv).astype(BF16)
        dlg = _dg(y, w_ref[0], 1, 0) * dmv * gate * (1.0 - gate)
        dlg_ref[...] = dlg.astype(BF16)
        dy_ref[0] = _dg(dbo, w_ref[0], 1, 1)
        dwp = _dg(y, dbo, 0, 0)
        dbp = jnp.sum(dlg, axis=0, keepdims=True)

        @pl.when(i == 0)
        def _():
            dw_acc[...] = dwp
            db_ref[0] = dbp

        @pl.when(i > 0)
        def _():
            dw_acc[...] += dwp
            db_ref[0] += dbp

        @pl.when(i == nt - 1)
        def _():
            dw_ref[0] = dw_acc[...].astype(BF16)

    return pl.pallas_call(
        body, name=name, grid=(N_BRANCH, nt),
        in_specs=[pl.BlockSpec((1, MERGE_BWD_T, BW), lambda k, i: (k, i, 0)),
                  pl.BlockSpec((MERGE_BWD_T, D_MODEL), lambda k, i: (i, C_MERGE // D_MODEL + k)),
                  pl.BlockSpec((MERGE_BWD_T, D_MODEL), lambda k, i: (i, 0)),
                  pl.BlockSpec((1, 1, D_MODEL), lambda k, i: (k, 0, 0)),
                  pl.BlockSpec((1, BW, D_MODEL), lambda k, i: (k, 0, 0))],
        out_specs=[pl.BlockSpec((1, MERGE_BWD_T, BW), lambda k, i: (k, i, 0)),
                   pl.BlockSpec((MERGE_BWD_T, D_MODEL), lambda k, i: (i, k)),
                   pl.BlockSpec((1, BW, D_MODEL), lambda k, i: (k, 0, 0)),
                   pl.BlockSpec((1, 1, D_MODEL), lambda k, i: (k, 0, 0))],
        out_shape=[jax.ShapeDtypeStruct((N_BRANCH, SEQ, BW), F32), jax.ShapeDtypeStruct((SEQ, IN_PAD), BF16),
                   jax.ShapeDtypeStruct((N_BRANCH, BW, D_MODEL), BF16), jax.ShapeDtypeStruct((N_BRANCH, 1, D_MODEL), F32)],
        scratch_shapes=[pltpu.VMEM((BW, D_MODEL), F32)],
        compiler_params=_cparams(("parallel", "arbitrary")),
    )(ys, proj, dm, merge_b, w_branch)


def _adamw(glist, w, m, v, rows, name):
    nl = len(glist)
    n, r, c = glist[0].shape
    assert w.shape == (nl, r, c) and r % rows == 0
    nb = r // rows

    def body(*refs):
        g_refs = refs[:nl]
        w_ref, m_ref, v_ref, go_ref, d_ref, mo_ref, vo_ref = refs[nl:]
        for layer in range(nl):
            @pl.when(pl.program_id(0) == layer)
            def _(g_ref=g_refs[layer]):
                g = g_ref[0].astype(F32)
                for s in range(1, n):
                    g = g + g_ref[s].astype(F32)
                mn = ADAM_B1 * m_ref[0] + (1.0 - ADAM_B1) * g
                vn = ADAM_B2 * v_ref[0] + (1.0 - ADAM_B2) * jnp.square(g)
                m_hat = mn / (1.0 - ADAM_B1 ** ADAM_STEP)
                v_hat = vn / (1.0 - ADAM_B2 ** ADAM_STEP)
                go_ref[0] = g
                d_ref[0] = -ADAM_LR * (m_hat / (jnp.sqrt(v_hat) + ADAM_EPS) + ADAM_WD * w_ref[0])
                mo_ref[0] = mn
                vo_ref[0] = vn

    def g_spec(layer):
        return pl.BlockSpec((n, rows, c), lambda a, i: (0, jnp.where(a < layer, 0, jnp.where(a == layer, i, nb - 1)), 0))

    blk = pl.BlockSpec((1, rows, c), lambda a, i: (a, i, 0))
    out = jax.ShapeDtypeStruct((nl, r, c), F32)
    return pl.pallas_call(
        body, name=name, grid=(nl, nb),
        in_specs=[g_spec(layer) for layer in range(nl)] + [blk, blk, blk],
        out_specs=[blk, blk, blk, blk], out_shape=[out, out, out, out],
        compiler_params=_cparams(("arbitrary", "arbitrary")),
    )(*glist, w, m, v)


X_ROWS_PER_COL = 2 * (D_MODEL // LANES)


def _w_in_to_x(w):
    t = jnp.transpose(w, (2, 0, 1)).reshape(SHARD_IN, DEPTH, D_MODEL // LANES, LANES)
    return jnp.transpose(t, (0, 2, 1, 3)).reshape(SHARD_IN * X_ROWS_PER_COL, LANES)


def _w_in_from_x(xv):
    t = jnp.transpose(xv.reshape(SHARD_IN, D_MODEL // LANES, DEPTH, LANES), (0, 2, 1, 3))
    return jnp.transpose(t.reshape(SHARD_IN, DEPTH, D_MODEL), (1, 2, 0))


def _adamw_w_in(glist, w, m, v, name):
    n = glist[0].shape[0]
    cols = LANES
    rows = cols * X_ROWS_PER_COL

    def body(g0_ref, g1_ref, w_ref, m_ref, v_ref, go_ref, d_ref, mo_ref, vo_ref):
        for layer, g_ref in enumerate((g0_ref, g1_ref)):
            g = g_ref[0].astype(F32)
            for s in range(1, n):
                g = g + g_ref[s].astype(F32)
            gt = g.T
            for t in range(D_MODEL // LANES):
                sel = (pl.ds(2 * t + layer, cols, stride=X_ROWS_PER_COL), slice(None))
                gs = gt[:, t * LANES:(t + 1) * LANES]
                mn = ADAM_B1 * m_ref[sel] + (1.0 - ADAM_B1) * gs
                vn = ADAM_B2 * v_ref[sel] + (1.0 - ADAM_B2) * jnp.square(gs)
                m_hat = mn / (1.0 - ADAM_B1 ** ADAM_STEP)
                v_hat = vn / (1.0 - ADAM_B2 ** ADAM_STEP)
                go_ref[sel] = gs
                d_ref[sel] = -ADAM_LR * (m_hat / (jnp.sqrt(v_hat) + ADAM_EPS) + ADAM_WD * w_ref[sel])
                mo_ref[sel] = mn
                vo_ref[sel] = vn

    g_spec = pl.BlockSpec((n, D_MODEL, cols), lambda i: (0, 0, i))
    blk = pl.BlockSpec((rows, LANES), lambda i: (i, 0))
    out = jax.ShapeDtypeStruct((SHARD_IN * X_ROWS_PER_COL, LANES), F32)
    res = pl.pallas_call(
        body, name=name, grid=(-(-SHARD_IN // cols),),
        in_specs=[g_spec, g_spec, blk, blk, blk], out_specs=[blk, blk, blk, blk], out_shape=[out, out, out, out],
        compiler_params=_cparams(("parallel",)),
    )(*glist, _w_in_to_x(w), _w_in_to_x(m), _w_in_to_x(v))
    return [_w_in_from_x(o) for o in res]


def _slot_sum(gslots, name):
    n, r, c = gslots.shape

    def body(g_ref, o_ref):
        g = g_ref[0]
        for s in range(1, n):
            g = g + g_ref[s]
        o_ref[...] = g

    return pl.pallas_call(
        body, name=name, in_specs=[pl.BlockSpec((n, r, c), lambda: (0, 0, 0))],
        out_specs=pl.BlockSpec((r, c), lambda: (0, 0)), out_shape=jax.ShapeDtypeStruct((r, c), F32),
        compiler_params=_cparams(None),
    )(gslots)


def _me_and_peers():
    x, y, c = lax.axis_index("x"), lax.axis_index("y"), lax.axis_index("c")
    me = 4 * x + 2 * y + c
    peers = []
    for k in range(1, N_DEV):
        px = 1 - x if (k >> 2) & 1 else x
        py = 1 - y if (k >> 1) & 1 else y
        pc = 1 - c if k & 1 else c
        peers.append((4 * px + 2 * py + pc, (px, py, pc)))
    return me, peers


def _exchange(tensors, gather, name):
    n = len(tensors)

    def body(*refs):
        ins, outs = refs[:n], refs[n:2 * n]
        send_sems, recv_sems, local_sems = refs[2 * n:]
        me, peers = _me_and_peers()
        started = []
        for t in range(n):
            own = pltpu.make_async_copy(ins[t] if gather else ins[t].at[me], outs[t].at[me], local_sems.at[t])
            own.start()
            started.append(own)
            for k, (pidx, pos) in enumerate(peers):
                cp = pltpu.make_async_remote_copy(
                    src_ref=ins[t] if gather else ins[t].at[pidx], dst_ref=outs[t].at[me],
                    send_sem=send_sems.at[t, k], recv_sem=recv_sems.at[t, k], device_id=pos, device_id_type=MESH)
                cp.start()
                started.append(cp)
        for cp in started:
            cp.wait()

    any_spec = pl.BlockSpec(memory_space=pl.ANY)
    outs = pl.pallas_call(
        body, name=name, in_specs=[any_spec] * n, out_specs=[any_spec] * n,
        out_shape=[jax.ShapeDtypeStruct(((N_DEV,) + t.shape) if gather else t.shape, t.dtype) for t in tensors],
        scratch_shapes=[pltpu.SemaphoreType.DMA((n, N_DEV - 1)), pltpu.SemaphoreType.DMA((n, N_DEV - 1)),
                        pltpu.SemaphoreType.DMA((n,))],
        compiler_params=pltpu.CompilerParams(has_side_effects=True),
    )(*tensors)
    return list(outs)


_HBM = pl.BlockSpec(memory_space=pltpu.HBM)
_SEM = pl.BlockSpec(memory_space=pltpu.SEMAPHORE)
_EFFECT = pltpu.SideEffectType.DATAFLOW_SIDE_EFFECTING


N_CHIP = N_DEV // 2


def _chip_peers():
    x, y, c = lax.axis_index("x"), lax.axis_index("y"), lax.axis_index("c")
    chips = []
    for d in range(1, N_CHIP):
        px = 1 - x if (d >> 1) & 1 else x
        py = 1 - y if d & 1 else y
        chips.append((2 * px + py, (px, py)))
    return (x, y, c), 2 * x + y, chips


def _plan_gather(ins, lands, send_sems, recv_sems, local_sems, first=0):
    (x, y, c), q, chips = _chip_peers()
    me = 2 * q + c
    plan = dict(start=[], relay_wait=[], relay_start=[], local=[], sends=[], recvs=[])
    for t in range(len(ins)):
        base = (first + t) * 7
        sem = lambda k: dict(send_sem=send_sems.at[base + k], recv_sem=recv_sems.at[base + k], device_id_type=MESH)
        own = pltpu.make_async_copy(ins[t], lands[t].at[me], local_sems.at[first + t])
        to_sib = pltpu.make_async_remote_copy(src_ref=ins[t], dst_ref=lands[t].at[me], device_id=(x, y, 1 - c), **sem(0))
        plan['start'] += [own, to_sib]
        plan['local'].append(own)
        plan['sends'].append(to_sib)
        plan['recvs'].append(to_sib)
        for d, (pq, (px, py)) in enumerate(chips):
            to_chip = pltpu.make_async_remote_copy(src_ref=ins[t], dst_ref=lands[t].at[me], device_id=(px, py, c), **sem(1 + d))
            blk = lands[t].at[2 * pq + c]
            fwd = pltpu.make_async_remote_copy(src_ref=blk, dst_ref=blk, device_id=(x, y, 1 - c), **sem(4 + d))
            plan['start'].append(to_chip)
            plan['relay_wait'].append(to_chip)
            plan['relay_start'].append(fwd)
            plan['sends'] += [to_chip, fwd]
            plan['recvs'].append(fwd)
    return plan


def _plan_pair(ins, lands, send_sems, recv_sems, local_sems):
    (x, y, c), q, chips = _chip_peers()
    plan = dict(start=[], local=[], sends=[], recvs=[])
    for t in range(len(ins)):
        for k in range(N_CHIP):
            cp = pltpu.make_async_remote_copy(
                src_ref=ins[t].at[2 * k + 1 - c], dst_ref=lands[t].at[k], send_sem=send_sems.at[t * N_CHIP + k],
                recv_sem=recv_sems.at[t * N_CHIP + k], device_id=(x, y, 1 - c), device_id_type=MESH)
            plan['start'].append(cp)
            plan['sends'].append(cp)
            plan['recvs'].append(cp)
    return plan


def _plan_chips(ins, lands, send_sems, recv_sems, local_sems):
    (x, y, c), q, chips = _chip_peers()
    plan = dict(start=[], local=[], sends=[], recvs=[])
    for t in range(len(ins)):
        own = pltpu.make_async_copy(ins[t].at[q], lands[t].at[q], local_sems.at[t])
        plan['start'].append(own)
        plan['local'].append(own)
        for d, (pq, (px, py)) in enumerate(chips):
            cp = pltpu.make_async_remote_copy(
                src_ref=ins[t].at[pq], dst_ref=lands[t].at[q], send_sem=send_sems.at[t * 3 + d],
                recv_sem=recv_sems.at[t * 3 + d], device_id=(px, py, c), device_id_type=MESH)
            plan['start'].append(cp)
            plan['sends'].append(cp)
            plan['recvs'].append(cp)
    return plan


def _split_start(plan_fn, tensors, land_shapes, n_sems, name, after=None):
    n = len(tensors)
    extra = [] if after is None else [after]

    def body(*refs):
        ins, lands = refs[:n], refs[n:2 * n]
        plan = plan_fn(ins, lands, *refs[2 * n + len(extra):2 * n + len(extra) + 3])
        for cp in plan['start']:
            cp.start()
        refs[-1][...] = jnp.zeros_like(refs[-1])

    outs = pl.pallas_call(
        body, name=name,
        out_shape=(pltpu.SemaphoreType.DMA((n_sems,)), pltpu.SemaphoreType.DMA((n_sems,)), pltpu.SemaphoreType.DMA((n,)),
                   *[pltpu.HBM(t.shape, t.dtype) for t in tensors],
                   *[pltpu.HBM(s, t.dtype) for s, t in zip(land_shapes, tensors)],
                   jax.ShapeDtypeStruct((8, LANES), F32)),
        in_specs=[_HBM] * (2 * n) + [pl.BlockSpec(memory_space=pl.ANY)] * len(extra),
        out_specs=(_SEM, _SEM, _SEM, *[_HBM] * (2 * n), pl.BlockSpec(memory_space=pltpu.VMEM)),
        input_output_aliases={t: 3 + t for t in range(2 * n)},
        compiler_params=pltpu.CompilerParams(has_side_effects=_EFFECT),
    )(*[pltpu.with_memory_space_constraint(t, pltpu.HBM) for t in tensors],
      *[pltpu.with_memory_space_constraint(lax.empty(s, t.dtype), pltpu.HBM) for s, t in zip(land_shapes, tensors)], *extra)
    return outs[:-1], outs[-1]


def _split_relay(plan_fn, state, after, name):
    sems, thru = state[:3], state[3:]
    n = len(thru) // 2

    def arrived(*refs):
        plan = plan_fn(refs[:n], refs[n:2 * n], *refs[2 * n:2 * n + 3])
        for cp in plan['relay_wait']:
            cp.wait_recv()

    thru = pl.pallas_call(
        arrived, name=name + "_arrived",
        out_shape=tuple(pltpu.HBM(t.shape, t.dtype) for t in thru),
        in_specs=[_HBM] * (2 * n) + [_SEM, _SEM, _SEM, pl.BlockSpec(memory_space=pl.ANY)],
        out_specs=tuple([_HBM] * (2 * n)),
        input_output_aliases={t: t for t in range(2 * n)},
        compiler_params=pltpu.CompilerParams(has_side_effects=_EFFECT),
    )(*thru, *sems, after)

    def forward(*refs):
        plan = plan_fn(refs[:n], refs[n:2 * n], *refs[2 * n:2 * n + 3])
        for cp in plan['relay_start']:
            cp.start()
        refs[-1][...] = jnp.zeros_like(refs[-1])

    outs = pl.pallas_call(
        forward, name=name + "_forward",
        out_shape=(*[pltpu.HBM(t.shape, t.dtype) for t in thru], jax.ShapeDtypeStruct((8, LANES), F32)),
        in_specs=[_HBM] * (2 * n) + [_SEM, _SEM, _SEM],
        out_specs=(*[_HBM] * (2 * n), pl.BlockSpec(memory_space=pltpu.VMEM)),
        input_output_aliases={t: t for t in range(2 * n)},
        compiler_params=pltpu.CompilerParams(has_side_effects=_EFFECT),
    )(*thru, *sems)
    return (*sems, *outs[:-1]), outs[-1]


def _split_wait(plan_fn, state, after, name, with_sources=False):
    sems, thru = state[:3], state[3:]
    n = len(thru) // 2

    def body(*refs):
        plan = plan_fn(refs[:n], refs[n:2 * n], *refs[2 * n:2 * n + 3])
        for cp in plan['local']:
            cp.wait()
        for cp in plan['sends']:
            cp.wait_send()
        for cp in plan['recvs']:
            cp.wait_recv()

    outs = pl.pallas_call(
        body, name=name,
        out_shape=tuple(pltpu.HBM(t.shape, t.dtype) for t in thru),
        in_specs=[_HBM] * (2 * n) + [_SEM, _SEM, _SEM, pl.BlockSpec(memory_space=pl.ANY)],
        out_specs=tuple([_HBM] * (2 * n)),
        input_output_aliases={t: t for t in range(2 * n)},
        compiler_params=pltpu.CompilerParams(has_side_effects=_EFFECT),
    )(*thru, *sems, after)
    return (list(outs[:n]), list(outs[n:])) if with_sources else list(outs[n:])


PAIR_SUM_BLOCK = 512 * 1024


def _pair_sum(mine, theirs, name):
    _, r, c = mine.shape
    rows = r
    while rows * c > PAIR_SUM_BLOCK and rows % 32 == 0:
        rows //= 2

    def body(core_ref, a_ref, b_ref, o_ref):
        o_ref[0] = (a_ref[0].astype(F32) + b_ref[0].astype(F32)).astype(o_ref.dtype)

    return pl.pallas_call(
        body, name=name,
        grid_spec=pltpu.PrefetchScalarGridSpec(
            num_scalar_prefetch=1, grid=(N_CHIP, r // rows),
            in_specs=[pl.BlockSpec((1, rows, c), lambda k, i, core: (2 * k + core[0], i, 0)),
                      pl.BlockSpec((1, rows, c), lambda k, i, core: (k, i, 0))],
            out_specs=pl.BlockSpec((1, rows, c), lambda k, i, core: (k, i, 0))),
        out_shape=jax.ShapeDtypeStruct((N_CHIP, r, c), mine.dtype),
        compiler_params=_cparams(("parallel", "parallel")),
    )(lax.axis_index("c").astype(jnp.int32).reshape(1), mine, theirs)


WEIGHTS = ['norm_w', 'w_in', 's5_lambda_re', 's5_lambda_im', 's5_b_re', 's5_b_im', 's5_c_re', 's5_c_im', 's5_d',
           's5_log_step', 's5_w_glu', 'sgu_ln_w', 'sgu_ln_b', 'sgu_w', 'sgu_b', 'm2_conv_w', 'm2_conv_b', 'm2_dt_bias',
           'm2_a_log', 'm2_d', 'm2_norm_w', 'sc_conv_w', 'merge_b', 'w_branch', 'w_out', 'final_norm_w']
BIG_SHARDED = ['w_in', 'w_branch', 'w_out', 's5_w_glu']
SMALL_SHARDED = ['m2_conv_w', 'sc_conv_w', 'merge_b']
REPLICATED = [n for n in WEIGHTS if n not in BIG_SHARDED + SMALL_SHARDED]
S5_NAMES = ['s5_lambda_re', 's5_lambda_im', 's5_b_re', 's5_b_im', 's5_c_re', 's5_c_im', 's5_d', 's5_log_step']


def _sc_interleave(t):
    lead = t.shape[:-1]
    return jnp.swapaxes(t.reshape(lead + (4, 4, LANES)), -3, -2).reshape(lead + (4 * BW,))


def _pad_in(w):
    z = lambda n: jnp.zeros(w.shape[:-1] + (n,), w.dtype)
    return jnp.concatenate([w[..., 6152:], w[..., 0:1024], w[..., 3072:4096], w[..., 1024:2560], z(512),
                            w[..., 2560:3072], w[..., 4096:4104], z(504), _sc_interleave(w[..., 4104:6152])], axis=-1)


def _unpad_in(g):
    return jnp.concatenate([g[..., C_S5U:C_S5U + 1024], g[..., C_SGU_U:C_SGU_U + 1536], g[..., C_M2Z:C_M2Z + 512],
                            g[..., C_M2X:C_M2X + 1024], g[..., C_DT:C_DT + 8], _sc_interleave(g[..., C_SC:]),
                            g[..., :N_BRANCH * D_MODEL]], axis=-1)


ROW_BLOCK = 8 * LANES


def _pack_rows(tensors, row_mult, batched=False):
    parts = []
    for t in tensors:
        f = t.reshape((t.shape[0], -1) if batched else (1, -1))
        f = jnp.pad(f, ((0, 0), (0, (-f.shape[1]) % ROW_BLOCK)))
        parts.append(f.reshape(f.shape[0], -1, LANES))
    out = jnp.concatenate(parts, axis=1)
    out = jnp.pad(out, ((0, 0), (0, (-out.shape[1]) % row_mult), (0, 0)))
    return out if batched else out[0]


def _unpack_rows(rows, shapes):
    out, r0 = [], 0
    for shp in shapes:
        size = 1
        for s in shp:
            size *= s
        nr = -(-size // ROW_BLOCK) * 8
        out.append(rows[r0:r0 + nr].reshape(-1)[:size].reshape(shp))
        r0 += nr
    return out


def _kernel_col_map():
    m = np.full(IN_PAD, -1, np.int64)
    m[C_MERGE:C_MERGE + 4096] = np.arange(6152, 10248)
    m[C_S5U:C_S5U + 1024] = np.arange(0, 1024)
    m[C_M2X:C_M2X + 1024] = np.arange(3072, 4096)
    m[C_SGU_U:C_SGU_U + 1536] = np.arange(1024, 2560)
    m[C_M2Z:C_M2Z + 512] = np.arange(2560, 3072)
    m[C_DT:C_DT + 8] = np.arange(4096, 4104)
    for j in range(4):
        for kind in range(4):
            k0 = C_SC + 4 * LANES * j + LANES * kind
            m[k0:k0 + LANES] = 4104 + BW * kind + LANES * j + np.arange(LANES)
    return m


def _lane_pieces(sources):
    pieces, cur = [], None
    for lane, src in enumerate(sources):
        key = None if src is None else (src[0], src[1] // LANES, (lane - src[1]) % LANES)
        if cur is not None and key == cur[0]:
            cur[2] = lane + 1
        else:
            if cur is not None and cur[0] is not None:
                pieces.append((*cur[0], cur[1], cur[2]))
            cur = [key, lane, lane + 1]
    if cur is not None and cur[0] is not None:
        pieces.append((*cur[0], cur[1], cur[2]))
    return pieces


def _assemble_block(pieces, load, rows, dtype):
    lane = lax.broadcasted_iota(jnp.int32, (rows, LANES), 1)
    out = None
    for arr, sb, shift, lo, hi in pieces:
        v = load(arr, sb)
        if shift:
            v = pltpu.roll(v, shift, 1)
        if out is None and lo == 0 and hi == LANES:
            out = v
        else:
            out = jnp.where((lane >= lo) & (lane < hi), v, jnp.zeros((rows, LANES), dtype) if out is None else out)
    return jnp.zeros((rows, LANES), dtype) if out is None else out


RELAYOUT_ROWS = 256
SHARD_BLOCKS = -(-SHARD_IN // LANES)


def _load_shard_block(ref, rows):
    def load(j, sb):
        if sb == SHARD_BLOCKS - 1:
            return jnp.broadcast_to(ref[j, :, SHARD_IN - 1:SHARD_IN], (rows, LANES))
        return ref[j, :, sb * LANES:(sb + 1) * LANES]
    return load


def _relayout_w_in(gathered, name):
    kmap = _kernel_col_map()
    dtype = gathered.dtype

    def body(src_ref, o_ref):
        load = _load_shard_block(src_ref, RELAYOUT_ROWS)
        for ob in range(IN_PAD // LANES):
            srcs = [None if kmap[ob * LANES + l] < 0 else (int(kmap[ob * LANES + l]) // SHARD_IN, int(kmap[ob * LANES + l]) % SHARD_IN)
                    for l in range(LANES)]
            o_ref[:, ob * LANES:(ob + 1) * LANES] = _assemble_block(_lane_pieces(srcs), load, RELAYOUT_ROWS, dtype)

    return pl.pallas_call(
        body, name=name, grid=(D_MODEL // RELAYOUT_ROWS,),
        in_specs=[pl.BlockSpec((N_DEV, RELAYOUT_ROWS, SHARD_IN), lambda i: (0, i, 0))],
        out_specs=pl.BlockSpec((RELAYOUT_ROWS, IN_PAD), lambda i: (i, 0)),
        out_shape=jax.ShapeDtypeStruct((D_MODEL, IN_PAD), dtype),
        compiler_params=_cparams(("parallel",)),
    )(gathered)


def _relayout_g_in(gw, name):
    kmap = _kernel_col_map()
    kinv = np.zeros(IN_DIM, np.int64)
    kinv[kmap[kmap >= 0]] = np.nonzero(kmap >= 0)[0]
    dtype = gw.dtype

    def body(src_ref, o_ref):
        load = lambda _, sb: src_ref[:, sb * LANES:(sb + 1) * LANES]
        for j in range(N_DEV):
            for ob in range(SHARD_BLOCKS):
                srcs = [(0, int(kinv[SHARD_IN * j + ob * LANES + l])) if ob * LANES + l < SHARD_IN else None for l in range(LANES)]
                blk = _assemble_block(_lane_pieces(srcs), load, RELAYOUT_ROWS, dtype)
                if ob == SHARD_BLOCKS - 1:
                    o_ref[j, :, SHARD_IN - 1:SHARD_IN] = blk[:, 0:1]
                else:
                    o_ref[j, :, ob * LANES:(ob + 1) * LANES] = blk

    return pl.pallas_call(
        body, name=name, grid=(D_MODEL // RELAYOUT_ROWS,),
        in_specs=[pl.BlockSpec((RELAYOUT_ROWS, IN_PAD), lambda i: (i, 0))],
        out_specs=pl.BlockSpec((N_DEV, RELAYOUT_ROWS, SHARD_IN), lambda i: (0, i, 0)),
        out_shape=jax.ShapeDtypeStruct((N_DEV, D_MODEL, SHARD_IN), dtype),
        compiler_params=_cparams(("parallel",)),
    )(gw)


def _rows128(flat, row_mult=8):
    n = flat.shape[0]
    per = LANES * row_mult
    total = -(-n // per) * per
    return jnp.pad(flat, (0, total - n)).reshape(total // LANES, LANES)


def _pad_lanes(v):
    return jnp.pad(v, (0, LANES - v.shape[0])).reshape(1, LANES)


def _layer_prep(i, p):
    disc, disc_vjp = jax.vjp(_s5_disc, *[p[n][i] for n in S5_NAMES])
    prep = dict(
        nw=p['norm_w'][i].reshape(1, D_MODEL), disc_vjp=disc_vjp,
        s5small=[t.astype(BF16) for t in disc[:4]] + [disc[4], disc[5]],
        sgw=[p['sgu_ln_w'][i].reshape(1, BW), p['sgu_ln_b'][i].reshape(1, BW), p['sgu_w'][i],
             jnp.repeat(p['sgu_b'][i].T, BW // SGU_HEADS, axis=1)],
        cb=p['m2_conv_b'][i].reshape(1, M2_CONV_CH),
        m2w=[_pad_lanes(p['m2_dt_bias'][i]), _pad_lanes(p['m2_a_log'][i]),
             jnp.repeat(p['m2_d'][i], M2_HEAD_DIM).reshape(1, BW), p['m2_norm_w'][i].reshape(1, BW)])
    touch = [t[0, 0].astype(F32) for t in prep['s5small']] + [prep['sgw'][3][0, 0], prep['m2w'][2][0, 0]]
    return prep, sum(touch[1:], touch[0])


def _layer_fwd(x, h, i, prep, w_in, other_weights, before_merge=None):
    proj = _matmul(h, w_in, 1, 0, F32, 1024, 1024, 1024, f"proj{i}")
    full = dict(other_weights(proj), w_in=w_in)
    s5w = prep['s5small'] + [full['s5_w_glu']]
    ya, sre, sim = _s5_fwd(proj, *s5w, f"s5_fwd{i}")
    yb = _sgu_fwd(proj, *prep['sgw'], f"sgu_fwd{i}")
    cw = full['m2_conv_w']
    xa = _m2_conv_fwd(proj, cw, prep['cb'], f"m2conv_fwd{i}")
    yc, s_in = _ssd_fwd(proj, xa, *prep['m2w'], f"ssd_fwd{i}")
    scw = full['sc_conv_w']
    yd = _sc_fwd(proj, scw, f"sc_fwd{i}")
    ys = jnp.stack([ya, yb, yc, yd])
    mb = full['merge_b'].reshape(N_BRANCH, 1, D_MODEL)
    if before_merge is not None:
        mb = mb + before_merge(ys)[0, 0]
    merged = _merge_fwd(proj, ys, mb, full['w_branch'], f"merge_fwd{i}")
    x_new = _matmul(merged, full['w_out'], 1, 0, F32, 1024, 1024, 1024, f"out{i}", residual=x)
    saved = dict(x=x, nw=prep['nw'], h=h, proj=proj, disc_vjp=prep['disc_vjp'], s5w=s5w, sre=sre, sim=sim, sgw=prep['sgw'],
                 cw=cw, cb=prep['cb'], xa=xa, m2w=prep['m2w'], s_in=s_in, scw=scw, ys=ys, mb=mb, merged=merged)
    return x_new, saved, full


def _layer_bwd(dx_out, i, sv, full, on_large_grads=None, after_dh=None):
    g = {}
    proj = sv['proj']
    dm = _matmul(dx_out, full['w_out'], 1, 1, F32, 1024, 1024, 1024, f"dmerged{i}")
    g['w_out'] = _matmul(sv['merged'], dx_out, 0, 0, BF16, 1024, 1024, 1024, f"gw_out{i}")
    dys, dproj, g['w_branch'], dmb = _merge_bwd(proj, sv['ys'], dm, sv['mb'], full['w_branch'], f"merge_bwd{i}")
    g['merge_b'] = dmb.reshape(N_BRANCH, D_MODEL)
    dproj, dbbre, dbbim, dcre, dcim, da, dd, dwg = _s5_bwd(proj, dproj, dys, sv['sre'], sv['sim'], *sv['s5w'], f"s5_bwd{i}")
    g['s5_dense'] = (dbbre, dbbim, dcre, dcim, da, dd)
    g['s5_w_glu'] = dwg.astype(BF16)
    dproj, dlw, dlb, g['sgu_w'], dbias = _sgu_bwd(proj, dproj, dys, *sv['sgw'], f"sgu_bwd{i}")
    g['sgu_ln_w'], g['sgu_ln_b'] = dlw[0], dlb[0]
    g['sgu_b'] = dbias.reshape(SGU_CHUNK, SGU_HEADS, BW // SGU_HEADS).sum(-1).T
    dproj, dxa, ddtb, dal, ddf, dnw = _ssd_bwd(proj, dproj, sv['xa'], dys, sv['s_in'], *sv['m2w'], f"ssd_bwd{i}")
    dproj, g['m2_conv_w'], dcb = _m2_conv_bwd(proj, dproj, dxa, sv['cw'], sv['cb'], f"m2conv_bwd{i}")
    g['m2_conv_b'], g['m2_norm_w'] = dcb[0], dnw[0]
    g['m2_dt_bias'], g['m2_a_log'] = ddtb[0, :M2_HEADS], dal[0, :M2_HEADS]
    g['m2_d'] = ddf.reshape(M2_HEADS, M2_HEAD_DIM).sum(-1)
    dproj, g['sc_conv_w'] = _sc_bwd(proj, dproj, dys, sv['scw'], f"sc_bwd{i}")
    g['w_in'] = _matmul(sv['h'], dproj, 0, 0, BF16, 1024, 1024, 1024, f"gw_in{i}")
    tok = on_large_grads(g) if on_large_grads else None
    dh = _matmul(dproj, full['w_in'], 1, 1, F32, 1024, 1024, 1024, f"dh{i}", after=tok)
    nw = sv['nw'] if after_dh is None else sv['nw'] + after_dh(dh)[0, 0]
    dx_in, dnw_l = _rmsnorm_bwd(sv['x'], nw, dh, dx_out, f"rms_bwd{i}")
    g['norm_w'] = dnw_l[0]
    return dx_in, g


def _split8(t, axis):
    shp = t.shape
    t = t.reshape(shp[:axis] + (N_DEV, shp[axis] // N_DEV) + shp[axis + 1:])
    return jnp.moveaxis(t, axis, 0)


def _join8(t, axis):
    t = jnp.moveaxis(t, 0, axis)
    shp = t.shape
    return t.reshape(shp[:axis] + (shp[axis] * shp[axis + 1],) + shp[axis + 2:])


SHARD_AXIS = {'w_in': 2, 'w_branch': 3, 'w_out': 1, 's5_w_glu': 1, 'm2_conv_w': 2, 'sc_conv_w': 2, 'merge_b': 2}


OTHER_BIG = [n for n in BIG_SHARDED if n != 'w_in']


def _other_weights(gathered):
    return {n: _join8(t, SHARD_AXIS[n] - 1) for n, t in zip(OTHER_BIG, gathered)}


def _layer_grad_blocks(g, i):
    blocks = [_relayout_g_in(g[n], f"relayout_g_in{i}") if n == 'w_in' else _split8(g[n], SHARD_AXIS[n] - 1) for n in BIG_SHARDED]
    return [b.reshape(N_DEV, -1, b.shape[-1]) for b in blocks]


def _pair_start(blocks, i):
    shapes = [(N_CHIP,) + b.shape[1:] for b in blocks]
    return _split_start(_plan_pair, blocks, shapes, N_CHIP * len(blocks), f"pair{i}_start")


def _pair_sums(state, after, i):
    mine, theirs = _split_wait(_plan_pair, state, after, f"pair{i}_wait", with_sources=True)
    return [_pair_sum(b, t, f"pair_sum{i}_{k}") for k, (b, t) in enumerate(zip(mine, theirs))]


def _chips_start(sums, i, after=None):
    return _split_start(_plan_chips, sums, [s.shape for s in sums], 3 * len(sums), f"chips{i}_start", after)


def kernel(x, norm_w, w_in, s5_lambda_re, s5_lambda_im, s5_b_re, s5_b_im, s5_c_re, s5_c_im, s5_d, s5_log_step, s5_w_glu, sgu_ln_w, sgu_ln_b, sgu_w, sgu_b, m2_conv_w, m2_conv_b, m2_dt_bias, m2_a_log, m2_d, m2_norm_w, sc_conv_w, merge_b, w_branch, w_out, final_norm_w, loss_target, m_norm_w, m_w_in, m_s5_lambda_re, m_s5_lambda_im, m_s5_b_re, m_s5_b_im, m_s5_c_re, m_s5_c_im, m_s5_d, m_s5_log_step, m_s5_w_glu, m_sgu_ln_w, m_sgu_ln_b, m_sgu_w, m_sgu_b, m_m2_conv_w, m_m2_conv_b, m_m2_dt_bias, m_m2_a_log, m_m2_d, m_m2_norm_w, m_sc_conv_w, m_merge_b, m_w_branch, m_w_out, m_final_norm_w, v_norm_w, v_w_in, v_s5_lambda_re, v_s5_lambda_im, v_s5_b_re, v_s5_b_im, v_s5_c_re, v_s5_c_im, v_s5_d, v_s5_log_step, v_s5_w_glu, v_sgu_ln_w, v_sgu_ln_b, v_sgu_w, v_sgu_b, v_m2_conv_w, v_m2_conv_b, v_m2_dt_bias, v_m2_a_log, v_m2_d, v_m2_norm_w, v_sc_conv_w, v_merge_b, v_w_branch, v_w_out, v_final_norm_w):
    loc = locals()
    p = {n: loc[n] for n in WEIGHTS}
    mom = {n: loc['m_' + n] for n in WEIGHTS}
    vel = {n: loc['v_' + n] for n in WEIGHTS}

    small_sizes = [p[n].size for n in SMALL_SHARDED]
    small_pack = _rows128(jnp.concatenate([p[n].reshape(-1) for n in SMALL_SHARDED]))
    shards = ([p['w_in'][0].astype(BF16)] + [p[n][0].astype(BF16) for n in OTHER_BIG] + [small_pack]
              + [p[n][1].astype(BF16) for n in BIG_SHARDED])
    gath, tok = _split_start(_plan_gather, shards, [(N_DEV,) + t.shape for t in shards], 7 * len(shards), "gather_start")
    sems, srcs, lands = gath[:3], gath[3:3 + len(shards)], gath[3 + len(shards):]

    def relayed(lo, hi, after, name):
        plan = functools.partial(_plan_gather, first=lo)
        state, tok = _split_relay(plan, (*sems, *srcs[lo:hi], *lands[lo:hi]), after, name + "_relay")
        return (plan, state, name), tok

    def arrived(relay, after):
        plan, state, name = relay
        return _split_wait(plan, state, after, name + "_wait")

    def gathered(lo, hi, after, name):
        relay, tok = relayed(lo, hi, after, name)
        return arrived(relay, tok)

    later = dict(p, **{n: p[n] + tok[0, 0] for n in ('norm_w', 's5_log_step', 'sgu_b', 'm2_d')})
    preps = [_layer_prep(i, later) for i in range(DEPTH)]
    h0 = _rmsnorm_fwd(x[0], preps[0][0]['nw'], "rms_fwd0")
    packed = [jnp.concatenate([_pack_rows([d[n] for n in SMALL_SHARDED], 8), _pack_rows([d[n] for n in REPLICATED], 8 * N_DEV)],
                              axis=0) + tok[0, 0] for d in (p, mom, vel)]
    ready = preps[0][1] + preps[1][1] + h0[0, 0].astype(F32) + (packed[0][0, 0] + packed[1][0, 0] + packed[2][0, 0])
    got = gathered(0, 1, tok + ready, "gather_w_in0")
    small_full = {}

    def other_weights0(proj):
        got = gathered(1, 5, proj, "gather_rest0")
        small_all, off = got[-1].reshape(N_DEV, -1), 0
        for n, sz in zip(SMALL_SHARDED, small_sizes):
            small_full[n] = _join8(small_all[:, off:off + sz].reshape((N_DEV,) + p[n].shape), SHARD_AXIS[n])
            off += sz
        return dict(_other_weights(got[:-1]), **{n: small_full[n][0] for n in SMALL_SHARDED})

    saved, layer_g, full = [None] * DEPTH, [None] * DEPTH, [None] * DEPTH
    relay1 = []

    def relay_layer1(ys):
        relay, tok = relayed(5, 9, ys, "gather1")
        relay1.append(relay)
        return tok

    xs, saved[0], full[0] = _layer_fwd(x[0], h0, 0, preps[0][0], _relayout_w_in(got[0], "relayout_w_in0"), other_weights0,
                                       relay_layer1)
    h1 = _rmsnorm_fwd(xs, preps[1][0]['nw'], "rms_fwd1")
    got = arrived(relay1[0], h1)
    xs, saved[1], full[1] = _layer_fwd(
        xs, h1, 1, preps[1][0], _relayout_w_in(got[0], "relayout_w_in1"),
        lambda proj: dict(_other_weights(got[1:]), **{n: small_full[n][1] for n in SMALL_SHARDED}))
    loss_row, dx, dfw = _loss_head(xs, final_norm_w.reshape(1, D_MODEL), loss_target[0])
    loss = lax.psum(loss_row[0, 0], ("x", "y", "c"))
    pairs, scat = [None] * DEPTH, [None] * DEPTH

    def start_pairs(i):
        def start(g):
            pairs[i], tok = _pair_start(_layer_grad_blocks(g, i), i)
            return tok
        return start

    def send_chip_sums1(dh):
        scat[1], tok = _chips_start(_pair_sums(pairs[1], dh, 1), 1)
        return tok

    dx, layer_g[1] = _layer_bwd(dx, 1, saved[1], full[1], start_pairs(1), send_chip_sums1)
    dx, layer_g[0] = _layer_bwd(dx, 0, saved[0], full[0], start_pairs(0))
    scat[0], tok = _chips_start(_pair_sums(pairs[0], dx, 0), 0)
    for i in range(DEPTH):
        dense = layer_g[i].pop('s5_dense')
        dense = dense[:4] + (dense[4] + tok[0, 0], dense[5])
        layer_g[i].update(zip(S5_NAMES, saved[i]['disc_vjp'](dense)))
    grads = {n: jnp.stack([layer_g[i][n] for i in range(DEPTH)]) for n in SMALL_SHARDED + REPLICATED if n != 'final_norm_w'}
    grads['final_norm_w'] = dfw[0]

    out_g, out_d, out_m, out_v = {}, {}, {}, {}
    repl_rows = _pack_rows([grads[n] for n in REPLICATED], 8 * N_DEV)
    rr = repl_rows.shape[0] // N_DEV
    shard_rows = _pack_rows([_split8(grads[n], SHARD_AXIS[n]) for n in SMALL_SHARDED], 8, batched=True)
    rs = shard_rows.shape[1]
    small_g = jnp.concatenate([shard_rows, repl_rows.reshape(N_DEV, rr, LANES)], axis=1)
    small_sum = _slot_sum(_exchange([small_g], False, "scatter_small")[0], "sum_small")
    repl_all = _exchange([small_sum[rs:]], True, "gather_small")[0].reshape(N_DEV * rr, LANES)
    g_all = jnp.concatenate([small_sum[:rs], repl_all], axis=0)
    names = SMALL_SHARDED + REPLICATED
    res = _adamw([g_all[None]], *[t[None] for t in packed], g_all.shape[0], "adamw_small")
    for o, dst in zip(res, (out_g, out_d, out_m, out_v)):
        pieces = (_unpack_rows(o[0, :rs], [p[n].shape for n in SMALL_SHARDED])
                  + _unpack_rows(o[0, rs:], [p[n].shape for n in REPLICATED]))
        dst.update(zip(names, pieces))

    landed1 = _split_wait(_plan_chips, scat[1], res[0], "chips1_wait")
    landed0 = _split_wait(_plan_chips, scat[0], landed1[0], "chips0_wait")
    for k, n in enumerate(BIG_SHARDED):
        shp = p[n].shape
        c = shp[-1]
        r = p[n].size // (DEPTH * c)
        if n == 'w_in':
            big = _adamw_w_in([landed0[k], landed1[k]], p[n], mom[n], vel[n], "adamw_w_in")
        else:
            big = _adamw([landed0[k], landed1[k]], *[d[n].reshape(DEPTH, r, c) for d in (p, mom, vel)],
                         {'w_branch': 512, 'w_out': 128, 's5_w_glu': 64}[n], "adamw_" + n)
        out_g[n], out_d[n], out_m[n], out_v[n] = [o.reshape(shp) for o in big]
    return (loss, dx[None], *[out_g[n] for n in WEIGHTS], *[out_d[n] for n in WEIGHTS],
            *[out_m[n] for n in WEIGHTS], *[out_v[n] for n in WEIGHTS])
```

```python
import functools

import jax
import jax.numpy as jnp
import numpy as np
from jax import lax
from jax.experimental import pallas as pl
from jax.experimental.pallas import tpu as pltpu

F32 = jnp.float32
BF16 = jnp.bfloat16

N_DEV = 8
SEQ = 2048
D_MODEL = 1024
DEPTH = 2
BW = 512
N_BRANCH = 4
EPS = 1e-6
S5_GROUPS, S5_STATE, S5_P = 32, 64, 16
S5_CH = S5_GROUPS * S5_STATE
SGU_CHUNK, SGU_HEADS = 128, 8
M2_HEADS, M2_HEAD_DIM, M2_STATE, M2_CHUNK, M2_CONV = 8, 64, 128, 128, 4
M2_CONV_CH = 1024
SC_CONV = 3
IN_DIM = 10248
IN_PAD = 11264
C_MERGE = 0
C_S5U, C_S5G = 4096, 4608
C_M2X = 5120
C_SGU_U, C_SGU_V, C_SGU_G = 6144, 6656, 7168
C_M2Z, C_DT = 8192, 8704
C_SC = 9216
SHARD_IN = IN_DIM // N_DEV

ADAM_LR, ADAM_B1, ADAM_B2, ADAM_EPS, ADAM_WD, ADAM_STEP = 0.001, 0.9, 0.999, 1e-08, 0.01, 10

VMEM_LIMIT = 56 * 1024 * 1024
LANES = 128

MESH = pl.DeviceIdType.MESH


def _cparams(sem=None, **kw):
    return pltpu.CompilerParams(dimension_semantics=sem, vmem_limit_bytes=VMEM_LIMIT, **kw)


def _dg(a, b, ca, cb, precision=None):
    return lax.dot_general(a, b, (((ca,), (cb,)), ((), ())), precision=precision,
                           preferred_element_type=F32)


@functools.partial(jax.custom_vjp, nondiff_argnums=(2, 3))
def _bdot(a, b, ca, cb):
    return _dg(a.astype(BF16), b.astype(BF16), ca, cb)


def _bdot_fwd(a, b, ca, cb):
    return _bdot(a, b, ca, cb), (a, b)


def _bdot_bwd(ca, cb, res, g):
    a, b = res
    gb, ab, bb = g.astype(BF16), a.astype(BF16), b.astype(BF16)
    da = _dg(gb, bb, 1, 1 - cb) if ca == 1 else _dg(bb, gb, 1 - cb, 1)
    db = _dg(ab, gb, 1 - ca, 0) if cb == 0 else _dg(gb, ab, 0, 1 - ca)
    return da.astype(a.dtype), db.astype(b.dtype)


_bdot.defvjp(_bdot_fwd, _bdot_bwd)


def _rms(x, w):
    return x * lax.rsqrt(jnp.mean(x * x, axis=-1, keepdims=True) + EPS) * w


def _silu(x):
    return x * jax.nn.sigmoid(x)


def _gelu(x):
    return 0.5 * x * (1.0 + jnp.tanh(0.7978845608028654 * (x + 0.044715 * (x * x * x))))


def _softplus(x):
    return jnp.maximum(x, 0.0) + jnp.log1p(jnp.exp(-jnp.abs(x)))


def _shift_down(x, s):
    if s == 0:
        return x
    row = lax.broadcasted_iota(jnp.int32, x.shape, 0)
    return jnp.where(row >= s, pltpu.roll(x, s, 0), 0.0)


def _shift_up(x, s):
    if s == 0:
        return x
    n = x.shape[0]
    row = lax.broadcasted_iota(jnp.int32, x.shape, 0)
    return jnp.where(row < n - s, pltpu.roll(x, n - s, 0), 0.0)


def _matmul(a, b, ca, cb, out_dtype, tm, tn, tk, name, residual=None, after=None):
    m = a.shape[1 - ca]
    k = a.shape[ca]
    n = b.shape[1 - cb]
    assert b.shape[cb] == k and m % tm == 0 and n % tn == 0 and k % tk == 0
    nk = k // tk
    a_spec = pl.BlockSpec((tm, tk), lambda i, j, kk: (i, kk)) if ca == 1 else pl.BlockSpec((tk, tm), lambda i, j, kk: (kk, i))
    b_spec = pl.BlockSpec((tk, tn), lambda i, j, kk: (kk, j)) if cb == 0 else pl.BlockSpec((tn, tk), lambda i, j, kk: (j, kk))
    o_spec = pl.BlockSpec((tm, tn), lambda i, j, kk: (i, j))
    has_res = residual is not None

    def body(*refs):
        refs = refs[:2 + has_res] + refs[2 + has_res + (after is not None):]
        if has_res:
            a_ref, b_ref, r_ref, o_ref, acc = refs
        else:
            a_ref, b_ref, o_ref, acc = refs
        kk = pl.program_id(2)
        part = _dg(a_ref[...].astype(BF16), b_ref[...].astype(BF16), ca, cb)

        @pl.when(kk == 0)
        def _():
            acc[...] = part

        @pl.when(kk > 0)
        def _():
            acc[...] += part

        @pl.when(kk == nk - 1)
        def _():
            r = acc[...]
            if has_res:
                r = r + r_ref[...]
            o_ref[...] = r.astype(out_dtype)

    ins = [a, b] + ([residual] if has_res else []) + ([after] if after is not None else [])
    specs = [a_spec, b_spec] + ([o_spec] if has_res else []) + ([pl.BlockSpec(memory_space=pl.ANY)] if after is not None else [])
    return pl.pallas_call(
        body, name=name, grid=(m // tm, n // tn, nk), in_specs=specs, out_specs=o_spec,
        out_shape=jax.ShapeDtypeStruct((m, n), out_dtype),
        scratch_shapes=[pltpu.VMEM((tm, tn), F32)],
        compiler_params=_cparams(("parallel", "parallel", "arbitrary")),
    )(*ins)


ROW_TILE = 512


def _rmsnorm_fwd(x, w, name):
    def body(x_ref, w_ref, o_ref):
        o_ref[...] = _rms(x_ref[...], w_ref[...]).astype(BF16)

    return pl.pallas_call(
        body, name=name, grid=(SEQ // ROW_TILE,),
        in_specs=[pl.BlockSpec((ROW_TILE, D_MODEL), lambda i: (i, 0)), pl.BlockSpec((1, D_MODEL), lambda i: (0, 0))],
        out_specs=pl.BlockSpec((ROW_TILE, D_MODEL), lambda i: (i, 0)),
        out_shape=jax.ShapeDtypeStruct((SEQ, D_MODEL), BF16),
        compiler_params=_cparams(("parallel",)),
    )(x, w)


def _rmsnorm_bwd(x, w, dh, dres, name):
    def body(x_ref, w_ref, dh_ref, dres_ref, dx_ref, dw_ref):
        _, vjp = jax.vjp(_rms, x_ref[...], w_ref[...])
        dx, dw = vjp(dh_ref[...])
        dx_ref[...] = dx + dres_ref[...]

        @pl.when(pl.program_id(0) == 0)
        def _():
            dw_ref[...] = dw

        @pl.when(pl.program_id(0) > 0)
        def _():
            dw_ref[...] += dw

    tile = pl.BlockSpec((ROW_TILE, D_MODEL), lambda i: (i, 0))
    vec = pl.BlockSpec((1, D_MODEL), lambda i: (0, 0))
    return pl.pallas_call(
        body, name=name, grid=(SEQ // ROW_TILE,),
        in_specs=[tile, vec, tile, tile], out_specs=[tile, vec],
        out_shape=[jax.ShapeDtypeStruct((SEQ, D_MODEL), F32), jax.ShapeDtypeStruct((1, D_MODEL), F32)],
        compiler_params=_cparams(("arbitrary",)),
    )(x, w, dh, dres)


def _loss_head(x, w, target):
    def body(x_ref, w_ref, t_ref, loss_ref, dx_ref, dw_ref):
        tgt = t_ref[...]

        def f(xv, wv):
            err = _rms(xv, wv) - tgt
            return 0.5 * jnp.sum(jnp.mean(err * err, axis=-1))

        loss, vjp = jax.vjp(f, x_ref[...], w_ref[...])
        dx, dw = vjp(jnp.ones((), F32))
        dx_ref[...] = dx
        lrow = jnp.full((1, LANES), loss, F32)

        @pl.when(pl.program_id(0) == 0)
        def _():
            dw_ref[...] = dw
            loss_ref[...] = lrow

        @pl.when(pl.program_id(0) > 0)
        def _():
            dw_ref[...] += dw
            loss_ref[...] += lrow

    tile = pl.BlockSpec((ROW_TILE, D_MODEL), lambda i: (i, 0))
    vec = pl.BlockSpec((1, D_MODEL), lambda i: (0, 0))
    return pl.pallas_call(
        body, name="loss_head", grid=(SEQ // ROW_TILE,),
        in_specs=[tile, vec, tile], out_specs=[pl.BlockSpec((1, LANES), lambda i: (0, 0)), tile, vec],
        out_shape=[jax.ShapeDtypeStruct((1, LANES), F32), jax.ShapeDtypeStruct((SEQ, D_MODEL), F32),
                   jax.ShapeDtypeStruct((1, D_MODEL), F32)],
        compiler_params=_cparams(("arbitrary",)),
    )(x, w, target)


S5_T = 256
S5_BLOCKS = [(slice(j * 256, (j + 1) * 256), slice(j * 1024, (j + 1) * 1024)) for j in range(2)]


def _s5_post(ypre, gate, wglu):
    y = _gelu(ypre)
    y = y * jax.nn.sigmoid(_bdot(y, wglu, 1, 0))
    return y * _silu(gate)


def _s5_fwd(proj, bbre, bbim, cre, cim, a2, dvec, wglu, name):
    def body(u_ref, g_ref, bbre_ref, bbim_ref, cre_ref, cim_ref, a_ref, d_ref, wg_ref, o_ref, sre_ref, sim_ref, st):
        @pl.when(pl.program_id(0) == 0)
        def _():
            st[...] = jnp.zeros_like(st)

        u = u_ref[...]
        ub = u.astype(BF16)
        for us, ss in S5_BLOCKS:
            sre_ref[:, ss] = _dg(ub[:, us], bbre_ref[us, ss], 1, 0)
            sim_ref[:, ss] = _dg(ub[:, us], bbim_ref[us, ss], 1, 0)
        ar, ai = a_ref[0:1, :], a_ref[1:2, :]

        def step(t, carry):
            sr, si = carry
            nr = ar * sr - ai * si + sre_ref[pl.ds(t, 1), :]
            ni = ar * si + ai * sr + sim_ref[pl.ds(t, 1), :]
            sre_ref[pl.ds(t, 1), :] = nr
            sim_ref[pl.ds(t, 1), :] = ni
            return nr, ni

        sr, si = lax.fori_loop(0, S5_T, step, (st[0:1, :], st[1:2, :]), unroll=8)
        st[0:1, :] = sr
        st[1:2, :] = si
        ypre = jnp.concatenate(
            [_dg(sre_ref[:, ss].astype(BF16), cre_ref[ss, us], 1, 0) - _dg(sim_ref[:, ss].astype(BF16), cim_ref[ss, us], 1, 0)
             for us, ss in S5_BLOCKS], axis=1) + d_ref[...] * u
        o_ref[...] = _s5_post(ypre, g_ref[...], wg_ref[...]).astype(BF16)

    full = lambda shape: pl.BlockSpec(shape, lambda c: (0, 0))
    return pl.pallas_call(
        body, name=name, grid=(SEQ // S5_T,),
        in_specs=[pl.BlockSpec((S5_T, BW), lambda c: (c, C_S5U // BW)), pl.BlockSpec((S5_T, BW), lambda c: (c, C_S5G // BW)),
                  full((BW, S5_CH)), full((BW, S5_CH)), full((S5_CH, BW)), full((S5_CH, BW)),
                  full((2, S5_CH)), full((1, BW)), full((BW, BW))],
        out_specs=[pl.BlockSpec((S5_T, BW), lambda c: (c, 0)), pl.BlockSpec((S5_T, S5_CH), lambda c: (c, 0)),
                   pl.BlockSpec((S5_T, S5_CH), lambda c: (c, 0))],
        out_shape=[jax.ShapeDtypeStruct((SEQ, BW), BF16), jax.ShapeDtypeStruct((SEQ, S5_CH), F32),
                   jax.ShapeDtypeStruct((SEQ, S5_CH), F32)],
        scratch_shapes=[pltpu.VMEM((2, S5_CH), F32)],
        compiler_params=_cparams(("arbitrary",)),
    )(proj, proj, bbre, bbim, cre, cim, a2, dvec, wglu)


def _s5_bwd(proj, dproj, dout, sre, sim, bbre, bbim, cre, cim, a2, dvec, wglu, name):
    nc = SEQ // S5_T

    def body(u_ref, g_ref, do_ref, sre_ref, sim_ref, pre_ref, pim_ref, bbre_ref, bbim_ref, cre_ref, cim_ref, a_ref,
             d_ref, wg_ref, dproj_in, dp_ref, dbbre_ref, dbbim_ref, dcre_ref, dcim_ref, da_ref, dd_ref, dwg_ref,
             gre, gim, st):
        c = nc - 1 - pl.program_id(0)

        @pl.when(pl.program_id(0) == 0)
        def _():
            st[...] = jnp.zeros_like(st)
            for r in (dbbre_ref, dbbim_ref, dcre_ref, dcim_ref, da_ref, dd_ref, dwg_ref):
                r[...] = jnp.zeros_like(r)

        u = u_ref[...]
        s_re, s_im = sre_ref[...], sim_ref[...]

        def head(s_res, s_ims, cres, cims, dv, uv, gv, wg):
            ypre = jnp.concatenate([_bdot(sr, cr, 1, 0) - _bdot(si, ci, 1, 0)
                                    for sr, si, cr, ci in zip(s_res, s_ims, cres, cims)], axis=1) + dv * uv
            return _s5_post(ypre, gv, wg)

        _, vjp = jax.vjp(head, [sre_ref[:, ss] for _, ss in S5_BLOCKS], [sim_ref[:, ss] for _, ss in S5_BLOCKS],
                         [cre_ref[ss, us].astype(F32) for us, ss in S5_BLOCKS],
                         [cim_ref[ss, us].astype(F32) for us, ss in S5_BLOCKS],
                         d_ref[...], u, g_ref[...], wg_ref[...].astype(F32))
        ds_res, ds_ims, dcres, dcims, dd, du_d, dgate, dwg = vjp(do_ref[0])
        for k, (us, ss) in enumerate(S5_BLOCKS):
            dcre_ref[ss, us] += dcres[k]
            dcim_ref[ss, us] += dcims[k]
            gre[:, ss] = ds_res[k]
            gim[:, ss] = ds_ims[k]
        dd_ref[...] += dd
        dwg_ref[...] += dwg
        dp_ref[:, BW:] = dgate.astype(BF16)
        ar, ai = a_ref[0:1, :], a_ref[1:2, :]

        def step(i, carry):
            t = S5_T - 1 - i
            gr, gi = carry
            nr = gre[pl.ds(t, 1), :] + gr
            ni = gim[pl.ds(t, 1), :] + gi
            gre[pl.ds(t, 1), :] = nr
            gim[pl.ds(t, 1), :] = ni
            return ar * nr + ai * ni, ar * ni - ai * nr

        gr, gi = lax.fori_loop(0, S5_T, step, (st[0:1, :], st[1:2, :]), unroll=8)
        st[0:1, :] = gr
        st[1:2, :] = gi
        g_re, g_im = gre[...], gim[...]
        first = jnp.where(c > 0, 1.0, 0.0)
        row = lax.broadcasted_iota(jnp.int32, (S5_T, S5_CH), 0)
        p_re = jnp.where(row == 0, pre_ref[7:8, :] * first, pltpu.roll(s_re, 1, 0))
        p_im = jnp.where(row == 0, pim_ref[7:8, :] * first, pltpu.roll(s_im, 1, 0))
        da_ref[0:1, :] += jnp.sum(g_re * p_re + g_im * p_im, axis=0, keepdims=True)
        da_ref[1:2, :] += jnp.sum(g_im * p_re - g_re * p_im, axis=0, keepdims=True)
        ub, grb, gib = u.astype(BF16), g_re.astype(BF16), g_im.astype(BF16)
        du_s = []
        for us, ss in S5_BLOCKS:
            dbbre_ref[us, ss] += _dg(ub[:, us], grb[:, ss], 0, 0)
            dbbim_ref[us, ss] += _dg(ub[:, us], gib[:, ss], 0, 0)
            du_s.append(_dg(grb[:, ss], bbre_ref[us, ss], 1, 1) + _dg(gib[:, ss], bbim_ref[us, ss], 1, 1))
        dp_ref[:, :BW] = (du_d + jnp.concatenate(du_s, axis=1)).astype(BF16)

    full = lambda shape: pl.BlockSpec(shape, lambda i: (0, 0))
    rev = lambda w, col=0: pl.BlockSpec((S5_T, w), lambda i: (nc - 1 - i, col))
    prev = pl.BlockSpec((8, S5_CH), lambda i: (jnp.maximum((nc - 1 - i) * (S5_T // 8) - 1, 0), 0))
    return pl.pallas_call(
        body, name=name, grid=(nc,),
        in_specs=[rev(BW, C_S5U // BW), rev(BW, C_S5G // BW), pl.BlockSpec((1, S5_T, BW), lambda i: (0, nc - 1 - i, 0)),
                  rev(S5_CH), rev(S5_CH), prev, prev,
                  full((BW, S5_CH)), full((BW, S5_CH)), full((S5_CH, BW)), full((S5_CH, BW)),
                  full((2, S5_CH)), full((1, BW)), full((BW, BW)), pl.BlockSpec(memory_space=pl.ANY)],
        out_specs=[rev(2 * BW, C_S5U // (2 * BW)), full((BW, S5_CH)), full((BW, S5_CH)), full((S5_CH, BW)), full((S5_CH, BW)),
                   full((2, S5_CH)), full((1, BW)), full((BW, BW))],
        input_output_aliases={14: 0},
        out_shape=[jax.ShapeDtypeStruct((SEQ, IN_PAD), BF16),
                   jax.ShapeDtypeStruct((BW, S5_CH), F32), jax.ShapeDtypeStruct((BW, S5_CH), F32),
                   jax.ShapeDtypeStruct((S5_CH, BW), F32), jax.ShapeDtypeStruct((S5_CH, BW), F32),
                   jax.ShapeDtypeStruct((2, S5_CH), F32), jax.ShapeDtypeStruct((1, BW), F32),
                   jax.ShapeDtypeStruct((BW, BW), F32)],
        scratch_shapes=[pltpu.VMEM((S5_T, S5_CH), F32), pltpu.VMEM((S5_T, S5_CH), F32), pltpu.VMEM((2, S5_CH), F32)],
        compiler_params=_cparams(("arbitrary",)),
    )(proj, proj, dout, sre, sim, sre, sim, bbre, bbim, cre, cim, a2, dvec, wglu, dproj)


def _s5_disc(lam_re, lam_im, b_re, b_im, c_re, c_im, d, log_step):
    step = jnp.exp(log_step)[:, None]
    mag = jnp.exp(lam_re * step)
    ab_re, ab_im = mag * jnp.cos(lam_im * step), mag * jnp.sin(lam_im * step)
    den = lam_re * lam_re + lam_im * lam_im
    nr = ab_re - 1.0
    coef_re = (nr * lam_re + ab_im * lam_im) / den
    coef_im = (ab_im * lam_re - nr * lam_im) / den
    bb_re = coef_re[..., None] * b_re - coef_im[..., None] * b_im
    bb_im = coef_re[..., None] * b_im + coef_im[..., None] * b_re
    def block_diag(t, rows_per, cols_per):
        wide = jnp.tile(t.reshape(S5_GROUPS * rows_per, cols_per), (1, S5_GROUPS))
        r = lax.broadcasted_iota(jnp.int32, wide.shape, 0) // rows_per
        c = lax.broadcasted_iota(jnp.int32, wide.shape, 1) // cols_per
        return jnp.where(r == c, wide, 0.0)

    bbre = block_diag(jnp.swapaxes(bb_re, 1, 2), S5_P, S5_STATE)
    bbim = block_diag(jnp.swapaxes(bb_im, 1, 2), S5_P, S5_STATE)
    cre = block_diag(jnp.swapaxes(c_re, 1, 2), S5_STATE, S5_P)
    cim = block_diag(jnp.swapaxes(c_im, 1, 2), S5_STATE, S5_P)
    a2 = jnp.stack([ab_re.reshape(-1), ab_im.reshape(-1)])
    return bbre, bbim, cre, cim, a2, d.reshape(1, BW)


def _left_lanes(shape):
    return lax.broadcasted_iota(jnp.int32, shape, 1) < 64


def _sgu_chunk(u, v, gate, ln_w, ln_b, w, bias):
    u32, v32 = _gelu(u), _gelu(v)
    mu = jnp.mean(v32, axis=-1, keepdims=True)
    var = jnp.mean(jnp.square(v32 - mu), axis=-1, keepdims=True)
    vn = (v32 - mu) * lax.rsqrt(var + EPS) * ln_w + ln_b
    t_i = lax.broadcasted_iota(jnp.int32, (SGU_CHUNK, SGU_CHUNK), 0)
    s_i = lax.broadcasted_iota(jnp.int32, (SGU_CHUNK, SGU_CHUNK), 1)
    causal = t_i >= s_i
    left = _left_lanes((SGU_CHUNK, LANES))
    sgate = _silu(gate)
    outs = []
    for j in range(BW // LANES):
        vb = vn[:, j * LANES:(j + 1) * LANES]
        s_blk = (_bdot(jnp.where(causal, w[2 * j], 0.0), jnp.where(left, vb, 0.0), 1, 0)
                 + _bdot(jnp.where(causal, w[2 * j + 1], 0.0), jnp.where(left, 0.0, vb), 1, 0))
        sl = slice(j * LANES, (j + 1) * LANES)
        outs.append(u32[:, sl] * (s_blk + bias[:, sl]) * sgate[:, sl])
    return outs


def _sgu_fwd(proj, ln_w, ln_b, w, bias, name):
    def body(u_ref, v_ref, g_ref, lw_ref, lb_ref, w_ref, b_ref, o_ref):
        outs = _sgu_chunk(u_ref[...], v_ref[...], g_ref[...], lw_ref[...], lb_ref[...], w_ref[...], b_ref[...])
        for j, o in enumerate(outs):
            o_ref[:, j * LANES:(j + 1) * LANES] = o.astype(BF16)

    blk = lambda col: pl.BlockSpec((SGU_CHUNK, BW), lambda c: (c, col // BW))
    vec = pl.BlockSpec((1, BW), lambda c: (0, 0))
    return pl.pallas_call(
        body, name=name, grid=(SEQ // SGU_CHUNK,),
        in_specs=[blk(C_SGU_U), blk(C_SGU_V), blk(C_SGU_G), vec, vec,
                  pl.BlockSpec((SGU_HEADS, SGU_CHUNK, SGU_CHUNK), lambda c: (0, 0, 0)),
                  pl.BlockSpec((SGU_CHUNK, BW), lambda c: (0, 0))],
        out_specs=pl.BlockSpec((SGU_CHUNK, BW), lambda c: (c, 0)),
        out_shape=jax.ShapeDtypeStruct((SEQ, BW), BF16),
        compiler_params=_cparams(("parallel",)),
    )(proj, proj, proj, ln_w, ln_b, w, bias)


def _sgu_bwd(proj, dproj, dout, ln_w, ln_b, w, bias, name):
    def body(u_ref, v_ref, g_ref, do_ref, lw_ref, lb_ref, w_ref, b_ref, dproj_in, dp_ref, dlw_ref, dlb_ref, dw_ref, db_ref):
        _, vjp = jax.vjp(_sgu_chunk, u_ref[...], v_ref[...], g_ref[...], lw_ref[...], lb_ref[...], w_ref[...], b_ref[...])
        do = do_ref[0]
        du, dv, dgate, dlw, dlb, dw, db = vjp([do[:, j * LANES:(j + 1) * LANES] for j in range(BW // LANES)])
        dp_ref[:, 0:BW] = du.astype(BF16)
        dp_ref[:, BW:2 * BW] = dv.astype(BF16)
        dp_ref[:, 2 * BW:3 * BW] = dgate.astype(BF16)
        dp_ref[:, 3 * BW:] = jnp.zeros((SGU_CHUNK, BW), BF16)

        @pl.when(pl.program_id(0) == 0)
        def _():
            dlw_ref[...] = dlw
            dlb_ref[...] = dlb
            dw_ref[...] = dw
            db_ref[...] = db

        @pl.when(pl.program_id(0) > 0)
        def _():
            dlw_ref[...] += dlw
            dlb_ref[...] += dlb
            dw_ref[...] += dw
            db_ref[...] += db

    blk = lambda col: pl.BlockSpec((SGU_CHUNK, BW), lambda c: (c, col // BW))
    vec = pl.BlockSpec((1, BW), lambda c: (0, 0))
    wsp = pl.BlockSpec((SGU_HEADS, SGU_CHUNK, SGU_CHUNK), lambda c: (0, 0, 0))
    bsp = pl.BlockSpec((SGU_CHUNK, BW), lambda c: (0, 0))
    return pl.pallas_call(
        body, name=name, grid=(SEQ // SGU_CHUNK,),
        in_specs=[blk(C_SGU_U), blk(C_SGU_V), blk(C_SGU_G), pl.BlockSpec((1, SGU_CHUNK, BW), lambda c: (1, c, 0)),
                  vec, vec, wsp, bsp, pl.BlockSpec(memory_space=pl.ANY)],
        out_specs=[pl.BlockSpec((SGU_CHUNK, 4 * BW), lambda c: (c, C_SGU_U // (4 * BW))), vec, vec, wsp, bsp],
        input_output_aliases={8: 0},
        out_shape=[jax.ShapeDtypeStruct((SEQ, IN_PAD), BF16), jax.ShapeDtypeStruct((1, BW), F32),
                   jax.ShapeDtypeStruct((1, BW), F32), jax.ShapeDtypeStruct((SGU_HEADS, SGU_CHUNK, SGU_CHUNK), F32),
                   jax.ShapeDtypeStruct((SGU_CHUNK, BW), F32)],
        compiler_params=_cparams(("arbitrary",)),
    )(proj, proj, proj, dout, ln_w, ln_b, w, bias, dproj)


CONV_BLK = 256


def _m2_conv_fwd(proj, w, b, name):
    def body(x_ref, w_ref, b_ref, o_ref):
        x = x_ref[...]
        acc = jnp.zeros_like(x) + b_ref[...]
        for k in range(M2_CONV):
            acc = acc + w_ref[k:k + 1, :] * _shift_down(x, M2_CONV - 1 - k)
        o_ref[...] = _silu(acc)

    return pl.pallas_call(
        body, name=name, grid=(M2_CONV_CH // CONV_BLK,),
        in_specs=[pl.BlockSpec((SEQ, CONV_BLK), lambda j: (0, C_M2X // CONV_BLK + j)),
                  pl.BlockSpec((M2_CONV, CONV_BLK), lambda j: (0, j)), pl.BlockSpec((1, CONV_BLK), lambda j: (0, j))],
        out_specs=pl.BlockSpec((SEQ, CONV_BLK), lambda j: (0, j)),
        out_shape=jax.ShapeDtypeStruct((SEQ, M2_CONV_CH), F32),
        compiler_params=_cparams(("parallel",)),
    )(proj, w, b)


def _m2_conv_bwd(proj, dproj, dxa, w, b, name):
    def body(x_ref, d_ref, w_ref, b_ref, dproj_in, dx_ref, dw_ref, db_ref):
        x = x_ref[...]
        xs = [_shift_down(x, M2_CONV - 1 - k) for k in range(M2_CONV)]
        acc = jnp.zeros_like(x) + b_ref[...]
        for k in range(M2_CONV):
            acc = acc + w_ref[k:k + 1, :] * xs[k]
        sg = jax.nn.sigmoid(acc)
        dacc = d_ref[...] * (sg * (1.0 + acc * (1.0 - sg)))
        dx = jnp.zeros_like(x)
        for k in range(M2_CONV):
            dx = dx + w_ref[k:k + 1, :] * _shift_up(dacc, M2_CONV - 1 - k)
            dw_ref[k:k + 1, :] = jnp.sum(dacc * xs[k], axis=0, keepdims=True)
        dx_ref[...] = dx.astype(BF16)
        db_ref[...] = jnp.sum(dacc, axis=0, keepdims=True)

    return pl.pallas_call(
        body, name=name, grid=(M2_CONV_CH // CONV_BLK,),
        in_specs=[pl.BlockSpec((SEQ, CONV_BLK), lambda j: (0, C_M2X // CONV_BLK + j)),
                  pl.BlockSpec((SEQ, CONV_BLK), lambda j: (0, j)),
                  pl.BlockSpec((M2_CONV, CONV_BLK), lambda j: (0, j)), pl.BlockSpec((1, CONV_BLK), lambda j: (0, j)),
                  pl.BlockSpec(memory_space=pl.ANY)],
        out_specs=[pl.BlockSpec((SEQ, CONV_BLK), lambda j: (0, C_M2X // CONV_BLK + j)),
                   pl.BlockSpec((M2_CONV, CONV_BLK), lambda j: (0, j)), pl.BlockSpec((1, CONV_BLK), lambda j: (0, j))],
        input_output_aliases={4: 0},
        out_shape=[jax.ShapeDtypeStruct((SEQ, IN_PAD), BF16), jax.ShapeDtypeStruct((M2_CONV, M2_CONV_CH), F32),
                   jax.ShapeDtypeStruct((1, M2_CONV_CH), F32)],
        compiler_params=_cparams(("parallel",)),
    )(proj, dxa, w, b, dproj)


N_PAIR = M2_HEADS // 2
HI = lax.Precision.HIGHEST


def _col(a, h):
    lane = lax.broadcasted_iota(jnp.int32, a.shape, 1)
    return jnp.sum(jnp.where(lane == h, a, 0.0), axis=1, keepdims=True)


def _row(a, h):
    sub = lax.broadcasted_iota(jnp.int32, a.shape, 0)
    return jnp.sum(jnp.where(sub == h, a, 0.0), axis=0, keepdims=True)


def _ssd_chunk(xs, bms, cms, dtr, zs, states, dt_bias, a_log, dfs, nws):
    q = M2_CHUNK
    dt = _softplus(dtr + dt_bias)
    da = dt * (-jnp.exp(a_log))
    l_i = lax.broadcasted_iota(jnp.int32, (q, q), 0)
    s_i = lax.broadcasted_iota(jnp.int32, (q, q), 1)
    causal = l_i >= s_i
    tril = jnp.where(causal, 1.0, 0.0)
    a_cs = _dg(tril, da, 1, 0, HI)
    a_cs_t = _dg(da, tril, 0, 1, HI)
    a_end = _row(a_cs, q - 1)
    left = _left_lanes((q, LANES))
    left1 = _left_lanes((1, LANES))
    ys, nexts = [], []
    for j in range(N_PAIR):
        grp = j // 2
        bm, cm = bms[grp], cms[grp]
        h0, h1 = 2 * j, 2 * j + 1
        cb = _bdot(cm, bm, 1, 1)
        xdt = xs[j] * jnp.where(left, _col(dt, h0), _col(dt, h1))
        acs0, acs1 = _col(a_cs, h0), _col(a_cs, h1)
        y = _bdot(cm, states[j], 1, 0) * jnp.where(left, jnp.exp(acs0), jnp.exp(acs1))
        s_new = states[j] * jnp.where(left1, jnp.exp(_col(a_end, h0)), jnp.exp(_col(a_end, h1)))
        for h, acs, xh in ((h0, acs0, jnp.where(left, xdt, 0.0)), (h1, acs1, jnp.where(left, 0.0, xdt))):
            decay = jnp.exp(jnp.where(causal, acs - _row(a_cs_t, h), -jnp.inf))
            y = y + _bdot(cb * decay, xh, 1, 0)
            s_new = s_new + _bdot(bm * jnp.exp(_col(a_end, h) - acs), xh, 0, 0)
        ys.append((y + dfs[j] * xs[j]) * _silu(zs[j]))
        nexts.append(s_new)
    ssq = sum(jnp.sum(y * y, axis=-1, keepdims=True) for y in ys)
    scale = lax.rsqrt(ssq / BW + EPS)
    return [y * scale * nw for y, nw in zip(ys, nws)], nexts


def _blocks(ref, n, width=LANES):
    return [ref[:, j * width:(j + 1) * width] for j in range(n)]


def _ssd_fwd(proj, xa, dt_bias, a_log, dfull, nw, name):
    nc = SEQ // M2_CHUNK

    def body(x_ref, b_ref, c_ref, dt_ref, z_ref, dtb_ref, al_ref, df_ref, nw_ref, o_ref, sin_ref, st):
        @pl.when(pl.program_id(0) == 0)
        def _():
            st[...] = jnp.zeros_like(st)

        states = [st[j] for j in range(N_PAIR)]
        for j in range(N_PAIR):
            sin_ref[0, j] = states[j]
        ys, nexts = _ssd_chunk(_blocks(x_ref, 4), _blocks(b_ref, 2), _blocks(c_ref, 2), dt_ref[...], _blocks(z_ref, 4),
                               states, dtb_ref[...], al_ref[...], _blocks(df_ref, 4), _blocks(nw_ref, 4))
        for j in range(N_PAIR):
            o_ref[:, j * LANES:(j + 1) * LANES] = ys[j].astype(BF16)
            st[j] = nexts[j]

    vec8 = pl.BlockSpec((1, LANES), lambda c: (0, 0))
    vec = pl.BlockSpec((1, BW), lambda c: (0, 0))
    return pl.pallas_call(
        body, name=name, grid=(nc,),
        in_specs=[pl.BlockSpec((M2_CHUNK, BW), lambda c: (c, 0)), pl.BlockSpec((M2_CHUNK, 256), lambda c: (c, 2)),
                  pl.BlockSpec((M2_CHUNK, 256), lambda c: (c, 3)), pl.BlockSpec((M2_CHUNK, LANES), lambda c: (c, C_DT // LANES)),
                  pl.BlockSpec((M2_CHUNK, BW), lambda c: (c, C_M2Z // BW)), vec8, vec8, vec, vec],
        out_specs=[pl.BlockSpec((M2_CHUNK, BW), lambda c: (c, 0)),
                   pl.BlockSpec((1, N_PAIR, M2_STATE, LANES), lambda c: (c, 0, 0, 0))],
        out_shape=[jax.ShapeDtypeStruct((SEQ, BW), BF16), jax.ShapeDtypeStruct((nc, N_PAIR, M2_STATE, LANES), F32)],
        scratch_shapes=[pltpu.VMEM((N_PAIR, M2_STATE, LANES), F32)],
        compiler_params=_cparams(("arbitrary",)),
    )(xa, xa, xa, proj, proj, dt_bias, a_log, dfull, nw)


def _ssd_bwd(proj, dproj, xa, dout, s_in, dt_bias, a_log, dfull, nw, name):
    nc = SEQ // M2_CHUNK

    def body(x_ref, b_ref, c_ref, dt_ref, z_ref, do_ref, sin_ref, dtb_ref, al_ref, df_ref, nw_ref, dproj_in,
             dp_ref, dxa_ref, ddtb_ref, dal_ref, ddf_ref, dnw_ref, dst):
        @pl.when(pl.program_id(0) == 0)
        def _():
            dst[...] = jnp.zeros_like(dst)
            for r in (ddtb_ref, dal_ref, ddf_ref, dnw_ref):
                r[...] = jnp.zeros_like(r)

        states = [sin_ref[0, j] for j in range(N_PAIR)]
        _, vjp = jax.vjp(_ssd_chunk, _blocks(x_ref, 4), _blocks(b_ref, 2), _blocks(c_ref, 2), dt_ref[...],
                         _blocks(z_ref, 4), states, dtb_ref[...], al_ref[...], _blocks(df_ref, 4), _blocks(nw_ref, 4))
        dxs, dbs, dcs, ddt, dzs, dstates, ddtb, dal, ddfs, dnws = vjp(
            ([do_ref[0, :, j * LANES:(j + 1) * LANES] for j in range(N_PAIR)], [dst[j] for j in range(N_PAIR)]))
        for j in range(N_PAIR):
            sl = slice(j * LANES, (j + 1) * LANES)
            dxa_ref[:, sl] = dxs[j]
            dp_ref[:, sl] = dzs[j].astype(BF16)
            dst[j] = dstates[j]
            ddf_ref[:, sl] += ddfs[j]
            dnw_ref[:, sl] += dnws[j]
        for g in range(2):
            dxa_ref[:, BW + g * LANES:BW + (g + 1) * LANES] = dbs[g]
            dxa_ref[:, BW + 256 + g * LANES:BW + 256 + (g + 1) * LANES] = dcs[g]
        dp_ref[:, BW:BW + LANES] = ddt.astype(BF16)
        dp_ref[:, BW + LANES:] = jnp.zeros((M2_CHUNK, 2 * BW - BW - LANES), BF16)
        ddtb_ref[...] += ddtb
        dal_ref[...] += dal

    rev = lambda w, col=0: pl.BlockSpec((M2_CHUNK, w), lambda i: (nc - 1 - i, col))
    vec8 = pl.BlockSpec((1, LANES), lambda i: (0, 0))
    vec = pl.BlockSpec((1, BW), lambda i: (0, 0))
    return pl.pallas_call(
        body, name=name, grid=(nc,),
        in_specs=[rev(BW), rev(256, 2), rev(256, 3), rev(LANES, C_DT // LANES), rev(BW, C_M2Z // BW),
                  pl.BlockSpec((1, M2_CHUNK, BW), lambda i: (2, nc - 1 - i, 0)),
                  pl.BlockSpec((1, N_PAIR, M2_STATE, LANES), lambda i: (nc - 1 - i, 0, 0, 0)), vec8, vec8, vec, vec,
                  pl.BlockSpec(memory_space=pl.ANY)],
        out_specs=[rev(2 * BW, C_M2Z // (2 * BW)), rev(M2_CONV_CH), vec8, vec8, vec, vec],
        input_output_aliases={11: 0},
        out_shape=[jax.ShapeDtypeStruct((SEQ, IN_PAD), BF16), jax.ShapeDtypeStruct((SEQ, M2_CONV_CH), F32),
                   jax.ShapeDtypeStruct((1, LANES), F32), jax.ShapeDtypeStruct((1, LANES), F32),
                   jax.ShapeDtypeStruct((1, BW), F32), jax.ShapeDtypeStruct((1, BW), F32)],
        scratch_shapes=[pltpu.VMEM((N_PAIR, M2_STATE, LANES), F32)],
        compiler_params=_cparams(("arbitrary",)),
    )(xa, xa, xa, proj, proj, dout, s_in, dt_bias, a_log, dfull, nw, dproj)


def _sc_specs():
    col = lambda kind: pl.BlockSpec((SEQ, LANES), lambda j: (0, C_SC // LANES + 4 * j + kind))
    return [col(0), col(1), col(2), col(3)]


def _sc_fwd(proj, w, name):
    def body(b_ref, c_ref, h_ref, g_ref, w_ref, o_ref):
        ch = c_ref[...] * h_ref[...]
        acc = jnp.zeros_like(ch)
        for k in range(SC_CONV):
            acc = acc + w_ref[k:k + 1, :] * _shift_down(ch, SC_CONV - 1 - k)
        o_ref[...] = (b_ref[...] * acc * _silu(g_ref[...])).astype(BF16)

    return pl.pallas_call(
        body, name=name, grid=(BW // LANES,),
        in_specs=_sc_specs() + [pl.BlockSpec((SC_CONV, LANES), lambda j: (0, j))],
        out_specs=pl.BlockSpec((SEQ, LANES), lambda j: (0, j)),
        out_shape=jax.ShapeDtypeStruct((SEQ, BW), BF16),
        compiler_params=_cparams(("parallel",)),
    )(proj, proj, proj, proj, w)


def _sc_bwd(proj, dproj, dout, w, name):
    def body(b_ref, c_ref, h_ref, g_ref, do_ref, w_ref, dproj_in, dp_ref, dw_ref):
        cv, hv, gv = c_ref[...], h_ref[...], g_ref[...]
        ch = cv * hv
        chs = [_shift_down(ch, SC_CONV - 1 - k) for k in range(SC_CONV)]
        acc = jnp.zeros_like(ch)
        for k in range(SC_CONV):
            acc = acc + w_ref[k:k + 1, :] * chs[k]
        sg = jax.nn.sigmoid(gv)
        do = do_ref[0]
        bv = b_ref[...]
        dp_ref[:, 0:LANES] = (do * acc * (gv * sg)).astype(BF16)
        dp_ref[:, 3 * LANES:] = (do * bv * acc * (sg * (1.0 + gv * (1.0 - sg)))).astype(BF16)
        dacc = do * bv * (gv * sg)
        dch = jnp.zeros_like(ch)
        for k in range(SC_CONV):
            dch = dch + w_ref[k:k + 1, :] * _shift_up(dacc, SC_CONV - 1 - k)
            dw_ref[k:k + 1, :] = jnp.sum(dacc * chs[k], axis=0, keepdims=True)
        dp_ref[:, LANES:2 * LANES] = (dch * hv).astype(BF16)
        dp_ref[:, 2 * LANES:3 * LANES] = (dch * cv).astype(BF16)

    wsp = pl.BlockSpec((SC_CONV, LANES), lambda j: (0, j))
    return pl.pallas_call(
        body, name=name, grid=(BW // LANES,),
        in_specs=_sc_specs() + [pl.BlockSpec((1, SEQ, LANES), lambda j: (3, 0, j)), wsp, pl.BlockSpec(memory_space=pl.ANY)],
        out_specs=[pl.BlockSpec((SEQ, 4 * LANES), lambda j: (0, C_SC // (4 * LANES) + j)), wsp],
        input_output_aliases={6: 0},
        out_shape=[jax.ShapeDtypeStruct((SEQ, IN_PAD), BF16), jax.ShapeDtypeStruct((SC_CONV, BW), F32)],
        compiler_params=_cparams(("parallel",)),
    )(proj, proj, proj, proj, dout, w, dproj)


MERGE_T = 256
MERGE_BWD_T = 512


def _merge_fwd(proj, ys, merge_b, w_branch, name):
    def body(y_ref, lg_ref, b_ref, w_ref, o_ref):
        acc = jnp.zeros((MERGE_T, D_MODEL), F32)
        for k in range(N_BRANCH):
            gate = jax.nn.sigmoid(lg_ref[:, k * D_MODEL:(k + 1) * D_MODEL] + b_ref[k])
            acc = acc + gate * _dg(y_ref[k], w_ref[k], 1, 0)
        o_ref[...] = acc.astype(BF16)

    return pl.pallas_call(
        body, name=name, grid=(SEQ // MERGE_T,),
        in_specs=[pl.BlockSpec((N_BRANCH, MERGE_T, BW), lambda i: (0, i, 0)),
                  pl.BlockSpec((MERGE_T, N_BRANCH * D_MODEL), lambda i: (i, C_MERGE // (N_BRANCH * D_MODEL))),
                  pl.BlockSpec((N_BRANCH, 1, D_MODEL), lambda i: (0, 0, 0)),
                  pl.BlockSpec((N_BRANCH, BW, D_MODEL), lambda i: (0, 0, 0))],
        out_specs=pl.BlockSpec((MERGE_T, D_MODEL), lambda i: (i, 0)),
        out_shape=jax.ShapeDtypeStruct((SEQ, D_MODEL), BF16),
        compiler_params=_cparams(("parallel",)),
    )(ys, proj, merge_b, w_branch)


def _merge_bwd(proj, ys, dm, merge_b, w_branch, name):
    nt = SEQ // MERGE_BWD_T

    def body(y_ref, lg_ref, dm_ref, b_ref, w_ref, dy_ref, dlg_ref, dw_ref, db_ref, dw_acc):
        i = pl.program_id(1)
        gate = jax.nn.sigmoid(lg_ref[...] + b_ref[0])
        y = y_ref[0]
        dmv = dm_ref[...]
        dbo = (gate * dmv).astype(BF16)
        dlg = _dg(y, w_ref[0], 1, 0) * dmv * gate * (1.0 - gate)
        dlg_ref[...] = dlg.astype(BF16)
        dy_ref[0] = _dg(dbo, w_ref[0], 1, 1)
        dwp = _dg(y, dbo, 0, 0)
        dbp = jnp.sum(dlg, axis=0, keepdims=True)

        @pl.when(i == 0)
        def _():
            dw_acc[...] = dwp
            db_ref[0] = dbp

        @pl.when(i > 0)
        def _():
            dw_acc[...] += dwp
            db_ref[0] += dbp

        @pl.when(i == nt - 1)
        def _():
            dw_ref[0] = dw_acc[...].astype(BF16)

    return pl.pallas_call(
        body, name=name, grid=(N_BRANCH, nt),
        in_specs=[pl.BlockSpec((1, MERGE_BWD_T, BW), lambda k, i: (k, i, 0)),
                  pl.BlockSpec((MERGE_BWD_T, D_MODEL), lambda k, i: (i, C_MERGE // D_MODEL + k)),
                  pl.BlockSpec((MERGE_BWD_T, D_MODEL), lambda k, i: (i, 0)),
                  pl.BlockSpec((1, 1, D_MODEL), lambda k, i: (k, 0, 0)),
                  pl.BlockSpec((1, BW, D_MODEL), lambda k, i: (k, 0, 0))],
        out_specs=[pl.BlockSpec((1, MERGE_BWD_T, BW), lambda k, i: (k, i, 0)),
                   pl.BlockSpec((MERGE_BWD_T, D_MODEL), lambda k, i: (i, k)),
                   pl.BlockSpec((1, BW, D_MODEL), lambda k, i: (k, 0, 0)),
                   pl.BlockSpec((1, 1, D_MODEL), lambda k, i: (k, 0, 0))],
        out_shape=[jax.ShapeDtypeStruct((N_BRANCH, SEQ, BW), F32), jax.ShapeDtypeStruct((SEQ, IN_PAD), BF16),
                   jax.ShapeDtypeStruct((N_BRANCH, BW, D_MODEL), BF16), jax.ShapeDtypeStruct((N_BRANCH, 1, D_MODEL), F32)],
        scratch_shapes=[pltpu.VMEM((BW, D_MODEL), F32)],
        compiler_params=_cparams(("parallel", "arbitrary")),
    )(ys, proj, dm, merge_b, w_branch)


def _adamw(glist, w, m, v, rows, name):
    nl = len(glist)
    n, r, c = glist[0].shape
    assert w.shape == (nl, r, c) and r % rows == 0
    nb = r // rows

    def body(*refs):
        g_refs = refs[:nl]
        w_ref, m_ref, v_ref, go_ref, d_ref, mo_ref, vo_ref = refs[nl:]
        for layer in range(nl):
            @pl.when(pl.program_id(0) == layer)
            def _(g_ref=g_refs[layer]):
                g = g_ref[0].astype(F32)
                for s in range(1, n):
                    g = g + g_ref[s].astype(F32)
                mn = ADAM_B1 * m_ref[0] + (1.0 - ADAM_B1) * g
                vn = ADAM_B2 * v_ref[0] + (1.0 - ADAM_B2) * jnp.square(g)
                m_hat = mn / (1.0 - ADAM_B1 ** ADAM_STEP)
                v_hat = vn / (1.0 - ADAM_B2 ** ADAM_STEP)
                go_ref[0] = g
                d_ref[0] = -ADAM_LR * (m_hat / (jnp.sqrt(v_hat) + ADAM_EPS) + ADAM_WD * w_ref[0])
                mo_ref[0] = mn
                vo_ref[0] = vn

    def g_spec(layer):
        return pl.BlockSpec((n, rows, c), lambda a, i: (0, jnp.where(a < layer, 0, jnp.where(a == layer, i, nb - 1)), 0))

    blk = pl.BlockSpec((1, rows, c), lambda a, i: (a, i, 0))
    out = jax.ShapeDtypeStruct((nl, r, c), F32)
    return pl.pallas_call(
        body, name=name, grid=(nl, nb),
        in_specs=[g_spec(layer) for layer in range(nl)] + [blk, blk, blk],
        out_specs=[blk, blk, blk, blk], out_shape=[out, out, out, out],
        compiler_params=_cparams(("arbitrary", "arbitrary")),
    )(*glist, w, m, v)


X_ROWS_PER_COL = 2 * (D_MODEL // LANES)


def _w_in_to_x(w):
    t = jnp.transpose(w, (2, 0, 1)).reshape(SHARD_IN, DEPTH, D_MODEL // LANES, LANES)
    return jnp.transpose(t, (0, 2, 1, 3)).reshape(SHARD_IN * X_ROWS_PER_COL, LANES)


def _w_in_from_x(xv):
    t = jnp.transpose(xv.reshape(SHARD_IN, D_MODEL // LANES, DEPTH, LANES), (0, 2, 1, 3))
    return jnp.transpose(t.reshape(SHARD_IN, DEPTH, D_MODEL), (1, 2, 0))


def _adamw_w_in(glist, w, m, v, name):
    n = glist[0].shape[0]
    cols = LANES
    rows = cols * X_ROWS_PER_COL

    def body(g0_ref, g1_ref, w_ref, m_ref, v_ref, go_ref, d_ref, mo_ref, vo_ref):
        for layer, g_ref in enumerate((g0_ref, g1_ref)):
            g = g_ref[0].astype(F32)
            for s in range(1, n):
                g = g + g_ref[s].astype(F32)
            gt = g.T
            for t in range(D_MODEL // LANES):
                sel = (pl.ds(2 * t + layer, cols, stride=X_ROWS_PER_COL), slice(None))
                gs = gt[:, t * LANES:(t + 1) * LANES]
                mn = ADAM_B1 * m_ref[sel] + (1.0 - ADAM_B1) * gs
                vn = ADAM_B2 * v_ref[sel] + (1.0 - ADAM_B2) * jnp.square(gs)
                m_hat = mn / (1.0 - ADAM_B1 ** ADAM_STEP)
                v_hat = vn / (1.0 - ADAM_B2 ** ADAM_STEP)
                go_ref[sel] = gs
                d_ref[sel] = -ADAM_LR * (m_hat / (jnp.sqrt(v_hat) + ADAM_EPS) + ADAM_WD * w_ref[sel])
                mo_ref[sel] = mn
                vo_ref[sel] = vn

    g_spec = pl.BlockSpec((n, D_MODEL, cols), lambda i: (0, 0, i))
    blk = pl.BlockSpec((rows, LANES), lambda i: (i, 0))
    out = jax.ShapeDtypeStruct((SHARD_IN * X_ROWS_PER_COL, LANES), F32)
    res = pl.pallas_call(
        body, name=name, grid=(-(-SHARD_IN // cols),),
        in_specs=[g_spec, g_spec, blk, blk, blk], out_specs=[blk, blk, blk, blk], out_shape=[out, out, out, out],
        compiler_params=_cparams(("parallel",)),
    )(*glist, _w_in_to_x(w), _w_in_to_x(m), _w_in_to_x(v))
    return [_w_in_from_x(o) for o in res]


def _adamw_many(gs, ws, ms, vs, name):
    k = len(gs)

    def body(*refs):
        g_refs, w_refs, m_refs, v_refs = refs[:k], refs[k:2 * k], refs[2 * k:3 * k], refs[3 * k:4 * k]
        d_refs, mo_refs, vo_refs = refs[4 * k:5 * k], refs[5 * k:6 * k], refs[6 * k:7 * k]
        for i in range(k):
            g = g_refs[i][...]
            mn = ADAM_B1 * m_refs[i][...] + (1.0 - ADAM_B1) * g
            vn = ADAM_B2 * v_refs[i][...] + (1.0 - ADAM_B2) * jnp.square(g)
            m_hat = mn / (1.0 - ADAM_B1 ** ADAM_STEP)
            v_hat = vn / (1.0 - ADAM_B2 ** ADAM_STEP)
            d_refs[i][...] = -ADAM_LR * (m_hat / (jnp.sqrt(v_hat) + ADAM_EPS) + ADAM_WD * w_refs[i][...])
            mo_refs[i][...] = mn
            vo_refs[i][...] = vn

    whole = pl.BlockSpec(memory_space=pltpu.VMEM)
    shapes = [jax.ShapeDtypeStruct(w.shape, F32) for w in ws]
    outs = pl.pallas_call(
        body, name=name, in_specs=[whole] * (4 * k), out_specs=[whole] * (3 * k), out_shape=shapes * 3,
        compiler_params=_cparams(None),
    )(*gs, *ws, *ms, *vs)
    return outs[:k], outs[k:2 * k], outs[2 * k:]


MEMORY_ORDER = {'s5_b_re': (0, 1, 3, 2), 's5_b_im': (0, 1, 3, 2), 's5_d': (0, 2, 1), 'sc_conv_w': (1, 0, 2)}


def _memory_view(name, t):
    return jnp.transpose(t, MEMORY_ORDER[name]) if name in MEMORY_ORDER else t


def _slot_sum(gslots, name):
    n, r, c = gslots.shape

    def body(g_ref, o_ref):
        g = g_ref[0]
        for s in range(1, n):
            g = g + g_ref[s]
        o_ref[...] = g

    return pl.pallas_call(
        body, name=name, in_specs=[pl.BlockSpec((n, r, c), lambda: (0, 0, 0))],
        out_specs=pl.BlockSpec((r, c), lambda: (0, 0)), out_shape=jax.ShapeDtypeStruct((r, c), F32),
        compiler_params=_cparams(None),
    )(gslots)


def _me_and_peers():
    x, y, c = lax.axis_index("x"), lax.axis_index("y"), lax.axis_index("c")
    me = 4 * x + 2 * y + c
    peers = []
    for k in range(1, N_DEV):
        px = 1 - x if (k >> 2) & 1 else x
        py = 1 - y if (k >> 1) & 1 else y
        pc = 1 - c if k & 1 else c
        peers.append((4 * px + 2 * py + pc, (px, py, pc)))
    return me, peers


def _exchange(tensors, gather, name):
    n = len(tensors)

    def body(*refs):
        ins, outs = refs[:n], refs[n:2 * n]
        send_sems, recv_sems, local_sems = refs[2 * n:]
        me, peers = _me_and_peers()
        started = []
        for t in range(n):
            own = pltpu.make_async_copy(ins[t] if gather else ins[t].at[me], outs[t].at[me], local_sems.at[t])
            own.start()
            started.append(own)
            for k, (pidx, pos) in enumerate(peers):
                cp = pltpu.make_async_remote_copy(
                    src_ref=ins[t] if gather else ins[t].at[pidx], dst_ref=outs[t].at[me],
                    send_sem=send_sems.at[t, k], recv_sem=recv_sems.at[t, k], device_id=pos, device_id_type=MESH)
                cp.start()
                started.append(cp)
        for cp in started:
            cp.wait()

    any_spec = pl.BlockSpec(memory_space=pl.ANY)
    outs = pl.pallas_call(
        body, name=name, in_specs=[any_spec] * n, out_specs=[any_spec] * n,
        out_shape=[jax.ShapeDtypeStruct(((N_DEV,) + t.shape) if gather else t.shape, t.dtype) for t in tensors],
        scratch_shapes=[pltpu.SemaphoreType.DMA((n, N_DEV - 1)), pltpu.SemaphoreType.DMA((n, N_DEV - 1)),
                        pltpu.SemaphoreType.DMA((n,))],
        compiler_params=pltpu.CompilerParams(has_side_effects=True),
    )(*tensors)
    return list(outs)


_HBM = pl.BlockSpec(memory_space=pltpu.HBM)
_SEM = pl.BlockSpec(memory_space=pltpu.SEMAPHORE)
_EFFECT = pltpu.SideEffectType.DATAFLOW_SIDE_EFFECTING


N_CHIP = N_DEV // 2


def _chip_peers():
    x, y, c = lax.axis_index("x"), lax.axis_index("y"), lax.axis_index("c")
    chips = []
    for d in range(1, N_CHIP):
        px = 1 - x if (d >> 1) & 1 else x
        py = 1 - y if d & 1 else y
        chips.append((2 * px + py, (px, py)))
    return (x, y, c), 2 * x + y, chips


def _plan_gather(ins, lands, send_sems, recv_sems, local_sems, first=0):
    (x, y, c), q, chips = _chip_peers()
    me = 2 * q + c
    plan = dict(start=[], relay_wait=[], relay_start=[], local=[], sends=[], recvs=[])
    for t in range(len(ins)):
        base = (first + t) * 7
        sem = lambda k: dict(send_sem=send_sems.at[base + k], recv_sem=recv_sems.at[base + k], device_id_type=MESH)
        own = pltpu.make_async_copy(ins[t], lands[t].at[me], local_sems.at[first + t])
        to_sib = pltpu.make_async_remote_copy(src_ref=ins[t], dst_ref=lands[t].at[me], device_id=(x, y, 1 - c), **sem(0))
        plan['start'] += [own, to_sib]
        plan['local'].append(own)
        plan['sends'].append(to_sib)
        plan['recvs'].append(to_sib)
        for d, (pq, (px, py)) in enumerate(chips):
            to_chip = pltpu.make_async_remote_copy(src_ref=ins[t], dst_ref=lands[t].at[me], device_id=(px, py, c), **sem(1 + d))
            blk = lands[t].at[2 * pq + c]
            fwd = pltpu.make_async_remote_copy(src_ref=blk, dst_ref=blk, device_id=(x, y, 1 - c), **sem(4 + d))
            plan['start'].append(to_chip)
            plan['relay_wait'].append(to_chip)
            plan['relay_start'].append(fwd)
            plan['sends'] += [to_chip, fwd]
            plan['recvs'].append(fwd)
    return plan


def _plan_pair(ins, lands, send_sems, recv_sems, local_sems):
    (x, y, c), q, chips = _chip_peers()
    plan = dict(start=[], local=[], sends=[], recvs=[])
    for t in range(len(ins)):
        for k in range(N_CHIP):
            cp = pltpu.make_async_remote_copy(
                src_ref=ins[t].at[2 * k + 1 - c], dst_ref=lands[t].at[k], send_sem=send_sems.at[t * N_CHIP + k],
                recv_sem=recv_sems.at[t * N_CHIP + k], device_id=(x, y, 1 - c), device_id_type=MESH)
            plan['start'].append(cp)
            plan['sends'].append(cp)
            plan['recvs'].append(cp)
    return plan


def _plan_chips(ins, lands, send_sems, recv_sems, local_sems):
    (x, y, c), q, chips = _chip_peers()
    plan = dict(start=[], local=[], sends=[], recvs=[])
    for t in range(len(ins)):
        own = pltpu.make_async_copy(ins[t].at[q], lands[t].at[q], local_sems.at[t])
        plan['start'].append(own)
        plan['local'].append(own)
        for d, (pq, (px, py)) in enumerate(chips):
            cp = pltpu.make_async_remote_copy(
                src_ref=ins[t].at[pq], dst_ref=lands[t].at[q], send_sem=send_sems.at[t * 3 + d],
                recv_sem=recv_sems.at[t * 3 + d], device_id=(px, py, c), device_id_type=MESH)
            plan['start'].append(cp)
            plan['sends'].append(cp)
            plan['recvs'].append(cp)
    return plan


def _split_start(plan_fn, tensors, land_shapes, n_sems, name, after=None):
    n = len(tensors)
    extra = [] if after is None else [after]

    def body(*refs):
        ins, lands = refs[:n], refs[n:2 * n]
        plan = plan_fn(ins, lands, *refs[2 * n + len(extra):2 * n + len(extra) + 3])
        for cp in plan['start']:
            cp.start()
        refs[-1][...] = jnp.zeros_like(refs[-1])

    outs = pl.pallas_call(
        body, name=name,
        out_shape=(pltpu.SemaphoreType.DMA((n_sems,)), pltpu.SemaphoreType.DMA((n_sems,)), pltpu.SemaphoreType.DMA((n,)),
                   *[pltpu.HBM(t.shape, t.dtype) for t in tensors],
                   *[pltpu.HBM(s, t.dtype) for s, t in zip(land_shapes, tensors)],
                   jax.ShapeDtypeStruct((8, LANES), F32)),
        in_specs=[_HBM] * (2 * n) + [pl.BlockSpec(memory_space=pl.ANY)] * len(extra),
        out_specs=(_SEM, _SEM, _SEM, *[_HBM] * (2 * n), pl.BlockSpec(memory_space=pltpu.VMEM)),
        input_output_aliases={t: 3 + t for t in range(2 * n)},
        compiler_params=pltpu.CompilerParams(has_side_effects=_EFFECT),
    )(*[pltpu.with_memory_space_constraint(t, pltpu.HBM) for t in tensors],
      *[pltpu.with_memory_space_constraint(lax.empty(s, t.dtype), pltpu.HBM) for s, t in zip(land_shapes, tensors)], *extra)
    return outs[:-1], outs[-1]


def _split_relay(plan_fn, state, after, name):
    sems, thru = state[:3], state[3:]
    n = len(thru) // 2

    def arrived(*refs):
        plan = plan_fn(refs[:n], refs[n:2 * n], *refs[2 * n:2 * n + 3])
        for cp in plan['relay_wait']:
            cp.wait_recv()

    thru = pl.pallas_call(
        arrived, name=name + "_arrived",
        out_shape=tuple(pltpu.HBM(t.shape, t.dtype) for t in thru),
        in_specs=[_HBM] * (2 * n) + [_SEM, _SEM, _SEM, pl.BlockSpec(memory_space=pl.ANY)],
        out_specs=tuple([_HBM] * (2 * n)),
        input_output_aliases={t: t for t in range(2 * n)},
        compiler_params=pltpu.CompilerParams(has_side_effects=_EFFECT),
    )(*thru, *sems, after)

    def forward(*refs):
        plan = plan_fn(refs[:n], refs[n:2 * n], *refs[2 * n:2 * n + 3])
        for cp in plan['relay_start']:
            cp.start()
        refs[-1][...] = jnp.zeros_like(refs[-1])

    outs = pl.pallas_call(
        forward, name=name + "_forward",
        out_shape=(*[pltpu.HBM(t.shape, t.dtype) for t in thru], jax.ShapeDtypeStruct((8, LANES), F32)),
        in_specs=[_HBM] * (2 * n) + [_SEM, _SEM, _SEM],
        out_specs=(*[_HBM] * (2 * n), pl.BlockSpec(memory_space=pltpu.VMEM)),
        input_output_aliases={t: t for t in range(2 * n)},
        compiler_params=pltpu.CompilerParams(has_side_effects=_EFFECT),
    )(*thru, *sems)
    return (*sems, *outs[:-1]), outs[-1]


def _split_wait(plan_fn, state, after, name, with_sources=False):
    sems, thru = state[:3], state[3:]
    n = len(thru) // 2

    def body(*refs):
        plan = plan_fn(refs[:n], refs[n:2 * n], *refs[2 * n:2 * n + 3])
        for cp in plan['local']:
            cp.wait()
        for cp in plan['sends']:
            cp.wait_send()
        for cp in plan['recvs']:
            cp.wait_recv()

    outs = pl.pallas_call(
        body, name=name,
        out_shape=tuple(pltpu.HBM(t.shape, t.dtype) for t in thru),
        in_specs=[_HBM] * (2 * n) + [_SEM, _SEM, _SEM, pl.BlockSpec(memory_space=pl.ANY)],
        out_specs=tuple([_HBM] * (2 * n)),
        input_output_aliases={t: t for t in range(2 * n)},
        compiler_params=pltpu.CompilerParams(has_side_effects=_EFFECT),
    )(*thru, *sems, after)
    return (list(outs[:n]), list(outs[n:])) if with_sources else list(outs[n:])


PAIR_SUM_BLOCK = 512 * 1024


def _pair_sum(mine, theirs, name):
    _, r, c = mine.shape
    rows = r
    while rows * c > PAIR_SUM_BLOCK and rows % 32 == 0:
        rows //= 2

    def body(core_ref, a_ref, b_ref, o_ref):
        o_ref[0] = (a_ref[0].astype(F32) + b_ref[0].astype(F32)).astype(o_ref.dtype)

    return pl.pallas_call(
        body, name=name,
        grid_spec=pltpu.PrefetchScalarGridSpec(
            num_scalar_prefetch=1, grid=(N_CHIP, r // rows),
            in_specs=[pl.BlockSpec((1, rows, c), lambda k, i, core: (2 * k + core[0], i, 0)),
                      pl.BlockSpec((1, rows, c), lambda k, i, core: (k, i, 0))],
            out_specs=pl.BlockSpec((1, rows, c), lambda k, i, core: (k, i, 0))),
        out_shape=jax.ShapeDtypeStruct((N_CHIP, r, c), mine.dtype),
        compiler_params=_cparams(("parallel", "parallel")),
    )(lax.axis_index("c").astype(jnp.int32).reshape(1), mine, theirs)


WEIGHTS = ['norm_w', 'w_in', 's5_lambda_re', 's5_lambda_im', 's5_b_re', 's5_b_im', 's5_c_re', 's5_c_im', 's5_d',
           's5_log_step', 's5_w_glu', 'sgu_ln_w', 'sgu_ln_b', 'sgu_w', 'sgu_b', 'm2_conv_w', 'm2_conv_b', 'm2_dt_bias',
           'm2_a_log', 'm2_d', 'm2_norm_w', 'sc_conv_w', 'merge_b', 'w_branch', 'w_out', 'final_norm_w']
BIG_SHARDED = ['w_in', 'w_branch', 'w_out', 's5_w_glu']
SMALL_SHARDED = ['m2_conv_w', 'sc_conv_w', 'merge_b']
REPLICATED = [n for n in WEIGHTS if n not in BIG_SHARDED + SMALL_SHARDED]
S5_NAMES = ['s5_lambda_re', 's5_lambda_im', 's5_b_re', 's5_b_im', 's5_c_re', 's5_c_im', 's5_d', 's5_log_step']


def _sc_interleave(t):
    lead = t.shape[:-1]
    return jnp.swapaxes(t.reshape(lead + (4, 4, LANES)), -3, -2).reshape(lead + (4 * BW,))


def _pad_in(w):
    z = lambda n: jnp.zeros(w.shape[:-1] + (n,), w.dtype)
    return jnp.concatenate([w[..., 6152:], w[..., 0:1024], w[..., 3072:4096], w[..., 1024:2560], z(512),
                            w[..., 2560:3072], w[..., 4096:4104], z(504), _sc_interleave(w[..., 4104:6152])], axis=-1)


def _unpad_in(g):
    return jnp.concatenate([g[..., C_S5U:C_S5U + 1024], g[..., C_SGU_U:C_SGU_U + 1536], g[..., C_M2Z:C_M2Z + 512],
                            g[..., C_M2X:C_M2X + 1024], g[..., C_DT:C_DT + 8], _sc_interleave(g[..., C_SC:]),
                            g[..., :N_BRANCH * D_MODEL]], axis=-1)


ROW_BLOCK = 8 * LANES


def _pack_rows(tensors, row_mult, batched=False):
    parts = []
    for t in tensors:
        f = t.reshape((t.shape[0], -1) if batched else (1, -1))
        f = jnp.pad(f, ((0, 0), (0, (-f.shape[1]) % ROW_BLOCK)))
        parts.append(f.reshape(f.shape[0], -1, LANES))
    out = jnp.concatenate(parts, axis=1)
    out = jnp.pad(out, ((0, 0), (0, (-out.shape[1]) % row_mult), (0, 0)))
    return out if batched else out[0]


def _unpack_rows(rows, shapes):
    out, r0 = [], 0
    for shp in shapes:
        size = 1
        for s in shp:
            size *= s
        nr = -(-size // ROW_BLOCK) * 8
        out.append(rows[r0:r0 + nr].reshape(-1)[:size].reshape(shp))
        r0 += nr
    return out


def _kernel_col_map():
    m = np.full(IN_PAD, -1, np.int64)
    m[C_MERGE:C_MERGE + 4096] = np.arange(6152, 10248)
    m[C_S5U:C_S5U + 1024] = np.arange(0, 1024)
    m[C_M2X:C_M2X + 1024] = np.arange(3072, 4096)
    m[C_SGU_U:C_SGU_U + 1536] = np.arange(1024, 2560)
    m[C_M2Z:C_M2Z + 512] = np.arange(2560, 3072)
    m[C_DT:C_DT + 8] = np.arange(4096, 4104)
    for j in range(4):
        for kind in range(4):
            k0 = C_SC + 4 * LANES * j + LANES * kind
            m[k0:k0 + LANES] = 4104 + BW * kind + LANES * j + np.arange(LANES)
    return m


def _lane_pieces(sources):
    pieces, cur = [], None
    for lane, src in enumerate(sources):
        key = None if src is None else (src[0], src[1] // LANES, (lane - src[1]) % LANES)
        if cur is not None and key == cur[0]:
            cur[2] = lane + 1
        else:
            if cur is not None and cur[0] is not None:
                pieces.append((*cur[0], cur[1], cur[2]))
            cur = [key, lane, lane + 1]
    if cur is not None and cur[0] is not None:
        pieces.append((*cur[0], cur[1], cur[2]))
    return pieces


def _assemble_block(pieces, load, rows, dtype):
    lane = lax.broadcasted_iota(jnp.int32, (rows, LANES), 1)
    out = None
    for arr, sb, shift, lo, hi in pieces:
        v = load(arr, sb)
        if shift:
            v = pltpu.roll(v, shift, 1)
        if out is None and lo == 0 and hi == LANES:
            out = v
        else:
            out = jnp.where((lane >= lo) & (lane < hi), v, jnp.zeros((rows, LANES), dtype) if out is None else out)
    return jnp.zeros((rows, LANES), dtype) if out is None else out


RELAYOUT_ROWS = 256
SHARD_BLOCKS = -(-SHARD_IN // LANES)


def _load_shard_block(ref, rows):
    def load(j, sb):
        if sb == SHARD_BLOCKS - 1:
            return jnp.broadcast_to(ref[j, :, SHARD_IN - 1:SHARD_IN], (rows, LANES))
        return ref[j, :, sb * LANES:(sb + 1) * LANES]
    return load


def _relayout_w_in(gathered, name):
    kmap = _kernel_col_map()
    dtype = gathered.dtype

    def body(src_ref, o_ref):
        load = _load_shard_block(src_ref, RELAYOUT_ROWS)
        for ob in range(IN_PAD // LANES):
            srcs = [None if kmap[ob * LANES + l] < 0 else (int(kmap[ob * LANES + l]) // SHARD_IN, int(kmap[ob * LANES + l]) % SHARD_IN)
                    for l in range(LANES)]
            o_ref[:, ob * LANES:(ob + 1) * LANES] = _assemble_block(_lane_pieces(srcs), load, RELAYOUT_ROWS, dtype)

    return pl.pallas_call(
        body, name=name, grid=(D_MODEL // RELAYOUT_ROWS,),
        in_specs=[pl.BlockSpec((N_DEV, RELAYOUT_ROWS, SHARD_IN), lambda i: (0, i, 0))],
        out_specs=pl.BlockSpec((RELAYOUT_ROWS, IN_PAD), lambda i: (i, 0)),
        out_shape=jax.ShapeDtypeStruct((D_MODEL, IN_PAD), dtype),
        compiler_params=_cparams(("parallel",)),
    )(gathered)


def _relayout_g_in(gw, name):
    kmap = _kernel_col_map()
    kinv = np.zeros(IN_DIM, np.int64)
    kinv[kmap[kmap >= 0]] = np.nonzero(kmap >= 0)[0]
    dtype = gw.dtype

    def body(src_ref, o_ref):
        load = lambda _, sb: src_ref[:, sb * LANES:(sb + 1) * LANES]
        for j in range(N_DEV):
            for ob in range(SHARD_BLOCKS):
                srcs = [(0, int(kinv[SHARD_IN * j + ob * LANES + l])) if ob * LANES + l < SHARD_IN else None for l in range(LANES)]
                blk = _assemble_block(_lane_pieces(srcs), load, RELAYOUT_ROWS, dtype)
                if ob == SHARD_BLOCKS - 1:
                    o_ref[j, :, SHARD_IN - 1:SHARD_IN] = blk[:, 0:1]
                else:
                    o_ref[j, :, ob * LANES:(ob + 1) * LANES] = blk

    return pl.pallas_call(
        body, name=name, grid=(D_MODEL // RELAYOUT_ROWS,),
        in_specs=[pl.BlockSpec((RELAYOUT_ROWS, IN_PAD), lambda i: (i, 0))],
        out_specs=pl.BlockSpec((N_DEV, RELAYOUT_ROWS, SHARD_IN), lambda i: (0, i, 0)),
        out_shape=jax.ShapeDtypeStruct((N_DEV, D_MODEL, SHARD_IN), dtype),
        compiler_params=_cparams(("parallel",)),
    )(gw)


def _rows128(flat, row_mult=8):
    n = flat.shape[0]
    per = LANES * row_mult
    total = -(-n // per) * per
    return jnp.pad(flat, (0, total - n)).reshape(total // LANES, LANES)


def _pad_lanes(v):
    return jnp.pad(v, (0, LANES - v.shape[0])).reshape(1, LANES)


def _layer_prep(i, p):
    disc, disc_vjp = jax.vjp(_s5_disc, *[p[n][i] for n in S5_NAMES])
    prep = dict(
        nw=p['norm_w'][i].reshape(1, D_MODEL), disc_vjp=disc_vjp,
        s5small=[t.astype(BF16) for t in disc[:4]] + [disc[4], disc[5]],
        sgw=[p['sgu_ln_w'][i].reshape(1, BW), p['sgu_ln_b'][i].reshape(1, BW), p['sgu_w'][i],
             jnp.repeat(p['sgu_b'][i].T, BW // SGU_HEADS, axis=1)],
        cb=p['m2_conv_b'][i].reshape(1, M2_CONV_CH),
        m2w=[_pad_lanes(p['m2_dt_bias'][i]), _pad_lanes(p['m2_a_log'][i]),
             jnp.repeat(p['m2_d'][i], M2_HEAD_DIM).reshape(1, BW), p['m2_norm_w'][i].reshape(1, BW)])
    touch = [t[0, 0].astype(F32) for t in prep['s5small']] + [prep['sgw'][3][0, 0], prep['m2w'][2][0, 0]]
    return prep, sum(touch[1:], touch[0])


def _layer_fwd(x, h, i, prep, w_in, other_weights, before_merge=None):
    proj = _matmul(h, w_in, 1, 0, F32, 1024, 1024, 1024, f"proj{i}")
    full = dict(other_weights(proj), w_in=w_in)
    s5w = prep['s5small'] + [full['s5_w_glu']]
    ya, sre, sim = _s5_fwd(proj, *s5w, f"s5_fwd{i}")
    yb = _sgu_fwd(proj, *prep['sgw'], f"sgu_fwd{i}")
    cw = full['m2_conv_w']
    xa = _m2_conv_fwd(proj, cw, prep['cb'], f"m2conv_fwd{i}")
    yc, s_in = _ssd_fwd(proj, xa, *prep['m2w'], f"ssd_fwd{i}")
    scw = full['sc_conv_w']
    yd = _sc_fwd(proj, scw, f"sc_fwd{i}")
    ys = jnp.stack([ya, yb, yc, yd])
    mb = full['merge_b'].reshape(N_BRANCH, 1, D_MODEL)
    if before_merge is not None:
        mb = mb + before_merge(ys)[0, 0]
    merged = _merge_fwd(proj, ys, mb, full['w_branch'], f"merge_fwd{i}")
    x_new = _matmul(merged, full['w_out'], 1, 0, F32, 1024, 1024, 1024, f"out{i}", residual=x)
    saved = dict(x=x, nw=prep['nw'], h=h, proj=proj, disc_vjp=prep['disc_vjp'], s5w=s5w, sre=sre, sim=sim, sgw=prep['sgw'],
                 cw=cw, cb=prep['cb'], xa=xa, m2w=prep['m2w'], s_in=s_in, scw=scw, ys=ys, mb=mb, merged=merged)
    return x_new, saved, full


def _layer_bwd(dx_out, i, sv, full, on_large_grads=None, after_dh=None):
    g = {}
    proj = sv['proj']
    dm = _matmul(dx_out, full['w_out'], 1, 1, F32, 1024, 1024, 1024, f"dmerged{i}")
    g['w_out'] = _matmul(sv['merged'], dx_out, 0, 0, BF16, 1024, 1024, 1024, f"gw_out{i}")
    dys, dproj, g['w_branch'], dmb = _merge_bwd(proj, sv['ys'], dm, sv['mb'], full['w_branch'], f"merge_bwd{i}")
    g['merge_b'] = dmb.reshape(N_BRANCH, D_MODEL)
    dproj, dbbre, dbbim, dcre, dcim, da, dd, dwg = _s5_bwd(proj, dproj, dys, sv['sre'], sv['sim'], *sv['s5w'], f"s5_bwd{i}")
    g['s5_dense'] = (dbbre, dbbim, dcre, dcim, da, dd)
    g['s5_w_glu'] = dwg.astype(BF16)
    dproj, dlw, dlb, g['sgu_w'], dbias = _sgu_bwd(proj, dproj, dys, *sv['sgw'], f"sgu_bwd{i}")
    g['sgu_ln_w'], g['sgu_ln_b'] = dlw[0], dlb[0]
    g['sgu_b'] = dbias.reshape(SGU_CHUNK, SGU_HEADS, BW // SGU_HEADS).sum(-1).T
    dproj, dxa, ddtb, dal, ddf, dnw = _ssd_bwd(proj, dproj, sv['xa'], dys, sv['s_in'], *sv['m2w'], f"ssd_bwd{i}")
    dproj, g['m2_conv_w'], dcb = _m2_conv_bwd(proj, dproj, dxa, sv['cw'], sv['cb'], f"m2conv_bwd{i}")
    g['m2_conv_b'], g['m2_norm_w'] = dcb[0], dnw[0]
    g['m2_dt_bias'], g['m2_a_log'] = ddtb[0, :M2_HEADS], dal[0, :M2_HEADS]
    g['m2_d'] = ddf.reshape(M2_HEADS, M2_HEAD_DIM).sum(-1)
    dproj, g['sc_conv_w'] = _sc_bwd(proj, dproj, dys, sv['scw'], f"sc_bwd{i}")
    g['w_in'] = _matmul(sv['h'], dproj, 0, 0, BF16, 1024, 1024, 1024, f"gw_in{i}")
    tok = on_large_grads(g) if on_large_grads else None
    dh = _matmul(dproj, full['w_in'], 1, 1, F32, 1024, 1024, 1024, f"dh{i}", after=tok)
    nw = sv['nw'] if after_dh is None else sv['nw'] + after_dh(dh)[0, 0]
    dx_in, dnw_l = _rmsnorm_bwd(sv['x'], nw, dh, dx_out, f"rms_bwd{i}")
    g['norm_w'] = dnw_l[0]
    return dx_in, g


def _split8(t, axis):
    shp = t.shape
    t = t.reshape(shp[:axis] + (N_DEV, shp[axis] // N_DEV) + shp[axis + 1:])
    return jnp.moveaxis(t, axis, 0)


def _join8(t, axis):
    t = jnp.moveaxis(t, 0, axis)
    shp = t.shape
    return t.reshape(shp[:axis] + (shp[axis] * shp[axis + 1],) + shp[axis + 2:])


SHARD_AXIS = {'w_in': 2, 'w_branch': 3, 'w_out': 1, 's5_w_glu': 1, 'm2_conv_w': 2, 'sc_conv_w': 2, 'merge_b': 2}


OTHER_BIG = [n for n in BIG_SHARDED if n != 'w_in']


def _other_weights(gathered):
    return {n: _join8(t, SHARD_AXIS[n] - 1) for n, t in zip(OTHER_BIG, gathered)}


def _layer_grad_blocks(g, i):
    blocks = [_relayout_g_in(g[n], f"relayout_g_in{i}") if n == 'w_in' else _split8(g[n], SHARD_AXIS[n] - 1) for n in BIG_SHARDED]
    return [b.reshape(N_DEV, -1, b.shape[-1]) for b in blocks]


def _pair_start(blocks, i):
    shapes = [(N_CHIP,) + b.shape[1:] for b in blocks]
    return _split_start(_plan_pair, blocks, shapes, N_CHIP * len(blocks), f"pair{i}_start")


def _pair_sums(state, after, i):
    mine, theirs = _split_wait(_plan_pair, state, after, f"pair{i}_wait", with_sources=True)
    return [_pair_sum(b, t, f"pair_sum{i}_{k}") for k, (b, t) in enumerate(zip(mine, theirs))]


def _chips_start(sums, i, after=None):
    return _split_start(_plan_chips, sums, [s.shape for s in sums], 3 * len(sums), f"chips{i}_start", after)


def kernel(x, norm_w, w_in, s5_lambda_re, s5_lambda_im, s5_b_re, s5_b_im, s5_c_re, s5_c_im, s5_d, s5_log_step, s5_w_glu, sgu_ln_w, sgu_ln_b, sgu_w, sgu_b, m2_conv_w, m2_conv_b, m2_dt_bias, m2_a_log, m2_d, m2_norm_w, sc_conv_w, merge_b, w_branch, w_out, final_norm_w, loss_target, m_norm_w, m_w_in, m_s5_lambda_re, m_s5_lambda_im, m_s5_b_re, m_s5_b_im, m_s5_c_re, m_s5_c_im, m_s5_d, m_s5_log_step, m_s5_w_glu, m_sgu_ln_w, m_sgu_ln_b, m_sgu_w, m_sgu_b, m_m2_conv_w, m_m2_conv_b, m_m2_dt_bias, m_m2_a_log, m_m2_d, m_m2_norm_w, m_sc_conv_w, m_merge_b, m_w_branch, m_w_out, m_final_norm_w, v_norm_w, v_w_in, v_s5_lambda_re, v_s5_lambda_im, v_s5_b_re, v_s5_b_im, v_s5_c_re, v_s5_c_im, v_s5_d, v_s5_log_step, v_s5_w_glu, v_sgu_ln_w, v_sgu_ln_b, v_sgu_w, v_sgu_b, v_m2_conv_w, v_m2_conv_b, v_m2_dt_bias, v_m2_a_log, v_m2_d, v_m2_norm_w, v_sc_conv_w, v_merge_b, v_w_branch, v_w_out, v_final_norm_w):
    loc = locals()
    p = {n: loc[n] for n in WEIGHTS}
    mom = {n: loc['m_' + n] for n in WEIGHTS}
    vel = {n: loc['v_' + n] for n in WEIGHTS}

    small_sizes = [p[n].size for n in SMALL_SHARDED]
    small_pack = _rows128(jnp.concatenate([p[n].reshape(-1) for n in SMALL_SHARDED]))
    shards = ([p['w_in'][0].astype(BF16)] + [p[n][0].astype(BF16) for n in OTHER_BIG] + [small_pack]
              + [p[n][1].astype(BF16) for n in BIG_SHARDED])
    gath, tok = _split_start(_plan_gather, shards, [(N_DEV,) + t.shape for t in shards], 7 * len(shards), "gather_start")
    sems, srcs, lands = gath[:3], gath[3:3 + len(shards)], gath[3 + len(shards):]

    def relayed(lo, hi, after, name):
        plan = functools.partial(_plan_gather, first=lo)
        state, tok = _split_relay(plan, (*sems, *srcs[lo:hi], *lands[lo:hi]), after, name + "_relay")
        return (plan, state, name), tok

    def arrived(relay, after):
        plan, state, name = relay
        return _split_wait(plan, state, after, name + "_wait")

    def gathered(lo, hi, after, name):
        relay, tok = relayed(lo, hi, after, name)
        return arrived(relay, tok)

    later = dict(p, **{n: p[n] + tok[0, 0] for n in ('norm_w', 's5_log_step', 'sgu_b', 'm2_d')})
    preps = [_layer_prep(i, later) for i in range(DEPTH)]
    h0 = _rmsnorm_fwd(x[0], preps[0][0]['nw'], "rms_fwd0")
    got = gathered(0, 1, tok + (preps[0][1] + preps[1][1] + h0[0, 0].astype(F32)), "gather_w_in0")
    small_full = {}

    def other_weights0(proj):
        got = gathered(1, 5, proj, "gather_rest0")
        small_all, off = got[-1].reshape(N_DEV, -1), 0
        for n, sz in zip(SMALL_SHARDED, small_sizes):
            small_full[n] = _join8(small_all[:, off:off + sz].reshape((N_DEV,) + p[n].shape), SHARD_AXIS[n])
            off += sz
        return dict(_other_weights(got[:-1]), **{n: small_full[n][0] for n in SMALL_SHARDED})

    saved, layer_g, full = [None] * DEPTH, [None] * DEPTH, [None] * DEPTH
    relay1 = []

    def relay_layer1(ys):
        relay, tok = relayed(5, 9, ys, "gather1")
        relay1.append(relay)
        return tok

    xs, saved[0], full[0] = _layer_fwd(x[0], h0, 0, preps[0][0], _relayout_w_in(got[0], "relayout_w_in0"), other_weights0,
                                       relay_layer1)
    h1 = _rmsnorm_fwd(xs, preps[1][0]['nw'], "rms_fwd1")
    got = arrived(relay1[0], h1)
    xs, saved[1], full[1] = _layer_fwd(
        xs, h1, 1, preps[1][0], _relayout_w_in(got[0], "relayout_w_in1"),
        lambda proj: dict(_other_weights(got[1:]), **{n: small_full[n][1] for n in SMALL_SHARDED}))
    loss_row, dx, dfw = _loss_head(xs, final_norm_w.reshape(1, D_MODEL), loss_target[0])
    loss = lax.psum(loss_row[0, 0], ("x", "y", "c"))
    pairs, scat = [None] * DEPTH, [None] * DEPTH

    def start_pairs(i):
        def start(g):
            pairs[i], tok = _pair_start(_layer_grad_blocks(g, i), i)
            return tok
        return start

    def send_chip_sums1(dh):
        scat[1], tok = _chips_start(_pair_sums(pairs[1], dh, 1), 1)
        return tok

    dx, layer_g[1] = _layer_bwd(dx, 1, saved[1], full[1], start_pairs(1), send_chip_sums1)
    dx, layer_g[0] = _layer_bwd(dx, 0, saved[0], full[0], start_pairs(0))
    scat[0], tok = _chips_start(_pair_sums(pairs[0], dx, 0), 0)
    for i in range(DEPTH):
        dense = layer_g[i].pop('s5_dense')
        dense = dense[:4] + (dense[4] + tok[0, 0], dense[5])
        layer_g[i].update(zip(S5_NAMES, saved[i]['disc_vjp'](dense)))
    grads = {n: jnp.stack([layer_g[i][n] for i in range(DEPTH)]) for n in SMALL_SHARDED + REPLICATED if n != 'final_norm_w'}
    grads['final_norm_w'] = dfw[0]

    out_g, out_d, out_m, out_v = {}, {}, {}, {}
    repl_rows = _pack_rows([grads[n] for n in REPLICATED], 8 * N_DEV)
    rr = repl_rows.shape[0] // N_DEV
    shard_rows = _pack_rows([_split8(grads[n], SHARD_AXIS[n]) for n in SMALL_SHARDED], 8, batched=True)
    rs = shard_rows.shape[1]
    small_g = jnp.concatenate([shard_rows, repl_rows.reshape(N_DEV, rr, LANES)], axis=1)
    small_sum = _slot_sum(_exchange([small_g], False, "scatter_small")[0], "sum_small")
    repl_all = _exchange([small_sum[rs:]], True, "gather_small")[0].reshape(N_DEV * rr, LANES)
    g_all = jnp.concatenate([small_sum[:rs], repl_all], axis=0)
    names = SMALL_SHARDED + REPLICATED
    pieces = (_unpack_rows(g_all[:rs], [p[n].shape for n in SMALL_SHARDED])
              + _unpack_rows(g_all[rs:], [p[n].shape for n in REPLICATED]))
    out_g.update(zip(names, pieces))
    res = _adamw_many(*[[_memory_view(n, d[n]) for n in names] for d in (out_g, p, mom, vel)], "adamw_small")
    for r, dst in zip(res, (out_d, out_m, out_v)):
        dst.update({n: _memory_view(n, t) for n, t in zip(names, r)})

    landed1 = _split_wait(_plan_chips, scat[1], res[0][0], "chips1_wait")
    landed0 = _split_wait(_plan_chips, scat[0], landed1[0], "chips0_wait")
    for k, n in enumerate(BIG_SHARDED):
        shp = p[n].shape
        c = shp[-1]
        r = p[n].size // (DEPTH * c)
        if n == 'w_in':
            big = _adamw_w_in([landed0[k], landed1[k]], p[n], mom[n], vel[n], "adamw_w_in")
        else:
            big = _adamw([landed0[k], landed1[k]], *[d[n].reshape(DEPTH, r, c) for d in (p, mom, vel)],
                         {'w_branch': 512, 'w_out': 128, 's5_w_glu': 64}[n], "adamw_" + n)
        out_g[n], out_d[n], out_m[n], out_v[n] = [o.reshape(shp) for o in big]
    return (loss, dx[None], *[out_g[n] for n in WEIGHTS], *[out_d[n] for n in WEIGHTS],
            *[out_m[n] for n in WEIGHTS], *[out_v[n] for n in WEIGHTS])
```

```python
import functools

import jax
import jax.numpy as jnp
import numpy as np
from jax import lax
from jax.experimental import pallas as pl
from jax.experimental.pallas import tpu as pltpu

F32 = jnp.float32
BF16 = jnp.bfloat16

N_DEV = 8
SEQ = 2048
D_MODEL = 1024
DEPTH = 2
BW = 512
N_BRANCH = 4
EPS = 1e-6
S5_GROUPS, S5_STATE, S5_P = 32, 64, 16
S5_CH = S5_GROUPS * S5_STATE
SGU_CHUNK, SGU_HEADS = 128, 8
M2_HEADS, M2_HEAD_DIM, M2_STATE, M2_CHUNK, M2_CONV = 8, 64, 128, 128, 4
M2_CONV_CH = 1024
SC_CONV = 3
IN_DIM = 10248
IN_PAD = 11264
C_MERGE = 0
C_S5U, C_S5G = 4096, 4608
C_M2X = 5120
C_SGU_U, C_SGU_V, C_SGU_G = 6144, 6656, 7168
C_M2Z, C_DT = 8192, 8704
C_SC = 9216
SHARD_IN = IN_DIM // N_DEV

ADAM_LR, ADAM_B1, ADAM_B2, ADAM_EPS, ADAM_WD, ADAM_STEP = 0.001, 0.9, 0.999, 1e-08, 0.01, 10

VMEM_LIMIT = 56 * 1024 * 1024
LANES = 128

MESH = pl.DeviceIdType.MESH


def _cparams(sem=None, **kw):
    return pltpu.CompilerParams(dimension_semantics=sem, vmem_limit_bytes=VMEM_LIMIT, **kw)


def _dg(a, b, ca, cb, precision=None):
    return lax.dot_general(a, b, (((ca,), (cb,)), ((), ())), precision=precision,
                           preferred_element_type=F32)


@functools.partial(jax.custom_vjp, nondiff_argnums=(2, 3))
def _bdot(a, b, ca, cb):
    return _dg(a.astype(BF16), b.astype(BF16), ca, cb)


def _bdot_fwd(a, b, ca, cb):
    return _bdot(a, b, ca, cb), (a, b)


def _bdot_bwd(ca, cb, res, g):
    a, b = res
    gb, ab, bb = g.astype(BF16), a.astype(BF16), b.astype(BF16)
    da = _dg(gb, bb, 1, 1 - cb) if ca == 1 else _dg(bb, gb, 1 - cb, 1)
    db = _dg(ab, gb, 1 - ca, 0) if cb == 0 else _dg(gb, ab, 0, 1 - ca)
    return da.astype(a.dtype), db.astype(b.dtype)


_bdot.defvjp(_bdot_fwd, _bdot_bwd)


def _rms(x, w):
    return x * lax.rsqrt(jnp.mean(x * x, axis=-1, keepdims=True) + EPS) * w


def _silu(x):
    return x * jax.nn.sigmoid(x)


def _gelu(x):
    return 0.5 * x * (1.0 + jnp.tanh(0.7978845608028654 * (x + 0.044715 * (x * x * x))))


def _softplus(x):
    return jnp.maximum(x, 0.0) + jnp.log1p(jnp.exp(-jnp.abs(x)))


def _shift_down(x, s):
    if s == 0:
        return x
    row = lax.broadcasted_iota(jnp.int32, x.shape, 0)
    return jnp.where(row >= s, pltpu.roll(x, s, 0), 0.0)


def _shift_up(x, s):
    if s == 0:
        return x
    n = x.shape[0]
    row = lax.broadcasted_iota(jnp.int32, x.shape, 0)
    return jnp.where(row < n - s, pltpu.roll(x, n - s, 0), 0.0)


def _matmul(a, b, ca, cb, out_dtype, tm, tn, tk, name, residual=None, after=None):
    m = a.shape[1 - ca]
    k = a.shape[ca]
    n = b.shape[1 - cb]
    assert b.shape[cb] == k and m % tm == 0 and n % tn == 0 and k % tk == 0
    nk = k // tk
    a_spec = pl.BlockSpec((tm, tk), lambda i, j, kk: (i, kk)) if ca == 1 else pl.BlockSpec((tk, tm), lambda i, j, kk: (kk, i))
    b_spec = pl.BlockSpec((tk, tn), lambda i, j, kk: (kk, j)) if cb == 0 else pl.BlockSpec((tn, tk), lambda i, j, kk: (j, kk))
    o_spec = pl.BlockSpec((tm, tn), lambda i, j, kk: (i, j))
    has_res = residual is not None

    def body(*refs):
        refs = refs[:2 + has_res] + refs[2 + has_res + (after is not None):]
        if has_res:
            a_ref, b_ref, r_ref, o_ref, acc = refs
        else:
            a_ref, b_ref, o_ref, acc = refs
        kk = pl.program_id(2)
        part = _dg(a_ref[...].astype(BF16), b_ref[...].astype(BF16), ca, cb)

        @pl.when(kk == 0)
        def _():
            acc[...] = part

        @pl.when(kk > 0)
        def _():
            acc[...] += part

        @pl.when(kk == nk - 1)
        def _():
            r = acc[...]
            if has_res:
                r = r + r_ref[...]
            o_ref[...] = r.astype(out_dtype)

    ins = [a, b] + ([residual] if has_res else []) + ([after] if after is not None else [])
    specs = [a_spec, b_spec] + ([o_spec] if has_res else []) + ([pl.BlockSpec(memory_space=pl.ANY)] if after is not None else [])
    return pl.pallas_call(
        body, name=name, grid=(m // tm, n // tn, nk), in_specs=specs, out_specs=o_spec,
        out_shape=jax.ShapeDtypeStruct((m, n), out_dtype),
        scratch_shapes=[pltpu.VMEM((tm, tn), F32)],
        compiler_params=_cparams(("parallel", "parallel", "arbitrary")),
    )(*ins)


ROW_TILE = 512


def _rmsnorm_fwd(x, w, name):
    def body(x_ref, w_ref, o_ref):
        o_ref[...] = _rms(x_ref[...], w_ref[...]).astype(BF16)

    return pl.pallas_call(
        body, name=name, grid=(SEQ // ROW_TILE,),
        in_specs=[pl.BlockSpec((ROW_TILE, D_MODEL), lambda i: (i, 0)), pl.BlockSpec((1, D_MODEL), lambda i: (0, 0))],
        out_specs=pl.BlockSpec((ROW_TILE, D_MODEL), lambda i: (i, 0)),
        out_shape=jax.ShapeDtypeStruct((SEQ, D_MODEL), BF16),
        compiler_params=_cparams(("parallel",)),
    )(x, w)


def _rmsnorm_bwd(x, w, dh, dres, name):
    def body(x_ref, w_ref, dh_ref, dres_ref, dx_ref, dw_ref):
        _, vjp = jax.vjp(_rms, x_ref[...], w_ref[...])
        dx, dw = vjp(dh_ref[...])
        dx_ref[...] = dx + dres_ref[...]

        @pl.when(pl.program_id(0) == 0)
        def _():
            dw_ref[...] = dw

        @pl.when(pl.program_id(0) > 0)
        def _():
            dw_ref[...] += dw

    tile = pl.BlockSpec((ROW_TILE, D_MODEL), lambda i: (i, 0))
    vec = pl.BlockSpec((1, D_MODEL), lambda i: (0, 0))
    return pl.pallas_call(
        body, name=name, grid=(SEQ // ROW_TILE,),
        in_specs=[tile, vec, tile, tile], out_specs=[tile, vec],
        out_shape=[jax.ShapeDtypeStruct((SEQ, D_MODEL), F32), jax.ShapeDtypeStruct((1, D_MODEL), F32)],
        compiler_params=_cparams(("arbitrary",)),
    )(x, w, dh, dres)


def _loss_head(x, w, target):
    def body(x_ref, w_ref, t_ref, loss_ref, dx_ref, dw_ref):
        tgt = t_ref[...]

        def f(xv, wv):
            err = _rms(xv, wv) - tgt
            return 0.5 * jnp.sum(jnp.mean(err * err, axis=-1))

        loss, vjp = jax.vjp(f, x_ref[...], w_ref[...])
        dx, dw = vjp(jnp.ones((), F32))
        dx_ref[...] = dx
        lrow = jnp.full((1, LANES), loss, F32)

        @pl.when(pl.program_id(0) == 0)
        def _():
            dw_ref[...] = dw
            loss_ref[...] = lrow

        @pl.when(pl.program_id(0) > 0)
        def _():
            dw_ref[...] += dw
            loss_ref[...] += lrow

    tile = pl.BlockSpec((ROW_TILE, D_MODEL), lambda i: (i, 0))
    vec = pl.BlockSpec((1, D_MODEL), lambda i: (0, 0))
    return pl.pallas_call(
        body, name="loss_head", grid=(SEQ // ROW_TILE,),
        in_specs=[tile, vec, tile], out_specs=[pl.BlockSpec((1, LANES), lambda i: (0, 0)), tile, vec],
        out_shape=[jax.ShapeDtypeStruct((1, LANES), F32), jax.ShapeDtypeStruct((SEQ, D_MODEL), F32),
                   jax.ShapeDtypeStruct((1, D_MODEL), F32)],
        compiler_params=_cparams(("arbitrary",)),
    )(x, w, target)


S5_T = 256
S5_BLOCKS = [(slice(j * 256, (j + 1) * 256), slice(j * 1024, (j + 1) * 1024)) for j in range(2)]


def _s5_post(ypre, gate, wglu):
    y = _gelu(ypre)
    y = y * jax.nn.sigmoid(_bdot(y, wglu, 1, 0))
    return y * _silu(gate)


def _s5_fwd(proj, bbre, bbim, cre, cim, a2, dvec, wglu, name):
    def body(u_ref, g_ref, bbre_ref, bbim_ref, cre_ref, cim_ref, a_ref, d_ref, wg_ref, o_ref, sre_ref, sim_ref, st):
        @pl.when(pl.program_id(0) == 0)
        def _():
            st[...] = jnp.zeros_like(st)

        u = u_ref[...]
        ub = u.astype(BF16)
        for us, ss in S5_BLOCKS:
            sre_ref[:, ss] = _dg(ub[:, us], bbre_ref[us, ss], 1, 0)
            sim_ref[:, ss] = _dg(ub[:, us], bbim_ref[us, ss], 1, 0)
        ar, ai = a_ref[0:1, :], a_ref[1:2, :]

        def step(t, carry):
            sr, si = carry
            nr = ar * sr - ai * si + sre_ref[pl.ds(t, 1), :]
            ni = ar * si + ai * sr + sim_ref[pl.ds(t, 1), :]
            sre_ref[pl.ds(t, 1), :] = nr
            sim_ref[pl.ds(t, 1), :] = ni
            return nr, ni

        sr, si = lax.fori_loop(0, S5_T, step, (st[0:1, :], st[1:2, :]), unroll=8)
        st[0:1, :] = sr
        st[1:2, :] = si
        ypre = jnp.concatenate(
            [_dg(sre_ref[:, ss].astype(BF16), cre_ref[ss, us], 1, 0) - _dg(sim_ref[:, ss].astype(BF16), cim_ref[ss, us], 1, 0)
             for us, ss in S5_BLOCKS], axis=1) + d_ref[...] * u
        o_ref[...] = _s5_post(ypre, g_ref[...], wg_ref[...]).astype(BF16)

    full = lambda shape: pl.BlockSpec(shape, lambda c: (0, 0))
    return pl.pallas_call(
        body, name=name, grid=(SEQ // S5_T,),
        in_specs=[pl.BlockSpec((S5_T, BW), lambda c: (c, C_S5U // BW)), pl.BlockSpec((S5_T, BW), lambda c: (c, C_S5G // BW)),
                  full((BW, S5_CH)), full((BW, S5_CH)), full((S5_CH, BW)), full((S5_CH, BW)),
                  full((2, S5_CH)), full((1, BW)), full((BW, BW))],
        out_specs=[pl.BlockSpec((S5_T, BW), lambda c: (c, 0)), pl.BlockSpec((S5_T, S5_CH), lambda c: (c, 0)),
                   pl.BlockSpec((S5_T, S5_CH), lambda c: (c, 0))],
        out_shape=[jax.ShapeDtypeStruct((SEQ, BW), BF16), jax.ShapeDtypeStruct((SEQ, S5_CH), F32),
                   jax.ShapeDtypeStruct((SEQ, S5_CH), F32)],
        scratch_shapes=[pltpu.VMEM((2, S5_CH), F32)],
        compiler_params=_cparams(("arbitrary",)),
    )(proj, proj, bbre, bbim, cre, cim, a2, dvec, wglu)


def _s5_bwd(proj, dproj, dout, sre, sim, bbre, bbim, cre, cim, a2, dvec, wglu, name):
    nc = SEQ // S5_T

    def body(u_ref, g_ref, do_ref, sre_ref, sim_ref, pre_ref, pim_ref, bbre_ref, bbim_ref, cre_ref, cim_ref, a_ref,
             d_ref, wg_ref, dproj_in, dp_ref, dbbre_ref, dbbim_ref, dcre_ref, dcim_ref, da_ref, dd_ref, dwg_ref,
             gre, gim, st):
        c = nc - 1 - pl.program_id(0)

        @pl.when(pl.program_id(0) == 0)
        def _():
            st[...] = jnp.zeros_like(st)
            for r in (dbbre_ref, dbbim_ref, dcre_ref, dcim_ref, da_ref, dd_ref, dwg_ref):
                r[...] = jnp.zeros_like(r)

        u = u_ref[...]
        s_re, s_im = sre_ref[...], sim_ref[...]

        def head(s_res, s_ims, cres, cims, dv, uv, gv, wg):
            ypre = jnp.concatenate([_bdot(sr, cr, 1, 0) - _bdot(si, ci, 1, 0)
                                    for sr, si, cr, ci in zip(s_res, s_ims, cres, cims)], axis=1) + dv * uv
            return _s5_post(ypre, gv, wg)

        _, vjp = jax.vjp(head, [sre_ref[:, ss] for _, ss in S5_BLOCKS], [sim_ref[:, ss] for _, ss in S5_BLOCKS],
                         [cre_ref[ss, us].astype(F32) for us, ss in S5_BLOCKS],
                         [cim_ref[ss, us].astype(F32) for us, ss in S5_BLOCKS],
                         d_ref[...], u, g_ref[...], wg_ref[...].astype(F32))
        ds_res, ds_ims, dcres, dcims, dd, du_d, dgate, dwg = vjp(do_ref[0])
        for k, (us, ss) in enumerate(S5_BLOCKS):
            dcre_ref[ss, us] += dcres[k]
            dcim_ref[ss, us] += dcims[k]
            gre[:, ss] = ds_res[k]
            gim[:, ss] = ds_ims[k]
        dd_ref[...] += dd
        dwg_ref[...] += dwg
        dp_ref[:, BW:] = dgate.astype(BF16)
        ar, ai = a_ref[0:1, :], a_ref[1:2, :]

        def step(i, carry):
            t = S5_T - 1 - i
            gr, gi = carry
            nr = gre[pl.ds(t, 1), :] + gr
            ni = gim[pl.ds(t, 1), :] + gi
            gre[pl.ds(t, 1), :] = nr
            gim[pl.ds(t, 1), :] = ni
            return ar * nr + ai * ni, ar * ni - ai * nr

        gr, gi = lax.fori_loop(0, S5_T, step, (st[0:1, :], st[1:2, :]), unroll=8)
        st[0:1, :] = gr
        st[1:2, :] = gi
        g_re, g_im = gre[...], gim[...]
        first = jnp.where(c > 0, 1.0, 0.0)
        row = lax.broadcasted_iota(jnp.int32, (S5_T, S5_CH), 0)
        p_re = jnp.where(row == 0, pre_ref[7:8, :] * first, pltpu.roll(s_re, 1, 0))
        p_im = jnp.where(row == 0, pim_ref[7:8, :] * first, pltpu.roll(s_im, 1, 0))
        da_ref[0:1, :] += jnp.sum(g_re * p_re + g_im * p_im, axis=0, keepdims=True)
        da_ref[1:2, :] += jnp.sum(g_im * p_re - g_re * p_im, axis=0, keepdims=True)
        ub, grb, gib = u.astype(BF16), g_re.astype(BF16), g_im.astype(BF16)
        du_s = []
        for us, ss in S5_BLOCKS:
            dbbre_ref[us, ss] += _dg(ub[:, us], grb[:, ss], 0, 0)
            dbbim_ref[us, ss] += _dg(ub[:, us], gib[:, ss], 0, 0)
            du_s.append(_dg(grb[:, ss], bbre_ref[us, ss], 1, 1) + _dg(gib[:, ss], bbim_ref[us, ss], 1, 1))
        dp_ref[:, :BW] = (du_d + jnp.concatenate(du_s, axis=1)).astype(BF16)

    full = lambda shape: pl.BlockSpec(shape, lambda i: (0, 0))
    rev = lambda w, col=0: pl.BlockSpec((S5_T, w), lambda i: (nc - 1 - i, col))
    prev = pl.BlockSpec((8, S5_CH), lambda i: (jnp.maximum((nc - 1 - i) * (S5_T // 8) - 1, 0), 0))
    return pl.pallas_call(
        body, name=name, grid=(nc,),
        in_specs=[rev(BW, C_S5U // BW), rev(BW, C_S5G // BW), pl.BlockSpec((1, S5_T, BW), lambda i: (0, nc - 1 - i, 0)),
                  rev(S5_CH), rev(S5_CH), prev, prev,
                  full((BW, S5_CH)), full((BW, S5_CH)), full((S5_CH, BW)), full((S5_CH, BW)),
                  full((2, S5_CH)), full((1, BW)), full((BW, BW)), pl.BlockSpec(memory_space=pl.ANY)],
        out_specs=[rev(2 * BW, C_S5U // (2 * BW)), full((BW, S5_CH)), full((BW, S5_CH)), full((S5_CH, BW)), full((S5_CH, BW)),
                   full((2, S5_CH)), full((1, BW)), full((BW, BW))],
        input_output_aliases={14: 0},
        out_shape=[jax.ShapeDtypeStruct((SEQ, IN_PAD), BF16),
                   jax.ShapeDtypeStruct((BW, S5_CH), F32), jax.ShapeDtypeStruct((BW, S5_CH), F32),
                   jax.ShapeDtypeStruct((S5_CH, BW), F32), jax.ShapeDtypeStruct((S5_CH, BW), F32),
                   jax.ShapeDtypeStruct((2, S5_CH), F32), jax.ShapeDtypeStruct((1, BW), F32),
                   jax.ShapeDtypeStruct((BW, BW), F32)],
        scratch_shapes=[pltpu.VMEM((S5_T, S5_CH), F32), pltpu.VMEM((S5_T, S5_CH), F32), pltpu.VMEM((2, S5_CH), F32)],
        compiler_params=_cparams(("arbitrary",)),
    )(proj, proj, dout, sre, sim, sre, sim, bbre, bbim, cre, cim, a2, dvec, wglu, dproj)


def _diag_blocks(dense, rows_per, cols_per):
    rows, cols = dense.shape
    per_lane_block = LANES // cols_per
    tile = 256

    def body(d_ref, o_ref):
        r0 = pl.program_id(0) * tile
        grp = (r0 + lax.broadcasted_iota(jnp.int32, (tile, LANES), 0)) // rows_per
        lane = lax.broadcasted_iota(jnp.int32, (tile, LANES), 1)
        acc = jnp.zeros((tile, LANES), F32)
        for hb in range(cols // LANES):
            acc = acc + jnp.where(grp == per_lane_block * hb + lane // cols_per, d_ref[:, hb * LANES:(hb + 1) * LANES], 0.0)
        shift = LANES // 2
        while shift >= cols_per:
            acc = acc + pltpu.roll(acc, LANES - shift, 1)
            shift //= 2
        o_ref[...] = acc

    return pl.pallas_call(
        body, name=f"diag_blocks_{rows_per}x{cols_per}", grid=(rows // tile,),
        in_specs=[pl.BlockSpec((tile, cols), lambda i: (i, 0))], out_specs=pl.BlockSpec((tile, LANES), lambda i: (i, 0)),
        out_shape=jax.ShapeDtypeStruct((rows, LANES), F32), compiler_params=_cparams(("parallel",)),
    )(dense)


@jax.custom_vjp
def _block_diag(t):
    g, rows_per, cols_per = t.shape
    wide = jnp.tile(t.reshape(g * rows_per, cols_per), (1, g))
    r = lax.broadcasted_iota(jnp.int32, wide.shape, 0) // rows_per
    c = lax.broadcasted_iota(jnp.int32, wide.shape, 1) // cols_per
    return jnp.where(r == c, wide, 0.0)


def _block_diag_fwd(t):
    return _block_diag(t), None


def _block_diag_bwd(_, dense):
    rows_per, cols_per = dense.shape[0] // S5_GROUPS, dense.shape[1] // S5_GROUPS
    return (_diag_blocks(dense, rows_per, cols_per)[:, :cols_per].reshape(S5_GROUPS, rows_per, cols_per),)


_block_diag.defvjp(_block_diag_fwd, _block_diag_bwd)


def _s5_disc(lam_re, lam_im, b_re, b_im, c_re, c_im, d, log_step):
    step = jnp.exp(log_step)[:, None]
    mag = jnp.exp(lam_re * step)
    ab_re, ab_im = mag * jnp.cos(lam_im * step), mag * jnp.sin(lam_im * step)
    den = lam_re * lam_re + lam_im * lam_im
    nr = ab_re - 1.0
    coef_re = (nr * lam_re + ab_im * lam_im) / den
    coef_im = (ab_im * lam_re - nr * lam_im) / den
    bb_re = coef_re[..., None] * b_re - coef_im[..., None] * b_im
    bb_im = coef_re[..., None] * b_im + coef_im[..., None] * b_re
    bbre = _block_diag(jnp.swapaxes(bb_re, 1, 2))
    bbim = _block_diag(jnp.swapaxes(bb_im, 1, 2))
    cre = _block_diag(jnp.swapaxes(c_re, 1, 2))
    cim = _block_diag(jnp.swapaxes(c_im, 1, 2))
    a2 = jnp.stack([ab_re.reshape(-1), ab_im.reshape(-1)])
    return bbre, bbim, cre, cim, a2, d.reshape(1, BW)


def _left_lanes(shape):
    return lax.broadcasted_iota(jnp.int32, shape, 1) < 64


def _sgu_chunk(u, v, gate, ln_w, ln_b, w, bias):
    u32, v32 = _gelu(u), _gelu(v)
    mu = jnp.mean(v32, axis=-1, keepdims=True)
    var = jnp.mean(jnp.square(v32 - mu), axis=-1, keepdims=True)
    vn = (v32 - mu) * lax.rsqrt(var + EPS) * ln_w + ln_b
    t_i = lax.broadcasted_iota(jnp.int32, (SGU_CHUNK, SGU_CHUNK), 0)
    s_i = lax.broadcasted_iota(jnp.int32, (SGU_CHUNK, SGU_CHUNK), 1)
    causal = t_i >= s_i
    left = _left_lanes((SGU_CHUNK, LANES))
    sgate = _silu(gate)
    outs = []
    for j in range(BW // LANES):
        vb = vn[:, j * LANES:(j + 1) * LANES]
        s_blk = (_bdot(jnp.where(causal, w[2 * j], 0.0), jnp.where(left, vb, 0.0), 1, 0)
                 + _bdot(jnp.where(causal, w[2 * j + 1], 0.0), jnp.where(left, 0.0, vb), 1, 0))
        sl = slice(j * LANES, (j + 1) * LANES)
        outs.append(u32[:, sl] * (s_blk + bias[:, sl]) * sgate[:, sl])
    return outs


def _sgu_fwd(proj, ln_w, ln_b, w, bias, name):
    def body(u_ref, v_ref, g_ref, lw_ref, lb_ref, w_ref, b_ref, o_ref):
        outs = _sgu_chunk(u_ref[...], v_ref[...], g_ref[...], lw_ref[...], lb_ref[...], w_ref[...], b_ref[...])
        for j, o in enumerate(outs):
            o_ref[:, j * LANES:(j + 1) * LANES] = o.astype(BF16)

    blk = lambda col: pl.BlockSpec((SGU_CHUNK, BW), lambda c: (c, col // BW))
    vec = pl.BlockSpec((1, BW), lambda c: (0, 0))
    return pl.pallas_call(
        body, name=name, grid=(SEQ // SGU_CHUNK,),
        in_specs=[blk(C_SGU_U), blk(C_SGU_V), blk(C_SGU_G), vec, vec,
                  pl.BlockSpec((SGU_HEADS, SGU_CHUNK, SGU_CHUNK), lambda c: (0, 0, 0)),
                  pl.BlockSpec((SGU_CHUNK, BW), lambda c: (0, 0))],
        out_specs=pl.BlockSpec((SGU_CHUNK, BW), lambda c: (c, 0)),
        out_shape=jax.ShapeDtypeStruct((SEQ, BW), BF16),
        compiler_params=_cparams(("parallel",)),
    )(proj, proj, proj, ln_w, ln_b, w, bias)


def _sgu_bwd(proj, dproj, dout, ln_w, ln_b, w, bias, name):
    def body(u_ref, v_ref, g_ref, do_ref, lw_ref, lb_ref, w_ref, b_ref, dproj_in, dp_ref, dlw_ref, dlb_ref, dw_ref, db_ref):
        _, vjp = jax.vjp(_sgu_chunk, u_ref[...], v_ref[...], g_ref[...], lw_ref[...], lb_ref[...], w_ref[...], b_ref[...])
        do = do_ref[0]
        du, dv, dgate, dlw, dlb, dw, db = vjp([do[:, j * LANES:(j + 1) * LANES] for j in range(BW // LANES)])
        dp_ref[:, 0:BW] = du.astype(BF16)
        dp_ref[:, BW:2 * BW] = dv.astype(BF16)
        dp_ref[:, 2 * BW:3 * BW] = dgate.astype(BF16)
        dp_ref[:, 3 * BW:] = jnp.zeros((SGU_CHUNK, BW), BF16)

        @pl.when(pl.program_id(0) == 0)
        def _():
            dlw_ref[...] = dlw
            dlb_ref[...] = dlb
            dw_ref[...] = dw
            db_ref[...] = db

        @pl.when(pl.program_id(0) > 0)
        def _():
            dlw_ref[...] += dlw
            dlb_ref[...] += dlb
            dw_ref[...] += dw
            db_ref[...] += db

    blk = lambda col: pl.BlockSpec((SGU_CHUNK, BW), lambda c: (c, col // BW))
    vec = pl.BlockSpec((1, BW), lambda c: (0, 0))
    wsp = pl.BlockSpec((SGU_HEADS, SGU_CHUNK, SGU_CHUNK), lambda c: (0, 0, 0))
    bsp = pl.BlockSpec((SGU_CHUNK, BW), lambda c: (0, 0))
    return pl.pallas_call(
        body, name=name, grid=(SEQ // SGU_CHUNK,),
        in_specs=[blk(C_SGU_U), blk(C_SGU_V), blk(C_SGU_G), pl.BlockSpec((1, SGU_CHUNK, BW), lambda c: (1, c, 0)),
                  vec, vec, wsp, bsp, pl.BlockSpec(memory_space=pl.ANY)],
        out_specs=[pl.BlockSpec((SGU_CHUNK, 4 * BW), lambda c: (c, C_SGU_U // (4 * BW))), vec, vec, wsp, bsp],
        input_output_aliases={8: 0},
        out_shape=[jax.ShapeDtypeStruct((SEQ, IN_PAD), BF16), jax.ShapeDtypeStruct((1, BW), F32),
                   jax.ShapeDtypeStruct((1, BW), F32), jax.ShapeDtypeStruct((SGU_HEADS, SGU_CHUNK, SGU_CHUNK), F32),
                   jax.ShapeDtypeStruct((SGU_CHUNK, BW), F32)],
        compiler_params=_cparams(("arbitrary",)),
    )(proj, proj, proj, dout, ln_w, ln_b, w, bias, dproj)


CONV_BLK = 256


def _m2_conv_fwd(proj, w, b, name):
    def body(x_ref, w_ref, b_ref, o_ref):
        x = x_ref[...]
        acc = jnp.zeros_like(x) + b_ref[...]
        for k in range(M2_CONV):
            acc = acc + w_ref[k:k + 1, :] * _shift_down(x, M2_CONV - 1 - k)
        o_ref[...] = _silu(acc)

    return pl.pallas_call(
        body, name=name, grid=(M2_CONV_CH // CONV_BLK,),
        in_specs=[pl.BlockSpec((SEQ, CONV_BLK), lambda j: (0, C_M2X // CONV_BLK + j)),
                  pl.BlockSpec((M2_CONV, CONV_BLK), lambda j: (0, j)), pl.BlockSpec((1, CONV_BLK), lambda j: (0, j))],
        out_specs=pl.BlockSpec((SEQ, CONV_BLK), lambda j: (0, j)),
        out_shape=jax.ShapeDtypeStruct((SEQ, M2_CONV_CH), F32),
        compiler_params=_cparams(("parallel",)),
    )(proj, w, b)


def _m2_conv_bwd(proj, dproj, dxa, w, b, name):
    def body(x_ref, d_ref, w_ref, b_ref, dproj_in, dx_ref, dw_ref, db_ref):
        x = x_ref[...]
        xs = [_shift_down(x, M2_CONV - 1 - k) for k in range(M2_CONV)]
        acc = jnp.zeros_like(x) + b_ref[...]
        for k in range(M2_CONV):
            acc = acc + w_ref[k:k + 1, :] * xs[k]
        sg = jax.nn.sigmoid(acc)
        dacc = d_ref[...] * (sg * (1.0 + acc * (1.0 - sg)))
        dx = jnp.zeros_like(x)
        for k in range(M2_CONV):
            dx = dx + w_ref[k:k + 1, :] * _shift_up(dacc, M2_CONV - 1 - k)
            dw_ref[k:k + 1, :] = jnp.sum(dacc * xs[k], axis=0, keepdims=True)
        dx_ref[...] = dx.astype(BF16)
        db_ref[...] = jnp.sum(dacc, axis=0, keepdims=True)

    return pl.pallas_call(
        body, name=name, grid=(M2_CONV_CH // CONV_BLK,),
        in_specs=[pl.BlockSpec((SEQ, CONV_BLK), lambda j: (0, C_M2X // CONV_BLK + j)),
                  pl.BlockSpec((SEQ, CONV_BLK), lambda j: (0, j)),
                  pl.BlockSpec((M2_CONV, CONV_BLK), lambda j: (0, j)), pl.BlockSpec((1, CONV_BLK), lambda j: (0, j)),
                  pl.BlockSpec(memory_space=pl.ANY)],
        out_specs=[pl.BlockSpec((SEQ, CONV_BLK), lambda j: (0, C_M2X // CONV_BLK + j)),
                   pl.BlockSpec((M2_CONV, CONV_BLK), lambda j: (0, j)), pl.BlockSpec((1, CONV_BLK), lambda j: (0, j))],
        input_output_aliases={4: 0},
        out_shape=[jax.ShapeDtypeStruct((SEQ, IN_PAD), BF16), jax.ShapeDtypeStruct((M2_CONV, M2_CONV_CH), F32),
                   jax.ShapeDtypeStruct((1, M2_CONV_CH), F32)],
        compiler_params=_cparams(("parallel",)),
    )(proj, dxa, w, b, dproj)


N_PAIR = M2_HEADS // 2
HI = lax.Precision.HIGHEST


def _col(a, h):
    lane = lax.broadcasted_iota(jnp.int32, a.shape, 1)
    return jnp.sum(jnp.where(lane == h, a, 0.0), axis=1, keepdims=True)


def _row(a, h):
    sub = lax.broadcasted_iota(jnp.int32, a.shape, 0)
    return jnp.sum(jnp.where(sub == h, a, 0.0), axis=0, keepdims=True)


def _ssd_chunk(xs, bms, cms, dtr, zs, states, dt_bias, a_log, dfs, nws):
    q = M2_CHUNK
    dt = _softplus(dtr + dt_bias)
    da = dt * (-jnp.exp(a_log))
    l_i = lax.broadcasted_iota(jnp.int32, (q, q), 0)
    s_i = lax.broadcasted_iota(jnp.int32, (q, q), 1)
    causal = l_i >= s_i
    tril = jnp.where(causal, 1.0, 0.0)
    a_cs = _dg(tril, da, 1, 0, HI)
    a_cs_t = _dg(da, tril, 0, 1, HI)
    a_end = _row(a_cs, q - 1)
    left = _left_lanes((q, LANES))
    left1 = _left_lanes((1, LANES))
    ys, nexts = [], []
    for j in range(N_PAIR):
        grp = j // 2
        bm, cm = bms[grp], cms[grp]
        h0, h1 = 2 * j, 2 * j + 1
        cb = _bdot(cm, bm, 1, 1)
        xdt = xs[j] * jnp.where(left, _col(dt, h0), _col(dt, h1))
        acs0, acs1 = _col(a_cs, h0), _col(a_cs, h1)
        y = _bdot(cm, states[j], 1, 0) * jnp.where(left, jnp.exp(acs0), jnp.exp(acs1))
        s_new = states[j] * jnp.where(left1, jnp.exp(_col(a_end, h0)), jnp.exp(_col(a_end, h1)))
        for h, acs, xh in ((h0, acs0, jnp.where(left, xdt, 0.0)), (h1, acs1, jnp.where(left, 0.0, xdt))):
            decay = jnp.exp(jnp.where(causal, acs - _row(a_cs_t, h), -jnp.inf))
            y = y + _bdot(cb * decay, xh, 1, 0)
            s_new = s_new + _bdot(bm * jnp.exp(_col(a_end, h) - acs), xh, 0, 0)
        ys.append((y + dfs[j] * xs[j]) * _silu(zs[j]))
        nexts.append(s_new)
    ssq = sum(jnp.sum(y * y, axis=-1, keepdims=True) for y in ys)
    scale = lax.rsqrt(ssq / BW + EPS)
    return [y * scale * nw for y, nw in zip(ys, nws)], nexts


def _blocks(ref, n, width=LANES):
    return [ref[:, j * width:(j + 1) * width] for j in range(n)]


def _ssd_fwd(proj, xa, dt_bias, a_log, dfull, nw, name):
    nc = SEQ // M2_CHUNK

    def body(x_ref, b_ref, c_ref, dt_ref, z_ref, dtb_ref, al_ref, df_ref, nw_ref, o_ref, sin_ref, st):
        @pl.when(pl.program_id(0) == 0)
        def _():
            st[...] = jnp.zeros_like(st)

        states = [st[j] for j in range(N_PAIR)]
        for j in range(N_PAIR):
            sin_ref[0, j] = states[j]
        ys, nexts = _ssd_chunk(_blocks(x_ref, 4), _blocks(b_ref, 2), _blocks(c_ref, 2), dt_ref[...], _blocks(z_ref, 4),
                               states, dtb_ref[...], al_ref[...], _blocks(df_ref, 4), _blocks(nw_ref, 4))
        for j in range(N_PAIR):
            o_ref[:, j * LANES:(j + 1) * LANES] = ys[j].astype(BF16)
            st[j] = nexts[j]

    vec8 = pl.BlockSpec((1, LANES), lambda c: (0, 0))
    vec = pl.BlockSpec((1, BW), lambda c: (0, 0))
    return pl.pallas_call(
        body, name=name, grid=(nc,),
        in_specs=[pl.BlockSpec((M2_CHUNK, BW), lambda c: (c, 0)), pl.BlockSpec((M2_CHUNK, 256), lambda c: (c, 2)),
                  pl.BlockSpec((M2_CHUNK, 256), lambda c: (c, 3)), pl.BlockSpec((M2_CHUNK, LANES), lambda c: (c, C_DT // LANES)),
                  pl.BlockSpec((M2_CHUNK, BW), lambda c: (c, C_M2Z // BW)), vec8, vec8, vec, vec],
        out_specs=[pl.BlockSpec((M2_CHUNK, BW), lambda c: (c, 0)),
                   pl.BlockSpec((1, N_PAIR, M2_STATE, LANES), lambda c: (c, 0, 0, 0))],
        out_shape=[jax.ShapeDtypeStruct((SEQ, BW), BF16), jax.ShapeDtypeStruct((nc, N_PAIR, M2_STATE, LANES), F32)],
        scratch_shapes=[pltpu.VMEM((N_PAIR, M2_STATE, LANES), F32)],
        compiler_params=_cparams(("arbitrary",)),
    )(xa, xa, xa, proj, proj, dt_bias, a_log, dfull, nw)


def _ssd_bwd(proj, dproj, xa, dout, s_in, dt_bias, a_log, dfull, nw, name):
    nc = SEQ // M2_CHUNK

    def body(x_ref, b_ref, c_ref, dt_ref, z_ref, do_ref, sin_ref, dtb_ref, al_ref, df_ref, nw_ref, dproj_in,
             dp_ref, dxa_ref, ddtb_ref, dal_ref, ddf_ref, dnw_ref, dst):
        @pl.when(pl.program_id(0) == 0)
        def _():
            dst[...] = jnp.zeros_like(dst)
            for r in (ddtb_ref, dal_ref, ddf_ref, dnw_ref):
                r[...] = jnp.zeros_like(r)

        states = [sin_ref[0, j] for j in range(N_PAIR)]
        _, vjp = jax.vjp(_ssd_chunk, _blocks(x_ref, 4), _blocks(b_ref, 2), _blocks(c_ref, 2), dt_ref[...],
                         _blocks(z_ref, 4), states, dtb_ref[...], al_ref[...], _blocks(df_ref, 4), _blocks(nw_ref, 4))
        dxs, dbs, dcs, ddt, dzs, dstates, ddtb, dal, ddfs, dnws = vjp(
            ([do_ref[0, :, j * LANES:(j + 1) * LANES] for j in range(N_PAIR)], [dst[j] for j in range(N_PAIR)]))
        for j in range(N_PAIR):
            sl = slice(j * LANES, (j + 1) * LANES)
            dxa_ref[:, sl] = dxs[j]
            dp_ref[:, sl] = dzs[j].astype(BF16)
            dst[j] = dstates[j]
            ddf_ref[:, sl] += ddfs[j]
            dnw_ref[:, sl] += dnws[j]
        for g in range(2):
            dxa_ref[:, BW + g * LANES:BW + (g + 1) * LANES] = dbs[g]
            dxa_ref[:, BW + 256 + g * LANES:BW + 256 + (g + 1) * LANES] = dcs[g]
        dp_ref[:, BW:BW + LANES] = ddt.astype(BF16)
        dp_ref[:, BW + LANES:] = jnp.zeros((M2_CHUNK, 2 * BW - BW - LANES), BF16)
        ddtb_ref[...] += ddtb
        dal_ref[...] += dal

    rev = lambda w, col=0: pl.BlockSpec((M2_CHUNK, w), lambda i: (nc - 1 - i, col))
    vec8 = pl.BlockSpec((1, LANES), lambda i: (0, 0))
    vec = pl.BlockSpec((1, BW), lambda i: (0, 0))
    return pl.pallas_call(
        body, name=name, grid=(nc,),
        in_specs=[rev(BW), rev(256, 2), rev(256, 3), rev(LANES, C_DT // LANES), rev(BW, C_M2Z // BW),
                  pl.BlockSpec((1, M2_CHUNK, BW), lambda i: (2, nc - 1 - i, 0)),
                  pl.BlockSpec((1, N_PAIR, M2_STATE, LANES), lambda i: (nc - 1 - i, 0, 0, 0)), vec8, vec8, vec, vec,
                  pl.BlockSpec(memory_space=pl.ANY)],
        out_specs=[rev(2 * BW, C_M2Z // (2 * BW)), rev(M2_CONV_CH), vec8, vec8, vec, vec],
        input_output_aliases={11: 0},
        out_shape=[jax.ShapeDtypeStruct((SEQ, IN_PAD), BF16), jax.ShapeDtypeStruct((SEQ, M2_CONV_CH), F32),
                   jax.ShapeDtypeStruct((1, LANES), F32), jax.ShapeDtypeStruct((1, LANES), F32),
                   jax.ShapeDtypeStruct((1, BW), F32), jax.ShapeDtypeStruct((1, BW), F32)],
        scratch_shapes=[pltpu.VMEM((N_PAIR, M2_STATE, LANES), F32)],
        compiler_params=_cparams(("arbitrary",)),
    )(xa, xa, xa, proj, proj, dout, s_in, dt_bias, a_log, dfull, nw, dproj)


def _sc_specs():
    col = lambda kind: pl.BlockSpec((SEQ, LANES), lambda j: (0, C_SC // LANES + 4 * j + kind))
    return [col(0), col(1), col(2), col(3)]


def _sc_fwd(proj, w, name):
    def body(b_ref, c_ref, h_ref, g_ref, w_ref, o_ref):
        ch = c_ref[...] * h_ref[...]
        acc = jnp.zeros_like(ch)
        for k in range(SC_CONV):
            acc = acc + w_ref[k:k + 1, :] * _shift_down(ch, SC_CONV - 1 - k)
        o_ref[...] = (b_ref[...] * acc * _silu(g_ref[...])).astype(BF16)

    return pl.pallas_call(
        body, name=name, grid=(BW // LANES,),
        in_specs=_sc_specs() + [pl.BlockSpec((SC_CONV, LANES), lambda j: (0, j))],
        out_specs=pl.BlockSpec((SEQ, LANES), lambda j: (0, j)),
        out_shape=jax.ShapeDtypeStruct((SEQ, BW), BF16),
        compiler_params=_cparams(("parallel",)),
    )(proj, proj, proj, proj, w)


def _sc_bwd(proj, dproj, dout, w, name):
    def body(b_ref, c_ref, h_ref, g_ref, do_ref, w_ref, dproj_in, dp_ref, dw_ref):
        cv, hv, gv = c_ref[...], h_ref[...], g_ref[...]
        ch = cv * hv
        chs = [_shift_down(ch, SC_CONV - 1 - k) for k in range(SC_CONV)]
        acc = jnp.zeros_like(ch)
        for k in range(SC_CONV):
            acc = acc + w_ref[k:k + 1, :] * chs[k]
        sg = jax.nn.sigmoid(gv)
        do = do_ref[0]
        bv = b_ref[...]
        dp_ref[:, 0:LANES] = (do * acc * (gv * sg)).astype(BF16)
        dp_ref[:, 3 * LANES:] = (do * bv * acc * (sg * (1.0 + gv * (1.0 - sg)))).astype(BF16)
        dacc = do * bv * (gv * sg)
        dch = jnp.zeros_like(ch)
        for k in range(SC_CONV):
            dch = dch + w_ref[k:k + 1, :] * _shift_up(dacc, SC_CONV - 1 - k)
            dw_ref[k:k + 1, :] = jnp.sum(dacc * chs[k], axis=0, keepdims=True)
        dp_ref[:, LANES:2 * LANES] = (dch * hv).astype(BF16)
        dp_ref[:, 2 * LANES:3 * LANES] = (dch * cv).astype(BF16)

    wsp = pl.BlockSpec((SC_CONV, LANES), lambda j: (0, j))
    return pl.pallas_call(
        body, name=name, grid=(BW // LANES,),
        in_specs=_sc_specs() + [pl.BlockSpec((1, SEQ, LANES), lambda j: (3, 0, j)), wsp, pl.BlockSpec(memory_space=pl.ANY)],
        out_specs=[pl.BlockSpec((SEQ, 4 * LANES), lambda j: (0, C_SC // (4 * LANES) + j)), wsp],
        input_output_aliases={6: 0},
        out_shape=[jax.ShapeDtypeStruct((SEQ, IN_PAD), BF16), jax.ShapeDtypeStruct((SC_CONV, BW), F32)],
        compiler_params=_cparams(("parallel",)),
    )(proj, proj, proj, proj, dout, w, dproj)


MERGE_T = 256
MERGE_BWD_T = 512


def _merge_fwd(proj, ys, merge_b, w_branch, name):
    def body(y_ref, lg_ref, b_ref, w_ref, o_ref):
        acc = jnp.zeros((MERGE_T, D_MODEL), F32)
        for k in range(N_BRANCH):
            gate = jax.nn.sigmoid(lg_ref[:, k * D_MODEL:(k + 1) * D_MODEL] + b_ref[k])
            acc = acc + gate * _dg(y_ref[k], w_ref[k], 1, 0)
        o_ref[...] = acc.astype(BF16)

    return pl.pallas_call(
        body, name=name, grid=(SEQ // MERGE_T,),
        in_specs=[pl.BlockSpec((N_BRANCH, MERGE_T, BW), lambda i: (0, i, 0)),
                  pl.BlockSpec((MERGE_T, N_BRANCH * D_MODEL), lambda i: (i, C_MERGE // (N_BRANCH * D_MODEL))),
                  pl.BlockSpec((N_BRANCH, 1, D_MODEL), lambda i: (0, 0, 0)),
                  pl.BlockSpec((N_BRANCH, BW, D_MODEL), lambda i: (0, 0, 0))],
        out_specs=pl.BlockSpec((MERGE_T, D_MODEL), lambda i: (i, 0)),
        out_shape=jax.ShapeDtypeStruct((SEQ, D_MODEL), BF16),
        compiler_params=_cparams(("parallel",)),
    )(ys, proj, merge_b, w_branch)


def _merge_bwd(proj, ys, dm, merge_b, w_branch, name):
    nt = SEQ // MERGE_BWD_T

    def body(y_ref, lg_ref, dm_ref, b_ref, w_ref, dy_ref, dlg_ref, dw_ref, db_ref, dw_acc):
        i = pl.program_id(1)
        gate = jax.nn.sigmoid(lg_ref[...] + b_ref[0])
        y = y_ref[0]
        dmv = dm_ref[...]
        dbo = (gate * dmv).astype(BF16)
        dlg = _dg(y, w_ref[0], 1, 0) * dmv * gate * (1.0 - gate)
        dlg_ref[...] = dlg.astype(BF16)
        dy_ref[0] = _dg(dbo, w_ref[0], 1, 1)
        dwp = _dg(y, dbo, 0, 0)
        dbp = jnp.sum(dlg, axis=0, keepdims=True)

        @pl.when(i == 0)
        def _():
            dw_acc[...] = dwp
            db_ref[0] = dbp

        @pl.when(i > 0)
        def _():
            dw_acc[...] += dwp
            db_ref[0] += dbp

        @pl.when(i == nt - 1)
        def _():
            dw_ref[0] = dw_acc[...].astype(BF16)

    return pl.pallas_call(
        body, name=name, grid=(N_BRANCH, nt),
        in_specs=[pl.BlockSpec((1, MERGE_BWD_T, BW), lambda k, i: (k, i, 0)),
                  pl.BlockSpec((MERGE_BWD_T, D_MODEL), lambda k, i: (i, C_MERGE // D_MODEL + k)),
                  pl.BlockSpec((MERGE_BWD_T, D_MODEL), lambda k, i: (i, 0)),
                  pl.BlockSpec((1, 1, D_MODEL), lambda k, i: (k, 0, 0)),
                  pl.BlockSpec((1, BW, D_MODEL), lambda k, i: (k, 0, 0))],
        out_specs=[pl.BlockSpec((1, MERGE_BWD_T, BW), lambda k, i: (k, i, 0)),
                   pl.BlockSpec((MERGE_BWD_T, D_MODEL), lambda k, i: (i, k)),
                   pl.BlockSpec((1, BW, D_MODEL), lambda k, i: (k, 0, 0)),
                   pl.BlockSpec((1, 1, D_MODEL), lambda k, i: (k, 0, 0))],
        out_shape=[jax.ShapeDtypeStruct((N_BRANCH, SEQ, BW), F32), jax.ShapeDtypeStruct((SEQ, IN_PAD), BF16),
                   jax.ShapeDtypeStruct((N_BRANCH, BW, D_MODEL), BF16), jax.ShapeDtypeStruct((N_BRANCH, 1, D_MODEL), F32)],
        scratch_shapes=[pltpu.VMEM((BW, D_MODEL), F32)],
        compiler_params=_cparams(("parallel", "arbitrary")),
    )(ys, proj, dm, merge_b, w_branch)


def _adamw(glist, w, m, v, rows, name):
    nl = len(glist)
    n, r, c = glist[0].shape
    assert w.shape == (nl, r, c) and r % rows == 0
    nb = r // rows

    def body(*refs):
        g_refs = refs[:nl]
        w_ref, m_ref, v_ref, go_ref, d_ref, mo_ref, vo_ref = refs[nl:]
        for layer in range(nl):
            @pl.when(pl.program_id(0) == layer)
            def _(g_ref=g_refs[layer]):
                g = g_ref[0].astype(F32)
                for s in range(1, n):
                    g = g + g_ref[s].astype(F32)
                mn = ADAM_B1 * m_ref[0] + (1.0 - ADAM_B1) * g
                vn = ADAM_B2 * v_ref[0] + (1.0 - ADAM_B2) * jnp.square(g)
                m_hat = mn / (1.0 - ADAM_B1 ** ADAM_STEP)
                v_hat = vn / (1.0 - ADAM_B2 ** ADAM_STEP)
                go_ref[0] = g
                d_ref[0] = -ADAM_LR * (m_hat / (jnp.sqrt(v_hat) + ADAM_EPS) + ADAM_WD * w_ref[0])
                mo_ref[0] = mn
                vo_ref[0] = vn

    def g_spec(layer):
        return pl.BlockSpec((n, rows, c), lambda a, i: (0, jnp.where(a < layer, 0, jnp.where(a == layer, i, nb - 1)), 0))

    blk = pl.BlockSpec((1, rows, c), lambda a, i: (a, i, 0))
    out = jax.ShapeDtypeStruct((nl, r, c), F32)
    return pl.pallas_call(
        body, name=name, grid=(nl, nb),
        in_specs=[g_spec(layer) for layer in range(nl)] + [blk, blk, blk],
        out_specs=[blk, blk, blk, blk], out_shape=[out, out, out, out],
        compiler_params=_cparams(("arbitrary", "arbitrary")),
    )(*glist, w, m, v)


X_ROWS_PER_COL = 2 * (D_MODEL // LANES)


def _w_in_to_x(w):
    t = jnp.transpose(w, (2, 0, 1)).reshape(SHARD_IN, DEPTH, D_MODEL // LANES, LANES)
    return jnp.transpose(t, (0, 2, 1, 3)).reshape(SHARD_IN * X_ROWS_PER_COL, LANES)


def _w_in_from_x(xv):
    t = jnp.transpose(xv.reshape(SHARD_IN, D_MODEL // LANES, DEPTH, LANES), (0, 2, 1, 3))
    return jnp.transpose(t.reshape(SHARD_IN, DEPTH, D_MODEL), (1, 2, 0))


def _adamw_w_in(glist, w, m, v, name):
    n = glist[0].shape[0]
    cols = LANES
    rows = cols * X_ROWS_PER_COL

    def body(g0_ref, g1_ref, w_ref, m_ref, v_ref, go_ref, d_ref, mo_ref, vo_ref):
        for layer, g_ref in enumerate((g0_ref, g1_ref)):
            g = g_ref[0].astype(F32)
            for s in range(1, n):
                g = g + g_ref[s].astype(F32)
            gt = g.T
            for t in range(D_MODEL // LANES):
                sel = (pl.ds(2 * t + layer, cols, stride=X_ROWS_PER_COL), slice(None))
                gs = gt[:, t * LANES:(t + 1) * LANES]
                mn = ADAM_B1 * m_ref[sel] + (1.0 - ADAM_B1) * gs
                vn = ADAM_B2 * v_ref[sel] + (1.0 - ADAM_B2) * jnp.square(gs)
                m_hat = mn / (1.0 - ADAM_B1 ** ADAM_STEP)
                v_hat = vn / (1.0 - ADAM_B2 ** ADAM_STEP)
                go_ref[sel] = gs
                d_ref[sel] = -ADAM_LR * (m_hat / (jnp.sqrt(v_hat) + ADAM_EPS) + ADAM_WD * w_ref[sel])
                mo_ref[sel] = mn
                vo_ref[sel] = vn

    g_spec = pl.BlockSpec((n, D_MODEL, cols), lambda i: (0, 0, i))
    blk = pl.BlockSpec((rows, LANES), lambda i: (i, 0))
    out = jax.ShapeDtypeStruct((SHARD_IN * X_ROWS_PER_COL, LANES), F32)
    res = pl.pallas_call(
        body, name=name, grid=(-(-SHARD_IN // cols),),
        in_specs=[g_spec, g_spec, blk, blk, blk], out_specs=[blk, blk, blk, blk], out_shape=[out, out, out, out],
        compiler_params=_cparams(("parallel",)),
    )(*glist, _w_in_to_x(w), _w_in_to_x(m), _w_in_to_x(v))
    return [_w_in_from_x(o) for o in res]


def _adamw_many(gs, ws, ms, vs, name):
    k = len(gs)

    def body(*refs):
        g_refs, w_refs, m_refs, v_refs = refs[:k], refs[k:2 * k], refs[2 * k:3 * k], refs[3 * k:4 * k]
        d_refs, mo_refs, vo_refs = refs[4 * k:5 * k], refs[5 * k:6 * k], refs[6 * k:7 * k]
        for i in range(k):
            g = g_refs[i][...]
            mn = ADAM_B1 * m_refs[i][...] + (1.0 - ADAM_B1) * g
            vn = ADAM_B2 * v_refs[i][...] + (1.0 - ADAM_B2) * jnp.square(g)
            m_hat = mn / (1.0 - ADAM_B1 ** ADAM_STEP)
            v_hat = vn / (1.0 - ADAM_B2 ** ADAM_STEP)
            d_refs[i][...] = -ADAM_LR * (m_hat / (jnp.sqrt(v_hat) + ADAM_EPS) + ADAM_WD * w_refs[i][...])
            mo_refs[i][...] = mn
            vo_refs[i][...] = vn

    whole = pl.BlockSpec(memory_space=pltpu.VMEM)
    shapes = [jax.ShapeDtypeStruct(w.shape, F32) for w in ws]
    outs = pl.pallas_call(
        body, name=name, in_specs=[whole] * (4 * k), out_specs=[whole] * (3 * k), out_shape=shapes * 3,
        compiler_params=_cparams(None),
    )(*gs, *ws, *ms, *vs)
    return outs[:k], outs[k:2 * k], outs[2 * k:]


MEMORY_ORDER = {'s5_b_re': (0, 1, 3, 2), 's5_b_im': (0, 1, 3, 2), 's5_d': (0, 2, 1), 'sc_conv_w': (1, 0, 2)}


def _memory_view(name, t):
    return jnp.transpose(t, MEMORY_ORDER[name]) if name in MEMORY_ORDER else t


def _slot_sum(gslots, name):
    n, r, c = gslots.shape

    def body(g_ref, o_ref):
        g = g_ref[0]
        for s in range(1, n):
            g = g + g_ref[s]
        o_ref[...] = g

    return pl.pallas_call(
        body, name=name, in_specs=[pl.BlockSpec((n, r, c), lambda: (0, 0, 0))],
        out_specs=pl.BlockSpec((r, c), lambda: (0, 0)), out_shape=jax.ShapeDtypeStruct((r, c), F32),
        compiler_params=_cparams(None),
    )(gslots)


def _me_and_peers():
    x, y, c = lax.axis_index("x"), lax.axis_index("y"), lax.axis_index("c")
    me = 4 * x + 2 * y + c
    peers = []
    for k in range(1, N_DEV):
        px = 1 - x if (k >> 2) & 1 else x
        py = 1 - y if (k >> 1) & 1 else y
        pc = 1 - c if k & 1 else c
        peers.append((4 * px + 2 * py + pc, (px, py, pc)))
    return me, peers


def _exchange(tensors, gather, name):
    n = len(tensors)

    def body(*refs):
        ins, outs = refs[:n], refs[n:2 * n]
        send_sems, recv_sems, local_sems = refs[2 * n:]
        me, peers = _me_and_peers()
        started = []
        for t in range(n):
            own = pltpu.make_async_copy(ins[t] if gather else ins[t].at[me], outs[t].at[me], local_sems.at[t])
            own.start()
            started.append(own)
            for k, (pidx, pos) in enumerate(peers):
                cp = pltpu.make_async_remote_copy(
                    src_ref=ins[t] if gather else ins[t].at[pidx], dst_ref=outs[t].at[me],
                    send_sem=send_sems.at[t, k], recv_sem=recv_sems.at[t, k], device_id=pos, device_id_type=MESH)
                cp.start()
                started.append(cp)
        for cp in started:
            cp.wait()

    any_spec = pl.BlockSpec(memory_space=pl.ANY)
    outs = pl.pallas_call(
        body, name=name, in_specs=[any_spec] * n, out_specs=[any_spec] * n,
        out_shape=[jax.ShapeDtypeStruct(((N_DEV,) + t.shape) if gather else t.shape, t.dtype) for t in tensors],
        scratch_shapes=[pltpu.SemaphoreType.DMA((n, N_DEV - 1)), pltpu.SemaphoreType.DMA((n, N_DEV - 1)),
                        pltpu.SemaphoreType.DMA((n,))],
        compiler_params=pltpu.CompilerParams(has_side_effects=True),
    )(*tensors)
    return list(outs)


_HBM = pl.BlockSpec(memory_space=pltpu.HBM)
_SEM = pl.BlockSpec(memory_space=pltpu.SEMAPHORE)
_EFFECT = pltpu.SideEffectType.DATAFLOW_SIDE_EFFECTING


N_CHIP = N_DEV // 2


def _chip_peers():
    x, y, c = lax.axis_index("x"), lax.axis_index("y"), lax.axis_index("c")
    chips = []
    for d in range(1, N_CHIP):
        px = 1 - x if (d >> 1) & 1 else x
        py = 1 - y if d & 1 else y
        chips.append((2 * px + py, (px, py)))
    return (x, y, c), 2 * x + y, chips


def _plan_gather(ins, lands, send_sems, recv_sems, local_sems, first=0):
    (x, y, c), q, chips = _chip_peers()
    me = 2 * q + c
    plan = dict(start=[], relay_wait=[], relay_start=[], local=[], sends=[], recvs=[])
    for t in range(len(ins)):
        base = (first + t) * 7
        sem = lambda k: dict(send_sem=send_sems.at[base + k], recv_sem=recv_sems.at[base + k], device_id_type=MESH)
        own = pltpu.make_async_copy(ins[t], lands[t].at[me], local_sems.at[first + t])
        to_sib = pltpu.make_async_remote_copy(src_ref=ins[t], dst_ref=lands[t].at[me], device_id=(x, y, 1 - c), **sem(0))
        plan['start'] += [own, to_sib]
        plan['local'].append(own)
        plan['sends'].append(to_sib)
        plan['recvs'].append(to_sib)
        for d, (pq, (px, py)) in enumerate(chips):
            to_chip = pltpu.make_async_remote_copy(src_ref=ins[t], dst_ref=lands[t].at[me], device_id=(px, py, c), **sem(1 + d))
            blk = lands[t].at[2 * pq + c]
            fwd = pltpu.make_async_remote_copy(src_ref=blk, dst_ref=blk, device_id=(x, y, 1 - c), **sem(4 + d))
            plan['start'].append(to_chip)
            plan['relay_wait'].append(to_chip)
            plan['relay_start'].append(fwd)
            plan['sends'] += [to_chip, fwd]
            plan['recvs'].append(fwd)
    return plan


def _plan_pair(ins, lands, send_sems, recv_sems, local_sems):
    (x, y, c), q, chips = _chip_peers()
    plan = dict(start=[], local=[], sends=[], recvs=[])
    for t in range(len(ins)):
        for k in range(N_CHIP):
            cp = pltpu.make_async_remote_copy(
                src_ref=ins[t].at[2 * k + 1 - c], dst_ref=lands[t].at[k], send_sem=send_sems.at[t * N_CHIP + k],
                recv_sem=recv_sems.at[t * N_CHIP + k], device_id=(x, y, 1 - c), device_id_type=MESH)
            plan['start'].append(cp)
            plan['sends'].append(cp)
            plan['recvs'].append(cp)
    return plan


def _plan_chips(ins, lands, send_sems, recv_sems, local_sems):
    (x, y, c), q, chips = _chip_peers()
    plan = dict(start=[], local=[], sends=[], recvs=[])
    for t in range(len(ins)):
        own = pltpu.make_async_copy(ins[t].at[q], lands[t].at[q], local_sems.at[t])
        plan['start'].append(own)
        plan['local'].append(own)
        for d, (pq, (px, py)) in enumerate(chips):
            cp = pltpu.make_async_remote_copy(
                src_ref=ins[t].at[pq], dst_ref=lands[t].at[q], send_sem=send_sems.at[t * 3 + d],
                recv_sem=recv_sems.at[t * 3 + d], device_id=(px, py, c), device_id_type=MESH)
            plan['start'].append(cp)
            plan['sends'].append(cp)
            plan['recvs'].append(cp)
    return plan


def _split_start(plan_fn, tensors, land_shapes, n_sems, name, after=None):
    n = len(tensors)
    extra = [] if after is None else [after]

    def body(*refs):
        ins, lands = refs[:n], refs[n:2 * n]
        plan = plan_fn(ins, lands, *refs[2 * n + len(extra):2 * n + len(extra) + 3])
        for cp in plan['start']:
            cp.start()
        refs[-1][...] = jnp.zeros_like(refs[-1])

    outs = pl.pallas_call(
        body, name=name,
        out_shape=(pltpu.SemaphoreType.DMA((n_sems,)), pltpu.SemaphoreType.DMA((n_sems,)), pltpu.SemaphoreType.DMA((n,)),
                   *[pltpu.HBM(t.shape, t.dtype) for t in tensors],
                   *[pltpu.HBM(s, t.dtype) for s, t in zip(land_shapes, tensors)],
                   jax.ShapeDtypeStruct((8, LANES), F32)),
        in_specs=[_HBM] * (2 * n) + [pl.BlockSpec(memory_space=pl.ANY)] * len(extra),
        out_specs=(_SEM, _SEM, _SEM, *[_HBM] * (2 * n), pl.BlockSpec(memory_space=pltpu.VMEM)),
        input_output_aliases={t: 3 + t for t in range(2 * n)},
        compiler_params=pltpu.CompilerParams(has_side_effects=_EFFECT),
    )(*[pltpu.with_memory_space_constraint(t, pltpu.HBM) for t in tensors],
      *[pltpu.with_memory_space_constraint(lax.empty(s, t.dtype), pltpu.HBM) for s, t in zip(land_shapes, tensors)], *extra)
    return outs[:-1], outs[-1]


def _split_relay(plan_fn, state, after, name):
    sems, thru = state[:3], state[3:]
    n = len(thru) // 2

    def arrived(*refs):
        plan = plan_fn(refs[:n], refs[n:2 * n], *refs[2 * n:2 * n + 3])
        for cp in plan['relay_wait']:
            cp.wait_recv()

    thru = pl.pallas_call(
        arrived, name=name + "_arrived",
        out_shape=tuple(pltpu.HBM(t.shape, t.dtype) for t in thru),
        in_specs=[_HBM] * (2 * n) + [_SEM, _SEM, _SEM, pl.BlockSpec(memory_space=pl.ANY)],
        out_specs=tuple([_HBM] * (2 * n)),
        input_output_aliases={t: t for t in range(2 * n)},
        compiler_params=pltpu.CompilerParams(has_side_effects=_EFFECT),
    )(*thru, *sems, after)

    def forward(*refs):
        plan = plan_fn(refs[:n], refs[n:2 * n], *refs[2 * n:2 * n + 3])
        for cp in plan['relay_start']:
            cp.start()
        refs[-1][...] = jnp.zeros_like(refs[-1])

    outs = pl.pallas_call(
        forward, name=name + "_forward",
        out_shape=(*[pltpu.HBM(t.shape, t.dtype) for t in thru], jax.ShapeDtypeStruct((8, LANES), F32)),
        in_specs=[_HBM] * (2 * n) + [_SEM, _SEM, _SEM],
        out_specs=(*[_HBM] * (2 * n), pl.BlockSpec(memory_space=pltpu.VMEM)),
        input_output_aliases={t: t for t in range(2 * n)},
        compiler_params=pltpu.CompilerParams(has_side_effects=_EFFECT),
    )(*thru, *sems)
    return (*sems, *outs[:-1]), outs[-1]


def _split_wait(plan_fn, state, after, name, with_sources=False):
    sems, thru = state[:3], state[3:]
    n = len(thru) // 2

    def body(*refs):
        plan = plan_fn(refs[:n], refs[n:2 * n], *refs[2 * n:2 * n + 3])
        for cp in plan['local']:
            cp.wait()
        for cp in plan['sends']:
            cp.wait_send()
        for cp in plan['recvs']:
            cp.wait_recv()

    outs = pl.pallas_call(
        body, name=name,
        out_shape=tuple(pltpu.HBM(t.shape, t.dtype) for t in thru),
        in_specs=[_HBM] * (2 * n) + [_SEM, _SEM, _SEM, pl.BlockSpec(memory_space=pl.ANY)],
        out_specs=tuple([_HBM] * (2 * n)),
        input_output_aliases={t: t for t in range(2 * n)},
        compiler_params=pltpu.CompilerParams(has_side_effects=_EFFECT),
    )(*thru, *sems, after)
    return (list(outs[:n]), list(outs[n:])) if with_sources else list(outs[n:])


PAIR_SUM_BLOCK = 512 * 1024


def _pair_sum(mine, theirs, name):
    _, r, c = mine.shape
    rows = r
    while rows * c > PAIR_SUM_BLOCK and rows % 32 == 0:
        rows //= 2

    def body(core_ref, a_ref, b_ref, o_ref):
        o_ref[0] = (a_ref[0].astype(F32) + b_ref[0].astype(F32)).astype(o_ref.dtype)

    return pl.pallas_call(
        body, name=name,
        grid_spec=pltpu.PrefetchScalarGridSpec(
            num_scalar_prefetch=1, grid=(N_CHIP, r // rows),
            in_specs=[pl.BlockSpec((1, rows, c), lambda k, i, core: (2 * k + core[0], i, 0)),
                      pl.BlockSpec((1, rows, c), lambda k, i, core: (k, i, 0))],
            out_specs=pl.BlockSpec((1, rows, c), lambda k, i, core: (k, i, 0))),
        out_shape=jax.ShapeDtypeStruct((N_CHIP, r, c), mine.dtype),
        compiler_params=_cparams(("parallel", "parallel")),
    )(lax.axis_index("c").astype(jnp.int32).reshape(1), mine, theirs)


WEIGHTS = ['norm_w', 'w_in', 's5_lambda_re', 's5_lambda_im', 's5_b_re', 's5_b_im', 's5_c_re', 's5_c_im', 's5_d',
           's5_log_step', 's5_w_glu', 'sgu_ln_w', 'sgu_ln_b', 'sgu_w', 'sgu_b', 'm2_conv_w', 'm2_conv_b', 'm2_dt_bias',
           'm2_a_log', 'm2_d', 'm2_norm_w', 'sc_conv_w', 'merge_b', 'w_branch', 'w_out', 'final_norm_w']
BIG_SHARDED = ['w_in', 'w_branch', 'w_out', 's5_w_glu']
SMALL_SHARDED = ['m2_conv_w', 'sc_conv_w', 'merge_b']
REPLICATED = [n for n in WEIGHTS if n not in BIG_SHARDED + SMALL_SHARDED]
S5_NAMES = ['s5_lambda_re', 's5_lambda_im', 's5_b_re', 's5_b_im', 's5_c_re', 's5_c_im', 's5_d', 's5_log_step']


def _sc_interleave(t):
    lead = t.shape[:-1]
    return jnp.swapaxes(t.reshape(lead + (4, 4, LANES)), -3, -2).reshape(lead + (4 * BW,))


def _pad_in(w):
    z = lambda n: jnp.zeros(w.shape[:-1] + (n,), w.dtype)
    return jnp.concatenate([w[..., 6152:], w[..., 0:1024], w[..., 3072:4096], w[..., 1024:2560], z(512),
                            w[..., 2560:3072], w[..., 4096:4104], z(504), _sc_interleave(w[..., 4104:6152])], axis=-1)


def _unpad_in(g):
    return jnp.concatenate([g[..., C_S5U:C_S5U + 1024], g[..., C_SGU_U:C_SGU_U + 1536], g[..., C_M2Z:C_M2Z + 512],
                            g[..., C_M2X:C_M2X + 1024], g[..., C_DT:C_DT + 8], _sc_interleave(g[..., C_SC:]),
                            g[..., :N_BRANCH * D_MODEL]], axis=-1)


ROW_BLOCK = 8 * LANES


def _pack_rows(tensors, row_mult, batched=False):
    parts = []
    for t in tensors:
        f = t.reshape((t.shape[0], -1) if batched else (1, -1))
        f = jnp.pad(f, ((0, 0), (0, (-f.shape[1]) % ROW_BLOCK)))
        parts.append(f.reshape(f.shape[0], -1, LANES))
    out = jnp.concatenate(parts, axis=1)
    out = jnp.pad(out, ((0, 0), (0, (-out.shape[1]) % row_mult), (0, 0)))
    return out if batched else out[0]


def _unpack_rows(rows, shapes):
    out, r0 = [], 0
    for shp in shapes:
        size = 1
        for s in shp:
            size *= s
        nr = -(-size // ROW_BLOCK) * 8
        out.append(rows[r0:r0 + nr].reshape(-1)[:size].reshape(shp))
        r0 += nr
    return out


def _kernel_col_map():
    m = np.full(IN_PAD, -1, np.int64)
    m[C_MERGE:C_MERGE + 4096] = np.arange(6152, 10248)
    m[C_S5U:C_S5U + 1024] = np.arange(0, 1024)
    m[C_M2X:C_M2X + 1024] = np.arange(3072, 4096)
    m[C_SGU_U:C_SGU_U + 1536] = np.arange(1024, 2560)
    m[C_M2Z:C_M2Z + 512] = np.arange(2560, 3072)
    m[C_DT:C_DT + 8] = np.arange(4096, 4104)
    for j in range(4):
        for kind in range(4):
            k0 = C_SC + 4 * LANES * j + LANES * kind
            m[k0:k0 + LANES] = 4104 + BW * kind + LANES * j + np.arange(LANES)
    return m


def _lane_pieces(sources):
    pieces, cur = [], None
    for lane, src in enumerate(sources):
        key = None if src is None else (src[0], src[1] // LANES, (lane - src[1]) % LANES)
        if cur is not None and key == cur[0]:
            cur[2] = lane + 1
        else:
            if cur is not None and cur[0] is not None:
                pieces.append((*cur[0], cur[1], cur[2]))
            cur = [key, lane, lane + 1]
    if cur is not None and cur[0] is not None:
        pieces.append((*cur[0], cur[1], cur[2]))
    return pieces


def _assemble_block(pieces, load, rows, dtype):
    lane = lax.broadcasted_iota(jnp.int32, (rows, LANES), 1)
    out = None
    for arr, sb, shift, lo, hi in pieces:
        v = load(arr, sb)
        if shift:
            v = pltpu.roll(v, shift, 1)
        if out is None and lo == 0 and hi == LANES:
            out = v
        else:
            out = jnp.where((lane >= lo) & (lane < hi), v, jnp.zeros((rows, LANES), dtype) if out is None else out)
    return jnp.zeros((rows, LANES), dtype) if out is None else out


RELAYOUT_ROWS = 256
SHARD_BLOCKS = -(-SHARD_IN // LANES)


def _load_shard_block(ref, rows):
    def load(j, sb):
        if sb == SHARD_BLOCKS - 1:
            return jnp.broadcast_to(ref[j, :, SHARD_IN - 1:SHARD_IN], (rows, LANES))
        return ref[j, :, sb * LANES:(sb + 1) * LANES]
    return load


def _relayout_w_in(gathered, name):
    kmap = _kernel_col_map()
    dtype = gathered.dtype

    def body(src_ref, o_ref):
        load = _load_shard_block(src_ref, RELAYOUT_ROWS)
        for ob in range(IN_PAD // LANES):
            srcs = [None if kmap[ob * LANES + l] < 0 else (int(kmap[ob * LANES + l]) // SHARD_IN, int(kmap[ob * LANES + l]) % SHARD_IN)
                    for l in range(LANES)]
            o_ref[:, ob * LANES:(ob + 1) * LANES] = _assemble_block(_lane_pieces(srcs), load, RELAYOUT_ROWS, dtype)

    return pl.pallas_call(
        body, name=name, grid=(D_MODEL // RELAYOUT_ROWS,),
        in_specs=[pl.BlockSpec((N_DEV, RELAYOUT_ROWS, SHARD_IN), lambda i: (0, i, 0))],
        out_specs=pl.BlockSpec((RELAYOUT_ROWS, IN_PAD), lambda i: (i, 0)),
        out_shape=jax.ShapeDtypeStruct((D_MODEL, IN_PAD), dtype),
        compiler_params=_cparams(("parallel",)),
    )(gathered)


def _relayout_g_in(gw, name):
    kmap = _kernel_col_map()
    kinv = np.zeros(IN_DIM, np.int64)
    kinv[kmap[kmap >= 0]] = np.nonzero(kmap >= 0)[0]
    dtype = gw.dtype

    def body(src_ref, o_ref):
        load = lambda _, sb: src_ref[:, sb * LANES:(sb + 1) * LANES]
        for j in range(N_DEV):
            for ob in range(SHARD_BLOCKS):
                srcs = [(0, int(kinv[SHARD_IN * j + ob * LANES + l])) if ob * LANES + l < SHARD_IN else None for l in range(LANES)]
                blk = _assemble_block(_lane_pieces(srcs), load, RELAYOUT_ROWS, dtype)
                if ob == SHARD_BLOCKS - 1:
                    o_ref[j, :, SHARD_IN - 1:SHARD_IN] = blk[:, 0:1]
                else:
                    o_ref[j, :, ob * LANES:(ob + 1) * LANES] = blk

    return pl.pallas_call(
        body, name=name, grid=(D_MODEL // RELAYOUT_ROWS,),
        in_specs=[pl.BlockSpec((RELAYOUT_ROWS, IN_PAD), lambda i: (i, 0))],
        out_specs=pl.BlockSpec((N_DEV, RELAYOUT_ROWS, SHARD_IN), lambda i: (0, i, 0)),
        out_shape=jax.ShapeDtypeStruct((N_DEV, D_MODEL, SHARD_IN), dtype),
        compiler_params=_cparams(("parallel",)),
    )(gw)


def _rows128(flat, row_mult=8):
    n = flat.shape[0]
    per = LANES * row_mult
    total = -(-n // per) * per
    return jnp.pad(flat, (0, total - n)).reshape(total // LANES, LANES)


def _pad_lanes(v):
    return jnp.pad(v, (0, LANES - v.shape[0])).reshape(1, LANES)


def _layer_prep(i, p):
    disc, disc_vjp = jax.vjp(_s5_disc, *[p[n][i] for n in S5_NAMES])
    prep = dict(
        nw=p['norm_w'][i].reshape(1, D_MODEL), disc_vjp=disc_vjp,
        s5small=[t.astype(BF16) for t in disc[:4]] + [disc[4], disc[5]],
        sgw=[p['sgu_ln_w'][i].reshape(1, BW), p['sgu_ln_b'][i].reshape(1, BW), p['sgu_w'][i],
             jnp.repeat(p['sgu_b'][i].T, BW // SGU_HEADS, axis=1)],
        cb=p['m2_conv_b'][i].reshape(1, M2_CONV_CH),
        m2w=[_pad_lanes(p['m2_dt_bias'][i]), _pad_lanes(p['m2_a_log'][i]),
             jnp.repeat(p['m2_d'][i], M2_HEAD_DIM).reshape(1, BW), p['m2_norm_w'][i].reshape(1, BW)])
    touch = [t[0, 0].astype(F32) for t in prep['s5small']] + [prep['sgw'][3][0, 0], prep['m2w'][2][0, 0]]
    return prep, sum(touch[1:], touch[0])


def _layer_fwd(x, h, i, prep, w_in, other_weights, before_merge=None):
    proj = _matmul(h, w_in, 1, 0, F32, 1024, 1024, 1024, f"proj{i}")
    full = dict(other_weights(proj), w_in=w_in)
    s5w = prep['s5small'] + [full['s5_w_glu']]
    ya, sre, sim = _s5_fwd(proj, *s5w, f"s5_fwd{i}")
    yb = _sgu_fwd(proj, *prep['sgw'], f"sgu_fwd{i}")
    cw = full['m2_conv_w']
    xa = _m2_conv_fwd(proj, cw, prep['cb'], f"m2conv_fwd{i}")
    yc, s_in = _ssd_fwd(proj, xa, *prep['m2w'], f"ssd_fwd{i}")
    scw = full['sc_conv_w']
    yd = _sc_fwd(proj, scw, f"sc_fwd{i}")
    ys = jnp.stack([ya, yb, yc, yd])
    mb = full['merge_b'].reshape(N_BRANCH, 1, D_MODEL)
    if before_merge is not None:
        mb = mb + before_merge(ys)[0, 0]
    merged = _merge_fwd(proj, ys, mb, full['w_branch'], f"merge_fwd{i}")
    x_new = _matmul(merged, full['w_out'], 1, 0, F32, 1024, 1024, 1024, f"out{i}", residual=x)
    saved = dict(x=x, nw=prep['nw'], h=h, proj=proj, disc_vjp=prep['disc_vjp'], s5w=s5w, sre=sre, sim=sim, sgw=prep['sgw'],
                 cw=cw, cb=prep['cb'], xa=xa, m2w=prep['m2w'], s_in=s_in, scw=scw, ys=ys, mb=mb, merged=merged)
    return x_new, saved, full


def _layer_bwd(dx_out, i, sv, full, on_large_grads=None, after_dh=None):
    g = {}
    proj = sv['proj']
    dm = _matmul(dx_out, full['w_out'], 1, 1, F32, 1024, 1024, 1024, f"dmerged{i}")
    g['w_out'] = _matmul(sv['merged'], dx_out, 0, 0, BF16, 1024, 1024, 1024, f"gw_out{i}")
    dys, dproj, g['w_branch'], dmb = _merge_bwd(proj, sv['ys'], dm, sv['mb'], full['w_branch'], f"merge_bwd{i}")
    g['merge_b'] = dmb.reshape(N_BRANCH, D_MODEL)
    dproj, dbbre, dbbim, dcre, dcim, da, dd, dwg = _s5_bwd(proj, dproj, dys, sv['sre'], sv['sim'], *sv['s5w'], f"s5_bwd{i}")
    g['s5_dense'] = (dbbre, dbbim, dcre, dcim, da, dd)
    g['s5_w_glu'] = dwg.astype(BF16)
    dproj, dlw, dlb, g['sgu_w'], dbias = _sgu_bwd(proj, dproj, dys, *sv['sgw'], f"sgu_bwd{i}")
    g['sgu_ln_w'], g['sgu_ln_b'] = dlw[0], dlb[0]
    g['sgu_b'] = dbias.reshape(SGU_CHUNK, SGU_HEADS, BW // SGU_HEADS).sum(-1).T
    dproj, dxa, ddtb, dal, ddf, dnw = _ssd_bwd(proj, dproj, sv['xa'], dys, sv['s_in'], *sv['m2w'], f"ssd_bwd{i}")
    dproj, g['m2_conv_w'], dcb = _m2_conv_bwd(proj, dproj, dxa, sv['cw'], sv['cb'], f"m2conv_bwd{i}")
    g['m2_conv_b'], g['m2_norm_w'] = dcb[0], dnw[0]
    g['m2_dt_bias'], g['m2_a_log'] = ddtb[0, :M2_HEADS], dal[0, :M2_HEADS]
    g['m2_d'] = ddf.reshape(M2_HEADS, M2_HEAD_DIM).sum(-1)
    dproj, g['sc_conv_w'] = _sc_bwd(proj, dproj, dys, sv['scw'], f"sc_bwd{i}")
    g['w_in'] = _matmul(sv['h'], dproj, 0, 0, BF16, 1024, 1024, 1024, f"gw_in{i}")
    tok = on_large_grads(g) if on_large_grads else None
    dh = _matmul(dproj, full['w_in'], 1, 1, F32, 1024, 1024, 1024, f"dh{i}", after=tok)
    nw = sv['nw'] if after_dh is None else sv['nw'] + after_dh(dh)[0, 0]
    dx_in, dnw_l = _rmsnorm_bwd(sv['x'], nw, dh, dx_out, f"rms_bwd{i}")
    g['norm_w'] = dnw_l[0]
    return dx_in, g


def _split8(t, axis):
    shp = t.shape
    t = t.reshape(shp[:axis] + (N_DEV, shp[axis] // N_DEV) + shp[axis + 1:])
    return jnp.moveaxis(t, axis, 0)


def _join8(t, axis):
    t = jnp.moveaxis(t, 0, axis)
    shp = t.shape
    return t.reshape(shp[:axis] + (shp[axis] * shp[axis + 1],) + shp[axis + 2:])


SHARD_AXIS = {'w_in': 2, 'w_branch': 3, 'w_out': 1, 's5_w_glu': 1, 'm2_conv_w': 2, 'sc_conv_w': 2, 'merge_b': 2}


OTHER_BIG = [n for n in BIG_SHARDED if n != 'w_in']


def _other_weights(gathered):
    return {n: _join8(t, SHARD_AXIS[n] - 1) for n, t in zip(OTHER_BIG, gathered)}


def _layer_grad_blocks(g, i):
    blocks = [_relayout_g_in(g[n], f"relayout_g_in{i}") if n == 'w_in' else _split8(g[n], SHARD_AXIS[n] - 1) for n in BIG_SHARDED]
    return [b.reshape(N_DEV, -1, b.shape[-1]) for b in blocks]


def _pair_start(blocks, i):
    shapes = [(N_CHIP,) + b.shape[1:] for b in blocks]
    return _split_start(_plan_pair, blocks, shapes, N_CHIP * len(blocks), f"pair{i}_start")


def _pair_sums(state, after, i):
    mine, theirs = _split_wait(_plan_pair, state, after, f"pair{i}_wait", with_sources=True)
    return [_pair_sum(b, t, f"pair_sum{i}_{k}") for k, (b, t) in enumerate(zip(mine, theirs))]


def _chips_start(sums, i, after=None):
    return _split_start(_plan_chips, sums, [s.shape for s in sums], 3 * len(sums), f"chips{i}_start", after)


def kernel(x, norm_w, w_in, s5_lambda_re, s5_lambda_im, s5_b_re, s5_b_im, s5_c_re, s5_c_im, s5_d, s5_log_step, s5_w_glu, sgu_ln_w, sgu_ln_b, sgu_w, sgu_b, m2_conv_w, m2_conv_b, m2_dt_bias, m2_a_log, m2_d, m2_norm_w, sc_conv_w, merge_b, w_branch, w_out, final_norm_w, loss_target, m_norm_w, m_w_in, m_s5_lambda_re, m_s5_lambda_im, m_s5_b_re, m_s5_b_im, m_s5_c_re, m_s5_c_im, m_s5_d, m_s5_log_step, m_s5_w_glu, m_sgu_ln_w, m_sgu_ln_b, m_sgu_w, m_sgu_b, m_m2_conv_w, m_m2_conv_b, m_m2_dt_bias, m_m2_a_log, m_m2_d, m_m2_norm_w, m_sc_conv_w, m_merge_b, m_w_branch, m_w_out, m_final_norm_w, v_norm_w, v_w_in, v_s5_lambda_re, v_s5_lambda_im, v_s5_b_re, v_s5_b_im, v_s5_c_re, v_s5_c_im, v_s5_d, v_s5_log_step, v_s5_w_glu, v_sgu_ln_w, v_sgu_ln_b, v_sgu_w, v_sgu_b, v_m2_conv_w, v_m2_conv_b, v_m2_dt_bias, v_m2_a_log, v_m2_d, v_m2_norm_w, v_sc_conv_w, v_merge_b, v_w_branch, v_w_out, v_final_norm_w):
    loc = locals()
    p = {n: loc[n] for n in WEIGHTS}
    mom = {n: loc['m_' + n] for n in WEIGHTS}
    vel = {n: loc['v_' + n] for n in WEIGHTS}

    small_sizes = [p[n].size for n in SMALL_SHARDED]
    small_pack = _rows128(jnp.concatenate([p[n].reshape(-1) for n in SMALL_SHARDED]))
    shards = ([p['w_in'][0].astype(BF16)] + [p[n][0].astype(BF16) for n in OTHER_BIG] + [small_pack]
              + [p[n][1].astype(BF16) for n in BIG_SHARDED])
    gath, tok = _split_start(_plan_gather, shards, [(N_DEV,) + t.shape for t in shards], 7 * len(shards), "gather_start")
    sems, srcs, lands = gath[:3], gath[3:3 + len(shards)], gath[3 + len(shards):]

    def relayed(lo, hi, after, name):
        plan = functools.partial(_plan_gather, first=lo)
        state, tok = _split_relay(plan, (*sems, *srcs[lo:hi], *lands[lo:hi]), after, name + "_relay")
        return (plan, state, name), tok

    def arrived(relay, after):
        plan, state, name = relay
        return _split_wait(plan, state, after, name + "_wait")

    def gathered(lo, hi, after, name):
        relay, tok = relayed(lo, hi, after, name)
        return arrived(relay, tok)

    later = dict(p, **{n: p[n] + tok[0, 0] for n in ('norm_w', 's5_log_step', 'sgu_b', 'm2_d')})
    preps = [_layer_prep(i, later) for i in range(DEPTH)]
    h0 = _rmsnorm_fwd(x[0], preps[0][0]['nw'], "rms_fwd0")
    got = gathered(0, 1, tok + (preps[0][1] + preps[1][1] + h0[0, 0].astype(F32)), "gather_w_in0")
    small_full = {}

    def other_weights0(proj):
        got = gathered(1, 5, proj, "gather_rest0")
        small_all, off = got[-1].reshape(N_DEV, -1), 0
        for n, sz in zip(SMALL_SHARDED, small_sizes):
            small_full[n] = _join8(small_all[:, off:off + sz].reshape((N_DEV,) + p[n].shape), SHARD_AXIS[n])
            off += sz
        return dict(_other_weights(got[:-1]), **{n: small_full[n][0] for n in SMALL_SHARDED})

    saved, layer_g, full = [None] * DEPTH, [None] * DEPTH, [None] * DEPTH
    relay1 = []

    def relay_layer1(ys):
        relay, tok = relayed(5, 9, ys, "gather1")
        relay1.append(relay)
        return tok

    xs, saved[0], full[0] = _layer_fwd(x[0], h0, 0, preps[0][0], _relayout_w_in(got[0], "relayout_w_in0"), other_weights0,
                                       relay_layer1)
    h1 = _rmsnorm_fwd(xs, preps[1][0]['nw'], "rms_fwd1")
    got = arrived(relay1[0], h1)
    xs, saved[1], full[1] = _layer_fwd(
        xs, h1, 1, preps[1][0], _relayout_w_in(got[0], "relayout_w_in1"),
        lambda proj: dict(_other_weights(got[1:]), **{n: small_full[n][1] for n in SMALL_SHARDED}))
    loss_row, dx, dfw = _loss_head(xs, final_norm_w.reshape(1, D_MODEL), loss_target[0])
    loss = lax.psum(loss_row[0, 0], ("x", "y", "c"))
    pairs, scat = [None] * DEPTH, [None] * DEPTH

    def start_pairs1(g):
        pairs[1], tok = _pair_start(_layer_grad_blocks(g, 1), 1)
        return tok

    def send_chip_sums1(dh):
        scat[1], tok = _chips_start(_pair_sums(pairs[1], dh, 1), 1)
        return tok

    def send_all0(g):
        pairs[0], tok = _pair_start(_layer_grad_blocks(g, 0), 0)
        scat[0], tok = _chips_start(_pair_sums(pairs[0], tok, 0), 0)
        return tok

    dx, layer_g[1] = _layer_bwd(dx, 1, saved[1], full[1], start_pairs1, send_chip_sums1)
    dx, layer_g[0] = _layer_bwd(dx, 0, saved[0], full[0], send_all0)
    for i in range(DEPTH):
        layer_g[i].update(zip(S5_NAMES, saved[i]['disc_vjp'](layer_g[i].pop('s5_dense'))))
    grads = {n: jnp.stack([layer_g[i][n] for i in range(DEPTH)]) for n in SMALL_SHARDED + REPLICATED if n != 'final_norm_w'}
    grads['final_norm_w'] = dfw[0]

    out_g, out_d, out_m, out_v = {}, {}, {}, {}
    repl_rows = _pack_rows([grads[n] for n in REPLICATED], 8 * N_DEV)
    rr = repl_rows.shape[0] // N_DEV
    shard_rows = _pack_rows([_split8(grads[n], SHARD_AXIS[n]) for n in SMALL_SHARDED], 8, batched=True)
    rs = shard_rows.shape[1]
    small_g = jnp.concatenate([shard_rows, repl_rows.reshape(N_DEV, rr, LANES)], axis=1)
    small_sum = _slot_sum(_exchange([small_g], False, "scatter_small")[0], "sum_small")
    repl_all = _exchange([small_sum[rs:]], True, "gather_small")[0].reshape(N_DEV * rr, LANES)
    g_all = jnp.concatenate([small_sum[:rs], repl_all], axis=0)
    names = SMALL_SHARDED + REPLICATED
    pieces = (_unpack_rows(g_all[:rs], [p[n].shape for n in SMALL_SHARDED])
              + _unpack_rows(g_all[rs:], [p[n].shape for n in REPLICATED]))
    out_g.update(zip(names, pieces))
    res = _adamw_many(*[[_memory_view(n, d[n]) for n in names] for d in (out_g, p, mom, vel)], "adamw_small")
    for r, dst in zip(res, (out_d, out_m, out_v)):
        dst.update({n: _memory_view(n, t) for n, t in zip(names, r)})

    landed1 = _split_wait(_plan_chips, scat[1], res[0][0], "chips1_wait")
    landed0 = _split_wait(_plan_chips, scat[0], landed1[0], "chips0_wait")
    for k, n in enumerate(BIG_SHARDED):
        shp = p[n].shape
        c = shp[-1]
        r = p[n].size // (DEPTH * c)
        if n == 'w_in':
            big = _adamw_w_in([landed0[k], landed1[k]], p[n], mom[n], vel[n], "adamw_w_in")
        else:
            big = _adamw([landed0[k], landed1[k]], *[d[n].reshape(DEPTH, r, c) for d in (p, mom, vel)],
                         {'w_branch': 512, 'w_out': 128, 's5_w_glu': 64}[n], "adamw_" + n)
        out_g[n], out_d[n], out_m[n], out_v[n] = [o.reshape(shp) for o in big]
    return (loss, dx[None], *[out_g[n] for n in WEIGHTS], *[out_d[n] for n in WEIGHTS],
            *[out_m[n] for n in WEIGHTS], *[out_v[n] for n in WEIGHTS])
```

```python
import functools

import jax
import jax.numpy as jnp
import numpy as np
from jax import lax
from jax.experimental import pallas as pl
from jax.experimental.pallas import tpu as pltpu

F32 = jnp.float32
BF16 = jnp.bfloat16

N_DEV = 8
SEQ = 2048
D_MODEL = 1024
DEPTH = 2
BW = 512
N_BRANCH = 4
EPS = 1e-6
S5_GROUPS, S5_STATE, S5_P = 32, 64, 16
S5_CH = S5_GROUPS * S5_STATE
SGU_CHUNK, SGU_HEADS = 128, 8
M2_HEADS, M2_HEAD_DIM, M2_STATE, M2_CHUNK, M2_CONV = 8, 64, 128, 128, 4
M2_CONV_CH = 1024
SC_CONV = 3
IN_DIM = 10248
IN_PAD = 11264
C_MERGE = 0
C_S5U, C_S5G = 4096, 4608
C_M2X = 5120
C_SGU_U, C_SGU_V, C_SGU_G = 6144, 6656, 7168
C_M2Z, C_DT = 8192, 8704
C_SC = 9216
SHARD_IN = IN_DIM // N_DEV

ADAM_LR, ADAM_B1, ADAM_B2, ADAM_EPS, ADAM_WD, ADAM_STEP = 0.001, 0.9, 0.999, 1e-08, 0.01, 10

VMEM_LIMIT = 56 * 1024 * 1024
LANES = 128

MESH = pl.DeviceIdType.MESH


def _cparams(sem=None, **kw):
    return pltpu.CompilerParams(dimension_semantics=sem, vmem_limit_bytes=VMEM_LIMIT, **kw)


def _dg(a, b, ca, cb, precision=None):
    return lax.dot_general(a, b, (((ca,), (cb,)), ((), ())), precision=precision,
                           preferred_element_type=F32)


@functools.partial(jax.custom_vjp, nondiff_argnums=(2, 3))
def _bdot(a, b, ca, cb):
    return _dg(a.astype(BF16), b.astype(BF16), ca, cb)


def _bdot_fwd(a, b, ca, cb):
    return _bdot(a, b, ca, cb), (a, b)


def _bdot_bwd(ca, cb, res, g):
    a, b = res
    gb, ab, bb = g.astype(BF16), a.astype(BF16), b.astype(BF16)
    da = _dg(gb, bb, 1, 1 - cb) if ca == 1 else _dg(bb, gb, 1 - cb, 1)
    db = _dg(ab, gb, 1 - ca, 0) if cb == 0 else _dg(gb, ab, 0, 1 - ca)
    return da.astype(a.dtype), db.astype(b.dtype)


_bdot.defvjp(_bdot_fwd, _bdot_bwd)


def _rms(x, w):
    return x * lax.rsqrt(jnp.mean(x * x, axis=-1, keepdims=True) + EPS) * w


def _silu(x):
    return x * jax.nn.sigmoid(x)


def _gelu(x):
    return 0.5 * x * (1.0 + jnp.tanh(0.7978845608028654 * (x + 0.044715 * (x * x * x))))


def _softplus(x):
    return jnp.maximum(x, 0.0) + jnp.log1p(jnp.exp(-jnp.abs(x)))


def _shift_down(x, s):
    if s == 0:
        return x
    row = lax.broadcasted_iota(jnp.int32, x.shape, 0)
    return jnp.where(row >= s, pltpu.roll(x, s, 0), 0.0)


def _shift_up(x, s):
    if s == 0:
        return x
    n = x.shape[0]
    row = lax.broadcasted_iota(jnp.int32, x.shape, 0)
    return jnp.where(row < n - s, pltpu.roll(x, n - s, 0), 0.0)


def _matmul(a, b, ca, cb, out_dtype, tm, tn, tk, name, residual=None, after=None):
    m = a.shape[1 - ca]
    k = a.shape[ca]
    n = b.shape[1 - cb]
    assert b.shape[cb] == k and m % tm == 0 and n % tn == 0 and k % tk == 0
    nk = k // tk
    a_spec = pl.BlockSpec((tm, tk), lambda i, j, kk: (i, kk)) if ca == 1 else pl.BlockSpec((tk, tm), lambda i, j, kk: (kk, i))
    b_spec = pl.BlockSpec((tk, tn), lambda i, j, kk: (kk, j)) if cb == 0 else pl.BlockSpec((tn, tk), lambda i, j, kk: (j, kk))
    o_spec = pl.BlockSpec((tm, tn), lambda i, j, kk: (i, j))
    has_res = residual is not None

    def body(*refs):
        refs = refs[:2 + has_res] + refs[2 + has_res + (after is not None):]
        if has_res:
            a_ref, b_ref, r_ref, o_ref, acc = refs
        else:
            a_ref, b_ref, o_ref, acc = refs
        kk = pl.program_id(2)
        part = _dg(a_ref[...].astype(BF16), b_ref[...].astype(BF16), ca, cb)

        @pl.when(kk == 0)
        def _():
            acc[...] = part

        @pl.when(kk > 0)
        def _():
            acc[...] += part

        @pl.when(kk == nk - 1)
        def _():
            r = acc[...]
            if has_res:
                r = r + r_ref[...]
            o_ref[...] = r.astype(out_dtype)

    ins = [a, b] + ([residual] if has_res else []) + ([after] if after is not None else [])
    specs = [a_spec, b_spec] + ([o_spec] if has_res else []) + ([pl.BlockSpec(memory_space=pl.ANY)] if after is not None else [])
    return pl.pallas_call(
        body, name=name, grid=(m // tm, n // tn, nk), in_specs=specs, out_specs=o_spec,
        out_shape=jax.ShapeDtypeStruct((m, n), out_dtype),
        scratch_shapes=[pltpu.VMEM((tm, tn), F32)],
        compiler_params=_cparams(("parallel", "parallel", "arbitrary")),
    )(*ins)


ROW_TILE = 512


def _rmsnorm_fwd(x, w, name):
    def body(x_ref, w_ref, o_ref):
        o_ref[...] = _rms(x_ref[...], w_ref[...]).astype(BF16)

    return pl.pallas_call(
        body, name=name, grid=(SEQ // ROW_TILE,),
        in_specs=[pl.BlockSpec((ROW_TILE, D_MODEL), lambda i: (i, 0)), pl.BlockSpec((1, D_MODEL), lambda i: (0, 0))],
        out_specs=pl.BlockSpec((ROW_TILE, D_MODEL), lambda i: (i, 0)),
        out_shape=jax.ShapeDtypeStruct((SEQ, D_MODEL), BF16),
        compiler_params=_cparams(("parallel",)),
    )(x, w)


def _rmsnorm_bwd(x, w, dh, dres, name):
    def body(x_ref, w_ref, dh_ref, dres_ref, dx_ref, dw_ref):
        _, vjp = jax.vjp(_rms, x_ref[...], w_ref[...])
        dx, dw = vjp(dh_ref[...])
        dx_ref[...] = dx + dres_ref[...]

        @pl.when(pl.program_id(0) == 0)
        def _():
            dw_ref[...] = dw

        @pl.when(pl.program_id(0) > 0)
        def _():
            dw_ref[...] += dw

    tile = pl.BlockSpec((ROW_TILE, D_MODEL), lambda i: (i, 0))
    vec = pl.BlockSpec((1, D_MODEL), lambda i: (0, 0))
    return pl.pallas_call(
        body, name=name, grid=(SEQ // ROW_TILE,),
        in_specs=[tile, vec, tile, tile], out_specs=[tile, vec],
        out_shape=[jax.ShapeDtypeStruct((SEQ, D_MODEL), F32), jax.ShapeDtypeStruct((1, D_MODEL), F32)],
        compiler_params=_cparams(("arbitrary",)),
    )(x, w, dh, dres)


def _loss_head(x, w, target):
    def body(x_ref, w_ref, t_ref, loss_ref, dx_ref, dw_ref):
        tgt = t_ref[...]

        def f(xv, wv):
            err = _rms(xv, wv) - tgt
            return 0.5 * jnp.sum(jnp.mean(err * err, axis=-1))

        loss, vjp = jax.vjp(f, x_ref[...], w_ref[...])
        dx, dw = vjp(jnp.ones((), F32))
        dx_ref[...] = dx
        lrow = jnp.full((1, LANES), loss, F32)

        @pl.when(pl.program_id(0) == 0)
        def _():
            dw_ref[...] = dw
            loss_ref[...] = lrow

        @pl.when(pl.program_id(0) > 0)
        def _():
            dw_ref[...] += dw
            loss_ref[...] += lrow

    tile = pl.BlockSpec((ROW_TILE, D_MODEL), lambda i: (i, 0))
    vec = pl.BlockSpec((1, D_MODEL), lambda i: (0, 0))
    return pl.pallas_call(
        body, name="loss_head", grid=(SEQ // ROW_TILE,),
        in_specs=[tile, vec, tile], out_specs=[pl.BlockSpec((1, LANES), lambda i: (0, 0)), tile, vec],
        out_shape=[jax.ShapeDtypeStruct((1, LANES), F32), jax.ShapeDtypeStruct((SEQ, D_MODEL), F32),
                   jax.ShapeDtypeStruct((1, D_MODEL), F32)],
        compiler_params=_cparams(("arbitrary",)),
    )(x, w, target)


S5_T = 256
S5_BLOCKS = [(slice(j * 256, (j + 1) * 256), slice(j * 1024, (j + 1) * 1024)) for j in range(2)]


def _s5_post(ypre, gate, wglu):
    y = _gelu(ypre)
    y = y * jax.nn.sigmoid(_bdot(y, wglu, 1, 0))
    return y * _silu(gate)


def _s5_fwd(proj, bbre, bbim, cre, cim, a2, dvec, wglu, name):
    def body(u_ref, g_ref, bbre_ref, bbim_ref, cre_ref, cim_ref, a_ref, d_ref, wg_ref, o_ref, sre_ref, sim_ref, st):
        @pl.when(pl.program_id(0) == 0)
        def _():
            st[...] = jnp.zeros_like(st)

        u = u_ref[...]
        ub = u.astype(BF16)
        for us, ss in S5_BLOCKS:
            sre_ref[:, ss] = _dg(ub[:, us], bbre_ref[us, ss], 1, 0)
            sim_ref[:, ss] = _dg(ub[:, us], bbim_ref[us, ss], 1, 0)
        ar, ai = a_ref[0:1, :], a_ref[1:2, :]

        def step(t, carry):
            sr, si = carry
            nr = ar * sr - ai * si + sre_ref[pl.ds(t, 1), :]
            ni = ar * si + ai * sr + sim_ref[pl.ds(t, 1), :]
            sre_ref[pl.ds(t, 1), :] = nr
            sim_ref[pl.ds(t, 1), :] = ni
            return nr, ni

        sr, si = lax.fori_loop(0, S5_T, step, (st[0:1, :], st[1:2, :]), unroll=8)
        st[0:1, :] = sr
        st[1:2, :] = si
        ypre = jnp.concatenate(
            [_dg(sre_ref[:, ss].astype(BF16), cre_ref[ss, us], 1, 0) - _dg(sim_ref[:, ss].astype(BF16), cim_ref[ss, us], 1, 0)
             for us, ss in S5_BLOCKS], axis=1) + d_ref[...] * u
        o_ref[...] = _s5_post(ypre, g_ref[...], wg_ref[...]).astype(BF16)

    full = lambda shape: pl.BlockSpec(shape, lambda c: (0, 0))
    return pl.pallas_call(
        body, name=name, grid=(SEQ // S5_T,),
        in_specs=[pl.BlockSpec((S5_T, BW), lambda c: (c, C_S5U // BW)), pl.BlockSpec((S5_T, BW), lambda c: (c, C_S5G // BW)),
                  full((BW, S5_CH)), full((BW, S5_CH)), full((S5_CH, BW)), full((S5_CH, BW)),
                  full((2, S5_CH)), full((1, BW)), full((BW, BW))],
        out_specs=[pl.BlockSpec((S5_T, BW), lambda c: (c, 0)), pl.BlockSpec((S5_T, S5_CH), lambda c: (c, 0)),
                   pl.BlockSpec((S5_T, S5_CH), lambda c: (c, 0))],
        out_shape=[jax.ShapeDtypeStruct((SEQ, BW), BF16), jax.ShapeDtypeStruct((SEQ, S5_CH), F32),
                   jax.ShapeDtypeStruct((SEQ, S5_CH), F32)],
        scratch_shapes=[pltpu.VMEM((2, S5_CH), F32)],
        compiler_params=_cparams(("arbitrary",)),
    )(proj, proj, bbre, bbim, cre, cim, a2, dvec, wglu)


def _s5_bwd(proj, dproj, dout, sre, sim, bbre, bbim, cre, cim, a2, dvec, wglu, name):
    nc = SEQ // S5_T

    def body(u_ref, g_ref, do_ref, sre_ref, sim_ref, pre_ref, pim_ref, bbre_ref, bbim_ref, cre_ref, cim_ref, a_ref,
             d_ref, wg_ref, dproj_in, dp_ref, dbbre_ref, dbbim_ref, dcre_ref, dcim_ref, da_ref, dd_ref, dwg_ref,
             gre, gim, st):
        c = nc - 1 - pl.program_id(0)

        @pl.when(pl.program_id(0) == 0)
        def _():
            st[...] = jnp.zeros_like(st)
            for r in (dbbre_ref, dbbim_ref, dcre_ref, dcim_ref, da_ref, dd_ref, dwg_ref):
                r[...] = jnp.zeros_like(r)

        u = u_ref[...]
        s_re, s_im = sre_ref[...], sim_ref[...]

        def head(s_res, s_ims, cres, cims, dv, uv, gv, wg):
            ypre = jnp.concatenate([_bdot(sr, cr, 1, 0) - _bdot(si, ci, 1, 0)
                                    for sr, si, cr, ci in zip(s_res, s_ims, cres, cims)], axis=1) + dv * uv
            return _s5_post(ypre, gv, wg)

        _, vjp = jax.vjp(head, [sre_ref[:, ss] for _, ss in S5_BLOCKS], [sim_ref[:, ss] for _, ss in S5_BLOCKS],
                         [cre_ref[ss, us].astype(F32) for us, ss in S5_BLOCKS],
                         [cim_ref[ss, us].astype(F32) for us, ss in S5_BLOCKS],
                         d_ref[...], u, g_ref[...], wg_ref[...].astype(F32))
        ds_res, ds_ims, dcres, dcims, dd, du_d, dgate, dwg = vjp(do_ref[0])
        for k, (us, ss) in enumerate(S5_BLOCKS):
            dcre_ref[ss, us] += dcres[k]
            dcim_ref[ss, us] += dcims[k]
            gre[:, ss] = ds_res[k]
            gim[:, ss] = ds_ims[k]
        dd_ref[...] += dd
        dwg_ref[...] += dwg
        dp_ref[:, BW:] = dgate.astype(BF16)
        ar, ai = a_ref[0:1, :], a_ref[1:2, :]

        def step(i, carry):
            t = S5_T - 1 - i
            gr, gi = carry
            nr = gre[pl.ds(t, 1), :] + gr
            ni = gim[pl.ds(t, 1), :] + gi
            gre[pl.ds(t, 1), :] = nr
            gim[pl.ds(t, 1), :] = ni
            return ar * nr + ai * ni, ar * ni - ai * nr

        gr, gi = lax.fori_loop(0, S5_T, step, (st[0:1, :], st[1:2, :]), unroll=8)
        st[0:1, :] = gr
        st[1:2, :] = gi
        g_re, g_im = gre[...], gim[...]
        first = jnp.where(c > 0, 1.0, 0.0)
        row = lax.broadcasted_iota(jnp.int32, (S5_T, S5_CH), 0)
        p_re = jnp.where(row == 0, pre_ref[7:8, :] * first, pltpu.roll(s_re, 1, 0))
        p_im = jnp.where(row == 0, pim_ref[7:8, :] * first, pltpu.roll(s_im, 1, 0))
        da_ref[0:1, :] += jnp.sum(g_re * p_re + g_im * p_im, axis=0, keepdims=True)
        da_ref[1:2, :] += jnp.sum(g_im * p_re - g_re * p_im, axis=0, keepdims=True)
        ub, grb, gib = u.astype(BF16), g_re.astype(BF16), g_im.astype(BF16)
        du_s = []
        for us, ss in S5_BLOCKS:
            dbbre_ref[us, ss] += _dg(ub[:, us], grb[:, ss], 0, 0)
            dbbim_ref[us, ss] += _dg(ub[:, us], gib[:, ss], 0, 0)
            du_s.append(_dg(grb[:, ss], bbre_ref[us, ss], 1, 1) + _dg(gib[:, ss], bbim_ref[us, ss], 1, 1))
        dp_ref[:, :BW] = (du_d + jnp.concatenate(du_s, axis=1)).astype(BF16)

    full = lambda shape: pl.BlockSpec(shape, lambda i: (0, 0))
    rev = lambda w, col=0: pl.BlockSpec((S5_T, w), lambda i: (nc - 1 - i, col))
    prev = pl.BlockSpec((8, S5_CH), lambda i: (jnp.maximum((nc - 1 - i) * (S5_T // 8) - 1, 0), 0))
    return pl.pallas_call(
        body, name=name, grid=(nc,),
        in_specs=[rev(BW, C_S5U // BW), rev(BW, C_S5G // BW), pl.BlockSpec((1, S5_T, BW), lambda i: (0, nc - 1 - i, 0)),
                  rev(S5_CH), rev(S5_CH), prev, prev,
                  full((BW, S5_CH)), full((BW, S5_CH)), full((S5_CH, BW)), full((S5_CH, BW)),
                  full((2, S5_CH)), full((1, BW)), full((BW, BW)), pl.BlockSpec(memory_space=pl.ANY)],
        out_specs=[rev(2 * BW, C_S5U // (2 * BW)), full((BW, S5_CH)), full((BW, S5_CH)), full((S5_CH, BW)), full((S5_CH, BW)),
                   full((2, S5_CH)), full((1, BW)), full((BW, BW))],
        input_output_aliases={14: 0},
        out_shape=[jax.ShapeDtypeStruct((SEQ, IN_PAD), BF16),
                   jax.ShapeDtypeStruct((BW, S5_CH), F32), jax.ShapeDtypeStruct((BW, S5_CH), F32),
                   jax.ShapeDtypeStruct((S5_CH, BW), F32), jax.ShapeDtypeStruct((S5_CH, BW), F32),
                   jax.ShapeDtypeStruct((2, S5_CH), F32), jax.ShapeDtypeStruct((1, BW), F32),
                   jax.ShapeDtypeStruct((BW, BW), F32)],
        scratch_shapes=[pltpu.VMEM((S5_T, S5_CH), F32), pltpu.VMEM((S5_T, S5_CH), F32), pltpu.VMEM((2, S5_CH), F32)],
        compiler_params=_cparams(("arbitrary",)),
    )(proj, proj, dout, sre, sim, sre, sim, bbre, bbim, cre, cim, a2, dvec, wglu, dproj)


def _diag_blocks(dense, rows_per, cols_per):
    rows, cols = dense.shape
    per_lane_block = LANES // cols_per
    tile = 512

    def body(d_ref, o_ref):
        r0 = pl.program_id(0) * tile
        grp = (r0 + lax.broadcasted_iota(jnp.int32, (tile, LANES), 0)) // rows_per
        lane = lax.broadcasted_iota(jnp.int32, (tile, LANES), 1)
        acc = jnp.zeros((tile, LANES), F32)
        for hb in range(cols // LANES):
            acc = acc + jnp.where(grp == per_lane_block * hb + lane // cols_per, d_ref[:, hb * LANES:(hb + 1) * LANES], 0.0)
        shift = LANES // 2
        while shift >= cols_per:
            acc = acc + pltpu.roll(acc, LANES - shift, 1)
            shift //= 2
        o_ref[...] = acc

    return pl.pallas_call(
        body, name=f"diag_blocks_{rows_per}x{cols_per}", grid=(rows // tile,),
        in_specs=[pl.BlockSpec((tile, cols), lambda i: (i, 0))], out_specs=pl.BlockSpec((tile, LANES), lambda i: (i, 0)),
        out_shape=jax.ShapeDtypeStruct((rows, LANES), F32), compiler_params=_cparams(("parallel",)),
    )(dense)


@jax.custom_vjp
def _block_diag(t):
    g, rows_per, cols_per = t.shape
    wide = jnp.tile(t.reshape(g * rows_per, cols_per), (1, g))
    r = lax.broadcasted_iota(jnp.int32, wide.shape, 0) // rows_per
    c = lax.broadcasted_iota(jnp.int32, wide.shape, 1) // cols_per
    return jnp.where(r == c, wide, 0.0)


def _block_diag_fwd(t):
    return _block_diag(t), None


def _block_diag_bwd(_, dense):
    rows_per, cols_per = dense.shape[0] // S5_GROUPS, dense.shape[1] // S5_GROUPS
    return (_diag_blocks(dense, rows_per, cols_per)[:, :cols_per].reshape(S5_GROUPS, rows_per, cols_per),)


_block_diag.defvjp(_block_diag_fwd, _block_diag_bwd)


def _s5_disc(lam_re, lam_im, b_re, b_im, c_re, c_im, d, log_step):
    step = jnp.exp(log_step)[:, None]
    mag = jnp.exp(lam_re * step)
    ab_re, ab_im = mag * jnp.cos(lam_im * step), mag * jnp.sin(lam_im * step)
    den = lam_re * lam_re + lam_im * lam_im
    nr = ab_re - 1.0
    coef_re = (nr * lam_re + ab_im * lam_im) / den
    coef_im = (ab_im * lam_re - nr * lam_im) / den
    bb_re = coef_re[..., None] * b_re - coef_im[..., None] * b_im
    bb_im = coef_re[..., None] * b_im + coef_im[..., None] * b_re
    bbre = _block_diag(jnp.swapaxes(bb_re, 1, 2))
    bbim = _block_diag(jnp.swapaxes(bb_im, 1, 2))
    cre = _block_diag(jnp.swapaxes(c_re, 1, 2))
    cim = _block_diag(jnp.swapaxes(c_im, 1, 2))
    a2 = jnp.stack([ab_re.reshape(-1), ab_im.reshape(-1)])
    return bbre, bbim, cre, cim, a2, d.reshape(1, BW)


def _left_lanes(shape):
    return lax.broadcasted_iota(jnp.int32, shape, 1) < 64


def _sgu_chunk(u, v, gate, ln_w, ln_b, w, bias):
    u32, v32 = _gelu(u), _gelu(v)
    mu = jnp.mean(v32, axis=-1, keepdims=True)
    var = jnp.mean(jnp.square(v32 - mu), axis=-1, keepdims=True)
    vn = (v32 - mu) * lax.rsqrt(var + EPS) * ln_w + ln_b
    t_i = lax.broadcasted_iota(jnp.int32, (SGU_CHUNK, SGU_CHUNK), 0)
    s_i = lax.broadcasted_iota(jnp.int32, (SGU_CHUNK, SGU_CHUNK), 1)
    causal = t_i >= s_i
    left = _left_lanes((SGU_CHUNK, LANES))
    sgate = _silu(gate)
    outs = []
    for j in range(BW // LANES):
        vb = vn[:, j * LANES:(j + 1) * LANES]
        s_blk = (_bdot(jnp.where(causal, w[2 * j], 0.0), jnp.where(left, vb, 0.0), 1, 0)
                 + _bdot(jnp.where(causal, w[2 * j + 1], 0.0), jnp.where(left, 0.0, vb), 1, 0))
        sl = slice(j * LANES, (j + 1) * LANES)
        outs.append(u32[:, sl] * (s_blk + bias[:, sl]) * sgate[:, sl])
    return outs


def _sgu_fwd(proj, ln_w, ln_b, w, bias, name):
    def body(u_ref, v_ref, g_ref, lw_ref, lb_ref, w_ref, b_ref, o_ref):
        outs = _sgu_chunk(u_ref[...], v_ref[...], g_ref[...], lw_ref[...], lb_ref[...], w_ref[...], b_ref[...])
        for j, o in enumerate(outs):
            o_ref[:, j * LANES:(j + 1) * LANES] = o.astype(BF16)

    blk = lambda col: pl.BlockSpec((SGU_CHUNK, BW), lambda c: (c, col // BW))
    vec = pl.BlockSpec((1, BW), lambda c: (0, 0))
    return pl.pallas_call(
        body, name=name, grid=(SEQ // SGU_CHUNK,),
        in_specs=[blk(C_SGU_U), blk(C_SGU_V), blk(C_SGU_G), vec, vec,
                  pl.BlockSpec((SGU_HEADS, SGU_CHUNK, SGU_CHUNK), lambda c: (0, 0, 0)),
                  pl.BlockSpec((SGU_CHUNK, BW), lambda c: (0, 0))],
        out_specs=pl.BlockSpec((SGU_CHUNK, BW), lambda c: (c, 0)),
        out_shape=jax.ShapeDtypeStruct((SEQ, BW), BF16),
        compiler_params=_cparams(("parallel",)),
    )(proj, proj, proj, ln_w, ln_b, w, bias)


def _sgu_bwd(proj, dproj, dout, ln_w, ln_b, w, bias, name):
    def body(u_ref, v_ref, g_ref, do_ref, lw_ref, lb_ref, w_ref, b_ref, dproj_in, dp_ref, dlw_ref, dlb_ref, dw_ref, db_ref):
        _, vjp = jax.vjp(_sgu_chunk, u_ref[...], v_ref[...], g_ref[...], lw_ref[...], lb_ref[...], w_ref[...], b_ref[...])
        do = do_ref[0]
        du, dv, dgate, dlw, dlb, dw, db = vjp([do[:, j * LANES:(j + 1) * LANES] for j in range(BW // LANES)])
        dp_ref[:, 0:BW] = du.astype(BF16)
        dp_ref[:, BW:2 * BW] = dv.astype(BF16)
        dp_ref[:, 2 * BW:3 * BW] = dgate.astype(BF16)
        dp_ref[:, 3 * BW:] = jnp.zeros((SGU_CHUNK, BW), BF16)

        @pl.when(pl.program_id(0) == 0)
        def _():
            dlw_ref[...] = dlw
            dlb_ref[...] = dlb
            dw_ref[...] = dw
            db_ref[...] = db

        @pl.when(pl.program_id(0) > 0)
        def _():
            dlw_ref[...] += dlw
            dlb_ref[...] += dlb
            dw_ref[...] += dw
            db_ref[...] += db

    blk = lambda col: pl.BlockSpec((SGU_CHUNK, BW), lambda c: (c, col // BW))
    vec = pl.BlockSpec((1, BW), lambda c: (0, 0))
    wsp = pl.BlockSpec((SGU_HEADS, SGU_CHUNK, SGU_CHUNK), lambda c: (0, 0, 0))
    bsp = pl.BlockSpec((SGU_CHUNK, BW), lambda c: (0, 0))
    return pl.pallas_call(
        body, name=name, grid=(SEQ // SGU_CHUNK,),
        in_specs=[blk(C_SGU_U), blk(C_SGU_V), blk(C_SGU_G), pl.BlockSpec((1, SGU_CHUNK, BW), lambda c: (1, c, 0)),
                  vec, vec, wsp, bsp, pl.BlockSpec(memory_space=pl.ANY)],
        out_specs=[pl.BlockSpec((SGU_CHUNK, 4 * BW), lambda c: (c, C_SGU_U // (4 * BW))), vec, vec, wsp, bsp],
        input_output_aliases={8: 0},
        out_shape=[jax.ShapeDtypeStruct((SEQ, IN_PAD), BF16), jax.ShapeDtypeStruct((1, BW), F32),
                   jax.ShapeDtypeStruct((1, BW), F32), jax.ShapeDtypeStruct((SGU_HEADS, SGU_CHUNK, SGU_CHUNK), F32),
                   jax.ShapeDtypeStruct((SGU_CHUNK, BW), F32)],
        compiler_params=_cparams(("arbitrary",)),
    )(proj, proj, proj, dout, ln_w, ln_b, w, bias, dproj)


CONV_BLK = 256


def _m2_conv_fwd(proj, w, b, name):
    def body(x_ref, w_ref, b_ref, o_ref):
        x = x_ref[...]
        acc = jnp.zeros_like(x) + b_ref[...]
        for k in range(M2_CONV):
            acc = acc + w_ref[k:k + 1, :] * _shift_down(x, M2_CONV - 1 - k)
        o_ref[...] = _silu(acc)

    return pl.pallas_call(
        body, name=name, grid=(M2_CONV_CH // CONV_BLK,),
        in_specs=[pl.BlockSpec((SEQ, CONV_BLK), lambda j: (0, C_M2X // CONV_BLK + j)),
                  pl.BlockSpec((M2_CONV, CONV_BLK), lambda j: (0, j)), pl.BlockSpec((1, CONV_BLK), lambda j: (0, j))],
        out_specs=pl.BlockSpec((SEQ, CONV_BLK), lambda j: (0, j)),
        out_shape=jax.ShapeDtypeStruct((SEQ, M2_CONV_CH), F32),
        compiler_params=_cparams(("parallel",)),
    )(proj, w, b)


def _m2_conv_bwd(proj, dproj, dxa, w, b, name):
    def body(x_ref, d_ref, w_ref, b_ref, dproj_in, dx_ref, dw_ref, db_ref):
        x = x_ref[...]
        xs = [_shift_down(x, M2_CONV - 1 - k) for k in range(M2_CONV)]
        acc = jnp.zeros_like(x) + b_ref[...]
        for k in range(M2_CONV):
            acc = acc + w_ref[k:k + 1, :] * xs[k]
        sg = jax.nn.sigmoid(acc)
        dacc = d_ref[...] * (sg * (1.0 + acc * (1.0 - sg)))
        dx = jnp.zeros_like(x)
        for k in range(M2_CONV):
            dx = dx + w_ref[k:k + 1, :] * _shift_up(dacc, M2_CONV - 1 - k)
            dw_ref[k:k + 1, :] = jnp.sum(dacc * xs[k], axis=0, keepdims=True)
        dx_ref[...] = dx.astype(BF16)
        db_ref[...] = jnp.sum(dacc, axis=0, keepdims=True)

    return pl.pallas_call(
        body, name=name, grid=(M2_CONV_CH // CONV_BLK,),
        in_specs=[pl.BlockSpec((SEQ, CONV_BLK), lambda j: (0, C_M2X // CONV_BLK + j)),
                  pl.BlockSpec((SEQ, CONV_BLK), lambda j: (0, j)),
                  pl.BlockSpec((M2_CONV, CONV_BLK), lambda j: (0, j)), pl.BlockSpec((1, CONV_BLK), lambda j: (0, j)),
                  pl.BlockSpec(memory_space=pl.ANY)],
        out_specs=[pl.BlockSpec((SEQ, CONV_BLK), lambda j: (0, C_M2X // CONV_BLK + j)),
                   pl.BlockSpec((M2_CONV, CONV_BLK), lambda j: (0, j)), pl.BlockSpec((1, CONV_BLK), lambda j: (0, j))],
        input_output_aliases={4: 0},
        out_shape=[jax.ShapeDtypeStruct((SEQ, IN_PAD), BF16), jax.ShapeDtypeStruct((M2_CONV, M2_CONV_CH), F32),
                   jax.ShapeDtypeStruct((1, M2_CONV_CH), F32)],
        compiler_params=_cparams(("parallel",)),
    )(proj, dxa, w, b, dproj)


N_PAIR = M2_HEADS // 2
HI = lax.Precision.HIGHEST


def _col(a, h):
    lane = lax.broadcasted_iota(jnp.int32, a.shape, 1)
    return jnp.sum(jnp.where(lane == h, a, 0.0), axis=1, keepdims=True)


def _row(a, h):
    sub = lax.broadcasted_iota(jnp.int32, a.shape, 0)
    return jnp.sum(jnp.where(sub == h, a, 0.0), axis=0, keepdims=True)


def _ssd_chunk(xs, bms, cms, dtr, zs, states, dt_bias, a_log, dfs, nws):
    q = M2_CHUNK
    dt = _softplus(dtr + dt_bias)
    da = dt * (-jnp.exp(a_log))
    l_i = lax.broadcasted_iota(jnp.int32, (q, q), 0)
    s_i = lax.broadcasted_iota(jnp.int32, (q, q), 1)
    causal = l_i >= s_i
    tril = jnp.where(causal, 1.0, 0.0)
    a_cs = _dg(tril, da, 1, 0, HI)
    a_cs_t = _dg(da, tril, 0, 1, HI)
    a_end = _row(a_cs, q - 1)
    left = _left_lanes((q, LANES))
    left1 = _left_lanes((1, LANES))
    ys, nexts = [], []
    for j in range(N_PAIR):
        grp = j // 2
        bm, cm = bms[grp], cms[grp]
        h0, h1 = 2 * j, 2 * j + 1
        cb = _bdot(cm, bm, 1, 1)
        xdt = xs[j] * jnp.where(left, _col(dt, h0), _col(dt, h1))
        acs0, acs1 = _col(a_cs, h0), _col(a_cs, h1)
        y = _bdot(cm, states[j], 1, 0) * jnp.where(left, jnp.exp(acs0), jnp.exp(acs1))
        s_new = states[j] * jnp.where(left1, jnp.exp(_col(a_end, h0)), jnp.exp(_col(a_end, h1)))
        for h, acs, xh in ((h0, acs0, jnp.where(left, xdt, 0.0)), (h1, acs1, jnp.where(left, 0.0, xdt))):
            decay = jnp.exp(jnp.where(causal, acs - _row(a_cs_t, h), -jnp.inf))
            y = y + _bdot(cb * decay, xh, 1, 0)
            s_new = s_new + _bdot(bm * jnp.exp(_col(a_end, h) - acs), xh, 0, 0)
        ys.append((y + dfs[j] * xs[j]) * _silu(zs[j]))
        nexts.append(s_new)
    ssq = sum(jnp.sum(y * y, axis=-1, keepdims=True) for y in ys)
    scale = lax.rsqrt(ssq / BW + EPS)
    return [y * scale * nw for y, nw in zip(ys, nws)], nexts


def _blocks(ref, n, width=LANES):
    return [ref[:, j * width:(j + 1) * width] for j in range(n)]


def _ssd_fwd(proj, xa, dt_bias, a_log, dfull, nw, name):
    nc = SEQ // M2_CHUNK

    def body(x_ref, b_ref, c_ref, dt_ref, z_ref, dtb_ref, al_ref, df_ref, nw_ref, o_ref, sin_ref, st):
        @pl.when(pl.program_id(0) == 0)
        def _():
            st[...] = jnp.zeros_like(st)

        states = [st[j] for j in range(N_PAIR)]
        for j in range(N_PAIR):
            sin_ref[0, j] = states[j]
        ys, nexts = _ssd_chunk(_blocks(x_ref, 4), _blocks(b_ref, 2), _blocks(c_ref, 2), dt_ref[...], _blocks(z_ref, 4),
                               states, dtb_ref[...], al_ref[...], _blocks(df_ref, 4), _blocks(nw_ref, 4))
        for j in range(N_PAIR):
            o_ref[:, j * LANES:(j + 1) * LANES] = ys[j].astype(BF16)
            st[j] = nexts[j]

    vec8 = pl.BlockSpec((1, LANES), lambda c: (0, 0))
    vec = pl.BlockSpec((1, BW), lambda c: (0, 0))
    return pl.pallas_call(
        body, name=name, grid=(nc,),
        in_specs=[pl.BlockSpec((M2_CHUNK, BW), lambda c: (c, 0)), pl.BlockSpec((M2_CHUNK, 256), lambda c: (c, 2)),
                  pl.BlockSpec((M2_CHUNK, 256), lambda c: (c, 3)), pl.BlockSpec((M2_CHUNK, LANES), lambda c: (c, C_DT // LANES)),
                  pl.BlockSpec((M2_CHUNK, BW), lambda c: (c, C_M2Z // BW)), vec8, vec8, vec, vec],
        out_specs=[pl.BlockSpec((M2_CHUNK, BW), lambda c: (c, 0)),
                   pl.BlockSpec((1, N_PAIR, M2_STATE, LANES), lambda c: (c, 0, 0, 0))],
        out_shape=[jax.ShapeDtypeStruct((SEQ, BW), BF16), jax.ShapeDtypeStruct((nc, N_PAIR, M2_STATE, LANES), F32)],
        scratch_shapes=[pltpu.VMEM((N_PAIR, M2_STATE, LANES), F32)],
        compiler_params=_cparams(("arbitrary",)),
    )(xa, xa, xa, proj, proj, dt_bias, a_log, dfull, nw)


def _ssd_bwd(proj, dproj, xa, dout, s_in, dt_bias, a_log, dfull, nw, name):
    nc = SEQ // M2_CHUNK

    def body(x_ref, b_ref, c_ref, dt_ref, z_ref, do_ref, sin_ref, dtb_ref, al_ref, df_ref, nw_ref, dproj_in,
             dp_ref, dxa_ref, ddtb_ref, dal_ref, ddf_ref, dnw_ref, dst):
        @pl.when(pl.program_id(0) == 0)
        def _():
            dst[...] = jnp.zeros_like(dst)
            for r in (ddtb_ref, dal_ref, ddf_ref, dnw_ref):
                r[...] = jnp.zeros_like(r)

        states = [sin_ref[0, j] for j in range(N_PAIR)]
        _, vjp = jax.vjp(_ssd_chunk, _blocks(x_ref, 4), _blocks(b_ref, 2), _blocks(c_ref, 2), dt_ref[...],
                         _blocks(z_ref, 4), states, dtb_ref[...], al_ref[...], _blocks(df_ref, 4), _blocks(nw_ref, 4))
        dxs, dbs, dcs, ddt, dzs, dstates, ddtb, dal, ddfs, dnws = vjp(
            ([do_ref[0, :, j * LANES:(j + 1) * LANES] for j in range(N_PAIR)], [dst[j] for j in range(N_PAIR)]))
        for j in range(N_PAIR):
            sl = slice(j * LANES, (j + 1) * LANES)
            dxa_ref[:, sl] = dxs[j]
            dp_ref[:, sl] = dzs[j].astype(BF16)
            dst[j] = dstates[j]
            ddf_ref[:, sl] += ddfs[j]
            dnw_ref[:, sl] += dnws[j]
        for g in range(2):
            dxa_ref[:, BW + g * LANES:BW + (g + 1) * LANES] = dbs[g]
            dxa_ref[:, BW + 256 + g * LANES:BW + 256 + (g + 1) * LANES] = dcs[g]
        dp_ref[:, BW:BW + LANES] = ddt.astype(BF16)
        dp_ref[:, BW + LANES:] = jnp.zeros((M2_CHUNK, 2 * BW - BW - LANES), BF16)
        ddtb_ref[...] += ddtb
        dal_ref[...] += dal

    rev = lambda w, col=0: pl.BlockSpec((M2_CHUNK, w), lambda i: (nc - 1 - i, col))
    vec8 = pl.BlockSpec((1, LANES), lambda i: (0, 0))
    vec = pl.BlockSpec((1, BW), lambda i: (0, 0))
    return pl.pallas_call(
        body, name=name, grid=(nc,),
        in_specs=[rev(BW), rev(256, 2), rev(256, 3), rev(LANES, C_DT // LANES), rev(BW, C_M2Z // BW),
                  pl.BlockSpec((1, M2_CHUNK, BW), lambda i: (2, nc - 1 - i, 0)),
                  pl.BlockSpec((1, N_PAIR, M2_STATE, LANES), lambda i: (nc - 1 - i, 0, 0, 0)), vec8, vec8, vec, vec,
                  pl.BlockSpec(memory_space=pl.ANY)],
        out_specs=[rev(2 * BW, C_M2Z // (2 * BW)), rev(M2_CONV_CH), vec8, vec8, vec, vec],
        input_output_aliases={11: 0},
        out_shape=[jax.ShapeDtypeStruct((SEQ, IN_PAD), BF16), jax.ShapeDtypeStruct((SEQ, M2_CONV_CH), F32),
                   jax.ShapeDtypeStruct((1, LANES), F32), jax.ShapeDtypeStruct((1, LANES), F32),
                   jax.ShapeDtypeStruct((1, BW), F32), jax.ShapeDtypeStruct((1, BW), F32)],
        scratch_shapes=[pltpu.VMEM((N_PAIR, M2_STATE, LANES), F32)],
        compiler_params=_cparams(("arbitrary",)),
    )(xa, xa, xa, proj, proj, dout, s_in, dt_bias, a_log, dfull, nw, dproj)


def _sc_specs():
    col = lambda kind: pl.BlockSpec((SEQ, LANES), lambda j: (0, C_SC // LANES + 4 * j + kind))
    return [col(0), col(1), col(2), col(3)]


def _sc_fwd(proj, w, name):
    def body(b_ref, c_ref, h_ref, g_ref, w_ref, o_ref):
        ch = c_ref[...] * h_ref[...]
        acc = jnp.zeros_like(ch)
        for k in range(SC_CONV):
            acc = acc + w_ref[k:k + 1, :] * _shift_down(ch, SC_CONV - 1 - k)
        o_ref[...] = (b_ref[...] * acc * _silu(g_ref[...])).astype(BF16)

    return pl.pallas_call(
        body, name=name, grid=(BW // LANES,),
        in_specs=_sc_specs() + [pl.BlockSpec((SC_CONV, LANES), lambda j: (0, j))],
        out_specs=pl.BlockSpec((SEQ, LANES), lambda j: (0, j)),
        out_shape=jax.ShapeDtypeStruct((SEQ, BW), BF16),
        compiler_params=_cparams(("parallel",)),
    )(proj, proj, proj, proj, w)


def _sc_bwd(proj, dproj, dout, w, name):
    def body(b_ref, c_ref, h_ref, g_ref, do_ref, w_ref, dproj_in, dp_ref, dw_ref):
        cv, hv, gv = c_ref[...], h_ref[...], g_ref[...]
        ch = cv * hv
        chs = [_shift_down(ch, SC_CONV - 1 - k) for k in range(SC_CONV)]
        acc = jnp.zeros_like(ch)
        for k in range(SC_CONV):
            acc = acc + w_ref[k:k + 1, :] * chs[k]
        sg = jax.nn.sigmoid(gv)
        do = do_ref[0]
        bv = b_ref[...]
        dp_ref[:, 0:LANES] = (do * acc * (gv * sg)).astype(BF16)
        dp_ref[:, 3 * LANES:] = (do * bv * acc * (sg * (1.0 + gv * (1.0 - sg)))).astype(BF16)
        dacc = do * bv * (gv * sg)
        dch = jnp.zeros_like(ch)
        for k in range(SC_CONV):
            dch = dch + w_ref[k:k + 1, :] * _shift_up(dacc, SC_CONV - 1 - k)
            dw_ref[k:k + 1, :] = jnp.sum(dacc * chs[k], axis=0, keepdims=True)
        dp_ref[:, LANES:2 * LANES] = (dch * hv).astype(BF16)
        dp_ref[:, 2 * LANES:3 * LANES] = (dch * cv).astype(BF16)

    wsp = pl.BlockSpec((SC_CONV, LANES), lambda j: (0, j))
    return pl.pallas_call(
        body, name=name, grid=(BW // LANES,),
        in_specs=_sc_specs() + [pl.BlockSpec((1, SEQ, LANES), lambda j: (3, 0, j)), wsp, pl.BlockSpec(memory_space=pl.ANY)],
        out_specs=[pl.BlockSpec((SEQ, 4 * LANES), lambda j: (0, C_SC // (4 * LANES) + j)), wsp],
        input_output_aliases={6: 0},
        out_shape=[jax.ShapeDtypeStruct((SEQ, IN_PAD), BF16), jax.ShapeDtypeStruct((SC_CONV, BW), F32)],
        compiler_params=_cparams(("parallel",)),
    )(proj, proj, proj, proj, dout, w, dproj)


MERGE_T = 256
MERGE_BWD_T = 512


def _merge_fwd(proj, ys, merge_b, w_branch, name):
    def body(y_ref, lg_ref, b_ref, w_ref, o_ref):
        acc = jnp.zeros((MERGE_T, D_MODEL), F32)
        for k in range(N_BRANCH):
            gate = jax.nn.sigmoid(lg_ref[:, k * D_MODEL:(k + 1) * D_MODEL] + b_ref[k])
            acc = acc + gate * _dg(y_ref[k], w_ref[k], 1, 0)
        o_ref[...] = acc.astype(BF16)

    return pl.pallas_call(
        body, name=name, grid=(SEQ // MERGE_T,),
        in_specs=[pl.BlockSpec((N_BRANCH, MERGE_T, BW), lambda i: (0, i, 0)),
                  pl.BlockSpec((MERGE_T, N_BRANCH * D_MODEL), lambda i: (i, C_MERGE // (N_BRANCH * D_MODEL))),
                  pl.BlockSpec((N_BRANCH, 1, D_MODEL), lambda i: (0, 0, 0)),
                  pl.BlockSpec((N_BRANCH, BW, D_MODEL), lambda i: (0, 0, 0))],
        out_specs=pl.BlockSpec((MERGE_T, D_MODEL), lambda i: (i, 0)),
        out_shape=jax.ShapeDtypeStruct((SEQ, D_MODEL), BF16),
        compiler_params=_cparams(("parallel",)),
    )(ys, proj, merge_b, w_branch)


def _merge_bwd(proj, ys, dm, merge_b, w_branch, name):
    nt = SEQ // MERGE_BWD_T

    def body(y_ref, lg_ref, dm_ref, b_ref, w_ref, dy_ref, dlg_ref, dw_ref, db_ref, dw_acc):
        i = pl.program_id(1)
        gate = jax.nn.sigmoid(lg_ref[...] + b_ref[0])
        y = y_ref[0]
        dmv = dm_ref[...]
        dbo = (gate * dmv).astype(BF16)
        dlg = _dg(y, w_ref[0], 1, 0) * dmv * gate * (1.0 - gate)
        dlg_ref[...] = dlg.astype(BF16)
        dy_ref[0] = _dg(dbo, w_ref[0], 1, 1)
        dwp = _dg(y, dbo, 0, 0)
        dbp = jnp.sum(dlg, axis=0, keepdims=True)

        @pl.when(i == 0)
        def _():
            dw_acc[...] = dwp
            db_ref[0] = dbp

        @pl.when(i > 0)
        def _():
            dw_acc[...] += dwp
            db_ref[0] += dbp

        @pl.when(i == nt - 1)
        def _():
            dw_ref[0] = dw_acc[...].astype(BF16)

    return pl.pallas_call(
        body, name=name, grid=(N_BRANCH, nt),
        in_specs=[pl.BlockSpec((1, MERGE_BWD_T, BW), lambda k, i: (k, i, 0)),
                  pl.BlockSpec((MERGE_BWD_T, D_MODEL), lambda k, i: (i, C_MERGE // D_MODEL + k)),
                  pl.BlockSpec((MERGE_BWD_T, D_MODEL), lambda k, i: (i, 0)),
                  pl.BlockSpec((1, 1, D_MODEL), lambda k, i: (k, 0, 0)),
                  pl.BlockSpec((1, BW, D_MODEL), lambda k, i: (k, 0, 0))],
        out_specs=[pl.BlockSpec((1, MERGE_BWD_T, BW), lambda k, i: (k, i, 0)),
                   pl.BlockSpec((MERGE_BWD_T, D_MODEL), lambda k, i: (i, k)),
                   pl.BlockSpec((1, BW, D_MODEL), lambda k, i: (k, 0, 0)),
                   pl.BlockSpec((1, 1, D_MODEL), lambda k, i: (k, 0, 0))],
        out_shape=[jax.ShapeDtypeStruct((N_BRANCH, SEQ, BW), F32), jax.ShapeDtypeStruct((SEQ, IN_PAD), BF16),
                   jax.ShapeDtypeStruct((N_BRANCH, BW, D_MODEL), BF16), jax.ShapeDtypeStruct((N_BRANCH, 1, D_MODEL), F32)],
        scratch_shapes=[pltpu.VMEM((BW, D_MODEL), F32)],
        compiler_params=_cparams(("parallel", "arbitrary")),
    )(ys, proj, dm, merge_b, w_branch)


def _adamw(glist, w, m, v, rows, name):
    nl = len(glist)
    n, r, c = glist[0].shape
    assert w.shape == (nl, r, c) and r % rows == 0
    nb = r // rows

    def body(*refs):
        g_refs = refs[:nl]
        w_ref, m_ref, v_ref, go_ref, d_ref, mo_ref, vo_ref = refs[nl:]
        for layer in range(nl):
            @pl.when(pl.program_id(0) == layer)
            def _(g_ref=g_refs[layer]):
                g = g_ref[0].astype(F32)
                for s in range(1, n):
                    g = g + g_ref[s].astype(F32)
                mn = ADAM_B1 * m_ref[0] + (1.0 - ADAM_B1) * g
                vn = ADAM_B2 * v_ref[0] + (1.0 - ADAM_B2) * jnp.square(g)
                m_hat = mn / (1.0 - ADAM_B1 ** ADAM_STEP)
                v_hat = vn / (1.0 - ADAM_B2 ** ADAM_STEP)
                go_ref[0] = g
                d_ref[0] = -ADAM_LR * (m_hat / (jnp.sqrt(v_hat) + ADAM_EPS) + ADAM_WD * w_ref[0])
                mo_ref[0] = mn
                vo_ref[0] = vn

    def g_spec(layer):
        return pl.BlockSpec((n, rows, c), lambda a, i: (0, jnp.where(a < layer, 0, jnp.where(a == layer, i, nb - 1)), 0))

    blk = pl.BlockSpec((1, rows, c), lambda a, i: (a, i, 0))
    out = jax.ShapeDtypeStruct((nl, r, c), F32)
    return pl.pallas_call(
        body, name=name, grid=(nl, nb),
        in_specs=[g_spec(layer) for layer in range(nl)] + [blk, blk, blk],
        out_specs=[blk, blk, blk, blk], out_shape=[out, out, out, out],
        compiler_params=_cparams(("arbitrary", "arbitrary")),
    )(*glist, w, m, v)


X_ROWS_PER_COL = 2 * (D_MODEL // LANES)


def _w_in_to_x(w):
    t = jnp.transpose(w, (2, 0, 1)).reshape(SHARD_IN, DEPTH, D_MODEL // LANES, LANES)
    return jnp.transpose(t, (0, 2, 1, 3)).reshape(SHARD_IN * X_ROWS_PER_COL, LANES)


def _w_in_from_x(xv):
    t = jnp.transpose(xv.reshape(SHARD_IN, D_MODEL // LANES, DEPTH, LANES), (0, 2, 1, 3))
    return jnp.transpose(t.reshape(SHARD_IN, DEPTH, D_MODEL), (1, 2, 0))


def _adamw_w_in(glist, w, m, v, name):
    n = glist[0].shape[0]
    cols = LANES
    rows = cols * X_ROWS_PER_COL

    def body(g0_ref, g1_ref, w_ref, m_ref, v_ref, go_ref, d_ref, mo_ref, vo_ref):
        for layer, g_ref in enumerate((g0_ref, g1_ref)):
            g = g_ref[0].astype(F32)
            for s in range(1, n):
                g = g + g_ref[s].astype(F32)
            gt = g.T
            for t in range(D_MODEL // LANES):
                sel = (pl.ds(2 * t + layer, cols, stride=X_ROWS_PER_COL), slice(None))
                gs = gt[:, t * LANES:(t + 1) * LANES]
                mn = ADAM_B1 * m_ref[sel] + (1.0 - ADAM_B1) * gs
                vn = ADAM_B2 * v_ref[sel] + (1.0 - ADAM_B2) * jnp.square(gs)
                m_hat = mn / (1.0 - ADAM_B1 ** ADAM_STEP)
                v_hat = vn / (1.0 - ADAM_B2 ** ADAM_STEP)
                go_ref[sel] = gs
                d_ref[sel] = -ADAM_LR * (m_hat / (jnp.sqrt(v_hat) + ADAM_EPS) + ADAM_WD * w_ref[sel])
                mo_ref[sel] = mn
                vo_ref[sel] = vn

    g_spec = pl.BlockSpec((n, D_MODEL, cols), lambda i: (0, 0, i))
    blk = pl.BlockSpec((rows, LANES), lambda i: (i, 0))
    out = jax.ShapeDtypeStruct((SHARD_IN * X_ROWS_PER_COL, LANES), F32)
    res = pl.pallas_call(
        body, name=name, grid=(-(-SHARD_IN // cols),),
        in_specs=[g_spec, g_spec, blk, blk, blk], out_specs=[blk, blk, blk, blk], out_shape=[out, out, out, out],
        compiler_params=_cparams(("parallel",)),
    )(*glist, _w_in_to_x(w), _w_in_to_x(m), _w_in_to_x(v))
    return [_w_in_from_x(o) for o in res]


def _adamw_many(gs, ws, ms, vs, name):
    k = len(gs)

    def body(*refs):
        g_refs, w_refs, m_refs, v_refs = refs[:k], refs[k:2 * k], refs[2 * k:3 * k], refs[3 * k:4 * k]
        d_refs, mo_refs, vo_refs = refs[4 * k:5 * k], refs[5 * k:6 * k], refs[6 * k:7 * k]
        for i in range(k):
            g = g_refs[i][...]
            mn = ADAM_B1 * m_refs[i][...] + (1.0 - ADAM_B1) * g
            vn = ADAM_B2 * v_refs[i][...] + (1.0 - ADAM_B2) * jnp.square(g)
            m_hat = mn / (1.0 - ADAM_B1 ** ADAM_STEP)
            v_hat = vn / (1.0 - ADAM_B2 ** ADAM_STEP)
            d_refs[i][...] = -ADAM_LR * (m_hat / (jnp.sqrt(v_hat) + ADAM_EPS) + ADAM_WD * w_refs[i][...])
            mo_refs[i][...] = mn
            vo_refs[i][...] = vn

    whole = pl.BlockSpec(memory_space=pltpu.VMEM)
    shapes = [jax.ShapeDtypeStruct(w.shape, F32) for w in ws]
    outs = pl.pallas_call(
        body, name=name, in_specs=[whole] * (4 * k), out_specs=[whole] * (3 * k), out_shape=shapes * 3,
        compiler_params=_cparams(None),
    )(*gs, *ws, *ms, *vs)
    return outs[:k], outs[k:2 * k], outs[2 * k:]


MEMORY_ORDER = {'s5_b_re': (0, 1, 3, 2), 's5_b_im': (0, 1, 3, 2), 's5_d': (0, 2, 1), 'sc_conv_w': (1, 0, 2)}


def _memory_view(name, t):
    return jnp.transpose(t, MEMORY_ORDER[name]) if name in MEMORY_ORDER else t


def _slot_sum(gslots, name):
    n, r, c = gslots.shape

    def body(g_ref, o_ref):
        g = g_ref[0]
        for s in range(1, n):
            g = g + g_ref[s]
        o_ref[...] = g

    return pl.pallas_call(
        body, name=name, in_specs=[pl.BlockSpec((n, r, c), lambda: (0, 0, 0))],
        out_specs=pl.BlockSpec((r, c), lambda: (0, 0)), out_shape=jax.ShapeDtypeStruct((r, c), F32),
        compiler_params=_cparams(None),
    )(gslots)


def _me_and_peers():
    x, y, c = lax.axis_index("x"), lax.axis_index("y"), lax.axis_index("c")
    me = 4 * x + 2 * y + c
    peers = []
    for k in range(1, N_DEV):
        px = 1 - x if (k >> 2) & 1 else x
        py = 1 - y if (k >> 1) & 1 else y
        pc = 1 - c if k & 1 else c
        peers.append((4 * px + 2 * py + pc, (px, py, pc)))
    return me, peers


def _exchange(tensors, gather, name):
    n = len(tensors)

    def body(*refs):
        ins, outs = refs[:n], refs[n:2 * n]
        send_sems, recv_sems, local_sems = refs[2 * n:]
        me, peers = _me_and_peers()
        started = []
        for t in range(n):
            own = pltpu.make_async_copy(ins[t] if gather else ins[t].at[me], outs[t].at[me], local_sems.at[t])
            own.start()
            started.append(own)
            for k, (pidx, pos) in enumerate(peers):
                cp = pltpu.make_async_remote_copy(
                    src_ref=ins[t] if gather else ins[t].at[pidx], dst_ref=outs[t].at[me],
                    send_sem=send_sems.at[t, k], recv_sem=recv_sems.at[t, k], device_id=pos, device_id_type=MESH)
                cp.start()
                started.append(cp)
        for cp in started:
            cp.wait()

    any_spec = pl.BlockSpec(memory_space=pl.ANY)
    outs = pl.pallas_call(
        body, name=name, in_specs=[any_spec] * n, out_specs=[any_spec] * n,
        out_shape=[jax.ShapeDtypeStruct(((N_DEV,) + t.shape) if gather else t.shape, t.dtype) for t in tensors],
        scratch_shapes=[pltpu.SemaphoreType.DMA((n, N_DEV - 1)), pltpu.SemaphoreType.DMA((n, N_DEV - 1)),
                        pltpu.SemaphoreType.DMA((n,))],
        compiler_params=pltpu.CompilerParams(has_side_effects=True),
    )(*tensors)
    return list(outs)


_HBM = pl.BlockSpec(memory_space=pltpu.HBM)
_SEM = pl.BlockSpec(memory_space=pltpu.SEMAPHORE)
_EFFECT = pltpu.SideEffectType.DATAFLOW_SIDE_EFFECTING


N_CHIP = N_DEV // 2


def _chip_peers():
    x, y, c = lax.axis_index("x"), lax.axis_index("y"), lax.axis_index("c")
    chips = []
    for d in range(1, N_CHIP):
        px = 1 - x if (d >> 1) & 1 else x
        py = 1 - y if d & 1 else y
        chips.append((2 * px + py, (px, py)))
    return (x, y, c), 2 * x + y, chips


def _plan_gather(ins, lands, send_sems, recv_sems, local_sems, first=0):
    (x, y, c), q, chips = _chip_peers()
    me = 2 * q + c
    plan = dict(start=[], relay_wait=[], relay_start=[], local=[], sends=[], recvs=[])
    for t in range(len(ins)):
        base = (first + t) * 7
        sem = lambda k: dict(send_sem=send_sems.at[base + k], recv_sem=recv_sems.at[base + k], device_id_type=MESH)
        own = pltpu.make_async_copy(ins[t], lands[t].at[me], local_sems.at[first + t])
        to_sib = pltpu.make_async_remote_copy(src_ref=ins[t], dst_ref=lands[t].at[me], device_id=(x, y, 1 - c), **sem(0))
        plan['start'] += [own, to_sib]
        plan['local'].append(own)
        plan['sends'].append(to_sib)
        plan['recvs'].append(to_sib)
        for d, (pq, (px, py)) in enumerate(chips):
            to_chip = pltpu.make_async_remote_copy(src_ref=ins[t], dst_ref=lands[t].at[me], device_id=(px, py, c), **sem(1 + d))
            blk = lands[t].at[2 * pq + c]
            fwd = pltpu.make_async_remote_copy(src_ref=blk, dst_ref=blk, device_id=(x, y, 1 - c), **sem(4 + d))
            plan['start'].append(to_chip)
            plan['relay_wait'].append(to_chip)
            plan['relay_start'].append(fwd)
            plan['sends'] += [to_chip, fwd]
            plan['recvs'].append(fwd)
    return plan


def _plan_pair(ins, lands, send_sems, recv_sems, local_sems):
    (x, y, c), q, chips = _chip_peers()
    plan = dict(start=[], local=[], sends=[], recvs=[])
    for t in range(len(ins)):
        for k in range(N_CHIP):
            cp = pltpu.make_async_remote_copy(
                src_ref=ins[t].at[2 * k + 1 - c], dst_ref=lands[t].at[k], send_sem=send_sems.at[t * N_CHIP + k],
                recv_sem=recv_sems.at[t * N_CHIP + k], device_id=(x, y, 1 - c), device_id_type=MESH)
            plan['start'].append(cp)
            plan['sends'].append(cp)
            plan['recvs'].append(cp)
    return plan


def _plan_chips(ins, lands, send_sems, recv_sems, local_sems):
    (x, y, c), q, chips = _chip_peers()
    plan = dict(start=[], local=[], sends=[], recvs=[])
    for t in range(len(ins)):
        own = pltpu.make_async_copy(ins[t].at[q], lands[t].at[q], local_sems.at[t])
        plan['start'].append(own)
        plan['local'].append(own)
        for d, (pq, (px, py)) in enumerate(chips):
            cp = pltpu.make_async_remote_copy(
                src_ref=ins[t].at[pq], dst_ref=lands[t].at[q], send_sem=send_sems.at[t * 3 + d],
                recv_sem=recv_sems.at[t * 3 + d], device_id=(px, py, c), device_id_type=MESH)
            plan['start'].append(cp)
            plan['sends'].append(cp)
            plan['recvs'].append(cp)
    return plan


def _split_start(plan_fn, tensors, land_shapes, n_sems, name, after=None):
    n = len(tensors)
    extra = [] if after is None else [after]

    def body(*refs):
        ins, lands = refs[:n], refs[n:2 * n]
        plan = plan_fn(ins, lands, *refs[2 * n + len(extra):2 * n + len(extra) + 3])
        for cp in plan['start']:
            cp.start()
        refs[-1][...] = jnp.zeros_like(refs[-1])

    outs = pl.pallas_call(
        body, name=name,
        out_shape=(pltpu.SemaphoreType.DMA((n_sems,)), pltpu.SemaphoreType.DMA((n_sems,)), pltpu.SemaphoreType.DMA((n,)),
                   *[pltpu.HBM(t.shape, t.dtype) for t in tensors],
                   *[pltpu.HBM(s, t.dtype) for s, t in zip(land_shapes, tensors)],
                   jax.ShapeDtypeStruct((8, LANES), F32)),
        in_specs=[_HBM] * (2 * n) + [pl.BlockSpec(memory_space=pl.ANY)] * len(extra),
        out_specs=(_SEM, _SEM, _SEM, *[_HBM] * (2 * n), pl.BlockSpec(memory_space=pltpu.VMEM)),
        input_output_aliases={t: 3 + t for t in range(2 * n)},
        compiler_params=pltpu.CompilerParams(has_side_effects=_EFFECT),
    )(*[pltpu.with_memory_space_constraint(t, pltpu.HBM) for t in tensors],
      *[pltpu.with_memory_space_constraint(lax.empty(s, t.dtype), pltpu.HBM) for s, t in zip(land_shapes, tensors)], *extra)
    return outs[:-1], outs[-1]


def _split_relay(plan_fn, state, after, name):
    sems, thru = state[:3], state[3:]
    n = len(thru) // 2

    def arrived(*refs):
        plan = plan_fn(refs[:n], refs[n:2 * n], *refs[2 * n:2 * n + 3])
        for cp in plan['relay_wait']:
            cp.wait_recv()

    thru = pl.pallas_call(
        arrived, name=name + "_arrived",
        out_shape=tuple(pltpu.HBM(t.shape, t.dtype) for t in thru),
        in_specs=[_HBM] * (2 * n) + [_SEM, _SEM, _SEM, pl.BlockSpec(memory_space=pl.ANY)],
        out_specs=tuple([_HBM] * (2 * n)),
        input_output_aliases={t: t for t in range(2 * n)},
        compiler_params=pltpu.CompilerParams(has_side_effects=_EFFECT),
    )(*thru, *sems, after)

    def forward(*refs):
        plan = plan_fn(refs[:n], refs[n:2 * n], *refs[2 * n:2 * n + 3])
        for cp in plan['relay_start']:
            cp.start()
        refs[-1][...] = jnp.zeros_like(refs[-1])

    outs = pl.pallas_call(
        forward, name=name + "_forward",
        out_shape=(*[pltpu.HBM(t.shape, t.dtype) for t in thru], jax.ShapeDtypeStruct((8, LANES), F32)),
        in_specs=[_HBM] * (2 * n) + [_SEM, _SEM, _SEM],
        out_specs=(*[_HBM] * (2 * n), pl.BlockSpec(memory_space=pltpu.VMEM)),
        input_output_aliases={t: t for t in range(2 * n)},
        compiler_params=pltpu.CompilerParams(has_side_effects=_EFFECT),
    )(*thru, *sems)
    return (*sems, *outs[:-1]), outs[-1]


def _split_wait(plan_fn, state, after, name, with_sources=False):
    sems, thru = state[:3], state[3:]
    n = len(thru) // 2

    def body(*refs):
        plan = plan_fn(refs[:n], refs[n:2 * n], *refs[2 * n:2 * n + 3])
        for cp in plan['local']:
            cp.wait()
        for cp in plan['sends']:
            cp.wait_send()
        for cp in plan['recvs']:
            cp.wait_recv()

    outs = pl.pallas_call(
        body, name=name,
        out_shape=tuple(pltpu.HBM(t.shape, t.dtype) for t in thru),
        in_specs=[_HBM] * (2 * n) + [_SEM, _SEM, _SEM, pl.BlockSpec(memory_space=pl.ANY)],
        out_specs=tuple([_HBM] * (2 * n)),
        input_output_aliases={t: t for t in range(2 * n)},
        compiler_params=pltpu.CompilerParams(has_side_effects=_EFFECT),
    )(*thru, *sems, after)
    return (list(outs[:n]), list(outs[n:])) if with_sources else list(outs[n:])


PAIR_SUM_BLOCK = 512 * 1024


def _pair_sum(mine, theirs, name):
    _, r, c = mine.shape
    rows = r
    while rows * c > PAIR_SUM_BLOCK and rows % 32 == 0:
        rows //= 2

    def body(core_ref, a_ref, b_ref, o_ref):
        o_ref[0] = (a_ref[0].astype(F32) + b_ref[0].astype(F32)).astype(o_ref.dtype)

    return pl.pallas_call(
        body, name=name,
        grid_spec=pltpu.PrefetchScalarGridSpec(
            num_scalar_prefetch=1, grid=(N_CHIP, r // rows),
            in_specs=[pl.BlockSpec((1, rows, c), lambda k, i, core: (2 * k + core[0], i, 0)),
                      pl.BlockSpec((1, rows, c), lambda k, i, core: (k, i, 0))],
            out_specs=pl.BlockSpec((1, rows, c), lambda k, i, core: (k, i, 0))),
        out_shape=jax.ShapeDtypeStruct((N_CHIP, r, c), mine.dtype),
        compiler_params=_cparams(("parallel", "parallel")),
    )(lax.axis_index("c").astype(jnp.int32).reshape(1), mine, theirs)


WEIGHTS = ['norm_w', 'w_in', 's5_lambda_re', 's5_lambda_im', 's5_b_re', 's5_b_im', 's5_c_re', 's5_c_im', 's5_d',
           's5_log_step', 's5_w_glu', 'sgu_ln_w', 'sgu_ln_b', 'sgu_w', 'sgu_b', 'm2_conv_w', 'm2_conv_b', 'm2_dt_bias',
           'm2_a_log', 'm2_d', 'm2_norm_w', 'sc_conv_w', 'merge_b', 'w_branch', 'w_out', 'final_norm_w']
BIG_SHARDED = ['w_in', 'w_branch', 'w_out', 's5_w_glu']
SMALL_SHARDED = ['m2_conv_w', 'sc_conv_w', 'merge_b']
REPLICATED = [n for n in WEIGHTS if n not in BIG_SHARDED + SMALL_SHARDED]
S5_NAMES = ['s5_lambda_re', 's5_lambda_im', 's5_b_re', 's5_b_im', 's5_c_re', 's5_c_im', 's5_d', 's5_log_step']


def _sc_interleave(t):
    lead = t.shape[:-1]
    return jnp.swapaxes(t.reshape(lead + (4, 4, LANES)), -3, -2).reshape(lead + (4 * BW,))


def _pad_in(w):
    z = lambda n: jnp.zeros(w.shape[:-1] + (n,), w.dtype)
    return jnp.concatenate([w[..., 6152:], w[..., 0:1024], w[..., 3072:4096], w[..., 1024:2560], z(512),
                            w[..., 2560:3072], w[..., 4096:4104], z(504), _sc_interleave(w[..., 4104:6152])], axis=-1)


def _unpad_in(g):
    return jnp.concatenate([g[..., C_S5U:C_S5U + 1024], g[..., C_SGU_U:C_SGU_U + 1536], g[..., C_M2Z:C_M2Z + 512],
                            g[..., C_M2X:C_M2X + 1024], g[..., C_DT:C_DT + 8], _sc_interleave(g[..., C_SC:]),
                            g[..., :N_BRANCH * D_MODEL]], axis=-1)


ROW_BLOCK = 8 * LANES


def _pack_rows(tensors, row_mult, batched=False):
    parts = []
    for t in tensors:
        f = t.reshape((t.shape[0], -1) if batched else (1, -1))
        f = jnp.pad(f, ((0, 0), (0, (-f.shape[1]) % ROW_BLOCK)))
        parts.append(f.reshape(f.shape[0], -1, LANES))
    out = jnp.concatenate(parts, axis=1)
    out = jnp.pad(out, ((0, 0), (0, (-out.shape[1]) % row_mult), (0, 0)))
    return out if batched else out[0]


def _unpack_rows(rows, shapes):
    out, r0 = [], 0
    for shp in shapes:
        size = 1
        for s in shp:
            size *= s
        nr = -(-size // ROW_BLOCK) * 8
        out.append(rows[r0:r0 + nr].reshape(-1)[:size].reshape(shp))
        r0 += nr
    return out


def _kernel_col_map():
    m = np.full(IN_PAD, -1, np.int64)
    m[C_MERGE:C_MERGE + 4096] = np.arange(6152, 10248)
    m[C_S5U:C_S5U + 1024] = np.arange(0, 1024)
    m[C_M2X:C_M2X + 1024] = np.arange(3072, 4096)
    m[C_SGU_U:C_SGU_U + 1536] = np.arange(1024, 2560)
    m[C_M2Z:C_M2Z + 512] = np.arange(2560, 3072)
    m[C_DT:C_DT + 8] = np.arange(4096, 4104)
    for j in range(4):
        for kind in range(4):
            k0 = C_SC + 4 * LANES * j + LANES * kind
            m[k0:k0 + LANES] = 4104 + BW * kind + LANES * j + np.arange(LANES)
    return m


def _lane_pieces(sources):
    pieces, cur = [], None
    for lane, src in enumerate(sources):
        key = None if src is None else (src[0], src[1] // LANES, (lane - src[1]) % LANES)
        if cur is not None and key == cur[0]:
            cur[2] = lane + 1
        else:
            if cur is not None and cur[0] is not None:
                pieces.append((*cur[0], cur[1], cur[2]))
            cur = [key, lane, lane + 1]
    if cur is not None and cur[0] is not None:
        pieces.append((*cur[0], cur[1], cur[2]))
    return pieces


def _assemble_block(pieces, load, rows, dtype):
    lane = lax.broadcasted_iota(jnp.int32, (rows, LANES), 1)
    out = None
    for arr, sb, shift, lo, hi in pieces:
        v = load(arr, sb)
        if shift:
            v = pltpu.roll(v, shift, 1)
        if out is None and lo == 0 and hi == LANES:
            out = v
        else:
            out = jnp.where((lane >= lo) & (lane < hi), v, jnp.zeros((rows, LANES), dtype) if out is None else out)
    return jnp.zeros((rows, LANES), dtype) if out is None else out


RELAYOUT_ROWS = 256
SHARD_BLOCKS = -(-SHARD_IN // LANES)


def _load_shard_block(ref, rows):
    def load(j, sb):
        if sb == SHARD_BLOCKS - 1:
            return jnp.broadcast_to(ref[j, :, SHARD_IN - 1:SHARD_IN], (rows, LANES))
        return ref[j, :, sb * LANES:(sb + 1) * LANES]
    return load


def _relayout_w_in(gathered, name):
    kmap = _kernel_col_map()
    dtype = gathered.dtype

    def body(src_ref, o_ref):
        load = _load_shard_block(src_ref, RELAYOUT_ROWS)
        for ob in range(IN_PAD // LANES):
            srcs = [None if kmap[ob * LANES + l] < 0 else (int(kmap[ob * LANES + l]) // SHARD_IN, int(kmap[ob * LANES + l]) % SHARD_IN)
                    for l in range(LANES)]
            o_ref[:, ob * LANES:(ob + 1) * LANES] = _assemble_block(_lane_pieces(srcs), load, RELAYOUT_ROWS, dtype)

    return pl.pallas_call(
        body, name=name, grid=(D_MODEL // RELAYOUT_ROWS,),
        in_specs=[pl.BlockSpec((N_DEV, RELAYOUT_ROWS, SHARD_IN), lambda i: (0, i, 0))],
        out_specs=pl.BlockSpec((RELAYOUT_ROWS, IN_PAD), lambda i: (i, 0)),
        out_shape=jax.ShapeDtypeStruct((D_MODEL, IN_PAD), dtype),
        compiler_params=_cparams(("parallel",)),
    )(gathered)


def _relayout_g_in(gw, name):
    kmap = _kernel_col_map()
    kinv = np.zeros(IN_DIM, np.int64)
    kinv[kmap[kmap >= 0]] = np.nonzero(kmap >= 0)[0]
    dtype = gw.dtype

    def body(src_ref, o_ref):
        load = lambda _, sb: src_ref[:, sb * LANES:(sb + 1) * LANES]
        for j in range(N_DEV):
            for ob in range(SHARD_BLOCKS):
                srcs = [(0, int(kinv[SHARD_IN * j + ob * LANES + l])) if ob * LANES + l < SHARD_IN else None for l in range(LANES)]
                blk = _assemble_block(_lane_pieces(srcs), load, RELAYOUT_ROWS, dtype)
                if ob == SHARD_BLOCKS - 1:
                    o_ref[j, :, SHARD_IN - 1:SHARD_IN] = blk[:, 0:1]
                else:
                    o_ref[j, :, ob * LANES:(ob + 1) * LANES] = blk

    return pl.pallas_call(
        body, name=name, grid=(D_MODEL // RELAYOUT_ROWS,),
        in_specs=[pl.BlockSpec((RELAYOUT_ROWS, IN_PAD), lambda i: (i, 0))],
        out_specs=pl.BlockSpec((N_DEV, RELAYOUT_ROWS, SHARD_IN), lambda i: (0, i, 0)),
        out_shape=jax.ShapeDtypeStruct((N_DEV, D_MODEL, SHARD_IN), dtype),
        compiler_params=_cparams(("parallel",)),
    )(gw)


def _rows128(flat, row_mult=8):
    n = flat.shape[0]
    per = LANES * row_mult
    total = -(-n // per) * per
    return jnp.pad(flat, (0, total - n)).reshape(total // LANES, LANES)


def _pad_lanes(v):
    return jnp.pad(v, (0, LANES - v.shape[0])).reshape(1, LANES)


def _layer_prep(i, p):
    disc, disc_vjp = jax.vjp(_s5_disc, *[p[n][i] for n in S5_NAMES])
    prep = dict(
        nw=p['norm_w'][i].reshape(1, D_MODEL), disc_vjp=disc_vjp,
        s5small=[t.astype(BF16) for t in disc[:4]] + [disc[4], disc[5]],
        sgw=[p['sgu_ln_w'][i].reshape(1, BW), p['sgu_ln_b'][i].reshape(1, BW), p['sgu_w'][i],
             jnp.repeat(p['sgu_b'][i].T, BW // SGU_HEADS, axis=1)],
        cb=p['m2_conv_b'][i].reshape(1, M2_CONV_CH),
        m2w=[_pad_lanes(p['m2_dt_bias'][i]), _pad_lanes(p['m2_a_log'][i]),
             jnp.repeat(p['m2_d'][i], M2_HEAD_DIM).reshape(1, BW), p['m2_norm_w'][i].reshape(1, BW)])
    touch = [t[0, 0].astype(F32) for t in prep['s5small']] + [prep['sgw'][3][0, 0], prep['m2w'][2][0, 0]]
    return prep, sum(touch[1:], touch[0])


def _layer_fwd(x, h, i, prep, w_in, other_weights, before_merge=None):
    proj = _matmul(h, w_in, 1, 0, F32, 1024, 1024, 1024, f"proj{i}")
    full = dict(other_weights(proj), w_in=w_in)
    s5w = prep['s5small'] + [full['s5_w_glu']]
    ya, sre, sim = _s5_fwd(proj, *s5w, f"s5_fwd{i}")
    yb = _sgu_fwd(proj, *prep['sgw'], f"sgu_fwd{i}")
    cw = full['m2_conv_w']
    xa = _m2_conv_fwd(proj, cw, prep['cb'], f"m2conv_fwd{i}")
    yc, s_in = _ssd_fwd(proj, xa, *prep['m2w'], f"ssd_fwd{i}")
    scw = full['sc_conv_w']
    yd = _sc_fwd(proj, scw, f"sc_fwd{i}")
    ys = jnp.stack([ya, yb, yc, yd])
    mb = full['merge_b'].reshape(N_BRANCH, 1, D_MODEL)
    if before_merge is not None:
        mb = mb + before_merge(ys)[0, 0]
    merged = _merge_fwd(proj, ys, mb, full['w_branch'], f"merge_fwd{i}")
    x_new = _matmul(merged, full['w_out'], 1, 0, F32, 1024, 1024, 1024, f"out{i}", residual=x)
    saved = dict(x=x, nw=prep['nw'], h=h, proj=proj, disc_vjp=prep['disc_vjp'], s5w=s5w, sre=sre, sim=sim, sgw=prep['sgw'],
                 cw=cw, cb=prep['cb'], xa=xa, m2w=prep['m2w'], s_in=s_in, scw=scw, ys=ys, mb=mb, merged=merged)
    return x_new, saved, full


def _layer_bwd(dx_out, i, sv, full, on_large_grads=None, after_dh=None):
    g = {}
    proj = sv['proj']
    dm = _matmul(dx_out, full['w_out'], 1, 1, F32, 1024, 1024, 1024, f"dmerged{i}")
    g['w_out'] = _matmul(sv['merged'], dx_out, 0, 0, BF16, 1024, 1024, 1024, f"gw_out{i}")
    dys, dproj, g['w_branch'], dmb = _merge_bwd(proj, sv['ys'], dm, sv['mb'], full['w_branch'], f"merge_bwd{i}")
    g['merge_b'] = dmb.reshape(N_BRANCH, D_MODEL)
    dproj, dbbre, dbbim, dcre, dcim, da, dd, dwg = _s5_bwd(proj, dproj, dys, sv['sre'], sv['sim'], *sv['s5w'], f"s5_bwd{i}")
    g['s5_dense'] = (dbbre, dbbim, dcre, dcim, da, dd)
    g['s5_w_glu'] = dwg.astype(BF16)
    dproj, dlw, dlb, g['sgu_w'], dbias = _sgu_bwd(proj, dproj, dys, *sv['sgw'], f"sgu_bwd{i}")
    g['sgu_ln_w'], g['sgu_ln_b'] = dlw[0], dlb[0]
    g['sgu_b'] = dbias.reshape(SGU_CHUNK, SGU_HEADS, BW // SGU_HEADS).sum(-1).T
    dproj, dxa, ddtb, dal, ddf, dnw = _ssd_bwd(proj, dproj, sv['xa'], dys, sv['s_in'], *sv['m2w'], f"ssd_bwd{i}")
    dproj, g['m2_conv_w'], dcb = _m2_conv_bwd(proj, dproj, dxa, sv['cw'], sv['cb'], f"m2conv_bwd{i}")
    g['m2_conv_b'], g['m2_norm_w'] = dcb[0], dnw[0]
    g['m2_dt_bias'], g['m2_a_log'] = ddtb[0, :M2_HEADS], dal[0, :M2_HEADS]
    g['m2_d'] = ddf.reshape(M2_HEADS, M2_HEAD_DIM).sum(-1)
    dproj, g['sc_conv_w'] = _sc_bwd(proj, dproj, dys, sv['scw'], f"sc_bwd{i}")
    g['w_in'] = _matmul(sv['h'], dproj, 0, 0, BF16, 1024, 1024, 1024, f"gw_in{i}")
    tok = on_large_grads(g) if on_large_grads else None
    dh = _matmul(dproj, full['w_in'], 1, 1, F32, 1024, 1024, 1024, f"dh{i}", after=tok)
    nw = sv['nw'] if after_dh is None else sv['nw'] + after_dh(dh)[0, 0]
    dx_in, dnw_l = _rmsnorm_bwd(sv['x'], nw, dh, dx_out, f"rms_bwd{i}")
    g['norm_w'] = dnw_l[0]
    return dx_in, g


def _split8(t, axis):
    shp = t.shape
    t = t.reshape(shp[:axis] + (N_DEV, shp[axis] // N_DEV) + shp[axis + 1:])
    return jnp.moveaxis(t, axis, 0)


def _join8(t, axis):
    t = jnp.moveaxis(t, 0, axis)
    shp = t.shape
    return t.reshape(shp[:axis] + (shp[axis] * shp[axis + 1],) + shp[axis + 2:])


SHARD_AXIS = {'w_in': 2, 'w_branch': 3, 'w_out': 1, 's5_w_glu': 1, 'm2_conv_w': 2, 'sc_conv_w': 2, 'merge_b': 2}


OTHER_BIG = [n for n in BIG_SHARDED if n != 'w_in']


def _other_weights(gathered):
    return {n: _join8(t, SHARD_AXIS[n] - 1) for n, t in zip(OTHER_BIG, gathered)}


def _layer_grad_blocks(g, i):
    blocks = [_relayout_g_in(g[n], f"relayout_g_in{i}") if n == 'w_in' else _split8(g[n], SHARD_AXIS[n] - 1) for n in BIG_SHARDED]
    return [b.reshape(N_DEV, -1, b.shape[-1]) for b in blocks]


def _pair_start(blocks, i):
    shapes = [(N_CHIP,) + b.shape[1:] for b in blocks]
    return _split_start(_plan_pair, blocks, shapes, N_CHIP * len(blocks), f"pair{i}_start")


def _pair_sums(state, after, i):
    mine, theirs = _split_wait(_plan_pair, state, after, f"pair{i}_wait", with_sources=True)
    return [_pair_sum(b, t, f"pair_sum{i}_{k}") for k, (b, t) in enumerate(zip(mine, theirs))]


def _chips_start(sums, i, after=None):
    return _split_start(_plan_chips, sums, [s.shape for s in sums], 3 * len(sums), f"chips{i}_start", after)


def kernel(x, norm_w, w_in, s5_lambda_re, s5_lambda_im, s5_b_re, s5_b_im, s5_c_re, s5_c_im, s5_d, s5_log_step, s5_w_glu, sgu_ln_w, sgu_ln_b, sgu_w, sgu_b, m2_conv_w, m2_conv_b, m2_dt_bias, m2_a_log, m2_d, m2_norm_w, sc_conv_w, merge_b, w_branch, w_out, final_norm_w, loss_target, m_norm_w, m_w_in, m_s5_lambda_re, m_s5_lambda_im, m_s5_b_re, m_s5_b_im, m_s5_c_re, m_s5_c_im, m_s5_d, m_s5_log_step, m_s5_w_glu, m_sgu_ln_w, m_sgu_ln_b, m_sgu_w, m_sgu_b, m_m2_conv_w, m_m2_conv_b, m_m2_dt_bias, m_m2_a_log, m_m2_d, m_m2_norm_w, m_sc_conv_w, m_merge_b, m_w_branch, m_w_out, m_final_norm_w, v_norm_w, v_w_in, v_s5_lambda_re, v_s5_lambda_im, v_s5_b_re, v_s5_b_im, v_s5_c_re, v_s5_c_im, v_s5_d, v_s5_log_step, v_s5_w_glu, v_sgu_ln_w, v_sgu_ln_b, v_sgu_w, v_sgu_b, v_m2_conv_w, v_m2_conv_b, v_m2_dt_bias, v_m2_a_log, v_m2_d, v_m2_norm_w, v_sc_conv_w, v_merge_b, v_w_branch, v_w_out, v_final_norm_w):
    loc = locals()
    p = {n: loc[n] for n in WEIGHTS}
    mom = {n: loc['m_' + n] for n in WEIGHTS}
    vel = {n: loc['v_' + n] for n in WEIGHTS}

    small_sizes = [p[n].size for n in SMALL_SHARDED]
    small_pack = _rows128(jnp.concatenate([p[n].reshape(-1) for n in SMALL_SHARDED]))
    shards = ([p['w_in'][0].astype(BF16)] + [p[n][0].astype(BF16) for n in OTHER_BIG] + [small_pack]
              + [p[n][1].astype(BF16) for n in BIG_SHARDED])
    gath, tok = _split_start(_plan_gather, shards, [(N_DEV,) + t.shape for t in shards], 7 * len(shards), "gather_start")
    sems, srcs, lands = gath[:3], gath[3:3 + len(shards)], gath[3 + len(shards):]

    def relayed(lo, hi, after, name):
        plan = functools.partial(_plan_gather, first=lo)
        state, tok = _split_relay(plan, (*sems, *srcs[lo:hi], *lands[lo:hi]), after, name + "_relay")
        return (plan, state, name), tok

    def arrived(relay, after):
        plan, state, name = relay
        return _split_wait(plan, state, after, name + "_wait")

    def gathered(lo, hi, after, name):
        relay, tok = relayed(lo, hi, after, name)
        return arrived(relay, tok)

    later = dict(p, **{n: p[n] + tok[0, 0] for n in ('norm_w', 's5_log_step', 'sgu_b', 'm2_d')})
    preps = [_layer_prep(i, later) for i in range(DEPTH)]
    h0 = _rmsnorm_fwd(x[0], preps[0][0]['nw'], "rms_fwd0")
    got = gathered(0, 1, tok + (preps[0][1] + preps[1][1] + h0[0, 0].astype(F32)), "gather_w_in0")
    small_full = {}

    def other_weights0(proj):
        got = gathered(1, 5, proj, "gather_rest0")
        small_all, off = got[-1].reshape(N_DEV, -1), 0
        for n, sz in zip(SMALL_SHARDED, small_sizes):
            small_full[n] = _join8(small_all[:, off:off + sz].reshape((N_DEV,) + p[n].shape), SHARD_AXIS[n])
            off += sz
        return dict(_other_weights(got[:-1]), **{n: small_full[n][0] for n in SMALL_SHARDED})

    saved, layer_g, full = [None] * DEPTH, [None] * DEPTH, [None] * DEPTH
    relay1 = []

    def relay_layer1(ys):
        relay, tok = relayed(5, 9, ys, "gather1")
        relay1.append(relay)
        return tok

    xs, saved[0], full[0] = _layer_fwd(x[0], h0, 0, preps[0][0], _relayout_w_in(got[0], "relayout_w_in0"), other_weights0,
                                       relay_layer1)
    h1 = _rmsnorm_fwd(xs, preps[1][0]['nw'], "rms_fwd1")
    got = arrived(relay1[0], h1)
    xs, saved[1], full[1] = _layer_fwd(
        xs, h1, 1, preps[1][0], _relayout_w_in(got[0], "relayout_w_in1"),
        lambda proj: dict(_other_weights(got[1:]), **{n: small_full[n][1] for n in SMALL_SHARDED}))
    loss_row, dx, dfw = _loss_head(xs, final_norm_w.reshape(1, D_MODEL), loss_target[0])
    loss = lax.psum(loss_row[0, 0], ("x", "y", "c"))
    loss, dx = lax.optimization_barrier((loss, dx))
    pairs, scat = [None] * DEPTH, [None] * DEPTH

    def start_pairs1(g):
        pairs[1], tok = _pair_start(_layer_grad_blocks(g, 1), 1)
        return tok

    def send_chip_sums1(dh):
        scat[1], tok = _chips_start(_pair_sums(pairs[1], dh, 1), 1)
        return tok

    def send_all0(g):
        pairs[0], tok = _pair_start(_layer_grad_blocks(g, 0), 0)
        scat[0], tok = _chips_start(_pair_sums(pairs[0], tok, 0), 0)
        return tok

    dx, layer_g[1] = _layer_bwd(dx, 1, saved[1], full[1], start_pairs1, send_chip_sums1)
    dx, layer_g[0] = _layer_bwd(dx, 0, saved[0], full[0], send_all0)
    for i in range(DEPTH):
        layer_g[i].update(zip(S5_NAMES, saved[i]['disc_vjp'](layer_g[i].pop('s5_dense'))))
    grads = {n: jnp.stack([layer_g[i][n] for i in range(DEPTH)]) for n in SMALL_SHARDED + REPLICATED if n != 'final_norm_w'}
    grads['final_norm_w'] = dfw[0]

    out_g, out_d, out_m, out_v = {}, {}, {}, {}
    repl_rows = _pack_rows([grads[n] for n in REPLICATED], 8 * N_DEV)
    rr = repl_rows.shape[0] // N_DEV
    shard_rows = _pack_rows([_split8(grads[n], SHARD_AXIS[n]) for n in SMALL_SHARDED], 8, batched=True)
    rs = shard_rows.shape[1]
    small_g = jnp.concatenate([shard_rows, repl_rows.reshape(N_DEV, rr, LANES)], axis=1)
    small_sum = _slot_sum(_exchange([small_g], False, "scatter_small")[0], "sum_small")
    repl_all = _exchange([small_sum[rs:]], True, "gather_small")[0].reshape(N_DEV * rr, LANES)
    g_all = jnp.concatenate([small_sum[:rs], repl_all], axis=0)
    names = SMALL_SHARDED + REPLICATED
    pieces = (_unpack_rows(g_all[:rs], [p[n].shape for n in SMALL_SHARDED])
              + _unpack_rows(g_all[rs:], [p[n].shape for n in REPLICATED]))
    out_g.update(zip(names, pieces))
    res = _adamw_many(*[[_memory_view(n, d[n]) for n in names] for d in (out_g, p, mom, vel)], "adamw_small")
    for r, dst in zip(res, (out_d, out_m, out_v)):
        dst.update({n: _memory_view(n, t) for n, t in zip(names, r)})

    landed1 = _split_wait(_plan_chips, scat[1], res[0][0], "chips1_wait")
    landed0 = _split_wait(_plan_chips, scat[0], landed1[0], "chips0_wait")
    for k, n in enumerate(BIG_SHARDED):
        shp = p[n].shape
        c = shp[-1]
        r = p[n].size // (DEPTH * c)
        if n == 'w_in':
            big = _adamw_w_in([landed0[k], landed1[k]], p[n], mom[n], vel[n], "adamw_w_in")
        else:
            big = _adamw([landed0[k], landed1[k]], *[d[n].reshape(DEPTH, r, c) for d in (p, mom, vel)],
                         {'w_branch': 512, 'w_out': 128, 's5_w_glu': 64}[n], "adamw_" + n)
        out_g[n], out_d[n], out_m[n], out_v[n] = [o.reshape(shp) for o in big]
    return (loss, dx[None], *[out_g[n] for n in WEIGHTS], *[out_d[n] for n in WEIGHTS],
            *[out_m[n] for n in WEIGHTS], *[out_v[n] for n in WEIGHTS])
```

```python
import functools

import jax
import jax.numpy as jnp
import numpy as np
from jax import lax
from jax.experimental import pallas as pl
from jax.experimental.pallas import tpu as pltpu

F32 = jnp.float32
BF16 = jnp.bfloat16

N_DEV = 8
SEQ = 2048
D_MODEL = 1024
DEPTH = 2
BW = 512
N_BRANCH = 4
EPS = 1e-6
S5_GROUPS, S5_STATE, S5_P = 32, 64, 16
S5_CH = S5_GROUPS * S5_STATE
SGU_CHUNK, SGU_HEADS = 128, 8
M2_HEADS, M2_HEAD_DIM, M2_STATE, M2_CHUNK, M2_CONV = 8, 64, 128, 128, 4
M2_CONV_CH = 1024
SC_CONV = 3
IN_DIM = 10248
IN_PAD = 11264
C_MERGE = 0
C_S5U, C_S5G = 4096, 4608
C_M2X = 5120
C_SGU_U, C_SGU_V, C_SGU_G = 6144, 6656, 7168
C_M2Z, C_DT = 8192, 8704
C_SC = 9216
SHARD_IN = IN_DIM // N_DEV

ADAM_LR, ADAM_B1, ADAM_B2, ADAM_EPS, ADAM_WD, ADAM_STEP = 0.001, 0.9, 0.999, 1e-08, 0.01, 10

VMEM_LIMIT = 56 * 1024 * 1024
LANES = 128

MESH = pl.DeviceIdType.MESH


def _cparams(sem=None, **kw):
    return pltpu.CompilerParams(dimension_semantics=sem, vmem_limit_bytes=VMEM_LIMIT, **kw)


def _dg(a, b, ca, cb, precision=None):
    return lax.dot_general(a, b, (((ca,), (cb,)), ((), ())), precision=precision,
                           preferred_element_type=F32)


@functools.partial(jax.custom_vjp, nondiff_argnums=(2, 3))
def _bdot(a, b, ca, cb):
    return _dg(a.astype(BF16), b.astype(BF16), ca, cb)


def _bdot_fwd(a, b, ca, cb):
    return _bdot(a, b, ca, cb), (a, b)


def _bdot_bwd(ca, cb, res, g):
    a, b = res
    gb, ab, bb = g.astype(BF16), a.astype(BF16), b.astype(BF16)
    da = _dg(gb, bb, 1, 1 - cb) if ca == 1 else _dg(bb, gb, 1 - cb, 1)
    db = _dg(ab, gb, 1 - ca, 0) if cb == 0 else _dg(gb, ab, 0, 1 - ca)
    return da.astype(a.dtype), db.astype(b.dtype)


_bdot.defvjp(_bdot_fwd, _bdot_bwd)


def _rms(x, w):
    return x * lax.rsqrt(jnp.mean(x * x, axis=-1, keepdims=True) + EPS) * w


def _silu(x):
    return x * jax.nn.sigmoid(x)


def _gelu(x):
    return 0.5 * x * (1.0 + jnp.tanh(0.7978845608028654 * (x + 0.044715 * (x * x * x))))


def _softplus(x):
    return jnp.maximum(x, 0.0) + jnp.log1p(jnp.exp(-jnp.abs(x)))


def _shift_down(x, s):
    if s == 0:
        return x
    row = lax.broadcasted_iota(jnp.int32, x.shape, 0)
    return jnp.where(row >= s, pltpu.roll(x, s, 0), 0.0)


def _shift_up(x, s):
    if s == 0:
        return x
    n = x.shape[0]
    row = lax.broadcasted_iota(jnp.int32, x.shape, 0)
    return jnp.where(row < n - s, pltpu.roll(x, n - s, 0), 0.0)


def _matmul(a, b, ca, cb, out_dtype, tm, tn, tk, name, residual=None, after=None):
    m = a.shape[1 - ca]
    k = a.shape[ca]
    n = b.shape[1 - cb]
    assert b.shape[cb] == k and m % tm == 0 and n % tn == 0 and k % tk == 0
    nk = k // tk
    a_spec = pl.BlockSpec((tm, tk), lambda i, j, kk: (i, kk)) if ca == 1 else pl.BlockSpec((tk, tm), lambda i, j, kk: (kk, i))
    b_spec = pl.BlockSpec((tk, tn), lambda i, j, kk: (kk, j)) if cb == 0 else pl.BlockSpec((tn, tk), lambda i, j, kk: (j, kk))
    o_spec = pl.BlockSpec((tm, tn), lambda i, j, kk: (i, j))
    has_res = residual is not None

    def body(*refs):
        refs = refs[:2 + has_res] + refs[2 + has_res + (after is not None):]
        if has_res:
            a_ref, b_ref, r_ref, o_ref, acc = refs
        else:
            a_ref, b_ref, o_ref, acc = refs
        kk = pl.program_id(2)
        part = _dg(a_ref[...].astype(BF16), b_ref[...].astype(BF16), ca, cb)

        @pl.when(kk == 0)
        def _():
            acc[...] = part

        @pl.when(kk > 0)
        def _():
            acc[...] += part

        @pl.when(kk == nk - 1)
        def _():
            r = acc[...]
            if has_res:
                r = r + r_ref[...]
            o_ref[...] = r.astype(out_dtype)

    ins = [a, b] + ([residual] if has_res else []) + ([after] if after is not None else [])
    specs = [a_spec, b_spec] + ([o_spec] if has_res else []) + ([pl.BlockSpec(memory_space=pl.ANY)] if after is not None else [])
    return pl.pallas_call(
        body, name=name, grid=(m // tm, n // tn, nk), in_specs=specs, out_specs=o_spec,
        out_shape=jax.ShapeDtypeStruct((m, n), out_dtype),
        scratch_shapes=[pltpu.VMEM((tm, tn), F32)],
        compiler_params=_cparams(("parallel", "parallel", "arbitrary")),
    )(*ins)


ROW_TILE = 512


def _rmsnorm_fwd(x, w, name):
    def body(x_ref, w_ref, o_ref):
        o_ref[...] = _rms(x_ref[...], w_ref[...]).astype(BF16)

    return pl.pallas_call(
        body, name=name, grid=(SEQ // ROW_TILE,),
        in_specs=[pl.BlockSpec((ROW_TILE, D_MODEL), lambda i: (i, 0)), pl.BlockSpec((1, D_MODEL), lambda i: (0, 0))],
        out_specs=pl.BlockSpec((ROW_TILE, D_MODEL), lambda i: (i, 0)),
        out_shape=jax.ShapeDtypeStruct((SEQ, D_MODEL), BF16),
        compiler_params=_cparams(("parallel",)),
    )(x, w)


def _rmsnorm_bwd(x, w, dh, dres, name):
    def body(x_ref, w_ref, dh_ref, dres_ref, dx_ref, dw_ref):
        _, vjp = jax.vjp(_rms, x_ref[...], w_ref[...])
        dx, dw = vjp(dh_ref[...])
        dx_ref[...] = dx + dres_ref[...]

        @pl.when(pl.program_id(0) == 0)
        def _():
            dw_ref[...] = dw

        @pl.when(pl.program_id(0) > 0)
        def _():
            dw_ref[...] += dw

    tile = pl.BlockSpec((ROW_TILE, D_MODEL), lambda i: (i, 0))
    vec = pl.BlockSpec((1, D_MODEL), lambda i: (0, 0))
    return pl.pallas_call(
        body, name=name, grid=(SEQ // ROW_TILE,),
        in_specs=[tile, vec, tile, tile], out_specs=[tile, vec],
        out_shape=[jax.ShapeDtypeStruct((SEQ, D_MODEL), F32), jax.ShapeDtypeStruct((1, D_MODEL), F32)],
        compiler_params=_cparams(("arbitrary",)),
    )(x, w, dh, dres)


def _loss_head(x, w, target):
    def body(x_ref, w_ref, t_ref, loss_ref, dx_ref, dw_ref):
        tgt = t_ref[...]

        def f(xv, wv):
            err = _rms(xv, wv) - tgt
            return 0.5 * jnp.sum(jnp.mean(err * err, axis=-1))

        loss, vjp = jax.vjp(f, x_ref[...], w_ref[...])
        dx, dw = vjp(jnp.ones((), F32))
        dx_ref[...] = dx
        lrow = jnp.full((1, LANES), loss, F32)

        @pl.when(pl.program_id(0) == 0)
        def _():
            dw_ref[...] = dw
            loss_ref[...] = lrow

        @pl.when(pl.program_id(0) > 0)
        def _():
            dw_ref[...] += dw
            loss_ref[...] += lrow

    tile = pl.BlockSpec((ROW_TILE, D_MODEL), lambda i: (i, 0))
    vec = pl.BlockSpec((1, D_MODEL), lambda i: (0, 0))
    return pl.pallas_call(
        body, name="loss_head", grid=(SEQ // ROW_TILE,),
        in_specs=[tile, vec, tile], out_specs=[pl.BlockSpec((1, LANES), lambda i: (0, 0)), tile, vec],
        out_shape=[jax.ShapeDtypeStruct((1, LANES), F32), jax.ShapeDtypeStruct((SEQ, D_MODEL), F32),
                   jax.ShapeDtypeStruct((1, D_MODEL), F32)],
        compiler_params=_cparams(("arbitrary",)),
    )(x, w, target)


S5_T = 256
S5_BLOCKS = [(slice(j * 256, (j + 1) * 256), slice(j * 1024, (j + 1) * 1024)) for j in range(2)]


def _s5_post(ypre, gate, wglu):
    y = _gelu(ypre)
    y = y * jax.nn.sigmoid(_bdot(y, wglu, 1, 0))
    return y * _silu(gate)


def _s5_fwd(proj, bbre, bbim, cre, cim, a2, dvec, wglu, name):
    def body(u_ref, g_ref, bbre_ref, bbim_ref, cre_ref, cim_ref, a_ref, d_ref, wg_ref, o_ref, sre_ref, sim_ref, st):
        @pl.when(pl.program_id(0) == 0)
        def _():
            st[...] = jnp.zeros_like(st)

        u = u_ref[...]
        ub = u.astype(BF16)
        for us, ss in S5_BLOCKS:
            sre_ref[:, ss] = _dg(ub[:, us], bbre_ref[us, ss], 1, 0)
            sim_ref[:, ss] = _dg(ub[:, us], bbim_ref[us, ss], 1, 0)
        ar, ai = a_ref[0:1, :], a_ref[1:2, :]

        def step(t, carry):
            sr, si = carry
            nr = ar * sr - ai * si + sre_ref[pl.ds(t, 1), :]
            ni = ar * si + ai * sr + sim_ref[pl.ds(t, 1), :]
            sre_ref[pl.ds(t, 1), :] = nr
            sim_ref[pl.ds(t, 1), :] = ni
            return nr, ni

        sr, si = lax.fori_loop(0, S5_T, step, (st[0:1, :], st[1:2, :]), unroll=8)
        st[0:1, :] = sr
        st[1:2, :] = si
        ypre = jnp.concatenate(
            [_dg(sre_ref[:, ss].astype(BF16), cre_ref[ss, us], 1, 0) - _dg(sim_ref[:, ss].astype(BF16), cim_ref[ss, us], 1, 0)
             for us, ss in S5_BLOCKS], axis=1) + d_ref[...] * u
        o_ref[...] = _s5_post(ypre, g_ref[...], wg_ref[...]).astype(BF16)

    full = lambda shape: pl.BlockSpec(shape, lambda c: (0, 0))
    return pl.pallas_call(
        body, name=name, grid=(SEQ // S5_T,),
        in_specs=[pl.BlockSpec((S5_T, BW), lambda c: (c, C_S5U // BW)), pl.BlockSpec((S5_T, BW), lambda c: (c, C_S5G // BW)),
                  full((BW, S5_CH)), full((BW, S5_CH)), full((S5_CH, BW)), full((S5_CH, BW)),
                  full((2, S5_CH)), full((1, BW)), full((BW, BW))],
        out_specs=[pl.BlockSpec((S5_T, BW), lambda c: (c, 0)), pl.BlockSpec((S5_T, S5_CH), lambda c: (c, 0)),
                   pl.BlockSpec((S5_T, S5_CH), lambda c: (c, 0))],
        out_shape=[jax.ShapeDtypeStruct((SEQ, BW), BF16), jax.ShapeDtypeStruct((SEQ, S5_CH), F32),
                   jax.ShapeDtypeStruct((SEQ, S5_CH), F32)],
        scratch_shapes=[pltpu.VMEM((2, S5_CH), F32)],
        compiler_params=_cparams(("arbitrary",)),
    )(proj, proj, bbre, bbim, cre, cim, a2, dvec, wglu)


def _s5_bwd(proj, dproj, dout, sre, sim, bbre, bbim, cre, cim, a2, dvec, wglu, name):
    nc = SEQ // S5_T

    def body(u_ref, g_ref, do_ref, sre_ref, sim_ref, pre_ref, pim_ref, bbre_ref, bbim_ref, cre_ref, cim_ref, a_ref,
             d_ref, wg_ref, dproj_in, dp_ref, dbbre_ref, dbbim_ref, dcre_ref, dcim_ref, da_ref, dd_ref, dwg_ref,
             gre, gim, st):
        c = nc - 1 - pl.program_id(0)

        @pl.when(pl.program_id(0) == 0)
        def _():
            st[...] = jnp.zeros_like(st)
            for r in (dbbre_ref, dbbim_ref, dcre_ref, dcim_ref, da_ref, dd_ref, dwg_ref):
                r[...] = jnp.zeros_like(r)

        u = u_ref[...]
        s_re, s_im = sre_ref[...], sim_ref[...]

        def head(s_res, s_ims, cres, cims, dv, uv, gv, wg):
            ypre = jnp.concatenate([_bdot(sr, cr, 1, 0) - _bdot(si, ci, 1, 0)
                                    for sr, si, cr, ci in zip(s_res, s_ims, cres, cims)], axis=1) + dv * uv
            return _s5_post(ypre, gv, wg)

        _, vjp = jax.vjp(head, [sre_ref[:, ss] for _, ss in S5_BLOCKS], [sim_ref[:, ss] for _, ss in S5_BLOCKS],
                         [cre_ref[ss, us].astype(F32) for us, ss in S5_BLOCKS],
                         [cim_ref[ss, us].astype(F32) for us, ss in S5_BLOCKS],
                         d_ref[...], u, g_ref[...], wg_ref[...].astype(F32))
        ds_res, ds_ims, dcres, dcims, dd, du_d, dgate, dwg = vjp(do_ref[0])
        for k, (us, ss) in enumerate(S5_BLOCKS):
            dcre_ref[ss, us] += dcres[k]
            dcim_ref[ss, us] += dcims[k]
            gre[:, ss] = ds_res[k]
            gim[:, ss] = ds_ims[k]
        dd_ref[...] += dd
        dwg_ref[...] += dwg
        dp_ref[:, BW:] = dgate.astype(BF16)
        ar, ai = a_ref[0:1, :], a_ref[1:2, :]

        def step(i, carry):
            t = S5_T - 1 - i
            gr, gi = carry
            nr = gre[pl.ds(t, 1), :] + gr
            ni = gim[pl.ds(t, 1), :] + gi
            gre[pl.ds(t, 1), :] = nr
            gim[pl.ds(t, 1), :] = ni
            return ar * nr + ai * ni, ar * ni - ai * nr

        gr, gi = lax.fori_loop(0, S5_T, step, (st[0:1, :], st[1:2, :]), unroll=8)
        st[0:1, :] = gr
        st[1:2, :] = gi
        g_re, g_im = gre[...], gim[...]
        first = jnp.where(c > 0, 1.0, 0.0)
        row = lax.broadcasted_iota(jnp.int32, (S5_T, S5_CH), 0)
        p_re = jnp.where(row == 0, pre_ref[7:8, :] * first, pltpu.roll(s_re, 1, 0))
        p_im = jnp.where(row == 0, pim_ref[7:8, :] * first, pltpu.roll(s_im, 1, 0))
        da_ref[0:1, :] += jnp.sum(g_re * p_re + g_im * p_im, axis=0, keepdims=True)
        da_ref[1:2, :] += jnp.sum(g_im * p_re - g_re * p_im, axis=0, keepdims=True)
        ub, grb, gib = u.astype(BF16), g_re.astype(BF16), g_im.astype(BF16)
        du_s = []
        for us, ss in S5_BLOCKS:
            dbbre_ref[us, ss] += _dg(ub[:, us], grb[:, ss], 0, 0)
            dbbim_ref[us, ss] += _dg(ub[:, us], gib[:, ss], 0, 0)
            du_s.append(_dg(grb[:, ss], bbre_ref[us, ss], 1, 1) + _dg(gib[:, ss], bbim_ref[us, ss], 1, 1))
        dp_ref[:, :BW] = (du_d + jnp.concatenate(du_s, axis=1)).astype(BF16)

    full = lambda shape: pl.BlockSpec(shape, lambda i: (0, 0))
    rev = lambda w, col=0: pl.BlockSpec((S5_T, w), lambda i: (nc - 1 - i, col))
    prev = pl.BlockSpec((8, S5_CH), lambda i: (jnp.maximum((nc - 1 - i) * (S5_T // 8) - 1, 0), 0))
    return pl.pallas_call(
        body, name=name, grid=(nc,),
        in_specs=[rev(BW, C_S5U // BW), rev(BW, C_S5G // BW), pl.BlockSpec((1, S5_T, BW), lambda i: (0, nc - 1 - i, 0)),
                  rev(S5_CH), rev(S5_CH), prev, prev,
                  full((BW, S5_CH)), full((BW, S5_CH)), full((S5_CH, BW)), full((S5_CH, BW)),
                  full((2, S5_CH)), full((1, BW)), full((BW, BW)), pl.BlockSpec(memory_space=pl.ANY)],
        out_specs=[rev(2 * BW, C_S5U // (2 * BW)), full((BW, S5_CH)), full((BW, S5_CH)), full((S5_CH, BW)), full((S5_CH, BW)),
                   full((2, S5_CH)), full((1, BW)), full((BW, BW))],
        input_output_aliases={14: 0},
        out_shape=[jax.ShapeDtypeStruct((SEQ, IN_PAD), BF16),
                   jax.ShapeDtypeStruct((BW, S5_CH), F32), jax.ShapeDtypeStruct((BW, S5_CH), F32),
                   jax.ShapeDtypeStruct((S5_CH, BW), F32), jax.ShapeDtypeStruct((S5_CH, BW), F32),
                   jax.ShapeDtypeStruct((2, S5_CH), F32), jax.ShapeDtypeStruct((1, BW), F32),
                   jax.ShapeDtypeStruct((BW, BW), F32)],
        scratch_shapes=[pltpu.VMEM((S5_T, S5_CH), F32), pltpu.VMEM((S5_T, S5_CH), F32), pltpu.VMEM((2, S5_CH), F32)],
        compiler_params=_cparams(("arbitrary",)),
    )(proj, proj, dout, sre, sim, sre, sim, bbre, bbim, cre, cim, a2, dvec, wglu, dproj)


def _diag_blocks(dense, after=None):
    rows, cols = dense.shape
    rows_per, cols_per = rows // S5_GROUPS, cols // S5_GROUPS
    per_lane_block = LANES // cols_per
    tile = 512

    def body(d_ref, *rest):
        o_ref = rest[-1]
        r0 = pl.program_id(0) * tile
        grp = (r0 + lax.broadcasted_iota(jnp.int32, (tile, LANES), 0)) // rows_per
        lane = lax.broadcasted_iota(jnp.int32, (tile, LANES), 1)
        acc = jnp.zeros((tile, LANES), F32)
        for hb in range(cols // LANES):
            acc = acc + jnp.where(grp == per_lane_block * hb + lane // cols_per, d_ref[:, hb * LANES:(hb + 1) * LANES], 0.0)
        shift = LANES // 2
        while shift >= cols_per:
            acc = acc + pltpu.roll(acc, LANES - shift, 1)
            shift //= 2
        o_ref[...] = acc

    folded = pl.pallas_call(
        body, name=f"diag_blocks_{rows_per}x{cols_per}", grid=(rows // tile,),
        in_specs=[pl.BlockSpec((tile, cols), lambda i: (i, 0))] + ([] if after is None else [pl.BlockSpec(memory_space=pl.ANY)]),
        out_specs=pl.BlockSpec((tile, LANES), lambda i: (i, 0)),
        out_shape=jax.ShapeDtypeStruct((rows, LANES), F32), compiler_params=_cparams(("parallel",)),
    )(dense, *([] if after is None else [after]))
    return folded[:, :cols_per].reshape(S5_GROUPS, rows_per, cols_per)


def _block_diag(t):
    g, rows_per, cols_per = t.shape
    wide = jnp.tile(t.reshape(g * rows_per, cols_per), (1, g))
    r = lax.broadcasted_iota(jnp.int32, wide.shape, 0) // rows_per
    c = lax.broadcasted_iota(jnp.int32, wide.shape, 1) // cols_per
    return jnp.where(r == c, wide, 0.0)


def _s5_disc(lam_re, lam_im, b_re, b_im, c_re, c_im, d, log_step):
    step = jnp.exp(log_step)[:, None]
    mag = jnp.exp(lam_re * step)
    ab_re, ab_im = mag * jnp.cos(lam_im * step), mag * jnp.sin(lam_im * step)
    den = lam_re * lam_re + lam_im * lam_im
    nr = ab_re - 1.0
    coef_re = (nr * lam_re + ab_im * lam_im) / den
    coef_im = (ab_im * lam_re - nr * lam_im) / den
    bb_re = coef_re[..., None] * b_re - coef_im[..., None] * b_im
    bb_im = coef_re[..., None] * b_im + coef_im[..., None] * b_re
    a2 = jnp.stack([ab_re.reshape(-1), ab_im.reshape(-1)])
    return (jnp.swapaxes(bb_re, 1, 2), jnp.swapaxes(bb_im, 1, 2),
            jnp.swapaxes(c_re, 1, 2), jnp.swapaxes(c_im, 1, 2),
            a2, d.reshape(1, BW))


def _left_lanes(shape):
    return lax.broadcasted_iota(jnp.int32, shape, 1) < 64


def _sgu_chunk(u, v, gate, ln_w, ln_b, w, bias):
    u32, v32 = _gelu(u), _gelu(v)
    mu = jnp.mean(v32, axis=-1, keepdims=True)
    var = jnp.mean(jnp.square(v32 - mu), axis=-1, keepdims=True)
    vn = (v32 - mu) * lax.rsqrt(var + EPS) * ln_w + ln_b
    t_i = lax.broadcasted_iota(jnp.int32, (SGU_CHUNK, SGU_CHUNK), 0)
    s_i = lax.broadcasted_iota(jnp.int32, (SGU_CHUNK, SGU_CHUNK), 1)
    causal = t_i >= s_i
    left = _left_lanes((SGU_CHUNK, LANES))
    sgate = _silu(gate)
    outs = []
    for j in range(BW // LANES):
        vb = vn[:, j * LANES:(j + 1) * LANES]
        s_blk = (_bdot(jnp.where(causal, w[2 * j], 0.0), jnp.where(left, vb, 0.0), 1, 0)
                 + _bdot(jnp.where(causal, w[2 * j + 1], 0.0), jnp.where(left, 0.0, vb), 1, 0))
        sl = slice(j * LANES, (j + 1) * LANES)
        outs.append(u32[:, sl] * (s_blk + bias[:, sl]) * sgate[:, sl])
    return outs


def _sgu_fwd(proj, ln_w, ln_b, w, bias, name):
    def body(u_ref, v_ref, g_ref, lw_ref, lb_ref, w_ref, b_ref, o_ref):
        outs = _sgu_chunk(u_ref[...], v_ref[...], g_ref[...], lw_ref[...], lb_ref[...], w_ref[...], b_ref[...])
        for j, o in enumerate(outs):
            o_ref[:, j * LANES:(j + 1) * LANES] = o.astype(BF16)

    blk = lambda col: pl.BlockSpec((SGU_CHUNK, BW), lambda c: (c, col // BW))
    vec = pl.BlockSpec((1, BW), lambda c: (0, 0))
    return pl.pallas_call(
        body, name=name, grid=(SEQ // SGU_CHUNK,),
        in_specs=[blk(C_SGU_U), blk(C_SGU_V), blk(C_SGU_G), vec, vec,
                  pl.BlockSpec((SGU_HEADS, SGU_CHUNK, SGU_CHUNK), lambda c: (0, 0, 0)),
                  pl.BlockSpec((SGU_CHUNK, BW), lambda c: (0, 0))],
        out_specs=pl.BlockSpec((SGU_CHUNK, BW), lambda c: (c, 0)),
        out_shape=jax.ShapeDtypeStruct((SEQ, BW), BF16),
        compiler_params=_cparams(("parallel",)),
    )(proj, proj, proj, ln_w, ln_b, w, bias)


def _sgu_bwd(proj, dproj, dout, ln_w, ln_b, w, bias, name):
    def body(u_ref, v_ref, g_ref, do_ref, lw_ref, lb_ref, w_ref, b_ref, dproj_in, dp_ref, dlw_ref, dlb_ref, dw_ref, db_ref):
        _, vjp = jax.vjp(_sgu_chunk, u_ref[...], v_ref[...], g_ref[...], lw_ref[...], lb_ref[...], w_ref[...], b_ref[...])
        do = do_ref[0]
        du, dv, dgate, dlw, dlb, dw, db = vjp([do[:, j * LANES:(j + 1) * LANES] for j in range(BW // LANES)])
        dp_ref[:, 0:BW] = du.astype(BF16)
        dp_ref[:, BW:2 * BW] = dv.astype(BF16)
        dp_ref[:, 2 * BW:3 * BW] = dgate.astype(BF16)
        dp_ref[:, 3 * BW:] = jnp.zeros((SGU_CHUNK, BW), BF16)

        @pl.when(pl.program_id(0) == 0)
        def _():
            dlw_ref[...] = dlw
            dlb_ref[...] = dlb
            dw_ref[...] = dw
            db_ref[...] = db

        @pl.when(pl.program_id(0) > 0)
        def _():
            dlw_ref[...] += dlw
            dlb_ref[...] += dlb
            dw_ref[...] += dw
            db_ref[...] += db

    blk = lambda col: pl.BlockSpec((SGU_CHUNK, BW), lambda c: (c, col // BW))
    vec = pl.BlockSpec((1, BW), lambda c: (0, 0))
    wsp = pl.BlockSpec((SGU_HEADS, SGU_CHUNK, SGU_CHUNK), lambda c: (0, 0, 0))
    bsp = pl.BlockSpec((SGU_CHUNK, BW), lambda c: (0, 0))
    return pl.pallas_call(
        body, name=name, grid=(SEQ // SGU_CHUNK,),
        in_specs=[blk(C_SGU_U), blk(C_SGU_V), blk(C_SGU_G), pl.BlockSpec((1, SGU_CHUNK, BW), lambda c: (1, c, 0)),
                  vec, vec, wsp, bsp, pl.BlockSpec(memory_space=pl.ANY)],
        out_specs=[pl.BlockSpec((SGU_CHUNK, 4 * BW), lambda c: (c, C_SGU_U // (4 * BW))), vec, vec, wsp, bsp],
        input_output_aliases={8: 0},
        out_shape=[jax.ShapeDtypeStruct((SEQ, IN_PAD), BF16), jax.ShapeDtypeStruct((1, BW), F32),
                   jax.ShapeDtypeStruct((1, BW), F32), jax.ShapeDtypeStruct((SGU_HEADS, SGU_CHUNK, SGU_CHUNK), F32),
                   jax.ShapeDtypeStruct((SGU_CHUNK, BW), F32)],
        compiler_params=_cparams(("arbitrary",)),
    )(proj, proj, proj, dout, ln_w, ln_b, w, bias, dproj)


CONV_BLK = 256


def _m2_conv_fwd(proj, w, b, name):
    def body(x_ref, w_ref, b_ref, o_ref):
        x = x_ref[...]
        acc = jnp.zeros_like(x) + b_ref[...]
        for k in range(M2_CONV):
            acc = acc + w_ref[k:k + 1, :] * _shift_down(x, M2_CONV - 1 - k)
        o_ref[...] = _silu(acc)

    return pl.pallas_call(
        body, name=name, grid=(M2_CONV_CH // CONV_BLK,),
        in_specs=[pl.BlockSpec((SEQ, CONV_BLK), lambda j: (0, C_M2X // CONV_BLK + j)),
                  pl.BlockSpec((M2_CONV, CONV_BLK), lambda j: (0, j)), pl.BlockSpec((1, CONV_BLK), lambda j: (0, j))],
        out_specs=pl.BlockSpec((SEQ, CONV_BLK), lambda j: (0, j)),
        out_shape=jax.ShapeDtypeStruct((SEQ, M2_CONV_CH), F32),
        compiler_params=_cparams(("parallel",)),
    )(proj, w, b)


def _m2_conv_bwd(proj, dproj, dxa, w, b, name):
    def body(x_ref, d_ref, w_ref, b_ref, dproj_in, dx_ref, dw_ref, db_ref):
        x = x_ref[...]
        xs = [_shift_down(x, M2_CONV - 1 - k) for k in range(M2_CONV)]
        acc = jnp.zeros_like(x) + b_ref[...]
        for k in range(M2_CONV):
            acc = acc + w_ref[k:k + 1, :] * xs[k]
        sg = jax.nn.sigmoid(acc)
        dacc = d_ref[...] * (sg * (1.0 + acc * (1.0 - sg)))
        dx = jnp.zeros_like(x)
        for k in range(M2_CONV):
            dx = dx + w_ref[k:k + 1, :] * _shift_up(dacc, M2_CONV - 1 - k)
            dw_ref[k:k + 1, :] = jnp.sum(dacc * xs[k], axis=0, keepdims=True)
        dx_ref[...] = dx.astype(BF16)
        db_ref[...] = jnp.sum(dacc, axis=0, keepdims=True)

    return pl.pallas_call(
        body, name=name, grid=(M2_CONV_CH // CONV_BLK,),
        in_specs=[pl.BlockSpec((SEQ, CONV_BLK), lambda j: (0, C_M2X // CONV_BLK + j)),
                  pl.BlockSpec((SEQ, CONV_BLK), lambda j: (0, j)),
                  pl.BlockSpec((M2_CONV, CONV_BLK), lambda j: (0, j)), pl.BlockSpec((1, CONV_BLK), lambda j: (0, j)),
                  pl.BlockSpec(memory_space=pl.ANY)],
        out_specs=[pl.BlockSpec((SEQ, CONV_BLK), lambda j: (0, C_M2X // CONV_BLK + j)),
                   pl.BlockSpec((M2_CONV, CONV_BLK), lambda j: (0, j)), pl.BlockSpec((1, CONV_BLK), lambda j: (0, j))],
        input_output_aliases={4: 0},
        out_shape=[jax.ShapeDtypeStruct((SEQ, IN_PAD), BF16), jax.ShapeDtypeStruct((M2_CONV, M2_CONV_CH), F32),
                   jax.ShapeDtypeStruct((1, M2_CONV_CH), F32)],
        compiler_params=_cparams(("parallel",)),
    )(proj, dxa, w, b, dproj)


N_PAIR = M2_HEADS // 2
HI = lax.Precision.HIGHEST


def _col(a, h):
    lane = lax.broadcasted_iota(jnp.int32, a.shape, 1)
    return jnp.sum(jnp.where(lane == h, a, 0.0), axis=1, keepdims=True)


def _row(a, h):
    sub = lax.broadcasted_iota(jnp.int32, a.shape, 0)
    return jnp.sum(jnp.where(sub == h, a, 0.0), axis=0, keepdims=True)


def _ssd_chunk(xs, bms, cms, dtr, zs, states, dt_bias, a_log, dfs, nws):
    q = M2_CHUNK
    dt = _softplus(dtr + dt_bias)
    da = dt * (-jnp.exp(a_log))
    l_i = lax.broadcasted_iota(jnp.int32, (q, q), 0)
    s_i = lax.broadcasted_iota(jnp.int32, (q, q), 1)
    causal = l_i >= s_i
    tril = jnp.where(causal, 1.0, 0.0)
    a_cs = _dg(tril, da, 1, 0, HI)
    a_cs_t = _dg(da, tril, 0, 1, HI)
    a_end = _row(a_cs, q - 1)
    left = _left_lanes((q, LANES))
    left1 = _left_lanes((1, LANES))
    ys, nexts = [], []
    for j in range(N_PAIR):
        grp = j // 2
        bm, cm = bms[grp], cms[grp]
        h0, h1 = 2 * j, 2 * j + 1
        cb = _bdot(cm, bm, 1, 1)
        xdt = xs[j] * jnp.where(left, _col(dt, h0), _col(dt, h1))
        acs0, acs1 = _col(a_cs, h0), _col(a_cs, h1)
        y = _bdot(cm, states[j], 1, 0) * jnp.where(left, jnp.exp(acs0), jnp.exp(acs1))
        s_new = states[j] * jnp.where(left1, jnp.exp(_col(a_end, h0)), jnp.exp(_col(a_end, h1)))
        for h, acs, xh in ((h0, acs0, jnp.where(left, xdt, 0.0)), (h1, acs1, jnp.where(left, 0.0, xdt))):
            decay = jnp.exp(jnp.where(causal, acs - _row(a_cs_t, h), -jnp.inf))
            y = y + _bdot(cb * decay, xh, 1, 0)
            s_new = s_new + _bdot(bm * jnp.exp(_col(a_end, h) - acs), xh, 0, 0)
        ys.append((y + dfs[j] * xs[j]) * _silu(zs[j]))
        nexts.append(s_new)
    ssq = sum(jnp.sum(y * y, axis=-1, keepdims=True) for y in ys)
    scale = lax.rsqrt(ssq / BW + EPS)
    return [y * scale * nw for y, nw in zip(ys, nws)], nexts


def _blocks(ref, n, width=LANES):
    return [ref[:, j * width:(j + 1) * width] for j in range(n)]


def _ssd_fwd(proj, xa, dt_bias, a_log, dfull, nw, name):
    nc = SEQ // M2_CHUNK

    def body(x_ref, b_ref, c_ref, dt_ref, z_ref, dtb_ref, al_ref, df_ref, nw_ref, o_ref, sin_ref, st):
        @pl.when(pl.program_id(0) == 0)
        def _():
            st[...] = jnp.zeros_like(st)

        states = [st[j] for j in range(N_PAIR)]
        for j in range(N_PAIR):
            sin_ref[0, j] = states[j]
        ys, nexts = _ssd_chunk(_blocks(x_ref, 4), _blocks(b_ref, 2), _blocks(c_ref, 2), dt_ref[...], _blocks(z_ref, 4),
                               states, dtb_ref[...], al_ref[...], _blocks(df_ref, 4), _blocks(nw_ref, 4))
        for j in range(N_PAIR):
            o_ref[:, j * LANES:(j + 1) * LANES] = ys[j].astype(BF16)
            st[j] = nexts[j]

    vec8 = pl.BlockSpec((1, LANES), lambda c: (0, 0))
    vec = pl.BlockSpec((1, BW), lambda c: (0, 0))
    return pl.pallas_call(
        body, name=name, grid=(nc,),
        in_specs=[pl.BlockSpec((M2_CHUNK, BW), lambda c: (c, 0)), pl.BlockSpec((M2_CHUNK, 256), lambda c: (c, 2)),
                  pl.BlockSpec((M2_CHUNK, 256), lambda c: (c, 3)), pl.BlockSpec((M2_CHUNK, LANES), lambda c: (c, C_DT // LANES)),
                  pl.BlockSpec((M2_CHUNK, BW), lambda c: (c, C_M2Z // BW)), vec8, vec8, vec, vec],
        out_specs=[pl.BlockSpec((M2_CHUNK, BW), lambda c: (c, 0)),
                   pl.BlockSpec((1, N_PAIR, M2_STATE, LANES), lambda c: (c, 0, 0, 0))],
        out_shape=[jax.ShapeDtypeStruct((SEQ, BW), BF16), jax.ShapeDtypeStruct((nc, N_PAIR, M2_STATE, LANES), F32)],
        scratch_shapes=[pltpu.VMEM((N_PAIR, M2_STATE, LANES), F32)],
        compiler_params=_cparams(("arbitrary",)),
    )(xa, xa, xa, proj, proj, dt_bias, a_log, dfull, nw)


def _ssd_bwd(proj, dproj, xa, dout, s_in, dt_bias, a_log, dfull, nw, name):
    nc = SEQ // M2_CHUNK

    def body(x_ref, b_ref, c_ref, dt_ref, z_ref, do_ref, sin_ref, dtb_ref, al_ref, df_ref, nw_ref, dproj_in,
             dp_ref, dxa_ref, ddtb_ref, dal_ref, ddf_ref, dnw_ref, dst):
        @pl.when(pl.program_id(0) == 0)
        def _():
            dst[...] = jnp.zeros_like(dst)
            for r in (ddtb_ref, dal_ref, ddf_ref, dnw_ref):
                r[...] = jnp.zeros_like(r)

        states = [sin_ref[0, j] for j in range(N_PAIR)]
        _, vjp = jax.vjp(_ssd_chunk, _blocks(x_ref, 4), _blocks(b_ref, 2), _blocks(c_ref, 2), dt_ref[...],
                         _blocks(z_ref, 4), states, dtb_ref[...], al_ref[...], _blocks(df_ref, 4), _blocks(nw_ref, 4))
        dxs, dbs, dcs, ddt, dzs, dstates, ddtb, dal, ddfs, dnws = vjp(
            ([do_ref[0, :, j * LANES:(j + 1) * LANES] for j in range(N_PAIR)], [dst[j] for j in range(N_PAIR)]))
        for j in range(N_PAIR):
            sl = slice(j * LANES, (j + 1) * LANES)
            dxa_ref[:, sl] = dxs[j]
            dp_ref[:, sl] = dzs[j].astype(BF16)
            dst[j] = dstates[j]
            ddf_ref[:, sl] += ddfs[j]
            dnw_ref[:, sl] += dnws[j]
        for g in range(2):
            dxa_ref[:, BW + g * LANES:BW + (g + 1) * LANES] = dbs[g]
            dxa_ref[:, BW + 256 + g * LANES:BW + 256 + (g + 1) * LANES] = dcs[g]
        dp_ref[:, BW:BW + LANES] = ddt.astype(BF16)
        dp_ref[:, BW + LANES:] = jnp.zeros((M2_CHUNK, 2 * BW - BW - LANES), BF16)
        ddtb_ref[...] += ddtb
        dal_ref[...] += dal

    rev = lambda w, col=0: pl.BlockSpec((M2_CHUNK, w), lambda i: (nc - 1 - i, col))
    vec8 = pl.BlockSpec((1, LANES), lambda i: (0, 0))
    vec = pl.BlockSpec((1, BW), lambda i: (0, 0))
    return pl.pallas_call(
        body, name=name, grid=(nc,),
        in_specs=[rev(BW), rev(256, 2), rev(256, 3), rev(LANES, C_DT // LANES), rev(BW, C_M2Z // BW),
                  pl.BlockSpec((1, M2_CHUNK, BW), lambda i: (2, nc - 1 - i, 0)),
                  pl.BlockSpec((1, N_PAIR, M2_STATE, LANES), lambda i: (nc - 1 - i, 0, 0, 0)), vec8, vec8, vec, vec,
                  pl.BlockSpec(memory_space=pl.ANY)],
        out_specs=[rev(2 * BW, C_M2Z // (2 * BW)), rev(M2_CONV_CH), vec8, vec8, vec, vec],
        input_output_aliases={11: 0},
        out_shape=[jax.ShapeDtypeStruct((SEQ, IN_PAD), BF16), jax.ShapeDtypeStruct((SEQ, M2_CONV_CH), F32),
                   jax.ShapeDtypeStruct((1, LANES), F32), jax.ShapeDtypeStruct((1, LANES), F32),
                   jax.ShapeDtypeStruct((1, BW), F32), jax.ShapeDtypeStruct((1, BW), F32)],
        scratch_shapes=[pltpu.VMEM((N_PAIR, M2_STATE, LANES), F32)],
        compiler_params=_cparams(("arbitrary",)),
    )(xa, xa, xa, proj, proj, dout, s_in, dt_bias, a_log, dfull, nw, dproj)


def _sc_specs():
    col = lambda kind: pl.BlockSpec((SEQ, LANES), lambda j: (0, C_SC // LANES + 4 * j + kind))
    return [col(0), col(1), col(2), col(3)]


def _sc_fwd(proj, w, name):
    def body(b_ref, c_ref, h_ref, g_ref, w_ref, o_ref):
        ch = c_ref[...] * h_ref[...]
        acc = jnp.zeros_like(ch)
        for k in range(SC_CONV):
            acc = acc + w_ref[k:k + 1, :] * _shift_down(ch, SC_CONV - 1 - k)
        o_ref[...] = (b_ref[...] * acc * _silu(g_ref[...])).astype(BF16)

    return pl.pallas_call(
        body, name=name, grid=(BW // LANES,),
        in_specs=_sc_specs() + [pl.BlockSpec((SC_CONV, LANES), lambda j: (0, j))],
        out_specs=pl.BlockSpec((SEQ, LANES), lambda j: (0, j)),
        out_shape=jax.ShapeDtypeStruct((SEQ, BW), BF16),
        compiler_params=_cparams(("parallel",)),
    )(proj, proj, proj, proj, w)


def _sc_bwd(proj, dproj, dout, w, name):
    def body(b_ref, c_ref, h_ref, g_ref, do_ref, w_ref, dproj_in, dp_ref, dw_ref):
        cv, hv, gv = c_ref[...], h_ref[...], g_ref[...]
        ch = cv * hv
        chs = [_shift_down(ch, SC_CONV - 1 - k) for k in range(SC_CONV)]
        acc = jnp.zeros_like(ch)
        for k in range(SC_CONV):
            acc = acc + w_ref[k:k + 1, :] * chs[k]
        sg = jax.nn.sigmoid(gv)
        do = do_ref[0]
        bv = b_ref[...]
        dp_ref[:, 0:LANES] = (do * acc * (gv * sg)).astype(BF16)
        dp_ref[:, 3 * LANES:] = (do * bv * acc * (sg * (1.0 + gv * (1.0 - sg)))).astype(BF16)
        dacc = do * bv * (gv * sg)
        dch = jnp.zeros_like(ch)
        for k in range(SC_CONV):
            dch = dch + w_ref[k:k + 1, :] * _shift_up(dacc, SC_CONV - 1 - k)
            dw_ref[k:k + 1, :] = jnp.sum(dacc * chs[k], axis=0, keepdims=True)
        dp_ref[:, LANES:2 * LANES] = (dch * hv).astype(BF16)
        dp_ref[:, 2 * LANES:3 * LANES] = (dch * cv).astype(BF16)

    wsp = pl.BlockSpec((SC_CONV, LANES), lambda j: (0, j))
    return pl.pallas_call(
        body, name=name, grid=(BW // LANES,),
        in_specs=_sc_specs() + [pl.BlockSpec((1, SEQ, LANES), lambda j: (3, 0, j)), wsp, pl.BlockSpec(memory_space=pl.ANY)],
        out_specs=[pl.BlockSpec((SEQ, 4 * LANES), lambda j: (0, C_SC // (4 * LANES) + j)), wsp],
        input_output_aliases={6: 0},
        out_shape=[jax.ShapeDtypeStruct((SEQ, IN_PAD), BF16), jax.ShapeDtypeStruct((SC_CONV, BW), F32)],
        compiler_params=_cparams(("parallel",)),
    )(proj, proj, proj, proj, dout, w, dproj)


MERGE_T = 256
MERGE_BWD_T = 512


def _merge_fwd(proj, ys, merge_b, w_branch, name):
    def body(y_ref, lg_ref, b_ref, w_ref, o_ref):
        acc = jnp.zeros((MERGE_T, D_MODEL), F32)
        for k in range(N_BRANCH):
            gate = jax.nn.sigmoid(lg_ref[:, k * D_MODEL:(k + 1) * D_MODEL] + b_ref[k])
            acc = acc + gate * _dg(y_ref[k], w_ref[k], 1, 0)
        o_ref[...] = acc.astype(BF16)

    return pl.pallas_call(
        body, name=name, grid=(SEQ // MERGE_T,),
        in_specs=[pl.BlockSpec((N_BRANCH, MERGE_T, BW), lambda i: (0, i, 0)),
                  pl.BlockSpec((MERGE_T, N_BRANCH * D_MODEL), lambda i: (i, C_MERGE // (N_BRANCH * D_MODEL))),
                  pl.BlockSpec((N_BRANCH, 1, D_MODEL), lambda i: (0, 0, 0)),
                  pl.BlockSpec((N_BRANCH, BW, D_MODEL), lambda i: (0, 0, 0))],
        out_specs=pl.BlockSpec((MERGE_T, D_MODEL), lambda i: (i, 0)),
        out_shape=jax.ShapeDtypeStruct((SEQ, D_MODEL), BF16),
        compiler_params=_cparams(("parallel",)),
    )(ys, proj, merge_b, w_branch)


def _merge_bwd(proj, ys, dm, merge_b, w_branch, name):
    nt = SEQ // MERGE_BWD_T

    def body(y_ref, lg_ref, dm_ref, b_ref, w_ref, dy_ref, dlg_ref, dw_ref, db_ref, dw_acc):
        i = pl.program_id(1)
        gate = jax.nn.sigmoid(lg_ref[...] + b_ref[0])
        y = y_ref[0]
        dmv = dm_ref[...]
        dbo = (gate * dmv).astype(BF16)
        dlg = _dg(y, w_ref[0], 1, 0) * dmv * gate * (1.0 - gate)
        dlg_ref[...] = dlg.astype(BF16)
        dy_ref[0] = _dg(dbo, w_ref[0], 1, 1)
        dwp = _dg(y, dbo, 0, 0)
        dbp = jnp.sum(dlg, axis=0, keepdims=True)

        @pl.when(i == 0)
        def _():
            dw_acc[...] = dwp
            db_ref[0] = dbp

        @pl.when(i > 0)
        def _():
            dw_acc[...] += dwp
            db_ref[0] += dbp

        @pl.when(i == nt - 1)
        def _():
            dw_ref[0] = dw_acc[...].astype(BF16)

    return pl.pallas_call(
        body, name=name, grid=(N_BRANCH, nt),
        in_specs=[pl.BlockSpec((1, MERGE_BWD_T, BW), lambda k, i: (k, i, 0)),
                  pl.BlockSpec((MERGE_BWD_T, D_MODEL), lambda k, i: (i, C_MERGE // D_MODEL + k)),
                  pl.BlockSpec((MERGE_BWD_T, D_MODEL), lambda k, i: (i, 0)),
                  pl.BlockSpec((1, 1, D_MODEL), lambda k, i: (k, 0, 0)),
                  pl.BlockSpec((1, BW, D_MODEL), lambda k, i: (k, 0, 0))],
        out_specs=[pl.BlockSpec((1, MERGE_BWD_T, BW), lambda k, i: (k, i, 0)),
                   pl.BlockSpec((MERGE_BWD_T, D_MODEL), lambda k, i: (i, k)),
                   pl.BlockSpec((1, BW, D_MODEL), lambda k, i: (k, 0, 0)),
                   pl.BlockSpec((1, 1, D_MODEL), lambda k, i: (k, 0, 0))],
        out_shape=[jax.ShapeDtypeStruct((N_BRANCH, SEQ, BW), F32), jax.ShapeDtypeStruct((SEQ, IN_PAD), BF16),
                   jax.ShapeDtypeStruct((N_BRANCH, BW, D_MODEL), BF16), jax.ShapeDtypeStruct((N_BRANCH, 1, D_MODEL), F32)],
        scratch_shapes=[pltpu.VMEM((BW, D_MODEL), F32)],
        compiler_params=_cparams(("parallel", "arbitrary")),
    )(ys, proj, dm, merge_b, w_branch)


def _adamw(glist, w, m, v, rows, name):
    nl = len(glist)
    n, r, c = glist[0].shape
    assert w.shape == (nl, r, c) and r % rows == 0
    nb = r // rows

    def body(*refs):
        g_refs = refs[:nl]
        w_ref, m_ref, v_ref, go_ref, d_ref, mo_ref, vo_ref = refs[nl:]
        for layer in range(nl):
            @pl.when(pl.program_id(0) == layer)
            def _(g_ref=g_refs[layer]):
                g = g_ref[0].astype(F32)
                for s in range(1, n):
                    g = g + g_ref[s].astype(F32)
                mn = ADAM_B1 * m_ref[0] + (1.0 - ADAM_B1) * g
                vn = ADAM_B2 * v_ref[0] + (1.0 - ADAM_B2) * jnp.square(g)
                m_hat = mn / (1.0 - ADAM_B1 ** ADAM_STEP)
                v_hat = vn / (1.0 - ADAM_B2 ** ADAM_STEP)
                go_ref[0] = g
                d_ref[0] = -ADAM_LR * (m_hat / (jnp.sqrt(v_hat) + ADAM_EPS) + ADAM_WD * w_ref[0])
                mo_ref[0] = mn
                vo_ref[0] = vn

    def g_spec(layer):
        return pl.BlockSpec((n, rows, c), lambda a, i: (0, jnp.where(a < layer, 0, jnp.where(a == layer, i, nb - 1)), 0))

    blk = pl.BlockSpec((1, rows, c), lambda a, i: (a, i, 0))
    out = jax.ShapeDtypeStruct((nl, r, c), F32)
    return pl.pallas_call(
        body, name=name, grid=(nl, nb),
        in_specs=[g_spec(layer) for layer in range(nl)] + [blk, blk, blk],
        out_specs=[blk, blk, blk, blk], out_shape=[out, out, out, out],
        compiler_params=_cparams(("arbitrary", "arbitrary")),
    )(*glist, w, m, v)


X_ROWS_PER_COL = 2 * (D_MODEL // LANES)


def _w_in_to_x(w):
    t = jnp.transpose(w, (2, 0, 1)).reshape(SHARD_IN, DEPTH, D_MODEL // LANES, LANES)
    return jnp.transpose(t, (0, 2, 1, 3)).reshape(SHARD_IN * X_ROWS_PER_COL, LANES)


def _w_in_from_x(xv):
    t = jnp.transpose(xv.reshape(SHARD_IN, D_MODEL // LANES, DEPTH, LANES), (0, 2, 1, 3))
    return jnp.transpose(t.reshape(SHARD_IN, DEPTH, D_MODEL), (1, 2, 0))


def _adamw_w_in(glist, w, m, v, name):
    n = glist[0].shape[0]
    cols = LANES
    rows = cols * X_ROWS_PER_COL

    def body(g0_ref, g1_ref, w_ref, m_ref, v_ref, go_ref, d_ref, mo_ref, vo_ref):
        for layer, g_ref in enumerate((g0_ref, g1_ref)):
            g = g_ref[0].astype(F32)
            for s in range(1, n):
                g = g + g_ref[s].astype(F32)
            gt = g.T
            for t in range(D_MODEL // LANES):
                sel = (pl.ds(2 * t + layer, cols, stride=X_ROWS_PER_COL), slice(None))
                gs = gt[:, t * LANES:(t + 1) * LANES]
                mn = ADAM_B1 * m_ref[sel] + (1.0 - ADAM_B1) * gs
                vn = ADAM_B2 * v_ref[sel] + (1.0 - ADAM_B2) * jnp.square(gs)
                m_hat = mn / (1.0 - ADAM_B1 ** ADAM_STEP)
                v_hat = vn / (1.0 - ADAM_B2 ** ADAM_STEP)
                go_ref[sel] = gs
                d_ref[sel] = -ADAM_LR * (m_hat / (jnp.sqrt(v_hat) + ADAM_EPS) + ADAM_WD * w_ref[sel])
                mo_ref[sel] = mn
                vo_ref[sel] = vn

    g_spec = pl.BlockSpec((n, D_MODEL, cols), lambda i: (0, 0, i))
    blk = pl.BlockSpec((rows, LANES), lambda i: (i, 0))
    out = jax.ShapeDtypeStruct((SHARD_IN * X_ROWS_PER_COL, LANES), F32)
    res = pl.pallas_call(
        body, name=name, grid=(-(-SHARD_IN // cols),),
        in_specs=[g_spec, g_spec, blk, blk, blk], out_specs=[blk, blk, blk, blk], out_shape=[out, out, out, out],
        compiler_params=_cparams(("parallel",)),
    )(*glist, _w_in_to_x(w), _w_in_to_x(m), _w_in_to_x(v))
    return [_w_in_from_x(o) for o in res]


def _adamw_many(gs, ws, ms, vs, name):
    k = len(gs)

    def body(*refs):
        g_refs, w_refs, m_refs, v_refs = refs[:k], refs[k:2 * k], refs[2 * k:3 * k], refs[3 * k:4 * k]
        d_refs, mo_refs, vo_refs = refs[4 * k:5 * k], refs[5 * k:6 * k], refs[6 * k:7 * k]
        for i in range(k):
            g = g_refs[i][...]
            mn = ADAM_B1 * m_refs[i][...] + (1.0 - ADAM_B1) * g
            vn = ADAM_B2 * v_refs[i][...] + (1.0 - ADAM_B2) * jnp.square(g)
            m_hat = mn / (1.0 - ADAM_B1 ** ADAM_STEP)
            v_hat = vn / (1.0 - ADAM_B2 ** ADAM_STEP)
            d_refs[i][...] = -ADAM_LR * (m_hat / (jnp.sqrt(v_hat) + ADAM_EPS) + ADAM_WD * w_refs[i][...])
            mo_refs[i][...] = mn
            vo_refs[i][...] = vn

    whole = pl.BlockSpec(memory_space=pltpu.VMEM)
    shapes = [jax.ShapeDtypeStruct(w.shape, F32) for w in ws]
    outs = pl.pallas_call(
        body, name=name, in_specs=[whole] * (4 * k), out_specs=[whole] * (3 * k), out_shape=shapes * 3,
        compiler_params=_cparams(None),
    )(*gs, *ws, *ms, *vs)
    return outs[:k], outs[k:2 * k], outs[2 * k:]


MEMORY_ORDER = {'s5_b_re': (0, 1, 3, 2), 's5_b_im': (0, 1, 3, 2), 's5_d': (0, 2, 1), 'sc_conv_w': (1, 0, 2)}


def _memory_view(name, t):
    return jnp.transpose(t, MEMORY_ORDER[name]) if name in MEMORY_ORDER else t


def _slot_sum(gslots, name):
    n, r, c = gslots.shape

    def body(g_ref, o_ref):
        g = g_ref[0]
        for s in range(1, n):
            g = g + g_ref[s]
        o_ref[...] = g

    return pl.pallas_call(
        body, name=name, in_specs=[pl.BlockSpec((n, r, c), lambda: (0, 0, 0))],
        out_specs=pl.BlockSpec((r, c), lambda: (0, 0)), out_shape=jax.ShapeDtypeStruct((r, c), F32),
        compiler_params=_cparams(None),
    )(gslots)


def _me_and_peers():
    x, y, c = lax.axis_index("x"), lax.axis_index("y"), lax.axis_index("c")
    me = 4 * x + 2 * y + c
    peers = []
    for k in range(1, N_DEV):
        px = 1 - x if (k >> 2) & 1 else x
        py = 1 - y if (k >> 1) & 1 else y
        pc = 1 - c if k & 1 else c
        peers.append((4 * px + 2 * py + pc, (px, py, pc)))
    return me, peers


def _exchange(tensors, gather, name):
    n = len(tensors)

    def body(*refs):
        ins, outs = refs[:n], refs[n:2 * n]
        send_sems, recv_sems, local_sems = refs[2 * n:]
        me, peers = _me_and_peers()
        started = []
        for t in range(n):
            own = pltpu.make_async_copy(ins[t] if gather else ins[t].at[me], outs[t].at[me], local_sems.at[t])
            own.start()
            started.append(own)
            for k, (pidx, pos) in enumerate(peers):
                cp = pltpu.make_async_remote_copy(
                    src_ref=ins[t] if gather else ins[t].at[pidx], dst_ref=outs[t].at[me],
                    send_sem=send_sems.at[t, k], recv_sem=recv_sems.at[t, k], device_id=pos, device_id_type=MESH)
                cp.start()
                started.append(cp)
        for cp in started:
            cp.wait()

    any_spec = pl.BlockSpec(memory_space=pl.ANY)
    outs = pl.pallas_call(
        body, name=name, in_specs=[any_spec] * n, out_specs=[any_spec] * n,
        out_shape=[jax.ShapeDtypeStruct(((N_DEV,) + t.shape) if gather else t.shape, t.dtype) for t in tensors],
        scratch_shapes=[pltpu.SemaphoreType.DMA((n, N_DEV - 1)), pltpu.SemaphoreType.DMA((n, N_DEV - 1)),
                        pltpu.SemaphoreType.DMA((n,))],
        compiler_params=pltpu.CompilerParams(has_side_effects=True),
    )(*tensors)
    return list(outs)


_HBM = pl.BlockSpec(memory_space=pltpu.HBM)
_SEM = pl.BlockSpec(memory_space=pltpu.SEMAPHORE)
_EFFECT = pltpu.SideEffectType.DATAFLOW_SIDE_EFFECTING


N_CHIP = N_DEV // 2


def _chip_peers():
    x, y, c = lax.axis_index("x"), lax.axis_index("y"), lax.axis_index("c")
    chips = []
    for d in range(1, N_CHIP):
        px = 1 - x if (d >> 1) & 1 else x
        py = 1 - y if d & 1 else y
        chips.append((2 * px + py, (px, py)))
    return (x, y, c), 2 * x + y, chips


def _plan_gather(ins, lands, send_sems, recv_sems, local_sems, first=0):
    (x, y, c), q, chips = _chip_peers()
    me = 2 * q + c
    plan = dict(start=[], relay_wait=[], relay_start=[], local=[], sends=[], recvs=[])
    for t in range(len(ins)):
        base = (first + t) * 7
        sem = lambda k: dict(send_sem=send_sems.at[base + k], recv_sem=recv_sems.at[base + k], device_id_type=MESH)
        own = pltpu.make_async_copy(ins[t], lands[t].at[me], local_sems.at[first + t])
        to_sib = pltpu.make_async_remote_copy(src_ref=ins[t], dst_ref=lands[t].at[me], device_id=(x, y, 1 - c), **sem(0))
        plan['start'] += [own, to_sib]
        plan['local'].append(own)
        plan['sends'].append(to_sib)
        plan['recvs'].append(to_sib)
        for d, (pq, (px, py)) in enumerate(chips):
            to_chip = pltpu.make_async_remote_copy(src_ref=ins[t], dst_ref=lands[t].at[me], device_id=(px, py, c), **sem(1 + d))
            blk = lands[t].at[2 * pq + c]
            fwd = pltpu.make_async_remote_copy(src_ref=blk, dst_ref=blk, device_id=(x, y, 1 - c), **sem(4 + d))
            plan['start'].append(to_chip)
            plan['relay_wait'].append(to_chip)
            plan['relay_start'].append(fwd)
            plan['sends'] += [to_chip, fwd]
            plan['recvs'].append(fwd)
    return plan


def _plan_pair(ins, lands, send_sems, recv_sems, local_sems):
    (x, y, c), q, chips = _chip_peers()
    plan = dict(start=[], local=[], sends=[], recvs=[])
    for t in range(len(ins)):
        for k in range(N_CHIP):
            cp = pltpu.make_async_remote_copy(
                src_ref=ins[t].at[2 * k + 1 - c], dst_ref=lands[t].at[k], send_sem=send_sems.at[t * N_CHIP + k],
                recv_sem=recv_sems.at[t * N_CHIP + k], device_id=(x, y, 1 - c), device_id_type=MESH)
            plan['start'].append(cp)
            plan['sends'].append(cp)
            plan['recvs'].append(cp)
    return plan


def _plan_chips(ins, lands, send_sems, recv_sems, local_sems):
    (x, y, c), q, chips = _chip_peers()
    plan = dict(start=[], local=[], sends=[], recvs=[])
    for t in range(len(ins)):
        own = pltpu.make_async_copy(ins[t].at[q], lands[t].at[q], local_sems.at[t])
        plan['start'].append(own)
        plan['local'].append(own)
        for d, (pq, (px, py)) in enumerate(chips):
            cp = pltpu.make_async_remote_copy(
                src_ref=ins[t].at[pq], dst_ref=lands[t].at[q], send_sem=send_sems.at[t * 3 + d],
                recv_sem=recv_sems.at[t * 3 + d], device_id=(px, py, c), device_id_type=MESH)
            plan['start'].append(cp)
            plan['sends'].append(cp)
            plan['recvs'].append(cp)
    return plan


def _split_start(plan_fn, tensors, land_shapes, n_sems, name, after=None):
    n = len(tensors)
    extra = [] if after is None else [after]

    def body(*refs):
        ins, lands = refs[:n], refs[n:2 * n]
        plan = plan_fn(ins, lands, *refs[2 * n + len(extra):2 * n + len(extra) + 3])
        for cp in plan['start']:
            cp.start()
        refs[-1][...] = jnp.zeros_like(refs[-1])

    outs = pl.pallas_call(
        body, name=name,
        out_shape=(pltpu.SemaphoreType.DMA((n_sems,)), pltpu.SemaphoreType.DMA((n_sems,)), pltpu.SemaphoreType.DMA((n,)),
                   *[pltpu.HBM(t.shape, t.dtype) for t in tensors],
                   *[pltpu.HBM(s, t.dtype) for s, t in zip(land_shapes, tensors)],
                   jax.ShapeDtypeStruct((8, LANES), F32)),
        in_specs=[_HBM] * (2 * n) + [pl.BlockSpec(memory_space=pl.ANY)] * len(extra),
        out_specs=(_SEM, _SEM, _SEM, *[_HBM] * (2 * n), pl.BlockSpec(memory_space=pltpu.VMEM)),
        input_output_aliases={t: 3 + t for t in range(2 * n)},
        compiler_params=pltpu.CompilerParams(has_side_effects=_EFFECT),
    )(*[pltpu.with_memory_space_constraint(t, pltpu.HBM) for t in tensors],
      *[pltpu.with_memory_space_constraint(lax.empty(s, t.dtype), pltpu.HBM) for s, t in zip(land_shapes, tensors)], *extra)
    return outs[:-1], outs[-1]


def _split_relay(plan_fn, state, after, name):
    sems, thru = state[:3], state[3:]
    n = len(thru) // 2

    def arrived(*refs):
        plan = plan_fn(refs[:n], refs[n:2 * n], *refs[2 * n:2 * n + 3])
        for cp in plan['relay_wait']:
            cp.wait_recv()

    thru = pl.pallas_call(
        arrived, name=name + "_arrived",
        out_shape=tuple(pltpu.HBM(t.shape, t.dtype) for t in thru),
        in_specs=[_HBM] * (2 * n) + [_SEM, _SEM, _SEM, pl.BlockSpec(memory_space=pl.ANY)],
        out_specs=tuple([_HBM] * (2 * n)),
        input_output_aliases={t: t for t in range(2 * n)},
        compiler_params=pltpu.CompilerParams(has_side_effects=_EFFECT),
    )(*thru, *sems, after)

    def forward(*refs):
        plan = plan_fn(refs[:n], refs[n:2 * n], *refs[2 * n:2 * n + 3])
        for cp in plan['relay_start']:
            cp.start()
        refs[-1][...] = jnp.zeros_like(refs[-1])

    outs = pl.pallas_call(
        forward, name=name + "_forward",
        out_shape=(*[pltpu.HBM(t.shape, t.dtype) for t in thru], jax.ShapeDtypeStruct((8, LANES), F32)),
        in_specs=[_HBM] * (2 * n) + [_SEM, _SEM, _SEM],
        out_specs=(*[_HBM] * (2 * n), pl.BlockSpec(memory_space=pltpu.VMEM)),
        input_output_aliases={t: t for t in range(2 * n)},
        compiler_params=pltpu.CompilerParams(has_side_effects=_EFFECT),
    )(*thru, *sems)
    return (*sems, *outs[:-1]), outs[-1]


def _split_wait(plan_fn, state, after, name, with_sources=False):
    sems, thru = state[:3], state[3:]
    n = len(thru) // 2

    def body(*refs):
        plan = plan_fn(refs[:n], refs[n:2 * n], *refs[2 * n:2 * n + 3])
        for cp in plan['local']:
            cp.wait()
        for cp in plan['sends']:
            cp.wait_send()
        for cp in plan['recvs']:
            cp.wait_recv()

    outs = pl.pallas_call(
        body, name=name,
        out_shape=tuple(pltpu.HBM(t.shape, t.dtype) for t in thru),
        in_specs=[_HBM] * (2 * n) + [_SEM, _SEM, _SEM, pl.BlockSpec(memory_space=pl.ANY)],
        out_specs=tuple([_HBM] * (2 * n)),
        input_output_aliases={t: t for t in range(2 * n)},
        compiler_params=pltpu.CompilerParams(has_side_effects=_EFFECT),
    )(*thru, *sems, after)
    return (list(outs[:n]), list(outs[n:])) if with_sources else list(outs[n:])


PAIR_SUM_BLOCK = 512 * 1024


def _pair_sum(mine, theirs, name):
    _, r, c = mine.shape
    rows = r
    while rows * c > PAIR_SUM_BLOCK and rows % 32 == 0:
        rows //= 2

    def body(core_ref, a_ref, b_ref, o_ref):
        o_ref[0] = (a_ref[0].astype(F32) + b_ref[0].astype(F32)).astype(o_ref.dtype)

    return pl.pallas_call(
        body, name=name,
        grid_spec=pltpu.PrefetchScalarGridSpec(
            num_scalar_prefetch=1, grid=(N_CHIP, r // rows),
            in_specs=[pl.BlockSpec((1, rows, c), lambda k, i, core: (2 * k + core[0], i, 0)),
                      pl.BlockSpec((1, rows, c), lambda k, i, core: (k, i, 0))],
            out_specs=pl.BlockSpec((1, rows, c), lambda k, i, core: (k, i, 0))),
        out_shape=jax.ShapeDtypeStruct((N_CHIP, r, c), mine.dtype),
        compiler_params=_cparams(("parallel", "parallel")),
    )(lax.axis_index("c").astype(jnp.int32).reshape(1), mine, theirs)


WEIGHTS = ['norm_w', 'w_in', 's5_lambda_re', 's5_lambda_im', 's5_b_re', 's5_b_im', 's5_c_re', 's5_c_im', 's5_d',
           's5_log_step', 's5_w_glu', 'sgu_ln_w', 'sgu_ln_b', 'sgu_w', 'sgu_b', 'm2_conv_w', 'm2_conv_b', 'm2_dt_bias',
           'm2_a_log', 'm2_d', 'm2_norm_w', 'sc_conv_w', 'merge_b', 'w_branch', 'w_out', 'final_norm_w']
BIG_SHARDED = ['w_in', 'w_branch', 'w_out', 's5_w_glu']
SMALL_SHARDED = ['m2_conv_w', 'sc_conv_w', 'merge_b']
REPLICATED = [n for n in WEIGHTS if n not in BIG_SHARDED + SMALL_SHARDED]
S5_NAMES = ['s5_lambda_re', 's5_lambda_im', 's5_b_re', 's5_b_im', 's5_c_re', 's5_c_im', 's5_d', 's5_log_step']


def _sc_interleave(t):
    lead = t.shape[:-1]
    return jnp.swapaxes(t.reshape(lead + (4, 4, LANES)), -3, -2).reshape(lead + (4 * BW,))


def _pad_in(w):
    z = lambda n: jnp.zeros(w.shape[:-1] + (n,), w.dtype)
    return jnp.concatenate([w[..., 6152:], w[..., 0:1024], w[..., 3072:4096], w[..., 1024:2560], z(512),
                            w[..., 2560:3072], w[..., 4096:4104], z(504), _sc_interleave(w[..., 4104:6152])], axis=-1)


def _unpad_in(g):
    return jnp.concatenate([g[..., C_S5U:C_S5U + 1024], g[..., C_SGU_U:C_SGU_U + 1536], g[..., C_M2Z:C_M2Z + 512],
                            g[..., C_M2X:C_M2X + 1024], g[..., C_DT:C_DT + 8], _sc_interleave(g[..., C_SC:]),
                            g[..., :N_BRANCH * D_MODEL]], axis=-1)


ROW_BLOCK = 8 * LANES


def _pack_rows(tensors, row_mult, batched=False):
    parts = []
    for t in tensors:
        f = t.reshape((t.shape[0], -1) if batched else (1, -1))
        f = jnp.pad(f, ((0, 0), (0, (-f.shape[1]) % ROW_BLOCK)))
        parts.append(f.reshape(f.shape[0], -1, LANES))
    out = jnp.concatenate(parts, axis=1)
    out = jnp.pad(out, ((0, 0), (0, (-out.shape[1]) % row_mult), (0, 0)))
    return out if batched else out[0]


def _unpack_rows(rows, shapes):
    out, r0 = [], 0
    for shp in shapes:
        size = 1
        for s in shp:
            size *= s
        nr = -(-size // ROW_BLOCK) * 8
        out.append(rows[r0:r0 + nr].reshape(-1)[:size].reshape(shp))
        r0 += nr
    return out


def _kernel_col_map():
    m = np.full(IN_PAD, -1, np.int64)
    m[C_MERGE:C_MERGE + 4096] = np.arange(6152, 10248)
    m[C_S5U:C_S5U + 1024] = np.arange(0, 1024)
    m[C_M2X:C_M2X + 1024] = np.arange(3072, 4096)
    m[C_SGU_U:C_SGU_U + 1536] = np.arange(1024, 2560)
    m[C_M2Z:C_M2Z + 512] = np.arange(2560, 3072)
    m[C_DT:C_DT + 8] = np.arange(4096, 4104)
    for j in range(4):
        for kind in range(4):
            k0 = C_SC + 4 * LANES * j + LANES * kind
            m[k0:k0 + LANES] = 4104 + BW * kind + LANES * j + np.arange(LANES)
    return m


def _lane_pieces(sources):
    pieces, cur = [], None
    for lane, src in enumerate(sources):
        key = None if src is None else (src[0], src[1] // LANES, (lane - src[1]) % LANES)
        if cur is not None and key == cur[0]:
            cur[2] = lane + 1
        else:
            if cur is not None and cur[0] is not None:
                pieces.append((*cur[0], cur[1], cur[2]))
            cur = [key, lane, lane + 1]
    if cur is not None and cur[0] is not None:
        pieces.append((*cur[0], cur[1], cur[2]))
    return pieces


def _assemble_block(pieces, load, rows, dtype):
    lane = lax.broadcasted_iota(jnp.int32, (rows, LANES), 1)
    out = None
    for arr, sb, shift, lo, hi in pieces:
        v = load(arr, sb)
        if shift:
            v = pltpu.roll(v, shift, 1)
        if out is None and lo == 0 and hi == LANES:
            out = v
        else:
            out = jnp.where((lane >= lo) & (lane < hi), v, jnp.zeros((rows, LANES), dtype) if out is None else out)
    return jnp.zeros((rows, LANES), dtype) if out is None else out


RELAYOUT_ROWS = 256
SHARD_BLOCKS = -(-SHARD_IN // LANES)


def _load_shard_block(ref, rows):
    def load(j, sb):
        if sb == SHARD_BLOCKS - 1:
            return jnp.broadcast_to(ref[j, :, SHARD_IN - 1:SHARD_IN], (rows, LANES))
        return ref[j, :, sb * LANES:(sb + 1) * LANES]
    return load


def _relayout_w_in(gathered, name):
    kmap = _kernel_col_map()
    dtype = gathered.dtype

    def body(src_ref, o_ref):
        load = _load_shard_block(src_ref, RELAYOUT_ROWS)
        for ob in range(IN_PAD // LANES):
            srcs = [None if kmap[ob * LANES + l] < 0 else (int(kmap[ob * LANES + l]) // SHARD_IN, int(kmap[ob * LANES + l]) % SHARD_IN)
                    for l in range(LANES)]
            o_ref[:, ob * LANES:(ob + 1) * LANES] = _assemble_block(_lane_pieces(srcs), load, RELAYOUT_ROWS, dtype)

    return pl.pallas_call(
        body, name=name, grid=(D_MODEL // RELAYOUT_ROWS,),
        in_specs=[pl.BlockSpec((N_DEV, RELAYOUT_ROWS, SHARD_IN), lambda i: (0, i, 0))],
        out_specs=pl.BlockSpec((RELAYOUT_ROWS, IN_PAD), lambda i: (i, 0)),
        out_shape=jax.ShapeDtypeStruct((D_MODEL, IN_PAD), dtype),
        compiler_params=_cparams(("parallel",)),
    )(gathered)


def _relayout_g_in(gw, name):
    kmap = _kernel_col_map()
    kinv = np.zeros(IN_DIM, np.int64)
    kinv[kmap[kmap >= 0]] = np.nonzero(kmap >= 0)[0]
    dtype = gw.dtype

    def body(src_ref, o_ref):
        load = lambda _, sb: src_ref[:, sb * LANES:(sb + 1) * LANES]
        for j in range(N_DEV):
            for ob in range(SHARD_BLOCKS):
                srcs = [(0, int(kinv[SHARD_IN * j + ob * LANES + l])) if ob * LANES + l < SHARD_IN else None for l in range(LANES)]
                blk = _assemble_block(_lane_pieces(srcs), load, RELAYOUT_ROWS, dtype)
                if ob == SHARD_BLOCKS - 1:
                    o_ref[j, :, SHARD_IN - 1:SHARD_IN] = blk[:, 0:1]
                else:
                    o_ref[j, :, ob * LANES:(ob + 1) * LANES] = blk

    return pl.pallas_call(
        body, name=name, grid=(D_MODEL // RELAYOUT_ROWS,),
        in_specs=[pl.BlockSpec((RELAYOUT_ROWS, IN_PAD), lambda i: (i, 0))],
        out_specs=pl.BlockSpec((N_DEV, RELAYOUT_ROWS, SHARD_IN), lambda i: (0, i, 0)),
        out_shape=jax.ShapeDtypeStruct((N_DEV, D_MODEL, SHARD_IN), dtype),
        compiler_params=_cparams(("parallel",)),
    )(gw)


def _rows128(flat, row_mult=8):
    n = flat.shape[0]
    per = LANES * row_mult
    total = -(-n // per) * per
    return jnp.pad(flat, (0, total - n)).reshape(total // LANES, LANES)


def _pad_lanes(v):
    return jnp.pad(v, (0, LANES - v.shape[0])).reshape(1, LANES)


def _layer_prep(i, p):
    disc, disc_vjp = jax.vjp(_s5_disc, *[p[n][i] for n in S5_NAMES])
    prep = dict(
        nw=p['norm_w'][i].reshape(1, D_MODEL), disc_vjp=disc_vjp,
        s5small=[_block_diag(t).astype(BF16) for t in disc[:4]] + [disc[4], disc[5]],
        sgw=[p['sgu_ln_w'][i].reshape(1, BW), p['sgu_ln_b'][i].reshape(1, BW), p['sgu_w'][i],
             jnp.repeat(p['sgu_b'][i].T, BW // SGU_HEADS, axis=1)],
        cb=p['m2_conv_b'][i].reshape(1, M2_CONV_CH),
        m2w=[_pad_lanes(p['m2_dt_bias'][i]), _pad_lanes(p['m2_a_log'][i]),
             jnp.repeat(p['m2_d'][i], M2_HEAD_DIM).reshape(1, BW), p['m2_norm_w'][i].reshape(1, BW)])
    touch = [t[0, 0].astype(F32) for t in prep['s5small']] + [prep['sgw'][3][0, 0], prep['m2w'][2][0, 0]]
    return prep, sum(touch[1:], touch[0])


def _layer_fwd(x, h, i, prep, w_in, other_weights, before_merge=None):
    proj = _matmul(h, w_in, 1, 0, F32, 1024, 1024, 1024, f"proj{i}")
    full = dict(other_weights(proj), w_in=w_in)
    s5w = prep['s5small'] + [full['s5_w_glu']]
    ya, sre, sim = _s5_fwd(proj, *s5w, f"s5_fwd{i}")
    yb = _sgu_fwd(proj, *prep['sgw'], f"sgu_fwd{i}")
    cw = full['m2_conv_w']
    xa = _m2_conv_fwd(proj, cw, prep['cb'], f"m2conv_fwd{i}")
    yc, s_in = _ssd_fwd(proj, xa, *prep['m2w'], f"ssd_fwd{i}")
    scw = full['sc_conv_w']
    yd = _sc_fwd(proj, scw, f"sc_fwd{i}")
    ys = jnp.stack([ya, yb, yc, yd])
    mb = full['merge_b'].reshape(N_BRANCH, 1, D_MODEL)
    if before_merge is not None:
        mb = mb + before_merge(ys)[0, 0]
    merged = _merge_fwd(proj, ys, mb, full['w_branch'], f"merge_fwd{i}")
    x_new = _matmul(merged, full['w_out'], 1, 0, F32, 1024, 1024, 1024, f"out{i}", residual=x)
    saved = dict(x=x, nw=prep['nw'], h=h, proj=proj, disc_vjp=prep['disc_vjp'], s5w=s5w, sre=sre, sim=sim, sgw=prep['sgw'],
                 cw=cw, cb=prep['cb'], xa=xa, m2w=prep['m2w'], s_in=s_in, scw=scw, ys=ys, mb=mb, merged=merged)
    return x_new, saved, full


def _layer_bwd(dx_out, i, sv, full, on_large_grads=None, after_dh=None):
    g = {}
    proj = sv['proj']
    dm = _matmul(dx_out, full['w_out'], 1, 1, F32, 1024, 1024, 1024, f"dmerged{i}")
    g['w_out'] = _matmul(sv['merged'], dx_out, 0, 0, BF16, 1024, 1024, 1024, f"gw_out{i}")
    dys, dproj, g['w_branch'], dmb = _merge_bwd(proj, sv['ys'], dm, sv['mb'], full['w_branch'], f"merge_bwd{i}")
    g['merge_b'] = dmb.reshape(N_BRANCH, D_MODEL)
    dproj, dbbre, dbbim, dcre, dcim, da, dd, dwg = _s5_bwd(proj, dproj, dys, sv['sre'], sv['sim'], *sv['s5w'], f"s5_bwd{i}")
    g['s5_dense'] = (dbbre, dbbim, dcre, dcim, da, dd)
    g['s5_w_glu'] = dwg.astype(BF16)
    dproj, dlw, dlb, g['sgu_w'], dbias = _sgu_bwd(proj, dproj, dys, *sv['sgw'], f"sgu_bwd{i}")
    g['sgu_ln_w'], g['sgu_ln_b'] = dlw[0], dlb[0]
    g['sgu_b'] = dbias.reshape(SGU_CHUNK, SGU_HEADS, BW // SGU_HEADS).sum(-1).T
    dproj, dxa, ddtb, dal, ddf, dnw = _ssd_bwd(proj, dproj, sv['xa'], dys, sv['s_in'], *sv['m2w'], f"ssd_bwd{i}")
    dproj, g['m2_conv_w'], dcb = _m2_conv_bwd(proj, dproj, dxa, sv['cw'], sv['cb'], f"m2conv_bwd{i}")
    g['m2_conv_b'], g['m2_norm_w'] = dcb[0], dnw[0]
    g['m2_dt_bias'], g['m2_a_log'] = ddtb[0, :M2_HEADS], dal[0, :M2_HEADS]
    g['m2_d'] = ddf.reshape(M2_HEADS, M2_HEAD_DIM).sum(-1)
    dproj, g['sc_conv_w'] = _sc_bwd(proj, dproj, dys, sv['scw'], f"sc_bwd{i}")
    g['w_in'] = _matmul(sv['h'], dproj, 0, 0, BF16, 1024, 1024, 1024, f"gw_in{i}")
    tok = on_large_grads(g) if on_large_grads else None
    dh = _matmul(dproj, full['w_in'], 1, 1, F32, 1024, 1024, 1024, f"dh{i}", after=tok)
    nw = sv['nw'] if after_dh is None else sv['nw'] + after_dh(dh)[0, 0]
    dx_in, dnw_l = _rmsnorm_bwd(sv['x'], nw, dh, dx_out, f"rms_bwd{i}")
    g['norm_w'] = dnw_l[0]
    return dx_in, g


def _split8(t, axis):
    shp = t.shape
    t = t.reshape(shp[:axis] + (N_DEV, shp[axis] // N_DEV) + shp[axis + 1:])
    return jnp.moveaxis(t, axis, 0)


def _join8(t, axis):
    t = jnp.moveaxis(t, 0, axis)
    shp = t.shape
    return t.reshape(shp[:axis] + (shp[axis] * shp[axis + 1],) + shp[axis + 2:])


SHARD_AXIS = {'w_in': 2, 'w_branch': 3, 'w_out': 1, 's5_w_glu': 1, 'm2_conv_w': 2, 'sc_conv_w': 2, 'merge_b': 2}


OTHER_BIG = [n for n in BIG_SHARDED if n != 'w_in']


def _other_weights(gathered):
    return {n: _join8(t, SHARD_AXIS[n] - 1) for n, t in zip(OTHER_BIG, gathered)}


def _layer_grad_blocks(g, i):
    blocks = [_relayout_g_in(g[n], f"relayout_g_in{i}") if n == 'w_in' else _split8(g[n], SHARD_AXIS[n] - 1) for n in BIG_SHARDED]
    return [b.reshape(N_DEV, -1, b.shape[-1]) for b in blocks]


def _pair_start(blocks, i):
    shapes = [(N_CHIP,) + b.shape[1:] for b in blocks]
    return _split_start(_plan_pair, blocks, shapes, N_CHIP * len(blocks), f"pair{i}_start")


def _pair_sums(state, after, i):
    mine, theirs = _split_wait(_plan_pair, state, after, f"pair{i}_wait", with_sources=True)
    return [_pair_sum(b, t, f"pair_sum{i}_{k}") for k, (b, t) in enumerate(zip(mine, theirs))]


def _chips_start(sums, i, after=None):
    return _split_start(_plan_chips, sums, [s.shape for s in sums], 3 * len(sums), f"chips{i}_start", after)


def kernel(x, norm_w, w_in, s5_lambda_re, s5_lambda_im, s5_b_re, s5_b_im, s5_c_re, s5_c_im, s5_d, s5_log_step, s5_w_glu, sgu_ln_w, sgu_ln_b, sgu_w, sgu_b, m2_conv_w, m2_conv_b, m2_dt_bias, m2_a_log, m2_d, m2_norm_w, sc_conv_w, merge_b, w_branch, w_out, final_norm_w, loss_target, m_norm_w, m_w_in, m_s5_lambda_re, m_s5_lambda_im, m_s5_b_re, m_s5_b_im, m_s5_c_re, m_s5_c_im, m_s5_d, m_s5_log_step, m_s5_w_glu, m_sgu_ln_w, m_sgu_ln_b, m_sgu_w, m_sgu_b, m_m2_conv_w, m_m2_conv_b, m_m2_dt_bias, m_m2_a_log, m_m2_d, m_m2_norm_w, m_sc_conv_w, m_merge_b, m_w_branch, m_w_out, m_final_norm_w, v_norm_w, v_w_in, v_s5_lambda_re, v_s5_lambda_im, v_s5_b_re, v_s5_b_im, v_s5_c_re, v_s5_c_im, v_s5_d, v_s5_log_step, v_s5_w_glu, v_sgu_ln_w, v_sgu_ln_b, v_sgu_w, v_sgu_b, v_m2_conv_w, v_m2_conv_b, v_m2_dt_bias, v_m2_a_log, v_m2_d, v_m2_norm_w, v_sc_conv_w, v_merge_b, v_w_branch, v_w_out, v_final_norm_w):
    loc = locals()
    p = {n: loc[n] for n in WEIGHTS}
    mom = {n: loc['m_' + n] for n in WEIGHTS}
    vel = {n: loc['v_' + n] for n in WEIGHTS}

    small_sizes = [p[n].size for n in SMALL_SHARDED]
    small_pack = _rows128(jnp.concatenate([p[n].reshape(-1) for n in SMALL_SHARDED]))
    shards = ([p['w_in'][0].astype(BF16)] + [p[n][0].astype(BF16) for n in OTHER_BIG] + [small_pack]
              + [p[n][1].astype(BF16) for n in BIG_SHARDED])
    gath, tok = _split_start(_plan_gather, shards, [(N_DEV,) + t.shape for t in shards], 7 * len(shards), "gather_start")
    sems, srcs, lands = gath[:3], gath[3:3 + len(shards)], gath[3 + len(shards):]

    def relayed(lo, hi, after, name):
        plan = functools.partial(_plan_gather, first=lo)
        state, tok = _split_relay(plan, (*sems, *srcs[lo:hi], *lands[lo:hi]), after, name + "_relay")
        return (plan, state, name), tok

    def arrived(relay, after):
        plan, state, name = relay
        return _split_wait(plan, state, after, name + "_wait")

    def gathered(lo, hi, after, name):
        relay, tok = relayed(lo, hi, after, name)
        return arrived(relay, tok)

    later = dict(p, **{n: p[n] + tok[0, 0] for n in ('norm_w', 's5_log_step', 'sgu_b', 'm2_d')})
    preps = [_layer_prep(i, later) for i in range(DEPTH)]
    h0 = _rmsnorm_fwd(x[0], preps[0][0]['nw'], "rms_fwd0")
    got = gathered(0, 1, tok + (preps[0][1] + preps[1][1] + h0[0, 0].astype(F32)), "gather_w_in0")
    small_full = {}

    def other_weights0(proj):
        got = gathered(1, 5, proj, "gather_rest0")
        small_all, off = got[-1].reshape(N_DEV, -1), 0
        for n, sz in zip(SMALL_SHARDED, small_sizes):
            small_full[n] = _join8(small_all[:, off:off + sz].reshape((N_DEV,) + p[n].shape), SHARD_AXIS[n])
            off += sz
        return dict(_other_weights(got[:-1]), **{n: small_full[n][0] for n in SMALL_SHARDED})

    saved, layer_g, full = [None] * DEPTH, [None] * DEPTH, [None] * DEPTH
    relay1 = []

    def relay_layer1(ys):
        relay, tok = relayed(5, 9, ys, "gather1")
        relay1.append(relay)
        return tok

    xs, saved[0], full[0] = _layer_fwd(x[0], h0, 0, preps[0][0], _relayout_w_in(got[0], "relayout_w_in0"), other_weights0,
                                       relay_layer1)
    h1 = _rmsnorm_fwd(xs, preps[1][0]['nw'], "rms_fwd1")
    got = arrived(relay1[0], h1)
    xs, saved[1], full[1] = _layer_fwd(
        xs, h1, 1, preps[1][0], _relayout_w_in(got[0], "relayout_w_in1"),
        lambda proj: dict(_other_weights(got[1:]), **{n: small_full[n][1] for n in SMALL_SHARDED}))
    loss_row, dx, dfw = _loss_head(xs, final_norm_w.reshape(1, D_MODEL), loss_target[0])
    loss = lax.psum(loss_row[0, 0], ("x", "y", "c"))
    loss, dx = lax.optimization_barrier((loss, dx))
    pairs, scat = [None] * DEPTH, [None] * DEPTH

    def start_pairs1(g):
        pairs[1], tok = _pair_start(_layer_grad_blocks(g, 1), 1)
        return tok

    def send_chip_sums1(dh):
        scat[1], tok = _chips_start(_pair_sums(pairs[1], dh, 1), 1)
        return tok

    sent0 = []

    def send_all0(g):
        pairs[0], tok = _pair_start(_layer_grad_blocks(g, 0), 0)
        scat[0], tok = _chips_start(_pair_sums(pairs[0], tok, 0), 0)
        sent0.append(tok)
        return tok

    dx, layer_g[1] = _layer_bwd(dx, 1, saved[1], full[1], start_pairs1, send_chip_sums1)
    dx, layer_g[0] = _layer_bwd(dx, 0, saved[0], full[0], send_all0)
    for i in range(DEPTH):
        dense = layer_g[i].pop('s5_dense')
        blocks = tuple(_diag_blocks(t, after=sent0[0]) for t in dense[:4])
        layer_g[i].update(zip(S5_NAMES, saved[i]['disc_vjp'](blocks + (dense[4] + sent0[0][0, 0], dense[5]))))
    grads = {n: jnp.stack([layer_g[i][n] for i in range(DEPTH)]) for n in SMALL_SHARDED + REPLICATED if n != 'final_norm_w'}
    grads['final_norm_w'] = dfw[0]

    out_g, out_d, out_m, out_v = {}, {}, {}, {}
    repl_rows = _pack_rows([grads[n] for n in REPLICATED], 8 * N_DEV)
    rr = repl_rows.shape[0] // N_DEV
    shard_rows = _pack_rows([_split8(grads[n], SHARD_AXIS[n]) for n in SMALL_SHARDED], 8, batched=True)
    rs = shard_rows.shape[1]
    small_g = jnp.concatenate([shard_rows, repl_rows.reshape(N_DEV, rr, LANES)], axis=1)
    small_sum = _slot_sum(_exchange([small_g], False, "scatter_small")[0], "sum_small")
    repl_all = _exchange([small_sum[rs:]], True, "gather_small")[0].reshape(N_DEV * rr, LANES)
    g_all = jnp.concatenate([small_sum[:rs], repl_all], axis=0)
    names = SMALL_SHARDED + REPLICATED
    pieces = (_unpack_rows(g_all[:rs], [p[n].shape for n in SMALL_SHARDED])
              + _unpack_rows(g_all[rs:], [p[n].shape for n in REPLICATED]))
    out_g.update(zip(names, pieces))
    res = _adamw_many(*[[_memory_view(n, d[n]) for n in names] for d in (out_g, p, mom, vel)], "adamw_small")
    for r, dst in zip(res, (out_d, out_m, out_v)):
        dst.update({n: _memory_view(n, t) for n, t in zip(names, r)})

    landed1 = _split_wait(_plan_chips, scat[1], res[0][0], "chips1_wait")
    landed0 = _split_wait(_plan_chips, scat[0], landed1[0], "chips0_wait")
    for k, n in enumerate(BIG_SHARDED):
        shp = p[n].shape
        c = shp[-1]
        r = p[n].size // (DEPTH * c)
        if n == 'w_in':
            big = _adamw_w_in([landed0[k], landed1[k]], p[n], mom[n], vel[n], "adamw_w_in")
        else:
            big = _adamw([landed0[k], landed1[k]], *[d[n].reshape(DEPTH, r, c) for d in (p, mom, vel)],
                         {'w_branch': 512, 'w_out': 128, 's5_w_glu': 64}[n], "adamw_" + n)
        out_g[n], out_d[n], out_m[n], out_v[n] = [o.reshape(shp) for o in big]
    return (loss, dx[None], *[out_g[n] for n in WEIGHTS], *[out_d[n] for n in WEIGHTS],
            *[out_m[n] for n in WEIGHTS], *[out_v[n] for n in WEIGHTS])
```

```python
import functools

import jax
import jax.numpy as jnp
import numpy as np
from jax import lax
from jax.experimental import pallas as pl
from jax.experimental.pallas import tpu as pltpu

F32 = jnp.float32
BF16 = jnp.bfloat16

N_DEV = 8
SEQ = 2048
D_MODEL = 1024
DEPTH = 2
BW = 512
N_BRANCH = 4
EPS = 1e-6
S5_GROUPS, S5_STATE, S5_P = 32, 64, 16
S5_CH = S5_GROUPS * S5_STATE
SGU_CHUNK, SGU_HEADS = 128, 8
M2_HEADS, M2_HEAD_DIM, M2_STATE, M2_CHUNK, M2_CONV = 8, 64, 128, 128, 4
M2_CONV_CH = 1024
SC_CONV = 3
IN_DIM = 10248
IN_PAD = 11264
C_MERGE = 0
C_S5U, C_S5G = 4096, 4608
C_M2X = 5120
C_SGU_U, C_SGU_V, C_SGU_G = 6144, 6656, 7168
C_M2Z, C_DT = 8192, 8704
C_SC = 9216
SHARD_IN = IN_DIM // N_DEV

ADAM_LR, ADAM_B1, ADAM_B2, ADAM_EPS, ADAM_WD, ADAM_STEP = 0.001, 0.9, 0.999, 1e-08, 0.01, 10

VMEM_LIMIT = 56 * 1024 * 1024
LANES = 128

MESH = pl.DeviceIdType.MESH


def _cparams(sem=None, **kw):
    return pltpu.CompilerParams(dimension_semantics=sem, vmem_limit_bytes=VMEM_LIMIT, **kw)


def _dg(a, b, ca, cb, precision=None):
    return lax.dot_general(a, b, (((ca,), (cb,)), ((), ())), precision=precision,
                           preferred_element_type=F32)


@functools.partial(jax.custom_vjp, nondiff_argnums=(2, 3))
def _bdot(a, b, ca, cb):
    return _dg(a.astype(BF16), b.astype(BF16), ca, cb)


def _bdot_fwd(a, b, ca, cb):
    return _bdot(a, b, ca, cb), (a, b)


def _bdot_bwd(ca, cb, res, g):
    a, b = res
    gb, ab, bb = g.astype(BF16), a.astype(BF16), b.astype(BF16)
    da = _dg(gb, bb, 1, 1 - cb) if ca == 1 else _dg(bb, gb, 1 - cb, 1)
    db = _dg(ab, gb, 1 - ca, 0) if cb == 0 else _dg(gb, ab, 0, 1 - ca)
    return da.astype(a.dtype), db.astype(b.dtype)


_bdot.defvjp(_bdot_fwd, _bdot_bwd)


def _rms(x, w):
    return x * lax.rsqrt(jnp.mean(x * x, axis=-1, keepdims=True) + EPS) * w


def _silu(x):
    return x * jax.nn.sigmoid(x)


def _gelu(x):
    return 0.5 * x * (1.0 + jnp.tanh(0.7978845608028654 * (x + 0.044715 * (x * x * x))))


def _softplus(x):
    return jnp.maximum(x, 0.0) + jnp.log1p(jnp.exp(-jnp.abs(x)))


def _shift_down(x, s):
    if s == 0:
        return x
    row = lax.broadcasted_iota(jnp.int32, x.shape, 0)
    return jnp.where(row >= s, pltpu.roll(x, s, 0), 0.0)


def _shift_up(x, s):
    if s == 0:
        return x
    n = x.shape[0]
    row = lax.broadcasted_iota(jnp.int32, x.shape, 0)
    return jnp.where(row < n - s, pltpu.roll(x, n - s, 0), 0.0)


def _matmul(a, b, ca, cb, out_dtype, tm, tn, tk, name, residual=None, after=None):
    m = a.shape[1 - ca]
    k = a.shape[ca]
    n = b.shape[1 - cb]
    assert b.shape[cb] == k and m % tm == 0 and n % tn == 0 and k % tk == 0
    nk = k // tk
    a_spec = pl.BlockSpec((tm, tk), lambda i, j, kk: (i, kk)) if ca == 1 else pl.BlockSpec((tk, tm), lambda i, j, kk: (kk, i))
    b_spec = pl.BlockSpec((tk, tn), lambda i, j, kk: (kk, j)) if cb == 0 else pl.BlockSpec((tn, tk), lambda i, j, kk: (j, kk))
    o_spec = pl.BlockSpec((tm, tn), lambda i, j, kk: (i, j))
    has_res = residual is not None

    def body(*refs):
        refs = refs[:2 + has_res] + refs[2 + has_res + (after is not None):]
        if has_res:
            a_ref, b_ref, r_ref, o_ref, acc = refs
        else:
            a_ref, b_ref, o_ref, acc = refs
        kk = pl.program_id(2)
        part = _dg(a_ref[...].astype(BF16), b_ref[...].astype(BF16), ca, cb)

        @pl.when(kk == 0)
        def _():
            acc[...] = part

        @pl.when(kk > 0)
        def _():
            acc[...] += part

        @pl.when(kk == nk - 1)
        def _():
            r = acc[...]
            if has_res:
                r = r + r_ref[...]
            o_ref[...] = r.astype(out_dtype)

    ins = [a, b] + ([residual] if has_res else []) + ([after] if after is not None else [])
    specs = [a_spec, b_spec] + ([o_spec] if has_res else []) + ([pl.BlockSpec(memory_space=pl.ANY)] if after is not None else [])
    return pl.pallas_call(
        body, name=name, grid=(m // tm, n // tn, nk), in_specs=specs, out_specs=o_spec,
        out_shape=jax.ShapeDtypeStruct((m, n), out_dtype),
        scratch_shapes=[pltpu.VMEM((tm, tn), F32)],
        compiler_params=_cparams(("parallel", "parallel", "arbitrary")),
    )(*ins)


ROW_TILE = 512


def _rmsnorm_fwd(x, w, name):
    def body(x_ref, w_ref, o_ref):
        o_ref[...] = _rms(x_ref[...], w_ref[...]).astype(BF16)

    return pl.pallas_call(
        body, name=name, grid=(SEQ // ROW_TILE,),
        in_specs=[pl.BlockSpec((ROW_TILE, D_MODEL), lambda i: (i, 0)), pl.BlockSpec((1, D_MODEL), lambda i: (0, 0))],
        out_specs=pl.BlockSpec((ROW_TILE, D_MODEL), lambda i: (i, 0)),
        out_shape=jax.ShapeDtypeStruct((SEQ, D_MODEL), BF16),
        compiler_params=_cparams(("parallel",)),
    )(x, w)


def _rmsnorm_bwd(x, w, dh, dres, name):
    def body(x_ref, w_ref, dh_ref, dres_ref, dx_ref, dw_ref):
        _, vjp = jax.vjp(_rms, x_ref[...], w_ref[...])
        dx, dw = vjp(dh_ref[...])
        dx_ref[...] = dx + dres_ref[...]

        @pl.when(pl.program_id(0) == 0)
        def _():
            dw_ref[...] = dw

        @pl.when(pl.program_id(0) > 0)
        def _():
            dw_ref[...] += dw

    tile = pl.BlockSpec((ROW_TILE, D_MODEL), lambda i: (i, 0))
    vec = pl.BlockSpec((1, D_MODEL), lambda i: (0, 0))
    return pl.pallas_call(
        body, name=name, grid=(SEQ // ROW_TILE,),
        in_specs=[tile, vec, tile, tile], out_specs=[tile, vec],
        out_shape=[jax.ShapeDtypeStruct((SEQ, D_MODEL), F32), jax.ShapeDtypeStruct((1, D_MODEL), F32)],
        compiler_params=_cparams(("arbitrary",)),
    )(x, w, dh, dres)


def _loss_head(x, w, target):
    def body(x_ref, w_ref, t_ref, loss_ref, dx_ref, dw_ref):
        tgt = t_ref[...]

        def f(xv, wv):
            err = _rms(xv, wv) - tgt
            return 0.5 * jnp.sum(jnp.mean(err * err, axis=-1))

        loss, vjp = jax.vjp(f, x_ref[...], w_ref[...])
        dx, dw = vjp(jnp.ones((), F32))
        dx_ref[...] = dx
        lrow = jnp.full((1, LANES), loss, F32)

        @pl.when(pl.program_id(0) == 0)
        def _():
            dw_ref[...] = dw
            loss_ref[...] = lrow

        @pl.when(pl.program_id(0) > 0)
        def _():
            dw_ref[...] += dw
            loss_ref[...] += lrow

    tile = pl.BlockSpec((ROW_TILE, D_MODEL), lambda i: (i, 0))
    vec = pl.BlockSpec((1, D_MODEL), lambda i: (0, 0))
    return pl.pallas_call(
        body, name="loss_head", grid=(SEQ // ROW_TILE,),
        in_specs=[tile, vec, tile], out_specs=[pl.BlockSpec((1, LANES), lambda i: (0, 0)), tile, vec],
        out_shape=[jax.ShapeDtypeStruct((1, LANES), F32), jax.ShapeDtypeStruct((SEQ, D_MODEL), F32),
                   jax.ShapeDtypeStruct((1, D_MODEL), F32)],
        compiler_params=_cparams(("arbitrary",)),
    )(x, w, target)


S5_T = 256
S5_BLOCKS = [(slice(j * 256, (j + 1) * 256), slice(j * 1024, (j + 1) * 1024)) for j in range(2)]


def _s5_post(ypre, gate, wglu):
    y = _gelu(ypre)
    y = y * jax.nn.sigmoid(_bdot(y, wglu, 1, 0))
    return y * _silu(gate)


def _s5_fwd(proj, bbre, bbim, cre, cim, a2, dvec, wglu, name):
    def body(u_ref, g_ref, bbre_ref, bbim_ref, cre_ref, cim_ref, a_ref, d_ref, wg_ref, o_ref, sre_ref, sim_ref, st):
        @pl.when(pl.program_id(0) == 0)
        def _():
            st[...] = jnp.zeros_like(st)

        u = u_ref[...]
        ub = u.astype(BF16)
        for us, ss in S5_BLOCKS:
            sre_ref[:, ss] = _dg(ub[:, us], bbre_ref[us, ss], 1, 0)
            sim_ref[:, ss] = _dg(ub[:, us], bbim_ref[us, ss], 1, 0)
        ar, ai = a_ref[0:1, :], a_ref[1:2, :]

        def step(t, carry):
            sr, si = carry
            nr = ar * sr - ai * si + sre_ref[pl.ds(t, 1), :]
            ni = ar * si + ai * sr + sim_ref[pl.ds(t, 1), :]
            sre_ref[pl.ds(t, 1), :] = nr
            sim_ref[pl.ds(t, 1), :] = ni
            return nr, ni

        sr, si = lax.fori_loop(0, S5_T, step, (st[0:1, :], st[1:2, :]), unroll=8)
        st[0:1, :] = sr
        st[1:2, :] = si
        ypre = jnp.concatenate(
            [_dg(sre_ref[:, ss].astype(BF16), cre_ref[ss, us], 1, 0) - _dg(sim_ref[:, ss].astype(BF16), cim_ref[ss, us], 1, 0)
             for us, ss in S5_BLOCKS], axis=1) + d_ref[...] * u
        o_ref[...] = _s5_post(ypre, g_ref[...], wg_ref[...]).astype(BF16)

    full = lambda shape: pl.BlockSpec(shape, lambda c: (0, 0))
    return pl.pallas_call(
        body, name=name, grid=(SEQ // S5_T,),
        in_specs=[pl.BlockSpec((S5_T, BW), lambda c: (c, C_S5U // BW)), pl.BlockSpec((S5_T, BW), lambda c: (c, C_S5G // BW)),
                  full((BW, S5_CH)), full((BW, S5_CH)), full((S5_CH, BW)), full((S5_CH, BW)),
                  full((2, S5_CH)), full((1, BW)), full((BW, BW))],
        out_specs=[pl.BlockSpec((S5_T, BW), lambda c: (c, 0)), pl.BlockSpec((S5_T, S5_CH), lambda c: (c, 0)),
                   pl.BlockSpec((S5_T, S5_CH), lambda c: (c, 0))],
        out_shape=[jax.ShapeDtypeStruct((SEQ, BW), BF16), jax.ShapeDtypeStruct((SEQ, S5_CH), F32),
                   jax.ShapeDtypeStruct((SEQ, S5_CH), F32)],
        scratch_shapes=[pltpu.VMEM((2, S5_CH), F32)],
        compiler_params=_cparams(("arbitrary",)),
    )(proj, proj, bbre, bbim, cre, cim, a2, dvec, wglu)


def _s5_bwd(proj, dproj, dout, sre, sim, bbre, bbim, cre, cim, a2, dvec, wglu, name):
    nc = SEQ // S5_T

    def body(u_ref, g_ref, do_ref, sre_ref, sim_ref, pre_ref, pim_ref, bbre_ref, bbim_ref, cre_ref, cim_ref, a_ref,
             d_ref, wg_ref, dproj_in, dp_ref, dbbre_ref, dbbim_ref, dcre_ref, dcim_ref, da_ref, dd_ref, dwg_ref,
             gre, gim, st):
        c = nc - 1 - pl.program_id(0)

        @pl.when(pl.program_id(0) == 0)
        def _():
            st[...] = jnp.zeros_like(st)
            for r in (dbbre_ref, dbbim_ref, dcre_ref, dcim_ref, da_ref, dd_ref, dwg_ref):
                r[...] = jnp.zeros_like(r)

        u = u_ref[...]
        s_re, s_im = sre_ref[...], sim_ref[...]

        def head(s_res, s_ims, cres, cims, dv, uv, gv, wg):
            ypre = jnp.concatenate([_bdot(sr, cr, 1, 0) - _bdot(si, ci, 1, 0)
                                    for sr, si, cr, ci in zip(s_res, s_ims, cres, cims)], axis=1) + dv * uv
            return _s5_post(ypre, gv, wg)

        _, vjp = jax.vjp(head, [sre_ref[:, ss] for _, ss in S5_BLOCKS], [sim_ref[:, ss] for _, ss in S5_BLOCKS],
                         [cre_ref[ss, us].astype(F32) for us, ss in S5_BLOCKS],
                         [cim_ref[ss, us].astype(F32) for us, ss in S5_BLOCKS],
                         d_ref[...], u, g_ref[...], wg_ref[...].astype(F32))
        ds_res, ds_ims, dcres, dcims, dd, du_d, dgate, dwg = vjp(do_ref[0])
        for k, (us, ss) in enumerate(S5_BLOCKS):
            dcre_ref[ss, us] += dcres[k]
            dcim_ref[ss, us] += dcims[k]
            gre[:, ss] = ds_res[k]
            gim[:, ss] = ds_ims[k]
        dd_ref[...] += dd
        dwg_ref[...] += dwg
        dp_ref[:, BW:] = dgate.astype(BF16)
        ar, ai = a_ref[0:1, :], a_ref[1:2, :]

        def step(i, carry):
            t = S5_T - 1 - i
            gr, gi = carry
            nr = gre[pl.ds(t, 1), :] + gr
            ni = gim[pl.ds(t, 1), :] + gi
            gre[pl.ds(t, 1), :] = nr
            gim[pl.ds(t, 1), :] = ni
            return ar * nr + ai * ni, ar * ni - ai * nr

        gr, gi = lax.fori_loop(0, S5_T, step, (st[0:1, :], st[1:2, :]), unroll=8)
        st[0:1, :] = gr
        st[1:2, :] = gi
        g_re, g_im = gre[...], gim[...]
        first = jnp.where(c > 0, 1.0, 0.0)
        row = lax.broadcasted_iota(jnp.int32, (S5_T, S5_CH), 0)
        p_re = jnp.where(row == 0, pre_ref[7:8, :] * first, pltpu.roll(s_re, 1, 0))
        p_im = jnp.where(row == 0, pim_ref[7:8, :] * first, pltpu.roll(s_im, 1, 0))
        da_ref[0:1, :] += jnp.sum(g_re * p_re + g_im * p_im, axis=0, keepdims=True)
        da_ref[1:2, :] += jnp.sum(g_im * p_re - g_re * p_im, axis=0, keepdims=True)
        ub, grb, gib = u.astype(BF16), g_re.astype(BF16), g_im.astype(BF16)
        du_s = []
        for us, ss in S5_BLOCKS:
            dbbre_ref[us, ss] += _dg(ub[:, us], grb[:, ss], 0, 0)
            dbbim_ref[us, ss] += _dg(ub[:, us], gib[:, ss], 0, 0)
            du_s.append(_dg(grb[:, ss], bbre_ref[us, ss], 1, 1) + _dg(gib[:, ss], bbim_ref[us, ss], 1, 1))
        dp_ref[:, :BW] = (du_d + jnp.concatenate(du_s, axis=1)).astype(BF16)

    full = lambda shape: pl.BlockSpec(shape, lambda i: (0, 0))
    rev = lambda w, col=0: pl.BlockSpec((S5_T, w), lambda i: (nc - 1 - i, col))
    prev = pl.BlockSpec((8, S5_CH), lambda i: (jnp.maximum((nc - 1 - i) * (S5_T // 8) - 1, 0), 0))
    return pl.pallas_call(
        body, name=name, grid=(nc,),
        in_specs=[rev(BW, C_S5U // BW), rev(BW, C_S5G // BW), pl.BlockSpec((1, S5_T, BW), lambda i: (0, nc - 1 - i, 0)),
                  rev(S5_CH), rev(S5_CH), prev, prev,
                  full((BW, S5_CH)), full((BW, S5_CH)), full((S5_CH, BW)), full((S5_CH, BW)),
                  full((2, S5_CH)), full((1, BW)), full((BW, BW)), pl.BlockSpec(memory_space=pl.ANY)],
        out_specs=[rev(2 * BW, C_S5U // (2 * BW)), full((BW, S5_CH)), full((BW, S5_CH)), full((S5_CH, BW)), full((S5_CH, BW)),
                   full((2, S5_CH)), full((1, BW)), full((BW, BW))],
        input_output_aliases={14: 0},
        out_shape=[jax.ShapeDtypeStruct((SEQ, IN_PAD), BF16),
                   jax.ShapeDtypeStruct((BW, S5_CH), F32), jax.ShapeDtypeStruct((BW, S5_CH), F32),
                   jax.ShapeDtypeStruct((S5_CH, BW), F32), jax.ShapeDtypeStruct((S5_CH, BW), F32),
                   jax.ShapeDtypeStruct((2, S5_CH), F32), jax.ShapeDtypeStruct((1, BW), F32),
                   jax.ShapeDtypeStruct((BW, BW), F32)],
        scratch_shapes=[pltpu.VMEM((S5_T, S5_CH), F32), pltpu.VMEM((S5_T, S5_CH), F32), pltpu.VMEM((2, S5_CH), F32)],
        compiler_params=_cparams(("arbitrary",)),
    )(proj, proj, dout, sre, sim, sre, sim, bbre, bbim, cre, cim, a2, dvec, wglu, dproj)


def _diag_blocks(dense, after=None):
    rows, cols = dense.shape
    rows_per, cols_per = rows // S5_GROUPS, cols // S5_GROUPS
    per_lane_block = LANES // cols_per
    tile = 512

    def body(d_ref, *rest):
        o_ref = rest[-1]
        r0 = pl.program_id(0) * tile
        grp = (r0 + lax.broadcasted_iota(jnp.int32, (tile, LANES), 0)) // rows_per
        lane = lax.broadcasted_iota(jnp.int32, (tile, LANES), 1)
        acc = jnp.zeros((tile, LANES), F32)
        for hb in range(cols // LANES):
            acc = acc + jnp.where(grp == per_lane_block * hb + lane // cols_per, d_ref[:, hb * LANES:(hb + 1) * LANES], 0.0)
        shift = LANES // 2
        while shift >= cols_per:
            acc = acc + pltpu.roll(acc, LANES - shift, 1)
            shift //= 2
        o_ref[...] = acc

    folded = pl.pallas_call(
        body, name=f"diag_blocks_{rows_per}x{cols_per}", grid=(rows // tile,),
        in_specs=[pl.BlockSpec((tile, cols), lambda i: (i, 0))] + ([] if after is None else [pl.BlockSpec(memory_space=pl.ANY)]),
        out_specs=pl.BlockSpec((tile, LANES), lambda i: (i, 0)),
        out_shape=jax.ShapeDtypeStruct((rows, LANES), F32), compiler_params=_cparams(("parallel",)),
    )(dense, *([] if after is None else [after]))
    return folded[:, :cols_per].reshape(S5_GROUPS, rows_per, cols_per)


def _block_diag(t):
    g, rows_per, cols_per = t.shape
    wide = jnp.tile(t.reshape(g * rows_per, cols_per), (1, g))
    r = lax.broadcasted_iota(jnp.int32, wide.shape, 0) // rows_per
    c = lax.broadcasted_iota(jnp.int32, wide.shape, 1) // cols_per
    return jnp.where(r == c, wide, 0.0)


def _s5_disc(lam_re, lam_im, b_re, b_im, c_re, c_im, d, log_step):
    step = jnp.exp(log_step)[:, None]
    mag = jnp.exp(lam_re * step)
    ab_re, ab_im = mag * jnp.cos(lam_im * step), mag * jnp.sin(lam_im * step)
    den = lam_re * lam_re + lam_im * lam_im
    nr = ab_re - 1.0
    coef_re = (nr * lam_re + ab_im * lam_im) / den
    coef_im = (ab_im * lam_re - nr * lam_im) / den
    bb_re = coef_re[..., None] * b_re - coef_im[..., None] * b_im
    bb_im = coef_re[..., None] * b_im + coef_im[..., None] * b_re
    a2 = jnp.stack([ab_re.reshape(-1), ab_im.reshape(-1)])
    return (jnp.swapaxes(bb_re, 1, 2), jnp.swapaxes(bb_im, 1, 2),
            jnp.swapaxes(c_re, 1, 2), jnp.swapaxes(c_im, 1, 2),
            a2, d.reshape(1, BW))


def _left_lanes(shape):
    return lax.broadcasted_iota(jnp.int32, shape, 1) < 64


def _sgu_chunk(u, v, gate, ln_w, ln_b, w, bias):
    u32, v32 = _gelu(u), _gelu(v)
    mu = jnp.mean(v32, axis=-1, keepdims=True)
    var = jnp.mean(jnp.square(v32 - mu), axis=-1, keepdims=True)
    vn = (v32 - mu) * lax.rsqrt(var + EPS) * ln_w + ln_b
    t_i = lax.broadcasted_iota(jnp.int32, (SGU_CHUNK, SGU_CHUNK), 0)
    s_i = lax.broadcasted_iota(jnp.int32, (SGU_CHUNK, SGU_CHUNK), 1)
    causal = t_i >= s_i
    left = _left_lanes((SGU_CHUNK, LANES))
    sgate = _silu(gate)
    outs = []
    for j in range(BW // LANES):
        vb = vn[:, j * LANES:(j + 1) * LANES]
        s_blk = (_bdot(jnp.where(causal, w[2 * j], 0.0), jnp.where(left, vb, 0.0), 1, 0)
                 + _bdot(jnp.where(causal, w[2 * j + 1], 0.0), jnp.where(left, 0.0, vb), 1, 0))
        sl = slice(j * LANES, (j + 1) * LANES)
        outs.append(u32[:, sl] * (s_blk + bias[:, sl]) * sgate[:, sl])
    return outs


def _sgu_fwd(proj, ln_w, ln_b, w, bias, name):
    def body(u_ref, v_ref, g_ref, lw_ref, lb_ref, w_ref, b_ref, o_ref):
        outs = _sgu_chunk(u_ref[...], v_ref[...], g_ref[...], lw_ref[...], lb_ref[...], w_ref[...], b_ref[...])
        for j, o in enumerate(outs):
            o_ref[:, j * LANES:(j + 1) * LANES] = o.astype(BF16)

    blk = lambda col: pl.BlockSpec((SGU_CHUNK, BW), lambda c: (c, col // BW))
    vec = pl.BlockSpec((1, BW), lambda c: (0, 0))
    return pl.pallas_call(
        body, name=name, grid=(SEQ // SGU_CHUNK,),
        in_specs=[blk(C_SGU_U), blk(C_SGU_V), blk(C_SGU_G), vec, vec,
                  pl.BlockSpec((SGU_HEADS, SGU_CHUNK, SGU_CHUNK), lambda c: (0, 0, 0)),
                  pl.BlockSpec((SGU_CHUNK, BW), lambda c: (0, 0))],
        out_specs=pl.BlockSpec((SGU_CHUNK, BW), lambda c: (c, 0)),
        out_shape=jax.ShapeDtypeStruct((SEQ, BW), BF16),
        compiler_params=_cparams(("parallel",)),
    )(proj, proj, proj, ln_w, ln_b, w, bias)


def _sgu_bwd(proj, dproj, dout, ln_w, ln_b, w, bias, name):
    def body(u_ref, v_ref, g_ref, do_ref, lw_ref, lb_ref, w_ref, b_ref, dproj_in, dp_ref, dlw_ref, dlb_ref, dw_ref, db_ref):
        _, vjp = jax.vjp(_sgu_chunk, u_ref[...], v_ref[...], g_ref[...], lw_ref[...], lb_ref[...], w_ref[...], b_ref[...])
        do = do_ref[0]
        du, dv, dgate, dlw, dlb, dw, db = vjp([do[:, j * LANES:(j + 1) * LANES] for j in range(BW // LANES)])
        dp_ref[:, 0:BW] = du.astype(BF16)
        dp_ref[:, BW:2 * BW] = dv.astype(BF16)
        dp_ref[:, 2 * BW:3 * BW] = dgate.astype(BF16)
        dp_ref[:, 3 * BW:] = jnp.zeros((SGU_CHUNK, BW), BF16)

        @pl.when(pl.program_id(0) == 0)
        def _():
            dlw_ref[...] = dlw
            dlb_ref[...] = dlb
            dw_ref[...] = dw
            db_ref[...] = db

        @pl.when(pl.program_id(0) > 0)
        def _():
            dlw_ref[...] += dlw
            dlb_ref[...] += dlb
            dw_ref[...] += dw
            db_ref[...] += db

    blk = lambda col: pl.BlockSpec((SGU_CHUNK, BW), lambda c: (c, col // BW))
    vec = pl.BlockSpec((1, BW), lambda c: (0, 0))
    wsp = pl.BlockSpec((SGU_HEADS, SGU_CHUNK, SGU_CHUNK), lambda c: (0, 0, 0))
    bsp = pl.BlockSpec((SGU_CHUNK, BW), lambda c: (0, 0))
    return pl.pallas_call(
        body, name=name, grid=(SEQ // SGU_CHUNK,),
        in_specs=[blk(C_SGU_U), blk(C_SGU_V), blk(C_SGU_G), pl.BlockSpec((1, SGU_CHUNK, BW), lambda c: (1, c, 0)),
                  vec, vec, wsp, bsp, pl.BlockSpec(memory_space=pl.ANY)],
        out_specs=[pl.BlockSpec((SGU_CHUNK, 4 * BW), lambda c: (c, C_SGU_U // (4 * BW))), vec, vec, wsp, bsp],
        input_output_aliases={8: 0},
        out_shape=[jax.ShapeDtypeStruct((SEQ, IN_PAD), BF16), jax.ShapeDtypeStruct((1, BW), F32),
                   jax.ShapeDtypeStruct((1, BW), F32), jax.ShapeDtypeStruct((SGU_HEADS, SGU_CHUNK, SGU_CHUNK), F32),
                   jax.ShapeDtypeStruct((SGU_CHUNK, BW), F32)],
        compiler_params=_cparams(("arbitrary",)),
    )(proj, proj, proj, dout, ln_w, ln_b, w, bias, dproj)


CONV_BLK = 256


def _m2_conv_fwd(proj, w, b, name):
    def body(x_ref, w_ref, b_ref, o_ref):
        x = x_ref[...]
        acc = jnp.zeros_like(x) + b_ref[...]
        for k in range(M2_CONV):
            acc = acc + w_ref[k:k + 1, :] * _shift_down(x, M2_CONV - 1 - k)
        o_ref[...] = _silu(acc)

    return pl.pallas_call(
        body, name=name, grid=(M2_CONV_CH // CONV_BLK,),
        in_specs=[pl.BlockSpec((SEQ, CONV_BLK), lambda j: (0, C_M2X // CONV_BLK + j)),
                  pl.BlockSpec((M2_CONV, CONV_BLK), lambda j: (0, j)), pl.BlockSpec((1, CONV_BLK), lambda j: (0, j))],
        out_specs=pl.BlockSpec((SEQ, CONV_BLK), lambda j: (0, j)),
        out_shape=jax.ShapeDtypeStruct((SEQ, M2_CONV_CH), F32),
        compiler_params=_cparams(("parallel",)),
    )(proj, w, b)


def _m2_conv_bwd(proj, dproj, dxa, w, b, name):
    def body(x_ref, d_ref, w_ref, b_ref, dproj_in, dx_ref, dw_ref, db_ref):
        x = x_ref[...]
        xs = [_shift_down(x, M2_CONV - 1 - k) for k in range(M2_CONV)]
        acc = jnp.zeros_like(x) + b_ref[...]
        for k in range(M2_CONV):
            acc = acc + w_ref[k:k + 1, :] * xs[k]
        sg = jax.nn.sigmoid(acc)
        dacc = d_ref[...] * (sg * (1.0 + acc * (1.0 - sg)))
        dx = jnp.zeros_like(x)
        for k in range(M2_CONV):
            dx = dx + w_ref[k:k + 1, :] * _shift_up(dacc, M2_CONV - 1 - k)
            dw_ref[k:k + 1, :] = jnp.sum(dacc * xs[k], axis=0, keepdims=True)
        dx_ref[...] = dx.astype(BF16)
        db_ref[...] = jnp.sum(dacc, axis=0, keepdims=True)

    return pl.pallas_call(
        body, name=name, grid=(M2_CONV_CH // CONV_BLK,),
        in_specs=[pl.BlockSpec((SEQ, CONV_BLK), lambda j: (0, C_M2X // CONV_BLK + j)),
                  pl.BlockSpec((SEQ, CONV_BLK), lambda j: (0, j)),
                  pl.BlockSpec((M2_CONV, CONV_BLK), lambda j: (0, j)), pl.BlockSpec((1, CONV_BLK), lambda j: (0, j)),
                  pl.BlockSpec(memory_space=pl.ANY)],
        out_specs=[pl.BlockSpec((SEQ, CONV_BLK), lambda j: (0, C_M2X // CONV_BLK + j)),
                   pl.BlockSpec((M2_CONV, CONV_BLK), lambda j: (0, j)), pl.BlockSpec((1, CONV_BLK), lambda j: (0, j))],
        input_output_aliases={4: 0},
        out_shape=[jax.ShapeDtypeStruct((SEQ, IN_PAD), BF16), jax.ShapeDtypeStruct((M2_CONV, M2_CONV_CH), F32),
                   jax.ShapeDtypeStruct((1, M2_CONV_CH), F32)],
        compiler_params=_cparams(("parallel",)),
    )(proj, dxa, w, b, dproj)


N_PAIR = M2_HEADS // 2
HI = lax.Precision.HIGHEST


def _col(a, h):
    lane = lax.broadcasted_iota(jnp.int32, a.shape, 1)
    return jnp.sum(jnp.where(lane == h, a, 0.0), axis=1, keepdims=True)


def _row(a, h):
    sub = lax.broadcasted_iota(jnp.int32, a.shape, 0)
    return jnp.sum(jnp.where(sub == h, a, 0.0), axis=0, keepdims=True)


def _ssd_chunk(xs, bms, cms, dtr, zs, states, dt_bias, a_log, dfs, nws):
    q = M2_CHUNK
    dt = _softplus(dtr + dt_bias)
    da = dt * (-jnp.exp(a_log))
    l_i = lax.broadcasted_iota(jnp.int32, (q, q), 0)
    s_i = lax.broadcasted_iota(jnp.int32, (q, q), 1)
    causal = l_i >= s_i
    tril = jnp.where(causal, 1.0, 0.0)
    a_cs = _dg(tril, da, 1, 0, HI)
    a_cs_t = _dg(da, tril, 0, 1, HI)
    a_end = _row(a_cs, q - 1)
    left = _left_lanes((q, LANES))
    left1 = _left_lanes((1, LANES))
    ys, nexts = [], []
    for j in range(N_PAIR):
        grp = j // 2
        bm, cm = bms[grp], cms[grp]
        h0, h1 = 2 * j, 2 * j + 1
        cb = _bdot(cm, bm, 1, 1)
        xdt = xs[j] * jnp.where(left, _col(dt, h0), _col(dt, h1))
        acs0, acs1 = _col(a_cs, h0), _col(a_cs, h1)
        y = _bdot(cm, states[j], 1, 0) * jnp.where(left, jnp.exp(acs0), jnp.exp(acs1))
        s_new = states[j] * jnp.where(left1, jnp.exp(_col(a_end, h0)), jnp.exp(_col(a_end, h1)))
        for h, acs, xh in ((h0, acs0, jnp.where(left, xdt, 0.0)), (h1, acs1, jnp.where(left, 0.0, xdt))):
            decay = jnp.exp(jnp.where(causal, acs - _row(a_cs_t, h), -jnp.inf))
            y = y + _bdot(cb * decay, xh, 1, 0)
            s_new = s_new + _bdot(bm * jnp.exp(_col(a_end, h) - acs), xh, 0, 0)
        ys.append((y + dfs[j] * xs[j]) * _silu(zs[j]))
        nexts.append(s_new)
    ssq = sum(jnp.sum(y * y, axis=-1, keepdims=True) for y in ys)
    scale = lax.rsqrt(ssq / BW + EPS)
    return [y * scale * nw for y, nw in zip(ys, nws)], nexts


def _blocks(ref, n, width=LANES):
    return [ref[:, j * width:(j + 1) * width] for j in range(n)]


def _ssd_fwd(proj, xa, dt_bias, a_log, dfull, nw, name):
    nc = SEQ // M2_CHUNK

    def body(x_ref, b_ref, c_ref, dt_ref, z_ref, dtb_ref, al_ref, df_ref, nw_ref, o_ref, sin_ref, st):
        @pl.when(pl.program_id(0) == 0)
        def _():
            st[...] = jnp.zeros_like(st)

        states = [st[j] for j in range(N_PAIR)]
        for j in range(N_PAIR):
            sin_ref[0, j] = states[j]
        ys, nexts = _ssd_chunk(_blocks(x_ref, 4), _blocks(b_ref, 2), _blocks(c_ref, 2), dt_ref[...], _blocks(z_ref, 4),
                               states, dtb_ref[...], al_ref[...], _blocks(df_ref, 4), _blocks(nw_ref, 4))
        for j in range(N_PAIR):
            o_ref[:, j * LANES:(j + 1) * LANES] = ys[j].astype(BF16)
            st[j] = nexts[j]

    vec8 = pl.BlockSpec((1, LANES), lambda c: (0, 0))
    vec = pl.BlockSpec((1, BW), lambda c: (0, 0))
    return pl.pallas_call(
        body, name=name, grid=(nc,),
        in_specs=[pl.BlockSpec((M2_CHUNK, BW), lambda c: (c, 0)), pl.BlockSpec((M2_CHUNK, 256), lambda c: (c, 2)),
                  pl.BlockSpec((M2_CHUNK, 256), lambda c: (c, 3)), pl.BlockSpec((M2_CHUNK, LANES), lambda c: (c, C_DT // LANES)),
                  pl.BlockSpec((M2_CHUNK, BW), lambda c: (c, C_M2Z // BW)), vec8, vec8, vec, vec],
        out_specs=[pl.BlockSpec((M2_CHUNK, BW), lambda c: (c, 0)),
                   pl.BlockSpec((1, N_PAIR, M2_STATE, LANES), lambda c: (c, 0, 0, 0))],
        out_shape=[jax.ShapeDtypeStruct((SEQ, BW), BF16), jax.ShapeDtypeStruct((nc, N_PAIR, M2_STATE, LANES), F32)],
        scratch_shapes=[pltpu.VMEM((N_PAIR, M2_STATE, LANES), F32)],
        compiler_params=_cparams(("arbitrary",)),
    )(xa, xa, xa, proj, proj, dt_bias, a_log, dfull, nw)


def _ssd_bwd(proj, dproj, xa, dout, s_in, dt_bias, a_log, dfull, nw, name):
    nc = SEQ // M2_CHUNK

    def body(x_ref, b_ref, c_ref, dt_ref, z_ref, do_ref, sin_ref, dtb_ref, al_ref, df_ref, nw_ref, dproj_in,
             dp_ref, dxa_ref, ddtb_ref, dal_ref, ddf_ref, dnw_ref, dst):
        @pl.when(pl.program_id(0) == 0)
        def _():
            dst[...] = jnp.zeros_like(dst)
            for r in (ddtb_ref, dal_ref, ddf_ref, dnw_ref):
                r[...] = jnp.zeros_like(r)

        states = [sin_ref[0, j] for j in range(N_PAIR)]
        _, vjp = jax.vjp(_ssd_chunk, _blocks(x_ref, 4), _blocks(b_ref, 2), _blocks(c_ref, 2), dt_ref[...],
                         _blocks(z_ref, 4), states, dtb_ref[...], al_ref[...], _blocks(df_ref, 4), _blocks(nw_ref, 4))
        dxs, dbs, dcs, ddt, dzs, dstates, ddtb, dal, ddfs, dnws = vjp(
            ([do_ref[0, :, j * LANES:(j + 1) * LANES] for j in range(N_PAIR)], [dst[j] for j in range(N_PAIR)]))
        for j in range(N_PAIR):
            sl = slice(j * LANES, (j + 1) * LANES)
            dxa_ref[:, sl] = dxs[j]
            dp_ref[:, sl] = dzs[j].astype(BF16)
            dst[j] = dstates[j]
            ddf_ref[:, sl] += ddfs[j]
            dnw_ref[:, sl] += dnws[j]
        for g in range(2):
            dxa_ref[:, BW + g * LANES:BW + (g + 1) * LANES] = dbs[g]
            dxa_ref[:, BW + 256 + g * LANES:BW + 256 + (g + 1) * LANES] = dcs[g]
        dp_ref[:, BW:BW + LANES] = ddt.astype(BF16)
        dp_ref[:, BW + LANES:] = jnp.zeros((M2_CHUNK, 2 * BW - BW - LANES), BF16)
        ddtb_ref[...] += ddtb
        dal_ref[...] += dal

    rev = lambda w, col=0: pl.BlockSpec((M2_CHUNK, w), lambda i: (nc - 1 - i, col))
    vec8 = pl.BlockSpec((1, LANES), lambda i: (0, 0))
    vec = pl.BlockSpec((1, BW), lambda i: (0, 0))
    return pl.pallas_call(
        body, name=name, grid=(nc,),
        in_specs=[rev(BW), rev(256, 2), rev(256, 3), rev(LANES, C_DT // LANES), rev(BW, C_M2Z // BW),
                  pl.BlockSpec((1, M2_CHUNK, BW), lambda i: (2, nc - 1 - i, 0)),
                  pl.BlockSpec((1, N_PAIR, M2_STATE, LANES), lambda i: (nc - 1 - i, 0, 0, 0)), vec8, vec8, vec, vec,
                  pl.BlockSpec(memory_space=pl.ANY)],
        out_specs=[rev(2 * BW, C_M2Z // (2 * BW)), rev(M2_CONV_CH), vec8, vec8, vec, vec],
        input_output_aliases={11: 0},
        out_shape=[jax.ShapeDtypeStruct((SEQ, IN_PAD), BF16), jax.ShapeDtypeStruct((SEQ, M2_CONV_CH), F32),
                   jax.ShapeDtypeStruct((1, LANES), F32), jax.ShapeDtypeStruct((1, LANES), F32),
                   jax.ShapeDtypeStruct((1, BW), F32), jax.ShapeDtypeStruct((1, BW), F32)],
        scratch_shapes=[pltpu.VMEM((N_PAIR, M2_STATE, LANES), F32)],
        compiler_params=_cparams(("arbitrary",)),
    )(xa, xa, xa, proj, proj, dout, s_in, dt_bias, a_log, dfull, nw, dproj)


def _sc_specs():
    col = lambda kind: pl.BlockSpec((SEQ, LANES), lambda j: (0, C_SC // LANES + 4 * j + kind))
    return [col(0), col(1), col(2), col(3)]


def _sc_fwd(proj, w, name):
    def body(b_ref, c_ref, h_ref, g_ref, w_ref, o_ref):
        ch = c_ref[...] * h_ref[...]
        acc = jnp.zeros_like(ch)
        for k in range(SC_CONV):
            acc = acc + w_ref[k:k + 1, :] * _shift_down(ch, SC_CONV - 1 - k)
        o_ref[...] = (b_ref[...] * acc * _silu(g_ref[...])).astype(BF16)

    return pl.pallas_call(
        body, name=name, grid=(BW // LANES,),
        in_specs=_sc_specs() + [pl.BlockSpec((SC_CONV, LANES), lambda j: (0, j))],
        out_specs=pl.BlockSpec((SEQ, LANES), lambda j: (0, j)),
        out_shape=jax.ShapeDtypeStruct((SEQ, BW), BF16),
        compiler_params=_cparams(("parallel",)),
    )(proj, proj, proj, proj, w)


def _sc_bwd(proj, dproj, dout, w, name):
    def body(b_ref, c_ref, h_ref, g_ref, do_ref, w_ref, dproj_in, dp_ref, dw_ref):
        cv, hv, gv = c_ref[...], h_ref[...], g_ref[...]
        ch = cv * hv
        chs = [_shift_down(ch, SC_CONV - 1 - k) for k in range(SC_CONV)]
        acc = jnp.zeros_like(ch)
        for k in range(SC_CONV):
            acc = acc + w_ref[k:k + 1, :] * chs[k]
        sg = jax.nn.sigmoid(gv)
        do = do_ref[0]
        bv = b_ref[...]
        dp_ref[:, 0:LANES] = (do * acc * (gv * sg)).astype(BF16)
        dp_ref[:, 3 * LANES:] = (do * bv * acc * (sg * (1.0 + gv * (1.0 - sg)))).astype(BF16)
        dacc = do * bv * (gv * sg)
        dch = jnp.zeros_like(ch)
        for k in range(SC_CONV):
            dch = dch + w_ref[k:k + 1, :] * _shift_up(dacc, SC_CONV - 1 - k)
            dw_ref[k:k + 1, :] = jnp.sum(dacc * chs[k], axis=0, keepdims=True)
        dp_ref[:, LANES:2 * LANES] = (dch * hv).astype(BF16)
        dp_ref[:, 2 * LANES:3 * LANES] = (dch * cv).astype(BF16)

    wsp = pl.BlockSpec((SC_CONV, LANES), lambda j: (0, j))
    return pl.pallas_call(
        body, name=name, grid=(BW // LANES,),
        in_specs=_sc_specs() + [pl.BlockSpec((1, SEQ, LANES), lambda j: (3, 0, j)), wsp, pl.BlockSpec(memory_space=pl.ANY)],
        out_specs=[pl.BlockSpec((SEQ, 4 * LANES), lambda j: (0, C_SC // (4 * LANES) + j)), wsp],
        input_output_aliases={6: 0},
        out_shape=[jax.ShapeDtypeStruct((SEQ, IN_PAD), BF16), jax.ShapeDtypeStruct((SC_CONV, BW), F32)],
        compiler_params=_cparams(("parallel",)),
    )(proj, proj, proj, proj, dout, w, dproj)


MERGE_T = 256
MERGE_BWD_T = 512


def _merge_fwd(proj, ys, merge_b, w_branch, name):
    def body(y_ref, lg_ref, b_ref, w_ref, o_ref):
        acc = jnp.zeros((MERGE_T, D_MODEL), F32)
        for k in range(N_BRANCH):
            gate = jax.nn.sigmoid(lg_ref[:, k * D_MODEL:(k + 1) * D_MODEL] + b_ref[k])
            acc = acc + gate * _dg(y_ref[k], w_ref[k], 1, 0)
        o_ref[...] = acc.astype(BF16)

    return pl.pallas_call(
        body, name=name, grid=(SEQ // MERGE_T,),
        in_specs=[pl.BlockSpec((N_BRANCH, MERGE_T, BW), lambda i: (0, i, 0)),
                  pl.BlockSpec((MERGE_T, N_BRANCH * D_MODEL), lambda i: (i, C_MERGE // (N_BRANCH * D_MODEL))),
                  pl.BlockSpec((N_BRANCH, 1, D_MODEL), lambda i: (0, 0, 0)),
                  pl.BlockSpec((N_BRANCH, BW, D_MODEL), lambda i: (0, 0, 0))],
        out_specs=pl.BlockSpec((MERGE_T, D_MODEL), lambda i: (i, 0)),
        out_shape=jax.ShapeDtypeStruct((SEQ, D_MODEL), BF16),
        compiler_params=_cparams(("parallel",)),
    )(ys, proj, merge_b, w_branch)


def _merge_bwd(proj, ys, dm, merge_b, w_branch, name):
    nt = SEQ // MERGE_BWD_T

    def body(y_ref, lg_ref, dm_ref, b_ref, w_ref, dy_ref, dlg_ref, dw_ref, db_ref, dw_acc):
        i = pl.program_id(1)
        gate = jax.nn.sigmoid(lg_ref[...] + b_ref[0])
        y = y_ref[0]
        dmv = dm_ref[...]
        dbo = (gate * dmv).astype(BF16)
        dlg = _dg(y, w_ref[0], 1, 0) * dmv * gate * (1.0 - gate)
        dlg_ref[...] = dlg.astype(BF16)
        dy_ref[0] = _dg(dbo, w_ref[0], 1, 1)
        dwp = _dg(y, dbo, 0, 0)
        dbp = jnp.sum(dlg, axis=0, keepdims=True)

        @pl.when(i == 0)
        def _():
            dw_acc[...] = dwp
            db_ref[0] = dbp

        @pl.when(i > 0)
        def _():
            dw_acc[...] += dwp
            db_ref[0] += dbp

        @pl.when(i == nt - 1)
        def _():
            dw_ref[0] = dw_acc[...].astype(BF16)

    return pl.pallas_call(
        body, name=name, grid=(N_BRANCH, nt),
        in_specs=[pl.BlockSpec((1, MERGE_BWD_T, BW), lambda k, i: (k, i, 0)),
                  pl.BlockSpec((MERGE_BWD_T, D_MODEL), lambda k, i: (i, C_MERGE // D_MODEL + k)),
                  pl.BlockSpec((MERGE_BWD_T, D_MODEL), lambda k, i: (i, 0)),
                  pl.BlockSpec((1, 1, D_MODEL), lambda k, i: (k, 0, 0)),
                  pl.BlockSpec((1, BW, D_MODEL), lambda k, i: (k, 0, 0))],
        out_specs=[pl.BlockSpec((1, MERGE_BWD_T, BW), lambda k, i: (k, i, 0)),
                   pl.BlockSpec((MERGE_BWD_T, D_MODEL), lambda k, i: (i, k)),
                   pl.BlockSpec((1, BW, D_MODEL), lambda k, i: (k, 0, 0)),
                   pl.BlockSpec((1, 1, D_MODEL), lambda k, i: (k, 0, 0))],
        out_shape=[jax.ShapeDtypeStruct((N_BRANCH, SEQ, BW), F32), jax.ShapeDtypeStruct((SEQ, IN_PAD), BF16),
                   jax.ShapeDtypeStruct((N_BRANCH, BW, D_MODEL), BF16), jax.ShapeDtypeStruct((N_BRANCH, 1, D_MODEL), F32)],
        scratch_shapes=[pltpu.VMEM((BW, D_MODEL), F32)],
        compiler_params=_cparams(("parallel", "arbitrary")),
    )(ys, proj, dm, merge_b, w_branch)


def _adamw(glist, w, m, v, rows, name):
    nl = len(glist)
    n, r, c = glist[0].shape
    assert w.shape == (nl, r, c) and r % rows == 0
    nb = r // rows

    def body(*refs):
        g_refs = refs[:nl]
        w_ref, m_ref, v_ref, go_ref, d_ref, mo_ref, vo_ref = refs[nl:]
        for layer in range(nl):
            @pl.when(pl.program_id(0) == layer)
            def _(g_ref=g_refs[layer]):
                g = g_ref[0].astype(F32)
                for s in range(1, n):
                    g = g + g_ref[s].astype(F32)
                mn = ADAM_B1 * m_ref[0] + (1.0 - ADAM_B1) * g
                vn = ADAM_B2 * v_ref[0] + (1.0 - ADAM_B2) * jnp.square(g)
                m_hat = mn / (1.0 - ADAM_B1 ** ADAM_STEP)
                v_hat = vn / (1.0 - ADAM_B2 ** ADAM_STEP)
                go_ref[0] = g
                d_ref[0] = -ADAM_LR * (m_hat / (jnp.sqrt(v_hat) + ADAM_EPS) + ADAM_WD * w_ref[0])
                mo_ref[0] = mn
                vo_ref[0] = vn

    def g_spec(layer):
        return pl.BlockSpec((n, rows, c), lambda a, i: (0, jnp.where(a < layer, 0, jnp.where(a == layer, i, nb - 1)), 0))

    blk = pl.BlockSpec((1, rows, c), lambda a, i: (a, i, 0))
    out = jax.ShapeDtypeStruct((nl, r, c), F32)
    return pl.pallas_call(
        body, name=name, grid=(nl, nb),
        in_specs=[g_spec(layer) for layer in range(nl)] + [blk, blk, blk],
        out_specs=[blk, blk, blk, blk], out_shape=[out, out, out, out],
        compiler_params=_cparams(("arbitrary", "arbitrary")),
    )(*glist, w, m, v)


X_ROWS_PER_COL = 2 * (D_MODEL // LANES)


def _w_in_to_x(w):
    t = jnp.transpose(w, (2, 0, 1)).reshape(SHARD_IN, DEPTH, D_MODEL // LANES, LANES)
    return jnp.transpose(t, (0, 2, 1, 3)).reshape(SHARD_IN * X_ROWS_PER_COL, LANES)


def _w_in_from_x(xv):
    t = jnp.transpose(xv.reshape(SHARD_IN, D_MODEL // LANES, DEPTH, LANES), (0, 2, 1, 3))
    return jnp.transpose(t.reshape(SHARD_IN, DEPTH, D_MODEL), (1, 2, 0))


def _adamw_w_in(glist, w, m, v, name):
    n = glist[0].shape[0]
    cols = LANES
    rows = cols * X_ROWS_PER_COL

    def body(g0_ref, g1_ref, w_ref, m_ref, v_ref, go_ref, d_ref, mo_ref, vo_ref):
        for layer, g_ref in enumerate((g0_ref, g1_ref)):
            g = g_ref[0].astype(F32)
            for s in range(1, n):
                g = g + g_ref[s].astype(F32)
            gt = g.T
            for t in range(D_MODEL // LANES):
                sel = (pl.ds(2 * t + layer, cols, stride=X_ROWS_PER_COL), slice(None))
                gs = gt[:, t * LANES:(t + 1) * LANES]
                mn = ADAM_B1 * m_ref[sel] + (1.0 - ADAM_B1) * gs
                vn = ADAM_B2 * v_ref[sel] + (1.0 - ADAM_B2) * jnp.square(gs)
                m_hat = mn / (1.0 - ADAM_B1 ** ADAM_STEP)
                v_hat = vn / (1.0 - ADAM_B2 ** ADAM_STEP)
                go_ref[sel] = gs
                d_ref[sel] = -ADAM_LR * (m_hat / (jnp.sqrt(v_hat) + ADAM_EPS) + ADAM_WD * w_ref[sel])
                mo_ref[sel] = mn
                vo_ref[sel] = vn

    g_spec = pl.BlockSpec((n, D_MODEL, cols), lambda i: (0, 0, i))
    blk = pl.BlockSpec((rows, LANES), lambda i: (i, 0))
    out = jax.ShapeDtypeStruct((SHARD_IN * X_ROWS_PER_COL, LANES), F32)
    res = pl.pallas_call(
        body, name=name, grid=(-(-SHARD_IN // cols),),
        in_specs=[g_spec, g_spec, blk, blk, blk], out_specs=[blk, blk, blk, blk], out_shape=[out, out, out, out],
        compiler_params=_cparams(("parallel",)),
    )(*glist, _w_in_to_x(w), _w_in_to_x(m), _w_in_to_x(v))
    return [_w_in_from_x(o) for o in res]


def _adamw_many(gs, ws, ms, vs, name):
    k = len(gs)

    def body(*refs):
        g_refs, w_refs, m_refs, v_refs = refs[:k], refs[k:2 * k], refs[2 * k:3 * k], refs[3 * k:4 * k]
        d_refs, mo_refs, vo_refs = refs[4 * k:5 * k], refs[5 * k:6 * k], refs[6 * k:7 * k]
        for i in range(k):
            g = g_refs[i][...]
            mn = ADAM_B1 * m_refs[i][...] + (1.0 - ADAM_B1) * g
            vn = ADAM_B2 * v_refs[i][...] + (1.0 - ADAM_B2) * jnp.square(g)
            m_hat = mn / (1.0 - ADAM_B1 ** ADAM_STEP)
            v_hat = vn / (1.0 - ADAM_B2 ** ADAM_STEP)
            d_refs[i][...] = -ADAM_LR * (m_hat / (jnp.sqrt(v_hat) + ADAM_EPS) + ADAM_WD * w_refs[i][...])
            mo_refs[i][...] = mn
            vo_refs[i][...] = vn

    whole = pl.BlockSpec(memory_space=pltpu.VMEM)
    shapes = [jax.ShapeDtypeStruct(w.shape, F32) for w in ws]
    outs = pl.pallas_call(
        body, name=name, in_specs=[whole] * (4 * k), out_specs=[whole] * (3 * k), out_shape=shapes * 3,
        compiler_params=_cparams(None),
    )(*gs, *ws, *ms, *vs)
    return outs[:k], outs[k:2 * k], outs[2 * k:]


MEMORY_ORDER = {'s5_b_re': (0, 1, 3, 2), 's5_b_im': (0, 1, 3, 2), 's5_d': (0, 2, 1), 'sc_conv_w': (1, 0, 2)}


def _memory_view(name, t):
    return jnp.transpose(t, MEMORY_ORDER[name]) if name in MEMORY_ORDER else t


def _slot_sum(gslots, name):
    n, r, c = gslots.shape

    def body(g_ref, o_ref):
        g = g_ref[0]
        for s in range(1, n):
            g = g + g_ref[s]
        o_ref[...] = g

    return pl.pallas_call(
        body, name=name, in_specs=[pl.BlockSpec((n, r, c), lambda: (0, 0, 0))],
        out_specs=pl.BlockSpec((r, c), lambda: (0, 0)), out_shape=jax.ShapeDtypeStruct((r, c), F32),
        compiler_params=_cparams(None),
    )(gslots)


def _me_and_peers():
    x, y, c = lax.axis_index("x"), lax.axis_index("y"), lax.axis_index("c")
    me = 4 * x + 2 * y + c
    peers = []
    for k in range(1, N_DEV):
        px = 1 - x if (k >> 2) & 1 else x
        py = 1 - y if (k >> 1) & 1 else y
        pc = 1 - c if k & 1 else c
        peers.append((4 * px + 2 * py + pc, (px, py, pc)))
    return me, peers


def _exchange(tensors, gather, name):
    n = len(tensors)

    def body(*refs):
        ins, outs = refs[:n], refs[n:2 * n]
        send_sems, recv_sems, local_sems = refs[2 * n:]
        me, peers = _me_and_peers()
        started = []
        for t in range(n):
            own = pltpu.make_async_copy(ins[t] if gather else ins[t].at[me], outs[t].at[me], local_sems.at[t])
            own.start()
            started.append(own)
            for k, (pidx, pos) in enumerate(peers):
                cp = pltpu.make_async_remote_copy(
                    src_ref=ins[t] if gather else ins[t].at[pidx], dst_ref=outs[t].at[me],
                    send_sem=send_sems.at[t, k], recv_sem=recv_sems.at[t, k], device_id=pos, device_id_type=MESH)
                cp.start()
                started.append(cp)
        for cp in started:
            cp.wait()

    any_spec = pl.BlockSpec(memory_space=pl.ANY)
    outs = pl.pallas_call(
        body, name=name, in_specs=[any_spec] * n, out_specs=[any_spec] * n,
        out_shape=[jax.ShapeDtypeStruct(((N_DEV,) + t.shape) if gather else t.shape, t.dtype) for t in tensors],
        scratch_shapes=[pltpu.SemaphoreType.DMA((n, N_DEV - 1)), pltpu.SemaphoreType.DMA((n, N_DEV - 1)),
                        pltpu.SemaphoreType.DMA((n,))],
        compiler_params=pltpu.CompilerParams(has_side_effects=True),
    )(*tensors)
    return list(outs)


_HBM = pl.BlockSpec(memory_space=pltpu.HBM)
_SEM = pl.BlockSpec(memory_space=pltpu.SEMAPHORE)
_EFFECT = pltpu.SideEffectType.DATAFLOW_SIDE_EFFECTING


N_CHIP = N_DEV // 2


def _chip_peers():
    x, y, c = lax.axis_index("x"), lax.axis_index("y"), lax.axis_index("c")
    chips = []
    for d in range(1, N_CHIP):
        px = 1 - x if (d >> 1) & 1 else x
        py = 1 - y if d & 1 else y
        chips.append((2 * px + py, (px, py)))
    return (x, y, c), 2 * x + y, chips


def _plan_gather(ins, lands, send_sems, recv_sems, local_sems, first=0):
    (x, y, c), q, chips = _chip_peers()
    me = 2 * q + c
    plan = dict(start=[], relay_wait=[], relay_start=[], local=[], sends=[], recvs=[])
    for t in range(len(ins)):
        base = (first + t) * 7
        sem = lambda k: dict(send_sem=send_sems.at[base + k], recv_sem=recv_sems.at[base + k], device_id_type=MESH)
        own = pltpu.make_async_copy(ins[t], lands[t].at[me], local_sems.at[first + t])
        to_sib = pltpu.make_async_remote_copy(src_ref=ins[t], dst_ref=lands[t].at[me], device_id=(x, y, 1 - c), **sem(0))
        plan['start'] += [own, to_sib]
        plan['local'].append(own)
        plan['sends'].append(to_sib)
        plan['recvs'].append(to_sib)
        for d, (pq, (px, py)) in enumerate(chips):
            to_chip = pltpu.make_async_remote_copy(src_ref=ins[t], dst_ref=lands[t].at[me], device_id=(px, py, c), **sem(1 + d))
            blk = lands[t].at[2 * pq + c]
            fwd = pltpu.make_async_remote_copy(src_ref=blk, dst_ref=blk, device_id=(x, y, 1 - c), **sem(4 + d))
            plan['start'].append(to_chip)
            plan['relay_wait'].append(to_chip)
            plan['relay_start'].append(fwd)
            plan['sends'] += [to_chip, fwd]
            plan['recvs'].append(fwd)
    return plan


def _plan_pair(ins, lands, send_sems, recv_sems, local_sems):
    (x, y, c), q, chips = _chip_peers()
    plan = dict(start=[], local=[], sends=[], recvs=[])
    for t in range(len(ins)):
        for k in range(N_CHIP):
            cp = pltpu.make_async_remote_copy(
                src_ref=ins[t].at[2 * k + 1 - c], dst_ref=lands[t].at[k], send_sem=send_sems.at[t * N_CHIP + k],
                recv_sem=recv_sems.at[t * N_CHIP + k], device_id=(x, y, 1 - c), device_id_type=MESH)
            plan['start'].append(cp)
            plan['sends'].append(cp)
            plan['recvs'].append(cp)
    return plan


def _plan_chips(ins, lands, send_sems, recv_sems, local_sems):
    (x, y, c), q, chips = _chip_peers()
    plan = dict(start=[], local=[], sends=[], recvs=[])
    for t in range(len(ins)):
        own = pltpu.make_async_copy(ins[t].at[q], lands[t].at[q], local_sems.at[t])
        plan['start'].append(own)
        plan['local'].append(own)
        for d, (pq, (px, py)) in enumerate(chips):
            cp = pltpu.make_async_remote_copy(
                src_ref=ins[t].at[pq], dst_ref=lands[t].at[q], send_sem=send_sems.at[t * 3 + d],
                recv_sem=recv_sems.at[t * 3 + d], device_id=(px, py, c), device_id_type=MESH)
            plan['start'].append(cp)
            plan['sends'].append(cp)
            plan['recvs'].append(cp)
    return plan


def _split_start(plan_fn, tensors, land_shapes, n_sems, name, after=None):
    n = len(tensors)
    extra = [] if after is None else [after]

    def body(*refs):
        ins, lands = refs[:n], refs[n:2 * n]
        plan = plan_fn(ins, lands, *refs[2 * n + len(extra):2 * n + len(extra) + 3])
        for cp in plan['start']:
            cp.start()
        refs[-1][...] = jnp.zeros_like(refs[-1])

    outs = pl.pallas_call(
        body, name=name,
        out_shape=(pltpu.SemaphoreType.DMA((n_sems,)), pltpu.SemaphoreType.DMA((n_sems,)), pltpu.SemaphoreType.DMA((n,)),
                   *[pltpu.HBM(t.shape, t.dtype) for t in tensors],
                   *[pltpu.HBM(s, t.dtype) for s, t in zip(land_shapes, tensors)],
                   jax.ShapeDtypeStruct((8, LANES), F32)),
        in_specs=[_HBM] * (2 * n) + [pl.BlockSpec(memory_space=pl.ANY)] * len(extra),
        out_specs=(_SEM, _SEM, _SEM, *[_HBM] * (2 * n), pl.BlockSpec(memory_space=pltpu.VMEM)),
        input_output_aliases={t: 3 + t for t in range(2 * n)},
        compiler_params=pltpu.CompilerParams(has_side_effects=_EFFECT),
    )(*[pltpu.with_memory_space_constraint(t, pltpu.HBM) for t in tensors],
      *[pltpu.with_memory_space_constraint(lax.empty(s, t.dtype), pltpu.HBM) for s, t in zip(land_shapes, tensors)], *extra)
    return outs[:-1], outs[-1]


def _split_relay(plan_fn, state, after, name):
    sems, thru = state[:3], state[3:]
    n = len(thru) // 2

    def arrived(*refs):
        plan = plan_fn(refs[:n], refs[n:2 * n], *refs[2 * n:2 * n + 3])
        for cp in plan['relay_wait']:
            cp.wait_recv()

    thru = pl.pallas_call(
        arrived, name=name + "_arrived",
        out_shape=tuple(pltpu.HBM(t.shape, t.dtype) for t in thru),
        in_specs=[_HBM] * (2 * n) + [_SEM, _SEM, _SEM, pl.BlockSpec(memory_space=pl.ANY)],
        out_specs=tuple([_HBM] * (2 * n)),
        input_output_aliases={t: t for t in range(2 * n)},
        compiler_params=pltpu.CompilerParams(has_side_effects=_EFFECT),
    )(*thru, *sems, after)

    def forward(*refs):
        plan = plan_fn(refs[:n], refs[n:2 * n], *refs[2 * n:2 * n + 3])
        for cp in plan['relay_start']:
            cp.start()
        refs[-1][...] = jnp.zeros_like(refs[-1])

    outs = pl.pallas_call(
        forward, name=name + "_forward",
        out_shape=(*[pltpu.HBM(t.shape, t.dtype) for t in thru], jax.ShapeDtypeStruct((8, LANES), F32)),
        in_specs=[_HBM] * (2 * n) + [_SEM, _SEM, _SEM],
        out_specs=(*[_HBM] * (2 * n), pl.BlockSpec(memory_space=pltpu.VMEM)),
        input_output_aliases={t: t for t in range(2 * n)},
        compiler_params=pltpu.CompilerParams(has_side_effects=_EFFECT),
    )(*thru, *sems)
    return (*sems, *outs[:-1]), outs[-1]


def _split_wait(plan_fn, state, after, name, with_sources=False):
    sems, thru = state[:3], state[3:]
    n = len(thru) // 2

    def body(*refs):
        plan = plan_fn(refs[:n], refs[n:2 * n], *refs[2 * n:2 * n + 3])
        for cp in plan['local']:
            cp.wait()
        for cp in plan['sends']:
            cp.wait_send()
        for cp in plan['recvs']:
            cp.wait_recv()

    outs = pl.pallas_call(
        body, name=name,
        out_shape=tuple(pltpu.HBM(t.shape, t.dtype) for t in thru),
        in_specs=[_HBM] * (2 * n) + [_SEM, _SEM, _SEM, pl.BlockSpec(memory_space=pl.ANY)],
        out_specs=tuple([_HBM] * (2 * n)),
        input_output_aliases={t: t for t in range(2 * n)},
        compiler_params=pltpu.CompilerParams(has_side_effects=_EFFECT),
    )(*thru, *sems, after)
    return (list(outs[:n]), list(outs[n:])) if with_sources else list(outs[n:])


PAIR_SUM_BLOCK = 768 * 1024


def _pair_sum(mine, theirs, name):
    _, r, c = mine.shape
    rows = r
    while rows * c > PAIR_SUM_BLOCK and rows % 32 == 0:
        rows //= 2

    def body(core_ref, a_ref, b_ref, o_ref):
        o_ref[0] = (a_ref[0].astype(F32) + b_ref[0].astype(F32)).astype(o_ref.dtype)

    return pl.pallas_call(
        body, name=name,
        grid_spec=pltpu.PrefetchScalarGridSpec(
            num_scalar_prefetch=1, grid=(N_CHIP, r // rows),
            in_specs=[pl.BlockSpec((1, rows, c), lambda k, i, core: (2 * k + core[0], i, 0)),
                      pl.BlockSpec((1, rows, c), lambda k, i, core: (k, i, 0))],
            out_specs=pl.BlockSpec((1, rows, c), lambda k, i, core: (k, i, 0))),
        out_shape=jax.ShapeDtypeStruct((N_CHIP, r, c), mine.dtype),
        compiler_params=_cparams(("parallel", "parallel")),
    )(lax.axis_index("c").astype(jnp.int32).reshape(1), mine, theirs)


WEIGHTS = ['norm_w', 'w_in', 's5_lambda_re', 's5_lambda_im', 's5_b_re', 's5_b_im', 's5_c_re', 's5_c_im', 's5_d',
           's5_log_step', 's5_w_glu', 'sgu_ln_w', 'sgu_ln_b', 'sgu_w', 'sgu_b', 'm2_conv_w', 'm2_conv_b', 'm2_dt_bias',
           'm2_a_log', 'm2_d', 'm2_norm_w', 'sc_conv_w', 'merge_b', 'w_branch', 'w_out', 'final_norm_w']
BIG_SHARDED = ['w_in', 'w_branch', 'w_out', 's5_w_glu']
SMALL_SHARDED = ['m2_conv_w', 'sc_conv_w', 'merge_b']
REPLICATED = [n for n in WEIGHTS if n not in BIG_SHARDED + SMALL_SHARDED]
S5_NAMES = ['s5_lambda_re', 's5_lambda_im', 's5_b_re', 's5_b_im', 's5_c_re', 's5_c_im', 's5_d', 's5_log_step']


def _sc_interleave(t):
    lead = t.shape[:-1]
    return jnp.swapaxes(t.reshape(lead + (4, 4, LANES)), -3, -2).reshape(lead + (4 * BW,))


def _pad_in(w):
    z = lambda n: jnp.zeros(w.shape[:-1] + (n,), w.dtype)
    return jnp.concatenate([w[..., 6152:], w[..., 0:1024], w[..., 3072:4096], w[..., 1024:2560], z(512),
                            w[..., 2560:3072], w[..., 4096:4104], z(504), _sc_interleave(w[..., 4104:6152])], axis=-1)


def _unpad_in(g):
    return jnp.concatenate([g[..., C_S5U:C_S5U + 1024], g[..., C_SGU_U:C_SGU_U + 1536], g[..., C_M2Z:C_M2Z + 512],
                            g[..., C_M2X:C_M2X + 1024], g[..., C_DT:C_DT + 8], _sc_interleave(g[..., C_SC:]),
                            g[..., :N_BRANCH * D_MODEL]], axis=-1)


ROW_BLOCK = 8 * LANES


def _pack_rows(tensors, row_mult, batched=False):
    parts = []
    for t in tensors:
        f = t.reshape((t.shape[0], -1) if batched else (1, -1))
        f = jnp.pad(f, ((0, 0), (0, (-f.shape[1]) % ROW_BLOCK)))
        parts.append(f.reshape(f.shape[0], -1, LANES))
    out = jnp.concatenate(parts, axis=1)
    out = jnp.pad(out, ((0, 0), (0, (-out.shape[1]) % row_mult), (0, 0)))
    return out if batched else out[0]


def _unpack_rows(rows, shapes):
    out, r0 = [], 0
    for shp in shapes:
        size = 1
        for s in shp:
            size *= s
        nr = -(-size // ROW_BLOCK) * 8
        out.append(rows[r0:r0 + nr].reshape(-1)[:size].reshape(shp))
        r0 += nr
    return out


def _kernel_col_map():
    m = np.full(IN_PAD, -1, np.int64)
    m[C_MERGE:C_MERGE + 4096] = np.arange(6152, 10248)
    m[C_S5U:C_S5U + 1024] = np.arange(0, 1024)
    m[C_M2X:C_M2X + 1024] = np.arange(3072, 4096)
    m[C_SGU_U:C_SGU_U + 1536] = np.arange(1024, 2560)
    m[C_M2Z:C_M2Z + 512] = np.arange(2560, 3072)
    m[C_DT:C_DT + 8] = np.arange(4096, 4104)
    for j in range(4):
        for kind in range(4):
            k0 = C_SC + 4 * LANES * j + LANES * kind
            m[k0:k0 + LANES] = 4104 + BW * kind + LANES * j + np.arange(LANES)
    return m


def _lane_pieces(sources):
    pieces, cur = [], None
    for lane, src in enumerate(sources):
        key = None if src is None else (src[0], src[1] // LANES, (lane - src[1]) % LANES)
        if cur is not None and key == cur[0]:
            cur[2] = lane + 1
        else:
            if cur is not None and cur[0] is not None:
                pieces.append((*cur[0], cur[1], cur[2]))
            cur = [key, lane, lane + 1]
    if cur is not None and cur[0] is not None:
        pieces.append((*cur[0], cur[1], cur[2]))
    return pieces


def _assemble_block(pieces, load, rows, dtype):
    lane = lax.broadcasted_iota(jnp.int32, (rows, LANES), 1)
    out = None
    for arr, sb, shift, lo, hi in pieces:
        v = load(arr, sb)
        if shift:
            v = pltpu.roll(v, shift, 1)
        if out is None and lo == 0 and hi == LANES:
            out = v
        else:
            out = jnp.where((lane >= lo) & (lane < hi), v, jnp.zeros((rows, LANES), dtype) if out is None else out)
    return jnp.zeros((rows, LANES), dtype) if out is None else out


RELAYOUT_ROWS = 256
SHARD_BLOCKS = -(-SHARD_IN // LANES)


def _load_shard_block(ref, rows):
    def load(j, sb):
        if sb == SHARD_BLOCKS - 1:
            return jnp.broadcast_to(ref[j, :, SHARD_IN - 1:SHARD_IN], (rows, LANES))
        return ref[j, :, sb * LANES:(sb + 1) * LANES]
    return load


def _relayout_w_in(gathered, name):
    kmap = _kernel_col_map()
    dtype = gathered.dtype

    def body(src_ref, o_ref):
        load = _load_shard_block(src_ref, RELAYOUT_ROWS)
        for ob in range(IN_PAD // LANES):
            srcs = [None if kmap[ob * LANES + l] < 0 else (int(kmap[ob * LANES + l]) // SHARD_IN, int(kmap[ob * LANES + l]) % SHARD_IN)
                    for l in range(LANES)]
            o_ref[:, ob * LANES:(ob + 1) * LANES] = _assemble_block(_lane_pieces(srcs), load, RELAYOUT_ROWS, dtype)

    return pl.pallas_call(
        body, name=name, grid=(D_MODEL // RELAYOUT_ROWS,),
        in_specs=[pl.BlockSpec((N_DEV, RELAYOUT_ROWS, SHARD_IN), lambda i: (0, i, 0))],
        out_specs=pl.BlockSpec((RELAYOUT_ROWS, IN_PAD), lambda i: (i, 0)),
        out_shape=jax.ShapeDtypeStruct((D_MODEL, IN_PAD), dtype),
        compiler_params=_cparams(("parallel",)),
    )(gathered)


def _relayout_g_in(gw, name):
    kmap = _kernel_col_map()
    kinv = np.zeros(IN_DIM, np.int64)
    kinv[kmap[kmap >= 0]] = np.nonzero(kmap >= 0)[0]
    dtype = gw.dtype

    def body(src_ref, o_ref):
        load = lambda _, sb: src_ref[:, sb * LANES:(sb + 1) * LANES]
        for j in range(N_DEV):
            for ob in range(SHARD_BLOCKS):
                srcs = [(0, int(kinv[SHARD_IN * j + ob * LANES + l])) if ob * LANES + l < SHARD_IN else None for l in range(LANES)]
                blk = _assemble_block(_lane_pieces(srcs), load, RELAYOUT_ROWS, dtype)
                if ob == SHARD_BLOCKS - 1:
                    o_ref[j, :, SHARD_IN - 1:SHARD_IN] = blk[:, 0:1]
                else:
                    o_ref[j, :, ob * LANES:(ob + 1) * LANES] = blk

    return pl.pallas_call(
        body, name=name, grid=(D_MODEL // RELAYOUT_ROWS,),
        in_specs=[pl.BlockSpec((RELAYOUT_ROWS, IN_PAD), lambda i: (i, 0))],
        out_specs=pl.BlockSpec((N_DEV, RELAYOUT_ROWS, SHARD_IN), lambda i: (0, i, 0)),
        out_shape=jax.ShapeDtypeStruct((N_DEV, D_MODEL, SHARD_IN), dtype),
        compiler_params=_cparams(("parallel",)),
    )(gw)


def _rows128(flat, row_mult=8):
    n = flat.shape[0]
    per = LANES * row_mult
    total = -(-n // per) * per
    return jnp.pad(flat, (0, total - n)).reshape(total // LANES, LANES)


def _pad_lanes(v):
    return jnp.pad(v, (0, LANES - v.shape[0])).reshape(1, LANES)


def _layer_prep(i, p):
    disc, disc_vjp = jax.vjp(_s5_disc, *[p[n][i] for n in S5_NAMES])
    prep = dict(
        nw=p['norm_w'][i].reshape(1, D_MODEL), disc_vjp=disc_vjp,
        s5small=[_block_diag(t).astype(BF16) for t in disc[:4]] + [disc[4], disc[5]],
        sgw=[p['sgu_ln_w'][i].reshape(1, BW), p['sgu_ln_b'][i].reshape(1, BW), p['sgu_w'][i],
             jnp.repeat(p['sgu_b'][i].T, BW // SGU_HEADS, axis=1)],
        cb=p['m2_conv_b'][i].reshape(1, M2_CONV_CH),
        m2w=[_pad_lanes(p['m2_dt_bias'][i]), _pad_lanes(p['m2_a_log'][i]),
             jnp.repeat(p['m2_d'][i], M2_HEAD_DIM).reshape(1, BW), p['m2_norm_w'][i].reshape(1, BW)])
    touch = [t[0, 0].astype(F32) for t in prep['s5small']] + [prep['sgw'][3][0, 0], prep['m2w'][2][0, 0]]
    return prep, sum(touch[1:], touch[0])


def _layer_fwd(x, h, i, prep, w_in, other_weights, before_merge=None):
    proj = _matmul(h, w_in, 1, 0, F32, 1024, 1024, 1024, f"proj{i}")
    full = dict(other_weights(proj), w_in=w_in)
    s5w = prep['s5small'] + [full['s5_w_glu']]
    ya, sre, sim = _s5_fwd(proj, *s5w, f"s5_fwd{i}")
    yb = _sgu_fwd(proj, *prep['sgw'], f"sgu_fwd{i}")
    cw = full['m2_conv_w']
    xa = _m2_conv_fwd(proj, cw, prep['cb'], f"m2conv_fwd{i}")
    yc, s_in = _ssd_fwd(proj, xa, *prep['m2w'], f"ssd_fwd{i}")
    scw = full['sc_conv_w']
    yd = _sc_fwd(proj, scw, f"sc_fwd{i}")
    ys = jnp.stack([ya, yb, yc, yd])
    mb = full['merge_b'].reshape(N_BRANCH, 1, D_MODEL)
    if before_merge is not None:
        mb = mb + before_merge(ys)[0, 0]
    merged = _merge_fwd(proj, ys, mb, full['w_branch'], f"merge_fwd{i}")
    x_new = _matmul(merged, full['w_out'], 1, 0, F32, 1024, 1024, 1024, f"out{i}", residual=x)
    saved = dict(x=x, nw=prep['nw'], h=h, proj=proj, disc_vjp=prep['disc_vjp'], s5w=s5w, sre=sre, sim=sim, sgw=prep['sgw'],
                 cw=cw, cb=prep['cb'], xa=xa, m2w=prep['m2w'], s_in=s_in, scw=scw, ys=ys, mb=mb, merged=merged)
    return x_new, saved, full


def _layer_bwd(dx_out, i, sv, full, on_other_grads=None, after_sgu=None, on_large_grads=None, after_dh=None):
    g = {}
    proj = sv['proj']
    dm = _matmul(dx_out, full['w_out'], 1, 1, F32, 1024, 1024, 1024, f"dmerged{i}")
    g['w_out'] = _matmul(sv['merged'], dx_out, 0, 0, BF16, 1024, 1024, 1024, f"gw_out{i}")
    dys, dproj, g['w_branch'], dmb = _merge_bwd(proj, sv['ys'], dm, sv['mb'], full['w_branch'], f"merge_bwd{i}")
    g['merge_b'] = dmb.reshape(N_BRANCH, D_MODEL)
    dproj, dbbre, dbbim, dcre, dcim, da, dd, dwg = _s5_bwd(proj, dproj, dys, sv['sre'], sv['sim'], *sv['s5w'], f"s5_bwd{i}")
    g['s5_dense'] = (dbbre, dbbim, dcre, dcim, da, dd)
    g['s5_w_glu'] = dwg.astype(BF16)
    sgw, m2w = list(sv['sgw']), list(sv['m2w'])
    if on_other_grads is not None:
        sgw[0] = sgw[0] + on_other_grads(g)[0, 0]
    dproj, dlw, dlb, g['sgu_w'], dbias = _sgu_bwd(proj, dproj, dys, *sgw, f"sgu_bwd{i}")
    g['sgu_ln_w'], g['sgu_ln_b'] = dlw[0], dlb[0]
    g['sgu_b'] = dbias.reshape(SGU_CHUNK, SGU_HEADS, BW // SGU_HEADS).sum(-1).T
    if after_sgu is not None:
        m2w[0] = m2w[0] + after_sgu(dlw)[0, 0]
    dproj, dxa, ddtb, dal, ddf, dnw = _ssd_bwd(proj, dproj, sv['xa'], dys, sv['s_in'], *m2w, f"ssd_bwd{i}")
    dproj, g['m2_conv_w'], dcb = _m2_conv_bwd(proj, dproj, dxa, sv['cw'], sv['cb'], f"m2conv_bwd{i}")
    g['m2_conv_b'], g['m2_norm_w'] = dcb[0], dnw[0]
    g['m2_dt_bias'], g['m2_a_log'] = ddtb[0, :M2_HEADS], dal[0, :M2_HEADS]
    g['m2_d'] = ddf.reshape(M2_HEADS, M2_HEAD_DIM).sum(-1)
    dproj, g['sc_conv_w'] = _sc_bwd(proj, dproj, dys, sv['scw'], f"sc_bwd{i}")
    g['w_in'] = _matmul(sv['h'], dproj, 0, 0, BF16, 1024, 1024, 1024, f"gw_in{i}")
    tok = on_large_grads(g) if on_large_grads else None
    dh = _matmul(dproj, full['w_in'], 1, 1, F32, 1024, 1024, 1024, f"dh{i}", after=tok)
    nw = sv['nw'] if after_dh is None else sv['nw'] + after_dh(dh)[0, 0]
    dx_in, dnw_l = _rmsnorm_bwd(sv['x'], nw, dh, dx_out, f"rms_bwd{i}")
    g['norm_w'] = dnw_l[0]
    return dx_in, g


def _split8(t, axis):
    shp = t.shape
    t = t.reshape(shp[:axis] + (N_DEV, shp[axis] // N_DEV) + shp[axis + 1:])
    return jnp.moveaxis(t, axis, 0)


def _join8(t, axis):
    t = jnp.moveaxis(t, 0, axis)
    shp = t.shape
    return t.reshape(shp[:axis] + (shp[axis] * shp[axis + 1],) + shp[axis + 2:])


SHARD_AXIS = {'w_in': 2, 'w_branch': 3, 'w_out': 1, 's5_w_glu': 1, 'm2_conv_w': 2, 'sc_conv_w': 2, 'merge_b': 2}


OTHER_BIG = [n for n in BIG_SHARDED if n != 'w_in']


def _other_weights(gathered):
    return {n: _join8(t, SHARD_AXIS[n] - 1) for n, t in zip(OTHER_BIG, gathered)}


def _other_grad_blocks(g):
    blocks = [_split8(g[n], SHARD_AXIS[n] - 1) for n in OTHER_BIG]
    return [b.reshape(N_DEV, -1, b.shape[-1]) for b in blocks]


def _pair_start(blocks, tag):
    shapes = [(N_CHIP,) + b.shape[1:] for b in blocks]
    return _split_start(_plan_pair, blocks, shapes, N_CHIP * len(blocks), f"pair{tag}_start")


def _pair_sums(state, after, tag):
    mine, theirs = _split_wait(_plan_pair, state, after, f"pair{tag}_wait", with_sources=True)
    return [_pair_sum(b, t, f"pair_sum{tag}_{k}") for k, (b, t) in enumerate(zip(mine, theirs))]


def _chips_start(sums, tag, after=None):
    return _split_start(_plan_chips, sums, [s.shape for s in sums], 3 * len(sums), f"chips{tag}_start", after)


def kernel(x, norm_w, w_in, s5_lambda_re, s5_lambda_im, s5_b_re, s5_b_im, s5_c_re, s5_c_im, s5_d, s5_log_step, s5_w_glu, sgu_ln_w, sgu_ln_b, sgu_w, sgu_b, m2_conv_w, m2_conv_b, m2_dt_bias, m2_a_log, m2_d, m2_norm_w, sc_conv_w, merge_b, w_branch, w_out, final_norm_w, loss_target, m_norm_w, m_w_in, m_s5_lambda_re, m_s5_lambda_im, m_s5_b_re, m_s5_b_im, m_s5_c_re, m_s5_c_im, m_s5_d, m_s5_log_step, m_s5_w_glu, m_sgu_ln_w, m_sgu_ln_b, m_sgu_w, m_sgu_b, m_m2_conv_w, m_m2_conv_b, m_m2_dt_bias, m_m2_a_log, m_m2_d, m_m2_norm_w, m_sc_conv_w, m_merge_b, m_w_branch, m_w_out, m_final_norm_w, v_norm_w, v_w_in, v_s5_lambda_re, v_s5_lambda_im, v_s5_b_re, v_s5_b_im, v_s5_c_re, v_s5_c_im, v_s5_d, v_s5_log_step, v_s5_w_glu, v_sgu_ln_w, v_sgu_ln_b, v_sgu_w, v_sgu_b, v_m2_conv_w, v_m2_conv_b, v_m2_dt_bias, v_m2_a_log, v_m2_d, v_m2_norm_w, v_sc_conv_w, v_merge_b, v_w_branch, v_w_out, v_final_norm_w):
    loc = locals()
    p = {n: loc[n] for n in WEIGHTS}
    mom = {n: loc['m_' + n] for n in WEIGHTS}
    vel = {n: loc['v_' + n] for n in WEIGHTS}

    small_sizes = [p[n].size for n in SMALL_SHARDED]
    small_pack = _rows128(jnp.concatenate([p[n].reshape(-1) for n in SMALL_SHARDED]))
    shards = ([p['w_in'][0].astype(BF16)] + [p[n][0].astype(BF16) for n in OTHER_BIG] + [small_pack]
              + [p[n][1].astype(BF16) for n in BIG_SHARDED])
    gath, tok = _split_start(_plan_gather, shards, [(N_DEV,) + t.shape for t in shards], 7 * len(shards), "gather_start")
    sems, srcs, lands = gath[:3], gath[3:3 + len(shards)], gath[3 + len(shards):]

    def relayed(lo, hi, after, name):
        plan = functools.partial(_plan_gather, first=lo)
        state, tok = _split_relay(plan, (*sems, *srcs[lo:hi], *lands[lo:hi]), after, name + "_relay")
        return (plan, state, name), tok

    def arrived(relay, after):
        plan, state, name = relay
        return _split_wait(plan, state, after, name + "_wait")

    def gathered(lo, hi, after, name):
        relay, tok = relayed(lo, hi, after, name)
        return arrived(relay, tok)

    later = dict(p, **{n: p[n] + tok[0, 0] for n in ('norm_w', 's5_log_step', 'sgu_b', 'm2_d')})
    preps = [_layer_prep(i, later) for i in range(DEPTH)]
    h0 = _rmsnorm_fwd(x[0], preps[0][0]['nw'], "rms_fwd0")
    got = gathered(0, 1, tok + (preps[0][1] + preps[1][1] + h0[0, 0].astype(F32)), "gather_w_in0")
    small_full = {}

    def other_weights0(proj):
        got = gathered(1, 5, proj, "gather_rest0")
        small_all, off = got[-1].reshape(N_DEV, -1), 0
        for n, sz in zip(SMALL_SHARDED, small_sizes):
            small_full[n] = _join8(small_all[:, off:off + sz].reshape((N_DEV,) + p[n].shape), SHARD_AXIS[n])
            off += sz
        return dict(_other_weights(got[:-1]), **{n: small_full[n][0] for n in SMALL_SHARDED})

    saved, layer_g, full = [None] * DEPTH, [None] * DEPTH, [None] * DEPTH
    relay1 = []

    def relay_layer1(ys):
        relay, tok = relayed(5, 9, ys, "gather1")
        relay1.append(relay)
        return tok

    xs, saved[0], full[0] = _layer_fwd(x[0], h0, 0, preps[0][0], _relayout_w_in(got[0], "relayout_w_in0"), other_weights0,
                                       relay_layer1)
    h1 = _rmsnorm_fwd(xs, preps[1][0]['nw'], "rms_fwd1")
    got = arrived(relay1[0], h1)
    xs, saved[1], full[1] = _layer_fwd(
        xs, h1, 1, preps[1][0], _relayout_w_in(got[0], "relayout_w_in1"),
        lambda proj: dict(_other_weights(got[1:]), **{n: small_full[n][1] for n in SMALL_SHARDED}))
    loss_row, dx, dfw = _loss_head(xs, final_norm_w.reshape(1, D_MODEL), loss_target[0])
    loss = lax.psum(loss_row[0, 0], ("x", "y", "c"))
    loss, dx = lax.optimization_barrier((loss, dx))
    pairs, scat_other, scat_w_in, sent0 = {}, [None] * DEPTH, [None] * DEPTH, []

    def start_other_pairs(i):
        def start(g):
            pairs[i, 'o'], tok = _pair_start(_other_grad_blocks(g), f"{i}o")
            return tok
        return start

    def send_other_chip_sums(i):
        def send(x_after):
            scat_other[i], tok = _chips_start(_pair_sums(pairs[i, 'o'], x_after, f"{i}o"), f"{i}o")
            return tok
        return send

    def start_pairs1(g):
        pairs[1, 'w'], tok = _pair_start([_relayout_g_in(g['w_in'], "relayout_g_in1")], "1w")
        return tok

    def send_chip_sums1(dh):
        scat_w_in[1], tok = _chips_start(_pair_sums(pairs[1, 'w'], dh, "1w"), "1w")
        return tok

    def send_all0(g):
        pairs[0, 'w'], tok = _pair_start([_relayout_g_in(g['w_in'], "relayout_g_in0")], "0w")
        scat_w_in[0], tok = _chips_start(_pair_sums(pairs[0, 'w'], tok, "0w"), "0w")
        sent0.append(tok)
        return tok

    dx, layer_g[1] = _layer_bwd(dx, 1, saved[1], full[1], start_other_pairs(1), send_other_chip_sums(1),
                                start_pairs1, send_chip_sums1)
    dx, layer_g[0] = _layer_bwd(dx, 0, saved[0], full[0], start_other_pairs(0), send_other_chip_sums(0), send_all0)
    for i in range(DEPTH):
        dense = layer_g[i].pop('s5_dense')
        blocks = tuple(_diag_blocks(t, after=sent0[0]) for t in dense[:4])
        layer_g[i].update(zip(S5_NAMES, saved[i]['disc_vjp'](blocks + (dense[4] + sent0[0][0, 0], dense[5]))))
    grads = {n: jnp.stack([layer_g[i][n] for i in range(DEPTH)]) for n in SMALL_SHARDED + REPLICATED if n != 'final_norm_w'}
    grads['final_norm_w'] = dfw[0]

    out_g, out_d, out_m, out_v = {}, {}, {}, {}
    repl_rows = _pack_rows([grads[n] for n in REPLICATED], 8 * N_DEV)
    rr = repl_rows.shape[0] // N_DEV
    shard_rows = _pack_rows([_split8(grads[n], SHARD_AXIS[n]) for n in SMALL_SHARDED], 8, batched=True)
    rs = shard_rows.shape[1]
    small_g = jnp.concatenate([shard_rows, repl_rows.reshape(N_DEV, rr, LANES)], axis=1)
    small_sum = _slot_sum(_exchange([small_g], False, "scatter_small")[0], "sum_small")
    repl_all = _exchange([small_sum[rs:]], True, "gather_small")[0].reshape(N_DEV * rr, LANES)
    g_all = jnp.concatenate([small_sum[:rs], repl_all], axis=0)
    names = SMALL_SHARDED + REPLICATED
    pieces = (_unpack_rows(g_all[:rs], [p[n].shape for n in SMALL_SHARDED])
              + _unpack_rows(g_all[rs:], [p[n].shape for n in REPLICATED]))
    out_g.update(zip(names, pieces))
    res = _adamw_many(*[[_memory_view(n, d[n]) for n in names] for d in (out_g, p, mom, vel)], "adamw_small")
    for r, dst in zip(res, (out_d, out_m, out_v)):
        dst.update({n: _memory_view(n, t) for n, t in zip(names, r)})

    after, landed = res[0][0], {}
    for i in reversed(range(DEPTH)):
        for tag, state, names_ in ((f"{i}o", scat_other[i], OTHER_BIG), (f"{i}w", scat_w_in[i], ['w_in'])):
            got = _split_wait(_plan_chips, state, after, f"chips{tag}_wait")
            landed.update({(n, i): t for n, t in zip(names_, got)})
            after = got[0]
    for n in BIG_SHARDED:
        shp = p[n].shape
        c = shp[-1]
        r = p[n].size // (DEPTH * c)
        if n == 'w_in':
            big = _adamw_w_in([landed[n, 0], landed[n, 1]], p[n], mom[n], vel[n], "adamw_w_in")
        else:
            big = _adamw([landed[n, 0], landed[n, 1]], *[d[n].reshape(DEPTH, r, c) for d in (p, mom, vel)],
                         {'w_branch': 512, 'w_out': 128, 's5_w_glu': 64}[n], "adamw_" + n)
        out_g[n], out_d[n], out_m[n], out_v[n] = [o.reshape(shp) for o in big]
    return (loss, dx[None], *[out_g[n] for n in WEIGHTS], *[out_d[n] for n in WEIGHTS],
            *[out_m[n] for n in WEIGHTS], *[out_v[n] for n in WEIGHTS])
```

```python
import functools

import jax
import jax.numpy as jnp
import numpy as np
from jax import lax
from jax.experimental import pallas as pl
from jax.experimental.pallas import tpu as pltpu

F32 = jnp.float32
BF16 = jnp.bfloat16

N_DEV = 8
SEQ = 2048
D_MODEL = 1024
DEPTH = 2
BW = 512
N_BRANCH = 4
EPS = 1e-6
S5_GROUPS, S5_STATE, S5_P = 32, 64, 16
S5_CH = S5_GROUPS * S5_STATE
SGU_CHUNK, SGU_HEADS = 128, 8
M2_HEADS, M2_HEAD_DIM, M2_STATE, M2_CHUNK, M2_CONV = 8, 64, 128, 128, 4
M2_CONV_CH = 1024
SC_CONV = 3
IN_DIM = 10248
IN_PAD = 11264
C_MERGE = 0
C_S5U, C_S5G = 4096, 4608
C_M2X = 5120
C_SGU_U, C_SGU_V, C_SGU_G = 6144, 6656, 7168
C_M2Z, C_DT = 8192, 8704
C_SC = 9216
SHARD_IN = IN_DIM // N_DEV

ADAM_LR, ADAM_B1, ADAM_B2, ADAM_EPS, ADAM_WD, ADAM_STEP = 0.001, 0.9, 0.999, 1e-08, 0.01, 10

VMEM_LIMIT = 56 * 1024 * 1024
LANES = 128

MESH = pl.DeviceIdType.MESH


def _cparams(sem=None, **kw):
    return pltpu.CompilerParams(dimension_semantics=sem, vmem_limit_bytes=VMEM_LIMIT, **kw)


def _dg(a, b, ca, cb, precision=None):
    return lax.dot_general(a, b, (((ca,), (cb,)), ((), ())), precision=precision,
                           preferred_element_type=F32)


@functools.partial(jax.custom_vjp, nondiff_argnums=(2, 3))
def _bdot(a, b, ca, cb):
    return _dg(a.astype(BF16), b.astype(BF16), ca, cb)


def _bdot_fwd(a, b, ca, cb):
    return _bdot(a, b, ca, cb), (a, b)


def _bdot_bwd(ca, cb, res, g):
    a, b = res
    gb, ab, bb = g.astype(BF16), a.astype(BF16), b.astype(BF16)
    da = _dg(gb, bb, 1, 1 - cb) if ca == 1 else _dg(bb, gb, 1 - cb, 1)
    db = _dg(ab, gb, 1 - ca, 0) if cb == 0 else _dg(gb, ab, 0, 1 - ca)
    return da.astype(a.dtype), db.astype(b.dtype)


_bdot.defvjp(_bdot_fwd, _bdot_bwd)


def _rms(x, w):
    return x * lax.rsqrt(jnp.mean(x * x, axis=-1, keepdims=True) + EPS) * w


def _silu(x):
    return x * jax.nn.sigmoid(x)


def _gelu(x):
    return 0.5 * x * (1.0 + jnp.tanh(0.7978845608028654 * (x + 0.044715 * (x * x * x))))


def _softplus(x):
    return jnp.maximum(x, 0.0) + jnp.log1p(jnp.exp(-jnp.abs(x)))


def _shift_down(x, s):
    if s == 0:
        return x
    row = lax.broadcasted_iota(jnp.int32, x.shape, 0)
    return jnp.where(row >= s, pltpu.roll(x, s, 0), 0.0)


def _shift_up(x, s):
    if s == 0:
        return x
    n = x.shape[0]
    row = lax.broadcasted_iota(jnp.int32, x.shape, 0)
    return jnp.where(row < n - s, pltpu.roll(x, n - s, 0), 0.0)


def _matmul(a, b, ca, cb, out_dtype, tm, tn, tk, name, residual=None, after=None):
    m = a.shape[1 - ca]
    k = a.shape[ca]
    n = b.shape[1 - cb]
    assert b.shape[cb] == k and m % tm == 0 and n % tn == 0 and k % tk == 0
    nk = k // tk
    a_spec = pl.BlockSpec((tm, tk), lambda i, j, kk: (i, kk)) if ca == 1 else pl.BlockSpec((tk, tm), lambda i, j, kk: (kk, i))
    b_spec = pl.BlockSpec((tk, tn), lambda i, j, kk: (kk, j)) if cb == 0 else pl.BlockSpec((tn, tk), lambda i, j, kk: (j, kk))
    o_spec = pl.BlockSpec((tm, tn), lambda i, j, kk: (i, j))
    has_res = residual is not None

    def body(*refs):
        refs = refs[:2 + has_res] + refs[2 + has_res + (after is not None):]
        if has_res:
            a_ref, b_ref, r_ref, o_ref, acc = refs
        else:
            a_ref, b_ref, o_ref, acc = refs
        kk = pl.program_id(2)
        part = _dg(a_ref[...].astype(BF16), b_ref[...].astype(BF16), ca, cb)

        @pl.when(kk == 0)
        def _():
            acc[...] = part

        @pl.when(kk > 0)
        def _():
            acc[...] += part

        @pl.when(kk == nk - 1)
        def _():
            r = acc[...]
            if has_res:
                r = r + r_ref[...]
            o_ref[...] = r.astype(out_dtype)

    ins = [a, b] + ([residual] if has_res else []) + ([after] if after is not None else [])
    specs = [a_spec, b_spec] + ([o_spec] if has_res else []) + ([pl.BlockSpec(memory_space=pl.ANY)] if after is not None else [])
    return pl.pallas_call(
        body, name=name, grid=(m // tm, n // tn, nk), in_specs=specs, out_specs=o_spec,
        out_shape=jax.ShapeDtypeStruct((m, n), out_dtype),
        scratch_shapes=[pltpu.VMEM((tm, tn), F32)],
        compiler_params=_cparams(("parallel", "parallel", "arbitrary")),
    )(*ins)


ROW_TILE = 512


def _rmsnorm_fwd(x, w, name):
    def body(x_ref, w_ref, o_ref):
        o_ref[...] = _rms(x_ref[...], w_ref[...]).astype(BF16)

    return pl.pallas_call(
        body, name=name, grid=(SEQ // ROW_TILE,),
        in_specs=[pl.BlockSpec((ROW_TILE, D_MODEL), lambda i: (i, 0)), pl.BlockSpec((1, D_MODEL), lambda i: (0, 0))],
        out_specs=pl.BlockSpec((ROW_TILE, D_MODEL), lambda i: (i, 0)),
        out_shape=jax.ShapeDtypeStruct((SEQ, D_MODEL), BF16),
        compiler_params=_cparams(("parallel",)),
    )(x, w)


def _rmsnorm_bwd(x, w, dh, dres, name):
    def body(x_ref, w_ref, dh_ref, dres_ref, dx_ref, dw_ref):
        _, vjp = jax.vjp(_rms, x_ref[...], w_ref[...])
        dx, dw = vjp(dh_ref[...])
        dx_ref[...] = dx + dres_ref[...]

        @pl.when(pl.program_id(0) == 0)
        def _():
            dw_ref[...] = dw

        @pl.when(pl.program_id(0) > 0)
        def _():
            dw_ref[...] += dw

    tile = pl.BlockSpec((ROW_TILE, D_MODEL), lambda i: (i, 0))
    vec = pl.BlockSpec((1, D_MODEL), lambda i: (0, 0))
    return pl.pallas_call(
        body, name=name, grid=(SEQ // ROW_TILE,),
        in_specs=[tile, vec, tile, tile], out_specs=[tile, vec],
        out_shape=[jax.ShapeDtypeStruct((SEQ, D_MODEL), F32), jax.ShapeDtypeStruct((1, D_MODEL), F32)],
        compiler_params=_cparams(("arbitrary",)),
    )(x, w, dh, dres)


def _loss_head(x, w, target):
    def body(x_ref, w_ref, t_ref, loss_ref, dx_ref, dw_ref):
        tgt = t_ref[...]

        def f(xv, wv):
            err = _rms(xv, wv) - tgt
            return 0.5 * jnp.sum(jnp.mean(err * err, axis=-1))

        loss, vjp = jax.vjp(f, x_ref[...], w_ref[...])
        dx, dw = vjp(jnp.ones((), F32))
        dx_ref[...] = dx
        lrow = jnp.full((1, LANES), loss, F32)

        @pl.when(pl.program_id(0) == 0)
        def _():
            dw_ref[...] = dw
            loss_ref[...] = lrow

        @pl.when(pl.program_id(0) > 0)
        def _():
            dw_ref[...] += dw
            loss_ref[...] += lrow

    tile = pl.BlockSpec((ROW_TILE, D_MODEL), lambda i: (i, 0))
    vec = pl.BlockSpec((1, D_MODEL), lambda i: (0, 0))
    return pl.pallas_call(
        body, name="loss_head", grid=(SEQ // ROW_TILE,),
        in_specs=[tile, vec, tile], out_specs=[pl.BlockSpec((1, LANES), lambda i: (0, 0)), tile, vec],
        out_shape=[jax.ShapeDtypeStruct((1, LANES), F32), jax.ShapeDtypeStruct((SEQ, D_MODEL), F32),
                   jax.ShapeDtypeStruct((1, D_MODEL), F32)],
        compiler_params=_cparams(("arbitrary",)),
    )(x, w, target)


S5_T = 256
S5_BLOCKS = [(slice(j * 256, (j + 1) * 256), slice(j * 1024, (j + 1) * 1024)) for j in range(2)]


def _s5_post(ypre, gate, wglu):
    y = _gelu(ypre)
    y = y * jax.nn.sigmoid(_bdot(y, wglu, 1, 0))
    return y * _silu(gate)


def _s5_fwd(proj, bbre, bbim, cre, cim, a2, dvec, wglu, name):
    def body(u_ref, g_ref, bbre_ref, bbim_ref, cre_ref, cim_ref, a_ref, d_ref, wg_ref, o_ref, sre_ref, sim_ref, st):
        @pl.when(pl.program_id(0) == 0)
        def _():
            st[...] = jnp.zeros_like(st)

        u = u_ref[...]
        ub = u.astype(BF16)
        for us, ss in S5_BLOCKS:
            sre_ref[:, ss] = _dg(ub[:, us], bbre_ref[us, ss], 1, 0)
            sim_ref[:, ss] = _dg(ub[:, us], bbim_ref[us, ss], 1, 0)
        ar, ai = a_ref[0:1, :], a_ref[1:2, :]

        def step(t, carry):
            sr, si = carry
            nr = ar * sr - ai * si + sre_ref[pl.ds(t, 1), :]
            ni = ar * si + ai * sr + sim_ref[pl.ds(t, 1), :]
            sre_ref[pl.ds(t, 1), :] = nr
            sim_ref[pl.ds(t, 1), :] = ni
            return nr, ni

        sr, si = lax.fori_loop(0, S5_T, step, (st[0:1, :], st[1:2, :]), unroll=8)
        st[0:1, :] = sr
        st[1:2, :] = si
        ypre = jnp.concatenate(
            [_dg(sre_ref[:, ss].astype(BF16), cre_ref[ss, us], 1, 0) - _dg(sim_ref[:, ss].astype(BF16), cim_ref[ss, us], 1, 0)
             for us, ss in S5_BLOCKS], axis=1) + d_ref[...] * u
        o_ref[...] = _s5_post(ypre, g_ref[...], wg_ref[...]).astype(BF16)

    full = lambda shape: pl.BlockSpec(shape, lambda c: (0, 0))
    return pl.pallas_call(
        body, name=name, grid=(SEQ // S5_T,),
        in_specs=[pl.BlockSpec((S5_T, BW), lambda c: (c, C_S5U // BW)), pl.BlockSpec((S5_T, BW), lambda c: (c, C_S5G // BW)),
                  full((BW, S5_CH)), full((BW, S5_CH)), full((S5_CH, BW)), full((S5_CH, BW)),
                  full((2, S5_CH)), full((1, BW)), full((BW, BW))],
        out_specs=[pl.BlockSpec((S5_T, BW), lambda c: (c, 0)), pl.BlockSpec((S5_T, S5_CH), lambda c: (c, 0)),
                   pl.BlockSpec((S5_T, S5_CH), lambda c: (c, 0))],
        out_shape=[jax.ShapeDtypeStruct((SEQ, BW), BF16), jax.ShapeDtypeStruct((SEQ, S5_CH), F32),
                   jax.ShapeDtypeStruct((SEQ, S5_CH), F32)],
        scratch_shapes=[pltpu.VMEM((2, S5_CH), F32)],
        compiler_params=_cparams(("arbitrary",)),
    )(proj, proj, bbre, bbim, cre, cim, a2, dvec, wglu)


def _s5_bwd(proj, dproj, dout, sre, sim, bbre, bbim, cre, cim, a2, dvec, wglu, name):
    nc = SEQ // S5_T

    def body(u_ref, g_ref, do_ref, sre_ref, sim_ref, pre_ref, pim_ref, bbre_ref, bbim_ref, cre_ref, cim_ref, a_ref,
             d_ref, wg_ref, dproj_in, dp_ref, dbbre_ref, dbbim_ref, dcre_ref, dcim_ref, da_ref, dd_ref, dwg_ref,
             gre, gim, st):
        c = nc - 1 - pl.program_id(0)

        @pl.when(pl.program_id(0) == 0)
        def _():
            st[...] = jnp.zeros_like(st)
            for r in (dbbre_ref, dbbim_ref, dcre_ref, dcim_ref, da_ref, dd_ref, dwg_ref):
                r[...] = jnp.zeros_like(r)

        u = u_ref[...]
        s_re, s_im = sre_ref[...], sim_ref[...]

        def head(s_res, s_ims, cres, cims, dv, uv, gv, wg):
            ypre = jnp.concatenate([_bdot(sr, cr, 1, 0) - _bdot(si, ci, 1, 0)
                                    for sr, si, cr, ci in zip(s_res, s_ims, cres, cims)], axis=1) + dv * uv
            return _s5_post(ypre, gv, wg)

        _, vjp = jax.vjp(head, [sre_ref[:, ss] for _, ss in S5_BLOCKS], [sim_ref[:, ss] for _, ss in S5_BLOCKS],
                         [cre_ref[ss, us].astype(F32) for us, ss in S5_BLOCKS],
                         [cim_ref[ss, us].astype(F32) for us, ss in S5_BLOCKS],
                         d_ref[...], u, g_ref[...], wg_ref[...].astype(F32))
        ds_res, ds_ims, dcres, dcims, dd, du_d, dgate, dwg = vjp(do_ref[0])
        for k, (us, ss) in enumerate(S5_BLOCKS):
            dcre_ref[ss, us] += dcres[k]
            dcim_ref[ss, us] += dcims[k]
            gre[:, ss] = ds_res[k]
            gim[:, ss] = ds_ims[k]
        dd_ref[...] += dd
        dwg_ref[...] += dwg
        dp_ref[:, BW:] = dgate.astype(BF16)
        ar, ai = a_ref[0:1, :], a_ref[1:2, :]

        def step(i, carry):
            t = S5_T - 1 - i
            gr, gi = carry
            nr = gre[pl.ds(t, 1), :] + gr
            ni = gim[pl.ds(t, 1), :] + gi
            gre[pl.ds(t, 1), :] = nr
            gim[pl.ds(t, 1), :] = ni
            return ar * nr + ai * ni, ar * ni - ai * nr

        gr, gi = lax.fori_loop(0, S5_T, step, (st[0:1, :], st[1:2, :]), unroll=8)
        st[0:1, :] = gr
        st[1:2, :] = gi
        g_re, g_im = gre[...], gim[...]
        first = jnp.where(c > 0, 1.0, 0.0)
        row = lax.broadcasted_iota(jnp.int32, (S5_T, S5_CH), 0)
        p_re = jnp.where(row == 0, pre_ref[7:8, :] * first, pltpu.roll(s_re, 1, 0))
        p_im = jnp.where(row == 0, pim_ref[7:8, :] * first, pltpu.roll(s_im, 1, 0))
        da_ref[0:1, :] += jnp.sum(g_re * p_re + g_im * p_im, axis=0, keepdims=True)
        da_ref[1:2, :] += jnp.sum(g_im * p_re - g_re * p_im, axis=0, keepdims=True)
        ub, grb, gib = u.astype(BF16), g_re.astype(BF16), g_im.astype(BF16)
        du_s = []
        for us, ss in S5_BLOCKS:
            dbbre_ref[us, ss] += _dg(ub[:, us], grb[:, ss], 0, 0)
            dbbim_ref[us, ss] += _dg(ub[:, us], gib[:, ss], 0, 0)
            du_s.append(_dg(grb[:, ss], bbre_ref[us, ss], 1, 1) + _dg(gib[:, ss], bbim_ref[us, ss], 1, 1))
        dp_ref[:, :BW] = (du_d + jnp.concatenate(du_s, axis=1)).astype(BF16)

    full = lambda shape: pl.BlockSpec(shape, lambda i: (0, 0))
    rev = lambda w, col=0: pl.BlockSpec((S5_T, w), lambda i: (nc - 1 - i, col))
    prev = pl.BlockSpec((8, S5_CH), lambda i: (jnp.maximum((nc - 1 - i) * (S5_T // 8) - 1, 0), 0))
    return pl.pallas_call(
        body, name=name, grid=(nc,),
        in_specs=[rev(BW, C_S5U // BW), rev(BW, C_S5G // BW), pl.BlockSpec((1, S5_T, BW), lambda i: (0, nc - 1 - i, 0)),
                  rev(S5_CH), rev(S5_CH), prev, prev,
                  full((BW, S5_CH)), full((BW, S5_CH)), full((S5_CH, BW)), full((S5_CH, BW)),
                  full((2, S5_CH)), full((1, BW)), full((BW, BW)), pl.BlockSpec(memory_space=pl.ANY)],
        out_specs=[rev(2 * BW, C_S5U // (2 * BW)), full((BW, S5_CH)), full((BW, S5_CH)), full((S5_CH, BW)), full((S5_CH, BW)),
                   full((2, S5_CH)), full((1, BW)), full((BW, BW))],
        input_output_aliases={14: 0},
        out_shape=[jax.ShapeDtypeStruct((SEQ, IN_PAD), BF16),
                   jax.ShapeDtypeStruct((BW, S5_CH), F32), jax.ShapeDtypeStruct((BW, S5_CH), F32),
                   jax.ShapeDtypeStruct((S5_CH, BW), F32), jax.ShapeDtypeStruct((S5_CH, BW), F32),
                   jax.ShapeDtypeStruct((2, S5_CH), F32), jax.ShapeDtypeStruct((1, BW), F32),
                   jax.ShapeDtypeStruct((BW, BW), F32)],
        scratch_shapes=[pltpu.VMEM((S5_T, S5_CH), F32), pltpu.VMEM((S5_T, S5_CH), F32), pltpu.VMEM((2, S5_CH), F32)],
        compiler_params=_cparams(("arbitrary",)),
    )(proj, proj, dout, sre, sim, sre, sim, bbre, bbim, cre, cim, a2, dvec, wglu, dproj)


def _diag_blocks(dense, after=None):
    rows, cols = dense.shape
    rows_per, cols_per = rows // S5_GROUPS, cols // S5_GROUPS
    per_lane_block = LANES // cols_per
    tile = 512

    def body(d_ref, *rest):
        o_ref = rest[-1]
        r0 = pl.program_id(0) * tile
        grp = (r0 + lax.broadcasted_iota(jnp.int32, (tile, LANES), 0)) // rows_per
        lane = lax.broadcasted_iota(jnp.int32, (tile, LANES), 1)
        acc = jnp.zeros((tile, LANES), F32)
        for hb in range(cols // LANES):
            acc = acc + jnp.where(grp == per_lane_block * hb + lane // cols_per, d_ref[:, hb * LANES:(hb + 1) * LANES], 0.0)
        shift = LANES // 2
        while shift >= cols_per:
            acc = acc + pltpu.roll(acc, LANES - shift, 1)
            shift //= 2
        o_ref[...] = acc

    folded = pl.pallas_call(
        body, name=f"diag_blocks_{rows_per}x{cols_per}", grid=(rows // tile,),
        in_specs=[pl.BlockSpec((tile, cols), lambda i: (i, 0))] + ([] if after is None else [pl.BlockSpec(memory_space=pl.ANY)]),
        out_specs=pl.BlockSpec((tile, LANES), lambda i: (i, 0)),
        out_shape=jax.ShapeDtypeStruct((rows, LANES), F32), compiler_params=_cparams(("parallel",)),
    )(dense, *([] if after is None else [after]))
    return folded[:, :cols_per].reshape(S5_GROUPS, rows_per, cols_per)


def _block_diag(t):
    g, rows_per, cols_per = t.shape
    wide = jnp.tile(t.reshape(g * rows_per, cols_per), (1, g))
    r = lax.broadcasted_iota(jnp.int32, wide.shape, 0) // rows_per
    c = lax.broadcasted_iota(jnp.int32, wide.shape, 1) // cols_per
    return jnp.where(r == c, wide, 0.0)


def _s5_disc(lam_re, lam_im, b_re, b_im, c_re, c_im, d, log_step):
    step = jnp.exp(log_step)[:, None]
    mag = jnp.exp(lam_re * step)
    ab_re, ab_im = mag * jnp.cos(lam_im * step), mag * jnp.sin(lam_im * step)
    den = lam_re * lam_re + lam_im * lam_im
    nr = ab_re - 1.0
    coef_re = (nr * lam_re + ab_im * lam_im) / den
    coef_im = (ab_im * lam_re - nr * lam_im) / den
    bb_re = coef_re[..., None] * b_re - coef_im[..., None] * b_im
    bb_im = coef_re[..., None] * b_im + coef_im[..., None] * b_re
    a2 = jnp.stack([ab_re.reshape(-1), ab_im.reshape(-1)])
    return (jnp.swapaxes(bb_re, 1, 2), jnp.swapaxes(bb_im, 1, 2),
            jnp.swapaxes(c_re, 1, 2), jnp.swapaxes(c_im, 1, 2),
            a2, d.reshape(1, BW))


def _left_lanes(shape):
    return lax.broadcasted_iota(jnp.int32, shape, 1) < 64


def _sgu_chunk(u, v, gate, ln_w, ln_b, w, bias):
    u32, v32 = _gelu(u), _gelu(v)
    mu = jnp.mean(v32, axis=-1, keepdims=True)
    var = jnp.mean(jnp.square(v32 - mu), axis=-1, keepdims=True)
    vn = (v32 - mu) * lax.rsqrt(var + EPS) * ln_w + ln_b
    t_i = lax.broadcasted_iota(jnp.int32, (SGU_CHUNK, SGU_CHUNK), 0)
    s_i = lax.broadcasted_iota(jnp.int32, (SGU_CHUNK, SGU_CHUNK), 1)
    causal = t_i >= s_i
    left = _left_lanes((SGU_CHUNK, LANES))
    sgate = _silu(gate)
    outs = []
    for j in range(BW // LANES):
        vb = vn[:, j * LANES:(j + 1) * LANES]
        s_blk = (_bdot(jnp.where(causal, w[2 * j], 0.0), jnp.where(left, vb, 0.0), 1, 0)
                 + _bdot(jnp.where(causal, w[2 * j + 1], 0.0), jnp.where(left, 0.0, vb), 1, 0))
        sl = slice(j * LANES, (j + 1) * LANES)
        outs.append(u32[:, sl] * (s_blk + bias[:, sl]) * sgate[:, sl])
    return outs


def _sgu_fwd(proj, ln_w, ln_b, w, bias, name):
    def body(u_ref, v_ref, g_ref, lw_ref, lb_ref, w_ref, b_ref, o_ref):
        outs = _sgu_chunk(u_ref[...], v_ref[...], g_ref[...], lw_ref[...], lb_ref[...], w_ref[...], b_ref[...])
        for j, o in enumerate(outs):
            o_ref[:, j * LANES:(j + 1) * LANES] = o.astype(BF16)

    blk = lambda col: pl.BlockSpec((SGU_CHUNK, BW), lambda c: (c, col // BW))
    vec = pl.BlockSpec((1, BW), lambda c: (0, 0))
    return pl.pallas_call(
        body, name=name, grid=(SEQ // SGU_CHUNK,),
        in_specs=[blk(C_SGU_U), blk(C_SGU_V), blk(C_SGU_G), vec, vec,
                  pl.BlockSpec((SGU_HEADS, SGU_CHUNK, SGU_CHUNK), lambda c: (0, 0, 0)),
                  pl.BlockSpec((SGU_CHUNK, BW), lambda c: (0, 0))],
        out_specs=pl.BlockSpec((SGU_CHUNK, BW), lambda c: (c, 0)),
        out_shape=jax.ShapeDtypeStruct((SEQ, BW), BF16),
        compiler_params=_cparams(("parallel",)),
    )(proj, proj, proj, ln_w, ln_b, w, bias)


def _sgu_bwd(proj, dproj, dout, ln_w, ln_b, w, bias, name):
    def body(u_ref, v_ref, g_ref, do_ref, lw_ref, lb_ref, w_ref, b_ref, dproj_in, dp_ref, dlw_ref, dlb_ref, dw_ref, db_ref):
        _, vjp = jax.vjp(_sgu_chunk, u_ref[...], v_ref[...], g_ref[...], lw_ref[...], lb_ref[...], w_ref[...], b_ref[...])
        do = do_ref[0]
        du, dv, dgate, dlw, dlb, dw, db = vjp([do[:, j * LANES:(j + 1) * LANES] for j in range(BW // LANES)])
        dp_ref[:, 0:BW] = du.astype(BF16)
        dp_ref[:, BW:2 * BW] = dv.astype(BF16)
        dp_ref[:, 2 * BW:3 * BW] = dgate.astype(BF16)
        dp_ref[:, 3 * BW:] = jnp.zeros((SGU_CHUNK, BW), BF16)

        @pl.when(pl.program_id(0) == 0)
        def _():
            dlw_ref[...] = dlw
            dlb_ref[...] = dlb
            dw_ref[...] = dw
            db_ref[...] = db

        @pl.when(pl.program_id(0) > 0)
        def _():
            dlw_ref[...] += dlw
            dlb_ref[...] += dlb
            dw_ref[...] += dw
            db_ref[...] += db

    blk = lambda col: pl.BlockSpec((SGU_CHUNK, BW), lambda c: (c, col // BW))
    vec = pl.BlockSpec((1, BW), lambda c: (0, 0))
    wsp = pl.BlockSpec((SGU_HEADS, SGU_CHUNK, SGU_CHUNK), lambda c: (0, 0, 0))
    bsp = pl.BlockSpec((SGU_CHUNK, BW), lambda c: (0, 0))
    return pl.pallas_call(
        body, name=name, grid=(SEQ // SGU_CHUNK,),
        in_specs=[blk(C_SGU_U), blk(C_SGU_V), blk(C_SGU_G), pl.BlockSpec((1, SGU_CHUNK, BW), lambda c: (1, c, 0)),
                  vec, vec, wsp, bsp, pl.BlockSpec(memory_space=pl.ANY)],
        out_specs=[pl.BlockSpec((SGU_CHUNK, 4 * BW), lambda c: (c, C_SGU_U // (4 * BW))), vec, vec, wsp, bsp],
        input_output_aliases={8: 0},
        out_shape=[jax.ShapeDtypeStruct((SEQ, IN_PAD), BF16), jax.ShapeDtypeStruct((1, BW), F32),
                   jax.ShapeDtypeStruct((1, BW), F32), jax.ShapeDtypeStruct((SGU_HEADS, SGU_CHUNK, SGU_CHUNK), F32),
                   jax.ShapeDtypeStruct((SGU_CHUNK, BW), F32)],
        compiler_params=_cparams(("arbitrary",)),
    )(proj, proj, proj, dout, ln_w, ln_b, w, bias, dproj)


CONV_BLK = 256


def _m2_conv_fwd(proj, w, b, name):
    def body(x_ref, w_ref, b_ref, o_ref):
        x = x_ref[...]
        acc = jnp.zeros_like(x) + b_ref[...]
        for k in range(M2_CONV):
            acc = acc + w_ref[k:k + 1, :] * _shift_down(x, M2_CONV - 1 - k)
        o_ref[...] = _silu(acc)

    return pl.pallas_call(
        body, name=name, grid=(M2_CONV_CH // CONV_BLK,),
        in_specs=[pl.BlockSpec((SEQ, CONV_BLK), lambda j: (0, C_M2X // CONV_BLK + j)),
                  pl.BlockSpec((M2_CONV, CONV_BLK), lambda j: (0, j)), pl.BlockSpec((1, CONV_BLK), lambda j: (0, j))],
        out_specs=pl.BlockSpec((SEQ, CONV_BLK), lambda j: (0, j)),
        out_shape=jax.ShapeDtypeStruct((SEQ, M2_CONV_CH), F32),
        compiler_params=_cparams(("parallel",)),
    )(proj, w, b)


def _m2_conv_bwd(proj, dproj, dxa, w, b, name):
    def body(x_ref, d_ref, w_ref, b_ref, dproj_in, dx_ref, dw_ref, db_ref):
        x = x_ref[...]
        xs = [_shift_down(x, M2_CONV - 1 - k) for k in range(M2_CONV)]
        acc = jnp.zeros_like(x) + b_ref[...]
        for k in range(M2_CONV):
            acc = acc + w_ref[k:k + 1, :] * xs[k]
        sg = jax.nn.sigmoid(acc)
        dacc = d_ref[...] * (sg * (1.0 + acc * (1.0 - sg)))
        dx = jnp.zeros_like(x)
        for k in range(M2_CONV):
            dx = dx + w_ref[k:k + 1, :] * _shift_up(dacc, M2_CONV - 1 - k)
            dw_ref[k:k + 1, :] = jnp.sum(dacc * xs[k], axis=0, keepdims=True)
        dx_ref[...] = dx.astype(BF16)
        db_ref[...] = jnp.sum(dacc, axis=0, keepdims=True)

    return pl.pallas_call(
        body, name=name, grid=(M2_CONV_CH // CONV_BLK,),
        in_specs=[pl.BlockSpec((SEQ, CONV_BLK), lambda j: (0, C_M2X // CONV_BLK + j)),
                  pl.BlockSpec((SEQ, CONV_BLK), lambda j: (0, j)),
                  pl.BlockSpec((M2_CONV, CONV_BLK), lambda j: (0, j)), pl.BlockSpec((1, CONV_BLK), lambda j: (0, j)),
                  pl.BlockSpec(memory_space=pl.ANY)],
        out_specs=[pl.BlockSpec((SEQ, CONV_BLK), lambda j: (0, C_M2X // CONV_BLK + j)),
                   pl.BlockSpec((M2_CONV, CONV_BLK), lambda j: (0, j)), pl.BlockSpec((1, CONV_BLK), lambda j: (0, j))],
        input_output_aliases={4: 0},
        out_shape=[jax.ShapeDtypeStruct((SEQ, IN_PAD), BF16), jax.ShapeDtypeStruct((M2_CONV, M2_CONV_CH), F32),
                   jax.ShapeDtypeStruct((1, M2_CONV_CH), F32)],
        compiler_params=_cparams(("parallel",)),
    )(proj, dxa, w, b, dproj)


N_PAIR = M2_HEADS // 2
HI = lax.Precision.HIGHEST


def _col(a, h):
    lane = lax.broadcasted_iota(jnp.int32, a.shape, 1)
    return jnp.sum(jnp.where(lane == h, a, 0.0), axis=1, keepdims=True)


def _row(a, h):
    sub = lax.broadcasted_iota(jnp.int32, a.shape, 0)
    return jnp.sum(jnp.where(sub == h, a, 0.0), axis=0, keepdims=True)


def _ssd_chunk(xs, bms, cms, dtr, zs, states, dt_bias, a_log, dfs, nws):
    q = M2_CHUNK
    dt = _softplus(dtr + dt_bias)
    da = dt * (-jnp.exp(a_log))
    l_i = lax.broadcasted_iota(jnp.int32, (q, q), 0)
    s_i = lax.broadcasted_iota(jnp.int32, (q, q), 1)
    causal = l_i >= s_i
    tril = jnp.where(causal, 1.0, 0.0)
    a_cs = _dg(tril, da, 1, 0, HI)
    a_cs_t = _dg(da, tril, 0, 1, HI)
    a_end = _row(a_cs, q - 1)
    left = _left_lanes((q, LANES))
    left1 = _left_lanes((1, LANES))
    ys, nexts = [], []
    for j in range(N_PAIR):
        grp = j // 2
        bm, cm = bms[grp], cms[grp]
        h0, h1 = 2 * j, 2 * j + 1
        cb = _bdot(cm, bm, 1, 1)
        xdt = xs[j] * jnp.where(left, _col(dt, h0), _col(dt, h1))
        acs0, acs1 = _col(a_cs, h0), _col(a_cs, h1)
        y = _bdot(cm, states[j], 1, 0) * jnp.where(left, jnp.exp(acs0), jnp.exp(acs1))
        s_new = states[j] * jnp.where(left1, jnp.exp(_col(a_end, h0)), jnp.exp(_col(a_end, h1)))
        for h, acs, xh in ((h0, acs0, jnp.where(left, xdt, 0.0)), (h1, acs1, jnp.where(left, 0.0, xdt))):
            decay = jnp.exp(jnp.where(causal, acs - _row(a_cs_t, h), -jnp.inf))
            y = y + _bdot(cb * decay, xh, 1, 0)
            s_new = s_new + _bdot(bm * jnp.exp(_col(a_end, h) - acs), xh, 0, 0)
        ys.append((y + dfs[j] * xs[j]) * _silu(zs[j]))
        nexts.append(s_new)
    ssq = sum(jnp.sum(y * y, axis=-1, keepdims=True) for y in ys)
    scale = lax.rsqrt(ssq / BW + EPS)
    return [y * scale * nw for y, nw in zip(ys, nws)], nexts


def _blocks(ref, n, width=LANES):
    return [ref[:, j * width:(j + 1) * width] for j in range(n)]


def _ssd_fwd(proj, xa, dt_bias, a_log, dfull, nw, name):
    nc = SEQ // M2_CHUNK

    def body(x_ref, b_ref, c_ref, dt_ref, z_ref, dtb_ref, al_ref, df_ref, nw_ref, o_ref, sin_ref, st):
        @pl.when(pl.program_id(0) == 0)
        def _():
            st[...] = jnp.zeros_like(st)

        states = [st[j] for j in range(N_PAIR)]
        for j in range(N_PAIR):
            sin_ref[0, j] = states[j]
        ys, nexts = _ssd_chunk(_blocks(x_ref, 4), _blocks(b_ref, 2), _blocks(c_ref, 2), dt_ref[...], _blocks(z_ref, 4),
                               states, dtb_ref[...], al_ref[...], _blocks(df_ref, 4), _blocks(nw_ref, 4))
        for j in range(N_PAIR):
            o_ref[:, j * LANES:(j + 1) * LANES] = ys[j].astype(BF16)
            st[j] = nexts[j]

    vec8 = pl.BlockSpec((1, LANES), lambda c: (0, 0))
    vec = pl.BlockSpec((1, BW), lambda c: (0, 0))
    return pl.pallas_call(
        body, name=name, grid=(nc,),
        in_specs=[pl.BlockSpec((M2_CHUNK, BW), lambda c: (c, 0)), pl.BlockSpec((M2_CHUNK, 256), lambda c: (c, 2)),
                  pl.BlockSpec((M2_CHUNK, 256), lambda c: (c, 3)), pl.BlockSpec((M2_CHUNK, LANES), lambda c: (c, C_DT // LANES)),
                  pl.BlockSpec((M2_CHUNK, BW), lambda c: (c, C_M2Z // BW)), vec8, vec8, vec, vec],
        out_specs=[pl.BlockSpec((M2_CHUNK, BW), lambda c: (c, 0)),
                   pl.BlockSpec((1, N_PAIR, M2_STATE, LANES), lambda c: (c, 0, 0, 0))],
        out_shape=[jax.ShapeDtypeStruct((SEQ, BW), BF16), jax.ShapeDtypeStruct((nc, N_PAIR, M2_STATE, LANES), F32)],
        scratch_shapes=[pltpu.VMEM((N_PAIR, M2_STATE, LANES), F32)],
        compiler_params=_cparams(("arbitrary",)),
    )(xa, xa, xa, proj, proj, dt_bias, a_log, dfull, nw)


def _ssd_bwd(proj, dproj, xa, dout, s_in, dt_bias, a_log, dfull, nw, name):
    nc = SEQ // M2_CHUNK

    def body(x_ref, b_ref, c_ref, dt_ref, z_ref, do_ref, sin_ref, dtb_ref, al_ref, df_ref, nw_ref, dproj_in,
             dp_ref, dxa_ref, ddtb_ref, dal_ref, ddf_ref, dnw_ref, dst):
        @pl.when(pl.program_id(0) == 0)
        def _():
            dst[...] = jnp.zeros_like(dst)
            for r in (ddtb_ref, dal_ref, ddf_ref, dnw_ref):
                r[...] = jnp.zeros_like(r)

        states = [sin_ref[0, j] for j in range(N_PAIR)]
        _, vjp = jax.vjp(_ssd_chunk, _blocks(x_ref, 4), _blocks(b_ref, 2), _blocks(c_ref, 2), dt_ref[...],
                         _blocks(z_ref, 4), states, dtb_ref[...], al_ref[...], _blocks(df_ref, 4), _blocks(nw_ref, 4))
        dxs, dbs, dcs, ddt, dzs, dstates, ddtb, dal, ddfs, dnws = vjp(
            ([do_ref[0, :, j * LANES:(j + 1) * LANES] for j in range(N_PAIR)], [dst[j] for j in range(N_PAIR)]))
        for j in range(N_PAIR):
            sl = slice(j * LANES, (j + 1) * LANES)
            dxa_ref[:, sl] = dxs[j]
            dp_ref[:, sl] = dzs[j].astype(BF16)
            dst[j] = dstates[j]
            ddf_ref[:, sl] += ddfs[j]
            dnw_ref[:, sl] += dnws[j]
        for g in range(2):
            dxa_ref[:, BW + g * LANES:BW + (g + 1) * LANES] = dbs[g]
            dxa_ref[:, BW + 256 + g * LANES:BW + 256 + (g + 1) * LANES] = dcs[g]
        dp_ref[:, BW:BW + LANES] = ddt.astype(BF16)
        dp_ref[:, BW + LANES:] = jnp.zeros((M2_CHUNK, 2 * BW - BW - LANES), BF16)
        ddtb_ref[...] += ddtb
        dal_ref[...] += dal

    rev = lambda w, col=0: pl.BlockSpec((M2_CHUNK, w), lambda i: (nc - 1 - i, col))
    vec8 = pl.BlockSpec((1, LANES), lambda i: (0, 0))
    vec = pl.BlockSpec((1, BW), lambda i: (0, 0))
    return pl.pallas_call(
        body, name=name, grid=(nc,),
        in_specs=[rev(BW), rev(256, 2), rev(256, 3), rev(LANES, C_DT // LANES), rev(BW, C_M2Z // BW),
                  pl.BlockSpec((1, M2_CHUNK, BW), lambda i: (2, nc - 1 - i, 0)),
                  pl.BlockSpec((1, N_PAIR, M2_STATE, LANES), lambda i: (nc - 1 - i, 0, 0, 0)), vec8, vec8, vec, vec,
                  pl.BlockSpec(memory_space=pl.ANY)],
        out_specs=[rev(2 * BW, C_M2Z // (2 * BW)), rev(M2_CONV_CH), vec8, vec8, vec, vec],
        input_output_aliases={11: 0},
        out_shape=[jax.ShapeDtypeStruct((SEQ, IN_PAD), BF16), jax.ShapeDtypeStruct((SEQ, M2_CONV_CH), F32),
                   jax.ShapeDtypeStruct((1, LANES), F32), jax.ShapeDtypeStruct((1, LANES), F32),
                   jax.ShapeDtypeStruct((1, BW), F32), jax.ShapeDtypeStruct((1, BW), F32)],
        scratch_shapes=[pltpu.VMEM((N_PAIR, M2_STATE, LANES), F32)],
        compiler_params=_cparams(("arbitrary",)),
    )(xa, xa, xa, proj, proj, dout, s_in, dt_bias, a_log, dfull, nw, dproj)


def _sc_specs():
    col = lambda kind: pl.BlockSpec((SEQ, LANES), lambda j: (0, C_SC // LANES + 4 * j + kind))
    return [col(0), col(1), col(2), col(3)]


def _sc_fwd(proj, w, name):
    def body(b_ref, c_ref, h_ref, g_ref, w_ref, o_ref):
        ch = c_ref[...] * h_ref[...]
        acc = jnp.zeros_like(ch)
        for k in range(SC_CONV):
            acc = acc + w_ref[k:k + 1, :] * _shift_down(ch, SC_CONV - 1 - k)
        o_ref[...] = (b_ref[...] * acc * _silu(g_ref[...])).astype(BF16)

    return pl.pallas_call(
        body, name=name, grid=(BW // LANES,),
        in_specs=_sc_specs() + [pl.BlockSpec((SC_CONV, LANES), lambda j: (0, j))],
        out_specs=pl.BlockSpec((SEQ, LANES), lambda j: (0, j)),
        out_shape=jax.ShapeDtypeStruct((SEQ, BW), BF16),
        compiler_params=_cparams(("parallel",)),
    )(proj, proj, proj, proj, w)


def _sc_bwd(proj, dproj, dout, w, name):
    def body(b_ref, c_ref, h_ref, g_ref, do_ref, w_ref, dproj_in, dp_ref, dw_ref):
        cv, hv, gv = c_ref[...], h_ref[...], g_ref[...]
        ch = cv * hv
        chs = [_shift_down(ch, SC_CONV - 1 - k) for k in range(SC_CONV)]
        acc = jnp.zeros_like(ch)
        for k in range(SC_CONV):
            acc = acc + w_ref[k:k + 1, :] * chs[k]
        sg = jax.nn.sigmoid(gv)
        do = do_ref[0]
        bv = b_ref[...]
        dp_ref[:, 0:LANES] = (do * acc * (gv * sg)).astype(BF16)
        dp_ref[:, 3 * LANES:] = (do * bv * acc * (sg * (1.0 + gv * (1.0 - sg)))).astype(BF16)
        dacc = do * bv * (gv * sg)
        dch = jnp.zeros_like(ch)
        for k in range(SC_CONV):
            dch = dch + w_ref[k:k + 1, :] * _shift_up(dacc, SC_CONV - 1 - k)
            dw_ref[k:k + 1, :] = jnp.sum(dacc * chs[k], axis=0, keepdims=True)
        dp_ref[:, LANES:2 * LANES] = (dch * hv).astype(BF16)
        dp_ref[:, 2 * LANES:3 * LANES] = (dch * cv).astype(BF16)

    wsp = pl.BlockSpec((SC_CONV, LANES), lambda j: (0, j))
    return pl.pallas_call(
        body, name=name, grid=(BW // LANES,),
        in_specs=_sc_specs() + [pl.BlockSpec((1, SEQ, LANES), lambda j: (3, 0, j)), wsp, pl.BlockSpec(memory_space=pl.ANY)],
        out_specs=[pl.BlockSpec((SEQ, 4 * LANES), lambda j: (0, C_SC // (4 * LANES) + j)), wsp],
        input_output_aliases={6: 0},
        out_shape=[jax.ShapeDtypeStruct((SEQ, IN_PAD), BF16), jax.ShapeDtypeStruct((SC_CONV, BW), F32)],
        compiler_params=_cparams(("parallel",)),
    )(proj, proj, proj, proj, dout, w, dproj)


MERGE_T = 256
MERGE_BWD_T = 512


def _merge_fwd(proj, ys, merge_b, w_branch, name):
    def body(y_ref, lg_ref, b_ref, w_ref, o_ref):
        acc = jnp.zeros((MERGE_T, D_MODEL), F32)
        for k in range(N_BRANCH):
            gate = jax.nn.sigmoid(lg_ref[:, k * D_MODEL:(k + 1) * D_MODEL] + b_ref[k])
            acc = acc + gate * _dg(y_ref[k], w_ref[k], 1, 0)
        o_ref[...] = acc.astype(BF16)

    return pl.pallas_call(
        body, name=name, grid=(SEQ // MERGE_T,),
        in_specs=[pl.BlockSpec((N_BRANCH, MERGE_T, BW), lambda i: (0, i, 0)),
                  pl.BlockSpec((MERGE_T, N_BRANCH * D_MODEL), lambda i: (i, C_MERGE // (N_BRANCH * D_MODEL))),
                  pl.BlockSpec((N_BRANCH, 1, D_MODEL), lambda i: (0, 0, 0)),
                  pl.BlockSpec((N_BRANCH, BW, D_MODEL), lambda i: (0, 0, 0))],
        out_specs=pl.BlockSpec((MERGE_T, D_MODEL), lambda i: (i, 0)),
        out_shape=jax.ShapeDtypeStruct((SEQ, D_MODEL), BF16),
        compiler_params=_cparams(("parallel",)),
    )(ys, proj, merge_b, w_branch)


def _merge_bwd(proj, ys, dm, merge_b, w_branch, name):
    nt = SEQ // MERGE_BWD_T

    def body(y_ref, lg_ref, dm_ref, b_ref, w_ref, dy_ref, dlg_ref, dw_ref, db_ref, dw_acc):
        i = pl.program_id(1)
        gate = jax.nn.sigmoid(lg_ref[...] + b_ref[0])
        y = y_ref[0]
        dmv = dm_ref[...]
        dbo = (gate * dmv).astype(BF16)
        dlg = _dg(y, w_ref[0], 1, 0) * dmv * gate * (1.0 - gate)
        dlg_ref[...] = dlg.astype(BF16)
        dy_ref[0] = _dg(dbo, w_ref[0], 1, 1)
        dwp = _dg(y, dbo, 0, 0)
        dbp = jnp.sum(dlg, axis=0, keepdims=True)

        @pl.when(i == 0)
        def _():
            dw_acc[...] = dwp
            db_ref[0] = dbp

        @pl.when(i > 0)
        def _():
            dw_acc[...] += dwp
            db_ref[0] += dbp

        @pl.when(i == nt - 1)
        def _():
            dw_ref[0] = dw_acc[...].astype(BF16)

    return pl.pallas_call(
        body, name=name, grid=(N_BRANCH, nt),
        in_specs=[pl.BlockSpec((1, MERGE_BWD_T, BW), lambda k, i: (k, i, 0)),
                  pl.BlockSpec((MERGE_BWD_T, D_MODEL), lambda k, i: (i, C_MERGE // D_MODEL + k)),
                  pl.BlockSpec((MERGE_BWD_T, D_MODEL), lambda k, i: (i, 0)),
                  pl.BlockSpec((1, 1, D_MODEL), lambda k, i: (k, 0, 0)),
                  pl.BlockSpec((1, BW, D_MODEL), lambda k, i: (k, 0, 0))],
        out_specs=[pl.BlockSpec((1, MERGE_BWD_T, BW), lambda k, i: (k, i, 0)),
                   pl.BlockSpec((MERGE_BWD_T, D_MODEL), lambda k, i: (i, k)),
                   pl.BlockSpec((1, BW, D_MODEL), lambda k, i: (k, 0, 0)),
                   pl.BlockSpec((1, 1, D_MODEL), lambda k, i: (k, 0, 0))],
        out_shape=[jax.ShapeDtypeStruct((N_BRANCH, SEQ, BW), F32), jax.ShapeDtypeStruct((SEQ, IN_PAD), BF16),
                   jax.ShapeDtypeStruct((N_BRANCH, BW, D_MODEL), BF16), jax.ShapeDtypeStruct((N_BRANCH, 1, D_MODEL), F32)],
        scratch_shapes=[pltpu.VMEM((BW, D_MODEL), F32)],
        compiler_params=_cparams(("parallel", "arbitrary")),
    )(ys, proj, dm, merge_b, w_branch)


def _adamw(glist, w, m, v, rows, name):
    nl = len(glist)
    n, r, c = glist[0].shape
    assert w.shape == (nl, r, c) and r % rows == 0
    nb = r // rows

    def body(*refs):
        g_refs = refs[:nl]
        w_ref, m_ref, v_ref, go_ref, d_ref, mo_ref, vo_ref = refs[nl:]
        for layer in range(nl):
            @pl.when(pl.program_id(0) == layer)
            def _(g_ref=g_refs[layer]):
                g = g_ref[0].astype(F32)
                for s in range(1, n):
                    g = g + g_ref[s].astype(F32)
                mn = ADAM_B1 * m_ref[0] + (1.0 - ADAM_B1) * g
                vn = ADAM_B2 * v_ref[0] + (1.0 - ADAM_B2) * jnp.square(g)
                m_hat = mn / (1.0 - ADAM_B1 ** ADAM_STEP)
                v_hat = vn / (1.0 - ADAM_B2 ** ADAM_STEP)
                go_ref[0] = g
                d_ref[0] = -ADAM_LR * (m_hat / (jnp.sqrt(v_hat) + ADAM_EPS) + ADAM_WD * w_ref[0])
                mo_ref[0] = mn
                vo_ref[0] = vn

    def g_spec(layer):
        return pl.BlockSpec((n, rows, c), lambda a, i: (0, jnp.where(a < layer, 0, jnp.where(a == layer, i, nb - 1)), 0))

    blk = pl.BlockSpec((1, rows, c), lambda a, i: (a, i, 0))
    out = jax.ShapeDtypeStruct((nl, r, c), F32)
    return pl.pallas_call(
        body, name=name, grid=(nl, nb),
        in_specs=[g_spec(layer) for layer in range(nl)] + [blk, blk, blk],
        out_specs=[blk, blk, blk, blk], out_shape=[out, out, out, out],
        compiler_params=_cparams(("arbitrary", "arbitrary")),
    )(*glist, w, m, v)


X_ROWS_PER_COL = 2 * (D_MODEL // LANES)


def _w_in_to_x(w):
    t = jnp.transpose(w, (2, 0, 1)).reshape(SHARD_IN, DEPTH, D_MODEL // LANES, LANES)
    return jnp.transpose(t, (0, 2, 1, 3)).reshape(SHARD_IN * X_ROWS_PER_COL, LANES)


def _w_in_from_x(xv):
    t = jnp.transpose(xv.reshape(SHARD_IN, D_MODEL // LANES, DEPTH, LANES), (0, 2, 1, 3))
    return jnp.transpose(t.reshape(SHARD_IN, DEPTH, D_MODEL), (1, 2, 0))


def _adamw_w_in(glist, w, m, v, name, after=None):
    n = glist[0].shape[0]
    cols = LANES
    rows = cols * X_ROWS_PER_COL
    extra = [] if after is None else [after]

    def body(g0_ref, g1_ref, w_ref, m_ref, v_ref, *rest):
        go_ref, d_ref, mo_ref, vo_ref = rest[len(extra):]
        for layer, g_ref in enumerate((g0_ref, g1_ref)):
            g = g_ref[0].astype(F32)
            for s in range(1, n):
                g = g + g_ref[s].astype(F32)
            gt = g.T
            for t in range(D_MODEL // LANES):
                sel = (pl.ds(2 * t + layer, cols, stride=X_ROWS_PER_COL), slice(None))
                gs = gt[:, t * LANES:(t + 1) * LANES]
                mn = ADAM_B1 * m_ref[sel] + (1.0 - ADAM_B1) * gs
                vn = ADAM_B2 * v_ref[sel] + (1.0 - ADAM_B2) * jnp.square(gs)
                m_hat = mn / (1.0 - ADAM_B1 ** ADAM_STEP)
                v_hat = vn / (1.0 - ADAM_B2 ** ADAM_STEP)
                go_ref[sel] = gs
                d_ref[sel] = -ADAM_LR * (m_hat / (jnp.sqrt(v_hat) + ADAM_EPS) + ADAM_WD * w_ref[sel])
                mo_ref[sel] = mn
                vo_ref[sel] = vn

    g_spec = pl.BlockSpec((n, D_MODEL, cols), lambda i: (0, 0, i))
    blk = pl.BlockSpec((rows, LANES), lambda i: (i, 0))
    out = jax.ShapeDtypeStruct((SHARD_IN * X_ROWS_PER_COL, LANES), F32)
    res = pl.pallas_call(
        body, name=name, grid=(-(-SHARD_IN // cols),),
        in_specs=[g_spec, g_spec, blk, blk, blk] + [pl.BlockSpec(memory_space=pl.ANY)] * len(extra),
        out_specs=[blk, blk, blk, blk], out_shape=[out, out, out, out],
        compiler_params=_cparams(("parallel",)),
    )(*glist, _w_in_to_x(w), _w_in_to_x(m), _w_in_to_x(v), *extra)
    return [_w_in_from_x(o) for o in res]


def _adamw_many(gs, ws, ms, vs, name):
    k = len(gs)

    def body(*refs):
        g_refs, w_refs, m_refs, v_refs = refs[:k], refs[k:2 * k], refs[2 * k:3 * k], refs[3 * k:4 * k]
        d_refs, mo_refs, vo_refs = refs[4 * k:5 * k], refs[5 * k:6 * k], refs[6 * k:7 * k]
        for i in range(k):
            g = g_refs[i][...]
            mn = ADAM_B1 * m_refs[i][...] + (1.0 - ADAM_B1) * g
            vn = ADAM_B2 * v_refs[i][...] + (1.0 - ADAM_B2) * jnp.square(g)
            m_hat = mn / (1.0 - ADAM_B1 ** ADAM_STEP)
            v_hat = vn / (1.0 - ADAM_B2 ** ADAM_STEP)
            d_refs[i][...] = -ADAM_LR * (m_hat / (jnp.sqrt(v_hat) + ADAM_EPS) + ADAM_WD * w_refs[i][...])
            mo_refs[i][...] = mn
            vo_refs[i][...] = vn

    whole = pl.BlockSpec(memory_space=pltpu.VMEM)
    shapes = [jax.ShapeDtypeStruct(w.shape, F32) for w in ws]
    outs = pl.pallas_call(
        body, name=name, in_specs=[whole] * (4 * k), out_specs=[whole] * (3 * k), out_shape=shapes * 3,
        compiler_params=_cparams(None),
    )(*gs, *ws, *ms, *vs)
    return outs[:k], outs[k:2 * k], outs[2 * k:]


MEMORY_ORDER = {'s5_b_re': (0, 1, 3, 2), 's5_b_im': (0, 1, 3, 2), 's5_d': (0, 2, 1), 'sc_conv_w': (1, 0, 2)}


def _memory_view(name, t):
    return jnp.transpose(t, MEMORY_ORDER[name]) if name in MEMORY_ORDER else t


def _slot_sum(gslots, name):
    n, r, c = gslots.shape

    def body(g_ref, o_ref):
        g = g_ref[0]
        for s in range(1, n):
            g = g + g_ref[s]
        o_ref[...] = g

    return pl.pallas_call(
        body, name=name, in_specs=[pl.BlockSpec((n, r, c), lambda: (0, 0, 0))],
        out_specs=pl.BlockSpec((r, c), lambda: (0, 0)), out_shape=jax.ShapeDtypeStruct((r, c), F32),
        compiler_params=_cparams(None),
    )(gslots)


def _me_and_peers():
    x, y, c = lax.axis_index("x"), lax.axis_index("y"), lax.axis_index("c")
    me = 4 * x + 2 * y + c
    peers = []
    for k in range(1, N_DEV):
        px = 1 - x if (k >> 2) & 1 else x
        py = 1 - y if (k >> 1) & 1 else y
        pc = 1 - c if k & 1 else c
        peers.append((4 * px + 2 * py + pc, (px, py, pc)))
    return me, peers


def _exchange(tensors, gather, name):
    n = len(tensors)

    def body(*refs):
        ins, outs = refs[:n], refs[n:2 * n]
        send_sems, recv_sems, local_sems = refs[2 * n:]
        me, peers = _me_and_peers()
        started = []
        for t in range(n):
            own = pltpu.make_async_copy(ins[t] if gather else ins[t].at[me], outs[t].at[me], local_sems.at[t])
            own.start()
            started.append(own)
            for k, (pidx, pos) in enumerate(peers):
                cp = pltpu.make_async_remote_copy(
                    src_ref=ins[t] if gather else ins[t].at[pidx], dst_ref=outs[t].at[me],
                    send_sem=send_sems.at[t, k], recv_sem=recv_sems.at[t, k], device_id=pos, device_id_type=MESH)
                cp.start()
                started.append(cp)
        for cp in started:
            cp.wait()

    any_spec = pl.BlockSpec(memory_space=pl.ANY)
    outs = pl.pallas_call(
        body, name=name, in_specs=[any_spec] * n, out_specs=[any_spec] * n,
        out_shape=[jax.ShapeDtypeStruct(((N_DEV,) + t.shape) if gather else t.shape, t.dtype) for t in tensors],
        scratch_shapes=[pltpu.SemaphoreType.DMA((n, N_DEV - 1)), pltpu.SemaphoreType.DMA((n, N_DEV - 1)),
                        pltpu.SemaphoreType.DMA((n,))],
        compiler_params=pltpu.CompilerParams(has_side_effects=True),
    )(*tensors)
    return list(outs)


_HBM = pl.BlockSpec(memory_space=pltpu.HBM)
_SEM = pl.BlockSpec(memory_space=pltpu.SEMAPHORE)
_EFFECT = pltpu.SideEffectType.DATAFLOW_SIDE_EFFECTING


N_CHIP = N_DEV // 2


def _chip_peers():
    x, y, c = lax.axis_index("x"), lax.axis_index("y"), lax.axis_index("c")
    chips = []
    for d in range(1, N_CHIP):
        px = 1 - x if (d >> 1) & 1 else x
        py = 1 - y if d & 1 else y
        chips.append((2 * px + py, (px, py)))
    return (x, y, c), 2 * x + y, chips


def _plan_direct(ins, lands, send_sems, recv_sems, local_sems, gather):
    me, peers = _me_and_peers()
    plan = dict(start=[], local=[], sends=[], recvs=[])
    for t in range(len(ins)):
        own = pltpu.make_async_copy(ins[t] if gather else ins[t].at[me], lands[t].at[me], local_sems.at[t])
        plan['start'].append(own)
        plan['local'].append(own)
        for k, (pidx, pos) in enumerate(peers):
            cp = pltpu.make_async_remote_copy(
                src_ref=ins[t] if gather else ins[t].at[pidx], dst_ref=lands[t].at[me],
                send_sem=send_sems.at[t * (N_DEV - 1) + k], recv_sem=recv_sems.at[t * (N_DEV - 1) + k],
                device_id=pos, device_id_type=MESH)
            plan['start'].append(cp)
            plan['sends'].append(cp)
            plan['recvs'].append(cp)
    return plan


def _plan_gather(ins, lands, send_sems, recv_sems, local_sems, first=0):
    (x, y, c), q, chips = _chip_peers()
    me = 2 * q + c
    plan = dict(start=[], relay_wait=[], relay_start=[], local=[], sends=[], recvs=[])
    for t in range(len(ins)):
        base = (first + t) * 7
        sem = lambda k: dict(send_sem=send_sems.at[base + k], recv_sem=recv_sems.at[base + k], device_id_type=MESH)
        own = pltpu.make_async_copy(ins[t], lands[t].at[me], local_sems.at[first + t])
        to_sib = pltpu.make_async_remote_copy(src_ref=ins[t], dst_ref=lands[t].at[me], device_id=(x, y, 1 - c), **sem(0))
        plan['start'] += [own, to_sib]
        plan['local'].append(own)
        plan['sends'].append(to_sib)
        plan['recvs'].append(to_sib)
        for d, (pq, (px, py)) in enumerate(chips):
            to_chip = pltpu.make_async_remote_copy(src_ref=ins[t], dst_ref=lands[t].at[me], device_id=(px, py, c), **sem(1 + d))
            blk = lands[t].at[2 * pq + c]
            fwd = pltpu.make_async_remote_copy(src_ref=blk, dst_ref=blk, device_id=(x, y, 1 - c), **sem(4 + d))
            plan['start'].append(to_chip)
            plan['relay_wait'].append(to_chip)
            plan['relay_start'].append(fwd)
            plan['sends'] += [to_chip, fwd]
            plan['recvs'].append(fwd)
    return plan


def _plan_pair(ins, lands, send_sems, recv_sems, local_sems):
    (x, y, c), q, chips = _chip_peers()
    plan = dict(start=[], local=[], sends=[], recvs=[])
    for t in range(len(ins)):
        for k in range(N_CHIP):
            cp = pltpu.make_async_remote_copy(
                src_ref=ins[t].at[2 * k + 1 - c], dst_ref=lands[t].at[k], send_sem=send_sems.at[t * N_CHIP + k],
                recv_sem=recv_sems.at[t * N_CHIP + k], device_id=(x, y, 1 - c), device_id_type=MESH)
            plan['start'].append(cp)
            plan['sends'].append(cp)
            plan['recvs'].append(cp)
    return plan


def _plan_chips(ins, lands, send_sems, recv_sems, local_sems):
    (x, y, c), q, chips = _chip_peers()
    plan = dict(start=[], local=[], sends=[], recvs=[])
    for t in range(len(ins)):
        own = pltpu.make_async_copy(ins[t].at[q], lands[t].at[q], local_sems.at[t])
        plan['start'].append(own)
        plan['local'].append(own)
        for d, (pq, (px, py)) in enumerate(chips):
            cp = pltpu.make_async_remote_copy(
                src_ref=ins[t].at[pq], dst_ref=lands[t].at[q], send_sem=send_sems.at[t * 3 + d],
                recv_sem=recv_sems.at[t * 3 + d], device_id=(px, py, c), device_id_type=MESH)
            plan['start'].append(cp)
            plan['sends'].append(cp)
            plan['recvs'].append(cp)
    return plan


def _split_start(plan_fn, tensors, land_shapes, n_sems, name, after=None):
    n = len(tensors)
    extra = [] if after is None else [after]

    def body(*refs):
        ins, lands = refs[:n], refs[n:2 * n]
        plan = plan_fn(ins, lands, *refs[2 * n + len(extra):2 * n + len(extra) + 3])
        for cp in plan['start']:
            cp.start()
        refs[-1][...] = jnp.zeros_like(refs[-1])

    outs = pl.pallas_call(
        body, name=name,
        out_shape=(pltpu.SemaphoreType.DMA((n_sems,)), pltpu.SemaphoreType.DMA((n_sems,)), pltpu.SemaphoreType.DMA((n,)),
                   *[pltpu.HBM(t.shape, t.dtype) for t in tensors],
                   *[pltpu.HBM(s, t.dtype) for s, t in zip(land_shapes, tensors)],
                   jax.ShapeDtypeStruct((8, LANES), F32)),
        in_specs=[_HBM] * (2 * n) + [pl.BlockSpec(memory_space=pl.ANY)] * len(extra),
        out_specs=(_SEM, _SEM, _SEM, *[_HBM] * (2 * n), pl.BlockSpec(memory_space=pltpu.VMEM)),
        input_output_aliases={t: 3 + t for t in range(2 * n)},
        compiler_params=pltpu.CompilerParams(has_side_effects=_EFFECT),
    )(*[pltpu.with_memory_space_constraint(t, pltpu.HBM) for t in tensors],
      *[pltpu.with_memory_space_constraint(lax.empty(s, t.dtype), pltpu.HBM) for s, t in zip(land_shapes, tensors)], *extra)
    return outs[:-1], outs[-1]


def _split_relay(plan_fn, state, after, name):
    sems, thru = state[:3], state[3:]
    n = len(thru) // 2

    def arrived(*refs):
        plan = plan_fn(refs[:n], refs[n:2 * n], *refs[2 * n:2 * n + 3])
        for cp in plan['relay_wait']:
            cp.wait_recv()

    thru = pl.pallas_call(
        arrived, name=name + "_arrived",
        out_shape=tuple(pltpu.HBM(t.shape, t.dtype) for t in thru),
        in_specs=[_HBM] * (2 * n) + [_SEM, _SEM, _SEM, pl.BlockSpec(memory_space=pl.ANY)],
        out_specs=tuple([_HBM] * (2 * n)),
        input_output_aliases={t: t for t in range(2 * n)},
        compiler_params=pltpu.CompilerParams(has_side_effects=_EFFECT),
    )(*thru, *sems, after)

    def forward(*refs):
        plan = plan_fn(refs[:n], refs[n:2 * n], *refs[2 * n:2 * n + 3])
        for cp in plan['relay_start']:
            cp.start()
        refs[-1][...] = jnp.zeros_like(refs[-1])

    outs = pl.pallas_call(
        forward, name=name + "_forward",
        out_shape=(*[pltpu.HBM(t.shape, t.dtype) for t in thru], jax.ShapeDtypeStruct((8, LANES), F32)),
        in_specs=[_HBM] * (2 * n) + [_SEM, _SEM, _SEM],
        out_specs=(*[_HBM] * (2 * n), pl.BlockSpec(memory_space=pltpu.VMEM)),
        input_output_aliases={t: t for t in range(2 * n)},
        compiler_params=pltpu.CompilerParams(has_side_effects=_EFFECT),
    )(*thru, *sems)
    return (*sems, *outs[:-1]), outs[-1]


def _split_wait(plan_fn, state, after, name, with_sources=False):
    sems, thru = state[:3], state[3:]
    n = len(thru) // 2

    def body(*refs):
        plan = plan_fn(refs[:n], refs[n:2 * n], *refs[2 * n:2 * n + 3])
        for cp in plan['local']:
            cp.wait()
        for cp in plan['sends']:
            cp.wait_send()
        for cp in plan['recvs']:
            cp.wait_recv()

    outs = pl.pallas_call(
        body, name=name,
        out_shape=tuple(pltpu.HBM(t.shape, t.dtype) for t in thru),
        in_specs=[_HBM] * (2 * n) + [_SEM, _SEM, _SEM, pl.BlockSpec(memory_space=pl.ANY)],
        out_specs=tuple([_HBM] * (2 * n)),
        input_output_aliases={t: t for t in range(2 * n)},
        compiler_params=pltpu.CompilerParams(has_side_effects=_EFFECT),
    )(*thru, *sems, after)
    return (list(outs[:n]), list(outs[n:])) if with_sources else list(outs[n:])


PAIR_SUM_BLOCK = 768 * 1024


def _pair_sum(mine, theirs, name):
    _, r, c = mine.shape
    rows = r
    while rows * c > PAIR_SUM_BLOCK and rows % 32 == 0:
        rows //= 2

    def body(core_ref, a_ref, b_ref, o_ref):
        o_ref[0] = (a_ref[0].astype(F32) + b_ref[0].astype(F32)).astype(o_ref.dtype)

    return pl.pallas_call(
        body, name=name,
        grid_spec=pltpu.PrefetchScalarGridSpec(
            num_scalar_prefetch=1, grid=(N_CHIP, r // rows),
            in_specs=[pl.BlockSpec((1, rows, c), lambda k, i, core: (2 * k + core[0], i, 0)),
                      pl.BlockSpec((1, rows, c), lambda k, i, core: (k, i, 0))],
            out_specs=pl.BlockSpec((1, rows, c), lambda k, i, core: (k, i, 0))),
        out_shape=jax.ShapeDtypeStruct((N_CHIP, r, c), mine.dtype),
        compiler_params=_cparams(("parallel", "parallel")),
    )(lax.axis_index("c").astype(jnp.int32).reshape(1), mine, theirs)


WEIGHTS = ['norm_w', 'w_in', 's5_lambda_re', 's5_lambda_im', 's5_b_re', 's5_b_im', 's5_c_re', 's5_c_im', 's5_d',
           's5_log_step', 's5_w_glu', 'sgu_ln_w', 'sgu_ln_b', 'sgu_w', 'sgu_b', 'm2_conv_w', 'm2_conv_b', 'm2_dt_bias',
           'm2_a_log', 'm2_d', 'm2_norm_w', 'sc_conv_w', 'merge_b', 'w_branch', 'w_out', 'final_norm_w']
BIG_SHARDED = ['w_in', 'w_branch', 'w_out', 's5_w_glu']
SMALL_SHARDED = ['m2_conv_w', 'sc_conv_w', 'merge_b']
REPLICATED = [n for n in WEIGHTS if n not in BIG_SHARDED + SMALL_SHARDED]
S5_NAMES = ['s5_lambda_re', 's5_lambda_im', 's5_b_re', 's5_b_im', 's5_c_re', 's5_c_im', 's5_d', 's5_log_step']


def _sc_interleave(t):
    lead = t.shape[:-1]
    return jnp.swapaxes(t.reshape(lead + (4, 4, LANES)), -3, -2).reshape(lead + (4 * BW,))


def _pad_in(w):
    z = lambda n: jnp.zeros(w.shape[:-1] + (n,), w.dtype)
    return jnp.concatenate([w[..., 6152:], w[..., 0:1024], w[..., 3072:4096], w[..., 1024:2560], z(512),
                            w[..., 2560:3072], w[..., 4096:4104], z(504), _sc_interleave(w[..., 4104:6152])], axis=-1)


def _unpad_in(g):
    return jnp.concatenate([g[..., C_S5U:C_S5U + 1024], g[..., C_SGU_U:C_SGU_U + 1536], g[..., C_M2Z:C_M2Z + 512],
                            g[..., C_M2X:C_M2X + 1024], g[..., C_DT:C_DT + 8], _sc_interleave(g[..., C_SC:]),
                            g[..., :N_BRANCH * D_MODEL]], axis=-1)


ROW_BLOCK = 8 * LANES


def _pack_rows(tensors, row_mult, batched=False):
    parts = []
    for t in tensors:
        f = t.reshape((t.shape[0], -1) if batched else (1, -1))
        f = jnp.pad(f, ((0, 0), (0, (-f.shape[1]) % ROW_BLOCK)))
        parts.append(f.reshape(f.shape[0], -1, LANES))
    out = jnp.concatenate(parts, axis=1)
    out = jnp.pad(out, ((0, 0), (0, (-out.shape[1]) % row_mult), (0, 0)))
    return out if batched else out[0]


def _unpack_rows(rows, shapes):
    out, r0 = [], 0
    for shp in shapes:
        size = 1
        for s in shp:
            size *= s
        nr = -(-size // ROW_BLOCK) * 8
        out.append(rows[r0:r0 + nr].reshape(-1)[:size].reshape(shp))
        r0 += nr
    return out


def _kernel_col_map():
    m = np.full(IN_PAD, -1, np.int64)
    m[C_MERGE:C_MERGE + 4096] = np.arange(6152, 10248)
    m[C_S5U:C_S5U + 1024] = np.arange(0, 1024)
    m[C_M2X:C_M2X + 1024] = np.arange(3072, 4096)
    m[C_SGU_U:C_SGU_U + 1536] = np.arange(1024, 2560)
    m[C_M2Z:C_M2Z + 512] = np.arange(2560, 3072)
    m[C_DT:C_DT + 8] = np.arange(4096, 4104)
    for j in range(4):
        for kind in range(4):
            k0 = C_SC + 4 * LANES * j + LANES * kind
            m[k0:k0 + LANES] = 4104 + BW * kind + LANES * j + np.arange(LANES)
    return m


def _lane_pieces(sources):
    pieces, cur = [], None
    for lane, src in enumerate(sources):
        key = None if src is None else (src[0], src[1] // LANES, (lane - src[1]) % LANES)
        if cur is not None and key == cur[0]:
            cur[2] = lane + 1
        else:
            if cur is not None and cur[0] is not None:
                pieces.append((*cur[0], cur[1], cur[2]))
            cur = [key, lane, lane + 1]
    if cur is not None and cur[0] is not None:
        pieces.append((*cur[0], cur[1], cur[2]))
    return pieces


def _assemble_block(pieces, load, rows, dtype):
    lane = lax.broadcasted_iota(jnp.int32, (rows, LANES), 1)
    out = None
    for arr, sb, shift, lo, hi in pieces:
        v = load(arr, sb)
        if shift:
            v = pltpu.roll(v, shift, 1)
        if out is None and lo == 0 and hi == LANES:
            out = v
        else:
            out = jnp.where((lane >= lo) & (lane < hi), v, jnp.zeros((rows, LANES), dtype) if out is None else out)
    return jnp.zeros((rows, LANES), dtype) if out is None else out


RELAYOUT_ROWS = 256
SHARD_BLOCKS = -(-SHARD_IN // LANES)


def _load_shard_block(ref, rows):
    def load(j, sb):
        if sb == SHARD_BLOCKS - 1:
            return jnp.broadcast_to(ref[j, :, SHARD_IN - 1:SHARD_IN], (rows, LANES))
        return ref[j, :, sb * LANES:(sb + 1) * LANES]
    return load


def _relayout_w_in(gathered, name):
    kmap = _kernel_col_map()
    dtype = gathered.dtype

    def body(src_ref, o_ref):
        load = _load_shard_block(src_ref, RELAYOUT_ROWS)
        for ob in range(IN_PAD // LANES):
            srcs = [None if kmap[ob * LANES + l] < 0 else (int(kmap[ob * LANES + l]) // SHARD_IN, int(kmap[ob * LANES + l]) % SHARD_IN)
                    for l in range(LANES)]
            o_ref[:, ob * LANES:(ob + 1) * LANES] = _assemble_block(_lane_pieces(srcs), load, RELAYOUT_ROWS, dtype)

    return pl.pallas_call(
        body, name=name, grid=(D_MODEL // RELAYOUT_ROWS,),
        in_specs=[pl.BlockSpec((N_DEV, RELAYOUT_ROWS, SHARD_IN), lambda i: (0, i, 0))],
        out_specs=pl.BlockSpec((RELAYOUT_ROWS, IN_PAD), lambda i: (i, 0)),
        out_shape=jax.ShapeDtypeStruct((D_MODEL, IN_PAD), dtype),
        compiler_params=_cparams(("parallel",)),
    )(gathered)


def _relayout_g_in(gw, name):
    kmap = _kernel_col_map()
    kinv = np.zeros(IN_DIM, np.int64)
    kinv[kmap[kmap >= 0]] = np.nonzero(kmap >= 0)[0]
    dtype = gw.dtype

    def body(src_ref, o_ref):
        load = lambda _, sb: src_ref[:, sb * LANES:(sb + 1) * LANES]
        for j in range(N_DEV):
            for ob in range(SHARD_BLOCKS):
                srcs = [(0, int(kinv[SHARD_IN * j + ob * LANES + l])) if ob * LANES + l < SHARD_IN else None for l in range(LANES)]
                blk = _assemble_block(_lane_pieces(srcs), load, RELAYOUT_ROWS, dtype)
                if ob == SHARD_BLOCKS - 1:
                    o_ref[j, :, SHARD_IN - 1:SHARD_IN] = blk[:, 0:1]
                else:
                    o_ref[j, :, ob * LANES:(ob + 1) * LANES] = blk

    return pl.pallas_call(
        body, name=name, grid=(D_MODEL // RELAYOUT_ROWS,),
        in_specs=[pl.BlockSpec((RELAYOUT_ROWS, IN_PAD), lambda i: (i, 0))],
        out_specs=pl.BlockSpec((N_DEV, RELAYOUT_ROWS, SHARD_IN), lambda i: (0, i, 0)),
        out_shape=jax.ShapeDtypeStruct((N_DEV, D_MODEL, SHARD_IN), dtype),
        compiler_params=_cparams(("parallel",)),
    )(gw)


def _rows128(flat, row_mult=8):
    n = flat.shape[0]
    per = LANES * row_mult
    total = -(-n // per) * per
    return jnp.pad(flat, (0, total - n)).reshape(total // LANES, LANES)


def _pad_lanes(v):
    return jnp.pad(v, (0, LANES - v.shape[0])).reshape(1, LANES)


def _layer_prep(i, p):
    disc, disc_vjp = jax.vjp(_s5_disc, *[p[n][i] for n in S5_NAMES])
    prep = dict(
        nw=p['norm_w'][i].reshape(1, D_MODEL), disc_vjp=disc_vjp,
        s5small=[_block_diag(t).astype(BF16) for t in disc[:4]] + [disc[4], disc[5]],
        sgw=[p['sgu_ln_w'][i].reshape(1, BW), p['sgu_ln_b'][i].reshape(1, BW), p['sgu_w'][i],
             jnp.repeat(p['sgu_b'][i].T, BW // SGU_HEADS, axis=1)],
        cb=p['m2_conv_b'][i].reshape(1, M2_CONV_CH),
        m2w=[_pad_lanes(p['m2_dt_bias'][i]), _pad_lanes(p['m2_a_log'][i]),
             jnp.repeat(p['m2_d'][i], M2_HEAD_DIM).reshape(1, BW), p['m2_norm_w'][i].reshape(1, BW)])
    touch = [t[0, 0].astype(F32) for t in prep['s5small']] + [prep['sgw'][3][0, 0], prep['m2w'][2][0, 0]]
    return prep, sum(touch[1:], touch[0])


def _layer_fwd(x, h, i, prep, w_in, other_weights, before_merge=None):
    proj = _matmul(h, w_in, 1, 0, F32, 1024, 1024, 1024, f"proj{i}")
    full = dict(other_weights(proj), w_in=w_in)
    s5w = prep['s5small'] + [full['s5_w_glu']]
    ya, sre, sim = _s5_fwd(proj, *s5w, f"s5_fwd{i}")
    yb = _sgu_fwd(proj, *prep['sgw'], f"sgu_fwd{i}")
    cw = full['m2_conv_w']
    xa = _m2_conv_fwd(proj, cw, prep['cb'], f"m2conv_fwd{i}")
    yc, s_in = _ssd_fwd(proj, xa, *prep['m2w'], f"ssd_fwd{i}")
    scw = full['sc_conv_w']
    yd = _sc_fwd(proj, scw, f"sc_fwd{i}")
    ys = jnp.stack([ya, yb, yc, yd])
    mb = full['merge_b'].reshape(N_BRANCH, 1, D_MODEL)
    if before_merge is not None:
        mb = mb + before_merge(ys)[0, 0]
    merged = _merge_fwd(proj, ys, mb, full['w_branch'], f"merge_fwd{i}")
    x_new = _matmul(merged, full['w_out'], 1, 0, F32, 1024, 1024, 1024, f"out{i}", residual=x)
    saved = dict(x=x, nw=prep['nw'], h=h, proj=proj, disc_vjp=prep['disc_vjp'], s5w=s5w, sre=sre, sim=sim, sgw=prep['sgw'],
                 cw=cw, cb=prep['cb'], xa=xa, m2w=prep['m2w'], s_in=s_in, scw=scw, ys=ys, mb=mb, merged=merged)
    return x_new, saved, full


def _layer_bwd(dx_out, i, sv, full, on_other_grads=None, after_sgu=None, on_large_grads=None, after_dh=None):
    g = {}
    proj = sv['proj']
    dm = _matmul(dx_out, full['w_out'], 1, 1, F32, 1024, 1024, 1024, f"dmerged{i}")
    g['w_out'] = _matmul(sv['merged'], dx_out, 0, 0, BF16, 1024, 1024, 1024, f"gw_out{i}")
    dys, dproj, g['w_branch'], dmb = _merge_bwd(proj, sv['ys'], dm, sv['mb'], full['w_branch'], f"merge_bwd{i}")
    g['merge_b'] = dmb.reshape(N_BRANCH, D_MODEL)
    dproj, dbbre, dbbim, dcre, dcim, da, dd, dwg = _s5_bwd(proj, dproj, dys, sv['sre'], sv['sim'], *sv['s5w'], f"s5_bwd{i}")
    g['s5_dense'] = (dbbre, dbbim, dcre, dcim, da, dd)
    g['s5_w_glu'] = dwg.astype(BF16)
    sgw, m2w = list(sv['sgw']), list(sv['m2w'])
    if on_other_grads is not None:
        sgw[0] = sgw[0] + on_other_grads(g)[0, 0]
    dproj, dlw, dlb, g['sgu_w'], dbias = _sgu_bwd(proj, dproj, dys, *sgw, f"sgu_bwd{i}")
    g['sgu_ln_w'], g['sgu_ln_b'] = dlw[0], dlb[0]
    g['sgu_b'] = dbias.reshape(SGU_CHUNK, SGU_HEADS, BW // SGU_HEADS).sum(-1).T
    if after_sgu is not None:
        m2w[0] = m2w[0] + after_sgu(dlw)[0, 0]
    dproj, dxa, ddtb, dal, ddf, dnw = _ssd_bwd(proj, dproj, sv['xa'], dys, sv['s_in'], *m2w, f"ssd_bwd{i}")
    dproj, g['m2_conv_w'], dcb = _m2_conv_bwd(proj, dproj, dxa, sv['cw'], sv['cb'], f"m2conv_bwd{i}")
    g['m2_conv_b'], g['m2_norm_w'] = dcb[0], dnw[0]
    g['m2_dt_bias'], g['m2_a_log'] = ddtb[0, :M2_HEADS], dal[0, :M2_HEADS]
    g['m2_d'] = ddf.reshape(M2_HEADS, M2_HEAD_DIM).sum(-1)
    dproj, g['sc_conv_w'] = _sc_bwd(proj, dproj, dys, sv['scw'], f"sc_bwd{i}")
    g['w_in'] = _matmul(sv['h'], dproj, 0, 0, BF16, 1024, 1024, 1024, f"gw_in{i}")
    tok = on_large_grads(g) if on_large_grads else None
    dh = _matmul(dproj, full['w_in'], 1, 1, F32, 1024, 1024, 1024, f"dh{i}", after=tok)
    nw = sv['nw'] if after_dh is None else sv['nw'] + after_dh(dh)[0, 0]
    dx_in, dnw_l = _rmsnorm_bwd(sv['x'], nw, dh, dx_out, f"rms_bwd{i}")
    g['norm_w'] = dnw_l[0]
    return dx_in, g


def _split8(t, axis):
    shp = t.shape
    t = t.reshape(shp[:axis] + (N_DEV, shp[axis] // N_DEV) + shp[axis + 1:])
    return jnp.moveaxis(t, axis, 0)


def _join8(t, axis):
    t = jnp.moveaxis(t, 0, axis)
    shp = t.shape
    return t.reshape(shp[:axis] + (shp[axis] * shp[axis + 1],) + shp[axis + 2:])


SHARD_AXIS = {'w_in': 2, 'w_branch': 3, 'w_out': 1, 's5_w_glu': 1, 'm2_conv_w': 2, 'sc_conv_w': 2, 'merge_b': 2}


OTHER_BIG = [n for n in BIG_SHARDED if n != 'w_in']


def _other_weights(gathered):
    return {n: _join8(t, SHARD_AXIS[n] - 1) for n, t in zip(OTHER_BIG, gathered)}


def _layer_grad_blocks(g, i):
    blocks = [_relayout_g_in(g[n], f"relayout_g_in{i}") if n == 'w_in' else _split8(g[n], SHARD_AXIS[n] - 1) for n in BIG_SHARDED]
    return [b.reshape(N_DEV, -1, b.shape[-1]) for b in blocks]


def _pair_start(blocks, tag):
    shapes = [(N_CHIP,) + b.shape[1:] for b in blocks]
    return _split_start(_plan_pair, blocks, shapes, N_CHIP * len(blocks), f"pair{tag}_start")


def _pair_sums(state, after, tag):
    mine, theirs = _split_wait(_plan_pair, state, after, f"pair{tag}_wait", with_sources=True)
    return [_pair_sum(b, t, f"pair_sum{tag}_{k}") for k, (b, t) in enumerate(zip(mine, theirs))]


def _chips_start(sums, tag, after=None):
    return _split_start(_plan_chips, sums, [s.shape for s in sums], 3 * len(sums), f"chips{tag}_start", after)


def kernel(x, norm_w, w_in, s5_lambda_re, s5_lambda_im, s5_b_re, s5_b_im, s5_c_re, s5_c_im, s5_d, s5_log_step, s5_w_glu, sgu_ln_w, sgu_ln_b, sgu_w, sgu_b, m2_conv_w, m2_conv_b, m2_dt_bias, m2_a_log, m2_d, m2_norm_w, sc_conv_w, merge_b, w_branch, w_out, final_norm_w, loss_target, m_norm_w, m_w_in, m_s5_lambda_re, m_s5_lambda_im, m_s5_b_re, m_s5_b_im, m_s5_c_re, m_s5_c_im, m_s5_d, m_s5_log_step, m_s5_w_glu, m_sgu_ln_w, m_sgu_ln_b, m_sgu_w, m_sgu_b, m_m2_conv_w, m_m2_conv_b, m_m2_dt_bias, m_m2_a_log, m_m2_d, m_m2_norm_w, m_sc_conv_w, m_merge_b, m_w_branch, m_w_out, m_final_norm_w, v_norm_w, v_w_in, v_s5_lambda_re, v_s5_lambda_im, v_s5_b_re, v_s5_b_im, v_s5_c_re, v_s5_c_im, v_s5_d, v_s5_log_step, v_s5_w_glu, v_sgu_ln_w, v_sgu_ln_b, v_sgu_w, v_sgu_b, v_m2_conv_w, v_m2_conv_b, v_m2_dt_bias, v_m2_a_log, v_m2_d, v_m2_norm_w, v_sc_conv_w, v_merge_b, v_w_branch, v_w_out, v_final_norm_w):
    loc = locals()
    p = {n: loc[n] for n in WEIGHTS}
    mom = {n: loc['m_' + n] for n in WEIGHTS}
    vel = {n: loc['v_' + n] for n in WEIGHTS}

    small_sizes = [p[n].size for n in SMALL_SHARDED]
    small_pack = _rows128(jnp.concatenate([p[n].reshape(-1) for n in SMALL_SHARDED]))
    shards = ([p['w_in'][0].astype(BF16)] + [p[n][0].astype(BF16) for n in OTHER_BIG] + [small_pack]
              + [p[n][1].astype(BF16) for n in BIG_SHARDED])
    gath, tok = _split_start(_plan_gather, shards, [(N_DEV,) + t.shape for t in shards], 7 * len(shards), "gather_start")
    sems, srcs, lands = gath[:3], gath[3:3 + len(shards)], gath[3 + len(shards):]

    def relayed(lo, hi, after, name):
        plan = functools.partial(_plan_gather, first=lo)
        state, tok = _split_relay(plan, (*sems, *srcs[lo:hi], *lands[lo:hi]), after, name + "_relay")
        return (plan, state, name), tok

    def arrived(relay, after):
        plan, state, name = relay
        return _split_wait(plan, state, after, name + "_wait")

    def gathered(lo, hi, after, name):
        relay, tok = relayed(lo, hi, after, name)
        return arrived(relay, tok)

    later = dict(p, **{n: p[n] + tok[0, 0] for n in ('norm_w', 's5_log_step', 'sgu_b', 'm2_d')})
    preps = [_layer_prep(i, later) for i in range(DEPTH)]
    h0 = _rmsnorm_fwd(x[0], preps[0][0]['nw'], "rms_fwd0")
    got = gathered(0, 1, tok + (preps[0][1] + preps[1][1] + h0[0, 0].astype(F32)), "gather_w_in0")
    small_full = {}

    def other_weights0(proj):
        got = gathered(1, 5, proj, "gather_rest0")
        small_all, off = got[-1].reshape(N_DEV, -1), 0
        for n, sz in zip(SMALL_SHARDED, small_sizes):
            small_full[n] = _join8(small_all[:, off:off + sz].reshape((N_DEV,) + p[n].shape), SHARD_AXIS[n])
            off += sz
        return dict(_other_weights(got[:-1]), **{n: small_full[n][0] for n in SMALL_SHARDED})

    saved, layer_g, full = [None] * DEPTH, [None] * DEPTH, [None] * DEPTH
    relay1 = []

    def relay_layer1(ys):
        relay, tok = relayed(5, 9, ys, "gather1")
        relay1.append(relay)
        return tok

    xs, saved[0], full[0] = _layer_fwd(x[0], h0, 0, preps[0][0], _relayout_w_in(got[0], "relayout_w_in0"), other_weights0,
                                       relay_layer1)
    h1 = _rmsnorm_fwd(xs, preps[1][0]['nw'], "rms_fwd1")
    got = arrived(relay1[0], h1)
    xs, saved[1], full[1] = _layer_fwd(
        xs, h1, 1, preps[1][0], _relayout_w_in(got[0], "relayout_w_in1"),
        lambda proj: dict(_other_weights(got[1:]), **{n: small_full[n][1] for n in SMALL_SHARDED}))
    loss_row, dx, dfw = _loss_head(xs, final_norm_w.reshape(1, D_MODEL), loss_target[0])
    loss = lax.psum(loss_row[0, 0], ("x", "y", "c"))
    loss, dx = lax.optimization_barrier((loss, dx))
    pairs, scat, sent0 = [None] * DEPTH, [None] * DEPTH, []

    def start_pairs1(g):
        pairs[1], tok = _pair_start(_layer_grad_blocks(g, 1), 1)
        return tok

    def send_chip_sums1(dh):
        scat[1], tok = _chips_start(_pair_sums(pairs[1], dh, 1), 1)
        return tok

    def send_all0(g):
        pairs[0], tok = _pair_start(_layer_grad_blocks(g, 0), 0)
        scat[0], tok = _chips_start(_pair_sums(pairs[0], tok, 0), 0)
        sent0.append(tok)
        return tok

    dx, layer_g[1] = _layer_bwd(dx, 1, saved[1], full[1], on_large_grads=start_pairs1, after_dh=send_chip_sums1)
    dx, layer_g[0] = _layer_bwd(dx, 0, saved[0], full[0], on_large_grads=send_all0)
    for i in range(DEPTH):
        dense = layer_g[i].pop('s5_dense')
        blocks = tuple(_diag_blocks(t, after=sent0[0]) for t in dense[:4])
        layer_g[i].update(zip(S5_NAMES, saved[i]['disc_vjp'](blocks + (dense[4] + sent0[0][0, 0], dense[5]))))
    grads = {n: jnp.stack([layer_g[i][n] for i in range(DEPTH)]) for n in SMALL_SHARDED + REPLICATED if n != 'final_norm_w'}
    grads['final_norm_w'] = dfw[0]

    out_g, out_d, out_m, out_v = {}, {}, {}, {}
    repl_rows = _pack_rows([grads[n] for n in REPLICATED], 8 * N_DEV)
    rr = repl_rows.shape[0] // N_DEV
    shard_rows = _pack_rows([_split8(grads[n], SHARD_AXIS[n]) for n in SMALL_SHARDED], 8, batched=True)
    rs = shard_rows.shape[1]
    small_g = jnp.concatenate([shard_rows, repl_rows.reshape(N_DEV, rr, LANES)], axis=1)
    all_to_all, all_gather = functools.partial(_plan_direct, gather=False), functools.partial(_plan_direct, gather=True)
    small_state, tok = _split_start(all_to_all, [small_g], [small_g.shape], N_DEV - 1, "scatter_small_start")
    landed1 = _split_wait(_plan_chips, scat[1], tok, "chips1_wait")
    landed0 = _split_wait(_plan_chips, scat[0], landed1[0], "chips0_wait")

    def big_adamw(n, after=None):
        k, shp = BIG_SHARDED.index(n), p[n].shape
        if n == 'w_in':
            return _adamw_w_in([landed0[k], landed1[k]], p[n], mom[n], vel[n], "adamw_w_in", after)
        c = shp[-1]
        r = p[n].size // (DEPTH * c)
        res = _adamw([landed0[k], landed1[k]], *[d[n].reshape(DEPTH, r, c) for d in (p, mom, vel)],
                     {'w_branch': 512, 'w_out': 128, 's5_w_glu': 64}[n], "adamw_" + n)
        return [o.reshape(shp) for o in res]

    for n in OTHER_BIG:
        out_g[n], out_d[n], out_m[n], out_v[n] = big_adamw(n)
    updated = sum(out_d[n].reshape(-1)[0] for n in OTHER_BIG).reshape(1, 1)
    small_sum = _slot_sum(_split_wait(all_to_all, small_state, updated, "scatter_small_wait")[0], "sum_small")
    repl_part = small_sum[rs:]
    repl_state, tok = _split_start(all_gather, [repl_part], [(N_DEV,) + repl_part.shape], N_DEV - 1, "gather_small_start")
    out_g['w_in'], out_d['w_in'], out_m['w_in'], out_v['w_in'] = big_adamw('w_in', tok)
    repl_all = _split_wait(all_gather, repl_state, out_d['w_in'], "gather_small_wait")[0].reshape(N_DEV * rr, LANES)
    g_all = jnp.concatenate([small_sum[:rs], repl_all], axis=0)
    names = SMALL_SHARDED + REPLICATED
    pieces = (_unpack_rows(g_all[:rs], [p[n].shape for n in SMALL_SHARDED])
              + _unpack_rows(g_all[rs:], [p[n].shape for n in REPLICATED]))
    out_g.update(zip(names, pieces))
    res = _adamw_many(*[[_memory_view(n, d[n]) for n in names] for d in (out_g, p, mom, vel)], "adamw_small")
    for r, dst in zip(res, (out_d, out_m, out_v)):
        dst.update({n: _memory_view(n, t) for n, t in zip(names, r)})
    return (loss, dx[None], *[out_g[n] for n in WEIGHTS], *[out_d[n] for n in WEIGHTS],
            *[out_m[n] for n in WEIGHTS], *[out_v[n] for n in WEIGHTS])
```

```python
import functools

import jax
import jax.numpy as jnp
import numpy as np
from jax import lax
from jax.experimental import pallas as pl
from jax.experimental.pallas import tpu as pltpu

F32 = jnp.float32
BF16 = jnp.bfloat16

N_DEV = 8
SEQ = 2048
D_MODEL = 1024
DEPTH = 2
BW = 512
N_BRANCH = 4
EPS = 1e-6
S5_GROUPS, S5_STATE, S5_P = 32, 64, 16
S5_CH = S5_GROUPS * S5_STATE
SGU_CHUNK, SGU_HEADS = 128, 8
M2_HEADS, M2_HEAD_DIM, M2_STATE, M2_CHUNK, M2_CONV = 8, 64, 128, 128, 4
M2_CONV_CH = 1024
SC_CONV = 3
IN_DIM = 10248
IN_PAD = 11264
C_MERGE = 0
C_S5U, C_S5G = 4096, 4608
C_M2X = 5120
C_SGU_U, C_SGU_V, C_SGU_G = 6144, 6656, 7168
C_M2Z, C_DT = 8192, 8704
C_SC = 9216
SHARD_IN = IN_DIM // N_DEV

ADAM_LR, ADAM_B1, ADAM_B2, ADAM_EPS, ADAM_WD, ADAM_STEP = 0.001, 0.9, 0.999, 1e-08, 0.01, 10

VMEM_LIMIT = 56 * 1024 * 1024
LANES = 128

MESH = pl.DeviceIdType.MESH


def _cparams(sem=None, **kw):
    return pltpu.CompilerParams(dimension_semantics=sem, vmem_limit_bytes=VMEM_LIMIT, **kw)


def _dg(a, b, ca, cb, precision=None):
    return lax.dot_general(a, b, (((ca,), (cb,)), ((), ())), precision=precision,
                           preferred_element_type=F32)


@functools.partial(jax.custom_vjp, nondiff_argnums=(2, 3))
def _bdot(a, b, ca, cb):
    return _dg(a.astype(BF16), b.astype(BF16), ca, cb)


def _bdot_fwd(a, b, ca, cb):
    return _bdot(a, b, ca, cb), (a, b)


def _bdot_bwd(ca, cb, res, g):
    a, b = res
    gb, ab, bb = g.astype(BF16), a.astype(BF16), b.astype(BF16)
    da = _dg(gb, bb, 1, 1 - cb) if ca == 1 else _dg(bb, gb, 1 - cb, 1)
    db = _dg(ab, gb, 1 - ca, 0) if cb == 0 else _dg(gb, ab, 0, 1 - ca)
    return da.astype(a.dtype), db.astype(b.dtype)


_bdot.defvjp(_bdot_fwd, _bdot_bwd)


def _rms(x, w):
    return x * lax.rsqrt(jnp.mean(x * x, axis=-1, keepdims=True) + EPS) * w


def _silu(x):
    return x * jax.nn.sigmoid(x)


def _gelu(x):
    return 0.5 * x * (1.0 + jnp.tanh(0.7978845608028654 * (x + 0.044715 * (x * x * x))))


def _softplus(x):
    return jnp.maximum(x, 0.0) + jnp.log1p(jnp.exp(-jnp.abs(x)))


def _shift_down(x, s):
    if s == 0:
        return x
    row = lax.broadcasted_iota(jnp.int32, x.shape, 0)
    return jnp.where(row >= s, pltpu.roll(x, s, 0), 0.0)


def _shift_up(x, s):
    if s == 0:
        return x
    n = x.shape[0]
    row = lax.broadcasted_iota(jnp.int32, x.shape, 0)
    return jnp.where(row < n - s, pltpu.roll(x, n - s, 0), 0.0)


def _matmul(a, b, ca, cb, out_dtype, tm, tn, tk, name, residual=None, after=None):
    m = a.shape[1 - ca]
    k = a.shape[ca]
    n = b.shape[1 - cb]
    assert b.shape[cb] == k and m % tm == 0 and n % tn == 0 and k % tk == 0
    nk = k // tk
    a_spec = pl.BlockSpec((tm, tk), lambda i, j, kk: (i, kk)) if ca == 1 else pl.BlockSpec((tk, tm), lambda i, j, kk: (kk, i))
    b_spec = pl.BlockSpec((tk, tn), lambda i, j, kk: (kk, j)) if cb == 0 else pl.BlockSpec((tn, tk), lambda i, j, kk: (j, kk))
    o_spec = pl.BlockSpec((tm, tn), lambda i, j, kk: (i, j))
    has_res = residual is not None

    def body(*refs):
        refs = refs[:2 + has_res] + refs[2 + has_res + (after is not None):]
        if has_res:
            a_ref, b_ref, r_ref, o_ref, acc = refs
        else:
            a_ref, b_ref, o_ref, acc = refs
        kk = pl.program_id(2)
        part = _dg(a_ref[...].astype(BF16), b_ref[...].astype(BF16), ca, cb)

        @pl.when(kk == 0)
        def _():
            acc[...] = part

        @pl.when(kk > 0)
        def _():
            acc[...] += part

        @pl.when(kk == nk - 1)
        def _():
            r = acc[...]
            if has_res:
                r = r + r_ref[...]
            o_ref[...] = r.astype(out_dtype)

    ins = [a, b] + ([residual] if has_res else []) + ([after] if after is not None else [])
    specs = [a_spec, b_spec] + ([o_spec] if has_res else []) + ([pl.BlockSpec(memory_space=pl.ANY)] if after is not None else [])
    return pl.pallas_call(
        body, name=name, grid=(m // tm, n // tn, nk), in_specs=specs, out_specs=o_spec,
        out_shape=jax.ShapeDtypeStruct((m, n), out_dtype),
        scratch_shapes=[pltpu.VMEM((tm, tn), F32)],
        compiler_params=_cparams(("parallel", "parallel", "arbitrary")),
    )(*ins)


ROW_TILE = 512


def _rmsnorm_fwd(x, w, name):
    def body(x_ref, w_ref, o_ref):
        o_ref[...] = _rms(x_ref[...], w_ref[...]).astype(BF16)

    return pl.pallas_call(
        body, name=name, grid=(SEQ // ROW_TILE,),
        in_specs=[pl.BlockSpec((ROW_TILE, D_MODEL), lambda i: (i, 0)), pl.BlockSpec((1, D_MODEL), lambda i: (0, 0))],
        out_specs=pl.BlockSpec((ROW_TILE, D_MODEL), lambda i: (i, 0)),
        out_shape=jax.ShapeDtypeStruct((SEQ, D_MODEL), BF16),
        compiler_params=_cparams(("parallel",)),
    )(x, w)


def _rmsnorm_bwd(x, w, dh, dres, name):
    def body(x_ref, w_ref, dh_ref, dres_ref, dx_ref, dw_ref):
        _, vjp = jax.vjp(_rms, x_ref[...], w_ref[...])
        dx, dw = vjp(dh_ref[...])
        dx_ref[...] = dx + dres_ref[...]

        @pl.when(pl.program_id(0) == 0)
        def _():
            dw_ref[...] = dw

        @pl.when(pl.program_id(0) > 0)
        def _():
            dw_ref[...] += dw

    tile = pl.BlockSpec((ROW_TILE, D_MODEL), lambda i: (i, 0))
    vec = pl.BlockSpec((1, D_MODEL), lambda i: (0, 0))
    return pl.pallas_call(
        body, name=name, grid=(SEQ // ROW_TILE,),
        in_specs=[tile, vec, tile, tile], out_specs=[tile, vec],
        out_shape=[jax.ShapeDtypeStruct((SEQ, D_MODEL), F32), jax.ShapeDtypeStruct((1, D_MODEL), F32)],
        compiler_params=_cparams(("arbitrary",)),
    )(x, w, dh, dres)


def _loss_head(x, w, target):
    def body(x_ref, w_ref, t_ref, loss_ref, dx_ref, dw_ref):
        tgt = t_ref[...]

        def f(xv, wv):
            err = _rms(xv, wv) - tgt
            return 0.5 * jnp.sum(jnp.mean(err * err, axis=-1))

        loss, vjp = jax.vjp(f, x_ref[...], w_ref[...])
        dx, dw = vjp(jnp.ones((), F32))
        dx_ref[...] = dx
        lrow = jnp.full((1, LANES), loss, F32)

        @pl.when(pl.program_id(0) == 0)
        def _():
            dw_ref[...] = dw
            loss_ref[...] = lrow

        @pl.when(pl.program_id(0) > 0)
        def _():
            dw_ref[...] += dw
            loss_ref[...] += lrow

    tile = pl.BlockSpec((ROW_TILE, D_MODEL), lambda i: (i, 0))
    vec = pl.BlockSpec((1, D_MODEL), lambda i: (0, 0))
    return pl.pallas_call(
        body, name="loss_head", grid=(SEQ // ROW_TILE,),
        in_specs=[tile, vec, tile], out_specs=[pl.BlockSpec((1, LANES), lambda i: (0, 0)), tile, vec],
        out_shape=[jax.ShapeDtypeStruct((1, LANES), F32), jax.ShapeDtypeStruct((SEQ, D_MODEL), F32),
                   jax.ShapeDtypeStruct((1, D_MODEL), F32)],
        compiler_params=_cparams(("arbitrary",)),
    )(x, w, target)


S5_T = 256
S5_BLOCKS = [(slice(j * 256, (j + 1) * 256), slice(j * 1024, (j + 1) * 1024)) for j in range(2)]


def _s5_post(ypre, gate, wglu):
    y = _gelu(ypre)
    y = y * jax.nn.sigmoid(_bdot(y, wglu, 1, 0))
    return y * _silu(gate)


def _s5_fwd(proj, bbre, bbim, cre, cim, a2, dvec, wglu, name):
    def body(u_ref, g_ref, bbre_ref, bbim_ref, cre_ref, cim_ref, a_ref, d_ref, wg_ref, o_ref, sre_ref, sim_ref, st):
        @pl.when(pl.program_id(0) == 0)
        def _():
            st[...] = jnp.zeros_like(st)

        u = u_ref[...]
        ub = u.astype(BF16)
        for us, ss in S5_BLOCKS:
            sre_ref[:, ss] = _dg(ub[:, us], bbre_ref[us, ss], 1, 0)
            sim_ref[:, ss] = _dg(ub[:, us], bbim_ref[us, ss], 1, 0)
        ar, ai = a_ref[0:1, :], a_ref[1:2, :]

        def step(t, carry):
            sr, si = carry
            nr = ar * sr - ai * si + sre_ref[pl.ds(t, 1), :]
            ni = ar * si + ai * sr + sim_ref[pl.ds(t, 1), :]
            sre_ref[pl.ds(t, 1), :] = nr
            sim_ref[pl.ds(t, 1), :] = ni
            return nr, ni

        sr, si = lax.fori_loop(0, S5_T, step, (st[0:1, :], st[1:2, :]), unroll=8)
        st[0:1, :] = sr
        st[1:2, :] = si
        ypre = jnp.concatenate(
            [_dg(sre_ref[:, ss].astype(BF16), cre_ref[ss, us], 1, 0) - _dg(sim_ref[:, ss].astype(BF16), cim_ref[ss, us], 1, 0)
             for us, ss in S5_BLOCKS], axis=1) + d_ref[...] * u
        o_ref[...] = _s5_post(ypre, g_ref[...], wg_ref[...]).astype(BF16)

    full = lambda shape: pl.BlockSpec(shape, lambda c: (0, 0))
    return pl.pallas_call(
        body, name=name, grid=(SEQ // S5_T,),
        in_specs=[pl.BlockSpec((S5_T, BW), lambda c: (c, C_S5U // BW)), pl.BlockSpec((S5_T, BW), lambda c: (c, C_S5G // BW)),
                  full((BW, S5_CH)), full((BW, S5_CH)), full((S5_CH, BW)), full((S5_CH, BW)),
                  full((2, S5_CH)), full((1, BW)), full((BW, BW))],
        out_specs=[pl.BlockSpec((S5_T, BW), lambda c: (c, 0)), pl.BlockSpec((S5_T, S5_CH), lambda c: (c, 0)),
                   pl.BlockSpec((S5_T, S5_CH), lambda c: (c, 0))],
        out_shape=[jax.ShapeDtypeStruct((SEQ, BW), BF16), jax.ShapeDtypeStruct((SEQ, S5_CH), F32),
                   jax.ShapeDtypeStruct((SEQ, S5_CH), F32)],
        scratch_shapes=[pltpu.VMEM((2, S5_CH), F32)],
        compiler_params=_cparams(("arbitrary",)),
    )(proj, proj, bbre, bbim, cre, cim, a2, dvec, wglu)


def _s5_bwd(proj, dproj, dout, sre, sim, bbre, bbim, cre, cim, a2, dvec, wglu, name):
    nc = SEQ // S5_T

    def body(u_ref, g_ref, do_ref, sre_ref, sim_ref, pre_ref, pim_ref, bbre_ref, bbim_ref, cre_ref, cim_ref, a_ref,
             d_ref, wg_ref, dproj_in, dp_ref, dbbre_ref, dbbim_ref, dcre_ref, dcim_ref, da_ref, dd_ref, dwg_ref,
             gre, gim, st):
        c = nc - 1 - pl.program_id(0)

        @pl.when(pl.program_id(0) == 0)
        def _():
            st[...] = jnp.zeros_like(st)
            for r in (dbbre_ref, dbbim_ref, dcre_ref, dcim_ref, da_ref, dd_ref, dwg_ref):
                r[...] = jnp.zeros_like(r)

        u = u_ref[...]
        s_re, s_im = sre_ref[...], sim_ref[...]

        def head(s_res, s_ims, cres, cims, dv, uv, gv, wg):
            ypre = jnp.concatenate([_bdot(sr, cr, 1, 0) - _bdot(si, ci, 1, 0)
                                    for sr, si, cr, ci in zip(s_res, s_ims, cres, cims)], axis=1) + dv * uv
            return _s5_post(ypre, gv, wg)

        _, vjp = jax.vjp(head, [sre_ref[:, ss] for _, ss in S5_BLOCKS], [sim_ref[:, ss] for _, ss in S5_BLOCKS],
                         [cre_ref[ss, us].astype(F32) for us, ss in S5_BLOCKS],
                         [cim_ref[ss, us].astype(F32) for us, ss in S5_BLOCKS],
                         d_ref[...], u, g_ref[...], wg_ref[...].astype(F32))
        ds_res, ds_ims, dcres, dcims, dd, du_d, dgate, dwg = vjp(do_ref[0])
        for k, (us, ss) in enumerate(S5_BLOCKS):
            dcre_ref[ss, us] += dcres[k]
            dcim_ref[ss, us] += dcims[k]
            gre[:, ss] = ds_res[k]
            gim[:, ss] = ds_ims[k]
        dd_ref[...] += dd
        dwg_ref[...] += dwg
        dp_ref[:, BW:] = dgate.astype(BF16)
        ar, ai = a_ref[0:1, :], a_ref[1:2, :]

        def step(i, carry):
            t = S5_T - 1 - i
            gr, gi = carry
            nr = gre[pl.ds(t, 1), :] + gr
            ni = gim[pl.ds(t, 1), :] + gi
            gre[pl.ds(t, 1), :] = nr
            gim[pl.ds(t, 1), :] = ni
            return ar * nr + ai * ni, ar * ni - ai * nr

        gr, gi = lax.fori_loop(0, S5_T, step, (st[0:1, :], st[1:2, :]), unroll=8)
        st[0:1, :] = gr
        st[1:2, :] = gi
        g_re, g_im = gre[...], gim[...]
        first = jnp.where(c > 0, 1.0, 0.0)
        row = lax.broadcasted_iota(jnp.int32, (S5_T, S5_CH), 0)
        p_re = jnp.where(row == 0, pre_ref[7:8, :] * first, pltpu.roll(s_re, 1, 0))
        p_im = jnp.where(row == 0, pim_ref[7:8, :] * first, pltpu.roll(s_im, 1, 0))
        da_ref[0:1, :] += jnp.sum(g_re * p_re + g_im * p_im, axis=0, keepdims=True)
        da_ref[1:2, :] += jnp.sum(g_im * p_re - g_re * p_im, axis=0, keepdims=True)
        ub, grb, gib = u.astype(BF16), g_re.astype(BF16), g_im.astype(BF16)
        du_s = []
        for us, ss in S5_BLOCKS:
            dbbre_ref[us, ss] += _dg(ub[:, us], grb[:, ss], 0, 0)
            dbbim_ref[us, ss] += _dg(ub[:, us], gib[:, ss], 0, 0)
            du_s.append(_dg(grb[:, ss], bbre_ref[us, ss], 1, 1) + _dg(gib[:, ss], bbim_ref[us, ss], 1, 1))
        dp_ref[:, :BW] = (du_d + jnp.concatenate(du_s, axis=1)).astype(BF16)

    full = lambda shape: pl.BlockSpec(shape, lambda i: (0, 0))
    rev = lambda w, col=0: pl.BlockSpec((S5_T, w), lambda i: (nc - 1 - i, col))
    prev = pl.BlockSpec((8, S5_CH), lambda i: (jnp.maximum((nc - 1 - i) * (S5_T // 8) - 1, 0), 0))
    return pl.pallas_call(
        body, name=name, grid=(nc,),
        in_specs=[rev(BW, C_S5U // BW), rev(BW, C_S5G // BW), pl.BlockSpec((1, S5_T, BW), lambda i: (0, nc - 1 - i, 0)),
                  rev(S5_CH), rev(S5_CH), prev, prev,
                  full((BW, S5_CH)), full((BW, S5_CH)), full((S5_CH, BW)), full((S5_CH, BW)),
                  full((2, S5_CH)), full((1, BW)), full((BW, BW)), pl.BlockSpec(memory_space=pl.ANY)],
        out_specs=[rev(2 * BW, C_S5U // (2 * BW)), full((BW, S5_CH)), full((BW, S5_CH)), full((S5_CH, BW)), full((S5_CH, BW)),
                   full((2, S5_CH)), full((1, BW)), full((BW, BW))],
        input_output_aliases={14: 0},
        out_shape=[jax.ShapeDtypeStruct((SEQ, IN_PAD), BF16),
                   jax.ShapeDtypeStruct((BW, S5_CH), F32), jax.ShapeDtypeStruct((BW, S5_CH), F32),
                   jax.ShapeDtypeStruct((S5_CH, BW), F32), jax.ShapeDtypeStruct((S5_CH, BW), F32),
                   jax.ShapeDtypeStruct((2, S5_CH), F32), jax.ShapeDtypeStruct((1, BW), F32),
                   jax.ShapeDtypeStruct((BW, BW), F32)],
        scratch_shapes=[pltpu.VMEM((S5_T, S5_CH), F32), pltpu.VMEM((S5_T, S5_CH), F32), pltpu.VMEM((2, S5_CH), F32)],
        compiler_params=_cparams(("arbitrary",)),
    )(proj, proj, dout, sre, sim, sre, sim, bbre, bbim, cre, cim, a2, dvec, wglu, dproj)


def _diag_blocks(dense, after=None):
    rows, cols = dense.shape
    rows_per, cols_per = rows // S5_GROUPS, cols // S5_GROUPS
    per_lane_block = LANES // cols_per
    tile = 512

    def body(d_ref, *rest):
        o_ref = rest[-1]
        r0 = pl.program_id(0) * tile
        grp = (r0 + lax.broadcasted_iota(jnp.int32, (tile, LANES), 0)) // rows_per
        lane = lax.broadcasted_iota(jnp.int32, (tile, LANES), 1)
        acc = jnp.zeros((tile, LANES), F32)
        for hb in range(cols // LANES):
            acc = acc + jnp.where(grp == per_lane_block * hb + lane // cols_per, d_ref[:, hb * LANES:(hb + 1) * LANES], 0.0)
        shift = LANES // 2
        while shift >= cols_per:
            acc = acc + pltpu.roll(acc, LANES - shift, 1)
            shift //= 2
        o_ref[...] = acc

    folded = pl.pallas_call(
        body, name=f"diag_blocks_{rows_per}x{cols_per}", grid=(rows // tile,),
        in_specs=[pl.BlockSpec((tile, cols), lambda i: (i, 0))] + ([] if after is None else [pl.BlockSpec(memory_space=pl.ANY)]),
        out_specs=pl.BlockSpec((tile, LANES), lambda i: (i, 0)),
        out_shape=jax.ShapeDtypeStruct((rows, LANES), F32), compiler_params=_cparams(("parallel",)),
    )(dense, *([] if after is None else [after]))
    return folded[:, :cols_per].reshape(S5_GROUPS, rows_per, cols_per)


def _block_diag(t):
    g, rows_per, cols_per = t.shape
    wide = jnp.tile(t.reshape(g * rows_per, cols_per), (1, g))
    r = lax.broadcasted_iota(jnp.int32, wide.shape, 0) // rows_per
    c = lax.broadcasted_iota(jnp.int32, wide.shape, 1) // cols_per
    return jnp.where(r == c, wide, 0.0)


def _s5_disc(lam_re, lam_im, b_re, b_im, c_re, c_im, d, log_step):
    step = jnp.exp(log_step)[:, None]
    mag = jnp.exp(lam_re * step)
    ab_re, ab_im = mag * jnp.cos(lam_im * step), mag * jnp.sin(lam_im * step)
    den = lam_re * lam_re + lam_im * lam_im
    nr = ab_re - 1.0
    coef_re = (nr * lam_re + ab_im * lam_im) / den
    coef_im = (ab_im * lam_re - nr * lam_im) / den
    bb_re = coef_re[..., None] * b_re - coef_im[..., None] * b_im
    bb_im = coef_re[..., None] * b_im + coef_im[..., None] * b_re
    a2 = jnp.stack([ab_re.reshape(-1), ab_im.reshape(-1)])
    return (jnp.swapaxes(bb_re, 1, 2), jnp.swapaxes(bb_im, 1, 2),
            jnp.swapaxes(c_re, 1, 2), jnp.swapaxes(c_im, 1, 2),
            a2, d.reshape(1, BW))


def _left_lanes(shape):
    return lax.broadcasted_iota(jnp.int32, shape, 1) < 64


def _sgu_chunk(u, v, gate, ln_w, ln_b, w, bias):
    u32, v32 = _gelu(u), _gelu(v)
    mu = jnp.mean(v32, axis=-1, keepdims=True)
    var = jnp.mean(jnp.square(v32 - mu), axis=-1, keepdims=True)
    vn = (v32 - mu) * lax.rsqrt(var + EPS) * ln_w + ln_b
    t_i = lax.broadcasted_iota(jnp.int32, (SGU_CHUNK, SGU_CHUNK), 0)
    s_i = lax.broadcasted_iota(jnp.int32, (SGU_CHUNK, SGU_CHUNK), 1)
    causal = t_i >= s_i
    left = _left_lanes((SGU_CHUNK, LANES))
    sgate = _silu(gate)
    outs = []
    for j in range(BW // LANES):
        vb = vn[:, j * LANES:(j + 1) * LANES]
        s_blk = (_bdot(jnp.where(causal, w[2 * j], 0.0), jnp.where(left, vb, 0.0), 1, 0)
                 + _bdot(jnp.where(causal, w[2 * j + 1], 0.0), jnp.where(left, 0.0, vb), 1, 0))
        sl = slice(j * LANES, (j + 1) * LANES)
        outs.append(u32[:, sl] * (s_blk + bias[:, sl]) * sgate[:, sl])
    return outs


def _sgu_fwd(proj, ln_w, ln_b, w, bias, name):
    def body(u_ref, v_ref, g_ref, lw_ref, lb_ref, w_ref, b_ref, o_ref):
        outs = _sgu_chunk(u_ref[...], v_ref[...], g_ref[...], lw_ref[...], lb_ref[...], w_ref[...], b_ref[...])
        for j, o in enumerate(outs):
            o_ref[:, j * LANES:(j + 1) * LANES] = o.astype(BF16)

    blk = lambda col: pl.BlockSpec((SGU_CHUNK, BW), lambda c: (c, col // BW))
    vec = pl.BlockSpec((1, BW), lambda c: (0, 0))
    return pl.pallas_call(
        body, name=name, grid=(SEQ // SGU_CHUNK,),
        in_specs=[blk(C_SGU_U), blk(C_SGU_V), blk(C_SGU_G), vec, vec,
                  pl.BlockSpec((SGU_HEADS, SGU_CHUNK, SGU_CHUNK), lambda c: (0, 0, 0)),
                  pl.BlockSpec((SGU_CHUNK, BW), lambda c: (0, 0))],
        out_specs=pl.BlockSpec((SGU_CHUNK, BW), lambda c: (c, 0)),
        out_shape=jax.ShapeDtypeStruct((SEQ, BW), BF16),
        compiler_params=_cparams(("parallel",)),
    )(proj, proj, proj, ln_w, ln_b, w, bias)


def _sgu_bwd(proj, dproj, dout, ln_w, ln_b, w, bias, name):
    def body(u_ref, v_ref, g_ref, do_ref, lw_ref, lb_ref, w_ref, b_ref, dproj_in, dp_ref, dlw_ref, dlb_ref, dw_ref, db_ref):
        _, vjp = jax.vjp(_sgu_chunk, u_ref[...], v_ref[...], g_ref[...], lw_ref[...], lb_ref[...], w_ref[...], b_ref[...])
        do = do_ref[0]
        du, dv, dgate, dlw, dlb, dw, db = vjp([do[:, j * LANES:(j + 1) * LANES] for j in range(BW // LANES)])
        dp_ref[:, 0:BW] = du.astype(BF16)
        dp_ref[:, BW:2 * BW] = dv.astype(BF16)
        dp_ref[:, 2 * BW:3 * BW] = dgate.astype(BF16)
        dp_ref[:, 3 * BW:] = jnp.zeros((SGU_CHUNK, BW), BF16)

        @pl.when(pl.program_id(0) == 0)
        def _():
            dlw_ref[...] = dlw
            dlb_ref[...] = dlb
            dw_ref[...] = dw
            db_ref[...] = db

        @pl.when(pl.program_id(0) > 0)
        def _():
            dlw_ref[...] += dlw
            dlb_ref[...] += dlb
            dw_ref[...] += dw
            db_ref[...] += db

    blk = lambda col: pl.BlockSpec((SGU_CHUNK, BW), lambda c: (c, col // BW))
    vec = pl.BlockSpec((1, BW), lambda c: (0, 0))
    wsp = pl.BlockSpec((SGU_HEADS, SGU_CHUNK, SGU_CHUNK), lambda c: (0, 0, 0))
    bsp = pl.BlockSpec((SGU_CHUNK, BW), lambda c: (0, 0))
    return pl.pallas_call(
        body, name=name, grid=(SEQ // SGU_CHUNK,),
        in_specs=[blk(C_SGU_U), blk(C_SGU_V), blk(C_SGU_G), pl.BlockSpec((1, SGU_CHUNK, BW), lambda c: (1, c, 0)),
                  vec, vec, wsp, bsp, pl.BlockSpec(memory_space=pl.ANY)],
        out_specs=[pl.BlockSpec((SGU_CHUNK, 4 * BW), lambda c: (c, C_SGU_U // (4 * BW))), vec, vec, wsp, bsp],
        input_output_aliases={8: 0},
        out_shape=[jax.ShapeDtypeStruct((SEQ, IN_PAD), BF16), jax.ShapeDtypeStruct((1, BW), F32),
                   jax.ShapeDtypeStruct((1, BW), F32), jax.ShapeDtypeStruct((SGU_HEADS, SGU_CHUNK, SGU_CHUNK), F32),
                   jax.ShapeDtypeStruct((SGU_CHUNK, BW), F32)],
        compiler_params=_cparams(("arbitrary",)),
    )(proj, proj, proj, dout, ln_w, ln_b, w, bias, dproj)


CONV_BLK = 256


def _m2_conv_fwd(proj, w, b, name):
    def body(x_ref, w_ref, b_ref, o_ref):
        x = x_ref[...]
        acc = jnp.zeros_like(x) + b_ref[...]
        for k in range(M2_CONV):
            acc = acc + w_ref[k:k + 1, :] * _shift_down(x, M2_CONV - 1 - k)
        o_ref[...] = _silu(acc)

    return pl.pallas_call(
        body, name=name, grid=(M2_CONV_CH // CONV_BLK,),
        in_specs=[pl.BlockSpec((SEQ, CONV_BLK), lambda j: (0, C_M2X // CONV_BLK + j)),
                  pl.BlockSpec((M2_CONV, CONV_BLK), lambda j: (0, j)), pl.BlockSpec((1, CONV_BLK), lambda j: (0, j))],
        out_specs=pl.BlockSpec((SEQ, CONV_BLK), lambda j: (0, j)),
        out_shape=jax.ShapeDtypeStruct((SEQ, M2_CONV_CH), F32),
        compiler_params=_cparams(("parallel",)),
    )(proj, w, b)


def _m2_conv_bwd(proj, dproj, dxa, w, b, name):
    def body(x_ref, d_ref, w_ref, b_ref, dproj_in, dx_ref, dw_ref, db_ref):
        x = x_ref[...]
        xs = [_shift_down(x, M2_CONV - 1 - k) for k in range(M2_CONV)]
        acc = jnp.zeros_like(x) + b_ref[...]
        for k in range(M2_CONV):
            acc = acc + w_ref[k:k + 1, :] * xs[k]
        sg = jax.nn.sigmoid(acc)
        dacc = d_ref[...] * (sg * (1.0 + acc * (1.0 - sg)))
        dx = jnp.zeros_like(x)
        for k in range(M2_CONV):
            dx = dx + w_ref[k:k + 1, :] * _shift_up(dacc, M2_CONV - 1 - k)
            dw_ref[k:k + 1, :] = jnp.sum(dacc * xs[k], axis=0, keepdims=True)
        dx_ref[...] = dx.astype(BF16)
        db_ref[...] = jnp.sum(dacc, axis=0, keepdims=True)

    return pl.pallas_call(
        body, name=name, grid=(M2_CONV_CH // CONV_BLK,),
        in_specs=[pl.BlockSpec((SEQ, CONV_BLK), lambda j: (0, C_M2X // CONV_BLK + j)),
                  pl.BlockSpec((SEQ, CONV_BLK), lambda j: (0, j)),
                  pl.BlockSpec((M2_CONV, CONV_BLK), lambda j: (0, j)), pl.BlockSpec((1, CONV_BLK), lambda j: (0, j)),
                  pl.BlockSpec(memory_space=pl.ANY)],
        out_specs=[pl.BlockSpec((SEQ, CONV_BLK), lambda j: (0, C_M2X // CONV_BLK + j)),
                   pl.BlockSpec((M2_CONV, CONV_BLK), lambda j: (0, j)), pl.BlockSpec((1, CONV_BLK), lambda j: (0, j))],
        input_output_aliases={4: 0},
        out_shape=[jax.ShapeDtypeStruct((SEQ, IN_PAD), BF16), jax.ShapeDtypeStruct((M2_CONV, M2_CONV_CH), F32),
                   jax.ShapeDtypeStruct((1, M2_CONV_CH), F32)],
        compiler_params=_cparams(("parallel",)),
    )(proj, dxa, w, b, dproj)


N_PAIR = M2_HEADS // 2
HI = lax.Precision.HIGHEST


def _col(a, h):
    lane = lax.broadcasted_iota(jnp.int32, a.shape, 1)
    return jnp.sum(jnp.where(lane == h, a, 0.0), axis=1, keepdims=True)


def _row(a, h):
    sub = lax.broadcasted_iota(jnp.int32, a.shape, 0)
    return jnp.sum(jnp.where(sub == h, a, 0.0), axis=0, keepdims=True)


def _ssd_chunk(xs, bms, cms, dtr, zs, states, dt_bias, a_log, dfs, nws):
    q = M2_CHUNK
    dt = _softplus(dtr + dt_bias)
    da = dt * (-jnp.exp(a_log))
    l_i = lax.broadcasted_iota(jnp.int32, (q, q), 0)
    s_i = lax.broadcasted_iota(jnp.int32, (q, q), 1)
    causal = l_i >= s_i
    tril = jnp.where(causal, 1.0, 0.0)
    a_cs = _dg(tril, da, 1, 0, HI)
    a_cs_t = _dg(da, tril, 0, 1, HI)
    a_end = _row(a_cs, q - 1)
    left = _left_lanes((q, LANES))
    left1 = _left_lanes((1, LANES))
    ys, nexts = [], []
    for j in range(N_PAIR):
        grp = j // 2
        bm, cm = bms[grp], cms[grp]
        h0, h1 = 2 * j, 2 * j + 1
        cb = _bdot(cm, bm, 1, 1)
        xdt = xs[j] * jnp.where(left, _col(dt, h0), _col(dt, h1))
        acs0, acs1 = _col(a_cs, h0), _col(a_cs, h1)
        y = _bdot(cm, states[j], 1, 0) * jnp.where(left, jnp.exp(acs0), jnp.exp(acs1))
        s_new = states[j] * jnp.where(left1, jnp.exp(_col(a_end, h0)), jnp.exp(_col(a_end, h1)))
        for h, acs, xh in ((h0, acs0, jnp.where(left, xdt, 0.0)), (h1, acs1, jnp.where(left, 0.0, xdt))):
            decay = jnp.exp(jnp.where(causal, acs - _row(a_cs_t, h), -jnp.inf))
            y = y + _bdot(cb * decay, xh, 1, 0)
            s_new = s_new + _bdot(bm * jnp.exp(_col(a_end, h) - acs), xh, 0, 0)
        ys.append((y + dfs[j] * xs[j]) * _silu(zs[j]))
        nexts.append(s_new)
    ssq = sum(jnp.sum(y * y, axis=-1, keepdims=True) for y in ys)
    scale = lax.rsqrt(ssq / BW + EPS)
    return [y * scale * nw for y, nw in zip(ys, nws)], nexts


def _blocks(ref, n, width=LANES):
    return [ref[:, j * width:(j + 1) * width] for j in range(n)]


def _ssd_fwd(proj, xa, dt_bias, a_log, dfull, nw, name):
    nc = SEQ // M2_CHUNK

    def body(x_ref, b_ref, c_ref, dt_ref, z_ref, dtb_ref, al_ref, df_ref, nw_ref, o_ref, sin_ref, st):
        @pl.when(pl.program_id(0) == 0)
        def _():
            st[...] = jnp.zeros_like(st)

        states = [st[j] for j in range(N_PAIR)]
        for j in range(N_PAIR):
            sin_ref[0, j] = states[j]
        ys, nexts = _ssd_chunk(_blocks(x_ref, 4), _blocks(b_ref, 2), _blocks(c_ref, 2), dt_ref[...], _blocks(z_ref, 4),
                               states, dtb_ref[...], al_ref[...], _blocks(df_ref, 4), _blocks(nw_ref, 4))
        for j in range(N_PAIR):
            o_ref[:, j * LANES:(j + 1) * LANES] = ys[j].astype(BF16)
            st[j] = nexts[j]

    vec8 = pl.BlockSpec((1, LANES), lambda c: (0, 0))
    vec = pl.BlockSpec((1, BW), lambda c: (0, 0))
    return pl.pallas_call(
        body, name=name, grid=(nc,),
        in_specs=[pl.BlockSpec((M2_CHUNK, BW), lambda c: (c, 0)), pl.BlockSpec((M2_CHUNK, 256), lambda c: (c, 2)),
                  pl.BlockSpec((M2_CHUNK, 256), lambda c: (c, 3)), pl.BlockSpec((M2_CHUNK, LANES), lambda c: (c, C_DT // LANES)),
                  pl.BlockSpec((M2_CHUNK, BW), lambda c: (c, C_M2Z // BW)), vec8, vec8, vec, vec],
        out_specs=[pl.BlockSpec((M2_CHUNK, BW), lambda c: (c, 0)),
                   pl.BlockSpec((1, N_PAIR, M2_STATE, LANES), lambda c: (c, 0, 0, 0))],
        out_shape=[jax.ShapeDtypeStruct((SEQ, BW), BF16), jax.ShapeDtypeStruct((nc, N_PAIR, M2_STATE, LANES), F32)],
        scratch_shapes=[pltpu.VMEM((N_PAIR, M2_STATE, LANES), F32)],
        compiler_params=_cparams(("arbitrary",)),
    )(xa, xa, xa, proj, proj, dt_bias, a_log, dfull, nw)


def _ssd_bwd(proj, dproj, xa, dout, s_in, dt_bias, a_log, dfull, nw, name):
    nc = SEQ // M2_CHUNK

    def body(x_ref, b_ref, c_ref, dt_ref, z_ref, do_ref, sin_ref, dtb_ref, al_ref, df_ref, nw_ref, dproj_in,
             dp_ref, dxa_ref, ddtb_ref, dal_ref, ddf_ref, dnw_ref, dst):
        @pl.when(pl.program_id(0) == 0)
        def _():
            dst[...] = jnp.zeros_like(dst)
            for r in (ddtb_ref, dal_ref, ddf_ref, dnw_ref):
                r[...] = jnp.zeros_like(r)

        states = [sin_ref[0, j] for j in range(N_PAIR)]
        _, vjp = jax.vjp(_ssd_chunk, _blocks(x_ref, 4), _blocks(b_ref, 2), _blocks(c_ref, 2), dt_ref[...],
                         _blocks(z_ref, 4), states, dtb_ref[...], al_ref[...], _blocks(df_ref, 4), _blocks(nw_ref, 4))
        dxs, dbs, dcs, ddt, dzs, dstates, ddtb, dal, ddfs, dnws = vjp(
            ([do_ref[0, :, j * LANES:(j + 1) * LANES] for j in range(N_PAIR)], [dst[j] for j in range(N_PAIR)]))
        for j in range(N_PAIR):
            sl = slice(j * LANES, (j + 1) * LANES)
            dxa_ref[:, sl] = dxs[j]
            dp_ref[:, sl] = dzs[j].astype(BF16)
            dst[j] = dstates[j]
            ddf_ref[:, sl] += ddfs[j]
            dnw_ref[:, sl] += dnws[j]
        for g in range(2):
            dxa_ref[:, BW + g * LANES:BW + (g + 1) * LANES] = dbs[g]
            dxa_ref[:, BW + 256 + g * LANES:BW + 256 + (g + 1) * LANES] = dcs[g]
        dp_ref[:, BW:BW + LANES] = ddt.astype(BF16)
        dp_ref[:, BW + LANES:] = jnp.zeros((M2_CHUNK, 2 * BW - BW - LANES), BF16)
        ddtb_ref[...] += ddtb
        dal_ref[...] += dal

    rev = lambda w, col=0: pl.BlockSpec((M2_CHUNK, w), lambda i: (nc - 1 - i, col))
    vec8 = pl.BlockSpec((1, LANES), lambda i: (0, 0))
    vec = pl.BlockSpec((1, BW), lambda i: (0, 0))
    return pl.pallas_call(
        body, name=name, grid=(nc,),
        in_specs=[rev(BW), rev(256, 2), rev(256, 3), rev(LANES, C_DT // LANES), rev(BW, C_M2Z // BW),
                  pl.BlockSpec((1, M2_CHUNK, BW), lambda i: (2, nc - 1 - i, 0)),
                  pl.BlockSpec((1, N_PAIR, M2_STATE, LANES), lambda i: (nc - 1 - i, 0, 0, 0)), vec8, vec8, vec, vec,
                  pl.BlockSpec(memory_space=pl.ANY)],
        out_specs=[rev(2 * BW, C_M2Z // (2 * BW)), rev(M2_CONV_CH), vec8, vec8, vec, vec],
        input_output_aliases={11: 0},
        out_shape=[jax.ShapeDtypeStruct((SEQ, IN_PAD), BF16), jax.ShapeDtypeStruct((SEQ, M2_CONV_CH), F32),
                   jax.ShapeDtypeStruct((1, LANES), F32), jax.ShapeDtypeStruct((1, LANES), F32),
                   jax.ShapeDtypeStruct((1, BW), F32), jax.ShapeDtypeStruct((1, BW), F32)],
        scratch_shapes=[pltpu.VMEM((N_PAIR, M2_STATE, LANES), F32)],
        compiler_params=_cparams(("arbitrary",)),
    )(xa, xa, xa, proj, proj, dout, s_in, dt_bias, a_log, dfull, nw, dproj)


def _sc_specs():
    col = lambda kind: pl.BlockSpec((SEQ, LANES), lambda j: (0, C_SC // LANES + 4 * j + kind))
    return [col(0), col(1), col(2), col(3)]


def _sc_fwd(proj, w, name):
    def body(b_ref, c_ref, h_ref, g_ref, w_ref, o_ref):
        ch = c_ref[...] * h_ref[...]
        acc = jnp.zeros_like(ch)
        for k in range(SC_CONV):
            acc = acc + w_ref[k:k + 1, :] * _shift_down(ch, SC_CONV - 1 - k)
        o_ref[...] = (b_ref[...] * acc * _silu(g_ref[...])).astype(BF16)

    return pl.pallas_call(
        body, name=name, grid=(BW // LANES,),
        in_specs=_sc_specs() + [pl.BlockSpec((SC_CONV, LANES), lambda j: (0, j))],
        out_specs=pl.BlockSpec((SEQ, LANES), lambda j: (0, j)),
        out_shape=jax.ShapeDtypeStruct((SEQ, BW), BF16),
        compiler_params=_cparams(("parallel",)),
    )(proj, proj, proj, proj, w)


def _sc_bwd(proj, dproj, dout, w, name):
    def body(b_ref, c_ref, h_ref, g_ref, do_ref, w_ref, dproj_in, dp_ref, dw_ref):
        cv, hv, gv = c_ref[...], h_ref[...], g_ref[...]
        ch = cv * hv
        chs = [_shift_down(ch, SC_CONV - 1 - k) for k in range(SC_CONV)]
        acc = jnp.zeros_like(ch)
        for k in range(SC_CONV):
            acc = acc + w_ref[k:k + 1, :] * chs[k]
        sg = jax.nn.sigmoid(gv)
        do = do_ref[0]
        bv = b_ref[...]
        dp_ref[:, 0:LANES] = (do * acc * (gv * sg)).astype(BF16)
        dp_ref[:, 3 * LANES:] = (do * bv * acc * (sg * (1.0 + gv * (1.0 - sg)))).astype(BF16)
        dacc = do * bv * (gv * sg)
        dch = jnp.zeros_like(ch)
        for k in range(SC_CONV):
            dch = dch + w_ref[k:k + 1, :] * _shift_up(dacc, SC_CONV - 1 - k)
            dw_ref[k:k + 1, :] = jnp.sum(dacc * chs[k], axis=0, keepdims=True)
        dp_ref[:, LANES:2 * LANES] = (dch * hv).astype(BF16)
        dp_ref[:, 2 * LANES:3 * LANES] = (dch * cv).astype(BF16)

    wsp = pl.BlockSpec((SC_CONV, LANES), lambda j: (0, j))
    return pl.pallas_call(
        body, name=name, grid=(BW // LANES,),
        in_specs=_sc_specs() + [pl.BlockSpec((1, SEQ, LANES), lambda j: (3, 0, j)), wsp, pl.BlockSpec(memory_space=pl.ANY)],
        out_specs=[pl.BlockSpec((SEQ, 4 * LANES), lambda j: (0, C_SC // (4 * LANES) + j)), wsp],
        input_output_aliases={6: 0},
        out_shape=[jax.ShapeDtypeStruct((SEQ, IN_PAD), BF16), jax.ShapeDtypeStruct((SC_CONV, BW), F32)],
        compiler_params=_cparams(("parallel",)),
    )(proj, proj, proj, proj, dout, w, dproj)


MERGE_T = 256
MERGE_BWD_T = 512


def _merge_fwd(proj, ys, merge_b, w_branch, name):
    def body(y_ref, lg_ref, b_ref, w_ref, o_ref):
        acc = jnp.zeros((MERGE_T, D_MODEL), F32)
        for k in range(N_BRANCH):
            gate = jax.nn.sigmoid(lg_ref[:, k * D_MODEL:(k + 1) * D_MODEL] + b_ref[k])
            acc = acc + gate * _dg(y_ref[k], w_ref[k], 1, 0)
        o_ref[...] = acc.astype(BF16)

    return pl.pallas_call(
        body, name=name, grid=(SEQ // MERGE_T,),
        in_specs=[pl.BlockSpec((N_BRANCH, MERGE_T, BW), lambda i: (0, i, 0)),
                  pl.BlockSpec((MERGE_T, N_BRANCH * D_MODEL), lambda i: (i, C_MERGE // (N_BRANCH * D_MODEL))),
                  pl.BlockSpec((N_BRANCH, 1, D_MODEL), lambda i: (0, 0, 0)),
                  pl.BlockSpec((N_BRANCH, BW, D_MODEL), lambda i: (0, 0, 0))],
        out_specs=pl.BlockSpec((MERGE_T, D_MODEL), lambda i: (i, 0)),
        out_shape=jax.ShapeDtypeStruct((SEQ, D_MODEL), BF16),
        compiler_params=_cparams(("parallel",)),
    )(ys, proj, merge_b, w_branch)


def _merge_bwd(proj, ys, dm, merge_b, w_branch, name):
    nt = SEQ // MERGE_BWD_T

    def body(y_ref, lg_ref, dm_ref, b_ref, w_ref, dy_ref, dlg_ref, dw_ref, db_ref, dw_acc):
        i = pl.program_id(1)
        gate = jax.nn.sigmoid(lg_ref[...] + b_ref[0])
        y = y_ref[0]
        dmv = dm_ref[...]
        dbo = (gate * dmv).astype(BF16)
        dlg = _dg(y, w_ref[0], 1, 0) * dmv * gate * (1.0 - gate)
        dlg_ref[...] = dlg.astype(BF16)
        dy_ref[0] = _dg(dbo, w_ref[0], 1, 1)
        dwp = _dg(y, dbo, 0, 0)
        dbp = jnp.sum(dlg, axis=0, keepdims=True)

        @pl.when(i == 0)
        def _():
            dw_acc[...] = dwp
            db_ref[0] = dbp

        @pl.when(i > 0)
        def _():
            dw_acc[...] += dwp
            db_ref[0] += dbp

        @pl.when(i == nt - 1)
        def _():
            dw_ref[0] = dw_acc[...].astype(BF16)

    return pl.pallas_call(
        body, name=name, grid=(N_BRANCH, nt),
        in_specs=[pl.BlockSpec((1, MERGE_BWD_T, BW), lambda k, i: (k, i, 0)),
                  pl.BlockSpec((MERGE_BWD_T, D_MODEL), lambda k, i: (i, C_MERGE // D_MODEL + k)),
                  pl.BlockSpec((MERGE_BWD_T, D_MODEL), lambda k, i: (i, 0)),
                  pl.BlockSpec((1, 1, D_MODEL), lambda k, i: (k, 0, 0)),
                  pl.BlockSpec((1, BW, D_MODEL), lambda k, i: (k, 0, 0))],
        out_specs=[pl.BlockSpec((1, MERGE_BWD_T, BW), lambda k, i: (k, i, 0)),
                   pl.BlockSpec((MERGE_BWD_T, D_MODEL), lambda k, i: (i, k)),
                   pl.BlockSpec((1, BW, D_MODEL), lambda k, i: (k, 0, 0)),
                   pl.BlockSpec((1, 1, D_MODEL), lambda k, i: (k, 0, 0))],
        out_shape=[jax.ShapeDtypeStruct((N_BRANCH, SEQ, BW), F32), jax.ShapeDtypeStruct((SEQ, IN_PAD), BF16),
                   jax.ShapeDtypeStruct((N_BRANCH, BW, D_MODEL), BF16), jax.ShapeDtypeStruct((N_BRANCH, 1, D_MODEL), F32)],
        scratch_shapes=[pltpu.VMEM((BW, D_MODEL), F32)],
        compiler_params=_cparams(("parallel", "arbitrary")),
    )(ys, proj, dm, merge_b, w_branch)


def _adamw(glist, w, m, v, rows, name):
    nl = len(glist)
    n, r, c = glist[0].shape
    assert w.shape == (nl, r, c) and r % rows == 0
    nb = r // rows

    def body(*refs):
        g_refs = refs[:nl]
        w_ref, m_ref, v_ref, go_ref, d_ref, mo_ref, vo_ref = refs[nl:]
        for layer in range(nl):
            @pl.when(pl.program_id(0) == layer)
            def _(g_ref=g_refs[layer]):
                g = g_ref[0].astype(F32)
                for s in range(1, n):
                    g = g + g_ref[s].astype(F32)
                mn = ADAM_B1 * m_ref[0] + (1.0 - ADAM_B1) * g
                vn = ADAM_B2 * v_ref[0] + (1.0 - ADAM_B2) * jnp.square(g)
                m_hat = mn / (1.0 - ADAM_B1 ** ADAM_STEP)
                v_hat = vn / (1.0 - ADAM_B2 ** ADAM_STEP)
                go_ref[0] = g
                d_ref[0] = -ADAM_LR * (m_hat / (jnp.sqrt(v_hat) + ADAM_EPS) + ADAM_WD * w_ref[0])
                mo_ref[0] = mn
                vo_ref[0] = vn

    def g_spec(layer):
        return pl.BlockSpec((n, rows, c), lambda a, i: (0, jnp.where(a < layer, 0, jnp.where(a == layer, i, nb - 1)), 0))

    blk = pl.BlockSpec((1, rows, c), lambda a, i: (a, i, 0))
    out = jax.ShapeDtypeStruct((nl, r, c), F32)
    return pl.pallas_call(
        body, name=name, grid=(nl, nb),
        in_specs=[g_spec(layer) for layer in range(nl)] + [blk, blk, blk],
        out_specs=[blk, blk, blk, blk], out_shape=[out, out, out, out],
        compiler_params=_cparams(("arbitrary", "arbitrary")),
    )(*glist, w, m, v)


X_ROWS_PER_COL = 2 * (D_MODEL // LANES)


def _w_in_to_x(w):
    t = jnp.transpose(w, (2, 0, 1)).reshape(SHARD_IN, DEPTH, D_MODEL // LANES, LANES)
    return jnp.transpose(t, (0, 2, 1, 3)).reshape(SHARD_IN * X_ROWS_PER_COL, LANES)


def _w_in_from_x(xv):
    t = jnp.transpose(xv.reshape(SHARD_IN, D_MODEL // LANES, DEPTH, LANES), (0, 2, 1, 3))
    return jnp.transpose(t.reshape(SHARD_IN, DEPTH, D_MODEL), (1, 2, 0))


def _adamw_w_in(glist, w, m, v, name, after=None):
    n = glist[0].shape[0]
    cols = LANES
    rows = cols * X_ROWS_PER_COL
    extra = [] if after is None else [after]

    def body(g0_ref, g1_ref, w_ref, m_ref, v_ref, *rest):
        go_ref, d_ref, mo_ref, vo_ref = rest[len(extra):]
        for layer, g_ref in enumerate((g0_ref, g1_ref)):
            g = g_ref[0].astype(F32)
            for s in range(1, n):
                g = g + g_ref[s].astype(F32)
            gt = g.T
            for t in range(D_MODEL // LANES):
                sel = (pl.ds(2 * t + layer, cols, stride=X_ROWS_PER_COL), slice(None))
                gs = gt[:, t * LANES:(t + 1) * LANES]
                mn = ADAM_B1 * m_ref[sel] + (1.0 - ADAM_B1) * gs
                vn = ADAM_B2 * v_ref[sel] + (1.0 - ADAM_B2) * jnp.square(gs)
                m_hat = mn / (1.0 - ADAM_B1 ** ADAM_STEP)
                v_hat = vn / (1.0 - ADAM_B2 ** ADAM_STEP)
                go_ref[sel] = gs
                d_ref[sel] = -ADAM_LR * (m_hat / (jnp.sqrt(v_hat) + ADAM_EPS) + ADAM_WD * w_ref[sel])
                mo_ref[sel] = mn
                vo_ref[sel] = vn

    g_spec = pl.BlockSpec((n, D_MODEL, cols), lambda i: (0, 0, i))
    blk = pl.BlockSpec((rows, LANES), lambda i: (i, 0))
    out = jax.ShapeDtypeStruct((SHARD_IN * X_ROWS_PER_COL, LANES), F32)
    res = pl.pallas_call(
        body, name=name, grid=(-(-SHARD_IN // cols),),
        in_specs=[g_spec, g_spec, blk, blk, blk] + [pl.BlockSpec(memory_space=pl.ANY)] * len(extra),
        out_specs=[blk, blk, blk, blk], out_shape=[out, out, out, out],
        compiler_params=_cparams(("parallel",)),
    )(*glist, _w_in_to_x(w), _w_in_to_x(m), _w_in_to_x(v), *extra)
    return [_w_in_from_x(o) for o in res]


def _adamw_many(gs, ws, ms, vs, name):
    k = len(gs)

    def body(*refs):
        g_refs, w_refs, m_refs, v_refs = refs[:k], refs[k:2 * k], refs[2 * k:3 * k], refs[3 * k:4 * k]
        d_refs, mo_refs, vo_refs = refs[4 * k:5 * k], refs[5 * k:6 * k], refs[6 * k:7 * k]
        for i in range(k):
            g = g_refs[i][...]
            mn = ADAM_B1 * m_refs[i][...] + (1.0 - ADAM_B1) * g
            vn = ADAM_B2 * v_refs[i][...] + (1.0 - ADAM_B2) * jnp.square(g)
            m_hat = mn / (1.0 - ADAM_B1 ** ADAM_STEP)
            v_hat = vn / (1.0 - ADAM_B2 ** ADAM_STEP)
            d_refs[i][...] = -ADAM_LR * (m_hat / (jnp.sqrt(v_hat) + ADAM_EPS) + ADAM_WD * w_refs[i][...])
            mo_refs[i][...] = mn
            vo_refs[i][...] = vn

    whole = pl.BlockSpec(memory_space=pltpu.VMEM)
    shapes = [jax.ShapeDtypeStruct(w.shape, F32) for w in ws]
    outs = pl.pallas_call(
        body, name=name, in_specs=[whole] * (4 * k), out_specs=[whole] * (3 * k), out_shape=shapes * 3,
        compiler_params=_cparams(None),
    )(*gs, *ws, *ms, *vs)
    return outs[:k], outs[k:2 * k], outs[2 * k:]


MEMORY_ORDER = {'s5_b_re': (0, 1, 3, 2), 's5_b_im': (0, 1, 3, 2), 's5_d': (0, 2, 1), 'sc_conv_w': (1, 0, 2)}


def _memory_view(name, t):
    return jnp.transpose(t, MEMORY_ORDER[name]) if name in MEMORY_ORDER else t


def _slot_sum(gslots, name):
    n, r, c = gslots.shape

    def body(g_ref, o_ref):
        g = g_ref[0]
        for s in range(1, n):
            g = g + g_ref[s]
        o_ref[...] = g

    return pl.pallas_call(
        body, name=name, in_specs=[pl.BlockSpec((n, r, c), lambda: (0, 0, 0))],
        out_specs=pl.BlockSpec((r, c), lambda: (0, 0)), out_shape=jax.ShapeDtypeStruct((r, c), F32),
        compiler_params=_cparams(None),
    )(gslots)


def _me_and_peers():
    x, y, c = lax.axis_index("x"), lax.axis_index("y"), lax.axis_index("c")
    me = 4 * x + 2 * y + c
    peers = []
    for k in range(1, N_DEV):
        px = 1 - x if (k >> 2) & 1 else x
        py = 1 - y if (k >> 1) & 1 else y
        pc = 1 - c if k & 1 else c
        peers.append((4 * px + 2 * py + pc, (px, py, pc)))
    return me, peers


_HBM = pl.BlockSpec(memory_space=pltpu.HBM)
_SEM = pl.BlockSpec(memory_space=pltpu.SEMAPHORE)
_EFFECT = pltpu.SideEffectType.DATAFLOW_SIDE_EFFECTING


N_CHIP = N_DEV // 2


def _chip_peers():
    x, y, c = lax.axis_index("x"), lax.axis_index("y"), lax.axis_index("c")
    chips = []
    for d in range(1, N_CHIP):
        px = 1 - x if (d >> 1) & 1 else x
        py = 1 - y if d & 1 else y
        chips.append((2 * px + py, (px, py)))
    return (x, y, c), 2 * x + y, chips


def _plan_direct(ins, lands, send_sems, recv_sems, local_sems, gather):
    me, peers = _me_and_peers()
    plan = dict(start=[], local=[], sends=[], recvs=[])
    for t in range(len(ins)):
        own = pltpu.make_async_copy(ins[t] if gather else ins[t].at[me], lands[t].at[me], local_sems.at[t])
        plan['start'].append(own)
        plan['local'].append(own)
        for k, (pidx, pos) in enumerate(peers):
            cp = pltpu.make_async_remote_copy(
                src_ref=ins[t] if gather else ins[t].at[pidx], dst_ref=lands[t].at[me],
                send_sem=send_sems.at[t * (N_DEV - 1) + k], recv_sem=recv_sems.at[t * (N_DEV - 1) + k],
                device_id=pos, device_id_type=MESH)
            plan['start'].append(cp)
            plan['sends'].append(cp)
            plan['recvs'].append(cp)
    return plan


def _plan_gather(ins, lands, send_sems, recv_sems, local_sems, first=0):
    (x, y, c), q, chips = _chip_peers()
    me = 2 * q + c
    plan = dict(start=[], relay_wait=[], relay_start=[], local=[], sends=[], recvs=[])
    for t in range(len(ins)):
        base = (first + t) * 7
        sem = lambda k: dict(send_sem=send_sems.at[base + k], recv_sem=recv_sems.at[base + k], device_id_type=MESH)
        own = pltpu.make_async_copy(ins[t], lands[t].at[me], local_sems.at[first + t])
        to_sib = pltpu.make_async_remote_copy(src_ref=ins[t], dst_ref=lands[t].at[me], device_id=(x, y, 1 - c), **sem(0))
        plan['start'] += [own, to_sib]
        plan['local'].append(own)
        plan['sends'].append(to_sib)
        plan['recvs'].append(to_sib)
        for d, (pq, (px, py)) in enumerate(chips):
            to_chip = pltpu.make_async_remote_copy(src_ref=ins[t], dst_ref=lands[t].at[me], device_id=(px, py, c), **sem(1 + d))
            blk = lands[t].at[2 * pq + c]
            fwd = pltpu.make_async_remote_copy(src_ref=blk, dst_ref=blk, device_id=(x, y, 1 - c), **sem(4 + d))
            plan['start'].append(to_chip)
            plan['relay_wait'].append(to_chip)
            plan['relay_start'].append(fwd)
            plan['sends'] += [to_chip, fwd]
            plan['recvs'].append(fwd)
    return plan


def _plan_pair(ins, lands, send_sems, recv_sems, local_sems):
    (x, y, c), q, chips = _chip_peers()
    plan = dict(start=[], local=[], sends=[], recvs=[])
    for t in range(len(ins)):
        for k in range(N_CHIP):
            cp = pltpu.make_async_remote_copy(
                src_ref=ins[t].at[2 * k + 1 - c], dst_ref=lands[t].at[k], send_sem=send_sems.at[t * N_CHIP + k],
                recv_sem=recv_sems.at[t * N_CHIP + k], device_id=(x, y, 1 - c), device_id_type=MESH)
            plan['start'].append(cp)
            plan['sends'].append(cp)
            plan['recvs'].append(cp)
    return plan


def _plan_chips(ins, lands, send_sems, recv_sems, local_sems):
    (x, y, c), q, chips = _chip_peers()
    plan = dict(start=[], local=[], sends=[], recvs=[])
    for t in range(len(ins)):
        own = pltpu.make_async_copy(ins[t].at[q], lands[t].at[q], local_sems.at[t])
        plan['start'].append(own)
        plan['local'].append(own)
        for d, (pq, (px, py)) in enumerate(chips):
            cp = pltpu.make_async_remote_copy(
                src_ref=ins[t].at[pq], dst_ref=lands[t].at[q], send_sem=send_sems.at[t * 3 + d],
                recv_sem=recv_sems.at[t * 3 + d], device_id=(px, py, c), device_id_type=MESH)
            plan['start'].append(cp)
            plan['sends'].append(cp)
            plan['recvs'].append(cp)
    return plan


def _split_start(plan_fn, tensors, land_shapes, n_sems, name, after=None):
    n = len(tensors)
    extra = [] if after is None else [after]

    def body(*refs):
        ins, lands = refs[:n], refs[n:2 * n]
        plan = plan_fn(ins, lands, *refs[2 * n + len(extra):2 * n + len(extra) + 3])
        for cp in plan['start']:
            cp.start()
        refs[-1][...] = jnp.zeros_like(refs[-1])

    outs = pl.pallas_call(
        body, name=name,
        out_shape=(pltpu.SemaphoreType.DMA((n_sems,)), pltpu.SemaphoreType.DMA((n_sems,)), pltpu.SemaphoreType.DMA((n,)),
                   *[pltpu.HBM(t.shape, t.dtype) for t in tensors],
                   *[pltpu.HBM(s, t.dtype) for s, t in zip(land_shapes, tensors)],
                   jax.ShapeDtypeStruct((8, LANES), F32)),
        in_specs=[_HBM] * (2 * n) + [pl.BlockSpec(memory_space=pl.ANY)] * len(extra),
        out_specs=(_SEM, _SEM, _SEM, *[_HBM] * (2 * n), pl.BlockSpec(memory_space=pltpu.VMEM)),
        input_output_aliases={t: 3 + t for t in range(2 * n)},
        compiler_params=pltpu.CompilerParams(has_side_effects=_EFFECT),
    )(*[pltpu.with_memory_space_constraint(t, pltpu.HBM) for t in tensors],
      *[pltpu.with_memory_space_constraint(lax.empty(s, t.dtype), pltpu.HBM) for s, t in zip(land_shapes, tensors)], *extra)
    return outs[:-1], outs[-1]


def _split_relay(plan_fn, state, after, name):
    sems, thru = state[:3], state[3:]
    n = len(thru) // 2

    def arrived(*refs):
        plan = plan_fn(refs[:n], refs[n:2 * n], *refs[2 * n:2 * n + 3])
        for cp in plan['relay_wait']:
            cp.wait_recv()

    thru = pl.pallas_call(
        arrived, name=name + "_arrived",
        out_shape=tuple(pltpu.HBM(t.shape, t.dtype) for t in thru),
        in_specs=[_HBM] * (2 * n) + [_SEM, _SEM, _SEM, pl.BlockSpec(memory_space=pl.ANY)],
        out_specs=tuple([_HBM] * (2 * n)),
        input_output_aliases={t: t for t in range(2 * n)},
        compiler_params=pltpu.CompilerParams(has_side_effects=_EFFECT),
    )(*thru, *sems, after)

    def forward(*refs):
        plan = plan_fn(refs[:n], refs[n:2 * n], *refs[2 * n:2 * n + 3])
        for cp in plan['relay_start']:
            cp.start()
        refs[-1][...] = jnp.zeros_like(refs[-1])

    outs = pl.pallas_call(
        forward, name=name + "_forward",
        out_shape=(*[pltpu.HBM(t.shape, t.dtype) for t in thru], jax.ShapeDtypeStruct((8, LANES), F32)),
        in_specs=[_HBM] * (2 * n) + [_SEM, _SEM, _SEM],
        out_specs=(*[_HBM] * (2 * n), pl.BlockSpec(memory_space=pltpu.VMEM)),
        input_output_aliases={t: t for t in range(2 * n)},
        compiler_params=pltpu.CompilerParams(has_side_effects=_EFFECT),
    )(*thru, *sems)
    return (*sems, *outs[:-1]), outs[-1]


def _split_wait(plan_fn, state, after, name, with_sources=False):
    sems, thru = state[:3], state[3:]
    n = len(thru) // 2

    def body(*refs):
        plan = plan_fn(refs[:n], refs[n:2 * n], *refs[2 * n:2 * n + 3])
        for cp in plan['local']:
            cp.wait()
        for cp in plan['sends']:
            cp.wait_send()
        for cp in plan['recvs']:
            cp.wait_recv()

    outs = pl.pallas_call(
        body, name=name,
        out_shape=tuple(pltpu.HBM(t.shape, t.dtype) for t in thru),
        in_specs=[_HBM] * (2 * n) + [_SEM, _SEM, _SEM, pl.BlockSpec(memory_space=pl.ANY)],
        out_specs=tuple([_HBM] * (2 * n)),
        input_output_aliases={t: t for t in range(2 * n)},
        compiler_params=pltpu.CompilerParams(has_side_effects=_EFFECT),
    )(*thru, *sems, after)
    return (list(outs[:n]), list(outs[n:])) if with_sources else list(outs[n:])


PAIR_SUM_BLOCK = 768 * 1024


def _pair_sum(mine, theirs, name):
    _, r, c = mine.shape
    rows = r
    while rows * c > PAIR_SUM_BLOCK and rows % 32 == 0:
        rows //= 2

    def body(core_ref, a_ref, b_ref, o_ref):
        o_ref[0] = (a_ref[0].astype(F32) + b_ref[0].astype(F32)).astype(o_ref.dtype)

    return pl.pallas_call(
        body, name=name,
        grid_spec=pltpu.PrefetchScalarGridSpec(
            num_scalar_prefetch=1, grid=(N_CHIP, r // rows),
            in_specs=[pl.BlockSpec((1, rows, c), lambda k, i, core: (2 * k + core[0], i, 0)),
                      pl.BlockSpec((1, rows, c), lambda k, i, core: (k, i, 0))],
            out_specs=pl.BlockSpec((1, rows, c), lambda k, i, core: (k, i, 0))),
        out_shape=jax.ShapeDtypeStruct((N_CHIP, r, c), mine.dtype),
        compiler_params=_cparams(("parallel", "parallel")),
    )(lax.axis_index("c").astype(jnp.int32).reshape(1), mine, theirs)


WEIGHTS = ['norm_w', 'w_in', 's5_lambda_re', 's5_lambda_im', 's5_b_re', 's5_b_im', 's5_c_re', 's5_c_im', 's5_d',
           's5_log_step', 's5_w_glu', 'sgu_ln_w', 'sgu_ln_b', 'sgu_w', 'sgu_b', 'm2_conv_w', 'm2_conv_b', 'm2_dt_bias',
           'm2_a_log', 'm2_d', 'm2_norm_w', 'sc_conv_w', 'merge_b', 'w_branch', 'w_out', 'final_norm_w']
BIG_SHARDED = ['w_in', 'w_branch', 'w_out', 's5_w_glu']
SMALL_SHARDED = ['m2_conv_w', 'sc_conv_w', 'merge_b']
REPLICATED = [n for n in WEIGHTS if n not in BIG_SHARDED + SMALL_SHARDED]
S5_NAMES = ['s5_lambda_re', 's5_lambda_im', 's5_b_re', 's5_b_im', 's5_c_re', 's5_c_im', 's5_d', 's5_log_step']


def _sc_interleave(t):
    lead = t.shape[:-1]
    return jnp.swapaxes(t.reshape(lead + (4, 4, LANES)), -3, -2).reshape(lead + (4 * BW,))


def _pad_in(w):
    z = lambda n: jnp.zeros(w.shape[:-1] + (n,), w.dtype)
    return jnp.concatenate([w[..., 6152:], w[..., 0:1024], w[..., 3072:4096], w[..., 1024:2560], z(512),
                            w[..., 2560:3072], w[..., 4096:4104], z(504), _sc_interleave(w[..., 4104:6152])], axis=-1)


def _unpad_in(g):
    return jnp.concatenate([g[..., C_S5U:C_S5U + 1024], g[..., C_SGU_U:C_SGU_U + 1536], g[..., C_M2Z:C_M2Z + 512],
                            g[..., C_M2X:C_M2X + 1024], g[..., C_DT:C_DT + 8], _sc_interleave(g[..., C_SC:]),
                            g[..., :N_BRANCH * D_MODEL]], axis=-1)


ROW_BLOCK = 8 * LANES


def _pack_rows(tensors, row_mult, batched=False):
    parts = []
    for t in tensors:
        f = t.reshape((t.shape[0], -1) if batched else (1, -1))
        f = jnp.pad(f, ((0, 0), (0, (-f.shape[1]) % ROW_BLOCK)))
        parts.append(f.reshape(f.shape[0], -1, LANES))
    out = jnp.concatenate(parts, axis=1)
    out = jnp.pad(out, ((0, 0), (0, (-out.shape[1]) % row_mult), (0, 0)))
    return out if batched else out[0]


def _unpack_rows(rows, shapes):
    out, r0 = [], 0
    for shp in shapes:
        size = 1
        for s in shp:
            size *= s
        nr = -(-size // ROW_BLOCK) * 8
        out.append(rows[r0:r0 + nr].reshape(-1)[:size].reshape(shp))
        r0 += nr
    return out


def _kernel_col_map():
    m = np.full(IN_PAD, -1, np.int64)
    m[C_MERGE:C_MERGE + 4096] = np.arange(6152, 10248)
    m[C_S5U:C_S5U + 1024] = np.arange(0, 1024)
    m[C_M2X:C_M2X + 1024] = np.arange(3072, 4096)
    m[C_SGU_U:C_SGU_U + 1536] = np.arange(1024, 2560)
    m[C_M2Z:C_M2Z + 512] = np.arange(2560, 3072)
    m[C_DT:C_DT + 8] = np.arange(4096, 4104)
    for j in range(4):
        for kind in range(4):
            k0 = C_SC + 4 * LANES * j + LANES * kind
            m[k0:k0 + LANES] = 4104 + BW * kind + LANES * j + np.arange(LANES)
    return m


def _lane_pieces(sources):
    pieces, cur = [], None
    for lane, src in enumerate(sources):
        key = None if src is None else (src[0], src[1] // LANES, (lane - src[1]) % LANES)
        if cur is not None and key == cur[0]:
            cur[2] = lane + 1
        else:
            if cur is not None and cur[0] is not None:
                pieces.append((*cur[0], cur[1], cur[2]))
            cur = [key, lane, lane + 1]
    if cur is not None and cur[0] is not None:
        pieces.append((*cur[0], cur[1], cur[2]))
    return pieces


def _assemble_block(pieces, load, rows, dtype):
    lane = lax.broadcasted_iota(jnp.int32, (rows, LANES), 1)
    out = None
    for arr, sb, shift, lo, hi in pieces:
        v = load(arr, sb)
        if shift:
            v = pltpu.roll(v, shift, 1)
        if out is None and lo == 0 and hi == LANES:
            out = v
        else:
            out = jnp.where((lane >= lo) & (lane < hi), v, jnp.zeros((rows, LANES), dtype) if out is None else out)
    return jnp.zeros((rows, LANES), dtype) if out is None else out


RELAYOUT_ROWS = 256
SHARD_BLOCKS = -(-SHARD_IN // LANES)


def _load_shard_block(ref, rows):
    def load(j, sb):
        if sb == SHARD_BLOCKS - 1:
            return jnp.broadcast_to(ref[j, :, SHARD_IN - 1:SHARD_IN], (rows, LANES))
        return ref[j, :, sb * LANES:(sb + 1) * LANES]
    return load


def _relayout_w_in(gathered, name):
    kmap = _kernel_col_map()
    dtype = gathered.dtype

    def body(src_ref, o_ref):
        load = _load_shard_block(src_ref, RELAYOUT_ROWS)
        for ob in range(IN_PAD // LANES):
            srcs = [None if kmap[ob * LANES + l] < 0 else (int(kmap[ob * LANES + l]) // SHARD_IN, int(kmap[ob * LANES + l]) % SHARD_IN)
                    for l in range(LANES)]
            o_ref[:, ob * LANES:(ob + 1) * LANES] = _assemble_block(_lane_pieces(srcs), load, RELAYOUT_ROWS, dtype)

    return pl.pallas_call(
        body, name=name, grid=(D_MODEL // RELAYOUT_ROWS,),
        in_specs=[pl.BlockSpec((N_DEV, RELAYOUT_ROWS, SHARD_IN), lambda i: (0, i, 0))],
        out_specs=pl.BlockSpec((RELAYOUT_ROWS, IN_PAD), lambda i: (i, 0)),
        out_shape=jax.ShapeDtypeStruct((D_MODEL, IN_PAD), dtype),
        compiler_params=_cparams(("parallel",)),
    )(gathered)


def _relayout_g_in(gw, name):
    kmap = _kernel_col_map()
    kinv = np.zeros(IN_DIM, np.int64)
    kinv[kmap[kmap >= 0]] = np.nonzero(kmap >= 0)[0]
    dtype = gw.dtype

    def body(src_ref, o_ref):
        load = lambda _, sb: src_ref[:, sb * LANES:(sb + 1) * LANES]
        for j in range(N_DEV):
            for ob in range(SHARD_BLOCKS):
                srcs = [(0, int(kinv[SHARD_IN * j + ob * LANES + l])) if ob * LANES + l < SHARD_IN else None for l in range(LANES)]
                blk = _assemble_block(_lane_pieces(srcs), load, RELAYOUT_ROWS, dtype)
                if ob == SHARD_BLOCKS - 1:
                    o_ref[j, :, SHARD_IN - 1:SHARD_IN] = blk[:, 0:1]
                else:
                    o_ref[j, :, ob * LANES:(ob + 1) * LANES] = blk

    return pl.pallas_call(
        body, name=name, grid=(D_MODEL // RELAYOUT_ROWS,),
        in_specs=[pl.BlockSpec((RELAYOUT_ROWS, IN_PAD), lambda i: (i, 0))],
        out_specs=pl.BlockSpec((N_DEV, RELAYOUT_ROWS, SHARD_IN), lambda i: (0, i, 0)),
        out_shape=jax.ShapeDtypeStruct((N_DEV, D_MODEL, SHARD_IN), dtype),
        compiler_params=_cparams(("parallel",)),
    )(gw)


def _rows128(flat, row_mult=8):
    n = flat.shape[0]
    per = LANES * row_mult
    total = -(-n // per) * per
    return jnp.pad(flat, (0, total - n)).reshape(total // LANES, LANES)


def _pad_lanes(v):
    return jnp.pad(v, (0, LANES - v.shape[0])).reshape(1, LANES)


def _layer_prep(i, p):
    disc, disc_vjp = jax.vjp(_s5_disc, *[p[n][i] for n in S5_NAMES])
    prep = dict(
        nw=p['norm_w'][i].reshape(1, D_MODEL), disc_vjp=disc_vjp,
        s5small=[_block_diag(t).astype(BF16) for t in disc[:4]] + [disc[4], disc[5]],
        sgw=[p['sgu_ln_w'][i].reshape(1, BW), p['sgu_ln_b'][i].reshape(1, BW), p['sgu_w'][i],
             jnp.repeat(p['sgu_b'][i].T, BW // SGU_HEADS, axis=1)],
        cb=p['m2_conv_b'][i].reshape(1, M2_CONV_CH),
        m2w=[_pad_lanes(p['m2_dt_bias'][i]), _pad_lanes(p['m2_a_log'][i]),
             jnp.repeat(p['m2_d'][i], M2_HEAD_DIM).reshape(1, BW), p['m2_norm_w'][i].reshape(1, BW)])
    touch = [t[0, 0].astype(F32) for t in prep['s5small']] + [prep['sgw'][3][0, 0], prep['m2w'][2][0, 0]]
    return prep, sum(touch[1:], touch[0])


def _layer_fwd(x, h, i, prep, w_in, other_weights, before_merge=None):
    proj = _matmul(h, w_in, 1, 0, F32, 1024, 1024, 1024, f"proj{i}")
    full = dict(other_weights(proj), w_in=w_in)
    s5w = prep['s5small'] + [full['s5_w_glu']]
    ya, sre, sim = _s5_fwd(proj, *s5w, f"s5_fwd{i}")
    yb = _sgu_fwd(proj, *prep['sgw'], f"sgu_fwd{i}")
    cw = full['m2_conv_w']
    xa = _m2_conv_fwd(proj, cw, prep['cb'], f"m2conv_fwd{i}")
    yc, s_in = _ssd_fwd(proj, xa, *prep['m2w'], f"ssd_fwd{i}")
    scw = full['sc_conv_w']
    yd = _sc_fwd(proj, scw, f"sc_fwd{i}")
    ys = jnp.stack([ya, yb, yc, yd])
    mb = full['merge_b'].reshape(N_BRANCH, 1, D_MODEL)
    if before_merge is not None:
        mb = mb + before_merge(ys)[0, 0]
    merged = _merge_fwd(proj, ys, mb, full['w_branch'], f"merge_fwd{i}")
    x_new = _matmul(merged, full['w_out'], 1, 0, F32, 1024, 1024, 1024, f"out{i}", residual=x)
    saved = dict(x=x, nw=prep['nw'], h=h, proj=proj, disc_vjp=prep['disc_vjp'], s5w=s5w, sre=sre, sim=sim, sgw=prep['sgw'],
                 cw=cw, cb=prep['cb'], xa=xa, m2w=prep['m2w'], s_in=s_in, scw=scw, ys=ys, mb=mb, merged=merged)
    return x_new, saved, full


def _layer_bwd(dx_out, i, sv, full, on_large_grads=None, after_dh=None):
    g = {}
    proj = sv['proj']
    dm = _matmul(dx_out, full['w_out'], 1, 1, F32, 1024, 1024, 1024, f"dmerged{i}")
    g['w_out'] = _matmul(sv['merged'], dx_out, 0, 0, BF16, 1024, 1024, 1024, f"gw_out{i}")
    dys, dproj, g['w_branch'], dmb = _merge_bwd(proj, sv['ys'], dm, sv['mb'], full['w_branch'], f"merge_bwd{i}")
    g['merge_b'] = dmb.reshape(N_BRANCH, D_MODEL)
    dproj, dbbre, dbbim, dcre, dcim, da, dd, dwg = _s5_bwd(proj, dproj, dys, sv['sre'], sv['sim'], *sv['s5w'], f"s5_bwd{i}")
    g['s5_dense'] = (dbbre, dbbim, dcre, dcim, da, dd)
    g['s5_w_glu'] = dwg.astype(BF16)
    dproj, dlw, dlb, g['sgu_w'], dbias = _sgu_bwd(proj, dproj, dys, *sv['sgw'], f"sgu_bwd{i}")
    g['sgu_ln_w'], g['sgu_ln_b'] = dlw[0], dlb[0]
    g['sgu_b'] = dbias.reshape(SGU_CHUNK, SGU_HEADS, BW // SGU_HEADS).sum(-1).T
    dproj, dxa, ddtb, dal, ddf, dnw = _ssd_bwd(proj, dproj, sv['xa'], dys, sv['s_in'], *sv['m2w'], f"ssd_bwd{i}")
    dproj, g['m2_conv_w'], dcb = _m2_conv_bwd(proj, dproj, dxa, sv['cw'], sv['cb'], f"m2conv_bwd{i}")
    g['m2_conv_b'], g['m2_norm_w'] = dcb[0], dnw[0]
    g['m2_dt_bias'], g['m2_a_log'] = ddtb[0, :M2_HEADS], dal[0, :M2_HEADS]
    g['m2_d'] = ddf.reshape(M2_HEADS, M2_HEAD_DIM).sum(-1)
    dproj, g['sc_conv_w'] = _sc_bwd(proj, dproj, dys, sv['scw'], f"sc_bwd{i}")
    g['w_in'] = _matmul(sv['h'], dproj, 0, 0, BF16, 1024, 1024, 1024, f"gw_in{i}")
    tok = on_large_grads(g) if on_large_grads else None
    dh = _matmul(dproj, full['w_in'], 1, 1, F32, 1024, 1024, 1024, f"dh{i}", after=tok)
    nw = sv['nw'] if after_dh is None else sv['nw'] + after_dh(dh)[0, 0]
    dx_in, dnw_l = _rmsnorm_bwd(sv['x'], nw, dh, dx_out, f"rms_bwd{i}")
    g['norm_w'] = dnw_l[0]
    return dx_in, g


def _split8(t, axis):
    shp = t.shape
    t = t.reshape(shp[:axis] + (N_DEV, shp[axis] // N_DEV) + shp[axis + 1:])
    return jnp.moveaxis(t, axis, 0)


def _join8(t, axis):
    t = jnp.moveaxis(t, 0, axis)
    shp = t.shape
    return t.reshape(shp[:axis] + (shp[axis] * shp[axis + 1],) + shp[axis + 2:])


SHARD_AXIS = {'w_in': 2, 'w_branch': 3, 'w_out': 1, 's5_w_glu': 1, 'm2_conv_w': 2, 'sc_conv_w': 2, 'merge_b': 2}


OTHER_BIG = [n for n in BIG_SHARDED if n != 'w_in']


def _other_weights(gathered):
    return {n: _join8(t, SHARD_AXIS[n] - 1) for n, t in zip(OTHER_BIG, gathered)}


def _layer_grad_blocks(g, i):
    blocks = [_relayout_g_in(g[n], f"relayout_g_in{i}") if n == 'w_in' else _split8(g[n], SHARD_AXIS[n] - 1) for n in BIG_SHARDED]
    return [b.reshape(N_DEV, -1, b.shape[-1]) for b in blocks]


def _pair_start(blocks, tag):
    shapes = [(N_CHIP,) + b.shape[1:] for b in blocks]
    return _split_start(_plan_pair, blocks, shapes, N_CHIP * len(blocks), f"pair{tag}_start")


def _pair_sums(state, after, tag):
    mine, theirs = _split_wait(_plan_pair, state, after, f"pair{tag}_wait", with_sources=True)
    return [_pair_sum(b, t, f"pair_sum{tag}_{k}") for k, (b, t) in enumerate(zip(mine, theirs))]


def _chips_start(sums, tag, after=None):
    return _split_start(_plan_chips, sums, [s.shape for s in sums], 3 * len(sums), f"chips{tag}_start", after)


def kernel(x, norm_w, w_in, s5_lambda_re, s5_lambda_im, s5_b_re, s5_b_im, s5_c_re, s5_c_im, s5_d, s5_log_step, s5_w_glu, sgu_ln_w, sgu_ln_b, sgu_w, sgu_b, m2_conv_w, m2_conv_b, m2_dt_bias, m2_a_log, m2_d, m2_norm_w, sc_conv_w, merge_b, w_branch, w_out, final_norm_w, loss_target, m_norm_w, m_w_in, m_s5_lambda_re, m_s5_lambda_im, m_s5_b_re, m_s5_b_im, m_s5_c_re, m_s5_c_im, m_s5_d, m_s5_log_step, m_s5_w_glu, m_sgu_ln_w, m_sgu_ln_b, m_sgu_w, m_sgu_b, m_m2_conv_w, m_m2_conv_b, m_m2_dt_bias, m_m2_a_log, m_m2_d, m_m2_norm_w, m_sc_conv_w, m_merge_b, m_w_branch, m_w_out, m_final_norm_w, v_norm_w, v_w_in, v_s5_lambda_re, v_s5_lambda_im, v_s5_b_re, v_s5_b_im, v_s5_c_re, v_s5_c_im, v_s5_d, v_s5_log_step, v_s5_w_glu, v_sgu_ln_w, v_sgu_ln_b, v_sgu_w, v_sgu_b, v_m2_conv_w, v_m2_conv_b, v_m2_dt_bias, v_m2_a_log, v_m2_d, v_m2_norm_w, v_sc_conv_w, v_merge_b, v_w_branch, v_w_out, v_final_norm_w):
    loc = locals()
    p = {n: loc[n] for n in WEIGHTS}
    mom = {n: loc['m_' + n] for n in WEIGHTS}
    vel = {n: loc['v_' + n] for n in WEIGHTS}

    small_sizes = [p[n].size for n in SMALL_SHARDED]
    small_pack = _rows128(jnp.concatenate([p[n].reshape(-1) for n in SMALL_SHARDED]))
    first = [p['w_in'][0].astype(BF16)]
    gath_first, tok = _split_start(_plan_gather, first, [(N_DEV,) + first[0].shape], 7, "gather_w_in0_start")
    shards = ([(p[n][0] + tok[0, 0]).astype(BF16) for n in OTHER_BIG] + [small_pack + tok[0, 0]]
              + [(p[n][1] + tok[0, 0]).astype(BF16) for n in BIG_SHARDED])
    gath, tok = _split_start(_plan_gather, shards, [(N_DEV,) + t.shape for t in shards], 7 * len(shards), "gather_start")

    def relayed(lo, hi, after, name, started=None):
        started = gath if started is None else started
        n = (len(started) - 3) // 2
        sems, srcs, lands = started[:3], started[3:3 + n], started[3 + n:]
        plan = functools.partial(_plan_gather, first=lo)
        state, tok = _split_relay(plan, (*sems, *srcs[lo:hi], *lands[lo:hi]), after, name + "_relay")
        return (plan, state, name), tok

    def arrived(relay, after):
        plan, state, name = relay
        return _split_wait(plan, state, after, name + "_wait")

    def gathered(lo, hi, after, name, started=None):
        relay, tok = relayed(lo, hi, after, name, started)
        return arrived(relay, tok)

    later = dict(p, **{n: p[n] + tok[0, 0] for n in ('norm_w', 's5_log_step', 'sgu_b', 'm2_d')})
    preps = [_layer_prep(i, later) for i in range(DEPTH)]
    h0 = _rmsnorm_fwd(x[0], preps[0][0]['nw'], "rms_fwd0")
    got = gathered(0, 1, tok + (preps[0][1] + preps[1][1] + h0[0, 0].astype(F32)), "gather_w_in0", gath_first)
    small_full = {}

    def other_weights0(proj):
        got = gathered(0, 4, proj, "gather_rest0")
        small_all, off = got[-1].reshape(N_DEV, -1), 0
        for n, sz in zip(SMALL_SHARDED, small_sizes):
            small_full[n] = _join8(small_all[:, off:off + sz].reshape((N_DEV,) + p[n].shape), SHARD_AXIS[n])
            off += sz
        return dict(_other_weights(got[:-1]), **{n: small_full[n][0] for n in SMALL_SHARDED})

    saved, layer_g, full = [None] * DEPTH, [None] * DEPTH, [None] * DEPTH
    relay1 = []

    def relay_layer1(ys):
        relay, tok = relayed(4, 8, ys, "gather1")
        relay1.append(relay)
        return tok

    xs, saved[0], full[0] = _layer_fwd(x[0], h0, 0, preps[0][0], _relayout_w_in(got[0], "relayout_w_in0"), other_weights0,
                                       relay_layer1)
    h1 = _rmsnorm_fwd(xs, preps[1][0]['nw'], "rms_fwd1")
    got = arrived(relay1[0], h1)
    xs, saved[1], full[1] = _layer_fwd(
        xs, h1, 1, preps[1][0], _relayout_w_in(got[0], "relayout_w_in1"),
        lambda proj: dict(_other_weights(got[1:]), **{n: small_full[n][1] for n in SMALL_SHARDED}))
    loss_row, dx, dfw = _loss_head(xs, final_norm_w.reshape(1, D_MODEL), loss_target[0])
    loss = lax.psum(loss_row[0, 0], ("x", "y", "c"))
    loss, dx = lax.optimization_barrier((loss, dx))
    pairs, scat, sent0 = [None] * DEPTH, [None] * DEPTH, []

    def start_pairs1(g):
        pairs[1], tok = _pair_start(_layer_grad_blocks(g, 1), 1)
        return tok

    def send_chip_sums1(dh):
        scat[1], tok = _chips_start(_pair_sums(pairs[1], dh, 1), 1)
        return tok

    def send_all0(g):
        pairs[0], tok = _pair_start(_layer_grad_blocks(g, 0), 0)
        scat[0], tok = _chips_start(_pair_sums(pairs[0], tok, 0), 0)
        sent0.append(tok)
        return tok

    dx, layer_g[1] = _layer_bwd(dx, 1, saved[1], full[1], on_large_grads=start_pairs1, after_dh=send_chip_sums1)
    dx, layer_g[0] = _layer_bwd(dx, 0, saved[0], full[0], on_large_grads=send_all0)
    for i in range(DEPTH):
        dense = layer_g[i].pop('s5_dense')
        blocks = tuple(_diag_blocks(t, after=sent0[0]) for t in dense[:4])
        layer_g[i].update(zip(S5_NAMES, saved[i]['disc_vjp'](blocks + (dense[4] + sent0[0][0, 0], dense[5]))))
    grads = {n: jnp.stack([layer_g[i][n] for i in range(DEPTH)]) for n in SMALL_SHARDED + REPLICATED if n != 'final_norm_w'}
    grads['final_norm_w'] = dfw[0]

    out_g, out_d, out_m, out_v = {}, {}, {}, {}
    repl_rows = _pack_rows([grads[n] for n in REPLICATED], 8 * N_DEV)
    rr = repl_rows.shape[0] // N_DEV
    shard_rows = _pack_rows([_split8(grads[n], SHARD_AXIS[n]) for n in SMALL_SHARDED], 8, batched=True)
    rs = shard_rows.shape[1]
    small_g = jnp.concatenate([shard_rows, repl_rows.reshape(N_DEV, rr, LANES)], axis=1)
    all_to_all, all_gather = functools.partial(_plan_direct, gather=False), functools.partial(_plan_direct, gather=True)
    small_state, tok = _split_start(all_to_all, [small_g], [small_g.shape], N_DEV - 1, "scatter_small_start")
    landed1 = _split_wait(_plan_chips, scat[1], tok, "chips1_wait")
    landed0 = _split_wait(_plan_chips, scat[0], landed1[0], "chips0_wait")

    def big_adamw(n, after=None):
        k, shp = BIG_SHARDED.index(n), p[n].shape
        if n == 'w_in':
            return _adamw_w_in([landed0[k], landed1[k]], p[n], mom[n], vel[n], "adamw_w_in", after)
        c = shp[-1]
        r = p[n].size // (DEPTH * c)
        res = _adamw([landed0[k], landed1[k]], *[d[n].reshape(DEPTH, r, c) for d in (p, mom, vel)],
                     {'w_branch': 512, 'w_out': 128, 's5_w_glu': 64}[n], "adamw_" + n)
        return [o.reshape(shp) for o in res]

    for n in OTHER_BIG:
        out_g[n], out_d[n], out_m[n], out_v[n] = big_adamw(n)
    updated = sum(out_d[n].reshape(-1)[0] for n in OTHER_BIG).reshape(1, 1)
    small_sum = _slot_sum(_split_wait(all_to_all, small_state, updated, "scatter_small_wait")[0], "sum_small")
    repl_part = small_sum[rs:]
    repl_state, tok = _split_start(all_gather, [repl_part], [(N_DEV,) + repl_part.shape], N_DEV - 1, "gather_small_start")
    out_g['w_in'], out_d['w_in'], out_m['w_in'], out_v['w_in'] = big_adamw('w_in', tok)
    repl_all = _split_wait(all_gather, repl_state, out_d['w_in'], "gather_small_wait")[0].reshape(N_DEV * rr, LANES)
    g_all = jnp.concatenate([small_sum[:rs], repl_all], axis=0)
    names = SMALL_SHARDED + REPLICATED
    pieces = (_unpack_rows(g_all[:rs], [p[n].shape for n in SMALL_SHARDED])
              + _unpack_rows(g_all[rs:], [p[n].shape for n in REPLICATED]))
    out_g.update(zip(names, pieces))
    res = _adamw_many(*[[_memory_view(n, d[n]) for n in names] for d in (out_g, p, mom, vel)], "adamw_small")
    for r, dst in zip(res, (out_d, out_m, out_v)):
        dst.update({n: _memory_view(n, t) for n, t in zip(names, r)})
    return (loss, dx[None], *[out_g[n] for n in WEIGHTS], *[out_d[n] for n in WEIGHTS],
            *[out_m[n] for n in WEIGHTS], *[out_v[n] for n in WEIGHTS])
```

```python
import functools

import jax
import jax.numpy as jnp
import numpy as np
from jax import lax
from jax.experimental import pallas as pl
from jax.experimental.pallas import tpu as pltpu

F32 = jnp.float32
BF16 = jnp.bfloat16

N_DEV = 8
SEQ = 2048
D_MODEL = 1024
DEPTH = 2
BW = 512
N_BRANCH = 4
EPS = 1e-6
S5_GROUPS, S5_STATE, S5_P = 32, 64, 16
S5_CH = S5_GROUPS * S5_STATE
SGU_CHUNK, SGU_HEADS = 128, 8
M2_HEADS, M2_HEAD_DIM, M2_STATE, M2_CHUNK, M2_CONV = 8, 64, 128, 128, 4
M2_CONV_CH = 1024
SC_CONV = 3
IN_DIM = 10248
IN_PAD = 11264
C_MERGE = 0
C_S5U, C_S5G = 4096, 4608
C_M2X = 5120
C_SGU_U, C_SGU_V, C_SGU_G = 6144, 6656, 7168
C_M2Z, C_DT = 8192, 8704
C_SC = 9216
SHARD_IN = IN_DIM // N_DEV

ADAM_LR, ADAM_B1, ADAM_B2, ADAM_EPS, ADAM_WD, ADAM_STEP = 0.001, 0.9, 0.999, 1e-08, 0.01, 10

VMEM_LIMIT = 56 * 1024 * 1024
LANES = 128

MESH = pl.DeviceIdType.MESH


def _cparams(sem=None, **kw):
    return pltpu.CompilerParams(dimension_semantics=sem, vmem_limit_bytes=VMEM_LIMIT, **kw)


def _dg(a, b, ca, cb, precision=None):
    return lax.dot_general(a, b, (((ca,), (cb,)), ((), ())), precision=precision,
                           preferred_element_type=F32)


@functools.partial(jax.custom_vjp, nondiff_argnums=(2, 3))
def _bdot(a, b, ca, cb):
    return _dg(a.astype(BF16), b.astype(BF16), ca, cb)


def _bdot_fwd(a, b, ca, cb):
    return _bdot(a, b, ca, cb), (a, b)


def _bdot_bwd(ca, cb, res, g):
    a, b = res
    gb, ab, bb = g.astype(BF16), a.astype(BF16), b.astype(BF16)
    da = _dg(gb, bb, 1, 1 - cb) if ca == 1 else _dg(bb, gb, 1 - cb, 1)
    db = _dg(ab, gb, 1 - ca, 0) if cb == 0 else _dg(gb, ab, 0, 1 - ca)
    return da.astype(a.dtype), db.astype(b.dtype)


_bdot.defvjp(_bdot_fwd, _bdot_bwd)


def _rms(x, w):
    return x * lax.rsqrt(jnp.mean(x * x, axis=-1, keepdims=True) + EPS) * w


def _silu(x):
    return x * jax.nn.sigmoid(x)


def _gelu(x):
    return 0.5 * x * (1.0 + jnp.tanh(0.7978845608028654 * (x + 0.044715 * (x * x * x))))


def _softplus(x):
    return jnp.maximum(x, 0.0) + jnp.log1p(jnp.exp(-jnp.abs(x)))


def _shift_down(x, s):
    if s == 0:
        return x
    row = lax.broadcasted_iota(jnp.int32, x.shape, 0)
    return jnp.where(row >= s, pltpu.roll(x, s, 0), 0.0)


def _shift_up(x, s):
    if s == 0:
        return x
    n = x.shape[0]
    row = lax.broadcasted_iota(jnp.int32, x.shape, 0)
    return jnp.where(row < n - s, pltpu.roll(x, n - s, 0), 0.0)


def _matmul(a, b, ca, cb, out_dtype, tm, tn, tk, name, residual=None, after=None):
    m = a.shape[1 - ca]
    k = a.shape[ca]
    n = b.shape[1 - cb]
    assert b.shape[cb] == k and m % tm == 0 and n % tn == 0 and k % tk == 0
    nk = k // tk
    a_spec = pl.BlockSpec((tm, tk), lambda i, j, kk: (i, kk)) if ca == 1 else pl.BlockSpec((tk, tm), lambda i, j, kk: (kk, i))
    b_spec = pl.BlockSpec((tk, tn), lambda i, j, kk: (kk, j)) if cb == 0 else pl.BlockSpec((tn, tk), lambda i, j, kk: (j, kk))
    o_spec = pl.BlockSpec((tm, tn), lambda i, j, kk: (i, j))
    has_res = residual is not None

    def body(*refs):
        refs = refs[:2 + has_res] + refs[2 + has_res + (after is not None):]
        if has_res:
            a_ref, b_ref, r_ref, o_ref, acc = refs
        else:
            a_ref, b_ref, o_ref, acc = refs
        kk = pl.program_id(2)
        part = _dg(a_ref[...].astype(BF16), b_ref[...].astype(BF16), ca, cb)

        @pl.when(kk == 0)
        def _():
            acc[...] = part

        @pl.when(kk > 0)
        def _():
            acc[...] += part

        @pl.when(kk == nk - 1)
        def _():
            r = acc[...]
            if has_res:
                r = r + r_ref[...]
            o_ref[...] = r.astype(out_dtype)

    ins = [a, b] + ([residual] if has_res else []) + ([after] if after is not None else [])
    specs = [a_spec, b_spec] + ([o_spec] if has_res else []) + ([pl.BlockSpec(memory_space=pl.ANY)] if after is not None else [])
    return pl.pallas_call(
        body, name=name, grid=(m // tm, n // tn, nk), in_specs=specs, out_specs=o_spec,
        out_shape=jax.ShapeDtypeStruct((m, n), out_dtype),
        scratch_shapes=[pltpu.VMEM((tm, tn), F32)],
        compiler_params=_cparams(("parallel", "parallel", "arbitrary")),
    )(*ins)


ROW_TILE = 512


def _rmsnorm_fwd(x, w, name):
    def body(x_ref, w_ref, o_ref):
        o_ref[...] = _rms(x_ref[...], w_ref[...]).astype(BF16)

    return pl.pallas_call(
        body, name=name, grid=(SEQ // ROW_TILE,),
        in_specs=[pl.BlockSpec((ROW_TILE, D_MODEL), lambda i: (i, 0)), pl.BlockSpec((1, D_MODEL), lambda i: (0, 0))],
        out_specs=pl.BlockSpec((ROW_TILE, D_MODEL), lambda i: (i, 0)),
        out_shape=jax.ShapeDtypeStruct((SEQ, D_MODEL), BF16),
        compiler_params=_cparams(("parallel",)),
    )(x, w)


def _rmsnorm_bwd(x, w, dh, dres, name):
    def body(x_ref, w_ref, dh_ref, dres_ref, dx_ref, dw_ref):
        _, vjp = jax.vjp(_rms, x_ref[...], w_ref[...])
        dx, dw = vjp(dh_ref[...])
        dx_ref[...] = dx + dres_ref[...]

        @pl.when(pl.program_id(0) == 0)
        def _():
            dw_ref[...] = dw

        @pl.when(pl.program_id(0) > 0)
        def _():
            dw_ref[...] += dw

    tile = pl.BlockSpec((ROW_TILE, D_MODEL), lambda i: (i, 0))
    vec = pl.BlockSpec((1, D_MODEL), lambda i: (0, 0))
    return pl.pallas_call(
        body, name=name, grid=(SEQ // ROW_TILE,),
        in_specs=[tile, vec, tile, tile], out_specs=[tile, vec],
        out_shape=[jax.ShapeDtypeStruct((SEQ, D_MODEL), F32), jax.ShapeDtypeStruct((1, D_MODEL), F32)],
        compiler_params=_cparams(("arbitrary",)),
    )(x, w, dh, dres)


def _loss_head(x, w, target):
    def body(x_ref, w_ref, t_ref, loss_ref, dx_ref, dw_ref):
        tgt = t_ref[...]

        def f(xv, wv):
            err = _rms(xv, wv) - tgt
            return 0.5 * jnp.sum(jnp.mean(err * err, axis=-1))

        loss, vjp = jax.vjp(f, x_ref[...], w_ref[...])
        dx, dw = vjp(jnp.ones((), F32))
        dx_ref[...] = dx
        lrow = jnp.full((1, LANES), loss, F32)

        @pl.when(pl.program_id(0) == 0)
        def _():
            dw_ref[...] = dw
            loss_ref[...] = lrow

        @pl.when(pl.program_id(0) > 0)
        def _():
            dw_ref[...] += dw
            loss_ref[...] += lrow

    tile = pl.BlockSpec((ROW_TILE, D_MODEL), lambda i: (i, 0))
    vec = pl.BlockSpec((1, D_MODEL), lambda i: (0, 0))
    return pl.pallas_call(
        body, name="loss_head", grid=(SEQ // ROW_TILE,),
        in_specs=[tile, vec, tile], out_specs=[pl.BlockSpec((1, LANES), lambda i: (0, 0)), tile, vec],
        out_shape=[jax.ShapeDtypeStruct((1, LANES), F32), jax.ShapeDtypeStruct((SEQ, D_MODEL), F32),
                   jax.ShapeDtypeStruct((1, D_MODEL), F32)],
        compiler_params=_cparams(("arbitrary",)),
    )(x, w, target)


S5_T = 256
S5_BLOCKS = [(slice(j * 256, (j + 1) * 256), slice(j * 1024, (j + 1) * 1024)) for j in range(2)]


def _s5_post(ypre, gate, wglu):
    y = _gelu(ypre)
    y = y * jax.nn.sigmoid(_bdot(y, wglu, 1, 0))
    return y * _silu(gate)


def _s5_fwd(proj, bbre, bbim, cre, cim, a2, dvec, wglu, name):
    def body(u_ref, g_ref, bbre_ref, bbim_ref, cre_ref, cim_ref, a_ref, d_ref, wg_ref, o_ref, sre_ref, sim_ref, st):
        @pl.when(pl.program_id(0) == 0)
        def _():
            st[...] = jnp.zeros_like(st)

        u = u_ref[...]
        ub = u.astype(BF16)
        for us, ss in S5_BLOCKS:
            sre_ref[:, ss] = _dg(ub[:, us], bbre_ref[us, ss], 1, 0)
            sim_ref[:, ss] = _dg(ub[:, us], bbim_ref[us, ss], 1, 0)
        ar, ai = a_ref[0:1, :], a_ref[1:2, :]

        def step(t, carry):
            sr, si = carry
            nr = ar * sr - ai * si + sre_ref[pl.ds(t, 1), :]
            ni = ar * si + ai * sr + sim_ref[pl.ds(t, 1), :]
            sre_ref[pl.ds(t, 1), :] = nr
            sim_ref[pl.ds(t, 1), :] = ni
            return nr, ni

        sr, si = lax.fori_loop(0, S5_T, step, (st[0:1, :], st[1:2, :]), unroll=8)
        st[0:1, :] = sr
        st[1:2, :] = si
        ypre = jnp.concatenate(
            [_dg(sre_ref[:, ss].astype(BF16), cre_ref[ss, us], 1, 0) - _dg(sim_ref[:, ss].astype(BF16), cim_ref[ss, us], 1, 0)
             for us, ss in S5_BLOCKS], axis=1) + d_ref[...] * u
        o_ref[...] = _s5_post(ypre, g_ref[...], wg_ref[...]).astype(BF16)

    full = lambda shape: pl.BlockSpec(shape, lambda c: (0, 0))
    return pl.pallas_call(
        body, name=name, grid=(SEQ // S5_T,),
        in_specs=[pl.BlockSpec((S5_T, BW), lambda c: (c, C_S5U // BW)), pl.BlockSpec((S5_T, BW), lambda c: (c, C_S5G // BW)),
                  full((BW, S5_CH)), full((BW, S5_CH)), full((S5_CH, BW)), full((S5_CH, BW)),
                  full((2, S5_CH)), full((1, BW)), full((BW, BW))],
        out_specs=[pl.BlockSpec((S5_T, BW), lambda c: (c, 0)), pl.BlockSpec((S5_T, S5_CH), lambda c: (c, 0)),
                   pl.BlockSpec((S5_T, S5_CH), lambda c: (c, 0))],
        out_shape=[jax.ShapeDtypeStruct((SEQ, BW), BF16), jax.ShapeDtypeStruct((SEQ, S5_CH), F32),
                   jax.ShapeDtypeStruct((SEQ, S5_CH), F32)],
        scratch_shapes=[pltpu.VMEM((2, S5_CH), F32)],
        compiler_params=_cparams(("arbitrary",)),
    )(proj, proj, bbre, bbim, cre, cim, a2, dvec, wglu)


def _s5_bwd(proj, dproj, dout, sre, sim, bbre, bbim, cre, cim, a2, dvec, wglu, name):
    nc = SEQ // S5_T

    def body(u_ref, g_ref, do_ref, sre_ref, sim_ref, pre_ref, pim_ref, bbre_ref, bbim_ref, cre_ref, cim_ref, a_ref,
             d_ref, wg_ref, dproj_in, dp_ref, dbbre_ref, dbbim_ref, dcre_ref, dcim_ref, da_ref, dd_ref, dwg_ref,
             gre, gim, st):
        c = nc - 1 - pl.program_id(0)

        @pl.when(pl.program_id(0) == 0)
        def _():
            st[...] = jnp.zeros_like(st)
            for r in (dbbre_ref, dbbim_ref, dcre_ref, dcim_ref, da_ref, dd_ref, dwg_ref):
                r[...] = jnp.zeros_like(r)

        u = u_ref[...]
        s_re, s_im = sre_ref[...], sim_ref[...]

        def head(s_res, s_ims, cres, cims, dv, uv, gv, wg):
            ypre = jnp.concatenate([_bdot(sr, cr, 1, 0) - _bdot(si, ci, 1, 0)
                                    for sr, si, cr, ci in zip(s_res, s_ims, cres, cims)], axis=1) + dv * uv
            return _s5_post(ypre, gv, wg)

        _, vjp = jax.vjp(head, [sre_ref[:, ss] for _, ss in S5_BLOCKS], [sim_ref[:, ss] for _, ss in S5_BLOCKS],
                         [cre_ref[ss, us].astype(F32) for us, ss in S5_BLOCKS],
                         [cim_ref[ss, us].astype(F32) for us, ss in S5_BLOCKS],
                         d_ref[...], u, g_ref[...], wg_ref[...].astype(F32))
        ds_res, ds_ims, dcres, dcims, dd, du_d, dgate, dwg = vjp(do_ref[0])
        for k, (us, ss) in enumerate(S5_BLOCKS):
            dcre_ref[ss, us] += dcres[k]
            dcim_ref[ss, us] += dcims[k]
            gre[:, ss] = ds_res[k]
            gim[:, ss] = ds_ims[k]
        dd_ref[...] += dd
        dwg_ref[...] += dwg
        dp_ref[:, BW:] = dgate.astype(BF16)
        ar, ai = a_ref[0:1, :], a_ref[1:2, :]

        def step(i, carry):
            t = S5_T - 1 - i
            gr, gi = carry
            nr = gre[pl.ds(t, 1), :] + gr
            ni = gim[pl.ds(t, 1), :] + gi
            gre[pl.ds(t, 1), :] = nr
            gim[pl.ds(t, 1), :] = ni
            return ar * nr + ai * ni, ar * ni - ai * nr

        gr, gi = lax.fori_loop(0, S5_T, step, (st[0:1, :], st[1:2, :]), unroll=8)
        st[0:1, :] = gr
        st[1:2, :] = gi
        g_re, g_im = gre[...], gim[...]
        first = jnp.where(c > 0, 1.0, 0.0)
        row = lax.broadcasted_iota(jnp.int32, (S5_T, S5_CH), 0)
        p_re = jnp.where(row == 0, pre_ref[7:8, :] * first, pltpu.roll(s_re, 1, 0))
        p_im = jnp.where(row == 0, pim_ref[7:8, :] * first, pltpu.roll(s_im, 1, 0))
        da_ref[0:1, :] += jnp.sum(g_re * p_re + g_im * p_im, axis=0, keepdims=True)
        da_ref[1:2, :] += jnp.sum(g_im * p_re - g_re * p_im, axis=0, keepdims=True)
        ub, grb, gib = u.astype(BF16), g_re.astype(BF16), g_im.astype(BF16)
        du_s = []
        for us, ss in S5_BLOCKS:
            dbbre_ref[us, ss] += _dg(ub[:, us], grb[:, ss], 0, 0)
            dbbim_ref[us, ss] += _dg(ub[:, us], gib[:, ss], 0, 0)
            du_s.append(_dg(grb[:, ss], bbre_ref[us, ss], 1, 1) + _dg(gib[:, ss], bbim_ref[us, ss], 1, 1))
        dp_ref[:, :BW] = (du_d + jnp.concatenate(du_s, axis=1)).astype(BF16)

    full = lambda shape: pl.BlockSpec(shape, lambda i: (0, 0))
    rev = lambda w, col=0: pl.BlockSpec((S5_T, w), lambda i: (nc - 1 - i, col))
    prev = pl.BlockSpec((8, S5_CH), lambda i: (jnp.maximum((nc - 1 - i) * (S5_T // 8) - 1, 0), 0))
    return pl.pallas_call(
        body, name=name, grid=(nc,),
        in_specs=[rev(BW, C_S5U // BW), rev(BW, C_S5G // BW), pl.BlockSpec((1, S5_T, BW), lambda i: (0, nc - 1 - i, 0)),
                  rev(S5_CH), rev(S5_CH), prev, prev,
                  full((BW, S5_CH)), full((BW, S5_CH)), full((S5_CH, BW)), full((S5_CH, BW)),
                  full((2, S5_CH)), full((1, BW)), full((BW, BW)), pl.BlockSpec(memory_space=pl.ANY)],
        out_specs=[rev(2 * BW, C_S5U // (2 * BW)), full((BW, S5_CH)), full((BW, S5_CH)), full((S5_CH, BW)), full((S5_CH, BW)),
                   full((2, S5_CH)), full((1, BW)), full((BW, BW))],
        input_output_aliases={14: 0},
        out_shape=[jax.ShapeDtypeStruct((SEQ, IN_PAD), BF16),
                   jax.ShapeDtypeStruct((BW, S5_CH), F32), jax.ShapeDtypeStruct((BW, S5_CH), F32),
                   jax.ShapeDtypeStruct((S5_CH, BW), F32), jax.ShapeDtypeStruct((S5_CH, BW), F32),
                   jax.ShapeDtypeStruct((2, S5_CH), F32), jax.ShapeDtypeStruct((1, BW), F32),
                   jax.ShapeDtypeStruct((BW, BW), F32)],
        scratch_shapes=[pltpu.VMEM((S5_T, S5_CH), F32), pltpu.VMEM((S5_T, S5_CH), F32), pltpu.VMEM((2, S5_CH), F32)],
        compiler_params=_cparams(("arbitrary",)),
    )(proj, proj, dout, sre, sim, sre, sim, bbre, bbim, cre, cim, a2, dvec, wglu, dproj)


def _diag_blocks(dense, after=None):
    rows, cols = dense.shape
    rows_per, cols_per = rows // S5_GROUPS, cols // S5_GROUPS
    per_lane_block = LANES // cols_per
    tile = 512

    def body(d_ref, *rest):
        o_ref = rest[-1]
        r0 = pl.program_id(0) * tile
        grp = (r0 + lax.broadcasted_iota(jnp.int32, (tile, LANES), 0)) // rows_per
        lane = lax.broadcasted_iota(jnp.int32, (tile, LANES), 1)
        acc = jnp.zeros((tile, LANES), F32)
        for hb in range(cols // LANES):
            acc = acc + jnp.where(grp == per_lane_block * hb + lane // cols_per, d_ref[:, hb * LANES:(hb + 1) * LANES], 0.0)
        shift = LANES // 2
        while shift >= cols_per:
            acc = acc + pltpu.roll(acc, LANES - shift, 1)
            shift //= 2
        o_ref[...] = acc

    folded = pl.pallas_call(
        body, name=f"diag_blocks_{rows_per}x{cols_per}", grid=(rows // tile,),
        in_specs=[pl.BlockSpec((tile, cols), lambda i: (i, 0))] + ([] if after is None else [pl.BlockSpec(memory_space=pl.ANY)]),
        out_specs=pl.BlockSpec((tile, LANES), lambda i: (i, 0)),
        out_shape=jax.ShapeDtypeStruct((rows, LANES), F32), compiler_params=_cparams(("parallel",)),
    )(dense, *([] if after is None else [after]))
    return folded[:, :cols_per].reshape(S5_GROUPS, rows_per, cols_per)


def _block_diag(t):
    g, rows_per, cols_per = t.shape
    wide = jnp.tile(t.reshape(g * rows_per, cols_per), (1, g))
    r = lax.broadcasted_iota(jnp.int32, wide.shape, 0) // rows_per
    c = lax.broadcasted_iota(jnp.int32, wide.shape, 1) // cols_per
    return jnp.where(r == c, wide, 0.0)


def _s5_disc(lam_re, lam_im, b_re, b_im, c_re, c_im, d, log_step):
    step = jnp.exp(log_step)[:, None]
    mag = jnp.exp(lam_re * step)
    ab_re, ab_im = mag * jnp.cos(lam_im * step), mag * jnp.sin(lam_im * step)
    den = lam_re * lam_re + lam_im * lam_im
    nr = ab_re - 1.0
    coef_re = (nr * lam_re + ab_im * lam_im) / den
    coef_im = (ab_im * lam_re - nr * lam_im) / den
    bb_re = coef_re[..., None] * b_re - coef_im[..., None] * b_im
    bb_im = coef_re[..., None] * b_im + coef_im[..., None] * b_re
    a2 = jnp.stack([ab_re.reshape(-1), ab_im.reshape(-1)])
    return (jnp.swapaxes(bb_re, 1, 2), jnp.swapaxes(bb_im, 1, 2),
            jnp.swapaxes(c_re, 1, 2), jnp.swapaxes(c_im, 1, 2),
            a2, d.reshape(1, BW))


def _left_lanes(shape):
    return lax.broadcasted_iota(jnp.int32, shape, 1) < 64


def _sgu_chunk(u, v, gate, ln_w, ln_b, w, bias):
    u32, v32 = _gelu(u), _gelu(v)
    mu = jnp.mean(v32, axis=-1, keepdims=True)
    var = jnp.mean(jnp.square(v32 - mu), axis=-1, keepdims=True)
    vn = (v32 - mu) * lax.rsqrt(var + EPS) * ln_w + ln_b
    t_i = lax.broadcasted_iota(jnp.int32, (SGU_CHUNK, SGU_CHUNK), 0)
    s_i = lax.broadcasted_iota(jnp.int32, (SGU_CHUNK, SGU_CHUNK), 1)
    causal = t_i >= s_i
    left = _left_lanes((SGU_CHUNK, LANES))
    sgate = _silu(gate)
    outs = []
    for j in range(BW // LANES):
        vb = vn[:, j * LANES:(j + 1) * LANES]
        s_blk = (_bdot(jnp.where(causal, w[2 * j], 0.0), jnp.where(left, vb, 0.0), 1, 0)
                 + _bdot(jnp.where(causal, w[2 * j + 1], 0.0), jnp.where(left, 0.0, vb), 1, 0))
        sl = slice(j * LANES, (j + 1) * LANES)
        outs.append(u32[:, sl] * (s_blk + bias[:, sl]) * sgate[:, sl])
    return outs


def _sgu_fwd(proj, ln_w, ln_b, w, bias, name):
    def body(u_ref, v_ref, g_ref, lw_ref, lb_ref, w_ref, b_ref, o_ref):
        outs = _sgu_chunk(u_ref[...], v_ref[...], g_ref[...], lw_ref[...], lb_ref[...], w_ref[...], b_ref[...])
        for j, o in enumerate(outs):
            o_ref[:, j * LANES:(j + 1) * LANES] = o.astype(BF16)

    blk = lambda col: pl.BlockSpec((SGU_CHUNK, BW), lambda c: (c, col // BW))
    vec = pl.BlockSpec((1, BW), lambda c: (0, 0))
    return pl.pallas_call(
        body, name=name, grid=(SEQ // SGU_CHUNK,),
        in_specs=[blk(C_SGU_U), blk(C_SGU_V), blk(C_SGU_G), vec, vec,
                  pl.BlockSpec((SGU_HEADS, SGU_CHUNK, SGU_CHUNK), lambda c: (0, 0, 0)),
                  pl.BlockSpec((SGU_CHUNK, BW), lambda c: (0, 0))],
        out_specs=pl.BlockSpec((SGU_CHUNK, BW), lambda c: (c, 0)),
        out_shape=jax.ShapeDtypeStruct((SEQ, BW), BF16),
        compiler_params=_cparams(("parallel",)),
    )(proj, proj, proj, ln_w, ln_b, w, bias)


def _sgu_bwd(proj, dproj, dout, ln_w, ln_b, w, bias, name):
    def body(u_ref, v_ref, g_ref, do_ref, lw_ref, lb_ref, w_ref, b_ref, dproj_in, dp_ref, dlw_ref, dlb_ref, dw_ref, db_ref):
        _, vjp = jax.vjp(_sgu_chunk, u_ref[...], v_ref[...], g_ref[...], lw_ref[...], lb_ref[...], w_ref[...], b_ref[...])
        do = do_ref[0]
        du, dv, dgate, dlw, dlb, dw, db = vjp([do[:, j * LANES:(j + 1) * LANES] for j in range(BW // LANES)])
        dp_ref[:, 0:BW] = du.astype(BF16)
        dp_ref[:, BW:2 * BW] = dv.astype(BF16)
        dp_ref[:, 2 * BW:3 * BW] = dgate.astype(BF16)
        dp_ref[:, 3 * BW:] = jnp.zeros((SGU_CHUNK, BW), BF16)

        @pl.when(pl.program_id(0) == 0)
        def _():
            dlw_ref[...] = dlw
            dlb_ref[...] = dlb
            dw_ref[...] = dw
            db_ref[...] = db

        @pl.when(pl.program_id(0) > 0)
        def _():
            dlw_ref[...] += dlw
            dlb_ref[...] += dlb
            dw_ref[...] += dw
            db_ref[...] += db

    blk = lambda col: pl.BlockSpec((SGU_CHUNK, BW), lambda c: (c, col // BW))
    vec = pl.BlockSpec((1, BW), lambda c: (0, 0))
    wsp = pl.BlockSpec((SGU_HEADS, SGU_CHUNK, SGU_CHUNK), lambda c: (0, 0, 0))
    bsp = pl.BlockSpec((SGU_CHUNK, BW), lambda c: (0, 0))
    return pl.pallas_call(
        body, name=name, grid=(SEQ // SGU_CHUNK,),
        in_specs=[blk(C_SGU_U), blk(C_SGU_V), blk(C_SGU_G), pl.BlockSpec((1, SGU_CHUNK, BW), lambda c: (1, c, 0)),
                  vec, vec, wsp, bsp, pl.BlockSpec(memory_space=pl.ANY)],
        out_specs=[pl.BlockSpec((SGU_CHUNK, 4 * BW), lambda c: (c, C_SGU_U // (4 * BW))), vec, vec, wsp, bsp],
        input_output_aliases={8: 0},
        out_shape=[jax.ShapeDtypeStruct((SEQ, IN_PAD), BF16), jax.ShapeDtypeStruct((1, BW), F32),
                   jax.ShapeDtypeStruct((1, BW), F32), jax.ShapeDtypeStruct((SGU_HEADS, SGU_CHUNK, SGU_CHUNK), F32),
                   jax.ShapeDtypeStruct((SGU_CHUNK, BW), F32)],
        compiler_params=_cparams(("arbitrary",)),
    )(proj, proj, proj, dout, ln_w, ln_b, w, bias, dproj)


CONV_BLK = 256


def _m2_conv_fwd(proj, w, b, name):
    def body(x_ref, w_ref, b_ref, o_ref):
        x = x_ref[...]
        acc = jnp.zeros_like(x) + b_ref[...]
        for k in range(M2_CONV):
            acc = acc + w_ref[k:k + 1, :] * _shift_down(x, M2_CONV - 1 - k)
        o_ref[...] = _silu(acc)

    return pl.pallas_call(
        body, name=name, grid=(M2_CONV_CH // CONV_BLK,),
        in_specs=[pl.BlockSpec((SEQ, CONV_BLK), lambda j: (0, C_M2X // CONV_BLK + j)),
                  pl.BlockSpec((M2_CONV, CONV_BLK), lambda j: (0, j)), pl.BlockSpec((1, CONV_BLK), lambda j: (0, j))],
        out_specs=pl.BlockSpec((SEQ, CONV_BLK), lambda j: (0, j)),
        out_shape=jax.ShapeDtypeStruct((SEQ, M2_CONV_CH), F32),
        compiler_params=_cparams(("parallel",)),
    )(proj, w, b)


def _m2_conv_bwd(proj, dproj, dxa, w, b, name):
    def body(x_ref, d_ref, w_ref, b_ref, dproj_in, dx_ref, dw_ref, db_ref):
        x = x_ref[...]
        xs = [_shift_down(x, M2_CONV - 1 - k) for k in range(M2_CONV)]
        acc = jnp.zeros_like(x) + b_ref[...]
        for k in range(M2_CONV):
            acc = acc + w_ref[k:k + 1, :] * xs[k]
        sg = jax.nn.sigmoid(acc)
        dacc = d_ref[...] * (sg * (1.0 + acc * (1.0 - sg)))
        dx = jnp.zeros_like(x)
        for k in range(M2_CONV):
            dx = dx + w_ref[k:k + 1, :] * _shift_up(dacc, M2_CONV - 1 - k)
            dw_ref[k:k + 1, :] = jnp.sum(dacc * xs[k], axis=0, keepdims=True)
        dx_ref[...] = dx.astype(BF16)
        db_ref[...] = jnp.sum(dacc, axis=0, keepdims=True)

    return pl.pallas_call(
        body, name=name, grid=(M2_CONV_CH // CONV_BLK,),
        in_specs=[pl.BlockSpec((SEQ, CONV_BLK), lambda j: (0, C_M2X // CONV_BLK + j)),
                  pl.BlockSpec((SEQ, CONV_BLK), lambda j: (0, j)),
                  pl.BlockSpec((M2_CONV, CONV_BLK), lambda j: (0, j)), pl.BlockSpec((1, CONV_BLK), lambda j: (0, j)),
                  pl.BlockSpec(memory_space=pl.ANY)],
        out_specs=[pl.BlockSpec((SEQ, CONV_BLK), lambda j: (0, C_M2X // CONV_BLK + j)),
                   pl.BlockSpec((M2_CONV, CONV_BLK), lambda j: (0, j)), pl.BlockSpec((1, CONV_BLK), lambda j: (0, j))],
        input_output_aliases={4: 0},
        out_shape=[jax.ShapeDtypeStruct((SEQ, IN_PAD), BF16), jax.ShapeDtypeStruct((M2_CONV, M2_CONV_CH), F32),
                   jax.ShapeDtypeStruct((1, M2_CONV_CH), F32)],
        compiler_params=_cparams(("parallel",)),
    )(proj, dxa, w, b, dproj)


N_PAIR = M2_HEADS // 2
HI = lax.Precision.HIGHEST


def _col(a, h):
    lane = lax.broadcasted_iota(jnp.int32, a.shape, 1)
    return jnp.sum(jnp.where(lane == h, a, 0.0), axis=1, keepdims=True)


def _row(a, h):
    sub = lax.broadcasted_iota(jnp.int32, a.shape, 0)
    return jnp.sum(jnp.where(sub == h, a, 0.0), axis=0, keepdims=True)


def _ssd_chunk(xs, bms, cms, dtr, zs, states, dt_bias, a_log, dfs, nws):
    q = M2_CHUNK
    dt = _softplus(dtr + dt_bias)
    da = dt * (-jnp.exp(a_log))
    l_i = lax.broadcasted_iota(jnp.int32, (q, q), 0)
    s_i = lax.broadcasted_iota(jnp.int32, (q, q), 1)
    causal = l_i >= s_i
    tril = jnp.where(causal, 1.0, 0.0)
    a_cs = _dg(tril, da, 1, 0, HI)
    a_cs_t = _dg(da, tril, 0, 1, HI)
    a_end = _row(a_cs, q - 1)
    left = _left_lanes((q, LANES))
    left1 = _left_lanes((1, LANES))
    ys, nexts = [], []
    for j in range(N_PAIR):
        grp = j // 2
        bm, cm = bms[grp], cms[grp]
        h0, h1 = 2 * j, 2 * j + 1
        cb = _bdot(cm, bm, 1, 1)
        xdt = xs[j] * jnp.where(left, _col(dt, h0), _col(dt, h1))
        acs0, acs1 = _col(a_cs, h0), _col(a_cs, h1)
        y = _bdot(cm, states[j], 1, 0) * jnp.where(left, jnp.exp(acs0), jnp.exp(acs1))
        s_new = states[j] * jnp.where(left1, jnp.exp(_col(a_end, h0)), jnp.exp(_col(a_end, h1)))
        for h, acs, xh in ((h0, acs0, jnp.where(left, xdt, 0.0)), (h1, acs1, jnp.where(left, 0.0, xdt))):
            decay = jnp.exp(jnp.where(causal, acs - _row(a_cs_t, h), -jnp.inf))
            y = y + _bdot(cb * decay, xh, 1, 0)
            s_new = s_new + _bdot(bm * jnp.exp(_col(a_end, h) - acs), xh, 0, 0)
        ys.append((y + dfs[j] * xs[j]) * _silu(zs[j]))
        nexts.append(s_new)
    ssq = sum(jnp.sum(y * y, axis=-1, keepdims=True) for y in ys)
    scale = lax.rsqrt(ssq / BW + EPS)
    return [y * scale * nw for y, nw in zip(ys, nws)], nexts


def _blocks(ref, n, width=LANES):
    return [ref[:, j * width:(j + 1) * width] for j in range(n)]


def _ssd_fwd(proj, xa, dt_bias, a_log, dfull, nw, name):
    nc = SEQ // M2_CHUNK

    def body(x_ref, b_ref, c_ref, dt_ref, z_ref, dtb_ref, al_ref, df_ref, nw_ref, o_ref, sin_ref, st):
        @pl.when(pl.program_id(0) == 0)
        def _():
            st[...] = jnp.zeros_like(st)

        states = [st[j] for j in range(N_PAIR)]
        for j in range(N_PAIR):
            sin_ref[0, j] = states[j]
        ys, nexts = _ssd_chunk(_blocks(x_ref, 4), _blocks(b_ref, 2), _blocks(c_ref, 2), dt_ref[...], _blocks(z_ref, 4),
                               states, dtb_ref[...], al_ref[...], _blocks(df_ref, 4), _blocks(nw_ref, 4))
        for j in range(N_PAIR):
            o_ref[:, j * LANES:(j + 1) * LANES] = ys[j].astype(BF16)
            st[j] = nexts[j]

    vec8 = pl.BlockSpec((1, LANES), lambda c: (0, 0))
    vec = pl.BlockSpec((1, BW), lambda c: (0, 0))
    return pl.pallas_call(
        body, name=name, grid=(nc,),
        in_specs=[pl.BlockSpec((M2_CHUNK, BW), lambda c: (c, 0)), pl.BlockSpec((M2_CHUNK, 256), lambda c: (c, 2)),
                  pl.BlockSpec((M2_CHUNK, 256), lambda c: (c, 3)), pl.BlockSpec((M2_CHUNK, LANES), lambda c: (c, C_DT // LANES)),
                  pl.BlockSpec((M2_CHUNK, BW), lambda c: (c, C_M2Z // BW)), vec8, vec8, vec, vec],
        out_specs=[pl.BlockSpec((M2_CHUNK, BW), lambda c: (c, 0)),
                   pl.BlockSpec((1, N_PAIR, M2_STATE, LANES), lambda c: (c, 0, 0, 0))],
        out_shape=[jax.ShapeDtypeStruct((SEQ, BW), BF16), jax.ShapeDtypeStruct((nc, N_PAIR, M2_STATE, LANES), F32)],
        scratch_shapes=[pltpu.VMEM((N_PAIR, M2_STATE, LANES), F32)],
        compiler_params=_cparams(("arbitrary",)),
    )(xa, xa, xa, proj, proj, dt_bias, a_log, dfull, nw)


def _ssd_bwd(proj, dproj, xa, dout, s_in, dt_bias, a_log, dfull, nw, name):
    nc = SEQ // M2_CHUNK

    def body(x_ref, b_ref, c_ref, dt_ref, z_ref, do_ref, sin_ref, dtb_ref, al_ref, df_ref, nw_ref, dproj_in,
             dp_ref, dxa_ref, ddtb_ref, dal_ref, ddf_ref, dnw_ref, dst):
        @pl.when(pl.program_id(0) == 0)
        def _():
            dst[...] = jnp.zeros_like(dst)
            for r in (ddtb_ref, dal_ref, ddf_ref, dnw_ref):
                r[...] = jnp.zeros_like(r)

        states = [sin_ref[0, j] for j in range(N_PAIR)]
        _, vjp = jax.vjp(_ssd_chunk, _blocks(x_ref, 4), _blocks(b_ref, 2), _blocks(c_ref, 2), dt_ref[...],
                         _blocks(z_ref, 4), states, dtb_ref[...], al_ref[...], _blocks(df_ref, 4), _blocks(nw_ref, 4))
        dxs, dbs, dcs, ddt, dzs, dstates, ddtb, dal, ddfs, dnws = vjp(
            ([do_ref[0, :, j * LANES:(j + 1) * LANES] for j in range(N_PAIR)], [dst[j] for j in range(N_PAIR)]))
        for j in range(N_PAIR):
            sl = slice(j * LANES, (j + 1) * LANES)
            dxa_ref[:, sl] = dxs[j]
            dp_ref[:, sl] = dzs[j].astype(BF16)
            dst[j] = dstates[j]
            ddf_ref[:, sl] += ddfs[j]
            dnw_ref[:, sl] += dnws[j]
        for g in range(2):
            dxa_ref[:, BW + g * LANES:BW + (g + 1) * LANES] = dbs[g]
            dxa_ref[:, BW + 256 + g * LANES:BW + 256 + (g + 1) * LANES] = dcs[g]
        dp_ref[:, BW:BW + LANES] = ddt.astype(BF16)
        dp_ref[:, BW + LANES:] = jnp.zeros((M2_CHUNK, 2 * BW - BW - LANES), BF16)
        ddtb_ref[...] += ddtb
        dal_ref[...] += dal

    rev = lambda w, col=0: pl.BlockSpec((M2_CHUNK, w), lambda i: (nc - 1 - i, col))
    vec8 = pl.BlockSpec((1, LANES), lambda i: (0, 0))
    vec = pl.BlockSpec((1, BW), lambda i: (0, 0))
    return pl.pallas_call(
        body, name=name, grid=(nc,),
        in_specs=[rev(BW), rev(256, 2), rev(256, 3), rev(LANES, C_DT // LANES), rev(BW, C_M2Z // BW),
                  pl.BlockSpec((1, M2_CHUNK, BW), lambda i: (2, nc - 1 - i, 0)),
                  pl.BlockSpec((1, N_PAIR, M2_STATE, LANES), lambda i: (nc - 1 - i, 0, 0, 0)), vec8, vec8, vec, vec,
                  pl.BlockSpec(memory_space=pl.ANY)],
        out_specs=[rev(2 * BW, C_M2Z // (2 * BW)), rev(M2_CONV_CH), vec8, vec8, vec, vec],
        input_output_aliases={11: 0},
        out_shape=[jax.ShapeDtypeStruct((SEQ, IN_PAD), BF16), jax.ShapeDtypeStruct((SEQ, M2_CONV_CH), F32),
                   jax.ShapeDtypeStruct((1, LANES), F32), jax.ShapeDtypeStruct((1, LANES), F32),
                   jax.ShapeDtypeStruct((1, BW), F32), jax.ShapeDtypeStruct((1, BW), F32)],
        scratch_shapes=[pltpu.VMEM((N_PAIR, M2_STATE, LANES), F32)],
        compiler_params=_cparams(("arbitrary",)),
    )(xa, xa, xa, proj, proj, dout, s_in, dt_bias, a_log, dfull, nw, dproj)


def _sc_specs():
    col = lambda kind: pl.BlockSpec((SEQ, LANES), lambda j: (0, C_SC // LANES + 4 * j + kind))
    return [col(0), col(1), col(2), col(3)]


def _sc_fwd(proj, w, name):
    def body(b_ref, c_ref, h_ref, g_ref, w_ref, o_ref):
        ch = c_ref[...] * h_ref[...]
        acc = jnp.zeros_like(ch)
        for k in range(SC_CONV):
            acc = acc + w_ref[k:k + 1, :] * _shift_down(ch, SC_CONV - 1 - k)
        o_ref[...] = (b_ref[...] * acc * _silu(g_ref[...])).astype(BF16)

    return pl.pallas_call(
        body, name=name, grid=(BW // LANES,),
        in_specs=_sc_specs() + [pl.BlockSpec((SC_CONV, LANES), lambda j: (0, j))],
        out_specs=pl.BlockSpec((SEQ, LANES), lambda j: (0, j)),
        out_shape=jax.ShapeDtypeStruct((SEQ, BW), BF16),
        compiler_params=_cparams(("parallel",)),
    )(proj, proj, proj, proj, w)


def _sc_bwd(proj, dproj, dout, w, name):
    def body(b_ref, c_ref, h_ref, g_ref, do_ref, w_ref, dproj_in, dp_ref, dw_ref):
        cv, hv, gv = c_ref[...], h_ref[...], g_ref[...]
        ch = cv * hv
        chs = [_shift_down(ch, SC_CONV - 1 - k) for k in range(SC_CONV)]
        acc = jnp.zeros_like(ch)
        for k in range(SC_CONV):
            acc = acc + w_ref[k:k + 1, :] * chs[k]
        sg = jax.nn.sigmoid(gv)
        do = do_ref[0]
        bv = b_ref[...]
        dp_ref[:, 0:LANES] = (do * acc * (gv * sg)).astype(BF16)
        dp_ref[:, 3 * LANES:] = (do * bv * acc * (sg * (1.0 + gv * (1.0 - sg)))).astype(BF16)
        dacc = do * bv * (gv * sg)
        dch = jnp.zeros_like(ch)
        for k in range(SC_CONV):
            dch = dch + w_ref[k:k + 1, :] * _shift_up(dacc, SC_CONV - 1 - k)
            dw_ref[k:k + 1, :] = jnp.sum(dacc * chs[k], axis=0, keepdims=True)
        dp_ref[:, LANES:2 * LANES] = (dch * hv).astype(BF16)
        dp_ref[:, 2 * LANES:3 * LANES] = (dch * cv).astype(BF16)

    wsp = pl.BlockSpec((SC_CONV, LANES), lambda j: (0, j))
    return pl.pallas_call(
        body, name=name, grid=(BW // LANES,),
        in_specs=_sc_specs() + [pl.BlockSpec((1, SEQ, LANES), lambda j: (3, 0, j)), wsp, pl.BlockSpec(memory_space=pl.ANY)],
        out_specs=[pl.BlockSpec((SEQ, 4 * LANES), lambda j: (0, C_SC // (4 * LANES) + j)), wsp],
        input_output_aliases={6: 0},
        out_shape=[jax.ShapeDtypeStruct((SEQ, IN_PAD), BF16), jax.ShapeDtypeStruct((SC_CONV, BW), F32)],
        compiler_params=_cparams(("parallel",)),
    )(proj, proj, proj, proj, dout, w, dproj)


MERGE_T = 256
MERGE_BWD_T = 1024


def _merge_fwd(proj, ys, merge_b, w_branch, name):
    def body(y_ref, lg_ref, b_ref, w_ref, o_ref):
        acc = jnp.zeros((MERGE_T, D_MODEL), F32)
        for k in range(N_BRANCH):
            gate = jax.nn.sigmoid(lg_ref[:, k * D_MODEL:(k + 1) * D_MODEL] + b_ref[k])
            acc = acc + gate * _dg(y_ref[k], w_ref[k], 1, 0)
        o_ref[...] = acc.astype(BF16)

    return pl.pallas_call(
        body, name=name, grid=(SEQ // MERGE_T,),
        in_specs=[pl.BlockSpec((N_BRANCH, MERGE_T, BW), lambda i: (0, i, 0)),
                  pl.BlockSpec((MERGE_T, N_BRANCH * D_MODEL), lambda i: (i, C_MERGE // (N_BRANCH * D_MODEL))),
                  pl.BlockSpec((N_BRANCH, 1, D_MODEL), lambda i: (0, 0, 0)),
                  pl.BlockSpec((N_BRANCH, BW, D_MODEL), lambda i: (0, 0, 0))],
        out_specs=pl.BlockSpec((MERGE_T, D_MODEL), lambda i: (i, 0)),
        out_shape=jax.ShapeDtypeStruct((SEQ, D_MODEL), BF16),
        compiler_params=_cparams(("parallel",)),
    )(ys, proj, merge_b, w_branch)


def _merge_bwd(proj, ys, dm, merge_b, w_branch, name):
    nt = SEQ // MERGE_BWD_T

    def body(y_ref, lg_ref, dm_ref, b_ref, w_ref, dy_ref, dlg_ref, dw_ref, db_ref, dw_acc):
        i = pl.program_id(1)
        gate = jax.nn.sigmoid(lg_ref[...] + b_ref[0])
        y = y_ref[0]
        dmv = dm_ref[...]
        dbo = (gate * dmv).astype(BF16)
        dlg = _dg(y, w_ref[0], 1, 0) * dmv * gate * (1.0 - gate)
        dlg_ref[...] = dlg.astype(BF16)
        dy_ref[0] = _dg(dbo, w_ref[0], 1, 1)
        dwp = _dg(y, dbo, 0, 0)
        dbp = jnp.sum(dlg, axis=0, keepdims=True)

        @pl.when(i == 0)
        def _():
            dw_acc[...] = dwp
            db_ref[0] = dbp

        @pl.when(i > 0)
        def _():
            dw_acc[...] += dwp
            db_ref[0] += dbp

        @pl.when(i == nt - 1)
        def _():
            dw_ref[0] = dw_acc[...].astype(BF16)

    return pl.pallas_call(
        body, name=name, grid=(N_BRANCH, nt),
        in_specs=[pl.BlockSpec((1, MERGE_BWD_T, BW), lambda k, i: (k, i, 0)),
                  pl.BlockSpec((MERGE_BWD_T, D_MODEL), lambda k, i: (i, C_MERGE // D_MODEL + k)),
                  pl.BlockSpec((MERGE_BWD_T, D_MODEL), lambda k, i: (i, 0)),
                  pl.BlockSpec((1, 1, D_MODEL), lambda k, i: (k, 0, 0)),
                  pl.BlockSpec((1, BW, D_MODEL), lambda k, i: (k, 0, 0))],
        out_specs=[pl.BlockSpec((1, MERGE_BWD_T, BW), lambda k, i: (k, i, 0)),
                   pl.BlockSpec((MERGE_BWD_T, D_MODEL), lambda k, i: (i, k)),
                   pl.BlockSpec((1, BW, D_MODEL), lambda k, i: (k, 0, 0)),
                   pl.BlockSpec((1, 1, D_MODEL), lambda k, i: (k, 0, 0))],
        out_shape=[jax.ShapeDtypeStruct((N_BRANCH, SEQ, BW), F32), jax.ShapeDtypeStruct((SEQ, IN_PAD), BF16),
                   jax.ShapeDtypeStruct((N_BRANCH, BW, D_MODEL), BF16), jax.ShapeDtypeStruct((N_BRANCH, 1, D_MODEL), F32)],
        scratch_shapes=[pltpu.VMEM((BW, D_MODEL), F32)],
        compiler_params=_cparams(("parallel", "arbitrary")),
    )(ys, proj, dm, merge_b, w_branch)


def _adamw(glist, w, m, v, rows, name):
    nl = len(glist)
    n, r, c = glist[0].shape
    assert w.shape == (nl, r, c) and r % rows == 0
    nb = r // rows

    def body(*refs):
        g_refs = refs[:nl]
        w_ref, m_ref, v_ref, go_ref, d_ref, mo_ref, vo_ref = refs[nl:]
        for layer in range(nl):
            @pl.when(pl.program_id(0) == layer)
            def _(g_ref=g_refs[layer]):
                g = g_ref[0].astype(F32)
                for s in range(1, n):
                    g = g + g_ref[s].astype(F32)
                mn = ADAM_B1 * m_ref[0] + (1.0 - ADAM_B1) * g
                vn = ADAM_B2 * v_ref[0] + (1.0 - ADAM_B2) * jnp.square(g)
                m_hat = mn / (1.0 - ADAM_B1 ** ADAM_STEP)
                v_hat = vn / (1.0 - ADAM_B2 ** ADAM_STEP)
                go_ref[0] = g
                d_ref[0] = -ADAM_LR * (m_hat / (jnp.sqrt(v_hat) + ADAM_EPS) + ADAM_WD * w_ref[0])
                mo_ref[0] = mn
                vo_ref[0] = vn

    def g_spec(layer):
        return pl.BlockSpec((n, rows, c), lambda a, i: (0, jnp.where(a < layer, 0, jnp.where(a == layer, i, nb - 1)), 0))

    blk = pl.BlockSpec((1, rows, c), lambda a, i: (a, i, 0))
    out = jax.ShapeDtypeStruct((nl, r, c), F32)
    return pl.pallas_call(
        body, name=name, grid=(nl, nb),
        in_specs=[g_spec(layer) for layer in range(nl)] + [blk, blk, blk],
        out_specs=[blk, blk, blk, blk], out_shape=[out, out, out, out],
        compiler_params=_cparams(("arbitrary", "arbitrary")),
    )(*glist, w, m, v)


X_ROWS_PER_COL = 2 * (D_MODEL // LANES)


def _w_in_to_x(w):
    t = jnp.transpose(w, (2, 0, 1)).reshape(SHARD_IN, DEPTH, D_MODEL // LANES, LANES)
    return jnp.transpose(t, (0, 2, 1, 3)).reshape(SHARD_IN * X_ROWS_PER_COL, LANES)


def _w_in_from_x(xv):
    t = jnp.transpose(xv.reshape(SHARD_IN, D_MODEL // LANES, DEPTH, LANES), (0, 2, 1, 3))
    return jnp.transpose(t.reshape(SHARD_IN, DEPTH, D_MODEL), (1, 2, 0))


def _adamw_w_in(glist, w, m, v, name, after=None):
    n = glist[0].shape[0]
    cols = LANES
    rows = cols * X_ROWS_PER_COL
    extra = [] if after is None else [after]

    def body(g0_ref, g1_ref, w_ref, m_ref, v_ref, *rest):
        go_ref, d_ref, mo_ref, vo_ref = rest[len(extra):]
        for layer, g_ref in enumerate((g0_ref, g1_ref)):
            g = g_ref[0].astype(F32)
            for s in range(1, n):
                g = g + g_ref[s].astype(F32)
            gt = g.T
            for t in range(D_MODEL // LANES):
                sel = (pl.ds(2 * t + layer, cols, stride=X_ROWS_PER_COL), slice(None))
                gs = gt[:, t * LANES:(t + 1) * LANES]
                mn = ADAM_B1 * m_ref[sel] + (1.0 - ADAM_B1) * gs
                vn = ADAM_B2 * v_ref[sel] + (1.0 - ADAM_B2) * jnp.square(gs)
                m_hat = mn / (1.0 - ADAM_B1 ** ADAM_STEP)
                v_hat = vn / (1.0 - ADAM_B2 ** ADAM_STEP)
                go_ref[sel] = gs
                d_ref[sel] = -ADAM_LR * (m_hat / (jnp.sqrt(v_hat) + ADAM_EPS) + ADAM_WD * w_ref[sel])
                mo_ref[sel] = mn
                vo_ref[sel] = vn

    g_spec = pl.BlockSpec((n, D_MODEL, cols), lambda i: (0, 0, i))
    blk = pl.BlockSpec((rows, LANES), lambda i: (i, 0))
    out = jax.ShapeDtypeStruct((SHARD_IN * X_ROWS_PER_COL, LANES), F32)
    res = pl.pallas_call(
        body, name=name, grid=(-(-SHARD_IN // cols),),
        in_specs=[g_spec, g_spec, blk, blk, blk] + [pl.BlockSpec(memory_space=pl.ANY)] * len(extra),
        out_specs=[blk, blk, blk, blk], out_shape=[out, out, out, out],
        compiler_params=_cparams(("parallel",)),
    )(*glist, _w_in_to_x(w), _w_in_to_x(m), _w_in_to_x(v), *extra)
    return [_w_in_from_x(o) for o in res]


def _adamw_many(gs, ws, ms, vs, name):
    k = len(gs)

    def body(*refs):
        g_refs, w_refs, m_refs, v_refs = refs[:k], refs[k:2 * k], refs[2 * k:3 * k], refs[3 * k:4 * k]
        d_refs, mo_refs, vo_refs = refs[4 * k:5 * k], refs[5 * k:6 * k], refs[6 * k:7 * k]
        for i in range(k):
            g = g_refs[i][...]
            mn = ADAM_B1 * m_refs[i][...] + (1.0 - ADAM_B1) * g
            vn = ADAM_B2 * v_refs[i][...] + (1.0 - ADAM_B2) * jnp.square(g)
            m_hat = mn / (1.0 - ADAM_B1 ** ADAM_STEP)
            v_hat = vn / (1.0 - ADAM_B2 ** ADAM_STEP)
            d_refs[i][...] = -ADAM_LR * (m_hat / (jnp.sqrt(v_hat) + ADAM_EPS) + ADAM_WD * w_refs[i][...])
            mo_refs[i][...] = mn
            vo_refs[i][...] = vn

    whole = pl.BlockSpec(memory_space=pltpu.VMEM)
    shapes = [jax.ShapeDtypeStruct(w.shape, F32) for w in ws]
    outs = pl.pallas_call(
        body, name=name, in_specs=[whole] * (4 * k), out_specs=[whole] * (3 * k), out_shape=shapes * 3,
        compiler_params=_cparams(None),
    )(*gs, *ws, *ms, *vs)
    return outs[:k], outs[k:2 * k], outs[2 * k:]


MEMORY_ORDER = {'s5_b_re': (0, 1, 3, 2), 's5_b_im': (0, 1, 3, 2), 's5_d': (0, 2, 1), 'sc_conv_w': (1, 0, 2)}


def _memory_view(name, t):
    return jnp.transpose(t, MEMORY_ORDER[name]) if name in MEMORY_ORDER else t


def _slot_sum(gslots, name):
    n, r, c = gslots.shape

    def body(g_ref, o_ref):
        g = g_ref[0]
        for s in range(1, n):
            g = g + g_ref[s]
        o_ref[...] = g

    return pl.pallas_call(
        body, name=name, in_specs=[pl.BlockSpec((n, r, c), lambda: (0, 0, 0))],
        out_specs=pl.BlockSpec((r, c), lambda: (0, 0)), out_shape=jax.ShapeDtypeStruct((r, c), F32),
        compiler_params=_cparams(None),
    )(gslots)


def _me_and_peers():
    x, y, c = lax.axis_index("x"), lax.axis_index("y"), lax.axis_index("c")
    me = 4 * x + 2 * y + c
    peers = []
    for k in range(1, N_DEV):
        px = 1 - x if (k >> 2) & 1 else x
        py = 1 - y if (k >> 1) & 1 else y
        pc = 1 - c if k & 1 else c
        peers.append((4 * px + 2 * py + pc, (px, py, pc)))
    return me, peers


_HBM = pl.BlockSpec(memory_space=pltpu.HBM)
_SEM = pl.BlockSpec(memory_space=pltpu.SEMAPHORE)
_EFFECT = pltpu.SideEffectType.DATAFLOW_SIDE_EFFECTING


N_CHIP = N_DEV // 2


def _chip_peers():
    x, y, c = lax.axis_index("x"), lax.axis_index("y"), lax.axis_index("c")
    chips = []
    for d in range(1, N_CHIP):
        px = 1 - x if (d >> 1) & 1 else x
        py = 1 - y if d & 1 else y
        chips.append((2 * px + py, (px, py)))
    return (x, y, c), 2 * x + y, chips


def _plan_direct(ins, lands, send_sems, recv_sems, local_sems, gather):
    me, peers = _me_and_peers()
    plan = dict(start=[], local=[], sends=[], recvs=[])
    for t in range(len(ins)):
        own = pltpu.make_async_copy(ins[t] if gather else ins[t].at[me], lands[t].at[me], local_sems.at[t])
        plan['start'].append(own)
        plan['local'].append(own)
        for k, (pidx, pos) in enumerate(peers):
            cp = pltpu.make_async_remote_copy(
                src_ref=ins[t] if gather else ins[t].at[pidx], dst_ref=lands[t].at[me],
                send_sem=send_sems.at[t * (N_DEV - 1) + k], recv_sem=recv_sems.at[t * (N_DEV - 1) + k],
                device_id=pos, device_id_type=MESH)
            plan['start'].append(cp)
            plan['sends'].append(cp)
            plan['recvs'].append(cp)
    return plan


def _plan_gather(ins, lands, send_sems, recv_sems, local_sems, first=0):
    (x, y, c), q, chips = _chip_peers()
    me = 2 * q + c
    plan = dict(start=[], relay_wait=[], relay_start=[], local=[], sends=[], recvs=[])
    for t in range(len(ins)):
        base = (first + t) * 7
        sem = lambda k: dict(send_sem=send_sems.at[base + k], recv_sem=recv_sems.at[base + k], device_id_type=MESH)
        own = pltpu.make_async_copy(ins[t], lands[t].at[me], local_sems.at[first + t])
        to_sib = pltpu.make_async_remote_copy(src_ref=ins[t], dst_ref=lands[t].at[me], device_id=(x, y, 1 - c), **sem(0))
        plan['start'] += [own, to_sib]
        plan['local'].append(own)
        plan['sends'].append(to_sib)
        plan['recvs'].append(to_sib)
        for d, (pq, (px, py)) in enumerate(chips):
            to_chip = pltpu.make_async_remote_copy(src_ref=ins[t], dst_ref=lands[t].at[me], device_id=(px, py, c), **sem(1 + d))
            blk = lands[t].at[2 * pq + c]
            fwd = pltpu.make_async_remote_copy(src_ref=blk, dst_ref=blk, device_id=(x, y, 1 - c), **sem(4 + d))
            plan['start'].append(to_chip)
            plan['relay_wait'].append(to_chip)
            plan['relay_start'].append(fwd)
            plan['sends'] += [to_chip, fwd]
            plan['recvs'].append(fwd)
    return plan


def _plan_pair(ins, lands, send_sems, recv_sems, local_sems):
    (x, y, c), q, chips = _chip_peers()
    plan = dict(start=[], local=[], sends=[], recvs=[])
    for t in range(len(ins)):
        for k in range(N_CHIP):
            cp = pltpu.make_async_remote_copy(
                src_ref=ins[t].at[2 * k + 1 - c], dst_ref=lands[t].at[k], send_sem=send_sems.at[t * N_CHIP + k],
                recv_sem=recv_sems.at[t * N_CHIP + k], device_id=(x, y, 1 - c), device_id_type=MESH)
            plan['start'].append(cp)
            plan['sends'].append(cp)
            plan['recvs'].append(cp)
    return plan


def _plan_chips(ins, lands, send_sems, recv_sems, local_sems):
    (x, y, c), q, chips = _chip_peers()
    plan = dict(start=[], local=[], sends=[], recvs=[])
    for t in range(len(ins)):
        own = pltpu.make_async_copy(ins[t].at[q], lands[t].at[q], local_sems.at[t])
        plan['start'].append(own)
        plan['local'].append(own)
        for d, (pq, (px, py)) in enumerate(chips):
            cp = pltpu.make_async_remote_copy(
                src_ref=ins[t].at[pq], dst_ref=lands[t].at[q], send_sem=send_sems.at[t * 3 + d],
                recv_sem=recv_sems.at[t * 3 + d], device_id=(px, py, c), device_id_type=MESH)
            plan['start'].append(cp)
            plan['sends'].append(cp)
            plan['recvs'].append(cp)
    return plan


def _split_start(plan_fn, tensors, land_shapes, n_sems, name, after=None):
    n = len(tensors)
    extra = [] if after is None else [after]

    def body(*refs):
        ins, lands = refs[:n], refs[n:2 * n]
        plan = plan_fn(ins, lands, *refs[2 * n + len(extra):2 * n + len(extra) + 3])
        for cp in plan['start']:
            cp.start()
        refs[-1][...] = jnp.zeros_like(refs[-1])

    outs = pl.pallas_call(
        body, name=name,
        out_shape=(pltpu.SemaphoreType.DMA((n_sems,)), pltpu.SemaphoreType.DMA((n_sems,)), pltpu.SemaphoreType.DMA((n,)),
                   *[pltpu.HBM(t.shape, t.dtype) for t in tensors],
                   *[pltpu.HBM(s, t.dtype) for s, t in zip(land_shapes, tensors)],
                   jax.ShapeDtypeStruct((8, LANES), F32)),
        in_specs=[_HBM] * (2 * n) + [pl.BlockSpec(memory_space=pl.ANY)] * len(extra),
        out_specs=(_SEM, _SEM, _SEM, *[_HBM] * (2 * n), pl.BlockSpec(memory_space=pltpu.VMEM)),
        input_output_aliases={t: 3 + t for t in range(2 * n)},
        compiler_params=pltpu.CompilerParams(has_side_effects=_EFFECT),
    )(*[pltpu.with_memory_space_constraint(t, pltpu.HBM) for t in tensors],
      *[pltpu.with_memory_space_constraint(lax.empty(s, t.dtype), pltpu.HBM) for s, t in zip(land_shapes, tensors)], *extra)
    return outs[:-1], outs[-1]


def _split_relay(plan_fn, state, after, name):
    sems, thru = state[:3], state[3:]
    n = len(thru) // 2

    def arrived(*refs):
        plan = plan_fn(refs[:n], refs[n:2 * n], *refs[2 * n:2 * n + 3])
        for cp in plan['relay_wait']:
            cp.wait_recv()

    thru = pl.pallas_call(
        arrived, name=name + "_arrived",
        out_shape=tuple(pltpu.HBM(t.shape, t.dtype) for t in thru),
        in_specs=[_HBM] * (2 * n) + [_SEM, _SEM, _SEM, pl.BlockSpec(memory_space=pl.ANY)],
        out_specs=tuple([_HBM] * (2 * n)),
        input_output_aliases={t: t for t in range(2 * n)},
        compiler_params=pltpu.CompilerParams(has_side_effects=_EFFECT),
    )(*thru, *sems, after)

    def forward(*refs):
        plan = plan_fn(refs[:n], refs[n:2 * n], *refs[2 * n:2 * n + 3])
        for cp in plan['relay_start']:
            cp.start()
        refs[-1][...] = jnp.zeros_like(refs[-1])

    outs = pl.pallas_call(
        forward, name=name + "_forward",
        out_shape=(*[pltpu.HBM(t.shape, t.dtype) for t in thru], jax.ShapeDtypeStruct((8, LANES), F32)),
        in_specs=[_HBM] * (2 * n) + [_SEM, _SEM, _SEM],
        out_specs=(*[_HBM] * (2 * n), pl.BlockSpec(memory_space=pltpu.VMEM)),
        input_output_aliases={t: t for t in range(2 * n)},
        compiler_params=pltpu.CompilerParams(has_side_effects=_EFFECT),
    )(*thru, *sems)
    return (*sems, *outs[:-1]), outs[-1]


def _split_wait(plan_fn, state, after, name, with_sources=False):
    sems, thru = state[:3], state[3:]
    n = len(thru) // 2

    def body(*refs):
        plan = plan_fn(refs[:n], refs[n:2 * n], *refs[2 * n:2 * n + 3])
        for cp in plan['local']:
            cp.wait()
        for cp in plan['sends']:
            cp.wait_send()
        for cp in plan['recvs']:
            cp.wait_recv()

    outs = pl.pallas_call(
        body, name=name,
        out_shape=tuple(pltpu.HBM(t.shape, t.dtype) for t in thru),
        in_specs=[_HBM] * (2 * n) + [_SEM, _SEM, _SEM, pl.BlockSpec(memory_space=pl.ANY)],
        out_specs=tuple([_HBM] * (2 * n)),
        input_output_aliases={t: t for t in range(2 * n)},
        compiler_params=pltpu.CompilerParams(has_side_effects=_EFFECT),
    )(*thru, *sems, after)
    return (list(outs[:n]), list(outs[n:])) if with_sources else list(outs[n:])


PAIR_SUM_BLOCK = 768 * 1024


def _pair_sum(mine, theirs, name):
    _, r, c = mine.shape
    rows = r
    while rows * c > PAIR_SUM_BLOCK and rows % 32 == 0:
        rows //= 2

    def body(core_ref, a_ref, b_ref, o_ref):
        o_ref[0] = (a_ref[0].astype(F32) + b_ref[0].astype(F32)).astype(o_ref.dtype)

    return pl.pallas_call(
        body, name=name,
        grid_spec=pltpu.PrefetchScalarGridSpec(
            num_scalar_prefetch=1, grid=(N_CHIP, r // rows),
            in_specs=[pl.BlockSpec((1, rows, c), lambda k, i, core: (2 * k + core[0], i, 0)),
                      pl.BlockSpec((1, rows, c), lambda k, i, core: (k, i, 0))],
            out_specs=pl.BlockSpec((1, rows, c), lambda k, i, core: (k, i, 0))),
        out_shape=jax.ShapeDtypeStruct((N_CHIP, r, c), mine.dtype),
        compiler_params=_cparams(("parallel", "parallel")),
    )(lax.axis_index("c").astype(jnp.int32).reshape(1), mine, theirs)


WEIGHTS = ['norm_w', 'w_in', 's5_lambda_re', 's5_lambda_im', 's5_b_re', 's5_b_im', 's5_c_re', 's5_c_im', 's5_d',
           's5_log_step', 's5_w_glu', 'sgu_ln_w', 'sgu_ln_b', 'sgu_w', 'sgu_b', 'm2_conv_w', 'm2_conv_b', 'm2_dt_bias',
           'm2_a_log', 'm2_d', 'm2_norm_w', 'sc_conv_w', 'merge_b', 'w_branch', 'w_out', 'final_norm_w']
BIG_SHARDED = ['w_in', 'w_branch', 'w_out', 's5_w_glu']
SMALL_SHARDED = ['m2_conv_w', 'sc_conv_w', 'merge_b']
REPLICATED = [n for n in WEIGHTS if n not in BIG_SHARDED + SMALL_SHARDED]
S5_NAMES = ['s5_lambda_re', 's5_lambda_im', 's5_b_re', 's5_b_im', 's5_c_re', 's5_c_im', 's5_d', 's5_log_step']


def _sc_interleave(t):
    lead = t.shape[:-1]
    return jnp.swapaxes(t.reshape(lead + (4, 4, LANES)), -3, -2).reshape(lead + (4 * BW,))


def _pad_in(w):
    z = lambda n: jnp.zeros(w.shape[:-1] + (n,), w.dtype)
    return jnp.concatenate([w[..., 6152:], w[..., 0:1024], w[..., 3072:4096], w[..., 1024:2560], z(512),
                            w[..., 2560:3072], w[..., 4096:4104], z(504), _sc_interleave(w[..., 4104:6152])], axis=-1)


def _unpad_in(g):
    return jnp.concatenate([g[..., C_S5U:C_S5U + 1024], g[..., C_SGU_U:C_SGU_U + 1536], g[..., C_M2Z:C_M2Z + 512],
                            g[..., C_M2X:C_M2X + 1024], g[..., C_DT:C_DT + 8], _sc_interleave(g[..., C_SC:]),
                            g[..., :N_BRANCH * D_MODEL]], axis=-1)


ROW_BLOCK = 8 * LANES


def _pack_rows(tensors, row_mult, batched=False):
    parts = []
    for t in tensors:
        f = t.reshape((t.shape[0], -1) if batched else (1, -1))
        f = jnp.pad(f, ((0, 0), (0, (-f.shape[1]) % ROW_BLOCK)))
        parts.append(f.reshape(f.shape[0], -1, LANES))
    out = jnp.concatenate(parts, axis=1)
    out = jnp.pad(out, ((0, 0), (0, (-out.shape[1]) % row_mult), (0, 0)))
    return out if batched else out[0]


def _unpack_rows(rows, shapes):
    out, r0 = [], 0
    for shp in shapes:
        size = 1
        for s in shp:
            size *= s
        nr = -(-size // ROW_BLOCK) * 8
        out.append(rows[r0:r0 + nr].reshape(-1)[:size].reshape(shp))
        r0 += nr
    return out


def _kernel_col_map():
    m = np.full(IN_PAD, -1, np.int64)
    m[C_MERGE:C_MERGE + 4096] = np.arange(6152, 10248)
    m[C_S5U:C_S5U + 1024] = np.arange(0, 1024)
    m[C_M2X:C_M2X + 1024] = np.arange(3072, 4096)
    m[C_SGU_U:C_SGU_U + 1536] = np.arange(1024, 2560)
    m[C_M2Z:C_M2Z + 512] = np.arange(2560, 3072)
    m[C_DT:C_DT + 8] = np.arange(4096, 4104)
    for j in range(4):
        for kind in range(4):
            k0 = C_SC + 4 * LANES * j + LANES * kind
            m[k0:k0 + LANES] = 4104 + BW * kind + LANES * j + np.arange(LANES)
    return m


def _lane_pieces(sources):
    pieces, cur = [], None
    for lane, src in enumerate(sources):
        key = None if src is None else (src[0], src[1] // LANES, (lane - src[1]) % LANES)
        if cur is not None and key == cur[0]:
            cur[2] = lane + 1
        else:
            if cur is not None and cur[0] is not None:
                pieces.append((*cur[0], cur[1], cur[2]))
            cur = [key, lane, lane + 1]
    if cur is not None and cur[0] is not None:
        pieces.append((*cur[0], cur[1], cur[2]))
    return pieces


def _assemble_block(pieces, load, rows, dtype):
    lane = lax.broadcasted_iota(jnp.int32, (rows, LANES), 1)
    out = None
    for arr, sb, shift, lo, hi in pieces:
        v = load(arr, sb)
        if shift:
            v = pltpu.roll(v, shift, 1)
        if out is None and lo == 0 and hi == LANES:
            out = v
        else:
            out = jnp.where((lane >= lo) & (lane < hi), v, jnp.zeros((rows, LANES), dtype) if out is None else out)
    return jnp.zeros((rows, LANES), dtype) if out is None else out


RELAYOUT_ROWS = 512
SHARD_BLOCKS = -(-SHARD_IN // LANES)


def _load_shard_block(ref, rows):
    def load(j, sb):
        if sb == SHARD_BLOCKS - 1:
            return jnp.broadcast_to(ref[j, :, SHARD_IN - 1:SHARD_IN], (rows, LANES))
        return ref[j, :, sb * LANES:(sb + 1) * LANES]
    return load


def _relayout_w_in(gathered, name):
    kmap = _kernel_col_map()
    dtype = gathered.dtype

    def body(src_ref, o_ref):
        load = _load_shard_block(src_ref, RELAYOUT_ROWS)
        for ob in range(IN_PAD // LANES):
            srcs = [None if kmap[ob * LANES + l] < 0 else (int(kmap[ob * LANES + l]) // SHARD_IN, int(kmap[ob * LANES + l]) % SHARD_IN)
                    for l in range(LANES)]
            o_ref[:, ob * LANES:(ob + 1) * LANES] = _assemble_block(_lane_pieces(srcs), load, RELAYOUT_ROWS, dtype)

    return pl.pallas_call(
        body, name=name, grid=(D_MODEL // RELAYOUT_ROWS,),
        in_specs=[pl.BlockSpec((N_DEV, RELAYOUT_ROWS, SHARD_IN), lambda i: (0, i, 0))],
        out_specs=pl.BlockSpec((RELAYOUT_ROWS, IN_PAD), lambda i: (i, 0)),
        out_shape=jax.ShapeDtypeStruct((D_MODEL, IN_PAD), dtype),
        compiler_params=_cparams(("parallel",)),
    )(gathered)


def _relayout_g_in(gw, name):
    kmap = _kernel_col_map()
    kinv = np.zeros(IN_DIM, np.int64)
    kinv[kmap[kmap >= 0]] = np.nonzero(kmap >= 0)[0]
    dtype = gw.dtype

    def body(src_ref, o_ref):
        load = lambda _, sb: src_ref[:, sb * LANES:(sb + 1) * LANES]
        for j in range(N_DEV):
            for ob in range(SHARD_BLOCKS):
                srcs = [(0, int(kinv[SHARD_IN * j + ob * LANES + l])) if ob * LANES + l < SHARD_IN else None for l in range(LANES)]
                blk = _assemble_block(_lane_pieces(srcs), load, RELAYOUT_ROWS, dtype)
                if ob == SHARD_BLOCKS - 1:
                    o_ref[j, :, SHARD_IN - 1:SHARD_IN] = blk[:, 0:1]
                else:
                    o_ref[j, :, ob * LANES:(ob + 1) * LANES] = blk

    return pl.pallas_call(
        body, name=name, grid=(D_MODEL // RELAYOUT_ROWS,),
        in_specs=[pl.BlockSpec((RELAYOUT_ROWS, IN_PAD), lambda i: (i, 0))],
        out_specs=pl.BlockSpec((N_DEV, RELAYOUT_ROWS, SHARD_IN), lambda i: (0, i, 0)),
        out_shape=jax.ShapeDtypeStruct((N_DEV, D_MODEL, SHARD_IN), dtype),
        compiler_params=_cparams(("parallel",)),
    )(gw)


def _rows128(flat, row_mult=8):
    n = flat.shape[0]
    per = LANES * row_mult
    total = -(-n // per) * per
    return jnp.pad(flat, (0, total - n)).reshape(total // LANES, LANES)


def _pad_lanes(v):
    return jnp.pad(v, (0, LANES - v.shape[0])).reshape(1, LANES)


def _layer_prep(i, p):
    disc, disc_vjp = jax.vjp(_s5_disc, *[p[n][i] for n in S5_NAMES])
    prep = dict(
        nw=p['norm_w'][i].reshape(1, D_MODEL), disc_vjp=disc_vjp,
        s5small=[_block_diag(t).astype(BF16) for t in disc[:4]] + [disc[4], disc[5]],
        sgw=[p['sgu_ln_w'][i].reshape(1, BW), p['sgu_ln_b'][i].reshape(1, BW), p['sgu_w'][i],
             jnp.repeat(p['sgu_b'][i].T, BW // SGU_HEADS, axis=1)],
        cb=p['m2_conv_b'][i].reshape(1, M2_CONV_CH),
        m2w=[_pad_lanes(p['m2_dt_bias'][i]), _pad_lanes(p['m2_a_log'][i]),
             jnp.repeat(p['m2_d'][i], M2_HEAD_DIM).reshape(1, BW), p['m2_norm_w'][i].reshape(1, BW)])
    touch = [t[0, 0].astype(F32) for t in prep['s5small']] + [prep['sgw'][3][0, 0], prep['m2w'][2][0, 0]]
    return prep, sum(touch[1:], touch[0])


def _layer_fwd(x, h, i, prep, w_in, other_weights, before_merge=None):
    proj = _matmul(h, w_in, 1, 0, F32, 1024, 1024, 1024, f"proj{i}")
    full = dict(other_weights(proj), w_in=w_in)
    s5w = prep['s5small'] + [full['s5_w_glu']]
    ya, sre, sim = _s5_fwd(proj, *s5w, f"s5_fwd{i}")
    yb = _sgu_fwd(proj, *prep['sgw'], f"sgu_fwd{i}")
    cw = full['m2_conv_w']
    xa = _m2_conv_fwd(proj, cw, prep['cb'], f"m2conv_fwd{i}")
    yc, s_in = _ssd_fwd(proj, xa, *prep['m2w'], f"ssd_fwd{i}")
    scw = full['sc_conv_w']
    yd = _sc_fwd(proj, scw, f"sc_fwd{i}")
    ys = jnp.stack([ya, yb, yc, yd])
    mb = full['merge_b'].reshape(N_BRANCH, 1, D_MODEL)
    if before_merge is not None:
        mb = mb + before_merge(ys)[0, 0]
    merged = _merge_fwd(proj, ys, mb, full['w_branch'], f"merge_fwd{i}")
    x_new = _matmul(merged, full['w_out'], 1, 0, F32, 1024, 1024, 1024, f"out{i}", residual=x)
    saved = dict(x=x, nw=prep['nw'], h=h, proj=proj, disc_vjp=prep['disc_vjp'], s5w=s5w, sre=sre, sim=sim, sgw=prep['sgw'],
                 cw=cw, cb=prep['cb'], xa=xa, m2w=prep['m2w'], s_in=s_in, scw=scw, ys=ys, mb=mb, merged=merged)
    return x_new, saved, full


def _layer_bwd(dx_out, i, sv, full, on_large_grads=None, after_dh=None):
    g = {}
    proj = sv['proj']
    dm = _matmul(dx_out, full['w_out'], 1, 1, F32, 1024, 1024, 1024, f"dmerged{i}")
    g['w_out'] = _matmul(sv['merged'], dx_out, 0, 0, BF16, 1024, 1024, 1024, f"gw_out{i}")
    dys, dproj, g['w_branch'], dmb = _merge_bwd(proj, sv['ys'], dm, sv['mb'], full['w_branch'], f"merge_bwd{i}")
    g['merge_b'] = dmb.reshape(N_BRANCH, D_MODEL)
    dproj, dbbre, dbbim, dcre, dcim, da, dd, dwg = _s5_bwd(proj, dproj, dys, sv['sre'], sv['sim'], *sv['s5w'], f"s5_bwd{i}")
    g['s5_dense'] = (dbbre, dbbim, dcre, dcim, da, dd)
    g['s5_w_glu'] = dwg.astype(BF16)
    dproj, dlw, dlb, g['sgu_w'], dbias = _sgu_bwd(proj, dproj, dys, *sv['sgw'], f"sgu_bwd{i}")
    g['sgu_ln_w'], g['sgu_ln_b'] = dlw[0], dlb[0]
    g['sgu_b'] = dbias.reshape(SGU_CHUNK, SGU_HEADS, BW // SGU_HEADS).sum(-1).T
    dproj, dxa, ddtb, dal, ddf, dnw = _ssd_bwd(proj, dproj, sv['xa'], dys, sv['s_in'], *sv['m2w'], f"ssd_bwd{i}")
    dproj, g['m2_conv_w'], dcb = _m2_conv_bwd(proj, dproj, dxa, sv['cw'], sv['cb'], f"m2conv_bwd{i}")
    g['m2_conv_b'], g['m2_norm_w'] = dcb[0], dnw[0]
    g['m2_dt_bias'], g['m2_a_log'] = ddtb[0, :M2_HEADS], dal[0, :M2_HEADS]
    g['m2_d'] = ddf.reshape(M2_HEADS, M2_HEAD_DIM).sum(-1)
    dproj, g['sc_conv_w'] = _sc_bwd(proj, dproj, dys, sv['scw'], f"sc_bwd{i}")
    g['w_in'] = _matmul(sv['h'], dproj, 0, 0, BF16, 1024, 1024, 1024, f"gw_in{i}")
    tok = on_large_grads(g) if on_large_grads else None
    dh = _matmul(dproj, full['w_in'], 1, 1, F32, 1024, 1024, 1024, f"dh{i}", after=tok)
    nw = sv['nw'] if after_dh is None else sv['nw'] + after_dh(dh)[0, 0]
    dx_in, dnw_l = _rmsnorm_bwd(sv['x'], nw, dh, dx_out, f"rms_bwd{i}")
    g['norm_w'] = dnw_l[0]
    return dx_in, g


def _split8(t, axis):
    shp = t.shape
    t = t.reshape(shp[:axis] + (N_DEV, shp[axis] // N_DEV) + shp[axis + 1:])
    return jnp.moveaxis(t, axis, 0)


def _join8(t, axis):
    t = jnp.moveaxis(t, 0, axis)
    shp = t.shape
    return t.reshape(shp[:axis] + (shp[axis] * shp[axis + 1],) + shp[axis + 2:])


SHARD_AXIS = {'w_in': 2, 'w_branch': 3, 'w_out': 1, 's5_w_glu': 1, 'm2_conv_w': 2, 'sc_conv_w': 2, 'merge_b': 2}


OTHER_BIG = [n for n in BIG_SHARDED if n != 'w_in']


def _other_weights(gathered):
    return {n: _join8(t, SHARD_AXIS[n] - 1) for n, t in zip(OTHER_BIG, gathered)}


def _layer_grad_blocks(g, i):
    blocks = [_relayout_g_in(g[n], f"relayout_g_in{i}") if n == 'w_in' else _split8(g[n], SHARD_AXIS[n] - 1) for n in BIG_SHARDED]
    return [b.reshape(N_DEV, -1, b.shape[-1]) for b in blocks]


def _pair_start(blocks, tag):
    shapes = [(N_CHIP,) + b.shape[1:] for b in blocks]
    return _split_start(_plan_pair, blocks, shapes, N_CHIP * len(blocks), f"pair{tag}_start")


def _pair_sums(state, after, tag):
    mine, theirs = _split_wait(_plan_pair, state, after, f"pair{tag}_wait", with_sources=True)
    return [_pair_sum(b, t, f"pair_sum{tag}_{k}") for k, (b, t) in enumerate(zip(mine, theirs))]


def _chips_start(sums, tag, after=None):
    return _split_start(_plan_chips, sums, [s.shape for s in sums], 3 * len(sums), f"chips{tag}_start", after)


def kernel(x, norm_w, w_in, s5_lambda_re, s5_lambda_im, s5_b_re, s5_b_im, s5_c_re, s5_c_im, s5_d, s5_log_step, s5_w_glu, sgu_ln_w, sgu_ln_b, sgu_w, sgu_b, m2_conv_w, m2_conv_b, m2_dt_bias, m2_a_log, m2_d, m2_norm_w, sc_conv_w, merge_b, w_branch, w_out, final_norm_w, loss_target, m_norm_w, m_w_in, m_s5_lambda_re, m_s5_lambda_im, m_s5_b_re, m_s5_b_im, m_s5_c_re, m_s5_c_im, m_s5_d, m_s5_log_step, m_s5_w_glu, m_sgu_ln_w, m_sgu_ln_b, m_sgu_w, m_sgu_b, m_m2_conv_w, m_m2_conv_b, m_m2_dt_bias, m_m2_a_log, m_m2_d, m_m2_norm_w, m_sc_conv_w, m_merge_b, m_w_branch, m_w_out, m_final_norm_w, v_norm_w, v_w_in, v_s5_lambda_re, v_s5_lambda_im, v_s5_b_re, v_s5_b_im, v_s5_c_re, v_s5_c_im, v_s5_d, v_s5_log_step, v_s5_w_glu, v_sgu_ln_w, v_sgu_ln_b, v_sgu_w, v_sgu_b, v_m2_conv_w, v_m2_conv_b, v_m2_dt_bias, v_m2_a_log, v_m2_d, v_m2_norm_w, v_sc_conv_w, v_merge_b, v_w_branch, v_w_out, v_final_norm_w):
    loc = locals()
    p = {n: loc[n] for n in WEIGHTS}
    mom = {n: loc['m_' + n] for n in WEIGHTS}
    vel = {n: loc['v_' + n] for n in WEIGHTS}

    small_sizes = [p[n].size for n in SMALL_SHARDED]
    small_pack = _rows128(jnp.concatenate([p[n].reshape(-1) for n in SMALL_SHARDED]))
    first = [p['w_in'][0].astype(BF16)]
    gath_first, tok = _split_start(_plan_gather, first, [(N_DEV,) + first[0].shape], 7, "gather_w_in0_start")
    shards = ([(p[n][0] + tok[0, 0]).astype(BF16) for n in OTHER_BIG] + [small_pack + tok[0, 0]]
              + [(p[n][1] + tok[0, 0]).astype(BF16) for n in BIG_SHARDED])
    gath, tok = _split_start(_plan_gather, shards, [(N_DEV,) + t.shape for t in shards], 7 * len(shards), "gather_start")

    def relayed(lo, hi, after, name, started=None):
        started = gath if started is None else started
        n = (len(started) - 3) // 2
        sems, srcs, lands = started[:3], started[3:3 + n], started[3 + n:]
        plan = functools.partial(_plan_gather, first=lo)
        state, tok = _split_relay(plan, (*sems, *srcs[lo:hi], *lands[lo:hi]), after, name + "_relay")
        return (plan, state, name), tok

    def arrived(relay, after):
        plan, state, name = relay
        return _split_wait(plan, state, after, name + "_wait")

    def gathered(lo, hi, after, name, started=None):
        relay, tok = relayed(lo, hi, after, name, started)
        return arrived(relay, tok)

    later = dict(p, **{n: p[n] + tok[0, 0] for n in ('norm_w', 's5_log_step', 'sgu_b', 'm2_d')})
    preps = [_layer_prep(i, later) for i in range(DEPTH)]
    h0 = _rmsnorm_fwd(x[0], preps[0][0]['nw'], "rms_fwd0")
    got = gathered(0, 1, tok + (preps[0][1] + preps[1][1] + h0[0, 0].astype(F32)), "gather_w_in0", gath_first)
    small_full = {}

    def other_weights0(proj):
        got = gathered(0, 4, proj, "gather_rest0")
        small_all, off = got[-1].reshape(N_DEV, -1), 0
        for n, sz in zip(SMALL_SHARDED, small_sizes):
            small_full[n] = _join8(small_all[:, off:off + sz].reshape((N_DEV,) + p[n].shape), SHARD_AXIS[n])
            off += sz
        return dict(_other_weights(got[:-1]), **{n: small_full[n][0] for n in SMALL_SHARDED})

    saved, layer_g, full = [None] * DEPTH, [None] * DEPTH, [None] * DEPTH
    relay1 = []

    def relay_layer1(ys):
        relay, tok = relayed(4, 8, ys, "gather1")
        relay1.append(relay)
        return tok

    xs, saved[0], full[0] = _layer_fwd(x[0], h0, 0, preps[0][0], _relayout_w_in(got[0], "relayout_w_in0"), other_weights0,
                                       relay_layer1)
    h1 = _rmsnorm_fwd(xs, preps[1][0]['nw'], "rms_fwd1")
    got = arrived(relay1[0], h1)
    xs, saved[1], full[1] = _layer_fwd(
        xs, h1, 1, preps[1][0], _relayout_w_in(got[0], "relayout_w_in1"),
        lambda proj: dict(_other_weights(got[1:]), **{n: small_full[n][1] for n in SMALL_SHARDED}))
    loss_row, dx, dfw = _loss_head(xs, final_norm_w.reshape(1, D_MODEL), loss_target[0])
    loss = lax.psum(loss_row[0, 0], ("x", "y", "c"))
    loss, dx = lax.optimization_barrier((loss, dx))
    pairs, scat, sent0 = [None] * DEPTH, [None] * DEPTH, []

    def start_pairs1(g):
        pairs[1], tok = _pair_start(_layer_grad_blocks(g, 1), 1)
        return tok

    def send_chip_sums1(dh):
        scat[1], tok = _chips_start(_pair_sums(pairs[1], dh, 1), 1)
        return tok

    def send_all0(g):
        pairs[0], tok = _pair_start(_layer_grad_blocks(g, 0), 0)
        scat[0], tok = _chips_start(_pair_sums(pairs[0], tok, 0), 0)
        sent0.append(tok)
        return tok

    dx, layer_g[1] = _layer_bwd(dx, 1, saved[1], full[1], on_large_grads=start_pairs1, after_dh=send_chip_sums1)
    dx, layer_g[0] = _layer_bwd(dx, 0, saved[0], full[0], on_large_grads=send_all0)
    for i in range(DEPTH):
        dense = layer_g[i].pop('s5_dense')
        blocks = tuple(_diag_blocks(t, after=sent0[0]) for t in dense[:4])
        layer_g[i].update(zip(S5_NAMES, saved[i]['disc_vjp'](blocks + (dense[4] + sent0[0][0, 0], dense[5]))))
    grads = {n: jnp.stack([layer_g[i][n] for i in range(DEPTH)]) for n in SMALL_SHARDED + REPLICATED if n != 'final_norm_w'}
    grads['final_norm_w'] = dfw[0]

    out_g, out_d, out_m, out_v = {}, {}, {}, {}
    repl_rows = _pack_rows([grads[n] for n in REPLICATED], 8 * N_DEV)
    rr = repl_rows.shape[0] // N_DEV
    shard_rows = _pack_rows([_split8(grads[n], SHARD_AXIS[n]) for n in SMALL_SHARDED], 8, batched=True)
    rs = shard_rows.shape[1]
    small_g = jnp.concatenate([shard_rows, repl_rows.reshape(N_DEV, rr, LANES)], axis=1)
    all_to_all, all_gather = functools.partial(_plan_direct, gather=False), functools.partial(_plan_direct, gather=True)
    small_state, tok = _split_start(all_to_all, [small_g], [small_g.shape], N_DEV - 1, "scatter_small_start")
    landed1 = _split_wait(_plan_chips, scat[1], tok, "chips1_wait")
    landed0 = _split_wait(_plan_chips, scat[0], landed1[0], "chips0_wait")

    def big_adamw(n, after=None):
        k, shp = BIG_SHARDED.index(n), p[n].shape
        if n == 'w_in':
            return _adamw_w_in([landed0[k], landed1[k]], p[n], mom[n], vel[n], "adamw_w_in", after)
        c = shp[-1]
        r = p[n].size // (DEPTH * c)
        res = _adamw([landed0[k], landed1[k]], *[d[n].reshape(DEPTH, r, c) for d in (p, mom, vel)],
                     {'w_branch': 512, 'w_out': 128, 's5_w_glu': 64}[n], "adamw_" + n)
        return [o.reshape(shp) for o in res]

    for n in OTHER_BIG:
        out_g[n], out_d[n], out_m[n], out_v[n] = big_adamw(n)
    updated = sum(out_d[n].reshape(-1)[0] for n in OTHER_BIG).reshape(1, 1)
    small_sum = _slot_sum(_split_wait(all_to_all, small_state, updated, "scatter_small_wait")[0], "sum_small")
    repl_part = small_sum[rs:]
    repl_state, tok = _split_start(all_gather, [repl_part], [(N_DEV,) + repl_part.shape], N_DEV - 1, "gather_small_start")
    out_g['w_in'], out_d['w_in'], out_m['w_in'], out_v['w_in'] = big_adamw('w_in', tok)
    repl_all = _split_wait(all_gather, repl_state, out_d['w_in'], "gather_small_wait")[0].reshape(N_DEV * rr, LANES)
    g_all = jnp.concatenate([small_sum[:rs], repl_all], axis=0)
    names = SMALL_SHARDED + REPLICATED
    pieces = (_unpack_rows(g_all[:rs], [p[n].shape for n in SMALL_SHARDED])
              + _unpack_rows(g_all[rs:], [p[n].shape for n in REPLICATED]))
    out_g.update(zip(names, pieces))
    res = _adamw_many(*[[_memory_view(n, d[n]) for n in names] for d in (out_g, p, mom, vel)], "adamw_small")
    for r, dst in zip(res, (out_d, out_m, out_v)):
        dst.update({n: _memory_view(n, t) for n, t in zip(names, r)})
    return (loss, dx[None], *[out_g[n] for n in WEIGHTS], *[out_d[n] for n in WEIGHTS],
            *[out_m[n] for n in WEIGHTS], *[out_v[n] for n in WEIGHTS])
```

```python
import functools

import jax
import jax.numpy as jnp
import numpy as np
from jax import lax
from jax.experimental import pallas as pl
from jax.experimental.pallas import tpu as pltpu

F32 = jnp.float32
BF16 = jnp.bfloat16

N_DEV = 8
SEQ = 2048
D_MODEL = 1024
DEPTH = 2
BW = 512
N_BRANCH = 4
EPS = 1e-6
S5_GROUPS, S5_STATE, S5_P = 32, 64, 16
S5_CH = S5_GROUPS * S5_STATE
SGU_CHUNK, SGU_HEADS = 128, 8
M2_HEADS, M2_HEAD_DIM, M2_STATE, M2_CHUNK, M2_CONV = 8, 64, 128, 128, 4
M2_CONV_CH = 1024
SC_CONV = 3
IN_DIM = 10248
IN_PAD = 11264
C_MERGE = 0
C_S5U, C_S5G = 4096, 4608
C_M2X = 5120
C_SGU_U, C_SGU_V, C_SGU_G = 6144, 6656, 7168
C_M2Z, C_DT = 8192, 8704
C_SC = 9216
SHARD_IN = IN_DIM // N_DEV

ADAM_LR, ADAM_B1, ADAM_B2, ADAM_EPS, ADAM_WD, ADAM_STEP = 0.001, 0.9, 0.999, 1e-08, 0.01, 10

VMEM_LIMIT = 56 * 1024 * 1024
LANES = 128

MESH = pl.DeviceIdType.MESH


def _cparams(sem=None, **kw):
    return pltpu.CompilerParams(dimension_semantics=sem, vmem_limit_bytes=VMEM_LIMIT, **kw)


def _dg(a, b, ca, cb, precision=None):
    return lax.dot_general(a, b, (((ca,), (cb,)), ((), ())), precision=precision,
                           preferred_element_type=F32)


@functools.partial(jax.custom_vjp, nondiff_argnums=(2, 3))
def _bdot(a, b, ca, cb):
    return _dg(a.astype(BF16), b.astype(BF16), ca, cb)


def _bdot_fwd(a, b, ca, cb):
    return _bdot(a, b, ca, cb), (a, b)


def _bdot_bwd(ca, cb, res, g):
    a, b = res
    gb, ab, bb = g.astype(BF16), a.astype(BF16), b.astype(BF16)
    da = _dg(gb, bb, 1, 1 - cb) if ca == 1 else _dg(bb, gb, 1 - cb, 1)
    db = _dg(ab, gb, 1 - ca, 0) if cb == 0 else _dg(gb, ab, 0, 1 - ca)
    return da.astype(a.dtype), db.astype(b.dtype)


_bdot.defvjp(_bdot_fwd, _bdot_bwd)


def _rms(x, w):
    return x * lax.rsqrt(jnp.mean(x * x, axis=-1, keepdims=True) + EPS) * w


def _silu(x):
    return x * jax.nn.sigmoid(x)


def _gelu(x):
    return 0.5 * x * (1.0 + jnp.tanh(0.7978845608028654 * (x + 0.044715 * (x * x * x))))


def _softplus(x):
    return jnp.maximum(x, 0.0) + jnp.log1p(jnp.exp(-jnp.abs(x)))


def _shift_down(x, s):
    if s == 0:
        return x
    row = lax.broadcasted_iota(jnp.int32, x.shape, 0)
    return jnp.where(row >= s, pltpu.roll(x, s, 0), 0.0)


def _shift_up(x, s):
    if s == 0:
        return x
    n = x.shape[0]
    row = lax.broadcasted_iota(jnp.int32, x.shape, 0)
    return jnp.where(row < n - s, pltpu.roll(x, n - s, 0), 0.0)


def _matmul(a, b, ca, cb, out_dtype, tm, tn, tk, name, residual=None, after=None):
    m = a.shape[1 - ca]
    k = a.shape[ca]
    n = b.shape[1 - cb]
    assert b.shape[cb] == k and m % tm == 0 and n % tn == 0 and k % tk == 0
    nk = k // tk
    a_spec = pl.BlockSpec((tm, tk), lambda i, j, kk: (i, kk)) if ca == 1 else pl.BlockSpec((tk, tm), lambda i, j, kk: (kk, i))
    b_spec = pl.BlockSpec((tk, tn), lambda i, j, kk: (kk, j)) if cb == 0 else pl.BlockSpec((tn, tk), lambda i, j, kk: (j, kk))
    o_spec = pl.BlockSpec((tm, tn), lambda i, j, kk: (i, j))
    has_res = residual is not None

    def body(*refs):
        refs = refs[:2 + has_res] + refs[2 + has_res + (after is not None):]
        if has_res:
            a_ref, b_ref, r_ref, o_ref, acc = refs
        else:
            a_ref, b_ref, o_ref, acc = refs
        kk = pl.program_id(2)
        part = _dg(a_ref[...].astype(BF16), b_ref[...].astype(BF16), ca, cb)

        @pl.when(kk == 0)
        def _():
            acc[...] = part

        @pl.when(kk > 0)
        def _():
            acc[...] += part

        @pl.when(kk == nk - 1)
        def _():
            r = acc[...]
            if has_res:
                r = r + r_ref[...]
            o_ref[...] = r.astype(out_dtype)

    ins = [a, b] + ([residual] if has_res else []) + ([after] if after is not None else [])
    specs = [a_spec, b_spec] + ([o_spec] if has_res else []) + ([pl.BlockSpec(memory_space=pl.ANY)] if after is not None else [])
    return pl.pallas_call(
        body, name=name, grid=(m // tm, n // tn, nk), in_specs=specs, out_specs=o_spec,
        out_shape=jax.ShapeDtypeStruct((m, n), out_dtype),
        scratch_shapes=[pltpu.VMEM((tm, tn), F32)],
        compiler_params=_cparams(("parallel", "parallel", "arbitrary")),
    )(*ins)


ROW_TILE = 512


def _rmsnorm_fwd(x, w, name):
    def body(x_ref, w_ref, o_ref):
        o_ref[...] = _rms(x_ref[...], w_ref[...]).astype(BF16)

    return pl.pallas_call(
        body, name=name, grid=(SEQ // ROW_TILE,),
        in_specs=[pl.BlockSpec((ROW_TILE, D_MODEL), lambda i: (i, 0)), pl.BlockSpec((1, D_MODEL), lambda i: (0, 0))],
        out_specs=pl.BlockSpec((ROW_TILE, D_MODEL), lambda i: (i, 0)),
        out_shape=jax.ShapeDtypeStruct((SEQ, D_MODEL), BF16),
        compiler_params=_cparams(("parallel",)),
    )(x, w)


def _rmsnorm_bwd(x, w, dh, dres, name):
    def body(x_ref, w_ref, dh_ref, dres_ref, dx_ref, dw_ref):
        _, vjp = jax.vjp(_rms, x_ref[...], w_ref[...])
        dx, dw = vjp(dh_ref[...])
        dx_ref[...] = dx + dres_ref[...]

        @pl.when(pl.program_id(0) == 0)
        def _():
            dw_ref[...] = dw

        @pl.when(pl.program_id(0) > 0)
        def _():
            dw_ref[...] += dw

    tile = pl.BlockSpec((ROW_TILE, D_MODEL), lambda i: (i, 0))
    vec = pl.BlockSpec((1, D_MODEL), lambda i: (0, 0))
    return pl.pallas_call(
        body, name=name, grid=(SEQ // ROW_TILE,),
        in_specs=[tile, vec, tile, tile], out_specs=[tile, vec],
        out_shape=[jax.ShapeDtypeStruct((SEQ, D_MODEL), F32), jax.ShapeDtypeStruct((1, D_MODEL), F32)],
        compiler_params=_cparams(("arbitrary",)),
    )(x, w, dh, dres)


def _loss_head(x, w, target):
    def body(x_ref, w_ref, t_ref, loss_ref, dx_ref, dw_ref):
        tgt = t_ref[...]

        def f(xv, wv):
            err = _rms(xv, wv) - tgt
            return 0.5 * jnp.sum(jnp.mean(err * err, axis=-1))

        loss, vjp = jax.vjp(f, x_ref[...], w_ref[...])
        dx, dw = vjp(jnp.ones((), F32))
        dx_ref[...] = dx
        lrow = jnp.full((1, LANES), loss, F32)

        @pl.when(pl.program_id(0) == 0)
        def _():
            dw_ref[...] = dw
            loss_ref[...] = lrow

        @pl.when(pl.program_id(0) > 0)
        def _():
            dw_ref[...] += dw
            loss_ref[...] += lrow

    tile = pl.BlockSpec((ROW_TILE, D_MODEL), lambda i: (i, 0))
    vec = pl.BlockSpec((1, D_MODEL), lambda i: (0, 0))
    return pl.pallas_call(
        body, name="loss_head", grid=(SEQ // ROW_TILE,),
        in_specs=[tile, vec, tile], out_specs=[pl.BlockSpec((1, LANES), lambda i: (0, 0)), tile, vec],
        out_shape=[jax.ShapeDtypeStruct((1, LANES), F32), jax.ShapeDtypeStruct((SEQ, D_MODEL), F32),
                   jax.ShapeDtypeStruct((1, D_MODEL), F32)],
        compiler_params=_cparams(("arbitrary",)),
    )(x, w, target)


S5_T = 256
S5_BLOCKS = [(slice(j * 256, (j + 1) * 256), slice(j * 1024, (j + 1) * 1024)) for j in range(2)]


def _s5_post(ypre, gate, wglu):
    y = _gelu(ypre)
    y = y * jax.nn.sigmoid(_bdot(y, wglu, 1, 0))
    return y * _silu(gate)


def _s5_fwd(proj, bbre, bbim, cre, cim, a2, dvec, wglu, name):
    def body(u_ref, g_ref, bbre_ref, bbim_ref, cre_ref, cim_ref, a_ref, d_ref, wg_ref, o_ref, sre_ref, sim_ref, st):
        @pl.when(pl.program_id(0) == 0)
        def _():
            st[...] = jnp.zeros_like(st)

        u = u_ref[...]
        ub = u.astype(BF16)
        for us, ss in S5_BLOCKS:
            sre_ref[:, ss] = _dg(ub[:, us], bbre_ref[us, ss], 1, 0)
            sim_ref[:, ss] = _dg(ub[:, us], bbim_ref[us, ss], 1, 0)
        ar, ai = a_ref[0:1, :], a_ref[1:2, :]

        def step(t, carry):
            sr, si = carry
            nr = ar * sr - ai * si + sre_ref[pl.ds(t, 1), :]
            ni = ar * si + ai * sr + sim_ref[pl.ds(t, 1), :]
            sre_ref[pl.ds(t, 1), :] = nr
            sim_ref[pl.ds(t, 1), :] = ni
            return nr, ni

        sr, si = lax.fori_loop(0, S5_T, step, (st[0:1, :], st[1:2, :]), unroll=8)
        st[0:1, :] = sr
        st[1:2, :] = si
        ypre = jnp.concatenate(
            [_dg(sre_ref[:, ss].astype(BF16), cre_ref[ss, us], 1, 0) - _dg(sim_ref[:, ss].astype(BF16), cim_ref[ss, us], 1, 0)
             for us, ss in S5_BLOCKS], axis=1) + d_ref[...] * u
        o_ref[...] = _s5_post(ypre, g_ref[...], wg_ref[...]).astype(BF16)

    full = lambda shape: pl.BlockSpec(shape, lambda c: (0, 0))
    return pl.pallas_call(
        body, name=name, grid=(SEQ // S5_T,),
        in_specs=[pl.BlockSpec((S5_T, BW), lambda c: (c, C_S5U // BW)), pl.BlockSpec((S5_T, BW), lambda c: (c, C_S5G // BW)),
                  full((BW, S5_CH)), full((BW, S5_CH)), full((S5_CH, BW)), full((S5_CH, BW)),
                  full((2, S5_CH)), full((1, BW)), full((BW, BW))],
        out_specs=[pl.BlockSpec((S5_T, BW), lambda c: (c, 0)), pl.BlockSpec((S5_T, S5_CH), lambda c: (c, 0)),
                   pl.BlockSpec((S5_T, S5_CH), lambda c: (c, 0))],
        out_shape=[jax.ShapeDtypeStruct((SEQ, BW), BF16), jax.ShapeDtypeStruct((SEQ, S5_CH), F32),
                   jax.ShapeDtypeStruct((SEQ, S5_CH), F32)],
        scratch_shapes=[pltpu.VMEM((2, S5_CH), F32)],
        compiler_params=_cparams(("arbitrary",)),
    )(proj, proj, bbre, bbim, cre, cim, a2, dvec, wglu)


def _s5_bwd(proj, dproj, dout, sre, sim, bbre, bbim, cre, cim, a2, dvec, wglu, name):
    nc = SEQ // S5_T

    def body(u_ref, g_ref, do_ref, sre_ref, sim_ref, pre_ref, pim_ref, bbre_ref, bbim_ref, cre_ref, cim_ref, a_ref,
             d_ref, wg_ref, dproj_in, dp_ref, dbbre_ref, dbbim_ref, dcre_ref, dcim_ref, da_ref, dd_ref, dwg_ref,
             gre, gim, st):
        c = nc - 1 - pl.program_id(0)

        @pl.when(pl.program_id(0) == 0)
        def _():
            st[...] = jnp.zeros_like(st)
            for r in (dbbre_ref, dbbim_ref, dcre_ref, dcim_ref, da_ref, dd_ref, dwg_ref):
                r[...] = jnp.zeros_like(r)

        u = u_ref[...]
        s_re, s_im = sre_ref[...], sim_ref[...]

        def head(s_res, s_ims, cres, cims, dv, uv, gv, wg):
            ypre = jnp.concatenate([_bdot(sr, cr, 1, 0) - _bdot(si, ci, 1, 0)
                                    for sr, si, cr, ci in zip(s_res, s_ims, cres, cims)], axis=1) + dv * uv
            return _s5_post(ypre, gv, wg)

        _, vjp = jax.vjp(head, [sre_ref[:, ss] for _, ss in S5_BLOCKS], [sim_ref[:, ss] for _, ss in S5_BLOCKS],
                         [cre_ref[ss, us].astype(F32) for us, ss in S5_BLOCKS],
                         [cim_ref[ss, us].astype(F32) for us, ss in S5_BLOCKS],
                         d_ref[...], u, g_ref[...], wg_ref[...].astype(F32))
        ds_res, ds_ims, dcres, dcims, dd, du_d, dgate, dwg = vjp(do_ref[0])
        for k, (us, ss) in enumerate(S5_BLOCKS):
            dcre_ref[ss, us] += dcres[k]
            dcim_ref[ss, us] += dcims[k]
            gre[:, ss] = ds_res[k]
            gim[:, ss] = ds_ims[k]
        dd_ref[...] += dd
        dwg_ref[...] += dwg
        dp_ref[:, BW:] = dgate.astype(BF16)
        ar, ai = a_ref[0:1, :], a_ref[1:2, :]

        def step(i, carry):
            t = S5_T - 1 - i
            gr, gi = carry
            nr = gre[pl.ds(t, 1), :] + gr
            ni = gim[pl.ds(t, 1), :] + gi
            gre[pl.ds(t, 1), :] = nr
            gim[pl.ds(t, 1), :] = ni
            return ar * nr + ai * ni, ar * ni - ai * nr

        gr, gi = lax.fori_loop(0, S5_T, step, (st[0:1, :], st[1:2, :]), unroll=8)
        st[0:1, :] = gr
        st[1:2, :] = gi
        g_re, g_im = gre[...], gim[...]
        first = jnp.where(c > 0, 1.0, 0.0)
        row = lax.broadcasted_iota(jnp.int32, (S5_T, S5_CH), 0)
        p_re = jnp.where(row == 0, pre_ref[7:8, :] * first, pltpu.roll(s_re, 1, 0))
        p_im = jnp.where(row == 0, pim_ref[7:8, :] * first, pltpu.roll(s_im, 1, 0))
        da_ref[0:1, :] += jnp.sum(g_re * p_re + g_im * p_im, axis=0, keepdims=True)
        da_ref[1:2, :] += jnp.sum(g_im * p_re - g_re * p_im, axis=0, keepdims=True)
        ub, grb, gib = u.astype(BF16), g_re.astype(BF16), g_im.astype(BF16)
        du_s = []
        for us, ss in S5_BLOCKS:
            dbbre_ref[us, ss] += _dg(ub[:, us], grb[:, ss], 0, 0)
            dbbim_ref[us, ss] += _dg(ub[:, us], gib[:, ss], 0, 0)
            du_s.append(_dg(grb[:, ss], bbre_ref[us, ss], 1, 1) + _dg(gib[:, ss], bbim_ref[us, ss], 1, 1))
        dp_ref[:, :BW] = (du_d + jnp.concatenate(du_s, axis=1)).astype(BF16)

    full = lambda shape: pl.BlockSpec(shape, lambda i: (0, 0))
    rev = lambda w, col=0: pl.BlockSpec((S5_T, w), lambda i: (nc - 1 - i, col))
    prev = pl.BlockSpec((8, S5_CH), lambda i: (jnp.maximum((nc - 1 - i) * (S5_T // 8) - 1, 0), 0))
    return pl.pallas_call(
        body, name=name, grid=(nc,),
        in_specs=[rev(BW, C_S5U // BW), rev(BW, C_S5G // BW), pl.BlockSpec((1, S5_T, BW), lambda i: (0, nc - 1 - i, 0)),
                  rev(S5_CH), rev(S5_CH), prev, prev,
                  full((BW, S5_CH)), full((BW, S5_CH)), full((S5_CH, BW)), full((S5_CH, BW)),
                  full((2, S5_CH)), full((1, BW)), full((BW, BW)), pl.BlockSpec(memory_space=pl.ANY)],
        out_specs=[rev(2 * BW, C_S5U // (2 * BW)), full((BW, S5_CH)), full((BW, S5_CH)), full((S5_CH, BW)), full((S5_CH, BW)),
                   full((2, S5_CH)), full((1, BW)), full((BW, BW))],
        input_output_aliases={14: 0},
        out_shape=[jax.ShapeDtypeStruct((SEQ, IN_PAD), BF16),
                   jax.ShapeDtypeStruct((BW, S5_CH), F32), jax.ShapeDtypeStruct((BW, S5_CH), F32),
                   jax.ShapeDtypeStruct((S5_CH, BW), F32), jax.ShapeDtypeStruct((S5_CH, BW), F32),
                   jax.ShapeDtypeStruct((2, S5_CH), F32), jax.ShapeDtypeStruct((1, BW), F32),
                   jax.ShapeDtypeStruct((BW, BW), F32)],
        scratch_shapes=[pltpu.VMEM((S5_T, S5_CH), F32), pltpu.VMEM((S5_T, S5_CH), F32), pltpu.VMEM((2, S5_CH), F32)],
        compiler_params=_cparams(("arbitrary",)),
    )(proj, proj, dout, sre, sim, sre, sim, bbre, bbim, cre, cim, a2, dvec, wglu, dproj)


def _diag_blocks(dense, after=None):
    rows, cols = dense.shape
    rows_per, cols_per = rows // S5_GROUPS, cols // S5_GROUPS
    per_lane_block = LANES // cols_per
    tile = 512

    def body(d_ref, *rest):
        o_ref = rest[-1]
        r0 = pl.program_id(0) * tile
        grp = (r0 + lax.broadcasted_iota(jnp.int32, (tile, LANES), 0)) // rows_per
        lane = lax.broadcasted_iota(jnp.int32, (tile, LANES), 1)
        acc = jnp.zeros((tile, LANES), F32)
        for hb in range(cols // LANES):
            acc = acc + jnp.where(grp == per_lane_block * hb + lane // cols_per, d_ref[:, hb * LANES:(hb + 1) * LANES], 0.0)
        shift = LANES // 2
        while shift >= cols_per:
            acc = acc + pltpu.roll(acc, LANES - shift, 1)
            shift //= 2
        o_ref[...] = acc

    folded = pl.pallas_call(
        body, name=f"diag_blocks_{rows_per}x{cols_per}", grid=(rows // tile,),
        in_specs=[pl.BlockSpec((tile, cols), lambda i: (i, 0))] + ([] if after is None else [pl.BlockSpec(memory_space=pl.ANY)]),
        out_specs=pl.BlockSpec((tile, LANES), lambda i: (i, 0)),
        out_shape=jax.ShapeDtypeStruct((rows, LANES), F32), compiler_params=_cparams(("parallel",)),
    )(dense, *([] if after is None else [after]))
    return folded[:, :cols_per].reshape(S5_GROUPS, rows_per, cols_per)


def _block_diag(t):
    g, rows_per, cols_per = t.shape
    wide = jnp.tile(t.reshape(g * rows_per, cols_per), (1, g))
    r = lax.broadcasted_iota(jnp.int32, wide.shape, 0) // rows_per
    c = lax.broadcasted_iota(jnp.int32, wide.shape, 1) // cols_per
    return jnp.where(r == c, wide, 0.0)


def _s5_disc(lam_re, lam_im, b_re, b_im, c_re, c_im, d, log_step):
    step = jnp.exp(log_step)[:, None]
    mag = jnp.exp(lam_re * step)
    ab_re, ab_im = mag * jnp.cos(lam_im * step), mag * jnp.sin(lam_im * step)
    den = lam_re * lam_re + lam_im * lam_im
    nr = ab_re - 1.0
    coef_re = (nr * lam_re + ab_im * lam_im) / den
    coef_im = (ab_im * lam_re - nr * lam_im) / den
    bb_re = coef_re[..., None] * b_re - coef_im[..., None] * b_im
    bb_im = coef_re[..., None] * b_im + coef_im[..., None] * b_re
    a2 = jnp.stack([ab_re.reshape(-1), ab_im.reshape(-1)])
    return (jnp.swapaxes(bb_re, 1, 2), jnp.swapaxes(bb_im, 1, 2),
            jnp.swapaxes(c_re, 1, 2), jnp.swapaxes(c_im, 1, 2),
            a2, d.reshape(1, BW))


def _left_lanes(shape):
    return lax.broadcasted_iota(jnp.int32, shape, 1) < 64


def _sgu_chunk(u, v, gate, ln_w, ln_b, w, bias):
    u32, v32 = _gelu(u), _gelu(v)
    mu = jnp.mean(v32, axis=-1, keepdims=True)
    var = jnp.mean(jnp.square(v32 - mu), axis=-1, keepdims=True)
    vn = (v32 - mu) * lax.rsqrt(var + EPS) * ln_w + ln_b
    t_i = lax.broadcasted_iota(jnp.int32, (SGU_CHUNK, SGU_CHUNK), 0)
    s_i = lax.broadcasted_iota(jnp.int32, (SGU_CHUNK, SGU_CHUNK), 1)
    causal = t_i >= s_i
    left = _left_lanes((SGU_CHUNK, LANES))
    sgate = _silu(gate)
    outs = []
    for j in range(BW // LANES):
        vb = vn[:, j * LANES:(j + 1) * LANES]
        s_blk = (_bdot(jnp.where(causal, w[2 * j], 0.0), jnp.where(left, vb, 0.0), 1, 0)
                 + _bdot(jnp.where(causal, w[2 * j + 1], 0.0), jnp.where(left, 0.0, vb), 1, 0))
        sl = slice(j * LANES, (j + 1) * LANES)
        outs.append(u32[:, sl] * (s_blk + bias[:, sl]) * sgate[:, sl])
    return outs


def _sgu_fwd(proj, ln_w, ln_b, w, bias, name):
    def body(u_ref, v_ref, g_ref, lw_ref, lb_ref, w_ref, b_ref, o_ref):
        outs = _sgu_chunk(u_ref[...], v_ref[...], g_ref[...], lw_ref[...], lb_ref[...], w_ref[...], b_ref[...])
        for j, o in enumerate(outs):
            o_ref[:, j * LANES:(j + 1) * LANES] = o.astype(BF16)

    blk = lambda col: pl.BlockSpec((SGU_CHUNK, BW), lambda c: (c, col // BW))
    vec = pl.BlockSpec((1, BW), lambda c: (0, 0))
    return pl.pallas_call(
        body, name=name, grid=(SEQ // SGU_CHUNK,),
        in_specs=[blk(C_SGU_U), blk(C_SGU_V), blk(C_SGU_G), vec, vec,
                  pl.BlockSpec((SGU_HEADS, SGU_CHUNK, SGU_CHUNK), lambda c: (0, 0, 0)),
                  pl.BlockSpec((SGU_CHUNK, BW), lambda c: (0, 0))],
        out_specs=pl.BlockSpec((SGU_CHUNK, BW), lambda c: (c, 0)),
        out_shape=jax.ShapeDtypeStruct((SEQ, BW), BF16),
        compiler_params=_cparams(("parallel",)),
    )(proj, proj, proj, ln_w, ln_b, w, bias)


def _sgu_bwd(proj, dproj, dout, ln_w, ln_b, w, bias, name):
    def body(u_ref, v_ref, g_ref, do_ref, lw_ref, lb_ref, w_ref, b_ref, dproj_in, dp_ref, dlw_ref, dlb_ref, dw_ref, db_ref):
        _, vjp = jax.vjp(_sgu_chunk, u_ref[...], v_ref[...], g_ref[...], lw_ref[...], lb_ref[...], w_ref[...], b_ref[...])
        do = do_ref[0]
        du, dv, dgate, dlw, dlb, dw, db = vjp([do[:, j * LANES:(j + 1) * LANES] for j in range(BW // LANES)])
        dp_ref[:, 0:BW] = du.astype(BF16)
        dp_ref[:, BW:2 * BW] = dv.astype(BF16)
        dp_ref[:, 2 * BW:3 * BW] = dgate.astype(BF16)
        dp_ref[:, 3 * BW:] = jnp.zeros((SGU_CHUNK, BW), BF16)

        @pl.when(pl.program_id(0) == 0)
        def _():
            dlw_ref[...] = dlw
            dlb_ref[...] = dlb
            dw_ref[...] = dw
            db_ref[...] = db

        @pl.when(pl.program_id(0) > 0)
        def _():
            dlw_ref[...] += dlw
            dlb_ref[...] += dlb
            dw_ref[...] += dw
            db_ref[...] += db

    blk = lambda col: pl.BlockSpec((SGU_CHUNK, BW), lambda c: (c, col // BW))
    vec = pl.BlockSpec((1, BW), lambda c: (0, 0))
    wsp = pl.BlockSpec((SGU_HEADS, SGU_CHUNK, SGU_CHUNK), lambda c: (0, 0, 0))
    bsp = pl.BlockSpec((SGU_CHUNK, BW), lambda c: (0, 0))
    return pl.pallas_call(
        body, name=name, grid=(SEQ // SGU_CHUNK,),
        in_specs=[blk(C_SGU_U), blk(C_SGU_V), blk(C_SGU_G), pl.BlockSpec((1, SGU_CHUNK, BW), lambda c: (1, c, 0)),
                  vec, vec, wsp, bsp, pl.BlockSpec(memory_space=pl.ANY)],
        out_specs=[pl.BlockSpec((SGU_CHUNK, 4 * BW), lambda c: (c, C_SGU_U // (4 * BW))), vec, vec, wsp, bsp],
        input_output_aliases={8: 0},
        out_shape=[jax.ShapeDtypeStruct((SEQ, IN_PAD), BF16), jax.ShapeDtypeStruct((1, BW), F32),
                   jax.ShapeDtypeStruct((1, BW), F32), jax.ShapeDtypeStruct((SGU_HEADS, SGU_CHUNK, SGU_CHUNK), F32),
                   jax.ShapeDtypeStruct((SGU_CHUNK, BW), F32)],
        compiler_params=_cparams(("arbitrary",)),
    )(proj, proj, proj, dout, ln_w, ln_b, w, bias, dproj)


CONV_BLK = 256


def _m2_conv_fwd(proj, w, b, name):
    def body(x_ref, w_ref, b_ref, o_ref):
        x = x_ref[...]
        acc = jnp.zeros_like(x) + b_ref[...]
        for k in range(M2_CONV):
            acc = acc + w_ref[k:k + 1, :] * _shift_down(x, M2_CONV - 1 - k)
        o_ref[...] = _silu(acc)

    return pl.pallas_call(
        body, name=name, grid=(M2_CONV_CH // CONV_BLK,),
        in_specs=[pl.BlockSpec((SEQ, CONV_BLK), lambda j: (0, C_M2X // CONV_BLK + j)),
                  pl.BlockSpec((M2_CONV, CONV_BLK), lambda j: (0, j)), pl.BlockSpec((1, CONV_BLK), lambda j: (0, j))],
        out_specs=pl.BlockSpec((SEQ, CONV_BLK), lambda j: (0, j)),
        out_shape=jax.ShapeDtypeStruct((SEQ, M2_CONV_CH), F32),
        compiler_params=_cparams(("parallel",)),
    )(proj, w, b)


def _m2_conv_bwd(proj, dproj, dxa, w, b, name):
    def body(x_ref, d_ref, w_ref, b_ref, dproj_in, dx_ref, dw_ref, db_ref):
        x = x_ref[...]
        xs = [_shift_down(x, M2_CONV - 1 - k) for k in range(M2_CONV)]
        acc = jnp.zeros_like(x) + b_ref[...]
        for k in range(M2_CONV):
            acc = acc + w_ref[k:k + 1, :] * xs[k]
        sg = jax.nn.sigmoid(acc)
        dacc = d_ref[...] * (sg * (1.0 + acc * (1.0 - sg)))
        dx = jnp.zeros_like(x)
        for k in range(M2_CONV):
            dx = dx + w_ref[k:k + 1, :] * _shift_up(dacc, M2_CONV - 1 - k)
            dw_ref[k:k + 1, :] = jnp.sum(dacc * xs[k], axis=0, keepdims=True)
        dx_ref[...] = dx.astype(BF16)
        db_ref[...] = jnp.sum(dacc, axis=0, keepdims=True)

    return pl.pallas_call(
        body, name=name, grid=(M2_CONV_CH // CONV_BLK,),
        in_specs=[pl.BlockSpec((SEQ, CONV_BLK), lambda j: (0, C_M2X // CONV_BLK + j)),
                  pl.BlockSpec((SEQ, CONV_BLK), lambda j: (0, j)),
                  pl.BlockSpec((M2_CONV, CONV_BLK), lambda j: (0, j)), pl.BlockSpec((1, CONV_BLK), lambda j: (0, j)),
                  pl.BlockSpec(memory_space=pl.ANY)],
        out_specs=[pl.BlockSpec((SEQ, CONV_BLK), lambda j: (0, C_M2X // CONV_BLK + j)),
                   pl.BlockSpec((M2_CONV, CONV_BLK), lambda j: (0, j)), pl.BlockSpec((1, CONV_BLK), lambda j: (0, j))],
        input_output_aliases={4: 0},
        out_shape=[jax.ShapeDtypeStruct((SEQ, IN_PAD), BF16), jax.ShapeDtypeStruct((M2_CONV, M2_CONV_CH), F32),
                   jax.ShapeDtypeStruct((1, M2_CONV_CH), F32)],
        compiler_params=_cparams(("parallel",)),
    )(proj, dxa, w, b, dproj)


N_PAIR = M2_HEADS // 2
HI = lax.Precision.HIGHEST


def _col(a, h):
    lane = lax.broadcasted_iota(jnp.int32, a.shape, 1)
    return jnp.sum(jnp.where(lane == h, a, 0.0), axis=1, keepdims=True)


def _row(a, h):
    sub = lax.broadcasted_iota(jnp.int32, a.shape, 0)
    return jnp.sum(jnp.where(sub == h, a, 0.0), axis=0, keepdims=True)


def _ssd_chunk(xs, bms, cms, dtr, zs, states, dt_bias, a_log, dfs, nws):
    q = M2_CHUNK
    dt = _softplus(dtr + dt_bias)
    da = dt * (-jnp.exp(a_log))
    l_i = lax.broadcasted_iota(jnp.int32, (q, q), 0)
    s_i = lax.broadcasted_iota(jnp.int32, (q, q), 1)
    causal = l_i >= s_i
    tril = jnp.where(causal, 1.0, 0.0)
    a_cs = _dg(tril, da, 1, 0, HI)
    a_cs_t = _dg(da, tril, 0, 1, HI)
    a_end = _row(a_cs, q - 1)
    left = _left_lanes((q, LANES))
    left1 = _left_lanes((1, LANES))
    ys, nexts = [], []
    for j in range(N_PAIR):
        grp = j // 2
        bm, cm = bms[grp], cms[grp]
        h0, h1 = 2 * j, 2 * j + 1
        cb = _bdot(cm, bm, 1, 1)
        xdt = xs[j] * jnp.where(left, _col(dt, h0), _col(dt, h1))
        acs0, acs1 = _col(a_cs, h0), _col(a_cs, h1)
        y = _bdot(cm, states[j], 1, 0) * jnp.where(left, jnp.exp(acs0), jnp.exp(acs1))
        s_new = states[j] * jnp.where(left1, jnp.exp(_col(a_end, h0)), jnp.exp(_col(a_end, h1)))
        for h, acs, xh in ((h0, acs0, jnp.where(left, xdt, 0.0)), (h1, acs1, jnp.where(left, 0.0, xdt))):
            decay = jnp.exp(jnp.where(causal, acs - _row(a_cs_t, h), -jnp.inf))
            y = y + _bdot(cb * decay, xh, 1, 0)
            s_new = s_new + _bdot(bm * jnp.exp(_col(a_end, h) - acs), xh, 0, 0)
        ys.append((y + dfs[j] * xs[j]) * _silu(zs[j]))
        nexts.append(s_new)
    ssq = sum(jnp.sum(y * y, axis=-1, keepdims=True) for y in ys)
    scale = lax.rsqrt(ssq / BW + EPS)
    return [y * scale * nw for y, nw in zip(ys, nws)], nexts


def _blocks(ref, n, width=LANES):
    return [ref[:, j * width:(j + 1) * width] for j in range(n)]


def _ssd_fwd(proj, xa, dt_bias, a_log, dfull, nw, name):
    nc = SEQ // M2_CHUNK

    def body(x_ref, b_ref, c_ref, dt_ref, z_ref, dtb_ref, al_ref, df_ref, nw_ref, o_ref, sin_ref, st):
        @pl.when(pl.program_id(0) == 0)
        def _():
            st[...] = jnp.zeros_like(st)

        states = [st[j] for j in range(N_PAIR)]
        for j in range(N_PAIR):
            sin_ref[0, j] = states[j]
        ys, nexts = _ssd_chunk(_blocks(x_ref, 4), _blocks(b_ref, 2), _blocks(c_ref, 2), dt_ref[...], _blocks(z_ref, 4),
                               states, dtb_ref[...], al_ref[...], _blocks(df_ref, 4), _blocks(nw_ref, 4))
        for j in range(N_PAIR):
            o_ref[:, j * LANES:(j + 1) * LANES] = ys[j].astype(BF16)
            st[j] = nexts[j]

    vec8 = pl.BlockSpec((1, LANES), lambda c: (0, 0))
    vec = pl.BlockSpec((1, BW), lambda c: (0, 0))
    return pl.pallas_call(
        body, name=name, grid=(nc,),
        in_specs=[pl.BlockSpec((M2_CHUNK, BW), lambda c: (c, 0)), pl.BlockSpec((M2_CHUNK, 256), lambda c: (c, 2)),
                  pl.BlockSpec((M2_CHUNK, 256), lambda c: (c, 3)), pl.BlockSpec((M2_CHUNK, LANES), lambda c: (c, C_DT // LANES)),
                  pl.BlockSpec((M2_CHUNK, BW), lambda c: (c, C_M2Z // BW)), vec8, vec8, vec, vec],
        out_specs=[pl.BlockSpec((M2_CHUNK, BW), lambda c: (c, 0)),
                   pl.BlockSpec((1, N_PAIR, M2_STATE, LANES), lambda c: (c, 0, 0, 0))],
        out_shape=[jax.ShapeDtypeStruct((SEQ, BW), BF16), jax.ShapeDtypeStruct((nc, N_PAIR, M2_STATE, LANES), F32)],
        scratch_shapes=[pltpu.VMEM((N_PAIR, M2_STATE, LANES), F32)],
        compiler_params=_cparams(("arbitrary",)),
    )(xa, xa, xa, proj, proj, dt_bias, a_log, dfull, nw)


def _ssd_bwd(proj, dproj, xa, dout, s_in, dt_bias, a_log, dfull, nw, name):
    nc = SEQ // M2_CHUNK

    def body(x_ref, b_ref, c_ref, dt_ref, z_ref, do_ref, sin_ref, dtb_ref, al_ref, df_ref, nw_ref, dproj_in,
             dp_ref, dxa_ref, ddtb_ref, dal_ref, ddf_ref, dnw_ref, dst):
        @pl.when(pl.program_id(0) == 0)
        def _():
            dst[...] = jnp.zeros_like(dst)
            for r in (ddtb_ref, dal_ref, ddf_ref, dnw_ref):
                r[...] = jnp.zeros_like(r)

        states = [sin_ref[0, j] for j in range(N_PAIR)]
        _, vjp = jax.vjp(_ssd_chunk, _blocks(x_ref, 4), _blocks(b_ref, 2), _blocks(c_ref, 2), dt_ref[...],
                         _blocks(z_ref, 4), states, dtb_ref[...], al_ref[...], _blocks(df_ref, 4), _blocks(nw_ref, 4))
        dxs, dbs, dcs, ddt, dzs, dstates, ddtb, dal, ddfs, dnws = vjp(
            ([do_ref[0, :, j * LANES:(j + 1) * LANES] for j in range(N_PAIR)], [dst[j] for j in range(N_PAIR)]))
        for j in range(N_PAIR):
            sl = slice(j * LANES, (j + 1) * LANES)
            dxa_ref[:, sl] = dxs[j]
            dp_ref[:, sl] = dzs[j].astype(BF16)
            dst[j] = dstates[j]
            ddf_ref[:, sl] += ddfs[j]
            dnw_ref[:, sl] += dnws[j]
        for g in range(2):
            dxa_ref[:, BW + g * LANES:BW + (g + 1) * LANES] = dbs[g]
            dxa_ref[:, BW + 256 + g * LANES:BW + 256 + (g + 1) * LANES] = dcs[g]
        dp_ref[:, BW:BW + LANES] = ddt.astype(BF16)
        dp_ref[:, BW + LANES:] = jnp.zeros((M2_CHUNK, 2 * BW - BW - LANES), BF16)
        ddtb_ref[...] += ddtb
        dal_ref[...] += dal

    rev = lambda w, col=0: pl.BlockSpec((M2_CHUNK, w), lambda i: (nc - 1 - i, col))
    vec8 = pl.BlockSpec((1, LANES), lambda i: (0, 0))
    vec = pl.BlockSpec((1, BW), lambda i: (0, 0))
    return pl.pallas_call(
        body, name=name, grid=(nc,),
        in_specs=[rev(BW), rev(256, 2), rev(256, 3), rev(LANES, C_DT // LANES), rev(BW, C_M2Z // BW),
                  pl.BlockSpec((1, M2_CHUNK, BW), lambda i: (2, nc - 1 - i, 0)),
                  pl.BlockSpec((1, N_PAIR, M2_STATE, LANES), lambda i: (nc - 1 - i, 0, 0, 0)), vec8, vec8, vec, vec,
                  pl.BlockSpec(memory_space=pl.ANY)],
        out_specs=[rev(2 * BW, C_M2Z // (2 * BW)), rev(M2_CONV_CH), vec8, vec8, vec, vec],
        input_output_aliases={11: 0},
        out_shape=[jax.ShapeDtypeStruct((SEQ, IN_PAD), BF16), jax.ShapeDtypeStruct((SEQ, M2_CONV_CH), F32),
                   jax.ShapeDtypeStruct((1, LANES), F32), jax.ShapeDtypeStruct((1, LANES), F32),
                   jax.ShapeDtypeStruct((1, BW), F32), jax.ShapeDtypeStruct((1, BW), F32)],
        scratch_shapes=[pltpu.VMEM((N_PAIR, M2_STATE, LANES), F32)],
        compiler_params=_cparams(("arbitrary",)),
    )(xa, xa, xa, proj, proj, dout, s_in, dt_bias, a_log, dfull, nw, dproj)


def _sc_specs():
    col = lambda kind: pl.BlockSpec((SEQ, LANES), lambda j: (0, C_SC // LANES + 4 * j + kind))
    return [col(0), col(1), col(2), col(3)]


def _sc_fwd(proj, w, name):
    def body(b_ref, c_ref, h_ref, g_ref, w_ref, o_ref):
        ch = c_ref[...] * h_ref[...]
        acc = jnp.zeros_like(ch)
        for k in range(SC_CONV):
            acc = acc + w_ref[k:k + 1, :] * _shift_down(ch, SC_CONV - 1 - k)
        o_ref[...] = (b_ref[...] * acc * _silu(g_ref[...])).astype(BF16)

    return pl.pallas_call(
        body, name=name, grid=(BW // LANES,),
        in_specs=_sc_specs() + [pl.BlockSpec((SC_CONV, LANES), lambda j: (0, j))],
        out_specs=pl.BlockSpec((SEQ, LANES), lambda j: (0, j)),
        out_shape=jax.ShapeDtypeStruct((SEQ, BW), BF16),
        compiler_params=_cparams(("parallel",)),
    )(proj, proj, proj, proj, w)


def _sc_bwd(proj, dproj, dout, w, name):
    def body(b_ref, c_ref, h_ref, g_ref, do_ref, w_ref, dproj_in, dp_ref, dw_ref):
        cv, hv, gv = c_ref[...], h_ref[...], g_ref[...]
        ch = cv * hv
        chs = [_shift_down(ch, SC_CONV - 1 - k) for k in range(SC_CONV)]
        acc = jnp.zeros_like(ch)
        for k in range(SC_CONV):
            acc = acc + w_ref[k:k + 1, :] * chs[k]
        sg = jax.nn.sigmoid(gv)
        do = do_ref[0]
        bv = b_ref[...]
        dp_ref[:, 0:LANES] = (do * acc * (gv * sg)).astype(BF16)
        dp_ref[:, 3 * LANES:] = (do * bv * acc * (sg * (1.0 + gv * (1.0 - sg)))).astype(BF16)
        dacc = do * bv * (gv * sg)
        dch = jnp.zeros_like(ch)
        for k in range(SC_CONV):
            dch = dch + w_ref[k:k + 1, :] * _shift_up(dacc, SC_CONV - 1 - k)
            dw_ref[k:k + 1, :] = jnp.sum(dacc * chs[k], axis=0, keepdims=True)
        dp_ref[:, LANES:2 * LANES] = (dch * hv).astype(BF16)
        dp_ref[:, 2 * LANES:3 * LANES] = (dch * cv).astype(BF16)

    wsp = pl.BlockSpec((SC_CONV, LANES), lambda j: (0, j))
    return pl.pallas_call(
        body, name=name, grid=(BW // LANES,),
        in_specs=_sc_specs() + [pl.BlockSpec((1, SEQ, LANES), lambda j: (3, 0, j)), wsp, pl.BlockSpec(memory_space=pl.ANY)],
        out_specs=[pl.BlockSpec((SEQ, 4 * LANES), lambda j: (0, C_SC // (4 * LANES) + j)), wsp],
        input_output_aliases={6: 0},
        out_shape=[jax.ShapeDtypeStruct((SEQ, IN_PAD), BF16), jax.ShapeDtypeStruct((SC_CONV, BW), F32)],
        compiler_params=_cparams(("parallel",)),
    )(proj, proj, proj, proj, dout, w, dproj)


MERGE_T = 512
MERGE_BWD_T = 1024


def _merge_fwd(proj, ys, merge_b, w_branch, name):
    def body(y_ref, lg_ref, b_ref, w_ref, o_ref):
        acc = jnp.zeros((MERGE_T, D_MODEL), F32)
        for k in range(N_BRANCH):
            gate = jax.nn.sigmoid(lg_ref[:, k * D_MODEL:(k + 1) * D_MODEL] + b_ref[k])
            acc = acc + gate * _dg(y_ref[k], w_ref[k], 1, 0)
        o_ref[...] = acc.astype(BF16)

    return pl.pallas_call(
        body, name=name, grid=(SEQ // MERGE_T,),
        in_specs=[pl.BlockSpec((N_BRANCH, MERGE_T, BW), lambda i: (0, i, 0)),
                  pl.BlockSpec((MERGE_T, N_BRANCH * D_MODEL), lambda i: (i, C_MERGE // (N_BRANCH * D_MODEL))),
                  pl.BlockSpec((N_BRANCH, 1, D_MODEL), lambda i: (0, 0, 0)),
                  pl.BlockSpec((N_BRANCH, BW, D_MODEL), lambda i: (0, 0, 0))],
        out_specs=pl.BlockSpec((MERGE_T, D_MODEL), lambda i: (i, 0)),
        out_shape=jax.ShapeDtypeStruct((SEQ, D_MODEL), BF16),
        compiler_params=_cparams(("parallel",)),
    )(ys, proj, merge_b, w_branch)


def _merge_bwd(proj, ys, dm, merge_b, w_branch, name):
    nt = SEQ // MERGE_BWD_T

    def body(y_ref, lg_ref, dm_ref, b_ref, w_ref, dy_ref, dlg_ref, dw_ref, db_ref, dw_acc):
        i = pl.program_id(1)
        gate = jax.nn.sigmoid(lg_ref[...] + b_ref[0])
        y = y_ref[0]
        dmv = dm_ref[...]
        dbo = (gate * dmv).astype(BF16)
        dlg = _dg(y, w_ref[0], 1, 0) * dmv * gate * (1.0 - gate)
        dlg_ref[...] = dlg.astype(BF16)
        dy_ref[0] = _dg(dbo, w_ref[0], 1, 1)
        dwp = _dg(y, dbo, 0, 0)
        dbp = jnp.sum(dlg, axis=0, keepdims=True)

        @pl.when(i == 0)
        def _():
            dw_acc[...] = dwp
            db_ref[0] = dbp

        @pl.when(i > 0)
        def _():
            dw_acc[...] += dwp
            db_ref[0] += dbp

        @pl.when(i == nt - 1)
        def _():
            dw_ref[0] = dw_acc[...].astype(BF16)

    return pl.pallas_call(
        body, name=name, grid=(N_BRANCH, nt),
        in_specs=[pl.BlockSpec((1, MERGE_BWD_T, BW), lambda k, i: (k, i, 0)),
                  pl.BlockSpec((MERGE_BWD_T, D_MODEL), lambda k, i: (i, C_MERGE // D_MODEL + k)),
                  pl.BlockSpec((MERGE_BWD_T, D_MODEL), lambda k, i: (i, 0)),
                  pl.BlockSpec((1, 1, D_MODEL), lambda k, i: (k, 0, 0)),
                  pl.BlockSpec((1, BW, D_MODEL), lambda k, i: (k, 0, 0))],
        out_specs=[pl.BlockSpec((1, MERGE_BWD_T, BW), lambda k, i: (k, i, 0)),
                   pl.BlockSpec((MERGE_BWD_T, D_MODEL), lambda k, i: (i, k)),
                   pl.BlockSpec((1, BW, D_MODEL), lambda k, i: (k, 0, 0)),
                   pl.BlockSpec((1, 1, D_MODEL), lambda k, i: (k, 0, 0))],
        out_shape=[jax.ShapeDtypeStruct((N_BRANCH, SEQ, BW), F32), jax.ShapeDtypeStruct((SEQ, IN_PAD), BF16),
                   jax.ShapeDtypeStruct((N_BRANCH, BW, D_MODEL), BF16), jax.ShapeDtypeStruct((N_BRANCH, 1, D_MODEL), F32)],
        scratch_shapes=[pltpu.VMEM((BW, D_MODEL), F32)],
        compiler_params=_cparams(("parallel", "arbitrary")),
    )(ys, proj, dm, merge_b, w_branch)


def _adamw(glist, w, m, v, rows, name):
    nl = len(glist)
    n, r, c = glist[0].shape
    assert w.shape == (nl, r, c) and r % rows == 0
    nb = r // rows

    def body(*refs):
        g_refs = refs[:nl]
        w_ref, m_ref, v_ref, go_ref, d_ref, mo_ref, vo_ref = refs[nl:]
        for layer in range(nl):
            @pl.when(pl.program_id(0) == layer)
            def _(g_ref=g_refs[layer]):
                g = g_ref[0].astype(F32)
                for s in range(1, n):
                    g = g + g_ref[s].astype(F32)
                mn = ADAM_B1 * m_ref[0] + (1.0 - ADAM_B1) * g
                vn = ADAM_B2 * v_ref[0] + (1.0 - ADAM_B2) * jnp.square(g)
                m_hat = mn / (1.0 - ADAM_B1 ** ADAM_STEP)
                v_hat = vn / (1.0 - ADAM_B2 ** ADAM_STEP)
                go_ref[0] = g
                d_ref[0] = -ADAM_LR * (m_hat / (jnp.sqrt(v_hat) + ADAM_EPS) + ADAM_WD * w_ref[0])
                mo_ref[0] = mn
                vo_ref[0] = vn

    def g_spec(layer):
        return pl.BlockSpec((n, rows, c), lambda a, i: (0, jnp.where(a < layer, 0, jnp.where(a == layer, i, nb - 1)), 0))

    blk = pl.BlockSpec((1, rows, c), lambda a, i: (a, i, 0))
    out = jax.ShapeDtypeStruct((nl, r, c), F32)
    return pl.pallas_call(
        body, name=name, grid=(nl, nb),
        in_specs=[g_spec(layer) for layer in range(nl)] + [blk, blk, blk],
        out_specs=[blk, blk, blk, blk], out_shape=[out, out, out, out],
        compiler_params=_cparams(("arbitrary", "arbitrary")),
    )(*glist, w, m, v)


X_ROWS_PER_COL = 2 * (D_MODEL // LANES)


def _w_in_to_x(w):
    t = jnp.transpose(w, (2, 0, 1)).reshape(SHARD_IN, DEPTH, D_MODEL // LANES, LANES)
    return jnp.transpose(t, (0, 2, 1, 3)).reshape(SHARD_IN * X_ROWS_PER_COL, LANES)


def _w_in_from_x(xv):
    t = jnp.transpose(xv.reshape(SHARD_IN, D_MODEL // LANES, DEPTH, LANES), (0, 2, 1, 3))
    return jnp.transpose(t.reshape(SHARD_IN, DEPTH, D_MODEL), (1, 2, 0))


def _adamw_w_in(glist, w, m, v, name, after=None):
    n = glist[0].shape[0]
    cols = LANES
    rows = cols * X_ROWS_PER_COL
    extra = [] if after is None else [after]

    def body(g0_ref, g1_ref, w_ref, m_ref, v_ref, *rest):
        go_ref, d_ref, mo_ref, vo_ref = rest[len(extra):]
        for layer, g_ref in enumerate((g0_ref, g1_ref)):
            g = g_ref[0].astype(F32)
            for s in range(1, n):
                g = g + g_ref[s].astype(F32)
            gt = g.T
            for t in range(D_MODEL // LANES):
                sel = (pl.ds(2 * t + layer, cols, stride=X_ROWS_PER_COL), slice(None))
                gs = gt[:, t * LANES:(t + 1) * LANES]
                mn = ADAM_B1 * m_ref[sel] + (1.0 - ADAM_B1) * gs
                vn = ADAM_B2 * v_ref[sel] + (1.0 - ADAM_B2) * jnp.square(gs)
                m_hat = mn / (1.0 - ADAM_B1 ** ADAM_STEP)
                v_hat = vn / (1.0 - ADAM_B2 ** ADAM_STEP)
                go_ref[sel] = gs
                d_ref[sel] = -ADAM_LR * (m_hat / (jnp.sqrt(v_hat) + ADAM_EPS) + ADAM_WD * w_ref[sel])
                mo_ref[sel] = mn
                vo_ref[sel] = vn

    g_spec = pl.BlockSpec((n, D_MODEL, cols), lambda i: (0, 0, i))
    blk = pl.BlockSpec((rows, LANES), lambda i: (i, 0))
    out = jax.ShapeDtypeStruct((SHARD_IN * X_ROWS_PER_COL, LANES), F32)
    res = pl.pallas_call(
        body, name=name, grid=(-(-SHARD_IN // cols),),
        in_specs=[g_spec, g_spec, blk, blk, blk] + [pl.BlockSpec(memory_space=pl.ANY)] * len(extra),
        out_specs=[blk, blk, blk, blk], out_shape=[out, out, out, out],
        compiler_params=_cparams(("parallel",)),
    )(*glist, _w_in_to_x(w), _w_in_to_x(m), _w_in_to_x(v), *extra)
    return [_w_in_from_x(o) for o in res]


def _adamw_many(gs, ws, ms, vs, name):
    k = len(gs)

    def body(*refs):
        g_refs, w_refs, m_refs, v_refs = refs[:k], refs[k:2 * k], refs[2 * k:3 * k], refs[3 * k:4 * k]
        d_refs, mo_refs, vo_refs = refs[4 * k:5 * k], refs[5 * k:6 * k], refs[6 * k:7 * k]
        for i in range(k):
            g = g_refs[i][...]
            mn = ADAM_B1 * m_refs[i][...] + (1.0 - ADAM_B1) * g
            vn = ADAM_B2 * v_refs[i][...] + (1.0 - ADAM_B2) * jnp.square(g)
            m_hat = mn / (1.0 - ADAM_B1 ** ADAM_STEP)
            v_hat = vn / (1.0 - ADAM_B2 ** ADAM_STEP)
            d_refs[i][...] = -ADAM_LR * (m_hat / (jnp.sqrt(v_hat) + ADAM_EPS) + ADAM_WD * w_refs[i][...])
            mo_refs[i][...] = mn
            vo_refs[i][...] = vn

    whole = pl.BlockSpec(memory_space=pltpu.VMEM)
    shapes = [jax.ShapeDtypeStruct(w.shape, F32) for w in ws]
    outs = pl.pallas_call(
        body, name=name, in_specs=[whole] * (4 * k), out_specs=[whole] * (3 * k), out_shape=shapes * 3,
        compiler_params=_cparams(None),
    )(*gs, *ws, *ms, *vs)
    return outs[:k], outs[k:2 * k], outs[2 * k:]


MEMORY_ORDER = {'s5_b_re': (0, 1, 3, 2), 's5_b_im': (0, 1, 3, 2), 's5_d': (0, 2, 1), 'sc_conv_w': (1, 0, 2)}


def _memory_view(name, t):
    return jnp.transpose(t, MEMORY_ORDER[name]) if name in MEMORY_ORDER else t


def _slot_sum(gslots, name):
    n, r, c = gslots.shape

    def body(g_ref, o_ref):
        g = g_ref[0]
        for s in range(1, n):
            g = g + g_ref[s]
        o_ref[...] = g

    return pl.pallas_call(
        body, name=name, in_specs=[pl.BlockSpec((n, r, c), lambda: (0, 0, 0))],
        out_specs=pl.BlockSpec((r, c), lambda: (0, 0)), out_shape=jax.ShapeDtypeStruct((r, c), F32),
        compiler_params=_cparams(None),
    )(gslots)


def _me_and_peers():
    x, y, c = lax.axis_index("x"), lax.axis_index("y"), lax.axis_index("c")
    me = 4 * x + 2 * y + c
    peers = []
    for k in range(1, N_DEV):
        px = 1 - x if (k >> 2) & 1 else x
        py = 1 - y if (k >> 1) & 1 else y
        pc = 1 - c if k & 1 else c
        peers.append((4 * px + 2 * py + pc, (px, py, pc)))
    return me, peers


_HBM = pl.BlockSpec(memory_space=pltpu.HBM)
_SEM = pl.BlockSpec(memory_space=pltpu.SEMAPHORE)
_EFFECT = pltpu.SideEffectType.DATAFLOW_SIDE_EFFECTING


N_CHIP = N_DEV // 2


def _chip_peers():
    x, y, c = lax.axis_index("x"), lax.axis_index("y"), lax.axis_index("c")
    chips = []
    for d in range(1, N_CHIP):
        px = 1 - x if (d >> 1) & 1 else x
        py = 1 - y if d & 1 else y
        chips.append((2 * px + py, (px, py)))
    return (x, y, c), 2 * x + y, chips


def _plan_direct(ins, lands, send_sems, recv_sems, local_sems, gather):
    me, peers = _me_and_peers()
    plan = dict(start=[], local=[], sends=[], recvs=[])
    for t in range(len(ins)):
        own = pltpu.make_async_copy(ins[t] if gather else ins[t].at[me], lands[t].at[me], local_sems.at[t])
        plan['start'].append(own)
        plan['local'].append(own)
        for k, (pidx, pos) in enumerate(peers):
            cp = pltpu.make_async_remote_copy(
                src_ref=ins[t] if gather else ins[t].at[pidx], dst_ref=lands[t].at[me],
                send_sem=send_sems.at[t * (N_DEV - 1) + k], recv_sem=recv_sems.at[t * (N_DEV - 1) + k],
                device_id=pos, device_id_type=MESH)
            plan['start'].append(cp)
            plan['sends'].append(cp)
            plan['recvs'].append(cp)
    return plan


def _plan_gather(ins, lands, send_sems, recv_sems, local_sems, first=0):
    (x, y, c), q, chips = _chip_peers()
    me = 2 * q + c
    plan = dict(start=[], relay_wait=[], relay_start=[], local=[], sends=[], recvs=[])
    for t in range(len(ins)):
        base = (first + t) * 7
        sem = lambda k: dict(send_sem=send_sems.at[base + k], recv_sem=recv_sems.at[base + k], device_id_type=MESH)
        own = pltpu.make_async_copy(ins[t], lands[t].at[me], local_sems.at[first + t])
        to_sib = pltpu.make_async_remote_copy(src_ref=ins[t], dst_ref=lands[t].at[me], device_id=(x, y, 1 - c), **sem(0))
        plan['start'] += [own, to_sib]
        plan['local'].append(own)
        plan['sends'].append(to_sib)
        plan['recvs'].append(to_sib)
        for d, (pq, (px, py)) in enumerate(chips):
            to_chip = pltpu.make_async_remote_copy(src_ref=ins[t], dst_ref=lands[t].at[me], device_id=(px, py, c), **sem(1 + d))
            blk = lands[t].at[2 * pq + c]
            fwd = pltpu.make_async_remote_copy(src_ref=blk, dst_ref=blk, device_id=(x, y, 1 - c), **sem(4 + d))
            plan['start'].append(to_chip)
            plan['relay_wait'].append(to_chip)
            plan['relay_start'].append(fwd)
            plan['sends'] += [to_chip, fwd]
            plan['recvs'].append(fwd)
    return plan


def _plan_pair(ins, lands, send_sems, recv_sems, local_sems):
    (x, y, c), q, chips = _chip_peers()
    plan = dict(start=[], local=[], sends=[], recvs=[])
    for t in range(len(ins)):
        for k in range(N_CHIP):
            cp = pltpu.make_async_remote_copy(
                src_ref=ins[t].at[2 * k + 1 - c], dst_ref=lands[t].at[k], send_sem=send_sems.at[t * N_CHIP + k],
                recv_sem=recv_sems.at[t * N_CHIP + k], device_id=(x, y, 1 - c), device_id_type=MESH)
            plan['start'].append(cp)
            plan['sends'].append(cp)
            plan['recvs'].append(cp)
    return plan


def _plan_chips(ins, lands, send_sems, recv_sems, local_sems):
    (x, y, c), q, chips = _chip_peers()
    plan = dict(start=[], local=[], sends=[], recvs=[])
    for t in range(len(ins)):
        own = pltpu.make_async_copy(ins[t].at[q], lands[t].at[q], local_sems.at[t])
        plan['start'].append(own)
        plan['local'].append(own)
        for d, (pq, (px, py)) in enumerate(chips):
            cp = pltpu.make_async_remote_copy(
                src_ref=ins[t].at[pq], dst_ref=lands[t].at[q], send_sem=send_sems.at[t * 3 + d],
                recv_sem=recv_sems.at[t * 3 + d], device_id=(px, py, c), device_id_type=MESH)
            plan['start'].append(cp)
            plan['sends'].append(cp)
            plan['recvs'].append(cp)
    return plan


def _split_start(plan_fn, tensors, land_shapes, n_sems, name, after=None):
    n = len(tensors)
    extra = [] if after is None else [after]

    def body(*refs):
        ins, lands = refs[:n], refs[n:2 * n]
        plan = plan_fn(ins, lands, *refs[2 * n + len(extra):2 * n + len(extra) + 3])
        for cp in plan['start']:
            cp.start()
        refs[-1][...] = jnp.zeros_like(refs[-1])

    outs = pl.pallas_call(
        body, name=name,
        out_shape=(pltpu.SemaphoreType.DMA((n_sems,)), pltpu.SemaphoreType.DMA((n_sems,)), pltpu.SemaphoreType.DMA((n,)),
                   *[pltpu.HBM(t.shape, t.dtype) for t in tensors],
                   *[pltpu.HBM(s, t.dtype) for s, t in zip(land_shapes, tensors)],
                   jax.ShapeDtypeStruct((8, LANES), F32)),
        in_specs=[_HBM] * (2 * n) + [pl.BlockSpec(memory_space=pl.ANY)] * len(extra),
        out_specs=(_SEM, _SEM, _SEM, *[_HBM] * (2 * n), pl.BlockSpec(memory_space=pltpu.VMEM)),
        input_output_aliases={t: 3 + t for t in range(2 * n)},
        compiler_params=pltpu.CompilerParams(has_side_effects=_EFFECT),
    )(*[pltpu.with_memory_space_constraint(t, pltpu.HBM) for t in tensors],
      *[pltpu.with_memory_space_constraint(lax.empty(s, t.dtype), pltpu.HBM) for s, t in zip(land_shapes, tensors)], *extra)
    return outs[:-1], outs[-1]


def _split_relay(plan_fn, state, after, name):
    sems, thru = state[:3], state[3:]
    n = len(thru) // 2

    def arrived(*refs):
        plan = plan_fn(refs[:n], refs[n:2 * n], *refs[2 * n:2 * n + 3])
        for cp in plan['relay_wait']:
            cp.wait_recv()

    thru = pl.pallas_call(
        arrived, name=name + "_arrived",
        out_shape=tuple(pltpu.HBM(t.shape, t.dtype) for t in thru),
        in_specs=[_HBM] * (2 * n) + [_SEM, _SEM, _SEM, pl.BlockSpec(memory_space=pl.ANY)],
        out_specs=tuple([_HBM] * (2 * n)),
        input_output_aliases={t: t for t in range(2 * n)},
        compiler_params=pltpu.CompilerParams(has_side_effects=_EFFECT),
    )(*thru, *sems, after)

    def forward(*refs):
        plan = plan_fn(refs[:n], refs[n:2 * n], *refs[2 * n:2 * n + 3])
        for cp in plan['relay_start']:
            cp.start()
        refs[-1][...] = jnp.zeros_like(refs[-1])

    outs = pl.pallas_call(
        forward, name=name + "_forward",
        out_shape=(*[pltpu.HBM(t.shape, t.dtype) for t in thru], jax.ShapeDtypeStruct((8, LANES), F32)),
        in_specs=[_HBM] * (2 * n) + [_SEM, _SEM, _SEM],
        out_specs=(*[_HBM] * (2 * n), pl.BlockSpec(memory_space=pltpu.VMEM)),
        input_output_aliases={t: t for t in range(2 * n)},
        compiler_params=pltpu.CompilerParams(has_side_effects=_EFFECT),
    )(*thru, *sems)
    return (*sems, *outs[:-1]), outs[-1]


def _split_wait(plan_fn, state, after, name, with_sources=False):
    sems, thru = state[:3], state[3:]
    n = len(thru) // 2

    def body(*refs):
        plan = plan_fn(refs[:n], refs[n:2 * n], *refs[2 * n:2 * n + 3])
        for cp in plan['local']:
            cp.wait()
        for cp in plan['sends']:
            cp.wait_send()
        for cp in plan['recvs']:
            cp.wait_recv()

    outs = pl.pallas_call(
        body, name=name,
        out_shape=tuple(pltpu.HBM(t.shape, t.dtype) for t in thru),
        in_specs=[_HBM] * (2 * n) + [_SEM, _SEM, _SEM, pl.BlockSpec(memory_space=pl.ANY)],
        out_specs=tuple([_HBM] * (2 * n)),
        input_output_aliases={t: t for t in range(2 * n)},
        compiler_params=pltpu.CompilerParams(has_side_effects=_EFFECT),
    )(*thru, *sems, after)
    return (list(outs[:n]), list(outs[n:])) if with_sources else list(outs[n:])


PAIR_SUM_BLOCK = 768 * 1024


def _pair_sum(mine, theirs, name):
    _, r, c = mine.shape
    rows = r
    while rows * c > PAIR_SUM_BLOCK and rows % 32 == 0:
        rows //= 2

    def body(core_ref, a_ref, b_ref, o_ref):
        o_ref[0] = (a_ref[0].astype(F32) + b_ref[0].astype(F32)).astype(o_ref.dtype)

    return pl.pallas_call(
        body, name=name,
        grid_spec=pltpu.PrefetchScalarGridSpec(
            num_scalar_prefetch=1, grid=(N_CHIP, r // rows),
            in_specs=[pl.BlockSpec((1, rows, c), lambda k, i, core: (2 * k + core[0], i, 0)),
                      pl.BlockSpec((1, rows, c), lambda k, i, core: (k, i, 0))],
            out_specs=pl.BlockSpec((1, rows, c), lambda k, i, core: (k, i, 0))),
        out_shape=jax.ShapeDtypeStruct((N_CHIP, r, c), mine.dtype),
        compiler_params=_cparams(("parallel", "parallel")),
    )(lax.axis_index("c").astype(jnp.int32).reshape(1), mine, theirs)


WEIGHTS = ['norm_w', 'w_in', 's5_lambda_re', 's5_lambda_im', 's5_b_re', 's5_b_im', 's5_c_re', 's5_c_im', 's5_d',
           's5_log_step', 's5_w_glu', 'sgu_ln_w', 'sgu_ln_b', 'sgu_w', 'sgu_b', 'm2_conv_w', 'm2_conv_b', 'm2_dt_bias',
           'm2_a_log', 'm2_d', 'm2_norm_w', 'sc_conv_w', 'merge_b', 'w_branch', 'w_out', 'final_norm_w']
BIG_SHARDED = ['w_in', 'w_branch', 'w_out', 's5_w_glu']
SMALL_SHARDED = ['m2_conv_w', 'sc_conv_w', 'merge_b']
REPLICATED = [n for n in WEIGHTS if n not in BIG_SHARDED + SMALL_SHARDED]
S5_NAMES = ['s5_lambda_re', 's5_lambda_im', 's5_b_re', 's5_b_im', 's5_c_re', 's5_c_im', 's5_d', 's5_log_step']


def _sc_interleave(t):
    lead = t.shape[:-1]
    return jnp.swapaxes(t.reshape(lead + (4, 4, LANES)), -3, -2).reshape(lead + (4 * BW,))


def _pad_in(w):
    z = lambda n: jnp.zeros(w.shape[:-1] + (n,), w.dtype)
    return jnp.concatenate([w[..., 6152:], w[..., 0:1024], w[..., 3072:4096], w[..., 1024:2560], z(512),
                            w[..., 2560:3072], w[..., 4096:4104], z(504), _sc_interleave(w[..., 4104:6152])], axis=-1)


def _unpad_in(g):
    return jnp.concatenate([g[..., C_S5U:C_S5U + 1024], g[..., C_SGU_U:C_SGU_U + 1536], g[..., C_M2Z:C_M2Z + 512],
                            g[..., C_M2X:C_M2X + 1024], g[..., C_DT:C_DT + 8], _sc_interleave(g[..., C_SC:]),
                            g[..., :N_BRANCH * D_MODEL]], axis=-1)


ROW_BLOCK = 8 * LANES


def _pack_rows(tensors, row_mult, batched=False):
    parts = []
    for t in tensors:
        f = t.reshape((t.shape[0], -1) if batched else (1, -1))
        f = jnp.pad(f, ((0, 0), (0, (-f.shape[1]) % ROW_BLOCK)))
        parts.append(f.reshape(f.shape[0], -1, LANES))
    out = jnp.concatenate(parts, axis=1)
    out = jnp.pad(out, ((0, 0), (0, (-out.shape[1]) % row_mult), (0, 0)))
    return out if batched else out[0]


def _unpack_rows(rows, shapes):
    out, r0 = [], 0
    for shp in shapes:
        size = 1
        for s in shp:
            size *= s
        nr = -(-size // ROW_BLOCK) * 8
        out.append(rows[r0:r0 + nr].reshape(-1)[:size].reshape(shp))
        r0 += nr
    return out


def _kernel_col_map():
    m = np.full(IN_PAD, -1, np.int64)
    m[C_MERGE:C_MERGE + 4096] = np.arange(6152, 10248)
    m[C_S5U:C_S5U + 1024] = np.arange(0, 1024)
    m[C_M2X:C_M2X + 1024] = np.arange(3072, 4096)
    m[C_SGU_U:C_SGU_U + 1536] = np.arange(1024, 2560)
    m[C_M2Z:C_M2Z + 512] = np.arange(2560, 3072)
    m[C_DT:C_DT + 8] = np.arange(4096, 4104)
    for j in range(4):
        for kind in range(4):
            k0 = C_SC + 4 * LANES * j + LANES * kind
            m[k0:k0 + LANES] = 4104 + BW * kind + LANES * j + np.arange(LANES)
    return m


def _lane_pieces(sources):
    pieces, cur = [], None
    for lane, src in enumerate(sources):
        key = None if src is None else (src[0], src[1] // LANES, (lane - src[1]) % LANES)
        if cur is not None and key == cur[0]:
            cur[2] = lane + 1
        else:
            if cur is not None and cur[0] is not None:
                pieces.append((*cur[0], cur[1], cur[2]))
            cur = [key, lane, lane + 1]
    if cur is not None and cur[0] is not None:
        pieces.append((*cur[0], cur[1], cur[2]))
    return pieces


def _assemble_block(pieces, load, rows, dtype):
    lane = lax.broadcasted_iota(jnp.int32, (rows, LANES), 1)
    out = None
    for arr, sb, shift, lo, hi in pieces:
        v = load(arr, sb)
        if shift:
            v = pltpu.roll(v, shift, 1)
        if out is None and lo == 0 and hi == LANES:
            out = v
        else:
            out = jnp.where((lane >= lo) & (lane < hi), v, jnp.zeros((rows, LANES), dtype) if out is None else out)
    return jnp.zeros((rows, LANES), dtype) if out is None else out


RELAYOUT_ROWS = 256
SHARD_BLOCKS = -(-SHARD_IN // LANES)


def _load_shard_block(ref, rows):
    def load(j, sb):
        if sb == SHARD_BLOCKS - 1:
            return jnp.broadcast_to(ref[j, :, SHARD_IN - 1:SHARD_IN], (rows, LANES))
        return ref[j, :, sb * LANES:(sb + 1) * LANES]
    return load


def _relayout_w_in(gathered, name):
    kmap = _kernel_col_map()
    dtype = gathered.dtype

    def body(src_ref, o_ref):
        load = _load_shard_block(src_ref, RELAYOUT_ROWS)
        for ob in range(IN_PAD // LANES):
            srcs = [None if kmap[ob * LANES + l] < 0 else (int(kmap[ob * LANES + l]) // SHARD_IN, int(kmap[ob * LANES + l]) % SHARD_IN)
                    for l in range(LANES)]
            o_ref[:, ob * LANES:(ob + 1) * LANES] = _assemble_block(_lane_pieces(srcs), load, RELAYOUT_ROWS, dtype)

    return pl.pallas_call(
        body, name=name, grid=(D_MODEL // RELAYOUT_ROWS,),
        in_specs=[pl.BlockSpec((N_DEV, RELAYOUT_ROWS, SHARD_IN), lambda i: (0, i, 0))],
        out_specs=pl.BlockSpec((RELAYOUT_ROWS, IN_PAD), lambda i: (i, 0)),
        out_shape=jax.ShapeDtypeStruct((D_MODEL, IN_PAD), dtype),
        compiler_params=_cparams(("parallel",)),
    )(gathered)


def _relayout_g_in(gw, name):
    kmap = _kernel_col_map()
    kinv = np.zeros(IN_DIM, np.int64)
    kinv[kmap[kmap >= 0]] = np.nonzero(kmap >= 0)[0]
    dtype = gw.dtype

    def body(src_ref, o_ref):
        load = lambda _, sb: src_ref[:, sb * LANES:(sb + 1) * LANES]
        for j in range(N_DEV):
            for ob in range(SHARD_BLOCKS):
                srcs = [(0, int(kinv[SHARD_IN * j + ob * LANES + l])) if ob * LANES + l < SHARD_IN else None for l in range(LANES)]
                blk = _assemble_block(_lane_pieces(srcs), load, RELAYOUT_ROWS, dtype)
                if ob == SHARD_BLOCKS - 1:
                    o_ref[j, :, SHARD_IN - 1:SHARD_IN] = blk[:, 0:1]
                else:
                    o_ref[j, :, ob * LANES:(ob + 1) * LANES] = blk

    return pl.pallas_call(
        body, name=name, grid=(D_MODEL // RELAYOUT_ROWS,),
        in_specs=[pl.BlockSpec((RELAYOUT_ROWS, IN_PAD), lambda i: (i, 0))],
        out_specs=pl.BlockSpec((N_DEV, RELAYOUT_ROWS, SHARD_IN), lambda i: (0, i, 0)),
        out_shape=jax.ShapeDtypeStruct((N_DEV, D_MODEL, SHARD_IN), dtype),
        compiler_params=_cparams(("parallel",)),
    )(gw)


def _rows128(flat, row_mult=8):
    n = flat.shape[0]
    per = LANES * row_mult
    total = -(-n // per) * per
    return jnp.pad(flat, (0, total - n)).reshape(total // LANES, LANES)


def _pad_lanes(v):
    return jnp.pad(v, (0, LANES - v.shape[0])).reshape(1, LANES)


def _layer_prep(i, p):
    disc, disc_vjp = jax.vjp(_s5_disc, *[p[n][i] for n in S5_NAMES])
    prep = dict(
        nw=p['norm_w'][i].reshape(1, D_MODEL), disc_vjp=disc_vjp,
        s5small=[_block_diag(t).astype(BF16) for t in disc[:4]] + [disc[4], disc[5]],
        sgw=[p['sgu_ln_w'][i].reshape(1, BW), p['sgu_ln_b'][i].reshape(1, BW), p['sgu_w'][i],
             jnp.repeat(p['sgu_b'][i].T, BW // SGU_HEADS, axis=1)],
        cb=p['m2_conv_b'][i].reshape(1, M2_CONV_CH),
        m2w=[_pad_lanes(p['m2_dt_bias'][i]), _pad_lanes(p['m2_a_log'][i]),
             jnp.repeat(p['m2_d'][i], M2_HEAD_DIM).reshape(1, BW), p['m2_norm_w'][i].reshape(1, BW)])
    touch = [t[0, 0].astype(F32) for t in prep['s5small']] + [prep['sgw'][3][0, 0], prep['m2w'][2][0, 0]]
    return prep, sum(touch[1:], touch[0])


def _layer_fwd(x, h, i, prep, w_in, other_weights, before_merge=None):
    proj = _matmul(h, w_in, 1, 0, F32, 1024, 1024, 1024, f"proj{i}")
    full = dict(other_weights(proj), w_in=w_in)
    s5w = prep['s5small'] + [full['s5_w_glu']]
    ya, sre, sim = _s5_fwd(proj, *s5w, f"s5_fwd{i}")
    yb = _sgu_fwd(proj, *prep['sgw'], f"sgu_fwd{i}")
    cw = full['m2_conv_w']
    xa = _m2_conv_fwd(proj, cw, prep['cb'], f"m2conv_fwd{i}")
    yc, s_in = _ssd_fwd(proj, xa, *prep['m2w'], f"ssd_fwd{i}")
    scw = full['sc_conv_w']
    yd = _sc_fwd(proj, scw, f"sc_fwd{i}")
    ys = jnp.stack([ya, yb, yc, yd])
    mb = full['merge_b'].reshape(N_BRANCH, 1, D_MODEL)
    if before_merge is not None:
        mb = mb + before_merge(ys)[0, 0]
    merged = _merge_fwd(proj, ys, mb, full['w_branch'], f"merge_fwd{i}")
    x_new = _matmul(merged, full['w_out'], 1, 0, F32, 512, 1024, 1024, f"out{i}", residual=x)
    saved = dict(x=x, nw=prep['nw'], h=h, proj=proj, disc_vjp=prep['disc_vjp'], s5w=s5w, sre=sre, sim=sim, sgw=prep['sgw'],
                 cw=cw, cb=prep['cb'], xa=xa, m2w=prep['m2w'], s_in=s_in, scw=scw, ys=ys, mb=mb, merged=merged)
    return x_new, saved, full


def _layer_bwd(dx_out, i, sv, full, on_large_grads=None, after_dh=None):
    g = {}
    proj = sv['proj']
    dm = _matmul(dx_out, full['w_out'], 1, 1, F32, 512, 1024, 1024, f"dmerged{i}")
    g['w_out'] = _matmul(sv['merged'], dx_out, 0, 0, BF16, 512, 1024, 1024, f"gw_out{i}")
    dys, dproj, g['w_branch'], dmb = _merge_bwd(proj, sv['ys'], dm, sv['mb'], full['w_branch'], f"merge_bwd{i}")
    g['merge_b'] = dmb.reshape(N_BRANCH, D_MODEL)
    dproj, dbbre, dbbim, dcre, dcim, da, dd, dwg = _s5_bwd(proj, dproj, dys, sv['sre'], sv['sim'], *sv['s5w'], f"s5_bwd{i}")
    g['s5_dense'] = (dbbre, dbbim, dcre, dcim, da, dd)
    g['s5_w_glu'] = dwg.astype(BF16)
    dproj, dlw, dlb, g['sgu_w'], dbias = _sgu_bwd(proj, dproj, dys, *sv['sgw'], f"sgu_bwd{i}")
    g['sgu_ln_w'], g['sgu_ln_b'] = dlw[0], dlb[0]
    g['sgu_b'] = dbias.reshape(SGU_CHUNK, SGU_HEADS, BW // SGU_HEADS).sum(-1).T
    dproj, dxa, ddtb, dal, ddf, dnw = _ssd_bwd(proj, dproj, sv['xa'], dys, sv['s_in'], *sv['m2w'], f"ssd_bwd{i}")
    dproj, g['m2_conv_w'], dcb = _m2_conv_bwd(proj, dproj, dxa, sv['cw'], sv['cb'], f"m2conv_bwd{i}")
    g['m2_conv_b'], g['m2_norm_w'] = dcb[0], dnw[0]
    g['m2_dt_bias'], g['m2_a_log'] = ddtb[0, :M2_HEADS], dal[0, :M2_HEADS]
    g['m2_d'] = ddf.reshape(M2_HEADS, M2_HEAD_DIM).sum(-1)
    dproj, g['sc_conv_w'] = _sc_bwd(proj, dproj, dys, sv['scw'], f"sc_bwd{i}")
    g['w_in'] = _matmul(sv['h'], dproj, 0, 0, BF16, 1024, 1024, 1024, f"gw_in{i}")
    tok = on_large_grads(g) if on_large_grads else None
    dh = _matmul(dproj, full['w_in'], 1, 1, F32, 1024, 1024, 1024, f"dh{i}", after=tok)
    nw = sv['nw'] if after_dh is None else sv['nw'] + after_dh(dh)[0, 0]
    dx_in, dnw_l = _rmsnorm_bwd(sv['x'], nw, dh, dx_out, f"rms_bwd{i}")
    g['norm_w'] = dnw_l[0]
    return dx_in, g


def _split8(t, axis):
    shp = t.shape
    t = t.reshape(shp[:axis] + (N_DEV, shp[axis] // N_DEV) + shp[axis + 1:])
    return jnp.moveaxis(t, axis, 0)


def _join8(t, axis):
    t = jnp.moveaxis(t, 0, axis)
    shp = t.shape
    return t.reshape(shp[:axis] + (shp[axis] * shp[axis + 1],) + shp[axis + 2:])


SHARD_AXIS = {'w_in': 2, 'w_branch': 3, 'w_out': 1, 's5_w_glu': 1, 'm2_conv_w': 2, 'sc_conv_w': 2, 'merge_b': 2}


OTHER_BIG = [n for n in BIG_SHARDED if n != 'w_in']


def _other_weights(gathered):
    return {n: _join8(t, SHARD_AXIS[n] - 1) for n, t in zip(OTHER_BIG, gathered)}


def _layer_grad_blocks(g, i):
    blocks = [_relayout_g_in(g[n], f"relayout_g_in{i}") if n == 'w_in' else _split8(g[n], SHARD_AXIS[n] - 1) for n in BIG_SHARDED]
    return [b.reshape(N_DEV, -1, b.shape[-1]) for b in blocks]


def _pair_start(blocks, tag):
    shapes = [(N_CHIP,) + b.shape[1:] for b in blocks]
    return _split_start(_plan_pair, blocks, shapes, N_CHIP * len(blocks), f"pair{tag}_start")


def _pair_sums(state, after, tag):
    mine, theirs = _split_wait(_plan_pair, state, after, f"pair{tag}_wait", with_sources=True)
    return [_pair_sum(b, t, f"pair_sum{tag}_{k}") for k, (b, t) in enumerate(zip(mine, theirs))]


def _chips_start(sums, tag, after=None):
    return _split_start(_plan_chips, sums, [s.shape for s in sums], 3 * len(sums), f"chips{tag}_start", after)


def kernel(x, norm_w, w_in, s5_lambda_re, s5_lambda_im, s5_b_re, s5_b_im, s5_c_re, s5_c_im, s5_d, s5_log_step, s5_w_glu, sgu_ln_w, sgu_ln_b, sgu_w, sgu_b, m2_conv_w, m2_conv_b, m2_dt_bias, m2_a_log, m2_d, m2_norm_w, sc_conv_w, merge_b, w_branch, w_out, final_norm_w, loss_target, m_norm_w, m_w_in, m_s5_lambda_re, m_s5_lambda_im, m_s5_b_re, m_s5_b_im, m_s5_c_re, m_s5_c_im, m_s5_d, m_s5_log_step, m_s5_w_glu, m_sgu_ln_w, m_sgu_ln_b, m_sgu_w, m_sgu_b, m_m2_conv_w, m_m2_conv_b, m_m2_dt_bias, m_m2_a_log, m_m2_d, m_m2_norm_w, m_sc_conv_w, m_merge_b, m_w_branch, m_w_out, m_final_norm_w, v_norm_w, v_w_in, v_s5_lambda_re, v_s5_lambda_im, v_s5_b_re, v_s5_b_im, v_s5_c_re, v_s5_c_im, v_s5_d, v_s5_log_step, v_s5_w_glu, v_sgu_ln_w, v_sgu_ln_b, v_sgu_w, v_sgu_b, v_m2_conv_w, v_m2_conv_b, v_m2_dt_bias, v_m2_a_log, v_m2_d, v_m2_norm_w, v_sc_conv_w, v_merge_b, v_w_branch, v_w_out, v_final_norm_w):
    loc = locals()
    p = {n: loc[n] for n in WEIGHTS}
    mom = {n: loc['m_' + n] for n in WEIGHTS}
    vel = {n: loc['v_' + n] for n in WEIGHTS}

    small_sizes = [p[n].size for n in SMALL_SHARDED]
    small_pack = _rows128(jnp.concatenate([p[n].reshape(-1) for n in SMALL_SHARDED]))
    first = [p['w_in'][0].astype(BF16)]
    gath_first, tok = _split_start(_plan_gather, first, [(N_DEV,) + first[0].shape], 7, "gather_w_in0_start")
    shards = ([(p[n][0] + tok[0, 0]).astype(BF16) for n in OTHER_BIG] + [small_pack + tok[0, 0]]
              + [(p[n][1] + tok[0, 0]).astype(BF16) for n in BIG_SHARDED])
    gath, tok = _split_start(_plan_gather, shards, [(N_DEV,) + t.shape for t in shards], 7 * len(shards), "gather_start")

    def relayed(lo, hi, after, name, started=None):
        started = gath if started is None else started
        n = (len(started) - 3) // 2
        sems, srcs, lands = started[:3], started[3:3 + n], started[3 + n:]
        plan = functools.partial(_plan_gather, first=lo)
        state, tok = _split_relay(plan, (*sems, *srcs[lo:hi], *lands[lo:hi]), after, name + "_relay")
        return (plan, state, name), tok

    def arrived(relay, after):
        plan, state, name = relay
        return _split_wait(plan, state, after, name + "_wait")

    def gathered(lo, hi, after, name, started=None):
        relay, tok = relayed(lo, hi, after, name, started)
        return arrived(relay, tok)

    later = dict(p, **{n: p[n] + tok[0, 0] for n in ('norm_w', 's5_log_step', 'sgu_b', 'm2_d')})
    preps = [_layer_prep(i, later) for i in range(DEPTH)]
    h0 = _rmsnorm_fwd(x[0], preps[0][0]['nw'], "rms_fwd0")
    got = gathered(0, 1, tok + (preps[0][1] + preps[1][1] + h0[0, 0].astype(F32)), "gather_w_in0", gath_first)
    small_full = {}

    def other_weights0(proj):
        got = gathered(0, 4, proj, "gather_rest0")
        small_all, off = got[-1].reshape(N_DEV, -1), 0
        for n, sz in zip(SMALL_SHARDED, small_sizes):
            small_full[n] = _join8(small_all[:, off:off + sz].reshape((N_DEV,) + p[n].shape), SHARD_AXIS[n])
            off += sz
        return dict(_other_weights(got[:-1]), **{n: small_full[n][0] for n in SMALL_SHARDED})

    saved, layer_g, full = [None] * DEPTH, [None] * DEPTH, [None] * DEPTH
    relay1 = []

    def relay_layer1(ys):
        relay, tok = relayed(4, 8, ys, "gather1")
        relay1.append(relay)
        return tok

    xs, saved[0], full[0] = _layer_fwd(x[0], h0, 0, preps[0][0], _relayout_w_in(got[0], "relayout_w_in0"), other_weights0,
                                       relay_layer1)
    h1 = _rmsnorm_fwd(xs, preps[1][0]['nw'], "rms_fwd1")
    got = arrived(relay1[0], h1)
    xs, saved[1], full[1] = _layer_fwd(
        xs, h1, 1, preps[1][0], _relayout_w_in(got[0], "relayout_w_in1"),
        lambda proj: dict(_other_weights(got[1:]), **{n: small_full[n][1] for n in SMALL_SHARDED}))
    loss_row, dx, dfw = _loss_head(xs, final_norm_w.reshape(1, D_MODEL), loss_target[0])
    loss = lax.psum(loss_row[0, 0], ("x", "y", "c"))
    loss, dx = lax.optimization_barrier((loss, dx))
    pairs, scat, sent0 = [None] * DEPTH, [None] * DEPTH, []

    def start_pairs1(g):
        pairs[1], tok = _pair_start(_layer_grad_blocks(g, 1), 1)
        return tok

    def send_chip_sums1(dh):
        scat[1], tok = _chips_start(_pair_sums(pairs[1], dh, 1), 1)
        return tok

    def send_all0(g):
        pairs[0], tok = _pair_start(_layer_grad_blocks(g, 0), 0)
        scat[0], tok = _chips_start(_pair_sums(pairs[0], tok, 0), 0)
        sent0.append(tok)
        return tok

    dx, layer_g[1] = _layer_bwd(dx, 1, saved[1], full[1], on_large_grads=start_pairs1, after_dh=send_chip_sums1)
    dx, layer_g[0] = _layer_bwd(dx, 0, saved[0], full[0], on_large_grads=send_all0)
    for i in range(DEPTH):
        dense = layer_g[i].pop('s5_dense')
        blocks = tuple(_diag_blocks(t, after=sent0[0]) for t in dense[:4])
        layer_g[i].update(zip(S5_NAMES, saved[i]['disc_vjp'](blocks + (dense[4] + sent0[0][0, 0], dense[5]))))
    grads = {n: jnp.stack([layer_g[i][n] for i in range(DEPTH)]) for n in SMALL_SHARDED + REPLICATED if n != 'final_norm_w'}
    grads['final_norm_w'] = dfw[0]

    out_g, out_d, out_m, out_v = {}, {}, {}, {}
    repl_rows = _pack_rows([grads[n] for n in REPLICATED], 8 * N_DEV)
    rr = repl_rows.shape[0] // N_DEV
    shard_rows = _pack_rows([_split8(grads[n], SHARD_AXIS[n]) for n in SMALL_SHARDED], 8, batched=True)
    rs = shard_rows.shape[1]
    small_g = jnp.concatenate([shard_rows, repl_rows.reshape(N_DEV, rr, LANES)], axis=1)
    all_to_all, all_gather = functools.partial(_plan_direct, gather=False), functools.partial(_plan_direct, gather=True)
    small_state, tok = _split_start(all_to_all, [small_g], [small_g.shape], N_DEV - 1, "scatter_small_start")
    landed1 = _split_wait(_plan_chips, scat[1], tok, "chips1_wait")
    landed0 = _split_wait(_plan_chips, scat[0], landed1[0], "chips0_wait")

    def big_adamw(n, after=None):
        k, shp = BIG_SHARDED.index(n), p[n].shape
        if n == 'w_in':
            return _adamw_w_in([landed0[k], landed1[k]], p[n], mom[n], vel[n], "adamw_w_in", after)
        c = shp[-1]
        r = p[n].size // (DEPTH * c)
        res = _adamw([landed0[k], landed1[k]], *[d[n].reshape(DEPTH, r, c) for d in (p, mom, vel)],
                     {'w_branch': 512, 'w_out': 128, 's5_w_glu': 64}[n], "adamw_" + n)
        return [o.reshape(shp) for o in res]

    for n in OTHER_BIG:
        out_g[n], out_d[n], out_m[n], out_v[n] = big_adamw(n)
    updated = sum(out_d[n].reshape(-1)[0] for n in OTHER_BIG).reshape(1, 1)
    small_sum = _slot_sum(_split_wait(all_to_all, small_state, updated, "scatter_small_wait")[0], "sum_small")
    repl_part = small_sum[rs:]
    repl_state, tok = _split_start(all_gather, [repl_part], [(N_DEV,) + repl_part.shape], N_DEV - 1, "gather_small_start")
    out_g['w_in'], out_d['w_in'], out_m['w_in'], out_v['w_in'] = big_adamw('w_in', tok)
    repl_all = _split_wait(all_gather, repl_state, out_d['w_in'], "gather_small_wait")[0].reshape(N_DEV * rr, LANES)
    g_all = jnp.concatenate([small_sum[:rs], repl_all], axis=0)
    names = SMALL_SHARDED + REPLICATED
    pieces = (_unpack_rows(g_all[:rs], [p[n].shape for n in SMALL_SHARDED])
              + _unpack_rows(g_all[rs:], [p[n].shape for n in REPLICATED]))
    out_g.update(zip(names, pieces))
    res = _adamw_many(*[[_memory_view(n, d[n]) for n in names] for d in (out_g, p, mom, vel)], "adamw_small")
    for r, dst in zip(res, (out_d, out_m, out_v)):
        dst.update({n: _memory_view(n, t) for n, t in zip(names, r)})
    return (loss, dx[None], *[out_g[n] for n in WEIGHTS], *[out_d[n] for n in WEIGHTS],
            *[out_m[n] for n in WEIGHTS], *[out_v[n] for n in WEIGHTS])
```

```python
import functools

import jax
import jax.numpy as jnp
import numpy as np
from jax import lax
from jax.experimental import pallas as pl
from jax.experimental.pallas import tpu as pltpu

F32 = jnp.float32
BF16 = jnp.bfloat16

N_DEV = 8
SEQ = 2048
D_MODEL = 1024
DEPTH = 2
BW = 512
N_BRANCH = 4
EPS = 1e-6
S5_GROUPS, S5_STATE, S5_P = 32, 64, 16
S5_CH = S5_GROUPS * S5_STATE
SGU_CHUNK, SGU_HEADS = 128, 8
M2_HEADS, M2_HEAD_DIM, M2_STATE, M2_CHUNK, M2_CONV = 8, 64, 128, 128, 4
M2_CONV_CH = 1024
SC_CONV = 3
IN_DIM = 10248
IN_PAD = 11264
C_MERGE = 0
C_S5U, C_S5G = 4096, 4608
C_M2X = 5120
C_SGU_U, C_SGU_V, C_SGU_G = 6144, 6656, 7168
C_M2Z, C_DT = 8192, 8704
C_SC = 9216
SHARD_IN = IN_DIM // N_DEV

ADAM_LR, ADAM_B1, ADAM_B2, ADAM_EPS, ADAM_WD, ADAM_STEP = 0.001, 0.9, 0.999, 1e-08, 0.01, 10

VMEM_LIMIT = 56 * 1024 * 1024
LANES = 128

MESH = pl.DeviceIdType.MESH


def _cparams(sem=None, **kw):
    return pltpu.CompilerParams(dimension_semantics=sem, vmem_limit_bytes=VMEM_LIMIT, **kw)


def _dg(a, b, ca, cb, precision=None):
    return lax.dot_general(a, b, (((ca,), (cb,)), ((), ())), precision=precision,
                           preferred_element_type=F32)


@functools.partial(jax.custom_vjp, nondiff_argnums=(2, 3))
def _bdot(a, b, ca, cb):
    return _dg(a.astype(BF16), b.astype(BF16), ca, cb)


def _bdot_fwd(a, b, ca, cb):
    return _bdot(a, b, ca, cb), (a, b)


def _bdot_bwd(ca, cb, res, g):
    a, b = res
    gb, ab, bb = g.astype(BF16), a.astype(BF16), b.astype(BF16)
    da = _dg(gb, bb, 1, 1 - cb) if ca == 1 else _dg(bb, gb, 1 - cb, 1)
    db = _dg(ab, gb, 1 - ca, 0) if cb == 0 else _dg(gb, ab, 0, 1 - ca)
    return da.astype(a.dtype), db.astype(b.dtype)


_bdot.defvjp(_bdot_fwd, _bdot_bwd)


def _rms(x, w):
    return x * lax.rsqrt(jnp.mean(x * x, axis=-1, keepdims=True) + EPS) * w


def _silu(x):
    return x * jax.nn.sigmoid(x)


def _gelu(x):
    return 0.5 * x * (1.0 + jnp.tanh(0.7978845608028654 * (x + 0.044715 * (x * x * x))))


def _softplus(x):
    return jnp.maximum(x, 0.0) + jnp.log1p(jnp.exp(-jnp.abs(x)))


def _shift_down(x, s):
    if s == 0:
        return x
    row = lax.broadcasted_iota(jnp.int32, x.shape, 0)
    return jnp.where(row >= s, pltpu.roll(x, s, 0), 0.0)


def _shift_up(x, s):
    if s == 0:
        return x
    n = x.shape[0]
    row = lax.broadcasted_iota(jnp.int32, x.shape, 0)
    return jnp.where(row < n - s, pltpu.roll(x, n - s, 0), 0.0)


def _matmul(a, b, ca, cb, out_dtype, tm, tn, tk, name, residual=None, after=None):
    m = a.shape[1 - ca]
    k = a.shape[ca]
    n = b.shape[1 - cb]
    assert b.shape[cb] == k and m % tm == 0 and n % tn == 0 and k % tk == 0
    nk = k // tk
    a_spec = pl.BlockSpec((tm, tk), lambda i, j, kk: (i, kk)) if ca == 1 else pl.BlockSpec((tk, tm), lambda i, j, kk: (kk, i))
    b_spec = pl.BlockSpec((tk, tn), lambda i, j, kk: (kk, j)) if cb == 0 else pl.BlockSpec((tn, tk), lambda i, j, kk: (j, kk))
    o_spec = pl.BlockSpec((tm, tn), lambda i, j, kk: (i, j))
    has_res = residual is not None

    def body(*refs):
        refs = refs[:2 + has_res] + refs[2 + has_res + (after is not None):]
        if has_res:
            a_ref, b_ref, r_ref, o_ref, acc = refs
        else:
            a_ref, b_ref, o_ref, acc = refs
        kk = pl.program_id(2)
        part = _dg(a_ref[...].astype(BF16), b_ref[...].astype(BF16), ca, cb)

        @pl.when(kk == 0)
        def _():
            acc[...] = part

        @pl.when(kk > 0)
        def _():
            acc[...] += part

        @pl.when(kk == nk - 1)
        def _():
            r = acc[...]
            if has_res:
                r = r + r_ref[...]
            o_ref[...] = r.astype(out_dtype)

    ins = [a, b] + ([residual] if has_res else []) + ([after] if after is not None else [])
    specs = [a_spec, b_spec] + ([o_spec] if has_res else []) + ([pl.BlockSpec(memory_space=pl.ANY)] if after is not None else [])
    return pl.pallas_call(
        body, name=name, grid=(m // tm, n // tn, nk), in_specs=specs, out_specs=o_spec,
        out_shape=jax.ShapeDtypeStruct((m, n), out_dtype),
        scratch_shapes=[pltpu.VMEM((tm, tn), F32)],
        compiler_params=_cparams(("parallel", "parallel", "arbitrary")),
    )(*ins)


ROW_TILE = 512


def _rmsnorm_fwd(x, w, name):
    def body(x_ref, w_ref, o_ref):
        o_ref[...] = _rms(x_ref[...], w_ref[...]).astype(BF16)

    return pl.pallas_call(
        body, name=name, grid=(SEQ // ROW_TILE,),
        in_specs=[pl.BlockSpec((ROW_TILE, D_MODEL), lambda i: (i, 0)), pl.BlockSpec((1, D_MODEL), lambda i: (0, 0))],
        out_specs=pl.BlockSpec((ROW_TILE, D_MODEL), lambda i: (i, 0)),
        out_shape=jax.ShapeDtypeStruct((SEQ, D_MODEL), BF16),
        compiler_params=_cparams(("parallel",)),
    )(x, w)


def _rmsnorm_bwd(x, w, dh, dres, name):
    def body(x_ref, w_ref, dh_ref, dres_ref, dx_ref, dw_ref):
        _, vjp = jax.vjp(_rms, x_ref[...], w_ref[...])
        dx, dw = vjp(dh_ref[...])
        dx_ref[...] = dx + dres_ref[...]

        @pl.when(pl.program_id(0) == 0)
        def _():
            dw_ref[...] = dw

        @pl.when(pl.program_id(0) > 0)
        def _():
            dw_ref[...] += dw

    tile = pl.BlockSpec((ROW_TILE, D_MODEL), lambda i: (i, 0))
    vec = pl.BlockSpec((1, D_MODEL), lambda i: (0, 0))
    return pl.pallas_call(
        body, name=name, grid=(SEQ // ROW_TILE,),
        in_specs=[tile, vec, tile, tile], out_specs=[tile, vec],
        out_shape=[jax.ShapeDtypeStruct((SEQ, D_MODEL), F32), jax.ShapeDtypeStruct((1, D_MODEL), F32)],
        compiler_params=_cparams(("arbitrary",)),
    )(x, w, dh, dres)


def _loss_head(x, w, target):
    def body(x_ref, w_ref, t_ref, loss_ref, dx_ref, dw_ref):
        tgt = t_ref[...]

        def f(xv, wv):
            err = _rms(xv, wv) - tgt
            return 0.5 * jnp.sum(jnp.mean(err * err, axis=-1))

        loss, vjp = jax.vjp(f, x_ref[...], w_ref[...])
        dx, dw = vjp(jnp.ones((), F32))
        dx_ref[...] = dx
        lrow = jnp.full((1, LANES), loss, F32)

        @pl.when(pl.program_id(0) == 0)
        def _():
            dw_ref[...] = dw
            loss_ref[...] = lrow

        @pl.when(pl.program_id(0) > 0)
        def _():
            dw_ref[...] += dw
            loss_ref[...] += lrow

    tile = pl.BlockSpec((ROW_TILE, D_MODEL), lambda i: (i, 0))
    vec = pl.BlockSpec((1, D_MODEL), lambda i: (0, 0))
    return pl.pallas_call(
        body, name="loss_head", grid=(SEQ // ROW_TILE,),
        in_specs=[tile, vec, tile], out_specs=[pl.BlockSpec((1, LANES), lambda i: (0, 0)), tile, vec],
        out_shape=[jax.ShapeDtypeStruct((1, LANES), F32), jax.ShapeDtypeStruct((SEQ, D_MODEL), F32),
                   jax.ShapeDtypeStruct((1, D_MODEL), F32)],
        compiler_params=_cparams(("arbitrary",)),
    )(x, w, target)


S5_T = 256
S5_BLOCKS = [(slice(j * 256, (j + 1) * 256), slice(j * 1024, (j + 1) * 1024)) for j in range(2)]


def _s5_post(ypre, gate, wglu):
    y = _gelu(ypre)
    y = y * jax.nn.sigmoid(_bdot(y, wglu, 1, 0))
    return y * _silu(gate)


def _s5_fwd(proj, bbre, bbim, cre, cim, a2, dvec, wglu, name):
    def body(u_ref, g_ref, bbre_ref, bbim_ref, cre_ref, cim_ref, a_ref, d_ref, wg_ref, o_ref, sre_ref, sim_ref, st):
        @pl.when(pl.program_id(0) == 0)
        def _():
            st[...] = jnp.zeros_like(st)

        u = u_ref[...]
        ub = u.astype(BF16)
        for us, ss in S5_BLOCKS:
            sre_ref[:, ss] = _dg(ub[:, us], bbre_ref[us, ss], 1, 0)
            sim_ref[:, ss] = _dg(ub[:, us], bbim_ref[us, ss], 1, 0)
        ar, ai = a_ref[0:1, :], a_ref[1:2, :]

        def step(t, carry):
            sr, si = carry
            nr = ar * sr - ai * si + sre_ref[pl.ds(t, 1), :]
            ni = ar * si + ai * sr + sim_ref[pl.ds(t, 1), :]
            sre_ref[pl.ds(t, 1), :] = nr
            sim_ref[pl.ds(t, 1), :] = ni
            return nr, ni

        sr, si = lax.fori_loop(0, S5_T, step, (st[0:1, :], st[1:2, :]), unroll=8)
        st[0:1, :] = sr
        st[1:2, :] = si
        ypre = jnp.concatenate(
            [_dg(sre_ref[:, ss].astype(BF16), cre_ref[ss, us], 1, 0) - _dg(sim_ref[:, ss].astype(BF16), cim_ref[ss, us], 1, 0)
             for us, ss in S5_BLOCKS], axis=1) + d_ref[...] * u
        o_ref[0] = _s5_post(ypre, g_ref[...], wg_ref[...]).astype(BF16)

    full = lambda shape: pl.BlockSpec(shape, lambda c: (0, 0))
    return pl.pallas_call(
        body, name=name, grid=(SEQ // S5_T,),
        in_specs=[pl.BlockSpec((S5_T, BW), lambda c: (c, C_S5U // BW)), pl.BlockSpec((S5_T, BW), lambda c: (c, C_S5G // BW)),
                  full((BW, S5_CH)), full((BW, S5_CH)), full((S5_CH, BW)), full((S5_CH, BW)),
                  full((2, S5_CH)), full((1, BW)), full((BW, BW))],
        out_specs=[pl.BlockSpec((1, S5_T, BW), lambda c: (0, c, 0)), pl.BlockSpec((S5_T, S5_CH), lambda c: (c, 0)),
                   pl.BlockSpec((S5_T, S5_CH), lambda c: (c, 0))],
        out_shape=[jax.ShapeDtypeStruct((N_BRANCH, SEQ, BW), BF16), jax.ShapeDtypeStruct((SEQ, S5_CH), F32),
                   jax.ShapeDtypeStruct((SEQ, S5_CH), F32)],
        scratch_shapes=[pltpu.VMEM((2, S5_CH), F32)],
        compiler_params=_cparams(("arbitrary",)),
    )(proj, proj, bbre, bbim, cre, cim, a2, dvec, wglu)


def _s5_bwd(proj, dproj, dout, sre, sim, bbre, bbim, cre, cim, a2, dvec, wglu, name):
    nc = SEQ // S5_T

    def body(u_ref, g_ref, do_ref, sre_ref, sim_ref, pre_ref, pim_ref, bbre_ref, bbim_ref, cre_ref, cim_ref, a_ref,
             d_ref, wg_ref, dproj_in, dp_ref, dbbre_ref, dbbim_ref, dcre_ref, dcim_ref, da_ref, dd_ref, dwg_ref,
             gre, gim, st):
        c = nc - 1 - pl.program_id(0)

        @pl.when(pl.program_id(0) == 0)
        def _():
            st[...] = jnp.zeros_like(st)
            for r in (dbbre_ref, dbbim_ref, dcre_ref, dcim_ref, da_ref, dd_ref, dwg_ref):
                r[...] = jnp.zeros_like(r)

        u = u_ref[...]
        s_re, s_im = sre_ref[...], sim_ref[...]

        def head(s_res, s_ims, cres, cims, dv, uv, gv, wg):
            ypre = jnp.concatenate([_bdot(sr, cr, 1, 0) - _bdot(si, ci, 1, 0)
                                    for sr, si, cr, ci in zip(s_res, s_ims, cres, cims)], axis=1) + dv * uv
            return _s5_post(ypre, gv, wg)

        _, vjp = jax.vjp(head, [sre_ref[:, ss] for _, ss in S5_BLOCKS], [sim_ref[:, ss] for _, ss in S5_BLOCKS],
                         [cre_ref[ss, us].astype(F32) for us, ss in S5_BLOCKS],
                         [cim_ref[ss, us].astype(F32) for us, ss in S5_BLOCKS],
                         d_ref[...], u, g_ref[...], wg_ref[...].astype(F32))
        ds_res, ds_ims, dcres, dcims, dd, du_d, dgate, dwg = vjp(do_ref[0])
        for k, (us, ss) in enumerate(S5_BLOCKS):
            dcre_ref[ss, us] += dcres[k]
            dcim_ref[ss, us] += dcims[k]
            gre[:, ss] = ds_res[k]
            gim[:, ss] = ds_ims[k]
        dd_ref[...] += dd
        dwg_ref[...] += dwg
        dp_ref[:, BW:] = dgate.astype(BF16)
        ar, ai = a_ref[0:1, :], a_ref[1:2, :]

        def step(i, carry):
            t = S5_T - 1 - i
            gr, gi = carry
            nr = gre[pl.ds(t, 1), :] + gr
            ni = gim[pl.ds(t, 1), :] + gi
            gre[pl.ds(t, 1), :] = nr
            gim[pl.ds(t, 1), :] = ni
            return ar * nr + ai * ni, ar * ni - ai * nr

        gr, gi = lax.fori_loop(0, S5_T, step, (st[0:1, :], st[1:2, :]), unroll=8)
        st[0:1, :] = gr
        st[1:2, :] = gi
        g_re, g_im = gre[...], gim[...]
        first = jnp.where(c > 0, 1.0, 0.0)
        row = lax.broadcasted_iota(jnp.int32, (S5_T, S5_CH), 0)
        p_re = jnp.where(row == 0, pre_ref[7:8, :] * first, pltpu.roll(s_re, 1, 0))
        p_im = jnp.where(row == 0, pim_ref[7:8, :] * first, pltpu.roll(s_im, 1, 0))
        da_ref[0:1, :] += jnp.sum(g_re * p_re + g_im * p_im, axis=0, keepdims=True)
        da_ref[1:2, :] += jnp.sum(g_im * p_re - g_re * p_im, axis=0, keepdims=True)
        ub, grb, gib = u.astype(BF16), g_re.astype(BF16), g_im.astype(BF16)
        du_s = []
        for us, ss in S5_BLOCKS:
            dbbre_ref[us, ss] += _dg(ub[:, us], grb[:, ss], 0, 0)
            dbbim_ref[us, ss] += _dg(ub[:, us], gib[:, ss], 0, 0)
            du_s.append(_dg(grb[:, ss], bbre_ref[us, ss], 1, 1) + _dg(gib[:, ss], bbim_ref[us, ss], 1, 1))
        dp_ref[:, :BW] = (du_d + jnp.concatenate(du_s, axis=1)).astype(BF16)

    full = lambda shape: pl.BlockSpec(shape, lambda i: (0, 0))
    rev = lambda w, col=0: pl.BlockSpec((S5_T, w), lambda i: (nc - 1 - i, col))
    prev = pl.BlockSpec((8, S5_CH), lambda i: (jnp.maximum((nc - 1 - i) * (S5_T // 8) - 1, 0), 0))
    return pl.pallas_call(
        body, name=name, grid=(nc,),
        in_specs=[rev(BW, C_S5U // BW), rev(BW, C_S5G // BW), pl.BlockSpec((1, S5_T, BW), lambda i: (0, nc - 1 - i, 0)),
                  rev(S5_CH), rev(S5_CH), prev, prev,
                  full((BW, S5_CH)), full((BW, S5_CH)), full((S5_CH, BW)), full((S5_CH, BW)),
                  full((2, S5_CH)), full((1, BW)), full((BW, BW)), pl.BlockSpec(memory_space=pl.ANY)],
        out_specs=[rev(2 * BW, C_S5U // (2 * BW)), full((BW, S5_CH)), full((BW, S5_CH)), full((S5_CH, BW)), full((S5_CH, BW)),
                   full((2, S5_CH)), full((1, BW)), full((BW, BW))],
        input_output_aliases={14: 0},
        out_shape=[jax.ShapeDtypeStruct((SEQ, IN_PAD), BF16),
                   jax.ShapeDtypeStruct((BW, S5_CH), F32), jax.ShapeDtypeStruct((BW, S5_CH), F32),
                   jax.ShapeDtypeStruct((S5_CH, BW), F32), jax.ShapeDtypeStruct((S5_CH, BW), F32),
                   jax.ShapeDtypeStruct((2, S5_CH), F32), jax.ShapeDtypeStruct((1, BW), F32),
                   jax.ShapeDtypeStruct((BW, BW), F32)],
        scratch_shapes=[pltpu.VMEM((S5_T, S5_CH), F32), pltpu.VMEM((S5_T, S5_CH), F32), pltpu.VMEM((2, S5_CH), F32)],
        compiler_params=_cparams(("arbitrary",)),
    )(proj, proj, dout, sre, sim, sre, sim, bbre, bbim, cre, cim, a2, dvec, wglu, dproj)


def _diag_blocks(dense, after=None):
    rows, cols = dense.shape
    rows_per, cols_per = rows // S5_GROUPS, cols // S5_GROUPS
    per_lane_block = LANES // cols_per
    tile = 512

    def body(d_ref, *rest):
        o_ref = rest[-1]
        r0 = pl.program_id(0) * tile
        grp = (r0 + lax.broadcasted_iota(jnp.int32, (tile, LANES), 0)) // rows_per
        lane = lax.broadcasted_iota(jnp.int32, (tile, LANES), 1)
        acc = jnp.zeros((tile, LANES), F32)
        for hb in range(cols // LANES):
            acc = acc + jnp.where(grp == per_lane_block * hb + lane // cols_per, d_ref[:, hb * LANES:(hb + 1) * LANES], 0.0)
        shift = LANES // 2
        while shift >= cols_per:
            acc = acc + pltpu.roll(acc, LANES - shift, 1)
            shift //= 2
        o_ref[...] = acc

    folded = pl.pallas_call(
        body, name=f"diag_blocks_{rows_per}x{cols_per}", grid=(rows // tile,),
        in_specs=[pl.BlockSpec((tile, cols), lambda i: (i, 0))] + ([] if after is None else [pl.BlockSpec(memory_space=pl.ANY)]),
        out_specs=pl.BlockSpec((tile, LANES), lambda i: (i, 0)),
        out_shape=jax.ShapeDtypeStruct((rows, LANES), F32), compiler_params=_cparams(("parallel",)),
    )(dense, *([] if after is None else [after]))
    return folded[:, :cols_per].reshape(S5_GROUPS, rows_per, cols_per)


def _block_diag(t):
    g, rows_per, cols_per = t.shape
    wide = jnp.tile(t.reshape(g * rows_per, cols_per), (1, g))
    r = lax.broadcasted_iota(jnp.int32, wide.shape, 0) // rows_per
    c = lax.broadcasted_iota(jnp.int32, wide.shape, 1) // cols_per
    return jnp.where(r == c, wide, 0.0)


def _s5_disc(lam_re, lam_im, b_re, b_im, c_re, c_im, d, log_step):
    step = jnp.exp(log_step)[:, None]
    mag = jnp.exp(lam_re * step)
    ab_re, ab_im = mag * jnp.cos(lam_im * step), mag * jnp.sin(lam_im * step)
    den = lam_re * lam_re + lam_im * lam_im
    nr = ab_re - 1.0
    coef_re = (nr * lam_re + ab_im * lam_im) / den
    coef_im = (ab_im * lam_re - nr * lam_im) / den
    bb_re = coef_re[..., None] * b_re - coef_im[..., None] * b_im
    bb_im = coef_re[..., None] * b_im + coef_im[..., None] * b_re
    a2 = jnp.stack([ab_re.reshape(-1), ab_im.reshape(-1)])
    return (jnp.swapaxes(bb_re, 1, 2), jnp.swapaxes(bb_im, 1, 2),
            jnp.swapaxes(c_re, 1, 2), jnp.swapaxes(c_im, 1, 2),
            a2, d.reshape(1, BW))


def _left_lanes(shape):
    return lax.broadcasted_iota(jnp.int32, shape, 1) < 64


def _sgu_chunk(u, v, gate, ln_w, ln_b, w, bias):
    u32, v32 = _gelu(u), _gelu(v)
    mu = jnp.mean(v32, axis=-1, keepdims=True)
    var = jnp.mean(jnp.square(v32 - mu), axis=-1, keepdims=True)
    vn = (v32 - mu) * lax.rsqrt(var + EPS) * ln_w + ln_b
    t_i = lax.broadcasted_iota(jnp.int32, (SGU_CHUNK, SGU_CHUNK), 0)
    s_i = lax.broadcasted_iota(jnp.int32, (SGU_CHUNK, SGU_CHUNK), 1)
    causal = t_i >= s_i
    left = _left_lanes((SGU_CHUNK, LANES))
    sgate = _silu(gate)
    outs = []
    for j in range(BW // LANES):
        vb = vn[:, j * LANES:(j + 1) * LANES]
        s_blk = (_bdot(jnp.where(causal, w[2 * j], 0.0), jnp.where(left, vb, 0.0), 1, 0)
                 + _bdot(jnp.where(causal, w[2 * j + 1], 0.0), jnp.where(left, 0.0, vb), 1, 0))
        sl = slice(j * LANES, (j + 1) * LANES)
        outs.append(u32[:, sl] * (s_blk + bias[:, sl]) * sgate[:, sl])
    return outs


def _sgu_fwd(proj, ys, ln_w, ln_b, w, bias, name):
    def body(u_ref, v_ref, g_ref, lw_ref, lb_ref, w_ref, b_ref, ys_in, o_ref):
        outs = _sgu_chunk(u_ref[...], v_ref[...], g_ref[...], lw_ref[...], lb_ref[...], w_ref[...], b_ref[...])
        for j, o in enumerate(outs):
            o_ref[0, :, j * LANES:(j + 1) * LANES] = o.astype(BF16)

    blk = lambda col: pl.BlockSpec((SGU_CHUNK, BW), lambda c: (c, col // BW))
    vec = pl.BlockSpec((1, BW), lambda c: (0, 0))
    return pl.pallas_call(
        body, name=name, grid=(SEQ // SGU_CHUNK,),
        in_specs=[blk(C_SGU_U), blk(C_SGU_V), blk(C_SGU_G), vec, vec,
                  pl.BlockSpec((SGU_HEADS, SGU_CHUNK, SGU_CHUNK), lambda c: (0, 0, 0)),
                  pl.BlockSpec((SGU_CHUNK, BW), lambda c: (0, 0)), pl.BlockSpec(memory_space=pl.ANY)],
        out_specs=pl.BlockSpec((1, SGU_CHUNK, BW), lambda c: (1, c, 0)),
        out_shape=jax.ShapeDtypeStruct((N_BRANCH, SEQ, BW), BF16), input_output_aliases={7: 0},
        compiler_params=_cparams(("parallel",)),
    )(proj, proj, proj, ln_w, ln_b, w, bias, ys)


def _sgu_bwd(proj, dproj, dout, ln_w, ln_b, w, bias, name):
    def body(u_ref, v_ref, g_ref, do_ref, lw_ref, lb_ref, w_ref, b_ref, dproj_in, dp_ref, dlw_ref, dlb_ref, dw_ref, db_ref):
        _, vjp = jax.vjp(_sgu_chunk, u_ref[...], v_ref[...], g_ref[...], lw_ref[...], lb_ref[...], w_ref[...], b_ref[...])
        do = do_ref[0]
        du, dv, dgate, dlw, dlb, dw, db = vjp([do[:, j * LANES:(j + 1) * LANES] for j in range(BW // LANES)])
        dp_ref[:, 0:BW] = du.astype(BF16)
        dp_ref[:, BW:2 * BW] = dv.astype(BF16)
        dp_ref[:, 2 * BW:3 * BW] = dgate.astype(BF16)
        dp_ref[:, 3 * BW:] = jnp.zeros((SGU_CHUNK, BW), BF16)

        @pl.when(pl.program_id(0) == 0)
        def _():
            dlw_ref[...] = dlw
            dlb_ref[...] = dlb
            dw_ref[...] = dw
            db_ref[...] = db

        @pl.when(pl.program_id(0) > 0)
        def _():
            dlw_ref[...] += dlw
            dlb_ref[...] += dlb
            dw_ref[...] += dw
            db_ref[...] += db

    blk = lambda col: pl.BlockSpec((SGU_CHUNK, BW), lambda c: (c, col // BW))
    vec = pl.BlockSpec((1, BW), lambda c: (0, 0))
    wsp = pl.BlockSpec((SGU_HEADS, SGU_CHUNK, SGU_CHUNK), lambda c: (0, 0, 0))
    bsp = pl.BlockSpec((SGU_CHUNK, BW), lambda c: (0, 0))
    return pl.pallas_call(
        body, name=name, grid=(SEQ // SGU_CHUNK,),
        in_specs=[blk(C_SGU_U), blk(C_SGU_V), blk(C_SGU_G), pl.BlockSpec((1, SGU_CHUNK, BW), lambda c: (1, c, 0)),
                  vec, vec, wsp, bsp, pl.BlockSpec(memory_space=pl.ANY)],
        out_specs=[pl.BlockSpec((SGU_CHUNK, 4 * BW), lambda c: (c, C_SGU_U // (4 * BW))), vec, vec, wsp, bsp],
        input_output_aliases={8: 0},
        out_shape=[jax.ShapeDtypeStruct((SEQ, IN_PAD), BF16), jax.ShapeDtypeStruct((1, BW), F32),
                   jax.ShapeDtypeStruct((1, BW), F32), jax.ShapeDtypeStruct((SGU_HEADS, SGU_CHUNK, SGU_CHUNK), F32),
                   jax.ShapeDtypeStruct((SGU_CHUNK, BW), F32)],
        compiler_params=_cparams(("arbitrary",)),
    )(proj, proj, proj, dout, ln_w, ln_b, w, bias, dproj)


CONV_BLK = 256


def _m2_conv_fwd(proj, w, b, name):
    def body(x_ref, w_ref, b_ref, o_ref):
        x = x_ref[...]
        acc = jnp.zeros_like(x) + b_ref[...]
        for k in range(M2_CONV):
            acc = acc + w_ref[k:k + 1, :] * _shift_down(x, M2_CONV - 1 - k)
        o_ref[...] = _silu(acc)

    return pl.pallas_call(
        body, name=name, grid=(M2_CONV_CH // CONV_BLK,),
        in_specs=[pl.BlockSpec((SEQ, CONV_BLK), lambda j: (0, C_M2X // CONV_BLK + j)),
                  pl.BlockSpec((M2_CONV, CONV_BLK), lambda j: (0, j)), pl.BlockSpec((1, CONV_BLK), lambda j: (0, j))],
        out_specs=pl.BlockSpec((SEQ, CONV_BLK), lambda j: (0, j)),
        out_shape=jax.ShapeDtypeStruct((SEQ, M2_CONV_CH), F32),
        compiler_params=_cparams(("parallel",)),
    )(proj, w, b)


def _m2_conv_bwd(proj, dproj, dxa, w, b, name):
    def body(x_ref, d_ref, w_ref, b_ref, dproj_in, dx_ref, dw_ref, db_ref):
        x = x_ref[...]
        xs = [_shift_down(x, M2_CONV - 1 - k) for k in range(M2_CONV)]
        acc = jnp.zeros_like(x) + b_ref[...]
        for k in range(M2_CONV):
            acc = acc + w_ref[k:k + 1, :] * xs[k]
        sg = jax.nn.sigmoid(acc)
        dacc = d_ref[...] * (sg * (1.0 + acc * (1.0 - sg)))
        dx = jnp.zeros_like(x)
        for k in range(M2_CONV):
            dx = dx + w_ref[k:k + 1, :] * _shift_up(dacc, M2_CONV - 1 - k)
            dw_ref[k:k + 1, :] = jnp.sum(dacc * xs[k], axis=0, keepdims=True)
        dx_ref[...] = dx.astype(BF16)
        db_ref[...] = jnp.sum(dacc, axis=0, keepdims=True)

    return pl.pallas_call(
        body, name=name, grid=(M2_CONV_CH // CONV_BLK,),
        in_specs=[pl.BlockSpec((SEQ, CONV_BLK), lambda j: (0, C_M2X // CONV_BLK + j)),
                  pl.BlockSpec((SEQ, CONV_BLK), lambda j: (0, j)),
                  pl.BlockSpec((M2_CONV, CONV_BLK), lambda j: (0, j)), pl.BlockSpec((1, CONV_BLK), lambda j: (0, j)),
                  pl.BlockSpec(memory_space=pl.ANY)],
        out_specs=[pl.BlockSpec((SEQ, CONV_BLK), lambda j: (0, C_M2X // CONV_BLK + j)),
                   pl.BlockSpec((M2_CONV, CONV_BLK), lambda j: (0, j)), pl.BlockSpec((1, CONV_BLK), lambda j: (0, j))],
        input_output_aliases={4: 0},
        out_shape=[jax.ShapeDtypeStruct((SEQ, IN_PAD), BF16), jax.ShapeDtypeStruct((M2_CONV, M2_CONV_CH), F32),
                   jax.ShapeDtypeStruct((1, M2_CONV_CH), F32)],
        compiler_params=_cparams(("parallel",)),
    )(proj, dxa, w, b, dproj)


N_PAIR = M2_HEADS // 2
HI = lax.Precision.HIGHEST


def _col(a, h):
    lane = lax.broadcasted_iota(jnp.int32, a.shape, 1)
    return jnp.sum(jnp.where(lane == h, a, 0.0), axis=1, keepdims=True)


def _row(a, h):
    sub = lax.broadcasted_iota(jnp.int32, a.shape, 0)
    return jnp.sum(jnp.where(sub == h, a, 0.0), axis=0, keepdims=True)


def _ssd_chunk(xs, bms, cms, dtr, zs, states, dt_bias, a_log, dfs, nws):
    q = M2_CHUNK
    dt = _softplus(dtr + dt_bias)
    da = dt * (-jnp.exp(a_log))
    l_i = lax.broadcasted_iota(jnp.int32, (q, q), 0)
    s_i = lax.broadcasted_iota(jnp.int32, (q, q), 1)
    causal = l_i >= s_i
    tril = jnp.where(causal, 1.0, 0.0)
    a_cs = _dg(tril, da, 1, 0, HI)
    a_cs_t = _dg(da, tril, 0, 1, HI)
    a_end = _row(a_cs, q - 1)
    left = _left_lanes((q, LANES))
    left1 = _left_lanes((1, LANES))
    ys, nexts = [], []
    for j in range(N_PAIR):
        grp = j // 2
        bm, cm = bms[grp], cms[grp]
        h0, h1 = 2 * j, 2 * j + 1
        cb = _bdot(cm, bm, 1, 1)
        xdt = xs[j] * jnp.where(left, _col(dt, h0), _col(dt, h1))
        acs0, acs1 = _col(a_cs, h0), _col(a_cs, h1)
        y = _bdot(cm, states[j], 1, 0) * jnp.where(left, jnp.exp(acs0), jnp.exp(acs1))
        s_new = states[j] * jnp.where(left1, jnp.exp(_col(a_end, h0)), jnp.exp(_col(a_end, h1)))
        for h, acs, xh in ((h0, acs0, jnp.where(left, xdt, 0.0)), (h1, acs1, jnp.where(left, 0.0, xdt))):
            decay = jnp.exp(jnp.where(causal, acs - _row(a_cs_t, h), -jnp.inf))
            y = y + _bdot(cb * decay, xh, 1, 0)
            s_new = s_new + _bdot(bm * jnp.exp(_col(a_end, h) - acs), xh, 0, 0)
        ys.append((y + dfs[j] * xs[j]) * _silu(zs[j]))
        nexts.append(s_new)
    ssq = sum(jnp.sum(y * y, axis=-1, keepdims=True) for y in ys)
    scale = lax.rsqrt(ssq / BW + EPS)
    return [y * scale * nw for y, nw in zip(ys, nws)], nexts


def _blocks(ref, n, width=LANES):
    return [ref[:, j * width:(j + 1) * width] for j in range(n)]


def _ssd_fwd(proj, ys, xa, dt_bias, a_log, dfull, nw, name):
    nc = SEQ // M2_CHUNK

    def body(x_ref, b_ref, c_ref, dt_ref, z_ref, dtb_ref, al_ref, df_ref, nw_ref, ys_in, o_ref, sin_ref, st):
        @pl.when(pl.program_id(0) == 0)
        def _():
            st[...] = jnp.zeros_like(st)

        states = [st[j] for j in range(N_PAIR)]
        for j in range(N_PAIR):
            sin_ref[0, j] = states[j]
        ys, nexts = _ssd_chunk(_blocks(x_ref, 4), _blocks(b_ref, 2), _blocks(c_ref, 2), dt_ref[...], _blocks(z_ref, 4),
                               states, dtb_ref[...], al_ref[...], _blocks(df_ref, 4), _blocks(nw_ref, 4))
        for j in range(N_PAIR):
            o_ref[0, :, j * LANES:(j + 1) * LANES] = ys[j].astype(BF16)
            st[j] = nexts[j]

    vec8 = pl.BlockSpec((1, LANES), lambda c: (0, 0))
    vec = pl.BlockSpec((1, BW), lambda c: (0, 0))
    return pl.pallas_call(
        body, name=name, grid=(nc,),
        in_specs=[pl.BlockSpec((M2_CHUNK, BW), lambda c: (c, 0)), pl.BlockSpec((M2_CHUNK, 256), lambda c: (c, 2)),
                  pl.BlockSpec((M2_CHUNK, 256), lambda c: (c, 3)), pl.BlockSpec((M2_CHUNK, LANES), lambda c: (c, C_DT // LANES)),
                  pl.BlockSpec((M2_CHUNK, BW), lambda c: (c, C_M2Z // BW)), vec8, vec8, vec, vec,
                  pl.BlockSpec(memory_space=pl.ANY)],
        out_specs=[pl.BlockSpec((1, M2_CHUNK, BW), lambda c: (2, c, 0)),
                   pl.BlockSpec((1, N_PAIR, M2_STATE, LANES), lambda c: (c, 0, 0, 0))],
        out_shape=[jax.ShapeDtypeStruct((N_BRANCH, SEQ, BW), BF16), jax.ShapeDtypeStruct((nc, N_PAIR, M2_STATE, LANES), F32)],
        input_output_aliases={9: 0},
        scratch_shapes=[pltpu.VMEM((N_PAIR, M2_STATE, LANES), F32)],
        compiler_params=_cparams(("arbitrary",)),
    )(xa, xa, xa, proj, proj, dt_bias, a_log, dfull, nw, ys)


def _ssd_bwd(proj, dproj, xa, dout, s_in, dt_bias, a_log, dfull, nw, name):
    nc = SEQ // M2_CHUNK

    def body(x_ref, b_ref, c_ref, dt_ref, z_ref, do_ref, sin_ref, dtb_ref, al_ref, df_ref, nw_ref, dproj_in,
             dp_ref, dxa_ref, ddtb_ref, dal_ref, ddf_ref, dnw_ref, dst):
        @pl.when(pl.program_id(0) == 0)
        def _():
            dst[...] = jnp.zeros_like(dst)
            for r in (ddtb_ref, dal_ref, ddf_ref, dnw_ref):
                r[...] = jnp.zeros_like(r)

        states = [sin_ref[0, j] for j in range(N_PAIR)]
        _, vjp = jax.vjp(_ssd_chunk, _blocks(x_ref, 4), _blocks(b_ref, 2), _blocks(c_ref, 2), dt_ref[...],
                         _blocks(z_ref, 4), states, dtb_ref[...], al_ref[...], _blocks(df_ref, 4), _blocks(nw_ref, 4))
        dxs, dbs, dcs, ddt, dzs, dstates, ddtb, dal, ddfs, dnws = vjp(
            ([do_ref[0, :, j * LANES:(j + 1) * LANES] for j in range(N_PAIR)], [dst[j] for j in range(N_PAIR)]))
        for j in range(N_PAIR):
            sl = slice(j * LANES, (j + 1) * LANES)
            dxa_ref[:, sl] = dxs[j]
            dp_ref[:, sl] = dzs[j].astype(BF16)
            dst[j] = dstates[j]
            ddf_ref[:, sl] += ddfs[j]
            dnw_ref[:, sl] += dnws[j]
        for g in range(2):
            dxa_ref[:, BW + g * LANES:BW + (g + 1) * LANES] = dbs[g]
            dxa_ref[:, BW + 256 + g * LANES:BW + 256 + (g + 1) * LANES] = dcs[g]
        dp_ref[:, BW:BW + LANES] = ddt.astype(BF16)
        dp_ref[:, BW + LANES:] = jnp.zeros((M2_CHUNK, 2 * BW - BW - LANES), BF16)
        ddtb_ref[...] += ddtb
        dal_ref[...] += dal

    rev = lambda w, col=0: pl.BlockSpec((M2_CHUNK, w), lambda i: (nc - 1 - i, col))
    vec8 = pl.BlockSpec((1, LANES), lambda i: (0, 0))
    vec = pl.BlockSpec((1, BW), lambda i: (0, 0))
    return pl.pallas_call(
        body, name=name, grid=(nc,),
        in_specs=[rev(BW), rev(256, 2), rev(256, 3), rev(LANES, C_DT // LANES), rev(BW, C_M2Z // BW),
                  pl.BlockSpec((1, M2_CHUNK, BW), lambda i: (2, nc - 1 - i, 0)),
                  pl.BlockSpec((1, N_PAIR, M2_STATE, LANES), lambda i: (nc - 1 - i, 0, 0, 0)), vec8, vec8, vec, vec,
                  pl.BlockSpec(memory_space=pl.ANY)],
        out_specs=[rev(2 * BW, C_M2Z // (2 * BW)), rev(M2_CONV_CH), vec8, vec8, vec, vec],
        input_output_aliases={11: 0},
        out_shape=[jax.ShapeDtypeStruct((SEQ, IN_PAD), BF16), jax.ShapeDtypeStruct((SEQ, M2_CONV_CH), F32),
                   jax.ShapeDtypeStruct((1, LANES), F32), jax.ShapeDtypeStruct((1, LANES), F32),
                   jax.ShapeDtypeStruct((1, BW), F32), jax.ShapeDtypeStruct((1, BW), F32)],
        scratch_shapes=[pltpu.VMEM((N_PAIR, M2_STATE, LANES), F32)],
        compiler_params=_cparams(("arbitrary",)),
    )(xa, xa, xa, proj, proj, dout, s_in, dt_bias, a_log, dfull, nw, dproj)


def _sc_specs():
    col = lambda kind: pl.BlockSpec((SEQ, LANES), lambda j: (0, C_SC // LANES + 4 * j + kind))
    return [col(0), col(1), col(2), col(3)]


def _sc_fwd(proj, ys, w, name):
    def body(b_ref, c_ref, h_ref, g_ref, w_ref, ys_in, o_ref):
        ch = c_ref[...] * h_ref[...]
        acc = jnp.zeros_like(ch)
        for k in range(SC_CONV):
            acc = acc + w_ref[k:k + 1, :] * _shift_down(ch, SC_CONV - 1 - k)
        o_ref[0] = (b_ref[...] * acc * _silu(g_ref[...])).astype(BF16)

    return pl.pallas_call(
        body, name=name, grid=(BW // LANES,),
        in_specs=_sc_specs() + [pl.BlockSpec((SC_CONV, LANES), lambda j: (0, j)), pl.BlockSpec(memory_space=pl.ANY)],
        out_specs=pl.BlockSpec((1, SEQ, LANES), lambda j: (3, 0, j)),
        out_shape=jax.ShapeDtypeStruct((N_BRANCH, SEQ, BW), BF16), input_output_aliases={5: 0},
        compiler_params=_cparams(("parallel",)),
    )(proj, proj, proj, proj, w, ys)


def _sc_bwd(proj, dproj, dout, w, name):
    def body(b_ref, c_ref, h_ref, g_ref, do_ref, w_ref, dproj_in, dp_ref, dw_ref):
        cv, hv, gv = c_ref[...], h_ref[...], g_ref[...]
        ch = cv * hv
        chs = [_shift_down(ch, SC_CONV - 1 - k) for k in range(SC_CONV)]
        acc = jnp.zeros_like(ch)
        for k in range(SC_CONV):
            acc = acc + w_ref[k:k + 1, :] * chs[k]
        sg = jax.nn.sigmoid(gv)
        do = do_ref[0]
        bv = b_ref[...]
        dp_ref[:, 0:LANES] = (do * acc * (gv * sg)).astype(BF16)
        dp_ref[:, 3 * LANES:] = (do * bv * acc * (sg * (1.0 + gv * (1.0 - sg)))).astype(BF16)
        dacc = do * bv * (gv * sg)
        dch = jnp.zeros_like(ch)
        for k in range(SC_CONV):
            dch = dch + w_ref[k:k + 1, :] * _shift_up(dacc, SC_CONV - 1 - k)
            dw_ref[k:k + 1, :] = jnp.sum(dacc * chs[k], axis=0, keepdims=True)
        dp_ref[:, LANES:2 * LANES] = (dch * hv).astype(BF16)
        dp_ref[:, 2 * LANES:3 * LANES] = (dch * cv).astype(BF16)

    wsp = pl.BlockSpec((SC_CONV, LANES), lambda j: (0, j))
    return pl.pallas_call(
        body, name=name, grid=(BW // LANES,),
        in_specs=_sc_specs() + [pl.BlockSpec((1, SEQ, LANES), lambda j: (3, 0, j)), wsp, pl.BlockSpec(memory_space=pl.ANY)],
        out_specs=[pl.BlockSpec((SEQ, 4 * LANES), lambda j: (0, C_SC // (4 * LANES) + j)), wsp],
        input_output_aliases={6: 0},
        out_shape=[jax.ShapeDtypeStruct((SEQ, IN_PAD), BF16), jax.ShapeDtypeStruct((SC_CONV, BW), F32)],
        compiler_params=_cparams(("parallel",)),
    )(proj, proj, proj, proj, dout, w, dproj)


MERGE_T = 256
MERGE_BWD_T = 1024


def _merge_fwd(proj, ys, merge_b, w_branch, name):
    def body(y_ref, lg_ref, b_ref, w_ref, o_ref):
        acc = jnp.zeros((MERGE_T, D_MODEL), F32)
        for k in range(N_BRANCH):
            gate = jax.nn.sigmoid(lg_ref[:, k * D_MODEL:(k + 1) * D_MODEL] + b_ref[k])
            acc = acc + gate * _dg(y_ref[k], w_ref[k], 1, 0)
        o_ref[...] = acc.astype(BF16)

    return pl.pallas_call(
        body, name=name, grid=(SEQ // MERGE_T,),
        in_specs=[pl.BlockSpec((N_BRANCH, MERGE_T, BW), lambda i: (0, i, 0)),
                  pl.BlockSpec((MERGE_T, N_BRANCH * D_MODEL), lambda i: (i, C_MERGE // (N_BRANCH * D_MODEL))),
                  pl.BlockSpec((N_BRANCH, 1, D_MODEL), lambda i: (0, 0, 0)),
                  pl.BlockSpec((N_BRANCH, BW, D_MODEL), lambda i: (0, 0, 0))],
        out_specs=pl.BlockSpec((MERGE_T, D_MODEL), lambda i: (i, 0)),
        out_shape=jax.ShapeDtypeStruct((SEQ, D_MODEL), BF16),
        compiler_params=_cparams(("parallel",)),
    )(ys, proj, merge_b, w_branch)


def _merge_bwd(proj, ys, dm, merge_b, w_branch, name):
    nt = SEQ // MERGE_BWD_T

    def body(y_ref, lg_ref, dm_ref, b_ref, w_ref, dy_ref, dlg_ref, dw_ref, db_ref, dw_acc):
        i = pl.program_id(1)
        gate = jax.nn.sigmoid(lg_ref[...] + b_ref[0])
        y = y_ref[0]
        dmv = dm_ref[...]
        dbo = (gate * dmv).astype(BF16)
        dlg = _dg(y, w_ref[0], 1, 0) * dmv * gate * (1.0 - gate)
        dlg_ref[...] = dlg.astype(BF16)
        dy_ref[0] = _dg(dbo, w_ref[0], 1, 1)
        dwp = _dg(y, dbo, 0, 0)
        dbp = jnp.sum(dlg, axis=0, keepdims=True)

        @pl.when(i == 0)
        def _():
            dw_acc[...] = dwp
            db_ref[0] = dbp

        @pl.when(i > 0)
        def _():
            dw_acc[...] += dwp
            db_ref[0] += dbp

        @pl.when(i == nt - 1)
        def _():
            dw_ref[0] = dw_acc[...].astype(BF16)

    return pl.pallas_call(
        body, name=name, grid=(N_BRANCH, nt),
        in_specs=[pl.BlockSpec((1, MERGE_BWD_T, BW), lambda k, i: (k, i, 0)),
                  pl.BlockSpec((MERGE_BWD_T, D_MODEL), lambda k, i: (i, C_MERGE // D_MODEL + k)),
                  pl.BlockSpec((MERGE_BWD_T, D_MODEL), lambda k, i: (i, 0)),
                  pl.BlockSpec((1, 1, D_MODEL), lambda k, i: (k, 0, 0)),
                  pl.BlockSpec((1, BW, D_MODEL), lambda k, i: (k, 0, 0))],
        out_specs=[pl.BlockSpec((1, MERGE_BWD_T, BW), lambda k, i: (k, i, 0)),
                   pl.BlockSpec((MERGE_BWD_T, D_MODEL), lambda k, i: (i, k)),
                   pl.BlockSpec((1, BW, D_MODEL), lambda k, i: (k, 0, 0)),
                   pl.BlockSpec((1, 1, D_MODEL), lambda k, i: (k, 0, 0))],
        out_shape=[jax.ShapeDtypeStruct((N_BRANCH, SEQ, BW), F32), jax.ShapeDtypeStruct((SEQ, IN_PAD), BF16),
                   jax.ShapeDtypeStruct((N_BRANCH, BW, D_MODEL), BF16), jax.ShapeDtypeStruct((N_BRANCH, 1, D_MODEL), F32)],
        scratch_shapes=[pltpu.VMEM((BW, D_MODEL), F32)],
        compiler_params=_cparams(("parallel", "arbitrary")),
    )(ys, proj, dm, merge_b, w_branch)


def _adamw(glist, w, m, v, rows, name):
    nl = len(glist)
    n, r, c = glist[0].shape
    assert w.shape == (nl, r, c) and r % rows == 0
    nb = r // rows

    def body(*refs):
        g_refs = refs[:nl]
        w_ref, m_ref, v_ref, go_ref, d_ref, mo_ref, vo_ref = refs[nl:]
        for layer in range(nl):
            @pl.when(pl.program_id(0) == layer)
            def _(g_ref=g_refs[layer]):
                g = g_ref[0].astype(F32)
                for s in range(1, n):
                    g = g + g_ref[s].astype(F32)
                mn = ADAM_B1 * m_ref[0] + (1.0 - ADAM_B1) * g
                vn = ADAM_B2 * v_ref[0] + (1.0 - ADAM_B2) * jnp.square(g)
                m_hat = mn / (1.0 - ADAM_B1 ** ADAM_STEP)
                v_hat = vn / (1.0 - ADAM_B2 ** ADAM_STEP)
                go_ref[0] = g
                d_ref[0] = -ADAM_LR * (m_hat / (jnp.sqrt(v_hat) + ADAM_EPS) + ADAM_WD * w_ref[0])
                mo_ref[0] = mn
                vo_ref[0] = vn

    def g_spec(layer):
        return pl.BlockSpec((n, rows, c), lambda a, i: (0, jnp.where(a < layer, 0, jnp.where(a == layer, i, nb - 1)), 0))

    blk = pl.BlockSpec((1, rows, c), lambda a, i: (a, i, 0))
    out = jax.ShapeDtypeStruct((nl, r, c), F32)
    return pl.pallas_call(
        body, name=name, grid=(nl, nb),
        in_specs=[g_spec(layer) for layer in range(nl)] + [blk, blk, blk],
        out_specs=[blk, blk, blk, blk], out_shape=[out, out, out, out],
        compiler_params=_cparams(("arbitrary", "arbitrary")),
    )(*glist, w, m, v)


X_ROWS_PER_COL = 2 * (D_MODEL // LANES)


def _w_in_to_x(w):
    t = jnp.transpose(w, (2, 0, 1)).reshape(SHARD_IN, DEPTH, D_MODEL // LANES, LANES)
    return jnp.transpose(t, (0, 2, 1, 3)).reshape(SHARD_IN * X_ROWS_PER_COL, LANES)


def _w_in_from_x(xv):
    t = jnp.transpose(xv.reshape(SHARD_IN, D_MODEL // LANES, DEPTH, LANES), (0, 2, 1, 3))
    return jnp.transpose(t.reshape(SHARD_IN, DEPTH, D_MODEL), (1, 2, 0))


def _adamw_w_in(glist, w, m, v, name, after=None):
    n = glist[0].shape[0]
    cols = 2 * LANES
    rows = cols * X_ROWS_PER_COL
    extra = [] if after is None else [after]

    def body(g0_ref, g1_ref, w_ref, m_ref, v_ref, *rest):
        go_ref, d_ref, mo_ref, vo_ref = rest[len(extra):]
        for layer, g_ref in enumerate((g0_ref, g1_ref)):
            g = g_ref[0].astype(F32)
            for s in range(1, n):
                g = g + g_ref[s].astype(F32)
            gt = g.T
            for t in range(D_MODEL // LANES):
                sel = (pl.ds(2 * t + layer, cols, stride=X_ROWS_PER_COL), slice(None))
                gs = gt[:, t * LANES:(t + 1) * LANES]
                mn = ADAM_B1 * m_ref[sel] + (1.0 - ADAM_B1) * gs
                vn = ADAM_B2 * v_ref[sel] + (1.0 - ADAM_B2) * jnp.square(gs)
                m_hat = mn / (1.0 - ADAM_B1 ** ADAM_STEP)
                v_hat = vn / (1.0 - ADAM_B2 ** ADAM_STEP)
                go_ref[sel] = gs
                d_ref[sel] = -ADAM_LR * (m_hat / (jnp.sqrt(v_hat) + ADAM_EPS) + ADAM_WD * w_ref[sel])
                mo_ref[sel] = mn
                vo_ref[sel] = vn

    g_spec = pl.BlockSpec((n, D_MODEL, cols), lambda i: (0, 0, i))
    blk = pl.BlockSpec((rows, LANES), lambda i: (i, 0))
    out = jax.ShapeDtypeStruct((SHARD_IN * X_ROWS_PER_COL, LANES), F32)
    res = pl.pallas_call(
        body, name=name, grid=(-(-SHARD_IN // cols),),
        in_specs=[g_spec, g_spec, blk, blk, blk] + [pl.BlockSpec(memory_space=pl.ANY)] * len(extra),
        out_specs=[blk, blk, blk, blk], out_shape=[out, out, out, out],
        compiler_params=_cparams(("parallel",)),
    )(*glist, _w_in_to_x(w), _w_in_to_x(m), _w_in_to_x(v), *extra)
    return [_w_in_from_x(o) for o in res]


def _adamw_many(gs, ws, ms, vs, name):
    k = len(gs)

    def body(*refs):
        g_refs, w_refs, m_refs, v_refs = refs[:k], refs[k:2 * k], refs[2 * k:3 * k], refs[3 * k:4 * k]
        d_refs, mo_refs, vo_refs = refs[4 * k:5 * k], refs[5 * k:6 * k], refs[6 * k:7 * k]
        for i in range(k):
            g = g_refs[i][...]
            mn = ADAM_B1 * m_refs[i][...] + (1.0 - ADAM_B1) * g
            vn = ADAM_B2 * v_refs[i][...] + (1.0 - ADAM_B2) * jnp.square(g)
            m_hat = mn / (1.0 - ADAM_B1 ** ADAM_STEP)
            v_hat = vn / (1.0 - ADAM_B2 ** ADAM_STEP)
            d_refs[i][...] = -ADAM_LR * (m_hat / (jnp.sqrt(v_hat) + ADAM_EPS) + ADAM_WD * w_refs[i][...])
            mo_refs[i][...] = mn
            vo_refs[i][...] = vn

    whole = pl.BlockSpec(memory_space=pltpu.VMEM)
    shapes = [jax.ShapeDtypeStruct(w.shape, F32) for w in ws]
    outs = pl.pallas_call(
        body, name=name, in_specs=[whole] * (4 * k), out_specs=[whole] * (3 * k), out_shape=shapes * 3,
        compiler_params=_cparams(None),
    )(*gs, *ws, *ms, *vs)
    return outs[:k], outs[k:2 * k], outs[2 * k:]


MEMORY_ORDER = {'s5_b_re': (0, 1, 3, 2), 's5_b_im': (0, 1, 3, 2), 's5_d': (0, 2, 1), 'sc_conv_w': (1, 0, 2)}


def _memory_view(name, t):
    return jnp.transpose(t, MEMORY_ORDER[name]) if name in MEMORY_ORDER else t


def _slot_sum(gslots, name):
    n, r, c = gslots.shape

    def body(g_ref, o_ref):
        g = g_ref[0]
        for s in range(1, n):
            g = g + g_ref[s]
        o_ref[...] = g

    return pl.pallas_call(
        body, name=name, in_specs=[pl.BlockSpec((n, r, c), lambda: (0, 0, 0))],
        out_specs=pl.BlockSpec((r, c), lambda: (0, 0)), out_shape=jax.ShapeDtypeStruct((r, c), F32),
        compiler_params=_cparams(None),
    )(gslots)


def _me_and_peers():
    x, y, c = lax.axis_index("x"), lax.axis_index("y"), lax.axis_index("c")
    me = 4 * x + 2 * y + c
    peers = []
    for k in range(1, N_DEV):
        px = 1 - x if (k >> 2) & 1 else x
        py = 1 - y if (k >> 1) & 1 else y
        pc = 1 - c if k & 1 else c
        peers.append((4 * px + 2 * py + pc, (px, py, pc)))
    return me, peers


_HBM = pl.BlockSpec(memory_space=pltpu.HBM)
_SEM = pl.BlockSpec(memory_space=pltpu.SEMAPHORE)
_EFFECT = pltpu.SideEffectType.DATAFLOW_SIDE_EFFECTING


N_CHIP = N_DEV // 2


def _chip_peers():
    x, y, c = lax.axis_index("x"), lax.axis_index("y"), lax.axis_index("c")
    chips = []
    for d in range(1, N_CHIP):
        px = 1 - x if (d >> 1) & 1 else x
        py = 1 - y if d & 1 else y
        chips.append((2 * px + py, (px, py)))
    return (x, y, c), 2 * x + y, chips


def _plan_direct(ins, lands, send_sems, recv_sems, local_sems, gather):
    me, peers = _me_and_peers()
    plan = dict(start=[], local=[], sends=[], recvs=[])
    for t in range(len(ins)):
        own = pltpu.make_async_copy(ins[t] if gather else ins[t].at[me], lands[t].at[me], local_sems.at[t])
        plan['start'].append(own)
        plan['local'].append(own)
        for k, (pidx, pos) in enumerate(peers):
            cp = pltpu.make_async_remote_copy(
                src_ref=ins[t] if gather else ins[t].at[pidx], dst_ref=lands[t].at[me],
                send_sem=send_sems.at[t * (N_DEV - 1) + k], recv_sem=recv_sems.at[t * (N_DEV - 1) + k],
                device_id=pos, device_id_type=MESH)
            plan['start'].append(cp)
            plan['sends'].append(cp)
            plan['recvs'].append(cp)
    return plan


def _plan_gather(ins, lands, send_sems, recv_sems, local_sems, first=0):
    (x, y, c), q, chips = _chip_peers()
    me = 2 * q + c
    plan = dict(start=[], relay_wait=[], relay_start=[], local=[], sends=[], recvs=[])
    for t in range(len(ins)):
        base = (first + t) * 7
        sem = lambda k: dict(send_sem=send_sems.at[base + k], recv_sem=recv_sems.at[base + k], device_id_type=MESH)
        own = pltpu.make_async_copy(ins[t], lands[t].at[me], local_sems.at[first + t])
        to_sib = pltpu.make_async_remote_copy(src_ref=ins[t], dst_ref=lands[t].at[me], device_id=(x, y, 1 - c), **sem(0))
        plan['start'] += [own, to_sib]
        plan['local'].append(own)
        plan['sends'].append(to_sib)
        plan['recvs'].append(to_sib)
        for d, (pq, (px, py)) in enumerate(chips):
            to_chip = pltpu.make_async_remote_copy(src_ref=ins[t], dst_ref=lands[t].at[me], device_id=(px, py, c), **sem(1 + d))
            blk = lands[t].at[2 * pq + c]
            fwd = pltpu.make_async_remote_copy(src_ref=blk, dst_ref=blk, device_id=(x, y, 1 - c), **sem(4 + d))
            plan['start'].append(to_chip)
            plan['relay_wait'].append(to_chip)
            plan['relay_start'].append(fwd)
            plan['sends'] += [to_chip, fwd]
            plan['recvs'].append(fwd)
    return plan


def _plan_pair(ins, lands, send_sems, recv_sems, local_sems):
    (x, y, c), q, chips = _chip_peers()
    plan = dict(start=[], local=[], sends=[], recvs=[])
    for t in range(len(ins)):
        for k in range(N_CHIP):
            cp = pltpu.make_async_remote_copy(
                src_ref=ins[t].at[2 * k + 1 - c], dst_ref=lands[t].at[k], send_sem=send_sems.at[t * N_CHIP + k],
                recv_sem=recv_sems.at[t * N_CHIP + k], device_id=(x, y, 1 - c), device_id_type=MESH)
            plan['start'].append(cp)
            plan['sends'].append(cp)
            plan['recvs'].append(cp)
    return plan


def _plan_chips(ins, lands, send_sems, recv_sems, local_sems):
    (x, y, c), q, chips = _chip_peers()
    plan = dict(start=[], local=[], sends=[], recvs=[])
    for t in range(len(ins)):
        own = pltpu.make_async_copy(ins[t].at[q], lands[t].at[q], local_sems.at[t])
        plan['start'].append(own)
        plan['local'].append(own)
        for d, (pq, (px, py)) in enumerate(chips):
            cp = pltpu.make_async_remote_copy(
                src_ref=ins[t].at[pq], dst_ref=lands[t].at[q], send_sem=send_sems.at[t * 3 + d],
                recv_sem=recv_sems.at[t * 3 + d], device_id=(px, py, c), device_id_type=MESH)
            plan['start'].append(cp)
            plan['sends'].append(cp)
            plan['recvs'].append(cp)
    return plan


def _split_start(plan_fn, tensors, land_shapes, n_sems, name, after=None):
    n = len(tensors)
    extra = [] if after is None else [after]

    def body(*refs):
        ins, lands = refs[:n], refs[n:2 * n]
        plan = plan_fn(ins, lands, *refs[2 * n + len(extra):2 * n + len(extra) + 3])
        for cp in plan['start']:
            cp.start()
        refs[-1][...] = jnp.zeros_like(refs[-1])

    outs = pl.pallas_call(
        body, name=name,
        out_shape=(pltpu.SemaphoreType.DMA((n_sems,)), pltpu.SemaphoreType.DMA((n_sems,)), pltpu.SemaphoreType.DMA((n,)),
                   *[pltpu.HBM(t.shape, t.dtype) for t in tensors],
                   *[pltpu.HBM(s, t.dtype) for s, t in zip(land_shapes, tensors)],
                   jax.ShapeDtypeStruct((8, LANES), F32)),
        in_specs=[_HBM] * (2 * n) + [pl.BlockSpec(memory_space=pl.ANY)] * len(extra),
        out_specs=(_SEM, _SEM, _SEM, *[_HBM] * (2 * n), pl.BlockSpec(memory_space=pltpu.VMEM)),
        input_output_aliases={t: 3 + t for t in range(2 * n)},
        compiler_params=pltpu.CompilerParams(has_side_effects=_EFFECT),
    )(*[pltpu.with_memory_space_constraint(t, pltpu.HBM) for t in tensors],
      *[pltpu.with_memory_space_constraint(lax.empty(s, t.dtype), pltpu.HBM) for s, t in zip(land_shapes, tensors)], *extra)
    return outs[:-1], outs[-1]


def _split_relay(plan_fn, state, after, name):
    sems, thru = state[:3], state[3:]
    n = len(thru) // 2

    def arrived(*refs):
        plan = plan_fn(refs[:n], refs[n:2 * n], *refs[2 * n:2 * n + 3])
        for cp in plan['relay_wait']:
            cp.wait_recv()

    thru = pl.pallas_call(
        arrived, name=name + "_arrived",
        out_shape=tuple(pltpu.HBM(t.shape, t.dtype) for t in thru),
        in_specs=[_HBM] * (2 * n) + [_SEM, _SEM, _SEM, pl.BlockSpec(memory_space=pl.ANY)],
        out_specs=tuple([_HBM] * (2 * n)),
        input_output_aliases={t: t for t in range(2 * n)},
        compiler_params=pltpu.CompilerParams(has_side_effects=_EFFECT),
    )(*thru, *sems, after)

    def forward(*refs):
        plan = plan_fn(refs[:n], refs[n:2 * n], *refs[2 * n:2 * n + 3])
        for cp in plan['relay_start']:
            cp.start()
        refs[-1][...] = jnp.zeros_like(refs[-1])

    outs = pl.pallas_call(
        forward, name=name + "_forward",
        out_shape=(*[pltpu.HBM(t.shape, t.dtype) for t in thru], jax.ShapeDtypeStruct((8, LANES), F32)),
        in_specs=[_HBM] * (2 * n) + [_SEM, _SEM, _SEM],
        out_specs=(*[_HBM] * (2 * n), pl.BlockSpec(memory_space=pltpu.VMEM)),
        input_output_aliases={t: t for t in range(2 * n)},
        compiler_params=pltpu.CompilerParams(has_side_effects=_EFFECT),
    )(*thru, *sems)
    return (*sems, *outs[:-1]), outs[-1]


def _split_wait(plan_fn, state, after, name, with_sources=False):
    sems, thru = state[:3], state[3:]
    n = len(thru) // 2

    def body(*refs):
        plan = plan_fn(refs[:n], refs[n:2 * n], *refs[2 * n:2 * n + 3])
        for cp in plan['local']:
            cp.wait()
        for cp in plan['sends']:
            cp.wait_send()
        for cp in plan['recvs']:
            cp.wait_recv()

    outs = pl.pallas_call(
        body, name=name,
        out_shape=tuple(pltpu.HBM(t.shape, t.dtype) for t in thru),
        in_specs=[_HBM] * (2 * n) + [_SEM, _SEM, _SEM, pl.BlockSpec(memory_space=pl.ANY)],
        out_specs=tuple([_HBM] * (2 * n)),
        input_output_aliases={t: t for t in range(2 * n)},
        compiler_params=pltpu.CompilerParams(has_side_effects=_EFFECT),
    )(*thru, *sems, after)
    return (list(outs[:n]), list(outs[n:])) if with_sources else list(outs[n:])


PAIR_SUM_BLOCK = 768 * 1024


def _pair_sum(mine, theirs, name):
    _, r, c = mine.shape
    rows = r
    while rows * c > PAIR_SUM_BLOCK and rows % 32 == 0:
        rows //= 2

    def body(core_ref, a_ref, b_ref, o_ref):
        o_ref[0] = (a_ref[0].astype(F32) + b_ref[0].astype(F32)).astype(o_ref.dtype)

    return pl.pallas_call(
        body, name=name,
        grid_spec=pltpu.PrefetchScalarGridSpec(
            num_scalar_prefetch=1, grid=(N_CHIP, r // rows),
            in_specs=[pl.BlockSpec((1, rows, c), lambda k, i, core: (2 * k + core[0], i, 0)),
                      pl.BlockSpec((1, rows, c), lambda k, i, core: (k, i, 0))],
            out_specs=pl.BlockSpec((1, rows, c), lambda k, i, core: (k, i, 0))),
        out_shape=jax.ShapeDtypeStruct((N_CHIP, r, c), mine.dtype),
        compiler_params=_cparams(("parallel", "parallel")),
    )(lax.axis_index("c").astype(jnp.int32).reshape(1), mine, theirs)


WEIGHTS = ['norm_w', 'w_in', 's5_lambda_re', 's5_lambda_im', 's5_b_re', 's5_b_im', 's5_c_re', 's5_c_im', 's5_d',
           's5_log_step', 's5_w_glu', 'sgu_ln_w', 'sgu_ln_b', 'sgu_w', 'sgu_b', 'm2_conv_w', 'm2_conv_b', 'm2_dt_bias',
           'm2_a_log', 'm2_d', 'm2_norm_w', 'sc_conv_w', 'merge_b', 'w_branch', 'w_out', 'final_norm_w']
BIG_SHARDED = ['w_in', 'w_branch', 'w_out', 's5_w_glu']
SMALL_SHARDED = ['m2_conv_w', 'sc_conv_w', 'merge_b']
REPLICATED = [n for n in WEIGHTS if n not in BIG_SHARDED + SMALL_SHARDED]
S5_NAMES = ['s5_lambda_re', 's5_lambda_im', 's5_b_re', 's5_b_im', 's5_c_re', 's5_c_im', 's5_d', 's5_log_step']


def _sc_interleave(t):
    lead = t.shape[:-1]
    return jnp.swapaxes(t.reshape(lead + (4, 4, LANES)), -3, -2).reshape(lead + (4 * BW,))


def _pad_in(w):
    z = lambda n: jnp.zeros(w.shape[:-1] + (n,), w.dtype)
    return jnp.concatenate([w[..., 6152:], w[..., 0:1024], w[..., 3072:4096], w[..., 1024:2560], z(512),
                            w[..., 2560:3072], w[..., 4096:4104], z(504), _sc_interleave(w[..., 4104:6152])], axis=-1)


def _unpad_in(g):
    return jnp.concatenate([g[..., C_S5U:C_S5U + 1024], g[..., C_SGU_U:C_SGU_U + 1536], g[..., C_M2Z:C_M2Z + 512],
                            g[..., C_M2X:C_M2X + 1024], g[..., C_DT:C_DT + 8], _sc_interleave(g[..., C_SC:]),
                            g[..., :N_BRANCH * D_MODEL]], axis=-1)


ROW_BLOCK = 8 * LANES


def _pack_rows(tensors, row_mult, batched=False):
    parts = []
    for t in tensors:
        f = t.reshape((t.shape[0], -1) if batched else (1, -1))
        f = jnp.pad(f, ((0, 0), (0, (-f.shape[1]) % ROW_BLOCK)))
        parts.append(f.reshape(f.shape[0], -1, LANES))
    out = jnp.concatenate(parts, axis=1)
    out = jnp.pad(out, ((0, 0), (0, (-out.shape[1]) % row_mult), (0, 0)))
    return out if batched else out[0]


def _unpack_rows(rows, shapes):
    out, r0 = [], 0
    for shp in shapes:
        size = 1
        for s in shp:
            size *= s
        nr = -(-size // ROW_BLOCK) * 8
        out.append(rows[r0:r0 + nr].reshape(-1)[:size].reshape(shp))
        r0 += nr
    return out


def _kernel_col_map():
    m = np.full(IN_PAD, -1, np.int64)
    m[C_MERGE:C_MERGE + 4096] = np.arange(6152, 10248)
    m[C_S5U:C_S5U + 1024] = np.arange(0, 1024)
    m[C_M2X:C_M2X + 1024] = np.arange(3072, 4096)
    m[C_SGU_U:C_SGU_U + 1536] = np.arange(1024, 2560)
    m[C_M2Z:C_M2Z + 512] = np.arange(2560, 3072)
    m[C_DT:C_DT + 8] = np.arange(4096, 4104)
    for j in range(4):
        for kind in range(4):
            k0 = C_SC + 4 * LANES * j + LANES * kind
            m[k0:k0 + LANES] = 4104 + BW * kind + LANES * j + np.arange(LANES)
    return m


def _lane_pieces(sources):
    pieces, cur = [], None
    for lane, src in enumerate(sources):
        key = None if src is None else (src[0], src[1] // LANES, (lane - src[1]) % LANES)
        if cur is not None and key == cur[0]:
            cur[2] = lane + 1
        else:
            if cur is not None and cur[0] is not None:
                pieces.append((*cur[0], cur[1], cur[2]))
            cur = [key, lane, lane + 1]
    if cur is not None and cur[0] is not None:
        pieces.append((*cur[0], cur[1], cur[2]))
    return pieces


def _assemble_block(pieces, load, rows, dtype):
    lane = lax.broadcasted_iota(jnp.int32, (rows, LANES), 1)
    out = None
    for arr, sb, shift, lo, hi in pieces:
        v = load(arr, sb)
        if shift:
            v = pltpu.roll(v, shift, 1)
        if out is None and lo == 0 and hi == LANES:
            out = v
        else:
            out = jnp.where((lane >= lo) & (lane < hi), v, jnp.zeros((rows, LANES), dtype) if out is None else out)
    return jnp.zeros((rows, LANES), dtype) if out is None else out


RELAYOUT_ROWS = 512
SHARD_BLOCKS = -(-SHARD_IN // LANES)


def _load_shard_block(ref, rows):
    def load(j, sb):
        if sb == SHARD_BLOCKS - 1:
            return jnp.broadcast_to(ref[j, :, SHARD_IN - 1:SHARD_IN], (rows, LANES))
        return ref[j, :, sb * LANES:(sb + 1) * LANES]
    return load


def _relayout_w_in(gathered, name):
    kmap = _kernel_col_map()
    dtype = gathered.dtype

    def body(src_ref, o_ref):
        load = _load_shard_block(src_ref, RELAYOUT_ROWS)
        for ob in range(IN_PAD // LANES):
            srcs = [None if kmap[ob * LANES + l] < 0 else (int(kmap[ob * LANES + l]) // SHARD_IN, int(kmap[ob * LANES + l]) % SHARD_IN)
                    for l in range(LANES)]
            o_ref[:, ob * LANES:(ob + 1) * LANES] = _assemble_block(_lane_pieces(srcs), load, RELAYOUT_ROWS, dtype)

    return pl.pallas_call(
        body, name=name, grid=(D_MODEL // RELAYOUT_ROWS,),
        in_specs=[pl.BlockSpec((N_DEV, RELAYOUT_ROWS, SHARD_IN), lambda i: (0, i, 0))],
        out_specs=pl.BlockSpec((RELAYOUT_ROWS, IN_PAD), lambda i: (i, 0)),
        out_shape=jax.ShapeDtypeStruct((D_MODEL, IN_PAD), dtype),
        compiler_params=_cparams(("parallel",)),
    )(gathered)


def _relayout_g_in(gw, name):
    kmap = _kernel_col_map()
    kinv = np.zeros(IN_DIM, np.int64)
    kinv[kmap[kmap >= 0]] = np.nonzero(kmap >= 0)[0]
    dtype = gw.dtype

    def body(src_ref, o_ref):
        load = lambda _, sb: src_ref[:, sb * LANES:(sb + 1) * LANES]
        for j in range(N_DEV):
            for ob in range(SHARD_BLOCKS):
                srcs = [(0, int(kinv[SHARD_IN * j + ob * LANES + l])) if ob * LANES + l < SHARD_IN else None for l in range(LANES)]
                blk = _assemble_block(_lane_pieces(srcs), load, RELAYOUT_ROWS, dtype)
                if ob == SHARD_BLOCKS - 1:
                    o_ref[j, :, SHARD_IN - 1:SHARD_IN] = blk[:, 0:1]
                else:
                    o_ref[j, :, ob * LANES:(ob + 1) * LANES] = blk

    return pl.pallas_call(
        body, name=name, grid=(D_MODEL // RELAYOUT_ROWS,),
        in_specs=[pl.BlockSpec((RELAYOUT_ROWS, IN_PAD), lambda i: (i, 0))],
        out_specs=pl.BlockSpec((N_DEV, RELAYOUT_ROWS, SHARD_IN), lambda i: (0, i, 0)),
        out_shape=jax.ShapeDtypeStruct((N_DEV, D_MODEL, SHARD_IN), dtype),
        compiler_params=_cparams(("parallel",)),
    )(gw)


def _rows128(flat, row_mult=8):
    n = flat.shape[0]
    per = LANES * row_mult
    total = -(-n // per) * per
    return jnp.pad(flat, (0, total - n)).reshape(total // LANES, LANES)


def _pad_lanes(v):
    return jnp.pad(v, (0, LANES - v.shape[0])).reshape(1, LANES)


def _layer_prep(i, p):
    disc, disc_vjp = jax.vjp(_s5_disc, *[p[n][i] for n in S5_NAMES])
    prep = dict(
        nw=p['norm_w'][i].reshape(1, D_MODEL), disc_vjp=disc_vjp,
        s5small=[_block_diag(t).astype(BF16) for t in disc[:4]] + [disc[4], disc[5]],
        sgw=[p['sgu_ln_w'][i].reshape(1, BW), p['sgu_ln_b'][i].reshape(1, BW), p['sgu_w'][i],
             jnp.repeat(p['sgu_b'][i].T, BW // SGU_HEADS, axis=1)],
        cb=p['m2_conv_b'][i].reshape(1, M2_CONV_CH),
        m2w=[_pad_lanes(p['m2_dt_bias'][i]), _pad_lanes(p['m2_a_log'][i]),
             jnp.repeat(p['m2_d'][i], M2_HEAD_DIM).reshape(1, BW), p['m2_norm_w'][i].reshape(1, BW)])
    touch = [t[0, 0].astype(F32) for t in prep['s5small']] + [prep['sgw'][3][0, 0], prep['m2w'][2][0, 0]]
    return prep, sum(touch[1:], touch[0])


def _layer_fwd(x, h, i, prep, w_in, other_weights, before_merge=None):
    proj = _matmul(h, w_in, 1, 0, F32, 1024, 1024, 1024, f"proj{i}")
    full = dict(other_weights(proj), w_in=w_in)
    s5w = prep['s5small'] + [full['s5_w_glu']]
    ys, sre, sim = _s5_fwd(proj, *s5w, f"s5_fwd{i}")
    ys = _sgu_fwd(proj, ys, *prep['sgw'], f"sgu_fwd{i}")
    cw = full['m2_conv_w']
    xa = _m2_conv_fwd(proj, cw, prep['cb'], f"m2conv_fwd{i}")
    ys, s_in = _ssd_fwd(proj, ys, xa, *prep['m2w'], f"ssd_fwd{i}")
    scw = full['sc_conv_w']
    ys = _sc_fwd(proj, ys, scw, f"sc_fwd{i}")
    mb = full['merge_b'].reshape(N_BRANCH, 1, D_MODEL)
    if before_merge is not None:
        mb = mb + before_merge(ys)[0, 0]
    merged = _merge_fwd(proj, ys, mb, full['w_branch'], f"merge_fwd{i}")
    x_new = _matmul(merged, full['w_out'], 1, 0, F32, 1024, 1024, 1024, f"out{i}", residual=x)
    saved = dict(x=x, nw=prep['nw'], h=h, proj=proj, disc_vjp=prep['disc_vjp'], s5w=s5w, sre=sre, sim=sim, sgw=prep['sgw'],
                 cw=cw, cb=prep['cb'], xa=xa, m2w=prep['m2w'], s_in=s_in, scw=scw, ys=ys, mb=mb, merged=merged)
    return x_new, saved, full


def _layer_bwd(dx_out, i, sv, full, on_large_grads=None, after_dh=None):
    g = {}
    proj = sv['proj']
    dm = _matmul(dx_out, full['w_out'], 1, 1, F32, 1024, 1024, 1024, f"dmerged{i}")
    g['w_out'] = _matmul(sv['merged'], dx_out, 0, 0, BF16, 1024, 1024, 1024, f"gw_out{i}")
    dys, dproj, g['w_branch'], dmb = _merge_bwd(proj, sv['ys'], dm, sv['mb'], full['w_branch'], f"merge_bwd{i}")
    g['merge_b'] = dmb.reshape(N_BRANCH, D_MODEL)
    dproj, dbbre, dbbim, dcre, dcim, da, dd, dwg = _s5_bwd(proj, dproj, dys, sv['sre'], sv['sim'], *sv['s5w'], f"s5_bwd{i}")
    g['s5_dense'] = (dbbre, dbbim, dcre, dcim, da, dd)
    g['s5_w_glu'] = dwg.astype(BF16)
    dproj, dlw, dlb, g['sgu_w'], dbias = _sgu_bwd(proj, dproj, dys, *sv['sgw'], f"sgu_bwd{i}")
    g['sgu_ln_w'], g['sgu_ln_b'] = dlw[0], dlb[0]
    g['sgu_b'] = dbias.reshape(SGU_CHUNK, SGU_HEADS, BW // SGU_HEADS).sum(-1).T
    dproj, dxa, ddtb, dal, ddf, dnw = _ssd_bwd(proj, dproj, sv['xa'], dys, sv['s_in'], *sv['m2w'], f"ssd_bwd{i}")
    dproj, g['m2_conv_w'], dcb = _m2_conv_bwd(proj, dproj, dxa, sv['cw'], sv['cb'], f"m2conv_bwd{i}")
    g['m2_conv_b'], g['m2_norm_w'] = dcb[0], dnw[0]
    g['m2_dt_bias'], g['m2_a_log'] = ddtb[0, :M2_HEADS], dal[0, :M2_HEADS]
    g['m2_d'] = ddf.reshape(M2_HEADS, M2_HEAD_DIM).sum(-1)
    dproj, g['sc_conv_w'] = _sc_bwd(proj, dproj, dys, sv['scw'], f"sc_bwd{i}")
    g['w_in'] = _matmul(sv['h'], dproj, 0, 0, BF16, 1024, 1024, 1024, f"gw_in{i}")
    tok = on_large_grads(g) if on_large_grads else None
    dh = _matmul(dproj, full['w_in'], 1, 1, F32, 1024, 1024, 1024, f"dh{i}", after=tok)
    nw = sv['nw'] if after_dh is None else sv['nw'] + after_dh(dh)[0, 0]
    dx_in, dnw_l = _rmsnorm_bwd(sv['x'], nw, dh, dx_out, f"rms_bwd{i}")
    g['norm_w'] = dnw_l[0]
    return dx_in, g


def _split8(t, axis):
    shp = t.shape
    t = t.reshape(shp[:axis] + (N_DEV, shp[axis] // N_DEV) + shp[axis + 1:])
    return jnp.moveaxis(t, axis, 0)


def _join8(t, axis):
    t = jnp.moveaxis(t, 0, axis)
    shp = t.shape
    return t.reshape(shp[:axis] + (shp[axis] * shp[axis + 1],) + shp[axis + 2:])


SHARD_AXIS = {'w_in': 2, 'w_branch': 3, 'w_out': 1, 's5_w_glu': 1, 'm2_conv_w': 2, 'sc_conv_w': 2, 'merge_b': 2}


OTHER_BIG = [n for n in BIG_SHARDED if n != 'w_in']


def _other_weights(gathered):
    return {n: _join8(t, SHARD_AXIS[n] - 1) for n, t in zip(OTHER_BIG, gathered)}


def _layer_grad_blocks(g, i):
    blocks = [_relayout_g_in(g[n], f"relayout_g_in{i}") if n == 'w_in' else _split8(g[n], SHARD_AXIS[n] - 1) for n in BIG_SHARDED]
    return [b.reshape(N_DEV, -1, b.shape[-1]) for b in blocks]


def _pair_start(blocks, tag):
    shapes = [(N_CHIP,) + b.shape[1:] for b in blocks]
    return _split_start(_plan_pair, blocks, shapes, N_CHIP * len(blocks), f"pair{tag}_start")


def _pair_sums(state, after, tag):
    mine, theirs = _split_wait(_plan_pair, state, after, f"pair{tag}_wait", with_sources=True)
    return [_pair_sum(b, t, f"pair_sum{tag}_{k}") for k, (b, t) in enumerate(zip(mine, theirs))]


def _chips_start(sums, tag, after=None):
    return _split_start(_plan_chips, sums, [s.shape for s in sums], 3 * len(sums), f"chips{tag}_start", after)


def kernel(x, norm_w, w_in, s5_lambda_re, s5_lambda_im, s5_b_re, s5_b_im, s5_c_re, s5_c_im, s5_d, s5_log_step, s5_w_glu, sgu_ln_w, sgu_ln_b, sgu_w, sgu_b, m2_conv_w, m2_conv_b, m2_dt_bias, m2_a_log, m2_d, m2_norm_w, sc_conv_w, merge_b, w_branch, w_out, final_norm_w, loss_target, m_norm_w, m_w_in, m_s5_lambda_re, m_s5_lambda_im, m_s5_b_re, m_s5_b_im, m_s5_c_re, m_s5_c_im, m_s5_d, m_s5_log_step, m_s5_w_glu, m_sgu_ln_w, m_sgu_ln_b, m_sgu_w, m_sgu_b, m_m2_conv_w, m_m2_conv_b, m_m2_dt_bias, m_m2_a_log, m_m2_d, m_m2_norm_w, m_sc_conv_w, m_merge_b, m_w_branch, m_w_out, m_final_norm_w, v_norm_w, v_w_in, v_s5_lambda_re, v_s5_lambda_im, v_s5_b_re, v_s5_b_im, v_s5_c_re, v_s5_c_im, v_s5_d, v_s5_log_step, v_s5_w_glu, v_sgu_ln_w, v_sgu_ln_b, v_sgu_w, v_sgu_b, v_m2_conv_w, v_m2_conv_b, v_m2_dt_bias, v_m2_a_log, v_m2_d, v_m2_norm_w, v_sc_conv_w, v_merge_b, v_w_branch, v_w_out, v_final_norm_w):
    loc = locals()
    p = {n: loc[n] for n in WEIGHTS}
    mom = {n: loc['m_' + n] for n in WEIGHTS}
    vel = {n: loc['v_' + n] for n in WEIGHTS}

    small_sizes = [p[n].size for n in SMALL_SHARDED]
    small_pack = _rows128(jnp.concatenate([p[n].reshape(-1) for n in SMALL_SHARDED]))
    first = [p['w_in'][0].astype(BF16)]
    gath_first, tok = _split_start(_plan_gather, first, [(N_DEV,) + first[0].shape], 7, "gather_w_in0_start")
    shards = ([(p[n][0] + tok[0, 0]).astype(BF16) for n in OTHER_BIG] + [small_pack + tok[0, 0]]
              + [(p[n][1] + tok[0, 0]).astype(BF16) for n in BIG_SHARDED])
    gath, tok = _split_start(_plan_gather, shards, [(N_DEV,) + t.shape for t in shards], 7 * len(shards), "gather_start")

    def relayed(lo, hi, after, name, started=None):
        started = gath if started is None else started
        n = (len(started) - 3) // 2
        sems, srcs, lands = started[:3], started[3:3 + n], started[3 + n:]
        plan = functools.partial(_plan_gather, first=lo)
        state, tok = _split_relay(plan, (*sems, *srcs[lo:hi], *lands[lo:hi]), after, name + "_relay")
        return (plan, state, name), tok

    def arrived(relay, after):
        plan, state, name = relay
        return _split_wait(plan, state, after, name + "_wait")

    def gathered(lo, hi, after, name, started=None):
        relay, tok = relayed(lo, hi, after, name, started)
        return arrived(relay, tok)

    later = dict(p, **{n: p[n] + tok[0, 0] for n in ('norm_w', 's5_log_step', 'sgu_b', 'm2_d')})
    preps = [_layer_prep(i, later) for i in range(DEPTH)]
    h0 = _rmsnorm_fwd(x[0], preps[0][0]['nw'], "rms_fwd0")
    got = gathered(0, 1, tok + (preps[0][1] + preps[1][1] + h0[0, 0].astype(F32)), "gather_w_in0", gath_first)
    small_full = {}

    def other_weights0(proj):
        got = gathered(0, 4, proj, "gather_rest0")
        small_all, off = got[-1].reshape(N_DEV, -1), 0
        for n, sz in zip(SMALL_SHARDED, small_sizes):
            small_full[n] = _join8(small_all[:, off:off + sz].reshape((N_DEV,) + p[n].shape), SHARD_AXIS[n])
            off += sz
        return dict(_other_weights(got[:-1]), **{n: small_full[n][0] for n in SMALL_SHARDED})

    saved, layer_g, full = [None] * DEPTH, [None] * DEPTH, [None] * DEPTH
    relay1 = []

    def relay_layer1(ys):
        relay, tok = relayed(4, 8, ys, "gather1")
        relay1.append(relay)
        return tok

    xs, saved[0], full[0] = _layer_fwd(x[0], h0, 0, preps[0][0], _relayout_w_in(got[0], "relayout_w_in0"), other_weights0,
                                       relay_layer1)
    h1 = _rmsnorm_fwd(xs, preps[1][0]['nw'], "rms_fwd1")
    got = arrived(relay1[0], h1)
    xs, saved[1], full[1] = _layer_fwd(
        xs, h1, 1, preps[1][0], _relayout_w_in(got[0], "relayout_w_in1"),
        lambda proj: dict(_other_weights(got[1:]), **{n: small_full[n][1] for n in SMALL_SHARDED}))
    loss_row, dx, dfw = _loss_head(xs, final_norm_w.reshape(1, D_MODEL), loss_target[0])
    loss = lax.psum(loss_row[0, 0], ("x", "y", "c"))
    loss, dx = lax.optimization_barrier((loss, dx))
    pairs, scat, sent0 = [None] * DEPTH, [None] * DEPTH, []

    def start_pairs1(g):
        pairs[1], tok = _pair_start(_layer_grad_blocks(g, 1), 1)
        return tok

    def send_chip_sums1(dh):
        scat[1], tok = _chips_start(_pair_sums(pairs[1], dh, 1), 1)
        return tok

    def send_all0(g):
        pairs[0], tok = _pair_start(_layer_grad_blocks(g, 0), 0)
        scat[0], tok = _chips_start(_pair_sums(pairs[0], tok, 0), 0)
        sent0.append(tok)
        return tok

    dx, layer_g[1] = _layer_bwd(dx, 1, saved[1], full[1], on_large_grads=start_pairs1, after_dh=send_chip_sums1)
    dx, layer_g[0] = _layer_bwd(dx, 0, saved[0], full[0], on_large_grads=send_all0)
    for i in range(DEPTH):
        dense = layer_g[i].pop('s5_dense')
        blocks = tuple(_diag_blocks(t, after=sent0[0]) for t in dense[:4])
        layer_g[i].update(zip(S5_NAMES, saved[i]['disc_vjp'](blocks + (dense[4] + sent0[0][0, 0], dense[5]))))
    grads = {n: jnp.stack([layer_g[i][n] for i in range(DEPTH)]) for n in SMALL_SHARDED + REPLICATED if n != 'final_norm_w'}
    grads['final_norm_w'] = dfw[0]

    out_g, out_d, out_m, out_v = {}, {}, {}, {}
    repl_rows = _pack_rows([grads[n] for n in REPLICATED], 8 * N_DEV)
    rr = repl_rows.shape[0] // N_DEV
    shard_rows = _pack_rows([_split8(grads[n], SHARD_AXIS[n]) for n in SMALL_SHARDED], 8, batched=True)
    rs = shard_rows.shape[1]
    small_g = jnp.concatenate([shard_rows, repl_rows.reshape(N_DEV, rr, LANES)], axis=1)
    all_to_all, all_gather = functools.partial(_plan_direct, gather=False), functools.partial(_plan_direct, gather=True)
    small_state, tok = _split_start(all_to_all, [small_g], [small_g.shape], N_DEV - 1, "scatter_small_start")
    landed1 = _split_wait(_plan_chips, scat[1], tok, "chips1_wait")
    landed0 = _split_wait(_plan_chips, scat[0], landed1[0], "chips0_wait")

    def big_adamw(n, after=None):
        k, shp = BIG_SHARDED.index(n), p[n].shape
        if n == 'w_in':
            return _adamw_w_in([landed0[k], landed1[k]], p[n], mom[n], vel[n], "adamw_w_in", after)
        c = shp[-1]
        r = p[n].size // (DEPTH * c)
        res = _adamw([landed0[k], landed1[k]], *[d[n].reshape(DEPTH, r, c) for d in (p, mom, vel)],
                     {'w_branch': 512, 'w_out': 128, 's5_w_glu': 64}[n], "adamw_" + n)
        return [o.reshape(shp) for o in res]

    for n in OTHER_BIG:
        out_g[n], out_d[n], out_m[n], out_v[n] = big_adamw(n)
    updated = sum(out_d[n].reshape(-1)[0] for n in OTHER_BIG).reshape(1, 1)
    small_sum = _slot_sum(_split_wait(all_to_all, small_state, updated, "scatter_small_wait")[0], "sum_small")
    repl_part = small_sum[rs:]
    repl_state, tok = _split_start(all_gather, [repl_part], [(N_DEV,) + repl_part.shape], N_DEV - 1, "gather_small_start")
    out_g['w_in'], out_d['w_in'], out_m['w_in'], out_v['w_in'] = big_adamw('w_in', tok)
    repl_all = _split_wait(all_gather, repl_state, out_d['w_in'], "gather_small_wait")[0].reshape(N_DEV * rr, LANES)
    g_all = jnp.concatenate([small_sum[:rs], repl_all], axis=0)
    names = SMALL_SHARDED + REPLICATED
    pieces = (_unpack_rows(g_all[:rs], [p[n].shape for n in SMALL_SHARDED])
              + _unpack_rows(g_all[rs:], [p[n].shape for n in REPLICATED]))
    out_g.update(zip(names, pieces))
    res = _adamw_many(*[[_memory_view(n, d[n]) for n in names] for d in (out_g, p, mom, vel)], "adamw_small")
    for r, dst in zip(res, (out_d, out_m, out_v)):
        dst.update({n: _memory_view(n, t) for n, t in zip(names, r)})
    return (loss, dx[None], *[out_g[n] for n in WEIGHTS], *[out_d[n] for n in WEIGHTS],
            *[out_m[n] for n in WEIGHTS], *[out_v[n] for n in WEIGHTS])
```

```python
import functools

import jax
import jax.numpy as jnp
import numpy as np
from jax import lax
from jax.experimental import pallas as pl
from jax.experimental.pallas import tpu as pltpu

F32 = jnp.float32
BF16 = jnp.bfloat16

N_DEV = 8
SEQ = 2048
D_MODEL = 1024
DEPTH = 2
BW = 512
N_BRANCH = 4
EPS = 1e-6
S5_GROUPS, S5_STATE, S5_P = 32, 64, 16
S5_CH = S5_GROUPS * S5_STATE
SGU_CHUNK, SGU_HEADS = 128, 8
M2_HEADS, M2_HEAD_DIM, M2_STATE, M2_CHUNK, M2_CONV = 8, 64, 128, 128, 4
M2_CONV_CH = 1024
SC_CONV = 3
IN_DIM = 10248
IN_PAD = 11264
C_MERGE = 0
C_S5U, C_S5G = 4096, 4608
C_M2X = 5120
C_SGU_U, C_SGU_V, C_SGU_G = 6144, 6656, 7168
C_M2Z, C_DT = 8192, 8704
C_SC = 9216
SHARD_IN = IN_DIM // N_DEV

ADAM_LR, ADAM_B1, ADAM_B2, ADAM_EPS, ADAM_WD, ADAM_STEP = 0.001, 0.9, 0.999, 1e-08, 0.01, 10

VMEM_LIMIT = 56 * 1024 * 1024
LANES = 128

MESH = pl.DeviceIdType.MESH


def _cparams(sem=None, **kw):
    return pltpu.CompilerParams(dimension_semantics=sem, vmem_limit_bytes=VMEM_LIMIT, **kw)


def _dg(a, b, ca, cb, precision=None):
    return lax.dot_general(a, b, (((ca,), (cb,)), ((), ())), precision=precision,
                           preferred_element_type=F32)


@functools.partial(jax.custom_vjp, nondiff_argnums=(2, 3))
def _bdot(a, b, ca, cb):
    return _dg(a.astype(BF16), b.astype(BF16), ca, cb)


def _bdot_fwd(a, b, ca, cb):
    return _bdot(a, b, ca, cb), (a, b)


def _bdot_bwd(ca, cb, res, g):
    a, b = res
    gb, ab, bb = g.astype(BF16), a.astype(BF16), b.astype(BF16)
    da = _dg(gb, bb, 1, 1 - cb) if ca == 1 else _dg(bb, gb, 1 - cb, 1)
    db = _dg(ab, gb, 1 - ca, 0) if cb == 0 else _dg(gb, ab, 0, 1 - ca)
    return da.astype(a.dtype), db.astype(b.dtype)


_bdot.defvjp(_bdot_fwd, _bdot_bwd)


def _rms(x, w):
    return x * lax.rsqrt(jnp.mean(x * x, axis=-1, keepdims=True) + EPS) * w


def _silu(x):
    return x * jax.nn.sigmoid(x)


def _gelu(x):
    return 0.5 * x * (1.0 + jnp.tanh(0.7978845608028654 * (x + 0.044715 * (x * x * x))))


def _softplus(x):
    return jnp.maximum(x, 0.0) + jnp.log1p(jnp.exp(-jnp.abs(x)))


def _shift_down(x, s):
    if s == 0:
        return x
    row = lax.broadcasted_iota(jnp.int32, x.shape, 0)
    return jnp.where(row >= s, pltpu.roll(x, s, 0), 0.0)


def _shift_up(x, s):
    if s == 0:
        return x
    n = x.shape[0]
    row = lax.broadcasted_iota(jnp.int32, x.shape, 0)
    return jnp.where(row < n - s, pltpu.roll(x, n - s, 0), 0.0)


def _matmul(a, b, ca, cb, out_dtype, tm, tn, tk, name, residual=None, after=None):
    m = a.shape[1 - ca]
    k = a.shape[ca]
    n = b.shape[1 - cb]
    assert b.shape[cb] == k and m % tm == 0 and n % tn == 0 and k % tk == 0
    nk = k // tk
    a_spec = pl.BlockSpec((tm, tk), lambda i, j, kk: (i, kk)) if ca == 1 else pl.BlockSpec((tk, tm), lambda i, j, kk: (kk, i))
    b_spec = pl.BlockSpec((tk, tn), lambda i, j, kk: (kk, j)) if cb == 0 else pl.BlockSpec((tn, tk), lambda i, j, kk: (j, kk))
    o_spec = pl.BlockSpec((tm, tn), lambda i, j, kk: (i, j))
    has_res = residual is not None

    def body(*refs):
        refs = refs[:2 + has_res] + refs[2 + has_res + (after is not None):]
        if has_res:
            a_ref, b_ref, r_ref, o_ref, acc = refs
        else:
            a_ref, b_ref, o_ref, acc = refs
        kk = pl.program_id(2)
        part = _dg(a_ref[...].astype(BF16), b_ref[...].astype(BF16), ca, cb)

        @pl.when(kk == 0)
        def _():
            acc[...] = part

        @pl.when(kk > 0)
        def _():
            acc[...] += part

        @pl.when(kk == nk - 1)
        def _():
            r = acc[...]
            if has_res:
                r = r + r_ref[...]
            o_ref[...] = r.astype(out_dtype)

    ins = [a, b] + ([residual] if has_res else []) + ([after] if after is not None else [])
    specs = [a_spec, b_spec] + ([o_spec] if has_res else []) + ([pl.BlockSpec(memory_space=pl.ANY)] if after is not None else [])
    return pl.pallas_call(
        body, name=name, grid=(m // tm, n // tn, nk), in_specs=specs, out_specs=o_spec,
        out_shape=jax.ShapeDtypeStruct((m, n), out_dtype),
        scratch_shapes=[pltpu.VMEM((tm, tn), F32)],
        compiler_params=_cparams(("parallel", "parallel", "arbitrary")),
    )(*ins)


ROW_TILE = 512


def _rmsnorm_fwd(x, w, name):
    def body(x_ref, w_ref, o_ref):
        o_ref[...] = _rms(x_ref[...], w_ref[...]).astype(BF16)

    return pl.pallas_call(
        body, name=name, grid=(SEQ // ROW_TILE,),
        in_specs=[pl.BlockSpec((ROW_TILE, D_MODEL), lambda i: (i, 0)), pl.BlockSpec((1, D_MODEL), lambda i: (0, 0))],
        out_specs=pl.BlockSpec((ROW_TILE, D_MODEL), lambda i: (i, 0)),
        out_shape=jax.ShapeDtypeStruct((SEQ, D_MODEL), BF16),
        compiler_params=_cparams(("parallel",)),
    )(x, w)


def _rmsnorm_bwd(x, w, dh, dres, name):
    def body(x_ref, w_ref, dh_ref, dres_ref, dx_ref, dw_ref):
        _, vjp = jax.vjp(_rms, x_ref[...], w_ref[...])
        dx, dw = vjp(dh_ref[...])
        dx_ref[...] = dx + dres_ref[...]

        @pl.when(pl.program_id(0) == 0)
        def _():
            dw_ref[...] = dw

        @pl.when(pl.program_id(0) > 0)
        def _():
            dw_ref[...] += dw

    tile = pl.BlockSpec((ROW_TILE, D_MODEL), lambda i: (i, 0))
    vec = pl.BlockSpec((1, D_MODEL), lambda i: (0, 0))
    return pl.pallas_call(
        body, name=name, grid=(SEQ // ROW_TILE,),
        in_specs=[tile, vec, tile, tile], out_specs=[tile, vec],
        out_shape=[jax.ShapeDtypeStruct((SEQ, D_MODEL), F32), jax.ShapeDtypeStruct((1, D_MODEL), F32)],
        compiler_params=_cparams(("arbitrary",)),
    )(x, w, dh, dres)


def _loss_head(x, w, target):
    def body(x_ref, w_ref, t_ref, loss_ref, dx_ref, dw_ref):
        tgt = t_ref[...]

        def f(xv, wv):
            err = _rms(xv, wv) - tgt
            return 0.5 * jnp.sum(jnp.mean(err * err, axis=-1))

        loss, vjp = jax.vjp(f, x_ref[...], w_ref[...])
        dx, dw = vjp(jnp.ones((), F32))
        dx_ref[...] = dx
        lrow = jnp.full((1, LANES), loss, F32)

        @pl.when(pl.program_id(0) == 0)
        def _():
            dw_ref[...] = dw
            loss_ref[...] = lrow

        @pl.when(pl.program_id(0) > 0)
        def _():
            dw_ref[...] += dw
            loss_ref[...] += lrow

    tile = pl.BlockSpec((ROW_TILE, D_MODEL), lambda i: (i, 0))
    vec = pl.BlockSpec((1, D_MODEL), lambda i: (0, 0))
    return pl.pallas_call(
        body, name="loss_head", grid=(SEQ // ROW_TILE,),
        in_specs=[tile, vec, tile], out_specs=[pl.BlockSpec((1, LANES), lambda i: (0, 0)), tile, vec],
        out_shape=[jax.ShapeDtypeStruct((1, LANES), F32), jax.ShapeDtypeStruct((SEQ, D_MODEL), F32),
                   jax.ShapeDtypeStruct((1, D_MODEL), F32)],
        compiler_params=_cparams(("arbitrary",)),
    )(x, w, target)


S5_T = 256
S5_BLOCKS = [(slice(j * 256, (j + 1) * 256), slice(j * 1024, (j + 1) * 1024)) for j in range(2)]


def _s5_post(ypre, gate, wglu):
    y = _gelu(ypre)
    y = y * jax.nn.sigmoid(_bdot(y, wglu, 1, 0))
    return y * _silu(gate)


def _s5_fwd(proj, bbre, bbim, cre, cim, a2, dvec, wglu, name):
    def body(u_ref, g_ref, bbre_ref, bbim_ref, cre_ref, cim_ref, a_ref, d_ref, wg_ref, o_ref, sre_ref, sim_ref, st):
        @pl.when(pl.program_id(0) == 0)
        def _():
            st[...] = jnp.zeros_like(st)

        u = u_ref[...]
        ub = u.astype(BF16)
        for us, ss in S5_BLOCKS:
            sre_ref[:, ss] = _dg(ub[:, us], bbre_ref[us, ss], 1, 0)
            sim_ref[:, ss] = _dg(ub[:, us], bbim_ref[us, ss], 1, 0)
        ar, ai = a_ref[0:1, :], a_ref[1:2, :]

        def step(t, carry):
            sr, si = carry
            nr = ar * sr - ai * si + sre_ref[pl.ds(t, 1), :]
            ni = ar * si + ai * sr + sim_ref[pl.ds(t, 1), :]
            sre_ref[pl.ds(t, 1), :] = nr
            sim_ref[pl.ds(t, 1), :] = ni
            return nr, ni

        sr, si = lax.fori_loop(0, S5_T, step, (st[0:1, :], st[1:2, :]), unroll=8)
        st[0:1, :] = sr
        st[1:2, :] = si
        ypre = jnp.concatenate(
            [_dg(sre_ref[:, ss].astype(BF16), cre_ref[ss, us], 1, 0) - _dg(sim_ref[:, ss].astype(BF16), cim_ref[ss, us], 1, 0)
             for us, ss in S5_BLOCKS], axis=1) + d_ref[...] * u
        o_ref[0] = _s5_post(ypre, g_ref[...], wg_ref[...]).astype(BF16)

    full = lambda shape: pl.BlockSpec(shape, lambda c: (0, 0))
    return pl.pallas_call(
        body, name=name, grid=(SEQ // S5_T,),
        in_specs=[pl.BlockSpec((S5_T, BW), lambda c: (c, C_S5U // BW)), pl.BlockSpec((S5_T, BW), lambda c: (c, C_S5G // BW)),
                  full((BW, S5_CH)), full((BW, S5_CH)), full((S5_CH, BW)), full((S5_CH, BW)),
                  full((2, S5_CH)), full((1, BW)), full((BW, BW))],
        out_specs=[pl.BlockSpec((1, S5_T, BW), lambda c: (0, c, 0)), pl.BlockSpec((S5_T, S5_CH), lambda c: (c, 0)),
                   pl.BlockSpec((S5_T, S5_CH), lambda c: (c, 0))],
        out_shape=[jax.ShapeDtypeStruct((N_BRANCH, SEQ, BW), BF16), jax.ShapeDtypeStruct((SEQ, S5_CH), F32),
                   jax.ShapeDtypeStruct((SEQ, S5_CH), F32)],
        scratch_shapes=[pltpu.VMEM((2, S5_CH), F32)],
        compiler_params=_cparams(("arbitrary",)),
    )(proj, proj, bbre, bbim, cre, cim, a2, dvec, wglu)


def _s5_bwd(proj, dproj, dout, sre, sim, bbre, bbim, cre, cim, a2, dvec, wglu, name):
    nc = SEQ // S5_T

    def body(u_ref, g_ref, do_ref, sre_ref, sim_ref, pre_ref, pim_ref, bbre_ref, bbim_ref, cre_ref, cim_ref, a_ref,
             d_ref, wg_ref, dproj_in, dp_ref, dbbre_ref, dbbim_ref, dcre_ref, dcim_ref, da_ref, dd_ref, dwg_ref,
             gre, gim, st):
        c = nc - 1 - pl.program_id(0)

        @pl.when(pl.program_id(0) == 0)
        def _():
            st[...] = jnp.zeros_like(st)
            for r in (dbbre_ref, dbbim_ref, dcre_ref, dcim_ref, da_ref, dd_ref, dwg_ref):
                r[...] = jnp.zeros_like(r)

        u = u_ref[...]
        s_re, s_im = sre_ref[...], sim_ref[...]

        def head(s_res, s_ims, cres, cims, dv, uv, gv, wg):
            ypre = jnp.concatenate([_bdot(sr, cr, 1, 0) - _bdot(si, ci, 1, 0)
                                    for sr, si, cr, ci in zip(s_res, s_ims, cres, cims)], axis=1) + dv * uv
            return _s5_post(ypre, gv, wg)

        _, vjp = jax.vjp(head, [sre_ref[:, ss] for _, ss in S5_BLOCKS], [sim_ref[:, ss] for _, ss in S5_BLOCKS],
                         [cre_ref[ss, us].astype(F32) for us, ss in S5_BLOCKS],
                         [cim_ref[ss, us].astype(F32) for us, ss in S5_BLOCKS],
                         d_ref[...], u, g_ref[...], wg_ref[...].astype(F32))
        ds_res, ds_ims, dcres, dcims, dd, du_d, dgate, dwg = vjp(do_ref[0])
        for k, (us, ss) in enumerate(S5_BLOCKS):
            dcre_ref[ss, us] += dcres[k]
            dcim_ref[ss, us] += dcims[k]
            gre[:, ss] = ds_res[k]
            gim[:, ss] = ds_ims[k]
        dd_ref[...] += dd
        dwg_ref[...] += dwg
        dp_ref[:, BW:] = dgate.astype(BF16)
        ar, ai = a_ref[0:1, :], a_ref[1:2, :]

        def step(i, carry):
            t = S5_T - 1 - i
            gr, gi = carry
            nr = gre[pl.ds(t, 1), :] + gr
            ni = gim[pl.ds(t, 1), :] + gi
            gre[pl.ds(t, 1), :] = nr
            gim[pl.ds(t, 1), :] = ni
            return ar * nr + ai * ni, ar * ni - ai * nr

        gr, gi = lax.fori_loop(0, S5_T, step, (st[0:1, :], st[1:2, :]), unroll=8)
        st[0:1, :] = gr
        st[1:2, :] = gi
        g_re, g_im = gre[...], gim[...]
        first = jnp.where(c > 0, 1.0, 0.0)
        row = lax.broadcasted_iota(jnp.int32, (S5_T, S5_CH), 0)
        p_re = jnp.where(row == 0, pre_ref[7:8, :] * first, pltpu.roll(s_re, 1, 0))
        p_im = jnp.where(row == 0, pim_ref[7:8, :] * first, pltpu.roll(s_im, 1, 0))
        da_ref[0:1, :] += jnp.sum(g_re * p_re + g_im * p_im, axis=0, keepdims=True)
        da_ref[1:2, :] += jnp.sum(g_im * p_re - g_re * p_im, axis=0, keepdims=True)
        ub, grb, gib = u.astype(BF16), g_re.astype(BF16), g_im.astype(BF16)
        du_s = []
        for us, ss in S5_BLOCKS:
            dbbre_ref[us, ss] += _dg(ub[:, us], grb[:, ss], 0, 0)
            dbbim_ref[us, ss] += _dg(ub[:, us], gib[:, ss], 0, 0)
            du_s.append(_dg(grb[:, ss], bbre_ref[us, ss], 1, 1) + _dg(gib[:, ss], bbim_ref[us, ss], 1, 1))
        dp_ref[:, :BW] = (du_d + jnp.concatenate(du_s, axis=1)).astype(BF16)

    full = lambda shape: pl.BlockSpec(shape, lambda i: (0, 0))
    rev = lambda w, col=0: pl.BlockSpec((S5_T, w), lambda i: (nc - 1 - i, col))
    prev = pl.BlockSpec((8, S5_CH), lambda i: (jnp.maximum((nc - 1 - i) * (S5_T // 8) - 1, 0), 0))
    return pl.pallas_call(
        body, name=name, grid=(nc,),
        in_specs=[rev(BW, C_S5U // BW), rev(BW, C_S5G // BW), pl.BlockSpec((1, S5_T, BW), lambda i: (0, nc - 1 - i, 0)),
                  rev(S5_CH), rev(S5_CH), prev, prev,
                  full((BW, S5_CH)), full((BW, S5_CH)), full((S5_CH, BW)), full((S5_CH, BW)),
                  full((2, S5_CH)), full((1, BW)), full((BW, BW)), pl.BlockSpec(memory_space=pl.ANY)],
        out_specs=[rev(2 * BW, C_S5U // (2 * BW)), full((BW, S5_CH)), full((BW, S5_CH)), full((S5_CH, BW)), full((S5_CH, BW)),
                   full((2, S5_CH)), full((1, BW)), full((BW, BW))],
        input_output_aliases={14: 0},
        out_shape=[jax.ShapeDtypeStruct((SEQ, IN_PAD), BF16),
                   jax.ShapeDtypeStruct((BW, S5_CH), F32), jax.ShapeDtypeStruct((BW, S5_CH), F32),
                   jax.ShapeDtypeStruct((S5_CH, BW), F32), jax.ShapeDtypeStruct((S5_CH, BW), F32),
                   jax.ShapeDtypeStruct((2, S5_CH), F32), jax.ShapeDtypeStruct((1, BW), F32),
                   jax.ShapeDtypeStruct((BW, BW), F32)],
        scratch_shapes=[pltpu.VMEM((S5_T, S5_CH), F32), pltpu.VMEM((S5_T, S5_CH), F32), pltpu.VMEM((2, S5_CH), F32)],
        compiler_params=_cparams(("arbitrary",)),
    )(proj, proj, dout, sre, sim, sre, sim, bbre, bbim, cre, cim, a2, dvec, wglu, dproj)


def _diag_blocks(dense, after=None):
    rows, cols = dense.shape
    rows_per, cols_per = rows // S5_GROUPS, cols // S5_GROUPS
    per_lane_block = LANES // cols_per
    tile = 512

    def body(d_ref, *rest):
        o_ref = rest[-1]
        r0 = pl.program_id(0) * tile
        grp = (r0 + lax.broadcasted_iota(jnp.int32, (tile, LANES), 0)) // rows_per
        lane = lax.broadcasted_iota(jnp.int32, (tile, LANES), 1)
        acc = jnp.zeros((tile, LANES), F32)
        for hb in range(cols // LANES):
            acc = acc + jnp.where(grp == per_lane_block * hb + lane // cols_per, d_ref[:, hb * LANES:(hb + 1) * LANES], 0.0)
        shift = LANES // 2
        while shift >= cols_per:
            acc = acc + pltpu.roll(acc, LANES - shift, 1)
            shift //= 2
        o_ref[...] = acc

    folded = pl.pallas_call(
        body, name=f"diag_blocks_{rows_per}x{cols_per}", grid=(rows // tile,),
        in_specs=[pl.BlockSpec((tile, cols), lambda i: (i, 0))] + ([] if after is None else [pl.BlockSpec(memory_space=pl.ANY)]),
        out_specs=pl.BlockSpec((tile, LANES), lambda i: (i, 0)),
        out_shape=jax.ShapeDtypeStruct((rows, LANES), F32), compiler_params=_cparams(("parallel",)),
    )(dense, *([] if after is None else [after]))
    return folded[:, :cols_per].reshape(S5_GROUPS, rows_per, cols_per)


def _block_diag(t):
    g, rows_per, cols_per = t.shape
    wide = jnp.tile(t.reshape(g * rows_per, cols_per), (1, g))
    r = lax.broadcasted_iota(jnp.int32, wide.shape, 0) // rows_per
    c = lax.broadcasted_iota(jnp.int32, wide.shape, 1) // cols_per
    return jnp.where(r == c, wide, 0.0)


def _s5_disc(lam_re, lam_im, b_re, b_im, c_re, c_im, d, log_step):
    step = jnp.exp(log_step)[:, None]
    mag = jnp.exp(lam_re * step)
    ab_re, ab_im = mag * jnp.cos(lam_im * step), mag * jnp.sin(lam_im * step)
    den = lam_re * lam_re + lam_im * lam_im
    nr = ab_re - 1.0
    coef_re = (nr * lam_re + ab_im * lam_im) / den
    coef_im = (ab_im * lam_re - nr * lam_im) / den
    bb_re = coef_re[..., None] * b_re - coef_im[..., None] * b_im
    bb_im = coef_re[..., None] * b_im + coef_im[..., None] * b_re
    a2 = jnp.stack([ab_re.reshape(-1), ab_im.reshape(-1)])
    return (jnp.swapaxes(bb_re, 1, 2), jnp.swapaxes(bb_im, 1, 2),
            jnp.swapaxes(c_re, 1, 2), jnp.swapaxes(c_im, 1, 2),
            a2, d.reshape(1, BW))


def _left_lanes(shape):
    return lax.broadcasted_iota(jnp.int32, shape, 1) < 64


def _sgu_chunk(u, v, gate, ln_w, ln_b, w, bias):
    u32, v32 = _gelu(u), _gelu(v)
    mu = jnp.mean(v32, axis=-1, keepdims=True)
    var = jnp.mean(jnp.square(v32 - mu), axis=-1, keepdims=True)
    vn = (v32 - mu) * lax.rsqrt(var + EPS) * ln_w + ln_b
    t_i = lax.broadcasted_iota(jnp.int32, (SGU_CHUNK, SGU_CHUNK), 0)
    s_i = lax.broadcasted_iota(jnp.int32, (SGU_CHUNK, SGU_CHUNK), 1)
    causal = t_i >= s_i
    left = _left_lanes((SGU_CHUNK, LANES))
    sgate = _silu(gate)
    outs = []
    for j in range(BW // LANES):
        vb = vn[:, j * LANES:(j + 1) * LANES]
        s_blk = (_bdot(jnp.where(causal, w[2 * j], 0.0), jnp.where(left, vb, 0.0), 1, 0)
                 + _bdot(jnp.where(causal, w[2 * j + 1], 0.0), jnp.where(left, 0.0, vb), 1, 0))
        sl = slice(j * LANES, (j + 1) * LANES)
        outs.append(u32[:, sl] * (s_blk + bias[:, sl]) * sgate[:, sl])
    return outs


def _sgu_fwd(proj, ys, ln_w, ln_b, w, bias, name):
    def body(u_ref, v_ref, g_ref, lw_ref, lb_ref, w_ref, b_ref, ys_in, o_ref):
        outs = _sgu_chunk(u_ref[...], v_ref[...], g_ref[...], lw_ref[...], lb_ref[...], w_ref[...], b_ref[...])
        for j, o in enumerate(outs):
            o_ref[0, :, j * LANES:(j + 1) * LANES] = o.astype(BF16)

    blk = lambda col: pl.BlockSpec((SGU_CHUNK, BW), lambda c: (c, col // BW))
    vec = pl.BlockSpec((1, BW), lambda c: (0, 0))
    return pl.pallas_call(
        body, name=name, grid=(SEQ // SGU_CHUNK,),
        in_specs=[blk(C_SGU_U), blk(C_SGU_V), blk(C_SGU_G), vec, vec,
                  pl.BlockSpec((SGU_HEADS, SGU_CHUNK, SGU_CHUNK), lambda c: (0, 0, 0)),
                  pl.BlockSpec((SGU_CHUNK, BW), lambda c: (0, 0)), pl.BlockSpec(memory_space=pl.ANY)],
        out_specs=pl.BlockSpec((1, SGU_CHUNK, BW), lambda c: (1, c, 0)),
        out_shape=jax.ShapeDtypeStruct((N_BRANCH, SEQ, BW), BF16), input_output_aliases={7: 0},
        compiler_params=_cparams(("parallel",)),
    )(proj, proj, proj, ln_w, ln_b, w, bias, ys)


def _sgu_bwd(proj, dproj, dout, ln_w, ln_b, w, bias, name):
    def body(u_ref, v_ref, g_ref, do_ref, lw_ref, lb_ref, w_ref, b_ref, dproj_in, dp_ref, dlw_ref, dlb_ref, dw_ref, db_ref):
        _, vjp = jax.vjp(_sgu_chunk, u_ref[...], v_ref[...], g_ref[...], lw_ref[...], lb_ref[...], w_ref[...], b_ref[...])
        do = do_ref[0]
        du, dv, dgate, dlw, dlb, dw, db = vjp([do[:, j * LANES:(j + 1) * LANES] for j in range(BW // LANES)])
        dp_ref[:, 0:BW] = du.astype(BF16)
        dp_ref[:, BW:2 * BW] = dv.astype(BF16)
        dp_ref[:, 2 * BW:3 * BW] = dgate.astype(BF16)
        dp_ref[:, 3 * BW:] = jnp.zeros((SGU_CHUNK, BW), BF16)

        @pl.when(pl.program_id(0) == 0)
        def _():
            dlw_ref[...] = dlw
            dlb_ref[...] = dlb
            dw_ref[...] = dw
            db_ref[...] = db

        @pl.when(pl.program_id(0) > 0)
        def _():
            dlw_ref[...] += dlw
            dlb_ref[...] += dlb
            dw_ref[...] += dw
            db_ref[...] += db

    blk = lambda col: pl.BlockSpec((SGU_CHUNK, BW), lambda c: (c, col // BW))
    vec = pl.BlockSpec((1, BW), lambda c: (0, 0))
    wsp = pl.BlockSpec((SGU_HEADS, SGU_CHUNK, SGU_CHUNK), lambda c: (0, 0, 0))
    bsp = pl.BlockSpec((SGU_CHUNK, BW), lambda c: (0, 0))
    return pl.pallas_call(
        body, name=name, grid=(SEQ // SGU_CHUNK,),
        in_specs=[blk(C_SGU_U), blk(C_SGU_V), blk(C_SGU_G), pl.BlockSpec((1, SGU_CHUNK, BW), lambda c: (1, c, 0)),
                  vec, vec, wsp, bsp, pl.BlockSpec(memory_space=pl.ANY)],
        out_specs=[pl.BlockSpec((SGU_CHUNK, 4 * BW), lambda c: (c, C_SGU_U // (4 * BW))), vec, vec, wsp, bsp],
        input_output_aliases={8: 0},
        out_shape=[jax.ShapeDtypeStruct((SEQ, IN_PAD), BF16), jax.ShapeDtypeStruct((1, BW), F32),
                   jax.ShapeDtypeStruct((1, BW), F32), jax.ShapeDtypeStruct((SGU_HEADS, SGU_CHUNK, SGU_CHUNK), F32),
                   jax.ShapeDtypeStruct((SGU_CHUNK, BW), F32)],
        compiler_params=_cparams(("arbitrary",)),
    )(proj, proj, proj, dout, ln_w, ln_b, w, bias, dproj)


CONV_BLK = 256


def _m2_conv_fwd(proj, w, b, name):
    def body(x_ref, w_ref, b_ref, o_ref):
        x = x_ref[...]
        acc = jnp.zeros_like(x) + b_ref[...]
        for k in range(M2_CONV):
            acc = acc + w_ref[k:k + 1, :] * _shift_down(x, M2_CONV - 1 - k)
        o_ref[...] = _silu(acc)

    return pl.pallas_call(
        body, name=name, grid=(M2_CONV_CH // CONV_BLK,),
        in_specs=[pl.BlockSpec((SEQ, CONV_BLK), lambda j: (0, C_M2X // CONV_BLK + j)),
                  pl.BlockSpec((M2_CONV, CONV_BLK), lambda j: (0, j)), pl.BlockSpec((1, CONV_BLK), lambda j: (0, j))],
        out_specs=pl.BlockSpec((SEQ, CONV_BLK), lambda j: (0, j)),
        out_shape=jax.ShapeDtypeStruct((SEQ, M2_CONV_CH), F32),
        compiler_params=_cparams(("parallel",)),
    )(proj, w, b)


def _m2_conv_bwd(proj, dproj, dxa, w, b, name):
    def body(x_ref, d_ref, w_ref, b_ref, dproj_in, dx_ref, dw_ref, db_ref):
        x = x_ref[...]
        xs = [_shift_down(x, M2_CONV - 1 - k) for k in range(M2_CONV)]
        acc = jnp.zeros_like(x) + b_ref[...]
        for k in range(M2_CONV):
            acc = acc + w_ref[k:k + 1, :] * xs[k]
        sg = jax.nn.sigmoid(acc)
        dacc = d_ref[...] * (sg * (1.0 + acc * (1.0 - sg)))
        dx = jnp.zeros_like(x)
        for k in range(M2_CONV):
            dx = dx + w_ref[k:k + 1, :] * _shift_up(dacc, M2_CONV - 1 - k)
            dw_ref[k:k + 1, :] = jnp.sum(dacc * xs[k], axis=0, keepdims=True)
        dx_ref[...] = dx.astype(BF16)
        db_ref[...] = jnp.sum(dacc, axis=0, keepdims=True)

    return pl.pallas_call(
        body, name=name, grid=(M2_CONV_CH // CONV_BLK,),
        in_specs=[pl.BlockSpec((SEQ, CONV_BLK), lambda j: (0, C_M2X // CONV_BLK + j)),
                  pl.BlockSpec((SEQ, CONV_BLK), lambda j: (0, j)),
                  pl.BlockSpec((M2_CONV, CONV_BLK), lambda j: (0, j)), pl.BlockSpec((1, CONV_BLK), lambda j: (0, j)),
                  pl.BlockSpec(memory_space=pl.ANY)],
        out_specs=[pl.BlockSpec((SEQ, CONV_BLK), lambda j: (0, C_M2X // CONV_BLK + j)),
                   pl.BlockSpec((M2_CONV, CONV_BLK), lambda j: (0, j)), pl.BlockSpec((1, CONV_BLK), lambda j: (0, j))],
        input_output_aliases={4: 0},
        out_shape=[jax.ShapeDtypeStruct((SEQ, IN_PAD), BF16), jax.ShapeDtypeStruct((M2_CONV, M2_CONV_CH), F32),
                   jax.ShapeDtypeStruct((1, M2_CONV_CH), F32)],
        compiler_params=_cparams(("parallel",)),
    )(proj, dxa, w, b, dproj)


N_PAIR = M2_HEADS // 2
HI = lax.Precision.HIGHEST


def _col(a, h):
    lane = lax.broadcasted_iota(jnp.int32, a.shape, 1)
    return jnp.sum(jnp.where(lane == h, a, 0.0), axis=1, keepdims=True)


def _row(a, h):
    sub = lax.broadcasted_iota(jnp.int32, a.shape, 0)
    return jnp.sum(jnp.where(sub == h, a, 0.0), axis=0, keepdims=True)


def _ssd_chunk(xs, bms, cms, dtr, zs, states, dt_bias, a_log, dfs, nws):
    q = M2_CHUNK
    dt = _softplus(dtr + dt_bias)
    da = dt * (-jnp.exp(a_log))
    l_i = lax.broadcasted_iota(jnp.int32, (q, q), 0)
    s_i = lax.broadcasted_iota(jnp.int32, (q, q), 1)
    causal = l_i >= s_i
    tril = jnp.where(causal, 1.0, 0.0)
    a_cs = _dg(tril, da, 1, 0, HI)
    a_cs_t = _dg(da, tril, 0, 1, HI)
    a_end = _row(a_cs, q - 1)
    left = _left_lanes((q, LANES))
    left1 = _left_lanes((1, LANES))
    ys, nexts = [], []
    for j in range(N_PAIR):
        grp = j // 2
        bm, cm = bms[grp], cms[grp]
        h0, h1 = 2 * j, 2 * j + 1
        cb = _bdot(cm, bm, 1, 1)
        xdt = xs[j] * jnp.where(left, _col(dt, h0), _col(dt, h1))
        acs0, acs1 = _col(a_cs, h0), _col(a_cs, h1)
        y = _bdot(cm, states[j], 1, 0) * jnp.where(left, jnp.exp(acs0), jnp.exp(acs1))
        s_new = states[j] * jnp.where(left1, jnp.exp(_col(a_end, h0)), jnp.exp(_col(a_end, h1)))
        for h, acs, xh in ((h0, acs0, jnp.where(left, xdt, 0.0)), (h1, acs1, jnp.where(left, 0.0, xdt))):
            decay = jnp.exp(jnp.where(causal, acs - _row(a_cs_t, h), -jnp.inf))
            y = y + _bdot(cb * decay, xh, 1, 0)
            s_new = s_new + _bdot(bm * jnp.exp(_col(a_end, h) - acs), xh, 0, 0)
        ys.append((y + dfs[j] * xs[j]) * _silu(zs[j]))
        nexts.append(s_new)
    ssq = sum(jnp.sum(y * y, axis=-1, keepdims=True) for y in ys)
    scale = lax.rsqrt(ssq / BW + EPS)
    return [y * scale * nw for y, nw in zip(ys, nws)], nexts


def _blocks(ref, n, width=LANES):
    return [ref[:, j * width:(j + 1) * width] for j in range(n)]


def _ssd_fwd(proj, ys, xa, dt_bias, a_log, dfull, nw, name):
    nc = SEQ // M2_CHUNK

    def body(x_ref, b_ref, c_ref, dt_ref, z_ref, dtb_ref, al_ref, df_ref, nw_ref, ys_in, o_ref, sin_ref, st):
        @pl.when(pl.program_id(0) == 0)
        def _():
            st[...] = jnp.zeros_like(st)

        states = [st[j] for j in range(N_PAIR)]
        for j in range(N_PAIR):
            sin_ref[0, j] = states[j]
        ys, nexts = _ssd_chunk(_blocks(x_ref, 4), _blocks(b_ref, 2), _blocks(c_ref, 2), dt_ref[...], _blocks(z_ref, 4),
                               states, dtb_ref[...], al_ref[...], _blocks(df_ref, 4), _blocks(nw_ref, 4))
        for j in range(N_PAIR):
            o_ref[0, :, j * LANES:(j + 1) * LANES] = ys[j].astype(BF16)
            st[j] = nexts[j]

    vec8 = pl.BlockSpec((1, LANES), lambda c: (0, 0))
    vec = pl.BlockSpec((1, BW), lambda c: (0, 0))
    return pl.pallas_call(
        body, name=name, grid=(nc,),
        in_specs=[pl.BlockSpec((M2_CHUNK, BW), lambda c: (c, 0)), pl.BlockSpec((M2_CHUNK, 256), lambda c: (c, 2)),
                  pl.BlockSpec((M2_CHUNK, 256), lambda c: (c, 3)), pl.BlockSpec((M2_CHUNK, LANES), lambda c: (c, C_DT // LANES)),
                  pl.BlockSpec((M2_CHUNK, BW), lambda c: (c, C_M2Z // BW)), vec8, vec8, vec, vec,
                  pl.BlockSpec(memory_space=pl.ANY)],
        out_specs=[pl.BlockSpec((1, M2_CHUNK, BW), lambda c: (2, c, 0)),
                   pl.BlockSpec((1, N_PAIR, M2_STATE, LANES), lambda c: (c, 0, 0, 0))],
        out_shape=[jax.ShapeDtypeStruct((N_BRANCH, SEQ, BW), BF16), jax.ShapeDtypeStruct((nc, N_PAIR, M2_STATE, LANES), F32)],
        input_output_aliases={9: 0},
        scratch_shapes=[pltpu.VMEM((N_PAIR, M2_STATE, LANES), F32)],
        compiler_params=_cparams(("arbitrary",)),
    )(xa, xa, xa, proj, proj, dt_bias, a_log, dfull, nw, ys)


def _ssd_bwd(proj, dproj, xa, dout, s_in, dt_bias, a_log, dfull, nw, name):
    nc = SEQ // M2_CHUNK

    def body(x_ref, b_ref, c_ref, dt_ref, z_ref, do_ref, sin_ref, dtb_ref, al_ref, df_ref, nw_ref, dproj_in,
             dp_ref, dxa_ref, ddtb_ref, dal_ref, ddf_ref, dnw_ref, dst):
        @pl.when(pl.program_id(0) == 0)
        def _():
            dst[...] = jnp.zeros_like(dst)
            for r in (ddtb_ref, dal_ref, ddf_ref, dnw_ref):
                r[...] = jnp.zeros_like(r)

        states = [sin_ref[0, j] for j in range(N_PAIR)]
        _, vjp = jax.vjp(_ssd_chunk, _blocks(x_ref, 4), _blocks(b_ref, 2), _blocks(c_ref, 2), dt_ref[...],
                         _blocks(z_ref, 4), states, dtb_ref[...], al_ref[...], _blocks(df_ref, 4), _blocks(nw_ref, 4))
        dxs, dbs, dcs, ddt, dzs, dstates, ddtb, dal, ddfs, dnws = vjp(
            ([do_ref[0, :, j * LANES:(j + 1) * LANES] for j in range(N_PAIR)], [dst[j] for j in range(N_PAIR)]))
        for j in range(N_PAIR):
            sl = slice(j * LANES, (j + 1) * LANES)
            dxa_ref[:, sl] = dxs[j]
            dp_ref[:, sl] = dzs[j].astype(BF16)
            dst[j] = dstates[j]
            ddf_ref[:, sl] += ddfs[j]
            dnw_ref[:, sl] += dnws[j]
        for g in range(2):
            dxa_ref[:, BW + g * LANES:BW + (g + 1) * LANES] = dbs[g]
            dxa_ref[:, BW + 256 + g * LANES:BW + 256 + (g + 1) * LANES] = dcs[g]
        dp_ref[:, BW:BW + LANES] = ddt.astype(BF16)
        dp_ref[:, BW + LANES:] = jnp.zeros((M2_CHUNK, 2 * BW - BW - LANES), BF16)
        ddtb_ref[...] += ddtb
        dal_ref[...] += dal

    rev = lambda w, col=0: pl.BlockSpec((M2_CHUNK, w), lambda i: (nc - 1 - i, col))
    vec8 = pl.BlockSpec((1, LANES), lambda i: (0, 0))
    vec = pl.BlockSpec((1, BW), lambda i: (0, 0))
    return pl.pallas_call(
        body, name=name, grid=(nc,),
        in_specs=[rev(BW), rev(256, 2), rev(256, 3), rev(LANES, C_DT // LANES), rev(BW, C_M2Z // BW),
                  pl.BlockSpec((1, M2_CHUNK, BW), lambda i: (2, nc - 1 - i, 0)),
                  pl.BlockSpec((1, N_PAIR, M2_STATE, LANES), lambda i: (nc - 1 - i, 0, 0, 0)), vec8, vec8, vec, vec,
                  pl.BlockSpec(memory_space=pl.ANY)],
        out_specs=[rev(2 * BW, C_M2Z // (2 * BW)), rev(M2_CONV_CH), vec8, vec8, vec, vec],
        input_output_aliases={11: 0},
        out_shape=[jax.ShapeDtypeStruct((SEQ, IN_PAD), BF16), jax.ShapeDtypeStruct((SEQ, M2_CONV_CH), F32),
                   jax.ShapeDtypeStruct((1, LANES), F32), jax.ShapeDtypeStruct((1, LANES), F32),
                   jax.ShapeDtypeStruct((1, BW), F32), jax.ShapeDtypeStruct((1, BW), F32)],
        scratch_shapes=[pltpu.VMEM((N_PAIR, M2_STATE, LANES), F32)],
        compiler_params=_cparams(("arbitrary",)),
    )(xa, xa, xa, proj, proj, dout, s_in, dt_bias, a_log, dfull, nw, dproj)


def _sc_specs():
    col = lambda kind: pl.BlockSpec((SEQ, LANES), lambda j: (0, C_SC // LANES + 4 * j + kind))
    return [col(0), col(1), col(2), col(3)]


def _sc_fwd(proj, ys, w, name):
    def body(b_ref, c_ref, h_ref, g_ref, w_ref, ys_in, o_ref):
        ch = c_ref[...] * h_ref[...]
        acc = jnp.zeros_like(ch)
        for k in range(SC_CONV):
            acc = acc + w_ref[k:k + 1, :] * _shift_down(ch, SC_CONV - 1 - k)
        o_ref[0] = (b_ref[...] * acc * _silu(g_ref[...])).astype(BF16)

    return pl.pallas_call(
        body, name=name, grid=(BW // LANES,),
        in_specs=_sc_specs() + [pl.BlockSpec((SC_CONV, LANES), lambda j: (0, j)), pl.BlockSpec(memory_space=pl.ANY)],
        out_specs=pl.BlockSpec((1, SEQ, LANES), lambda j: (3, 0, j)),
        out_shape=jax.ShapeDtypeStruct((N_BRANCH, SEQ, BW), BF16), input_output_aliases={5: 0},
        compiler_params=_cparams(("parallel",)),
    )(proj, proj, proj, proj, w, ys)


def _sc_bwd(proj, dproj, dout, w, name):
    def body(b_ref, c_ref, h_ref, g_ref, do_ref, w_ref, dproj_in, dp_ref, dw_ref):
        cv, hv, gv = c_ref[...], h_ref[...], g_ref[...]
        ch = cv * hv
        chs = [_shift_down(ch, SC_CONV - 1 - k) for k in range(SC_CONV)]
        acc = jnp.zeros_like(ch)
        for k in range(SC_CONV):
            acc = acc + w_ref[k:k + 1, :] * chs[k]
        sg = jax.nn.sigmoid(gv)
        do = do_ref[0]
        bv = b_ref[...]
        dp_ref[:, 0:LANES] = (do * acc * (gv * sg)).astype(BF16)
        dp_ref[:, 3 * LANES:] = (do * bv * acc * (sg * (1.0 + gv * (1.0 - sg)))).astype(BF16)
        dacc = do * bv * (gv * sg)
        dch = jnp.zeros_like(ch)
        for k in range(SC_CONV):
            dch = dch + w_ref[k:k + 1, :] * _shift_up(dacc, SC_CONV - 1 - k)
            dw_ref[k:k + 1, :] = jnp.sum(dacc * chs[k], axis=0, keepdims=True)
        dp_ref[:, LANES:2 * LANES] = (dch * hv).astype(BF16)
        dp_ref[:, 2 * LANES:3 * LANES] = (dch * cv).astype(BF16)

    wsp = pl.BlockSpec((SC_CONV, LANES), lambda j: (0, j))
    return pl.pallas_call(
        body, name=name, grid=(BW // LANES,),
        in_specs=_sc_specs() + [pl.BlockSpec((1, SEQ, LANES), lambda j: (3, 0, j)), wsp, pl.BlockSpec(memory_space=pl.ANY)],
        out_specs=[pl.BlockSpec((SEQ, 4 * LANES), lambda j: (0, C_SC // (4 * LANES) + j)), wsp],
        input_output_aliases={6: 0},
        out_shape=[jax.ShapeDtypeStruct((SEQ, IN_PAD), BF16), jax.ShapeDtypeStruct((SC_CONV, BW), F32)],
        compiler_params=_cparams(("parallel",)),
    )(proj, proj, proj, proj, dout, w, dproj)


MERGE_T = 256
MERGE_BWD_T = 1024


def _merge_fwd(proj, ys, merge_b, w_branch, name):
    def body(y_ref, lg_ref, b_ref, w_ref, o_ref):
        acc = jnp.zeros((MERGE_T, D_MODEL), F32)
        for k in range(N_BRANCH):
            gate = jax.nn.sigmoid(lg_ref[:, k * D_MODEL:(k + 1) * D_MODEL] + b_ref[k])
            acc = acc + gate * _dg(y_ref[k], w_ref[k], 1, 0)
        o_ref[...] = acc.astype(BF16)

    return pl.pallas_call(
        body, name=name, grid=(SEQ // MERGE_T,),
        in_specs=[pl.BlockSpec((N_BRANCH, MERGE_T, BW), lambda i: (0, i, 0)),
                  pl.BlockSpec((MERGE_T, N_BRANCH * D_MODEL), lambda i: (i, C_MERGE // (N_BRANCH * D_MODEL))),
                  pl.BlockSpec((N_BRANCH, 1, D_MODEL), lambda i: (0, 0, 0)),
                  pl.BlockSpec((N_BRANCH, BW, D_MODEL), lambda i: (0, 0, 0))],
        out_specs=pl.BlockSpec((MERGE_T, D_MODEL), lambda i: (i, 0)),
        out_shape=jax.ShapeDtypeStruct((SEQ, D_MODEL), BF16),
        compiler_params=_cparams(("parallel",)),
    )(ys, proj, merge_b, w_branch)


def _merge_bwd(proj, ys, dm, merge_b, w_branch, name):
    nt = SEQ // MERGE_BWD_T

    def body(y_ref, lg_ref, dm_ref, b_ref, w_ref, dy_ref, dlg_ref, dw_ref, db_ref, dw_acc):
        i = pl.program_id(1)
        gate = jax.nn.sigmoid(lg_ref[...] + b_ref[0])
        y = y_ref[0]
        dmv = dm_ref[...]
        dbo = (gate * dmv).astype(BF16)
        dlg = _dg(y, w_ref[0], 1, 0) * dmv * gate * (1.0 - gate)
        dlg_ref[...] = dlg.astype(BF16)
        dy_ref[0] = _dg(dbo, w_ref[0], 1, 1)
        dwp = _dg(y, dbo, 0, 0)
        dbp = jnp.sum(dlg, axis=0, keepdims=True)

        @pl.when(i == 0)
        def _():
            dw_acc[...] = dwp
            db_ref[0] = dbp

        @pl.when(i > 0)
        def _():
            dw_acc[...] += dwp
            db_ref[0] += dbp

        @pl.when(i == nt - 1)
        def _():
            dw_ref[0] = dw_acc[...].astype(BF16)

    return pl.pallas_call(
        body, name=name, grid=(N_BRANCH, nt),
        in_specs=[pl.BlockSpec((1, MERGE_BWD_T, BW), lambda k, i: (k, i, 0)),
                  pl.BlockSpec((MERGE_BWD_T, D_MODEL), lambda k, i: (i, C_MERGE // D_MODEL + k)),
                  pl.BlockSpec((MERGE_BWD_T, D_MODEL), lambda k, i: (i, 0)),
                  pl.BlockSpec((1, 1, D_MODEL), lambda k, i: (k, 0, 0)),
                  pl.BlockSpec((1, BW, D_MODEL), lambda k, i: (k, 0, 0))],
        out_specs=[pl.BlockSpec((1, MERGE_BWD_T, BW), lambda k, i: (k, i, 0)),
                   pl.BlockSpec((MERGE_BWD_T, D_MODEL), lambda k, i: (i, k)),
                   pl.BlockSpec((1, BW, D_MODEL), lambda k, i: (k, 0, 0)),
                   pl.BlockSpec((1, 1, D_MODEL), lambda k, i: (k, 0, 0))],
        out_shape=[jax.ShapeDtypeStruct((N_BRANCH, SEQ, BW), F32), jax.ShapeDtypeStruct((SEQ, IN_PAD), BF16),
                   jax.ShapeDtypeStruct((N_BRANCH, BW, D_MODEL), BF16), jax.ShapeDtypeStruct((N_BRANCH, 1, D_MODEL), F32)],
        scratch_shapes=[pltpu.VMEM((BW, D_MODEL), F32)],
        compiler_params=_cparams(("parallel", "arbitrary")),
    )(ys, proj, dm, merge_b, w_branch)


def _adamw(glist, w, m, v, rows, name):
    nl = len(glist)
    n, r, c = glist[0].shape
    assert w.shape == (nl, r, c) and r % rows == 0
    nb = r // rows

    def body(*refs):
        g_refs = refs[:nl]
        w_ref, m_ref, v_ref, go_ref, d_ref, mo_ref, vo_ref = refs[nl:]
        for layer in range(nl):
            @pl.when(pl.program_id(0) == layer)
            def _(g_ref=g_refs[layer]):
                g = g_ref[0].astype(F32)
                for s in range(1, n):
                    g = g + g_ref[s].astype(F32)
                mn = ADAM_B1 * m_ref[0] + (1.0 - ADAM_B1) * g
                vn = ADAM_B2 * v_ref[0] + (1.0 - ADAM_B2) * jnp.square(g)
                m_hat = mn / (1.0 - ADAM_B1 ** ADAM_STEP)
                v_hat = vn / (1.0 - ADAM_B2 ** ADAM_STEP)
                go_ref[0] = g
                d_ref[0] = -ADAM_LR * (m_hat / (jnp.sqrt(v_hat) + ADAM_EPS) + ADAM_WD * w_ref[0])
                mo_ref[0] = mn
                vo_ref[0] = vn

    def g_spec(layer):
        return pl.BlockSpec((n, rows, c), lambda a, i: (0, jnp.where(a < layer, 0, jnp.where(a == layer, i, nb - 1)), 0))

    blk = pl.BlockSpec((1, rows, c), lambda a, i: (a, i, 0))
    out = jax.ShapeDtypeStruct((nl, r, c), F32)
    return pl.pallas_call(
        body, name=name, grid=(nl, nb),
        in_specs=[g_spec(layer) for layer in range(nl)] + [blk, blk, blk],
        out_specs=[blk, blk, blk, blk], out_shape=[out, out, out, out],
        compiler_params=_cparams(("arbitrary", "arbitrary")),
    )(*glist, w, m, v)


X_ROWS_PER_COL = 2 * (D_MODEL // LANES)


def _w_in_to_x(w):
    t = jnp.transpose(w, (2, 0, 1)).reshape(SHARD_IN, DEPTH, D_MODEL // LANES, LANES)
    return jnp.transpose(t, (0, 2, 1, 3)).reshape(SHARD_IN * X_ROWS_PER_COL, LANES)


def _w_in_from_x(xv):
    t = jnp.transpose(xv.reshape(SHARD_IN, D_MODEL // LANES, DEPTH, LANES), (0, 2, 1, 3))
    return jnp.transpose(t.reshape(SHARD_IN, DEPTH, D_MODEL), (1, 2, 0))


def _adamw_w_in(glist, w, m, v, name, after=None):
    n = glist[0].shape[0]
    cols = LANES
    rows = cols * X_ROWS_PER_COL
    extra = [] if after is None else [after]

    def body(g0_ref, g1_ref, w_ref, m_ref, v_ref, *rest):
        go_ref, d_ref, mo_ref, vo_ref = rest[len(extra):]
        for layer, g_ref in enumerate((g0_ref, g1_ref)):
            g = g_ref[0].astype(F32)
            for s in range(1, n):
                g = g + g_ref[s].astype(F32)
            gt = g.T
            for t in range(D_MODEL // LANES):
                sel = (pl.ds(2 * t + layer, cols, stride=X_ROWS_PER_COL), slice(None))
                gs = gt[:, t * LANES:(t + 1) * LANES]
                mn = ADAM_B1 * m_ref[sel] + (1.0 - ADAM_B1) * gs
                vn = ADAM_B2 * v_ref[sel] + (1.0 - ADAM_B2) * jnp.square(gs)
                m_hat = mn / (1.0 - ADAM_B1 ** ADAM_STEP)
                v_hat = vn / (1.0 - ADAM_B2 ** ADAM_STEP)
                go_ref[sel] = gs
                d_ref[sel] = -ADAM_LR * (m_hat / (jnp.sqrt(v_hat) + ADAM_EPS) + ADAM_WD * w_ref[sel])
                mo_ref[sel] = mn
                vo_ref[sel] = vn

    g_spec = pl.BlockSpec((n, D_MODEL, cols), lambda i: (0, 0, i))
    blk = pl.BlockSpec((rows, LANES), lambda i: (i, 0))
    out = jax.ShapeDtypeStruct((SHARD_IN * X_ROWS_PER_COL, LANES), F32)
    res = pl.pallas_call(
        body, name=name, grid=(-(-SHARD_IN // cols),),
        in_specs=[g_spec, g_spec, blk, blk, blk] + [pl.BlockSpec(memory_space=pl.ANY)] * len(extra),
        out_specs=[blk, blk, blk, blk], out_shape=[out, out, out, out],
        compiler_params=_cparams(("parallel",)),
    )(*glist, _w_in_to_x(w), _w_in_to_x(m), _w_in_to_x(v), *extra)
    return [_w_in_from_x(o) for o in res]


def _adamw_many(gs, ws, ms, vs, name):
    k = len(gs)

    def body(*refs):
        g_refs, w_refs, m_refs, v_refs = refs[:k], refs[k:2 * k], refs[2 * k:3 * k], refs[3 * k:4 * k]
        d_refs, mo_refs, vo_refs = refs[4 * k:5 * k], refs[5 * k:6 * k], refs[6 * k:7 * k]
        for i in range(k):
            g = g_refs[i][...]
            mn = ADAM_B1 * m_refs[i][...] + (1.0 - ADAM_B1) * g
            vn = ADAM_B2 * v_refs[i][...] + (1.0 - ADAM_B2) * jnp.square(g)
            m_hat = mn / (1.0 - ADAM_B1 ** ADAM_STEP)
            v_hat = vn / (1.0 - ADAM_B2 ** ADAM_STEP)
            d_refs[i][...] = -ADAM_LR * (m_hat / (jnp.sqrt(v_hat) + ADAM_EPS) + ADAM_WD * w_refs[i][...])
            mo_refs[i][...] = mn
            vo_refs[i][...] = vn

    whole = pl.BlockSpec(memory_space=pltpu.VMEM)
    shapes = [jax.ShapeDtypeStruct(w.shape, F32) for w in ws]
    outs = pl.pallas_call(
        body, name=name, in_specs=[whole] * (4 * k), out_specs=[whole] * (3 * k), out_shape=shapes * 3,
        compiler_params=_cparams(None),
    )(*gs, *ws, *ms, *vs)
    return outs[:k], outs[k:2 * k], outs[2 * k:]


MEMORY_ORDER = {'s5_b_re': (0, 1, 3, 2), 's5_b_im': (0, 1, 3, 2), 's5_d': (0, 2, 1), 'sc_conv_w': (1, 0, 2)}


def _memory_view(name, t):
    return jnp.transpose(t, MEMORY_ORDER[name]) if name in MEMORY_ORDER else t


def _slot_sum(gslots, name):
    n, r, c = gslots.shape

    def body(g_ref, o_ref):
        g = g_ref[0]
        for s in range(1, n):
            g = g + g_ref[s]
        o_ref[...] = g

    return pl.pallas_call(
        body, name=name, in_specs=[pl.BlockSpec((n, r, c), lambda: (0, 0, 0))],
        out_specs=pl.BlockSpec((r, c), lambda: (0, 0)), out_shape=jax.ShapeDtypeStruct((r, c), F32),
        compiler_params=_cparams(None),
    )(gslots)


def _me_and_peers():
    x, y, c = lax.axis_index("x"), lax.axis_index("y"), lax.axis_index("c")
    me = 4 * x + 2 * y + c
    peers = []
    for k in range(1, N_DEV):
        px = 1 - x if (k >> 2) & 1 else x
        py = 1 - y if (k >> 1) & 1 else y
        pc = 1 - c if k & 1 else c
        peers.append((4 * px + 2 * py + pc, (px, py, pc)))
    return me, peers


_HBM = pl.BlockSpec(memory_space=pltpu.HBM)
_SEM = pl.BlockSpec(memory_space=pltpu.SEMAPHORE)
_EFFECT = pltpu.SideEffectType.DATAFLOW_SIDE_EFFECTING


N_CHIP = N_DEV // 2


def _chip_peers():
    x, y, c = lax.axis_index("x"), lax.axis_index("y"), lax.axis_index("c")
    chips = []
    for d in range(1, N_CHIP):
        px = 1 - x if (d >> 1) & 1 else x
        py = 1 - y if d & 1 else y
        chips.append((2 * px + py, (px, py)))
    return (x, y, c), 2 * x + y, chips


def _plan_direct(ins, lands, send_sems, recv_sems, local_sems, gather):
    me, peers = _me_and_peers()
    plan = dict(start=[], local=[], sends=[], recvs=[])
    for t in range(len(ins)):
        own = pltpu.make_async_copy(ins[t] if gather else ins[t].at[me], lands[t].at[me], local_sems.at[t])
        plan['start'].append(own)
        plan['local'].append(own)
        for k, (pidx, pos) in enumerate(peers):
            cp = pltpu.make_async_remote_copy(
                src_ref=ins[t] if gather else ins[t].at[pidx], dst_ref=lands[t].at[me],
                send_sem=send_sems.at[t * (N_DEV - 1) + k], recv_sem=recv_sems.at[t * (N_DEV - 1) + k],
                device_id=pos, device_id_type=MESH)
            plan['start'].append(cp)
            plan['sends'].append(cp)
            plan['recvs'].append(cp)
    return plan


def _plan_gather(ins, lands, send_sems, recv_sems, local_sems, first=0):
    (x, y, c), q, chips = _chip_peers()
    me = 2 * q + c
    plan = dict(start=[], relay_wait=[], relay_start=[], local=[], sends=[], recvs=[])
    for t in range(len(ins)):
        base = (first + t) * 7
        sem = lambda k: dict(send_sem=send_sems.at[base + k], recv_sem=recv_sems.at[base + k], device_id_type=MESH)
        own = pltpu.make_async_copy(ins[t], lands[t].at[me], local_sems.at[first + t])
        to_sib = pltpu.make_async_remote_copy(src_ref=ins[t], dst_ref=lands[t].at[me], device_id=(x, y, 1 - c), **sem(0))
        plan['start'] += [own, to_sib]
        plan['local'].append(own)
        plan['sends'].append(to_sib)
        plan['recvs'].append(to_sib)
        for d, (pq, (px, py)) in enumerate(chips):
            to_chip = pltpu.make_async_remote_copy(src_ref=ins[t], dst_ref=lands[t].at[me], device_id=(px, py, c), **sem(1 + d))
            blk = lands[t].at[2 * pq + c]
            fwd = pltpu.make_async_remote_copy(src_ref=blk, dst_ref=blk, device_id=(x, y, 1 - c), **sem(4 + d))
            plan['start'].append(to_chip)
            plan['relay_wait'].append(to_chip)
            plan['relay_start'].append(fwd)
            plan['sends'] += [to_chip, fwd]
            plan['recvs'].append(fwd)
    return plan


def _plan_pair(ins, lands, send_sems, recv_sems, local_sems):
    (x, y, c), q, chips = _chip_peers()
    plan = dict(start=[], local=[], sends=[], recvs=[])
    for t in range(len(ins)):
        for k in range(N_CHIP):
            cp = pltpu.make_async_remote_copy(
                src_ref=ins[t].at[2 * k + 1 - c], dst_ref=lands[t].at[k], send_sem=send_sems.at[t * N_CHIP + k],
                recv_sem=recv_sems.at[t * N_CHIP + k], device_id=(x, y, 1 - c), device_id_type=MESH)
            plan['start'].append(cp)
            plan['sends'].append(cp)
            plan['recvs'].append(cp)
    return plan


def _plan_chips(ins, lands, send_sems, recv_sems, local_sems):
    (x, y, c), q, chips = _chip_peers()
    plan = dict(start=[], local=[], sends=[], recvs=[])
    for t in range(len(ins)):
        own = pltpu.make_async_copy(ins[t].at[q], lands[t].at[q], local_sems.at[t])
        plan['start'].append(own)
        plan['local'].append(own)
        for d, (pq, (px, py)) in enumerate(chips):
            cp = pltpu.make_async_remote_copy(
                src_ref=ins[t].at[pq], dst_ref=lands[t].at[q], send_sem=send_sems.at[t * 3 + d],
                recv_sem=recv_sems.at[t * 3 + d], device_id=(px, py, c), device_id_type=MESH)
            plan['start'].append(cp)
            plan['sends'].append(cp)
            plan['recvs'].append(cp)
    return plan


def _split_start(plan_fn, tensors, land_shapes, n_sems, name, after=None):
    n = len(tensors)
    extra = [] if after is None else [after]

    def body(*refs):
        ins, lands = refs[:n], refs[n:2 * n]
        plan = plan_fn(ins, lands, *refs[2 * n + len(extra):2 * n + len(extra) + 3])
        for cp in plan['start']:
            cp.start()
        refs[-1][...] = jnp.zeros_like(refs[-1])

    outs = pl.pallas_call(
        body, name=name,
        out_shape=(pltpu.SemaphoreType.DMA((n_sems,)), pltpu.SemaphoreType.DMA((n_sems,)), pltpu.SemaphoreType.DMA((n,)),
                   *[pltpu.HBM(t.shape, t.dtype) for t in tensors],
                   *[pltpu.HBM(s, t.dtype) for s, t in zip(land_shapes, tensors)],
                   jax.ShapeDtypeStruct((8, LANES), F32)),
        in_specs=[_HBM] * (2 * n) + [pl.BlockSpec(memory_space=pl.ANY)] * len(extra),
        out_specs=(_SEM, _SEM, _SEM, *[_HBM] * (2 * n), pl.BlockSpec(memory_space=pltpu.VMEM)),
        input_output_aliases={t: 3 + t for t in range(2 * n)},
        compiler_params=pltpu.CompilerParams(has_side_effects=_EFFECT),
    )(*[pltpu.with_memory_space_constraint(t, pltpu.HBM) for t in tensors],
      *[pltpu.with_memory_space_constraint(lax.empty(s, t.dtype), pltpu.HBM) for s, t in zip(land_shapes, tensors)], *extra)
    return outs[:-1], outs[-1]


def _split_relay(plan_fn, state, after, name):
    sems, thru = state[:3], state[3:]
    n = len(thru) // 2

    def arrived(*refs):
        plan = plan_fn(refs[:n], refs[n:2 * n], *refs[2 * n:2 * n + 3])
        for cp in plan['relay_wait']:
            cp.wait_recv()

    thru = pl.pallas_call(
        arrived, name=name + "_arrived",
        out_shape=tuple(pltpu.HBM(t.shape, t.dtype) for t in thru),
        in_specs=[_HBM] * (2 * n) + [_SEM, _SEM, _SEM, pl.BlockSpec(memory_space=pl.ANY)],
        out_specs=tuple([_HBM] * (2 * n)),
        input_output_aliases={t: t for t in range(2 * n)},
        compiler_params=pltpu.CompilerParams(has_side_effects=_EFFECT),
    )(*thru, *sems, after)

    def forward(*refs):
        plan = plan_fn(refs[:n], refs[n:2 * n], *refs[2 * n:2 * n + 3])
        for cp in plan['relay_start']:
            cp.start()
        refs[-1][...] = jnp.zeros_like(refs[-1])

    outs = pl.pallas_call(
        forward, name=name + "_forward",
        out_shape=(*[pltpu.HBM(t.shape, t.dtype) for t in thru], jax.ShapeDtypeStruct((8, LANES), F32)),
        in_specs=[_HBM] * (2 * n) + [_SEM, _SEM, _SEM],
        out_specs=(*[_HBM] * (2 * n), pl.BlockSpec(memory_space=pltpu.VMEM)),
        input_output_aliases={t: t for t in range(2 * n)},
        compiler_params=pltpu.CompilerParams(has_side_effects=_EFFECT),
    )(*thru, *sems)
    return (*sems, *outs[:-1]), outs[-1]


def _split_wait(plan_fn, state, after, name, with_sources=False):
    sems, thru = state[:3], state[3:]
    n = len(thru) // 2

    def body(*refs):
        plan = plan_fn(refs[:n], refs[n:2 * n], *refs[2 * n:2 * n + 3])
        for cp in plan['local']:
            cp.wait()
        for cp in plan['sends']:
            cp.wait_send()
        for cp in plan['recvs']:
            cp.wait_recv()

    outs = pl.pallas_call(
        body, name=name,
        out_shape=tuple(pltpu.HBM(t.shape, t.dtype) for t in thru),
        in_specs=[_HBM] * (2 * n) + [_SEM, _SEM, _SEM, pl.BlockSpec(memory_space=pl.ANY)],
        out_specs=tuple([_HBM] * (2 * n)),
        input_output_aliases={t: t for t in range(2 * n)},
        compiler_params=pltpu.CompilerParams(has_side_effects=_EFFECT),
    )(*thru, *sems, after)
    return (list(outs[:n]), list(outs[n:])) if with_sources else list(outs[n:])


PAIR_SUM_BLOCK = 768 * 1024


def _pair_sum(mine, theirs, name):
    _, r, c = mine.shape
    rows = r
    while rows * c > PAIR_SUM_BLOCK and rows % 32 == 0:
        rows //= 2

    def body(core_ref, a_ref, b_ref, o_ref):
        o_ref[0] = (a_ref[0].astype(F32) + b_ref[0].astype(F32)).astype(o_ref.dtype)

    return pl.pallas_call(
        body, name=name,
        grid_spec=pltpu.PrefetchScalarGridSpec(
            num_scalar_prefetch=1, grid=(N_CHIP, r // rows),
            in_specs=[pl.BlockSpec((1, rows, c), lambda k, i, core: (2 * k + core[0], i, 0)),
                      pl.BlockSpec((1, rows, c), lambda k, i, core: (k, i, 0))],
            out_specs=pl.BlockSpec((1, rows, c), lambda k, i, core: (k, i, 0))),
        out_shape=jax.ShapeDtypeStruct((N_CHIP, r, c), mine.dtype),
        compiler_params=_cparams(("parallel", "parallel")),
    )(lax.axis_index("c").astype(jnp.int32).reshape(1), mine, theirs)


WEIGHTS = ['norm_w', 'w_in', 's5_lambda_re', 's5_lambda_im', 's5_b_re', 's5_b_im', 's5_c_re', 's5_c_im', 's5_d',
           's5_log_step', 's5_w_glu', 'sgu_ln_w', 'sgu_ln_b', 'sgu_w', 'sgu_b', 'm2_conv_w', 'm2_conv_b', 'm2_dt_bias',
           'm2_a_log', 'm2_d', 'm2_norm_w', 'sc_conv_w', 'merge_b', 'w_branch', 'w_out', 'final_norm_w']
BIG_SHARDED = ['w_in', 'w_branch', 'w_out', 's5_w_glu']
SMALL_SHARDED = ['m2_conv_w', 'sc_conv_w', 'merge_b']
REPLICATED = [n for n in WEIGHTS if n not in BIG_SHARDED + SMALL_SHARDED]
S5_NAMES = ['s5_lambda_re', 's5_lambda_im', 's5_b_re', 's5_b_im', 's5_c_re', 's5_c_im', 's5_d', 's5_log_step']


def _sc_interleave(t):
    lead = t.shape[:-1]
    return jnp.swapaxes(t.reshape(lead + (4, 4, LANES)), -3, -2).reshape(lead + (4 * BW,))


def _pad_in(w):
    z = lambda n: jnp.zeros(w.shape[:-1] + (n,), w.dtype)
    return jnp.concatenate([w[..., 6152:], w[..., 0:1024], w[..., 3072:4096], w[..., 1024:2560], z(512),
                            w[..., 2560:3072], w[..., 4096:4104], z(504), _sc_interleave(w[..., 4104:6152])], axis=-1)


def _unpad_in(g):
    return jnp.concatenate([g[..., C_S5U:C_S5U + 1024], g[..., C_SGU_U:C_SGU_U + 1536], g[..., C_M2Z:C_M2Z + 512],
                            g[..., C_M2X:C_M2X + 1024], g[..., C_DT:C_DT + 8], _sc_interleave(g[..., C_SC:]),
                            g[..., :N_BRANCH * D_MODEL]], axis=-1)


ROW_BLOCK = 8 * LANES


def _pack_rows(tensors, row_mult, batched=False):
    parts = []
    for t in tensors:
        f = t.reshape((t.shape[0], -1) if batched else (1, -1))
        f = jnp.pad(f, ((0, 0), (0, (-f.shape[1]) % ROW_BLOCK)))
        parts.append(f.reshape(f.shape[0], -1, LANES))
    out = jnp.concatenate(parts, axis=1)
    out = jnp.pad(out, ((0, 0), (0, (-out.shape[1]) % row_mult), (0, 0)))
    return out if batched else out[0]


def _unpack_rows(rows, shapes):
    out, r0 = [], 0
    for shp in shapes:
        size = 1
        for s in shp:
            size *= s
        nr = -(-size // ROW_BLOCK) * 8
        out.append(rows[r0:r0 + nr].reshape(-1)[:size].reshape(shp))
        r0 += nr
    return out


def _kernel_col_map():
    m = np.full(IN_PAD, -1, np.int64)
    m[C_MERGE:C_MERGE + 4096] = np.arange(6152, 10248)
    m[C_S5U:C_S5U + 1024] = np.arange(0, 1024)
    m[C_M2X:C_M2X + 1024] = np.arange(3072, 4096)
    m[C_SGU_U:C_SGU_U + 1536] = np.arange(1024, 2560)
    m[C_M2Z:C_M2Z + 512] = np.arange(2560, 3072)
    m[C_DT:C_DT + 8] = np.arange(4096, 4104)
    for j in range(4):
        for kind in range(4):
            k0 = C_SC + 4 * LANES * j + LANES * kind
            m[k0:k0 + LANES] = 4104 + BW * kind + LANES * j + np.arange(LANES)
    return m


def _lane_pieces(sources):
    pieces, cur = [], None
    for lane, src in enumerate(sources):
        key = None if src is None else (src[0], src[1] // LANES, (lane - src[1]) % LANES)
        if cur is not None and key == cur[0]:
            cur[2] = lane + 1
        else:
            if cur is not None and cur[0] is not None:
                pieces.append((*cur[0], cur[1], cur[2]))
            cur = [key, lane, lane + 1]
    if cur is not None and cur[0] is not None:
        pieces.append((*cur[0], cur[1], cur[2]))
    return pieces


def _assemble_block(pieces, load, rows, dtype):
    lane = lax.broadcasted_iota(jnp.int32, (rows, LANES), 1)
    out = None
    for arr, sb, shift, lo, hi in pieces:
        v = load(arr, sb)
        if shift:
            v = pltpu.roll(v, shift, 1)
        if out is None and lo == 0 and hi == LANES:
            out = v
        else:
            out = jnp.where((lane >= lo) & (lane < hi), v, jnp.zeros((rows, LANES), dtype) if out is None else out)
    return jnp.zeros((rows, LANES), dtype) if out is None else out


RELAYOUT_ROWS = 512
SHARD_BLOCKS = -(-SHARD_IN // LANES)


def _load_shard_block(ref, rows):
    def load(j, sb):
        if sb == SHARD_BLOCKS - 1:
            return jnp.broadcast_to(ref[j, :, SHARD_IN - 1:SHARD_IN], (rows, LANES))
        return ref[j, :, sb * LANES:(sb + 1) * LANES]
    return load


def _relayout_w_in(gathered, name):
    kmap = _kernel_col_map()
    dtype = gathered.dtype

    def body(src_ref, o_ref):
        load = _load_shard_block(src_ref, RELAYOUT_ROWS)
        for ob in range(IN_PAD // LANES):
            srcs = [None if kmap[ob * LANES + l] < 0 else (int(kmap[ob * LANES + l]) // SHARD_IN, int(kmap[ob * LANES + l]) % SHARD_IN)
                    for l in range(LANES)]
            o_ref[:, ob * LANES:(ob + 1) * LANES] = _assemble_block(_lane_pieces(srcs), load, RELAYOUT_ROWS, dtype)

    return pl.pallas_call(
        body, name=name, grid=(D_MODEL // RELAYOUT_ROWS,),
        in_specs=[pl.BlockSpec((N_DEV, RELAYOUT_ROWS, SHARD_IN), lambda i: (0, i, 0))],
        out_specs=pl.BlockSpec((RELAYOUT_ROWS, IN_PAD), lambda i: (i, 0)),
        out_shape=jax.ShapeDtypeStruct((D_MODEL, IN_PAD), dtype),
        compiler_params=_cparams(("parallel",)),
    )(gathered)


def _relayout_g_in(gw, name):
    kmap = _kernel_col_map()
    kinv = np.zeros(IN_DIM, np.int64)
    kinv[kmap[kmap >= 0]] = np.nonzero(kmap >= 0)[0]
    dtype = gw.dtype

    def body(src_ref, o_ref):
        load = lambda _, sb: src_ref[:, sb * LANES:(sb + 1) * LANES]
        for j in range(N_DEV):
            for ob in range(SHARD_BLOCKS):
                srcs = [(0, int(kinv[SHARD_IN * j + ob * LANES + l])) if ob * LANES + l < SHARD_IN else None for l in range(LANES)]
                blk = _assemble_block(_lane_pieces(srcs), load, RELAYOUT_ROWS, dtype)
                if ob == SHARD_BLOCKS - 1:
                    o_ref[j, :, SHARD_IN - 1:SHARD_IN] = blk[:, 0:1]
                else:
                    o_ref[j, :, ob * LANES:(ob + 1) * LANES] = blk

    return pl.pallas_call(
        body, name=name, grid=(D_MODEL // RELAYOUT_ROWS,),
        in_specs=[pl.BlockSpec((RELAYOUT_ROWS, IN_PAD), lambda i: (i, 0))],
        out_specs=pl.BlockSpec((N_DEV, RELAYOUT_ROWS, SHARD_IN), lambda i: (0, i, 0)),
        out_shape=jax.ShapeDtypeStruct((N_DEV, D_MODEL, SHARD_IN), dtype),
        compiler_params=_cparams(("parallel",)),
    )(gw)


def _rows128(flat, row_mult=8):
    n = flat.shape[0]
    per = LANES * row_mult
    total = -(-n // per) * per
    return jnp.pad(flat, (0, total - n)).reshape(total // LANES, LANES)


def _pad_lanes(v):
    return jnp.pad(v, (0, LANES - v.shape[0])).reshape(1, LANES)


def _layer_prep(i, p):
    disc, disc_vjp = jax.vjp(_s5_disc, *[p[n][i] for n in S5_NAMES])
    prep = dict(
        nw=p['norm_w'][i].reshape(1, D_MODEL), disc_vjp=disc_vjp,
        s5small=[_block_diag(t).astype(BF16) for t in disc[:4]] + [disc[4], disc[5]],
        sgw=[p['sgu_ln_w'][i].reshape(1, BW), p['sgu_ln_b'][i].reshape(1, BW), p['sgu_w'][i],
             jnp.repeat(p['sgu_b'][i].T, BW // SGU_HEADS, axis=1)],
        cb=p['m2_conv_b'][i].reshape(1, M2_CONV_CH),
        m2w=[_pad_lanes(p['m2_dt_bias'][i]), _pad_lanes(p['m2_a_log'][i]),
             jnp.repeat(p['m2_d'][i], M2_HEAD_DIM).reshape(1, BW), p['m2_norm_w'][i].reshape(1, BW)])
    touch = [t[0, 0].astype(F32) for t in prep['s5small']] + [prep['sgw'][3][0, 0], prep['m2w'][2][0, 0]]
    return prep, sum(touch[1:], touch[0])


def _layer_fwd(x, h, i, prep, w_in, other_weights, before_merge=None):
    proj = _matmul(h, w_in, 1, 0, F32, 1024, 1024, 1024, f"proj{i}")
    full = dict(other_weights(proj), w_in=w_in)
    s5w = prep['s5small'] + [full['s5_w_glu']]
    ys, sre, sim = _s5_fwd(proj, *s5w, f"s5_fwd{i}")
    ys = _sgu_fwd(proj, ys, *prep['sgw'], f"sgu_fwd{i}")
    cw = full['m2_conv_w']
    xa = _m2_conv_fwd(proj, cw, prep['cb'], f"m2conv_fwd{i}")
    ys, s_in = _ssd_fwd(proj, ys, xa, *prep['m2w'], f"ssd_fwd{i}")
    scw = full['sc_conv_w']
    ys = _sc_fwd(proj, ys, scw, f"sc_fwd{i}")
    mb = full['merge_b'].reshape(N_BRANCH, 1, D_MODEL)
    if before_merge is not None:
        mb = mb + before_merge(ys)[0, 0]
    merged = _merge_fwd(proj, ys, mb, full['w_branch'], f"merge_fwd{i}")
    x_new = _matmul(merged, full['w_out'], 1, 0, F32, 1024, 1024, 1024, f"out{i}", residual=x)
    saved = dict(x=x, nw=prep['nw'], h=h, proj=proj, disc_vjp=prep['disc_vjp'], s5w=s5w, sre=sre, sim=sim, sgw=prep['sgw'],
                 cw=cw, cb=prep['cb'], xa=xa, m2w=prep['m2w'], s_in=s_in, scw=scw, ys=ys, mb=mb, merged=merged)
    return x_new, saved, full


def _layer_bwd(dx_out, i, sv, full, on_large_grads=None, after_dh=None):
    g = {}
    proj = sv['proj']
    dm = _matmul(dx_out, full['w_out'], 1, 1, F32, 1024, 1024, 1024, f"dmerged{i}")
    g['w_out'] = _matmul(sv['merged'], dx_out, 0, 0, BF16, 1024, 1024, 1024, f"gw_out{i}")
    dys, dproj, g['w_branch'], dmb = _merge_bwd(proj, sv['ys'], dm, sv['mb'], full['w_branch'], f"merge_bwd{i}")
    g['merge_b'] = dmb.reshape(N_BRANCH, D_MODEL)
    dproj, dbbre, dbbim, dcre, dcim, da, dd, dwg = _s5_bwd(proj, dproj, dys, sv['sre'], sv['sim'], *sv['s5w'], f"s5_bwd{i}")
    g['s5_dense'] = (dbbre, dbbim, dcre, dcim, da, dd)
    g['s5_w_glu'] = dwg.astype(BF16)
    dproj, dlw, dlb, g['sgu_w'], dbias = _sgu_bwd(proj, dproj, dys, *sv['sgw'], f"sgu_bwd{i}")
    g['sgu_ln_w'], g['sgu_ln_b'] = dlw[0], dlb[0]
    g['sgu_b'] = dbias.reshape(SGU_CHUNK, SGU_HEADS, BW // SGU_HEADS).sum(-1).T
    dproj, dxa, ddtb, dal, ddf, dnw = _ssd_bwd(proj, dproj, sv['xa'], dys, sv['s_in'], *sv['m2w'], f"ssd_bwd{i}")
    dproj, g['m2_conv_w'], dcb = _m2_conv_bwd(proj, dproj, dxa, sv['cw'], sv['cb'], f"m2conv_bwd{i}")
    g['m2_conv_b'], g['m2_norm_w'] = dcb[0], dnw[0]
    g['m2_dt_bias'], g['m2_a_log'] = ddtb[0, :M2_HEADS], dal[0, :M2_HEADS]
    g['m2_d'] = ddf.reshape(M2_HEADS, M2_HEAD_DIM).sum(-1)
    dproj, g['sc_conv_w'] = _sc_bwd(proj, dproj, dys, sv['scw'], f"sc_bwd{i}")
    g['w_in'] = _matmul(sv['h'], dproj, 0, 0, BF16, 1024, 1024, 1024, f"gw_in{i}")
    tok = on_large_grads(g) if on_large_grads else None
    dh = _matmul(dproj, full['w_in'], 1, 1, F32, 1024, 1024, 1024, f"dh{i}", after=tok)
    nw = sv['nw'] if after_dh is None else sv['nw'] + after_dh(dh)[0, 0]
    dx_in, dnw_l = _rmsnorm_bwd(sv['x'], nw, dh, dx_out, f"rms_bwd{i}")
    g['norm_w'] = dnw_l[0]
    return dx_in, g


def _split8(t, axis):
    shp = t.shape
    t = t.reshape(shp[:axis] + (N_DEV, shp[axis] // N_DEV) + shp[axis + 1:])
    return jnp.moveaxis(t, axis, 0)


def _join8(t, axis):
    t = jnp.moveaxis(t, 0, axis)
    shp = t.shape
    return t.reshape(shp[:axis] + (shp[axis] * shp[axis + 1],) + shp[axis + 2:])


SHARD_AXIS = {'w_in': 2, 'w_branch': 3, 'w_out': 1, 's5_w_glu': 1, 'm2_conv_w': 2, 'sc_conv_w': 2, 'merge_b': 2}


OTHER_BIG = [n for n in BIG_SHARDED if n != 'w_in']


def _other_weights(gathered):
    return {n: _join8(t, SHARD_AXIS[n] - 1) for n, t in zip(OTHER_BIG, gathered)}


def _layer_grad_blocks(g, i):
    blocks = [_relayout_g_in(g[n], f"relayout_g_in{i}") if n == 'w_in' else _split8(g[n], SHARD_AXIS[n] - 1) for n in BIG_SHARDED]
    return [b.reshape(N_DEV, -1, b.shape[-1]) for b in blocks]


def _pair_start(blocks, tag):
    shapes = [(N_CHIP,) + b.shape[1:] for b in blocks]
    return _split_start(_plan_pair, blocks, shapes, N_CHIP * len(blocks), f"pair{tag}_start")


def _pair_sums(state, after, tag):
    mine, theirs = _split_wait(_plan_pair, state, after, f"pair{tag}_wait", with_sources=True)
    return [_pair_sum(b, t, f"pair_sum{tag}_{k}") for k, (b, t) in enumerate(zip(mine, theirs))]


def _chips_start(sums, tag, after=None):
    return _split_start(_plan_chips, sums, [s.shape for s in sums], 3 * len(sums), f"chips{tag}_start", after)


def kernel(x, norm_w, w_in, s5_lambda_re, s5_lambda_im, s5_b_re, s5_b_im, s5_c_re, s5_c_im, s5_d, s5_log_step, s5_w_glu, sgu_ln_w, sgu_ln_b, sgu_w, sgu_b, m2_conv_w, m2_conv_b, m2_dt_bias, m2_a_log, m2_d, m2_norm_w, sc_conv_w, merge_b, w_branch, w_out, final_norm_w, loss_target, m_norm_w, m_w_in, m_s5_lambda_re, m_s5_lambda_im, m_s5_b_re, m_s5_b_im, m_s5_c_re, m_s5_c_im, m_s5_d, m_s5_log_step, m_s5_w_glu, m_sgu_ln_w, m_sgu_ln_b, m_sgu_w, m_sgu_b, m_m2_conv_w, m_m2_conv_b, m_m2_dt_bias, m_m2_a_log, m_m2_d, m_m2_norm_w, m_sc_conv_w, m_merge_b, m_w_branch, m_w_out, m_final_norm_w, v_norm_w, v_w_in, v_s5_lambda_re, v_s5_lambda_im, v_s5_b_re, v_s5_b_im, v_s5_c_re, v_s5_c_im, v_s5_d, v_s5_log_step, v_s5_w_glu, v_sgu_ln_w, v_sgu_ln_b, v_sgu_w, v_sgu_b, v_m2_conv_w, v_m2_conv_b, v_m2_dt_bias, v_m2_a_log, v_m2_d, v_m2_norm_w, v_sc_conv_w, v_merge_b, v_w_branch, v_w_out, v_final_norm_w):
    loc = locals()
    p = {n: loc[n] for n in WEIGHTS}
    mom = {n: loc['m_' + n] for n in WEIGHTS}
    vel = {n: loc['v_' + n] for n in WEIGHTS}

    small_sizes = [p[n].size for n in SMALL_SHARDED]
    small_pack = _rows128(jnp.concatenate([p[n].reshape(-1) for n in SMALL_SHARDED]))
    first = [p['w_in'][0].astype(BF16)]
    gath_first, tok = _split_start(_plan_gather, first, [(N_DEV,) + first[0].shape], 7, "gather_w_in0_start")
    shards = ([(p[n][0] + tok[0, 0]).astype(BF16) for n in OTHER_BIG] + [small_pack + tok[0, 0]]
              + [(p[n][1] + tok[0, 0]).astype(BF16) for n in BIG_SHARDED])
    gath, tok = _split_start(_plan_gather, shards, [(N_DEV,) + t.shape for t in shards], 7 * len(shards), "gather_start")

    def relayed(lo, hi, after, name, started=None):
        started = gath if started is None else started
        n = (len(started) - 3) // 2
        sems, srcs, lands = started[:3], started[3:3 + n], started[3 + n:]
        plan = functools.partial(_plan_gather, first=lo)
        state, tok = _split_relay(plan, (*sems, *srcs[lo:hi], *lands[lo:hi]), after, name + "_relay")
        return (plan, state, name), tok

    def arrived(relay, after):
        plan, state, name = relay
        return _split_wait(plan, state, after, name + "_wait")

    def gathered(lo, hi, after, name, started=None):
        relay, tok = relayed(lo, hi, after, name, started)
        return arrived(relay, tok)

    later = dict(p, **{n: p[n] + tok[0, 0] for n in ('norm_w', 's5_log_step', 'sgu_b', 'm2_d')})
    preps = [_layer_prep(i, later) for i in range(DEPTH)]
    h0 = _rmsnorm_fwd(x[0], preps[0][0]['nw'], "rms_fwd0")
    got = gathered(0, 1, tok + (preps[0][1] + preps[1][1] + h0[0, 0].astype(F32)), "gather_w_in0", gath_first)
    small_full = {}

    def other_weights0(proj):
        got = gathered(0, 4, proj, "gather_rest0")
        small_all, off = got[-1].reshape(N_DEV, -1), 0
        for n, sz in zip(SMALL_SHARDED, small_sizes):
            small_full[n] = _join8(small_all[:, off:off + sz].reshape((N_DEV,) + p[n].shape), SHARD_AXIS[n])
            off += sz
        return dict(_other_weights(got[:-1]), **{n: small_full[n][0] for n in SMALL_SHARDED})

    saved, layer_g, full = [None] * DEPTH, [None] * DEPTH, [None] * DEPTH
    relay1 = []

    def relay_layer1(ys):
        relay, tok = relayed(4, 8, ys, "gather1")
        relay1.append(relay)
        return tok

    xs, saved[0], full[0] = _layer_fwd(x[0], h0, 0, preps[0][0], _relayout_w_in(got[0], "relayout_w_in0"), other_weights0,
                                       relay_layer1)
    h1 = _rmsnorm_fwd(xs, preps[1][0]['nw'], "rms_fwd1")
    got = arrived(relay1[0], h1)
    xs, saved[1], full[1] = _layer_fwd(
        xs, h1, 1, preps[1][0], _relayout_w_in(got[0], "relayout_w_in1"),
        lambda proj: dict(_other_weights(got[1:]), **{n: small_full[n][1] for n in SMALL_SHARDED}))
    loss_row, dx, dfw = _loss_head(xs, final_norm_w.reshape(1, D_MODEL), loss_target[0])
    loss = lax.psum(loss_row[0, 0], ("x", "y", "c"))
    loss, dx = lax.optimization_barrier((loss, dx))
    pairs, scat, sent0 = [None] * DEPTH, [None] * DEPTH, []

    def start_pairs1(g):
        pairs[1], tok = _pair_start(_layer_grad_blocks(g, 1), 1)
        return tok

    def send_chip_sums1(dh):
        scat[1], tok = _chips_start(_pair_sums(pairs[1], dh, 1), 1)
        return tok

    def send_all0(g):
        pairs[0], tok = _pair_start(_layer_grad_blocks(g, 0), 0)
        scat[0], tok = _chips_start(_pair_sums(pairs[0], tok, 0), 0)
        sent0.append(tok)
        return tok

    dx, layer_g[1] = _layer_bwd(dx, 1, saved[1], full[1], on_large_grads=start_pairs1, after_dh=send_chip_sums1)
    dx, layer_g[0] = _layer_bwd(dx, 0, saved[0], full[0], on_large_grads=send_all0)
    for i in range(DEPTH):
        dense = layer_g[i].pop('s5_dense')
        blocks = tuple(_diag_blocks(t, after=sent0[0]) for t in dense[:4])
        layer_g[i].update(zip(S5_NAMES, saved[i]['disc_vjp'](blocks + (dense[4] + sent0[0][0, 0], dense[5]))))
    grads = {n: jnp.stack([layer_g[i][n] for i in range(DEPTH)]) for n in SMALL_SHARDED + REPLICATED if n != 'final_norm_w'}
    grads['final_norm_w'] = dfw[0]

    out_g, out_d, out_m, out_v = {}, {}, {}, {}
    repl_rows = _pack_rows([grads[n] for n in REPLICATED], 8 * N_DEV)
    rr = repl_rows.shape[0] // N_DEV
    shard_rows = _pack_rows([_split8(grads[n], SHARD_AXIS[n]) for n in SMALL_SHARDED], 8, batched=True)
    rs = shard_rows.shape[1]
    small_g = jnp.concatenate([shard_rows, repl_rows.reshape(N_DEV, rr, LANES)], axis=1)
    all_to_all, all_gather = functools.partial(_plan_direct, gather=False), functools.partial(_plan_direct, gather=True)
    small_state, tok = _split_start(all_to_all, [small_g], [small_g.shape], N_DEV - 1, "scatter_small_start")
    landed1 = _split_wait(_plan_chips, scat[1], tok, "chips1_wait")
    landed0 = _split_wait(_plan_chips, scat[0], landed1[0], "chips0_wait")

    def big_adamw(n, after=None):
        k, shp = BIG_SHARDED.index(n), p[n].shape
        if n == 'w_in':
            return _adamw_w_in([landed0[k], landed1[k]], p[n], mom[n], vel[n], "adamw_w_in", after)
        c = shp[-1]
        r = p[n].size // (DEPTH * c)
        res = _adamw([landed0[k], landed1[k]], *[d[n].reshape(DEPTH, r, c) for d in (p, mom, vel)],
                     {'w_branch': 512, 'w_out': 128, 's5_w_glu': 64}[n], "adamw_" + n)
        return [o.reshape(shp) for o in res]

    for n in OTHER_BIG:
        out_g[n], out_d[n], out_m[n], out_v[n] = big_adamw(n)
    updated = sum(out_d[n].reshape(-1)[0] for n in OTHER_BIG).reshape(1, 1)
    small_sum = _slot_sum(_split_wait(all_to_all, small_state, updated, "scatter_small_wait")[0], "sum_small")
    repl_part = small_sum[rs:]
    repl_state, tok = _split_start(all_gather, [repl_part], [(N_DEV,) + repl_part.shape], N_DEV - 1, "gather_small_start")
    out_g['w_in'], out_d['w_in'], out_m['w_in'], out_v['w_in'] = big_adamw('w_in', tok)
    repl_all = _split_wait(all_gather, repl_state, out_d['w_in'], "gather_small_wait")[0].reshape(N_DEV * rr, LANES)
    g_all = jnp.concatenate([small_sum[:rs], repl_all], axis=0)
    names = SMALL_SHARDED + REPLICATED
    pieces = (_unpack_rows(g_all[:rs], [p[n].shape for n in SMALL_SHARDED])
              + _unpack_rows(g_all[rs:], [p[n].shape for n in REPLICATED]))
    out_g.update(zip(names, pieces))
    res = _adamw_many(*[[_memory_view(n, d[n]) for n in names] for d in (out_g, p, mom, vel)], "adamw_small")
    for r, dst in zip(res, (out_d, out_m, out_v)):
        dst.update({n: _memory_view(n, t) for n, t in zip(names, r)})
    return (loss, dx[None], *[out_g[n] for n in WEIGHTS], *[out_d[n] for n in WEIGHTS],
            *[out_m[n] for n in WEIGHTS], *[out_v[n] for n in WEIGHTS])
```

```python
import functools

import jax
import jax.numpy as jnp
import numpy as np
from jax import lax
from jax.experimental import pallas as pl
from jax.experimental.pallas import tpu as pltpu

F32 = jnp.float32
BF16 = jnp.bfloat16

N_DEV = 8
SEQ = 2048
D_MODEL = 1024
DEPTH = 2
BW = 512
N_BRANCH = 4
EPS = 1e-6
S5_GROUPS, S5_STATE, S5_P = 32, 64, 16
S5_CH = S5_GROUPS * S5_STATE
SGU_CHUNK, SGU_HEADS = 128, 8
M2_HEADS, M2_HEAD_DIM, M2_STATE, M2_CHUNK, M2_CONV = 8, 64, 128, 128, 4
M2_CONV_CH = 1024
SC_CONV = 3
IN_DIM = 10248
IN_PAD = 11264
C_MERGE = 0
C_S5U, C_S5G = 4096, 4608
C_M2X = 5120
C_SGU_U, C_SGU_V, C_SGU_G = 6144, 6656, 7168
C_M2Z, C_DT = 8192, 8704
C_SC = 9216
SHARD_IN = IN_DIM // N_DEV

ADAM_LR, ADAM_B1, ADAM_B2, ADAM_EPS, ADAM_WD, ADAM_STEP = 0.001, 0.9, 0.999, 1e-08, 0.01, 10

VMEM_LIMIT = 56 * 1024 * 1024
LANES = 128

MESH = pl.DeviceIdType.MESH


def _cparams(sem=None, **kw):
    return pltpu.CompilerParams(dimension_semantics=sem, vmem_limit_bytes=VMEM_LIMIT, **kw)


def _dg(a, b, ca, cb, precision=None):
    return lax.dot_general(a, b, (((ca,), (cb,)), ((), ())), precision=precision,
                           preferred_element_type=F32)


@functools.partial(jax.custom_vjp, nondiff_argnums=(2, 3))
def _bdot(a, b, ca, cb):
    return _dg(a.astype(BF16), b.astype(BF16), ca, cb)


def _bdot_fwd(a, b, ca, cb):
    return _bdot(a, b, ca, cb), (a, b)


def _bdot_bwd(ca, cb, res, g):
    a, b = res
    gb, ab, bb = g.astype(BF16), a.astype(BF16), b.astype(BF16)
    da = _dg(gb, bb, 1, 1 - cb) if ca == 1 else _dg(bb, gb, 1 - cb, 1)
    db = _dg(ab, gb, 1 - ca, 0) if cb == 0 else _dg(gb, ab, 0, 1 - ca)
    return da.astype(a.dtype), db.astype(b.dtype)


_bdot.defvjp(_bdot_fwd, _bdot_bwd)


def _rms(x, w):
    return x * lax.rsqrt(jnp.mean(x * x, axis=-1, keepdims=True) + EPS) * w


def _silu(x):
    return x * jax.nn.sigmoid(x)


def _gelu(x):
    return 0.5 * x * (1.0 + jnp.tanh(0.7978845608028654 * (x + 0.044715 * (x * x * x))))


def _softplus(x):
    return jnp.maximum(x, 0.0) + jnp.log1p(jnp.exp(-jnp.abs(x)))


def _shift_down(x, s):
    if s == 0:
        return x
    row = lax.broadcasted_iota(jnp.int32, x.shape, 0)
    return jnp.where(row >= s, pltpu.roll(x, s, 0), 0.0)


def _shift_up(x, s):
    if s == 0:
        return x
    n = x.shape[0]
    row = lax.broadcasted_iota(jnp.int32, x.shape, 0)
    return jnp.where(row < n - s, pltpu.roll(x, n - s, 0), 0.0)


def _matmul(a, b, ca, cb, out_dtype, tm, tn, tk, name, residual=None, after=None):
    m = a.shape[1 - ca]
    k = a.shape[ca]
    n = b.shape[1 - cb]
    assert b.shape[cb] == k and m % tm == 0 and n % tn == 0 and k % tk == 0
    nk = k // tk
    a_spec = pl.BlockSpec((tm, tk), lambda i, j, kk: (i, kk)) if ca == 1 else pl.BlockSpec((tk, tm), lambda i, j, kk: (kk, i))
    b_spec = pl.BlockSpec((tk, tn), lambda i, j, kk: (kk, j)) if cb == 0 else pl.BlockSpec((tn, tk), lambda i, j, kk: (j, kk))
    o_spec = pl.BlockSpec((tm, tn), lambda i, j, kk: (i, j))
    has_res = residual is not None

    def body(*refs):
        refs = refs[:2 + has_res] + refs[2 + has_res + (after is not None):]
        if has_res:
            a_ref, b_ref, r_ref, o_ref, acc = refs
        else:
            a_ref, b_ref, o_ref, acc = refs
        kk = pl.program_id(2)
        part = _dg(a_ref[...].astype(BF16), b_ref[...].astype(BF16), ca, cb)

        @pl.when(kk == 0)
        def _():
            acc[...] = part

        @pl.when(kk > 0)
        def _():
            acc[...] += part

        @pl.when(kk == nk - 1)
        def _():
            r = acc[...]
            if has_res:
                r = r + r_ref[...]
            o_ref[...] = r.astype(out_dtype)

    ins = [a, b] + ([residual] if has_res else []) + ([after] if after is not None else [])
    specs = [a_spec, b_spec] + ([o_spec] if has_res else []) + ([pl.BlockSpec(memory_space=pl.ANY)] if after is not None else [])
    return pl.pallas_call(
        body, name=name, grid=(m // tm, n // tn, nk), in_specs=specs, out_specs=o_spec,
        out_shape=jax.ShapeDtypeStruct((m, n), out_dtype),
        scratch_shapes=[pltpu.VMEM((tm, tn), F32)],
        compiler_params=_cparams(("parallel", "parallel", "arbitrary")),
    )(*ins)


ROW_TILE = 512


def _rmsnorm_fwd(x, w, name):
    def body(x_ref, w_ref, o_ref):
        o_ref[...] = _rms(x_ref[...], w_ref[...]).astype(BF16)

    return pl.pallas_call(
        body, name=name, grid=(SEQ // ROW_TILE,),
        in_specs=[pl.BlockSpec((ROW_TILE, D_MODEL), lambda i: (i, 0)), pl.BlockSpec((1, D_MODEL), lambda i: (0, 0))],
        out_specs=pl.BlockSpec((ROW_TILE, D_MODEL), lambda i: (i, 0)),
        out_shape=jax.ShapeDtypeStruct((SEQ, D_MODEL), BF16),
        compiler_params=_cparams(("parallel",)),
    )(x, w)


def _rmsnorm_bwd(x, w, dh, dres, name):
    def body(x_ref, w_ref, dh_ref, dres_ref, dx_ref, dw_ref):
        _, vjp = jax.vjp(_rms, x_ref[...], w_ref[...])
        dx, dw = vjp(dh_ref[...])
        dx_ref[...] = dx + dres_ref[...]

        @pl.when(pl.program_id(0) == 0)
        def _():
            dw_ref[...] = dw

        @pl.when(pl.program_id(0) > 0)
        def _():
            dw_ref[...] += dw

    tile = pl.BlockSpec((ROW_TILE, D_MODEL), lambda i: (i, 0))
    vec = pl.BlockSpec((1, D_MODEL), lambda i: (0, 0))
    return pl.pallas_call(
        body, name=name, grid=(SEQ // ROW_TILE,),
        in_specs=[tile, vec, tile, tile], out_specs=[tile, vec],
        out_shape=[jax.ShapeDtypeStruct((SEQ, D_MODEL), F32), jax.ShapeDtypeStruct((1, D_MODEL), F32)],
        compiler_params=_cparams(("arbitrary",)),
    )(x, w, dh, dres)


def _loss_head(x, w, target):
    def body(x_ref, w_ref, t_ref, loss_ref, dx_ref, dw_ref):
        tgt = t_ref[...]

        def f(xv, wv):
            err = _rms(xv, wv) - tgt
            return 0.5 * jnp.sum(jnp.mean(err * err, axis=-1))

        loss, vjp = jax.vjp(f, x_ref[...], w_ref[...])
        dx, dw = vjp(jnp.ones((), F32))
        dx_ref[...] = dx
        lrow = jnp.full((1, LANES), loss, F32)

        @pl.when(pl.program_id(0) == 0)
        def _():
            dw_ref[...] = dw
            loss_ref[...] = lrow

        @pl.when(pl.program_id(0) > 0)
        def _():
            dw_ref[...] += dw
            loss_ref[...] += lrow

    tile = pl.BlockSpec((ROW_TILE, D_MODEL), lambda i: (i, 0))
    vec = pl.BlockSpec((1, D_MODEL), lambda i: (0, 0))
    return pl.pallas_call(
        body, name="loss_head", grid=(SEQ // ROW_TILE,),
        in_specs=[tile, vec, tile], out_specs=[pl.BlockSpec((1, LANES), lambda i: (0, 0)), tile, vec],
        out_shape=[jax.ShapeDtypeStruct((1, LANES), F32), jax.ShapeDtypeStruct((SEQ, D_MODEL), F32),
                   jax.ShapeDtypeStruct((1, D_MODEL), F32)],
        compiler_params=_cparams(("arbitrary",)),
    )(x, w, target)


S5_T = 256
S5_BLOCKS = [(slice(j * 256, (j + 1) * 256), slice(j * 1024, (j + 1) * 1024)) for j in range(2)]


def _s5_post(ypre, gate, wglu):
    y = _gelu(ypre)
    y = y * jax.nn.sigmoid(_bdot(y, wglu, 1, 0))
    return y * _silu(gate)


def _s5_fwd(proj, bbre, bbim, cre, cim, a2, dvec, wglu, name):
    def body(u_ref, g_ref, bbre_ref, bbim_ref, cre_ref, cim_ref, a_ref, d_ref, wg_ref, o_ref, sre_ref, sim_ref, st):
        @pl.when(pl.program_id(0) == 0)
        def _():
            st[...] = jnp.zeros_like(st)

        u = u_ref[...]
        ub = u.astype(BF16)
        for us, ss in S5_BLOCKS:
            sre_ref[:, ss] = _dg(ub[:, us], bbre_ref[us, ss], 1, 0)
            sim_ref[:, ss] = _dg(ub[:, us], bbim_ref[us, ss], 1, 0)
        ar, ai = a_ref[0:1, :], a_ref[1:2, :]

        def step(t, carry):
            sr, si = carry
            nr = ar * sr - ai * si + sre_ref[pl.ds(t, 1), :]
            ni = ar * si + ai * sr + sim_ref[pl.ds(t, 1), :]
            sre_ref[pl.ds(t, 1), :] = nr
            sim_ref[pl.ds(t, 1), :] = ni
            return nr, ni

        sr, si = lax.fori_loop(0, S5_T, step, (st[0:1, :], st[1:2, :]), unroll=8)
        st[0:1, :] = sr
        st[1:2, :] = si
        ypre = jnp.concatenate(
            [_dg(sre_ref[:, ss].astype(BF16), cre_ref[ss, us], 1, 0) - _dg(sim_ref[:, ss].astype(BF16), cim_ref[ss, us], 1, 0)
             for us, ss in S5_BLOCKS], axis=1) + d_ref[...] * u
        o_ref[0] = _s5_post(ypre, g_ref[...], wg_ref[...]).astype(BF16)

    full = lambda shape: pl.BlockSpec(shape, lambda c: (0, 0))
    return pl.pallas_call(
        body, name=name, grid=(SEQ // S5_T,),
        in_specs=[pl.BlockSpec((S5_T, BW), lambda c: (c, C_S5U // BW)), pl.BlockSpec((S5_T, BW), lambda c: (c, C_S5G // BW)),
                  full((BW, S5_CH)), full((BW, S5_CH)), full((S5_CH, BW)), full((S5_CH, BW)),
                  full((2, S5_CH)), full((1, BW)), full((BW, BW))],
        out_specs=[pl.BlockSpec((1, S5_T, BW), lambda c: (0, c, 0)), pl.BlockSpec((S5_T, S5_CH), lambda c: (c, 0)),
                   pl.BlockSpec((S5_T, S5_CH), lambda c: (c, 0))],
        out_shape=[jax.ShapeDtypeStruct((N_BRANCH, SEQ, BW), BF16), jax.ShapeDtypeStruct((SEQ, S5_CH), F32),
                   jax.ShapeDtypeStruct((SEQ, S5_CH), F32)],
        scratch_shapes=[pltpu.VMEM((2, S5_CH), F32)],
        compiler_params=_cparams(("arbitrary",)),
    )(proj, proj, bbre, bbim, cre, cim, a2, dvec, wglu)


def _s5_bwd(proj, dproj, dout, sre, sim, bbre, bbim, cre, cim, a2, dvec, wglu, name):
    nc = SEQ // S5_T

    def body(u_ref, g_ref, do_ref, sre_ref, sim_ref, pre_ref, pim_ref, bbre_ref, bbim_ref, cre_ref, cim_ref, a_ref,
             d_ref, wg_ref, dproj_in, dp_ref, dbbre_ref, dbbim_ref, dcre_ref, dcim_ref, da_ref, dd_ref, dwg_ref,
             gre, gim, st):
        c = nc - 1 - pl.program_id(0)

        @pl.when(pl.program_id(0) == 0)
        def _():
            st[...] = jnp.zeros_like(st)
            for r in (dbbre_ref, dbbim_ref, dcre_ref, dcim_ref, da_ref, dd_ref, dwg_ref):
                r[...] = jnp.zeros_like(r)

        u = u_ref[...]
        s_re, s_im = sre_ref[...], sim_ref[...]

        def head(s_res, s_ims, cres, cims, dv, uv, gv, wg):
            ypre = jnp.concatenate([_bdot(sr, cr, 1, 0) - _bdot(si, ci, 1, 0)
                                    for sr, si, cr, ci in zip(s_res, s_ims, cres, cims)], axis=1) + dv * uv
            return _s5_post(ypre, gv, wg)

        _, vjp = jax.vjp(head, [sre_ref[:, ss] for _, ss in S5_BLOCKS], [sim_ref[:, ss] for _, ss in S5_BLOCKS],
                         [cre_ref[ss, us].astype(F32) for us, ss in S5_BLOCKS],
                         [cim_ref[ss, us].astype(F32) for us, ss in S5_BLOCKS],
                         d_ref[...], u, g_ref[...], wg_ref[...].astype(F32))
        ds_res, ds_ims, dcres, dcims, dd, du_d, dgate, dwg = vjp(do_ref[0])
        for k, (us, ss) in enumerate(S5_BLOCKS):
            dcre_ref[ss, us] += dcres[k]
            dcim_ref[ss, us] += dcims[k]
            gre[:, ss] = ds_res[k]
            gim[:, ss] = ds_ims[k]
        dd_ref[...] += dd
        dwg_ref[...] += dwg
        dp_ref[:, BW:] = dgate.astype(BF16)
        ar, ai = a_ref[0:1, :], a_ref[1:2, :]

        def step(i, carry):
            t = S5_T - 1 - i
            gr, gi = carry
            nr = gre[pl.ds(t, 1), :] + gr
            ni = gim[pl.ds(t, 1), :] + gi
            gre[pl.ds(t, 1), :] = nr
            gim[pl.ds(t, 1), :] = ni
            return ar * nr + ai * ni, ar * ni - ai * nr

        gr, gi = lax.fori_loop(0, S5_T, step, (st[0:1, :], st[1:2, :]), unroll=8)
        st[0:1, :] = gr
        st[1:2, :] = gi
        g_re, g_im = gre[...], gim[...]
        first = jnp.where(c > 0, 1.0, 0.0)
        row = lax.broadcasted_iota(jnp.int32, (S5_T, S5_CH), 0)
        p_re = jnp.where(row == 0, pre_ref[7:8, :] * first, pltpu.roll(s_re, 1, 0))
        p_im = jnp.where(row == 0, pim_ref[7:8, :] * first, pltpu.roll(s_im, 1, 0))
        da_ref[0:1, :] += jnp.sum(g_re * p_re + g_im * p_im, axis=0, keepdims=True)
        da_ref[1:2, :] += jnp.sum(g_im * p_re - g_re * p_im, axis=0, keepdims=True)
        ub, grb, gib = u.astype(BF16), g_re.astype(BF16), g_im.astype(BF16)
        du_s = []
        for us, ss in S5_BLOCKS:
            dbbre_ref[us, ss] += _dg(ub[:, us], grb[:, ss], 0, 0)
            dbbim_ref[us, ss] += _dg(ub[:, us], gib[:, ss], 0, 0)
            du_s.append(_dg(grb[:, ss], bbre_ref[us, ss], 1, 1) + _dg(gib[:, ss], bbim_ref[us, ss], 1, 1))
        dp_ref[:, :BW] = (du_d + jnp.concatenate(du_s, axis=1)).astype(BF16)

    full = lambda shape: pl.BlockSpec(shape, lambda i: (0, 0))
    rev = lambda w, col=0: pl.BlockSpec((S5_T, w), lambda i: (nc - 1 - i, col))
    prev = pl.BlockSpec((8, S5_CH), lambda i: (jnp.maximum((nc - 1 - i) * (S5_T // 8) - 1, 0), 0))
    return pl.pallas_call(
        body, name=name, grid=(nc,),
        in_specs=[rev(BW, C_S5U // BW), rev(BW, C_S5G // BW), pl.BlockSpec((1, S5_T, BW), lambda i: (0, nc - 1 - i, 0)),
                  rev(S5_CH), rev(S5_CH), prev, prev,
                  full((BW, S5_CH)), full((BW, S5_CH)), full((S5_CH, BW)), full((S5_CH, BW)),
                  full((2, S5_CH)), full((1, BW)), full((BW, BW)), pl.BlockSpec(memory_space=pl.ANY)],
        out_specs=[rev(2 * BW, C_S5U // (2 * BW)), full((BW, S5_CH)), full((BW, S5_CH)), full((S5_CH, BW)), full((S5_CH, BW)),
                   full((2, S5_CH)), full((1, BW)), full((BW, BW))],
        input_output_aliases={14: 0},
        out_shape=[jax.ShapeDtypeStruct((SEQ, IN_PAD), BF16),
                   jax.ShapeDtypeStruct((BW, S5_CH), F32), jax.ShapeDtypeStruct((BW, S5_CH), F32),
                   jax.ShapeDtypeStruct((S5_CH, BW), F32), jax.ShapeDtypeStruct((S5_CH, BW), F32),
                   jax.ShapeDtypeStruct((2, S5_CH), F32), jax.ShapeDtypeStruct((1, BW), F32),
                   jax.ShapeDtypeStruct((BW, BW), F32)],
        scratch_shapes=[pltpu.VMEM((S5_T, S5_CH), F32), pltpu.VMEM((S5_T, S5_CH), F32), pltpu.VMEM((2, S5_CH), F32)],
        compiler_params=_cparams(("arbitrary",)),
    )(proj, proj, dout, sre, sim, sre, sim, bbre, bbim, cre, cim, a2, dvec, wglu, dproj)


def _diag_blocks(dense, after=None):
    rows, cols = dense.shape
    rows_per, cols_per = rows // S5_GROUPS, cols // S5_GROUPS
    per_lane_block = LANES // cols_per
    tile = 512

    def body(d_ref, *rest):
        o_ref = rest[-1]
        r0 = pl.program_id(0) * tile
        grp = (r0 + lax.broadcasted_iota(jnp.int32, (tile, LANES), 0)) // rows_per
        lane = lax.broadcasted_iota(jnp.int32, (tile, LANES), 1)
        acc = jnp.zeros((tile, LANES), F32)
        for hb in range(cols // LANES):
            acc = acc + jnp.where(grp == per_lane_block * hb + lane // cols_per, d_ref[:, hb * LANES:(hb + 1) * LANES], 0.0)
        shift = LANES // 2
        while shift >= cols_per:
            acc = acc + pltpu.roll(acc, LANES - shift, 1)
            shift //= 2
        o_ref[...] = acc

    folded = pl.pallas_call(
        body, name=f"diag_blocks_{rows_per}x{cols_per}", grid=(rows // tile,),
        in_specs=[pl.BlockSpec((tile, cols), lambda i: (i, 0))] + ([] if after is None else [pl.BlockSpec(memory_space=pl.ANY)]),
        out_specs=pl.BlockSpec((tile, LANES), lambda i: (i, 0)),
        out_shape=jax.ShapeDtypeStruct((rows, LANES), F32), compiler_params=_cparams(("parallel",)),
    )(dense, *([] if after is None else [after]))
    return folded[:, :cols_per].reshape(S5_GROUPS, rows_per, cols_per)


def _block_diag(t):
    g, rows_per, cols_per = t.shape
    wide = jnp.tile(t.reshape(g * rows_per, cols_per), (1, g))
    r = lax.broadcasted_iota(jnp.int32, wide.shape, 0) // rows_per
    c = lax.broadcasted_iota(jnp.int32, wide.shape, 1) // cols_per
    return jnp.where(r == c, wide, 0.0)


def _s5_disc(lam_re, lam_im, b_re, b_im, c_re, c_im, d, log_step):
    step = jnp.exp(log_step)[:, None]
    mag = jnp.exp(lam_re * step)
    ab_re, ab_im = mag * jnp.cos(lam_im * step), mag * jnp.sin(lam_im * step)
    den = lam_re * lam_re + lam_im * lam_im
    nr = ab_re - 1.0
    coef_re = (nr * lam_re + ab_im * lam_im) / den
    coef_im = (ab_im * lam_re - nr * lam_im) / den
    bb_re = coef_re[..., None] * b_re - coef_im[..., None] * b_im
    bb_im = coef_re[..., None] * b_im + coef_im[..., None] * b_re
    a2 = jnp.stack([ab_re.reshape(-1), ab_im.reshape(-1)])
    return (jnp.swapaxes(bb_re, 1, 2), jnp.swapaxes(bb_im, 1, 2),
            jnp.swapaxes(c_re, 1, 2), jnp.swapaxes(c_im, 1, 2),
            a2, d.reshape(1, BW))


def _left_lanes(shape):
    return lax.broadcasted_iota(jnp.int32, shape, 1) < 64


def _sgu_chunk(u, v, gate, ln_w, ln_b, w, bias):
    u32, v32 = _gelu(u), _gelu(v)
    mu = jnp.mean(v32, axis=-1, keepdims=True)
    var = jnp.mean(jnp.square(v32 - mu), axis=-1, keepdims=True)
    vn = (v32 - mu) * lax.rsqrt(var + EPS) * ln_w + ln_b
    t_i = lax.broadcasted_iota(jnp.int32, (SGU_CHUNK, SGU_CHUNK), 0)
    s_i = lax.broadcasted_iota(jnp.int32, (SGU_CHUNK, SGU_CHUNK), 1)
    causal = t_i >= s_i
    left = _left_lanes((SGU_CHUNK, LANES))
    sgate = _silu(gate)
    outs = []
    for j in range(BW // LANES):
        vb = vn[:, j * LANES:(j + 1) * LANES]
        s_blk = (_bdot(jnp.where(causal, w[2 * j], 0.0), jnp.where(left, vb, 0.0), 1, 0)
                 + _bdot(jnp.where(causal, w[2 * j + 1], 0.0), jnp.where(left, 0.0, vb), 1, 0))
        sl = slice(j * LANES, (j + 1) * LANES)
        outs.append(u32[:, sl] * (s_blk + bias[:, sl]) * sgate[:, sl])
    return outs


def _sgu_fwd(proj, ys, ln_w, ln_b, w, bias, name):
    def body(u_ref, v_ref, g_ref, lw_ref, lb_ref, w_ref, b_ref, ys_in, o_ref):
        outs = _sgu_chunk(u_ref[...], v_ref[...], g_ref[...], lw_ref[...], lb_ref[...], w_ref[...], b_ref[...])
        for j, o in enumerate(outs):
            o_ref[0, :, j * LANES:(j + 1) * LANES] = o.astype(BF16)

    blk = lambda col: pl.BlockSpec((SGU_CHUNK, BW), lambda c: (c, col // BW))
    vec = pl.BlockSpec((1, BW), lambda c: (0, 0))
    return pl.pallas_call(
        body, name=name, grid=(SEQ // SGU_CHUNK,),
        in_specs=[blk(C_SGU_U), blk(C_SGU_V), blk(C_SGU_G), vec, vec,
                  pl.BlockSpec((SGU_HEADS, SGU_CHUNK, SGU_CHUNK), lambda c: (0, 0, 0)),
                  pl.BlockSpec((SGU_CHUNK, BW), lambda c: (0, 0)), pl.BlockSpec(memory_space=pl.ANY)],
        out_specs=pl.BlockSpec((1, SGU_CHUNK, BW), lambda c: (1, c, 0)),
        out_shape=jax.ShapeDtypeStruct((N_BRANCH, SEQ, BW), BF16), input_output_aliases={7: 0},
        compiler_params=_cparams(("parallel",)),
    )(proj, proj, proj, ln_w, ln_b, w, bias, ys)


def _sgu_bwd(proj, dproj, dout, ln_w, ln_b, w, bias, name):
    def body(u_ref, v_ref, g_ref, do_ref, lw_ref, lb_ref, w_ref, b_ref, dproj_in, dp_ref, dlw_ref, dlb_ref, dw_ref, db_ref):
        _, vjp = jax.vjp(_sgu_chunk, u_ref[...], v_ref[...], g_ref[...], lw_ref[...], lb_ref[...], w_ref[...], b_ref[...])
        do = do_ref[0]
        du, dv, dgate, dlw, dlb, dw, db = vjp([do[:, j * LANES:(j + 1) * LANES] for j in range(BW // LANES)])
        dp_ref[:, 0:BW] = du.astype(BF16)
        dp_ref[:, BW:2 * BW] = dv.astype(BF16)
        dp_ref[:, 2 * BW:3 * BW] = dgate.astype(BF16)
        dp_ref[:, 3 * BW:] = jnp.zeros((SGU_CHUNK, BW), BF16)

        @pl.when(pl.program_id(0) == 0)
        def _():
            dlw_ref[...] = dlw
            dlb_ref[...] = dlb
            dw_ref[...] = dw
            db_ref[...] = db

        @pl.when(pl.program_id(0) > 0)
        def _():
            dlw_ref[...] += dlw
            dlb_ref[...] += dlb
            dw_ref[...] += dw
            db_ref[...] += db

    blk = lambda col: pl.BlockSpec((SGU_CHUNK, BW), lambda c: (c, col // BW))
    vec = pl.BlockSpec((1, BW), lambda c: (0, 0))
    wsp = pl.BlockSpec((SGU_HEADS, SGU_CHUNK, SGU_CHUNK), lambda c: (0, 0, 0))
    bsp = pl.BlockSpec((SGU_CHUNK, BW), lambda c: (0, 0))
    return pl.pallas_call(
        body, name=name, grid=(SEQ // SGU_CHUNK,),
        in_specs=[blk(C_SGU_U), blk(C_SGU_V), blk(C_SGU_G), pl.BlockSpec((1, SGU_CHUNK, BW), lambda c: (1, c, 0)),
                  vec, vec, wsp, bsp, pl.BlockSpec(memory_space=pl.ANY)],
        out_specs=[pl.BlockSpec((SGU_CHUNK, 4 * BW), lambda c: (c, C_SGU_U // (4 * BW))), vec, vec, wsp, bsp],
        input_output_aliases={8: 0},
        out_shape=[jax.ShapeDtypeStruct((SEQ, IN_PAD), BF16), jax.ShapeDtypeStruct((1, BW), F32),
                   jax.ShapeDtypeStruct((1, BW), F32), jax.ShapeDtypeStruct((SGU_HEADS, SGU_CHUNK, SGU_CHUNK), F32),
                   jax.ShapeDtypeStruct((SGU_CHUNK, BW), F32)],
        compiler_params=_cparams(("arbitrary",)),
    )(proj, proj, proj, dout, ln_w, ln_b, w, bias, dproj)


CONV_BLK = 256


def _m2_conv_fwd(proj, w, b, name):
    def body(x_ref, w_ref, b_ref, o_ref):
        x = x_ref[...]
        acc = jnp.zeros_like(x) + b_ref[...]
        for k in range(M2_CONV):
            acc = acc + w_ref[k:k + 1, :] * _shift_down(x, M2_CONV - 1 - k)
        o_ref[...] = _silu(acc)

    return pl.pallas_call(
        body, name=name, grid=(M2_CONV_CH // CONV_BLK,),
        in_specs=[pl.BlockSpec((SEQ, CONV_BLK), lambda j: (0, C_M2X // CONV_BLK + j)),
                  pl.BlockSpec((M2_CONV, CONV_BLK), lambda j: (0, j)), pl.BlockSpec((1, CONV_BLK), lambda j: (0, j))],
        out_specs=pl.BlockSpec((SEQ, CONV_BLK), lambda j: (0, j)),
        out_shape=jax.ShapeDtypeStruct((SEQ, M2_CONV_CH), F32),
        compiler_params=_cparams(("parallel",)),
    )(proj, w, b)


def _m2_conv_bwd(proj, dproj, dxa, w, b, name):
    def body(x_ref, d_ref, w_ref, b_ref, dproj_in, dx_ref, dw_ref, db_ref):
        x = x_ref[...]
        xs = [_shift_down(x, M2_CONV - 1 - k) for k in range(M2_CONV)]
        acc = jnp.zeros_like(x) + b_ref[...]
        for k in range(M2_CONV):
            acc = acc + w_ref[k:k + 1, :] * xs[k]
        sg = jax.nn.sigmoid(acc)
        dacc = d_ref[...] * (sg * (1.0 + acc * (1.0 - sg)))
        dx = jnp.zeros_like(x)
        for k in range(M2_CONV):
            dx = dx + w_ref[k:k + 1, :] * _shift_up(dacc, M2_CONV - 1 - k)
            dw_ref[k:k + 1, :] = jnp.sum(dacc * xs[k], axis=0, keepdims=True)
        dx_ref[...] = dx.astype(BF16)
        db_ref[...] = jnp.sum(dacc, axis=0, keepdims=True)

    return pl.pallas_call(
        body, name=name, grid=(M2_CONV_CH // CONV_BLK,),
        in_specs=[pl.BlockSpec((SEQ, CONV_BLK), lambda j: (0, C_M2X // CONV_BLK + j)),
                  pl.BlockSpec((SEQ, CONV_BLK), lambda j: (0, j)),
                  pl.BlockSpec((M2_CONV, CONV_BLK), lambda j: (0, j)), pl.BlockSpec((1, CONV_BLK), lambda j: (0, j)),
                  pl.BlockSpec(memory_space=pl.ANY)],
        out_specs=[pl.BlockSpec((SEQ, CONV_BLK), lambda j: (0, C_M2X // CONV_BLK + j)),
                   pl.BlockSpec((M2_CONV, CONV_BLK), lambda j: (0, j)), pl.BlockSpec((1, CONV_BLK), lambda j: (0, j))],
        input_output_aliases={4: 0},
        out_shape=[jax.ShapeDtypeStruct((SEQ, IN_PAD), BF16), jax.ShapeDtypeStruct((M2_CONV, M2_CONV_CH), F32),
                   jax.ShapeDtypeStruct((1, M2_CONV_CH), F32)],
        compiler_params=_cparams(("parallel",)),
    )(proj, dxa, w, b, dproj)


N_PAIR = M2_HEADS // 2
HI = lax.Precision.HIGHEST


def _col(a, h):
    lane = lax.broadcasted_iota(jnp.int32, a.shape, 1)
    return jnp.sum(jnp.where(lane == h, a, 0.0), axis=1, keepdims=True)


def _row(a, h):
    sub = lax.broadcasted_iota(jnp.int32, a.shape, 0)
    return jnp.sum(jnp.where(sub == h, a, 0.0), axis=0, keepdims=True)


def _ssd_chunk(xs, bms, cms, dtr, zs, states, dt_bias, a_log, dfs, nws):
    q = M2_CHUNK
    dt = _softplus(dtr + dt_bias)
    da = dt * (-jnp.exp(a_log))
    l_i = lax.broadcasted_iota(jnp.int32, (q, q), 0)
    s_i = lax.broadcasted_iota(jnp.int32, (q, q), 1)
    causal = l_i >= s_i
    tril = jnp.where(causal, 1.0, 0.0)
    a_cs = _dg(tril, da, 1, 0, HI)
    a_cs_t = _dg(da, tril, 0, 1, HI)
    a_end = _row(a_cs, q - 1)
    left = _left_lanes((q, LANES))
    left1 = _left_lanes((1, LANES))
    ys, nexts = [], []
    for j in range(N_PAIR):
        grp = j // 2
        bm, cm = bms[grp], cms[grp]
        h0, h1 = 2 * j, 2 * j + 1
        cb = _bdot(cm, bm, 1, 1)
        xdt = xs[j] * jnp.where(left, _col(dt, h0), _col(dt, h1))
        acs0, acs1 = _col(a_cs, h0), _col(a_cs, h1)
        y = _bdot(cm, states[j], 1, 0) * jnp.where(left, jnp.exp(acs0), jnp.exp(acs1))
        s_new = states[j] * jnp.where(left1, jnp.exp(_col(a_end, h0)), jnp.exp(_col(a_end, h1)))
        for h, acs, xh in ((h0, acs0, jnp.where(left, xdt, 0.0)), (h1, acs1, jnp.where(left, 0.0, xdt))):
            decay = jnp.exp(jnp.where(causal, acs - _row(a_cs_t, h), -jnp.inf))
            y = y + _bdot(cb * decay, xh, 1, 0)
            s_new = s_new + _bdot(bm * jnp.exp(_col(a_end, h) - acs), xh, 0, 0)
        ys.append((y + dfs[j] * xs[j]) * _silu(zs[j]))
        nexts.append(s_new)
    ssq = sum(jnp.sum(y * y, axis=-1, keepdims=True) for y in ys)
    scale = lax.rsqrt(ssq / BW + EPS)
    return [y * scale * nw for y, nw in zip(ys, nws)], nexts


def _blocks(ref, n, width=LANES):
    return [ref[:, j * width:(j + 1) * width] for j in range(n)]


def _ssd_fwd(proj, ys, xa, dt_bias, a_log, dfull, nw, name):
    nc = SEQ // M2_CHUNK

    def body(x_ref, b_ref, c_ref, dt_ref, z_ref, dtb_ref, al_ref, df_ref, nw_ref, ys_in, o_ref, sin_ref, st):
        @pl.when(pl.program_id(0) == 0)
        def _():
            st[...] = jnp.zeros_like(st)

        states = [st[j] for j in range(N_PAIR)]
        for j in range(N_PAIR):
            sin_ref[0, j] = states[j]
        ys, nexts = _ssd_chunk(_blocks(x_ref, 4), _blocks(b_ref, 2), _blocks(c_ref, 2), dt_ref[...], _blocks(z_ref, 4),
                               states, dtb_ref[...], al_ref[...], _blocks(df_ref, 4), _blocks(nw_ref, 4))
        for j in range(N_PAIR):
            o_ref[0, :, j * LANES:(j + 1) * LANES] = ys[j].astype(BF16)
            st[j] = nexts[j]

    vec8 = pl.BlockSpec((1, LANES), lambda c: (0, 0))
    vec = pl.BlockSpec((1, BW), lambda c: (0, 0))
    return pl.pallas_call(
        body, name=name, grid=(nc,),
        in_specs=[pl.BlockSpec((M2_CHUNK, BW), lambda c: (c, 0)), pl.BlockSpec((M2_CHUNK, 256), lambda c: (c, 2)),
                  pl.BlockSpec((M2_CHUNK, 256), lambda c: (c, 3)), pl.BlockSpec((M2_CHUNK, LANES), lambda c: (c, C_DT // LANES)),
                  pl.BlockSpec((M2_CHUNK, BW), lambda c: (c, C_M2Z // BW)), vec8, vec8, vec, vec,
                  pl.BlockSpec(memory_space=pl.ANY)],
        out_specs=[pl.BlockSpec((1, M2_CHUNK, BW), lambda c: (2, c, 0)),
                   pl.BlockSpec((1, N_PAIR, M2_STATE, LANES), lambda c: (c, 0, 0, 0))],
        out_shape=[jax.ShapeDtypeStruct((N_BRANCH, SEQ, BW), BF16), jax.ShapeDtypeStruct((nc, N_PAIR, M2_STATE, LANES), F32)],
        input_output_aliases={9: 0},
        scratch_shapes=[pltpu.VMEM((N_PAIR, M2_STATE, LANES), F32)],
        compiler_params=_cparams(("arbitrary",)),
    )(xa, xa, xa, proj, proj, dt_bias, a_log, dfull, nw, ys)


def _ssd_bwd(proj, dproj, xa, dout, s_in, dt_bias, a_log, dfull, nw, name):
    nc = SEQ // M2_CHUNK

    def body(x_ref, b_ref, c_ref, dt_ref, z_ref, do_ref, sin_ref, dtb_ref, al_ref, df_ref, nw_ref, dproj_in,
             dp_ref, dxa_ref, ddtb_ref, dal_ref, ddf_ref, dnw_ref, dst):
        @pl.when(pl.program_id(0) == 0)
        def _():
            dst[...] = jnp.zeros_like(dst)
            for r in (ddtb_ref, dal_ref, ddf_ref, dnw_ref):
                r[...] = jnp.zeros_like(r)

        states = [sin_ref[0, j] for j in range(N_PAIR)]
        _, vjp = jax.vjp(_ssd_chunk, _blocks(x_ref, 4), _blocks(b_ref, 2), _blocks(c_ref, 2), dt_ref[...],
                         _blocks(z_ref, 4), states, dtb_ref[...], al_ref[...], _blocks(df_ref, 4), _blocks(nw_ref, 4))
        dxs, dbs, dcs, ddt, dzs, dstates, ddtb, dal, ddfs, dnws = vjp(
            ([do_ref[0, :, j * LANES:(j + 1) * LANES] for j in range(N_PAIR)], [dst[j] for j in range(N_PAIR)]))
        for j in range(N_PAIR):
            sl = slice(j * LANES, (j + 1) * LANES)
            dxa_ref[:, sl] = dxs[j]
            dp_ref[:, sl] = dzs[j].astype(BF16)
            dst[j] = dstates[j]
            ddf_ref[:, sl] += ddfs[j]
            dnw_ref[:, sl] += dnws[j]
        for g in range(2):
            dxa_ref[:, BW + g * LANES:BW + (g + 1) * LANES] = dbs[g]
            dxa_ref[:, BW + 256 + g * LANES:BW + 256 + (g + 1) * LANES] = dcs[g]
        dp_ref[:, BW:BW + LANES] = ddt.astype(BF16)
        dp_ref[:, BW + LANES:] = jnp.zeros((M2_CHUNK, 2 * BW - BW - LANES), BF16)
        ddtb_ref[...] += ddtb
        dal_ref[...] += dal

    rev = lambda w, col=0: pl.BlockSpec((M2_CHUNK, w), lambda i: (nc - 1 - i, col))
    vec8 = pl.BlockSpec((1, LANES), lambda i: (0, 0))
    vec = pl.BlockSpec((1, BW), lambda i: (0, 0))
    return pl.pallas_call(
        body, name=name, grid=(nc,),
        in_specs=[rev(BW), rev(256, 2), rev(256, 3), rev(LANES, C_DT // LANES), rev(BW, C_M2Z // BW),
                  pl.BlockSpec((1, M2_CHUNK, BW), lambda i: (2, nc - 1 - i, 0)),
                  pl.BlockSpec((1, N_PAIR, M2_STATE, LANES), lambda i: (nc - 1 - i, 0, 0, 0)), vec8, vec8, vec, vec,
                  pl.BlockSpec(memory_space=pl.ANY)],
        out_specs=[rev(2 * BW, C_M2Z // (2 * BW)), rev(M2_CONV_CH), vec8, vec8, vec, vec],
        input_output_aliases={11: 0},
        out_shape=[jax.ShapeDtypeStruct((SEQ, IN_PAD), BF16), jax.ShapeDtypeStruct((SEQ, M2_CONV_CH), F32),
                   jax.ShapeDtypeStruct((1, LANES), F32), jax.ShapeDtypeStruct((1, LANES), F32),
                   jax.ShapeDtypeStruct((1, BW), F32), jax.ShapeDtypeStruct((1, BW), F32)],
        scratch_shapes=[pltpu.VMEM((N_PAIR, M2_STATE, LANES), F32)],
        compiler_params=_cparams(("arbitrary",)),
    )(xa, xa, xa, proj, proj, dout, s_in, dt_bias, a_log, dfull, nw, dproj)


def _sc_specs():
    col = lambda kind: pl.BlockSpec((SEQ, LANES), lambda j: (0, C_SC // LANES + 4 * j + kind))
    return [col(0), col(1), col(2), col(3)]


def _sc_fwd(proj, ys, w, name):
    def body(b_ref, c_ref, h_ref, g_ref, w_ref, ys_in, o_ref):
        ch = c_ref[...] * h_ref[...]
        acc = jnp.zeros_like(ch)
        for k in range(SC_CONV):
            acc = acc + w_ref[k:k + 1, :] * _shift_down(ch, SC_CONV - 1 - k)
        o_ref[0] = (b_ref[...] * acc * _silu(g_ref[...])).astype(BF16)

    return pl.pallas_call(
        body, name=name, grid=(BW // LANES,),
        in_specs=_sc_specs() + [pl.BlockSpec((SC_CONV, LANES), lambda j: (0, j)), pl.BlockSpec(memory_space=pl.ANY)],
        out_specs=pl.BlockSpec((1, SEQ, LANES), lambda j: (3, 0, j)),
        out_shape=jax.ShapeDtypeStruct((N_BRANCH, SEQ, BW), BF16), input_output_aliases={5: 0},
        compiler_params=_cparams(("parallel",)),
    )(proj, proj, proj, proj, w, ys)


def _sc_bwd(proj, dproj, dout, w, name):
    def body(b_ref, c_ref, h_ref, g_ref, do_ref, w_ref, dproj_in, dp_ref, dw_ref):
        cv, hv, gv = c_ref[...], h_ref[...], g_ref[...]
        ch = cv * hv
        chs = [_shift_down(ch, SC_CONV - 1 - k) for k in range(SC_CONV)]
        acc = jnp.zeros_like(ch)
        for k in range(SC_CONV):
            acc = acc + w_ref[k:k + 1, :] * chs[k]
        sg = jax.nn.sigmoid(gv)
        do = do_ref[0]
        bv = b_ref[...]
        dp_ref[:, 0:LANES] = (do * acc * (gv * sg)).astype(BF16)
        dp_ref[:, 3 * LANES:] = (do * bv * acc * (sg * (1.0 + gv * (1.0 - sg)))).astype(BF16)
        dacc = do * bv * (gv * sg)
        dch = jnp.zeros_like(ch)
        for k in range(SC_CONV):
            dch = dch + w_ref[k:k + 1, :] * _shift_up(dacc, SC_CONV - 1 - k)
            dw_ref[k:k + 1, :] = jnp.sum(dacc * chs[k], axis=0, keepdims=True)
        dp_ref[:, LANES:2 * LANES] = (dch * hv).astype(BF16)
        dp_ref[:, 2 * LANES:3 * LANES] = (dch * cv).astype(BF16)

    wsp = pl.BlockSpec((SC_CONV, LANES), lambda j: (0, j))
    return pl.pallas_call(
        body, name=name, grid=(BW // LANES,),
        in_specs=_sc_specs() + [pl.BlockSpec((1, SEQ, LANES), lambda j: (3, 0, j)), wsp, pl.BlockSpec(memory_space=pl.ANY)],
        out_specs=[pl.BlockSpec((SEQ, 4 * LANES), lambda j: (0, C_SC // (4 * LANES) + j)), wsp],
        input_output_aliases={6: 0},
        out_shape=[jax.ShapeDtypeStruct((SEQ, IN_PAD), BF16), jax.ShapeDtypeStruct((SC_CONV, BW), F32)],
        compiler_params=_cparams(("parallel",)),
    )(proj, proj, proj, proj, dout, w, dproj)


MERGE_T = 256
MERGE_BWD_T = 1024


def _merge_fwd(proj, ys, merge_b, w_branch, name):
    def body(y_ref, lg_ref, b_ref, w_ref, o_ref):
        acc = jnp.zeros((MERGE_T, D_MODEL), F32)
        for k in range(N_BRANCH):
            gate = jax.nn.sigmoid(lg_ref[:, k * D_MODEL:(k + 1) * D_MODEL] + b_ref[k])
            acc = acc + gate * _dg(y_ref[k], w_ref[k], 1, 0)
        o_ref[...] = acc.astype(BF16)

    return pl.pallas_call(
        body, name=name, grid=(SEQ // MERGE_T,),
        in_specs=[pl.BlockSpec((N_BRANCH, MERGE_T, BW), lambda i: (0, i, 0)),
                  pl.BlockSpec((MERGE_T, N_BRANCH * D_MODEL), lambda i: (i, C_MERGE // (N_BRANCH * D_MODEL))),
                  pl.BlockSpec((N_BRANCH, 1, D_MODEL), lambda i: (0, 0, 0)),
                  pl.BlockSpec((N_BRANCH, BW, D_MODEL), lambda i: (0, 0, 0))],
        out_specs=pl.BlockSpec((MERGE_T, D_MODEL), lambda i: (i, 0)),
        out_shape=jax.ShapeDtypeStruct((SEQ, D_MODEL), BF16),
        compiler_params=_cparams(("parallel",)),
    )(ys, proj, merge_b, w_branch)


def _merge_bwd(proj, ys, dm, merge_b, w_branch, name):
    nt = SEQ // MERGE_BWD_T

    def body(y_ref, lg_ref, dm_ref, b_ref, w_ref, dy_ref, dlg_ref, dw_ref, db_ref, dw_acc):
        i = pl.program_id(1)
        gate = jax.nn.sigmoid(lg_ref[...] + b_ref[0])
        y = y_ref[0]
        dmv = dm_ref[...]
        dbo = (gate * dmv).astype(BF16)
        dlg = _dg(y, w_ref[0], 1, 0) * dmv * gate * (1.0 - gate)
        dlg_ref[...] = dlg.astype(BF16)
        dy_ref[0] = _dg(dbo, w_ref[0], 1, 1)
        dwp = _dg(y, dbo, 0, 0)
        dbp = jnp.sum(dlg, axis=0, keepdims=True)

        @pl.when(i == 0)
        def _():
            dw_acc[...] = dwp
            db_ref[0] = dbp

        @pl.when(i > 0)
        def _():
            dw_acc[...] += dwp
            db_ref[0] += dbp

        @pl.when(i == nt - 1)
        def _():
            dw_ref[0] = dw_acc[...].astype(BF16)

    return pl.pallas_call(
        body, name=name, grid=(N_BRANCH, nt),
        in_specs=[pl.BlockSpec((1, MERGE_BWD_T, BW), lambda k, i: (k, i, 0)),
                  pl.BlockSpec((MERGE_BWD_T, D_MODEL), lambda k, i: (i, C_MERGE // D_MODEL + k)),
                  pl.BlockSpec((MERGE_BWD_T, D_MODEL), lambda k, i: (i, 0)),
                  pl.BlockSpec((1, 1, D_MODEL), lambda k, i: (k, 0, 0)),
                  pl.BlockSpec((1, BW, D_MODEL), lambda k, i: (k, 0, 0))],
        out_specs=[pl.BlockSpec((1, MERGE_BWD_T, BW), lambda k, i: (k, i, 0)),
                   pl.BlockSpec((MERGE_BWD_T, D_MODEL), lambda k, i: (i, k)),
                   pl.BlockSpec((1, BW, D_MODEL), lambda k, i: (k, 0, 0)),
                   pl.BlockSpec((1, 1, D_MODEL), lambda k, i: (k, 0, 0))],
        out_shape=[jax.ShapeDtypeStruct((N_BRANCH, SEQ, BW), F32), jax.ShapeDtypeStruct((SEQ, IN_PAD), BF16),
                   jax.ShapeDtypeStruct((N_BRANCH, BW, D_MODEL), BF16), jax.ShapeDtypeStruct((N_BRANCH, 1, D_MODEL), F32)],
        scratch_shapes=[pltpu.VMEM((BW, D_MODEL), F32)],
        compiler_params=_cparams(("parallel", "arbitrary")),
    )(ys, proj, dm, merge_b, w_branch)


def _adamw(glist, w, m, v, rows, name):
    nl = len(glist)
    n, r, c = glist[0].shape
    assert w.shape == (nl, r, c) and r % rows == 0
    nb = r // rows

    def body(*refs):
        g_refs = refs[:nl]
        w_ref, m_ref, v_ref, go_ref, d_ref, mo_ref, vo_ref = refs[nl:]
        for layer in range(nl):
            @pl.when(pl.program_id(0) == layer)
            def _(g_ref=g_refs[layer]):
                g = g_ref[0].astype(F32)
                for s in range(1, n):
                    g = g + g_ref[s].astype(F32)
                mn = ADAM_B1 * m_ref[0] + (1.0 - ADAM_B1) * g
                vn = ADAM_B2 * v_ref[0] + (1.0 - ADAM_B2) * jnp.square(g)
                m_hat = mn / (1.0 - ADAM_B1 ** ADAM_STEP)
                v_hat = vn / (1.0 - ADAM_B2 ** ADAM_STEP)
                go_ref[0] = g
                d_ref[0] = -ADAM_LR * (m_hat / (jnp.sqrt(v_hat) + ADAM_EPS) + ADAM_WD * w_ref[0])
                mo_ref[0] = mn
                vo_ref[0] = vn

    def g_spec(layer):
        return pl.BlockSpec((n, rows, c), lambda a, i: (0, jnp.where(a < layer, 0, jnp.where(a == layer, i, nb - 1)), 0))

    blk = pl.BlockSpec((1, rows, c), lambda a, i: (a, i, 0))
    out = jax.ShapeDtypeStruct((nl, r, c), F32)
    return pl.pallas_call(
        body, name=name, grid=(nl, nb),
        in_specs=[g_spec(layer) for layer in range(nl)] + [blk, blk, blk],
        out_specs=[blk, blk, blk, blk], out_shape=[out, out, out, out],
        compiler_params=_cparams(("arbitrary", "arbitrary")),
    )(*glist, w, m, v)


X_ROWS_PER_COL = 2 * (D_MODEL // LANES)


def _w_in_to_x(w):
    t = jnp.transpose(w, (2, 0, 1)).reshape(SHARD_IN, DEPTH, D_MODEL // LANES, LANES)
    return jnp.transpose(t, (0, 2, 1, 3)).reshape(SHARD_IN * X_ROWS_PER_COL, LANES)


def _w_in_from_x(xv):
    t = jnp.transpose(xv.reshape(SHARD_IN, D_MODEL // LANES, DEPTH, LANES), (0, 2, 1, 3))
    return jnp.transpose(t.reshape(SHARD_IN, DEPTH, D_MODEL), (1, 2, 0))


def _adamw_w_in(glist, w, m, v, name, after=None):
    n = glist[0].shape[0]
    cols = 2 * LANES
    rows = cols * X_ROWS_PER_COL
    extra = [] if after is None else [after]

    def body(g0_ref, g1_ref, w_ref, m_ref, v_ref, *rest):
        go_ref, d_ref, mo_ref, vo_ref = rest[len(extra):]
        for layer, g_ref in enumerate((g0_ref, g1_ref)):
            g = g_ref[0].astype(F32)
            for s in range(1, n):
                g = g + g_ref[s].astype(F32)
            gt = g.T
            for t in range(D_MODEL // LANES):
                sel = (pl.ds(2 * t + layer, cols, stride=X_ROWS_PER_COL), slice(None))
                gs = gt[:, t * LANES:(t + 1) * LANES]
                mn = ADAM_B1 * m_ref[sel] + (1.0 - ADAM_B1) * gs
                vn = ADAM_B2 * v_ref[sel] + (1.0 - ADAM_B2) * jnp.square(gs)
                m_hat = mn / (1.0 - ADAM_B1 ** ADAM_STEP)
                v_hat = vn / (1.0 - ADAM_B2 ** ADAM_STEP)
                go_ref[sel] = gs
                d_ref[sel] = -ADAM_LR * (m_hat / (jnp.sqrt(v_hat) + ADAM_EPS) + ADAM_WD * w_ref[sel])
                mo_ref[sel] = mn
                vo_ref[sel] = vn

    g_spec = pl.BlockSpec((n, D_MODEL, cols), lambda i: (0, 0, i))
    blk = pl.BlockSpec((rows, LANES), lambda i: (i, 0))
    out = jax.ShapeDtypeStruct((SHARD_IN * X_ROWS_PER_COL, LANES), F32)
    res = pl.pallas_call(
        body, name=name, grid=(-(-SHARD_IN // cols),),
        in_specs=[g_spec, g_spec, blk, blk, blk] + [pl.BlockSpec(memory_space=pl.ANY)] * len(extra),
        out_specs=[blk, blk, blk, blk], out_shape=[out, out, out, out],
        compiler_params=_cparams(("parallel",)),
    )(*glist, _w_in_to_x(w), _w_in_to_x(m), _w_in_to_x(v), *extra)
    return [_w_in_from_x(o) for o in res]


def _adamw_many(gs, ws, ms, vs, name):
    k = len(gs)

    def body(*refs):
        g_refs, w_refs, m_refs, v_refs = refs[:k], refs[k:2 * k], refs[2 * k:3 * k], refs[3 * k:4 * k]
        d_refs, mo_refs, vo_refs = refs[4 * k:5 * k], refs[5 * k:6 * k], refs[6 * k:7 * k]
        for i in range(k):
            g = g_refs[i][...]
            mn = ADAM_B1 * m_refs[i][...] + (1.0 - ADAM_B1) * g
            vn = ADAM_B2 * v_refs[i][...] + (1.0 - ADAM_B2) * jnp.square(g)
            m_hat = mn / (1.0 - ADAM_B1 ** ADAM_STEP)
            v_hat = vn / (1.0 - ADAM_B2 ** ADAM_STEP)
            d_refs[i][...] = -ADAM_LR * (m_hat / (jnp.sqrt(v_hat) + ADAM_EPS) + ADAM_WD * w_refs[i][...])
            mo_refs[i][...] = mn
            vo_refs[i][...] = vn

    whole = pl.BlockSpec(memory_space=pltpu.VMEM)
    shapes = [jax.ShapeDtypeStruct(w.shape, F32) for w in ws]
    outs = pl.pallas_call(
        body, name=name, in_specs=[whole] * (4 * k), out_specs=[whole] * (3 * k), out_shape=shapes * 3,
        compiler_params=_cparams(None),
    )(*gs, *ws, *ms, *vs)
    return outs[:k], outs[k:2 * k], outs[2 * k:]


MEMORY_ORDER = {'s5_b_re': (0, 1, 3, 2), 's5_b_im': (0, 1, 3, 2), 's5_d': (0, 2, 1), 'sc_conv_w': (1, 0, 2)}


def _memory_view(name, t):
    return jnp.transpose(t, MEMORY_ORDER[name]) if name in MEMORY_ORDER else t


def _slot_sum(gslots, name):
    n, r, c = gslots.shape

    def body(g_ref, o_ref):
        g = g_ref[0]
        for s in range(1, n):
            g = g + g_ref[s]
        o_ref[...] = g

    return pl.pallas_call(
        body, name=name, in_specs=[pl.BlockSpec((n, r, c), lambda: (0, 0, 0))],
        out_specs=pl.BlockSpec((r, c), lambda: (0, 0)), out_shape=jax.ShapeDtypeStruct((r, c), F32),
        compiler_params=_cparams(None),
    )(gslots)


def _me_and_peers():
    x, y, c = lax.axis_index("x"), lax.axis_index("y"), lax.axis_index("c")
    me = 4 * x + 2 * y + c
    peers = []
    for k in range(1, N_DEV):
        px = 1 - x if (k >> 2) & 1 else x
        py = 1 - y if (k >> 1) & 1 else y
        pc = 1 - c if k & 1 else c
        peers.append((4 * px + 2 * py + pc, (px, py, pc)))
    return me, peers


_HBM = pl.BlockSpec(memory_space=pltpu.HBM)
_SEM = pl.BlockSpec(memory_space=pltpu.SEMAPHORE)
_EFFECT = pltpu.SideEffectType.DATAFLOW_SIDE_EFFECTING


N_CHIP = N_DEV // 2


def _chip_peers():
    x, y, c = lax.axis_index("x"), lax.axis_index("y"), lax.axis_index("c")
    chips = []
    for d in range(1, N_CHIP):
        px = 1 - x if (d >> 1) & 1 else x
        py = 1 - y if d & 1 else y
        chips.append((2 * px + py, (px, py)))
    return (x, y, c), 2 * x + y, chips


def _plan_direct(ins, lands, send_sems, recv_sems, local_sems, gather):
    me, peers = _me_and_peers()
    plan = dict(start=[], local=[], sends=[], recvs=[])
    for t in range(len(ins)):
        own = pltpu.make_async_copy(ins[t] if gather else ins[t].at[me], lands[t].at[me], local_sems.at[t])
        plan['start'].append(own)
        plan['local'].append(own)
        for k, (pidx, pos) in enumerate(peers):
            cp = pltpu.make_async_remote_copy(
                src_ref=ins[t] if gather else ins[t].at[pidx], dst_ref=lands[t].at[me],
                send_sem=send_sems.at[t * (N_DEV - 1) + k], recv_sem=recv_sems.at[t * (N_DEV - 1) + k],
                device_id=pos, device_id_type=MESH)
            plan['start'].append(cp)
            plan['sends'].append(cp)
            plan['recvs'].append(cp)
    return plan


def _plan_gather(ins, lands, send_sems, recv_sems, local_sems, first=0):
    (x, y, c), q, chips = _chip_peers()
    me = 2 * q + c
    plan = dict(start=[], relay_wait=[], relay_start=[], local=[], sends=[], recvs=[])
    for t in range(len(ins)):
        base = (first + t) * 7
        sem = lambda k: dict(send_sem=send_sems.at[base + k], recv_sem=recv_sems.at[base + k], device_id_type=MESH)
        own = pltpu.make_async_copy(ins[t], lands[t].at[me], local_sems.at[first + t])
        to_sib = pltpu.make_async_remote_copy(src_ref=ins[t], dst_ref=lands[t].at[me], device_id=(x, y, 1 - c), **sem(0))
        plan['start'] += [own, to_sib]
        plan['local'].append(own)
        plan['sends'].append(to_sib)
        plan['recvs'].append(to_sib)
        for d, (pq, (px, py)) in enumerate(chips):
            to_chip = pltpu.make_async_remote_copy(src_ref=ins[t], dst_ref=lands[t].at[me], device_id=(px, py, c), **sem(1 + d))
            blk = lands[t].at[2 * pq + c]
            fwd = pltpu.make_async_remote_copy(src_ref=blk, dst_ref=blk, device_id=(x, y, 1 - c), **sem(4 + d))
            plan['start'].append(to_chip)
            plan['relay_wait'].append(to_chip)
            plan['relay_start'].append(fwd)
            plan['sends'] += [to_chip, fwd]
            plan['recvs'].append(fwd)
    return plan


def _plan_pair(ins, lands, send_sems, recv_sems, local_sems):
    (x, y, c), q, chips = _chip_peers()
    plan = dict(start=[], local=[], sends=[], recvs=[])
    for t in range(len(ins)):
        for k in range(N_CHIP):
            cp = pltpu.make_async_remote_copy(
                src_ref=ins[t].at[2 * k + 1 - c], dst_ref=lands[t].at[k], send_sem=send_sems.at[t * N_CHIP + k],
                recv_sem=recv_sems.at[t * N_CHIP + k], device_id=(x, y, 1 - c), device_id_type=MESH)
            plan['start'].append(cp)
            plan['sends'].append(cp)
            plan['recvs'].append(cp)
    return plan


def _plan_chips(ins, lands, send_sems, recv_sems, local_sems):
    (x, y, c), q, chips = _chip_peers()
    plan = dict(start=[], local=[], sends=[], recvs=[])
    for t in range(len(ins)):
        own = pltpu.make_async_copy(ins[t].at[q], lands[t].at[q], local_sems.at[t])
        plan['start'].append(own)
        plan['local'].append(own)
        for d, (pq, (px, py)) in enumerate(chips):
            cp = pltpu.make_async_remote_copy(
                src_ref=ins[t].at[pq], dst_ref=lands[t].at[q], send_sem=send_sems.at[t * 3 + d],
                recv_sem=recv_sems.at[t * 3 + d], device_id=(px, py, c), device_id_type=MESH)
            plan['start'].append(cp)
            plan['sends'].append(cp)
            plan['recvs'].append(cp)
    return plan


def _split_start(plan_fn, tensors, land_shapes, n_sems, name, after=None):
    n = len(tensors)
    extra = [] if after is None else [after]

    def body(*refs):
        ins, lands = refs[:n], refs[n:2 * n]
        plan = plan_fn(ins, lands, *refs[2 * n + len(extra):2 * n + len(extra) + 3])
        for cp in plan['start']:
            cp.start()
        refs[-1][...] = jnp.zeros_like(refs[-1])

    outs = pl.pallas_call(
        body, name=name,
        out_shape=(pltpu.SemaphoreType.DMA((n_sems,)), pltpu.SemaphoreType.DMA((n_sems,)), pltpu.SemaphoreType.DMA((n,)),
                   *[pltpu.HBM(t.shape, t.dtype) for t in tensors],
                   *[pltpu.HBM(s, t.dtype) for s, t in zip(land_shapes, tensors)],
                   jax.ShapeDtypeStruct((8, LANES), F32)),
        in_specs=[_HBM] * (2 * n) + [pl.BlockSpec(memory_space=pl.ANY)] * len(extra),
        out_specs=(_SEM, _SEM, _SEM, *[_HBM] * (2 * n), pl.BlockSpec(memory_space=pltpu.VMEM)),
        input_output_aliases={t: 3 + t for t in range(2 * n)},
        compiler_params=pltpu.CompilerParams(has_side_effects=_EFFECT),
    )(*[pltpu.with_memory_space_constraint(t, pltpu.HBM) for t in tensors],
      *[pltpu.with_memory_space_constraint(lax.empty(s, t.dtype), pltpu.HBM) for s, t in zip(land_shapes, tensors)], *extra)
    return outs[:-1], outs[-1]


def _split_relay(plan_fn, state, after, name):
    sems, thru = state[:3], state[3:]
    n = len(thru) // 2

    def arrived(*refs):
        plan = plan_fn(refs[:n], refs[n:2 * n], *refs[2 * n:2 * n + 3])
        for cp in plan['relay_wait']:
            cp.wait_recv()

    thru = pl.pallas_call(
        arrived, name=name + "_arrived",
        out_shape=tuple(pltpu.HBM(t.shape, t.dtype) for t in thru),
        in_specs=[_HBM] * (2 * n) + [_SEM, _SEM, _SEM, pl.BlockSpec(memory_space=pl.ANY)],
        out_specs=tuple([_HBM] * (2 * n)),
        input_output_aliases={t: t for t in range(2 * n)},
        compiler_params=pltpu.CompilerParams(has_side_effects=_EFFECT),
    )(*thru, *sems, after)

    def forward(*refs):
        plan = plan_fn(refs[:n], refs[n:2 * n], *refs[2 * n:2 * n + 3])
        for cp in plan['relay_start']:
            cp.start()
        refs[-1][...] = jnp.zeros_like(refs[-1])

    outs = pl.pallas_call(
        forward, name=name + "_forward",
        out_shape=(*[pltpu.HBM(t.shape, t.dtype) for t in thru], jax.ShapeDtypeStruct((8, LANES), F32)),
        in_specs=[_HBM] * (2 * n) + [_SEM, _SEM, _SEM],
        out_specs=(*[_HBM] * (2 * n), pl.BlockSpec(memory_space=pltpu.VMEM)),
        input_output_aliases={t: t for t in range(2 * n)},
        compiler_params=pltpu.CompilerParams(has_side_effects=_EFFECT),
    )(*thru, *sems)
    return (*sems, *outs[:-1]), outs[-1]


def _split_wait(plan_fn, state, after, name, with_sources=False):
    sems, thru = state[:3], state[3:]
    n = len(thru) // 2

    def body(*refs):
        plan = plan_fn(refs[:n], refs[n:2 * n], *refs[2 * n:2 * n + 3])
        for cp in plan['local']:
            cp.wait()
        for cp in plan['sends']:
            cp.wait_send()
        for cp in plan['recvs']:
            cp.wait_recv()

    outs = pl.pallas_call(
        body, name=name,
        out_shape=tuple(pltpu.HBM(t.shape, t.dtype) for t in thru),
        in_specs=[_HBM] * (2 * n) + [_SEM, _SEM, _SEM, pl.BlockSpec(memory_space=pl.ANY)],
        out_specs=tuple([_HBM] * (2 * n)),
        input_output_aliases={t: t for t in range(2 * n)},
        compiler_params=pltpu.CompilerParams(has_side_effects=_EFFECT),
    )(*thru, *sems, after)
    return (list(outs[:n]), list(outs[n:])) if with_sources else list(outs[n:])


PAIR_SUM_BLOCK = 768 * 1024


def _pair_sum(mine, theirs, name):
    _, r, c = mine.shape
    rows = r
    while rows * c > PAIR_SUM_BLOCK and rows % 32 == 0:
        rows //= 2

    def body(core_ref, a_ref, b_ref, o_ref):
        o_ref[0] = (a_ref[0].astype(F32) + b_ref[0].astype(F32)).astype(o_ref.dtype)

    return pl.pallas_call(
        body, name=name,
        grid_spec=pltpu.PrefetchScalarGridSpec(
            num_scalar_prefetch=1, grid=(N_CHIP, r // rows),
            in_specs=[pl.BlockSpec((1, rows, c), lambda k, i, core: (2 * k + core[0], i, 0)),
                      pl.BlockSpec((1, rows, c), lambda k, i, core: (k, i, 0))],
            out_specs=pl.BlockSpec((1, rows, c), lambda k, i, core: (k, i, 0))),
        out_shape=jax.ShapeDtypeStruct((N_CHIP, r, c), mine.dtype),
        compiler_params=_cparams(("parallel", "parallel")),
    )(lax.axis_index("c").astype(jnp.int32).reshape(1), mine, theirs)


WEIGHTS = ['norm_w', 'w_in', 's5_lambda_re', 's5_lambda_im', 's5_b_re', 's5_b_im', 's5_c_re', 's5_c_im', 's5_d',
           's5_log_step', 's5_w_glu', 'sgu_ln_w', 'sgu_ln_b', 'sgu_w', 'sgu_b', 'm2_conv_w', 'm2_conv_b', 'm2_dt_bias',
           'm2_a_log', 'm2_d', 'm2_norm_w', 'sc_conv_w', 'merge_b', 'w_branch', 'w_out', 'final_norm_w']
BIG_SHARDED = ['w_in', 'w_branch', 'w_out', 's5_w_glu']
SMALL_SHARDED = ['m2_conv_w', 'sc_conv_w', 'merge_b']
REPLICATED = [n for n in WEIGHTS if n not in BIG_SHARDED + SMALL_SHARDED]
S5_NAMES = ['s5_lambda_re', 's5_lambda_im', 's5_b_re', 's5_b_im', 's5_c_re', 's5_c_im', 's5_d', 's5_log_step']


def _sc_interleave(t):
    lead = t.shape[:-1]
    return jnp.swapaxes(t.reshape(lead + (4, 4, LANES)), -3, -2).reshape(lead + (4 * BW,))


def _pad_in(w):
    z = lambda n: jnp.zeros(w.shape[:-1] + (n,), w.dtype)
    return jnp.concatenate([w[..., 6152:], w[..., 0:1024], w[..., 3072:4096], w[..., 1024:2560], z(512),
                            w[..., 2560:3072], w[..., 4096:4104], z(504), _sc_interleave(w[..., 4104:6152])], axis=-1)


def _unpad_in(g):
    return jnp.concatenate([g[..., C_S5U:C_S5U + 1024], g[..., C_SGU_U:C_SGU_U + 1536], g[..., C_M2Z:C_M2Z + 512],
                            g[..., C_M2X:C_M2X + 1024], g[..., C_DT:C_DT + 8], _sc_interleave(g[..., C_SC:]),
                            g[..., :N_BRANCH * D_MODEL]], axis=-1)


ROW_BLOCK = 8 * LANES


def _pack_rows(tensors, row_mult, batched=False):
    parts = []
    for t in tensors:
        f = t.reshape((t.shape[0], -1) if batched else (1, -1))
        f = jnp.pad(f, ((0, 0), (0, (-f.shape[1]) % ROW_BLOCK)))
        parts.append(f.reshape(f.shape[0], -1, LANES))
    out = jnp.concatenate(parts, axis=1)
    out = jnp.pad(out, ((0, 0), (0, (-out.shape[1]) % row_mult), (0, 0)))
    return out if batched else out[0]


def _unpack_rows(rows, shapes):
    out, r0 = [], 0
    for shp in shapes:
        size = 1
        for s in shp:
            size *= s
        nr = -(-size // ROW_BLOCK) * 8
        out.append(rows[r0:r0 + nr].reshape(-1)[:size].reshape(shp))
        r0 += nr
    return out


def _kernel_col_map():
    m = np.full(IN_PAD, -1, np.int64)
    m[C_MERGE:C_MERGE + 4096] = np.arange(6152, 10248)
    m[C_S5U:C_S5U + 1024] = np.arange(0, 1024)
    m[C_M2X:C_M2X + 1024] = np.arange(3072, 4096)
    m[C_SGU_U:C_SGU_U + 1536] = np.arange(1024, 2560)
    m[C_M2Z:C_M2Z + 512] = np.arange(2560, 3072)
    m[C_DT:C_DT + 8] = np.arange(4096, 4104)
    for j in range(4):
        for kind in range(4):
            k0 = C_SC + 4 * LANES * j + LANES * kind
            m[k0:k0 + LANES] = 4104 + BW * kind + LANES * j + np.arange(LANES)
    return m


def _lane_pieces(sources):
    pieces, cur = [], None
    for lane, src in enumerate(sources):
        key = None if src is None else (src[0], src[1] // LANES, (lane - src[1]) % LANES)
        if cur is not None and key == cur[0]:
            cur[2] = lane + 1
        else:
            if cur is not None and cur[0] is not None:
                pieces.append((*cur[0], cur[1], cur[2]))
            cur = [key, lane, lane + 1]
    if cur is not None and cur[0] is not None:
        pieces.append((*cur[0], cur[1], cur[2]))
    return pieces


def _assemble_block(pieces, load, rows, dtype):
    lane = lax.broadcasted_iota(jnp.int32, (rows, LANES), 1)
    out = None
    for arr, sb, shift, lo, hi in pieces:
        v = load(arr, sb)
        if shift:
            v = pltpu.roll(v, shift, 1)
        if out is None and lo == 0 and hi == LANES:
            out = v
        else:
            out = jnp.where((lane >= lo) & (lane < hi), v, jnp.zeros((rows, LANES), dtype) if out is None else out)
    return jnp.zeros((rows, LANES), dtype) if out is None else out


RELAYOUT_ROWS = 512
SHARD_BLOCKS = -(-SHARD_IN // LANES)


def _load_shard_block(ref, rows):
    def load(j, sb):
        if sb == SHARD_BLOCKS - 1:
            return jnp.broadcast_to(ref[j, :, SHARD_IN - 1:SHARD_IN], (rows, LANES))
        return ref[j, :, sb * LANES:(sb + 1) * LANES]
    return load


def _relayout_w_in(gathered, name):
    kmap = _kernel_col_map()
    dtype = gathered.dtype

    def body(src_ref, o_ref):
        load = _load_shard_block(src_ref, RELAYOUT_ROWS)
        for ob in range(IN_PAD // LANES):
            srcs = [None if kmap[ob * LANES + l] < 0 else (int(kmap[ob * LANES + l]) // SHARD_IN, int(kmap[ob * LANES + l]) % SHARD_IN)
                    for l in range(LANES)]
            o_ref[:, ob * LANES:(ob + 1) * LANES] = _assemble_block(_lane_pieces(srcs), load, RELAYOUT_ROWS, dtype)

    return pl.pallas_call(
        body, name=name, grid=(D_MODEL // RELAYOUT_ROWS,),
        in_specs=[pl.BlockSpec((N_DEV, RELAYOUT_ROWS, SHARD_IN), lambda i: (0, i, 0))],
        out_specs=pl.BlockSpec((RELAYOUT_ROWS, IN_PAD), lambda i: (i, 0)),
        out_shape=jax.ShapeDtypeStruct((D_MODEL, IN_PAD), dtype),
        compiler_params=_cparams(("parallel",)),
    )(gathered)


def _relayout_g_in(gw, name):
    kmap = _kernel_col_map()
    kinv = np.zeros(IN_DIM, np.int64)
    kinv[kmap[kmap >= 0]] = np.nonzero(kmap >= 0)[0]
    dtype = gw.dtype

    def body(src_ref, o_ref):
        load = lambda _, sb: src_ref[:, sb * LANES:(sb + 1) * LANES]
        for j in range(N_DEV):
            for ob in range(SHARD_BLOCKS):
                srcs = [(0, int(kinv[SHARD_IN * j + ob * LANES + l])) if ob * LANES + l < SHARD_IN else None for l in range(LANES)]
                blk = _assemble_block(_lane_pieces(srcs), load, RELAYOUT_ROWS, dtype)
                if ob == SHARD_BLOCKS - 1:
                    o_ref[j, :, SHARD_IN - 1:SHARD_IN] = blk[:, 0:1]
                else:
                    o_ref[j, :, ob * LANES:(ob + 1) * LANES] = blk

    return pl.pallas_call(
        body, name=name, grid=(D_MODEL // RELAYOUT_ROWS,),
        in_specs=[pl.BlockSpec((RELAYOUT_ROWS, IN_PAD), lambda i: (i, 0))],
        out_specs=pl.BlockSpec((N_DEV, RELAYOUT_ROWS, SHARD_IN), lambda i: (0, i, 0)),
        out_shape=jax.ShapeDtypeStruct((N_DEV, D_MODEL, SHARD_IN), dtype),
        compiler_params=_cparams(("parallel",)),
    )(gw)


def _rows128(flat, row_mult=8):
    n = flat.shape[0]
    per = LANES * row_mult
    total = -(-n // per) * per
    return jnp.pad(flat, (0, total - n)).reshape(total // LANES, LANES)


def _pad_lanes(v):
    return jnp.pad(v, (0, LANES - v.shape[0])).reshape(1, LANES)


def _layer_prep(i, p):
    disc, disc_vjp = jax.vjp(_s5_disc, *[p[n][i] for n in S5_NAMES])
    prep = dict(
        nw=p['norm_w'][i].reshape(1, D_MODEL), disc_vjp=disc_vjp,
        s5small=[_block_diag(t).astype(BF16) for t in disc[:4]] + [disc[4], disc[5]],
        sgw=[p['sgu_ln_w'][i].reshape(1, BW), p['sgu_ln_b'][i].reshape(1, BW), p['sgu_w'][i],
             jnp.repeat(p['sgu_b'][i].T, BW // SGU_HEADS, axis=1)],
        cb=p['m2_conv_b'][i].reshape(1, M2_CONV_CH),
        m2w=[_pad_lanes(p['m2_dt_bias'][i]), _pad_lanes(p['m2_a_log'][i]),
             jnp.repeat(p['m2_d'][i], M2_HEAD_DIM).reshape(1, BW), p['m2_norm_w'][i].reshape(1, BW)])
    touch = [t[0, 0].astype(F32) for t in prep['s5small']] + [prep['sgw'][3][0, 0], prep['m2w'][2][0, 0]]
    return prep, sum(touch[1:], touch[0])


def _layer_fwd(x, h, i, prep, w_in, other_weights, before_merge=None):
    proj = _matmul(h, w_in, 1, 0, F32, 2048, 1024, 1024, f"proj{i}")
    full = dict(other_weights(proj), w_in=w_in)
    s5w = prep['s5small'] + [full['s5_w_glu']]
    ys, sre, sim = _s5_fwd(proj, *s5w, f"s5_fwd{i}")
    ys = _sgu_fwd(proj, ys, *prep['sgw'], f"sgu_fwd{i}")
    cw = full['m2_conv_w']
    xa = _m2_conv_fwd(proj, cw, prep['cb'], f"m2conv_fwd{i}")
    ys, s_in = _ssd_fwd(proj, ys, xa, *prep['m2w'], f"ssd_fwd{i}")
    scw = full['sc_conv_w']
    ys = _sc_fwd(proj, ys, scw, f"sc_fwd{i}")
    mb = full['merge_b'].reshape(N_BRANCH, 1, D_MODEL)
    if before_merge is not None:
        mb = mb + before_merge(ys)[0, 0]
    merged = _merge_fwd(proj, ys, mb, full['w_branch'], f"merge_fwd{i}")
    x_new = _matmul(merged, full['w_out'], 1, 0, F32, 1024, 1024, 1024, f"out{i}", residual=x)
    saved = dict(x=x, nw=prep['nw'], h=h, proj=proj, disc_vjp=prep['disc_vjp'], s5w=s5w, sre=sre, sim=sim, sgw=prep['sgw'],
                 cw=cw, cb=prep['cb'], xa=xa, m2w=prep['m2w'], s_in=s_in, scw=scw, ys=ys, mb=mb, merged=merged)
    return x_new, saved, full


def _layer_bwd(dx_out, i, sv, full, on_large_grads=None, after_dh=None):
    g = {}
    proj = sv['proj']
    dm = _matmul(dx_out, full['w_out'], 1, 1, F32, 1024, 1024, 1024, f"dmerged{i}")
    g['w_out'] = _matmul(sv['merged'], dx_out, 0, 0, BF16, 1024, 1024, 1024, f"gw_out{i}")
    dys, dproj, g['w_branch'], dmb = _merge_bwd(proj, sv['ys'], dm, sv['mb'], full['w_branch'], f"merge_bwd{i}")
    g['merge_b'] = dmb.reshape(N_BRANCH, D_MODEL)
    dproj, dbbre, dbbim, dcre, dcim, da, dd, dwg = _s5_bwd(proj, dproj, dys, sv['sre'], sv['sim'], *sv['s5w'], f"s5_bwd{i}")
    g['s5_dense'] = (dbbre, dbbim, dcre, dcim, da, dd)
    g['s5_w_glu'] = dwg.astype(BF16)
    dproj, dlw, dlb, g['sgu_w'], dbias = _sgu_bwd(proj, dproj, dys, *sv['sgw'], f"sgu_bwd{i}")
    g['sgu_ln_w'], g['sgu_ln_b'] = dlw[0], dlb[0]
    g['sgu_b'] = dbias.reshape(SGU_CHUNK, SGU_HEADS, BW // SGU_HEADS).sum(-1).T
    dproj, dxa, ddtb, dal, ddf, dnw = _ssd_bwd(proj, dproj, sv['xa'], dys, sv['s_in'], *sv['m2w'], f"ssd_bwd{i}")
    dproj, g['m2_conv_w'], dcb = _m2_conv_bwd(proj, dproj, dxa, sv['cw'], sv['cb'], f"m2conv_bwd{i}")
    g['m2_conv_b'], g['m2_norm_w'] = dcb[0], dnw[0]
    g['m2_dt_bias'], g['m2_a_log'] = ddtb[0, :M2_HEADS], dal[0, :M2_HEADS]
    g['m2_d'] = ddf.reshape(M2_HEADS, M2_HEAD_DIM).sum(-1)
    dproj, g['sc_conv_w'] = _sc_bwd(proj, dproj, dys, sv['scw'], f"sc_bwd{i}")
    g['w_in'] = _matmul(sv['h'], dproj, 0, 0, BF16, 1024, 1024, 2048, f"gw_in{i}")
    tok = on_large_grads(g) if on_large_grads else None
    dh = _matmul(dproj, full['w_in'], 1, 1, F32, 2048, 1024, 1024, f"dh{i}", after=tok)
    nw = sv['nw'] if after_dh is None else sv['nw'] + after_dh(dh)[0, 0]
    dx_in, dnw_l = _rmsnorm_bwd(sv['x'], nw, dh, dx_out, f"rms_bwd{i}")
    g['norm_w'] = dnw_l[0]
    return dx_in, g


def _split8(t, axis):
    shp = t.shape
    t = t.reshape(shp[:axis] + (N_DEV, shp[axis] // N_DEV) + shp[axis + 1:])
    return jnp.moveaxis(t, axis, 0)


def _join8(t, axis):
    t = jnp.moveaxis(t, 0, axis)
    shp = t.shape
    return t.reshape(shp[:axis] + (shp[axis] * shp[axis + 1],) + shp[axis + 2:])


SHARD_AXIS = {'w_in': 2, 'w_branch': 3, 'w_out': 1, 's5_w_glu': 1, 'm2_conv_w': 2, 'sc_conv_w': 2, 'merge_b': 2}


OTHER_BIG = [n for n in BIG_SHARDED if n != 'w_in']


def _other_weights(gathered):
    return {n: _join8(t, SHARD_AXIS[n] - 1) for n, t in zip(OTHER_BIG, gathered)}


def _layer_grad_blocks(g, i):
    blocks = [_relayout_g_in(g[n], f"relayout_g_in{i}") if n == 'w_in' else _split8(g[n], SHARD_AXIS[n] - 1) for n in BIG_SHARDED]
    return [b.reshape(N_DEV, -1, b.shape[-1]) for b in blocks]


def _pair_start(blocks, tag):
    shapes = [(N_CHIP,) + b.shape[1:] for b in blocks]
    return _split_start(_plan_pair, blocks, shapes, N_CHIP * len(blocks), f"pair{tag}_start")


def _pair_sums(state, after, tag):
    mine, theirs = _split_wait(_plan_pair, state, after, f"pair{tag}_wait", with_sources=True)
    return [_pair_sum(b, t, f"pair_sum{tag}_{k}") for k, (b, t) in enumerate(zip(mine, theirs))]


def _chips_start(sums, tag, after=None):
    return _split_start(_plan_chips, sums, [s.shape for s in sums], 3 * len(sums), f"chips{tag}_start", after)


def kernel(x, norm_w, w_in, s5_lambda_re, s5_lambda_im, s5_b_re, s5_b_im, s5_c_re, s5_c_im, s5_d, s5_log_step, s5_w_glu, sgu_ln_w, sgu_ln_b, sgu_w, sgu_b, m2_conv_w, m2_conv_b, m2_dt_bias, m2_a_log, m2_d, m2_norm_w, sc_conv_w, merge_b, w_branch, w_out, final_norm_w, loss_target, m_norm_w, m_w_in, m_s5_lambda_re, m_s5_lambda_im, m_s5_b_re, m_s5_b_im, m_s5_c_re, m_s5_c_im, m_s5_d, m_s5_log_step, m_s5_w_glu, m_sgu_ln_w, m_sgu_ln_b, m_sgu_w, m_sgu_b, m_m2_conv_w, m_m2_conv_b, m_m2_dt_bias, m_m2_a_log, m_m2_d, m_m2_norm_w, m_sc_conv_w, m_merge_b, m_w_branch, m_w_out, m_final_norm_w, v_norm_w, v_w_in, v_s5_lambda_re, v_s5_lambda_im, v_s5_b_re, v_s5_b_im, v_s5_c_re, v_s5_c_im, v_s5_d, v_s5_log_step, v_s5_w_glu, v_sgu_ln_w, v_sgu_ln_b, v_sgu_w, v_sgu_b, v_m2_conv_w, v_m2_conv_b, v_m2_dt_bias, v_m2_a_log, v_m2_d, v_m2_norm_w, v_sc_conv_w, v_merge_b, v_w_branch, v_w_out, v_final_norm_w):
    loc = locals()
    p = {n: loc[n] for n in WEIGHTS}
    mom = {n: loc['m_' + n] for n in WEIGHTS}
    vel = {n: loc['v_' + n] for n in WEIGHTS}

    small_sizes = [p[n].size for n in SMALL_SHARDED]
    small_pack = _rows128(jnp.concatenate([p[n].reshape(-1) for n in SMALL_SHARDED]))
    first = [p['w_in'][0].astype(BF16)]
    gath_first, tok = _split_start(_plan_gather, first, [(N_DEV,) + first[0].shape], 7, "gather_w_in0_start")
    shards = ([(p[n][0] + tok[0, 0]).astype(BF16) for n in OTHER_BIG] + [small_pack + tok[0, 0]]
              + [(p[n][1] + tok[0, 0]).astype(BF16) for n in BIG_SHARDED])
    gath, tok = _split_start(_plan_gather, shards, [(N_DEV,) + t.shape for t in shards], 7 * len(shards), "gather_start")

    def relayed(lo, hi, after, name, started=None):
        started = gath if started is None else started
        n = (len(started) - 3) // 2
        sems, srcs, lands = started[:3], started[3:3 + n], started[3 + n:]
        plan = functools.partial(_plan_gather, first=lo)
        state, tok = _split_relay(plan, (*sems, *srcs[lo:hi], *lands[lo:hi]), after, name + "_relay")
        return (plan, state, name), tok

    def arrived(relay, after):
        plan, state, name = relay
        return _split_wait(plan, state, after, name + "_wait")

    def gathered(lo, hi, after, name, started=None):
        relay, tok = relayed(lo, hi, after, name, started)
        return arrived(relay, tok)

    later = dict(p, **{n: p[n] + tok[0, 0] for n in ('norm_w', 's5_log_step', 'sgu_b', 'm2_d')})
    preps = [_layer_prep(i, later) for i in range(DEPTH)]
    h0 = _rmsnorm_fwd(x[0], preps[0][0]['nw'], "rms_fwd0")
    got = gathered(0, 1, tok + (preps[0][1] + preps[1][1] + h0[0, 0].astype(F32)), "gather_w_in0", gath_first)
    small_full = {}

    def other_weights0(proj):
        got = gathered(0, 4, proj, "gather_rest0")
        small_all, off = got[-1].reshape(N_DEV, -1), 0
        for n, sz in zip(SMALL_SHARDED, small_sizes):
            small_full[n] = _join8(small_all[:, off:off + sz].reshape((N_DEV,) + p[n].shape), SHARD_AXIS[n])
            off += sz
        return dict(_other_weights(got[:-1]), **{n: small_full[n][0] for n in SMALL_SHARDED})

    saved, layer_g, full = [None] * DEPTH, [None] * DEPTH, [None] * DEPTH
    relay1 = []

    def relay_layer1(ys):
        relay, tok = relayed(4, 8, ys, "gather1")
        relay1.append(relay)
        return tok

    xs, saved[0], full[0] = _layer_fwd(x[0], h0, 0, preps[0][0], _relayout_w_in(got[0], "relayout_w_in0"), other_weights0,
                                       relay_layer1)
    h1 = _rmsnorm_fwd(xs, preps[1][0]['nw'], "rms_fwd1")
    got = arrived(relay1[0], h1)
    xs, saved[1], full[1] = _layer_fwd(
        xs, h1, 1, preps[1][0], _relayout_w_in(got[0], "relayout_w_in1"),
        lambda proj: dict(_other_weights(got[1:]), **{n: small_full[n][1] for n in SMALL_SHARDED}))
    loss_row, dx, dfw = _loss_head(xs, final_norm_w.reshape(1, D_MODEL), loss_target[0])
    loss = lax.psum(loss_row[0, 0], ("x", "y", "c"))
    loss, dx = lax.optimization_barrier((loss, dx))
    pairs, scat, sent0 = [None] * DEPTH, [None] * DEPTH, []

    def start_pairs1(g):
        pairs[1], tok = _pair_start(_layer_grad_blocks(g, 1), 1)
        return tok

    def send_chip_sums1(dh):
        scat[1], tok = _chips_start(_pair_sums(pairs[1], dh, 1), 1)
        return tok

    def send_all0(g):
        pairs[0], tok = _pair_start(_layer_grad_blocks(g, 0), 0)
        scat[0], tok = _chips_start(_pair_sums(pairs[0], tok, 0), 0)
        sent0.append(tok)
        return tok

    dx, layer_g[1] = _layer_bwd(dx, 1, saved[1], full[1], on_large_grads=start_pairs1, after_dh=send_chip_sums1)
    dx, layer_g[0] = _layer_bwd(dx, 0, saved[0], full[0], on_large_grads=send_all0)
    for i in range(DEPTH):
        dense = layer_g[i].pop('s5_dense')
        blocks = tuple(_diag_blocks(t, after=sent0[0]) for t in dense[:4])
        layer_g[i].update(zip(S5_NAMES, saved[i]['disc_vjp'](blocks + (dense[4] + sent0[0][0, 0], dense[5]))))
    grads = {n: jnp.stack([layer_g[i][n] for i in range(DEPTH)]) for n in SMALL_SHARDED + REPLICATED if n != 'final_norm_w'}
    grads['final_norm_w'] = dfw[0]

    out_g, out_d, out_m, out_v = {}, {}, {}, {}
    repl_rows = _pack_rows([grads[n] for n in REPLICATED], 8 * N_DEV)
    rr = repl_rows.shape[0] // N_DEV
    shard_rows = _pack_rows([_split8(grads[n], SHARD_AXIS[n]) for n in SMALL_SHARDED], 8, batched=True)
    rs = shard_rows.shape[1]
    small_g = jnp.concatenate([shard_rows, repl_rows.reshape(N_DEV, rr, LANES)], axis=1)
    all_to_all, all_gather = functools.partial(_plan_direct, gather=False), functools.partial(_plan_direct, gather=True)
    small_state, tok = _split_start(all_to_all, [small_g], [small_g.shape], N_DEV - 1, "scatter_small_start")
    landed1 = _split_wait(_plan_chips, scat[1], tok, "chips1_wait")
    landed0 = _split_wait(_plan_chips, scat[0], landed1[0], "chips0_wait")

    def big_adamw(n, after=None):
        k, shp = BIG_SHARDED.index(n), p[n].shape
        if n == 'w_in':
            return _adamw_w_in([landed0[k], landed1[k]], p[n], mom[n], vel[n], "adamw_w_in", after)
        c = shp[-1]
        r = p[n].size // (DEPTH * c)
        res = _adamw([landed0[k], landed1[k]], *[d[n].reshape(DEPTH, r, c) for d in (p, mom, vel)],
                     {'w_branch': 512, 'w_out': 128, 's5_w_glu': 64}[n], "adamw_" + n)
        return [o.reshape(shp) for o in res]

    for n in OTHER_BIG:
        out_g[n], out_d[n], out_m[n], out_v[n] = big_adamw(n)
    updated = sum(out_d[n].reshape(-1)[0] for n in OTHER_BIG).reshape(1, 1)
    small_sum = _slot_sum(_split_wait(all_to_all, small_state, updated, "scatter_small_wait")[0], "sum_small")
    repl_part = small_sum[rs:]
    repl_state, tok = _split_start(all_gather, [repl_part], [(N_DEV,) + repl_part.shape], N_DEV - 1, "gather_small_start")
    out_g['w_in'], out_d['w_in'], out_m['w_in'], out_v['w_in'] = big_adamw('w_in', tok)
    repl_all = _split_wait(all_gather, repl_state, out_d['w_in'], "gather_small_wait")[0].reshape(N_DEV * rr, LANES)
    g_all = jnp.concatenate([small_sum[:rs], repl_all], axis=0)
    names = SMALL_SHARDED + REPLICATED
    pieces = (_unpack_rows(g_all[:rs], [p[n].shape for n in SMALL_SHARDED])
              + _unpack_rows(g_all[rs:], [p[n].shape for n in REPLICATED]))
    out_g.update(zip(names, pieces))
    res = _adamw_many(*[[_memory_view(n, d[n]) for n in names] for d in (out_g, p, mom, vel)], "adamw_small")
    for r, dst in zip(res, (out_d, out_m, out_v)):
        dst.update({n: _memory_view(n, t) for n, t in zip(names, r)})
    return (loss, dx[None], *[out_g[n] for n in WEIGHTS], *[out_d[n] for n in WEIGHTS],
            *[out_m[n] for n in WEIGHTS], *[out_v[n] for n in WEIGHTS])
```

```python
import functools

import jax
import jax.numpy as jnp
import numpy as np
from jax import lax
from jax.experimental import pallas as pl
from jax.experimental.pallas import tpu as pltpu

F32 = jnp.float32
BF16 = jnp.bfloat16

N_DEV = 8
SEQ = 2048
D_MODEL = 1024
DEPTH = 2
BW = 512
N_BRANCH = 4
EPS = 1e-6
S5_GROUPS, S5_STATE, S5_P = 32, 64, 16
S5_CH = S5_GROUPS * S5_STATE
SGU_CHUNK, SGU_HEADS = 128, 8
M2_HEADS, M2_HEAD_DIM, M2_STATE, M2_CHUNK, M2_CONV = 8, 64, 128, 128, 4
M2_CONV_CH = 1024
SC_CONV = 3
IN_DIM = 10248
IN_PAD = 11264
C_MERGE = 0
C_S5U, C_S5G = 4096, 4608
C_M2X = 5120
C_SGU_U, C_SGU_V, C_SGU_G = 6144, 6656, 7168
C_M2Z, C_DT = 8192, 8704
C_SC = 9216
SHARD_IN = IN_DIM // N_DEV

ADAM_LR, ADAM_B1, ADAM_B2, ADAM_EPS, ADAM_WD, ADAM_STEP = 0.001, 0.9, 0.999, 1e-08, 0.01, 10

VMEM_LIMIT = 56 * 1024 * 1024
LANES = 128

MESH = pl.DeviceIdType.MESH


def _cparams(sem=None, **kw):
    return pltpu.CompilerParams(dimension_semantics=sem, vmem_limit_bytes=VMEM_LIMIT, **kw)


def _dg(a, b, ca, cb, precision=None):
    return lax.dot_general(a, b, (((ca,), (cb,)), ((), ())), precision=precision,
                           preferred_element_type=F32)


@functools.partial(jax.custom_vjp, nondiff_argnums=(2, 3))
def _bdot(a, b, ca, cb):
    return _dg(a.astype(BF16), b.astype(BF16), ca, cb)


def _bdot_fwd(a, b, ca, cb):
    return _bdot(a, b, ca, cb), (a, b)


def _bdot_bwd(ca, cb, res, g):
    a, b = res
    gb, ab, bb = g.astype(BF16), a.astype(BF16), b.astype(BF16)
    da = _dg(gb, bb, 1, 1 - cb) if ca == 1 else _dg(bb, gb, 1 - cb, 1)
    db = _dg(ab, gb, 1 - ca, 0) if cb == 0 else _dg(gb, ab, 0, 1 - ca)
    return da.astype(a.dtype), db.astype(b.dtype)


_bdot.defvjp(_bdot_fwd, _bdot_bwd)


def _rms(x, w):
    return x * lax.rsqrt(jnp.mean(x * x, axis=-1, keepdims=True) + EPS) * w


def _silu(x):
    return x * jax.nn.sigmoid(x)


def _gelu(x):
    return 0.5 * x * (1.0 + jnp.tanh(0.7978845608028654 * (x + 0.044715 * (x * x * x))))


def _softplus(x):
    return jnp.maximum(x, 0.0) + jnp.log1p(jnp.exp(-jnp.abs(x)))


def _shift_down(x, s):
    if s == 0:
        return x
    row = lax.broadcasted_iota(jnp.int32, x.shape, 0)
    return jnp.where(row >= s, pltpu.roll(x, s, 0), 0.0)


def _shift_up(x, s):
    if s == 0:
        return x
    n = x.shape[0]
    row = lax.broadcasted_iota(jnp.int32, x.shape, 0)
    return jnp.where(row < n - s, pltpu.roll(x, n - s, 0), 0.0)


def _matmul(a, b, ca, cb, out_dtype, tm, tn, tk, name, residual=None, after=None):
    m = a.shape[1 - ca]
    k = a.shape[ca]
    n = b.shape[1 - cb]
    assert b.shape[cb] == k and m % tm == 0 and n % tn == 0 and k % tk == 0
    nk = k // tk
    a_spec = pl.BlockSpec((tm, tk), lambda i, j, kk: (i, kk)) if ca == 1 else pl.BlockSpec((tk, tm), lambda i, j, kk: (kk, i))
    b_spec = pl.BlockSpec((tk, tn), lambda i, j, kk: (kk, j)) if cb == 0 else pl.BlockSpec((tn, tk), lambda i, j, kk: (j, kk))
    o_spec = pl.BlockSpec((tm, tn), lambda i, j, kk: (i, j))
    has_res = residual is not None

    def body(*refs):
        refs = refs[:2 + has_res] + refs[2 + has_res + (after is not None):]
        if has_res:
            a_ref, b_ref, r_ref, o_ref, acc = refs
        else:
            a_ref, b_ref, o_ref, acc = refs
        kk = pl.program_id(2)
        part = _dg(a_ref[...].astype(BF16), b_ref[...].astype(BF16), ca, cb)
        if nk == 1:
            o_ref[...] = (part + r_ref[...] if has_res else part).astype(out_dtype)
            return

        @pl.when(kk == 0)
        def _():
            acc[...] = part

        @pl.when(kk > 0)
        def _():
            acc[...] += part

        @pl.when(kk == nk - 1)
        def _():
            r = acc[...]
            if has_res:
                r = r + r_ref[...]
            o_ref[...] = r.astype(out_dtype)

    ins = [a, b] + ([residual] if has_res else []) + ([after] if after is not None else [])
    specs = [a_spec, b_spec] + ([o_spec] if has_res else []) + ([pl.BlockSpec(memory_space=pl.ANY)] if after is not None else [])
    return pl.pallas_call(
        body, name=name, grid=(m // tm, n // tn, nk), in_specs=specs, out_specs=o_spec,
        out_shape=jax.ShapeDtypeStruct((m, n), out_dtype),
        scratch_shapes=[pltpu.VMEM((tm, tn) if nk > 1 else (8, LANES), F32)],
        compiler_params=_cparams(("parallel", "parallel", "arbitrary")),
    )(*ins)


ROW_TILE = 512


def _rmsnorm_fwd(x, w, name):
    def body(x_ref, w_ref, o_ref):
        o_ref[...] = _rms(x_ref[...], w_ref[...]).astype(BF16)

    return pl.pallas_call(
        body, name=name, grid=(SEQ // ROW_TILE,),
        in_specs=[pl.BlockSpec((ROW_TILE, D_MODEL), lambda i: (i, 0)), pl.BlockSpec((1, D_MODEL), lambda i: (0, 0))],
        out_specs=pl.BlockSpec((ROW_TILE, D_MODEL), lambda i: (i, 0)),
        out_shape=jax.ShapeDtypeStruct((SEQ, D_MODEL), BF16),
        compiler_params=_cparams(("parallel",)),
    )(x, w)


def _rmsnorm_bwd(x, w, dh, dres, name):
    def body(x_ref, w_ref, dh_ref, dres_ref, dx_ref, dw_ref):
        _, vjp = jax.vjp(_rms, x_ref[...], w_ref[...])
        dx, dw = vjp(dh_ref[...])
        dx_ref[...] = dx + dres_ref[...]

        @pl.when(pl.program_id(0) == 0)
        def _():
            dw_ref[...] = dw

        @pl.when(pl.program_id(0) > 0)
        def _():
            dw_ref[...] += dw

    tile = pl.BlockSpec((ROW_TILE, D_MODEL), lambda i: (i, 0))
    vec = pl.BlockSpec((1, D_MODEL), lambda i: (0, 0))
    return pl.pallas_call(
        body, name=name, grid=(SEQ // ROW_TILE,),
        in_specs=[tile, vec, tile, tile], out_specs=[tile, vec],
        out_shape=[jax.ShapeDtypeStruct((SEQ, D_MODEL), F32), jax.ShapeDtypeStruct((1, D_MODEL), F32)],
        compiler_params=_cparams(("arbitrary",)),
    )(x, w, dh, dres)


def _loss_head(x, w, target):
    def body(x_ref, w_ref, t_ref, loss_ref, dx_ref, dw_ref):
        tgt = t_ref[...]

        def f(xv, wv):
            err = _rms(xv, wv) - tgt
            return 0.5 * jnp.sum(jnp.mean(err * err, axis=-1))

        loss, vjp = jax.vjp(f, x_ref[...], w_ref[...])
        dx, dw = vjp(jnp.ones((), F32))
        dx_ref[...] = dx
        lrow = jnp.full((1, LANES), loss, F32)

        @pl.when(pl.program_id(0) == 0)
        def _():
            dw_ref[...] = dw
            loss_ref[...] = lrow

        @pl.when(pl.program_id(0) > 0)
        def _():
            dw_ref[...] += dw
            loss_ref[...] += lrow

    tile = pl.BlockSpec((ROW_TILE, D_MODEL), lambda i: (i, 0))
    vec = pl.BlockSpec((1, D_MODEL), lambda i: (0, 0))
    return pl.pallas_call(
        body, name="loss_head", grid=(SEQ // ROW_TILE,),
        in_specs=[tile, vec, tile], out_specs=[pl.BlockSpec((1, LANES), lambda i: (0, 0)), tile, vec],
        out_shape=[jax.ShapeDtypeStruct((1, LANES), F32), jax.ShapeDtypeStruct((SEQ, D_MODEL), F32),
                   jax.ShapeDtypeStruct((1, D_MODEL), F32)],
        compiler_params=_cparams(("arbitrary",)),
    )(x, w, target)


S5_T = 256
S5_BLOCKS = [(slice(j * 256, (j + 1) * 256), slice(j * 1024, (j + 1) * 1024)) for j in range(2)]


def _s5_post(ypre, gate, wglu):
    y = _gelu(ypre)
    y = y * jax.nn.sigmoid(_bdot(y, wglu, 1, 0))
    return y * _silu(gate)


def _s5_fwd(proj, bbre, bbim, cre, cim, a2, dvec, wglu, name):
    def body(u_ref, g_ref, bbre_ref, bbim_ref, cre_ref, cim_ref, a_ref, d_ref, wg_ref, o_ref, sre_ref, sim_ref, st):
        @pl.when(pl.program_id(0) == 0)
        def _():
            st[...] = jnp.zeros_like(st)

        u = u_ref[...]
        ub = u.astype(BF16)
        for us, ss in S5_BLOCKS:
            sre_ref[:, ss] = _dg(ub[:, us], bbre_ref[us, ss], 1, 0)
            sim_ref[:, ss] = _dg(ub[:, us], bbim_ref[us, ss], 1, 0)
        ar, ai = a_ref[0:1, :], a_ref[1:2, :]

        def step(t, carry):
            sr, si = carry
            nr = ar * sr - ai * si + sre_ref[pl.ds(t, 1), :]
            ni = ar * si + ai * sr + sim_ref[pl.ds(t, 1), :]
            sre_ref[pl.ds(t, 1), :] = nr
            sim_ref[pl.ds(t, 1), :] = ni
            return nr, ni

        sr, si = lax.fori_loop(0, S5_T, step, (st[0:1, :], st[1:2, :]), unroll=8)
        st[0:1, :] = sr
        st[1:2, :] = si
        ypre = jnp.concatenate(
            [_dg(sre_ref[:, ss].astype(BF16), cre_ref[ss, us], 1, 0) - _dg(sim_ref[:, ss].astype(BF16), cim_ref[ss, us], 1, 0)
             for us, ss in S5_BLOCKS], axis=1) + d_ref[...] * u
        o_ref[0] = _s5_post(ypre, g_ref[...], wg_ref[...]).astype(BF16)

    full = lambda shape: pl.BlockSpec(shape, lambda c: (0, 0))
    return pl.pallas_call(
        body, name=name, grid=(SEQ // S5_T,),
        in_specs=[pl.BlockSpec((S5_T, BW), lambda c: (c, C_S5U // BW)), pl.BlockSpec((S5_T, BW), lambda c: (c, C_S5G // BW)),
                  full((BW, S5_CH)), full((BW, S5_CH)), full((S5_CH, BW)), full((S5_CH, BW)),
                  full((2, S5_CH)), full((1, BW)), full((BW, BW))],
        out_specs=[pl.BlockSpec((1, S5_T, BW), lambda c: (0, c, 0)), pl.BlockSpec((S5_T, S5_CH), lambda c: (c, 0)),
                   pl.BlockSpec((S5_T, S5_CH), lambda c: (c, 0))],
        out_shape=[jax.ShapeDtypeStruct((N_BRANCH, SEQ, BW), BF16), jax.ShapeDtypeStruct((SEQ, S5_CH), F32),
                   jax.ShapeDtypeStruct((SEQ, S5_CH), F32)],
        scratch_shapes=[pltpu.VMEM((2, S5_CH), F32)],
        compiler_params=_cparams(("arbitrary",)),
    )(proj, proj, bbre, bbim, cre, cim, a2, dvec, wglu)


def _s5_bwd(proj, dproj, dout, sre, sim, bbre, bbim, cre, cim, a2, dvec, wglu, name):
    nc = SEQ // S5_T

    def body(u_ref, g_ref, do_ref, sre_ref, sim_ref, pre_ref, pim_ref, bbre_ref, bbim_ref, cre_ref, cim_ref, a_ref,
             d_ref, wg_ref, dproj_in, dp_ref, dbbre_ref, dbbim_ref, dcre_ref, dcim_ref, da_ref, dd_ref, dwg_ref,
             gre, gim, st):
        c = nc - 1 - pl.program_id(0)

        @pl.when(pl.program_id(0) == 0)
        def _():
            st[...] = jnp.zeros_like(st)
            for r in (dbbre_ref, dbbim_ref, dcre_ref, dcim_ref, da_ref, dd_ref, dwg_ref):
                r[...] = jnp.zeros_like(r)

        u = u_ref[...]
        s_re, s_im = sre_ref[...], sim_ref[...]

        def head(s_res, s_ims, cres, cims, dv, uv, gv, wg):
            ypre = jnp.concatenate([_bdot(sr, cr, 1, 0) - _bdot(si, ci, 1, 0)
                                    for sr, si, cr, ci in zip(s_res, s_ims, cres, cims)], axis=1) + dv * uv
            return _s5_post(ypre, gv, wg)

        _, vjp = jax.vjp(head, [sre_ref[:, ss] for _, ss in S5_BLOCKS], [sim_ref[:, ss] for _, ss in S5_BLOCKS],
                         [cre_ref[ss, us].astype(F32) for us, ss in S5_BLOCKS],
                         [cim_ref[ss, us].astype(F32) for us, ss in S5_BLOCKS],
                         d_ref[...], u, g_ref[...], wg_ref[...].astype(F32))
        ds_res, ds_ims, dcres, dcims, dd, du_d, dgate, dwg = vjp(do_ref[0])
        for k, (us, ss) in enumerate(S5_BLOCKS):
            dcre_ref[ss, us] += dcres[k]
            dcim_ref[ss, us] += dcims[k]
            gre[:, ss] = ds_res[k]
            gim[:, ss] = ds_ims[k]
        dd_ref[...] += dd
        dwg_ref[...] += dwg
        dp_ref[:, BW:] = dgate.astype(BF16)
        ar, ai = a_ref[0:1, :], a_ref[1:2, :]

        def step(i, carry):
            t = S5_T - 1 - i
            gr, gi = carry
            nr = gre[pl.ds(t, 1), :] + gr
            ni = gim[pl.ds(t, 1), :] + gi
            gre[pl.ds(t, 1), :] = nr
            gim[pl.ds(t, 1), :] = ni
            return ar * nr + ai * ni, ar * ni - ai * nr

        gr, gi = lax.fori_loop(0, S5_T, step, (st[0:1, :], st[1:2, :]), unroll=8)
        st[0:1, :] = gr
        st[1:2, :] = gi
        g_re, g_im = gre[...], gim[...]
        first = jnp.where(c > 0, 1.0, 0.0)
        row = lax.broadcasted_iota(jnp.int32, (S5_T, S5_CH), 0)
        p_re = jnp.where(row == 0, pre_ref[7:8, :] * first, pltpu.roll(s_re, 1, 0))
        p_im = jnp.where(row == 0, pim_ref[7:8, :] * first, pltpu.roll(s_im, 1, 0))
        da_ref[0:1, :] += jnp.sum(g_re * p_re + g_im * p_im, axis=0, keepdims=True)
        da_ref[1:2, :] += jnp.sum(g_im * p_re - g_re * p_im, axis=0, keepdims=True)
        ub, grb, gib = u.astype(BF16), g_re.astype(BF16), g_im.astype(BF16)
        du_s = []
        for us, ss in S5_BLOCKS:
            dbbre_ref[us, ss] += _dg(ub[:, us], grb[:, ss], 0, 0)
            dbbim_ref[us, ss] += _dg(ub[:, us], gib[:, ss], 0, 0)
            du_s.append(_dg(grb[:, ss], bbre_ref[us, ss], 1, 1) + _dg(gib[:, ss], bbim_ref[us, ss], 1, 1))
        dp_ref[:, :BW] = (du_d + jnp.concatenate(du_s, axis=1)).astype(BF16)

    full = lambda shape: pl.BlockSpec(shape, lambda i: (0, 0))
    rev = lambda w, col=0: pl.BlockSpec((S5_T, w), lambda i: (nc - 1 - i, col))
    prev = pl.BlockSpec((8, S5_CH), lambda i: (jnp.maximum((nc - 1 - i) * (S5_T // 8) - 1, 0), 0))
    return pl.pallas_call(
        body, name=name, grid=(nc,),
        in_specs=[rev(BW, C_S5U // BW), rev(BW, C_S5G // BW), pl.BlockSpec((1, S5_T, BW), lambda i: (0, nc - 1 - i, 0)),
                  rev(S5_CH), rev(S5_CH), prev, prev,
                  full((BW, S5_CH)), full((BW, S5_CH)), full((S5_CH, BW)), full((S5_CH, BW)),
                  full((2, S5_CH)), full((1, BW)), full((BW, BW)), pl.BlockSpec(memory_space=pl.ANY)],
        out_specs=[rev(2 * BW, C_S5U // (2 * BW)), full((BW, S5_CH)), full((BW, S5_CH)), full((S5_CH, BW)), full((S5_CH, BW)),
                   full((2, S5_CH)), full((1, BW)), full((BW, BW))],
        input_output_aliases={14: 0},
        out_shape=[jax.ShapeDtypeStruct((SEQ, IN_PAD), BF16),
                   jax.ShapeDtypeStruct((BW, S5_CH), F32), jax.ShapeDtypeStruct((BW, S5_CH), F32),
                   jax.ShapeDtypeStruct((S5_CH, BW), F32), jax.ShapeDtypeStruct((S5_CH, BW), F32),
                   jax.ShapeDtypeStruct((2, S5_CH), F32), jax.ShapeDtypeStruct((1, BW), F32),
                   jax.ShapeDtypeStruct((BW, BW), F32)],
        scratch_shapes=[pltpu.VMEM((S5_T, S5_CH), F32), pltpu.VMEM((S5_T, S5_CH), F32), pltpu.VMEM((2, S5_CH), F32)],
        compiler_params=_cparams(("arbitrary",)),
    )(proj, proj, dout, sre, sim, sre, sim, bbre, bbim, cre, cim, a2, dvec, wglu, dproj)


def _diag_blocks(dense, after=None):
    rows, cols = dense.shape
    rows_per, cols_per = rows // S5_GROUPS, cols // S5_GROUPS
    per_lane_block = LANES // cols_per
    tile = 512

    def body(d_ref, *rest):
        o_ref = rest[-1]
        r0 = pl.program_id(0) * tile
        grp = (r0 + lax.broadcasted_iota(jnp.int32, (tile, LANES), 0)) // rows_per
        lane = lax.broadcasted_iota(jnp.int32, (tile, LANES), 1)
        acc = jnp.zeros((tile, LANES), F32)
        for hb in range(cols // LANES):
            acc = acc + jnp.where(grp == per_lane_block * hb + lane // cols_per, d_ref[:, hb * LANES:(hb + 1) * LANES], 0.0)
        shift = LANES // 2
        while shift >= cols_per:
            acc = acc + pltpu.roll(acc, LANES - shift, 1)
            shift //= 2
        o_ref[...] = acc

    folded = pl.pallas_call(
        body, name=f"diag_blocks_{rows_per}x{cols_per}", grid=(rows // tile,),
        in_specs=[pl.BlockSpec((tile, cols), lambda i: (i, 0))] + ([] if after is None else [pl.BlockSpec(memory_space=pl.ANY)]),
        out_specs=pl.BlockSpec((tile, LANES), lambda i: (i, 0)),
        out_shape=jax.ShapeDtypeStruct((rows, LANES), F32), compiler_params=_cparams(("parallel",)),
    )(dense, *([] if after is None else [after]))
    return folded[:, :cols_per].reshape(S5_GROUPS, rows_per, cols_per)


def _block_diag(t):
    g, rows_per, cols_per = t.shape
    wide = jnp.tile(t.reshape(g * rows_per, cols_per), (1, g))
    r = lax.broadcasted_iota(jnp.int32, wide.shape, 0) // rows_per
    c = lax.broadcasted_iota(jnp.int32, wide.shape, 1) // cols_per
    return jnp.where(r == c, wide, 0.0)


def _s5_disc(lam_re, lam_im, b_re, b_im, c_re, c_im, d, log_step):
    step = jnp.exp(log_step)[:, None]
    mag = jnp.exp(lam_re * step)
    ab_re, ab_im = mag * jnp.cos(lam_im * step), mag * jnp.sin(lam_im * step)
    den = lam_re * lam_re + lam_im * lam_im
    nr = ab_re - 1.0
    coef_re = (nr * lam_re + ab_im * lam_im) / den
    coef_im = (ab_im * lam_re - nr * lam_im) / den
    bb_re = coef_re[..., None] * b_re - coef_im[..., None] * b_im
    bb_im = coef_re[..., None] * b_im + coef_im[..., None] * b_re
    a2 = jnp.stack([ab_re.reshape(-1), ab_im.reshape(-1)])
    return (jnp.swapaxes(bb_re, 1, 2), jnp.swapaxes(bb_im, 1, 2),
            jnp.swapaxes(c_re, 1, 2), jnp.swapaxes(c_im, 1, 2),
            a2, d.reshape(1, BW))


def _left_lanes(shape):
    return lax.broadcasted_iota(jnp.int32, shape, 1) < 64


def _sgu_chunk(u, v, gate, ln_w, ln_b, w, bias):
    u32, v32 = _gelu(u), _gelu(v)
    mu = jnp.mean(v32, axis=-1, keepdims=True)
    var = jnp.mean(jnp.square(v32 - mu), axis=-1, keepdims=True)
    vn = (v32 - mu) * lax.rsqrt(var + EPS) * ln_w + ln_b
    t_i = lax.broadcasted_iota(jnp.int32, (SGU_CHUNK, SGU_CHUNK), 0)
    s_i = lax.broadcasted_iota(jnp.int32, (SGU_CHUNK, SGU_CHUNK), 1)
    causal = t_i >= s_i
    left = _left_lanes((SGU_CHUNK, LANES))
    sgate = _silu(gate)
    outs = []
    for j in range(BW // LANES):
        vb = vn[:, j * LANES:(j + 1) * LANES]
        s_blk = (_bdot(jnp.where(causal, w[2 * j], 0.0), jnp.where(left, vb, 0.0), 1, 0)
                 + _bdot(jnp.where(causal, w[2 * j + 1], 0.0), jnp.where(left, 0.0, vb), 1, 0))
        sl = slice(j * LANES, (j + 1) * LANES)
        outs.append(u32[:, sl] * (s_blk + bias[:, sl]) * sgate[:, sl])
    return outs


def _sgu_fwd(proj, ys, ln_w, ln_b, w, bias, name):
    def body(u_ref, v_ref, g_ref, lw_ref, lb_ref, w_ref, b_ref, ys_in, o_ref):
        outs = _sgu_chunk(u_ref[...], v_ref[...], g_ref[...], lw_ref[...], lb_ref[...], w_ref[...], b_ref[...])
        for j, o in enumerate(outs):
            o_ref[0, :, j * LANES:(j + 1) * LANES] = o.astype(BF16)

    blk = lambda col: pl.BlockSpec((SGU_CHUNK, BW), lambda c: (c, col // BW))
    vec = pl.BlockSpec((1, BW), lambda c: (0, 0))
    return pl.pallas_call(
        body, name=name, grid=(SEQ // SGU_CHUNK,),
        in_specs=[blk(C_SGU_U), blk(C_SGU_V), blk(C_SGU_G), vec, vec,
                  pl.BlockSpec((SGU_HEADS, SGU_CHUNK, SGU_CHUNK), lambda c: (0, 0, 0)),
                  pl.BlockSpec((SGU_CHUNK, BW), lambda c: (0, 0)), pl.BlockSpec(memory_space=pl.ANY)],
        out_specs=pl.BlockSpec((1, SGU_CHUNK, BW), lambda c: (1, c, 0)),
        out_shape=jax.ShapeDtypeStruct((N_BRANCH, SEQ, BW), BF16), input_output_aliases={7: 0},
        compiler_params=_cparams(("parallel",)),
    )(proj, proj, proj, ln_w, ln_b, w, bias, ys)


def _sgu_bwd(proj, dproj, dout, ln_w, ln_b, w, bias, name):
    def body(u_ref, v_ref, g_ref, do_ref, lw_ref, lb_ref, w_ref, b_ref, dproj_in, dp_ref, dlw_ref, dlb_ref, dw_ref, db_ref):
        _, vjp = jax.vjp(_sgu_chunk, u_ref[...], v_ref[...], g_ref[...], lw_ref[...], lb_ref[...], w_ref[...], b_ref[...])
        do = do_ref[0]
        du, dv, dgate, dlw, dlb, dw, db = vjp([do[:, j * LANES:(j + 1) * LANES] for j in range(BW // LANES)])
        dp_ref[:, 0:BW] = du.astype(BF16)
        dp_ref[:, BW:2 * BW] = dv.astype(BF16)
        dp_ref[:, 2 * BW:3 * BW] = dgate.astype(BF16)
        dp_ref[:, 3 * BW:] = jnp.zeros((SGU_CHUNK, BW), BF16)

        @pl.when(pl.program_id(0) == 0)
        def _():
            dlw_ref[...] = dlw
            dlb_ref[...] = dlb
            dw_ref[...] = dw
            db_ref[...] = db

        @pl.when(pl.program_id(0) > 0)
        def _():
            dlw_ref[...] += dlw
            dlb_ref[...] += dlb
            dw_ref[...] += dw
            db_ref[...] += db

    blk = lambda col: pl.BlockSpec((SGU_CHUNK, BW), lambda c: (c, col // BW))
    vec = pl.BlockSpec((1, BW), lambda c: (0, 0))
    wsp = pl.BlockSpec((SGU_HEADS, SGU_CHUNK, SGU_CHUNK), lambda c: (0, 0, 0))
    bsp = pl.BlockSpec((SGU_CHUNK, BW), lambda c: (0, 0))
    return pl.pallas_call(
        body, name=name, grid=(SEQ // SGU_CHUNK,),
        in_specs=[blk(C_SGU_U), blk(C_SGU_V), blk(C_SGU_G), pl.BlockSpec((1, SGU_CHUNK, BW), lambda c: (1, c, 0)),
                  vec, vec, wsp, bsp, pl.BlockSpec(memory_space=pl.ANY)],
        out_specs=[pl.BlockSpec((SGU_CHUNK, 4 * BW), lambda c: (c, C_SGU_U // (4 * BW))), vec, vec, wsp, bsp],
        input_output_aliases={8: 0},
        out_shape=[jax.ShapeDtypeStruct((SEQ, IN_PAD), BF16), jax.ShapeDtypeStruct((1, BW), F32),
                   jax.ShapeDtypeStruct((1, BW), F32), jax.ShapeDtypeStruct((SGU_HEADS, SGU_CHUNK, SGU_CHUNK), F32),
                   jax.ShapeDtypeStruct((SGU_CHUNK, BW), F32)],
        compiler_params=_cparams(("arbitrary",)),
    )(proj, proj, proj, dout, ln_w, ln_b, w, bias, dproj)


CONV_BLK = 256


def _m2_conv_fwd(proj, w, b, name):
    def body(x_ref, w_ref, b_ref, o_ref):
        x = x_ref[...]
        acc = jnp.zeros_like(x) + b_ref[...]
        for k in range(M2_CONV):
            acc = acc + w_ref[k:k + 1, :] * _shift_down(x, M2_CONV - 1 - k)
        o_ref[...] = _silu(acc)

    return pl.pallas_call(
        body, name=name, grid=(M2_CONV_CH // CONV_BLK,),
        in_specs=[pl.BlockSpec((SEQ, CONV_BLK), lambda j: (0, C_M2X // CONV_BLK + j)),
                  pl.BlockSpec((M2_CONV, CONV_BLK), lambda j: (0, j)), pl.BlockSpec((1, CONV_BLK), lambda j: (0, j))],
        out_specs=pl.BlockSpec((SEQ, CONV_BLK), lambda j: (0, j)),
        out_shape=jax.ShapeDtypeStruct((SEQ, M2_CONV_CH), F32),
        compiler_params=_cparams(("parallel",)),
    )(proj, w, b)


def _m2_conv_bwd(proj, dproj, dxa, w, b, name):
    def body(x_ref, d_ref, w_ref, b_ref, dproj_in, dx_ref, dw_ref, db_ref):
        x = x_ref[...]
        xs = [_shift_down(x, M2_CONV - 1 - k) for k in range(M2_CONV)]
        acc = jnp.zeros_like(x) + b_ref[...]
        for k in range(M2_CONV):
            acc = acc + w_ref[k:k + 1, :] * xs[k]
        sg = jax.nn.sigmoid(acc)
        dacc = d_ref[...] * (sg * (1.0 + acc * (1.0 - sg)))
        dx = jnp.zeros_like(x)
        for k in range(M2_CONV):
            dx = dx + w_ref[k:k + 1, :] * _shift_up(dacc, M2_CONV - 1 - k)
            dw_ref[k:k + 1, :] = jnp.sum(dacc * xs[k], axis=0, keepdims=True)
        dx_ref[...] = dx.astype(BF16)
        db_ref[...] = jnp.sum(dacc, axis=0, keepdims=True)

    return pl.pallas_call(
        body, name=name, grid=(M2_CONV_CH // CONV_BLK,),
        in_specs=[pl.BlockSpec((SEQ, CONV_BLK), lambda j: (0, C_M2X // CONV_BLK + j)),
                  pl.BlockSpec((SEQ, CONV_BLK), lambda j: (0, j)),
                  pl.BlockSpec((M2_CONV, CONV_BLK), lambda j: (0, j)), pl.BlockSpec((1, CONV_BLK), lambda j: (0, j)),
                  pl.BlockSpec(memory_space=pl.ANY)],
        out_specs=[pl.BlockSpec((SEQ, CONV_BLK), lambda j: (0, C_M2X // CONV_BLK + j)),
                   pl.BlockSpec((M2_CONV, CONV_BLK), lambda j: (0, j)), pl.BlockSpec((1, CONV_BLK), lambda j: (0, j))],
        input_output_aliases={4: 0},
        out_shape=[jax.ShapeDtypeStruct((SEQ, IN_PAD), BF16), jax.ShapeDtypeStruct((M2_CONV, M2_CONV_CH), F32),
                   jax.ShapeDtypeStruct((1, M2_CONV_CH), F32)],
        compiler_params=_cparams(("parallel",)),
    )(proj, dxa, w, b, dproj)


N_PAIR = M2_HEADS // 2
HI = lax.Precision.HIGHEST


def _col(a, h):
    lane = lax.broadcasted_iota(jnp.int32, a.shape, 1)
    return jnp.sum(jnp.where(lane == h, a, 0.0), axis=1, keepdims=True)


def _row(a, h):
    sub = lax.broadcasted_iota(jnp.int32, a.shape, 0)
    return jnp.sum(jnp.where(sub == h, a, 0.0), axis=0, keepdims=True)


def _ssd_chunk(xs, bms, cms, dtr, zs, states, dt_bias, a_log, dfs, nws):
    q = M2_CHUNK
    dt = _softplus(dtr + dt_bias)
    da = dt * (-jnp.exp(a_log))
    l_i = lax.broadcasted_iota(jnp.int32, (q, q), 0)
    s_i = lax.broadcasted_iota(jnp.int32, (q, q), 1)
    causal = l_i >= s_i
    tril = jnp.where(causal, 1.0, 0.0)
    a_cs = _dg(tril, da, 1, 0, HI)
    a_cs_t = _dg(da, tril, 0, 1, HI)
    a_end = _row(a_cs, q - 1)
    left = _left_lanes((q, LANES))
    left1 = _left_lanes((1, LANES))
    ys, nexts = [], []
    for j in range(N_PAIR):
        grp = j // 2
        bm, cm = bms[grp], cms[grp]
        h0, h1 = 2 * j, 2 * j + 1
        cb = _bdot(cm, bm, 1, 1)
        xdt = xs[j] * jnp.where(left, _col(dt, h0), _col(dt, h1))
        acs0, acs1 = _col(a_cs, h0), _col(a_cs, h1)
        y = _bdot(cm, states[j], 1, 0) * jnp.where(left, jnp.exp(acs0), jnp.exp(acs1))
        s_new = states[j] * jnp.where(left1, jnp.exp(_col(a_end, h0)), jnp.exp(_col(a_end, h1)))
        for h, acs, xh in ((h0, acs0, jnp.where(left, xdt, 0.0)), (h1, acs1, jnp.where(left, 0.0, xdt))):
            decay = jnp.exp(jnp.where(causal, acs - _row(a_cs_t, h), -jnp.inf))
            y = y + _bdot(cb * decay, xh, 1, 0)
            s_new = s_new + _bdot(bm * jnp.exp(_col(a_end, h) - acs), xh, 0, 0)
        ys.append((y + dfs[j] * xs[j]) * _silu(zs[j]))
        nexts.append(s_new)
    ssq = sum(jnp.sum(y * y, axis=-1, keepdims=True) for y in ys)
    scale = lax.rsqrt(ssq / BW + EPS)
    return [y * scale * nw for y, nw in zip(ys, nws)], nexts


def _blocks(ref, n, width=LANES):
    return [ref[:, j * width:(j + 1) * width] for j in range(n)]


def _ssd_fwd(proj, ys, xa, dt_bias, a_log, dfull, nw, name):
    nc = SEQ // M2_CHUNK

    def body(x_ref, b_ref, c_ref, dt_ref, z_ref, dtb_ref, al_ref, df_ref, nw_ref, ys_in, o_ref, sin_ref, st):
        @pl.when(pl.program_id(0) == 0)
        def _():
            st[...] = jnp.zeros_like(st)

        states = [st[j] for j in range(N_PAIR)]
        for j in range(N_PAIR):
            sin_ref[0, j] = states[j]
        ys, nexts = _ssd_chunk(_blocks(x_ref, 4), _blocks(b_ref, 2), _blocks(c_ref, 2), dt_ref[...], _blocks(z_ref, 4),
                               states, dtb_ref[...], al_ref[...], _blocks(df_ref, 4), _blocks(nw_ref, 4))
        for j in range(N_PAIR):
            o_ref[0, :, j * LANES:(j + 1) * LANES] = ys[j].astype(BF16)
            st[j] = nexts[j]

    vec8 = pl.BlockSpec((1, LANES), lambda c: (0, 0))
    vec = pl.BlockSpec((1, BW), lambda c: (0, 0))
    return pl.pallas_call(
        body, name=name, grid=(nc,),
        in_specs=[pl.BlockSpec((M2_CHUNK, BW), lambda c: (c, 0)), pl.BlockSpec((M2_CHUNK, 256), lambda c: (c, 2)),
                  pl.BlockSpec((M2_CHUNK, 256), lambda c: (c, 3)), pl.BlockSpec((M2_CHUNK, LANES), lambda c: (c, C_DT // LANES)),
                  pl.BlockSpec((M2_CHUNK, BW), lambda c: (c, C_M2Z // BW)), vec8, vec8, vec, vec,
                  pl.BlockSpec(memory_space=pl.ANY)],
        out_specs=[pl.BlockSpec((1, M2_CHUNK, BW), lambda c: (2, c, 0)),
                   pl.BlockSpec((1, N_PAIR, M2_STATE, LANES), lambda c: (c, 0, 0, 0))],
        out_shape=[jax.ShapeDtypeStruct((N_BRANCH, SEQ, BW), BF16), jax.ShapeDtypeStruct((nc, N_PAIR, M2_STATE, LANES), F32)],
        input_output_aliases={9: 0},
        scratch_shapes=[pltpu.VMEM((N_PAIR, M2_STATE, LANES), F32)],
        compiler_params=_cparams(("arbitrary",)),
    )(xa, xa, xa, proj, proj, dt_bias, a_log, dfull, nw, ys)


def _ssd_bwd(proj, dproj, xa, dout, s_in, dt_bias, a_log, dfull, nw, name):
    nc = SEQ // M2_CHUNK

    def body(x_ref, b_ref, c_ref, dt_ref, z_ref, do_ref, sin_ref, dtb_ref, al_ref, df_ref, nw_ref, dproj_in,
             dp_ref, dxa_ref, ddtb_ref, dal_ref, ddf_ref, dnw_ref, dst):
        @pl.when(pl.program_id(0) == 0)
        def _():
            dst[...] = jnp.zeros_like(dst)
            for r in (ddtb_ref, dal_ref, ddf_ref, dnw_ref):
                r[...] = jnp.zeros_like(r)

        states = [sin_ref[0, j] for j in range(N_PAIR)]
        _, vjp = jax.vjp(_ssd_chunk, _blocks(x_ref, 4), _blocks(b_ref, 2), _blocks(c_ref, 2), dt_ref[...],
                         _blocks(z_ref, 4), states, dtb_ref[...], al_ref[...], _blocks(df_ref, 4), _blocks(nw_ref, 4))
        dxs, dbs, dcs, ddt, dzs, dstates, ddtb, dal, ddfs, dnws = vjp(
            ([do_ref[0, :, j * LANES:(j + 1) * LANES] for j in range(N_PAIR)], [dst[j] for j in range(N_PAIR)]))
        for j in range(N_PAIR):
            sl = slice(j * LANES, (j + 1) * LANES)
            dxa_ref[:, sl] = dxs[j]
            dp_ref[:, sl] = dzs[j].astype(BF16)
            dst[j] = dstates[j]
            ddf_ref[:, sl] += ddfs[j]
            dnw_ref[:, sl] += dnws[j]
        for g in range(2):
            dxa_ref[:, BW + g * LANES:BW + (g + 1) * LANES] = dbs[g]
            dxa_ref[:, BW + 256 + g * LANES:BW + 256 + (g + 1) * LANES] = dcs[g]
        dp_ref[:, BW:BW + LANES] = ddt.astype(BF16)
        dp_ref[:, BW + LANES:] = jnp.zeros((M2_CHUNK, 2 * BW - BW - LANES), BF16)
        ddtb_ref[...] += ddtb
        dal_ref[...] += dal

    rev = lambda w, col=0: pl.BlockSpec((M2_CHUNK, w), lambda i: (nc - 1 - i, col))
    vec8 = pl.BlockSpec((1, LANES), lambda i: (0, 0))
    vec = pl.BlockSpec((1, BW), lambda i: (0, 0))
    return pl.pallas_call(
        body, name=name, grid=(nc,),
        in_specs=[rev(BW), rev(256, 2), rev(256, 3), rev(LANES, C_DT // LANES), rev(BW, C_M2Z // BW),
                  pl.BlockSpec((1, M2_CHUNK, BW), lambda i: (2, nc - 1 - i, 0)),
                  pl.BlockSpec((1, N_PAIR, M2_STATE, LANES), lambda i: (nc - 1 - i, 0, 0, 0)), vec8, vec8, vec, vec,
                  pl.BlockSpec(memory_space=pl.ANY)],
        out_specs=[rev(2 * BW, C_M2Z // (2 * BW)), rev(M2_CONV_CH), vec8, vec8, vec, vec],
        input_output_aliases={11: 0},
        out_shape=[jax.ShapeDtypeStruct((SEQ, IN_PAD), BF16), jax.ShapeDtypeStruct((SEQ, M2_CONV_CH), F32),
                   jax.ShapeDtypeStruct((1, LANES), F32), jax.ShapeDtypeStruct((1, LANES), F32),
                   jax.ShapeDtypeStruct((1, BW), F32), jax.ShapeDtypeStruct((1, BW), F32)],
        scratch_shapes=[pltpu.VMEM((N_PAIR, M2_STATE, LANES), F32)],
        compiler_params=_cparams(("arbitrary",)),
    )(xa, xa, xa, proj, proj, dout, s_in, dt_bias, a_log, dfull, nw, dproj)


def _sc_specs():
    col = lambda kind: pl.BlockSpec((SEQ, LANES), lambda j: (0, C_SC // LANES + 4 * j + kind))
    return [col(0), col(1), col(2), col(3)]


def _sc_fwd(proj, ys, w, name):
    def body(b_ref, c_ref, h_ref, g_ref, w_ref, ys_in, o_ref):
        ch = c_ref[...] * h_ref[...]
        acc = jnp.zeros_like(ch)
        for k in range(SC_CONV):
            acc = acc + w_ref[k:k + 1, :] * _shift_down(ch, SC_CONV - 1 - k)
        o_ref[0] = (b_ref[...] * acc * _silu(g_ref[...])).astype(BF16)

    return pl.pallas_call(
        body, name=name, grid=(BW // LANES,),
        in_specs=_sc_specs() + [pl.BlockSpec((SC_CONV, LANES), lambda j: (0, j)), pl.BlockSpec(memory_space=pl.ANY)],
        out_specs=pl.BlockSpec((1, SEQ, LANES), lambda j: (3, 0, j)),
        out_shape=jax.ShapeDtypeStruct((N_BRANCH, SEQ, BW), BF16), input_output_aliases={5: 0},
        compiler_params=_cparams(("parallel",)),
    )(proj, proj, proj, proj, w, ys)


def _sc_bwd(proj, dproj, dout, w, name):
    def body(b_ref, c_ref, h_ref, g_ref, do_ref, w_ref, dproj_in, dp_ref, dw_ref):
        cv, hv, gv = c_ref[...], h_ref[...], g_ref[...]
        ch = cv * hv
        chs = [_shift_down(ch, SC_CONV - 1 - k) for k in range(SC_CONV)]
        acc = jnp.zeros_like(ch)
        for k in range(SC_CONV):
            acc = acc + w_ref[k:k + 1, :] * chs[k]
        sg = jax.nn.sigmoid(gv)
        do = do_ref[0]
        bv = b_ref[...]
        dp_ref[:, 0:LANES] = (do * acc * (gv * sg)).astype(BF16)
        dp_ref[:, 3 * LANES:] = (do * bv * acc * (sg * (1.0 + gv * (1.0 - sg)))).astype(BF16)
        dacc = do * bv * (gv * sg)
        dch = jnp.zeros_like(ch)
        for k in range(SC_CONV):
            dch = dch + w_ref[k:k + 1, :] * _shift_up(dacc, SC_CONV - 1 - k)
            dw_ref[k:k + 1, :] = jnp.sum(dacc * chs[k], axis=0, keepdims=True)
        dp_ref[:, LANES:2 * LANES] = (dch * hv).astype(BF16)
        dp_ref[:, 2 * LANES:3 * LANES] = (dch * cv).astype(BF16)

    wsp = pl.BlockSpec((SC_CONV, LANES), lambda j: (0, j))
    return pl.pallas_call(
        body, name=name, grid=(BW // LANES,),
        in_specs=_sc_specs() + [pl.BlockSpec((1, SEQ, LANES), lambda j: (3, 0, j)), wsp, pl.BlockSpec(memory_space=pl.ANY)],
        out_specs=[pl.BlockSpec((SEQ, 4 * LANES), lambda j: (0, C_SC // (4 * LANES) + j)), wsp],
        input_output_aliases={6: 0},
        out_shape=[jax.ShapeDtypeStruct((SEQ, IN_PAD), BF16), jax.ShapeDtypeStruct((SC_CONV, BW), F32)],
        compiler_params=_cparams(("parallel",)),
    )(proj, proj, proj, proj, dout, w, dproj)


MERGE_T = 256
MERGE_BWD_T = 1024


def _merge_fwd(proj, ys, merge_b, w_branch, name):
    def body(y_ref, lg_ref, b_ref, w_ref, o_ref):
        acc = jnp.zeros((MERGE_T, D_MODEL), F32)
        for k in range(N_BRANCH):
            gate = jax.nn.sigmoid(lg_ref[:, k * D_MODEL:(k + 1) * D_MODEL] + b_ref[k])
            acc = acc + gate * _dg(y_ref[k], w_ref[k], 1, 0)
        o_ref[...] = acc.astype(BF16)

    return pl.pallas_call(
        body, name=name, grid=(SEQ // MERGE_T,),
        in_specs=[pl.BlockSpec((N_BRANCH, MERGE_T, BW), lambda i: (0, i, 0)),
                  pl.BlockSpec((MERGE_T, N_BRANCH * D_MODEL), lambda i: (i, C_MERGE // (N_BRANCH * D_MODEL))),
                  pl.BlockSpec((N_BRANCH, 1, D_MODEL), lambda i: (0, 0, 0)),
                  pl.BlockSpec((N_BRANCH, BW, D_MODEL), lambda i: (0, 0, 0))],
        out_specs=pl.BlockSpec((MERGE_T, D_MODEL), lambda i: (i, 0)),
        out_shape=jax.ShapeDtypeStruct((SEQ, D_MODEL), BF16),
        compiler_params=_cparams(("parallel",)),
    )(ys, proj, merge_b, w_branch)


def _merge_bwd(proj, ys, dm, merge_b, w_branch, name):
    nt = SEQ // MERGE_BWD_T

    def body(y_ref, lg_ref, dm_ref, b_ref, w_ref, dy_ref, dlg_ref, dw_ref, db_ref, dw_acc):
        i = pl.program_id(1)
        gate = jax.nn.sigmoid(lg_ref[...] + b_ref[0])
        y = y_ref[0]
        dmv = dm_ref[...]
        dbo = (gate * dmv).astype(BF16)
        dlg = _dg(y, w_ref[0], 1, 0) * dmv * gate * (1.0 - gate)
        dlg_ref[...] = dlg.astype(BF16)
        dy_ref[0] = _dg(dbo, w_ref[0], 1, 1)
        dwp = _dg(y, dbo, 0, 0)
        dbp = jnp.sum(dlg, axis=0, keepdims=True)

        @pl.when(i == 0)
        def _():
            dw_acc[...] = dwp
            db_ref[0] = dbp

        @pl.when(i > 0)
        def _():
            dw_acc[...] += dwp
            db_ref[0] += dbp

        @pl.when(i == nt - 1)
        def _():
            dw_ref[0] = dw_acc[...].astype(BF16)

    return pl.pallas_call(
        body, name=name, grid=(N_BRANCH, nt),
        in_specs=[pl.BlockSpec((1, MERGE_BWD_T, BW), lambda k, i: (k, i, 0)),
                  pl.BlockSpec((MERGE_BWD_T, D_MODEL), lambda k, i: (i, C_MERGE // D_MODEL + k)),
                  pl.BlockSpec((MERGE_BWD_T, D_MODEL), lambda k, i: (i, 0)),
                  pl.BlockSpec((1, 1, D_MODEL), lambda k, i: (k, 0, 0)),
                  pl.BlockSpec((1, BW, D_MODEL), lambda k, i: (k, 0, 0))],
        out_specs=[pl.BlockSpec((1, MERGE_BWD_T, BW), lambda k, i: (k, i, 0)),
                   pl.BlockSpec((MERGE_BWD_T, D_MODEL), lambda k, i: (i, k)),
                   pl.BlockSpec((1, BW, D_MODEL), lambda k, i: (k, 0, 0)),
                   pl.BlockSpec((1, 1, D_MODEL), lambda k, i: (k, 0, 0))],
        out_shape=[jax.ShapeDtypeStruct((N_BRANCH, SEQ, BW), F32), jax.ShapeDtypeStruct((SEQ, IN_PAD), BF16),
                   jax.ShapeDtypeStruct((N_BRANCH, BW, D_MODEL), BF16), jax.ShapeDtypeStruct((N_BRANCH, 1, D_MODEL), F32)],
        scratch_shapes=[pltpu.VMEM((BW, D_MODEL), F32)],
        compiler_params=_cparams(("parallel", "arbitrary")),
    )(ys, proj, dm, merge_b, w_branch)


def _adamw(glist, w, m, v, rows, name):
    nl = len(glist)
    n, r, c = glist[0].shape
    assert w.shape == (nl, r, c) and r % rows == 0
    nb = r // rows

    def body(*refs):
        g_refs = refs[:nl]
        w_ref, m_ref, v_ref, go_ref, d_ref, mo_ref, vo_ref = refs[nl:]
        for layer in range(nl):
            @pl.when(pl.program_id(0) == layer)
            def _(g_ref=g_refs[layer]):
                g = g_ref[0].astype(F32)
                for s in range(1, n):
                    g = g + g_ref[s].astype(F32)
                mn = ADAM_B1 * m_ref[0] + (1.0 - ADAM_B1) * g
                vn = ADAM_B2 * v_ref[0] + (1.0 - ADAM_B2) * jnp.square(g)
                m_hat = mn / (1.0 - ADAM_B1 ** ADAM_STEP)
                v_hat = vn / (1.0 - ADAM_B2 ** ADAM_STEP)
                go_ref[0] = g
                d_ref[0] = -ADAM_LR * (m_hat / (jnp.sqrt(v_hat) + ADAM_EPS) + ADAM_WD * w_ref[0])
                mo_ref[0] = mn
                vo_ref[0] = vn

    def g_spec(layer):
        return pl.BlockSpec((n, rows, c), lambda a, i: (0, jnp.where(a < layer, 0, jnp.where(a == layer, i, nb - 1)), 0))

    blk = pl.BlockSpec((1, rows, c), lambda a, i: (a, i, 0))
    out = jax.ShapeDtypeStruct((nl, r, c), F32)
    return pl.pallas_call(
        body, name=name, grid=(nl, nb),
        in_specs=[g_spec(layer) for layer in range(nl)] + [blk, blk, blk],
        out_specs=[blk, blk, blk, blk], out_shape=[out, out, out, out],
        compiler_params=_cparams(("arbitrary", "arbitrary")),
    )(*glist, w, m, v)


X_ROWS_PER_COL = 2 * (D_MODEL // LANES)


def _w_in_to_x(w):
    t = jnp.transpose(w, (2, 0, 1)).reshape(SHARD_IN, DEPTH, D_MODEL // LANES, LANES)
    return jnp.transpose(t, (0, 2, 1, 3)).reshape(SHARD_IN * X_ROWS_PER_COL, LANES)


def _w_in_from_x(xv):
    t = jnp.transpose(xv.reshape(SHARD_IN, D_MODEL // LANES, DEPTH, LANES), (0, 2, 1, 3))
    return jnp.transpose(t.reshape(SHARD_IN, DEPTH, D_MODEL), (1, 2, 0))


def _adamw_w_in(glist, w, m, v, name, after=None):
    n = glist[0].shape[0]
    cols = 2 * LANES
    rows = cols * X_ROWS_PER_COL
    extra = [] if after is None else [after]

    def body(g0_ref, g1_ref, w_ref, m_ref, v_ref, *rest):
        go_ref, d_ref, mo_ref, vo_ref = rest[len(extra):]
        for layer, g_ref in enumerate((g0_ref, g1_ref)):
            g = g_ref[0].astype(F32)
            for s in range(1, n):
                g = g + g_ref[s].astype(F32)
            gt = g.T
            for t in range(D_MODEL // LANES):
                sel = (pl.ds(2 * t + layer, cols, stride=X_ROWS_PER_COL), slice(None))
                gs = gt[:, t * LANES:(t + 1) * LANES]
                mn = ADAM_B1 * m_ref[sel] + (1.0 - ADAM_B1) * gs
                vn = ADAM_B2 * v_ref[sel] + (1.0 - ADAM_B2) * jnp.square(gs)
                m_hat = mn / (1.0 - ADAM_B1 ** ADAM_STEP)
                v_hat = vn / (1.0 - ADAM_B2 ** ADAM_STEP)
                go_ref[sel] = gs
                d_ref[sel] = -ADAM_LR * (m_hat / (jnp.sqrt(v_hat) + ADAM_EPS) + ADAM_WD * w_ref[sel])
                mo_ref[sel] = mn
                vo_ref[sel] = vn

    g_spec = pl.BlockSpec((n, D_MODEL, cols), lambda i: (0, 0, i))
    blk = pl.BlockSpec((rows, LANES), lambda i: (i, 0))
    out = jax.ShapeDtypeStruct((SHARD_IN * X_ROWS_PER_COL, LANES), F32)
    res = pl.pallas_call(
        body, name=name, grid=(-(-SHARD_IN // cols),),
        in_specs=[g_spec, g_spec, blk, blk, blk] + [pl.BlockSpec(memory_space=pl.ANY)] * len(extra),
        out_specs=[blk, blk, blk, blk], out_shape=[out, out, out, out],
        compiler_params=_cparams(("parallel",)),
    )(*glist, _w_in_to_x(w), _w_in_to_x(m), _w_in_to_x(v), *extra)
    return [_w_in_from_x(o) for o in res]


def _adamw_many(gs, ws, ms, vs, name):
    k = len(gs)

    def body(*refs):
        g_refs, w_refs, m_refs, v_refs = refs[:k], refs[k:2 * k], refs[2 * k:3 * k], refs[3 * k:4 * k]
        d_refs, mo_refs, vo_refs = refs[4 * k:5 * k], refs[5 * k:6 * k], refs[6 * k:7 * k]
        for i in range(k):
            g = g_refs[i][...]
            mn = ADAM_B1 * m_refs[i][...] + (1.0 - ADAM_B1) * g
            vn = ADAM_B2 * v_refs[i][...] + (1.0 - ADAM_B2) * jnp.square(g)
            m_hat = mn / (1.0 - ADAM_B1 ** ADAM_STEP)
            v_hat = vn / (1.0 - ADAM_B2 ** ADAM_STEP)
            d_refs[i][...] = -ADAM_LR * (m_hat / (jnp.sqrt(v_hat) + ADAM_EPS) + ADAM_WD * w_refs[i][...])
            mo_refs[i][...] = mn
            vo_refs[i][...] = vn

    whole = pl.BlockSpec(memory_space=pltpu.VMEM)
    shapes = [jax.ShapeDtypeStruct(w.shape, F32) for w in ws]
    outs = pl.pallas_call(
        body, name=name, in_specs=[whole] * (4 * k), out_specs=[whole] * (3 * k), out_shape=shapes * 3,
        compiler_params=_cparams(None),
    )(*gs, *ws, *ms, *vs)
    return outs[:k], outs[k:2 * k], outs[2 * k:]


MEMORY_ORDER = {'s5_b_re': (0, 1, 3, 2), 's5_b_im': (0, 1, 3, 2), 's5_d': (0, 2, 1), 'sc_conv_w': (1, 0, 2)}


def _memory_view(name, t):
    return jnp.transpose(t, MEMORY_ORDER[name]) if name in MEMORY_ORDER else t


def _slot_sum(gslots, name):
    n, r, c = gslots.shape

    def body(g_ref, o_ref):
        g = g_ref[0]
        for s in range(1, n):
            g = g + g_ref[s]
        o_ref[...] = g

    return pl.pallas_call(
        body, name=name, in_specs=[pl.BlockSpec((n, r, c), lambda: (0, 0, 0))],
        out_specs=pl.BlockSpec((r, c), lambda: (0, 0)), out_shape=jax.ShapeDtypeStruct((r, c), F32),
        compiler_params=_cparams(None),
    )(gslots)


def _me_and_peers():
    x, y, c = lax.axis_index("x"), lax.axis_index("y"), lax.axis_index("c")
    me = 4 * x + 2 * y + c
    peers = []
    for k in range(1, N_DEV):
        px = 1 - x if (k >> 2) & 1 else x
        py = 1 - y if (k >> 1) & 1 else y
        pc = 1 - c if k & 1 else c
        peers.append((4 * px + 2 * py + pc, (px, py, pc)))
    return me, peers


_HBM = pl.BlockSpec(memory_space=pltpu.HBM)
_SEM = pl.BlockSpec(memory_space=pltpu.SEMAPHORE)
_EFFECT = pltpu.SideEffectType.DATAFLOW_SIDE_EFFECTING


N_CHIP = N_DEV // 2


def _chip_peers():
    x, y, c = lax.axis_index("x"), lax.axis_index("y"), lax.axis_index("c")
    chips = []
    for d in range(1, N_CHIP):
        px = 1 - x if (d >> 1) & 1 else x
        py = 1 - y if d & 1 else y
        chips.append((2 * px + py, (px, py)))
    return (x, y, c), 2 * x + y, chips


def _plan_direct(ins, lands, send_sems, recv_sems, local_sems, gather):
    me, peers = _me_and_peers()
    plan = dict(start=[], local=[], sends=[], recvs=[])
    for t in range(len(ins)):
        own = pltpu.make_async_copy(ins[t] if gather else ins[t].at[me], lands[t].at[me], local_sems.at[t])
        plan['start'].append(own)
        plan['local'].append(own)
        for k, (pidx, pos) in enumerate(peers):
            cp = pltpu.make_async_remote_copy(
                src_ref=ins[t] if gather else ins[t].at[pidx], dst_ref=lands[t].at[me],
                send_sem=send_sems.at[t * (N_DEV - 1) + k], recv_sem=recv_sems.at[t * (N_DEV - 1) + k],
                device_id=pos, device_id_type=MESH)
            plan['start'].append(cp)
            plan['sends'].append(cp)
            plan['recvs'].append(cp)
    return plan


def _plan_gather(ins, lands, send_sems, recv_sems, local_sems, first=0):
    (x, y, c), q, chips = _chip_peers()
    me = 2 * q + c
    plan = dict(start=[], relay_wait=[], relay_start=[], local=[], sends=[], recvs=[])
    for t in range(len(ins)):
        base = (first + t) * 7
        sem = lambda k: dict(send_sem=send_sems.at[base + k], recv_sem=recv_sems.at[base + k], device_id_type=MESH)
        own = pltpu.make_async_copy(ins[t], lands[t].at[me], local_sems.at[first + t])
        to_sib = pltpu.make_async_remote_copy(src_ref=ins[t], dst_ref=lands[t].at[me], device_id=(x, y, 1 - c), **sem(0))
        plan['start'] += [own, to_sib]
        plan['local'].append(own)
        plan['sends'].append(to_sib)
        plan['recvs'].append(to_sib)
        for d, (pq, (px, py)) in enumerate(chips):
            to_chip = pltpu.make_async_remote_copy(src_ref=ins[t], dst_ref=lands[t].at[me], device_id=(px, py, c), **sem(1 + d))
            blk = lands[t].at[2 * pq + c]
            fwd = pltpu.make_async_remote_copy(src_ref=blk, dst_ref=blk, device_id=(x, y, 1 - c), **sem(4 + d))
            plan['start'].append(to_chip)
            plan['relay_wait'].append(to_chip)
            plan['relay_start'].append(fwd)
            plan['sends'] += [to_chip, fwd]
            plan['recvs'].append(fwd)
    return plan


def _plan_pair(ins, lands, send_sems, recv_sems, local_sems):
    (x, y, c), q, chips = _chip_peers()
    plan = dict(start=[], local=[], sends=[], recvs=[])
    for t in range(len(ins)):
        for k in range(N_CHIP):
            cp = pltpu.make_async_remote_copy(
                src_ref=ins[t].at[2 * k + 1 - c], dst_ref=lands[t].at[k], send_sem=send_sems.at[t * N_CHIP + k],
                recv_sem=recv_sems.at[t * N_CHIP + k], device_id=(x, y, 1 - c), device_id_type=MESH)
            plan['start'].append(cp)
            plan['sends'].append(cp)
            plan['recvs'].append(cp)
    return plan


def _plan_chips(ins, lands, send_sems, recv_sems, local_sems):
    (x, y, c), q, chips = _chip_peers()
    plan = dict(start=[], local=[], sends=[], recvs=[])
    for t in range(len(ins)):
        own = pltpu.make_async_copy(ins[t].at[q], lands[t].at[q], local_sems.at[t])
        plan['start'].append(own)
        plan['local'].append(own)
        for d, (pq, (px, py)) in enumerate(chips):
            cp = pltpu.make_async_remote_copy(
                src_ref=ins[t].at[pq], dst_ref=lands[t].at[q], send_sem=send_sems.at[t * 3 + d],
                recv_sem=recv_sems.at[t * 3 + d], device_id=(px, py, c), device_id_type=MESH)
            plan['start'].append(cp)
            plan['sends'].append(cp)
            plan['recvs'].append(cp)
    return plan


def _split_start(plan_fn, tensors, land_shapes, n_sems, name, after=None):
    n = len(tensors)
    extra = [] if after is None else [after]

    def body(*refs):
        ins, lands = refs[:n], refs[n:2 * n]
        plan = plan_fn(ins, lands, *refs[2 * n + len(extra):2 * n + len(extra) + 3])
        for cp in plan['start']:
            cp.start()
        refs[-1][...] = jnp.zeros_like(refs[-1])

    outs = pl.pallas_call(
        body, name=name,
        out_shape=(pltpu.SemaphoreType.DMA((n_sems,)), pltpu.SemaphoreType.DMA((n_sems,)), pltpu.SemaphoreType.DMA((n,)),
                   *[pltpu.HBM(t.shape, t.dtype) for t in tensors],
                   *[pltpu.HBM(s, t.dtype) for s, t in zip(land_shapes, tensors)],
                   jax.ShapeDtypeStruct((8, LANES), F32)),
        in_specs=[_HBM] * (2 * n) + [pl.BlockSpec(memory_space=pl.ANY)] * len(extra),
        out_specs=(_SEM, _SEM, _SEM, *[_HBM] * (2 * n), pl.BlockSpec(memory_space=pltpu.VMEM)),
        input_output_aliases={t: 3 + t for t in range(2 * n)},
        compiler_params=pltpu.CompilerParams(has_side_effects=_EFFECT),
    )(*[pltpu.with_memory_space_constraint(t, pltpu.HBM) for t in tensors],
      *[pltpu.with_memory_space_constraint(lax.empty(s, t.dtype), pltpu.HBM) for s, t in zip(land_shapes, tensors)], *extra)
    return outs[:-1], outs[-1]


def _split_relay(plan_fn, state, after, name):
    sems, thru = state[:3], state[3:]
    n = len(thru) // 2

    def arrived(*refs):
        plan = plan_fn(refs[:n], refs[n:2 * n], *refs[2 * n:2 * n + 3])
        for cp in plan['relay_wait']:
            cp.wait_recv()

    thru = pl.pallas_call(
        arrived, name=name + "_arrived",
        out_shape=tuple(pltpu.HBM(t.shape, t.dtype) for t in thru),
        in_specs=[_HBM] * (2 * n) + [_SEM, _SEM, _SEM, pl.BlockSpec(memory_space=pl.ANY)],
        out_specs=tuple([_HBM] * (2 * n)),
        input_output_aliases={t: t for t in range(2 * n)},
        compiler_params=pltpu.CompilerParams(has_side_effects=_EFFECT),
    )(*thru, *sems, after)

    def forward(*refs):
        plan = plan_fn(refs[:n], refs[n:2 * n], *refs[2 * n:2 * n + 3])
        for cp in plan['relay_start']:
            cp.start()
        refs[-1][...] = jnp.zeros_like(refs[-1])

    outs = pl.pallas_call(
        forward, name=name + "_forward",
        out_shape=(*[pltpu.HBM(t.shape, t.dtype) for t in thru], jax.ShapeDtypeStruct((8, LANES), F32)),
        in_specs=[_HBM] * (2 * n) + [_SEM, _SEM, _SEM],
        out_specs=(*[_HBM] * (2 * n), pl.BlockSpec(memory_space=pltpu.VMEM)),
        input_output_aliases={t: t for t in range(2 * n)},
        compiler_params=pltpu.CompilerParams(has_side_effects=_EFFECT),
    )(*thru, *sems)
    return (*sems, *outs[:-1]), outs[-1]


def _split_wait(plan_fn, state, after, name, with_sources=False):
    sems, thru = state[:3], state[3:]
    n = len(thru) // 2

    def body(*refs):
        plan = plan_fn(refs[:n], refs[n:2 * n], *refs[2 * n:2 * n + 3])
        for cp in plan['local']:
            cp.wait()
        for cp in plan['sends']:
            cp.wait_send()
        for cp in plan['recvs']:
            cp.wait_recv()

    outs = pl.pallas_call(
        body, name=name,
        out_shape=tuple(pltpu.HBM(t.shape, t.dtype) for t in thru),
        in_specs=[_HBM] * (2 * n) + [_SEM, _SEM, _SEM, pl.BlockSpec(memory_space=pl.ANY)],
        out_specs=tuple([_HBM] * (2 * n)),
        input_output_aliases={t: t for t in range(2 * n)},
        compiler_params=pltpu.CompilerParams(has_side_effects=_EFFECT),
    )(*thru, *sems, after)
    return (list(outs[:n]), list(outs[n:])) if with_sources else list(outs[n:])


PAIR_SUM_BLOCK = 768 * 1024


def _pair_sum(mine, theirs, name):
    _, r, c = mine.shape
    rows = r
    while rows * c > PAIR_SUM_BLOCK and rows % 32 == 0:
        rows //= 2

    def body(core_ref, a_ref, b_ref, o_ref):
        o_ref[0] = (a_ref[0].astype(F32) + b_ref[0].astype(F32)).astype(o_ref.dtype)

    return pl.pallas_call(
        body, name=name,
        grid_spec=pltpu.PrefetchScalarGridSpec(
            num_scalar_prefetch=1, grid=(N_CHIP, r // rows),
            in_specs=[pl.BlockSpec((1, rows, c), lambda k, i, core: (2 * k + core[0], i, 0)),
                      pl.BlockSpec((1, rows, c), lambda k, i, core: (k, i, 0))],
            out_specs=pl.BlockSpec((1, rows, c), lambda k, i, core: (k, i, 0))),
        out_shape=jax.ShapeDtypeStruct((N_CHIP, r, c), mine.dtype),
        compiler_params=_cparams(("parallel", "parallel")),
    )(lax.axis_index("c").astype(jnp.int32).reshape(1), mine, theirs)


WEIGHTS = ['norm_w', 'w_in', 's5_lambda_re', 's5_lambda_im', 's5_b_re', 's5_b_im', 's5_c_re', 's5_c_im', 's5_d',
           's5_log_step', 's5_w_glu', 'sgu_ln_w', 'sgu_ln_b', 'sgu_w', 'sgu_b', 'm2_conv_w', 'm2_conv_b', 'm2_dt_bias',
           'm2_a_log', 'm2_d', 'm2_norm_w', 'sc_conv_w', 'merge_b', 'w_branch', 'w_out', 'final_norm_w']
BIG_SHARDED = ['w_in', 'w_branch', 'w_out', 's5_w_glu']
SMALL_SHARDED = ['m2_conv_w', 'sc_conv_w', 'merge_b']
REPLICATED = [n for n in WEIGHTS if n not in BIG_SHARDED + SMALL_SHARDED]
S5_NAMES = ['s5_lambda_re', 's5_lambda_im', 's5_b_re', 's5_b_im', 's5_c_re', 's5_c_im', 's5_d', 's5_log_step']


def _sc_interleave(t):
    lead = t.shape[:-1]
    return jnp.swapaxes(t.reshape(lead + (4, 4, LANES)), -3, -2).reshape(lead + (4 * BW,))


def _pad_in(w):
    z = lambda n: jnp.zeros(w.shape[:-1] + (n,), w.dtype)
    return jnp.concatenate([w[..., 6152:], w[..., 0:1024], w[..., 3072:4096], w[..., 1024:2560], z(512),
                            w[..., 2560:3072], w[..., 4096:4104], z(504), _sc_interleave(w[..., 4104:6152])], axis=-1)


def _unpad_in(g):
    return jnp.concatenate([g[..., C_S5U:C_S5U + 1024], g[..., C_SGU_U:C_SGU_U + 1536], g[..., C_M2Z:C_M2Z + 512],
                            g[..., C_M2X:C_M2X + 1024], g[..., C_DT:C_DT + 8], _sc_interleave(g[..., C_SC:]),
                            g[..., :N_BRANCH * D_MODEL]], axis=-1)


ROW_BLOCK = 8 * LANES


def _pack_rows(tensors, row_mult, batched=False):
    parts = []
    for t in tensors:
        f = t.reshape((t.shape[0], -1) if batched else (1, -1))
        f = jnp.pad(f, ((0, 0), (0, (-f.shape[1]) % ROW_BLOCK)))
        parts.append(f.reshape(f.shape[0], -1, LANES))
    out = jnp.concatenate(parts, axis=1)
    out = jnp.pad(out, ((0, 0), (0, (-out.shape[1]) % row_mult), (0, 0)))
    return out if batched else out[0]


def _unpack_rows(rows, shapes):
    out, r0 = [], 0
    for shp in shapes:
        size = 1
        for s in shp:
            size *= s
        nr = -(-size // ROW_BLOCK) * 8
        out.append(rows[r0:r0 + nr].reshape(-1)[:size].reshape(shp))
        r0 += nr
    return out


def _kernel_col_map():
    m = np.full(IN_PAD, -1, np.int64)
    m[C_MERGE:C_MERGE + 4096] = np.arange(6152, 10248)
    m[C_S5U:C_S5U + 1024] = np.arange(0, 1024)
    m[C_M2X:C_M2X + 1024] = np.arange(3072, 4096)
    m[C_SGU_U:C_SGU_U + 1536] = np.arange(1024, 2560)
    m[C_M2Z:C_M2Z + 512] = np.arange(2560, 3072)
    m[C_DT:C_DT + 8] = np.arange(4096, 4104)
    for j in range(4):
        for kind in range(4):
            k0 = C_SC + 4 * LANES * j + LANES * kind
            m[k0:k0 + LANES] = 4104 + BW * kind + LANES * j + np.arange(LANES)
    return m


def _lane_pieces(sources):
    pieces, cur = [], None
    for lane, src in enumerate(sources):
        key = None if src is None else (src[0], src[1] // LANES, (lane - src[1]) % LANES)
        if cur is not None and key == cur[0]:
            cur[2] = lane + 1
        else:
            if cur is not None and cur[0] is not None:
                pieces.append((*cur[0], cur[1], cur[2]))
            cur = [key, lane, lane + 1]
    if cur is not None and cur[0] is not None:
        pieces.append((*cur[0], cur[1], cur[2]))
    return pieces


def _assemble_block(pieces, load, rows, dtype):
    lane = lax.broadcasted_iota(jnp.int32, (rows, LANES), 1)
    out = None
    for arr, sb, shift, lo, hi in pieces:
        v = load(arr, sb)
        if shift:
            v = pltpu.roll(v, shift, 1)
        if out is None and lo == 0 and hi == LANES:
            out = v
        else:
            out = jnp.where((lane >= lo) & (lane < hi), v, jnp.zeros((rows, LANES), dtype) if out is None else out)
    return jnp.zeros((rows, LANES), dtype) if out is None else out


RELAYOUT_ROWS = 512
SHARD_BLOCKS = -(-SHARD_IN // LANES)


def _load_shard_block(ref, rows):
    def load(j, sb):
        if sb == SHARD_BLOCKS - 1:
            return jnp.broadcast_to(ref[j, :, SHARD_IN - 1:SHARD_IN], (rows, LANES))
        return ref[j, :, sb * LANES:(sb + 1) * LANES]
    return load


def _relayout_w_in(gathered, name):
    kmap = _kernel_col_map()
    dtype = gathered.dtype

    def body(src_ref, o_ref):
        load = _load_shard_block(src_ref, RELAYOUT_ROWS)
        for ob in range(IN_PAD // LANES):
            srcs = [None if kmap[ob * LANES + l] < 0 else (int(kmap[ob * LANES + l]) // SHARD_IN, int(kmap[ob * LANES + l]) % SHARD_IN)
                    for l in range(LANES)]
            o_ref[:, ob * LANES:(ob + 1) * LANES] = _assemble_block(_lane_pieces(srcs), load, RELAYOUT_ROWS, dtype)

    return pl.pallas_call(
        body, name=name, grid=(D_MODEL // RELAYOUT_ROWS,),
        in_specs=[pl.BlockSpec((N_DEV, RELAYOUT_ROWS, SHARD_IN), lambda i: (0, i, 0))],
        out_specs=pl.BlockSpec((RELAYOUT_ROWS, IN_PAD), lambda i: (i, 0)),
        out_shape=jax.ShapeDtypeStruct((D_MODEL, IN_PAD), dtype),
        compiler_params=_cparams(("parallel",)),
    )(gathered)


def _relayout_g_in(gw, name):
    kmap = _kernel_col_map()
    kinv = np.zeros(IN_DIM, np.int64)
    kinv[kmap[kmap >= 0]] = np.nonzero(kmap >= 0)[0]
    dtype = gw.dtype

    def body(src_ref, o_ref):
        load = lambda _, sb: src_ref[:, sb * LANES:(sb + 1) * LANES]
        for j in range(N_DEV):
            for ob in range(SHARD_BLOCKS):
                srcs = [(0, int(kinv[SHARD_IN * j + ob * LANES + l])) if ob * LANES + l < SHARD_IN else None for l in range(LANES)]
                blk = _assemble_block(_lane_pieces(srcs), load, RELAYOUT_ROWS, dtype)
                if ob == SHARD_BLOCKS - 1:
                    o_ref[j, :, SHARD_IN - 1:SHARD_IN] = blk[:, 0:1]
                else:
                    o_ref[j, :, ob * LANES:(ob + 1) * LANES] = blk

    return pl.pallas_call(
        body, name=name, grid=(D_MODEL // RELAYOUT_ROWS,),
        in_specs=[pl.BlockSpec((RELAYOUT_ROWS, IN_PAD), lambda i: (i, 0))],
        out_specs=pl.BlockSpec((N_DEV, RELAYOUT_ROWS, SHARD_IN), lambda i: (0, i, 0)),
        out_shape=jax.ShapeDtypeStruct((N_DEV, D_MODEL, SHARD_IN), dtype),
        compiler_params=_cparams(("parallel",)),
    )(gw)


def _rows128(flat, row_mult=8):
    n = flat.shape[0]
    per = LANES * row_mult
    total = -(-n // per) * per
    return jnp.pad(flat, (0, total - n)).reshape(total // LANES, LANES)


def _pad_lanes(v):
    return jnp.pad(v, (0, LANES - v.shape[0])).reshape(1, LANES)


def _layer_prep(i, p):
    disc, disc_vjp = jax.vjp(_s5_disc, *[p[n][i] for n in S5_NAMES])
    prep = dict(
        nw=p['norm_w'][i].reshape(1, D_MODEL), disc_vjp=disc_vjp,
        s5small=[_block_diag(t).astype(BF16) for t in disc[:4]] + [disc[4], disc[5]],
        sgw=[p['sgu_ln_w'][i].reshape(1, BW), p['sgu_ln_b'][i].reshape(1, BW), p['sgu_w'][i],
             jnp.repeat(p['sgu_b'][i].T, BW // SGU_HEADS, axis=1)],
        cb=p['m2_conv_b'][i].reshape(1, M2_CONV_CH),
        m2w=[_pad_lanes(p['m2_dt_bias'][i]), _pad_lanes(p['m2_a_log'][i]),
             jnp.repeat(p['m2_d'][i], M2_HEAD_DIM).reshape(1, BW), p['m2_norm_w'][i].reshape(1, BW)])
    touch = [t[0, 0].astype(F32) for t in prep['s5small']] + [prep['sgw'][3][0, 0], prep['m2w'][2][0, 0]]
    return prep, sum(touch[1:], touch[0])


def _layer_fwd(x, h, i, prep, w_in, other_weights, before_merge=None):
    proj = _matmul(h, w_in, 1, 0, F32, 2048, 1024, 1024, f"proj{i}")
    full = dict(other_weights(proj), w_in=w_in)
    s5w = prep['s5small'] + [full['s5_w_glu']]
    ys, sre, sim = _s5_fwd(proj, *s5w, f"s5_fwd{i}")
    ys = _sgu_fwd(proj, ys, *prep['sgw'], f"sgu_fwd{i}")
    cw = full['m2_conv_w']
    xa = _m2_conv_fwd(proj, cw, prep['cb'], f"m2conv_fwd{i}")
    ys, s_in = _ssd_fwd(proj, ys, xa, *prep['m2w'], f"ssd_fwd{i}")
    scw = full['sc_conv_w']
    ys = _sc_fwd(proj, ys, scw, f"sc_fwd{i}")
    mb = full['merge_b'].reshape(N_BRANCH, 1, D_MODEL)
    if before_merge is not None:
        mb = mb + before_merge(ys)[0, 0]
    merged = _merge_fwd(proj, ys, mb, full['w_branch'], f"merge_fwd{i}")
    x_new = _matmul(merged, full['w_out'], 1, 0, F32, 1024, 1024, 1024, f"out{i}", residual=x)
    saved = dict(x=x, nw=prep['nw'], h=h, proj=proj, disc_vjp=prep['disc_vjp'], s5w=s5w, sre=sre, sim=sim, sgw=prep['sgw'],
                 cw=cw, cb=prep['cb'], xa=xa, m2w=prep['m2w'], s_in=s_in, scw=scw, ys=ys, mb=mb, merged=merged)
    return x_new, saved, full


def _layer_bwd(dx_out, i, sv, full, on_large_grads=None, after_dh=None):
    g = {}
    proj = sv['proj']
    dm = _matmul(dx_out, full['w_out'], 1, 1, F32, 1024, 1024, 1024, f"dmerged{i}")
    g['w_out'] = _matmul(sv['merged'], dx_out, 0, 0, BF16, 1024, 1024, 1024, f"gw_out{i}")
    dys, dproj, g['w_branch'], dmb = _merge_bwd(proj, sv['ys'], dm, sv['mb'], full['w_branch'], f"merge_bwd{i}")
    g['merge_b'] = dmb.reshape(N_BRANCH, D_MODEL)
    dproj, dbbre, dbbim, dcre, dcim, da, dd, dwg = _s5_bwd(proj, dproj, dys, sv['sre'], sv['sim'], *sv['s5w'], f"s5_bwd{i}")
    g['s5_dense'] = (dbbre, dbbim, dcre, dcim, da, dd)
    g['s5_w_glu'] = dwg.astype(BF16)
    dproj, dlw, dlb, g['sgu_w'], dbias = _sgu_bwd(proj, dproj, dys, *sv['sgw'], f"sgu_bwd{i}")
    g['sgu_ln_w'], g['sgu_ln_b'] = dlw[0], dlb[0]
    g['sgu_b'] = dbias.reshape(SGU_CHUNK, SGU_HEADS, BW // SGU_HEADS).sum(-1).T
    dproj, dxa, ddtb, dal, ddf, dnw = _ssd_bwd(proj, dproj, sv['xa'], dys, sv['s_in'], *sv['m2w'], f"ssd_bwd{i}")
    dproj, g['m2_conv_w'], dcb = _m2_conv_bwd(proj, dproj, dxa, sv['cw'], sv['cb'], f"m2conv_bwd{i}")
    g['m2_conv_b'], g['m2_norm_w'] = dcb[0], dnw[0]
    g['m2_dt_bias'], g['m2_a_log'] = ddtb[0, :M2_HEADS], dal[0, :M2_HEADS]
    g['m2_d'] = ddf.reshape(M2_HEADS, M2_HEAD_DIM).sum(-1)
    dproj, g['sc_conv_w'] = _sc_bwd(proj, dproj, dys, sv['scw'], f"sc_bwd{i}")
    g['w_in'] = _matmul(sv['h'], dproj, 0, 0, BF16, 1024, 1024, 2048, f"gw_in{i}")
    tok = on_large_grads(g) if on_large_grads else None
    dh = _matmul(dproj, full['w_in'], 1, 1, F32, 1024, 1024, IN_PAD // 4, f"dh{i}", after=tok)
    nw = sv['nw'] if after_dh is None else sv['nw'] + after_dh(dh)[0, 0]
    dx_in, dnw_l = _rmsnorm_bwd(sv['x'], nw, dh, dx_out, f"rms_bwd{i}")
    g['norm_w'] = dnw_l[0]
    return dx_in, g


def _split8(t, axis):
    shp = t.shape
    t = t.reshape(shp[:axis] + (N_DEV, shp[axis] // N_DEV) + shp[axis + 1:])
    return jnp.moveaxis(t, axis, 0)


def _join8(t, axis):
    t = jnp.moveaxis(t, 0, axis)
    shp = t.shape
    return t.reshape(shp[:axis] + (shp[axis] * shp[axis + 1],) + shp[axis + 2:])


SHARD_AXIS = {'w_in': 2, 'w_branch': 3, 'w_out': 1, 's5_w_glu': 1, 'm2_conv_w': 2, 'sc_conv_w': 2, 'merge_b': 2}


OTHER_BIG = [n for n in BIG_SHARDED if n != 'w_in']


def _other_weights(gathered):
    return {n: _join8(t, SHARD_AXIS[n] - 1) for n, t in zip(OTHER_BIG, gathered)}


def _layer_grad_blocks(g, i):
    blocks = [_relayout_g_in(g[n], f"relayout_g_in{i}") if n == 'w_in' else _split8(g[n], SHARD_AXIS[n] - 1) for n in BIG_SHARDED]
    return [b.reshape(N_DEV, -1, b.shape[-1]) for b in blocks]


def _pair_start(blocks, tag):
    shapes = [(N_CHIP,) + b.shape[1:] for b in blocks]
    return _split_start(_plan_pair, blocks, shapes, N_CHIP * len(blocks), f"pair{tag}_start")


def _pair_sums(state, after, tag):
    mine, theirs = _split_wait(_plan_pair, state, after, f"pair{tag}_wait", with_sources=True)
    return [_pair_sum(b, t, f"pair_sum{tag}_{k}") for k, (b, t) in enumerate(zip(mine, theirs))]


def _chips_start(sums, tag, after=None):
    return _split_start(_plan_chips, sums, [s.shape for s in sums], 3 * len(sums), f"chips{tag}_start", after)


def kernel(x, norm_w, w_in, s5_lambda_re, s5_lambda_im, s5_b_re, s5_b_im, s5_c_re, s5_c_im, s5_d, s5_log_step, s5_w_glu, sgu_ln_w, sgu_ln_b, sgu_w, sgu_b, m2_conv_w, m2_conv_b, m2_dt_bias, m2_a_log, m2_d, m2_norm_w, sc_conv_w, merge_b, w_branch, w_out, final_norm_w, loss_target, m_norm_w, m_w_in, m_s5_lambda_re, m_s5_lambda_im, m_s5_b_re, m_s5_b_im, m_s5_c_re, m_s5_c_im, m_s5_d, m_s5_log_step, m_s5_w_glu, m_sgu_ln_w, m_sgu_ln_b, m_sgu_w, m_sgu_b, m_m2_conv_w, m_m2_conv_b, m_m2_dt_bias, m_m2_a_log, m_m2_d, m_m2_norm_w, m_sc_conv_w, m_merge_b, m_w_branch, m_w_out, m_final_norm_w, v_norm_w, v_w_in, v_s5_lambda_re, v_s5_lambda_im, v_s5_b_re, v_s5_b_im, v_s5_c_re, v_s5_c_im, v_s5_d, v_s5_log_step, v_s5_w_glu, v_sgu_ln_w, v_sgu_ln_b, v_sgu_w, v_sgu_b, v_m2_conv_w, v_m2_conv_b, v_m2_dt_bias, v_m2_a_log, v_m2_d, v_m2_norm_w, v_sc_conv_w, v_merge_b, v_w_branch, v_w_out, v_final_norm_w):
    loc = locals()
    p = {n: loc[n] for n in WEIGHTS}
    mom = {n: loc['m_' + n] for n in WEIGHTS}
    vel = {n: loc['v_' + n] for n in WEIGHTS}

    small_sizes = [p[n].size for n in SMALL_SHARDED]
    small_pack = _rows128(jnp.concatenate([p[n].reshape(-1) for n in SMALL_SHARDED]))
    first = [p['w_in'][0].astype(BF16)]
    gath_first, tok = _split_start(_plan_gather, first, [(N_DEV,) + first[0].shape], 7, "gather_w_in0_start")
    shards = ([(p[n][0] + tok[0, 0]).astype(BF16) for n in OTHER_BIG] + [small_pack + tok[0, 0]]
              + [(p[n][1] + tok[0, 0]).astype(BF16) for n in BIG_SHARDED])
    gath, tok = _split_start(_plan_gather, shards, [(N_DEV,) + t.shape for t in shards], 7 * len(shards), "gather_start")

    def relayed(lo, hi, after, name, started=None):
        started = gath if started is None else started
        n = (len(started) - 3) // 2
        sems, srcs, lands = started[:3], started[3:3 + n], started[3 + n:]
        plan = functools.partial(_plan_gather, first=lo)
        state, tok = _split_relay(plan, (*sems, *srcs[lo:hi], *lands[lo:hi]), after, name + "_relay")
        return (plan, state, name), tok

    def arrived(relay, after):
        plan, state, name = relay
        return _split_wait(plan, state, after, name + "_wait")

    def gathered(lo, hi, after, name, started=None):
        relay, tok = relayed(lo, hi, after, name, started)
        return arrived(relay, tok)

    later = dict(p, **{n: p[n] + tok[0, 0] for n in ('norm_w', 's5_log_step', 'sgu_b', 'm2_d')})
    preps = [_layer_prep(i, later) for i in range(DEPTH)]
    h0 = _rmsnorm_fwd(x[0], preps[0][0]['nw'], "rms_fwd0")
    got = gathered(0, 1, tok + (preps[0][1] + preps[1][1] + h0[0, 0].astype(F32)), "gather_w_in0", gath_first)
    small_full = {}

    def other_weights0(proj):
        got = gathered(0, 4, proj, "gather_rest0")
        small_all, off = got[-1].reshape(N_DEV, -1), 0
        for n, sz in zip(SMALL_SHARDED, small_sizes):
            small_full[n] = _join8(small_all[:, off:off + sz].reshape((N_DEV,) + p[n].shape), SHARD_AXIS[n])
            off += sz
        return dict(_other_weights(got[:-1]), **{n: small_full[n][0] for n in SMALL_SHARDED})

    saved, layer_g, full = [None] * DEPTH, [None] * DEPTH, [None] * DEPTH
    relay1 = []

    def relay_layer1(ys):
        relay, tok = relayed(4, 8, ys, "gather1")
        relay1.append(relay)
        return tok

    xs, saved[0], full[0] = _layer_fwd(x[0], h0, 0, preps[0][0], _relayout_w_in(got[0], "relayout_w_in0"), other_weights0,
                                       relay_layer1)
    h1 = _rmsnorm_fwd(xs, preps[1][0]['nw'], "rms_fwd1")
    got = arrived(relay1[0], h1)
    xs, saved[1], full[1] = _layer_fwd(
        xs, h1, 1, preps[1][0], _relayout_w_in(got[0], "relayout_w_in1"),
        lambda proj: dict(_other_weights(got[1:]), **{n: small_full[n][1] for n in SMALL_SHARDED}))
    loss_row, dx, dfw = _loss_head(xs, final_norm_w.reshape(1, D_MODEL), loss_target[0])
    loss = lax.psum(loss_row[0, 0], ("x", "y", "c"))
    loss, dx = lax.optimization_barrier((loss, dx))
    pairs, scat, sent0 = [None] * DEPTH, [None] * DEPTH, []

    def start_pairs1(g):
        pairs[1], tok = _pair_start(_layer_grad_blocks(g, 1), 1)
        return tok

    def send_chip_sums1(dh):
        scat[1], tok = _chips_start(_pair_sums(pairs[1], dh, 1), 1)
        return tok

    def send_all0(g):
        pairs[0], tok = _pair_start(_layer_grad_blocks(g, 0), 0)
        scat[0], tok = _chips_start(_pair_sums(pairs[0], tok, 0), 0)
        sent0.append(tok)
        return tok

    dx, layer_g[1] = _layer_bwd(dx, 1, saved[1], full[1], on_large_grads=start_pairs1, after_dh=send_chip_sums1)
    dx, layer_g[0] = _layer_bwd(dx, 0, saved[0], full[0], on_large_grads=send_all0)
    for i in range(DEPTH):
        dense = layer_g[i].pop('s5_dense')
        blocks = tuple(_diag_blocks(t, after=sent0[0]) for t in dense[:4])
        layer_g[i].update(zip(S5_NAMES, saved[i]['disc_vjp'](blocks + (dense[4] + sent0[0][0, 0], dense[5]))))
    grads = {n: jnp.stack([layer_g[i][n] for i in range(DEPTH)]) for n in SMALL_SHARDED + REPLICATED if n != 'final_norm_w'}
    grads['final_norm_w'] = dfw[0]

    out_g, out_d, out_m, out_v = {}, {}, {}, {}
    repl_rows = _pack_rows([grads[n] for n in REPLICATED], 8 * N_DEV)
    rr = repl_rows.shape[0] // N_DEV
    shard_rows = _pack_rows([_split8(grads[n], SHARD_AXIS[n]) for n in SMALL_SHARDED], 8, batched=True)
    rs = shard_rows.shape[1]
    small_g = jnp.concatenate([shard_rows, repl_rows.reshape(N_DEV, rr, LANES)], axis=1)
    all_to_all, all_gather = functools.partial(_plan_direct, gather=False), functools.partial(_plan_direct, gather=True)
    small_state, tok = _split_start(all_to_all, [small_g], [small_g.shape], N_DEV - 1, "scatter_small_start")
    landed1 = _split_wait(_plan_chips, scat[1], tok, "chips1_wait")
    landed0 = _split_wait(_plan_chips, scat[0], landed1[0], "chips0_wait")

    def big_adamw(n, after=None):
        k, shp = BIG_SHARDED.index(n), p[n].shape
        if n == 'w_in':
            return _adamw_w_in([landed0[k], landed1[k]], p[n], mom[n], vel[n], "adamw_w_in", after)
        c = shp[-1]
        r = p[n].size // (DEPTH * c)
        res = _adamw([landed0[k], landed1[k]], *[d[n].reshape(DEPTH, r, c) for d in (p, mom, vel)],
                     {'w_branch': 512, 'w_out': 128, 's5_w_glu': 64}[n], "adamw_" + n)
        return [o.reshape(shp) for o in res]

    for n in OTHER_BIG:
        out_g[n], out_d[n], out_m[n], out_v[n] = big_adamw(n)
    updated = sum(out_d[n].reshape(-1)[0] for n in OTHER_BIG).reshape(1, 1)
    small_sum = _slot_sum(_split_wait(all_to_all, small_state, updated, "scatter_small_wait")[0], "sum_small")
    repl_part = small_sum[rs:]
    repl_state, tok = _split_start(all_gather, [repl_part], [(N_DEV,) + repl_part.shape], N_DEV - 1, "gather_small_start")
    out_g['w_in'], out_d['w_in'], out_m['w_in'], out_v['w_in'] = big_adamw('w_in', tok)
    repl_all = _split_wait(all_gather, repl_state, out_d['w_in'], "gather_small_wait")[0].reshape(N_DEV * rr, LANES)
    g_all = jnp.concatenate([small_sum[:rs], repl_all], axis=0)
    names = SMALL_SHARDED + REPLICATED
    pieces = (_unpack_rows(g_all[:rs], [p[n].shape for n in SMALL_SHARDED])
              + _unpack_rows(g_all[rs:], [p[n].shape for n in REPLICATED]))
    out_g.update(zip(names, pieces))
    res = _adamw_many(*[[_memory_view(n, d[n]) for n in names] for d in (out_g, p, mom, vel)], "adamw_small")
    for r, dst in zip(res, (out_d, out_m, out_v)):
        dst.update({n: _memory_view(n, t) for n, t in zip(names, r)})
    return (loss, dx[None], *[out_g[n] for n in WEIGHTS], *[out_d[n] for n in WEIGHTS],
            *[out_m[n] for n in WEIGHTS], *[out_v[n] for n in WEIGHTS])
```

```python
import functools

import jax
import jax.numpy as jnp
import numpy as np
from jax import lax
from jax.experimental import pallas as pl
from jax.experimental.pallas import tpu as pltpu

F32 = jnp.float32
BF16 = jnp.bfloat16

N_DEV = 8
SEQ = 2048
D_MODEL = 1024
DEPTH = 2
BW = 512
N_BRANCH = 4
EPS = 1e-6
S5_GROUPS, S5_STATE, S5_P = 32, 64, 16
S5_CH = S5_GROUPS * S5_STATE
SGU_CHUNK, SGU_HEADS = 128, 8
M2_HEADS, M2_HEAD_DIM, M2_STATE, M2_CHUNK, M2_CONV = 8, 64, 128, 128, 4
M2_CONV_CH = 1024
SC_CONV = 3
IN_DIM = 10248
IN_PAD = 11264
C_MERGE = 0
C_S5U, C_S5G = 4096, 4608
C_M2X = 5120
C_SGU_U, C_SGU_V, C_SGU_G = 6144, 6656, 7168
C_M2Z, C_DT = 8192, 8704
C_SC = 9216
SHARD_IN = IN_DIM // N_DEV

ADAM_LR, ADAM_B1, ADAM_B2, ADAM_EPS, ADAM_WD, ADAM_STEP = 0.001, 0.9, 0.999, 1e-08, 0.01, 10

VMEM_LIMIT = 56 * 1024 * 1024
LANES = 128

MESH = pl.DeviceIdType.MESH


def _cparams(sem=None, **kw):
    return pltpu.CompilerParams(dimension_semantics=sem, vmem_limit_bytes=VMEM_LIMIT, **kw)


def _dg(a, b, ca, cb, precision=None):
    return lax.dot_general(a, b, (((ca,), (cb,)), ((), ())), precision=precision,
                           preferred_element_type=F32)


@functools.partial(jax.custom_vjp, nondiff_argnums=(2, 3))
def _bdot(a, b, ca, cb):
    return _dg(a.astype(BF16), b.astype(BF16), ca, cb)


def _bdot_fwd(a, b, ca, cb):
    return _bdot(a, b, ca, cb), (a, b)


def _bdot_bwd(ca, cb, res, g):
    a, b = res
    gb, ab, bb = g.astype(BF16), a.astype(BF16), b.astype(BF16)
    da = _dg(gb, bb, 1, 1 - cb) if ca == 1 else _dg(bb, gb, 1 - cb, 1)
    db = _dg(ab, gb, 1 - ca, 0) if cb == 0 else _dg(gb, ab, 0, 1 - ca)
    return da.astype(a.dtype), db.astype(b.dtype)


_bdot.defvjp(_bdot_fwd, _bdot_bwd)


def _rms(x, w):
    return x * lax.rsqrt(jnp.mean(x * x, axis=-1, keepdims=True) + EPS) * w


def _silu(x):
    return x * jax.nn.sigmoid(x)


def _gelu(x):
    return 0.5 * x * (1.0 + jnp.tanh(0.7978845608028654 * (x + 0.044715 * (x * x * x))))


def _softplus(x):
    return jnp.maximum(x, 0.0) + jnp.log1p(jnp.exp(-jnp.abs(x)))


def _shift_down(x, s):
    if s == 0:
        return x
    row = lax.broadcasted_iota(jnp.int32, x.shape, 0)
    return jnp.where(row >= s, pltpu.roll(x, s, 0), 0.0)


def _shift_up(x, s):
    if s == 0:
        return x
    n = x.shape[0]
    row = lax.broadcasted_iota(jnp.int32, x.shape, 0)
    return jnp.where(row < n - s, pltpu.roll(x, n - s, 0), 0.0)


def _matmul(a, b, ca, cb, out_dtype, tm, tn, tk, name, residual=None, after=None):
    m = a.shape[1 - ca]
    k = a.shape[ca]
    n = b.shape[1 - cb]
    assert b.shape[cb] == k and m % tm == 0 and n % tn == 0 and k % tk == 0
    nk = k // tk
    a_spec = pl.BlockSpec((tm, tk), lambda i, j, kk: (i, kk)) if ca == 1 else pl.BlockSpec((tk, tm), lambda i, j, kk: (kk, i))
    b_spec = pl.BlockSpec((tk, tn), lambda i, j, kk: (kk, j)) if cb == 0 else pl.BlockSpec((tn, tk), lambda i, j, kk: (j, kk))
    o_spec = pl.BlockSpec((tm, tn), lambda i, j, kk: (i, j))
    has_res = residual is not None

    def body(*refs):
        refs = refs[:2 + has_res] + refs[2 + has_res + (after is not None):]
        if has_res:
            a_ref, b_ref, r_ref, o_ref, acc = refs
        else:
            a_ref, b_ref, o_ref, acc = refs
        kk = pl.program_id(2)
        part = _dg(a_ref[...].astype(BF16), b_ref[...].astype(BF16), ca, cb)
        if nk == 1:
            o_ref[...] = (part + r_ref[...] if has_res else part).astype(out_dtype)
            return

        @pl.when(kk == 0)
        def _():
            acc[...] = part

        @pl.when(kk > 0)
        def _():
            acc[...] += part

        @pl.when(kk == nk - 1)
        def _():
            r = acc[...]
            if has_res:
                r = r + r_ref[...]
            o_ref[...] = r.astype(out_dtype)

    ins = [a, b] + ([residual] if has_res else []) + ([after] if after is not None else [])
    specs = [a_spec, b_spec] + ([o_spec] if has_res else []) + ([pl.BlockSpec(memory_space=pl.ANY)] if after is not None else [])
    return pl.pallas_call(
        body, name=name, grid=(m // tm, n // tn, nk), in_specs=specs, out_specs=o_spec,
        out_shape=jax.ShapeDtypeStruct((m, n), out_dtype),
        scratch_shapes=[pltpu.VMEM((tm, tn) if nk > 1 else (8, LANES), F32)],
        compiler_params=_cparams(("parallel", "parallel", "arbitrary")),
    )(*ins)


ROW_TILE = 512


def _rmsnorm_fwd(x, w, name):
    def body(x_ref, w_ref, o_ref):
        o_ref[...] = _rms(x_ref[...], w_ref[...]).astype(BF16)

    return pl.pallas_call(
        body, name=name, grid=(SEQ // ROW_TILE,),
        in_specs=[pl.BlockSpec((ROW_TILE, D_MODEL), lambda i: (i, 0)), pl.BlockSpec((1, D_MODEL), lambda i: (0, 0))],
        out_specs=pl.BlockSpec((ROW_TILE, D_MODEL), lambda i: (i, 0)),
        out_shape=jax.ShapeDtypeStruct((SEQ, D_MODEL), BF16),
        compiler_params=_cparams(("parallel",)),
    )(x, w)


def _rmsnorm_bwd(x, w, dh, dres, name):
    def body(x_ref, w_ref, dh_ref, dres_ref, dx_ref, dw_ref):
        _, vjp = jax.vjp(_rms, x_ref[...], w_ref[...])
        dx, dw = vjp(dh_ref[...])
        dx_ref[...] = dx + dres_ref[...]

        @pl.when(pl.program_id(0) == 0)
        def _():
            dw_ref[...] = dw

        @pl.when(pl.program_id(0) > 0)
        def _():
            dw_ref[...] += dw

    tile = pl.BlockSpec((ROW_TILE, D_MODEL), lambda i: (i, 0))
    vec = pl.BlockSpec((1, D_MODEL), lambda i: (0, 0))
    return pl.pallas_call(
        body, name=name, grid=(SEQ // ROW_TILE,),
        in_specs=[tile, vec, tile, tile], out_specs=[tile, vec],
        out_shape=[jax.ShapeDtypeStruct((SEQ, D_MODEL), F32), jax.ShapeDtypeStruct((1, D_MODEL), F32)],
        compiler_params=_cparams(("arbitrary",)),
    )(x, w, dh, dres)


def _loss_head(x, w, target):
    def body(x_ref, w_ref, t_ref, loss_ref, dx_ref, dw_ref):
        tgt = t_ref[...]

        def f(xv, wv):
            err = _rms(xv, wv) - tgt
            return 0.5 * jnp.sum(jnp.mean(err * err, axis=-1))

        loss, vjp = jax.vjp(f, x_ref[...], w_ref[...])
        dx, dw = vjp(jnp.ones((), F32))
        dx_ref[...] = dx
        lrow = jnp.full((1, LANES), loss, F32)

        @pl.when(pl.program_id(0) == 0)
        def _():
            dw_ref[...] = dw
            loss_ref[...] = lrow

        @pl.when(pl.program_id(0) > 0)
        def _():
            dw_ref[...] += dw
            loss_ref[...] += lrow

    tile = pl.BlockSpec((ROW_TILE, D_MODEL), lambda i: (i, 0))
    vec = pl.BlockSpec((1, D_MODEL), lambda i: (0, 0))
    return pl.pallas_call(
        body, name="loss_head", grid=(SEQ // ROW_TILE,),
        in_specs=[tile, vec, tile], out_specs=[pl.BlockSpec((1, LANES), lambda i: (0, 0)), tile, vec],
        out_shape=[jax.ShapeDtypeStruct((1, LANES), F32), jax.ShapeDtypeStruct((SEQ, D_MODEL), F32),
                   jax.ShapeDtypeStruct((1, D_MODEL), F32)],
        compiler_params=_cparams(("arbitrary",)),
    )(x, w, target)


S5_T = 256
S5_BLOCKS = [(slice(j * 256, (j + 1) * 256), slice(j * 1024, (j + 1) * 1024)) for j in range(2)]


def _s5_post(ypre, gate, wglu):
    y = _gelu(ypre)
    y = y * jax.nn.sigmoid(_bdot(y, wglu, 1, 0))
    return y * _silu(gate)


def _s5_fwd(proj, bbre, bbim, cre, cim, a2, dvec, wglu, name):
    def body(u_ref, g_ref, bbre_ref, bbim_ref, cre_ref, cim_ref, a_ref, d_ref, wg_ref, o_ref, sre_ref, sim_ref, st):
        @pl.when(pl.program_id(0) == 0)
        def _():
            st[...] = jnp.zeros_like(st)

        u = u_ref[...]
        ub = u.astype(BF16)
        for us, ss in S5_BLOCKS:
            sre_ref[:, ss] = _dg(ub[:, us], bbre_ref[us, ss], 1, 0)
            sim_ref[:, ss] = _dg(ub[:, us], bbim_ref[us, ss], 1, 0)
        ar, ai = a_ref[0:1, :], a_ref[1:2, :]

        def step(t, carry):
            sr, si = carry
            nr = ar * sr - ai * si + sre_ref[pl.ds(t, 1), :]
            ni = ar * si + ai * sr + sim_ref[pl.ds(t, 1), :]
            sre_ref[pl.ds(t, 1), :] = nr
            sim_ref[pl.ds(t, 1), :] = ni
            return nr, ni

        sr, si = lax.fori_loop(0, S5_T, step, (st[0:1, :], st[1:2, :]), unroll=8)
        st[0:1, :] = sr
        st[1:2, :] = si
        ypre = jnp.concatenate(
            [_dg(sre_ref[:, ss].astype(BF16), cre_ref[ss, us], 1, 0) - _dg(sim_ref[:, ss].astype(BF16), cim_ref[ss, us], 1, 0)
             for us, ss in S5_BLOCKS], axis=1) + d_ref[...] * u
        o_ref[0] = _s5_post(ypre, g_ref[...], wg_ref[...]).astype(BF16)

    full = lambda shape: pl.BlockSpec(shape, lambda c: (0, 0))
    return pl.pallas_call(
        body, name=name, grid=(SEQ // S5_T,),
        in_specs=[pl.BlockSpec((S5_T, BW), lambda c: (c, C_S5U // BW)), pl.BlockSpec((S5_T, BW), lambda c: (c, C_S5G // BW)),
                  full((BW, S5_CH)), full((BW, S5_CH)), full((S5_CH, BW)), full((S5_CH, BW)),
                  full((2, S5_CH)), full((1, BW)), full((BW, BW))],
        out_specs=[pl.BlockSpec((1, S5_T, BW), lambda c: (0, c, 0)), pl.BlockSpec((S5_T, S5_CH), lambda c: (c, 0)),
                   pl.BlockSpec((S5_T, S5_CH), lambda c: (c, 0))],
        out_shape=[jax.ShapeDtypeStruct((N_BRANCH, SEQ, BW), BF16), jax.ShapeDtypeStruct((SEQ, S5_CH), F32),
                   jax.ShapeDtypeStruct((SEQ, S5_CH), F32)],
        scratch_shapes=[pltpu.VMEM((2, S5_CH), F32)],
        compiler_params=_cparams(("arbitrary",)),
    )(proj, proj, bbre, bbim, cre, cim, a2, dvec, wglu)


def _s5_bwd(proj, dproj, dout, sre, sim, bbre, bbim, cre, cim, a2, dvec, wglu, name):
    nc = SEQ // S5_T

    def body(u_ref, g_ref, do_ref, sre_ref, sim_ref, pre_ref, pim_ref, bbre_ref, bbim_ref, cre_ref, cim_ref, a_ref,
             d_ref, wg_ref, dproj_in, dp_ref, dbbre_ref, dbbim_ref, dcre_ref, dcim_ref, da_ref, dd_ref, dwg_ref,
             gre, gim, st):
        c = nc - 1 - pl.program_id(0)

        @pl.when(pl.program_id(0) == 0)
        def _():
            st[...] = jnp.zeros_like(st)
            for r in (dbbre_ref, dbbim_ref, dcre_ref, dcim_ref, da_ref, dd_ref, dwg_ref):
                r[...] = jnp.zeros_like(r)

        u = u_ref[...]
        s_re, s_im = sre_ref[...], sim_ref[...]

        def head(s_res, s_ims, cres, cims, dv, uv, gv, wg):
            ypre = jnp.concatenate([_bdot(sr, cr, 1, 0) - _bdot(si, ci, 1, 0)
                                    for sr, si, cr, ci in zip(s_res, s_ims, cres, cims)], axis=1) + dv * uv
            return _s5_post(ypre, gv, wg)

        _, vjp = jax.vjp(head, [sre_ref[:, ss] for _, ss in S5_BLOCKS], [sim_ref[:, ss] for _, ss in S5_BLOCKS],
                         [cre_ref[ss, us].astype(F32) for us, ss in S5_BLOCKS],
                         [cim_ref[ss, us].astype(F32) for us, ss in S5_BLOCKS],
                         d_ref[...], u, g_ref[...], wg_ref[...].astype(F32))
        ds_res, ds_ims, dcres, dcims, dd, du_d, dgate, dwg = vjp(do_ref[0])
        for k, (us, ss) in enumerate(S5_BLOCKS):
            dcre_ref[ss, us] += dcres[k]
            dcim_ref[ss, us] += dcims[k]
            gre[:, ss] = ds_res[k]
            gim[:, ss] = ds_ims[k]
        dd_ref[...] += dd
        dwg_ref[...] += dwg
        dp_ref[:, BW:] = dgate.astype(BF16)
        ar, ai = a_ref[0:1, :], a_ref[1:2, :]

        def step(i, carry):
            t = S5_T - 1 - i
            gr, gi = carry
            nr = gre[pl.ds(t, 1), :] + gr
            ni = gim[pl.ds(t, 1), :] + gi
            gre[pl.ds(t, 1), :] = nr
            gim[pl.ds(t, 1), :] = ni
            return ar * nr + ai * ni, ar * ni - ai * nr

        gr, gi = lax.fori_loop(0, S5_T, step, (st[0:1, :], st[1:2, :]), unroll=8)
        st[0:1, :] = gr
        st[1:2, :] = gi
        g_re, g_im = gre[...], gim[...]
        first = jnp.where(c > 0, 1.0, 0.0)
        row = lax.broadcasted_iota(jnp.int32, (S5_T, S5_CH), 0)
        p_re = jnp.where(row == 0, pre_ref[7:8, :] * first, pltpu.roll(s_re, 1, 0))
        p_im = jnp.where(row == 0, pim_ref[7:8, :] * first, pltpu.roll(s_im, 1, 0))
        da_ref[0:1, :] += jnp.sum(g_re * p_re + g_im * p_im, axis=0, keepdims=True)
        da_ref[1:2, :] += jnp.sum(g_im * p_re - g_re * p_im, axis=0, keepdims=True)
        ub, grb, gib = u.astype(BF16), g_re.astype(BF16), g_im.astype(BF16)
        du_s = []
        for us, ss in S5_BLOCKS:
            dbbre_ref[us, ss] += _dg(ub[:, us], grb[:, ss], 0, 0)
            dbbim_ref[us, ss] += _dg(ub[:, us], gib[:, ss], 0, 0)
            du_s.append(_dg(grb[:, ss], bbre_ref[us, ss], 1, 1) + _dg(gib[:, ss], bbim_ref[us, ss], 1, 1))
        dp_ref[:, :BW] = (du_d + jnp.concatenate(du_s, axis=1)).astype(BF16)

    full = lambda shape: pl.BlockSpec(shape, lambda i: (0, 0))
    rev = lambda w, col=0: pl.BlockSpec((S5_T, w), lambda i: (nc - 1 - i, col))
    prev = pl.BlockSpec((8, S5_CH), lambda i: (jnp.maximum((nc - 1 - i) * (S5_T // 8) - 1, 0), 0))
    return pl.pallas_call(
        body, name=name, grid=(nc,),
        in_specs=[rev(BW, C_S5U // BW), rev(BW, C_S5G // BW), pl.BlockSpec((1, S5_T, BW), lambda i: (0, nc - 1 - i, 0)),
                  rev(S5_CH), rev(S5_CH), prev, prev,
                  full((BW, S5_CH)), full((BW, S5_CH)), full((S5_CH, BW)), full((S5_CH, BW)),
                  full((2, S5_CH)), full((1, BW)), full((BW, BW)), pl.BlockSpec(memory_space=pl.ANY)],
        out_specs=[rev(2 * BW, C_S5U // (2 * BW)), full((BW, S5_CH)), full((BW, S5_CH)), full((S5_CH, BW)), full((S5_CH, BW)),
                   full((2, S5_CH)), full((1, BW)), full((BW, BW))],
        input_output_aliases={14: 0},
        out_shape=[jax.ShapeDtypeStruct((SEQ, IN_PAD), BF16),
                   jax.ShapeDtypeStruct((BW, S5_CH), F32), jax.ShapeDtypeStruct((BW, S5_CH), F32),
                   jax.ShapeDtypeStruct((S5_CH, BW), F32), jax.ShapeDtypeStruct((S5_CH, BW), F32),
                   jax.ShapeDtypeStruct((2, S5_CH), F32), jax.ShapeDtypeStruct((1, BW), F32),
                   jax.ShapeDtypeStruct((BW, BW), F32)],
        scratch_shapes=[pltpu.VMEM((S5_T, S5_CH), F32), pltpu.VMEM((S5_T, S5_CH), F32), pltpu.VMEM((2, S5_CH), F32)],
        compiler_params=_cparams(("arbitrary",)),
    )(proj, proj, dout, sre, sim, sre, sim, bbre, bbim, cre, cim, a2, dvec, wglu, dproj)


def _diag_blocks(dense, after=None):
    rows, cols = dense.shape
    rows_per, cols_per = rows // S5_GROUPS, cols // S5_GROUPS
    per_lane_block = LANES // cols_per
    tile = 512

    def body(d_ref, *rest):
        o_ref = rest[-1]
        r0 = pl.program_id(0) * tile
        grp = (r0 + lax.broadcasted_iota(jnp.int32, (tile, LANES), 0)) // rows_per
        lane = lax.broadcasted_iota(jnp.int32, (tile, LANES), 1)
        acc = jnp.zeros((tile, LANES), F32)
        for hb in range(cols // LANES):
            acc = acc + jnp.where(grp == per_lane_block * hb + lane // cols_per, d_ref[:, hb * LANES:(hb + 1) * LANES], 0.0)
        shift = LANES // 2
        while shift >= cols_per:
            acc = acc + pltpu.roll(acc, LANES - shift, 1)
            shift //= 2
        o_ref[...] = acc

    folded = pl.pallas_call(
        body, name=f"diag_blocks_{rows_per}x{cols_per}", grid=(rows // tile,),
        in_specs=[pl.BlockSpec((tile, cols), lambda i: (i, 0))] + ([] if after is None else [pl.BlockSpec(memory_space=pl.ANY)]),
        out_specs=pl.BlockSpec((tile, LANES), lambda i: (i, 0)),
        out_shape=jax.ShapeDtypeStruct((rows, LANES), F32), compiler_params=_cparams(("parallel",)),
    )(dense, *([] if after is None else [after]))
    return folded[:, :cols_per].reshape(S5_GROUPS, rows_per, cols_per)


def _block_diag(t):
    g, rows_per, cols_per = t.shape
    wide = jnp.tile(t.reshape(g * rows_per, cols_per), (1, g))
    r = lax.broadcasted_iota(jnp.int32, wide.shape, 0) // rows_per
    c = lax.broadcasted_iota(jnp.int32, wide.shape, 1) // cols_per
    return jnp.where(r == c, wide, 0.0)


def _s5_disc(lam_re, lam_im, b_re, b_im, c_re, c_im, d, log_step):
    step = jnp.exp(log_step)[:, None]
    mag = jnp.exp(lam_re * step)
    ab_re, ab_im = mag * jnp.cos(lam_im * step), mag * jnp.sin(lam_im * step)
    den = lam_re * lam_re + lam_im * lam_im
    nr = ab_re - 1.0
    coef_re = (nr * lam_re + ab_im * lam_im) / den
    coef_im = (ab_im * lam_re - nr * lam_im) / den
    bb_re = coef_re[..., None] * b_re - coef_im[..., None] * b_im
    bb_im = coef_re[..., None] * b_im + coef_im[..., None] * b_re
    a2 = jnp.stack([ab_re.reshape(-1), ab_im.reshape(-1)])
    return (jnp.swapaxes(bb_re, 1, 2), jnp.swapaxes(bb_im, 1, 2),
            jnp.swapaxes(c_re, 1, 2), jnp.swapaxes(c_im, 1, 2),
            a2, d.reshape(1, BW))


def _left_lanes(shape):
    return lax.broadcasted_iota(jnp.int32, shape, 1) < 64


def _sgu_chunk(u, v, gate, ln_w, ln_b, w, bias):
    u32, v32 = _gelu(u), _gelu(v)
    mu = jnp.mean(v32, axis=-1, keepdims=True)
    var = jnp.mean(jnp.square(v32 - mu), axis=-1, keepdims=True)
    vn = (v32 - mu) * lax.rsqrt(var + EPS) * ln_w + ln_b
    t_i = lax.broadcasted_iota(jnp.int32, (SGU_CHUNK, SGU_CHUNK), 0)
    s_i = lax.broadcasted_iota(jnp.int32, (SGU_CHUNK, SGU_CHUNK), 1)
    causal = t_i >= s_i
    left = _left_lanes((SGU_CHUNK, LANES))
    sgate = _silu(gate)
    outs = []
    for j in range(BW // LANES):
        vb = vn[:, j * LANES:(j + 1) * LANES]
        s_blk = (_bdot(jnp.where(causal, w[2 * j], 0.0), jnp.where(left, vb, 0.0), 1, 0)
                 + _bdot(jnp.where(causal, w[2 * j + 1], 0.0), jnp.where(left, 0.0, vb), 1, 0))
        sl = slice(j * LANES, (j + 1) * LANES)
        outs.append(u32[:, sl] * (s_blk + bias[:, sl]) * sgate[:, sl])
    return outs


def _sgu_fwd(proj, ys, ln_w, ln_b, w, bias, name):
    def body(u_ref, v_ref, g_ref, lw_ref, lb_ref, w_ref, b_ref, ys_in, o_ref):
        outs = _sgu_chunk(u_ref[...], v_ref[...], g_ref[...], lw_ref[...], lb_ref[...], w_ref[...], b_ref[...])
        for j, o in enumerate(outs):
            o_ref[0, :, j * LANES:(j + 1) * LANES] = o.astype(BF16)

    blk = lambda col: pl.BlockSpec((SGU_CHUNK, BW), lambda c: (c, col // BW))
    vec = pl.BlockSpec((1, BW), lambda c: (0, 0))
    return pl.pallas_call(
        body, name=name, grid=(SEQ // SGU_CHUNK,),
        in_specs=[blk(C_SGU_U), blk(C_SGU_V), blk(C_SGU_G), vec, vec,
                  pl.BlockSpec((SGU_HEADS, SGU_CHUNK, SGU_CHUNK), lambda c: (0, 0, 0)),
                  pl.BlockSpec((SGU_CHUNK, BW), lambda c: (0, 0)), pl.BlockSpec(memory_space=pl.ANY)],
        out_specs=pl.BlockSpec((1, SGU_CHUNK, BW), lambda c: (1, c, 0)),
        out_shape=jax.ShapeDtypeStruct((N_BRANCH, SEQ, BW), BF16), input_output_aliases={7: 0},
        compiler_params=_cparams(("parallel",)),
    )(proj, proj, proj, ln_w, ln_b, w, bias, ys)


def _sgu_bwd(proj, dproj, dout, ln_w, ln_b, w, bias, name):
    def body(u_ref, v_ref, g_ref, do_ref, lw_ref, lb_ref, w_ref, b_ref, dproj_in, dp_ref, dlw_ref, dlb_ref, dw_ref, db_ref):
        _, vjp = jax.vjp(_sgu_chunk, u_ref[...], v_ref[...], g_ref[...], lw_ref[...], lb_ref[...], w_ref[...], b_ref[...])
        do = do_ref[0]
        du, dv, dgate, dlw, dlb, dw, db = vjp([do[:, j * LANES:(j + 1) * LANES] for j in range(BW // LANES)])
        dp_ref[:, 0:BW] = du.astype(BF16)
        dp_ref[:, BW:2 * BW] = dv.astype(BF16)
        dp_ref[:, 2 * BW:3 * BW] = dgate.astype(BF16)
        dp_ref[:, 3 * BW:] = jnp.zeros((SGU_CHUNK, BW), BF16)

        @pl.when(pl.program_id(0) == 0)
        def _():
            dlw_ref[...] = dlw
            dlb_ref[...] = dlb
            dw_ref[...] = dw
            db_ref[...] = db

        @pl.when(pl.program_id(0) > 0)
        def _():
            dlw_ref[...] += dlw
            dlb_ref[...] += dlb
            dw_ref[...] += dw
            db_ref[...] += db

    blk = lambda col: pl.BlockSpec((SGU_CHUNK, BW), lambda c: (c, col // BW))
    vec = pl.BlockSpec((1, BW), lambda c: (0, 0))
    wsp = pl.BlockSpec((SGU_HEADS, SGU_CHUNK, SGU_CHUNK), lambda c: (0, 0, 0))
    bsp = pl.BlockSpec((SGU_CHUNK, BW), lambda c: (0, 0))
    return pl.pallas_call(
        body, name=name, grid=(SEQ // SGU_CHUNK,),
        in_specs=[blk(C_SGU_U), blk(C_SGU_V), blk(C_SGU_G), pl.BlockSpec((1, SGU_CHUNK, BW), lambda c: (1, c, 0)),
                  vec, vec, wsp, bsp, pl.BlockSpec(memory_space=pl.ANY)],
        out_specs=[pl.BlockSpec((SGU_CHUNK, 4 * BW), lambda c: (c, C_SGU_U // (4 * BW))), vec, vec, wsp, bsp],
        input_output_aliases={8: 0},
        out_shape=[jax.ShapeDtypeStruct((SEQ, IN_PAD), BF16), jax.ShapeDtypeStruct((1, BW), F32),
                   jax.ShapeDtypeStruct((1, BW), F32), jax.ShapeDtypeStruct((SGU_HEADS, SGU_CHUNK, SGU_CHUNK), F32),
                   jax.ShapeDtypeStruct((SGU_CHUNK, BW), F32)],
        compiler_params=_cparams(("arbitrary",)),
    )(proj, proj, proj, dout, ln_w, ln_b, w, bias, dproj)


CONV_BLK = 256


def _m2_conv_fwd(proj, w, b, name):
    def body(x_ref, w_ref, b_ref, o_ref):
        x = x_ref[...]
        acc = jnp.zeros_like(x) + b_ref[...]
        for k in range(M2_CONV):
            acc = acc + w_ref[k:k + 1, :] * _shift_down(x, M2_CONV - 1 - k)
        o_ref[...] = _silu(acc)

    return pl.pallas_call(
        body, name=name, grid=(M2_CONV_CH // CONV_BLK,),
        in_specs=[pl.BlockSpec((SEQ, CONV_BLK), lambda j: (0, C_M2X // CONV_BLK + j)),
                  pl.BlockSpec((M2_CONV, CONV_BLK), lambda j: (0, j)), pl.BlockSpec((1, CONV_BLK), lambda j: (0, j))],
        out_specs=pl.BlockSpec((SEQ, CONV_BLK), lambda j: (0, j)),
        out_shape=jax.ShapeDtypeStruct((SEQ, M2_CONV_CH), F32),
        compiler_params=_cparams(("parallel",)),
    )(proj, w, b)


def _m2_conv_bwd(proj, dproj, dxa, w, b, name):
    def body(x_ref, d_ref, w_ref, b_ref, dproj_in, dx_ref, dw_ref, db_ref):
        x = x_ref[...]
        xs = [_shift_down(x, M2_CONV - 1 - k) for k in range(M2_CONV)]
        acc = jnp.zeros_like(x) + b_ref[...]
        for k in range(M2_CONV):
            acc = acc + w_ref[k:k + 1, :] * xs[k]
        sg = jax.nn.sigmoid(acc)
        dacc = d_ref[...] * (sg * (1.0 + acc * (1.0 - sg)))
        dx = jnp.zeros_like(x)
        for k in range(M2_CONV):
            dx = dx + w_ref[k:k + 1, :] * _shift_up(dacc, M2_CONV - 1 - k)
            dw_ref[k:k + 1, :] = jnp.sum(dacc * xs[k], axis=0, keepdims=True)
        dx_ref[...] = dx.astype(BF16)
        db_ref[...] = jnp.sum(dacc, axis=0, keepdims=True)

    return pl.pallas_call(
        body, name=name, grid=(M2_CONV_CH // CONV_BLK,),
        in_specs=[pl.BlockSpec((SEQ, CONV_BLK), lambda j: (0, C_M2X // CONV_BLK + j)),
                  pl.BlockSpec((SEQ, CONV_BLK), lambda j: (0, j)),
                  pl.BlockSpec((M2_CONV, CONV_BLK), lambda j: (0, j)), pl.BlockSpec((1, CONV_BLK), lambda j: (0, j)),
                  pl.BlockSpec(memory_space=pl.ANY)],
        out_specs=[pl.BlockSpec((SEQ, CONV_BLK), lambda j: (0, C_M2X // CONV_BLK + j)),
                   pl.BlockSpec((M2_CONV, CONV_BLK), lambda j: (0, j)), pl.BlockSpec((1, CONV_BLK), lambda j: (0, j))],
        input_output_aliases={4: 0},
        out_shape=[jax.ShapeDtypeStruct((SEQ, IN_PAD), BF16), jax.ShapeDtypeStruct((M2_CONV, M2_CONV_CH), F32),
                   jax.ShapeDtypeStruct((1, M2_CONV_CH), F32)],
        compiler_params=_cparams(("parallel",)),
    )(proj, dxa, w, b, dproj)


N_PAIR = M2_HEADS // 2
HI = lax.Precision.HIGHEST


def _col(a, h):
    lane = lax.broadcasted_iota(jnp.int32, a.shape, 1)
    return jnp.sum(jnp.where(lane == h, a, 0.0), axis=1, keepdims=True)


def _row(a, h):
    sub = lax.broadcasted_iota(jnp.int32, a.shape, 0)
    return jnp.sum(jnp.where(sub == h, a, 0.0), axis=0, keepdims=True)


def _ssd_chunk(xs, bms, cms, dtr, zs, states, dt_bias, a_log, dfs, nws):
    q = M2_CHUNK
    dt = _softplus(dtr + dt_bias)
    da = dt * (-jnp.exp(a_log))
    l_i = lax.broadcasted_iota(jnp.int32, (q, q), 0)
    s_i = lax.broadcasted_iota(jnp.int32, (q, q), 1)
    causal = l_i >= s_i
    tril = jnp.where(causal, 1.0, 0.0)
    a_cs = _dg(tril, da, 1, 0, HI)
    a_cs_t = _dg(da, tril, 0, 1, HI)
    a_end = _row(a_cs, q - 1)
    left = _left_lanes((q, LANES))
    left1 = _left_lanes((1, LANES))
    ys, nexts = [], []
    for j in range(N_PAIR):
        grp = j // 2
        bm, cm = bms[grp], cms[grp]
        h0, h1 = 2 * j, 2 * j + 1
        cb = _bdot(cm, bm, 1, 1)
        xdt = xs[j] * jnp.where(left, _col(dt, h0), _col(dt, h1))
        acs0, acs1 = _col(a_cs, h0), _col(a_cs, h1)
        y = _bdot(cm, states[j], 1, 0) * jnp.where(left, jnp.exp(acs0), jnp.exp(acs1))
        s_new = states[j] * jnp.where(left1, jnp.exp(_col(a_end, h0)), jnp.exp(_col(a_end, h1)))
        for h, acs, xh in ((h0, acs0, jnp.where(left, xdt, 0.0)), (h1, acs1, jnp.where(left, 0.0, xdt))):
            decay = jnp.exp(jnp.where(causal, acs - _row(a_cs_t, h), -jnp.inf))
            y = y + _bdot(cb * decay, xh, 1, 0)
            s_new = s_new + _bdot(bm * jnp.exp(_col(a_end, h) - acs), xh, 0, 0)
        ys.append((y + dfs[j] * xs[j]) * _silu(zs[j]))
        nexts.append(s_new)
    ssq = sum(jnp.sum(y * y, axis=-1, keepdims=True) for y in ys)
    scale = lax.rsqrt(ssq / BW + EPS)
    return [y * scale * nw for y, nw in zip(ys, nws)], nexts


def _blocks(ref, n, width=LANES):
    return [ref[:, j * width:(j + 1) * width] for j in range(n)]


def _ssd_fwd(proj, ys, xa, dt_bias, a_log, dfull, nw, name):
    nc = SEQ // M2_CHUNK

    def body(x_ref, b_ref, c_ref, dt_ref, z_ref, dtb_ref, al_ref, df_ref, nw_ref, ys_in, o_ref, sin_ref, st):
        @pl.when(pl.program_id(0) == 0)
        def _():
            st[...] = jnp.zeros_like(st)

        states = [st[j] for j in range(N_PAIR)]
        for j in range(N_PAIR):
            sin_ref[0, j] = states[j]
        ys, nexts = _ssd_chunk(_blocks(x_ref, 4), _blocks(b_ref, 2), _blocks(c_ref, 2), dt_ref[...], _blocks(z_ref, 4),
                               states, dtb_ref[...], al_ref[...], _blocks(df_ref, 4), _blocks(nw_ref, 4))
        for j in range(N_PAIR):
            o_ref[0, :, j * LANES:(j + 1) * LANES] = ys[j].astype(BF16)
            st[j] = nexts[j]

    vec8 = pl.BlockSpec((1, LANES), lambda c: (0, 0))
    vec = pl.BlockSpec((1, BW), lambda c: (0, 0))
    return pl.pallas_call(
        body, name=name, grid=(nc,),
        in_specs=[pl.BlockSpec((M2_CHUNK, BW), lambda c: (c, 0)), pl.BlockSpec((M2_CHUNK, 256), lambda c: (c, 2)),
                  pl.BlockSpec((M2_CHUNK, 256), lambda c: (c, 3)), pl.BlockSpec((M2_CHUNK, LANES), lambda c: (c, C_DT // LANES)),
                  pl.BlockSpec((M2_CHUNK, BW), lambda c: (c, C_M2Z // BW)), vec8, vec8, vec, vec,
                  pl.BlockSpec(memory_space=pl.ANY)],
        out_specs=[pl.BlockSpec((1, M2_CHUNK, BW), lambda c: (2, c, 0)),
                   pl.BlockSpec((1, N_PAIR, M2_STATE, LANES), lambda c: (c, 0, 0, 0))],
        out_shape=[jax.ShapeDtypeStruct((N_BRANCH, SEQ, BW), BF16), jax.ShapeDtypeStruct((nc, N_PAIR, M2_STATE, LANES), F32)],
        input_output_aliases={9: 0},
        scratch_shapes=[pltpu.VMEM((N_PAIR, M2_STATE, LANES), F32)],
        compiler_params=_cparams(("arbitrary",)),
    )(xa, xa, xa, proj, proj, dt_bias, a_log, dfull, nw, ys)


def _ssd_bwd(proj, dproj, xa, dout, s_in, dt_bias, a_log, dfull, nw, name):
    nc = SEQ // M2_CHUNK

    def body(x_ref, b_ref, c_ref, dt_ref, z_ref, do_ref, sin_ref, dtb_ref, al_ref, df_ref, nw_ref, dproj_in,
             dp_ref, dxa_ref, ddtb_ref, dal_ref, ddf_ref, dnw_ref, dst):
        @pl.when(pl.program_id(0) == 0)
        def _():
            dst[...] = jnp.zeros_like(dst)
            for r in (ddtb_ref, dal_ref, ddf_ref, dnw_ref):
                r[...] = jnp.zeros_like(r)

        states = [sin_ref[0, j] for j in range(N_PAIR)]
        _, vjp = jax.vjp(_ssd_chunk, _blocks(x_ref, 4), _blocks(b_ref, 2), _blocks(c_ref, 2), dt_ref[...],
                         _blocks(z_ref, 4), states, dtb_ref[...], al_ref[...], _blocks(df_ref, 4), _blocks(nw_ref, 4))
        dxs, dbs, dcs, ddt, dzs, dstates, ddtb, dal, ddfs, dnws = vjp(
            ([do_ref[0, :, j * LANES:(j + 1) * LANES] for j in range(N_PAIR)], [dst[j] for j in range(N_PAIR)]))
        for j in range(N_PAIR):
            sl = slice(j * LANES, (j + 1) * LANES)
            dxa_ref[:, sl] = dxs[j]
            dp_ref[:, sl] = dzs[j].astype(BF16)
            dst[j] = dstates[j]
            ddf_ref[:, sl] += ddfs[j]
            dnw_ref[:, sl] += dnws[j]
        for g in range(2):
            dxa_ref[:, BW + g * LANES:BW + (g + 1) * LANES] = dbs[g]
            dxa_ref[:, BW + 256 + g * LANES:BW + 256 + (g + 1) * LANES] = dcs[g]
        dp_ref[:, BW:BW + LANES] = ddt.astype(BF16)
        dp_ref[:, BW + LANES:] = jnp.zeros((M2_CHUNK, 2 * BW - BW - LANES), BF16)
        ddtb_ref[...] += ddtb
        dal_ref[...] += dal

    rev = lambda w, col=0: pl.BlockSpec((M2_CHUNK, w), lambda i: (nc - 1 - i, col))
    vec8 = pl.BlockSpec((1, LANES), lambda i: (0, 0))
    vec = pl.BlockSpec((1, BW), lambda i: (0, 0))
    return pl.pallas_call(
        body, name=name, grid=(nc,),
        in_specs=[rev(BW), rev(256, 2), rev(256, 3), rev(LANES, C_DT // LANES), rev(BW, C_M2Z // BW),
                  pl.BlockSpec((1, M2_CHUNK, BW), lambda i: (2, nc - 1 - i, 0)),
                  pl.BlockSpec((1, N_PAIR, M2_STATE, LANES), lambda i: (nc - 1 - i, 0, 0, 0)), vec8, vec8, vec, vec,
                  pl.BlockSpec(memory_space=pl.ANY)],
        out_specs=[rev(2 * BW, C_M2Z // (2 * BW)), rev(M2_CONV_CH), vec8, vec8, vec, vec],
        input_output_aliases={11: 0},
        out_shape=[jax.ShapeDtypeStruct((SEQ, IN_PAD), BF16), jax.ShapeDtypeStruct((SEQ, M2_CONV_CH), F32),
                   jax.ShapeDtypeStruct((1, LANES), F32), jax.ShapeDtypeStruct((1, LANES), F32),
                   jax.ShapeDtypeStruct((1, BW), F32), jax.ShapeDtypeStruct((1, BW), F32)],
        scratch_shapes=[pltpu.VMEM((N_PAIR, M2_STATE, LANES), F32)],
        compiler_params=_cparams(("arbitrary",)),
    )(xa, xa, xa, proj, proj, dout, s_in, dt_bias, a_log, dfull, nw, dproj)


def _sc_specs():
    col = lambda kind: pl.BlockSpec((SEQ, LANES), lambda j: (0, C_SC // LANES + 4 * j + kind))
    return [col(0), col(1), col(2), col(3)]


def _sc_fwd(proj, ys, w, name):
    def body(b_ref, c_ref, h_ref, g_ref, w_ref, ys_in, o_ref):
        ch = c_ref[...] * h_ref[...]
        acc = jnp.zeros_like(ch)
        for k in range(SC_CONV):
            acc = acc + w_ref[k:k + 1, :] * _shift_down(ch, SC_CONV - 1 - k)
        o_ref[0] = (b_ref[...] * acc * _silu(g_ref[...])).astype(BF16)

    return pl.pallas_call(
        body, name=name, grid=(BW // LANES,),
        in_specs=_sc_specs() + [pl.BlockSpec((SC_CONV, LANES), lambda j: (0, j)), pl.BlockSpec(memory_space=pl.ANY)],
        out_specs=pl.BlockSpec((1, SEQ, LANES), lambda j: (3, 0, j)),
        out_shape=jax.ShapeDtypeStruct((N_BRANCH, SEQ, BW), BF16), input_output_aliases={5: 0},
        compiler_params=_cparams(("parallel",)),
    )(proj, proj, proj, proj, w, ys)


def _sc_bwd(proj, dproj, dout, w, name):
    def body(b_ref, c_ref, h_ref, g_ref, do_ref, w_ref, dproj_in, dp_ref, dw_ref):
        cv, hv, gv = c_ref[...], h_ref[...], g_ref[...]
        ch = cv * hv
        chs = [_shift_down(ch, SC_CONV - 1 - k) for k in range(SC_CONV)]
        acc = jnp.zeros_like(ch)
        for k in range(SC_CONV):
            acc = acc + w_ref[k:k + 1, :] * chs[k]
        sg = jax.nn.sigmoid(gv)
        do = do_ref[0]
        bv = b_ref[...]
        dp_ref[:, 0:LANES] = (do * acc * (gv * sg)).astype(BF16)
        dp_ref[:, 3 * LANES:] = (do * bv * acc * (sg * (1.0 + gv * (1.0 - sg)))).astype(BF16)
        dacc = do * bv * (gv * sg)
        dch = jnp.zeros_like(ch)
        for k in range(SC_CONV):
            dch = dch + w_ref[k:k + 1, :] * _shift_up(dacc, SC_CONV - 1 - k)
            dw_ref[k:k + 1, :] = jnp.sum(dacc * chs[k], axis=0, keepdims=True)
        dp_ref[:, LANES:2 * LANES] = (dch * hv).astype(BF16)
        dp_ref[:, 2 * LANES:3 * LANES] = (dch * cv).astype(BF16)

    wsp = pl.BlockSpec((SC_CONV, LANES), lambda j: (0, j))
    return pl.pallas_call(
        body, name=name, grid=(BW // LANES,),
        in_specs=_sc_specs() + [pl.BlockSpec((1, SEQ, LANES), lambda j: (3, 0, j)), wsp, pl.BlockSpec(memory_space=pl.ANY)],
        out_specs=[pl.BlockSpec((SEQ, 4 * LANES), lambda j: (0, C_SC // (4 * LANES) + j)), wsp],
        input_output_aliases={6: 0},
        out_shape=[jax.ShapeDtypeStruct((SEQ, IN_PAD), BF16), jax.ShapeDtypeStruct((SC_CONV, BW), F32)],
        compiler_params=_cparams(("parallel",)),
    )(proj, proj, proj, proj, dout, w, dproj)


MERGE_T = 256
MERGE_BWD_T = 1024


def _merge_fwd(proj, ys, merge_b, w_branch, name):
    def body(y_ref, lg_ref, b_ref, w_ref, o_ref):
        acc = jnp.zeros((MERGE_T, D_MODEL), F32)
        for k in range(N_BRANCH):
            gate = jax.nn.sigmoid(lg_ref[:, k * D_MODEL:(k + 1) * D_MODEL] + b_ref[k])
            acc = acc + gate * _dg(y_ref[k], w_ref[k], 1, 0)
        o_ref[...] = acc.astype(BF16)

    return pl.pallas_call(
        body, name=name, grid=(SEQ // MERGE_T,),
        in_specs=[pl.BlockSpec((N_BRANCH, MERGE_T, BW), lambda i: (0, i, 0)),
                  pl.BlockSpec((MERGE_T, N_BRANCH * D_MODEL), lambda i: (i, C_MERGE // (N_BRANCH * D_MODEL))),
                  pl.BlockSpec((N_BRANCH, 1, D_MODEL), lambda i: (0, 0, 0)),
                  pl.BlockSpec((N_BRANCH, BW, D_MODEL), lambda i: (0, 0, 0))],
        out_specs=pl.BlockSpec((MERGE_T, D_MODEL), lambda i: (i, 0)),
        out_shape=jax.ShapeDtypeStruct((SEQ, D_MODEL), BF16),
        compiler_params=_cparams(("parallel",)),
    )(ys, proj, merge_b, w_branch)


def _merge_bwd(proj, ys, dm, merge_b, w_branch, name):
    nt = SEQ // MERGE_BWD_T

    def body(y_ref, lg_ref, dm_ref, b_ref, w_ref, dy_ref, dlg_ref, dw_ref, db_ref, dw_acc):
        i = pl.program_id(1)
        gate = jax.nn.sigmoid(lg_ref[...] + b_ref[0])
        y = y_ref[0]
        dmv = dm_ref[...]
        dbo = (gate * dmv).astype(BF16)
        dlg = _dg(y, w_ref[0], 1, 0) * dmv * gate * (1.0 - gate)
        dlg_ref[...] = dlg.astype(BF16)
        dy_ref[0] = _dg(dbo, w_ref[0], 1, 1)
        dwp = _dg(y, dbo, 0, 0)
        dbp = jnp.sum(dlg, axis=0, keepdims=True)

        @pl.when(i == 0)
        def _():
            dw_acc[...] = dwp
            db_ref[0] = dbp

        @pl.when(i > 0)
        def _():
            dw_acc[...] += dwp
            db_ref[0] += dbp

        @pl.when(i == nt - 1)
        def _():
            dw_ref[0] = dw_acc[...].astype(BF16)

    return pl.pallas_call(
        body, name=name, grid=(N_BRANCH, nt),
        in_specs=[pl.BlockSpec((1, MERGE_BWD_T, BW), lambda k, i: (k, i, 0)),
                  pl.BlockSpec((MERGE_BWD_T, D_MODEL), lambda k, i: (i, C_MERGE // D_MODEL + k)),
                  pl.BlockSpec((MERGE_BWD_T, D_MODEL), lambda k, i: (i, 0)),
                  pl.BlockSpec((1, 1, D_MODEL), lambda k, i: (k, 0, 0)),
                  pl.BlockSpec((1, BW, D_MODEL), lambda k, i: (k, 0, 0))],
        out_specs=[pl.BlockSpec((1, MERGE_BWD_T, BW), lambda k, i: (k, i, 0)),
                   pl.BlockSpec((MERGE_BWD_T, D_MODEL), lambda k, i: (i, k)),
                   pl.BlockSpec((1, BW, D_MODEL), lambda k, i: (k, 0, 0)),
                   pl.BlockSpec((1, 1, D_MODEL), lambda k, i: (k, 0, 0))],
        out_shape=[jax.ShapeDtypeStruct((N_BRANCH, SEQ, BW), F32), jax.ShapeDtypeStruct((SEQ, IN_PAD), BF16),
                   jax.ShapeDtypeStruct((N_BRANCH, BW, D_MODEL), BF16), jax.ShapeDtypeStruct((N_BRANCH, 1, D_MODEL), F32)],
        scratch_shapes=[pltpu.VMEM((BW, D_MODEL), F32)],
        compiler_params=_cparams(("parallel", "arbitrary")),
    )(ys, proj, dm, merge_b, w_branch)


def _adamw(glist, w, m, v, rows, name):
    nl = len(glist)
    n, r, c = glist[0].shape
    assert w.shape == (nl, r, c) and r % rows == 0
    nb = r // rows

    def body(*refs):
        g_refs = refs[:nl]
        w_ref, m_ref, v_ref, go_ref, d_ref, mo_ref, vo_ref = refs[nl:]
        for layer in range(nl):
            @pl.when(pl.program_id(0) == layer)
            def _(g_ref=g_refs[layer]):
                g = g_ref[0].astype(F32)
                for s in range(1, n):
                    g = g + g_ref[s].astype(F32)
                mn = ADAM_B1 * m_ref[0] + (1.0 - ADAM_B1) * g
                vn = ADAM_B2 * v_ref[0] + (1.0 - ADAM_B2) * jnp.square(g)
                m_hat = mn / (1.0 - ADAM_B1 ** ADAM_STEP)
                v_hat = vn / (1.0 - ADAM_B2 ** ADAM_STEP)
                go_ref[0] = g
                d_ref[0] = -ADAM_LR * (m_hat / (jnp.sqrt(v_hat) + ADAM_EPS) + ADAM_WD * w_ref[0])
                mo_ref[0] = mn
                vo_ref[0] = vn

    def g_spec(layer):
        return pl.BlockSpec((n, rows, c), lambda a, i: (0, jnp.where(a < layer, 0, jnp.where(a == layer, i, nb - 1)), 0))

    blk = pl.BlockSpec((1, rows, c), lambda a, i: (a, i, 0))
    out = jax.ShapeDtypeStruct((nl, r, c), F32)
    return pl.pallas_call(
        body, name=name, grid=(nl, nb),
        in_specs=[g_spec(layer) for layer in range(nl)] + [blk, blk, blk],
        out_specs=[blk, blk, blk, blk], out_shape=[out, out, out, out],
        compiler_params=_cparams(("arbitrary", "arbitrary")),
    )(*glist, w, m, v)


X_ROWS_PER_COL = 2 * (D_MODEL // LANES)


def _w_in_to_x(w):
    t = jnp.transpose(w, (2, 0, 1)).reshape(SHARD_IN, DEPTH, D_MODEL // LANES, LANES)
    return jnp.transpose(t, (0, 2, 1, 3)).reshape(SHARD_IN * X_ROWS_PER_COL, LANES)


def _w_in_from_x(xv):
    t = jnp.transpose(xv.reshape(SHARD_IN, D_MODEL // LANES, DEPTH, LANES), (0, 2, 1, 3))
    return jnp.transpose(t.reshape(SHARD_IN, DEPTH, D_MODEL), (1, 2, 0))


def _adamw_w_in(glist, w, m, v, name, after=None):
    n = glist[0].shape[0]
    cols = 2 * LANES
    rows = cols * X_ROWS_PER_COL
    extra = [] if after is None else [after]

    def body(g0_ref, g1_ref, w_ref, m_ref, v_ref, *rest):
        go_ref, d_ref, mo_ref, vo_ref = rest[len(extra):]
        for layer, g_ref in enumerate((g0_ref, g1_ref)):
            g = g_ref[0].astype(F32)
            for s in range(1, n):
                g = g + g_ref[s].astype(F32)
            gt = g.T
            for t in range(D_MODEL // LANES):
                sel = (pl.ds(2 * t + layer, cols, stride=X_ROWS_PER_COL), slice(None))
                gs = gt[:, t * LANES:(t + 1) * LANES]
                mn = ADAM_B1 * m_ref[sel] + (1.0 - ADAM_B1) * gs
                vn = ADAM_B2 * v_ref[sel] + (1.0 - ADAM_B2) * jnp.square(gs)
                m_hat = mn / (1.0 - ADAM_B1 ** ADAM_STEP)
                v_hat = vn / (1.0 - ADAM_B2 ** ADAM_STEP)
                go_ref[sel] = gs
                d_ref[sel] = -ADAM_LR * (m_hat / (jnp.sqrt(v_hat) + ADAM_EPS) + ADAM_WD * w_ref[sel])
                mo_ref[sel] = mn
                vo_ref[sel] = vn

    g_spec = pl.BlockSpec((n, D_MODEL, cols), lambda i: (0, 0, i))
    blk = pl.BlockSpec((rows, LANES), lambda i: (i, 0))
    out = jax.ShapeDtypeStruct((SHARD_IN * X_ROWS_PER_COL, LANES), F32)
    res = pl.pallas_call(
        body, name=name, grid=(-(-SHARD_IN // cols),),
        in_specs=[g_spec, g_spec, blk, blk, blk] + [pl.BlockSpec(memory_space=pl.ANY)] * len(extra),
        out_specs=[blk, blk, blk, blk], out_shape=[out, out, out, out],
        compiler_params=_cparams(("parallel",)),
    )(*glist, _w_in_to_x(w), _w_in_to_x(m), _w_in_to_x(v), *extra)
    return [_w_in_from_x(o) for o in res]


def _adamw_many(gs, ws, ms, vs, name):
    k = len(gs)

    def body(*refs):
        g_refs, w_refs, m_refs, v_refs = refs[:k], refs[k:2 * k], refs[2 * k:3 * k], refs[3 * k:4 * k]
        d_refs, mo_refs, vo_refs = refs[4 * k:5 * k], refs[5 * k:6 * k], refs[6 * k:7 * k]
        for i in range(k):
            g = g_refs[i][...]
            mn = ADAM_B1 * m_refs[i][...] + (1.0 - ADAM_B1) * g
            vn = ADAM_B2 * v_refs[i][...] + (1.0 - ADAM_B2) * jnp.square(g)
            m_hat = mn / (1.0 - ADAM_B1 ** ADAM_STEP)
            v_hat = vn / (1.0 - ADAM_B2 ** ADAM_STEP)
            d_refs[i][...] = -ADAM_LR * (m_hat / (jnp.sqrt(v_hat) + ADAM_EPS) + ADAM_WD * w_refs[i][...])
            mo_refs[i][...] = mn
            vo_refs[i][...] = vn

    whole = pl.BlockSpec(memory_space=pltpu.VMEM)
    shapes = [jax.ShapeDtypeStruct(w.shape, F32) for w in ws]
    outs = pl.pallas_call(
        body, name=name, in_specs=[whole] * (4 * k), out_specs=[whole] * (3 * k), out_shape=shapes * 3,
        compiler_params=_cparams(None),
    )(*gs, *ws, *ms, *vs)
    return outs[:k], outs[k:2 * k], outs[2 * k:]


MEMORY_ORDER = {'s5_b_re': (0, 1, 3, 2), 's5_b_im': (0, 1, 3, 2), 's5_d': (0, 2, 1), 'sc_conv_w': (1, 0, 2)}


def _memory_view(name, t):
    return jnp.transpose(t, MEMORY_ORDER[name]) if name in MEMORY_ORDER else t


def _slot_sum(gslots, name):
    n, r, c = gslots.shape

    def body(g_ref, o_ref):
        g = g_ref[0]
        for s in range(1, n):
            g = g + g_ref[s]
        o_ref[...] = g

    return pl.pallas_call(
        body, name=name, in_specs=[pl.BlockSpec((n, r, c), lambda: (0, 0, 0))],
        out_specs=pl.BlockSpec((r, c), lambda: (0, 0)), out_shape=jax.ShapeDtypeStruct((r, c), F32),
        compiler_params=_cparams(None),
    )(gslots)


def _me_and_peers():
    x, y, c = lax.axis_index("x"), lax.axis_index("y"), lax.axis_index("c")
    me = 4 * x + 2 * y + c
    peers = []
    for k in range(1, N_DEV):
        px = 1 - x if (k >> 2) & 1 else x
        py = 1 - y if (k >> 1) & 1 else y
        pc = 1 - c if k & 1 else c
        peers.append((4 * px + 2 * py + pc, (px, py, pc)))
    return me, peers


_HBM = pl.BlockSpec(memory_space=pltpu.HBM)
_SEM = pl.BlockSpec(memory_space=pltpu.SEMAPHORE)
_EFFECT = pltpu.SideEffectType.DATAFLOW_SIDE_EFFECTING


N_CHIP = N_DEV // 2


def _chip_peers():
    x, y, c = lax.axis_index("x"), lax.axis_index("y"), lax.axis_index("c")
    chips = []
    for d in range(1, N_CHIP):
        px = 1 - x if (d >> 1) & 1 else x
        py = 1 - y if d & 1 else y
        chips.append((2 * px + py, (px, py)))
    return (x, y, c), 2 * x + y, chips


def _plan_direct(ins, lands, send_sems, recv_sems, local_sems, gather):
    me, peers = _me_and_peers()
    plan = dict(start=[], local=[], sends=[], recvs=[])
    for t in range(len(ins)):
        own = pltpu.make_async_copy(ins[t] if gather else ins[t].at[me], lands[t].at[me], local_sems.at[t])
        plan['start'].append(own)
        plan['local'].append(own)
        for k, (pidx, pos) in enumerate(peers):
            cp = pltpu.make_async_remote_copy(
                src_ref=ins[t] if gather else ins[t].at[pidx], dst_ref=lands[t].at[me],
                send_sem=send_sems.at[t * (N_DEV - 1) + k], recv_sem=recv_sems.at[t * (N_DEV - 1) + k],
                device_id=pos, device_id_type=MESH)
            plan['start'].append(cp)
            plan['sends'].append(cp)
            plan['recvs'].append(cp)
    return plan


def _plan_gather(ins, lands, send_sems, recv_sems, local_sems, first=0):
    (x, y, c), q, chips = _chip_peers()
    me = 2 * q + c
    plan = dict(start=[], relay_wait=[], relay_start=[], local=[], sends=[], recvs=[])
    for t in range(len(ins)):
        base = (first + t) * 7
        sem = lambda k: dict(send_sem=send_sems.at[base + k], recv_sem=recv_sems.at[base + k], device_id_type=MESH)
        own = pltpu.make_async_copy(ins[t], lands[t].at[me], local_sems.at[first + t])
        to_sib = pltpu.make_async_remote_copy(src_ref=ins[t], dst_ref=lands[t].at[me], device_id=(x, y, 1 - c), **sem(0))
        plan['start'] += [own, to_sib]
        plan['local'].append(own)
        plan['sends'].append(to_sib)
        plan['recvs'].append(to_sib)
        for d, (pq, (px, py)) in enumerate(chips):
            to_chip = pltpu.make_async_remote_copy(src_ref=ins[t], dst_ref=lands[t].at[me], device_id=(px, py, c), **sem(1 + d))
            blk = lands[t].at[2 * pq + c]
            fwd = pltpu.make_async_remote_copy(src_ref=blk, dst_ref=blk, device_id=(x, y, 1 - c), **sem(4 + d))
            plan['start'].append(to_chip)
            plan['relay_wait'].append(to_chip)
            plan['relay_start'].append(fwd)
            plan['sends'] += [to_chip, fwd]
            plan['recvs'].append(fwd)
    return plan


def _plan_pair(ins, lands, send_sems, recv_sems, local_sems):
    (x, y, c), q, chips = _chip_peers()
    plan = dict(start=[], local=[], sends=[], recvs=[])
    for t in range(len(ins)):
        for k in range(N_CHIP):
            cp = pltpu.make_async_remote_copy(
                src_ref=ins[t].at[2 * k + 1 - c], dst_ref=lands[t].at[k], send_sem=send_sems.at[t * N_CHIP + k],
                recv_sem=recv_sems.at[t * N_CHIP + k], device_id=(x, y, 1 - c), device_id_type=MESH)
            plan['start'].append(cp)
            plan['sends'].append(cp)
            plan['recvs'].append(cp)
    return plan


def _plan_chips(ins, lands, send_sems, recv_sems, local_sems):
    (x, y, c), q, chips = _chip_peers()
    plan = dict(start=[], local=[], sends=[], recvs=[])
    for t in range(len(ins)):
        own = pltpu.make_async_copy(ins[t].at[q], lands[t].at[q], local_sems.at[t])
        plan['start'].append(own)
        plan['local'].append(own)
        for d, (pq, (px, py)) in enumerate(chips):
            cp = pltpu.make_async_remote_copy(
                src_ref=ins[t].at[pq], dst_ref=lands[t].at[q], send_sem=send_sems.at[t * 3 + d],
                recv_sem=recv_sems.at[t * 3 + d], device_id=(px, py, c), device_id_type=MESH)
            plan['start'].append(cp)
            plan['sends'].append(cp)
            plan['recvs'].append(cp)
    return plan


def _split_start(plan_fn, tensors, land_shapes, n_sems, name, after=None):
    n = len(tensors)
    extra = [] if after is None else [after]

    def body(*refs):
        ins, lands = refs[:n], refs[n:2 * n]
        plan = plan_fn(ins, lands, *refs[2 * n + len(extra):2 * n + len(extra) + 3])
        for cp in plan['start']:
            cp.start()
        refs[-1][...] = jnp.zeros_like(refs[-1])

    outs = pl.pallas_call(
        body, name=name,
        out_shape=(pltpu.SemaphoreType.DMA((n_sems,)), pltpu.SemaphoreType.DMA((n_sems,)), pltpu.SemaphoreType.DMA((n,)),
                   *[pltpu.HBM(t.shape, t.dtype) for t in tensors],
                   *[pltpu.HBM(s, t.dtype) for s, t in zip(land_shapes, tensors)],
                   jax.ShapeDtypeStruct((8, LANES), F32)),
        in_specs=[_HBM] * (2 * n) + [pl.BlockSpec(memory_space=pl.ANY)] * len(extra),
        out_specs=(_SEM, _SEM, _SEM, *[_HBM] * (2 * n), pl.BlockSpec(memory_space=pltpu.VMEM)),
        input_output_aliases={t: 3 + t for t in range(2 * n)},
        compiler_params=pltpu.CompilerParams(has_side_effects=_EFFECT),
    )(*[pltpu.with_memory_space_constraint(t, pltpu.HBM) for t in tensors],
      *[pltpu.with_memory_space_constraint(lax.empty(s, t.dtype), pltpu.HBM) for s, t in zip(land_shapes, tensors)], *extra)
    return outs[:-1], outs[-1]


def _split_relay(plan_fn, state, after, name):
    sems, thru = state[:3], state[3:]
    n = len(thru) // 2

    def arrived(*refs):
        plan = plan_fn(refs[:n], refs[n:2 * n], *refs[2 * n:2 * n + 3])
        for cp in plan['relay_wait']:
            cp.wait_recv()

    thru = pl.pallas_call(
        arrived, name=name + "_arrived",
        out_shape=tuple(pltpu.HBM(t.shape, t.dtype) for t in thru),
        in_specs=[_HBM] * (2 * n) + [_SEM, _SEM, _SEM, pl.BlockSpec(memory_space=pl.ANY)],
        out_specs=tuple([_HBM] * (2 * n)),
        input_output_aliases={t: t for t in range(2 * n)},
        compiler_params=pltpu.CompilerParams(has_side_effects=_EFFECT),
    )(*thru, *sems, after)

    def forward(*refs):
        plan = plan_fn(refs[:n], refs[n:2 * n], *refs[2 * n:2 * n + 3])
        for cp in plan['relay_start']:
            cp.start()
        refs[-1][...] = jnp.zeros_like(refs[-1])

    outs = pl.pallas_call(
        forward, name=name + "_forward",
        out_shape=(*[pltpu.HBM(t.shape, t.dtype) for t in thru], jax.ShapeDtypeStruct((8, LANES), F32)),
        in_specs=[_HBM] * (2 * n) + [_SEM, _SEM, _SEM],
        out_specs=(*[_HBM] * (2 * n), pl.BlockSpec(memory_space=pltpu.VMEM)),
        input_output_aliases={t: t for t in range(2 * n)},
        compiler_params=pltpu.CompilerParams(has_side_effects=_EFFECT),
    )(*thru, *sems)
    return (*sems, *outs[:-1]), outs[-1]


def _split_wait(plan_fn, state, after, name, with_sources=False):
    sems, thru = state[:3], state[3:]
    n = len(thru) // 2

    def body(*refs):
        plan = plan_fn(refs[:n], refs[n:2 * n], *refs[2 * n:2 * n + 3])
        for cp in plan['local']:
            cp.wait()
        for cp in plan['sends']:
            cp.wait_send()
        for cp in plan['recvs']:
            cp.wait_recv()

    outs = pl.pallas_call(
        body, name=name,
        out_shape=tuple(pltpu.HBM(t.shape, t.dtype) for t in thru),
        in_specs=[_HBM] * (2 * n) + [_SEM, _SEM, _SEM, pl.BlockSpec(memory_space=pl.ANY)],
        out_specs=tuple([_HBM] * (2 * n)),
        input_output_aliases={t: t for t in range(2 * n)},
        compiler_params=pltpu.CompilerParams(has_side_effects=_EFFECT),
    )(*thru, *sems, after)
    return (list(outs[:n]), list(outs[n:])) if with_sources else list(outs[n:])


PAIR_SUM_BLOCK = 768 * 1024


def _pair_sum(mine, theirs, name):
    _, r, c = mine.shape
    rows = r
    while rows * c > PAIR_SUM_BLOCK and rows % 32 == 0:
        rows //= 2

    def body(core_ref, a_ref, b_ref, o_ref):
        o_ref[0] = (a_ref[0].astype(F32) + b_ref[0].astype(F32)).astype(o_ref.dtype)

    return pl.pallas_call(
        body, name=name,
        grid_spec=pltpu.PrefetchScalarGridSpec(
            num_scalar_prefetch=1, grid=(N_CHIP, r // rows),
            in_specs=[pl.BlockSpec((1, rows, c), lambda k, i, core: (2 * k + core[0], i, 0)),
                      pl.BlockSpec((1, rows, c), lambda k, i, core: (k, i, 0))],
            out_specs=pl.BlockSpec((1, rows, c), lambda k, i, core: (k, i, 0))),
        out_shape=jax.ShapeDtypeStruct((N_CHIP, r, c), mine.dtype),
        compiler_params=_cparams(("parallel", "parallel")),
    )(lax.axis_index("c").astype(jnp.int32).reshape(1), mine, theirs)


WEIGHTS = ['norm_w', 'w_in', 's5_lambda_re', 's5_lambda_im', 's5_b_re', 's5_b_im', 's5_c_re', 's5_c_im', 's5_d',
           's5_log_step', 's5_w_glu', 'sgu_ln_w', 'sgu_ln_b', 'sgu_w', 'sgu_b', 'm2_conv_w', 'm2_conv_b', 'm2_dt_bias',
           'm2_a_log', 'm2_d', 'm2_norm_w', 'sc_conv_w', 'merge_b', 'w_branch', 'w_out', 'final_norm_w']
BIG_SHARDED = ['w_in', 'w_branch', 'w_out', 's5_w_glu']
SMALL_SHARDED = ['m2_conv_w', 'sc_conv_w', 'merge_b']
REPLICATED = [n for n in WEIGHTS if n not in BIG_SHARDED + SMALL_SHARDED]
S5_NAMES = ['s5_lambda_re', 's5_lambda_im', 's5_b_re', 's5_b_im', 's5_c_re', 's5_c_im', 's5_d', 's5_log_step']


def _sc_interleave(t):
    lead = t.shape[:-1]
    return jnp.swapaxes(t.reshape(lead + (4, 4, LANES)), -3, -2).reshape(lead + (4 * BW,))


def _pad_in(w):
    z = lambda n: jnp.zeros(w.shape[:-1] + (n,), w.dtype)
    return jnp.concatenate([w[..., 6152:], w[..., 0:1024], w[..., 3072:4096], w[..., 1024:2560], z(512),
                            w[..., 2560:3072], w[..., 4096:4104], z(504), _sc_interleave(w[..., 4104:6152])], axis=-1)


def _unpad_in(g):
    return jnp.concatenate([g[..., C_S5U:C_S5U + 1024], g[..., C_SGU_U:C_SGU_U + 1536], g[..., C_M2Z:C_M2Z + 512],
                            g[..., C_M2X:C_M2X + 1024], g[..., C_DT:C_DT + 8], _sc_interleave(g[..., C_SC:]),
                            g[..., :N_BRANCH * D_MODEL]], axis=-1)


ROW_BLOCK = 8 * LANES


def _pack_rows(tensors, row_mult, batched=False):
    parts = []
    for t in tensors:
        f = t.reshape((t.shape[0], -1) if batched else (1, -1))
        f = jnp.pad(f, ((0, 0), (0, (-f.shape[1]) % ROW_BLOCK)))
        parts.append(f.reshape(f.shape[0], -1, LANES))
    out = jnp.concatenate(parts, axis=1)
    out = jnp.pad(out, ((0, 0), (0, (-out.shape[1]) % row_mult), (0, 0)))
    return out if batched else out[0]


def _unpack_rows(rows, shapes):
    out, r0 = [], 0
    for shp in shapes:
        size = 1
        for s in shp:
            size *= s
        nr = -(-size // ROW_BLOCK) * 8
        out.append(rows[r0:r0 + nr].reshape(-1)[:size].reshape(shp))
        r0 += nr
    return out


def _kernel_col_map():
    m = np.full(IN_PAD, -1, np.int64)
    m[C_MERGE:C_MERGE + 4096] = np.arange(6152, 10248)
    m[C_S5U:C_S5U + 1024] = np.arange(0, 1024)
    m[C_M2X:C_M2X + 1024] = np.arange(3072, 4096)
    m[C_SGU_U:C_SGU_U + 1536] = np.arange(1024, 2560)
    m[C_M2Z:C_M2Z + 512] = np.arange(2560, 3072)
    m[C_DT:C_DT + 8] = np.arange(4096, 4104)
    for j in range(4):
        for kind in range(4):
            k0 = C_SC + 4 * LANES * j + LANES * kind
            m[k0:k0 + LANES] = 4104 + BW * kind + LANES * j + np.arange(LANES)
    return m


def _lane_pieces(sources):
    pieces, cur = [], None
    for lane, src in enumerate(sources):
        key = None if src is None else (src[0], src[1] // LANES, (lane - src[1]) % LANES)
        if cur is not None and key == cur[0]:
            cur[2] = lane + 1
        else:
            if cur is not None and cur[0] is not None:
                pieces.append((*cur[0], cur[1], cur[2]))
            cur = [key, lane, lane + 1]
    if cur is not None and cur[0] is not None:
        pieces.append((*cur[0], cur[1], cur[2]))
    return pieces


def _assemble_block(pieces, load, rows, dtype):
    lane = lax.broadcasted_iota(jnp.int32, (rows, LANES), 1)
    out = None
    for arr, sb, shift, lo, hi in pieces:
        v = load(arr, sb)
        if shift:
            v = pltpu.roll(v, shift, 1)
        if out is None and lo == 0 and hi == LANES:
            out = v
        else:
            out = jnp.where((lane >= lo) & (lane < hi), v, jnp.zeros((rows, LANES), dtype) if out is None else out)
    return jnp.zeros((rows, LANES), dtype) if out is None else out


RELAYOUT_ROWS = 512
SHARD_BLOCKS = -(-SHARD_IN // LANES)


def _load_shard_block(ref, rows):
    def load(j, sb):
        if sb == SHARD_BLOCKS - 1:
            return jnp.broadcast_to(ref[j, :, SHARD_IN - 1:SHARD_IN], (rows, LANES))
        return ref[j, :, sb * LANES:(sb + 1) * LANES]
    return load


def _relayout_w_in(gathered, name):
    kmap = _kernel_col_map()
    dtype = gathered.dtype

    def body(src_ref, o_ref):
        load = _load_shard_block(src_ref, RELAYOUT_ROWS)
        for ob in range(IN_PAD // LANES):
            srcs = [None if kmap[ob * LANES + l] < 0 else (int(kmap[ob * LANES + l]) // SHARD_IN, int(kmap[ob * LANES + l]) % SHARD_IN)
                    for l in range(LANES)]
            o_ref[:, ob * LANES:(ob + 1) * LANES] = _assemble_block(_lane_pieces(srcs), load, RELAYOUT_ROWS, dtype)

    return pl.pallas_call(
        body, name=name, grid=(D_MODEL // RELAYOUT_ROWS,),
        in_specs=[pl.BlockSpec((N_DEV, RELAYOUT_ROWS, SHARD_IN), lambda i: (0, i, 0))],
        out_specs=pl.BlockSpec((RELAYOUT_ROWS, IN_PAD), lambda i: (i, 0)),
        out_shape=jax.ShapeDtypeStruct((D_MODEL, IN_PAD), dtype),
        compiler_params=_cparams(("parallel",)),
    )(gathered)


def _relayout_g_in(gw, name):
    kmap = _kernel_col_map()
    kinv = np.zeros(IN_DIM, np.int64)
    kinv[kmap[kmap >= 0]] = np.nonzero(kmap >= 0)[0]
    dtype = gw.dtype

    def body(src_ref, o_ref):
        load = lambda _, sb: src_ref[:, sb * LANES:(sb + 1) * LANES]
        for j in range(N_DEV):
            for ob in range(SHARD_BLOCKS):
                srcs = [(0, int(kinv[SHARD_IN * j + ob * LANES + l])) if ob * LANES + l < SHARD_IN else None for l in range(LANES)]
                blk = _assemble_block(_lane_pieces(srcs), load, RELAYOUT_ROWS, dtype)
                if ob == SHARD_BLOCKS - 1:
                    o_ref[j, :, SHARD_IN - 1:SHARD_IN] = blk[:, 0:1]
                else:
                    o_ref[j, :, ob * LANES:(ob + 1) * LANES] = blk

    return pl.pallas_call(
        body, name=name, grid=(D_MODEL // RELAYOUT_ROWS,),
        in_specs=[pl.BlockSpec((RELAYOUT_ROWS, IN_PAD), lambda i: (i, 0))],
        out_specs=pl.BlockSpec((N_DEV, RELAYOUT_ROWS, SHARD_IN), lambda i: (0, i, 0)),
        out_shape=jax.ShapeDtypeStruct((N_DEV, D_MODEL, SHARD_IN), dtype),
        compiler_params=_cparams(("parallel",)),
    )(gw)


def _rows128(flat, row_mult=8):
    n = flat.shape[0]
    per = LANES * row_mult
    total = -(-n // per) * per
    return jnp.pad(flat, (0, total - n)).reshape(total // LANES, LANES)


def _pad_lanes(v):
    return jnp.pad(v, (0, LANES - v.shape[0])).reshape(1, LANES)


def _layer_prep(i, p):
    disc, disc_vjp = jax.vjp(_s5_disc, *[p[n][i] for n in S5_NAMES])
    prep = dict(
        nw=p['norm_w'][i].reshape(1, D_MODEL), disc_vjp=disc_vjp,
        s5small=[_block_diag(t).astype(BF16) for t in disc[:4]] + [disc[4], disc[5]],
        sgw=[p['sgu_ln_w'][i].reshape(1, BW), p['sgu_ln_b'][i].reshape(1, BW), p['sgu_w'][i],
             jnp.repeat(p['sgu_b'][i].T, BW // SGU_HEADS, axis=1)],
        cb=p['m2_conv_b'][i].reshape(1, M2_CONV_CH),
        m2w=[_pad_lanes(p['m2_dt_bias'][i]), _pad_lanes(p['m2_a_log'][i]),
             jnp.repeat(p['m2_d'][i], M2_HEAD_DIM).reshape(1, BW), p['m2_norm_w'][i].reshape(1, BW)])
    touch = [t[0, 0].astype(F32) for t in prep['s5small']] + [prep['sgw'][3][0, 0], prep['m2w'][2][0, 0]]
    return prep, sum(touch[1:], touch[0])


def _layer_fwd(x, h, i, prep, w_in, other_weights, before_merge=None):
    proj = _matmul(h, w_in, 1, 0, F32, 2048, IN_PAD // 8, 1024, f"proj{i}")
    full = dict(other_weights(proj), w_in=w_in)
    s5w = prep['s5small'] + [full['s5_w_glu']]
    ys, sre, sim = _s5_fwd(proj, *s5w, f"s5_fwd{i}")
    ys = _sgu_fwd(proj, ys, *prep['sgw'], f"sgu_fwd{i}")
    cw = full['m2_conv_w']
    xa = _m2_conv_fwd(proj, cw, prep['cb'], f"m2conv_fwd{i}")
    ys, s_in = _ssd_fwd(proj, ys, xa, *prep['m2w'], f"ssd_fwd{i}")
    scw = full['sc_conv_w']
    ys = _sc_fwd(proj, ys, scw, f"sc_fwd{i}")
    mb = full['merge_b'].reshape(N_BRANCH, 1, D_MODEL)
    if before_merge is not None:
        mb = mb + before_merge(ys)[0, 0]
    merged = _merge_fwd(proj, ys, mb, full['w_branch'], f"merge_fwd{i}")
    x_new = _matmul(merged, full['w_out'], 1, 0, F32, 1024, 1024, 1024, f"out{i}", residual=x)
    saved = dict(x=x, nw=prep['nw'], h=h, proj=proj, disc_vjp=prep['disc_vjp'], s5w=s5w, sre=sre, sim=sim, sgw=prep['sgw'],
                 cw=cw, cb=prep['cb'], xa=xa, m2w=prep['m2w'], s_in=s_in, scw=scw, ys=ys, mb=mb, merged=merged)
    return x_new, saved, full


def _layer_bwd(dx_out, i, sv, full, on_large_grads=None, after_dh=None):
    g = {}
    proj = sv['proj']
    dm = _matmul(dx_out, full['w_out'], 1, 1, F32, 1024, 1024, 1024, f"dmerged{i}")
    g['w_out'] = _matmul(sv['merged'], dx_out, 0, 0, BF16, 1024, 1024, 1024, f"gw_out{i}")
    dys, dproj, g['w_branch'], dmb = _merge_bwd(proj, sv['ys'], dm, sv['mb'], full['w_branch'], f"merge_bwd{i}")
    g['merge_b'] = dmb.reshape(N_BRANCH, D_MODEL)
    dproj, dbbre, dbbim, dcre, dcim, da, dd, dwg = _s5_bwd(proj, dproj, dys, sv['sre'], sv['sim'], *sv['s5w'], f"s5_bwd{i}")
    g['s5_dense'] = (dbbre, dbbim, dcre, dcim, da, dd)
    g['s5_w_glu'] = dwg.astype(BF16)
    dproj, dlw, dlb, g['sgu_w'], dbias = _sgu_bwd(proj, dproj, dys, *sv['sgw'], f"sgu_bwd{i}")
    g['sgu_ln_w'], g['sgu_ln_b'] = dlw[0], dlb[0]
    g['sgu_b'] = dbias.reshape(SGU_CHUNK, SGU_HEADS, BW // SGU_HEADS).sum(-1).T
    dproj, dxa, ddtb, dal, ddf, dnw = _ssd_bwd(proj, dproj, sv['xa'], dys, sv['s_in'], *sv['m2w'], f"ssd_bwd{i}")
    dproj, g['m2_conv_w'], dcb = _m2_conv_bwd(proj, dproj, dxa, sv['cw'], sv['cb'], f"m2conv_bwd{i}")
    g['m2_conv_b'], g['m2_norm_w'] = dcb[0], dnw[0]
    g['m2_dt_bias'], g['m2_a_log'] = ddtb[0, :M2_HEADS], dal[0, :M2_HEADS]
    g['m2_d'] = ddf.reshape(M2_HEADS, M2_HEAD_DIM).sum(-1)
    dproj, g['sc_conv_w'] = _sc_bwd(proj, dproj, dys, sv['scw'], f"sc_bwd{i}")
    g['w_in'] = _matmul(sv['h'], dproj, 0, 0, BF16, 1024, IN_PAD // 4, 2048, f"gw_in{i}")
    tok = on_large_grads(g) if on_large_grads else None
    dh = _matmul(dproj, full['w_in'], 1, 1, F32, 1024, 1024, IN_PAD // 4, f"dh{i}", after=tok)
    nw = sv['nw'] if after_dh is None else sv['nw'] + after_dh(dh)[0, 0]
    dx_in, dnw_l = _rmsnorm_bwd(sv['x'], nw, dh, dx_out, f"rms_bwd{i}")
    g['norm_w'] = dnw_l[0]
    return dx_in, g


def _split8(t, axis):
    shp = t.shape
    t = t.reshape(shp[:axis] + (N_DEV, shp[axis] // N_DEV) + shp[axis + 1:])
    return jnp.moveaxis(t, axis, 0)


def _join8(t, axis):
    t = jnp.moveaxis(t, 0, axis)
    shp = t.shape
    return t.reshape(shp[:axis] + (shp[axis] * shp[axis + 1],) + shp[axis + 2:])


SHARD_AXIS = {'w_in': 2, 'w_branch': 3, 'w_out': 1, 's5_w_glu': 1, 'm2_conv_w': 2, 'sc_conv_w': 2, 'merge_b': 2}


OTHER_BIG = [n for n in BIG_SHARDED if n != 'w_in']


def _other_weights(gathered):
    return {n: _join8(t, SHARD_AXIS[n] - 1) for n, t in zip(OTHER_BIG, gathered)}


def _layer_grad_blocks(g, i):
    blocks = [_relayout_g_in(g[n], f"relayout_g_in{i}") if n == 'w_in' else _split8(g[n], SHARD_AXIS[n] - 1) for n in BIG_SHARDED]
    return [b.reshape(N_DEV, -1, b.shape[-1]) for b in blocks]


def _pair_start(blocks, tag):
    shapes = [(N_CHIP,) + b.shape[1:] for b in blocks]
    return _split_start(_plan_pair, blocks, shapes, N_CHIP * len(blocks), f"pair{tag}_start")


def _pair_sums(state, after, tag):
    mine, theirs = _split_wait(_plan_pair, state, after, f"pair{tag}_wait", with_sources=True)
    return [_pair_sum(b, t, f"pair_sum{tag}_{k}") for k, (b, t) in enumerate(zip(mine, theirs))]


def _chips_start(sums, tag, after=None):
    return _split_start(_plan_chips, sums, [s.shape for s in sums], 3 * len(sums), f"chips{tag}_start", after)


def kernel(x, norm_w, w_in, s5_lambda_re, s5_lambda_im, s5_b_re, s5_b_im, s5_c_re, s5_c_im, s5_d, s5_log_step, s5_w_glu, sgu_ln_w, sgu_ln_b, sgu_w, sgu_b, m2_conv_w, m2_conv_b, m2_dt_bias, m2_a_log, m2_d, m2_norm_w, sc_conv_w, merge_b, w_branch, w_out, final_norm_w, loss_target, m_norm_w, m_w_in, m_s5_lambda_re, m_s5_lambda_im, m_s5_b_re, m_s5_b_im, m_s5_c_re, m_s5_c_im, m_s5_d, m_s5_log_step, m_s5_w_glu, m_sgu_ln_w, m_sgu_ln_b, m_sgu_w, m_sgu_b, m_m2_conv_w, m_m2_conv_b, m_m2_dt_bias, m_m2_a_log, m_m2_d, m_m2_norm_w, m_sc_conv_w, m_merge_b, m_w_branch, m_w_out, m_final_norm_w, v_norm_w, v_w_in, v_s5_lambda_re, v_s5_lambda_im, v_s5_b_re, v_s5_b_im, v_s5_c_re, v_s5_c_im, v_s5_d, v_s5_log_step, v_s5_w_glu, v_sgu_ln_w, v_sgu_ln_b, v_sgu_w, v_sgu_b, v_m2_conv_w, v_m2_conv_b, v_m2_dt_bias, v_m2_a_log, v_m2_d, v_m2_norm_w, v_sc_conv_w, v_merge_b, v_w_branch, v_w_out, v_final_norm_w):
    loc = locals()
    p = {n: loc[n] for n in WEIGHTS}
    mom = {n: loc['m_' + n] for n in WEIGHTS}
    vel = {n: loc['v_' + n] for n in WEIGHTS}

    small_sizes = [p[n].size for n in SMALL_SHARDED]
    small_pack = _rows128(jnp.concatenate([p[n].reshape(-1) for n in SMALL_SHARDED]))
    first = [p['w_in'][0].astype(BF16)]
    gath_first, tok = _split_start(_plan_gather, first, [(N_DEV,) + first[0].shape], 7, "gather_w_in0_start")
    shards = ([(p[n][0] + tok[0, 0]).astype(BF16) for n in OTHER_BIG] + [small_pack + tok[0, 0]]
              + [(p[n][1] + tok[0, 0]).astype(BF16) for n in BIG_SHARDED])
    gath, tok = _split_start(_plan_gather, shards, [(N_DEV,) + t.shape for t in shards], 7 * len(shards), "gather_start")

    def relayed(lo, hi, after, name, started=None):
        started = gath if started is None else started
        n = (len(started) - 3) // 2
        sems, srcs, lands = started[:3], started[3:3 + n], started[3 + n:]
        plan = functools.partial(_plan_gather, first=lo)
        state, tok = _split_relay(plan, (*sems, *srcs[lo:hi], *lands[lo:hi]), after, name + "_relay")
        return (plan, state, name), tok

    def arrived(relay, after):
        plan, state, name = relay
        return _split_wait(plan, state, after, name + "_wait")

    def gathered(lo, hi, after, name, started=None):
        relay, tok = relayed(lo, hi, after, name, started)
        return arrived(relay, tok)

    later = dict(p, **{n: p[n] + tok[0, 0] for n in ('norm_w', 's5_log_step', 'sgu_b', 'm2_d')})
    preps = [_layer_prep(i, later) for i in range(DEPTH)]
    h0 = _rmsnorm_fwd(x[0], preps[0][0]['nw'], "rms_fwd0")
    got = gathered(0, 1, tok + (preps[0][1] + preps[1][1] + h0[0, 0].astype(F32)), "gather_w_in0", gath_first)
    small_full = {}

    def other_weights0(proj):
        got = gathered(0, 4, proj, "gather_rest0")
        small_all, off = got[-1].reshape(N_DEV, -1), 0
        for n, sz in zip(SMALL_SHARDED, small_sizes):
            small_full[n] = _join8(small_all[:, off:off + sz].reshape((N_DEV,) + p[n].shape), SHARD_AXIS[n])
            off += sz
        return dict(_other_weights(got[:-1]), **{n: small_full[n][0] for n in SMALL_SHARDED})

    saved, layer_g, full = [None] * DEPTH, [None] * DEPTH, [None] * DEPTH
    relay1 = []

    def relay_layer1(ys):
        relay, tok = relayed(4, 8, ys, "gather1")
        relay1.append(relay)
        return tok

    xs, saved[0], full[0] = _layer_fwd(x[0], h0, 0, preps[0][0], _relayout_w_in(got[0], "relayout_w_in0"), other_weights0,
                                       relay_layer1)
    h1 = _rmsnorm_fwd(xs, preps[1][0]['nw'], "rms_fwd1")
    got = arrived(relay1[0], h1)
    xs, saved[1], full[1] = _layer_fwd(
        xs, h1, 1, preps[1][0], _relayout_w_in(got[0], "relayout_w_in1"),
        lambda proj: dict(_other_weights(got[1:]), **{n: small_full[n][1] for n in SMALL_SHARDED}))
    loss_row, dx, dfw = _loss_head(xs, final_norm_w.reshape(1, D_MODEL), loss_target[0])
    loss = lax.psum(loss_row[0, 0], ("x", "y", "c"))
    loss, dx = lax.optimization_barrier((loss, dx))
    pairs, scat, sent0 = [None] * DEPTH, [None] * DEPTH, []

    def start_pairs1(g):
        pairs[1], tok = _pair_start(_layer_grad_blocks(g, 1), 1)
        return tok

    def send_chip_sums1(dh):
        scat[1], tok = _chips_start(_pair_sums(pairs[1], dh, 1), 1)
        return tok

    def send_all0(g):
        pairs[0], tok = _pair_start(_layer_grad_blocks(g, 0), 0)
        scat[0], tok = _chips_start(_pair_sums(pairs[0], tok, 0), 0)
        sent0.append(tok)
        return tok

    dx, layer_g[1] = _layer_bwd(dx, 1, saved[1], full[1], on_large_grads=start_pairs1, after_dh=send_chip_sums1)
    dx, layer_g[0] = _layer_bwd(dx, 0, saved[0], full[0], on_large_grads=send_all0)
    for i in range(DEPTH):
        dense = layer_g[i].pop('s5_dense')
        blocks = tuple(_diag_blocks(t, after=sent0[0]) for t in dense[:4])
        layer_g[i].update(zip(S5_NAMES, saved[i]['disc_vjp'](blocks + (dense[4] + sent0[0][0, 0], dense[5]))))
    grads = {n: jnp.stack([layer_g[i][n] for i in range(DEPTH)]) for n in SMALL_SHARDED + REPLICATED if n != 'final_norm_w'}
    grads['final_norm_w'] = dfw[0]

    out_g, out_d, out_m, out_v = {}, {}, {}, {}
    repl_rows = _pack_rows([grads[n] for n in REPLICATED], 8 * N_DEV)
    rr = repl_rows.shape[0] // N_DEV
    shard_rows = _pack_rows([_split8(grads[n], SHARD_AXIS[n]) for n in SMALL_SHARDED], 8, batched=True)
    rs = shard_rows.shape[1]
    small_g = jnp.concatenate([shard_rows, repl_rows.reshape(N_DEV, rr, LANES)], axis=1)
    all_to_all, all_gather = functools.partial(_plan_direct, gather=False), functools.partial(_plan_direct, gather=True)
    small_state, tok = _split_start(all_to_all, [small_g], [small_g.shape], N_DEV - 1, "scatter_small_start")
    landed1 = _split_wait(_plan_chips, scat[1], tok, "chips1_wait")
    landed0 = _split_wait(_plan_chips, scat[0], landed1[0], "chips0_wait")

    def big_adamw(n, after=None):
        k, shp = BIG_SHARDED.index(n), p[n].shape
        if n == 'w_in':
            return _adamw_w_in([landed0[k], landed1[k]], p[n], mom[n], vel[n], "adamw_w_in", after)
        c = shp[-1]
        r = p[n].size // (DEPTH * c)
        res = _adamw([landed0[k], landed1[k]], *[d[n].reshape(DEPTH, r, c) for d in (p, mom, vel)],
                     {'w_branch': 512, 'w_out': 128, 's5_w_glu': 64}[n], "adamw_" + n)
        return [o.reshape(shp) for o in res]

    for n in OTHER_BIG:
        out_g[n], out_d[n], out_m[n], out_v[n] = big_adamw(n)
    updated = sum(out_d[n].reshape(-1)[0] for n in OTHER_BIG).reshape(1, 1)
    small_sum = _slot_sum(_split_wait(all_to_all, small_state, updated, "scatter_small_wait")[0], "sum_small")
    repl_part = small_sum[rs:]
    repl_state, tok = _split_start(all_gather, [repl_part], [(N_DEV,) + repl_part.shape], N_DEV - 1, "gather_small_start")
    out_g['w_in'], out_d['w_in'], out_m['w_in'], out_v['w_in'] = big_adamw('w_in', tok)
    repl_all = _split_wait(all_gather, repl_state, out_d['w_in'], "gather_small_wait")[0].reshape(N_DEV * rr, LANES)
    g_all = jnp.concatenate([small_sum[:rs], repl_all], axis=0)
    names = SMALL_SHARDED + REPLICATED
    pieces = (_unpack_rows(g_all[:rs], [p[n].shape for n in SMALL_SHARDED])
              + _unpack_rows(g_all[rs:], [p[n].shape for n in REPLICATED]))
    out_g.update(zip(names, pieces))
    res = _adamw_many(*[[_memory_view(n, d[n]) for n in names] for d in (out_g, p, mom, vel)], "adamw_small")
    for r, dst in zip(res, (out_d, out_m, out_v)):
        dst.update({n: _memory_view(n, t) for n, t in zip(names, r)})
    return (loss, dx[None], *[out_g[n] for n in WEIGHTS], *[out_d[n] for n in WEIGHTS],
            *[out_m[n] for n in WEIGHTS], *[out_v[n] for n in WEIGHTS])
```

```python
import functools

import jax
import jax.numpy as jnp
import numpy as np
from jax import lax
from jax.experimental import pallas as pl
from jax.experimental.pallas import tpu as pltpu

F32 = jnp.float32
BF16 = jnp.bfloat16

N_DEV = 8
SEQ = 2048
D_MODEL = 1024
DEPTH = 2
BW = 512
N_BRANCH = 4
EPS = 1e-6
S5_GROUPS, S5_STATE, S5_P = 32, 64, 16
S5_CH = S5_GROUPS * S5_STATE
SGU_CHUNK, SGU_HEADS = 128, 8
M2_HEADS, M2_HEAD_DIM, M2_STATE, M2_CHUNK, M2_CONV = 8, 64, 128, 128, 4
M2_CONV_CH = 1024
SC_CONV = 3
IN_DIM = 10248
IN_PAD = 11264
C_MERGE = 0
C_S5U, C_S5G = 4096, 4608
C_M2X = 5120
C_SGU_U, C_SGU_V, C_SGU_G = 6144, 6656, 7168
C_M2Z, C_DT = 8192, 8704
C_SC = 9216
SHARD_IN = IN_DIM // N_DEV

ADAM_LR, ADAM_B1, ADAM_B2, ADAM_EPS, ADAM_WD, ADAM_STEP = 0.001, 0.9, 0.999, 1e-08, 0.01, 10

VMEM_LIMIT = 56 * 1024 * 1024
LANES = 128

MESH = pl.DeviceIdType.MESH


def _cparams(sem=None, **kw):
    return pltpu.CompilerParams(dimension_semantics=sem, vmem_limit_bytes=VMEM_LIMIT, **kw)


def _dg(a, b, ca, cb, precision=None):
    return lax.dot_general(a, b, (((ca,), (cb,)), ((), ())), precision=precision,
                           preferred_element_type=F32)


@functools.partial(jax.custom_vjp, nondiff_argnums=(2, 3))
def _bdot(a, b, ca, cb):
    return _dg(a.astype(BF16), b.astype(BF16), ca, cb)


def _bdot_fwd(a, b, ca, cb):
    return _bdot(a, b, ca, cb), (a, b)


def _bdot_bwd(ca, cb, res, g):
    a, b = res
    gb, ab, bb = g.astype(BF16), a.astype(BF16), b.astype(BF16)
    da = _dg(gb, bb, 1, 1 - cb) if ca == 1 else _dg(bb, gb, 1 - cb, 1)
    db = _dg(ab, gb, 1 - ca, 0) if cb == 0 else _dg(gb, ab, 0, 1 - ca)
    return da.astype(a.dtype), db.astype(b.dtype)


_bdot.defvjp(_bdot_fwd, _bdot_bwd)


def _rms(x, w):
    return x * lax.rsqrt(jnp.mean(x * x, axis=-1, keepdims=True) + EPS) * w


def _silu(x):
    return x * jax.nn.sigmoid(x)


def _gelu(x):
    return 0.5 * x * (1.0 + jnp.tanh(0.7978845608028654 * (x + 0.044715 * (x * x * x))))


def _softplus(x):
    return jnp.maximum(x, 0.0) + jnp.log1p(jnp.exp(-jnp.abs(x)))


def _shift_down(x, s):
    if s == 0:
        return x
    row = lax.broadcasted_iota(jnp.int32, x.shape, 0)
    return jnp.where(row >= s, pltpu.roll(x, s, 0), 0.0)


def _shift_up(x, s):
    if s == 0:
        return x
    n = x.shape[0]
    row = lax.broadcasted_iota(jnp.int32, x.shape, 0)
    return jnp.where(row < n - s, pltpu.roll(x, n - s, 0), 0.0)


def _matmul(a, b, ca, cb, out_dtype, tm, tn, tk, name, residual=None, after=None):
    m = a.shape[1 - ca]
    k = a.shape[ca]
    n = b.shape[1 - cb]
    assert b.shape[cb] == k and m % tm == 0 and n % tn == 0 and k % tk == 0
    nk = k // tk
    a_spec = pl.BlockSpec((tm, tk), lambda i, j, kk: (i, kk)) if ca == 1 else pl.BlockSpec((tk, tm), lambda i, j, kk: (kk, i))
    b_spec = pl.BlockSpec((tk, tn), lambda i, j, kk: (kk, j)) if cb == 0 else pl.BlockSpec((tn, tk), lambda i, j, kk: (j, kk))
    o_spec = pl.BlockSpec((tm, tn), lambda i, j, kk: (i, j))
    has_res = residual is not None

    def body(*refs):
        refs = refs[:2 + has_res] + refs[2 + has_res + (after is not None):]
        if has_res:
            a_ref, b_ref, r_ref, o_ref, acc = refs
        else:
            a_ref, b_ref, o_ref, acc = refs
        kk = pl.program_id(2)
        part = _dg(a_ref[...].astype(BF16), b_ref[...].astype(BF16), ca, cb)
        if nk == 1:
            o_ref[...] = (part + r_ref[...] if has_res else part).astype(out_dtype)
            return

        @pl.when(kk == 0)
        def _():
            acc[...] = part

        @pl.when(kk > 0)
        def _():
            acc[...] += part

        @pl.when(kk == nk - 1)
        def _():
            r = acc[...]
            if has_res:
                r = r + r_ref[...]
            o_ref[...] = r.astype(out_dtype)

    ins = [a, b] + ([residual] if has_res else []) + ([after] if after is not None else [])
    specs = [a_spec, b_spec] + ([o_spec] if has_res else []) + ([pl.BlockSpec(memory_space=pl.ANY)] if after is not None else [])
    return pl.pallas_call(
        body, name=name, grid=(m // tm, n // tn, nk), in_specs=specs, out_specs=o_spec,
        out_shape=jax.ShapeDtypeStruct((m, n), out_dtype),
        scratch_shapes=[pltpu.VMEM((tm, tn) if nk > 1 else (8, LANES), F32)],
        compiler_params=_cparams(("parallel", "parallel", "arbitrary")),
    )(*ins)


ROW_TILE = 512


def _rmsnorm_fwd(x, w, name):
    def body(x_ref, w_ref, o_ref):
        o_ref[...] = _rms(x_ref[...], w_ref[...]).astype(BF16)

    return pl.pallas_call(
        body, name=name, grid=(SEQ // ROW_TILE,),
        in_specs=[pl.BlockSpec((ROW_TILE, D_MODEL), lambda i: (i, 0)), pl.BlockSpec((1, D_MODEL), lambda i: (0, 0))],
        out_specs=pl.BlockSpec((ROW_TILE, D_MODEL), lambda i: (i, 0)),
        out_shape=jax.ShapeDtypeStruct((SEQ, D_MODEL), BF16),
        compiler_params=_cparams(("parallel",)),
    )(x, w)


def _rmsnorm_bwd(x, w, dh, dres, name):
    def body(x_ref, w_ref, dh_ref, dres_ref, dx_ref, dw_ref):
        _, vjp = jax.vjp(_rms, x_ref[...], w_ref[...])
        dx, dw = vjp(dh_ref[...])
        dx_ref[...] = dx + dres_ref[...]

        @pl.when(pl.program_id(0) == 0)
        def _():
            dw_ref[...] = dw

        @pl.when(pl.program_id(0) > 0)
        def _():
            dw_ref[...] += dw

    tile = pl.BlockSpec((ROW_TILE, D_MODEL), lambda i: (i, 0))
    vec = pl.BlockSpec((1, D_MODEL), lambda i: (0, 0))
    return pl.pallas_call(
        body, name=name, grid=(SEQ // ROW_TILE,),
        in_specs=[tile, vec, tile, tile], out_specs=[tile, vec],
        out_shape=[jax.ShapeDtypeStruct((SEQ, D_MODEL), F32), jax.ShapeDtypeStruct((1, D_MODEL), F32)],
        compiler_params=_cparams(("arbitrary",)),
    )(x, w, dh, dres)


def _loss_head(x, w, target):
    def body(x_ref, w_ref, t_ref, loss_ref, dx_ref, dw_ref):
        tgt = t_ref[...]

        def f(xv, wv):
            err = _rms(xv, wv) - tgt
            return 0.5 * jnp.sum(jnp.mean(err * err, axis=-1))

        loss, vjp = jax.vjp(f, x_ref[...], w_ref[...])
        dx, dw = vjp(jnp.ones((), F32))
        dx_ref[...] = dx
        lrow = jnp.full((1, LANES), loss, F32)

        @pl.when(pl.program_id(0) == 0)
        def _():
            dw_ref[...] = dw
            loss_ref[...] = lrow

        @pl.when(pl.program_id(0) > 0)
        def _():
            dw_ref[...] += dw
            loss_ref[...] += lrow

    tile = pl.BlockSpec((ROW_TILE, D_MODEL), lambda i: (i, 0))
    vec = pl.BlockSpec((1, D_MODEL), lambda i: (0, 0))
    return pl.pallas_call(
        body, name="loss_head", grid=(SEQ // ROW_TILE,),
        in_specs=[tile, vec, tile], out_specs=[pl.BlockSpec((1, LANES), lambda i: (0, 0)), tile, vec],
        out_shape=[jax.ShapeDtypeStruct((1, LANES), F32), jax.ShapeDtypeStruct((SEQ, D_MODEL), F32),
                   jax.ShapeDtypeStruct((1, D_MODEL), F32)],
        compiler_params=_cparams(("arbitrary",)),
    )(x, w, target)


def _out_norm(merged, w_out, x, nw, name):
    def body(m_ref, w_ref, x_ref, nw_ref, xo_ref, h_ref):
        x_new = _dg(m_ref[...].astype(BF16), w_ref[...].astype(BF16), 1, 0) + x_ref[...]
        xo_ref[...] = x_new
        h_ref[...] = _rms(x_new, nw_ref[...]).astype(BF16)

    tile = pl.BlockSpec((ROW_TILE, D_MODEL), lambda i: (i, 0))
    return pl.pallas_call(
        body, name=name, grid=(SEQ // ROW_TILE,),
        in_specs=[tile, pl.BlockSpec((D_MODEL, D_MODEL), lambda i: (0, 0)), tile, pl.BlockSpec((1, D_MODEL), lambda i: (0, 0))],
        out_specs=[tile, tile],
        out_shape=[jax.ShapeDtypeStruct((SEQ, D_MODEL), F32), jax.ShapeDtypeStruct((SEQ, D_MODEL), BF16)],
        compiler_params=_cparams(("parallel",)),
    )(merged, w_out, x, nw)


def _out_loss(merged, w_out, x, w, target, name):
    def body(m_ref, wo_ref, x_ref, w_ref, t_ref, loss_ref, dx_ref, dw_ref):
        x_new = _dg(m_ref[...].astype(BF16), wo_ref[...].astype(BF16), 1, 0) + x_ref[...]
        tgt = t_ref[...]

        def f(xv, wv):
            err = _rms(xv, wv) - tgt
            return 0.5 * jnp.sum(jnp.mean(err * err, axis=-1))

        loss, vjp = jax.vjp(f, x_new, w_ref[...])
        dx, dw = vjp(jnp.ones((), F32))
        dx_ref[...] = dx
        lrow = jnp.full((1, LANES), loss, F32)

        @pl.when(pl.program_id(0) == 0)
        def _():
            dw_ref[...] = dw
            loss_ref[...] = lrow

        @pl.when(pl.program_id(0) > 0)
        def _():
            dw_ref[...] += dw
            loss_ref[...] += lrow

    tile = pl.BlockSpec((ROW_TILE, D_MODEL), lambda i: (i, 0))
    vec = pl.BlockSpec((1, D_MODEL), lambda i: (0, 0))
    return pl.pallas_call(
        body, name=name, grid=(SEQ // ROW_TILE,),
        in_specs=[tile, pl.BlockSpec((D_MODEL, D_MODEL), lambda i: (0, 0)), tile, vec, tile],
        out_specs=[pl.BlockSpec((1, LANES), lambda i: (0, 0)), tile, vec],
        out_shape=[jax.ShapeDtypeStruct((1, LANES), F32), jax.ShapeDtypeStruct((SEQ, D_MODEL), F32),
                   jax.ShapeDtypeStruct((1, D_MODEL), F32)],
        compiler_params=_cparams(("arbitrary",)),
    )(merged, w_out, x, w, target)


S5_T = 256
S5_BLOCKS = [(slice(j * 256, (j + 1) * 256), slice(j * 1024, (j + 1) * 1024)) for j in range(2)]


def _s5_post(ypre, gate, wglu):
    y = _gelu(ypre)
    y = y * jax.nn.sigmoid(_bdot(y, wglu, 1, 0))
    return y * _silu(gate)


def _s5_fwd(proj, bbre, bbim, cre, cim, a2, dvec, wglu, name):
    def body(u_ref, g_ref, bbre_ref, bbim_ref, cre_ref, cim_ref, a_ref, d_ref, wg_ref, o_ref, sre_ref, sim_ref, st):
        @pl.when(pl.program_id(0) == 0)
        def _():
            st[...] = jnp.zeros_like(st)

        u = u_ref[...]
        ub = u.astype(BF16)
        for us, ss in S5_BLOCKS:
            sre_ref[:, ss] = _dg(ub[:, us], bbre_ref[us, ss], 1, 0)
            sim_ref[:, ss] = _dg(ub[:, us], bbim_ref[us, ss], 1, 0)
        ar, ai = a_ref[0:1, :], a_ref[1:2, :]

        def step(t, carry):
            sr, si = carry
            nr = ar * sr - ai * si + sre_ref[pl.ds(t, 1), :]
            ni = ar * si + ai * sr + sim_ref[pl.ds(t, 1), :]
            sre_ref[pl.ds(t, 1), :] = nr
            sim_ref[pl.ds(t, 1), :] = ni
            return nr, ni

        sr, si = lax.fori_loop(0, S5_T, step, (st[0:1, :], st[1:2, :]), unroll=8)
        st[0:1, :] = sr
        st[1:2, :] = si
        ypre = jnp.concatenate(
            [_dg(sre_ref[:, ss].astype(BF16), cre_ref[ss, us], 1, 0) - _dg(sim_ref[:, ss].astype(BF16), cim_ref[ss, us], 1, 0)
             for us, ss in S5_BLOCKS], axis=1) + d_ref[...] * u
        o_ref[0] = _s5_post(ypre, g_ref[...], wg_ref[...]).astype(BF16)

    full = lambda shape: pl.BlockSpec(shape, lambda c: (0, 0))
    return pl.pallas_call(
        body, name=name, grid=(SEQ // S5_T,),
        in_specs=[pl.BlockSpec((S5_T, BW), lambda c: (c, C_S5U // BW)), pl.BlockSpec((S5_T, BW), lambda c: (c, C_S5G // BW)),
                  full((BW, S5_CH)), full((BW, S5_CH)), full((S5_CH, BW)), full((S5_CH, BW)),
                  full((2, S5_CH)), full((1, BW)), full((BW, BW))],
        out_specs=[pl.BlockSpec((1, S5_T, BW), lambda c: (0, c, 0)), pl.BlockSpec((S5_T, S5_CH), lambda c: (c, 0)),
                   pl.BlockSpec((S5_T, S5_CH), lambda c: (c, 0))],
        out_shape=[jax.ShapeDtypeStruct((N_BRANCH, SEQ, BW), BF16), jax.ShapeDtypeStruct((SEQ, S5_CH), F32),
                   jax.ShapeDtypeStruct((SEQ, S5_CH), F32)],
        scratch_shapes=[pltpu.VMEM((2, S5_CH), F32)],
        compiler_params=_cparams(("arbitrary",)),
    )(proj, proj, bbre, bbim, cre, cim, a2, dvec, wglu)


def _s5_bwd(proj, dproj, dout, sre, sim, bbre, bbim, cre, cim, a2, dvec, wglu, name):
    nc = SEQ // S5_T

    def body(u_ref, g_ref, do_ref, sre_ref, sim_ref, pre_ref, pim_ref, bbre_ref, bbim_ref, cre_ref, cim_ref, a_ref,
             d_ref, wg_ref, dproj_in, dp_ref, dbbre_ref, dbbim_ref, dcre_ref, dcim_ref, da_ref, dd_ref, dwg_ref,
             gre, gim, st):
        c = nc - 1 - pl.program_id(0)

        @pl.when(pl.program_id(0) == 0)
        def _():
            st[...] = jnp.zeros_like(st)
            for r in (dbbre_ref, dbbim_ref, dcre_ref, dcim_ref, da_ref, dd_ref, dwg_ref):
                r[...] = jnp.zeros_like(r)

        u = u_ref[...]
        s_re, s_im = sre_ref[...], sim_ref[...]

        def head(s_res, s_ims, cres, cims, dv, uv, gv, wg):
            ypre = jnp.concatenate([_bdot(sr, cr, 1, 0) - _bdot(si, ci, 1, 0)
                                    for sr, si, cr, ci in zip(s_res, s_ims, cres, cims)], axis=1) + dv * uv
            return _s5_post(ypre, gv, wg)

        _, vjp = jax.vjp(head, [sre_ref[:, ss] for _, ss in S5_BLOCKS], [sim_ref[:, ss] for _, ss in S5_BLOCKS],
                         [cre_ref[ss, us].astype(F32) for us, ss in S5_BLOCKS],
                         [cim_ref[ss, us].astype(F32) for us, ss in S5_BLOCKS],
                         d_ref[...], u, g_ref[...], wg_ref[...].astype(F32))
        ds_res, ds_ims, dcres, dcims, dd, du_d, dgate, dwg = vjp(do_ref[0])
        for k, (us, ss) in enumerate(S5_BLOCKS):
            dcre_ref[ss, us] += dcres[k]
            dcim_ref[ss, us] += dcims[k]
            gre[:, ss] = ds_res[k]
            gim[:, ss] = ds_ims[k]
        dd_ref[...] += dd
        dwg_ref[...] += dwg
        dp_ref[:, BW:] = dgate.astype(BF16)
        ar, ai = a_ref[0:1, :], a_ref[1:2, :]

        def step(i, carry):
            t = S5_T - 1 - i
            gr, gi = carry
            nr = gre[pl.ds(t, 1), :] + gr
            ni = gim[pl.ds(t, 1), :] + gi
            gre[pl.ds(t, 1), :] = nr
            gim[pl.ds(t, 1), :] = ni
            return ar * nr + ai * ni, ar * ni - ai * nr

        gr, gi = lax.fori_loop(0, S5_T, step, (st[0:1, :], st[1:2, :]), unroll=8)
        st[0:1, :] = gr
        st[1:2, :] = gi
        g_re, g_im = gre[...], gim[...]
        first = jnp.where(c > 0, 1.0, 0.0)
        row = lax.broadcasted_iota(jnp.int32, (S5_T, S5_CH), 0)
        p_re = jnp.where(row == 0, pre_ref[7:8, :] * first, pltpu.roll(s_re, 1, 0))
        p_im = jnp.where(row == 0, pim_ref[7:8, :] * first, pltpu.roll(s_im, 1, 0))
        da_ref[0:1, :] += jnp.sum(g_re * p_re + g_im * p_im, axis=0, keepdims=True)
        da_ref[1:2, :] += jnp.sum(g_im * p_re - g_re * p_im, axis=0, keepdims=True)
        ub, grb, gib = u.astype(BF16), g_re.astype(BF16), g_im.astype(BF16)
        du_s = []
        for us, ss in S5_BLOCKS:
            dbbre_ref[us, ss] += _dg(ub[:, us], grb[:, ss], 0, 0)
            dbbim_ref[us, ss] += _dg(ub[:, us], gib[:, ss], 0, 0)
            du_s.append(_dg(grb[:, ss], bbre_ref[us, ss], 1, 1) + _dg(gib[:, ss], bbim_ref[us, ss], 1, 1))
        dp_ref[:, :BW] = (du_d + jnp.concatenate(du_s, axis=1)).astype(BF16)

    full = lambda shape: pl.BlockSpec(shape, lambda i: (0, 0))
    rev = lambda w, col=0: pl.BlockSpec((S5_T, w), lambda i: (nc - 1 - i, col))
    prev = pl.BlockSpec((8, S5_CH), lambda i: (jnp.maximum((nc - 1 - i) * (S5_T // 8) - 1, 0), 0))
    return pl.pallas_call(
        body, name=name, grid=(nc,),
        in_specs=[rev(BW, C_S5U // BW), rev(BW, C_S5G // BW), pl.BlockSpec((1, S5_T, BW), lambda i: (0, nc - 1 - i, 0)),
                  rev(S5_CH), rev(S5_CH), prev, prev,
                  full((BW, S5_CH)), full((BW, S5_CH)), full((S5_CH, BW)), full((S5_CH, BW)),
                  full((2, S5_CH)), full((1, BW)), full((BW, BW)), pl.BlockSpec(memory_space=pl.ANY)],
        out_specs=[rev(2 * BW, C_S5U // (2 * BW)), full((BW, S5_CH)), full((BW, S5_CH)), full((S5_CH, BW)), full((S5_CH, BW)),
                   full((2, S5_CH)), full((1, BW)), full((BW, BW))],
        input_output_aliases={14: 0},
        out_shape=[jax.ShapeDtypeStruct((SEQ, IN_PAD), BF16),
                   jax.ShapeDtypeStruct((BW, S5_CH), F32), jax.ShapeDtypeStruct((BW, S5_CH), F32),
                   jax.ShapeDtypeStruct((S5_CH, BW), F32), jax.ShapeDtypeStruct((S5_CH, BW), F32),
                   jax.ShapeDtypeStruct((2, S5_CH), F32), jax.ShapeDtypeStruct((1, BW), F32),
                   jax.ShapeDtypeStruct((BW, BW), F32)],
        scratch_shapes=[pltpu.VMEM((S5_T, S5_CH), F32), pltpu.VMEM((S5_T, S5_CH), F32), pltpu.VMEM((2, S5_CH), F32)],
        compiler_params=_cparams(("arbitrary",)),
    )(proj, proj, dout, sre, sim, sre, sim, bbre, bbim, cre, cim, a2, dvec, wglu, dproj)


def _diag_blocks(dense, after=None):
    rows, cols = dense.shape
    rows_per, cols_per = rows // S5_GROUPS, cols // S5_GROUPS
    per_lane_block = LANES // cols_per
    tile = 512

    def body(d_ref, *rest):
        o_ref = rest[-1]
        r0 = pl.program_id(0) * tile
        grp = (r0 + lax.broadcasted_iota(jnp.int32, (tile, LANES), 0)) // rows_per
        lane = lax.broadcasted_iota(jnp.int32, (tile, LANES), 1)
        acc = jnp.zeros((tile, LANES), F32)
        for hb in range(cols // LANES):
            acc = acc + jnp.where(grp == per_lane_block * hb + lane // cols_per, d_ref[:, hb * LANES:(hb + 1) * LANES], 0.0)
        shift = LANES // 2
        while shift >= cols_per:
            acc = acc + pltpu.roll(acc, LANES - shift, 1)
            shift //= 2
        o_ref[...] = acc

    folded = pl.pallas_call(
        body, name=f"diag_blocks_{rows_per}x{cols_per}", grid=(rows // tile,),
        in_specs=[pl.BlockSpec((tile, cols), lambda i: (i, 0))] + ([] if after is None else [pl.BlockSpec(memory_space=pl.ANY)]),
        out_specs=pl.BlockSpec((tile, LANES), lambda i: (i, 0)),
        out_shape=jax.ShapeDtypeStruct((rows, LANES), F32), compiler_params=_cparams(("parallel",)),
    )(dense, *([] if after is None else [after]))
    return folded[:, :cols_per].reshape(S5_GROUPS, rows_per, cols_per)


def _block_diag(t):
    g, rows_per, cols_per = t.shape
    wide = jnp.tile(t.reshape(g * rows_per, cols_per), (1, g))
    r = lax.broadcasted_iota(jnp.int32, wide.shape, 0) // rows_per
    c = lax.broadcasted_iota(jnp.int32, wide.shape, 1) // cols_per
    return jnp.where(r == c, wide, 0.0)


def _s5_disc(lam_re, lam_im, b_re, b_im, c_re, c_im, d, log_step):
    step = jnp.exp(log_step)[:, None]
    mag = jnp.exp(lam_re * step)
    ab_re, ab_im = mag * jnp.cos(lam_im * step), mag * jnp.sin(lam_im * step)
    den = lam_re * lam_re + lam_im * lam_im
    nr = ab_re - 1.0
    coef_re = (nr * lam_re + ab_im * lam_im) / den
    coef_im = (ab_im * lam_re - nr * lam_im) / den
    bb_re = coef_re[..., None] * b_re - coef_im[..., None] * b_im
    bb_im = coef_re[..., None] * b_im + coef_im[..., None] * b_re
    a2 = jnp.stack([ab_re.reshape(-1), ab_im.reshape(-1)])
    return (jnp.swapaxes(bb_re, 1, 2), jnp.swapaxes(bb_im, 1, 2),
            jnp.swapaxes(c_re, 1, 2), jnp.swapaxes(c_im, 1, 2),
            a2, d.reshape(1, BW))


def _left_lanes(shape):
    return lax.broadcasted_iota(jnp.int32, shape, 1) < 64


def _sgu_chunk(u, v, gate, ln_w, ln_b, w, bias):
    u32, v32 = _gelu(u), _gelu(v)
    mu = jnp.mean(v32, axis=-1, keepdims=True)
    var = jnp.mean(jnp.square(v32 - mu), axis=-1, keepdims=True)
    vn = (v32 - mu) * lax.rsqrt(var + EPS) * ln_w + ln_b
    t_i = lax.broadcasted_iota(jnp.int32, (SGU_CHUNK, SGU_CHUNK), 0)
    s_i = lax.broadcasted_iota(jnp.int32, (SGU_CHUNK, SGU_CHUNK), 1)
    causal = t_i >= s_i
    left = _left_lanes((SGU_CHUNK, LANES))
    sgate = _silu(gate)
    outs = []
    for j in range(BW // LANES):
        vb = vn[:, j * LANES:(j + 1) * LANES]
        s_blk = (_bdot(jnp.where(causal, w[2 * j], 0.0), jnp.where(left, vb, 0.0), 1, 0)
                 + _bdot(jnp.where(causal, w[2 * j + 1], 0.0), jnp.where(left, 0.0, vb), 1, 0))
        sl = slice(j * LANES, (j + 1) * LANES)
        outs.append(u32[:, sl] * (s_blk + bias[:, sl]) * sgate[:, sl])
    return outs


def _sgu_fwd(proj, ys, ln_w, ln_b, w, bias, name):
    def body(u_ref, v_ref, g_ref, lw_ref, lb_ref, w_ref, b_ref, ys_in, o_ref):
        outs = _sgu_chunk(u_ref[...], v_ref[...], g_ref[...], lw_ref[...], lb_ref[...], w_ref[...], b_ref[...])
        for j, o in enumerate(outs):
            o_ref[0, :, j * LANES:(j + 1) * LANES] = o.astype(BF16)

    blk = lambda col: pl.BlockSpec((SGU_CHUNK, BW), lambda c: (c, col // BW))
    vec = pl.BlockSpec((1, BW), lambda c: (0, 0))
    return pl.pallas_call(
        body, name=name, grid=(SEQ // SGU_CHUNK,),
        in_specs=[blk(C_SGU_U), blk(C_SGU_V), blk(C_SGU_G), vec, vec,
                  pl.BlockSpec((SGU_HEADS, SGU_CHUNK, SGU_CHUNK), lambda c: (0, 0, 0)),
                  pl.BlockSpec((SGU_CHUNK, BW), lambda c: (0, 0)), pl.BlockSpec(memory_space=pl.ANY)],
        out_specs=pl.BlockSpec((1, SGU_CHUNK, BW), lambda c: (1, c, 0)),
        out_shape=jax.ShapeDtypeStruct((N_BRANCH, SEQ, BW), BF16), input_output_aliases={7: 0},
        compiler_params=_cparams(("parallel",)),
    )(proj, proj, proj, ln_w, ln_b, w, bias, ys)


def _sgu_bwd(proj, dproj, dout, ln_w, ln_b, w, bias, name):
    def body(u_ref, v_ref, g_ref, do_ref, lw_ref, lb_ref, w_ref, b_ref, dproj_in, dp_ref, dlw_ref, dlb_ref, dw_ref, db_ref):
        _, vjp = jax.vjp(_sgu_chunk, u_ref[...], v_ref[...], g_ref[...], lw_ref[...], lb_ref[...], w_ref[...], b_ref[...])
        do = do_ref[0]
        du, dv, dgate, dlw, dlb, dw, db = vjp([do[:, j * LANES:(j + 1) * LANES] for j in range(BW // LANES)])
        dp_ref[:, 0:BW] = du.astype(BF16)
        dp_ref[:, BW:2 * BW] = dv.astype(BF16)
        dp_ref[:, 2 * BW:3 * BW] = dgate.astype(BF16)
        dp_ref[:, 3 * BW:] = jnp.zeros((SGU_CHUNK, BW), BF16)

        @pl.when(pl.program_id(0) == 0)
        def _():
            dlw_ref[...] = dlw
            dlb_ref[...] = dlb
            dw_ref[...] = dw
            db_ref[...] = db

        @pl.when(pl.program_id(0) > 0)
        def _():
            dlw_ref[...] += dlw
            dlb_ref[...] += dlb
            dw_ref[...] += dw
            db_ref[...] += db

    blk = lambda col: pl.BlockSpec((SGU_CHUNK, BW), lambda c: (c, col // BW))
    vec = pl.BlockSpec((1, BW), lambda c: (0, 0))
    wsp = pl.BlockSpec((SGU_HEADS, SGU_CHUNK, SGU_CHUNK), lambda c: (0, 0, 0))
    bsp = pl.BlockSpec((SGU_CHUNK, BW), lambda c: (0, 0))
    return pl.pallas_call(
        body, name=name, grid=(SEQ // SGU_CHUNK,),
        in_specs=[blk(C_SGU_U), blk(C_SGU_V), blk(C_SGU_G), pl.BlockSpec((1, SGU_CHUNK, BW), lambda c: (1, c, 0)),
                  vec, vec, wsp, bsp, pl.BlockSpec(memory_space=pl.ANY)],
        out_specs=[pl.BlockSpec((SGU_CHUNK, 4 * BW), lambda c: (c, C_SGU_U // (4 * BW))), vec, vec, wsp, bsp],
        input_output_aliases={8: 0},
        out_shape=[jax.ShapeDtypeStruct((SEQ, IN_PAD), BF16), jax.ShapeDtypeStruct((1, BW), F32),
                   jax.ShapeDtypeStruct((1, BW), F32), jax.ShapeDtypeStruct((SGU_HEADS, SGU_CHUNK, SGU_CHUNK), F32),
                   jax.ShapeDtypeStruct((SGU_CHUNK, BW), F32)],
        compiler_params=_cparams(("arbitrary",)),
    )(proj, proj, proj, dout, ln_w, ln_b, w, bias, dproj)


CONV_BLK = 256


def _m2_conv_fwd(proj, w, b, name):
    def body(x_ref, w_ref, b_ref, o_ref):
        x = x_ref[...]
        acc = jnp.zeros_like(x) + b_ref[...]
        for k in range(M2_CONV):
            acc = acc + w_ref[k:k + 1, :] * _shift_down(x, M2_CONV - 1 - k)
        o_ref[...] = _silu(acc)

    return pl.pallas_call(
        body, name=name, grid=(M2_CONV_CH // CONV_BLK,),
        in_specs=[pl.BlockSpec((SEQ, CONV_BLK), lambda j: (0, C_M2X // CONV_BLK + j)),
                  pl.BlockSpec((M2_CONV, CONV_BLK), lambda j: (0, j)), pl.BlockSpec((1, CONV_BLK), lambda j: (0, j))],
        out_specs=pl.BlockSpec((SEQ, CONV_BLK), lambda j: (0, j)),
        out_shape=jax.ShapeDtypeStruct((SEQ, M2_CONV_CH), F32),
        compiler_params=_cparams(("parallel",)),
    )(proj, w, b)


def _m2_conv_bwd(proj, dproj, dxa, w, b, name):
    def body(x_ref, d_ref, w_ref, b_ref, dproj_in, dx_ref, dw_ref, db_ref):
        x = x_ref[...]
        xs = [_shift_down(x, M2_CONV - 1 - k) for k in range(M2_CONV)]
        acc = jnp.zeros_like(x) + b_ref[...]
        for k in range(M2_CONV):
            acc = acc + w_ref[k:k + 1, :] * xs[k]
        sg = jax.nn.sigmoid(acc)
        dacc = d_ref[...] * (sg * (1.0 + acc * (1.0 - sg)))
        dx = jnp.zeros_like(x)
        for k in range(M2_CONV):
            dx = dx + w_ref[k:k + 1, :] * _shift_up(dacc, M2_CONV - 1 - k)
            dw_ref[k:k + 1, :] = jnp.sum(dacc * xs[k], axis=0, keepdims=True)
        dx_ref[...] = dx.astype(BF16)
        db_ref[...] = jnp.sum(dacc, axis=0, keepdims=True)

    return pl.pallas_call(
        body, name=name, grid=(M2_CONV_CH // CONV_BLK,),
        in_specs=[pl.BlockSpec((SEQ, CONV_BLK), lambda j: (0, C_M2X // CONV_BLK + j)),
                  pl.BlockSpec((SEQ, CONV_BLK), lambda j: (0, j)),
                  pl.BlockSpec((M2_CONV, CONV_BLK), lambda j: (0, j)), pl.BlockSpec((1, CONV_BLK), lambda j: (0, j)),
                  pl.BlockSpec(memory_space=pl.ANY)],
        out_specs=[pl.BlockSpec((SEQ, CONV_BLK), lambda j: (0, C_M2X // CONV_BLK + j)),
                   pl.BlockSpec((M2_CONV, CONV_BLK), lambda j: (0, j)), pl.BlockSpec((1, CONV_BLK), lambda j: (0, j))],
        input_output_aliases={4: 0},
        out_shape=[jax.ShapeDtypeStruct((SEQ, IN_PAD), BF16), jax.ShapeDtypeStruct((M2_CONV, M2_CONV_CH), F32),
                   jax.ShapeDtypeStruct((1, M2_CONV_CH), F32)],
        compiler_params=_cparams(("parallel",)),
    )(proj, dxa, w, b, dproj)


N_PAIR = M2_HEADS // 2
HI = lax.Precision.HIGHEST


def _col(a, h):
    lane = lax.broadcasted_iota(jnp.int32, a.shape, 1)
    return jnp.sum(jnp.where(lane == h, a, 0.0), axis=1, keepdims=True)


def _row(a, h):
    sub = lax.broadcasted_iota(jnp.int32, a.shape, 0)
    return jnp.sum(jnp.where(sub == h, a, 0.0), axis=0, keepdims=True)


def _ssd_chunk(xs, bms, cms, dtr, zs, states, dt_bias, a_log, dfs, nws):
    q = M2_CHUNK
    dt = _softplus(dtr + dt_bias)
    da = dt * (-jnp.exp(a_log))
    l_i = lax.broadcasted_iota(jnp.int32, (q, q), 0)
    s_i = lax.broadcasted_iota(jnp.int32, (q, q), 1)
    causal = l_i >= s_i
    tril = jnp.where(causal, 1.0, 0.0)
    a_cs = _dg(tril, da, 1, 0, HI)
    a_cs_t = _dg(da, tril, 0, 1, HI)
    a_end = _row(a_cs, q - 1)
    left = _left_lanes((q, LANES))
    left1 = _left_lanes((1, LANES))
    ys, nexts = [], []
    for j in range(N_PAIR):
        grp = j // 2
        bm, cm = bms[grp], cms[grp]
        h0, h1 = 2 * j, 2 * j + 1
        cb = _bdot(cm, bm, 1, 1)
        xdt = xs[j] * jnp.where(left, _col(dt, h0), _col(dt, h1))
        acs0, acs1 = _col(a_cs, h0), _col(a_cs, h1)
        y = _bdot(cm, states[j], 1, 0) * jnp.where(left, jnp.exp(acs0), jnp.exp(acs1))
        s_new = states[j] * jnp.where(left1, jnp.exp(_col(a_end, h0)), jnp.exp(_col(a_end, h1)))
        for h, acs, xh in ((h0, acs0, jnp.where(left, xdt, 0.0)), (h1, acs1, jnp.where(left, 0.0, xdt))):
            decay = jnp.exp(jnp.where(causal, acs - _row(a_cs_t, h), -jnp.inf))
            y = y + _bdot(cb * decay, xh, 1, 0)
            s_new = s_new + _bdot(bm * jnp.exp(_col(a_end, h) - acs), xh, 0, 0)
        ys.append((y + dfs[j] * xs[j]) * _silu(zs[j]))
        nexts.append(s_new)
    ssq = sum(jnp.sum(y * y, axis=-1, keepdims=True) for y in ys)
    scale = lax.rsqrt(ssq / BW + EPS)
    return [y * scale * nw for y, nw in zip(ys, nws)], nexts


def _blocks(ref, n, width=LANES):
    return [ref[:, j * width:(j + 1) * width] for j in range(n)]


def _ssd_fwd(proj, ys, xa, dt_bias, a_log, dfull, nw, name):
    nc = SEQ // M2_CHUNK

    def body(x_ref, b_ref, c_ref, dt_ref, z_ref, dtb_ref, al_ref, df_ref, nw_ref, ys_in, o_ref, sin_ref, st):
        @pl.when(pl.program_id(0) == 0)
        def _():
            st[...] = jnp.zeros_like(st)

        states = [st[j] for j in range(N_PAIR)]
        for j in range(N_PAIR):
            sin_ref[0, j] = states[j]
        ys, nexts = _ssd_chunk(_blocks(x_ref, 4), _blocks(b_ref, 2), _blocks(c_ref, 2), dt_ref[...], _blocks(z_ref, 4),
                               states, dtb_ref[...], al_ref[...], _blocks(df_ref, 4), _blocks(nw_ref, 4))
        for j in range(N_PAIR):
            o_ref[0, :, j * LANES:(j + 1) * LANES] = ys[j].astype(BF16)
            st[j] = nexts[j]

    vec8 = pl.BlockSpec((1, LANES), lambda c: (0, 0))
    vec = pl.BlockSpec((1, BW), lambda c: (0, 0))
    return pl.pallas_call(
        body, name=name, grid=(nc,),
        in_specs=[pl.BlockSpec((M2_CHUNK, BW), lambda c: (c, 0)), pl.BlockSpec((M2_CHUNK, 256), lambda c: (c, 2)),
                  pl.BlockSpec((M2_CHUNK, 256), lambda c: (c, 3)), pl.BlockSpec((M2_CHUNK, LANES), lambda c: (c, C_DT // LANES)),
                  pl.BlockSpec((M2_CHUNK, BW), lambda c: (c, C_M2Z // BW)), vec8, vec8, vec, vec,
                  pl.BlockSpec(memory_space=pl.ANY)],
        out_specs=[pl.BlockSpec((1, M2_CHUNK, BW), lambda c: (2, c, 0)),
                   pl.BlockSpec((1, N_PAIR, M2_STATE, LANES), lambda c: (c, 0, 0, 0))],
        out_shape=[jax.ShapeDtypeStruct((N_BRANCH, SEQ, BW), BF16), jax.ShapeDtypeStruct((nc, N_PAIR, M2_STATE, LANES), F32)],
        input_output_aliases={9: 0},
        scratch_shapes=[pltpu.VMEM((N_PAIR, M2_STATE, LANES), F32)],
        compiler_params=_cparams(("arbitrary",)),
    )(xa, xa, xa, proj, proj, dt_bias, a_log, dfull, nw, ys)


def _ssd_bwd(proj, dproj, xa, dout, s_in, dt_bias, a_log, dfull, nw, name):
    nc = SEQ // M2_CHUNK

    def body(x_ref, b_ref, c_ref, dt_ref, z_ref, do_ref, sin_ref, dtb_ref, al_ref, df_ref, nw_ref, dproj_in,
             dp_ref, dxa_ref, ddtb_ref, dal_ref, ddf_ref, dnw_ref, dst):
        @pl.when(pl.program_id(0) == 0)
        def _():
            dst[...] = jnp.zeros_like(dst)
            for r in (ddtb_ref, dal_ref, ddf_ref, dnw_ref):
                r[...] = jnp.zeros_like(r)

        states = [sin_ref[0, j] for j in range(N_PAIR)]
        _, vjp = jax.vjp(_ssd_chunk, _blocks(x_ref, 4), _blocks(b_ref, 2), _blocks(c_ref, 2), dt_ref[...],
                         _blocks(z_ref, 4), states, dtb_ref[...], al_ref[...], _blocks(df_ref, 4), _blocks(nw_ref, 4))
        dxs, dbs, dcs, ddt, dzs, dstates, ddtb, dal, ddfs, dnws = vjp(
            ([do_ref[0, :, j * LANES:(j + 1) * LANES] for j in range(N_PAIR)], [dst[j] for j in range(N_PAIR)]))
        for j in range(N_PAIR):
            sl = slice(j * LANES, (j + 1) * LANES)
            dxa_ref[:, sl] = dxs[j]
            dp_ref[:, sl] = dzs[j].astype(BF16)
            dst[j] = dstates[j]
            ddf_ref[:, sl] += ddfs[j]
            dnw_ref[:, sl] += dnws[j]
        for g in range(2):
            dxa_ref[:, BW + g * LANES:BW + (g + 1) * LANES] = dbs[g]
            dxa_ref[:, BW + 256 + g * LANES:BW + 256 + (g + 1) * LANES] = dcs[g]
        dp_ref[:, BW:BW + LANES] = ddt.astype(BF16)
        dp_ref[:, BW + LANES:] = jnp.zeros((M2_CHUNK, 2 * BW - BW - LANES), BF16)
        ddtb_ref[...] += ddtb
        dal_ref[...] += dal

    rev = lambda w, col=0: pl.BlockSpec((M2_CHUNK, w), lambda i: (nc - 1 - i, col))
    vec8 = pl.BlockSpec((1, LANES), lambda i: (0, 0))
    vec = pl.BlockSpec((1, BW), lambda i: (0, 0))
    return pl.pallas_call(
        body, name=name, grid=(nc,),
        in_specs=[rev(BW), rev(256, 2), rev(256, 3), rev(LANES, C_DT // LANES), rev(BW, C_M2Z // BW),
                  pl.BlockSpec((1, M2_CHUNK, BW), lambda i: (2, nc - 1 - i, 0)),
                  pl.BlockSpec((1, N_PAIR, M2_STATE, LANES), lambda i: (nc - 1 - i, 0, 0, 0)), vec8, vec8, vec, vec,
                  pl.BlockSpec(memory_space=pl.ANY)],
        out_specs=[rev(2 * BW, C_M2Z // (2 * BW)), rev(M2_CONV_CH), vec8, vec8, vec, vec],
        input_output_aliases={11: 0},
        out_shape=[jax.ShapeDtypeStruct((SEQ, IN_PAD), BF16), jax.ShapeDtypeStruct((SEQ, M2_CONV_CH), F32),
                   jax.ShapeDtypeStruct((1, LANES), F32), jax.ShapeDtypeStruct((1, LANES), F32),
                   jax.ShapeDtypeStruct((1, BW), F32), jax.ShapeDtypeStruct((1, BW), F32)],
        scratch_shapes=[pltpu.VMEM((N_PAIR, M2_STATE, LANES), F32)],
        compiler_params=_cparams(("arbitrary",)),
    )(xa, xa, xa, proj, proj, dout, s_in, dt_bias, a_log, dfull, nw, dproj)


def _sc_specs():
    col = lambda kind: pl.BlockSpec((SEQ, LANES), lambda j: (0, C_SC // LANES + 4 * j + kind))
    return [col(0), col(1), col(2), col(3)]


def _sc_fwd(proj, ys, w, name):
    def body(b_ref, c_ref, h_ref, g_ref, w_ref, ys_in, o_ref):
        ch = c_ref[...] * h_ref[...]
        acc = jnp.zeros_like(ch)
        for k in range(SC_CONV):
            acc = acc + w_ref[k:k + 1, :] * _shift_down(ch, SC_CONV - 1 - k)
        o_ref[0] = (b_ref[...] * acc * _silu(g_ref[...])).astype(BF16)

    return pl.pallas_call(
        body, name=name, grid=(BW // LANES,),
        in_specs=_sc_specs() + [pl.BlockSpec((SC_CONV, LANES), lambda j: (0, j)), pl.BlockSpec(memory_space=pl.ANY)],
        out_specs=pl.BlockSpec((1, SEQ, LANES), lambda j: (3, 0, j)),
        out_shape=jax.ShapeDtypeStruct((N_BRANCH, SEQ, BW), BF16), input_output_aliases={5: 0},
        compiler_params=_cparams(("parallel",)),
    )(proj, proj, proj, proj, w, ys)


def _sc_bwd(proj, dproj, dout, w, name):
    def body(b_ref, c_ref, h_ref, g_ref, do_ref, w_ref, dproj_in, dp_ref, dw_ref):
        cv, hv, gv = c_ref[...], h_ref[...], g_ref[...]
        ch = cv * hv
        chs = [_shift_down(ch, SC_CONV - 1 - k) for k in range(SC_CONV)]
        acc = jnp.zeros_like(ch)
        for k in range(SC_CONV):
            acc = acc + w_ref[k:k + 1, :] * chs[k]
        sg = jax.nn.sigmoid(gv)
        do = do_ref[0]
        bv = b_ref[...]
        dp_ref[:, 0:LANES] = (do * acc * (gv * sg)).astype(BF16)
        dp_ref[:, 3 * LANES:] = (do * bv * acc * (sg * (1.0 + gv * (1.0 - sg)))).astype(BF16)
        dacc = do * bv * (gv * sg)
        dch = jnp.zeros_like(ch)
        for k in range(SC_CONV):
            dch = dch + w_ref[k:k + 1, :] * _shift_up(dacc, SC_CONV - 1 - k)
            dw_ref[k:k + 1, :] = jnp.sum(dacc * chs[k], axis=0, keepdims=True)
        dp_ref[:, LANES:2 * LANES] = (dch * hv).astype(BF16)
        dp_ref[:, 2 * LANES:3 * LANES] = (dch * cv).astype(BF16)

    wsp = pl.BlockSpec((SC_CONV, LANES), lambda j: (0, j))
    return pl.pallas_call(
        body, name=name, grid=(BW // LANES,),
        in_specs=_sc_specs() + [pl.BlockSpec((1, SEQ, LANES), lambda j: (3, 0, j)), wsp, pl.BlockSpec(memory_space=pl.ANY)],
        out_specs=[pl.BlockSpec((SEQ, 4 * LANES), lambda j: (0, C_SC // (4 * LANES) + j)), wsp],
        input_output_aliases={6: 0},
        out_shape=[jax.ShapeDtypeStruct((SEQ, IN_PAD), BF16), jax.ShapeDtypeStruct((SC_CONV, BW), F32)],
        compiler_params=_cparams(("parallel",)),
    )(proj, proj, proj, proj, dout, w, dproj)


MERGE_T = 256
MERGE_BWD_T = 1024


def _merge_fwd(proj, ys, merge_b, w_branch, name):
    def body(y_ref, lg_ref, b_ref, w_ref, o_ref):
        acc = jnp.zeros((MERGE_T, D_MODEL), F32)
        for k in range(N_BRANCH):
            gate = jax.nn.sigmoid(lg_ref[:, k * D_MODEL:(k + 1) * D_MODEL] + b_ref[k])
            acc = acc + gate * _dg(y_ref[k], w_ref[k], 1, 0)
        o_ref[...] = acc.astype(BF16)

    return pl.pallas_call(
        body, name=name, grid=(SEQ // MERGE_T,),
        in_specs=[pl.BlockSpec((N_BRANCH, MERGE_T, BW), lambda i: (0, i, 0)),
                  pl.BlockSpec((MERGE_T, N_BRANCH * D_MODEL), lambda i: (i, C_MERGE // (N_BRANCH * D_MODEL))),
                  pl.BlockSpec((N_BRANCH, 1, D_MODEL), lambda i: (0, 0, 0)),
                  pl.BlockSpec((N_BRANCH, BW, D_MODEL), lambda i: (0, 0, 0))],
        out_specs=pl.BlockSpec((MERGE_T, D_MODEL), lambda i: (i, 0)),
        out_shape=jax.ShapeDtypeStruct((SEQ, D_MODEL), BF16),
        compiler_params=_cparams(("parallel",)),
    )(ys, proj, merge_b, w_branch)


def _merge_bwd(proj, ys, dm, merge_b, w_branch, name):
    nt = SEQ // MERGE_BWD_T

    def body(y_ref, lg_ref, dm_ref, b_ref, w_ref, dy_ref, dlg_ref, dw_ref, db_ref, dw_acc):
        i = pl.program_id(1)
        gate = jax.nn.sigmoid(lg_ref[...] + b_ref[0])
        y = y_ref[0]
        dmv = dm_ref[...]
        dbo = (gate * dmv).astype(BF16)
        dlg = _dg(y, w_ref[0], 1, 0) * dmv * gate * (1.0 - gate)
        dlg_ref[...] = dlg.astype(BF16)
        dy_ref[0] = _dg(dbo, w_ref[0], 1, 1)
        dwp = _dg(y, dbo, 0, 0)
        dbp = jnp.sum(dlg, axis=0, keepdims=True)

        @pl.when(i == 0)
        def _():
            dw_acc[...] = dwp
            db_ref[0] = dbp

        @pl.when(i > 0)
        def _():
            dw_acc[...] += dwp
            db_ref[0] += dbp

        @pl.when(i == nt - 1)
        def _():
            dw_ref[0] = dw_acc[...].astype(BF16)

    return pl.pallas_call(
        body, name=name, grid=(N_BRANCH, nt),
        in_specs=[pl.BlockSpec((1, MERGE_BWD_T, BW), lambda k, i: (k, i, 0)),
                  pl.BlockSpec((MERGE_BWD_T, D_MODEL), lambda k, i: (i, C_MERGE // D_MODEL + k)),
                  pl.BlockSpec((MERGE_BWD_T, D_MODEL), lambda k, i: (i, 0)),
                  pl.BlockSpec((1, 1, D_MODEL), lambda k, i: (k, 0, 0)),
                  pl.BlockSpec((1, BW, D_MODEL), lambda k, i: (k, 0, 0))],
        out_specs=[pl.BlockSpec((1, MERGE_BWD_T, BW), lambda k, i: (k, i, 0)),
                   pl.BlockSpec((MERGE_BWD_T, D_MODEL), lambda k, i: (i, k)),
                   pl.BlockSpec((1, BW, D_MODEL), lambda k, i: (k, 0, 0)),
                   pl.BlockSpec((1, 1, D_MODEL), lambda k, i: (k, 0, 0))],
        out_shape=[jax.ShapeDtypeStruct((N_BRANCH, SEQ, BW), F32), jax.ShapeDtypeStruct((SEQ, IN_PAD), BF16),
                   jax.ShapeDtypeStruct((N_BRANCH, BW, D_MODEL), BF16), jax.ShapeDtypeStruct((N_BRANCH, 1, D_MODEL), F32)],
        scratch_shapes=[pltpu.VMEM((BW, D_MODEL), F32)],
        compiler_params=_cparams(("parallel", "arbitrary")),
    )(ys, proj, dm, merge_b, w_branch)


def _adamw(glist, w, m, v, rows, name):
    nl = len(glist)
    n, r, c = glist[0].shape
    assert w.shape == (nl, r, c) and r % rows == 0
    nb = r // rows

    def body(*refs):
        g_refs = refs[:nl]
        w_ref, m_ref, v_ref, go_ref, d_ref, mo_ref, vo_ref = refs[nl:]
        for layer in range(nl):
            @pl.when(pl.program_id(0) == layer)
            def _(g_ref=g_refs[layer]):
                g = g_ref[0].astype(F32)
                for s in range(1, n):
                    g = g + g_ref[s].astype(F32)
                mn = ADAM_B1 * m_ref[0] + (1.0 - ADAM_B1) * g
                vn = ADAM_B2 * v_ref[0] + (1.0 - ADAM_B2) * jnp.square(g)
                m_hat = mn / (1.0 - ADAM_B1 ** ADAM_STEP)
                v_hat = vn / (1.0 - ADAM_B2 ** ADAM_STEP)
                go_ref[0] = g
                d_ref[0] = -ADAM_LR * (m_hat / (jnp.sqrt(v_hat) + ADAM_EPS) + ADAM_WD * w_ref[0])
                mo_ref[0] = mn
                vo_ref[0] = vn

    def g_spec(layer):
        return pl.BlockSpec((n, rows, c), lambda a, i: (0, jnp.where(a < layer, 0, jnp.where(a == layer, i, nb - 1)), 0))

    blk = pl.BlockSpec((1, rows, c), lambda a, i: (a, i, 0))
    out = jax.ShapeDtypeStruct((nl, r, c), F32)
    return pl.pallas_call(
        body, name=name, grid=(nl, nb),
        in_specs=[g_spec(layer) for layer in range(nl)] + [blk, blk, blk],
        out_specs=[blk, blk, blk, blk], out_shape=[out, out, out, out],
        compiler_params=_cparams(("arbitrary", "arbitrary")),
    )(*glist, w, m, v)


X_ROWS_PER_COL = 2 * (D_MODEL // LANES)


def _w_in_to_x(w):
    t = jnp.transpose(w, (2, 0, 1)).reshape(SHARD_IN, DEPTH, D_MODEL // LANES, LANES)
    return jnp.transpose(t, (0, 2, 1, 3)).reshape(SHARD_IN * X_ROWS_PER_COL, LANES)


def _w_in_from_x(xv):
    t = jnp.transpose(xv.reshape(SHARD_IN, D_MODEL // LANES, DEPTH, LANES), (0, 2, 1, 3))
    return jnp.transpose(t.reshape(SHARD_IN, DEPTH, D_MODEL), (1, 2, 0))


def _adamw_w_in(glist, w, m, v, name, after=None):
    n = glist[0].shape[0]
    cols = 2 * LANES
    rows = cols * X_ROWS_PER_COL
    extra = [] if after is None else [after]

    def body(g0_ref, g1_ref, w_ref, m_ref, v_ref, *rest):
        go_ref, d_ref, mo_ref, vo_ref = rest[len(extra):]
        for layer, g_ref in enumerate((g0_ref, g1_ref)):
            g = g_ref[0].astype(F32)
            for s in range(1, n):
                g = g + g_ref[s].astype(F32)
            gt = g.T
            for t in range(D_MODEL // LANES):
                sel = (pl.ds(2 * t + layer, cols, stride=X_ROWS_PER_COL), slice(None))
                gs = gt[:, t * LANES:(t + 1) * LANES]
                mn = ADAM_B1 * m_ref[sel] + (1.0 - ADAM_B1) * gs
                vn = ADAM_B2 * v_ref[sel] + (1.0 - ADAM_B2) * jnp.square(gs)
                m_hat = mn / (1.0 - ADAM_B1 ** ADAM_STEP)
                v_hat = vn / (1.0 - ADAM_B2 ** ADAM_STEP)
                go_ref[sel] = gs
                d_ref[sel] = -ADAM_LR * (m_hat / (jnp.sqrt(v_hat) + ADAM_EPS) + ADAM_WD * w_ref[sel])
                mo_ref[sel] = mn
                vo_ref[sel] = vn

    g_spec = pl.BlockSpec((n, D_MODEL, cols), lambda i: (0, 0, i))
    blk = pl.BlockSpec((rows, LANES), lambda i: (i, 0))
    out = jax.ShapeDtypeStruct((SHARD_IN * X_ROWS_PER_COL, LANES), F32)
    res = pl.pallas_call(
        body, name=name, grid=(-(-SHARD_IN // cols),),
        in_specs=[g_spec, g_spec, blk, blk, blk] + [pl.BlockSpec(memory_space=pl.ANY)] * len(extra),
        out_specs=[blk, blk, blk, blk], out_shape=[out, out, out, out],
        compiler_params=_cparams(("parallel",)),
    )(*glist, _w_in_to_x(w), _w_in_to_x(m), _w_in_to_x(v), *extra)
    return [_w_in_from_x(o) for o in res]


def _adamw_many(gs, ws, ms, vs, name):
    k = len(gs)

    def body(*refs):
        g_refs, w_refs, m_refs, v_refs = refs[:k], refs[k:2 * k], refs[2 * k:3 * k], refs[3 * k:4 * k]
        d_refs, mo_refs, vo_refs = refs[4 * k:5 * k], refs[5 * k:6 * k], refs[6 * k:7 * k]
        for i in range(k):
            g = g_refs[i][...]
            mn = ADAM_B1 * m_refs[i][...] + (1.0 - ADAM_B1) * g
            vn = ADAM_B2 * v_refs[i][...] + (1.0 - ADAM_B2) * jnp.square(g)
            m_hat = mn / (1.0 - ADAM_B1 ** ADAM_STEP)
            v_hat = vn / (1.0 - ADAM_B2 ** ADAM_STEP)
            d_refs[i][...] = -ADAM_LR * (m_hat / (jnp.sqrt(v_hat) + ADAM_EPS) + ADAM_WD * w_refs[i][...])
            mo_refs[i][...] = mn
            vo_refs[i][...] = vn

    whole = pl.BlockSpec(memory_space=pltpu.VMEM)
    shapes = [jax.ShapeDtypeStruct(w.shape, F32) for w in ws]
    outs = pl.pallas_call(
        body, name=name, in_specs=[whole] * (4 * k), out_specs=[whole] * (3 * k), out_shape=shapes * 3,
        compiler_params=_cparams(None),
    )(*gs, *ws, *ms, *vs)
    return outs[:k], outs[k:2 * k], outs[2 * k:]


MEMORY_ORDER = {'s5_b_re': (0, 1, 3, 2), 's5_b_im': (0, 1, 3, 2), 's5_d': (0, 2, 1), 'sc_conv_w': (1, 0, 2)}


def _memory_view(name, t):
    return jnp.transpose(t, MEMORY_ORDER[name]) if name in MEMORY_ORDER else t


def _slot_sum(gslots, name):
    n, r, c = gslots.shape

    def body(g_ref, o_ref):
        g = g_ref[0]
        for s in range(1, n):
            g = g + g_ref[s]
        o_ref[...] = g

    return pl.pallas_call(
        body, name=name, in_specs=[pl.BlockSpec((n, r, c), lambda: (0, 0, 0))],
        out_specs=pl.BlockSpec((r, c), lambda: (0, 0)), out_shape=jax.ShapeDtypeStruct((r, c), F32),
        compiler_params=_cparams(None),
    )(gslots)


def _me_and_peers():
    x, y, c = lax.axis_index("x"), lax.axis_index("y"), lax.axis_index("c")
    me = 4 * x + 2 * y + c
    peers = []
    for k in range(1, N_DEV):
        px = 1 - x if (k >> 2) & 1 else x
        py = 1 - y if (k >> 1) & 1 else y
        pc = 1 - c if k & 1 else c
        peers.append((4 * px + 2 * py + pc, (px, py, pc)))
    return me, peers


_HBM = pl.BlockSpec(memory_space=pltpu.HBM)
_SEM = pl.BlockSpec(memory_space=pltpu.SEMAPHORE)
_EFFECT = pltpu.SideEffectType.DATAFLOW_SIDE_EFFECTING


N_CHIP = N_DEV // 2


def _chip_peers():
    x, y, c = lax.axis_index("x"), lax.axis_index("y"), lax.axis_index("c")
    chips = []
    for d in range(1, N_CHIP):
        px = 1 - x if (d >> 1) & 1 else x
        py = 1 - y if d & 1 else y
        chips.append((2 * px + py, (px, py)))
    return (x, y, c), 2 * x + y, chips


def _plan_direct(ins, lands, send_sems, recv_sems, local_sems, gather):
    me, peers = _me_and_peers()
    plan = dict(start=[], local=[], sends=[], recvs=[])
    for t in range(len(ins)):
        own = pltpu.make_async_copy(ins[t] if gather else ins[t].at[me], lands[t].at[me], local_sems.at[t])
        plan['start'].append(own)
        plan['local'].append(own)
        for k, (pidx, pos) in enumerate(peers):
            cp = pltpu.make_async_remote_copy(
                src_ref=ins[t] if gather else ins[t].at[pidx], dst_ref=lands[t].at[me],
                send_sem=send_sems.at[t * (N_DEV - 1) + k], recv_sem=recv_sems.at[t * (N_DEV - 1) + k],
                device_id=pos, device_id_type=MESH)
            plan['start'].append(cp)
            plan['sends'].append(cp)
            plan['recvs'].append(cp)
    return plan


def _plan_gather(ins, lands, send_sems, recv_sems, local_sems, first=0):
    (x, y, c), q, chips = _chip_peers()
    me = 2 * q + c
    plan = dict(start=[], relay_wait=[], relay_start=[], local=[], sends=[], recvs=[])
    for t in range(len(ins)):
        base = (first + t) * 7
        sem = lambda k: dict(send_sem=send_sems.at[base + k], recv_sem=recv_sems.at[base + k], device_id_type=MESH)
        own = pltpu.make_async_copy(ins[t], lands[t].at[me], local_sems.at[first + t])
        to_sib = pltpu.make_async_remote_copy(src_ref=ins[t], dst_ref=lands[t].at[me], device_id=(x, y, 1 - c), **sem(0))
        plan['start'] += [own, to_sib]
        plan['local'].append(own)
        plan['sends'].append(to_sib)
        plan['recvs'].append(to_sib)
        for d, (pq, (px, py)) in enumerate(chips):
            to_chip = pltpu.make_async_remote_copy(src_ref=ins[t], dst_ref=lands[t].at[me], device_id=(px, py, c), **sem(1 + d))
            blk = lands[t].at[2 * pq + c]
            fwd = pltpu.make_async_remote_copy(src_ref=blk, dst_ref=blk, device_id=(x, y, 1 - c), **sem(4 + d))
            plan['start'].append(to_chip)
            plan['relay_wait'].append(to_chip)
            plan['relay_start'].append(fwd)
            plan['sends'] += [to_chip, fwd]
            plan['recvs'].append(fwd)
    return plan


def _plan_pair(ins, lands, send_sems, recv_sems, local_sems):
    (x, y, c), q, chips = _chip_peers()
    plan = dict(start=[], local=[], sends=[], recvs=[])
    for t in range(len(ins)):
        for k in range(N_CHIP):
            cp = pltpu.make_async_remote_copy(
                src_ref=ins[t].at[2 * k + 1 - c], dst_ref=lands[t].at[k], send_sem=send_sems.at[t * N_CHIP + k],
                recv_sem=recv_sems.at[t * N_CHIP + k], device_id=(x, y, 1 - c), device_id_type=MESH)
            plan['start'].append(cp)
            plan['sends'].append(cp)
            plan['recvs'].append(cp)
    return plan


def _plan_chips(ins, lands, send_sems, recv_sems, local_sems):
    (x, y, c), q, chips = _chip_peers()
    plan = dict(start=[], local=[], sends=[], recvs=[])
    for t in range(len(ins)):
        own = pltpu.make_async_copy(ins[t].at[q], lands[t].at[q], local_sems.at[t])
        plan['start'].append(own)
        plan['local'].append(own)
        for d, (pq, (px, py)) in enumerate(chips):
            cp = pltpu.make_async_remote_copy(
                src_ref=ins[t].at[pq], dst_ref=lands[t].at[q], send_sem=send_sems.at[t * 3 + d],
                recv_sem=recv_sems.at[t * 3 + d], device_id=(px, py, c), device_id_type=MESH)
            plan['start'].append(cp)
            plan['sends'].append(cp)
            plan['recvs'].append(cp)
    return plan


def _split_start(plan_fn, tensors, land_shapes, n_sems, name, after=None):
    n = len(tensors)
    extra = [] if after is None else [after]

    def body(*refs):
        ins, lands = refs[:n], refs[n:2 * n]
        plan = plan_fn(ins, lands, *refs[2 * n + len(extra):2 * n + len(extra) + 3])
        for cp in plan['start']:
            cp.start()
        refs[-1][...] = jnp.zeros_like(refs[-1])

    outs = pl.pallas_call(
        body, name=name,
        out_shape=(pltpu.SemaphoreType.DMA((n_sems,)), pltpu.SemaphoreType.DMA((n_sems,)), pltpu.SemaphoreType.DMA((n,)),
                   *[pltpu.HBM(t.shape, t.dtype) for t in tensors],
                   *[pltpu.HBM(s, t.dtype) for s, t in zip(land_shapes, tensors)],
                   jax.ShapeDtypeStruct((8, LANES), F32)),
        in_specs=[_HBM] * (2 * n) + [pl.BlockSpec(memory_space=pl.ANY)] * len(extra),
        out_specs=(_SEM, _SEM, _SEM, *[_HBM] * (2 * n), pl.BlockSpec(memory_space=pltpu.VMEM)),
        input_output_aliases={t: 3 + t for t in range(2 * n)},
        compiler_params=pltpu.CompilerParams(has_side_effects=_EFFECT),
    )(*[pltpu.with_memory_space_constraint(t, pltpu.HBM) for t in tensors],
      *[pltpu.with_memory_space_constraint(lax.empty(s, t.dtype), pltpu.HBM) for s, t in zip(land_shapes, tensors)], *extra)
    return outs[:-1], outs[-1]


def _split_relay(plan_fn, state, after, name):
    sems, thru = state[:3], state[3:]
    n = len(thru) // 2

    def arrived(*refs):
        plan = plan_fn(refs[:n], refs[n:2 * n], *refs[2 * n:2 * n + 3])
        for cp in plan['relay_wait']:
            cp.wait_recv()

    thru = pl.pallas_call(
        arrived, name=name + "_arrived",
        out_shape=tuple(pltpu.HBM(t.shape, t.dtype) for t in thru),
        in_specs=[_HBM] * (2 * n) + [_SEM, _SEM, _SEM, pl.BlockSpec(memory_space=pl.ANY)],
        out_specs=tuple([_HBM] * (2 * n)),
        input_output_aliases={t: t for t in range(2 * n)},
        compiler_params=pltpu.CompilerParams(has_side_effects=_EFFECT),
    )(*thru, *sems, after)

    def forward(*refs):
        plan = plan_fn(refs[:n], refs[n:2 * n], *refs[2 * n:2 * n + 3])
        for cp in plan['relay_start']:
            cp.start()
        refs[-1][...] = jnp.zeros_like(refs[-1])

    outs = pl.pallas_call(
        forward, name=name + "_forward",
        out_shape=(*[pltpu.HBM(t.shape, t.dtype) for t in thru], jax.ShapeDtypeStruct((8, LANES), F32)),
        in_specs=[_HBM] * (2 * n) + [_SEM, _SEM, _SEM],
        out_specs=(*[_HBM] * (2 * n), pl.BlockSpec(memory_space=pltpu.VMEM)),
        input_output_aliases={t: t for t in range(2 * n)},
        compiler_params=pltpu.CompilerParams(has_side_effects=_EFFECT),
    )(*thru, *sems)
    return (*sems, *outs[:-1]), outs[-1]


def _split_wait(plan_fn, state, after, name, with_sources=False):
    sems, thru = state[:3], state[3:]
    n = len(thru) // 2

    def body(*refs):
        plan = plan_fn(refs[:n], refs[n:2 * n], *refs[2 * n:2 * n + 3])
        for cp in plan['local']:
            cp.wait()
        for cp in plan['sends']:
            cp.wait_send()
        for cp in plan['recvs']:
            cp.wait_recv()

    outs = pl.pallas_call(
        body, name=name,
        out_shape=tuple(pltpu.HBM(t.shape, t.dtype) for t in thru),
        in_specs=[_HBM] * (2 * n) + [_SEM, _SEM, _SEM, pl.BlockSpec(memory_space=pl.ANY)],
        out_specs=tuple([_HBM] * (2 * n)),
        input_output_aliases={t: t for t in range(2 * n)},
        compiler_params=pltpu.CompilerParams(has_side_effects=_EFFECT),
    )(*thru, *sems, after)
    return (list(outs[:n]), list(outs[n:])) if with_sources else list(outs[n:])


PAIR_SUM_BLOCK = 768 * 1024


def _pair_sum(mine, theirs, name):
    _, r, c = mine.shape
    rows = r
    while rows * c > PAIR_SUM_BLOCK and rows % 32 == 0:
        rows //= 2

    def body(core_ref, a_ref, b_ref, o_ref):
        o_ref[0] = (a_ref[0].astype(F32) + b_ref[0].astype(F32)).astype(o_ref.dtype)

    return pl.pallas_call(
        body, name=name,
        grid_spec=pltpu.PrefetchScalarGridSpec(
            num_scalar_prefetch=1, grid=(N_CHIP, r // rows),
            in_specs=[pl.BlockSpec((1, rows, c), lambda k, i, core: (2 * k + core[0], i, 0)),
                      pl.BlockSpec((1, rows, c), lambda k, i, core: (k, i, 0))],
            out_specs=pl.BlockSpec((1, rows, c), lambda k, i, core: (k, i, 0))),
        out_shape=jax.ShapeDtypeStruct((N_CHIP, r, c), mine.dtype),
        compiler_params=_cparams(("parallel", "parallel")),
    )(lax.axis_index("c").astype(jnp.int32).reshape(1), mine, theirs)


WEIGHTS = ['norm_w', 'w_in', 's5_lambda_re', 's5_lambda_im', 's5_b_re', 's5_b_im', 's5_c_re', 's5_c_im', 's5_d',
           's5_log_step', 's5_w_glu', 'sgu_ln_w', 'sgu_ln_b', 'sgu_w', 'sgu_b', 'm2_conv_w', 'm2_conv_b', 'm2_dt_bias',
           'm2_a_log', 'm2_d', 'm2_norm_w', 'sc_conv_w', 'merge_b', 'w_branch', 'w_out', 'final_norm_w']
BIG_SHARDED = ['w_in', 'w_branch', 'w_out', 's5_w_glu']
SMALL_SHARDED = ['m2_conv_w', 'sc_conv_w', 'merge_b']
REPLICATED = [n for n in WEIGHTS if n not in BIG_SHARDED + SMALL_SHARDED]
S5_NAMES = ['s5_lambda_re', 's5_lambda_im', 's5_b_re', 's5_b_im', 's5_c_re', 's5_c_im', 's5_d', 's5_log_step']


def _sc_interleave(t):
    lead = t.shape[:-1]
    return jnp.swapaxes(t.reshape(lead + (4, 4, LANES)), -3, -2).reshape(lead + (4 * BW,))


def _pad_in(w):
    z = lambda n: jnp.zeros(w.shape[:-1] + (n,), w.dtype)
    return jnp.concatenate([w[..., 6152:], w[..., 0:1024], w[..., 3072:4096], w[..., 1024:2560], z(512),
                            w[..., 2560:3072], w[..., 4096:4104], z(504), _sc_interleave(w[..., 4104:6152])], axis=-1)


def _unpad_in(g):
    return jnp.concatenate([g[..., C_S5U:C_S5U + 1024], g[..., C_SGU_U:C_SGU_U + 1536], g[..., C_M2Z:C_M2Z + 512],
                            g[..., C_M2X:C_M2X + 1024], g[..., C_DT:C_DT + 8], _sc_interleave(g[..., C_SC:]),
                            g[..., :N_BRANCH * D_MODEL]], axis=-1)


ROW_BLOCK = 8 * LANES


def _pack_rows(tensors, row_mult, batched=False):
    parts = []
    for t in tensors:
        f = t.reshape((t.shape[0], -1) if batched else (1, -1))
        f = jnp.pad(f, ((0, 0), (0, (-f.shape[1]) % ROW_BLOCK)))
        parts.append(f.reshape(f.shape[0], -1, LANES))
    out = jnp.concatenate(parts, axis=1)
    out = jnp.pad(out, ((0, 0), (0, (-out.shape[1]) % row_mult), (0, 0)))
    return out if batched else out[0]


def _unpack_rows(rows, shapes):
    out, r0 = [], 0
    for shp in shapes:
        size = 1
        for s in shp:
            size *= s
        nr = -(-size // ROW_BLOCK) * 8
        out.append(rows[r0:r0 + nr].reshape(-1)[:size].reshape(shp))
        r0 += nr
    return out


def _kernel_col_map():
    m = np.full(IN_PAD, -1, np.int64)
    m[C_MERGE:C_MERGE + 4096] = np.arange(6152, 10248)
    m[C_S5U:C_S5U + 1024] = np.arange(0, 1024)
    m[C_M2X:C_M2X + 1024] = np.arange(3072, 4096)
    m[C_SGU_U:C_SGU_U + 1536] = np.arange(1024, 2560)
    m[C_M2Z:C_M2Z + 512] = np.arange(2560, 3072)
    m[C_DT:C_DT + 8] = np.arange(4096, 4104)
    for j in range(4):
        for kind in range(4):
            k0 = C_SC + 4 * LANES * j + LANES * kind
            m[k0:k0 + LANES] = 4104 + BW * kind + LANES * j + np.arange(LANES)
    return m


def _lane_pieces(sources):
    pieces, cur = [], None
    for lane, src in enumerate(sources):
        key = None if src is None else (src[0], src[1] // LANES, (lane - src[1]) % LANES)
        if cur is not None and key == cur[0]:
            cur[2] = lane + 1
        else:
            if cur is not None and cur[0] is not None:
                pieces.append((*cur[0], cur[1], cur[2]))
            cur = [key, lane, lane + 1]
    if cur is not None and cur[0] is not None:
        pieces.append((*cur[0], cur[1], cur[2]))
    return pieces


def _assemble_block(pieces, load, rows, dtype):
    lane = lax.broadcasted_iota(jnp.int32, (rows, LANES), 1)
    out = None
    for arr, sb, shift, lo, hi in pieces:
        v = load(arr, sb)
        if shift:
            v = pltpu.roll(v, shift, 1)
        if out is None and lo == 0 and hi == LANES:
            out = v
        else:
            out = jnp.where((lane >= lo) & (lane < hi), v, jnp.zeros((rows, LANES), dtype) if out is None else out)
    return jnp.zeros((rows, LANES), dtype) if out is None else out


RELAYOUT_ROWS = 512
SHARD_BLOCKS = -(-SHARD_IN // LANES)


def _load_shard_block(ref, rows):
    def load(j, sb):
        if sb == SHARD_BLOCKS - 1:
            return jnp.broadcast_to(ref[j, :, SHARD_IN - 1:SHARD_IN], (rows, LANES))
        return ref[j, :, sb * LANES:(sb + 1) * LANES]
    return load


def _relayout_w_in(gathered, name):
    kmap = _kernel_col_map()
    dtype = gathered.dtype

    def body(src_ref, o_ref):
        load = _load_shard_block(src_ref, RELAYOUT_ROWS)
        for ob in range(IN_PAD // LANES):
            srcs = [None if kmap[ob * LANES + l] < 0 else (int(kmap[ob * LANES + l]) // SHARD_IN, int(kmap[ob * LANES + l]) % SHARD_IN)
                    for l in range(LANES)]
            o_ref[:, ob * LANES:(ob + 1) * LANES] = _assemble_block(_lane_pieces(srcs), load, RELAYOUT_ROWS, dtype)

    return pl.pallas_call(
        body, name=name, grid=(D_MODEL // RELAYOUT_ROWS,),
        in_specs=[pl.BlockSpec((N_DEV, RELAYOUT_ROWS, SHARD_IN), lambda i: (0, i, 0))],
        out_specs=pl.BlockSpec((RELAYOUT_ROWS, IN_PAD), lambda i: (i, 0)),
        out_shape=jax.ShapeDtypeStruct((D_MODEL, IN_PAD), dtype),
        compiler_params=_cparams(("parallel",)),
    )(gathered)


def _relayout_g_in(gw, name):
    kmap = _kernel_col_map()
    kinv = np.zeros(IN_DIM, np.int64)
    kinv[kmap[kmap >= 0]] = np.nonzero(kmap >= 0)[0]
    dtype = gw.dtype

    def body(src_ref, o_ref):
        load = lambda _, sb: src_ref[:, sb * LANES:(sb + 1) * LANES]
        for j in range(N_DEV):
            for ob in range(SHARD_BLOCKS):
                srcs = [(0, int(kinv[SHARD_IN * j + ob * LANES + l])) if ob * LANES + l < SHARD_IN else None for l in range(LANES)]
                blk = _assemble_block(_lane_pieces(srcs), load, RELAYOUT_ROWS, dtype)
                if ob == SHARD_BLOCKS - 1:
                    o_ref[j, :, SHARD_IN - 1:SHARD_IN] = blk[:, 0:1]
                else:
                    o_ref[j, :, ob * LANES:(ob + 1) * LANES] = blk

    return pl.pallas_call(
        body, name=name, grid=(D_MODEL // RELAYOUT_ROWS,),
        in_specs=[pl.BlockSpec((RELAYOUT_ROWS, IN_PAD), lambda i: (i, 0))],
        out_specs=pl.BlockSpec((N_DEV, RELAYOUT_ROWS, SHARD_IN), lambda i: (0, i, 0)),
        out_shape=jax.ShapeDtypeStruct((N_DEV, D_MODEL, SHARD_IN), dtype),
        compiler_params=_cparams(("parallel",)),
    )(gw)


def _rows128(flat, row_mult=8):
    n = flat.shape[0]
    per = LANES * row_mult
    total = -(-n // per) * per
    return jnp.pad(flat, (0, total - n)).reshape(total // LANES, LANES)


def _pad_lanes(v):
    return jnp.pad(v, (0, LANES - v.shape[0])).reshape(1, LANES)


def _layer_prep(i, p):
    disc, disc_vjp = jax.vjp(_s5_disc, *[p[n][i] for n in S5_NAMES])
    prep = dict(
        nw=p['norm_w'][i].reshape(1, D_MODEL), disc_vjp=disc_vjp,
        s5small=[_block_diag(t).astype(BF16) for t in disc[:4]] + [disc[4], disc[5]],
        sgw=[p['sgu_ln_w'][i].reshape(1, BW), p['sgu_ln_b'][i].reshape(1, BW), p['sgu_w'][i],
             jnp.repeat(p['sgu_b'][i].T, BW // SGU_HEADS, axis=1)],
        cb=p['m2_conv_b'][i].reshape(1, M2_CONV_CH),
        m2w=[_pad_lanes(p['m2_dt_bias'][i]), _pad_lanes(p['m2_a_log'][i]),
             jnp.repeat(p['m2_d'][i], M2_HEAD_DIM).reshape(1, BW), p['m2_norm_w'][i].reshape(1, BW)])
    touch = [t[0, 0].astype(F32) for t in prep['s5small']] + [prep['sgw'][3][0, 0], prep['m2w'][2][0, 0]]
    return prep, sum(touch[1:], touch[0])


def _layer_fwd(x, h, i, prep, w_in, other_weights, finish, before_merge=None):
    proj = _matmul(h, w_in, 1, 0, F32, 2048, 1024, 1024, f"proj{i}")
    full = dict(other_weights(proj), w_in=w_in)
    s5w = prep['s5small'] + [full['s5_w_glu']]
    ys, sre, sim = _s5_fwd(proj, *s5w, f"s5_fwd{i}")
    ys = _sgu_fwd(proj, ys, *prep['sgw'], f"sgu_fwd{i}")
    cw = full['m2_conv_w']
    xa = _m2_conv_fwd(proj, cw, prep['cb'], f"m2conv_fwd{i}")
    ys, s_in = _ssd_fwd(proj, ys, xa, *prep['m2w'], f"ssd_fwd{i}")
    scw = full['sc_conv_w']
    ys = _sc_fwd(proj, ys, scw, f"sc_fwd{i}")
    mb = full['merge_b'].reshape(N_BRANCH, 1, D_MODEL)
    if before_merge is not None:
        mb = mb + before_merge(ys)[0, 0]
    merged = _merge_fwd(proj, ys, mb, full['w_branch'], f"merge_fwd{i}")
    x_new = finish(merged, full['w_out'], x)
    saved = dict(x=x, nw=prep['nw'], h=h, proj=proj, disc_vjp=prep['disc_vjp'], s5w=s5w, sre=sre, sim=sim, sgw=prep['sgw'],
                 cw=cw, cb=prep['cb'], xa=xa, m2w=prep['m2w'], s_in=s_in, scw=scw, ys=ys, mb=mb, merged=merged)
    return x_new, saved, full


def _layer_bwd(dx_out, i, sv, full, on_large_grads=None, after_dh=None):
    g = {}
    proj = sv['proj']
    dm = _matmul(dx_out, full['w_out'], 1, 1, F32, 1024, 1024, 1024, f"dmerged{i}")
    g['w_out'] = _matmul(sv['merged'], dx_out, 0, 0, BF16, 1024, 1024, 1024, f"gw_out{i}")
    dys, dproj, g['w_branch'], dmb = _merge_bwd(proj, sv['ys'], dm, sv['mb'], full['w_branch'], f"merge_bwd{i}")
    g['merge_b'] = dmb.reshape(N_BRANCH, D_MODEL)
    dproj, dbbre, dbbim, dcre, dcim, da, dd, dwg = _s5_bwd(proj, dproj, dys, sv['sre'], sv['sim'], *sv['s5w'], f"s5_bwd{i}")
    g['s5_dense'] = (dbbre, dbbim, dcre, dcim, da, dd)
    g['s5_w_glu'] = dwg.astype(BF16)
    dproj, dlw, dlb, g['sgu_w'], dbias = _sgu_bwd(proj, dproj, dys, *sv['sgw'], f"sgu_bwd{i}")
    g['sgu_ln_w'], g['sgu_ln_b'] = dlw[0], dlb[0]
    g['sgu_b'] = dbias.reshape(SGU_CHUNK, SGU_HEADS, BW // SGU_HEADS).sum(-1).T
    dproj, dxa, ddtb, dal, ddf, dnw = _ssd_bwd(proj, dproj, sv['xa'], dys, sv['s_in'], *sv['m2w'], f"ssd_bwd{i}")
    dproj, g['m2_conv_w'], dcb = _m2_conv_bwd(proj, dproj, dxa, sv['cw'], sv['cb'], f"m2conv_bwd{i}")
    g['m2_conv_b'], g['m2_norm_w'] = dcb[0], dnw[0]
    g['m2_dt_bias'], g['m2_a_log'] = ddtb[0, :M2_HEADS], dal[0, :M2_HEADS]
    g['m2_d'] = ddf.reshape(M2_HEADS, M2_HEAD_DIM).sum(-1)
    dproj, g['sc_conv_w'] = _sc_bwd(proj, dproj, dys, sv['scw'], f"sc_bwd{i}")
    g['w_in'] = _matmul(sv['h'], dproj, 0, 0, BF16, 1024, 1024, 2048, f"gw_in{i}")
    tok = on_large_grads(g) if on_large_grads else None
    dh = _matmul(dproj, full['w_in'], 1, 1, F32, 1024, 1024, IN_PAD // 4, f"dh{i}", after=tok)
    nw = sv['nw'] if after_dh is None else sv['nw'] + after_dh(dh)[0, 0]
    dx_in, dnw_l = _rmsnorm_bwd(sv['x'], nw, dh, dx_out, f"rms_bwd{i}")
    g['norm_w'] = dnw_l[0]
    return dx_in, g


def _split8(t, axis):
    shp = t.shape
    t = t.reshape(shp[:axis] + (N_DEV, shp[axis] // N_DEV) + shp[axis + 1:])
    return jnp.moveaxis(t, axis, 0)


def _join8(t, axis):
    t = jnp.moveaxis(t, 0, axis)
    shp = t.shape
    return t.reshape(shp[:axis] + (shp[axis] * shp[axis + 1],) + shp[axis + 2:])


SHARD_AXIS = {'w_in': 2, 'w_branch': 3, 'w_out': 1, 's5_w_glu': 1, 'm2_conv_w': 2, 'sc_conv_w': 2, 'merge_b': 2}


OTHER_BIG = [n for n in BIG_SHARDED if n != 'w_in']


def _other_weights(gathered):
    return {n: _join8(t, SHARD_AXIS[n] - 1) for n, t in zip(OTHER_BIG, gathered)}


def _layer_grad_blocks(g, i):
    blocks = [_relayout_g_in(g[n], f"relayout_g_in{i}") if n == 'w_in' else _split8(g[n], SHARD_AXIS[n] - 1) for n in BIG_SHARDED]
    return [b.reshape(N_DEV, -1, b.shape[-1]) for b in blocks]


def _pair_start(blocks, tag):
    shapes = [(N_CHIP,) + b.shape[1:] for b in blocks]
    return _split_start(_plan_pair, blocks, shapes, N_CHIP * len(blocks), f"pair{tag}_start")


def _pair_sums(state, after, tag):
    mine, theirs = _split_wait(_plan_pair, state, after, f"pair{tag}_wait", with_sources=True)
    return [_pair_sum(b, t, f"pair_sum{tag}_{k}") for k, (b, t) in enumerate(zip(mine, theirs))]


def _chips_start(sums, tag, after=None):
    return _split_start(_plan_chips, sums, [s.shape for s in sums], 3 * len(sums), f"chips{tag}_start", after)


def kernel(x, norm_w, w_in, s5_lambda_re, s5_lambda_im, s5_b_re, s5_b_im, s5_c_re, s5_c_im, s5_d, s5_log_step, s5_w_glu, sgu_ln_w, sgu_ln_b, sgu_w, sgu_b, m2_conv_w, m2_conv_b, m2_dt_bias, m2_a_log, m2_d, m2_norm_w, sc_conv_w, merge_b, w_branch, w_out, final_norm_w, loss_target, m_norm_w, m_w_in, m_s5_lambda_re, m_s5_lambda_im, m_s5_b_re, m_s5_b_im, m_s5_c_re, m_s5_c_im, m_s5_d, m_s5_log_step, m_s5_w_glu, m_sgu_ln_w, m_sgu_ln_b, m_sgu_w, m_sgu_b, m_m2_conv_w, m_m2_conv_b, m_m2_dt_bias, m_m2_a_log, m_m2_d, m_m2_norm_w, m_sc_conv_w, m_merge_b, m_w_branch, m_w_out, m_final_norm_w, v_norm_w, v_w_in, v_s5_lambda_re, v_s5_lambda_im, v_s5_b_re, v_s5_b_im, v_s5_c_re, v_s5_c_im, v_s5_d, v_s5_log_step, v_s5_w_glu, v_sgu_ln_w, v_sgu_ln_b, v_sgu_w, v_sgu_b, v_m2_conv_w, v_m2_conv_b, v_m2_dt_bias, v_m2_a_log, v_m2_d, v_m2_norm_w, v_sc_conv_w, v_merge_b, v_w_branch, v_w_out, v_final_norm_w):
    loc = locals()
    p = {n: loc[n] for n in WEIGHTS}
    mom = {n: loc['m_' + n] for n in WEIGHTS}
    vel = {n: loc['v_' + n] for n in WEIGHTS}

    small_sizes = [p[n].size for n in SMALL_SHARDED]
    small_pack = _rows128(jnp.concatenate([p[n].reshape(-1) for n in SMALL_SHARDED]))
    first = [p['w_in'][0].astype(BF16)]
    gath_first, tok = _split_start(_plan_gather, first, [(N_DEV,) + first[0].shape], 7, "gather_w_in0_start")
    shards = ([(p[n][0] + tok[0, 0]).astype(BF16) for n in OTHER_BIG] + [small_pack + tok[0, 0]]
              + [(p[n][1] + tok[0, 0]).astype(BF16) for n in BIG_SHARDED])
    gath, tok = _split_start(_plan_gather, shards, [(N_DEV,) + t.shape for t in shards], 7 * len(shards), "gather_start")

    def relayed(lo, hi, after, name, started=None):
        started = gath if started is None else started
        n = (len(started) - 3) // 2
        sems, srcs, lands = started[:3], started[3:3 + n], started[3 + n:]
        plan = functools.partial(_plan_gather, first=lo)
        state, tok = _split_relay(plan, (*sems, *srcs[lo:hi], *lands[lo:hi]), after, name + "_relay")
        return (plan, state, name), tok

    def arrived(relay, after):
        plan, state, name = relay
        return _split_wait(plan, state, after, name + "_wait")

    def gathered(lo, hi, after, name, started=None):
        relay, tok = relayed(lo, hi, after, name, started)
        return arrived(relay, tok)

    later = dict(p, **{n: p[n] + tok[0, 0] for n in ('norm_w', 's5_log_step', 'sgu_b', 'm2_d')})
    preps = [_layer_prep(i, later) for i in range(DEPTH)]
    h0 = _rmsnorm_fwd(x[0], preps[0][0]['nw'], "rms_fwd0")
    got = gathered(0, 1, tok + (preps[0][1] + preps[1][1] + h0[0, 0].astype(F32)), "gather_w_in0", gath_first)
    small_full = {}

    def other_weights0(proj):
        got = gathered(0, 4, proj, "gather_rest0")
        small_all, off = got[-1].reshape(N_DEV, -1), 0
        for n, sz in zip(SMALL_SHARDED, small_sizes):
            small_full[n] = _join8(small_all[:, off:off + sz].reshape((N_DEV,) + p[n].shape), SHARD_AXIS[n])
            off += sz
        return dict(_other_weights(got[:-1]), **{n: small_full[n][0] for n in SMALL_SHARDED})

    saved, layer_g, full = [None] * DEPTH, [None] * DEPTH, [None] * DEPTH
    relay1 = []

    def relay_layer1(ys):
        relay, tok = relayed(4, 8, ys, "gather1")
        relay1.append(relay)
        return tok

    (xs, h1), saved[0], full[0] = _layer_fwd(
        x[0], h0, 0, preps[0][0], _relayout_w_in(got[0], "relayout_w_in0"), other_weights0,
        lambda merged, w_out, xin: _out_norm(merged, w_out, xin, preps[1][0]['nw'], "out0_rms_fwd1"), relay_layer1)
    got = arrived(relay1[0], h1)
    (loss_row, dx, dfw), saved[1], full[1] = _layer_fwd(
        xs, h1, 1, preps[1][0], _relayout_w_in(got[0], "relayout_w_in1"),
        lambda proj: dict(_other_weights(got[1:]), **{n: small_full[n][1] for n in SMALL_SHARDED}),
        lambda merged, w_out, xin: _out_loss(merged, w_out, xin, final_norm_w.reshape(1, D_MODEL), loss_target[0],
                                             "out1_loss_head"))
    loss = lax.psum(loss_row[0, 0], ("x", "y", "c"))
    loss, dx = lax.optimization_barrier((loss, dx))
    pairs, scat, sent0 = [None] * DEPTH, [None] * DEPTH, []

    def start_pairs1(g):
        pairs[1], tok = _pair_start(_layer_grad_blocks(g, 1), 1)
        return tok

    def send_chip_sums1(dh):
        scat[1], tok = _chips_start(_pair_sums(pairs[1], dh, 1), 1)
        return tok

    def send_all0(g):
        pairs[0], tok = _pair_start(_layer_grad_blocks(g, 0), 0)
        scat[0], tok = _chips_start(_pair_sums(pairs[0], tok, 0), 0)
        sent0.append(tok)
        return tok

    dx, layer_g[1] = _layer_bwd(dx, 1, saved[1], full[1], on_large_grads=start_pairs1, after_dh=send_chip_sums1)
    dx, layer_g[0] = _layer_bwd(dx, 0, saved[0], full[0], on_large_grads=send_all0)
    for i in range(DEPTH):
        dense = layer_g[i].pop('s5_dense')
        blocks = tuple(_diag_blocks(t, after=sent0[0]) for t in dense[:4])
        layer_g[i].update(zip(S5_NAMES, saved[i]['disc_vjp'](blocks + (dense[4] + sent0[0][0, 0], dense[5]))))
    grads = {n: jnp.stack([layer_g[i][n] for i in range(DEPTH)]) for n in SMALL_SHARDED + REPLICATED if n != 'final_norm_w'}
    grads['final_norm_w'] = dfw[0]

    out_g, out_d, out_m, out_v = {}, {}, {}, {}
    repl_rows = _pack_rows([grads[n] for n in REPLICATED], 8 * N_DEV)
    rr = repl_rows.shape[0] // N_DEV
    shard_rows = _pack_rows([_split8(grads[n], SHARD_AXIS[n]) for n in SMALL_SHARDED], 8, batched=True)
    rs = shard_rows.shape[1]
    small_g = jnp.concatenate([shard_rows, repl_rows.reshape(N_DEV, rr, LANES)], axis=1)
    all_to_all, all_gather = functools.partial(_plan_direct, gather=False), functools.partial(_plan_direct, gather=True)
    small_state, tok = _split_start(all_to_all, [small_g], [small_g.shape], N_DEV - 1, "scatter_small_start")
    landed1 = _split_wait(_plan_chips, scat[1], tok, "chips1_wait")
    landed0 = _split_wait(_plan_chips, scat[0], landed1[0], "chips0_wait")

    def big_adamw(n, after=None):
        k, shp = BIG_SHARDED.index(n), p[n].shape
        if n == 'w_in':
            return _adamw_w_in([landed0[k], landed1[k]], p[n], mom[n], vel[n], "adamw_w_in", after)
        c = shp[-1]
        r = p[n].size // (DEPTH * c)
        res = _adamw([landed0[k], landed1[k]], *[d[n].reshape(DEPTH, r, c) for d in (p, mom, vel)],
                     {'w_branch': 512, 'w_out': 128, 's5_w_glu': 64}[n], "adamw_" + n)
        return [o.reshape(shp) for o in res]

    for n in OTHER_BIG:
        out_g[n], out_d[n], out_m[n], out_v[n] = big_adamw(n)
    updated = sum(out_d[n].reshape(-1)[0] for n in OTHER_BIG).reshape(1, 1)
    small_sum = _slot_sum(_split_wait(all_to_all, small_state, updated, "scatter_small_wait")[0], "sum_small")
    repl_part = small_sum[rs:]
    repl_state, tok = _split_start(all_gather, [repl_part], [(N_DEV,) + repl_part.shape], N_DEV - 1, "gather_small_start")
    out_g['w_in'], out_d['w_in'], out_m['w_in'], out_v['w_in'] = big_adamw('w_in', tok)
    repl_all = _split_wait(all_gather, repl_state, out_d['w_in'], "gather_small_wait")[0].reshape(N_DEV * rr, LANES)
    g_all = jnp.concatenate([small_sum[:rs], repl_all], axis=0)
    names = SMALL_SHARDED + REPLICATED
    pieces = (_unpack_rows(g_all[:rs], [p[n].shape for n in SMALL_SHARDED])
              + _unpack_rows(g_all[rs:], [p[n].shape for n in REPLICATED]))
    out_g.update(zip(names, pieces))
    res = _adamw_many(*[[_memory_view(n, d[n]) for n in names] for d in (out_g, p, mom, vel)], "adamw_small")
    for r, dst in zip(res, (out_d, out_m, out_v)):
        dst.update({n: _memory_view(n, t) for n, t in zip(names, r)})
    return (loss, dx[None], *[out_g[n] for n in WEIGHTS], *[out_d[n] for n in WEIGHTS],
            *[out_m[n] for n in WEIGHTS], *[out_v[n] for n in WEIGHTS])
```

```python
import functools

import jax
import jax.numpy as jnp
import numpy as np
from jax import lax
from jax.experimental import pallas as pl
from jax.experimental.pallas import tpu as pltpu

F32 = jnp.float32
BF16 = jnp.bfloat16

N_DEV = 8
SEQ = 2048
D_MODEL = 1024
DEPTH = 2
BW = 512
N_BRANCH = 4
EPS = 1e-6
S5_GROUPS, S5_STATE, S5_P = 32, 64, 16
S5_CH = S5_GROUPS * S5_STATE
SGU_CHUNK, SGU_HEADS = 128, 8
M2_HEADS, M2_HEAD_DIM, M2_STATE, M2_CHUNK, M2_CONV = 8, 64, 128, 128, 4
M2_CONV_CH = 1024
SC_CONV = 3
IN_DIM = 10248
IN_PAD = 11264
C_MERGE = 0
C_S5U, C_S5G = 4096, 4608
C_M2X = 5120
C_SGU_U, C_SGU_V, C_SGU_G = 6144, 6656, 7168
C_M2Z, C_DT = 8192, 8704
C_SC = 9216
SHARD_IN = IN_DIM // N_DEV

ADAM_LR, ADAM_B1, ADAM_B2, ADAM_EPS, ADAM_WD, ADAM_STEP = 0.001, 0.9, 0.999, 1e-08, 0.01, 10

VMEM_LIMIT = 56 * 1024 * 1024
LANES = 128

MESH = pl.DeviceIdType.MESH


def _cparams(sem=None, **kw):
    return pltpu.CompilerParams(dimension_semantics=sem, vmem_limit_bytes=VMEM_LIMIT, **kw)


def _dg(a, b, ca, cb, precision=None):
    return lax.dot_general(a, b, (((ca,), (cb,)), ((), ())), precision=precision,
                           preferred_element_type=F32)


@functools.partial(jax.custom_vjp, nondiff_argnums=(2, 3))
def _bdot(a, b, ca, cb):
    return _dg(a.astype(BF16), b.astype(BF16), ca, cb)


def _bdot_fwd(a, b, ca, cb):
    return _bdot(a, b, ca, cb), (a, b)


def _bdot_bwd(ca, cb, res, g):
    a, b = res
    gb, ab, bb = g.astype(BF16), a.astype(BF16), b.astype(BF16)
    da = _dg(gb, bb, 1, 1 - cb) if ca == 1 else _dg(bb, gb, 1 - cb, 1)
    db = _dg(ab, gb, 1 - ca, 0) if cb == 0 else _dg(gb, ab, 0, 1 - ca)
    return da.astype(a.dtype), db.astype(b.dtype)


_bdot.defvjp(_bdot_fwd, _bdot_bwd)


def _rms(x, w):
    return x * lax.rsqrt(jnp.mean(x * x, axis=-1, keepdims=True) + EPS) * w


def _silu(x):
    return x * jax.nn.sigmoid(x)


def _gelu(x):
    return 0.5 * x * (1.0 + jnp.tanh(0.7978845608028654 * (x + 0.044715 * (x * x * x))))


def _softplus(x):
    return jnp.maximum(x, 0.0) + jnp.log1p(jnp.exp(-jnp.abs(x)))


def _shift_down(x, s):
    if s == 0:
        return x
    row = lax.broadcasted_iota(jnp.int32, x.shape, 0)
    return jnp.where(row >= s, pltpu.roll(x, s, 0), 0.0)


def _shift_up(x, s):
    if s == 0:
        return x
    n = x.shape[0]
    row = lax.broadcasted_iota(jnp.int32, x.shape, 0)
    return jnp.where(row < n - s, pltpu.roll(x, n - s, 0), 0.0)


def _matmul(a, b, ca, cb, out_dtype, tm, tn, tk, name, residual=None, after=None):
    m = a.shape[1 - ca]
    k = a.shape[ca]
    n = b.shape[1 - cb]
    assert b.shape[cb] == k and m % tm == 0 and n % tn == 0 and k % tk == 0
    nk = k // tk
    a_spec = pl.BlockSpec((tm, tk), lambda i, j, kk: (i, kk)) if ca == 1 else pl.BlockSpec((tk, tm), lambda i, j, kk: (kk, i))
    b_spec = pl.BlockSpec((tk, tn), lambda i, j, kk: (kk, j)) if cb == 0 else pl.BlockSpec((tn, tk), lambda i, j, kk: (j, kk))
    o_spec = pl.BlockSpec((tm, tn), lambda i, j, kk: (i, j))
    has_res = residual is not None

    def body(*refs):
        refs = refs[:2 + has_res] + refs[2 + has_res + (after is not None):]
        if has_res:
            a_ref, b_ref, r_ref, o_ref, acc = refs
        else:
            a_ref, b_ref, o_ref, acc = refs
        kk = pl.program_id(2)
        part = _dg(a_ref[...].astype(BF16), b_ref[...].astype(BF16), ca, cb)
        if nk == 1:
            o_ref[...] = (part + r_ref[...] if has_res else part).astype(out_dtype)
            return

        @pl.when(kk == 0)
        def _():
            acc[...] = part

        @pl.when(kk > 0)
        def _():
            acc[...] += part

        @pl.when(kk == nk - 1)
        def _():
            r = acc[...]
            if has_res:
                r = r + r_ref[...]
            o_ref[...] = r.astype(out_dtype)

    ins = [a, b] + ([residual] if has_res else []) + ([after] if after is not None else [])
    specs = [a_spec, b_spec] + ([o_spec] if has_res else []) + ([pl.BlockSpec(memory_space=pl.ANY)] if after is not None else [])
    return pl.pallas_call(
        body, name=name, grid=(m // tm, n // tn, nk), in_specs=specs, out_specs=o_spec,
        out_shape=jax.ShapeDtypeStruct((m, n), out_dtype),
        scratch_shapes=[pltpu.VMEM((tm, tn) if nk > 1 else (8, LANES), F32)],
        compiler_params=_cparams(("parallel", "parallel", "arbitrary")),
    )(*ins)


ROW_TILE = 512


def _rmsnorm_fwd(x, w, name):
    def body(x_ref, w_ref, o_ref):
        o_ref[...] = _rms(x_ref[...], w_ref[...]).astype(BF16)

    return pl.pallas_call(
        body, name=name, grid=(SEQ // ROW_TILE,),
        in_specs=[pl.BlockSpec((ROW_TILE, D_MODEL), lambda i: (i, 0)), pl.BlockSpec((1, D_MODEL), lambda i: (0, 0))],
        out_specs=pl.BlockSpec((ROW_TILE, D_MODEL), lambda i: (i, 0)),
        out_shape=jax.ShapeDtypeStruct((SEQ, D_MODEL), BF16),
        compiler_params=_cparams(("parallel",)),
    )(x, w)


def _rmsnorm_bwd(x, w, dh, dres, name):
    def body(x_ref, w_ref, dh_ref, dres_ref, dx_ref, dw_ref):
        _, vjp = jax.vjp(_rms, x_ref[...], w_ref[...])
        dx, dw = vjp(dh_ref[...])
        dx_ref[...] = dx + dres_ref[...]

        @pl.when(pl.program_id(0) == 0)
        def _():
            dw_ref[...] = dw

        @pl.when(pl.program_id(0) > 0)
        def _():
            dw_ref[...] += dw

    tile = pl.BlockSpec((ROW_TILE, D_MODEL), lambda i: (i, 0))
    vec = pl.BlockSpec((1, D_MODEL), lambda i: (0, 0))
    return pl.pallas_call(
        body, name=name, grid=(SEQ // ROW_TILE,),
        in_specs=[tile, vec, tile, tile], out_specs=[tile, vec],
        out_shape=[jax.ShapeDtypeStruct((SEQ, D_MODEL), F32), jax.ShapeDtypeStruct((1, D_MODEL), F32)],
        compiler_params=_cparams(("arbitrary",)),
    )(x, w, dh, dres)


def _dh_rms_bwd(dproj, w_in, x, w, dres, name, after=None):
    tm, tk = 512, IN_PAD // 4
    nk = IN_PAD // tk

    def body(*refs):
        a_ref, b_ref, x_ref, w_ref, dres_ref = refs[:5]
        dx_ref, dw_ref, acc = refs[-3:]
        i, kk = pl.program_id(0), pl.program_id(1)
        part = _dg(a_ref[...].astype(BF16), b_ref[...].astype(BF16), 1, 1)

        @pl.when(kk == 0)
        def _():
            acc[...] = part

        @pl.when(kk > 0)
        def _():
            acc[...] += part

        @pl.when(kk == nk - 1)
        def _():
            _, vjp = jax.vjp(_rms, x_ref[...], w_ref[...])
            dx, dw = vjp(acc[...])
            dx_ref[...] = dx + dres_ref[...]

            @pl.when(i == 0)
            def _():
                dw_ref[...] = dw

            @pl.when(i > 0)
            def _():
                dw_ref[...] += dw

    tile = pl.BlockSpec((tm, D_MODEL), lambda i, kk: (i, 0))
    vec = pl.BlockSpec((1, D_MODEL), lambda i, kk: (0, 0))
    ins = [dproj, w_in, x, w, dres] + ([after] if after is not None else [])
    specs = [pl.BlockSpec((tm, tk), lambda i, kk: (i, kk)), pl.BlockSpec((D_MODEL, tk), lambda i, kk: (0, kk)), tile, vec, tile]
    return pl.pallas_call(
        body, name=name, grid=(SEQ // tm, nk),
        in_specs=specs + ([pl.BlockSpec(memory_space=pl.ANY)] if after is not None else []), out_specs=[tile, vec],
        out_shape=[jax.ShapeDtypeStruct((SEQ, D_MODEL), F32), jax.ShapeDtypeStruct((1, D_MODEL), F32)],
        scratch_shapes=[pltpu.VMEM((tm, D_MODEL), F32)],
        compiler_params=_cparams(("arbitrary", "arbitrary")),
    )(*ins)


def _loss_head(x, w, target):
    def body(x_ref, w_ref, t_ref, loss_ref, dx_ref, dw_ref):
        tgt = t_ref[...]

        def f(xv, wv):
            err = _rms(xv, wv) - tgt
            return 0.5 * jnp.sum(jnp.mean(err * err, axis=-1))

        loss, vjp = jax.vjp(f, x_ref[...], w_ref[...])
        dx, dw = vjp(jnp.ones((), F32))
        dx_ref[...] = dx
        lrow = jnp.full((1, LANES), loss, F32)

        @pl.when(pl.program_id(0) == 0)
        def _():
            dw_ref[...] = dw
            loss_ref[...] = lrow

        @pl.when(pl.program_id(0) > 0)
        def _():
            dw_ref[...] += dw
            loss_ref[...] += lrow

    tile = pl.BlockSpec((ROW_TILE, D_MODEL), lambda i: (i, 0))
    vec = pl.BlockSpec((1, D_MODEL), lambda i: (0, 0))
    return pl.pallas_call(
        body, name="loss_head", grid=(SEQ // ROW_TILE,),
        in_specs=[tile, vec, tile], out_specs=[pl.BlockSpec((1, LANES), lambda i: (0, 0)), tile, vec],
        out_shape=[jax.ShapeDtypeStruct((1, LANES), F32), jax.ShapeDtypeStruct((SEQ, D_MODEL), F32),
                   jax.ShapeDtypeStruct((1, D_MODEL), F32)],
        compiler_params=_cparams(("arbitrary",)),
    )(x, w, target)


def _out_norm(merged, w_out, x, nw, name):
    def body(m_ref, w_ref, x_ref, nw_ref, xo_ref, h_ref):
        x_new = _dg(m_ref[...].astype(BF16), w_ref[...].astype(BF16), 1, 0) + x_ref[...]
        xo_ref[...] = x_new
        h_ref[...] = _rms(x_new, nw_ref[...]).astype(BF16)

    tile = pl.BlockSpec((ROW_TILE, D_MODEL), lambda i: (i, 0))
    return pl.pallas_call(
        body, name=name, grid=(SEQ // ROW_TILE,),
        in_specs=[tile, pl.BlockSpec((D_MODEL, D_MODEL), lambda i: (0, 0)), tile, pl.BlockSpec((1, D_MODEL), lambda i: (0, 0))],
        out_specs=[tile, tile],
        out_shape=[jax.ShapeDtypeStruct((SEQ, D_MODEL), F32), jax.ShapeDtypeStruct((SEQ, D_MODEL), BF16)],
        compiler_params=_cparams(("parallel",)),
    )(merged, w_out, x, nw)


def _out_loss(merged, w_out, x, w, target, name):
    def body(m_ref, wo_ref, x_ref, w_ref, t_ref, loss_ref, dx_ref, dw_ref):
        x_new = _dg(m_ref[...].astype(BF16), wo_ref[...].astype(BF16), 1, 0) + x_ref[...]
        tgt = t_ref[...]

        def f(xv, wv):
            err = _rms(xv, wv) - tgt
            return 0.5 * jnp.sum(jnp.mean(err * err, axis=-1))

        loss, vjp = jax.vjp(f, x_new, w_ref[...])
        dx, dw = vjp(jnp.ones((), F32))
        dx_ref[...] = dx
        lrow = jnp.full((1, LANES), loss, F32)

        @pl.when(pl.program_id(0) == 0)
        def _():
            dw_ref[...] = dw
            loss_ref[...] = lrow

        @pl.when(pl.program_id(0) > 0)
        def _():
            dw_ref[...] += dw
            loss_ref[...] += lrow

    tile = pl.BlockSpec((ROW_TILE, D_MODEL), lambda i: (i, 0))
    vec = pl.BlockSpec((1, D_MODEL), lambda i: (0, 0))
    return pl.pallas_call(
        body, name=name, grid=(SEQ // ROW_TILE,),
        in_specs=[tile, pl.BlockSpec((D_MODEL, D_MODEL), lambda i: (0, 0)), tile, vec, tile],
        out_specs=[pl.BlockSpec((1, LANES), lambda i: (0, 0)), tile, vec],
        out_shape=[jax.ShapeDtypeStruct((1, LANES), F32), jax.ShapeDtypeStruct((SEQ, D_MODEL), F32),
                   jax.ShapeDtypeStruct((1, D_MODEL), F32)],
        compiler_params=_cparams(("arbitrary",)),
    )(merged, w_out, x, w, target)


S5_T = 256
S5_BLOCKS = [(slice(j * 256, (j + 1) * 256), slice(j * 1024, (j + 1) * 1024)) for j in range(2)]


def _s5_post(ypre, gate, wglu):
    y = _gelu(ypre)
    y = y * jax.nn.sigmoid(_bdot(y, wglu, 1, 0))
    return y * _silu(gate)


def _s5_fwd(proj, bbre, bbim, cre, cim, a2, dvec, wglu, name):
    def body(u_ref, g_ref, bbre_ref, bbim_ref, cre_ref, cim_ref, a_ref, d_ref, wg_ref, o_ref, sre_ref, sim_ref, st):
        @pl.when(pl.program_id(0) == 0)
        def _():
            st[...] = jnp.zeros_like(st)

        u = u_ref[...]
        ub = u.astype(BF16)
        for us, ss in S5_BLOCKS:
            sre_ref[:, ss] = _dg(ub[:, us], bbre_ref[us, ss], 1, 0)
            sim_ref[:, ss] = _dg(ub[:, us], bbim_ref[us, ss], 1, 0)
        ar, ai = a_ref[0:1, :], a_ref[1:2, :]

        def step(t, carry):
            sr, si = carry
            nr = ar * sr - ai * si + sre_ref[pl.ds(t, 1), :]
            ni = ar * si + ai * sr + sim_ref[pl.ds(t, 1), :]
            sre_ref[pl.ds(t, 1), :] = nr
            sim_ref[pl.ds(t, 1), :] = ni
            return nr, ni

        sr, si = lax.fori_loop(0, S5_T, step, (st[0:1, :], st[1:2, :]), unroll=8)
        st[0:1, :] = sr
        st[1:2, :] = si
        ypre = jnp.concatenate(
            [_dg(sre_ref[:, ss].astype(BF16), cre_ref[ss, us], 1, 0) - _dg(sim_ref[:, ss].astype(BF16), cim_ref[ss, us], 1, 0)
             for us, ss in S5_BLOCKS], axis=1) + d_ref[...] * u
        o_ref[0] = _s5_post(ypre, g_ref[...], wg_ref[...]).astype(BF16)

    full = lambda shape: pl.BlockSpec(shape, lambda c: (0, 0))
    return pl.pallas_call(
        body, name=name, grid=(SEQ // S5_T,),
        in_specs=[pl.BlockSpec((S5_T, BW), lambda c: (c, C_S5U // BW)), pl.BlockSpec((S5_T, BW), lambda c: (c, C_S5G // BW)),
                  full((BW, S5_CH)), full((BW, S5_CH)), full((S5_CH, BW)), full((S5_CH, BW)),
                  full((2, S5_CH)), full((1, BW)), full((BW, BW))],
        out_specs=[pl.BlockSpec((1, S5_T, BW), lambda c: (0, c, 0)), pl.BlockSpec((S5_T, S5_CH), lambda c: (c, 0)),
                   pl.BlockSpec((S5_T, S5_CH), lambda c: (c, 0))],
        out_shape=[jax.ShapeDtypeStruct((N_BRANCH, SEQ, BW), BF16), jax.ShapeDtypeStruct((SEQ, S5_CH), F32),
                   jax.ShapeDtypeStruct((SEQ, S5_CH), F32)],
        scratch_shapes=[pltpu.VMEM((2, S5_CH), F32)],
        compiler_params=_cparams(("arbitrary",)),
    )(proj, proj, bbre, bbim, cre, cim, a2, dvec, wglu)


def _s5_bwd(proj, dproj, dout, sre, sim, bbre, bbim, cre, cim, a2, dvec, wglu, name):
    nc = SEQ // S5_T

    def body(u_ref, g_ref, do_ref, sre_ref, sim_ref, pre_ref, pim_ref, bbre_ref, bbim_ref, cre_ref, cim_ref, a_ref,
             d_ref, wg_ref, dproj_in, dp_ref, dbbre_ref, dbbim_ref, dcre_ref, dcim_ref, da_ref, dd_ref, dwg_ref,
             gre, gim, st):
        c = nc - 1 - pl.program_id(0)

        @pl.when(pl.program_id(0) == 0)
        def _():
            st[...] = jnp.zeros_like(st)
            for r in (dbbre_ref, dbbim_ref, dcre_ref, dcim_ref, da_ref, dd_ref, dwg_ref):
                r[...] = jnp.zeros_like(r)

        u = u_ref[...]
        s_re, s_im = sre_ref[...], sim_ref[...]

        def head(s_res, s_ims, cres, cims, dv, uv, gv, wg):
            ypre = jnp.concatenate([_bdot(sr, cr, 1, 0) - _bdot(si, ci, 1, 0)
                                    for sr, si, cr, ci in zip(s_res, s_ims, cres, cims)], axis=1) + dv * uv
            return _s5_post(ypre, gv, wg)

        _, vjp = jax.vjp(head, [sre_ref[:, ss] for _, ss in S5_BLOCKS], [sim_ref[:, ss] for _, ss in S5_BLOCKS],
                         [cre_ref[ss, us].astype(F32) for us, ss in S5_BLOCKS],
                         [cim_ref[ss, us].astype(F32) for us, ss in S5_BLOCKS],
                         d_ref[...], u, g_ref[...], wg_ref[...].astype(F32))
        ds_res, ds_ims, dcres, dcims, dd, du_d, dgate, dwg = vjp(do_ref[0])
        for k, (us, ss) in enumerate(S5_BLOCKS):
            dcre_ref[ss, us] += dcres[k]
            dcim_ref[ss, us] += dcims[k]
            gre[:, ss] = ds_res[k]
            gim[:, ss] = ds_ims[k]
        dd_ref[...] += dd
        dwg_ref[...] += dwg
        dp_ref[:, BW:] = dgate.astype(BF16)
        ar, ai = a_ref[0:1, :], a_ref[1:2, :]

        def step(i, carry):
            t = S5_T - 1 - i
            gr, gi = carry
            nr = gre[pl.ds(t, 1), :] + gr
            ni = gim[pl.ds(t, 1), :] + gi
            gre[pl.ds(t, 1), :] = nr
            gim[pl.ds(t, 1), :] = ni
            return ar * nr + ai * ni, ar * ni - ai * nr

        gr, gi = lax.fori_loop(0, S5_T, step, (st[0:1, :], st[1:2, :]), unroll=8)
        st[0:1, :] = gr
        st[1:2, :] = gi
        g_re, g_im = gre[...], gim[...]
        first = jnp.where(c > 0, 1.0, 0.0)
        row = lax.broadcasted_iota(jnp.int32, (S5_T, S5_CH), 0)
        p_re = jnp.where(row == 0, pre_ref[7:8, :] * first, pltpu.roll(s_re, 1, 0))
        p_im = jnp.where(row == 0, pim_ref[7:8, :] * first, pltpu.roll(s_im, 1, 0))
        da_ref[0:1, :] += jnp.sum(g_re * p_re + g_im * p_im, axis=0, keepdims=True)
        da_ref[1:2, :] += jnp.sum(g_im * p_re - g_re * p_im, axis=0, keepdims=True)
        ub, grb, gib = u.astype(BF16), g_re.astype(BF16), g_im.astype(BF16)
        du_s = []
        for us, ss in S5_BLOCKS:
            dbbre_ref[us, ss] += _dg(ub[:, us], grb[:, ss], 0, 0)
            dbbim_ref[us, ss] += _dg(ub[:, us], gib[:, ss], 0, 0)
            du_s.append(_dg(grb[:, ss], bbre_ref[us, ss], 1, 1) + _dg(gib[:, ss], bbim_ref[us, ss], 1, 1))
        dp_ref[:, :BW] = (du_d + jnp.concatenate(du_s, axis=1)).astype(BF16)

    full = lambda shape: pl.BlockSpec(shape, lambda i: (0, 0))
    rev = lambda w, col=0: pl.BlockSpec((S5_T, w), lambda i: (nc - 1 - i, col))
    prev = pl.BlockSpec((8, S5_CH), lambda i: (jnp.maximum((nc - 1 - i) * (S5_T // 8) - 1, 0), 0))
    return pl.pallas_call(
        body, name=name, grid=(nc,),
        in_specs=[rev(BW, C_S5U // BW), rev(BW, C_S5G // BW), pl.BlockSpec((1, S5_T, BW), lambda i: (0, nc - 1 - i, 0)),
                  rev(S5_CH), rev(S5_CH), prev, prev,
                  full((BW, S5_CH)), full((BW, S5_CH)), full((S5_CH, BW)), full((S5_CH, BW)),
                  full((2, S5_CH)), full((1, BW)), full((BW, BW)), pl.BlockSpec(memory_space=pl.ANY)],
        out_specs=[rev(2 * BW, C_S5U // (2 * BW)), full((BW, S5_CH)), full((BW, S5_CH)), full((S5_CH, BW)), full((S5_CH, BW)),
                   full((2, S5_CH)), full((1, BW)), full((BW, BW))],
        input_output_aliases={14: 0},
        out_shape=[jax.ShapeDtypeStruct((SEQ, IN_PAD), BF16),
                   jax.ShapeDtypeStruct((BW, S5_CH), F32), jax.ShapeDtypeStruct((BW, S5_CH), F32),
                   jax.ShapeDtypeStruct((S5_CH, BW), F32), jax.ShapeDtypeStruct((S5_CH, BW), F32),
                   jax.ShapeDtypeStruct((2, S5_CH), F32), jax.ShapeDtypeStruct((1, BW), F32),
                   jax.ShapeDtypeStruct((BW, BW), F32)],
        scratch_shapes=[pltpu.VMEM((S5_T, S5_CH), F32), pltpu.VMEM((S5_T, S5_CH), F32), pltpu.VMEM((2, S5_CH), F32)],
        compiler_params=_cparams(("arbitrary",)),
    )(proj, proj, dout, sre, sim, sre, sim, bbre, bbim, cre, cim, a2, dvec, wglu, dproj)


def _diag_blocks(dense, after=None):
    rows, cols = dense.shape
    rows_per, cols_per = rows // S5_GROUPS, cols // S5_GROUPS
    per_lane_block = LANES // cols_per
    tile = 512

    def body(d_ref, *rest):
        o_ref = rest[-1]
        r0 = pl.program_id(0) * tile
        grp = (r0 + lax.broadcasted_iota(jnp.int32, (tile, LANES), 0)) // rows_per
        lane = lax.broadcasted_iota(jnp.int32, (tile, LANES), 1)
        acc = jnp.zeros((tile, LANES), F32)
        for hb in range(cols // LANES):
            acc = acc + jnp.where(grp == per_lane_block * hb + lane // cols_per, d_ref[:, hb * LANES:(hb + 1) * LANES], 0.0)
        shift = LANES // 2
        while shift >= cols_per:
            acc = acc + pltpu.roll(acc, LANES - shift, 1)
            shift //= 2
        o_ref[...] = acc

    folded = pl.pallas_call(
        body, name=f"diag_blocks_{rows_per}x{cols_per}", grid=(rows // tile,),
        in_specs=[pl.BlockSpec((tile, cols), lambda i: (i, 0))] + ([] if after is None else [pl.BlockSpec(memory_space=pl.ANY)]),
        out_specs=pl.BlockSpec((tile, LANES), lambda i: (i, 0)),
        out_shape=jax.ShapeDtypeStruct((rows, LANES), F32), compiler_params=_cparams(("parallel",)),
    )(dense, *([] if after is None else [after]))
    return folded[:, :cols_per].reshape(S5_GROUPS, rows_per, cols_per)


def _block_diag(t):
    g, rows_per, cols_per = t.shape
    wide = jnp.tile(t.reshape(g * rows_per, cols_per), (1, g))
    r = lax.broadcasted_iota(jnp.int32, wide.shape, 0) // rows_per
    c = lax.broadcasted_iota(jnp.int32, wide.shape, 1) // cols_per
    return jnp.where(r == c, wide, 0.0)


def _s5_disc(lam_re, lam_im, b_re, b_im, c_re, c_im, d, log_step):
    step = jnp.exp(log_step)[:, None]
    mag = jnp.exp(lam_re * step)
    ab_re, ab_im = mag * jnp.cos(lam_im * step), mag * jnp.sin(lam_im * step)
    den = lam_re * lam_re + lam_im * lam_im
    nr = ab_re - 1.0
    coef_re = (nr * lam_re + ab_im * lam_im) / den
    coef_im = (ab_im * lam_re - nr * lam_im) / den
    bb_re = coef_re[..., None] * b_re - coef_im[..., None] * b_im
    bb_im = coef_re[..., None] * b_im + coef_im[..., None] * b_re
    a2 = jnp.stack([ab_re.reshape(-1), ab_im.reshape(-1)])
    return (jnp.swapaxes(bb_re, 1, 2), jnp.swapaxes(bb_im, 1, 2),
            jnp.swapaxes(c_re, 1, 2), jnp.swapaxes(c_im, 1, 2),
            a2, d.reshape(1, BW))


def _left_lanes(shape):
    return lax.broadcasted_iota(jnp.int32, shape, 1) < 64


def _sgu_chunk(u, v, gate, ln_w, ln_b, w, bias):
    u32, v32 = _gelu(u), _gelu(v)
    mu = jnp.mean(v32, axis=-1, keepdims=True)
    var = jnp.mean(jnp.square(v32 - mu), axis=-1, keepdims=True)
    vn = (v32 - mu) * lax.rsqrt(var + EPS) * ln_w + ln_b
    t_i = lax.broadcasted_iota(jnp.int32, (SGU_CHUNK, SGU_CHUNK), 0)
    s_i = lax.broadcasted_iota(jnp.int32, (SGU_CHUNK, SGU_CHUNK), 1)
    causal = t_i >= s_i
    left = _left_lanes((SGU_CHUNK, LANES))
    sgate = _silu(gate)
    outs = []
    for j in range(BW // LANES):
        vb = vn[:, j * LANES:(j + 1) * LANES]
        s_blk = (_bdot(jnp.where(causal, w[2 * j], 0.0), jnp.where(left, vb, 0.0), 1, 0)
                 + _bdot(jnp.where(causal, w[2 * j + 1], 0.0), jnp.where(left, 0.0, vb), 1, 0))
        sl = slice(j * LANES, (j + 1) * LANES)
        outs.append(u32[:, sl] * (s_blk + bias[:, sl]) * sgate[:, sl])
    return outs


def _sgu_fwd(proj, ys, ln_w, ln_b, w, bias, name):
    def body(u_ref, v_ref, g_ref, lw_ref, lb_ref, w_ref, b_ref, ys_in, o_ref):
        outs = _sgu_chunk(u_ref[...], v_ref[...], g_ref[...], lw_ref[...], lb_ref[...], w_ref[...], b_ref[...])
        for j, o in enumerate(outs):
            o_ref[0, :, j * LANES:(j + 1) * LANES] = o.astype(BF16)

    blk = lambda col: pl.BlockSpec((SGU_CHUNK, BW), lambda c: (c, col // BW))
    vec = pl.BlockSpec((1, BW), lambda c: (0, 0))
    return pl.pallas_call(
        body, name=name, grid=(SEQ // SGU_CHUNK,),
        in_specs=[blk(C_SGU_U), blk(C_SGU_V), blk(C_SGU_G), vec, vec,
                  pl.BlockSpec((SGU_HEADS, SGU_CHUNK, SGU_CHUNK), lambda c: (0, 0, 0)),
                  pl.BlockSpec((SGU_CHUNK, BW), lambda c: (0, 0)), pl.BlockSpec(memory_space=pl.ANY)],
        out_specs=pl.BlockSpec((1, SGU_CHUNK, BW), lambda c: (1, c, 0)),
        out_shape=jax.ShapeDtypeStruct((N_BRANCH, SEQ, BW), BF16), input_output_aliases={7: 0},
        compiler_params=_cparams(("parallel",)),
    )(proj, proj, proj, ln_w, ln_b, w, bias, ys)


def _sgu_bwd(proj, dproj, dout, ln_w, ln_b, w, bias, name):
    def body(u_ref, v_ref, g_ref, do_ref, lw_ref, lb_ref, w_ref, b_ref, dproj_in, dp_ref, dlw_ref, dlb_ref, dw_ref, db_ref):
        _, vjp = jax.vjp(_sgu_chunk, u_ref[...], v_ref[...], g_ref[...], lw_ref[...], lb_ref[...], w_ref[...], b_ref[...])
        do = do_ref[0]
        du, dv, dgate, dlw, dlb, dw, db = vjp([do[:, j * LANES:(j + 1) * LANES] for j in range(BW // LANES)])
        dp_ref[:, 0:BW] = du.astype(BF16)
        dp_ref[:, BW:2 * BW] = dv.astype(BF16)
        dp_ref[:, 2 * BW:3 * BW] = dgate.astype(BF16)
        dp_ref[:, 3 * BW:] = jnp.zeros((SGU_CHUNK, BW), BF16)

        @pl.when(pl.program_id(0) == 0)
        def _():
            dlw_ref[...] = dlw
            dlb_ref[...] = dlb
            dw_ref[...] = dw
            db_ref[...] = db

        @pl.when(pl.program_id(0) > 0)
        def _():
            dlw_ref[...] += dlw
            dlb_ref[...] += dlb
            dw_ref[...] += dw
            db_ref[...] += db

    blk = lambda col: pl.BlockSpec((SGU_CHUNK, BW), lambda c: (c, col // BW))
    vec = pl.BlockSpec((1, BW), lambda c: (0, 0))
    wsp = pl.BlockSpec((SGU_HEADS, SGU_CHUNK, SGU_CHUNK), lambda c: (0, 0, 0))
    bsp = pl.BlockSpec((SGU_CHUNK, BW), lambda c: (0, 0))
    return pl.pallas_call(
        body, name=name, grid=(SEQ // SGU_CHUNK,),
        in_specs=[blk(C_SGU_U), blk(C_SGU_V), blk(C_SGU_G), pl.BlockSpec((1, SGU_CHUNK, BW), lambda c: (1, c, 0)),
                  vec, vec, wsp, bsp, pl.BlockSpec(memory_space=pl.ANY)],
        out_specs=[pl.BlockSpec((SGU_CHUNK, 4 * BW), lambda c: (c, C_SGU_U // (4 * BW))), vec, vec, wsp, bsp],
        input_output_aliases={8: 0},
        out_shape=[jax.ShapeDtypeStruct((SEQ, IN_PAD), BF16), jax.ShapeDtypeStruct((1, BW), F32),
                   jax.ShapeDtypeStruct((1, BW), F32), jax.ShapeDtypeStruct((SGU_HEADS, SGU_CHUNK, SGU_CHUNK), F32),
                   jax.ShapeDtypeStruct((SGU_CHUNK, BW), F32)],
        compiler_params=_cparams(("arbitrary",)),
    )(proj, proj, proj, dout, ln_w, ln_b, w, bias, dproj)


CONV_BLK = 256


def _m2_conv_fwd(proj, w, b, name):
    def body(x_ref, w_ref, b_ref, o_ref):
        x = x_ref[...]
        acc = jnp.zeros_like(x) + b_ref[...]
        for k in range(M2_CONV):
            acc = acc + w_ref[k:k + 1, :] * _shift_down(x, M2_CONV - 1 - k)
        o_ref[...] = _silu(acc)

    return pl.pallas_call(
        body, name=name, grid=(M2_CONV_CH // CONV_BLK,),
        in_specs=[pl.BlockSpec((SEQ, CONV_BLK), lambda j: (0, C_M2X // CONV_BLK + j)),
                  pl.BlockSpec((M2_CONV, CONV_BLK), lambda j: (0, j)), pl.BlockSpec((1, CONV_BLK), lambda j: (0, j))],
        out_specs=pl.BlockSpec((SEQ, CONV_BLK), lambda j: (0, j)),
        out_shape=jax.ShapeDtypeStruct((SEQ, M2_CONV_CH), F32),
        compiler_params=_cparams(("parallel",)),
    )(proj, w, b)


def _m2_conv_bwd(proj, dproj, dxa, w, b, name):
    def body(x_ref, d_ref, w_ref, b_ref, dproj_in, dx_ref, dw_ref, db_ref):
        x = x_ref[...]
        xs = [_shift_down(x, M2_CONV - 1 - k) for k in range(M2_CONV)]
        acc = jnp.zeros_like(x) + b_ref[...]
        for k in range(M2_CONV):
            acc = acc + w_ref[k:k + 1, :] * xs[k]
        sg = jax.nn.sigmoid(acc)
        dacc = d_ref[...] * (sg * (1.0 + acc * (1.0 - sg)))
        dx = jnp.zeros_like(x)
        for k in range(M2_CONV):
            dx = dx + w_ref[k:k + 1, :] * _shift_up(dacc, M2_CONV - 1 - k)
            dw_ref[k:k + 1, :] = jnp.sum(dacc * xs[k], axis=0, keepdims=True)
        dx_ref[...] = dx.astype(BF16)
        db_ref[...] = jnp.sum(dacc, axis=0, keepdims=True)

    return pl.pallas_call(
        body, name=name, grid=(M2_CONV_CH // CONV_BLK,),
        in_specs=[pl.BlockSpec((SEQ, CONV_BLK), lambda j: (0, C_M2X // CONV_BLK + j)),
                  pl.BlockSpec((SEQ, CONV_BLK), lambda j: (0, j)),
                  pl.BlockSpec((M2_CONV, CONV_BLK), lambda j: (0, j)), pl.BlockSpec((1, CONV_BLK), lambda j: (0, j)),
                  pl.BlockSpec(memory_space=pl.ANY)],
        out_specs=[pl.BlockSpec((SEQ, CONV_BLK), lambda j: (0, C_M2X // CONV_BLK + j)),
                   pl.BlockSpec((M2_CONV, CONV_BLK), lambda j: (0, j)), pl.BlockSpec((1, CONV_BLK), lambda j: (0, j))],
        input_output_aliases={4: 0},
        out_shape=[jax.ShapeDtypeStruct((SEQ, IN_PAD), BF16), jax.ShapeDtypeStruct((M2_CONV, M2_CONV_CH), F32),
                   jax.ShapeDtypeStruct((1, M2_CONV_CH), F32)],
        compiler_params=_cparams(("parallel",)),
    )(proj, dxa, w, b, dproj)


N_PAIR = M2_HEADS // 2
HI = lax.Precision.HIGHEST


def _col(a, h):
    lane = lax.broadcasted_iota(jnp.int32, a.shape, 1)
    return jnp.sum(jnp.where(lane == h, a, 0.0), axis=1, keepdims=True)


def _row(a, h):
    sub = lax.broadcasted_iota(jnp.int32, a.shape, 0)
    return jnp.sum(jnp.where(sub == h, a, 0.0), axis=0, keepdims=True)


def _ssd_chunk(xs, bms, cms, dtr, zs, states, dt_bias, a_log, dfs, nws):
    q = M2_CHUNK
    dt = _softplus(dtr + dt_bias)
    da = dt * (-jnp.exp(a_log))
    l_i = lax.broadcasted_iota(jnp.int32, (q, q), 0)
    s_i = lax.broadcasted_iota(jnp.int32, (q, q), 1)
    causal = l_i >= s_i
    tril = jnp.where(causal, 1.0, 0.0)
    a_cs = _dg(tril, da, 1, 0, HI)
    a_cs_t = _dg(da, tril, 0, 1, HI)
    a_end = _row(a_cs, q - 1)
    left = _left_lanes((q, LANES))
    left1 = _left_lanes((1, LANES))
    ys, nexts = [], []
    for j in range(N_PAIR):
        grp = j // 2
        bm, cm = bms[grp], cms[grp]
        h0, h1 = 2 * j, 2 * j + 1
        cb = _bdot(cm, bm, 1, 1)
        xdt = xs[j] * jnp.where(left, _col(dt, h0), _col(dt, h1))
        acs0, acs1 = _col(a_cs, h0), _col(a_cs, h1)
        y = _bdot(cm, states[j], 1, 0) * jnp.where(left, jnp.exp(acs0), jnp.exp(acs1))
        s_new = states[j] * jnp.where(left1, jnp.exp(_col(a_end, h0)), jnp.exp(_col(a_end, h1)))
        for h, acs, xh in ((h0, acs0, jnp.where(left, xdt, 0.0)), (h1, acs1, jnp.where(left, 0.0, xdt))):
            decay = jnp.exp(jnp.where(causal, acs - _row(a_cs_t, h), -jnp.inf))
            y = y + _bdot(cb * decay, xh, 1, 0)
            s_new = s_new + _bdot(bm * jnp.exp(_col(a_end, h) - acs), xh, 0, 0)
        ys.append((y + dfs[j] * xs[j]) * _silu(zs[j]))
        nexts.append(s_new)
    ssq = sum(jnp.sum(y * y, axis=-1, keepdims=True) for y in ys)
    scale = lax.rsqrt(ssq / BW + EPS)
    return [y * scale * nw for y, nw in zip(ys, nws)], nexts


def _blocks(ref, n, width=LANES):
    return [ref[:, j * width:(j + 1) * width] for j in range(n)]


def _ssd_fwd(proj, ys, xa, dt_bias, a_log, dfull, nw, name):
    nc = SEQ // M2_CHUNK

    def body(x_ref, b_ref, c_ref, dt_ref, z_ref, dtb_ref, al_ref, df_ref, nw_ref, ys_in, o_ref, sin_ref, st):
        @pl.when(pl.program_id(0) == 0)
        def _():
            st[...] = jnp.zeros_like(st)

        states = [st[j] for j in range(N_PAIR)]
        for j in range(N_PAIR):
            sin_ref[0, j] = states[j]
        ys, nexts = _ssd_chunk(_blocks(x_ref, 4), _blocks(b_ref, 2), _blocks(c_ref, 2), dt_ref[...], _blocks(z_ref, 4),
                               states, dtb_ref[...], al_ref[...], _blocks(df_ref, 4), _blocks(nw_ref, 4))
        for j in range(N_PAIR):
            o_ref[0, :, j * LANES:(j + 1) * LANES] = ys[j].astype(BF16)
            st[j] = nexts[j]

    vec8 = pl.BlockSpec((1, LANES), lambda c: (0, 0))
    vec = pl.BlockSpec((1, BW), lambda c: (0, 0))
    return pl.pallas_call(
        body, name=name, grid=(nc,),
        in_specs=[pl.BlockSpec((M2_CHUNK, BW), lambda c: (c, 0)), pl.BlockSpec((M2_CHUNK, 256), lambda c: (c, 2)),
                  pl.BlockSpec((M2_CHUNK, 256), lambda c: (c, 3)), pl.BlockSpec((M2_CHUNK, LANES), lambda c: (c, C_DT // LANES)),
                  pl.BlockSpec((M2_CHUNK, BW), lambda c: (c, C_M2Z // BW)), vec8, vec8, vec, vec,
                  pl.BlockSpec(memory_space=pl.ANY)],
        out_specs=[pl.BlockSpec((1, M2_CHUNK, BW), lambda c: (2, c, 0)),
                   pl.BlockSpec((1, N_PAIR, M2_STATE, LANES), lambda c: (c, 0, 0, 0))],
        out_shape=[jax.ShapeDtypeStruct((N_BRANCH, SEQ, BW), BF16), jax.ShapeDtypeStruct((nc, N_PAIR, M2_STATE, LANES), F32)],
        input_output_aliases={9: 0},
        scratch_shapes=[pltpu.VMEM((N_PAIR, M2_STATE, LANES), F32)],
        compiler_params=_cparams(("arbitrary",)),
    )(xa, xa, xa, proj, proj, dt_bias, a_log, dfull, nw, ys)


def _ssd_bwd(proj, dproj, xa, dout, s_in, dt_bias, a_log, dfull, nw, name):
    nc = SEQ // M2_CHUNK

    def body(x_ref, b_ref, c_ref, dt_ref, z_ref, do_ref, sin_ref, dtb_ref, al_ref, df_ref, nw_ref, dproj_in,
             dp_ref, dxa_ref, ddtb_ref, dal_ref, ddf_ref, dnw_ref, dst):
        @pl.when(pl.program_id(0) == 0)
        def _():
            dst[...] = jnp.zeros_like(dst)
            for r in (ddtb_ref, dal_ref, ddf_ref, dnw_ref):
                r[...] = jnp.zeros_like(r)

        states = [sin_ref[0, j] for j in range(N_PAIR)]
        _, vjp = jax.vjp(_ssd_chunk, _blocks(x_ref, 4), _blocks(b_ref, 2), _blocks(c_ref, 2), dt_ref[...],
                         _blocks(z_ref, 4), states, dtb_ref[...], al_ref[...], _blocks(df_ref, 4), _blocks(nw_ref, 4))
        dxs, dbs, dcs, ddt, dzs, dstates, ddtb, dal, ddfs, dnws = vjp(
            ([do_ref[0, :, j * LANES:(j + 1) * LANES] for j in range(N_PAIR)], [dst[j] for j in range(N_PAIR)]))
        for j in range(N_PAIR):
            sl = slice(j * LANES, (j + 1) * LANES)
            dxa_ref[:, sl] = dxs[j]
            dp_ref[:, sl] = dzs[j].astype(BF16)
            dst[j] = dstates[j]
            ddf_ref[:, sl] += ddfs[j]
            dnw_ref[:, sl] += dnws[j]
        for g in range(2):
            dxa_ref[:, BW + g * LANES:BW + (g + 1) * LANES] = dbs[g]
            dxa_ref[:, BW + 256 + g * LANES:BW + 256 + (g + 1) * LANES] = dcs[g]
        dp_ref[:, BW:BW + LANES] = ddt.astype(BF16)
        dp_ref[:, BW + LANES:] = jnp.zeros((M2_CHUNK, 2 * BW - BW - LANES), BF16)
        ddtb_ref[...] += ddtb
        dal_ref[...] += dal

    rev = lambda w, col=0: pl.BlockSpec((M2_CHUNK, w), lambda i: (nc - 1 - i, col))
    vec8 = pl.BlockSpec((1, LANES), lambda i: (0, 0))
    vec = pl.BlockSpec((1, BW), lambda i: (0, 0))
    return pl.pallas_call(
        body, name=name, grid=(nc,),
        in_specs=[rev(BW), rev(256, 2), rev(256, 3), rev(LANES, C_DT // LANES), rev(BW, C_M2Z // BW),
                  pl.BlockSpec((1, M2_CHUNK, BW), lambda i: (2, nc - 1 - i, 0)),
                  pl.BlockSpec((1, N_PAIR, M2_STATE, LANES), lambda i: (nc - 1 - i, 0, 0, 0)), vec8, vec8, vec, vec,
                  pl.BlockSpec(memory_space=pl.ANY)],
        out_specs=[rev(2 * BW, C_M2Z // (2 * BW)), rev(M2_CONV_CH), vec8, vec8, vec, vec],
        input_output_aliases={11: 0},
        out_shape=[jax.ShapeDtypeStruct((SEQ, IN_PAD), BF16), jax.ShapeDtypeStruct((SEQ, M2_CONV_CH), F32),
                   jax.ShapeDtypeStruct((1, LANES), F32), jax.ShapeDtypeStruct((1, LANES), F32),
                   jax.ShapeDtypeStruct((1, BW), F32), jax.ShapeDtypeStruct((1, BW), F32)],
        scratch_shapes=[pltpu.VMEM((N_PAIR, M2_STATE, LANES), F32)],
        compiler_params=_cparams(("arbitrary",)),
    )(xa, xa, xa, proj, proj, dout, s_in, dt_bias, a_log, dfull, nw, dproj)


def _sc_specs():
    col = lambda kind: pl.BlockSpec((SEQ, LANES), lambda j: (0, C_SC // LANES + 4 * j + kind))
    return [col(0), col(1), col(2), col(3)]


def _sc_fwd(proj, ys, w, name):
    def body(b_ref, c_ref, h_ref, g_ref, w_ref, ys_in, o_ref):
        ch = c_ref[...] * h_ref[...]
        acc = jnp.zeros_like(ch)
        for k in range(SC_CONV):
            acc = acc + w_ref[k:k + 1, :] * _shift_down(ch, SC_CONV - 1 - k)
        o_ref[0] = (b_ref[...] * acc * _silu(g_ref[...])).astype(BF16)

    return pl.pallas_call(
        body, name=name, grid=(BW // LANES,),
        in_specs=_sc_specs() + [pl.BlockSpec((SC_CONV, LANES), lambda j: (0, j)), pl.BlockSpec(memory_space=pl.ANY)],
        out_specs=pl.BlockSpec((1, SEQ, LANES), lambda j: (3, 0, j)),
        out_shape=jax.ShapeDtypeStruct((N_BRANCH, SEQ, BW), BF16), input_output_aliases={5: 0},
        compiler_params=_cparams(("parallel",)),
    )(proj, proj, proj, proj, w, ys)


def _sc_bwd(proj, dproj, dout, w, name):
    def body(b_ref, c_ref, h_ref, g_ref, do_ref, w_ref, dproj_in, dp_ref, dw_ref):
        cv, hv, gv = c_ref[...], h_ref[...], g_ref[...]
        ch = cv * hv
        chs = [_shift_down(ch, SC_CONV - 1 - k) for k in range(SC_CONV)]
        acc = jnp.zeros_like(ch)
        for k in range(SC_CONV):
            acc = acc + w_ref[k:k + 1, :] * chs[k]
        sg = jax.nn.sigmoid(gv)
        do = do_ref[0]
        bv = b_ref[...]
        dp_ref[:, 0:LANES] = (do * acc * (gv * sg)).astype(BF16)
        dp_ref[:, 3 * LANES:] = (do * bv * acc * (sg * (1.0 + gv * (1.0 - sg)))).astype(BF16)
        dacc = do * bv * (gv * sg)
        dch = jnp.zeros_like(ch)
        for k in range(SC_CONV):
            dch = dch + w_ref[k:k + 1, :] * _shift_up(dacc, SC_CONV - 1 - k)
            dw_ref[k:k + 1, :] = jnp.sum(dacc * chs[k], axis=0, keepdims=True)
        dp_ref[:, LANES:2 * LANES] = (dch * hv).astype(BF16)
        dp_ref[:, 2 * LANES:3 * LANES] = (dch * cv).astype(BF16)

    wsp = pl.BlockSpec((SC_CONV, LANES), lambda j: (0, j))
    return pl.pallas_call(
        body, name=name, grid=(BW // LANES,),
        in_specs=_sc_specs() + [pl.BlockSpec((1, SEQ, LANES), lambda j: (3, 0, j)), wsp, pl.BlockSpec(memory_space=pl.ANY)],
        out_specs=[pl.BlockSpec((SEQ, 4 * LANES), lambda j: (0, C_SC // (4 * LANES) + j)), wsp],
        input_output_aliases={6: 0},
        out_shape=[jax.ShapeDtypeStruct((SEQ, IN_PAD), BF16), jax.ShapeDtypeStruct((SC_CONV, BW), F32)],
        compiler_params=_cparams(("parallel",)),
    )(proj, proj, proj, proj, dout, w, dproj)


MERGE_T = 256
MERGE_BWD_T = 1024


def _merge_fwd(proj, ys, merge_b, w_branch, name):
    def body(y_ref, lg_ref, b_ref, w_ref, o_ref):
        acc = jnp.zeros((MERGE_T, D_MODEL), F32)
        for k in range(N_BRANCH):
            gate = jax.nn.sigmoid(lg_ref[:, k * D_MODEL:(k + 1) * D_MODEL] + b_ref[k])
            acc = acc + gate * _dg(y_ref[k], w_ref[k], 1, 0)
        o_ref[...] = acc.astype(BF16)

    return pl.pallas_call(
        body, name=name, grid=(SEQ // MERGE_T,),
        in_specs=[pl.BlockSpec((N_BRANCH, MERGE_T, BW), lambda i: (0, i, 0)),
                  pl.BlockSpec((MERGE_T, N_BRANCH * D_MODEL), lambda i: (i, C_MERGE // (N_BRANCH * D_MODEL))),
                  pl.BlockSpec((N_BRANCH, 1, D_MODEL), lambda i: (0, 0, 0)),
                  pl.BlockSpec((N_BRANCH, BW, D_MODEL), lambda i: (0, 0, 0))],
        out_specs=pl.BlockSpec((MERGE_T, D_MODEL), lambda i: (i, 0)),
        out_shape=jax.ShapeDtypeStruct((SEQ, D_MODEL), BF16),
        compiler_params=_cparams(("parallel",)),
    )(ys, proj, merge_b, w_branch)


def _merge_bwd(proj, ys, dm, merge_b, w_branch, name):
    nt = SEQ // MERGE_BWD_T

    def body(y_ref, lg_ref, dm_ref, b_ref, w_ref, dy_ref, dlg_ref, dw_ref, db_ref, dw_acc):
        i = pl.program_id(1)
        gate = jax.nn.sigmoid(lg_ref[...] + b_ref[0])
        y = y_ref[0]
        dmv = dm_ref[...]
        dbo = (gate * dmv).astype(BF16)
        dlg = _dg(y, w_ref[0], 1, 0) * dmv * gate * (1.0 - gate)
        dlg_ref[...] = dlg.astype(BF16)
        dy_ref[0] = _dg(dbo, w_ref[0], 1, 1)
        dwp = _dg(y, dbo, 0, 0)
        dbp = jnp.sum(dlg, axis=0, keepdims=True)

        @pl.when(i == 0)
        def _():
            dw_acc[...] = dwp
            db_ref[0] = dbp

        @pl.when(i > 0)
        def _():
            dw_acc[...] += dwp
            db_ref[0] += dbp

        @pl.when(i == nt - 1)
        def _():
            dw_ref[0] = dw_acc[...].astype(BF16)

    return pl.pallas_call(
        body, name=name, grid=(N_BRANCH, nt),
        in_specs=[pl.BlockSpec((1, MERGE_BWD_T, BW), lambda k, i: (k, i, 0)),
                  pl.BlockSpec((MERGE_BWD_T, D_MODEL), lambda k, i: (i, C_MERGE // D_MODEL + k)),
                  pl.BlockSpec((MERGE_BWD_T, D_MODEL), lambda k, i: (i, 0)),
                  pl.BlockSpec((1, 1, D_MODEL), lambda k, i: (k, 0, 0)),
                  pl.BlockSpec((1, BW, D_MODEL), lambda k, i: (k, 0, 0))],
        out_specs=[pl.BlockSpec((1, MERGE_BWD_T, BW), lambda k, i: (k, i, 0)),
                   pl.BlockSpec((MERGE_BWD_T, D_MODEL), lambda k, i: (i, k)),
                   pl.BlockSpec((1, BW, D_MODEL), lambda k, i: (k, 0, 0)),
                   pl.BlockSpec((1, 1, D_MODEL), lambda k, i: (k, 0, 0))],
        out_shape=[jax.ShapeDtypeStruct((N_BRANCH, SEQ, BW), F32), jax.ShapeDtypeStruct((SEQ, IN_PAD), BF16),
                   jax.ShapeDtypeStruct((N_BRANCH, BW, D_MODEL), BF16), jax.ShapeDtypeStruct((N_BRANCH, 1, D_MODEL), F32)],
        scratch_shapes=[pltpu.VMEM((BW, D_MODEL), F32)],
        compiler_params=_cparams(("parallel", "arbitrary")),
    )(ys, proj, dm, merge_b, w_branch)


def _adamw(glist, w, m, v, rows, name):
    nl = len(glist)
    n, r, c = glist[0].shape
    assert w.shape == (nl, r, c) and r % rows == 0
    nb = r // rows

    def body(*refs):
        g_refs = refs[:nl]
        w_ref, m_ref, v_ref, go_ref, d_ref, mo_ref, vo_ref = refs[nl:]
        for layer in range(nl):
            @pl.when(pl.program_id(0) == layer)
            def _(g_ref=g_refs[layer]):
                g = g_ref[0].astype(F32)
                for s in range(1, n):
                    g = g + g_ref[s].astype(F32)
                mn = ADAM_B1 * m_ref[0] + (1.0 - ADAM_B1) * g
                vn = ADAM_B2 * v_ref[0] + (1.0 - ADAM_B2) * jnp.square(g)
                m_hat = mn / (1.0 - ADAM_B1 ** ADAM_STEP)
                v_hat = vn / (1.0 - ADAM_B2 ** ADAM_STEP)
                go_ref[0] = g
                d_ref[0] = -ADAM_LR * (m_hat / (jnp.sqrt(v_hat) + ADAM_EPS) + ADAM_WD * w_ref[0])
                mo_ref[0] = mn
                vo_ref[0] = vn

    def g_spec(layer):
        return pl.BlockSpec((n, rows, c), lambda a, i: (0, jnp.where(a < layer, 0, jnp.where(a == layer, i, nb - 1)), 0))

    blk = pl.BlockSpec((1, rows, c), lambda a, i: (a, i, 0))
    out = jax.ShapeDtypeStruct((nl, r, c), F32)
    return pl.pallas_call(
        body, name=name, grid=(nl, nb),
        in_specs=[g_spec(layer) for layer in range(nl)] + [blk, blk, blk],
        out_specs=[blk, blk, blk, blk], out_shape=[out, out, out, out],
        compiler_params=_cparams(("arbitrary", "arbitrary")),
    )(*glist, w, m, v)


X_ROWS_PER_COL = 2 * (D_MODEL // LANES)


def _w_in_to_x(w):
    t = jnp.transpose(w, (2, 0, 1)).reshape(SHARD_IN, DEPTH, D_MODEL // LANES, LANES)
    return jnp.transpose(t, (0, 2, 1, 3)).reshape(SHARD_IN * X_ROWS_PER_COL, LANES)


def _w_in_from_x(xv):
    t = jnp.transpose(xv.reshape(SHARD_IN, D_MODEL // LANES, DEPTH, LANES), (0, 2, 1, 3))
    return jnp.transpose(t.reshape(SHARD_IN, DEPTH, D_MODEL), (1, 2, 0))


def _adamw_w_in(glist, w, m, v, name, after=None):
    n = glist[0].shape[0]
    cols = 2 * LANES
    rows = cols * X_ROWS_PER_COL
    extra = [] if after is None else [after]

    def body(g0_ref, g1_ref, w_ref, m_ref, v_ref, *rest):
        go_ref, d_ref, mo_ref, vo_ref = rest[len(extra):]
        for layer, g_ref in enumerate((g0_ref, g1_ref)):
            g = g_ref[0].astype(F32)
            for s in range(1, n):
                g = g + g_ref[s].astype(F32)
            gt = g.T
            for t in range(D_MODEL // LANES):
                sel = (pl.ds(2 * t + layer, cols, stride=X_ROWS_PER_COL), slice(None))
                gs = gt[:, t * LANES:(t + 1) * LANES]
                mn = ADAM_B1 * m_ref[sel] + (1.0 - ADAM_B1) * gs
                vn = ADAM_B2 * v_ref[sel] + (1.0 - ADAM_B2) * jnp.square(gs)
                m_hat = mn / (1.0 - ADAM_B1 ** ADAM_STEP)
                v_hat = vn / (1.0 - ADAM_B2 ** ADAM_STEP)
                go_ref[sel] = gs
                d_ref[sel] = -ADAM_LR * (m_hat / (jnp.sqrt(v_hat) + ADAM_EPS) + ADAM_WD * w_ref[sel])
                mo_ref[sel] = mn
                vo_ref[sel] = vn

    g_spec = pl.BlockSpec((n, D_MODEL, cols), lambda i: (0, 0, i))
    blk = pl.BlockSpec((rows, LANES), lambda i: (i, 0))
    out = jax.ShapeDtypeStruct((SHARD_IN * X_ROWS_PER_COL, LANES), F32)
    res = pl.pallas_call(
        body, name=name, grid=(-(-SHARD_IN // cols),),
        in_specs=[g_spec, g_spec, blk, blk, blk] + [pl.BlockSpec(memory_space=pl.ANY)] * len(extra),
        out_specs=[blk, blk, blk, blk], out_shape=[out, out, out, out],
        compiler_params=_cparams(("parallel",)),
    )(*glist, _w_in_to_x(w), _w_in_to_x(m), _w_in_to_x(v), *extra)
    return [_w_in_from_x(o) for o in res]


def _adamw_many(gs, ws, ms, vs, name):
    k = len(gs)

    def body(*refs):
        g_refs, w_refs, m_refs, v_refs = refs[:k], refs[k:2 * k], refs[2 * k:3 * k], refs[3 * k:4 * k]
        d_refs, mo_refs, vo_refs = refs[4 * k:5 * k], refs[5 * k:6 * k], refs[6 * k:7 * k]
        for i in range(k):
            g = g_refs[i][...]
            mn = ADAM_B1 * m_refs[i][...] + (1.0 - ADAM_B1) * g
            vn = ADAM_B2 * v_refs[i][...] + (1.0 - ADAM_B2) * jnp.square(g)
            m_hat = mn / (1.0 - ADAM_B1 ** ADAM_STEP)
            v_hat = vn / (1.0 - ADAM_B2 ** ADAM_STEP)
            d_refs[i][...] = -ADAM_LR * (m_hat / (jnp.sqrt(v_hat) + ADAM_EPS) + ADAM_WD * w_refs[i][...])
            mo_refs[i][...] = mn
            vo_refs[i][...] = vn

    whole = pl.BlockSpec(memory_space=pltpu.VMEM)
    shapes = [jax.ShapeDtypeStruct(w.shape, F32) for w in ws]
    outs = pl.pallas_call(
        body, name=name, in_specs=[whole] * (4 * k), out_specs=[whole] * (3 * k), out_shape=shapes * 3,
        compiler_params=_cparams(None),
    )(*gs, *ws, *ms, *vs)
    return outs[:k], outs[k:2 * k], outs[2 * k:]


MEMORY_ORDER = {'s5_b_re': (0, 1, 3, 2), 's5_b_im': (0, 1, 3, 2), 's5_d': (0, 2, 1), 'sc_conv_w': (1, 0, 2)}


def _memory_view(name, t):
    return jnp.transpose(t, MEMORY_ORDER[name]) if name in MEMORY_ORDER else t


def _slot_sum(gslots, name):
    n, r, c = gslots.shape

    def body(g_ref, o_ref):
        g = g_ref[0]
        for s in range(1, n):
            g = g + g_ref[s]
        o_ref[...] = g

    return pl.pallas_call(
        body, name=name, in_specs=[pl.BlockSpec((n, r, c), lambda: (0, 0, 0))],
        out_specs=pl.BlockSpec((r, c), lambda: (0, 0)), out_shape=jax.ShapeDtypeStruct((r, c), F32),
        compiler_params=_cparams(None),
    )(gslots)


def _me_and_peers():
    x, y, c = lax.axis_index("x"), lax.axis_index("y"), lax.axis_index("c")
    me = 4 * x + 2 * y + c
    peers = []
    for k in range(1, N_DEV):
        px = 1 - x if (k >> 2) & 1 else x
        py = 1 - y if (k >> 1) & 1 else y
        pc = 1 - c if k & 1 else c
        peers.append((4 * px + 2 * py + pc, (px, py, pc)))
    return me, peers


_HBM = pl.BlockSpec(memory_space=pltpu.HBM)
_SEM = pl.BlockSpec(memory_space=pltpu.SEMAPHORE)
_EFFECT = pltpu.SideEffectType.DATAFLOW_SIDE_EFFECTING


N_CHIP = N_DEV // 2


def _chip_peers():
    x, y, c = lax.axis_index("x"), lax.axis_index("y"), lax.axis_index("c")
    chips = []
    for d in range(1, N_CHIP):
        px = 1 - x if (d >> 1) & 1 else x
        py = 1 - y if d & 1 else y
        chips.append((2 * px + py, (px, py)))
    return (x, y, c), 2 * x + y, chips


def _plan_direct(ins, lands, send_sems, recv_sems, local_sems, gather):
    me, peers = _me_and_peers()
    plan = dict(start=[], local=[], sends=[], recvs=[])
    for t in range(len(ins)):
        own = pltpu.make_async_copy(ins[t] if gather else ins[t].at[me], lands[t].at[me], local_sems.at[t])
        plan['start'].append(own)
        plan['local'].append(own)
        for k, (pidx, pos) in enumerate(peers):
            cp = pltpu.make_async_remote_copy(
                src_ref=ins[t] if gather else ins[t].at[pidx], dst_ref=lands[t].at[me],
                send_sem=send_sems.at[t * (N_DEV - 1) + k], recv_sem=recv_sems.at[t * (N_DEV - 1) + k],
                device_id=pos, device_id_type=MESH)
            plan['start'].append(cp)
            plan['sends'].append(cp)
            plan['recvs'].append(cp)
    return plan


def _plan_gather(ins, lands, send_sems, recv_sems, local_sems, first=0):
    (x, y, c), q, chips = _chip_peers()
    me = 2 * q + c
    plan = dict(start=[], relay_wait=[], relay_start=[], local=[], sends=[], recvs=[])
    for t in range(len(ins)):
        base = (first + t) * 7
        sem = lambda k: dict(send_sem=send_sems.at[base + k], recv_sem=recv_sems.at[base + k], device_id_type=MESH)
        own = pltpu.make_async_copy(ins[t], lands[t].at[me], local_sems.at[first + t])
        to_sib = pltpu.make_async_remote_copy(src_ref=ins[t], dst_ref=lands[t].at[me], device_id=(x, y, 1 - c), **sem(0))
        plan['start'] += [own, to_sib]
        plan['local'].append(own)
        plan['sends'].append(to_sib)
        plan['recvs'].append(to_sib)
        for d, (pq, (px, py)) in enumerate(chips):
            to_chip = pltpu.make_async_remote_copy(src_ref=ins[t], dst_ref=lands[t].at[me], device_id=(px, py, c), **sem(1 + d))
            blk = lands[t].at[2 * pq + c]
            fwd = pltpu.make_async_remote_copy(src_ref=blk, dst_ref=blk, device_id=(x, y, 1 - c), **sem(4 + d))
            plan['start'].append(to_chip)
            plan['relay_wait'].append(to_chip)
            plan['relay_start'].append(fwd)
            plan['sends'] += [to_chip, fwd]
            plan['recvs'].append(fwd)
    return plan


def _plan_pair(ins, lands, send_sems, recv_sems, local_sems):
    (x, y, c), q, chips = _chip_peers()
    plan = dict(start=[], local=[], sends=[], recvs=[])
    for t in range(len(ins)):
        for k in range(N_CHIP):
            cp = pltpu.make_async_remote_copy(
                src_ref=ins[t].at[2 * k + 1 - c], dst_ref=lands[t].at[k], send_sem=send_sems.at[t * N_CHIP + k],
                recv_sem=recv_sems.at[t * N_CHIP + k], device_id=(x, y, 1 - c), device_id_type=MESH)
            plan['start'].append(cp)
            plan['sends'].append(cp)
            plan['recvs'].append(cp)
    return plan


def _plan_chips(ins, lands, send_sems, recv_sems, local_sems):
    (x, y, c), q, chips = _chip_peers()
    plan = dict(start=[], local=[], sends=[], recvs=[])
    for t in range(len(ins)):
        own = pltpu.make_async_copy(ins[t].at[q], lands[t].at[q], local_sems.at[t])
        plan['start'].append(own)
        plan['local'].append(own)
        for d, (pq, (px, py)) in enumerate(chips):
            cp = pltpu.make_async_remote_copy(
                src_ref=ins[t].at[pq], dst_ref=lands[t].at[q], send_sem=send_sems.at[t * 3 + d],
                recv_sem=recv_sems.at[t * 3 + d], device_id=(px, py, c), device_id_type=MESH)
            plan['start'].append(cp)
            plan['sends'].append(cp)
            plan['recvs'].append(cp)
    return plan


def _split_start(plan_fn, tensors, land_shapes, n_sems, name, after=None):
    n = len(tensors)
    extra = [] if after is None else [after]

    def body(*refs):
        ins, lands = refs[:n], refs[n:2 * n]
        plan = plan_fn(ins, lands, *refs[2 * n + len(extra):2 * n + len(extra) + 3])
        for cp in plan['start']:
            cp.start()
        refs[-1][...] = jnp.zeros_like(refs[-1])

    outs = pl.pallas_call(
        body, name=name,
        out_shape=(pltpu.SemaphoreType.DMA((n_sems,)), pltpu.SemaphoreType.DMA((n_sems,)), pltpu.SemaphoreType.DMA((n,)),
                   *[pltpu.HBM(t.shape, t.dtype) for t in tensors],
                   *[pltpu.HBM(s, t.dtype) for s, t in zip(land_shapes, tensors)],
                   jax.ShapeDtypeStruct((8, LANES), F32)),
        in_specs=[_HBM] * (2 * n) + [pl.BlockSpec(memory_space=pl.ANY)] * len(extra),
        out_specs=(_SEM, _SEM, _SEM, *[_HBM] * (2 * n), pl.BlockSpec(memory_space=pltpu.VMEM)),
        input_output_aliases={t: 3 + t for t in range(2 * n)},
        compiler_params=pltpu.CompilerParams(has_side_effects=_EFFECT),
    )(*[pltpu.with_memory_space_constraint(t, pltpu.HBM) for t in tensors],
      *[pltpu.with_memory_space_constraint(lax.empty(s, t.dtype), pltpu.HBM) for s, t in zip(land_shapes, tensors)], *extra)
    return outs[:-1], outs[-1]


def _split_relay(plan_fn, state, after, name):
    sems, thru = state[:3], state[3:]
    n = len(thru) // 2

    def arrived(*refs):
        plan = plan_fn(refs[:n], refs[n:2 * n], *refs[2 * n:2 * n + 3])
        for cp in plan['relay_wait']:
            cp.wait_recv()

    thru = pl.pallas_call(
        arrived, name=name + "_arrived",
        out_shape=tuple(pltpu.HBM(t.shape, t.dtype) for t in thru),
        in_specs=[_HBM] * (2 * n) + [_SEM, _SEM, _SEM, pl.BlockSpec(memory_space=pl.ANY)],
        out_specs=tuple([_HBM] * (2 * n)),
        input_output_aliases={t: t for t in range(2 * n)},
        compiler_params=pltpu.CompilerParams(has_side_effects=_EFFECT),
    )(*thru, *sems, after)

    def forward(*refs):
        plan = plan_fn(refs[:n], refs[n:2 * n], *refs[2 * n:2 * n + 3])
        for cp in plan['relay_start']:
            cp.start()
        refs[-1][...] = jnp.zeros_like(refs[-1])

    outs = pl.pallas_call(
        forward, name=name + "_forward",
        out_shape=(*[pltpu.HBM(t.shape, t.dtype) for t in thru], jax.ShapeDtypeStruct((8, LANES), F32)),
        in_specs=[_HBM] * (2 * n) + [_SEM, _SEM, _SEM],
        out_specs=(*[_HBM] * (2 * n), pl.BlockSpec(memory_space=pltpu.VMEM)),
        input_output_aliases={t: t for t in range(2 * n)},
        compiler_params=pltpu.CompilerParams(has_side_effects=_EFFECT),
    )(*thru, *sems)
    return (*sems, *outs[:-1]), outs[-1]


def _split_wait(plan_fn, state, after, name, with_sources=False):
    sems, thru = state[:3], state[3:]
    n = len(thru) // 2

    def body(*refs):
        plan = plan_fn(refs[:n], refs[n:2 * n], *refs[2 * n:2 * n + 3])
        for cp in plan['local']:
            cp.wait()
        for cp in plan['sends']:
            cp.wait_send()
        for cp in plan['recvs']:
            cp.wait_recv()

    outs = pl.pallas_call(
        body, name=name,
        out_shape=tuple(pltpu.HBM(t.shape, t.dtype) for t in thru),
        in_specs=[_HBM] * (2 * n) + [_SEM, _SEM, _SEM, pl.BlockSpec(memory_space=pl.ANY)],
        out_specs=tuple([_HBM] * (2 * n)),
        input_output_aliases={t: t for t in range(2 * n)},
        compiler_params=pltpu.CompilerParams(has_side_effects=_EFFECT),
    )(*thru, *sems, after)
    return (list(outs[:n]), list(outs[n:])) if with_sources else list(outs[n:])


PAIR_SUM_BLOCK = 768 * 1024


def _pair_sum(mine, theirs, name):
    _, r, c = mine.shape
    rows = r
    while rows * c > PAIR_SUM_BLOCK and rows % 32 == 0:
        rows //= 2

    def body(core_ref, a_ref, b_ref, o_ref):
        o_ref[0] = (a_ref[0].astype(F32) + b_ref[0].astype(F32)).astype(o_ref.dtype)

    return pl.pallas_call(
        body, name=name,
        grid_spec=pltpu.PrefetchScalarGridSpec(
            num_scalar_prefetch=1, grid=(N_CHIP, r // rows),
            in_specs=[pl.BlockSpec((1, rows, c), lambda k, i, core: (2 * k + core[0], i, 0)),
                      pl.BlockSpec((1, rows, c), lambda k, i, core: (k, i, 0))],
            out_specs=pl.BlockSpec((1, rows, c), lambda k, i, core: (k, i, 0))),
        out_shape=jax.ShapeDtypeStruct((N_CHIP, r, c), mine.dtype),
        compiler_params=_cparams(("parallel", "parallel")),
    )(lax.axis_index("c").astype(jnp.int32).reshape(1), mine, theirs)


WEIGHTS = ['norm_w', 'w_in', 's5_lambda_re', 's5_lambda_im', 's5_b_re', 's5_b_im', 's5_c_re', 's5_c_im', 's5_d',
           's5_log_step', 's5_w_glu', 'sgu_ln_w', 'sgu_ln_b', 'sgu_w', 'sgu_b', 'm2_conv_w', 'm2_conv_b', 'm2_dt_bias',
           'm2_a_log', 'm2_d', 'm2_norm_w', 'sc_conv_w', 'merge_b', 'w_branch', 'w_out', 'final_norm_w']
BIG_SHARDED = ['w_in', 'w_branch', 'w_out', 's5_w_glu']
SMALL_SHARDED = ['m2_conv_w', 'sc_conv_w', 'merge_b']
REPLICATED = [n for n in WEIGHTS if n not in BIG_SHARDED + SMALL_SHARDED]
S5_NAMES = ['s5_lambda_re', 's5_lambda_im', 's5_b_re', 's5_b_im', 's5_c_re', 's5_c_im', 's5_d', 's5_log_step']


def _sc_interleave(t):
    lead = t.shape[:-1]
    return jnp.swapaxes(t.reshape(lead + (4, 4, LANES)), -3, -2).reshape(lead + (4 * BW,))


def _pad_in(w):
    z = lambda n: jnp.zeros(w.shape[:-1] + (n,), w.dtype)
    return jnp.concatenate([w[..., 6152:], w[..., 0:1024], w[..., 3072:4096], w[..., 1024:2560], z(512),
                            w[..., 2560:3072], w[..., 4096:4104], z(504), _sc_interleave(w[..., 4104:6152])], axis=-1)


def _unpad_in(g):
    return jnp.concatenate([g[..., C_S5U:C_S5U + 1024], g[..., C_SGU_U:C_SGU_U + 1536], g[..., C_M2Z:C_M2Z + 512],
                            g[..., C_M2X:C_M2X + 1024], g[..., C_DT:C_DT + 8], _sc_interleave(g[..., C_SC:]),
                            g[..., :N_BRANCH * D_MODEL]], axis=-1)


ROW_BLOCK = 8 * LANES


def _pack_rows(tensors, row_mult, batched=False):
    parts = []
    for t in tensors:
        f = t.reshape((t.shape[0], -1) if batched else (1, -1))
        f = jnp.pad(f, ((0, 0), (0, (-f.shape[1]) % ROW_BLOCK)))
        parts.append(f.reshape(f.shape[0], -1, LANES))
    out = jnp.concatenate(parts, axis=1)
    out = jnp.pad(out, ((0, 0), (0, (-out.shape[1]) % row_mult), (0, 0)))
    return out if batched else out[0]


def _unpack_rows(rows, shapes):
    out, r0 = [], 0
    for shp in shapes:
        size = 1
        for s in shp:
            size *= s
        nr = -(-size // ROW_BLOCK) * 8
        out.append(rows[r0:r0 + nr].reshape(-1)[:size].reshape(shp))
        r0 += nr
    return out


def _kernel_col_map():
    m = np.full(IN_PAD, -1, np.int64)
    m[C_MERGE:C_MERGE + 4096] = np.arange(6152, 10248)
    m[C_S5U:C_S5U + 1024] = np.arange(0, 1024)
    m[C_M2X:C_M2X + 1024] = np.arange(3072, 4096)
    m[C_SGU_U:C_SGU_U + 1536] = np.arange(1024, 2560)
    m[C_M2Z:C_M2Z + 512] = np.arange(2560, 3072)
    m[C_DT:C_DT + 8] = np.arange(4096, 4104)
    for j in range(4):
        for kind in range(4):
            k0 = C_SC + 4 * LANES * j + LANES * kind
            m[k0:k0 + LANES] = 4104 + BW * kind + LANES * j + np.arange(LANES)
    return m


def _lane_pieces(sources):
    pieces, cur = [], None
    for lane, src in enumerate(sources):
        key = None if src is None else (src[0], src[1] // LANES, (lane - src[1]) % LANES)
        if cur is not None and key == cur[0]:
            cur[2] = lane + 1
        else:
            if cur is not None and cur[0] is not None:
                pieces.append((*cur[0], cur[1], cur[2]))
            cur = [key, lane, lane + 1]
    if cur is not None and cur[0] is not None:
        pieces.append((*cur[0], cur[1], cur[2]))
    return pieces


def _assemble_block(pieces, load, rows, dtype):
    lane = lax.broadcasted_iota(jnp.int32, (rows, LANES), 1)
    out = None
    for arr, sb, shift, lo, hi in pieces:
        v = load(arr, sb)
        if shift:
            v = pltpu.roll(v, shift, 1)
        if out is None and lo == 0 and hi == LANES:
            out = v
        else:
            out = jnp.where((lane >= lo) & (lane < hi), v, jnp.zeros((rows, LANES), dtype) if out is None else out)
    return jnp.zeros((rows, LANES), dtype) if out is None else out


RELAYOUT_ROWS = 512
SHARD_BLOCKS = -(-SHARD_IN // LANES)


def _load_shard_block(ref, rows):
    def load(j, sb):
        if sb == SHARD_BLOCKS - 1:
            return jnp.broadcast_to(ref[j, :, SHARD_IN - 1:SHARD_IN], (rows, LANES))
        return ref[j, :, sb * LANES:(sb + 1) * LANES]
    return load


def _relayout_w_in(gathered, name):
    kmap = _kernel_col_map()
    dtype = gathered.dtype

    def body(src_ref, o_ref):
        load = _load_shard_block(src_ref, RELAYOUT_ROWS)
        for ob in range(IN_PAD // LANES):
            srcs = [None if kmap[ob * LANES + l] < 0 else (int(kmap[ob * LANES + l]) // SHARD_IN, int(kmap[ob * LANES + l]) % SHARD_IN)
                    for l in range(LANES)]
            o_ref[:, ob * LANES:(ob + 1) * LANES] = _assemble_block(_lane_pieces(srcs), load, RELAYOUT_ROWS, dtype)

    return pl.pallas_call(
        body, name=name, grid=(D_MODEL // RELAYOUT_ROWS,),
        in_specs=[pl.BlockSpec((N_DEV, RELAYOUT_ROWS, SHARD_IN), lambda i: (0, i, 0))],
        out_specs=pl.BlockSpec((RELAYOUT_ROWS, IN_PAD), lambda i: (i, 0)),
        out_shape=jax.ShapeDtypeStruct((D_MODEL, IN_PAD), dtype),
        compiler_params=_cparams(("parallel",)),
    )(gathered)


def _relayout_g_in(gw, name):
    kmap = _kernel_col_map()
    kinv = np.zeros(IN_DIM, np.int64)
    kinv[kmap[kmap >= 0]] = np.nonzero(kmap >= 0)[0]
    dtype = gw.dtype

    def body(src_ref, o_ref):
        load = lambda _, sb: src_ref[:, sb * LANES:(sb + 1) * LANES]
        for j in range(N_DEV):
            for ob in range(SHARD_BLOCKS):
                srcs = [(0, int(kinv[SHARD_IN * j + ob * LANES + l])) if ob * LANES + l < SHARD_IN else None for l in range(LANES)]
                blk = _assemble_block(_lane_pieces(srcs), load, RELAYOUT_ROWS, dtype)
                if ob == SHARD_BLOCKS - 1:
                    o_ref[j, :, SHARD_IN - 1:SHARD_IN] = blk[:, 0:1]
                else:
                    o_ref[j, :, ob * LANES:(ob + 1) * LANES] = blk

    return pl.pallas_call(
        body, name=name, grid=(D_MODEL // RELAYOUT_ROWS,),
        in_specs=[pl.BlockSpec((RELAYOUT_ROWS, IN_PAD), lambda i: (i, 0))],
        out_specs=pl.BlockSpec((N_DEV, RELAYOUT_ROWS, SHARD_IN), lambda i: (0, i, 0)),
        out_shape=jax.ShapeDtypeStruct((N_DEV, D_MODEL, SHARD_IN), dtype),
        compiler_params=_cparams(("parallel",)),
    )(gw)


def _rows128(flat, row_mult=8):
    n = flat.shape[0]
    per = LANES * row_mult
    total = -(-n // per) * per
    return jnp.pad(flat, (0, total - n)).reshape(total // LANES, LANES)


def _pad_lanes(v):
    return jnp.pad(v, (0, LANES - v.shape[0])).reshape(1, LANES)


def _layer_prep(i, p):
    disc, disc_vjp = jax.vjp(_s5_disc, *[p[n][i] for n in S5_NAMES])
    prep = dict(
        nw=p['norm_w'][i].reshape(1, D_MODEL), disc_vjp=disc_vjp,
        s5small=[_block_diag(t).astype(BF16) for t in disc[:4]] + [disc[4], disc[5]],
        sgw=[p['sgu_ln_w'][i].reshape(1, BW), p['sgu_ln_b'][i].reshape(1, BW), p['sgu_w'][i],
             jnp.repeat(p['sgu_b'][i].T, BW // SGU_HEADS, axis=1)],
        cb=p['m2_conv_b'][i].reshape(1, M2_CONV_CH),
        m2w=[_pad_lanes(p['m2_dt_bias'][i]), _pad_lanes(p['m2_a_log'][i]),
             jnp.repeat(p['m2_d'][i], M2_HEAD_DIM).reshape(1, BW), p['m2_norm_w'][i].reshape(1, BW)])
    touch = [t[0, 0].astype(F32) for t in prep['s5small']] + [prep['sgw'][3][0, 0], prep['m2w'][2][0, 0]]
    return prep, sum(touch[1:], touch[0])


def _layer_fwd(x, h, i, prep, w_in, other_weights, finish, before_merge=None):
    proj = _matmul(h, w_in, 1, 0, F32, 2048, 1024, 1024, f"proj{i}")
    full = dict(other_weights(proj), w_in=w_in)
    s5w = prep['s5small'] + [full['s5_w_glu']]
    ys, sre, sim = _s5_fwd(proj, *s5w, f"s5_fwd{i}")
    ys = _sgu_fwd(proj, ys, *prep['sgw'], f"sgu_fwd{i}")
    cw = full['m2_conv_w']
    xa = _m2_conv_fwd(proj, cw, prep['cb'], f"m2conv_fwd{i}")
    ys, s_in = _ssd_fwd(proj, ys, xa, *prep['m2w'], f"ssd_fwd{i}")
    scw = full['sc_conv_w']
    ys = _sc_fwd(proj, ys, scw, f"sc_fwd{i}")
    mb = full['merge_b'].reshape(N_BRANCH, 1, D_MODEL)
    if before_merge is not None:
        mb = mb + before_merge(ys)[0, 0]
    merged = _merge_fwd(proj, ys, mb, full['w_branch'], f"merge_fwd{i}")
    x_new = finish(merged, full['w_out'], x)
    saved = dict(x=x, nw=prep['nw'], h=h, proj=proj, disc_vjp=prep['disc_vjp'], s5w=s5w, sre=sre, sim=sim, sgw=prep['sgw'],
                 cw=cw, cb=prep['cb'], xa=xa, m2w=prep['m2w'], s_in=s_in, scw=scw, ys=ys, mb=mb, merged=merged)
    return x_new, saved, full


def _layer_bwd(dx_out, i, sv, full, on_large_grads=None, after_dh=None, after=None):
    g = {}
    proj = sv['proj']
    dm = _matmul(dx_out, full['w_out'], 1, 1, F32, 1024, 1024, 1024, f"dmerged{i}", after=after)
    g['w_out'] = _matmul(sv['merged'], dx_out, 0, 0, BF16, 1024, 1024, 1024, f"gw_out{i}")
    dys, dproj, g['w_branch'], dmb = _merge_bwd(proj, sv['ys'], dm, sv['mb'], full['w_branch'], f"merge_bwd{i}")
    g['merge_b'] = dmb.reshape(N_BRANCH, D_MODEL)
    dproj, dbbre, dbbim, dcre, dcim, da, dd, dwg = _s5_bwd(proj, dproj, dys, sv['sre'], sv['sim'], *sv['s5w'], f"s5_bwd{i}")
    g['s5_dense'] = (dbbre, dbbim, dcre, dcim, da, dd)
    g['s5_w_glu'] = dwg.astype(BF16)
    dproj, dlw, dlb, g['sgu_w'], dbias = _sgu_bwd(proj, dproj, dys, *sv['sgw'], f"sgu_bwd{i}")
    g['sgu_ln_w'], g['sgu_ln_b'] = dlw[0], dlb[0]
    g['sgu_b'] = dbias.reshape(SGU_CHUNK, SGU_HEADS, BW // SGU_HEADS).sum(-1).T
    dproj, dxa, ddtb, dal, ddf, dnw = _ssd_bwd(proj, dproj, sv['xa'], dys, sv['s_in'], *sv['m2w'], f"ssd_bwd{i}")
    dproj, g['m2_conv_w'], dcb = _m2_conv_bwd(proj, dproj, dxa, sv['cw'], sv['cb'], f"m2conv_bwd{i}")
    g['m2_conv_b'], g['m2_norm_w'] = dcb[0], dnw[0]
    g['m2_dt_bias'], g['m2_a_log'] = ddtb[0, :M2_HEADS], dal[0, :M2_HEADS]
    g['m2_d'] = ddf.reshape(M2_HEADS, M2_HEAD_DIM).sum(-1)
    dproj, g['sc_conv_w'] = _sc_bwd(proj, dproj, dys, sv['scw'], f"sc_bwd{i}")
    g['w_in'] = _matmul(sv['h'], dproj, 0, 0, BF16, 1024, 1024, 2048, f"gw_in{i}")
    tok = on_large_grads(g) if on_large_grads else None
    if after_dh is not None:
        dx_in, dnw_l = _dh_rms_bwd(dproj, full['w_in'], sv['x'], sv['nw'], dx_out, f"dh{i}_rms_bwd{i}", after=tok)
        after_dh(dx_in)
    else:
        dh = _matmul(dproj, full['w_in'], 1, 1, F32, 1024, 1024, IN_PAD // 4, f"dh{i}", after=tok)
        dx_in, dnw_l = _rmsnorm_bwd(sv['x'], sv['nw'], dh, dx_out, f"rms_bwd{i}")
    g['norm_w'] = dnw_l[0]
    return dx_in, g


def _split8(t, axis):
    shp = t.shape
    t = t.reshape(shp[:axis] + (N_DEV, shp[axis] // N_DEV) + shp[axis + 1:])
    return jnp.moveaxis(t, axis, 0)


def _join8(t, axis):
    t = jnp.moveaxis(t, 0, axis)
    shp = t.shape
    return t.reshape(shp[:axis] + (shp[axis] * shp[axis + 1],) + shp[axis + 2:])


SHARD_AXIS = {'w_in': 2, 'w_branch': 3, 'w_out': 1, 's5_w_glu': 1, 'm2_conv_w': 2, 'sc_conv_w': 2, 'merge_b': 2}


OTHER_BIG = [n for n in BIG_SHARDED if n != 'w_in']


def _other_weights(gathered):
    return {n: _join8(t, SHARD_AXIS[n] - 1) for n, t in zip(OTHER_BIG, gathered)}


def _layer_grad_blocks(g, i):
    blocks = [_relayout_g_in(g[n], f"relayout_g_in{i}") if n == 'w_in' else _split8(g[n], SHARD_AXIS[n] - 1) for n in BIG_SHARDED]
    return [b.reshape(N_DEV, -1, b.shape[-1]) for b in blocks]


def _pair_start(blocks, tag):
    shapes = [(N_CHIP,) + b.shape[1:] for b in blocks]
    return _split_start(_plan_pair, blocks, shapes, N_CHIP * len(blocks), f"pair{tag}_start")


def _pair_sums(state, after, tag):
    mine, theirs = _split_wait(_plan_pair, state, after, f"pair{tag}_wait", with_sources=True)
    return [_pair_sum(b, t, f"pair_sum{tag}_{k}") for k, (b, t) in enumerate(zip(mine, theirs))]


def _chips_start(sums, tag, after=None):
    return _split_start(_plan_chips, sums, [s.shape for s in sums], 3 * len(sums), f"chips{tag}_start", after)


def kernel(x, norm_w, w_in, s5_lambda_re, s5_lambda_im, s5_b_re, s5_b_im, s5_c_re, s5_c_im, s5_d, s5_log_step, s5_w_glu, sgu_ln_w, sgu_ln_b, sgu_w, sgu_b, m2_conv_w, m2_conv_b, m2_dt_bias, m2_a_log, m2_d, m2_norm_w, sc_conv_w, merge_b, w_branch, w_out, final_norm_w, loss_target, m_norm_w, m_w_in, m_s5_lambda_re, m_s5_lambda_im, m_s5_b_re, m_s5_b_im, m_s5_c_re, m_s5_c_im, m_s5_d, m_s5_log_step, m_s5_w_glu, m_sgu_ln_w, m_sgu_ln_b, m_sgu_w, m_sgu_b, m_m2_conv_w, m_m2_conv_b, m_m2_dt_bias, m_m2_a_log, m_m2_d, m_m2_norm_w, m_sc_conv_w, m_merge_b, m_w_branch, m_w_out, m_final_norm_w, v_norm_w, v_w_in, v_s5_lambda_re, v_s5_lambda_im, v_s5_b_re, v_s5_b_im, v_s5_c_re, v_s5_c_im, v_s5_d, v_s5_log_step, v_s5_w_glu, v_sgu_ln_w, v_sgu_ln_b, v_sgu_w, v_sgu_b, v_m2_conv_w, v_m2_conv_b, v_m2_dt_bias, v_m2_a_log, v_m2_d, v_m2_norm_w, v_sc_conv_w, v_merge_b, v_w_branch, v_w_out, v_final_norm_w):
    loc = locals()
    p = {n: loc[n] for n in WEIGHTS}
    mom = {n: loc['m_' + n] for n in WEIGHTS}
    vel = {n: loc['v_' + n] for n in WEIGHTS}

    small_sizes = [p[n].size for n in SMALL_SHARDED]
    small_pack = _rows128(jnp.concatenate([p[n].reshape(-1) for n in SMALL_SHARDED]))
    first = [p['w_in'][0].astype(BF16)]
    gath_first, tok = _split_start(_plan_gather, first, [(N_DEV,) + first[0].shape], 7, "gather_w_in0_start")
    shards = ([(p[n][0] + tok[0, 0]).astype(BF16) for n in OTHER_BIG] + [small_pack + tok[0, 0]]
              + [(p[n][1] + tok[0, 0]).astype(BF16) for n in BIG_SHARDED])
    gath, tok = _split_start(_plan_gather, shards, [(N_DEV,) + t.shape for t in shards], 7 * len(shards), "gather_start")

    def relayed(lo, hi, after, name, started=None):
        started = gath if started is None else started
        n = (len(started) - 3) // 2
        sems, srcs, lands = started[:3], started[3:3 + n], started[3 + n:]
        plan = functools.partial(_plan_gather, first=lo)
        state, tok = _split_relay(plan, (*sems, *srcs[lo:hi], *lands[lo:hi]), after, name + "_relay")
        return (plan, state, name), tok

    def arrived(relay, after):
        plan, state, name = relay
        return _split_wait(plan, state, after, name + "_wait")

    def gathered(lo, hi, after, name, started=None):
        relay, tok = relayed(lo, hi, after, name, started)
        return arrived(relay, tok)

    later = dict(p, **{n: p[n] + tok[0, 0] for n in ('norm_w', 's5_log_step', 'sgu_b', 'm2_d')})
    preps = [_layer_prep(i, later) for i in range(DEPTH)]
    h0 = _rmsnorm_fwd(x[0], preps[0][0]['nw'], "rms_fwd0")
    got = gathered(0, 1, tok + (preps[0][1] + preps[1][1] + h0[0, 0].astype(F32)), "gather_w_in0", gath_first)
    small_full = {}

    def other_weights0(proj):
        got = gathered(0, 4, proj, "gather_rest0")
        small_all, off = got[-1].reshape(N_DEV, -1), 0
        for n, sz in zip(SMALL_SHARDED, small_sizes):
            small_full[n] = _join8(small_all[:, off:off + sz].reshape((N_DEV,) + p[n].shape), SHARD_AXIS[n])
            off += sz
        return dict(_other_weights(got[:-1]), **{n: small_full[n][0] for n in SMALL_SHARDED})

    saved, layer_g, full = [None] * DEPTH, [None] * DEPTH, [None] * DEPTH
    relay1 = []

    def relay_layer1(ys):
        relay, tok = relayed(4, 8, ys, "gather1")
        relay1.append(relay)
        return tok

    (xs, h1), saved[0], full[0] = _layer_fwd(
        x[0], h0, 0, preps[0][0], _relayout_w_in(got[0], "relayout_w_in0"), other_weights0,
        lambda merged, w_out, xin: _out_norm(merged, w_out, xin, preps[1][0]['nw'], "out0_rms_fwd1"), relay_layer1)
    got = arrived(relay1[0], h1)
    (loss_row, dx, dfw), saved[1], full[1] = _layer_fwd(
        xs, h1, 1, preps[1][0], _relayout_w_in(got[0], "relayout_w_in1"),
        lambda proj: dict(_other_weights(got[1:]), **{n: small_full[n][1] for n in SMALL_SHARDED}),
        lambda merged, w_out, xin: _out_loss(merged, w_out, xin, final_norm_w.reshape(1, D_MODEL), loss_target[0],
                                             "out1_loss_head"))
    loss = lax.psum(loss_row[0, 0], ("x", "y", "c"))
    loss, dx = lax.optimization_barrier((loss, dx))
    pairs, scat, sent0, sent1 = [None] * DEPTH, [None] * DEPTH, [], []

    def start_pairs1(g):
        pairs[1], tok = _pair_start(_layer_grad_blocks(g, 1), 1)
        return tok

    def send_chip_sums1(dh):
        scat[1], tok = _chips_start(_pair_sums(pairs[1], dh, 1), 1)
        sent1.append(tok)
        return tok

    def send_all0(g):
        pairs[0], tok = _pair_start(_layer_grad_blocks(g, 0), 0)
        scat[0], tok = _chips_start(_pair_sums(pairs[0], tok, 0), 0)
        sent0.append(tok)
        return tok

    dx, layer_g[1] = _layer_bwd(dx, 1, saved[1], full[1], on_large_grads=start_pairs1, after_dh=send_chip_sums1)
    dx, layer_g[0] = _layer_bwd(dx, 0, saved[0], full[0], on_large_grads=send_all0, after=sent1[0])
    for i in range(DEPTH):
        dense = layer_g[i].pop('s5_dense')
        blocks = tuple(_diag_blocks(t, after=sent0[0]) for t in dense[:4])
        layer_g[i].update(zip(S5_NAMES, saved[i]['disc_vjp'](blocks + (dense[4] + sent0[0][0, 0], dense[5]))))
    grads = {n: jnp.stack([layer_g[i][n] for i in range(DEPTH)]) for n in SMALL_SHARDED + REPLICATED if n != 'final_norm_w'}
    grads['final_norm_w'] = dfw[0]

    out_g, out_d, out_m, out_v = {}, {}, {}, {}
    repl_rows = _pack_rows([grads[n] for n in REPLICATED], 8 * N_DEV)
    rr = repl_rows.shape[0] // N_DEV
    shard_rows = _pack_rows([_split8(grads[n], SHARD_AXIS[n]) for n in SMALL_SHARDED], 8, batched=True)
    rs = shard_rows.shape[1]
    small_g = jnp.concatenate([shard_rows, repl_rows.reshape(N_DEV, rr, LANES)], axis=1)
    all_to_all, all_gather = functools.partial(_plan_direct, gather=False), functools.partial(_plan_direct, gather=True)
    small_state, tok = _split_start(all_to_all, [small_g], [small_g.shape], N_DEV - 1, "scatter_small_start")
    landed1 = _split_wait(_plan_chips, scat[1], tok, "chips1_wait")
    landed0 = _split_wait(_plan_chips, scat[0], landed1[0], "chips0_wait")

    def big_adamw(n, after=None):
        k, shp = BIG_SHARDED.index(n), p[n].shape
        if n == 'w_in':
            return _adamw_w_in([landed0[k], landed1[k]], p[n], mom[n], vel[n], "adamw_w_in", after)
        c = shp[-1]
        r = p[n].size // (DEPTH * c)
        res = _adamw([landed0[k], landed1[k]], *[d[n].reshape(DEPTH, r, c) for d in (p, mom, vel)],
                     {'w_branch': 512, 'w_out': 128, 's5_w_glu': 64}[n], "adamw_" + n)
        return [o.reshape(shp) for o in res]

    for n in OTHER_BIG:
        out_g[n], out_d[n], out_m[n], out_v[n] = big_adamw(n)
    updated = sum(out_d[n].reshape(-1)[0] for n in OTHER_BIG).reshape(1, 1)
    small_sum = _slot_sum(_split_wait(all_to_all, small_state, updated, "scatter_small_wait")[0], "sum_small")
    repl_part = small_sum[rs:]
    repl_state, tok = _split_start(all_gather, [repl_part], [(N_DEV,) + repl_part.shape], N_DEV - 1, "gather_small_start")
    out_g['w_in'], out_d['w_in'], out_m['w_in'], out_v['w_in'] = big_adamw('w_in', tok)
    repl_all = _split_wait(all_gather, repl_state, out_d['w_in'], "gather_small_wait")[0].reshape(N_DEV * rr, LANES)
    g_all = jnp.concatenate([small_sum[:rs], repl_all], axis=0)
    names = SMALL_SHARDED + REPLICATED
    pieces = (_unpack_rows(g_all[:rs], [p[n].shape for n in SMALL_SHARDED])
              + _unpack_rows(g_all[rs:], [p[n].shape for n in REPLICATED]))
    out_g.update(zip(names, pieces))
    res = _adamw_many(*[[_memory_view(n, d[n]) for n in names] for d in (out_g, p, mom, vel)], "adamw_small")
    for r, dst in zip(res, (out_d, out_m, out_v)):
        dst.update({n: _memory_view(n, t) for n, t in zip(names, r)})
    return (loss, dx[None], *[out_g[n] for n in WEIGHTS], *[out_d[n] for n in WEIGHTS],
            *[out_m[n] for n in WEIGHTS], *[out_v[n] for n in WEIGHTS])
```

```python
import functools

import jax
import jax.numpy as jnp
import numpy as np
from jax import lax
from jax.experimental import pallas as pl
from jax.experimental.pallas import tpu as pltpu

F32 = jnp.float32
BF16 = jnp.bfloat16

N_DEV = 8
SEQ = 2048
D_MODEL = 1024
DEPTH = 2
BW = 512
N_BRANCH = 4
EPS = 1e-6
S5_GROUPS, S5_STATE, S5_P = 32, 64, 16
S5_CH = S5_GROUPS * S5_STATE
SGU_CHUNK, SGU_HEADS = 128, 8
M2_HEADS, M2_HEAD_DIM, M2_STATE, M2_CHUNK, M2_CONV = 8, 64, 128, 128, 4
M2_CONV_CH = 1024
SC_CONV = 3
IN_DIM = 10248
IN_PAD = 11264
C_MERGE = 0
C_S5U, C_S5G = 4096, 4608
C_M2X = 5120
C_SGU_U, C_SGU_V, C_SGU_G = 6144, 6656, 7168
C_M2Z, C_DT = 8192, 8704
C_SC = 9216
SHARD_IN = IN_DIM // N_DEV

ADAM_LR, ADAM_B1, ADAM_B2, ADAM_EPS, ADAM_WD, ADAM_STEP = 0.001, 0.9, 0.999, 1e-08, 0.01, 10

VMEM_LIMIT = 56 * 1024 * 1024
LANES = 128

MESH = pl.DeviceIdType.MESH


def _cparams(sem=None, **kw):
    return pltpu.CompilerParams(dimension_semantics=sem, vmem_limit_bytes=VMEM_LIMIT, **kw)


def _dg(a, b, ca, cb, precision=None):
    return lax.dot_general(a, b, (((ca,), (cb,)), ((), ())), precision=precision,
                           preferred_element_type=F32)


@functools.partial(jax.custom_vjp, nondiff_argnums=(2, 3))
def _bdot(a, b, ca, cb):
    return _dg(a.astype(BF16), b.astype(BF16), ca, cb)


def _bdot_fwd(a, b, ca, cb):
    return _bdot(a, b, ca, cb), (a, b)


def _bdot_bwd(ca, cb, res, g):
    a, b = res
    gb, ab, bb = g.astype(BF16), a.astype(BF16), b.astype(BF16)
    da = _dg(gb, bb, 1, 1 - cb) if ca == 1 else _dg(bb, gb, 1 - cb, 1)
    db = _dg(ab, gb, 1 - ca, 0) if cb == 0 else _dg(gb, ab, 0, 1 - ca)
    return da.astype(a.dtype), db.astype(b.dtype)


_bdot.defvjp(_bdot_fwd, _bdot_bwd)


def _rms(x, w):
    return x * lax.rsqrt(jnp.mean(x * x, axis=-1, keepdims=True) + EPS) * w


def _silu(x):
    return x * jax.nn.sigmoid(x)


def _gelu(x):
    return 0.5 * x * (1.0 + jnp.tanh(0.7978845608028654 * (x + 0.044715 * (x * x * x))))


def _softplus(x):
    return jnp.maximum(x, 0.0) + jnp.log1p(jnp.exp(-jnp.abs(x)))


def _shift_down(x, s):
    if s == 0:
        return x
    row = lax.broadcasted_iota(jnp.int32, x.shape, 0)
    return jnp.where(row >= s, pltpu.roll(x, s, 0), 0.0)


def _shift_up(x, s):
    if s == 0:
        return x
    n = x.shape[0]
    row = lax.broadcasted_iota(jnp.int32, x.shape, 0)
    return jnp.where(row < n - s, pltpu.roll(x, n - s, 0), 0.0)


def _matmul(a, b, ca, cb, out_dtype, tm, tn, tk, name, residual=None, after=None):
    m = a.shape[1 - ca]
    k = a.shape[ca]
    n = b.shape[1 - cb]
    assert b.shape[cb] == k and m % tm == 0 and n % tn == 0 and k % tk == 0
    nk = k // tk
    a_spec = pl.BlockSpec((tm, tk), lambda i, j, kk: (i, kk)) if ca == 1 else pl.BlockSpec((tk, tm), lambda i, j, kk: (kk, i))
    b_spec = pl.BlockSpec((tk, tn), lambda i, j, kk: (kk, j)) if cb == 0 else pl.BlockSpec((tn, tk), lambda i, j, kk: (j, kk))
    o_spec = pl.BlockSpec((tm, tn), lambda i, j, kk: (i, j))
    has_res = residual is not None

    def body(*refs):
        refs = refs[:2 + has_res] + refs[2 + has_res + (after is not None):]
        if has_res:
            a_ref, b_ref, r_ref, o_ref, acc = refs
        else:
            a_ref, b_ref, o_ref, acc = refs
        kk = pl.program_id(2)
        part = _dg(a_ref[...].astype(BF16), b_ref[...].astype(BF16), ca, cb)
        if nk == 1:
            o_ref[...] = (part + r_ref[...] if has_res else part).astype(out_dtype)
            return

        @pl.when(kk == 0)
        def _():
            acc[...] = part

        @pl.when(kk > 0)
        def _():
            acc[...] += part

        @pl.when(kk == nk - 1)
        def _():
            r = acc[...]
            if has_res:
                r = r + r_ref[...]
            o_ref[...] = r.astype(out_dtype)

    ins = [a, b] + ([residual] if has_res else []) + ([after] if after is not None else [])
    specs = [a_spec, b_spec] + ([o_spec] if has_res else []) + ([pl.BlockSpec(memory_space=pl.ANY)] if after is not None else [])
    return pl.pallas_call(
        body, name=name, grid=(m // tm, n // tn, nk), in_specs=specs, out_specs=o_spec,
        out_shape=jax.ShapeDtypeStruct((m, n), out_dtype),
        scratch_shapes=[pltpu.VMEM((tm, tn) if nk > 1 else (8, LANES), F32)],
        compiler_params=_cparams(("parallel", "parallel", "arbitrary")),
    )(*ins)


ROW_TILE = 512


def _rmsnorm_fwd(x, w, name):
    def body(x_ref, w_ref, o_ref):
        o_ref[...] = _rms(x_ref[...], w_ref[...]).astype(BF16)

    return pl.pallas_call(
        body, name=name, grid=(SEQ // ROW_TILE,),
        in_specs=[pl.BlockSpec((ROW_TILE, D_MODEL), lambda i: (i, 0)), pl.BlockSpec((1, D_MODEL), lambda i: (0, 0))],
        out_specs=pl.BlockSpec((ROW_TILE, D_MODEL), lambda i: (i, 0)),
        out_shape=jax.ShapeDtypeStruct((SEQ, D_MODEL), BF16),
        compiler_params=_cparams(("parallel",)),
    )(x, w)


def _rmsnorm_bwd(x, w, dh, dres, name):
    def body(x_ref, w_ref, dh_ref, dres_ref, dx_ref, dw_ref):
        _, vjp = jax.vjp(_rms, x_ref[...], w_ref[...])
        dx, dw = vjp(dh_ref[...])
        dx_ref[...] = dx + dres_ref[...]

        @pl.when(pl.program_id(0) == 0)
        def _():
            dw_ref[...] = dw

        @pl.when(pl.program_id(0) > 0)
        def _():
            dw_ref[...] += dw

    tile = pl.BlockSpec((ROW_TILE, D_MODEL), lambda i: (i, 0))
    vec = pl.BlockSpec((1, D_MODEL), lambda i: (0, 0))
    return pl.pallas_call(
        body, name=name, grid=(SEQ // ROW_TILE,),
        in_specs=[tile, vec, tile, tile], out_specs=[tile, vec],
        out_shape=[jax.ShapeDtypeStruct((SEQ, D_MODEL), F32), jax.ShapeDtypeStruct((1, D_MODEL), F32)],
        compiler_params=_cparams(("arbitrary",)),
    )(x, w, dh, dres)


def _out_bwd(dx, w_out, merged, name, after=None):
    nt = SEQ // ROW_TILE

    def body(*refs):
        dx_ref, w_ref, m_ref = refs[:3]
        dm_ref, gw_ref, acc = refs[-3:]
        i = pl.program_id(0)
        dxb = dx_ref[...].astype(BF16)
        dm_ref[...] = _dg(dxb, w_ref[...].astype(BF16), 1, 1)
        part = _dg(m_ref[...].astype(BF16), dxb, 0, 0)

        @pl.when(i == 0)
        def _():
            acc[...] = part

        @pl.when(i > 0)
        def _():
            acc[...] += part

        @pl.when(i == nt - 1)
        def _():
            gw_ref[...] = acc[...].astype(BF16)

    tile = pl.BlockSpec((ROW_TILE, D_MODEL), lambda i: (i, 0))
    full = pl.BlockSpec((D_MODEL, D_MODEL), lambda i: (0, 0))
    ins = [dx, w_out, merged] + ([after] if after is not None else [])
    return pl.pallas_call(
        body, name=name, grid=(nt,),
        in_specs=[tile, full, tile] + ([pl.BlockSpec(memory_space=pl.ANY)] if after is not None else []),
        out_specs=[tile, full],
        out_shape=[jax.ShapeDtypeStruct((SEQ, D_MODEL), F32), jax.ShapeDtypeStruct((D_MODEL, D_MODEL), BF16)],
        scratch_shapes=[pltpu.VMEM((D_MODEL, D_MODEL), F32)],
        compiler_params=_cparams(("arbitrary",)),
    )(*ins)


def _dh_rms_bwd(dproj, w_in, x, w, dres, name, after=None):
    tm, tk = 512, IN_PAD // 4
    nk = IN_PAD // tk

    def body(*refs):
        a_ref, b_ref, x_ref, w_ref, dres_ref = refs[:5]
        dx_ref, dw_ref, acc = refs[-3:]
        i, kk = pl.program_id(0), pl.program_id(1)
        part = _dg(a_ref[...].astype(BF16), b_ref[...].astype(BF16), 1, 1)

        @pl.when(kk == 0)
        def _():
            acc[...] = part

        @pl.when(kk > 0)
        def _():
            acc[...] += part

        @pl.when(kk == nk - 1)
        def _():
            _, vjp = jax.vjp(_rms, x_ref[...], w_ref[...])
            dx, dw = vjp(acc[...])
            dx_ref[...] = dx + dres_ref[...]

            @pl.when(i == 0)
            def _():
                dw_ref[...] = dw

            @pl.when(i > 0)
            def _():
                dw_ref[...] += dw

    tile = pl.BlockSpec((tm, D_MODEL), lambda i, kk: (i, 0))
    vec = pl.BlockSpec((1, D_MODEL), lambda i, kk: (0, 0))
    ins = [dproj, w_in, x, w, dres] + ([after] if after is not None else [])
    specs = [pl.BlockSpec((tm, tk), lambda i, kk: (i, kk)), pl.BlockSpec((D_MODEL, tk), lambda i, kk: (0, kk)), tile, vec, tile]
    return pl.pallas_call(
        body, name=name, grid=(SEQ // tm, nk),
        in_specs=specs + ([pl.BlockSpec(memory_space=pl.ANY)] if after is not None else []), out_specs=[tile, vec],
        out_shape=[jax.ShapeDtypeStruct((SEQ, D_MODEL), F32), jax.ShapeDtypeStruct((1, D_MODEL), F32)],
        scratch_shapes=[pltpu.VMEM((tm, D_MODEL), F32)],
        compiler_params=_cparams(("arbitrary", "arbitrary")),
    )(*ins)


def _loss_head(x, w, target):
    def body(x_ref, w_ref, t_ref, loss_ref, dx_ref, dw_ref):
        tgt = t_ref[...]

        def f(xv, wv):
            err = _rms(xv, wv) - tgt
            return 0.5 * jnp.sum(jnp.mean(err * err, axis=-1))

        loss, vjp = jax.vjp(f, x_ref[...], w_ref[...])
        dx, dw = vjp(jnp.ones((), F32))
        dx_ref[...] = dx
        lrow = jnp.full((1, LANES), loss, F32)

        @pl.when(pl.program_id(0) == 0)
        def _():
            dw_ref[...] = dw
            loss_ref[...] = lrow

        @pl.when(pl.program_id(0) > 0)
        def _():
            dw_ref[...] += dw
            loss_ref[...] += lrow

    tile = pl.BlockSpec((ROW_TILE, D_MODEL), lambda i: (i, 0))
    vec = pl.BlockSpec((1, D_MODEL), lambda i: (0, 0))
    return pl.pallas_call(
        body, name="loss_head", grid=(SEQ // ROW_TILE,),
        in_specs=[tile, vec, tile], out_specs=[pl.BlockSpec((1, LANES), lambda i: (0, 0)), tile, vec],
        out_shape=[jax.ShapeDtypeStruct((1, LANES), F32), jax.ShapeDtypeStruct((SEQ, D_MODEL), F32),
                   jax.ShapeDtypeStruct((1, D_MODEL), F32)],
        compiler_params=_cparams(("arbitrary",)),
    )(x, w, target)


def _out_norm(merged, w_out, x, nw, name):
    def body(m_ref, w_ref, x_ref, nw_ref, xo_ref, h_ref):
        x_new = _dg(m_ref[...].astype(BF16), w_ref[...].astype(BF16), 1, 0) + x_ref[...]
        xo_ref[...] = x_new
        h_ref[...] = _rms(x_new, nw_ref[...]).astype(BF16)

    tile = pl.BlockSpec((ROW_TILE, D_MODEL), lambda i: (i, 0))
    return pl.pallas_call(
        body, name=name, grid=(SEQ // ROW_TILE,),
        in_specs=[tile, pl.BlockSpec((D_MODEL, D_MODEL), lambda i: (0, 0)), tile, pl.BlockSpec((1, D_MODEL), lambda i: (0, 0))],
        out_specs=[tile, tile],
        out_shape=[jax.ShapeDtypeStruct((SEQ, D_MODEL), F32), jax.ShapeDtypeStruct((SEQ, D_MODEL), BF16)],
        compiler_params=_cparams(("parallel",)),
    )(merged, w_out, x, nw)


def _out_loss(merged, w_out, x, w, target, name):
    def body(m_ref, wo_ref, x_ref, w_ref, t_ref, loss_ref, dx_ref, dw_ref):
        x_new = _dg(m_ref[...].astype(BF16), wo_ref[...].astype(BF16), 1, 0) + x_ref[...]
        tgt = t_ref[...]

        def f(xv, wv):
            err = _rms(xv, wv) - tgt
            return 0.5 * jnp.sum(jnp.mean(err * err, axis=-1))

        loss, vjp = jax.vjp(f, x_new, w_ref[...])
        dx, dw = vjp(jnp.ones((), F32))
        dx_ref[...] = dx
        lrow = jnp.full((1, LANES), loss, F32)

        @pl.when(pl.program_id(0) == 0)
        def _():
            dw_ref[...] = dw
            loss_ref[...] = lrow

        @pl.when(pl.program_id(0) > 0)
        def _():
            dw_ref[...] += dw
            loss_ref[...] += lrow

    tile = pl.BlockSpec((ROW_TILE, D_MODEL), lambda i: (i, 0))
    vec = pl.BlockSpec((1, D_MODEL), lambda i: (0, 0))
    return pl.pallas_call(
        body, name=name, grid=(SEQ // ROW_TILE,),
        in_specs=[tile, pl.BlockSpec((D_MODEL, D_MODEL), lambda i: (0, 0)), tile, vec, tile],
        out_specs=[pl.BlockSpec((1, LANES), lambda i: (0, 0)), tile, vec],
        out_shape=[jax.ShapeDtypeStruct((1, LANES), F32), jax.ShapeDtypeStruct((SEQ, D_MODEL), F32),
                   jax.ShapeDtypeStruct((1, D_MODEL), F32)],
        compiler_params=_cparams(("arbitrary",)),
    )(merged, w_out, x, w, target)


S5_T = 256
S5_BLOCKS = [(slice(j * 256, (j + 1) * 256), slice(j * 1024, (j + 1) * 1024)) for j in range(2)]


def _s5_post(ypre, gate, wglu):
    y = _gelu(ypre)
    y = y * jax.nn.sigmoid(_bdot(y, wglu, 1, 0))
    return y * _silu(gate)


def _s5_fwd(proj, bbre, bbim, cre, cim, a2, dvec, wglu, name):
    def body(u_ref, g_ref, bbre_ref, bbim_ref, cre_ref, cim_ref, a_ref, d_ref, wg_ref, o_ref, sre_ref, sim_ref, st):
        @pl.when(pl.program_id(0) == 0)
        def _():
            st[...] = jnp.zeros_like(st)

        u = u_ref[...]
        ub = u.astype(BF16)
        for us, ss in S5_BLOCKS:
            sre_ref[:, ss] = _dg(ub[:, us], bbre_ref[us, ss], 1, 0)
            sim_ref[:, ss] = _dg(ub[:, us], bbim_ref[us, ss], 1, 0)
        ar, ai = a_ref[0:1, :], a_ref[1:2, :]

        def step(t, carry):
            sr, si = carry
            nr = ar * sr - ai * si + sre_ref[pl.ds(t, 1), :]
            ni = ar * si + ai * sr + sim_ref[pl.ds(t, 1), :]
            sre_ref[pl.ds(t, 1), :] = nr
            sim_ref[pl.ds(t, 1), :] = ni
            return nr, ni

        sr, si = lax.fori_loop(0, S5_T, step, (st[0:1, :], st[1:2, :]), unroll=8)
        st[0:1, :] = sr
        st[1:2, :] = si
        ypre = jnp.concatenate(
            [_dg(sre_ref[:, ss].astype(BF16), cre_ref[ss, us], 1, 0) - _dg(sim_ref[:, ss].astype(BF16), cim_ref[ss, us], 1, 0)
             for us, ss in S5_BLOCKS], axis=1) + d_ref[...] * u
        o_ref[0] = _s5_post(ypre, g_ref[...], wg_ref[...]).astype(BF16)

    full = lambda shape: pl.BlockSpec(shape, lambda c: (0, 0))
    return pl.pallas_call(
        body, name=name, grid=(SEQ // S5_T,),
        in_specs=[pl.BlockSpec((S5_T, BW), lambda c: (c, C_S5U // BW)), pl.BlockSpec((S5_T, BW), lambda c: (c, C_S5G // BW)),
                  full((BW, S5_CH)), full((BW, S5_CH)), full((S5_CH, BW)), full((S5_CH, BW)),
                  full((2, S5_CH)), full((1, BW)), full((BW, BW))],
        out_specs=[pl.BlockSpec((1, S5_T, BW), lambda c: (0, c, 0)), pl.BlockSpec((S5_T, S5_CH), lambda c: (c, 0)),
                   pl.BlockSpec((S5_T, S5_CH), lambda c: (c, 0))],
        out_shape=[jax.ShapeDtypeStruct((N_BRANCH, SEQ, BW), BF16), jax.ShapeDtypeStruct((SEQ, S5_CH), F32),
                   jax.ShapeDtypeStruct((SEQ, S5_CH), F32)],
        scratch_shapes=[pltpu.VMEM((2, S5_CH), F32)],
        compiler_params=_cparams(("arbitrary",)),
    )(proj, proj, bbre, bbim, cre, cim, a2, dvec, wglu)


def _s5_bwd(proj, dproj, dout, sre, sim, bbre, bbim, cre, cim, a2, dvec, wglu, name):
    nc = SEQ // S5_T

    def body(u_ref, g_ref, do_ref, sre_ref, sim_ref, pre_ref, pim_ref, bbre_ref, bbim_ref, cre_ref, cim_ref, a_ref,
             d_ref, wg_ref, dproj_in, dp_ref, dbbre_ref, dbbim_ref, dcre_ref, dcim_ref, da_ref, dd_ref, dwg_ref,
             gre, gim, st):
        c = nc - 1 - pl.program_id(0)

        @pl.when(pl.program_id(0) == 0)
        def _():
            st[...] = jnp.zeros_like(st)
            for r in (dbbre_ref, dbbim_ref, dcre_ref, dcim_ref, da_ref, dd_ref, dwg_ref):
                r[...] = jnp.zeros_like(r)

        u = u_ref[...]
        s_re, s_im = sre_ref[...], sim_ref[...]

        def head(s_res, s_ims, cres, cims, dv, uv, gv, wg):
            ypre = jnp.concatenate([_bdot(sr, cr, 1, 0) - _bdot(si, ci, 1, 0)
                                    for sr, si, cr, ci in zip(s_res, s_ims, cres, cims)], axis=1) + dv * uv
            return _s5_post(ypre, gv, wg)

        _, vjp = jax.vjp(head, [sre_ref[:, ss] for _, ss in S5_BLOCKS], [sim_ref[:, ss] for _, ss in S5_BLOCKS],
                         [cre_ref[ss, us].astype(F32) for us, ss in S5_BLOCKS],
                         [cim_ref[ss, us].astype(F32) for us, ss in S5_BLOCKS],
                         d_ref[...], u, g_ref[...], wg_ref[...].astype(F32))
        ds_res, ds_ims, dcres, dcims, dd, du_d, dgate, dwg = vjp(do_ref[0])
        for k, (us, ss) in enumerate(S5_BLOCKS):
            dcre_ref[ss, us] += dcres[k]
            dcim_ref[ss, us] += dcims[k]
            gre[:, ss] = ds_res[k]
            gim[:, ss] = ds_ims[k]
        dd_ref[...] += dd
        dwg_ref[...] += dwg
        dp_ref[:, BW:] = dgate.astype(BF16)
        ar, ai = a_ref[0:1, :], a_ref[1:2, :]

        def step(i, carry):
            t = S5_T - 1 - i
            gr, gi = carry
            nr = gre[pl.ds(t, 1), :] + gr
            ni = gim[pl.ds(t, 1), :] + gi
            gre[pl.ds(t, 1), :] = nr
            gim[pl.ds(t, 1), :] = ni
            return ar * nr + ai * ni, ar * ni - ai * nr

        gr, gi = lax.fori_loop(0, S5_T, step, (st[0:1, :], st[1:2, :]), unroll=8)
        st[0:1, :] = gr
        st[1:2, :] = gi
        g_re, g_im = gre[...], gim[...]
        first = jnp.where(c > 0, 1.0, 0.0)
        row = lax.broadcasted_iota(jnp.int32, (S5_T, S5_CH), 0)
        p_re = jnp.where(row == 0, pre_ref[7:8, :] * first, pltpu.roll(s_re, 1, 0))
        p_im = jnp.where(row == 0, pim_ref[7:8, :] * first, pltpu.roll(s_im, 1, 0))
        da_ref[0:1, :] += jnp.sum(g_re * p_re + g_im * p_im, axis=0, keepdims=True)
        da_ref[1:2, :] += jnp.sum(g_im * p_re - g_re * p_im, axis=0, keepdims=True)
        ub, grb, gib = u.astype(BF16), g_re.astype(BF16), g_im.astype(BF16)
        du_s = []
        for us, ss in S5_BLOCKS:
            dbbre_ref[us, ss] += _dg(ub[:, us], grb[:, ss], 0, 0)
            dbbim_ref[us, ss] += _dg(ub[:, us], gib[:, ss], 0, 0)
            du_s.append(_dg(grb[:, ss], bbre_ref[us, ss], 1, 1) + _dg(gib[:, ss], bbim_ref[us, ss], 1, 1))
        dp_ref[:, :BW] = (du_d + jnp.concatenate(du_s, axis=1)).astype(BF16)

    full = lambda shape: pl.BlockSpec(shape, lambda i: (0, 0))
    rev = lambda w, col=0: pl.BlockSpec((S5_T, w), lambda i: (nc - 1 - i, col))
    prev = pl.BlockSpec((8, S5_CH), lambda i: (jnp.maximum((nc - 1 - i) * (S5_T // 8) - 1, 0), 0))
    return pl.pallas_call(
        body, name=name, grid=(nc,),
        in_specs=[rev(BW, C_S5U // BW), rev(BW, C_S5G // BW), pl.BlockSpec((1, S5_T, BW), lambda i: (0, nc - 1 - i, 0)),
                  rev(S5_CH), rev(S5_CH), prev, prev,
                  full((BW, S5_CH)), full((BW, S5_CH)), full((S5_CH, BW)), full((S5_CH, BW)),
                  full((2, S5_CH)), full((1, BW)), full((BW, BW)), pl.BlockSpec(memory_space=pl.ANY)],
        out_specs=[rev(2 * BW, C_S5U // (2 * BW)), full((BW, S5_CH)), full((BW, S5_CH)), full((S5_CH, BW)), full((S5_CH, BW)),
                   full((2, S5_CH)), full((1, BW)), full((BW, BW))],
        input_output_aliases={14: 0},
        out_shape=[jax.ShapeDtypeStruct((SEQ, IN_PAD), BF16),
                   jax.ShapeDtypeStruct((BW, S5_CH), F32), jax.ShapeDtypeStruct((BW, S5_CH), F32),
                   jax.ShapeDtypeStruct((S5_CH, BW), F32), jax.ShapeDtypeStruct((S5_CH, BW), F32),
                   jax.ShapeDtypeStruct((2, S5_CH), F32), jax.ShapeDtypeStruct((1, BW), F32),
                   jax.ShapeDtypeStruct((BW, BW), F32)],
        scratch_shapes=[pltpu.VMEM((S5_T, S5_CH), F32), pltpu.VMEM((S5_T, S5_CH), F32), pltpu.VMEM((2, S5_CH), F32)],
        compiler_params=_cparams(("arbitrary",)),
    )(proj, proj, dout, sre, sim, sre, sim, bbre, bbim, cre, cim, a2, dvec, wglu, dproj)


def _diag_blocks(dense, after=None):
    rows, cols = dense.shape
    rows_per, cols_per = rows // S5_GROUPS, cols // S5_GROUPS
    per_lane_block = LANES // cols_per
    tile = 512

    def body(d_ref, *rest):
        o_ref = rest[-1]
        r0 = pl.program_id(0) * tile
        grp = (r0 + lax.broadcasted_iota(jnp.int32, (tile, LANES), 0)) // rows_per
        lane = lax.broadcasted_iota(jnp.int32, (tile, LANES), 1)
        acc = jnp.zeros((tile, LANES), F32)
        for hb in range(cols // LANES):
            acc = acc + jnp.where(grp == per_lane_block * hb + lane // cols_per, d_ref[:, hb * LANES:(hb + 1) * LANES], 0.0)
        shift = LANES // 2
        while shift >= cols_per:
            acc = acc + pltpu.roll(acc, LANES - shift, 1)
            shift //= 2
        o_ref[...] = acc

    folded = pl.pallas_call(
        body, name=f"diag_blocks_{rows_per}x{cols_per}", grid=(rows // tile,),
        in_specs=[pl.BlockSpec((tile, cols), lambda i: (i, 0))] + ([] if after is None else [pl.BlockSpec(memory_space=pl.ANY)]),
        out_specs=pl.BlockSpec((tile, LANES), lambda i: (i, 0)),
        out_shape=jax.ShapeDtypeStruct((rows, LANES), F32), compiler_params=_cparams(("parallel",)),
    )(dense, *([] if after is None else [after]))
    return folded[:, :cols_per].reshape(S5_GROUPS, rows_per, cols_per)


def _block_diag(t):
    g, rows_per, cols_per = t.shape
    wide = jnp.tile(t.reshape(g * rows_per, cols_per), (1, g))
    r = lax.broadcasted_iota(jnp.int32, wide.shape, 0) // rows_per
    c = lax.broadcasted_iota(jnp.int32, wide.shape, 1) // cols_per
    return jnp.where(r == c, wide, 0.0)


def _s5_disc(lam_re, lam_im, b_re, b_im, c_re, c_im, d, log_step):
    step = jnp.exp(log_step)[:, None]
    mag = jnp.exp(lam_re * step)
    ab_re, ab_im = mag * jnp.cos(lam_im * step), mag * jnp.sin(lam_im * step)
    den = lam_re * lam_re + lam_im * lam_im
    nr = ab_re - 1.0
    coef_re = (nr * lam_re + ab_im * lam_im) / den
    coef_im = (ab_im * lam_re - nr * lam_im) / den
    bb_re = coef_re[..., None] * b_re - coef_im[..., None] * b_im
    bb_im = coef_re[..., None] * b_im + coef_im[..., None] * b_re
    a2 = jnp.stack([ab_re.reshape(-1), ab_im.reshape(-1)])
    return (jnp.swapaxes(bb_re, 1, 2), jnp.swapaxes(bb_im, 1, 2),
            jnp.swapaxes(c_re, 1, 2), jnp.swapaxes(c_im, 1, 2),
            a2, d.reshape(1, BW))


def _left_lanes(shape):
    return lax.broadcasted_iota(jnp.int32, shape, 1) < 64


def _sgu_chunk(u, v, gate, ln_w, ln_b, w, bias):
    u32, v32 = _gelu(u), _gelu(v)
    mu = jnp.mean(v32, axis=-1, keepdims=True)
    var = jnp.mean(jnp.square(v32 - mu), axis=-1, keepdims=True)
    vn = (v32 - mu) * lax.rsqrt(var + EPS) * ln_w + ln_b
    t_i = lax.broadcasted_iota(jnp.int32, (SGU_CHUNK, SGU_CHUNK), 0)
    s_i = lax.broadcasted_iota(jnp.int32, (SGU_CHUNK, SGU_CHUNK), 1)
    causal = t_i >= s_i
    left = _left_lanes((SGU_CHUNK, LANES))
    sgate = _silu(gate)
    outs = []
    for j in range(BW // LANES):
        vb = vn[:, j * LANES:(j + 1) * LANES]
        s_blk = (_bdot(jnp.where(causal, w[2 * j], 0.0), jnp.where(left, vb, 0.0), 1, 0)
                 + _bdot(jnp.where(causal, w[2 * j + 1], 0.0), jnp.where(left, 0.0, vb), 1, 0))
        sl = slice(j * LANES, (j + 1) * LANES)
        outs.append(u32[:, sl] * (s_blk + bias[:, sl]) * sgate[:, sl])
    return outs


def _sgu_fwd(proj, ys, ln_w, ln_b, w, bias, name):
    def body(u_ref, v_ref, g_ref, lw_ref, lb_ref, w_ref, b_ref, ys_in, o_ref):
        outs = _sgu_chunk(u_ref[...], v_ref[...], g_ref[...], lw_ref[...], lb_ref[...], w_ref[...], b_ref[...])
        for j, o in enumerate(outs):
            o_ref[0, :, j * LANES:(j + 1) * LANES] = o.astype(BF16)

    blk = lambda col: pl.BlockSpec((SGU_CHUNK, BW), lambda c: (c, col // BW))
    vec = pl.BlockSpec((1, BW), lambda c: (0, 0))
    return pl.pallas_call(
        body, name=name, grid=(SEQ // SGU_CHUNK,),
        in_specs=[blk(C_SGU_U), blk(C_SGU_V), blk(C_SGU_G), vec, vec,
                  pl.BlockSpec((SGU_HEADS, SGU_CHUNK, SGU_CHUNK), lambda c: (0, 0, 0)),
                  pl.BlockSpec((SGU_CHUNK, BW), lambda c: (0, 0)), pl.BlockSpec(memory_space=pl.ANY)],
        out_specs=pl.BlockSpec((1, SGU_CHUNK, BW), lambda c: (1, c, 0)),
        out_shape=jax.ShapeDtypeStruct((N_BRANCH, SEQ, BW), BF16), input_output_aliases={7: 0},
        compiler_params=_cparams(("parallel",)),
    )(proj, proj, proj, ln_w, ln_b, w, bias, ys)


def _sgu_bwd(proj, dproj, dout, ln_w, ln_b, w, bias, name):
    def body(u_ref, v_ref, g_ref, do_ref, lw_ref, lb_ref, w_ref, b_ref, dproj_in, dp_ref, dlw_ref, dlb_ref, dw_ref, db_ref):
        _, vjp = jax.vjp(_sgu_chunk, u_ref[...], v_ref[...], g_ref[...], lw_ref[...], lb_ref[...], w_ref[...], b_ref[...])
        do = do_ref[0]
        du, dv, dgate, dlw, dlb, dw, db = vjp([do[:, j * LANES:(j + 1) * LANES] for j in range(BW // LANES)])
        dp_ref[:, 0:BW] = du.astype(BF16)
        dp_ref[:, BW:2 * BW] = dv.astype(BF16)
        dp_ref[:, 2 * BW:3 * BW] = dgate.astype(BF16)
        dp_ref[:, 3 * BW:] = jnp.zeros((SGU_CHUNK, BW), BF16)

        @pl.when(pl.program_id(0) == 0)
        def _():
            dlw_ref[...] = dlw
            dlb_ref[...] = dlb
            dw_ref[...] = dw
            db_ref[...] = db

        @pl.when(pl.program_id(0) > 0)
        def _():
            dlw_ref[...] += dlw
            dlb_ref[...] += dlb
            dw_ref[...] += dw
            db_ref[...] += db

    blk = lambda col: pl.BlockSpec((SGU_CHUNK, BW), lambda c: (c, col // BW))
    vec = pl.BlockSpec((1, BW), lambda c: (0, 0))
    wsp = pl.BlockSpec((SGU_HEADS, SGU_CHUNK, SGU_CHUNK), lambda c: (0, 0, 0))
    bsp = pl.BlockSpec((SGU_CHUNK, BW), lambda c: (0, 0))
    return pl.pallas_call(
        body, name=name, grid=(SEQ // SGU_CHUNK,),
        in_specs=[blk(C_SGU_U), blk(C_SGU_V), blk(C_SGU_G), pl.BlockSpec((1, SGU_CHUNK, BW), lambda c: (1, c, 0)),
                  vec, vec, wsp, bsp, pl.BlockSpec(memory_space=pl.ANY)],
        out_specs=[pl.BlockSpec((SGU_CHUNK, 4 * BW), lambda c: (c, C_SGU_U // (4 * BW))), vec, vec, wsp, bsp],
        input_output_aliases={8: 0},
        out_shape=[jax.ShapeDtypeStruct((SEQ, IN_PAD), BF16), jax.ShapeDtypeStruct((1, BW), F32),
                   jax.ShapeDtypeStruct((1, BW), F32), jax.ShapeDtypeStruct((SGU_HEADS, SGU_CHUNK, SGU_CHUNK), F32),
                   jax.ShapeDtypeStruct((SGU_CHUNK, BW), F32)],
        compiler_params=_cparams(("arbitrary",)),
    )(proj, proj, proj, dout, ln_w, ln_b, w, bias, dproj)


CONV_BLK = 256


def _m2_conv_fwd(proj, w, b, name):
    def body(x_ref, w_ref, b_ref, o_ref):
        x = x_ref[...]
        acc = jnp.zeros_like(x) + b_ref[...]
        for k in range(M2_CONV):
            acc = acc + w_ref[k:k + 1, :] * _shift_down(x, M2_CONV - 1 - k)
        o_ref[...] = _silu(acc)

    return pl.pallas_call(
        body, name=name, grid=(M2_CONV_CH // CONV_BLK,),
        in_specs=[pl.BlockSpec((SEQ, CONV_BLK), lambda j: (0, C_M2X // CONV_BLK + j)),
                  pl.BlockSpec((M2_CONV, CONV_BLK), lambda j: (0, j)), pl.BlockSpec((1, CONV_BLK), lambda j: (0, j))],
        out_specs=pl.BlockSpec((SEQ, CONV_BLK), lambda j: (0, j)),
        out_shape=jax.ShapeDtypeStruct((SEQ, M2_CONV_CH), F32),
        compiler_params=_cparams(("parallel",)),
    )(proj, w, b)


def _m2_conv_bwd(proj, dproj, dxa, w, b, name):
    def body(x_ref, d_ref, w_ref, b_ref, dproj_in, dx_ref, dw_ref, db_ref):
        x = x_ref[...]
        xs = [_shift_down(x, M2_CONV - 1 - k) for k in range(M2_CONV)]
        acc = jnp.zeros_like(x) + b_ref[...]
        for k in range(M2_CONV):
            acc = acc + w_ref[k:k + 1, :] * xs[k]
        sg = jax.nn.sigmoid(acc)
        dacc = d_ref[...] * (sg * (1.0 + acc * (1.0 - sg)))
        dx = jnp.zeros_like(x)
        for k in range(M2_CONV):
            dx = dx + w_ref[k:k + 1, :] * _shift_up(dacc, M2_CONV - 1 - k)
            dw_ref[k:k + 1, :] = jnp.sum(dacc * xs[k], axis=0, keepdims=True)
        dx_ref[...] = dx.astype(BF16)
        db_ref[...] = jnp.sum(dacc, axis=0, keepdims=True)

    return pl.pallas_call(
        body, name=name, grid=(M2_CONV_CH // CONV_BLK,),
        in_specs=[pl.BlockSpec((SEQ, CONV_BLK), lambda j: (0, C_M2X // CONV_BLK + j)),
                  pl.BlockSpec((SEQ, CONV_BLK), lambda j: (0, j)),
                  pl.BlockSpec((M2_CONV, CONV_BLK), lambda j: (0, j)), pl.BlockSpec((1, CONV_BLK), lambda j: (0, j)),
                  pl.BlockSpec(memory_space=pl.ANY)],
        out_specs=[pl.BlockSpec((SEQ, CONV_BLK), lambda j: (0, C_M2X // CONV_BLK + j)),
                   pl.BlockSpec((M2_CONV, CONV_BLK), lambda j: (0, j)), pl.BlockSpec((1, CONV_BLK), lambda j: (0, j))],
        input_output_aliases={4: 0},
        out_shape=[jax.ShapeDtypeStruct((SEQ, IN_PAD), BF16), jax.ShapeDtypeStruct((M2_CONV, M2_CONV_CH), F32),
                   jax.ShapeDtypeStruct((1, M2_CONV_CH), F32)],
        compiler_params=_cparams(("parallel",)),
    )(proj, dxa, w, b, dproj)


N_PAIR = M2_HEADS // 2
HI = lax.Precision.HIGHEST


def _col(a, h):
    lane = lax.broadcasted_iota(jnp.int32, a.shape, 1)
    return jnp.sum(jnp.where(lane == h, a, 0.0), axis=1, keepdims=True)


def _row(a, h):
    sub = lax.broadcasted_iota(jnp.int32, a.shape, 0)
    return jnp.sum(jnp.where(sub == h, a, 0.0), axis=0, keepdims=True)


def _ssd_chunk(xs, bms, cms, dtr, zs, states, dt_bias, a_log, dfs, nws):
    q = M2_CHUNK
    dt = _softplus(dtr + dt_bias)
    da = dt * (-jnp.exp(a_log))
    l_i = lax.broadcasted_iota(jnp.int32, (q, q), 0)
    s_i = lax.broadcasted_iota(jnp.int32, (q, q), 1)
    causal = l_i >= s_i
    tril = jnp.where(causal, 1.0, 0.0)
    a_cs = _dg(tril, da, 1, 0, HI)
    a_cs_t = _dg(da, tril, 0, 1, HI)
    a_end = _row(a_cs, q - 1)
    left = _left_lanes((q, LANES))
    left1 = _left_lanes((1, LANES))
    ys, nexts = [], []
    for j in range(N_PAIR):
        grp = j // 2
        bm, cm = bms[grp], cms[grp]
        h0, h1 = 2 * j, 2 * j + 1
        cb = _bdot(cm, bm, 1, 1)
        xdt = xs[j] * jnp.where(left, _col(dt, h0), _col(dt, h1))
        acs0, acs1 = _col(a_cs, h0), _col(a_cs, h1)
        y = _bdot(cm, states[j], 1, 0) * jnp.where(left, jnp.exp(acs0), jnp.exp(acs1))
        s_new = states[j] * jnp.where(left1, jnp.exp(_col(a_end, h0)), jnp.exp(_col(a_end, h1)))
        for h, acs, xh in ((h0, acs0, jnp.where(left, xdt, 0.0)), (h1, acs1, jnp.where(left, 0.0, xdt))):
            decay = jnp.exp(jnp.where(causal, acs - _row(a_cs_t, h), -jnp.inf))
            y = y + _bdot(cb * decay, xh, 1, 0)
            s_new = s_new + _bdot(bm * jnp.exp(_col(a_end, h) - acs), xh, 0, 0)
        ys.append((y + dfs[j] * xs[j]) * _silu(zs[j]))
        nexts.append(s_new)
    ssq = sum(jnp.sum(y * y, axis=-1, keepdims=True) for y in ys)
    scale = lax.rsqrt(ssq / BW + EPS)
    return [y * scale * nw for y, nw in zip(ys, nws)], nexts


def _blocks(ref, n, width=LANES):
    return [ref[:, j * width:(j + 1) * width] for j in range(n)]


def _ssd_fwd(proj, ys, xa, dt_bias, a_log, dfull, nw, name):
    nc = SEQ // M2_CHUNK

    def body(x_ref, b_ref, c_ref, dt_ref, z_ref, dtb_ref, al_ref, df_ref, nw_ref, ys_in, o_ref, sin_ref, st):
        @pl.when(pl.program_id(0) == 0)
        def _():
            st[...] = jnp.zeros_like(st)

        states = [st[j] for j in range(N_PAIR)]
        for j in range(N_PAIR):
            sin_ref[0, j] = states[j]
        ys, nexts = _ssd_chunk(_blocks(x_ref, 4), _blocks(b_ref, 2), _blocks(c_ref, 2), dt_ref[...], _blocks(z_ref, 4),
                               states, dtb_ref[...], al_ref[...], _blocks(df_ref, 4), _blocks(nw_ref, 4))
        for j in range(N_PAIR):
            o_ref[0, :, j * LANES:(j + 1) * LANES] = ys[j].astype(BF16)
            st[j] = nexts[j]

    vec8 = pl.BlockSpec((1, LANES), lambda c: (0, 0))
    vec = pl.BlockSpec((1, BW), lambda c: (0, 0))
    return pl.pallas_call(
        body, name=name, grid=(nc,),
        in_specs=[pl.BlockSpec((M2_CHUNK, BW), lambda c: (c, 0)), pl.BlockSpec((M2_CHUNK, 256), lambda c: (c, 2)),
                  pl.BlockSpec((M2_CHUNK, 256), lambda c: (c, 3)), pl.BlockSpec((M2_CHUNK, LANES), lambda c: (c, C_DT // LANES)),
                  pl.BlockSpec((M2_CHUNK, BW), lambda c: (c, C_M2Z // BW)), vec8, vec8, vec, vec,
                  pl.BlockSpec(memory_space=pl.ANY)],
        out_specs=[pl.BlockSpec((1, M2_CHUNK, BW), lambda c: (2, c, 0)),
                   pl.BlockSpec((1, N_PAIR, M2_STATE, LANES), lambda c: (c, 0, 0, 0))],
        out_shape=[jax.ShapeDtypeStruct((N_BRANCH, SEQ, BW), BF16), jax.ShapeDtypeStruct((nc, N_PAIR, M2_STATE, LANES), F32)],
        input_output_aliases={9: 0},
        scratch_shapes=[pltpu.VMEM((N_PAIR, M2_STATE, LANES), F32)],
        compiler_params=_cparams(("arbitrary",)),
    )(xa, xa, xa, proj, proj, dt_bias, a_log, dfull, nw, ys)


def _ssd_bwd(proj, dproj, xa, dout, s_in, dt_bias, a_log, dfull, nw, name):
    nc = SEQ // M2_CHUNK

    def body(x_ref, b_ref, c_ref, dt_ref, z_ref, do_ref, sin_ref, dtb_ref, al_ref, df_ref, nw_ref, dproj_in,
             dp_ref, dxa_ref, ddtb_ref, dal_ref, ddf_ref, dnw_ref, dst):
        @pl.when(pl.program_id(0) == 0)
        def _():
            dst[...] = jnp.zeros_like(dst)
            for r in (ddtb_ref, dal_ref, ddf_ref, dnw_ref):
                r[...] = jnp.zeros_like(r)

        states = [sin_ref[0, j] for j in range(N_PAIR)]
        _, vjp = jax.vjp(_ssd_chunk, _blocks(x_ref, 4), _blocks(b_ref, 2), _blocks(c_ref, 2), dt_ref[...],
                         _blocks(z_ref, 4), states, dtb_ref[...], al_ref[...], _blocks(df_ref, 4), _blocks(nw_ref, 4))
        dxs, dbs, dcs, ddt, dzs, dstates, ddtb, dal, ddfs, dnws = vjp(
            ([do_ref[0, :, j * LANES:(j + 1) * LANES] for j in range(N_PAIR)], [dst[j] for j in range(N_PAIR)]))
        for j in range(N_PAIR):
            sl = slice(j * LANES, (j + 1) * LANES)
            dxa_ref[:, sl] = dxs[j]
            dp_ref[:, sl] = dzs[j].astype(BF16)
            dst[j] = dstates[j]
            ddf_ref[:, sl] += ddfs[j]
            dnw_ref[:, sl] += dnws[j]
        for g in range(2):
            dxa_ref[:, BW + g * LANES:BW + (g + 1) * LANES] = dbs[g]
            dxa_ref[:, BW + 256 + g * LANES:BW + 256 + (g + 1) * LANES] = dcs[g]
        dp_ref[:, BW:BW + LANES] = ddt.astype(BF16)
        dp_ref[:, BW + LANES:] = jnp.zeros((M2_CHUNK, 2 * BW - BW - LANES), BF16)
        ddtb_ref[...] += ddtb
        dal_ref[...] += dal

    rev = lambda w, col=0: pl.BlockSpec((M2_CHUNK, w), lambda i: (nc - 1 - i, col))
    vec8 = pl.BlockSpec((1, LANES), lambda i: (0, 0))
    vec = pl.BlockSpec((1, BW), lambda i: (0, 0))
    return pl.pallas_call(
        body, name=name, grid=(nc,),
        in_specs=[rev(BW), rev(256, 2), rev(256, 3), rev(LANES, C_DT // LANES), rev(BW, C_M2Z // BW),
                  pl.BlockSpec((1, M2_CHUNK, BW), lambda i: (2, nc - 1 - i, 0)),
                  pl.BlockSpec((1, N_PAIR, M2_STATE, LANES), lambda i: (nc - 1 - i, 0, 0, 0)), vec8, vec8, vec, vec,
                  pl.BlockSpec(memory_space=pl.ANY)],
        out_specs=[rev(2 * BW, C_M2Z // (2 * BW)), rev(M2_CONV_CH), vec8, vec8, vec, vec],
        input_output_aliases={11: 0},
        out_shape=[jax.ShapeDtypeStruct((SEQ, IN_PAD), BF16), jax.ShapeDtypeStruct((SEQ, M2_CONV_CH), F32),
                   jax.ShapeDtypeStruct((1, LANES), F32), jax.ShapeDtypeStruct((1, LANES), F32),
                   jax.ShapeDtypeStruct((1, BW), F32), jax.ShapeDtypeStruct((1, BW), F32)],
        scratch_shapes=[pltpu.VMEM((N_PAIR, M2_STATE, LANES), F32)],
        compiler_params=_cparams(("arbitrary",)),
    )(xa, xa, xa, proj, proj, dout, s_in, dt_bias, a_log, dfull, nw, dproj)


def _sc_specs():
    col = lambda kind: pl.BlockSpec((SEQ, LANES), lambda j: (0, C_SC // LANES + 4 * j + kind))
    return [col(0), col(1), col(2), col(3)]


def _sc_fwd(proj, ys, w, name):
    def body(b_ref, c_ref, h_ref, g_ref, w_ref, ys_in, o_ref):
        ch = c_ref[...] * h_ref[...]
        acc = jnp.zeros_like(ch)
        for k in range(SC_CONV):
            acc = acc + w_ref[k:k + 1, :] * _shift_down(ch, SC_CONV - 1 - k)
        o_ref[0] = (b_ref[...] * acc * _silu(g_ref[...])).astype(BF16)

    return pl.pallas_call(
        body, name=name, grid=(BW // LANES,),
        in_specs=_sc_specs() + [pl.BlockSpec((SC_CONV, LANES), lambda j: (0, j)), pl.BlockSpec(memory_space=pl.ANY)],
        out_specs=pl.BlockSpec((1, SEQ, LANES), lambda j: (3, 0, j)),
        out_shape=jax.ShapeDtypeStruct((N_BRANCH, SEQ, BW), BF16), input_output_aliases={5: 0},
        compiler_params=_cparams(("parallel",)),
    )(proj, proj, proj, proj, w, ys)


def _sc_bwd(proj, dproj, dout, w, name):
    def body(b_ref, c_ref, h_ref, g_ref, do_ref, w_ref, dproj_in, dp_ref, dw_ref):
        cv, hv, gv = c_ref[...], h_ref[...], g_ref[...]
        ch = cv * hv
        chs = [_shift_down(ch, SC_CONV - 1 - k) for k in range(SC_CONV)]
        acc = jnp.zeros_like(ch)
        for k in range(SC_CONV):
            acc = acc + w_ref[k:k + 1, :] * chs[k]
        sg = jax.nn.sigmoid(gv)
        do = do_ref[0]
        bv = b_ref[...]
        dp_ref[:, 0:LANES] = (do * acc * (gv * sg)).astype(BF16)
        dp_ref[:, 3 * LANES:] = (do * bv * acc * (sg * (1.0 + gv * (1.0 - sg)))).astype(BF16)
        dacc = do * bv * (gv * sg)
        dch = jnp.zeros_like(ch)
        for k in range(SC_CONV):
            dch = dch + w_ref[k:k + 1, :] * _shift_up(dacc, SC_CONV - 1 - k)
            dw_ref[k:k + 1, :] = jnp.sum(dacc * chs[k], axis=0, keepdims=True)
        dp_ref[:, LANES:2 * LANES] = (dch * hv).astype(BF16)
        dp_ref[:, 2 * LANES:3 * LANES] = (dch * cv).astype(BF16)

    wsp = pl.BlockSpec((SC_CONV, LANES), lambda j: (0, j))
    return pl.pallas_call(
        body, name=name, grid=(BW // LANES,),
        in_specs=_sc_specs() + [pl.BlockSpec((1, SEQ, LANES), lambda j: (3, 0, j)), wsp, pl.BlockSpec(memory_space=pl.ANY)],
        out_specs=[pl.BlockSpec((SEQ, 4 * LANES), lambda j: (0, C_SC // (4 * LANES) + j)), wsp],
        input_output_aliases={6: 0},
        out_shape=[jax.ShapeDtypeStruct((SEQ, IN_PAD), BF16), jax.ShapeDtypeStruct((SC_CONV, BW), F32)],
        compiler_params=_cparams(("parallel",)),
    )(proj, proj, proj, proj, dout, w, dproj)


MERGE_T = 256
MERGE_BWD_T = 1024


def _merge_fwd(proj, ys, merge_b, w_branch, name):
    def body(y_ref, lg_ref, b_ref, w_ref, o_ref):
        acc = jnp.zeros((MERGE_T, D_MODEL), F32)
        for k in range(N_BRANCH):
            gate = jax.nn.sigmoid(lg_ref[:, k * D_MODEL:(k + 1) * D_MODEL] + b_ref[k])
            acc = acc + gate * _dg(y_ref[k], w_ref[k], 1, 0)
        o_ref[...] = acc.astype(BF16)

    return pl.pallas_call(
        body, name=name, grid=(SEQ // MERGE_T,),
        in_specs=[pl.BlockSpec((N_BRANCH, MERGE_T, BW), lambda i: (0, i, 0)),
                  pl.BlockSpec((MERGE_T, N_BRANCH * D_MODEL), lambda i: (i, C_MERGE // (N_BRANCH * D_MODEL))),
                  pl.BlockSpec((N_BRANCH, 1, D_MODEL), lambda i: (0, 0, 0)),
                  pl.BlockSpec((N_BRANCH, BW, D_MODEL), lambda i: (0, 0, 0))],
        out_specs=pl.BlockSpec((MERGE_T, D_MODEL), lambda i: (i, 0)),
        out_shape=jax.ShapeDtypeStruct((SEQ, D_MODEL), BF16),
        compiler_params=_cparams(("parallel",)),
    )(ys, proj, merge_b, w_branch)


def _merge_bwd(proj, ys, dm, merge_b, w_branch, name):
    nt = SEQ // MERGE_BWD_T

    def body(y_ref, lg_ref, dm_ref, b_ref, w_ref, dy_ref, dlg_ref, dw_ref, db_ref, dw_acc):
        i = pl.program_id(1)
        gate = jax.nn.sigmoid(lg_ref[...] + b_ref[0])
        y = y_ref[0]
        dmv = dm_ref[...]
        dbo = (gate * dmv).astype(BF16)
        dlg = _dg(y, w_ref[0], 1, 0) * dmv * gate * (1.0 - gate)
        dlg_ref[...] = dlg.astype(BF16)
        dy_ref[0] = _dg(dbo, w_ref[0], 1, 1)
        dwp = _dg(y, dbo, 0, 0)
        dbp = jnp.sum(dlg, axis=0, keepdims=True)

        @pl.when(i == 0)
        def _():
            dw_acc[...] = dwp
            db_ref[0] = dbp

        @pl.when(i > 0)
        def _():
            dw_acc[...] += dwp
            db_ref[0] += dbp

        @pl.when(i == nt - 1)
        def _():
            dw_ref[0] = dw_acc[...].astype(BF16)

    return pl.pallas_call(
        body, name=name, grid=(N_BRANCH, nt),
        in_specs=[pl.BlockSpec((1, MERGE_BWD_T, BW), lambda k, i: (k, i, 0)),
                  pl.BlockSpec((MERGE_BWD_T, D_MODEL), lambda k, i: (i, C_MERGE // D_MODEL + k)),
                  pl.BlockSpec((MERGE_BWD_T, D_MODEL), lambda k, i: (i, 0)),
                  pl.BlockSpec((1, 1, D_MODEL), lambda k, i: (k, 0, 0)),
                  pl.BlockSpec((1, BW, D_MODEL), lambda k, i: (k, 0, 0))],
        out_specs=[pl.BlockSpec((1, MERGE_BWD_T, BW), lambda k, i: (k, i, 0)),
                   pl.BlockSpec((MERGE_BWD_T, D_MODEL), lambda k, i: (i, k)),
                   pl.BlockSpec((1, BW, D_MODEL), lambda k, i: (k, 0, 0)),
                   pl.BlockSpec((1, 1, D_MODEL), lambda k, i: (k, 0, 0))],
        out_shape=[jax.ShapeDtypeStruct((N_BRANCH, SEQ, BW), F32), jax.ShapeDtypeStruct((SEQ, IN_PAD), BF16),
                   jax.ShapeDtypeStruct((N_BRANCH, BW, D_MODEL), BF16), jax.ShapeDtypeStruct((N_BRANCH, 1, D_MODEL), F32)],
        scratch_shapes=[pltpu.VMEM((BW, D_MODEL), F32)],
        compiler_params=_cparams(("parallel", "arbitrary")),
    )(ys, proj, dm, merge_b, w_branch)


def _adamw(glist, w, m, v, rows, name):
    nl = len(glist)
    n, r, c = glist[0].shape
    assert w.shape == (nl, r, c) and r % rows == 0
    nb = r // rows

    def body(*refs):
        g_refs = refs[:nl]
        w_ref, m_ref, v_ref, go_ref, d_ref, mo_ref, vo_ref = refs[nl:]
        for layer in range(nl):
            @pl.when(pl.program_id(0) == layer)
            def _(g_ref=g_refs[layer]):
                g = g_ref[0].astype(F32)
                for s in range(1, n):
                    g = g + g_ref[s].astype(F32)
                mn = ADAM_B1 * m_ref[0] + (1.0 - ADAM_B1) * g
                vn = ADAM_B2 * v_ref[0] + (1.0 - ADAM_B2) * jnp.square(g)
                m_hat = mn / (1.0 - ADAM_B1 ** ADAM_STEP)
                v_hat = vn / (1.0 - ADAM_B2 ** ADAM_STEP)
                go_ref[0] = g
                d_ref[0] = -ADAM_LR * (m_hat / (jnp.sqrt(v_hat) + ADAM_EPS) + ADAM_WD * w_ref[0])
                mo_ref[0] = mn
                vo_ref[0] = vn

    def g_spec(layer):
        return pl.BlockSpec((n, rows, c), lambda a, i: (0, jnp.where(a < layer, 0, jnp.where(a == layer, i, nb - 1)), 0))

    blk = pl.BlockSpec((1, rows, c), lambda a, i: (a, i, 0))
    out = jax.ShapeDtypeStruct((nl, r, c), F32)
    return pl.pallas_call(
        body, name=name, grid=(nl, nb),
        in_specs=[g_spec(layer) for layer in range(nl)] + [blk, blk, blk],
        out_specs=[blk, blk, blk, blk], out_shape=[out, out, out, out],
        compiler_params=_cparams(("arbitrary", "arbitrary")),
    )(*glist, w, m, v)


X_ROWS_PER_COL = 2 * (D_MODEL // LANES)


def _w_in_to_x(w):
    t = jnp.transpose(w, (2, 0, 1)).reshape(SHARD_IN, DEPTH, D_MODEL // LANES, LANES)
    return jnp.transpose(t, (0, 2, 1, 3)).reshape(SHARD_IN * X_ROWS_PER_COL, LANES)


def _w_in_from_x(xv):
    t = jnp.transpose(xv.reshape(SHARD_IN, D_MODEL // LANES, DEPTH, LANES), (0, 2, 1, 3))
    return jnp.transpose(t.reshape(SHARD_IN, DEPTH, D_MODEL), (1, 2, 0))


def _adamw_w_in(glist, w, m, v, name, after=None):
    n = glist[0].shape[0]
    cols = 2 * LANES
    rows = cols * X_ROWS_PER_COL
    extra = [] if after is None else [after]

    def body(g0_ref, g1_ref, w_ref, m_ref, v_ref, *rest):
        go_ref, d_ref, mo_ref, vo_ref = rest[len(extra):]
        for layer, g_ref in enumerate((g0_ref, g1_ref)):
            g = g_ref[0].astype(F32)
            for s in range(1, n):
                g = g + g_ref[s].astype(F32)
            gt = g.T
            for t in range(D_MODEL // LANES):
                sel = (pl.ds(2 * t + layer, cols, stride=X_ROWS_PER_COL), slice(None))
                gs = gt[:, t * LANES:(t + 1) * LANES]
                mn = ADAM_B1 * m_ref[sel] + (1.0 - ADAM_B1) * gs
                vn = ADAM_B2 * v_ref[sel] + (1.0 - ADAM_B2) * jnp.square(gs)
                m_hat = mn / (1.0 - ADAM_B1 ** ADAM_STEP)
                v_hat = vn / (1.0 - ADAM_B2 ** ADAM_STEP)
                go_ref[sel] = gs
                d_ref[sel] = -ADAM_LR * (m_hat / (jnp.sqrt(v_hat) + ADAM_EPS) + ADAM_WD * w_ref[sel])
                mo_ref[sel] = mn
                vo_ref[sel] = vn

    g_spec = pl.BlockSpec((n, D_MODEL, cols), lambda i: (0, 0, i))
    blk = pl.BlockSpec((rows, LANES), lambda i: (i, 0))
    out = jax.ShapeDtypeStruct((SHARD_IN * X_ROWS_PER_COL, LANES), F32)
    res = pl.pallas_call(
        body, name=name, grid=(-(-SHARD_IN // cols),),
        in_specs=[g_spec, g_spec, blk, blk, blk] + [pl.BlockSpec(memory_space=pl.ANY)] * len(extra),
        out_specs=[blk, blk, blk, blk], out_shape=[out, out, out, out],
        compiler_params=_cparams(("parallel",)),
    )(*glist, _w_in_to_x(w), _w_in_to_x(m), _w_in_to_x(v), *extra)
    return [_w_in_from_x(o) for o in res]


def _adamw_many(gs, ws, ms, vs, name):
    k = len(gs)

    def body(*refs):
        g_refs, w_refs, m_refs, v_refs = refs[:k], refs[k:2 * k], refs[2 * k:3 * k], refs[3 * k:4 * k]
        d_refs, mo_refs, vo_refs = refs[4 * k:5 * k], refs[5 * k:6 * k], refs[6 * k:7 * k]
        for i in range(k):
            g = g_refs[i][...]
            mn = ADAM_B1 * m_refs[i][...] + (1.0 - ADAM_B1) * g
            vn = ADAM_B2 * v_refs[i][...] + (1.0 - ADAM_B2) * jnp.square(g)
            m_hat = mn / (1.0 - ADAM_B1 ** ADAM_STEP)
            v_hat = vn / (1.0 - ADAM_B2 ** ADAM_STEP)
            d_refs[i][...] = -ADAM_LR * (m_hat / (jnp.sqrt(v_hat) + ADAM_EPS) + ADAM_WD * w_refs[i][...])
            mo_refs[i][...] = mn
            vo_refs[i][...] = vn

    whole = pl.BlockSpec(memory_space=pltpu.VMEM)
    shapes = [jax.ShapeDtypeStruct(w.shape, F32) for w in ws]
    outs = pl.pallas_call(
        body, name=name, in_specs=[whole] * (4 * k), out_specs=[whole] * (3 * k), out_shape=shapes * 3,
        compiler_params=_cparams(None),
    )(*gs, *ws, *ms, *vs)
    return outs[:k], outs[k:2 * k], outs[2 * k:]


MEMORY_ORDER = {'s5_b_re': (0, 1, 3, 2), 's5_b_im': (0, 1, 3, 2), 's5_d': (0, 2, 1), 'sc_conv_w': (1, 0, 2)}


def _memory_view(name, t):
    return jnp.transpose(t, MEMORY_ORDER[name]) if name in MEMORY_ORDER else t


def _slot_sum(gslots, name):
    n, r, c = gslots.shape

    def body(g_ref, o_ref):
        g = g_ref[0]
        for s in range(1, n):
            g = g + g_ref[s]
        o_ref[...] = g

    return pl.pallas_call(
        body, name=name, in_specs=[pl.BlockSpec((n, r, c), lambda: (0, 0, 0))],
        out_specs=pl.BlockSpec((r, c), lambda: (0, 0)), out_shape=jax.ShapeDtypeStruct((r, c), F32),
        compiler_params=_cparams(None),
    )(gslots)


def _me_and_peers():
    x, y, c = lax.axis_index("x"), lax.axis_index("y"), lax.axis_index("c")
    me = 4 * x + 2 * y + c
    peers = []
    for k in range(1, N_DEV):
        px = 1 - x if (k >> 2) & 1 else x
        py = 1 - y if (k >> 1) & 1 else y
        pc = 1 - c if k & 1 else c
        peers.append((4 * px + 2 * py + pc, (px, py, pc)))
    return me, peers


_HBM = pl.BlockSpec(memory_space=pltpu.HBM)
_SEM = pl.BlockSpec(memory_space=pltpu.SEMAPHORE)
_EFFECT = pltpu.SideEffectType.DATAFLOW_SIDE_EFFECTING


N_CHIP = N_DEV // 2


def _chip_peers():
    x, y, c = lax.axis_index("x"), lax.axis_index("y"), lax.axis_index("c")
    chips = []
    for d in range(1, N_CHIP):
        px = 1 - x if (d >> 1) & 1 else x
        py = 1 - y if d & 1 else y
        chips.append((2 * px + py, (px, py)))
    return (x, y, c), 2 * x + y, chips


def _plan_direct(ins, lands, send_sems, recv_sems, local_sems, gather):
    me, peers = _me_and_peers()
    plan = dict(start=[], local=[], sends=[], recvs=[])
    for t in range(len(ins)):
        own = pltpu.make_async_copy(ins[t] if gather else ins[t].at[me], lands[t].at[me], local_sems.at[t])
        plan['start'].append(own)
        plan['local'].append(own)
        for k, (pidx, pos) in enumerate(peers):
            cp = pltpu.make_async_remote_copy(
                src_ref=ins[t] if gather else ins[t].at[pidx], dst_ref=lands[t].at[me],
                send_sem=send_sems.at[t * (N_DEV - 1) + k], recv_sem=recv_sems.at[t * (N_DEV - 1) + k],
                device_id=pos, device_id_type=MESH)
            plan['start'].append(cp)
            plan['sends'].append(cp)
            plan['recvs'].append(cp)
    return plan


def _plan_gather(ins, lands, send_sems, recv_sems, local_sems, first=0):
    (x, y, c), q, chips = _chip_peers()
    me = 2 * q + c
    plan = dict(start=[], relay_wait=[], relay_start=[], local=[], sends=[], recvs=[])
    for t in range(len(ins)):
        base = (first + t) * 7
        sem = lambda k: dict(send_sem=send_sems.at[base + k], recv_sem=recv_sems.at[base + k], device_id_type=MESH)
        own = pltpu.make_async_copy(ins[t], lands[t].at[me], local_sems.at[first + t])
        to_sib = pltpu.make_async_remote_copy(src_ref=ins[t], dst_ref=lands[t].at[me], device_id=(x, y, 1 - c), **sem(0))
        plan['start'] += [own, to_sib]
        plan['local'].append(own)
        plan['sends'].append(to_sib)
        plan['recvs'].append(to_sib)
        for d, (pq, (px, py)) in enumerate(chips):
            to_chip = pltpu.make_async_remote_copy(src_ref=ins[t], dst_ref=lands[t].at[me], device_id=(px, py, c), **sem(1 + d))
            blk = lands[t].at[2 * pq + c]
            fwd = pltpu.make_async_remote_copy(src_ref=blk, dst_ref=blk, device_id=(x, y, 1 - c), **sem(4 + d))
            plan['start'].append(to_chip)
            plan['relay_wait'].append(to_chip)
            plan['relay_start'].append(fwd)
            plan['sends'] += [to_chip, fwd]
            plan['recvs'].append(fwd)
    return plan


def _plan_pair(ins, lands, send_sems, recv_sems, local_sems):
    (x, y, c), q, chips = _chip_peers()
    plan = dict(start=[], local=[], sends=[], recvs=[])
    for t in range(len(ins)):
        for k in range(N_CHIP):
            cp = pltpu.make_async_remote_copy(
                src_ref=ins[t].at[2 * k + 1 - c], dst_ref=lands[t].at[k], send_sem=send_sems.at[t * N_CHIP + k],
                recv_sem=recv_sems.at[t * N_CHIP + k], device_id=(x, y, 1 - c), device_id_type=MESH)
            plan['start'].append(cp)
            plan['sends'].append(cp)
            plan['recvs'].append(cp)
    return plan


def _plan_chips(ins, lands, send_sems, recv_sems, local_sems):
    (x, y, c), q, chips = _chip_peers()
    plan = dict(start=[], local=[], sends=[], recvs=[])
    for t in range(len(ins)):
        own = pltpu.make_async_copy(ins[t].at[q], lands[t].at[q], local_sems.at[t])
        plan['start'].append(own)
        plan['local'].append(own)
        for d, (pq, (px, py)) in enumerate(chips):
            cp = pltpu.make_async_remote_copy(
                src_ref=ins[t].at[pq], dst_ref=lands[t].at[q], send_sem=send_sems.at[t * 3 + d],
                recv_sem=recv_sems.at[t * 3 + d], device_id=(px, py, c), device_id_type=MESH)
            plan['start'].append(cp)
            plan['sends'].append(cp)
            plan['recvs'].append(cp)
    return plan


def _split_start(plan_fn, tensors, land_shapes, n_sems, name, after=None):
    n = len(tensors)
    extra = [] if after is None else [after]

    def body(*refs):
        ins, lands = refs[:n], refs[n:2 * n]
        plan = plan_fn(ins, lands, *refs[2 * n + len(extra):2 * n + len(extra) + 3])
        for cp in plan['start']:
            cp.start()
        refs[-1][...] = jnp.zeros_like(refs[-1])

    outs = pl.pallas_call(
        body, name=name,
        out_shape=(pltpu.SemaphoreType.DMA((n_sems,)), pltpu.SemaphoreType.DMA((n_sems,)), pltpu.SemaphoreType.DMA((n,)),
                   *[pltpu.HBM(t.shape, t.dtype) for t in tensors],
                   *[pltpu.HBM(s, t.dtype) for s, t in zip(land_shapes, tensors)],
                   jax.ShapeDtypeStruct((8, LANES), F32)),
        in_specs=[_HBM] * (2 * n) + [pl.BlockSpec(memory_space=pl.ANY)] * len(extra),
        out_specs=(_SEM, _SEM, _SEM, *[_HBM] * (2 * n), pl.BlockSpec(memory_space=pltpu.VMEM)),
        input_output_aliases={t: 3 + t for t in range(2 * n)},
        compiler_params=pltpu.CompilerParams(has_side_effects=_EFFECT),
    )(*[pltpu.with_memory_space_constraint(t, pltpu.HBM) for t in tensors],
      *[pltpu.with_memory_space_constraint(lax.empty(s, t.dtype), pltpu.HBM) for s, t in zip(land_shapes, tensors)], *extra)
    return outs[:-1], outs[-1]


def _split_relay(plan_fn, state, after, name):
    sems, thru = state[:3], state[3:]
    n = len(thru) // 2

    def arrived(*refs):
        plan = plan_fn(refs[:n], refs[n:2 * n], *refs[2 * n:2 * n + 3])
        for cp in plan['relay_wait']:
            cp.wait_recv()

    thru = pl.pallas_call(
        arrived, name=name + "_arrived",
        out_shape=tuple(pltpu.HBM(t.shape, t.dtype) for t in thru),
        in_specs=[_HBM] * (2 * n) + [_SEM, _SEM, _SEM, pl.BlockSpec(memory_space=pl.ANY)],
        out_specs=tuple([_HBM] * (2 * n)),
        input_output_aliases={t: t for t in range(2 * n)},
        compiler_params=pltpu.CompilerParams(has_side_effects=_EFFECT),
    )(*thru, *sems, after)

    def forward(*refs):
        plan = plan_fn(refs[:n], refs[n:2 * n], *refs[2 * n:2 * n + 3])
        for cp in plan['relay_start']:
            cp.start()
        refs[-1][...] = jnp.zeros_like(refs[-1])

    outs = pl.pallas_call(
        forward, name=name + "_forward",
        out_shape=(*[pltpu.HBM(t.shape, t.dtype) for t in thru], jax.ShapeDtypeStruct((8, LANES), F32)),
        in_specs=[_HBM] * (2 * n) + [_SEM, _SEM, _SEM],
        out_specs=(*[_HBM] * (2 * n), pl.BlockSpec(memory_space=pltpu.VMEM)),
        input_output_aliases={t: t for t in range(2 * n)},
        compiler_params=pltpu.CompilerParams(has_side_effects=_EFFECT),
    )(*thru, *sems)
    return (*sems, *outs[:-1]), outs[-1]


def _split_wait(plan_fn, state, after, name, with_sources=False):
    sems, thru = state[:3], state[3:]
    n = len(thru) // 2

    def body(*refs):
        plan = plan_fn(refs[:n], refs[n:2 * n], *refs[2 * n:2 * n + 3])
        for cp in plan['local']:
            cp.wait()
        for cp in plan['sends']:
            cp.wait_send()
        for cp in plan['recvs']:
            cp.wait_recv()

    outs = pl.pallas_call(
        body, name=name,
        out_shape=tuple(pltpu.HBM(t.shape, t.dtype) for t in thru),
        in_specs=[_HBM] * (2 * n) + [_SEM, _SEM, _SEM, pl.BlockSpec(memory_space=pl.ANY)],
        out_specs=tuple([_HBM] * (2 * n)),
        input_output_aliases={t: t for t in range(2 * n)},
        compiler_params=pltpu.CompilerParams(has_side_effects=_EFFECT),
    )(*thru, *sems, after)
    return (list(outs[:n]), list(outs[n:])) if with_sources else list(outs[n:])


PAIR_SUM_BLOCK = 768 * 1024


def _pair_sum(mine, theirs, name):
    _, r, c = mine.shape
    rows = r
    while rows * c > PAIR_SUM_BLOCK and rows % 32 == 0:
        rows //= 2

    def body(core_ref, a_ref, b_ref, o_ref):
        o_ref[0] = (a_ref[0].astype(F32) + b_ref[0].astype(F32)).astype(o_ref.dtype)

    return pl.pallas_call(
        body, name=name,
        grid_spec=pltpu.PrefetchScalarGridSpec(
            num_scalar_prefetch=1, grid=(N_CHIP, r // rows),
            in_specs=[pl.BlockSpec((1, rows, c), lambda k, i, core: (2 * k + core[0], i, 0)),
                      pl.BlockSpec((1, rows, c), lambda k, i, core: (k, i, 0))],
            out_specs=pl.BlockSpec((1, rows, c), lambda k, i, core: (k, i, 0))),
        out_shape=jax.ShapeDtypeStruct((N_CHIP, r, c), mine.dtype),
        compiler_params=_cparams(("parallel", "parallel")),
    )(lax.axis_index("c").astype(jnp.int32).reshape(1), mine, theirs)


WEIGHTS = ['norm_w', 'w_in', 's5_lambda_re', 's5_lambda_im', 's5_b_re', 's5_b_im', 's5_c_re', 's5_c_im', 's5_d',
           's5_log_step', 's5_w_glu', 'sgu_ln_w', 'sgu_ln_b', 'sgu_w', 'sgu_b', 'm2_conv_w', 'm2_conv_b', 'm2_dt_bias',
           'm2_a_log', 'm2_d', 'm2_norm_w', 'sc_conv_w', 'merge_b', 'w_branch', 'w_out', 'final_norm_w']
BIG_SHARDED = ['w_in', 'w_branch', 'w_out', 's5_w_glu']
SMALL_SHARDED = ['m2_conv_w', 'sc_conv_w', 'merge_b']
REPLICATED = [n for n in WEIGHTS if n not in BIG_SHARDED + SMALL_SHARDED]
S5_NAMES = ['s5_lambda_re', 's5_lambda_im', 's5_b_re', 's5_b_im', 's5_c_re', 's5_c_im', 's5_d', 's5_log_step']


def _sc_interleave(t):
    lead = t.shape[:-1]
    return jnp.swapaxes(t.reshape(lead + (4, 4, LANES)), -3, -2).reshape(lead + (4 * BW,))


def _pad_in(w):
    z = lambda n: jnp.zeros(w.shape[:-1] + (n,), w.dtype)
    return jnp.concatenate([w[..., 6152:], w[..., 0:1024], w[..., 3072:4096], w[..., 1024:2560], z(512),
                            w[..., 2560:3072], w[..., 4096:4104], z(504), _sc_interleave(w[..., 4104:6152])], axis=-1)


def _unpad_in(g):
    return jnp.concatenate([g[..., C_S5U:C_S5U + 1024], g[..., C_SGU_U:C_SGU_U + 1536], g[..., C_M2Z:C_M2Z + 512],
                            g[..., C_M2X:C_M2X + 1024], g[..., C_DT:C_DT + 8], _sc_interleave(g[..., C_SC:]),
                            g[..., :N_BRANCH * D_MODEL]], axis=-1)


ROW_BLOCK = 8 * LANES


def _pack_rows(tensors, row_mult, batched=False):
    parts = []
    for t in tensors:
        f = t.reshape((t.shape[0], -1) if batched else (1, -1))
        f = jnp.pad(f, ((0, 0), (0, (-f.shape[1]) % ROW_BLOCK)))
        parts.append(f.reshape(f.shape[0], -1, LANES))
    out = jnp.concatenate(parts, axis=1)
    out = jnp.pad(out, ((0, 0), (0, (-out.shape[1]) % row_mult), (0, 0)))
    return out if batched else out[0]


def _unpack_rows(rows, shapes):
    out, r0 = [], 0
    for shp in shapes:
        size = 1
        for s in shp:
            size *= s
        nr = -(-size // ROW_BLOCK) * 8
        out.append(rows[r0:r0 + nr].reshape(-1)[:size].reshape(shp))
        r0 += nr
    return out


def _kernel_col_map():
    m = np.full(IN_PAD, -1, np.int64)
    m[C_MERGE:C_MERGE + 4096] = np.arange(6152, 10248)
    m[C_S5U:C_S5U + 1024] = np.arange(0, 1024)
    m[C_M2X:C_M2X + 1024] = np.arange(3072, 4096)
    m[C_SGU_U:C_SGU_U + 1536] = np.arange(1024, 2560)
    m[C_M2Z:C_M2Z + 512] = np.arange(2560, 3072)
    m[C_DT:C_DT + 8] = np.arange(4096, 4104)
    for j in range(4):
        for kind in range(4):
            k0 = C_SC + 4 * LANES * j + LANES * kind
            m[k0:k0 + LANES] = 4104 + BW * kind + LANES * j + np.arange(LANES)
    return m


def _lane_pieces(sources):
    pieces, cur = [], None
    for lane, src in enumerate(sources):
        key = None if src is None else (src[0], src[1] // LANES, (lane - src[1]) % LANES)
        if cur is not None and key == cur[0]:
            cur[2] = lane + 1
        else:
            if cur is not None and cur[0] is not None:
                pieces.append((*cur[0], cur[1], cur[2]))
            cur = [key, lane, lane + 1]
    if cur is not None and cur[0] is not None:
        pieces.append((*cur[0], cur[1], cur[2]))
    return pieces


def _assemble_block(pieces, load, rows, dtype):
    lane = lax.broadcasted_iota(jnp.int32, (rows, LANES), 1)
    out = None
    for arr, sb, shift, lo, hi in pieces:
        v = load(arr, sb)
        if shift:
            v = pltpu.roll(v, shift, 1)
        if out is None and lo == 0 and hi == LANES:
            out = v
        else:
            out = jnp.where((lane >= lo) & (lane < hi), v, jnp.zeros((rows, LANES), dtype) if out is None else out)
    return jnp.zeros((rows, LANES), dtype) if out is None else out


RELAYOUT_ROWS = 512
SHARD_BLOCKS = -(-SHARD_IN // LANES)


def _load_shard_block(ref, rows):
    def load(j, sb):
        if sb == SHARD_BLOCKS - 1:
            return jnp.broadcast_to(ref[j, :, SHARD_IN - 1:SHARD_IN], (rows, LANES))
        return ref[j, :, sb * LANES:(sb + 1) * LANES]
    return load


def _relayout_w_in(gathered, name):
    kmap = _kernel_col_map()
    dtype = gathered.dtype

    def body(src_ref, o_ref):
        load = _load_shard_block(src_ref, RELAYOUT_ROWS)
        for ob in range(IN_PAD // LANES):
            srcs = [None if kmap[ob * LANES + l] < 0 else (int(kmap[ob * LANES + l]) // SHARD_IN, int(kmap[ob * LANES + l]) % SHARD_IN)
                    for l in range(LANES)]
            o_ref[:, ob * LANES:(ob + 1) * LANES] = _assemble_block(_lane_pieces(srcs), load, RELAYOUT_ROWS, dtype)

    return pl.pallas_call(
        body, name=name, grid=(D_MODEL // RELAYOUT_ROWS,),
        in_specs=[pl.BlockSpec((N_DEV, RELAYOUT_ROWS, SHARD_IN), lambda i: (0, i, 0))],
        out_specs=pl.BlockSpec((RELAYOUT_ROWS, IN_PAD), lambda i: (i, 0)),
        out_shape=jax.ShapeDtypeStruct((D_MODEL, IN_PAD), dtype),
        compiler_params=_cparams(("parallel",)),
    )(gathered)


def _relayout_g_in(gw, name):
    kmap = _kernel_col_map()
    kinv = np.zeros(IN_DIM, np.int64)
    kinv[kmap[kmap >= 0]] = np.nonzero(kmap >= 0)[0]
    dtype = gw.dtype

    def body(src_ref, o_ref):
        load = lambda _, sb: src_ref[:, sb * LANES:(sb + 1) * LANES]
        for j in range(N_DEV):
            for ob in range(SHARD_BLOCKS):
                srcs = [(0, int(kinv[SHARD_IN * j + ob * LANES + l])) if ob * LANES + l < SHARD_IN else None for l in range(LANES)]
                blk = _assemble_block(_lane_pieces(srcs), load, RELAYOUT_ROWS, dtype)
                if ob == SHARD_BLOCKS - 1:
                    o_ref[j, :, SHARD_IN - 1:SHARD_IN] = blk[:, 0:1]
                else:
                    o_ref[j, :, ob * LANES:(ob + 1) * LANES] = blk

    return pl.pallas_call(
        body, name=name, grid=(D_MODEL // RELAYOUT_ROWS,),
        in_specs=[pl.BlockSpec((RELAYOUT_ROWS, IN_PAD), lambda i: (i, 0))],
        out_specs=pl.BlockSpec((N_DEV, RELAYOUT_ROWS, SHARD_IN), lambda i: (0, i, 0)),
        out_shape=jax.ShapeDtypeStruct((N_DEV, D_MODEL, SHARD_IN), dtype),
        compiler_params=_cparams(("parallel",)),
    )(gw)


def _rows128(flat, row_mult=8):
    n = flat.shape[0]
    per = LANES * row_mult
    total = -(-n // per) * per
    return jnp.pad(flat, (0, total - n)).reshape(total // LANES, LANES)


def _pad_lanes(v):
    return jnp.pad(v, (0, LANES - v.shape[0])).reshape(1, LANES)


def _layer_prep(i, p):
    disc, disc_vjp = jax.vjp(_s5_disc, *[p[n][i] for n in S5_NAMES])
    prep = dict(
        nw=p['norm_w'][i].reshape(1, D_MODEL), disc_vjp=disc_vjp,
        s5small=[_block_diag(t).astype(BF16) for t in disc[:4]] + [disc[4], disc[5]],
        sgw=[p['sgu_ln_w'][i].reshape(1, BW), p['sgu_ln_b'][i].reshape(1, BW), p['sgu_w'][i],
             jnp.repeat(p['sgu_b'][i].T, BW // SGU_HEADS, axis=1)],
        cb=p['m2_conv_b'][i].reshape(1, M2_CONV_CH),
        m2w=[_pad_lanes(p['m2_dt_bias'][i]), _pad_lanes(p['m2_a_log'][i]),
             jnp.repeat(p['m2_d'][i], M2_HEAD_DIM).reshape(1, BW), p['m2_norm_w'][i].reshape(1, BW)])
    touch = [t[0, 0].astype(F32) for t in prep['s5small']] + [prep['sgw'][3][0, 0], prep['m2w'][2][0, 0]]
    return prep, sum(touch[1:], touch[0])


def _layer_fwd(x, h, i, prep, w_in, other_weights, finish, before_merge=None):
    proj = _matmul(h, w_in, 1, 0, F32, 2048, 1024, 1024, f"proj{i}")
    full = dict(other_weights(proj), w_in=w_in)
    s5w = prep['s5small'] + [full['s5_w_glu']]
    ys, sre, sim = _s5_fwd(proj, *s5w, f"s5_fwd{i}")
    ys = _sgu_fwd(proj, ys, *prep['sgw'], f"sgu_fwd{i}")
    cw = full['m2_conv_w']
    xa = _m2_conv_fwd(proj, cw, prep['cb'], f"m2conv_fwd{i}")
    ys, s_in = _ssd_fwd(proj, ys, xa, *prep['m2w'], f"ssd_fwd{i}")
    scw = full['sc_conv_w']
    ys = _sc_fwd(proj, ys, scw, f"sc_fwd{i}")
    mb = full['merge_b'].reshape(N_BRANCH, 1, D_MODEL)
    if before_merge is not None:
        mb = mb + before_merge(ys)[0, 0]
    merged = _merge_fwd(proj, ys, mb, full['w_branch'], f"merge_fwd{i}")
    x_new = finish(merged, full['w_out'], x)
    saved = dict(x=x, nw=prep['nw'], h=h, proj=proj, disc_vjp=prep['disc_vjp'], s5w=s5w, sre=sre, sim=sim, sgw=prep['sgw'],
                 cw=cw, cb=prep['cb'], xa=xa, m2w=prep['m2w'], s_in=s_in, scw=scw, ys=ys, mb=mb, merged=merged)
    return x_new, saved, full


def _layer_bwd(dx_out, i, sv, full, on_large_grads=None, after_dh=None, after=None):
    g = {}
    proj = sv['proj']
    dm, g['w_out'] = _out_bwd(dx_out, full['w_out'], sv['merged'], f"dmerged{i}_gw_out{i}", after=after)
    dys, dproj, g['w_branch'], dmb = _merge_bwd(proj, sv['ys'], dm, sv['mb'], full['w_branch'], f"merge_bwd{i}")
    g['merge_b'] = dmb.reshape(N_BRANCH, D_MODEL)
    dproj, dbbre, dbbim, dcre, dcim, da, dd, dwg = _s5_bwd(proj, dproj, dys, sv['sre'], sv['sim'], *sv['s5w'], f"s5_bwd{i}")
    g['s5_dense'] = (dbbre, dbbim, dcre, dcim, da, dd)
    g['s5_w_glu'] = dwg.astype(BF16)
    dproj, dlw, dlb, g['sgu_w'], dbias = _sgu_bwd(proj, dproj, dys, *sv['sgw'], f"sgu_bwd{i}")
    g['sgu_ln_w'], g['sgu_ln_b'] = dlw[0], dlb[0]
    g['sgu_b'] = dbias.reshape(SGU_CHUNK, SGU_HEADS, BW // SGU_HEADS).sum(-1).T
    dproj, dxa, ddtb, dal, ddf, dnw = _ssd_bwd(proj, dproj, sv['xa'], dys, sv['s_in'], *sv['m2w'], f"ssd_bwd{i}")
    dproj, g['m2_conv_w'], dcb = _m2_conv_bwd(proj, dproj, dxa, sv['cw'], sv['cb'], f"m2conv_bwd{i}")
    g['m2_conv_b'], g['m2_norm_w'] = dcb[0], dnw[0]
    g['m2_dt_bias'], g['m2_a_log'] = ddtb[0, :M2_HEADS], dal[0, :M2_HEADS]
    g['m2_d'] = ddf.reshape(M2_HEADS, M2_HEAD_DIM).sum(-1)
    dproj, g['sc_conv_w'] = _sc_bwd(proj, dproj, dys, sv['scw'], f"sc_bwd{i}")
    g['w_in'] = _matmul(sv['h'], dproj, 0, 0, BF16, 1024, 1024, 2048, f"gw_in{i}")
    tok = on_large_grads(g) if on_large_grads else None
    if after_dh is not None:
        dx_in, dnw_l = _dh_rms_bwd(dproj, full['w_in'], sv['x'], sv['nw'], dx_out, f"dh{i}_rms_bwd{i}", after=tok)
        after_dh(dx_in)
    else:
        dh = _matmul(dproj, full['w_in'], 1, 1, F32, 1024, 1024, IN_PAD // 4, f"dh{i}", after=tok)
        dx_in, dnw_l = _rmsnorm_bwd(sv['x'], sv['nw'], dh, dx_out, f"rms_bwd{i}")
    g['norm_w'] = dnw_l[0]
    return dx_in, g


def _split8(t, axis):
    shp = t.shape
    t = t.reshape(shp[:axis] + (N_DEV, shp[axis] // N_DEV) + shp[axis + 1:])
    return jnp.moveaxis(t, axis, 0)


def _join8(t, axis):
    t = jnp.moveaxis(t, 0, axis)
    shp = t.shape
    return t.reshape(shp[:axis] + (shp[axis] * shp[axis + 1],) + shp[axis + 2:])


SHARD_AXIS = {'w_in': 2, 'w_branch': 3, 'w_out': 1, 's5_w_glu': 1, 'm2_conv_w': 2, 'sc_conv_w': 2, 'merge_b': 2}


OTHER_BIG = [n for n in BIG_SHARDED if n != 'w_in']


def _other_weights(gathered):
    return {n: _join8(t, SHARD_AXIS[n] - 1) for n, t in zip(OTHER_BIG, gathered)}


def _layer_grad_blocks(g, i):
    blocks = [_relayout_g_in(g[n], f"relayout_g_in{i}") if n == 'w_in' else _split8(g[n], SHARD_AXIS[n] - 1) for n in BIG_SHARDED]
    return [b.reshape(N_DEV, -1, b.shape[-1]) for b in blocks]


def _pair_start(blocks, tag):
    shapes = [(N_CHIP,) + b.shape[1:] for b in blocks]
    return _split_start(_plan_pair, blocks, shapes, N_CHIP * len(blocks), f"pair{tag}_start")


def _pair_sums(state, after, tag):
    mine, theirs = _split_wait(_plan_pair, state, after, f"pair{tag}_wait", with_sources=True)
    return [_pair_sum(b, t, f"pair_sum{tag}_{k}") for k, (b, t) in enumerate(zip(mine, theirs))]


def _chips_start(sums, tag, after=None):
    return _split_start(_plan_chips, sums, [s.shape for s in sums], 3 * len(sums), f"chips{tag}_start", after)


def kernel(x, norm_w, w_in, s5_lambda_re, s5_lambda_im, s5_b_re, s5_b_im, s5_c_re, s5_c_im, s5_d, s5_log_step, s5_w_glu, sgu_ln_w, sgu_ln_b, sgu_w, sgu_b, m2_conv_w, m2_conv_b, m2_dt_bias, m2_a_log, m2_d, m2_norm_w, sc_conv_w, merge_b, w_branch, w_out, final_norm_w, loss_target, m_norm_w, m_w_in, m_s5_lambda_re, m_s5_lambda_im, m_s5_b_re, m_s5_b_im, m_s5_c_re, m_s5_c_im, m_s5_d, m_s5_log_step, m_s5_w_glu, m_sgu_ln_w, m_sgu_ln_b, m_sgu_w, m_sgu_b, m_m2_conv_w, m_m2_conv_b, m_m2_dt_bias, m_m2_a_log, m_m2_d, m_m2_norm_w, m_sc_conv_w, m_merge_b, m_w_branch, m_w_out, m_final_norm_w, v_norm_w, v_w_in, v_s5_lambda_re, v_s5_lambda_im, v_s5_b_re, v_s5_b_im, v_s5_c_re, v_s5_c_im, v_s5_d, v_s5_log_step, v_s5_w_glu, v_sgu_ln_w, v_sgu_ln_b, v_sgu_w, v_sgu_b, v_m2_conv_w, v_m2_conv_b, v_m2_dt_bias, v_m2_a_log, v_m2_d, v_m2_norm_w, v_sc_conv_w, v_merge_b, v_w_branch, v_w_out, v_final_norm_w):
    loc = locals()
    p = {n: loc[n] for n in WEIGHTS}
    mom = {n: loc['m_' + n] for n in WEIGHTS}
    vel = {n: loc['v_' + n] for n in WEIGHTS}

    small_sizes = [p[n].size for n in SMALL_SHARDED]
    small_pack = _rows128(jnp.concatenate([p[n].reshape(-1) for n in SMALL_SHARDED]))
    first = [p['w_in'][0].astype(BF16)]
    gath_first, tok = _split_start(_plan_gather, first, [(N_DEV,) + first[0].shape], 7, "gather_w_in0_start")
    shards = ([(p[n][0] + tok[0, 0]).astype(BF16) for n in OTHER_BIG] + [small_pack + tok[0, 0]]
              + [(p[n][1] + tok[0, 0]).astype(BF16) for n in BIG_SHARDED])
    gath, tok = _split_start(_plan_gather, shards, [(N_DEV,) + t.shape for t in shards], 7 * len(shards), "gather_start")

    def relayed(lo, hi, after, name, started=None):
        started = gath if started is None else started
        n = (len(started) - 3) // 2
        sems, srcs, lands = started[:3], started[3:3 + n], started[3 + n:]
        plan = functools.partial(_plan_gather, first=lo)
        state, tok = _split_relay(plan, (*sems, *srcs[lo:hi], *lands[lo:hi]), after, name + "_relay")
        return (plan, state, name), tok

    def arrived(relay, after):
        plan, state, name = relay
        return _split_wait(plan, state, after, name + "_wait")

    def gathered(lo, hi, after, name, started=None):
        relay, tok = relayed(lo, hi, after, name, started)
        return arrived(relay, tok)

    later = dict(p, **{n: p[n] + tok[0, 0] for n in ('norm_w', 's5_log_step', 'sgu_b', 'm2_d')})
    preps = [_layer_prep(i, later) for i in range(DEPTH)]
    h0 = _rmsnorm_fwd(x[0], preps[0][0]['nw'], "rms_fwd0")
    got = gathered(0, 1, tok + (preps[0][1] + preps[1][1] + h0[0, 0].astype(F32)), "gather_w_in0", gath_first)
    small_full = {}

    def other_weights0(proj):
        got = gathered(0, 4, proj, "gather_rest0")
        small_all, off = got[-1].reshape(N_DEV, -1), 0
        for n, sz in zip(SMALL_SHARDED, small_sizes):
            small_full[n] = _join8(small_all[:, off:off + sz].reshape((N_DEV,) + p[n].shape), SHARD_AXIS[n])
            off += sz
        return dict(_other_weights(got[:-1]), **{n: small_full[n][0] for n in SMALL_SHARDED})

    saved, layer_g, full = [None] * DEPTH, [None] * DEPTH, [None] * DEPTH
    relay1 = []

    def relay_layer1(ys):
        relay, tok = relayed(4, 8, ys, "gather1")
        relay1.append(relay)
        return tok

    (xs, h1), saved[0], full[0] = _layer_fwd(
        x[0], h0, 0, preps[0][0], _relayout_w_in(got[0], "relayout_w_in0"), other_weights0,
        lambda merged, w_out, xin: _out_norm(merged, w_out, xin, preps[1][0]['nw'], "out0_rms_fwd1"), relay_layer1)
    got = arrived(relay1[0], h1)
    (loss_row, dx, dfw), saved[1], full[1] = _layer_fwd(
        xs, h1, 1, preps[1][0], _relayout_w_in(got[0], "relayout_w_in1"),
        lambda proj: dict(_other_weights(got[1:]), **{n: small_full[n][1] for n in SMALL_SHARDED}),
        lambda merged, w_out, xin: _out_loss(merged, w_out, xin, final_norm_w.reshape(1, D_MODEL), loss_target[0],
                                             "out1_loss_head"))
    loss = lax.psum(loss_row[0, 0], ("x", "y", "c"))
    loss, dx = lax.optimization_barrier((loss, dx))
    pairs, scat, sent0, sent1 = [None] * DEPTH, [None] * DEPTH, [], []

    def start_pairs1(g):
        pairs[1], tok = _pair_start(_layer_grad_blocks(g, 1), 1)
        return tok

    def send_chip_sums1(dh):
        scat[1], tok = _chips_start(_pair_sums(pairs[1], dh, 1), 1)
        sent1.append(tok)
        return tok

    def send_all0(g):
        pairs[0], tok = _pair_start(_layer_grad_blocks(g, 0), 0)
        scat[0], tok = _chips_start(_pair_sums(pairs[0], tok, 0), 0)
        sent0.append(tok)
        return tok

    dx, layer_g[1] = _layer_bwd(dx, 1, saved[1], full[1], on_large_grads=start_pairs1, after_dh=send_chip_sums1)
    dx, layer_g[0] = _layer_bwd(dx, 0, saved[0], full[0], on_large_grads=send_all0, after=sent1[0])
    for i in range(DEPTH):
        dense = layer_g[i].pop('s5_dense')
        blocks = tuple(_diag_blocks(t, after=sent0[0]) for t in dense[:4])
        layer_g[i].update(zip(S5_NAMES, saved[i]['disc_vjp'](blocks + (dense[4] + sent0[0][0, 0], dense[5]))))
    grads = {n: jnp.stack([layer_g[i][n] for i in range(DEPTH)]) for n in SMALL_SHARDED + REPLICATED if n != 'final_norm_w'}
    grads['final_norm_w'] = dfw[0]

    out_g, out_d, out_m, out_v = {}, {}, {}, {}
    repl_rows = _pack_rows([grads[n] for n in REPLICATED], 8 * N_DEV)
    rr = repl_rows.shape[0] // N_DEV
    shard_rows = _pack_rows([_split8(grads[n], SHARD_AXIS[n]) for n in SMALL_SHARDED], 8, batched=True)
    rs = shard_rows.shape[1]
    small_g = jnp.concatenate([shard_rows, repl_rows.reshape(N_DEV, rr, LANES)], axis=1)
    all_to_all, all_gather = functools.partial(_plan_direct, gather=False), functools.partial(_plan_direct, gather=True)
    small_state, tok = _split_start(all_to_all, [small_g], [small_g.shape], N_DEV - 1, "scatter_small_start")
    landed1 = _split_wait(_plan_chips, scat[1], tok, "chips1_wait")
    landed0 = _split_wait(_plan_chips, scat[0], landed1[0], "chips0_wait")

    def big_adamw(n, after=None):
        k, shp = BIG_SHARDED.index(n), p[n].shape
        if n == 'w_in':
            return _adamw_w_in([landed0[k], landed1[k]], p[n], mom[n], vel[n], "adamw_w_in", after)
        c = shp[-1]
        r = p[n].size // (DEPTH * c)
        res = _adamw([landed0[k], landed1[k]], *[d[n].reshape(DEPTH, r, c) for d in (p, mom, vel)],
                     {'w_branch': 512, 'w_out': 128, 's5_w_glu': 64}[n], "adamw_" + n)
        return [o.reshape(shp) for o in res]

    for n in OTHER_BIG:
        out_g[n], out_d[n], out_m[n], out_v[n] = big_adamw(n)
    updated = sum(out_d[n].reshape(-1)[0] for n in OTHER_BIG).reshape(1, 1)
    small_sum = _slot_sum(_split_wait(all_to_all, small_state, updated, "scatter_small_wait")[0], "sum_small")
    repl_part = small_sum[rs:]
    repl_state, tok = _split_start(all_gather, [repl_part], [(N_DEV,) + repl_part.shape], N_DEV - 1, "gather_small_start")
    out_g['w_in'], out_d['w_in'], out_m['w_in'], out_v['w_in'] = big_adamw('w_in', tok)
    repl_all = _split_wait(all_gather, repl_state, out_d['w_in'], "gather_small_wait")[0].reshape(N_DEV * rr, LANES)
    g_all = jnp.concatenate([small_sum[:rs], repl_all], axis=0)
    names = SMALL_SHARDED + REPLICATED
    pieces = (_unpack_rows(g_all[:rs], [p[n].shape for n in SMALL_SHARDED])
              + _unpack_rows(g_all[rs:], [p[n].shape for n in REPLICATED]))
    out_g.update(zip(names, pieces))
    res = _adamw_many(*[[_memory_view(n, d[n]) for n in names] for d in (out_g, p, mom, vel)], "adamw_small")
    for r, dst in zip(res, (out_d, out_m, out_v)):
        dst.update({n: _memory_view(n, t) for n, t in zip(names, r)})
    return (loss, dx[None], *[out_g[n] for n in WEIGHTS], *[out_d[n] for n in WEIGHTS],
            *[out_m[n] for n in WEIGHTS], *[out_v[n] for n in WEIGHTS])
```
